```python
import math
import jax, jax.numpy as jnp
from jax import lax
import numpy as np

D_MODEL = 1024
BATCH = 8
SEQ = 4096
DEPTH = 1

D_MIX = D_MODEL
D_CONV = D_MIX // 2
D_RNN = D_MIX - D_CONV
N_RNN_HEADS = 8
RNN_HEAD_DIM = D_RNN // N_RNN_HEADS
CONV_WIDTH = 31
RNN_CONV_WIDTH = 4
RG_C = 8.0
D_FF = ((8 * D_MODEL // 3 + 127) // 128) * 128
N_SUBLAYERS = 3
MACARON_W = 0.5
EPS = 1e-6

kernel_name = "hymba_style_conformer_rglru_macaron_adaln_block"


def rmsnorm(x, g):
    xf = x.astype(jnp.float32)
    y = xf * lax.rsqrt(jnp.mean(xf * xf, axis=-1, keepdims=True) + EPS)
    return y.astype(x.dtype) * g


def layernorm(x, g, b):
    xf = x.astype(jnp.float32)
    mu = jnp.mean(xf, axis=-1, keepdims=True)
    var = jnp.mean(jnp.square(xf - mu), axis=-1, keepdims=True)
    y = (xf - mu) * lax.rsqrt(var + EPS)
    return y.astype(x.dtype) * g + b


def ada_rmsnorm(x, g, shift, scale):
    return rmsnorm(x, g) * (1.0 + scale[:, None, :]) + shift[:, None, :]


def causal_depthwise_conv(u, w, b):
    k = w.shape[0]
    y = lax.conv_general_dilated(
        u, w[:, None, :], window_strides=(1,), padding=[(k - 1, 0)],
        dimension_numbers=("NWC", "WIO", "NWC"), feature_group_count=u.shape[-1])
    return y + b


def swiglu_ffn(h, w_in, w_out):
    gate, up = jnp.split(h @ w_in, 2, axis=-1)
    return (jax.nn.silu(gate) * up) @ w_out


def conformer_conv_group(u_val, u_gate, conv_w, conv_b, ln_g, ln_b):
    u = u_val * jax.nn.sigmoid(u_gate)
    u = causal_depthwise_conv(u, conv_w, conv_b)
    u = layernorm(u, ln_g, ln_b)
    return jax.nn.silu(u)


def _lru_combine(left, right):
    a_l, b_l = left
    a_r, b_r = right
    return a_l * a_r, a_r * b_l + b_r


def rglru_group(u_x, u_y, conv_w, conv_b, w_a, b_a, w_i, b_i, lru_lambda):
    bsz, seq, _ = u_x.shape
    xr = causal_depthwise_conv(u_x, conv_w, conv_b)
    xh = xr.reshape(bsz, seq, N_RNN_HEADS, RNN_HEAD_DIM)
    r = jax.nn.sigmoid(jnp.einsum("bshd,hde->bshe", xh, w_a).reshape(bsz, seq, D_RNN) + b_a)
    i = jax.nn.sigmoid(jnp.einsum("bshd,hde->bshe", xh, w_i).reshape(bsz, seq, D_RNN) + b_i)
    log_a = RG_C * r.astype(jnp.float32) * jax.nn.log_sigmoid(lru_lambda.astype(jnp.float32))
    a = jnp.exp(log_a)
    mult = jnp.sqrt(-jnp.expm1(2.0 * log_a))
    bterm = mult * (i.astype(jnp.float32) * xr.astype(jnp.float32))
    _, h = lax.associative_scan(_lru_combine, (a, bterm), axis=1)
    return jax.nn.gelu(u_y) * h.astype(u_x.dtype)


def _fwd_setup_inputs(seed: int = 0) -> dict:
    key = jax.random.key(seed)
    ks = jax.random.split(key, 32)
    f32 = jnp.float32
    L = DEPTH
    nrm = lambda k, shape, s: jax.random.normal(k, shape, f32) * s
    gain = lambda k, shape: 1.0 + 0.05 * jax.random.normal(k, shape, f32)
    a0 = jax.random.uniform(ks[20], (L, D_RNN), f32, 0.9, 0.999)
    s = a0 ** (1.0 / RG_C)
    lru_lambda = jnp.log(s) - jnp.log1p(-s)
    return {
        "x": jax.random.normal(ks[0], (BATCH, SEQ, D_MODEL), f32),
        "c": jax.random.normal(ks[1], (BATCH, D_MODEL), f32),
        "w_mod": nrm(ks[2], (L, D_MODEL, 3 * N_SUBLAYERS * D_MODEL), 0.5 * D_MODEL ** -0.5),
        "b_mod": nrm(ks[3], (L, 3 * N_SUBLAYERS * D_MODEL), 0.02),
        "g_ffn1": gain(ks[4], (L, D_MODEL)),
        "w_ffn1_in": nrm(ks[5], (L, D_MODEL, 2 * D_FF), D_MODEL ** -0.5),
        "w_ffn1_out": nrm(ks[6], (L, D_FF, D_MODEL), D_FF ** -0.5),
        "g_mix": gain(ks[7], (L, D_MODEL)),
        "w_in": nrm(ks[8], (L, D_MODEL, 2 * D_CONV + 2 * D_RNN), D_MODEL ** -0.5),
        "conv_w": nrm(ks[9], (L, CONV_WIDTH, D_CONV), CONV_WIDTH ** -0.5),
        "conv_b": nrm(ks[10], (L, D_CONV), 0.02),
        "ln_g": gain(ks[11], (L, D_CONV)),
        "ln_b": nrm(ks[12], (L, D_CONV), 0.02),
        "rnn_conv_w": nrm(ks[13], (L, RNN_CONV_WIDTH, D_RNN), RNN_CONV_WIDTH ** -0.5),
        "rnn_conv_b": nrm(ks[14], (L, D_RNN), 0.02),
        "w_a": nrm(ks[15], (L, N_RNN_HEADS, RNN_HEAD_DIM, RNN_HEAD_DIM), RNN_HEAD_DIM ** -0.5),
        "b_a": nrm(ks[16], (L, D_RNN), 0.02),
        "w_i": nrm(ks[17], (L, N_RNN_HEADS, RNN_HEAD_DIM, RNN_HEAD_DIM), RNN_HEAD_DIM ** -0.5),
        "b_i": nrm(ks[18], (L, D_RNN), 0.02),
        "lru_lambda": lru_lambda,
        "w_out": nrm(ks[19], (L, D_MIX, D_MODEL), D_MIX ** -0.5),
        "g_ffn2": gain(ks[21], (L, D_MODEL)),
        "w_ffn2_in": nrm(ks[22], (L, D_MODEL, 2 * D_FF), D_MODEL ** -0.5),
        "w_ffn2_out": nrm(ks[23], (L, D_FF, D_MODEL), D_FF ** -0.5),
        "w_fmod": nrm(ks[24], (D_MODEL, 2 * D_MODEL), 0.5 * D_MODEL ** -0.5),
        "b_fmod": nrm(ks[25], (2 * D_MODEL,), 0.02),
        "g_final": gain(ks[26], (D_MODEL,)),
    }


def _fwd_reference(x, c, w_mod, b_mod, g_ffn1, w_ffn1_in, w_ffn1_out, g_mix, w_in,
              conv_w, conv_b, ln_g, ln_b, rnn_conv_w, rnn_conv_b, w_a, b_a, w_i, b_i,
              lru_lambda, w_out, g_ffn2, w_ffn2_in, w_ffn2_out, w_fmod, b_fmod, g_final):
    c_act = jax.nn.silu(c)
    for l in range(DEPTH):
        mod = c_act @ w_mod[l] + b_mod[l]
        sh1, sc1, gt1, sh2, sc2, gt2, sh3, sc3, gt3 = jnp.split(mod, 3 * N_SUBLAYERS, axis=-1)

        h = ada_rmsnorm(x, g_ffn1[l], sh1, sc1)
        x = x + MACARON_W * gt1[:, None, :] * swiglu_ffn(h, w_ffn1_in[l], w_ffn1_out[l])

        h = ada_rmsnorm(x, g_mix[l], sh2, sc2)
        proj = h @ w_in[l]
        u_val, u_gate, u_x, u_y = jnp.split(
            proj, [D_CONV, 2 * D_CONV, 2 * D_CONV + D_RNN], axis=-1)
        y_conv = conformer_conv_group(u_val, u_gate, conv_w[l], conv_b[l], ln_g[l], ln_b[l])
        y_rnn = rglru_group(u_x, u_y, rnn_conv_w[l], rnn_conv_b[l], w_a[l], b_a[l],
                            w_i[l], b_i[l], lru_lambda[l])
        y_mix = jnp.concatenate([y_conv, y_rnn], axis=-1) @ w_out[l]
        x = x + gt2[:, None, :] * y_mix

        h = ada_rmsnorm(x, g_ffn2[l], sh3, sc3)
        x = x + MACARON_W * gt3[:, None, :] * swiglu_ffn(h, w_ffn2_in[l], w_ffn2_out[l])

    fmod = c_act @ w_fmod + b_fmod
    f_shift, f_scale = jnp.split(fmod, 2, axis=-1)
    return ada_rmsnorm(x, g_final, f_shift, f_scale)


import jax as _jax
import jax.numpy as _jnp

TWIN_FORMAT = 'train_step'
FWD_PARAMS = ['x', 'c', 'w_mod', 'b_mod', 'g_ffn1', 'w_ffn1_in', 'w_ffn1_out', 'g_mix', 'w_in', 'conv_w', 'conv_b', 'ln_g', 'ln_b', 'rnn_conv_w', 'rnn_conv_b', 'w_a', 'b_a', 'w_i', 'b_i', 'lru_lambda', 'w_out', 'g_ffn2', 'w_ffn2_in', 'w_ffn2_out', 'w_fmod', 'b_fmod', 'g_final']
TWIN_WEIGHTS = ['w_mod', 'b_mod', 'g_ffn1', 'w_ffn1_in', 'w_ffn1_out', 'g_mix', 'w_in', 'conv_w', 'conv_b', 'ln_g', 'ln_b', 'rnn_conv_w', 'rnn_conv_b', 'w_a', 'b_a', 'w_i', 'b_i', 'lru_lambda', 'w_out', 'g_ffn2', 'w_ffn2_in', 'w_ffn2_out', 'w_fmod', 'b_fmod', 'g_final']
TWIN_DIFF_INPUT = 'x'
TWIN_INPUTS = ['x', 'c', 'w_mod', 'b_mod', 'g_ffn1', 'w_ffn1_in', 'w_ffn1_out', 'g_mix', 'w_in', 'conv_w', 'conv_b', 'ln_g', 'ln_b', 'rnn_conv_w', 'rnn_conv_b', 'w_a', 'b_a', 'w_i', 'b_i', 'lru_lambda', 'w_out', 'g_ffn2', 'w_ffn2_in', 'w_ffn2_out', 'w_fmod', 'b_fmod', 'g_final', 'loss_target', 'm_w_mod', 'm_b_mod', 'm_g_ffn1', 'm_w_ffn1_in', 'm_w_ffn1_out', 'm_g_mix', 'm_w_in', 'm_conv_w', 'm_conv_b', 'm_ln_g', 'm_ln_b', 'm_rnn_conv_w', 'm_rnn_conv_b', 'm_w_a', 'm_b_a', 'm_w_i', 'm_b_i', 'm_lru_lambda', 'm_w_out', 'm_g_ffn2', 'm_w_ffn2_in', 'm_w_ffn2_out', 'm_w_fmod', 'm_b_fmod', 'm_g_final', 'v_w_mod', 'v_b_mod', 'v_g_ffn1', 'v_w_ffn1_in', 'v_w_ffn1_out', 'v_g_mix', 'v_w_in', 'v_conv_w', 'v_conv_b', 'v_ln_g', 'v_ln_b', 'v_rnn_conv_w', 'v_rnn_conv_b', 'v_w_a', 'v_b_a', 'v_w_i', 'v_b_i', 'v_lru_lambda', 'v_w_out', 'v_g_ffn2', 'v_w_ffn2_in', 'v_w_ffn2_out', 'v_w_fmod', 'v_b_fmod', 'v_g_final']
TWIN_OUTPUTS = ['loss', 'grad_x', 'grad_w_mod', 'grad_b_mod', 'grad_g_ffn1', 'grad_w_ffn1_in', 'grad_w_ffn1_out', 'grad_g_mix', 'grad_w_in', 'grad_conv_w', 'grad_conv_b', 'grad_ln_g', 'grad_ln_b', 'grad_rnn_conv_w', 'grad_rnn_conv_b', 'grad_w_a', 'grad_b_a', 'grad_w_i', 'grad_b_i', 'grad_lru_lambda', 'grad_w_out', 'grad_g_ffn2', 'grad_w_ffn2_in', 'grad_w_ffn2_out', 'grad_w_fmod', 'grad_b_fmod', 'grad_g_final', 'delta_w_mod', 'delta_b_mod', 'delta_g_ffn1', 'delta_w_ffn1_in', 'delta_w_ffn1_out', 'delta_g_mix', 'delta_w_in', 'delta_conv_w', 'delta_conv_b', 'delta_ln_g', 'delta_ln_b', 'delta_rnn_conv_w', 'delta_rnn_conv_b', 'delta_w_a', 'delta_b_a', 'delta_w_i', 'delta_b_i', 'delta_lru_lambda', 'delta_w_out', 'delta_g_ffn2', 'delta_w_ffn2_in', 'delta_w_ffn2_out', 'delta_w_fmod', 'delta_b_fmod', 'delta_g_final', 'new_m_w_mod', 'new_m_b_mod', 'new_m_g_ffn1', 'new_m_w_ffn1_in', 'new_m_w_ffn1_out', 'new_m_g_mix', 'new_m_w_in', 'new_m_conv_w', 'new_m_conv_b', 'new_m_ln_g', 'new_m_ln_b', 'new_m_rnn_conv_w', 'new_m_rnn_conv_b', 'new_m_w_a', 'new_m_b_a', 'new_m_w_i', 'new_m_b_i', 'new_m_lru_lambda', 'new_m_w_out', 'new_m_g_ffn2', 'new_m_w_ffn2_in', 'new_m_w_ffn2_out', 'new_m_w_fmod', 'new_m_b_fmod', 'new_m_g_final', 'new_v_w_mod', 'new_v_b_mod', 'new_v_g_ffn1', 'new_v_w_ffn1_in', 'new_v_w_ffn1_out', 'new_v_g_mix', 'new_v_w_in', 'new_v_conv_w', 'new_v_conv_b', 'new_v_ln_g', 'new_v_ln_b', 'new_v_rnn_conv_w', 'new_v_rnn_conv_b', 'new_v_w_a', 'new_v_b_a', 'new_v_w_i', 'new_v_b_i', 'new_v_lru_lambda', 'new_v_w_out', 'new_v_g_ffn2', 'new_v_w_ffn2_in', 'new_v_w_ffn2_out', 'new_v_w_fmod', 'new_v_b_fmod', 'new_v_g_final']
TWIN_LEAF_KINDS = {'loss': 'loss', 'grad_x': 'grad_x', 'grad_w_mod': 'grad_w', 'grad_b_mod': 'grad_w', 'grad_g_ffn1': 'grad_w', 'grad_w_ffn1_in': 'grad_w', 'grad_w_ffn1_out': 'grad_w', 'grad_g_mix': 'grad_w', 'grad_w_in': 'grad_w', 'grad_conv_w': 'grad_w', 'grad_conv_b': 'grad_w', 'grad_ln_g': 'grad_w', 'grad_ln_b': 'grad_w', 'grad_rnn_conv_w': 'grad_w', 'grad_rnn_conv_b': 'grad_w', 'grad_w_a': 'grad_w', 'grad_b_a': 'grad_w', 'grad_w_i': 'grad_w', 'grad_b_i': 'grad_w', 'grad_lru_lambda': 'grad_w', 'grad_w_out': 'grad_w', 'grad_g_ffn2': 'grad_w', 'grad_w_ffn2_in': 'grad_w', 'grad_w_ffn2_out': 'grad_w', 'grad_w_fmod': 'grad_w', 'grad_b_fmod': 'grad_w', 'grad_g_final': 'grad_w', 'delta_w_mod': 'delta_w', 'delta_b_mod': 'delta_w', 'delta_g_ffn1': 'delta_w', 'delta_w_ffn1_in': 'delta_w', 'delta_w_ffn1_out': 'delta_w', 'delta_g_mix': 'delta_w', 'delta_w_in': 'delta_w', 'delta_conv_w': 'delta_w', 'delta_conv_b': 'delta_w', 'delta_ln_g': 'delta_w', 'delta_ln_b': 'delta_w', 'delta_rnn_conv_w': 'delta_w', 'delta_rnn_conv_b': 'delta_w', 'delta_w_a': 'delta_w', 'delta_b_a': 'delta_w', 'delta_w_i': 'delta_w', 'delta_b_i': 'delta_w', 'delta_lru_lambda': 'delta_w', 'delta_w_out': 'delta_w', 'delta_g_ffn2': 'delta_w', 'delta_w_ffn2_in': 'delta_w', 'delta_w_ffn2_out': 'delta_w', 'delta_w_fmod': 'delta_w', 'delta_b_fmod': 'delta_w', 'delta_g_final': 'delta_w', 'new_m_w_mod': 'new_m', 'new_m_b_mod': 'new_m', 'new_m_g_ffn1': 'new_m', 'new_m_w_ffn1_in': 'new_m', 'new_m_w_ffn1_out': 'new_m', 'new_m_g_mix': 'new_m', 'new_m_w_in': 'new_m', 'new_m_conv_w': 'new_m', 'new_m_conv_b': 'new_m', 'new_m_ln_g': 'new_m', 'new_m_ln_b': 'new_m', 'new_m_rnn_conv_w': 'new_m', 'new_m_rnn_conv_b': 'new_m', 'new_m_w_a': 'new_m', 'new_m_b_a': 'new_m', 'new_m_w_i': 'new_m', 'new_m_b_i': 'new_m', 'new_m_lru_lambda': 'new_m', 'new_m_w_out': 'new_m', 'new_m_g_ffn2': 'new_m', 'new_m_w_ffn2_in': 'new_m', 'new_m_w_ffn2_out': 'new_m', 'new_m_w_fmod': 'new_m', 'new_m_b_fmod': 'new_m', 'new_m_g_final': 'new_m', 'new_v_w_mod': 'new_v', 'new_v_b_mod': 'new_v', 'new_v_g_ffn1': 'new_v', 'new_v_w_ffn1_in': 'new_v', 'new_v_w_ffn1_out': 'new_v', 'new_v_g_mix': 'new_v', 'new_v_w_in': 'new_v', 'new_v_conv_w': 'new_v', 'new_v_conv_b': 'new_v', 'new_v_ln_g': 'new_v', 'new_v_ln_b': 'new_v', 'new_v_rnn_conv_w': 'new_v', 'new_v_rnn_conv_b': 'new_v', 'new_v_w_a': 'new_v', 'new_v_b_a': 'new_v', 'new_v_w_i': 'new_v', 'new_v_b_i': 'new_v', 'new_v_lru_lambda': 'new_v', 'new_v_w_out': 'new_v', 'new_v_g_ffn2': 'new_v', 'new_v_w_ffn2_in': 'new_v', 'new_v_w_ffn2_out': 'new_v', 'new_v_w_fmod': 'new_v', 'new_v_b_fmod': 'new_v', 'new_v_g_final': 'new_v'}


def _forward(args):
    return _fwd_reference(*[args[k] for k in FWD_PARAMS])


def _output_shape():
    out = _jax.eval_shape(lambda: _forward(_fwd_setup_inputs(0)))
    return out.shape, out.dtype

N_MICROBATCH = 1
ADAM_LR = 0.001
ADAM_B1 = 0.9
ADAM_B2 = 0.999
ADAM_EPS = 1e-08
ADAM_WD = 0.01
ADAM_STEP = 10
PER_EXAMPLE_BATCH_AXIS = {'x': 0, 'c': 0, 'loss_target': 0}
SHARED_INPUTS = []
_WEIGHT_DTYPES = {'w_mod': _jnp.float32, 'b_mod': _jnp.float32, 'g_ffn1': _jnp.float32, 'w_ffn1_in': _jnp.float32, 'w_ffn1_out': _jnp.float32, 'g_mix': _jnp.float32, 'w_in': _jnp.float32, 'conv_w': _jnp.float32, 'conv_b': _jnp.float32, 'ln_g': _jnp.float32, 'ln_b': _jnp.float32, 'rnn_conv_w': _jnp.float32, 'rnn_conv_b': _jnp.float32, 'w_a': _jnp.float32, 'b_a': _jnp.float32, 'w_i': _jnp.float32, 'b_i': _jnp.float32, 'lru_lambda': _jnp.float32, 'w_out': _jnp.float32, 'g_ffn2': _jnp.float32, 'w_ffn2_in': _jnp.float32, 'w_ffn2_out': _jnp.float32, 'w_fmod': _jnp.float32, 'b_fmod': _jnp.float32, 'g_final': _jnp.float32}
MOMENT_SCALE = {'w_mod': 7.804683e-01, 'b_mod': 1.291960e+00, 'g_ffn1': 4.005124e-02, 'w_ffn1_in': 2.791556e-02, 'w_ffn1_out': 5.593924e-02, 'g_mix': 1.386097e-01, 'w_in': 5.013213e-01, 'conv_w': 1.970086e-01, 'conv_b': 1.400054e+00, 'ln_g': 6.597567e-01, 'ln_b': 8.784259e-01, 'rnn_conv_w': 8.155713e-01, 'rnn_conv_b': 2.461764e+00, 'w_a': 1.074944e-01, 'b_a': 1.062408e-01, 'w_i': 2.354822e-01, 'b_i': 3.457376e-01, 'lru_lambda': 2.550076e-01, 'w_out': 7.101688e-01, 'g_ffn2': 4.846924e-02, 'w_ffn2_in': 2.854979e-02, 'w_ffn2_out': 5.854882e-02, 'w_fmod': 8.844224e+00, 'b_fmod': 2.338480e+01, 'g_final': 3.643892e+01}


def _to_microbatches(a, axis):
    t = _jnp.moveaxis(a, axis, 0)
    t = t.reshape((N_MICROBATCH, t.shape[0] // N_MICROBATCH) + t.shape[1:])
    return _jnp.moveaxis(t, 1, axis + 1)


def setup_inputs(seed: int = 0) -> dict:
    inp = _fwd_setup_inputs(seed)
    key = _jax.random.fold_in(_jax.random.key(seed), 7919)
    shape, _ = _output_shape()
    out = dict(inp)
    out["loss_target"] = _jax.random.normal(_jax.random.fold_in(key, 0), shape, _jnp.float32)
    for i, name in enumerate(TWIN_WEIGHTS):
        w = inp[name].astype(_jnp.float32)
        if MOMENT_SCALE is None:
            s = _jnp.sqrt(_jnp.mean(_jnp.square(w)) + 1e-30)
        else:
            s = MOMENT_SCALE[name]
        km, kv = _jax.random.split(_jax.random.fold_in(key, i + 1))
        out[name] = w
        out["m_" + name] = s * _jax.random.normal(km, w.shape, _jnp.float32)
        out["v_" + name] = (s * s) * _jax.random.uniform(kv, w.shape, _jnp.float32, 0.5, 1.5)
    if N_MICROBATCH > 1:
        for name, axis in PER_EXAMPLE_BATCH_AXIS.items():
            out[name] = _to_microbatches(out[name], axis)
    return {'x': out['x'], 'c': out['c'], 'w_mod': out['w_mod'], 'b_mod': out['b_mod'], 'g_ffn1': out['g_ffn1'], 'w_ffn1_in': out['w_ffn1_in'], 'w_ffn1_out': out['w_ffn1_out'], 'g_mix': out['g_mix'], 'w_in': out['w_in'], 'conv_w': out['conv_w'], 'conv_b': out['conv_b'], 'ln_g': out['ln_g'], 'ln_b': out['ln_b'], 'rnn_conv_w': out['rnn_conv_w'], 'rnn_conv_b': out['rnn_conv_b'], 'w_a': out['w_a'], 'b_a': out['b_a'], 'w_i': out['w_i'], 'b_i': out['b_i'], 'lru_lambda': out['lru_lambda'], 'w_out': out['w_out'], 'g_ffn2': out['g_ffn2'], 'w_ffn2_in': out['w_ffn2_in'], 'w_ffn2_out': out['w_ffn2_out'], 'w_fmod': out['w_fmod'], 'b_fmod': out['b_fmod'], 'g_final': out['g_final'], 'loss_target': out['loss_target'], 'm_w_mod': out['m_w_mod'], 'm_b_mod': out['m_b_mod'], 'm_g_ffn1': out['m_g_ffn1'], 'm_w_ffn1_in': out['m_w_ffn1_in'], 'm_w_ffn1_out': out['m_w_ffn1_out'], 'm_g_mix': out['m_g_mix'], 'm_w_in': out['m_w_in'], 'm_conv_w': out['m_conv_w'], 'm_conv_b': out['m_conv_b'], 'm_ln_g': out['m_ln_g'], 'm_ln_b': out['m_ln_b'], 'm_rnn_conv_w': out['m_rnn_conv_w'], 'm_rnn_conv_b': out['m_rnn_conv_b'], 'm_w_a': out['m_w_a'], 'm_b_a': out['m_b_a'], 'm_w_i': out['m_w_i'], 'm_b_i': out['m_b_i'], 'm_lru_lambda': out['m_lru_lambda'], 'm_w_out': out['m_w_out'], 'm_g_ffn2': out['m_g_ffn2'], 'm_w_ffn2_in': out['m_w_ffn2_in'], 'm_w_ffn2_out': out['m_w_ffn2_out'], 'm_w_fmod': out['m_w_fmod'], 'm_b_fmod': out['m_b_fmod'], 'm_g_final': out['m_g_final'], 'v_w_mod': out['v_w_mod'], 'v_b_mod': out['v_b_mod'], 'v_g_ffn1': out['v_g_ffn1'], 'v_w_ffn1_in': out['v_w_ffn1_in'], 'v_w_ffn1_out': out['v_w_ffn1_out'], 'v_g_mix': out['v_g_mix'], 'v_w_in': out['v_w_in'], 'v_conv_w': out['v_conv_w'], 'v_conv_b': out['v_conv_b'], 'v_ln_g': out['v_ln_g'], 'v_ln_b': out['v_ln_b'], 'v_rnn_conv_w': out['v_rnn_conv_w'], 'v_rnn_conv_b': out['v_rnn_conv_b'], 'v_w_a': out['v_w_a'], 'v_b_a': out['v_b_a'], 'v_w_i': out['v_w_i'], 'v_b_i': out['v_b_i'], 'v_lru_lambda': out['v_lru_lambda'], 'v_w_out': out['v_w_out'], 'v_g_ffn2': out['v_g_ffn2'], 'v_w_ffn2_in': out['v_w_ffn2_in'], 'v_w_ffn2_out': out['v_w_ffn2_out'], 'v_w_fmod': out['v_w_fmod'], 'v_b_fmod': out['v_b_fmod'], 'v_g_final': out['v_g_final']}


def _loss(weights, diff, rest, loss_target):
    with _jax.named_scope("forward"):
        args = {**rest, TWIN_DIFF_INPUT: diff, **{k: w.astype(_WEIGHT_DTYPES[k]) for k, w in weights.items()}}
        y = _forward(args)
    with _jax.named_scope("loss_head"):
        err = _jnp.square(y.astype(_jnp.float32) - loss_target)
        return 0.5 * _jnp.sum(_jnp.mean(err, axis=-1)) if err.ndim else 0.5 * err


def _adamw(w, g, m, v):
    m = ADAM_B1 * m + (1.0 - ADAM_B1) * g
    v = ADAM_B2 * v + (1.0 - ADAM_B2) * _jnp.square(g)
    m_hat = m / (1.0 - ADAM_B1 ** ADAM_STEP)
    v_hat = v / (1.0 - ADAM_B2 ** ADAM_STEP)
    delta = -ADAM_LR * (m_hat / (_jnp.sqrt(v_hat) + ADAM_EPS) + ADAM_WD * w)
    return delta, m, v


def reference(x, c, w_mod, b_mod, g_ffn1, w_ffn1_in, w_ffn1_out, g_mix, w_in, conv_w, conv_b, ln_g, ln_b, rnn_conv_w, rnn_conv_b, w_a, b_a, w_i, b_i, lru_lambda, w_out, g_ffn2, w_ffn2_in, w_ffn2_out, w_fmod, b_fmod, g_final, loss_target, m_w_mod, m_b_mod, m_g_ffn1, m_w_ffn1_in, m_w_ffn1_out, m_g_mix, m_w_in, m_conv_w, m_conv_b, m_ln_g, m_ln_b, m_rnn_conv_w, m_rnn_conv_b, m_w_a, m_b_a, m_w_i, m_b_i, m_lru_lambda, m_w_out, m_g_ffn2, m_w_ffn2_in, m_w_ffn2_out, m_w_fmod, m_b_fmod, m_g_final, v_w_mod, v_b_mod, v_g_ffn1, v_w_ffn1_in, v_w_ffn1_out, v_g_mix, v_w_in, v_conv_w, v_conv_b, v_ln_g, v_ln_b, v_rnn_conv_w, v_rnn_conv_b, v_w_a, v_b_a, v_w_i, v_b_i, v_lru_lambda, v_w_out, v_g_ffn2, v_w_ffn2_in, v_w_ffn2_out, v_w_fmod, v_b_fmod, v_g_final):
    given = dict(x=x, c=c, w_mod=w_mod, b_mod=b_mod, g_ffn1=g_ffn1, w_ffn1_in=w_ffn1_in, w_ffn1_out=w_ffn1_out, g_mix=g_mix, w_in=w_in, conv_w=conv_w, conv_b=conv_b, ln_g=ln_g, ln_b=ln_b, rnn_conv_w=rnn_conv_w, rnn_conv_b=rnn_conv_b, w_a=w_a, b_a=b_a, w_i=w_i, b_i=b_i, lru_lambda=lru_lambda, w_out=w_out, g_ffn2=g_ffn2, w_ffn2_in=w_ffn2_in, w_ffn2_out=w_ffn2_out, w_fmod=w_fmod, b_fmod=b_fmod, g_final=g_final, loss_target=loss_target, m_w_mod=m_w_mod, m_b_mod=m_b_mod, m_g_ffn1=m_g_ffn1, m_w_ffn1_in=m_w_ffn1_in, m_w_ffn1_out=m_w_ffn1_out, m_g_mix=m_g_mix, m_w_in=m_w_in, m_conv_w=m_conv_w, m_conv_b=m_conv_b, m_ln_g=m_ln_g, m_ln_b=m_ln_b, m_rnn_conv_w=m_rnn_conv_w, m_rnn_conv_b=m_rnn_conv_b, m_w_a=m_w_a, m_b_a=m_b_a, m_w_i=m_w_i, m_b_i=m_b_i, m_lru_lambda=m_lru_lambda, m_w_out=m_w_out, m_g_ffn2=m_g_ffn2, m_w_ffn2_in=m_w_ffn2_in, m_w_ffn2_out=m_w_ffn2_out, m_w_fmod=m_w_fmod, m_b_fmod=m_b_fmod, m_g_final=m_g_final, v_w_mod=v_w_mod, v_b_mod=v_b_mod, v_g_ffn1=v_g_ffn1, v_w_ffn1_in=v_w_ffn1_in, v_w_ffn1_out=v_w_ffn1_out, v_g_mix=v_g_mix, v_w_in=v_w_in, v_conv_w=v_conv_w, v_conv_b=v_conv_b, v_ln_g=v_ln_g, v_ln_b=v_ln_b, v_rnn_conv_w=v_rnn_conv_w, v_rnn_conv_b=v_rnn_conv_b, v_w_a=v_w_a, v_b_a=v_b_a, v_w_i=v_w_i, v_b_i=v_b_i, v_lru_lambda=v_lru_lambda, v_w_out=v_w_out, v_g_ffn2=v_g_ffn2, v_w_ffn2_in=v_w_ffn2_in, v_w_ffn2_out=v_w_ffn2_out, v_w_fmod=v_w_fmod, v_b_fmod=v_b_fmod, v_g_final=v_g_final)
    weights = {n: given[n] for n in TWIN_WEIGHTS}
    shared = {n: given[n] for n in SHARED_INPUTS}
    per_example = {n: given[n] for n in ['x', 'c']}
    grad_fn = _jax.value_and_grad(_loss, argnums=(0, 1))

    def one_microbatch(ex, loss_target):
        ex = dict(ex)
        diff = ex.pop(TWIN_DIFF_INPUT)
        return grad_fn(weights, diff, {**shared, **ex}, loss_target)

    if N_MICROBATCH == 1:
        loss, (grad_w, grad_x) = one_microbatch(per_example, given["loss_target"])
    else:
        def body(carry, xs):
            loss_sum, grad_sum = carry
            l_k, (gw_k, gx_k) = one_microbatch(xs[0], xs[1])
            with _jax.named_scope("update"):
                return (loss_sum + l_k, _jax.tree.map(_jnp.add, grad_sum, gw_k)), gx_k

        init = (_jnp.zeros((), _jnp.float32), _jax.tree.map(_jnp.zeros_like, weights))
        (loss, grad_w), grad_x = _jax.lax.scan(body, init, (per_example, given["loss_target"]))
    with _jax.named_scope("update"):
        delta_w, new_m, new_v = {}, {}, {}
        for n in TWIN_WEIGHTS:
            delta_w[n], new_m[n], new_v[n] = _adamw(weights[n], grad_w[n], given["m_" + n], given["v_" + n])
    return (loss, grad_x, *[grad_w[n] for n in TWIN_WEIGHTS], *[delta_w[n] for n in TWIN_WEIGHTS],
            *[new_m[n] for n in TWIN_WEIGHTS], *[new_v[n] for n in TWIN_WEIGHTS])
```

```python
import functools

import jax
import jax.numpy as jnp
from jax import lax
from jax.experimental import pallas as pl
from jax.experimental.pallas import tpu as pltpu

F32 = jnp.float32
BF = jnp.bfloat16
I32 = jnp.int32
MESH = pl.DeviceIdType.MESH

EPS = 1e-6
RG_C = 8.0
MACARON_W = 0.5
CONV_WIDTH = 31
RNN_CONV_WIDTH = 4
ADAM_LR = 0.001
ADAM_B1 = 0.9
ADAM_B2 = 0.999
ADAM_EPS = 1e-08
ADAM_WD = 0.01
ADAM_STEP = 10

LANES = 128
SUBLANES = 8
VMEM_LIMIT = 48 * 1024 * 1024
N_CHIPS = 4
N_DEV = 8

R_SH1, R_SC1, R_GT1, R_SH2, R_SC2, R_GT2, R_SH3, R_SC3, R_GT3, R_FSH, R_FSC, R_G1, R_G2, R_G3, R_GF = range(15)

CONTRACT_LAST = (((1,), (1,)), ((), ()))
CONTRACT_FIRST = (((0,), (0,)), ((), ()))


def _pcall(body, **kw):
    return pl.pallas_call(body, **kw)


def _params(sem=None):
    if sem is None:
        return pltpu.CompilerParams(vmem_limit_bytes=VMEM_LIMIT)
    return pltpu.CompilerParams(dimension_semantics=sem, vmem_limit_bytes=VMEM_LIMIT)


def _row(ref, r):
    return ref[r:r + 1, :]


def _sigmoid(x):
    return 1.0 / (1.0 + jnp.exp(-x))


def _colsum(x):
    return jnp.sum(x, axis=0, keepdims=True)


def _rowmean(x):
    return jnp.mean(x, axis=-1, keepdims=True)


def matmul(a, b, mode, *, tm, tn, tk, name, out_dtype=F32, out_cols=None, col_off=0, prev=None):
    if mode == "nn":
        (M, K), (K2, N) = a.shape, b.shape
    elif mode == "nt":
        (M, K), (N, K2) = a.shape, b.shape
    else:
        (K, M), (K2, N) = a.shape, b.shape
    assert K == K2 and M % tm == 0 and N % tn == 0 and K % tk == 0 and col_off % tn == 0
    nk = K // tk
    out_cols = N if out_cols is None else out_cols
    off = col_off // tn

    def body(*refs):
        if prev is None:
            a_ref, b_ref, o_ref, acc = refs
        else:
            a_ref, b_ref, _, o_ref, acc = refs
        k = pl.program_id(2)

        @pl.when(k == 0)
        def _():
            acc[...] = jnp.zeros_like(acc)

        av = a_ref[...].astype(BF)
        bv = b_ref[...].astype(BF)
        if mode == "nn":
            acc[...] += jnp.dot(av, bv, preferred_element_type=F32)
        elif mode == "nt":
            acc[...] += lax.dot_general(av, bv, CONTRACT_LAST, preferred_element_type=F32)
        else:
            acc[...] += lax.dot_general(av, bv, CONTRACT_FIRST, preferred_element_type=F32)

        @pl.when(k == nk - 1)
        def _():
            o_ref[...] = acc[...].astype(out_dtype)

    if mode == "nn":
        a_spec = pl.BlockSpec((tm, tk), lambda m, n, k: (m, k))
        b_spec = pl.BlockSpec((tk, tn), lambda m, n, k: (k, n))
    elif mode == "nt":
        a_spec = pl.BlockSpec((tm, tk), lambda m, n, k: (m, k))
        b_spec = pl.BlockSpec((tn, tk), lambda m, n, k: (n, k))
    else:
        a_spec = pl.BlockSpec((tk, tm), lambda m, n, k: (k, m))
        b_spec = pl.BlockSpec((tk, tn), lambda m, n, k: (k, n))
    in_specs = [a_spec, b_spec]
    args = [a, b]
    aliases = {}
    if prev is not None:
        in_specs.append(pl.BlockSpec(memory_space=pl.ANY))
        args.append(prev)
        aliases = {2: 0}
    return _pcall(
        body, name=name, grid=(M // tm, N // tn, nk), in_specs=in_specs,
        out_specs=pl.BlockSpec((tm, tn), lambda m, n, k: (m, n + off)),
        out_shape=jax.ShapeDtypeStruct((M, out_cols), out_dtype),
        scratch_shapes=[pltpu.VMEM((tm, tn), F32)], input_output_aliases=aliases,
        compiler_params=_params(("parallel", "parallel", "arbitrary")),
    )(*args)


def cond_matmul(c_all, w, name):
    B, K = c_all.shape
    N = w.shape[1]
    tn = 256
    assert N % tn == 0

    def body(c_ref, w_ref, o_ref):
        cv = c_ref[...]
        ca = cv * _sigmoid(cv)
        o_ref[...] = jnp.dot(ca, w_ref[...], preferred_element_type=F32, precision=lax.Precision.HIGHEST)

    return _pcall(
        body, name=name, grid=(N // tn,),
        in_specs=[pl.BlockSpec((B, K), lambda n: (0, 0)), pl.BlockSpec((K, tn), lambda n: (0, n))],
        out_specs=pl.BlockSpec((B, tn), lambda n: (0, n)),
        out_shape=jax.ShapeDtypeStruct((B, N), F32), compiler_params=_params(("parallel",)),
    )(c_all, w)


FFN_TS = 512
FFN_CH = 256


def ffn_fwd(x, vecs, wi, wo, rows, name):
    r_sh, r_sc, r_gt, r_g = rows
    S, D = x.shape
    Fd = wo.shape[0]
    ts, ch = min(FFN_TS, S), FFN_CH
    ni, nj = S // ts, Fd // ch

    def body(x_ref, v_ref, wg_ref, wu_ref, wo_ref, xo_ref, g_ref, u_ref, y_ref, h_sc, acc):
        j = pl.program_id(1)

        @pl.when(j == 0)
        def _():
            xv = x_ref[...]
            r = lax.rsqrt(_rowmean(xv * xv) + EPS)
            gs = _row(v_ref, r_g) * (1.0 + _row(v_ref, r_sc))
            h_sc[...] = (xv * r * gs + _row(v_ref, r_sh)).astype(BF)
            acc[...] = jnp.zeros_like(acc)

        hb = h_sc[...]
        G = jnp.dot(hb, wg_ref[...], preferred_element_type=F32)
        U = jnp.dot(hb, wu_ref[...], preferred_element_type=F32)
        g_ref[...] = G.astype(BF)
        u_ref[...] = U.astype(BF)
        act = (G * _sigmoid(G) * U).astype(BF)
        acc[...] += jnp.dot(act, wo_ref[...], preferred_element_type=F32)

        @pl.when(j == nj - 1)
        def _():
            Y = acc[...]
            y_ref[...] = Y
            xo_ref[...] = x_ref[...] + (MACARON_W * _row(v_ref, r_gt)) * Y

    tok = pl.BlockSpec((ts, D), lambda i, j: (i, 0))
    hid = pl.BlockSpec((ts, ch), lambda i, j: (i, j))
    return _pcall(
        body, name=name, grid=(ni, nj),
        in_specs=[tok, pl.BlockSpec(vecs.shape, lambda i, j: (0, 0)),
                  pl.BlockSpec((D, ch), lambda i, j: (0, j)), pl.BlockSpec((D, ch), lambda i, j: (0, j + nj)),
                  pl.BlockSpec((ch, D), lambda i, j: (j, 0))],
        out_specs=[tok, hid, hid, tok],
        out_shape=[jax.ShapeDtypeStruct((S, D), F32), jax.ShapeDtypeStruct((S, Fd), BF),
                   jax.ShapeDtypeStruct((S, Fd), BF), jax.ShapeDtypeStruct((S, D), F32)],
        scratch_shapes=[pltpu.VMEM((ts, D), BF), pltpu.VMEM((ts, D), F32)],
        compiler_params=_params(("arbitrary", "arbitrary")),
    )(x, vecs, wi, wi, wo)


def ffn_bwd(dxo, x, vecs, gs_, us_, y, wi, wo, rows, name):
    r_sh, r_sc, r_gt, r_g = rows
    S, D = x.shape
    Fd = wo.shape[0]
    ts, ch = min(FFN_TS, S), FFN_CH
    ni, nj = S // ts, Fd // ch

    def body(dxo_ref, x_ref, v_ref, g_ref, u_ref, y_ref, wg_ref, wu_ref, wo_ref,
             dx_ref, act_ref, dg_ref, du_ref, hb_ref, dyb_ref, vg_ref, dyb_sc, dh_sc):
        i, j = pl.program_id(0), pl.program_id(1)

        @pl.when((i == 0) & (j == 0))
        def _():
            vg_ref[...] = jnp.zeros_like(vg_ref)

        @pl.when(j == 0)
        def _():
            dxo_v = dxo_ref[...]
            dyb = ((MACARON_W * _row(v_ref, r_gt)) * dxo_v).astype(BF)
            dyb_sc[...] = dyb
            dyb_ref[...] = dyb
            vg_ref[0:1, :] += MACARON_W * _colsum(dxo_v * y_ref[...])
            dh_sc[...] = jnp.zeros_like(dh_sc)

        dA = lax.dot_general(dyb_sc[...], wo_ref[...], CONTRACT_LAST, preferred_element_type=F32)
        G = g_ref[...].astype(F32)
        U = u_ref[...].astype(F32)
        sg = _sigmoid(G)
        sl = G * sg
        dU = (dA * sl).astype(BF)
        dG = (dA * U * (sg * (1.0 + G * (1.0 - sg)))).astype(BF)
        act_ref[...] = (sl * U).astype(BF)
        dg_ref[...] = dG
        du_ref[...] = dU
        dh_sc[...] += (lax.dot_general(dG, wg_ref[...], CONTRACT_LAST, preferred_element_type=F32)
                       + lax.dot_general(dU, wu_ref[...], CONTRACT_LAST, preferred_element_type=F32))

        @pl.when(j == nj - 1)
        def _():
            dh = dh_sc[...]
            xv = x_ref[...]
            r = lax.rsqrt(_rowmean(xv * xv) + EPS)
            n = xv * r
            g = _row(v_ref, r_g)
            sc1 = 1.0 + _row(v_ref, r_sc)
            gsc = g * sc1
            hb_ref[...] = (n * gsc + _row(v_ref, r_sh)).astype(BF)
            dhn = dh * n
            vg_ref[1:2, :] += _colsum(dh)
            vg_ref[2:3, :] += _colsum(dhn) * g
            vg_ref[3:4, :] += _colsum(dhn) * sc1
            dn = dh * gsc
            dx_ref[...] = dxo_ref[...] + r * (dn - n * _rowmean(dn * n))

    tok = pl.BlockSpec((ts, D), lambda i, j: (i, 0))
    hid = pl.BlockSpec((ts, ch), lambda i, j: (i, j))
    return _pcall(
        body, name=name, grid=(ni, nj),
        in_specs=[tok, tok, pl.BlockSpec(vecs.shape, lambda i, j: (0, 0)), hid, hid, tok,
                  pl.BlockSpec((D, ch), lambda i, j: (0, j)), pl.BlockSpec((D, ch), lambda i, j: (0, j + nj)),
                  pl.BlockSpec((ch, D), lambda i, j: (j, 0))],
        out_specs=[tok, hid, hid, hid, tok, tok, pl.BlockSpec((SUBLANES, D), lambda i, j: (0, 0))],
        out_shape=[jax.ShapeDtypeStruct((S, D), F32), jax.ShapeDtypeStruct((S, Fd), BF),
                   jax.ShapeDtypeStruct((S, Fd), BF), jax.ShapeDtypeStruct((S, Fd), BF),
                   jax.ShapeDtypeStruct((S, D), BF), jax.ShapeDtypeStruct((S, D), BF),
                   jax.ShapeDtypeStruct((SUBLANES, D), F32)],
        scratch_shapes=[pltpu.VMEM((ts, D), BF), pltpu.VMEM((ts, D), F32)],
        compiler_params=_params(("arbitrary", "arbitrary")),
    )(dxo, x, vecs, gs_, us_, y, wi, wi, wo)


def final_fwd_bwd(x, tgt, vecs, name):
    S, D = x.shape
    ts = min(512, S)

    def body(x_ref, t_ref, v_ref, dx_ref, vg_ref):
        @pl.when(pl.program_id(0) == 0)
        def _():
            vg_ref[...] = jnp.zeros_like(vg_ref)

        xv = x_ref[...]
        r = lax.rsqrt(_rowmean(xv * xv) + EPS)
        n = xv * r
        g = _row(v_ref, R_GF)
        sc1 = 1.0 + _row(v_ref, R_FSC)
        gsc = g * sc1
        e = n * gsc + _row(v_ref, R_FSH) - t_ref[...]
        vg_ref[3:4, :] += _colsum(e * e)
        dout = e * (1.0 / D)
        dn_ = dout * n
        vg_ref[0:1, :] += _colsum(dout)
        vg_ref[1:2, :] += _colsum(dn_) * g
        vg_ref[2:3, :] += _colsum(dn_) * sc1
        dn = dout * gsc
        dx_ref[...] = r * (dn - n * _rowmean(dn * n))

    tok = pl.BlockSpec((ts, D), lambda i: (i, 0))
    return _pcall(
        body, name=name, grid=(S // ts,),
        in_specs=[tok, tok, pl.BlockSpec(vecs.shape, lambda i: (0, 0))],
        out_specs=[tok, pl.BlockSpec((SUBLANES, D), lambda i: (0, 0))],
        out_shape=[jax.ShapeDtypeStruct((S, D), F32), jax.ShapeDtypeStruct((SUBLANES, D), F32)],
        compiler_params=_params(("arbitrary",)),
    )(x, tgt, vecs)


def norm_matmul(x, vecs, w, rows, name):
    r_sh, r_sc, r_g = rows
    S, D = x.shape
    N = w.shape[1]
    ts, tn = min(512, S), 512
    nn = N // tn

    def body(x_ref, v_ref, w_ref, o_ref, h_sc):
        @pl.when(pl.program_id(1) == 0)
        def _():
            xv = x_ref[...]
            r = lax.rsqrt(_rowmean(xv * xv) + EPS)
            gs = _row(v_ref, r_g) * (1.0 + _row(v_ref, r_sc))
            h_sc[...] = (xv * r * gs + _row(v_ref, r_sh)).astype(BF)

        o_ref[...] = jnp.dot(h_sc[...], w_ref[...], preferred_element_type=F32)

    return _pcall(
        body, name=name, grid=(S // ts, nn),
        in_specs=[pl.BlockSpec((ts, D), lambda i, n: (i, 0)), pl.BlockSpec(vecs.shape, lambda i, n: (0, 0)),
                  pl.BlockSpec((D, tn), lambda i, n: (0, n))],
        out_specs=pl.BlockSpec((ts, tn), lambda i, n: (i, n)),
        out_shape=jax.ShapeDtypeStruct((S, N), F32),
        scratch_shapes=[pltpu.VMEM((ts, D), BF)],
        compiler_params=_params(("arbitrary", "arbitrary")),
    )(x, vecs, w)


SEQ_TT = 256
CONV_PAD = 32


def conv_fwd(proj, convw4, conv_b, name):
    S = proj.shape[0]
    M = conv_b.shape[1]
    nb = M // LANES
    tt = min(SEQ_TT, S)

    def body(uv_ref, ug_ref, w_ref, b_ref, cq_ref, qp):
        qp[0:CONV_PAD, :] = jnp.zeros((CONV_PAD, LANES), F32)

        def step(t, carry):
            base = pl.multiple_of(t * tt, tt)
            qp[pl.ds(base + CONV_PAD, tt), :] = uv_ref[pl.ds(base, tt), :] * _sigmoid(ug_ref[pl.ds(base, tt), :])
            acc = jnp.broadcast_to(b_ref[...], (tt, LANES))
            for k in range(CONV_WIDTH):
                acc = acc + w_ref[k:k + 1, :] * qp[pl.ds(base + (CONV_PAD - CONV_WIDTH + 1) + k, tt), :]
            cq_ref[pl.ds(base, tt), :] = acc
            return carry

        lax.fori_loop(0, S // tt, step, 0)

    return _pcall(
        body, name=name, grid=(nb,),
        in_specs=[pl.BlockSpec((S, LANES), lambda c: (0, c)), pl.BlockSpec((S, LANES), lambda c: (0, c + nb)),
                  pl.BlockSpec((None, 32, LANES), lambda c: (c, 0, 0)), pl.BlockSpec((1, LANES), lambda c: (0, c))],
        out_specs=pl.BlockSpec((S, LANES), lambda c: (0, c)),
        out_shape=jax.ShapeDtypeStruct((S, M), F32),
        scratch_shapes=[pltpu.VMEM((S + CONV_PAD, LANES), F32)],
        compiler_params=_params(("arbitrary",)),
    )(proj, proj, convw4, conv_b)


def conv_bwd(dcq, proj, convw4, name):
    S, M = dcq.shape
    nb = M // LANES
    tt = min(SEQ_TT, S)
    off = CONV_PAD - CONV_WIDTH + 1

    def body(dcq_ref, uv_ref, ug_ref, w_ref, duv_ref, dug_ref, dw_ref, db_ref, qp, dp, dw8, db8):
        qp[0:CONV_PAD, :] = jnp.zeros((CONV_PAD, LANES), F32)
        dp[S:S + CONV_PAD, :] = jnp.zeros((CONV_PAD, LANES), F32)
        dw8[...] = jnp.zeros_like(dw8)
        db8[...] = jnp.zeros_like(db8)

        def fill(t, carry):
            base = pl.multiple_of(t * tt, tt)
            qp[pl.ds(base + CONV_PAD, tt), :] = uv_ref[pl.ds(base, tt), :] * _sigmoid(ug_ref[pl.ds(base, tt), :])
            dp[pl.ds(base, tt), :] = dcq_ref[pl.ds(base, tt), :]
            return carry

        lax.fori_loop(0, S // tt, fill, 0)

        def step(t, carry):
            base = pl.multiple_of(t * tt, tt)
            d_t = dcq_ref[pl.ds(base, tt), :]
            db8[...] += d_t.reshape(tt // SUBLANES, SUBLANES, LANES).sum(axis=0)
            dq = jnp.zeros((tt, LANES), F32)
            for k in range(CONV_WIDTH):
                prod = d_t * qp[pl.ds(base + off + k, tt), :]
                dw8[k] += prod.reshape(tt // SUBLANES, SUBLANES, LANES).sum(axis=0)
                dq = dq + w_ref[k:k + 1, :] * dp[pl.ds(base + (CONV_WIDTH - 1) - k, tt), :]
            uv = uv_ref[pl.ds(base, tt), :]
            sg = _sigmoid(ug_ref[pl.ds(base, tt), :])
            duv_ref[pl.ds(base, tt), :] = dq * sg
            dug_ref[pl.ds(base, tt), :] = dq * uv * sg * (1.0 - sg)
            return carry

        lax.fori_loop(0, S // tt, step, 0)
        dw_ref[...] = jnp.zeros_like(dw_ref)
        for k in range(CONV_WIDTH):
            dw_ref[k:k + 1, :] = _colsum(dw8[k])
        db_ref[...] = _colsum(db8[...])

    col = lambda o: pl.BlockSpec((S, LANES), lambda c: (0, c + o))
    return _pcall(
        body, name=name, grid=(nb,),
        in_specs=[col(0), col(0), col(nb), pl.BlockSpec((None, 32, LANES), lambda c: (c, 0, 0))],
        out_specs=[col(0), col(0), pl.BlockSpec((None, 32, LANES), lambda c: (c, 0, 0)),
                   pl.BlockSpec((1, LANES), lambda c: (0, c))],
        out_shape=[jax.ShapeDtypeStruct((S, M), F32), jax.ShapeDtypeStruct((S, M), F32),
                   jax.ShapeDtypeStruct((nb, 32, LANES), F32), jax.ShapeDtypeStruct((1, M), F32)],
        scratch_shapes=[pltpu.VMEM((S + CONV_PAD, LANES), F32), pltpu.VMEM((S + CONV_PAD, LANES), F32),
                        pltpu.VMEM((32, SUBLANES, LANES), F32), pltpu.VMEM((SUBLANES, LANES), F32)],
        compiler_params=_params(("arbitrary",)),
    )(dcq, proj, proj, convw4)


def _log_sigmoid(x):
    return jnp.minimum(x, 0.0) - jnp.log(1.0 + jnp.exp(-jnp.abs(x)))


def _rg_gate_terms(ra, ls):
    la = RG_C * ra * ls
    a = jnp.exp(la)
    th = jnp.tanh(la)
    mult = jnp.sqrt(-2.0 * th / (1.0 - th))
    return a, mult


def rnn_fwd(proj, rnnw4, rnn_b, bda, bdi, b_a, b_i, lam, name):
    S = proj.shape[0]
    M = rnn_b.shape[1]
    nb = M // LANES
    tt = min(SEQ_TT, S)
    KW = RNN_CONV_WIDTH

    def body(ux_ref, w_ref, rb_ref, bda_ref, bdi_ref, ba_ref, bi_ref, lam_ref,
             xr_ref, ra_ref, ii_ref, h_ref, uxp, a_sc, b_sc):
        uxp[0:SUBLANES, :] = jnp.zeros((SUBLANES, LANES), F32)
        ls = _log_sigmoid(lam_ref[...])

        def step(t, carry):
            base = pl.multiple_of(t * tt, tt)
            uxp[pl.ds(base + SUBLANES, tt), :] = ux_ref[pl.ds(base, tt), :]
            xr = jnp.broadcast_to(rb_ref[...], (tt, LANES))
            for k in range(KW):
                xr = xr + w_ref[k:k + 1, :] * uxp[pl.ds(base + (SUBLANES - KW + 1) + k, tt), :]
            xb = xr.astype(BF)
            ra = _sigmoid(jnp.dot(xb, bda_ref[...], preferred_element_type=F32) + ba_ref[...])
            ii = _sigmoid(jnp.dot(xb, bdi_ref[...], preferred_element_type=F32) + bi_ref[...])
            a, mult = _rg_gate_terms(ra, ls)
            xr_ref[pl.ds(base, tt), :] = xr
            ra_ref[pl.ds(base, tt), :] = ra
            ii_ref[pl.ds(base, tt), :] = ii
            a_sc[pl.ds(base, tt), :] = a
            b_sc[pl.ds(base, tt), :] = mult * (ii * xr)
            return carry

        lax.fori_loop(0, S // tt, step, 0)

        rows = lax.broadcasted_iota(I32, (SUBLANES, LANES), 0)

        def scan(t, hprev):
            base = pl.multiple_of(t * SUBLANES, SUBLANES)
            A = a_sc[pl.ds(base, SUBLANES), :]
            B = b_sc[pl.ds(base, SUBLANES), :]
            for d in (1, 2, 4):
                As = jnp.where(rows >= d, pltpu.roll(A, d, axis=0), 1.0)
                Bs = jnp.where(rows >= d, pltpu.roll(B, d, axis=0), 0.0)
                B = A * Bs + B
                A = A * As
            hh = B + A * hprev
            h_ref[pl.ds(base, SUBLANES), :] = hh
            return jnp.broadcast_to(hh[SUBLANES - 1:SUBLANES, :], (SUBLANES, LANES))

        lax.fori_loop(0, S // SUBLANES, scan, jnp.zeros((SUBLANES, LANES), F32))

    col = lambda o: pl.BlockSpec((S, LANES), lambda c: (0, c + o))
    vec = pl.BlockSpec((1, LANES), lambda c: (0, c))
    diag = pl.BlockSpec((LANES, LANES), lambda c: (c, c))
    return _pcall(
        body, name=name, grid=(nb,),
        in_specs=[col(2 * nb), pl.BlockSpec((None, SUBLANES, LANES), lambda c: (c, 0, 0)), vec, diag, diag, vec, vec, vec],
        out_specs=[col(0)] * 4,
        out_shape=[jax.ShapeDtypeStruct((S, M), F32)] * 4,
        scratch_shapes=[pltpu.VMEM((S + SUBLANES, LANES), F32), pltpu.VMEM((S, LANES), F32), pltpu.VMEM((S, LANES), F32)],
        compiler_params=_params(("arbitrary",)),
    )(proj, rnnw4, rnn_b, bda, bdi, b_a, b_i, lam)


def rnn_bwd(dhout, h, xr, ra, ii, proj, rnnw4, bda, bdi, lam, name):
    S, M = h.shape
    nb = M // LANES
    tt = min(SEQ_TT, S)
    KW = RNN_CONV_WIDTH
    SL = SUBLANES

    def body(dh_ref, h_ref, xr_ref, ra_ref, ii_ref, ux_ref, w_ref, bda_ref, bdi_ref, lam_ref,
             dux_ref, dwa_ref, dwi_ref, drw_ref, vec_ref,
             a_sc, hp, g_sc, dpa_sc, dpi_sc, dxp, uxp, acc8, drw8):
        zero8 = jnp.zeros((SL, LANES), F32)
        a_sc[S:S + SL, :] = zero8
        hp[0:SL, :] = zero8
        dxp[S:S + SL, :] = zero8
        uxp[0:SL, :] = zero8
        acc8[...] = jnp.zeros_like(acc8)
        drw8[...] = jnp.zeros_like(drw8)
        lamv = lam_ref[...]
        ls = _log_sigmoid(lamv)

        def fill(t, carry):
            base = pl.multiple_of(t * tt, tt)
            a_sc[pl.ds(base, tt), :] = jnp.exp(RG_C * ra_ref[pl.ds(base, tt), :] * ls)
            hp[pl.ds(base + SL, tt), :] = h_ref[pl.ds(base, tt), :]
            uxp[pl.ds(base + SL, tt), :] = ux_ref[pl.ds(base, tt), :]
            return carry

        lax.fori_loop(0, S // tt, fill, 0)

        rows = lax.broadcasted_iota(I32, (SL, LANES), 0)
        nt8 = S // SL

        def rscan(t, gnext):
            base = pl.multiple_of((nt8 - 1 - t) * SL, SL)
            A = a_sc[pl.ds(base + 1, SL), :]
            B = dh_ref[pl.ds(base, SL), :]
            for d in (1, 2, 4):
                As = jnp.where(rows < SL - d, pltpu.roll(A, SL - d, axis=0), 1.0)
                Bs = jnp.where(rows < SL - d, pltpu.roll(B, SL - d, axis=0), 0.0)
                B = A * Bs + B
                A = A * As
            g = B + A * gnext
            g_sc[pl.ds(base, SL), :] = g
            return jnp.broadcast_to(g[0:1, :], (SL, LANES))

        lax.fori_loop(0, nt8, rscan, zero8)

        def red8(v):
            return v.reshape(tt // SL, SL, LANES).sum(axis=0)

        def step(t, carry):
            base = pl.multiple_of(t * tt, tt)
            g = g_sc[pl.ds(base, tt), :]
            hprev = hp[pl.ds(base + SL - 1, tt), :]
            xr_t = xr_ref[pl.ds(base, tt), :]
            ra_t = ra_ref[pl.ds(base, tt), :]
            ii_t = ii_ref[pl.ds(base, tt), :]
            a, mult = _rg_gate_terms(ra_t, ls)
            gx = g * xr_t
            dmult = gx * ii_t
            dii = gx * mult
            dxr = g * (mult * ii_t)
            dla = g * hprev * a - dmult * (a * a) / mult
            acc8[3] += red8(dla * ra_t)
            dpa = dla * (RG_C * ls) * ra_t * (1.0 - ra_t)
            dpi = dii * ii_t * (1.0 - ii_t)
            dpab = dpa.astype(BF)
            dpib = dpi.astype(BF)
            dxr = dxr + (lax.dot_general(dpab, bda_ref[...], CONTRACT_LAST, preferred_element_type=F32)
                         + lax.dot_general(dpib, bdi_ref[...], CONTRACT_LAST, preferred_element_type=F32))
            dpa_sc[pl.ds(base, tt), :] = dpab
            dpi_sc[pl.ds(base, tt), :] = dpib
            dxp[pl.ds(base, tt), :] = dxr
            acc8[0] += red8(dxr)
            acc8[1] += red8(dpa)
            acc8[2] += red8(dpi)
            return carry

        lax.fori_loop(0, S // tt, step, 0)

        def convb(t, carry):
            base = pl.multiple_of(t * tt, tt)
            d_t = dxp[pl.ds(base, tt), :]
            dux = jnp.zeros((tt, LANES), F32)
            for k in range(KW):
                drw8[k] += red8(d_t * uxp[pl.ds(base + (SL - KW + 1) + k, tt), :])
                dux = dux + w_ref[k:k + 1, :] * dxp[pl.ds(base + (KW - 1) - k, tt), :]
            dux_ref[pl.ds(base, tt), :] = dux
            return carry

        lax.fori_loop(0, S // tt, convb, 0)

        xb = xr_ref[...].astype(BF)
        dwa_ref[...] = lax.dot_general(xb, dpa_sc[...], CONTRACT_FIRST, preferred_element_type=F32)
        dwi_ref[...] = lax.dot_general(xb, dpi_sc[...], CONTRACT_FIRST, preferred_element_type=F32)
        drw_ref[...] = jnp.zeros_like(drw_ref)
        vec_ref[...] = jnp.zeros_like(vec_ref)
        for k in range(KW):
            drw_ref[k:k + 1, :] = _colsum(drw8[k])
        for k in range(3):
            vec_ref[k:k + 1, :] = _colsum(acc8[k])
        vec_ref[3:4, :] = _colsum(acc8[3]) * (RG_C * _sigmoid(-lamv))

    col = lambda o: pl.BlockSpec((S, LANES), lambda c: (0, c + o))
    vec = pl.BlockSpec((1, LANES), lambda c: (0, c))
    diag = pl.BlockSpec((LANES, LANES), lambda c: (c, c))
    blk3 = lambda r: pl.BlockSpec((None, r, LANES), lambda c: (c, 0, 0))
    return _pcall(
        body, name=name, grid=(nb,),
        in_specs=[col(0), col(0), col(0), col(0), col(0), col(2 * nb), blk3(SL), diag, diag, vec],
        out_specs=[col(0), blk3(LANES), blk3(LANES), blk3(SL), pl.BlockSpec((SL, LANES), lambda c: (0, c))],
        out_shape=[jax.ShapeDtypeStruct((S, M), F32), jax.ShapeDtypeStruct((nb, LANES, LANES), F32),
                   jax.ShapeDtypeStruct((nb, LANES, LANES), F32), jax.ShapeDtypeStruct((nb, SL, LANES), F32),
                   jax.ShapeDtypeStruct((SL, M), F32)],
        scratch_shapes=[pltpu.VMEM((S + SL, LANES), F32), pltpu.VMEM((S + SL, LANES), F32), pltpu.VMEM((S, LANES), F32),
                        pltpu.VMEM((S, LANES), BF), pltpu.VMEM((S, LANES), BF), pltpu.VMEM((S + SL, LANES), F32),
                        pltpu.VMEM((S + SL, LANES), F32), pltpu.VMEM((SL, SL, LANES), F32), pltpu.VMEM((SL, SL, LANES), F32)],
        compiler_params=_params(("arbitrary",)),
    )(dhout, h, xr, ra, ii, proj, rnnw4, bda, bdi, lam)


GELU_K = 0.7978845608028654
GELU_C = 0.044715


def _layernorm_parts(cq):
    mu = _rowmean(cq)
    d = cq - mu
    rstd = lax.rsqrt(_rowmean(d * d) + EPS)
    return d * rstd, rstd


def mix_out(cq, proj, h, x, vecs, lnv, wout, name):
    S, D = x.shape
    M = cq.shape[1]
    ts = min(512, S)

    def body(cq_ref, uy_ref, h_ref, x_ref, v_ref, ln_ref, w_ref, xo_ref, ym_ref, yc_ref):
        z, _ = _layernorm_parts(cq_ref[...])
        l = z * _row(ln_ref, 0) + _row(ln_ref, 1)
        yc_ref[:, 0:M] = (l * _sigmoid(l)).astype(BF)
        uy = uy_ref[...]
        gelu = 0.5 * uy * (1.0 + jnp.tanh(GELU_K * (uy + GELU_C * uy * uy * uy)))
        yc_ref[:, M:2 * M] = (gelu * h_ref[...]).astype(BF)
        ym = jnp.dot(yc_ref[...], w_ref[...], preferred_element_type=F32)
        ym_ref[...] = ym
        xo_ref[...] = x_ref[...] + _row(v_ref, R_GT2) * ym

    tok = pl.BlockSpec((ts, D), lambda i: (i, 0))
    mtok = lambda o: pl.BlockSpec((ts, M), lambda i: (i, o))
    return _pcall(
        body, name=name, grid=(S // ts,),
        in_specs=[mtok(0), mtok(3), mtok(0), tok, pl.BlockSpec(vecs.shape, lambda i: (0, 0)),
                  pl.BlockSpec(lnv.shape, lambda i: (0, 0)), pl.BlockSpec(wout.shape, lambda i: (0, 0))],
        out_specs=[tok, tok, pl.BlockSpec((ts, 2 * M), lambda i: (i, 0))],
        out_shape=[jax.ShapeDtypeStruct((S, D), F32), jax.ShapeDtypeStruct((S, D), F32),
                   jax.ShapeDtypeStruct((S, 2 * M), BF)],
        compiler_params=_params(("arbitrary",)),
    )(cq, proj, h, x, vecs, lnv, wout)


def mix_out_bwd(dxo, ym, vecs, wout, cq, lnv, proj, h, name):
    S, D = dxo.shape
    M = cq.shape[1]
    ts = min(512, S)

    def body(dxo_ref, ym_ref, v_ref, w_ref, cq_ref, ln_ref, uy_ref, h_ref,
             dcq_ref, dh_ref, duy_ref, dyb_ref, vgd_ref, vgm_ref):
        @pl.when(pl.program_id(0) == 0)
        def _():
            vgd_ref[...] = jnp.zeros_like(vgd_ref)
            vgm_ref[...] = jnp.zeros_like(vgm_ref)

        dxo_v = dxo_ref[...]
        dyb = (_row(v_ref, R_GT2) * dxo_v).astype(BF)
        dyb_ref[...] = dyb
        vgd_ref[0:1, :] += _colsum(dxo_v * ym_ref[...])
        dycat = lax.dot_general(dyb, w_ref[...], CONTRACT_LAST, preferred_element_type=F32)
        dyc = dycat[:, 0:M]
        dyr = dycat[:, M:2 * M]
        z, rstd = _layernorm_parts(cq_ref[...])
        lng = _row(ln_ref, 0)
        l = z * lng + _row(ln_ref, 1)
        sl = _sigmoid(l)
        dl = dyc * (sl * (1.0 + l * (1.0 - sl)))
        vgm_ref[0:1, :] += _colsum(dl * z)
        vgm_ref[1:2, :] += _colsum(dl)
        dz = dl * lng
        dcq_ref[...] = rstd * (dz - _rowmean(dz) - z * _rowmean(dz * z))
        uy = uy_ref[...]
        u2 = uy * uy
        th = jnp.tanh(GELU_K * (uy + GELU_C * uy * u2))
        gelu = 0.5 * uy * (1.0 + th)
        dgelu = 0.5 * (1.0 + th) + 0.5 * uy * (1.0 - th * th) * (GELU_K * (1.0 + 3.0 * GELU_C * u2))
        dh_ref[...] = dyr * gelu
        duy_ref[...] = dyr * h_ref[...] * dgelu

    tok = pl.BlockSpec((ts, D), lambda i: (i, 0))
    mtok = lambda o: pl.BlockSpec((ts, M), lambda i: (i, o))
    return _pcall(
        body, name=name, grid=(S // ts,),
        in_specs=[tok, tok, pl.BlockSpec(vecs.shape, lambda i: (0, 0)), pl.BlockSpec(wout.shape, lambda i: (0, 0)),
                  mtok(0), pl.BlockSpec(lnv.shape, lambda i: (0, 0)), mtok(3), mtok(0)],
        out_specs=[mtok(0), mtok(0), mtok(0), tok, pl.BlockSpec((SUBLANES, D), lambda i: (0, 0)),
                   pl.BlockSpec((SUBLANES, M), lambda i: (0, 0))],
        out_shape=[jax.ShapeDtypeStruct((S, M), F32)] * 3 + [jax.ShapeDtypeStruct((S, D), BF),
                   jax.ShapeDtypeStruct((SUBLANES, D), F32), jax.ShapeDtypeStruct((SUBLANES, M), F32)],
        compiler_params=_params(("arbitrary",)),
    )(dxo, ym, vecs, wout, cq, lnv, proj, h)


def mix_in_bwd(dparts, x, dxo, vecs, win, name):
    S, D = x.shape
    M = dparts[0].shape[1]
    ts = min(512, S)

    def body(d0, d1, d2, d3, x_ref, dxo_ref, v_ref, w_ref, dx_ref, hb_ref, dp_ref, vg_ref):
        @pl.when(pl.program_id(0) == 0)
        def _():
            vg_ref[...] = jnp.zeros_like(vg_ref)

        for q, dref in enumerate((d0, d1, d2, d3)):
            dp_ref[:, q * M:(q + 1) * M] = dref[...].astype(BF)
        dh = lax.dot_general(dp_ref[...], w_ref[...], CONTRACT_LAST, preferred_element_type=F32)
        xv = x_ref[...]
        r = lax.rsqrt(_rowmean(xv * xv) + EPS)
        n = xv * r
        g = _row(v_ref, R_G2)
        sc1 = 1.0 + _row(v_ref, R_SC2)
        gsc = g * sc1
        hb_ref[...] = (n * gsc + _row(v_ref, R_SH2)).astype(BF)
        dhn = dh * n
        vg_ref[0:1, :] += _colsum(dh)
        vg_ref[1:2, :] += _colsum(dhn) * g
        vg_ref[2:3, :] += _colsum(dhn) * sc1
        dn = dh * gsc
        dx_ref[...] = dxo_ref[...] + r * (dn - n * _rowmean(dn * n))

    tok = pl.BlockSpec((ts, D), lambda i: (i, 0))
    mtok = pl.BlockSpec((ts, M), lambda i: (i, 0))
    return _pcall(
        body, name=name, grid=(S // ts,),
        in_specs=[mtok] * 4 + [tok, tok, pl.BlockSpec(vecs.shape, lambda i: (0, 0)), pl.BlockSpec(win.shape, lambda i: (0, 0))],
        out_specs=[tok, tok, pl.BlockSpec((ts, 4 * M), lambda i: (i, 0)), pl.BlockSpec((SUBLANES, D), lambda i: (0, 0))],
        out_shape=[jax.ShapeDtypeStruct((S, D), F32), jax.ShapeDtypeStruct((S, D), BF),
                   jax.ShapeDtypeStruct((S, 4 * M), BF), jax.ShapeDtypeStruct((SUBLANES, D), F32)],
        compiler_params=_params(("arbitrary",)),
    )(*dparts, x, dxo, vecs, win)


def _adamw(w, g, m, v):
    m = ADAM_B1 * m + (1.0 - ADAM_B1) * g
    v = ADAM_B2 * v + (1.0 - ADAM_B2) * (g * g)
    m_hat = m / (1.0 - ADAM_B1 ** ADAM_STEP)
    v_hat = v / (1.0 - ADAM_B2 ** ADAM_STEP)
    delta = -ADAM_LR * (m_hat / (jnp.sqrt(v_hat) + ADAM_EPS) + ADAM_WD * w)
    return delta, m, v


def adam_big(w, g, m, v, name):
    R, C = w.shape
    tr = 256 if R % 256 == 0 else R // 2 if (R // 2) % SUBLANES == 0 and R > 512 else R
    tc = C if C <= 1536 else (1152 if C % 1152 == 0 else 1024)
    assert R % tr == 0 and C % tc == 0

    def body(w_ref, g_ref, m_ref, v_ref, d_ref, nm_ref, nv_ref):
        d, nm, nv = _adamw(w_ref[...], g_ref[...], m_ref[...], v_ref[...])
        d_ref[...] = d
        nm_ref[...] = nm
        nv_ref[...] = nv

    blk = pl.BlockSpec((tr, tc), lambda i, j: (i, j))
    return _pcall(
        body, name=name, grid=(R // tr, C // tc), in_specs=[blk] * 4, out_specs=[blk] * 3,
        out_shape=[jax.ShapeDtypeStruct((R, C), F32)] * 3, compiler_params=_params(("parallel", "parallel")),
    )(w, g, m, v)


def adam_cond(c_all, dmod, w, m, v, name):
    B, Kin = c_all.shape
    N = w.shape[1]
    tn = 256
    assert N % tn == 0

    def body(c_ref, d_ref, w_ref, m_ref, v_ref, g_ref, dl_ref, nm_ref, nv_ref):
        cv = c_ref[...]
        ca = cv * _sigmoid(cv)
        g = lax.dot_general(ca, d_ref[...], CONTRACT_FIRST, preferred_element_type=F32, precision=lax.Precision.HIGHEST)
        d, nm, nv = _adamw(w_ref[...], g, m_ref[...], v_ref[...])
        g_ref[...] = g
        dl_ref[...] = d
        nm_ref[...] = nm
        nv_ref[...] = nv

    blk = pl.BlockSpec((Kin, tn), lambda n: (0, n))
    return _pcall(
        body, name=name, grid=(N // tn,),
        in_specs=[pl.BlockSpec((B, Kin), lambda n: (0, 0)), pl.BlockSpec((B, tn), lambda n: (0, n)), blk, blk, blk],
        out_specs=[blk] * 4, out_shape=[jax.ShapeDtypeStruct((Kin, N), F32)] * 4,
        compiler_params=_params(("parallel",)),
    )(c_all, dmod, w, m, v)


def adam_small(ws, gs, ms, vs, name):
    n = len(ws)

    def body(*refs):
        ins, outs = refs[:4 * n], refs[4 * n:]
        for k in range(n):
            d, nm, nv = _adamw(ins[k][...], ins[n + k][...], ins[2 * n + k][...], ins[3 * n + k][...])
            outs[k][...] = d
            outs[n + k][...] = nm
            outs[2 * n + k][...] = nv

    vm = pl.BlockSpec(memory_space=pltpu.VMEM)
    shapes = [jax.ShapeDtypeStruct(w.shape, F32) for w in ws]
    out = _pcall(body, name=name, in_specs=[vm] * (4 * n), out_specs=[vm] * (3 * n), out_shape=shapes * 3,
                 compiler_params=_params())(*ws, *gs, *ms, *vs)
    return out[:n], out[n:2 * n], out[2 * n:]


def _me():
    return lax.axis_index("x"), lax.axis_index("y"), lax.axis_index("c")


def _flip(x, y, p):
    return (x ^ (p >> 1) if (p >> 1) else x), (y ^ (p & 1) if (p & 1) else y)


def allgather_devices(v, name, with_sum=False):
    R, L = v.shape

    def body(v_ref, out_ref, *rest):
        if with_sum:
            sum_ref, send_sems, recv_sems = rest
        else:
            send_sems, recv_sems = rest
        x, y, c = _me()
        me = 4 * x + 2 * y + c
        out_ref[me] = v_ref[...]
        copies = []
        for p in range(1, N_DEV):
            px, py = _flip(x, y, p >> 1)
            pc = (1 - c) if (p & 1) else c
            peer = 4 * px + 2 * py + pc
            send = pltpu.make_async_remote_copy(src_ref=v_ref, dst_ref=out_ref.at[me], send_sem=send_sems.at[p - 1],
                                                recv_sem=recv_sems.at[p - 1], device_id=(px, py, pc), device_id_type=MESH)
            send.start()
            recv = pltpu.make_async_remote_copy(src_ref=v_ref, dst_ref=out_ref.at[peer], send_sem=send_sems.at[p - 1],
                                                recv_sem=recv_sems.at[p - 1], device_id=(px, py, pc), device_id_type=MESH)
            copies.append((send, recv))
        for send, recv in copies:
            recv.wait_recv()
        for send, recv in copies:
            send.wait_send()
        if with_sum:
            s = out_ref[0]
            for k in range(1, N_DEV):
                s = s + out_ref[k]
            sum_ref[...] = s

    vm = pl.BlockSpec(memory_space=pltpu.VMEM)
    out_shape = [jax.ShapeDtypeStruct((N_DEV, R, L), F32)]
    if with_sum:
        out_shape.append(jax.ShapeDtypeStruct((R, L), F32))
    return _pcall(
        body, name=name, in_specs=[vm], out_specs=[vm] * len(out_shape), out_shape=out_shape,
        scratch_shapes=[pltpu.SemaphoreType.DMA((N_DEV - 1,)), pltpu.SemaphoreType.DMA((N_DEV - 1,))],
        compiler_params=_params(),
    )(v)


def allgather_chips(v, name):
    R, L = v.shape

    def body(v_ref, out_ref, send_sems, recv_sems):
        x, y, c = _me()
        chip = 2 * x + y
        out_ref[chip] = v_ref[...]
        copies = []
        for p in range(1, N_CHIPS):
            px, py = _flip(x, y, p)
            send = pltpu.make_async_remote_copy(src_ref=v_ref, dst_ref=out_ref.at[chip], send_sem=send_sems.at[p - 1],
                                                recv_sem=recv_sems.at[p - 1], device_id=(px, py, c), device_id_type=MESH)
            send.start()
            recv = pltpu.make_async_remote_copy(src_ref=v_ref, dst_ref=out_ref.at[2 * px + py], send_sem=send_sems.at[p - 1],
                                                recv_sem=recv_sems.at[p - 1], device_id=(px, py, c), device_id_type=MESH)
            copies.append((send, recv))
        for send, recv in copies:
            recv.wait_recv()
        for send, recv in copies:
            send.wait_send()

    vm = pl.BlockSpec(memory_space=pltpu.VMEM)
    return _pcall(
        body, name=name, in_specs=[vm], out_specs=vm, out_shape=jax.ShapeDtypeStruct((N_CHIPS, R, L), F32),
        scratch_shapes=[pltpu.SemaphoreType.DMA((N_CHIPS - 1,)), pltpu.SemaphoreType.DMA((N_CHIPS - 1,))],
        compiler_params=_params(),
    )(v)


def _shard_window(ref, kind, shard_shape, chip, half):
    r, c = shard_shape
    hr = r // 2
    if kind == "col":
        return ref.at[pl.ds(pl.multiple_of(half * hr, hr), hr), pl.ds(pl.multiple_of(chip * c, c), c)]
    return ref.at[pl.ds(pl.multiple_of(chip * r + half * hr, hr), hr), :]


def allgather_weights(shards, kinds, name):
    n = len(shards)
    fulls = []
    for s, kind in zip(shards, kinds):
        r, c = s.shape
        fulls.append(jax.ShapeDtypeStruct((r, N_CHIPS * c) if kind == "col" else (N_CHIPS * r, c), s.dtype))

    def body(*refs):
        srcs, outs = refs[:n], refs[n:2 * n]
        send_sems, recv_sems, fsend_sems, frecv_sems, local_sems = refs[2 * n:]
        x, y, c = _me()
        chip = 2 * x + y
        sib = (x, y, 1 - c)
        locals_, sends, fwds = [], [], []
        for i in range(n):
            shp = srcs[i].shape
            hr = shp[0] // 2
            win_all = (outs[i].at[:, pl.ds(pl.multiple_of(chip * shp[1], shp[1]), shp[1])] if kinds[i] == "col"
                       else outs[i].at[pl.ds(pl.multiple_of(chip * shp[0], shp[0]), shp[0]), :])
            lc = pltpu.make_async_copy(srcs[i], win_all, local_sems.at[i])
            lc.start()
            locals_.append(lc)
            my_half = srcs[i].at[pl.ds(pl.multiple_of(c * hr, hr), hr), :]
            for p in range(1, N_CHIPS):
                px, py = _flip(x, y, p)
                k = i * (N_CHIPS - 1) + p - 1
                cp = pltpu.make_async_remote_copy(src_ref=my_half, dst_ref=_shard_window(outs[i], kinds[i], shp, chip, c),
                                                  send_sem=send_sems.at[k], recv_sem=recv_sems.at[k],
                                                  device_id=(px, py, c), device_id_type=MESH)
                cp.start()
                sends.append(cp)
        for i in range(n):
            shp = srcs[i].shape
            for p in range(1, N_CHIPS):
                px, py = _flip(x, y, p)
                k = i * (N_CHIPS - 1) + p - 1
                landed = _shard_window(outs[i], kinds[i], shp, 2 * px + py, c)
                pltpu.make_async_remote_copy(src_ref=landed, dst_ref=landed, send_sem=send_sems.at[k], recv_sem=recv_sems.at[k],
                                             device_id=(px, py, c), device_id_type=MESH).wait_recv()
                fw = pltpu.make_async_remote_copy(src_ref=landed, dst_ref=landed, send_sem=fsend_sems.at[k],
                                                  recv_sem=frecv_sems.at[k], device_id=sib, device_id_type=MESH)
                fw.start()
                fwds.append(fw)
        for i in range(n):
            shp = srcs[i].shape
            for p in range(1, N_CHIPS):
                px, py = _flip(x, y, p)
                k = i * (N_CHIPS - 1) + p - 1
                other = _shard_window(outs[i], kinds[i], shp, 2 * px + py, 1 - c)
                pltpu.make_async_remote_copy(src_ref=other, dst_ref=other, send_sem=fsend_sems.at[k], recv_sem=frecv_sems.at[k],
                                             device_id=sib, device_id_type=MESH).wait_recv()
        for cp in sends + fwds:
            cp.wait_send()
        for lc in locals_:
            lc.wait()

    nk = n * (N_CHIPS - 1)
    anyspec = pl.BlockSpec(memory_space=pl.ANY)
    return _pcall(
        body, name=name, in_specs=[anyspec] * n, out_specs=[anyspec] * n, out_shape=fulls,
        scratch_shapes=[pltpu.SemaphoreType.DMA((nk,)), pltpu.SemaphoreType.DMA((nk,)), pltpu.SemaphoreType.DMA((nk,)),
                        pltpu.SemaphoreType.DMA((nk,)), pltpu.SemaphoreType.DMA((n,))],
        compiler_params=_params(),
    )(*shards)


def _as_halves(g, kind, shard_shape):
    r, c = shard_shape
    if kind == "col":
        return g.reshape(2, r // 2, N_CHIPS * c)
    return g.reshape(N_CHIPS, 2, r // 2, c)


def exchange_sibling_halves(grads, kinds, shard_shapes, name):
    n = len(grads)
    views = [_as_halves(g, k, s) for g, k, s in zip(grads, kinds, shard_shapes)]
    outs = []
    for k, (r, c) in zip(kinds, shard_shapes):
        outs.append(jax.ShapeDtypeStruct((r // 2, N_CHIPS * c) if k == "col" else (N_CHIPS, r // 2, c), F32))

    def body(*refs):
        srcs, dsts = refs[:n], refs[n:2 * n]
        send_sems, recv_sems = refs[2 * n:]
        x, y, c = _me()
        cps = []
        for i in range(n):
            src = srcs[i].at[1 - c] if kinds[i] == "col" else srcs[i].at[:, 1 - c]
            cp = pltpu.make_async_remote_copy(src_ref=src, dst_ref=dsts[i], send_sem=send_sems.at[i], recv_sem=recv_sems.at[i],
                                              device_id=(x, y, 1 - c), device_id_type=MESH)
            cp.start()
            cps.append(cp)
        for cp in cps:
            cp.wait_recv()
        for cp in cps:
            cp.wait_send()

    anyspec = pl.BlockSpec(memory_space=pl.ANY)
    return _pcall(
        body, name=name, in_specs=[anyspec] * n, out_specs=[anyspec] * n, out_shape=outs,
        scratch_shapes=[pltpu.SemaphoreType.DMA((n,)), pltpu.SemaphoreType.DMA((n,))], compiler_params=_params(),
    )(*views)


def add_sibling_half(g, recv, kind, shard_shape, cidx, name):
    r, c = shard_shape
    hr = r // 2
    gv = _as_halves(g, kind, shard_shape)
    tr = hr if hr <= 512 else (256 if hr % 256 == 0 else hr // 2)
    assert hr % tr == 0

    def body(ci_ref, g_ref, r_ref, h_ref, hb_ref):
        s = g_ref[...] + r_ref[...]
        h_ref[...] = s
        hb_ref[...] = s.astype(BF)

    if kind == "col":
        grid = (hr // tr, N_CHIPS)
        g_spec = pl.BlockSpec((None, tr, c), lambda i, k, ci: (ci[0], i, k))
        o_spec = pl.BlockSpec((tr, c), lambda i, k, ci: (i, k))
    else:
        grid = (hr // tr, N_CHIPS)
        g_spec = pl.BlockSpec((None, None, tr, c), lambda i, k, ci: (k, ci[0], i, 0))
        o_spec = pl.BlockSpec((None, tr, c), lambda i, k, ci: (k, i, 0))
    gs = pltpu.PrefetchScalarGridSpec(num_scalar_prefetch=1, grid=grid, in_specs=[g_spec, o_spec], out_specs=[o_spec, o_spec])
    return _pcall(
        body, name=name, grid_spec=gs,
        out_shape=[jax.ShapeDtypeStruct(recv.shape, F32), jax.ShapeDtypeStruct(recv.shape, BF)],
        compiler_params=_params(("parallel", "parallel")),
    )(cidx, gv, recv)


def exchange_chip_pieces(hbs, kinds, shard_shapes, name):
    n = len(hbs)
    outs = [jax.ShapeDtypeStruct((N_CHIPS - 1, r // 2, c), BF) for (r, c) in shard_shapes]

    def body(*refs):
        srcs, dsts = refs[:n], refs[n:2 * n]
        send_sems, recv_sems = refs[2 * n:]
        x, y, c = _me()
        cps = []
        for i in range(n):
            cc = shard_shapes[i][1]
            for p in range(1, N_CHIPS):
                px, py = _flip(x, y, p)
                pchip = 2 * px + py
                src = (srcs[i].at[:, pl.ds(pl.multiple_of(pchip * cc, cc), cc)] if kinds[i] == "col" else srcs[i].at[pchip])
                k = i * (N_CHIPS - 1) + p - 1
                cp = pltpu.make_async_remote_copy(src_ref=src, dst_ref=dsts[i].at[p - 1], send_sem=send_sems.at[k],
                                                  recv_sem=recv_sems.at[k], device_id=(px, py, c), device_id_type=MESH)
                cp.start()
                cps.append(cp)
        for cp in cps:
            cp.wait_recv()
        for cp in cps:
            cp.wait_send()

    nk = n * (N_CHIPS - 1)
    anyspec = pl.BlockSpec(memory_space=pl.ANY)
    return _pcall(
        body, name=name, in_specs=[anyspec] * n, out_specs=[anyspec] * n, out_shape=outs,
        scratch_shapes=[pltpu.SemaphoreType.DMA((nk,)), pltpu.SemaphoreType.DMA((nk,))], compiler_params=_params(),
    )(*hbs)


def sum_chip_pieces(h, pieces, kind, shard_shape, chipidx, name):
    r, c = shard_shape
    hr = r // 2
    tr = hr if hr <= 512 else (256 if hr % 256 == 0 else hr // 2)
    assert hr % tr == 0

    def body(ci_ref, h_ref, p_ref, q_ref):
        q_ref[...] = ((h_ref[...] + p_ref[0].astype(F32)) + p_ref[1].astype(F32)) + p_ref[2].astype(F32)

    if kind == "col":
        h_spec = pl.BlockSpec((tr, c), lambda i, ci: (i, ci[0]))
    else:
        h_spec = pl.BlockSpec((None, tr, c), lambda i, ci: (ci[0], i, 0))
    gs = pltpu.PrefetchScalarGridSpec(
        num_scalar_prefetch=1, grid=(hr // tr,),
        in_specs=[h_spec, pl.BlockSpec((N_CHIPS - 1, tr, c), lambda i, ci: (0, i, 0))],
        out_specs=pl.BlockSpec((tr, c), lambda i, ci: (i, 0)))
    return _pcall(body, name=name, grid_spec=gs, out_shape=jax.ShapeDtypeStruct((hr, c), F32),
                  compiler_params=_params(("parallel",)))(chipidx, h, pieces)


def exchange_reduced_halves(qs, name):
    n = len(qs)
    outs = [jax.ShapeDtypeStruct((2 * q.shape[0], q.shape[1]), F32) for q in qs]

    def body(*refs):
        srcs, dsts = refs[:n], refs[n:2 * n]
        send_sems, recv_sems, local_sems = refs[2 * n:]
        x, y, c = _me()
        cps, lcs = [], []
        for i in range(n):
            hr = srcs[i].shape[0]
            mine = dsts[i].at[pl.ds(pl.multiple_of(c * hr, hr), hr), :]
            lc = pltpu.make_async_copy(srcs[i], mine, local_sems.at[i])
            lc.start()
            lcs.append(lc)
            cp = pltpu.make_async_remote_copy(src_ref=srcs[i], dst_ref=mine, send_sem=send_sems.at[i], recv_sem=recv_sems.at[i],
                                              device_id=(x, y, 1 - c), device_id_type=MESH)
            cp.start()
            other = dsts[i].at[pl.ds(pl.multiple_of((1 - c) * hr, hr), hr), :]
            cps.append((cp, pltpu.make_async_remote_copy(src_ref=srcs[i], dst_ref=other, send_sem=send_sems.at[i],
                                                         recv_sem=recv_sems.at[i], device_id=(x, y, 1 - c), device_id_type=MESH)))
        for cp, rv in cps:
            rv.wait_recv()
        for cp, rv in cps:
            cp.wait_send()
        for lc in lcs:
            lc.wait()

    anyspec = pl.BlockSpec(memory_space=pl.ANY)
    return _pcall(
        body, name=name, in_specs=[anyspec] * n, out_specs=[anyspec] * n, out_shape=outs,
        scratch_shapes=[pltpu.SemaphoreType.DMA((n,)), pltpu.SemaphoreType.DMA((n,)), pltpu.SemaphoreType.DMA((n,))],
        compiler_params=_params(),
    )(*qs)


def _rows128(a):
    return a.reshape(-1, LANES)


def _block_diag(w):
    H, d, _ = w.shape
    eye = jnp.eye(H, dtype=w.dtype)
    return jnp.einsum("hde,hg->hdge", w, eye).reshape(H * d, H * d)


def _diag_blocks(g4, H, d):
    nb = g4.shape[0]
    per = LANES // d
    g = g4.reshape(nb, per, d, per, d)
    return jnp.stack([g[:, j, :, j, :] for j in range(per)], axis=1).reshape(H, d, d)


def kernel(x, c, w_mod, b_mod, g_ffn1, w_ffn1_in, w_ffn1_out, g_mix, w_in, conv_w, conv_b, ln_g, ln_b, rnn_conv_w, rnn_conv_b, w_a, b_a, w_i, b_i, lru_lambda, w_out, g_ffn2, w_ffn2_in, w_ffn2_out, w_fmod, b_fmod, g_final, loss_target, m_w_mod, m_b_mod, m_g_ffn1, m_w_ffn1_in, m_w_ffn1_out, m_g_mix, m_w_in, m_conv_w, m_conv_b, m_ln_g, m_ln_b, m_rnn_conv_w, m_rnn_conv_b, m_w_a, m_b_a, m_w_i, m_b_i, m_lru_lambda, m_w_out, m_g_ffn2, m_w_ffn2_in, m_w_ffn2_out, m_w_fmod, m_b_fmod, m_g_final, v_w_mod, v_b_mod, v_g_ffn1, v_w_ffn1_in, v_w_ffn1_out, v_g_mix, v_w_in, v_conv_w, v_conv_b, v_ln_g, v_ln_b, v_rnn_conv_w, v_rnn_conv_b, v_w_a, v_b_a, v_w_i, v_b_i, v_lru_lambda, v_w_out, v_g_ffn2, v_w_ffn2_in, v_w_ffn2_out, v_w_fmod, v_b_fmod, v_g_final):
    S, D = x.shape[1], x.shape[2]
    M = conv_b.shape[1]
    H, HD = w_a.shape[1], w_a.shape[2]
    nb = M // LANES
    ix, iy, ic = lax.axis_index("x"), lax.axis_index("y"), lax.axis_index("c")
    chip = 2 * ix + iy
    dev = 2 * chip + ic
    cidx = jnp.reshape(ic, (1,)).astype(I32)
    chipidx = jnp.reshape(chip, (1,)).astype(I32)
    xs = x[0]
    tgt = loss_target[0]

    c_all = allgather_devices(_rows128(c), "gather_c")[0].reshape(N_DEV, D)
    mod_cols = cond_matmul(c_all, w_mod[0], "mod_proj")
    fmod_cols = cond_matmul(c_all, w_fmod, "fmod_proj")
    convw_pad = jnp.pad(conv_w[0], ((0, 32 - CONV_WIDTH), (0, 0)))
    rnnw_pad = jnp.pad(rnn_conv_w[0], ((0, SUBLANES - RNN_CONV_WIDTH), (0, 0)))
    n_mod, n_fmod = mod_cols.shape[1], fmod_cols.shape[1]
    small = jnp.concatenate([_rows128(mod_cols), _rows128(fmod_cols), convw_pad, rnnw_pad], axis=0)
    small4 = allgather_chips(small, "gather_cond")
    r0 = N_DEV * n_mod // LANES
    r1 = r0 + N_DEV * n_fmod // LANES
    mod_all = small4[:, :r0].reshape(N_CHIPS, N_DEV, n_mod)
    fmod_all = small4[:, r0:r1].reshape(N_CHIPS, N_DEV, n_fmod)
    convw4 = small4[:, r1:r1 + 32]
    rnnw4 = small4[:, r1 + 32:r1 + 32 + SUBLANES]
    mod_row = lax.dynamic_index_in_dim(mod_all, dev, axis=1, keepdims=False).reshape(1, N_CHIPS * n_mod) + b_mod
    fmod_row = lax.dynamic_index_in_dim(fmod_all, dev, axis=1, keepdims=False).reshape(1, N_CHIPS * n_fmod) + b_fmod[None, :]
    vecs = jnp.concatenate([mod_row.reshape(9, D), fmod_row.reshape(2, D), g_ffn1, g_mix, g_ffn2, g_final[None, :],
                            jnp.zeros((1, D), F32)], axis=0)
    lnv = jnp.concatenate([ln_g, ln_b, jnp.zeros((SUBLANES - 2, M), F32)], axis=0)
    bda = _block_diag(w_a[0]).astype(BF)
    bdi = _block_diag(w_i[0]).astype(BF)

    kinds = ["col", "row", "col", "row", "col", "row"]
    big_w = [w_ffn1_in[0], w_ffn1_out[0], w_in[0], w_out[0], w_ffn2_in[0], w_ffn2_out[0]]
    shard_shapes = [w.shape for w in big_w]
    wi1, wo1, win, wout, wi2, wo2 = allgather_weights([w.astype(BF) for w in big_w], kinds, "gather_weights")

    rows1 = (R_SH1, R_SC1, R_GT1, R_G1)
    rows3 = (R_SH3, R_SC3, R_GT3, R_G3)
    x1, g1s, u1s, y1 = ffn_fwd(xs, vecs, wi1, wo1, rows1, "ffn1_fwd")
    proj = norm_matmul(x1, vecs, win, (R_SH2, R_SC2, R_G2), "mix_in_proj")
    cq = conv_fwd(proj, convw4, conv_b, "conv_fwd")
    xr, ra, ii, hh = rnn_fwd(proj, rnnw4, rnn_conv_b, bda, bdi, b_a, b_i, lru_lambda, "rnn_fwd")
    x2, ym, ycat = mix_out(cq, proj, hh, x1, vecs, lnv, wout, "mix_out")
    x3, g2s, u2s, y2 = ffn_fwd(x2, vecs, wi2, wo2, rows3, "ffn2_fwd")
    dx3, vgf = final_fwd_bwd(x3, tgt, vecs, "final_loss")
    loss = lax.psum((0.5 / D) * jnp.sum(vgf[3]), ("x", "y", "c"))

    Fd = wo1.shape[0]
    tk = min(512, S)
    dx2, act2, dg2, du2, h3b, dy2b, vg3 = ffn_bwd(dx3, x2, vecs, g2s, u2s, y2, wi2, wo2, rows3, "ffn2_bwd")
    gwo2 = matmul(act2, dy2b, "tn", tm=Fd // 2, tn=D, tk=tk, name="ffn2_dwo")
    gwi2 = matmul(h3b, dg2, "tn", tm=D, tn=Fd // 2, tk=tk, name="ffn2_dwg", out_cols=2 * Fd)
    gwi2 = matmul(h3b, du2, "tn", tm=D, tn=Fd // 2, tk=tk, name="ffn2_dwu", out_cols=2 * Fd, col_off=Fd, prev=gwi2)
    dcq, dhout, duy, dymb, vgd, vgm = mix_out_bwd(dx2, ym, vecs, wout, cq, lnv, proj, hh, "mix_out_bwd")
    gwout = matmul(ycat, dymb, "tn", tm=2 * M, tn=D, tk=tk, name="mix_dwout")
    duv, dug, dconvw4, dconvb = conv_bwd(dcq, proj, convw4, "conv_bwd")
    dux, dwa4, dwi4, drnnw4, rvec = rnn_bwd(dhout, hh, xr, ra, ii, proj, rnnw4, bda, bdi, lru_lambda, "rnn_bwd")
    dx1, h2b, dpb, vg2 = mix_in_bwd((duv, dug, dux, duy), x1, dx2, vecs, win, "mix_in_bwd")
    gwin = matmul(h2b, dpb, "tn", tm=D, tn=1024, tk=tk, name="mix_dwin")
    dx0, act1, dg1, du1, h1b, dy1b, vg1 = ffn_bwd(dx1, xs, vecs, g1s, u1s, y1, wi1, wo1, rows1, "ffn1_bwd")
    gwo1 = matmul(act1, dy1b, "tn", tm=Fd // 2, tn=D, tk=tk, name="ffn1_dwo")
    gwi1 = matmul(h1b, dg1, "tn", tm=D, tn=Fd // 2, tk=tk, name="ffn1_dwg", out_cols=2 * Fd)
    gwi1 = matmul(h1b, du1, "tn", tm=D, tn=Fd // 2, tk=tk, name="ffn1_dwu", out_cols=2 * Fd, col_off=Fd, prev=gwi1)

    dmod_row = jnp.concatenate([vg1[1:3], vg1[0:1], vg2[0:2], vgd[0:1], vg3[1:3], vg3[0:1]], axis=0)
    gains = jnp.concatenate([vg1[3:4], vg2[2:3], vg3[3:4], vgf[2:3]], axis=0)
    mvecs = jnp.concatenate([dconvb, vgm[0:2], rvec[0:4], jnp.zeros((1, M), F32)], axis=0)
    parts = [_rows128(dmod_row), _rows128(vgf[0:2]), _rows128(gains), _rows128(mvecs),
             _rows128(dconvw4), _rows128(drnnw4), _rows128(dwa4), _rows128(dwi4)]
    sizes = [p.shape[0] for p in parts]
    packed = jnp.concatenate(parts, axis=0)
    gathered, summed = allgather_devices(packed, "reduce_small", with_sum=True)
    offs = [0]
    for s in sizes:
        offs.append(offs[-1] + s)
    seg = lambda k: summed[offs[k]:offs[k + 1]]
    g_b_mod = seg(0).reshape(1, 9 * D)
    g_b_fmod = seg(1).reshape(1, 2 * D)
    gsum = seg(2).reshape(4, D)
    msum = seg(3).reshape(SUBLANES, M)
    g_conv_w = lax.dynamic_index_in_dim(seg(4).reshape(nb, 32, LANES), chip, axis=0, keepdims=False)[:CONV_WIDTH]
    g_rnn_w = lax.dynamic_index_in_dim(seg(5).reshape(nb, SUBLANES, LANES), chip, axis=0, keepdims=False)[:RNN_CONV_WIDTH]
    g_w_a = _diag_blocks(seg(6).reshape(nb, LANES, LANES), H, HD)
    g_w_i = _diag_blocks(seg(7).reshape(nb, LANES, LANES), H, HD)
    dmod_all = gathered[:, offs[0]:offs[1]].reshape(N_DEV, 9 * D)
    dfmod_all = gathered[:, offs[1]:offs[2]].reshape(N_DEV, 2 * D)
    dmod_cols = lax.dynamic_slice_in_dim(dmod_all, chip * n_mod, n_mod, axis=1)
    dfmod_cols = lax.dynamic_slice_in_dim(dfmod_all, chip * n_fmod, n_fmod, axis=1)

    big_g = [gwi1, gwo1, gwin, gwout, gwi2, gwo2]
    names = ["ffn1_in", "ffn1_out", "w_in", "w_out", "ffn2_in", "ffn2_out"]
    recv1 = exchange_sibling_halves(big_g, kinds, shard_shapes, "reduce_stage1")
    hs, hbs = [], []
    for g, r1_, k, s, nm in zip(big_g, recv1, kinds, shard_shapes, names):
        h_, hb_ = add_sibling_half(g, r1_, k, s, cidx, "add_sibling_" + nm)
        hs.append(h_)
        hbs.append(hb_)
    recv2 = exchange_chip_pieces(hbs, kinds, shard_shapes, "reduce_stage2")
    qs = [sum_chip_pieces(h_, p_, k, s, chipidx, "sum_chips_" + nm)
          for h_, p_, k, s, nm in zip(hs, recv2, kinds, shard_shapes, names)]
    g_big = exchange_reduced_halves(qs, "reduce_stage3")

    big_m = [m_w_ffn1_in[0], m_w_ffn1_out[0], m_w_in[0], m_w_out[0], m_w_ffn2_in[0], m_w_ffn2_out[0]]
    big_v = [v_w_ffn1_in[0], v_w_ffn1_out[0], v_w_in[0], v_w_out[0], v_w_ffn2_in[0], v_w_ffn2_out[0]]
    big_out = [adam_big(w, g, m, v, "adam_" + nm) for w, g, m, v, nm in zip(big_w, g_big, big_m, big_v, names)]
    g_w_mod, d_w_mod, nm_w_mod, nv_w_mod = adam_cond(c_all, dmod_cols, w_mod[0], m_w_mod[0], v_w_mod[0], "adam_w_mod")
    g_w_fmod, d_w_fmod, nm_w_fmod, nv_w_fmod = adam_cond(c_all, dfmod_cols, w_fmod, m_w_fmod, v_w_fmod, "adam_w_fmod")

    flat2 = lambda a: a.reshape(-1, a.shape[-1])
    small_names = ["b_mod", "g_ffn1", "g_mix", "conv_w", "conv_b", "ln_g", "ln_b", "rnn_conv_w", "rnn_conv_b", "w_a", "b_a",
                   "w_i", "b_i", "lru_lambda", "g_ffn2", "b_fmod", "g_final"]
    small_w = [b_mod, g_ffn1, g_mix, conv_w, conv_b, ln_g, ln_b, rnn_conv_w, rnn_conv_b, w_a, b_a, w_i, b_i, lru_lambda,
               g_ffn2, b_fmod, g_final]
    small_m = [m_b_mod, m_g_ffn1, m_g_mix, m_conv_w, m_conv_b, m_ln_g, m_ln_b, m_rnn_conv_w, m_rnn_conv_b, m_w_a, m_b_a,
               m_w_i, m_b_i, m_lru_lambda, m_g_ffn2, m_b_fmod, m_g_final]
    small_v = [v_b_mod, v_g_ffn1, v_g_mix, v_conv_w, v_conv_b, v_ln_g, v_ln_b, v_rnn_conv_w, v_rnn_conv_b, v_w_a, v_b_a,
               v_w_i, v_b_i, v_lru_lambda, v_g_ffn2, v_b_fmod, v_g_final]
    small_g = [g_b_mod, gsum[0:1], gsum[1:2], g_conv_w, msum[0:1], msum[1:2], msum[2:3], g_rnn_w, msum[3:4], g_w_a, msum[4:5],
               g_w_i, msum[5:6], msum[6:7], gsum[2:3], g_b_fmod, gsum[3:4]]
    small_g = [g.reshape(w.shape) for g, w in zip(small_g, small_w)]
    two_d = lambda a: a.reshape(1, -1) if a.ndim == 1 else flat2(a)
    sd, sm, sv = adam_small([two_d(a) for a in small_w], [two_d(a) for a in small_g], [two_d(a) for a in small_m],
                            [two_d(a) for a in small_v], "adam_small")
    small = {}
    for k, nm in enumerate(small_names):
        shp = small_w[k].shape
        small[nm] = (small_g[k], sd[k].reshape(shp), sm[k].reshape(shp), sv[k].reshape(shp))

    big = {"w_mod": tuple(a[None] for a in (g_w_mod, d_w_mod, nm_w_mod, nv_w_mod)),
           "w_fmod": (g_w_fmod, d_w_fmod, nm_w_fmod, nv_w_fmod)}
    for nm, full, g, (d, nmm, nvv) in zip(["w_ffn1_in", "w_ffn1_out", "w_in", "w_out", "w_ffn2_in", "w_ffn2_out"],
                                         big_w, g_big, big_out):
        big[nm] = tuple(a[None] for a in (g, d, nmm, nvv))
    order = ["w_mod", "b_mod", "g_ffn1", "w_ffn1_in", "w_ffn1_out", "g_mix", "w_in", "conv_w", "conv_b", "ln_g", "ln_b",
             "rnn_conv_w", "rnn_conv_b", "w_a", "b_a", "w_i", "b_i", "lru_lambda", "w_out", "g_ffn2", "w_ffn2_in",
             "w_ffn2_out", "w_fmod", "b_fmod", "g_final"]
    table = {**small, **big}
    outs = [loss, dx0[None]]
    for kind_ in range(4):
        outs.extend(table[nm][kind_] for nm in order)
    return tuple(outs)
```

```python
import functools

import jax
import jax.numpy as jnp
from jax import lax
from jax.experimental import pallas as pl
from jax.experimental.pallas import tpu as pltpu

F32 = jnp.float32
BF = jnp.bfloat16
I32 = jnp.int32
MESH = pl.DeviceIdType.MESH

EPS = 1e-6
RG_C = 8.0
MACARON_W = 0.5
CONV_WIDTH = 31
RNN_CONV_WIDTH = 4
ADAM_LR = 0.001
ADAM_B1 = 0.9
ADAM_B2 = 0.999
ADAM_EPS = 1e-08
ADAM_WD = 0.01
ADAM_STEP = 10

LANES = 128
SUBLANES = 8
VMEM_LIMIT = 48 * 1024 * 1024
N_CHIPS = 4
N_DEV = 8

R_SH1, R_SC1, R_GT1, R_SH2, R_SC2, R_GT2, R_SH3, R_SC3, R_GT3, R_FSH, R_FSC, R_G1, R_G2, R_G3, R_GF = range(15)

CONTRACT_LAST = (((1,), (1,)), ((), ()))
CONTRACT_FIRST = (((0,), (0,)), ((), ()))


def _pcall(body, **kw):
    return pl.pallas_call(body, **kw)


def _params(sem=None):
    if sem is None:
        return pltpu.CompilerParams(vmem_limit_bytes=VMEM_LIMIT)
    return pltpu.CompilerParams(dimension_semantics=sem, vmem_limit_bytes=VMEM_LIMIT)


def _row(ref, r):
    return ref[r:r + 1, :]


def _sigmoid(x):
    return 1.0 / (1.0 + jnp.exp(-x))


def _colsum(x):
    return jnp.sum(x, axis=0, keepdims=True)


def _rowmean(x):
    return jnp.mean(x, axis=-1, keepdims=True)


def matmul(a, b, mode, *, tm, tn, tk, name, out_dtype=F32, out_cols=None, col_off=0, prev=None):
    if mode == "nn":
        (M, K), (K2, N) = a.shape, b.shape
    elif mode == "nt":
        (M, K), (N, K2) = a.shape, b.shape
    else:
        (K, M), (K2, N) = a.shape, b.shape
    assert K == K2 and M % tm == 0 and N % tn == 0 and K % tk == 0 and col_off % tn == 0
    nk = K // tk
    out_cols = N if out_cols is None else out_cols
    off = col_off // tn

    def body(*refs):
        if prev is None:
            a_ref, b_ref, o_ref, acc = refs
        else:
            a_ref, b_ref, _, o_ref, acc = refs
        k = pl.program_id(2)

        @pl.when(k == 0)
        def _():
            acc[...] = jnp.zeros_like(acc)

        av = a_ref[...].astype(BF)
        bv = b_ref[...].astype(BF)
        if mode == "nn":
            acc[...] += jnp.dot(av, bv, preferred_element_type=F32)
        elif mode == "nt":
            acc[...] += lax.dot_general(av, bv, CONTRACT_LAST, preferred_element_type=F32)
        else:
            acc[...] += lax.dot_general(av, bv, CONTRACT_FIRST, preferred_element_type=F32)

        @pl.when(k == nk - 1)
        def _():
            o_ref[...] = acc[...].astype(out_dtype)

    if mode == "nn":
        a_spec = pl.BlockSpec((tm, tk), lambda m, n, k: (m, k))
        b_spec = pl.BlockSpec((tk, tn), lambda m, n, k: (k, n))
    elif mode == "nt":
        a_spec = pl.BlockSpec((tm, tk), lambda m, n, k: (m, k))
        b_spec = pl.BlockSpec((tn, tk), lambda m, n, k: (n, k))
    else:
        a_spec = pl.BlockSpec((tk, tm), lambda m, n, k: (k, m))
        b_spec = pl.BlockSpec((tk, tn), lambda m, n, k: (k, n))
    in_specs = [a_spec, b_spec]
    args = [a, b]
    aliases = {}
    if prev is not None:
        in_specs.append(pl.BlockSpec(memory_space=pl.ANY))
        args.append(prev)
        aliases = {2: 0}
    return _pcall(
        body, name=name, grid=(M // tm, N // tn, nk), in_specs=in_specs,
        out_specs=pl.BlockSpec((tm, tn), lambda m, n, k: (m, n + off)),
        out_shape=jax.ShapeDtypeStruct((M, out_cols), out_dtype),
        scratch_shapes=[pltpu.VMEM((tm, tn), F32)], input_output_aliases=aliases,
        compiler_params=_params(("parallel", "parallel", "arbitrary")),
    )(*args)


def cond_matmul(c_all, w, name):
    B, K = c_all.shape
    N = w.shape[1]
    tn = 256
    assert N % tn == 0

    def body(c_ref, w_ref, o_ref):
        cv = c_ref[...]
        ca = cv * _sigmoid(cv)
        o_ref[...] = jnp.dot(ca, w_ref[...], preferred_element_type=F32, precision=lax.Precision.HIGHEST)

    return _pcall(
        body, name=name, grid=(N // tn,),
        in_specs=[pl.BlockSpec((B, K), lambda n: (0, 0)), pl.BlockSpec((K, tn), lambda n: (0, n))],
        out_specs=pl.BlockSpec((B, tn), lambda n: (0, n)),
        out_shape=jax.ShapeDtypeStruct((B, N), F32), compiler_params=_params(("parallel",)),
    )(c_all, w)


FFN_TS = 512
FFN_CH = 256


def ffn_fwd(x, vecs, wi, wo, rows, name):
    r_sh, r_sc, r_gt, r_g = rows
    S, D = x.shape
    Fd = wo.shape[0]
    ts, ch = min(FFN_TS, S), FFN_CH
    ni, nj = S // ts, Fd // ch

    def body(x_ref, v_ref, wg_ref, wu_ref, wo_ref, xo_ref, g_ref, u_ref, y_ref, h_sc, acc):
        j = pl.program_id(1)

        @pl.when(j == 0)
        def _():
            xv = x_ref[...]
            r = lax.rsqrt(_rowmean(xv * xv) + EPS)
            gs = _row(v_ref, r_g) * (1.0 + _row(v_ref, r_sc))
            h_sc[...] = (xv * r * gs + _row(v_ref, r_sh)).astype(BF)
            acc[...] = jnp.zeros_like(acc)

        hb = h_sc[...]
        G = jnp.dot(hb, wg_ref[...], preferred_element_type=F32)
        U = jnp.dot(hb, wu_ref[...], preferred_element_type=F32)
        g_ref[...] = G.astype(BF)
        u_ref[...] = U.astype(BF)
        act = (G * _sigmoid(G) * U).astype(BF)
        acc[...] += jnp.dot(act, wo_ref[...], preferred_element_type=F32)

        @pl.when(j == nj - 1)
        def _():
            Y = acc[...]
            y_ref[...] = Y
            xo_ref[...] = x_ref[...] + (MACARON_W * _row(v_ref, r_gt)) * Y

    tok = pl.BlockSpec((ts, D), lambda i, j: (i, 0))
    hid = pl.BlockSpec((ts, ch), lambda i, j: (i, j))
    return _pcall(
        body, name=name, grid=(ni, nj),
        in_specs=[tok, pl.BlockSpec(vecs.shape, lambda i, j: (0, 0)),
                  pl.BlockSpec((D, ch), lambda i, j: (0, j)), pl.BlockSpec((D, ch), lambda i, j: (0, j + nj)),
                  pl.BlockSpec((ch, D), lambda i, j: (j, 0))],
        out_specs=[tok, hid, hid, tok],
        out_shape=[jax.ShapeDtypeStruct((S, D), F32), jax.ShapeDtypeStruct((S, Fd), BF),
                   jax.ShapeDtypeStruct((S, Fd), BF), jax.ShapeDtypeStruct((S, D), F32)],
        scratch_shapes=[pltpu.VMEM((ts, D), BF), pltpu.VMEM((ts, D), F32)],
        compiler_params=_params(("arbitrary", "arbitrary")),
    )(x, vecs, wi, wi, wo)


def ffn_bwd(dxo, x, vecs, gs_, us_, y, wi, wo, rows, name):
    r_sh, r_sc, r_gt, r_g = rows
    S, D = x.shape
    Fd = wo.shape[0]
    ts, ch = min(FFN_TS, S), FFN_CH
    ni, nj = S // ts, Fd // ch

    def body(dxo_ref, x_ref, v_ref, g_ref, u_ref, y_ref, wg_ref, wu_ref, wo_ref,
             dx_ref, act_ref, dg_ref, du_ref, hb_ref, dyb_ref, vg_ref, dyb_sc, dh_sc):
        i, j = pl.program_id(0), pl.program_id(1)

        @pl.when((i == 0) & (j == 0))
        def _():
            vg_ref[...] = jnp.zeros_like(vg_ref)

        @pl.when(j == 0)
        def _():
            dxo_v = dxo_ref[...]
            dyb = ((MACARON_W * _row(v_ref, r_gt)) * dxo_v).astype(BF)
            dyb_sc[...] = dyb
            dyb_ref[...] = dyb
            vg_ref[0:1, :] += MACARON_W * _colsum(dxo_v * y_ref[...])
            dh_sc[...] = jnp.zeros_like(dh_sc)

        dA = lax.dot_general(dyb_sc[...], wo_ref[...], CONTRACT_LAST, preferred_element_type=F32)
        G = g_ref[...].astype(F32)
        U = u_ref[...].astype(F32)
        sg = _sigmoid(G)
        sl = G * sg
        dU = (dA * sl).astype(BF)
        dG = (dA * U * (sg * (1.0 + G * (1.0 - sg)))).astype(BF)
        act_ref[...] = (sl * U).astype(BF)
        dg_ref[...] = dG
        du_ref[...] = dU
        dh_sc[...] += (lax.dot_general(dG, wg_ref[...], CONTRACT_LAST, preferred_element_type=F32)
                       + lax.dot_general(dU, wu_ref[...], CONTRACT_LAST, preferred_element_type=F32))

        @pl.when(j == nj - 1)
        def _():
            dh = dh_sc[...]
            xv = x_ref[...]
            r = lax.rsqrt(_rowmean(xv * xv) + EPS)
            n = xv * r
            g = _row(v_ref, r_g)
            sc1 = 1.0 + _row(v_ref, r_sc)
            gsc = g * sc1
            hb_ref[...] = (n * gsc + _row(v_ref, r_sh)).astype(BF)
            dhn = dh * n
            vg_ref[1:2, :] += _colsum(dh)
            vg_ref[2:3, :] += _colsum(dhn) * g
            vg_ref[3:4, :] += _colsum(dhn) * sc1
            dn = dh * gsc
            dx_ref[...] = dxo_ref[...] + r * (dn - n * _rowmean(dn * n))

    tok = pl.BlockSpec((ts, D), lambda i, j: (i, 0))
    hid = pl.BlockSpec((ts, ch), lambda i, j: (i, j))
    return _pcall(
        body, name=name, grid=(ni, nj),
        in_specs=[tok, tok, pl.BlockSpec(vecs.shape, lambda i, j: (0, 0)), hid, hid, tok,
                  pl.BlockSpec((D, ch), lambda i, j: (0, j)), pl.BlockSpec((D, ch), lambda i, j: (0, j + nj)),
                  pl.BlockSpec((ch, D), lambda i, j: (j, 0))],
        out_specs=[tok, hid, hid, hid, tok, tok, pl.BlockSpec((SUBLANES, D), lambda i, j: (0, 0))],
        out_shape=[jax.ShapeDtypeStruct((S, D), F32), jax.ShapeDtypeStruct((S, Fd), BF),
                   jax.ShapeDtypeStruct((S, Fd), BF), jax.ShapeDtypeStruct((S, Fd), BF),
                   jax.ShapeDtypeStruct((S, D), BF), jax.ShapeDtypeStruct((S, D), BF),
                   jax.ShapeDtypeStruct((SUBLANES, D), F32)],
        scratch_shapes=[pltpu.VMEM((ts, D), BF), pltpu.VMEM((ts, D), F32)],
        compiler_params=_params(("arbitrary", "arbitrary")),
    )(dxo, x, vecs, gs_, us_, y, wi, wi, wo)


def final_fwd_bwd(x, tgt, vecs, name):
    S, D = x.shape
    ts = min(512, S)

    def body(x_ref, t_ref, v_ref, dx_ref, vg_ref):
        @pl.when(pl.program_id(0) == 0)
        def _():
            vg_ref[...] = jnp.zeros_like(vg_ref)

        xv = x_ref[...]
        r = lax.rsqrt(_rowmean(xv * xv) + EPS)
        n = xv * r
        g = _row(v_ref, R_GF)
        sc1 = 1.0 + _row(v_ref, R_FSC)
        gsc = g * sc1
        e = n * gsc + _row(v_ref, R_FSH) - t_ref[...]
        vg_ref[3:4, :] += _colsum(e * e)
        dout = e * (1.0 / D)
        dn_ = dout * n
        vg_ref[0:1, :] += _colsum(dout)
        vg_ref[1:2, :] += _colsum(dn_) * g
        vg_ref[2:3, :] += _colsum(dn_) * sc1
        dn = dout * gsc
        dx_ref[...] = r * (dn - n * _rowmean(dn * n))

    tok = pl.BlockSpec((ts, D), lambda i: (i, 0))
    return _pcall(
        body, name=name, grid=(S // ts,),
        in_specs=[tok, tok, pl.BlockSpec(vecs.shape, lambda i: (0, 0))],
        out_specs=[tok, pl.BlockSpec((SUBLANES, D), lambda i: (0, 0))],
        out_shape=[jax.ShapeDtypeStruct((S, D), F32), jax.ShapeDtypeStruct((SUBLANES, D), F32)],
        compiler_params=_params(("arbitrary",)),
    )(x, tgt, vecs)


def norm_matmul(x, vecs, w, rows, name):
    r_sh, r_sc, r_g = rows
    S, D = x.shape
    N = w.shape[1]
    ts, tn = min(512, S), 512
    nn = N // tn

    def body(x_ref, v_ref, w_ref, o_ref, h_sc):
        @pl.when(pl.program_id(1) == 0)
        def _():
            xv = x_ref[...]
            r = lax.rsqrt(_rowmean(xv * xv) + EPS)
            gs = _row(v_ref, r_g) * (1.0 + _row(v_ref, r_sc))
            h_sc[...] = (xv * r * gs + _row(v_ref, r_sh)).astype(BF)

        o_ref[...] = jnp.dot(h_sc[...], w_ref[...], preferred_element_type=F32)

    return _pcall(
        body, name=name, grid=(S // ts, nn),
        in_specs=[pl.BlockSpec((ts, D), lambda i, n: (i, 0)), pl.BlockSpec(vecs.shape, lambda i, n: (0, 0)),
                  pl.BlockSpec((D, tn), lambda i, n: (0, n))],
        out_specs=pl.BlockSpec((ts, tn), lambda i, n: (i, n)),
        out_shape=jax.ShapeDtypeStruct((S, N), F32),
        scratch_shapes=[pltpu.VMEM((ts, D), BF)],
        compiler_params=_params(("arbitrary", "arbitrary")),
    )(x, vecs, w)


SEQ_TT = 256
CONV_PAD = 32


def conv_fwd(proj, convw4, conv_b, name):
    S = proj.shape[0]
    M = conv_b.shape[1]
    nb = M // LANES
    tt = min(SEQ_TT, S)

    def body(uv_ref, ug_ref, w_ref, b_ref, cq_ref, qp):
        qp[0:CONV_PAD, :] = jnp.zeros((CONV_PAD, LANES), F32)

        def step(t, carry):
            base = pl.multiple_of(t * tt, tt)
            qp[pl.ds(base + CONV_PAD, tt), :] = uv_ref[pl.ds(base, tt), :] * _sigmoid(ug_ref[pl.ds(base, tt), :])
            acc = jnp.broadcast_to(b_ref[...], (tt, LANES))
            for k in range(CONV_WIDTH):
                acc = acc + w_ref[k:k + 1, :] * qp[pl.ds(base + (CONV_PAD - CONV_WIDTH + 1) + k, tt), :]
            cq_ref[pl.ds(base, tt), :] = acc
            return carry

        lax.fori_loop(0, S // tt, step, 0)

    return _pcall(
        body, name=name, grid=(nb,),
        in_specs=[pl.BlockSpec((S, LANES), lambda c: (0, c)), pl.BlockSpec((S, LANES), lambda c: (0, c + nb)),
                  pl.BlockSpec((None, 32, LANES), lambda c: (c, 0, 0)), pl.BlockSpec((1, LANES), lambda c: (0, c))],
        out_specs=pl.BlockSpec((S, LANES), lambda c: (0, c)),
        out_shape=jax.ShapeDtypeStruct((S, M), F32),
        scratch_shapes=[pltpu.VMEM((S + CONV_PAD, LANES), F32)],
        compiler_params=_params(("arbitrary",)),
    )(proj, proj, convw4, conv_b)


def conv_bwd(dcq, proj, convw4, name):
    S, M = dcq.shape
    nb = M // LANES
    tt = min(SEQ_TT, S)
    off = CONV_PAD - CONV_WIDTH + 1

    def body(dcq_ref, uv_ref, ug_ref, w_ref, duv_ref, dug_ref, dw_ref, db_ref, qp, dp, dw8, db8):
        qp[0:CONV_PAD, :] = jnp.zeros((CONV_PAD, LANES), F32)
        dp[S:S + CONV_PAD, :] = jnp.zeros((CONV_PAD, LANES), F32)
        dw8[...] = jnp.zeros_like(dw8)
        db8[...] = jnp.zeros_like(db8)

        def fill(t, carry):
            base = pl.multiple_of(t * tt, tt)
            qp[pl.ds(base + CONV_PAD, tt), :] = uv_ref[pl.ds(base, tt), :] * _sigmoid(ug_ref[pl.ds(base, tt), :])
            dp[pl.ds(base, tt), :] = dcq_ref[pl.ds(base, tt), :]
            return carry

        lax.fori_loop(0, S // tt, fill, 0)

        def step(t, carry):
            base = pl.multiple_of(t * tt, tt)
            d_t = dcq_ref[pl.ds(base, tt), :]
            db8[...] += d_t.reshape(tt // SUBLANES, SUBLANES, LANES).sum(axis=0)
            dq = jnp.zeros((tt, LANES), F32)
            for k in range(CONV_WIDTH):
                prod = d_t * qp[pl.ds(base + off + k, tt), :]
                dw8[k] += prod.reshape(tt // SUBLANES, SUBLANES, LANES).sum(axis=0)
                dq = dq + w_ref[k:k + 1, :] * dp[pl.ds(base + (CONV_WIDTH - 1) - k, tt), :]
            uv = uv_ref[pl.ds(base, tt), :]
            sg = _sigmoid(ug_ref[pl.ds(base, tt), :])
            duv_ref[pl.ds(base, tt), :] = dq * sg
            dug_ref[pl.ds(base, tt), :] = dq * uv * sg * (1.0 - sg)
            return carry

        lax.fori_loop(0, S // tt, step, 0)
        dw_ref[...] = jnp.zeros_like(dw_ref)
        for k in range(CONV_WIDTH):
            dw_ref[k:k + 1, :] = _colsum(dw8[k])
        db_ref[...] = _colsum(db8[...])

    col = lambda o: pl.BlockSpec((S, LANES), lambda c: (0, c + o))
    return _pcall(
        body, name=name, grid=(nb,),
        in_specs=[col(0), col(0), col(nb), pl.BlockSpec((None, 32, LANES), lambda c: (c, 0, 0))],
        out_specs=[col(0), col(0), pl.BlockSpec((None, 32, LANES), lambda c: (c, 0, 0)),
                   pl.BlockSpec((1, LANES), lambda c: (0, c))],
        out_shape=[jax.ShapeDtypeStruct((S, M), F32), jax.ShapeDtypeStruct((S, M), F32),
                   jax.ShapeDtypeStruct((nb, 32, LANES), F32), jax.ShapeDtypeStruct((1, M), F32)],
        scratch_shapes=[pltpu.VMEM((S + CONV_PAD, LANES), F32), pltpu.VMEM((S + CONV_PAD, LANES), F32),
                        pltpu.VMEM((32, SUBLANES, LANES), F32), pltpu.VMEM((SUBLANES, LANES), F32)],
        compiler_params=_params(("arbitrary",)),
    )(dcq, proj, proj, convw4)


def _log_sigmoid(x):
    return jnp.minimum(x, 0.0) - jnp.log(1.0 + jnp.exp(-jnp.abs(x)))


def _rg_gate_terms(ra, ls):
    la = RG_C * ra * ls
    a = jnp.exp(la)
    th = jnp.tanh(la)
    mult = jnp.sqrt(-2.0 * th / (1.0 - th))
    return a, mult


def rnn_fwd(proj, rnnw4, rnn_b, bda, bdi, b_a, b_i, lam, name):
    S = proj.shape[0]
    M = rnn_b.shape[1]
    nb = M // LANES
    tt = min(SEQ_TT, S)
    KW = RNN_CONV_WIDTH

    def body(ux_ref, w_ref, rb_ref, bda_ref, bdi_ref, ba_ref, bi_ref, lam_ref,
             xr_ref, ra_ref, ii_ref, h_ref, uxp, a_sc, b_sc):
        uxp[0:SUBLANES, :] = jnp.zeros((SUBLANES, LANES), F32)
        ls = _log_sigmoid(lam_ref[...])

        def step(t, carry):
            base = pl.multiple_of(t * tt, tt)
            uxp[pl.ds(base + SUBLANES, tt), :] = ux_ref[pl.ds(base, tt), :]
            xr = jnp.broadcast_to(rb_ref[...], (tt, LANES))
            for k in range(KW):
                xr = xr + w_ref[k:k + 1, :] * uxp[pl.ds(base + (SUBLANES - KW + 1) + k, tt), :]
            xb = xr.astype(BF)
            ra = _sigmoid(jnp.dot(xb, bda_ref[...], preferred_element_type=F32) + ba_ref[...])
            ii = _sigmoid(jnp.dot(xb, bdi_ref[...], preferred_element_type=F32) + bi_ref[...])
            a, mult = _rg_gate_terms(ra, ls)
            xr_ref[pl.ds(base, tt), :] = xr
            ra_ref[pl.ds(base, tt), :] = ra
            ii_ref[pl.ds(base, tt), :] = ii
            a_sc[pl.ds(base, tt), :] = a
            b_sc[pl.ds(base, tt), :] = mult * (ii * xr)
            return carry

        lax.fori_loop(0, S // tt, step, 0)

        rows = lax.broadcasted_iota(I32, (SUBLANES, LANES), 0)

        def scan(t, hprev):
            base = pl.multiple_of(t * SUBLANES, SUBLANES)
            A = a_sc[pl.ds(base, SUBLANES), :]
            B = b_sc[pl.ds(base, SUBLANES), :]
            for d in (1, 2, 4):
                As = jnp.where(rows >= d, pltpu.roll(A, d, axis=0), 1.0)
                Bs = jnp.where(rows >= d, pltpu.roll(B, d, axis=0), 0.0)
                B = A * Bs + B
                A = A * As
            hh = B + A * hprev
            h_ref[pl.ds(base, SUBLANES), :] = hh
            return jnp.broadcast_to(hh[SUBLANES - 1:SUBLANES, :], (SUBLANES, LANES))

        lax.fori_loop(0, S // SUBLANES, scan, jnp.zeros((SUBLANES, LANES), F32))

    col = lambda o: pl.BlockSpec((S, LANES), lambda c: (0, c + o))
    vec = pl.BlockSpec((1, LANES), lambda c: (0, c))
    diag = pl.BlockSpec((LANES, LANES), lambda c: (c, c))
    return _pcall(
        body, name=name, grid=(nb,),
        in_specs=[col(2 * nb), pl.BlockSpec((None, SUBLANES, LANES), lambda c: (c, 0, 0)), vec, diag, diag, vec, vec, vec],
        out_specs=[col(0)] * 4,
        out_shape=[jax.ShapeDtypeStruct((S, M), F32)] * 4,
        scratch_shapes=[pltpu.VMEM((S + SUBLANES, LANES), F32), pltpu.VMEM((S, LANES), F32), pltpu.VMEM((S, LANES), F32)],
        compiler_params=_params(("arbitrary",)),
    )(proj, rnnw4, rnn_b, bda, bdi, b_a, b_i, lam)


def rnn_bwd(dhout, h, xr, ra, ii, proj, rnnw4, bda, bdi, lam, name):
    S, M = h.shape
    nb = M // LANES
    tt = min(SEQ_TT, S)
    KW = RNN_CONV_WIDTH
    SL = SUBLANES

    def body(dh_ref, h_ref, xr_ref, ra_ref, ii_ref, ux_ref, w_ref, bda_ref, bdi_ref, lam_ref,
             dux_ref, dwa_ref, dwi_ref, drw_ref, vec_ref,
             a_sc, hp, g_sc, dpa_sc, dpi_sc, dxp, uxp, acc8, drw8):
        zero8 = jnp.zeros((SL, LANES), F32)
        a_sc[S:S + SL, :] = zero8
        hp[0:SL, :] = zero8
        dxp[S:S + SL, :] = zero8
        uxp[0:SL, :] = zero8
        acc8[...] = jnp.zeros_like(acc8)
        drw8[...] = jnp.zeros_like(drw8)
        lamv = lam_ref[...]
        ls = _log_sigmoid(lamv)

        def fill(t, carry):
            base = pl.multiple_of(t * tt, tt)
            a_sc[pl.ds(base, tt), :] = jnp.exp(RG_C * ra_ref[pl.ds(base, tt), :] * ls)
            hp[pl.ds(base + SL, tt), :] = h_ref[pl.ds(base, tt), :]
            uxp[pl.ds(base + SL, tt), :] = ux_ref[pl.ds(base, tt), :]
            return carry

        lax.fori_loop(0, S // tt, fill, 0)

        rows = lax.broadcasted_iota(I32, (SL, LANES), 0)
        nt8 = S // SL

        def rscan(t, gnext):
            base = pl.multiple_of((nt8 - 1 - t) * SL, SL)
            A = a_sc[pl.ds(base + 1, SL), :]
            B = dh_ref[pl.ds(base, SL), :]
            for d in (1, 2, 4):
                As = jnp.where(rows < SL - d, pltpu.roll(A, SL - d, axis=0), 1.0)
                Bs = jnp.where(rows < SL - d, pltpu.roll(B, SL - d, axis=0), 0.0)
                B = A * Bs + B
                A = A * As
            g = B + A * gnext
            g_sc[pl.ds(base, SL), :] = g
            return jnp.broadcast_to(g[0:1, :], (SL, LANES))

        lax.fori_loop(0, nt8, rscan, zero8)

        def red8(v):
            return v.reshape(tt // SL, SL, LANES).sum(axis=0)

        def step(t, carry):
            base = pl.multiple_of(t * tt, tt)
            g = g_sc[pl.ds(base, tt), :]
            hprev = hp[pl.ds(base + SL - 1, tt), :]
            xr_t = xr_ref[pl.ds(base, tt), :]
            ra_t = ra_ref[pl.ds(base, tt), :]
            ii_t = ii_ref[pl.ds(base, tt), :]
            a, mult = _rg_gate_terms(ra_t, ls)
            gx = g * xr_t
            dmult = gx * ii_t
            dii = gx * mult
            dxr = g * (mult * ii_t)
            dla = g * hprev * a - dmult * (a * a) / mult
            acc8[3] += red8(dla * ra_t)
            dpa = dla * (RG_C * ls) * ra_t * (1.0 - ra_t)
            dpi = dii * ii_t * (1.0 - ii_t)
            dpab = dpa.astype(BF)
            dpib = dpi.astype(BF)
            dxr = dxr + (lax.dot_general(dpab, bda_ref[...], CONTRACT_LAST, preferred_element_type=F32)
                         + lax.dot_general(dpib, bdi_ref[...], CONTRACT_LAST, preferred_element_type=F32))
            dpa_sc[pl.ds(base, tt), :] = dpab
            dpi_sc[pl.ds(base, tt), :] = dpib
            dxp[pl.ds(base, tt), :] = dxr
            acc8[0] += red8(dxr)
            acc8[1] += red8(dpa)
            acc8[2] += red8(dpi)
            return carry

        lax.fori_loop(0, S // tt, step, 0)

        def convb(t, carry):
            base = pl.multiple_of(t * tt, tt)
            d_t = dxp[pl.ds(base, tt), :]
            dux = jnp.zeros((tt, LANES), F32)
            for k in range(KW):
                drw8[k] += red8(d_t * uxp[pl.ds(base + (SL - KW + 1) + k, tt), :])
                dux = dux + w_ref[k:k + 1, :] * dxp[pl.ds(base + (KW - 1) - k, tt), :]
            dux_ref[pl.ds(base, tt), :] = dux
            return carry

        lax.fori_loop(0, S // tt, convb, 0)

        xb = xr_ref[...].astype(BF)
        dwa_ref[...] = lax.dot_general(xb, dpa_sc[...], CONTRACT_FIRST, preferred_element_type=F32)
        dwi_ref[...] = lax.dot_general(xb, dpi_sc[...], CONTRACT_FIRST, preferred_element_type=F32)
        drw_ref[...] = jnp.zeros_like(drw_ref)
        vec_ref[...] = jnp.zeros_like(vec_ref)
        for k in range(KW):
            drw_ref[k:k + 1, :] = _colsum(drw8[k])
        for k in range(3):
            vec_ref[k:k + 1, :] = _colsum(acc8[k])
        vec_ref[3:4, :] = _colsum(acc8[3]) * (RG_C * _sigmoid(-lamv))

    col = lambda o: pl.BlockSpec((S, LANES), lambda c: (0, c + o))
    vec = pl.BlockSpec((1, LANES), lambda c: (0, c))
    diag = pl.BlockSpec((LANES, LANES), lambda c: (c, c))
    blk3 = lambda r: pl.BlockSpec((None, r, LANES), lambda c: (c, 0, 0))
    return _pcall(
        body, name=name, grid=(nb,),
        in_specs=[col(0), col(0), col(0), col(0), col(0), col(2 * nb), blk3(SL), diag, diag, vec],
        out_specs=[col(0), blk3(LANES), blk3(LANES), blk3(SL), pl.BlockSpec((SL, LANES), lambda c: (0, c))],
        out_shape=[jax.ShapeDtypeStruct((S, M), F32), jax.ShapeDtypeStruct((nb, LANES, LANES), F32),
                   jax.ShapeDtypeStruct((nb, LANES, LANES), F32), jax.ShapeDtypeStruct((nb, SL, LANES), F32),
                   jax.ShapeDtypeStruct((SL, M), F32)],
        scratch_shapes=[pltpu.VMEM((S + SL, LANES), F32), pltpu.VMEM((S + SL, LANES), F32), pltpu.VMEM((S, LANES), F32),
                        pltpu.VMEM((S, LANES), BF), pltpu.VMEM((S, LANES), BF), pltpu.VMEM((S + SL, LANES), F32),
                        pltpu.VMEM((S + SL, LANES), F32), pltpu.VMEM((SL, SL, LANES), F32), pltpu.VMEM((SL, SL, LANES), F32)],
        compiler_params=_params(("arbitrary",)),
    )(dhout, h, xr, ra, ii, proj, rnnw4, bda, bdi, lam)


GELU_K = 0.7978845608028654
GELU_C = 0.044715


def _layernorm_parts(cq):
    mu = _rowmean(cq)
    d = cq - mu
    rstd = lax.rsqrt(_rowmean(d * d) + EPS)
    return d * rstd, rstd


def mix_out(cq, proj, h, x, vecs, lnv, wout, name):
    S, D = x.shape
    M = cq.shape[1]
    ts = min(512, S)

    def body(cq_ref, uy_ref, h_ref, x_ref, v_ref, ln_ref, w_ref, xo_ref, ym_ref, yc_ref):
        z, _ = _layernorm_parts(cq_ref[...])
        l = z * _row(ln_ref, 0) + _row(ln_ref, 1)
        yc_ref[:, 0:M] = (l * _sigmoid(l)).astype(BF)
        uy = uy_ref[...]
        gelu = 0.5 * uy * (1.0 + jnp.tanh(GELU_K * (uy + GELU_C * uy * uy * uy)))
        yc_ref[:, M:2 * M] = (gelu * h_ref[...]).astype(BF)
        ym = jnp.dot(yc_ref[...], w_ref[...], preferred_element_type=F32)
        ym_ref[...] = ym
        xo_ref[...] = x_ref[...] + _row(v_ref, R_GT2) * ym

    tok = pl.BlockSpec((ts, D), lambda i: (i, 0))
    mtok = lambda o: pl.BlockSpec((ts, M), lambda i: (i, o))
    return _pcall(
        body, name=name, grid=(S // ts,),
        in_specs=[mtok(0), mtok(3), mtok(0), tok, pl.BlockSpec(vecs.shape, lambda i: (0, 0)),
                  pl.BlockSpec(lnv.shape, lambda i: (0, 0)), pl.BlockSpec(wout.shape, lambda i: (0, 0))],
        out_specs=[tok, tok, pl.BlockSpec((ts, 2 * M), lambda i: (i, 0))],
        out_shape=[jax.ShapeDtypeStruct((S, D), F32), jax.ShapeDtypeStruct((S, D), F32),
                   jax.ShapeDtypeStruct((S, 2 * M), BF)],
        compiler_params=_params(("arbitrary",)),
    )(cq, proj, h, x, vecs, lnv, wout)


def mix_out_bwd(dxo, ym, vecs, wout, cq, lnv, proj, h, name):
    S, D = dxo.shape
    M = cq.shape[1]
    ts = min(512, S)

    def body(dxo_ref, ym_ref, v_ref, w_ref, cq_ref, ln_ref, uy_ref, h_ref,
             dcq_ref, dh_ref, duy_ref, dyb_ref, vgd_ref, vgm_ref):
        @pl.when(pl.program_id(0) == 0)
        def _():
            vgd_ref[...] = jnp.zeros_like(vgd_ref)
            vgm_ref[...] = jnp.zeros_like(vgm_ref)

        dxo_v = dxo_ref[...]
        dyb = (_row(v_ref, R_GT2) * dxo_v).astype(BF)
        dyb_ref[...] = dyb
        vgd_ref[0:1, :] += _colsum(dxo_v * ym_ref[...])
        dycat = lax.dot_general(dyb, w_ref[...], CONTRACT_LAST, preferred_element_type=F32)
        dyc = dycat[:, 0:M]
        dyr = dycat[:, M:2 * M]
        z, rstd = _layernorm_parts(cq_ref[...])
        lng = _row(ln_ref, 0)
        l = z * lng + _row(ln_ref, 1)
        sl = _sigmoid(l)
        dl = dyc * (sl * (1.0 + l * (1.0 - sl)))
        vgm_ref[0:1, :] += _colsum(dl * z)
        vgm_ref[1:2, :] += _colsum(dl)
        dz = dl * lng
        dcq_ref[...] = rstd * (dz - _rowmean(dz) - z * _rowmean(dz * z))
        uy = uy_ref[...]
        u2 = uy * uy
        th = jnp.tanh(GELU_K * (uy + GELU_C * uy * u2))
        gelu = 0.5 * uy * (1.0 + th)
        dgelu = 0.5 * (1.0 + th) + 0.5 * uy * (1.0 - th * th) * (GELU_K * (1.0 + 3.0 * GELU_C * u2))
        dh_ref[...] = dyr * gelu
        duy_ref[...] = dyr * h_ref[...] * dgelu

    tok = pl.BlockSpec((ts, D), lambda i: (i, 0))
    mtok = lambda o: pl.BlockSpec((ts, M), lambda i: (i, o))
    return _pcall(
        body, name=name, grid=(S // ts,),
        in_specs=[tok, tok, pl.BlockSpec(vecs.shape, lambda i: (0, 0)), pl.BlockSpec(wout.shape, lambda i: (0, 0)),
                  mtok(0), pl.BlockSpec(lnv.shape, lambda i: (0, 0)), mtok(3), mtok(0)],
        out_specs=[mtok(0), mtok(0), mtok(0), tok, pl.BlockSpec((SUBLANES, D), lambda i: (0, 0)),
                   pl.BlockSpec((SUBLANES, M), lambda i: (0, 0))],
        out_shape=[jax.ShapeDtypeStruct((S, M), F32)] * 3 + [jax.ShapeDtypeStruct((S, D), BF),
                   jax.ShapeDtypeStruct((SUBLANES, D), F32), jax.ShapeDtypeStruct((SUBLANES, M), F32)],
        compiler_params=_params(("arbitrary",)),
    )(dxo, ym, vecs, wout, cq, lnv, proj, h)


def mix_in_bwd(dparts, x, dxo, vecs, win, name):
    S, D = x.shape
    M = dparts[0].shape[1]
    ts = min(512, S)

    def body(d0, d1, d2, d3, x_ref, dxo_ref, v_ref, w_ref, dx_ref, hb_ref, dp_ref, vg_ref):
        @pl.when(pl.program_id(0) == 0)
        def _():
            vg_ref[...] = jnp.zeros_like(vg_ref)

        for q, dref in enumerate((d0, d1, d2, d3)):
            dp_ref[:, q * M:(q + 1) * M] = dref[...].astype(BF)
        dh = lax.dot_general(dp_ref[...], w_ref[...], CONTRACT_LAST, preferred_element_type=F32)
        xv = x_ref[...]
        r = lax.rsqrt(_rowmean(xv * xv) + EPS)
        n = xv * r
        g = _row(v_ref, R_G2)
        sc1 = 1.0 + _row(v_ref, R_SC2)
        gsc = g * sc1
        hb_ref[...] = (n * gsc + _row(v_ref, R_SH2)).astype(BF)
        dhn = dh * n
        vg_ref[0:1, :] += _colsum(dh)
        vg_ref[1:2, :] += _colsum(dhn) * g
        vg_ref[2:3, :] += _colsum(dhn) * sc1
        dn = dh * gsc
        dx_ref[...] = dxo_ref[...] + r * (dn - n * _rowmean(dn * n))

    tok = pl.BlockSpec((ts, D), lambda i: (i, 0))
    mtok = pl.BlockSpec((ts, M), lambda i: (i, 0))
    return _pcall(
        body, name=name, grid=(S // ts,),
        in_specs=[mtok] * 4 + [tok, tok, pl.BlockSpec(vecs.shape, lambda i: (0, 0)), pl.BlockSpec(win.shape, lambda i: (0, 0))],
        out_specs=[tok, tok, pl.BlockSpec((ts, 4 * M), lambda i: (i, 0)), pl.BlockSpec((SUBLANES, D), lambda i: (0, 0))],
        out_shape=[jax.ShapeDtypeStruct((S, D), F32), jax.ShapeDtypeStruct((S, D), BF),
                   jax.ShapeDtypeStruct((S, 4 * M), BF), jax.ShapeDtypeStruct((SUBLANES, D), F32)],
        compiler_params=_params(("arbitrary",)),
    )(*dparts, x, dxo, vecs, win)


def _adamw(w, g, m, v):
    m = ADAM_B1 * m + (1.0 - ADAM_B1) * g
    v = ADAM_B2 * v + (1.0 - ADAM_B2) * (g * g)
    m_hat = m / (1.0 - ADAM_B1 ** ADAM_STEP)
    v_hat = v / (1.0 - ADAM_B2 ** ADAM_STEP)
    delta = -ADAM_LR * (m_hat / (jnp.sqrt(v_hat) + ADAM_EPS) + ADAM_WD * w)
    return delta, m, v


def adam_big(w, g, m, v, name):
    R, C = w.shape
    tr = 256 if R % 256 == 0 else R // 2 if (R // 2) % SUBLANES == 0 and R > 512 else R
    tc = C if C <= 1536 else (1152 if C % 1152 == 0 else 1024)
    assert R % tr == 0 and C % tc == 0

    def body(w_ref, g_ref, m_ref, v_ref, d_ref, nm_ref, nv_ref):
        d, nm, nv = _adamw(w_ref[...], g_ref[...], m_ref[...], v_ref[...])
        d_ref[...] = d
        nm_ref[...] = nm
        nv_ref[...] = nv

    blk = pl.BlockSpec((tr, tc), lambda i, j: (i, j))
    return _pcall(
        body, name=name, grid=(R // tr, C // tc), in_specs=[blk] * 4, out_specs=[blk] * 3,
        out_shape=[jax.ShapeDtypeStruct((R, C), F32)] * 3, compiler_params=_params(("parallel", "parallel")),
    )(w, g, m, v)


def adam_cond(c_all, dmod, w, m, v, name):
    B, Kin = c_all.shape
    N = w.shape[1]
    tn = 256
    assert N % tn == 0

    def body(c_ref, d_ref, w_ref, m_ref, v_ref, g_ref, dl_ref, nm_ref, nv_ref):
        cv = c_ref[...]
        ca = cv * _sigmoid(cv)
        g = lax.dot_general(ca, d_ref[...], CONTRACT_FIRST, preferred_element_type=F32, precision=lax.Precision.HIGHEST)
        d, nm, nv = _adamw(w_ref[...], g, m_ref[...], v_ref[...])
        g_ref[...] = g
        dl_ref[...] = d
        nm_ref[...] = nm
        nv_ref[...] = nv

    blk = pl.BlockSpec((Kin, tn), lambda n: (0, n))
    return _pcall(
        body, name=name, grid=(N // tn,),
        in_specs=[pl.BlockSpec((B, Kin), lambda n: (0, 0)), pl.BlockSpec((B, tn), lambda n: (0, n)), blk, blk, blk],
        out_specs=[blk] * 4, out_shape=[jax.ShapeDtypeStruct((Kin, N), F32)] * 4,
        compiler_params=_params(("parallel",)),
    )(c_all, dmod, w, m, v)


def adam_small(ws, gs, ms, vs, name):
    n = len(ws)

    def body(*refs):
        ins, outs = refs[:4 * n], refs[4 * n:]
        for k in range(n):
            d, nm, nv = _adamw(ins[k][...], ins[n + k][...], ins[2 * n + k][...], ins[3 * n + k][...])
            outs[k][...] = d
            outs[n + k][...] = nm
            outs[2 * n + k][...] = nv

    vm = pl.BlockSpec(memory_space=pltpu.VMEM)
    shapes = [jax.ShapeDtypeStruct(w.shape, F32) for w in ws]
    out = _pcall(body, name=name, in_specs=[vm] * (4 * n), out_specs=[vm] * (3 * n), out_shape=shapes * 3,
                 compiler_params=_params())(*ws, *gs, *ms, *vs)
    return out[:n], out[n:2 * n], out[2 * n:]


def _me():
    return lax.axis_index("x"), lax.axis_index("y"), lax.axis_index("c")


def _flip(x, y, p):
    return (x ^ (p >> 1) if (p >> 1) else x), (y ^ (p & 1) if (p & 1) else y)


def allgather_devices(v, name, with_sum=False):
    R, L = v.shape

    def body(v_ref, out_ref, *rest):
        if with_sum:
            sum_ref, send_sems, recv_sems = rest
        else:
            send_sems, recv_sems = rest
        x, y, c = _me()
        me = 4 * x + 2 * y + c
        out_ref[me] = v_ref[...]
        copies = []
        for p in range(1, N_DEV):
            px, py = _flip(x, y, p >> 1)
            pc = (1 - c) if (p & 1) else c
            peer = 4 * px + 2 * py + pc
            send = pltpu.make_async_remote_copy(src_ref=v_ref, dst_ref=out_ref.at[me], send_sem=send_sems.at[p - 1],
                                                recv_sem=recv_sems.at[p - 1], device_id=(px, py, pc), device_id_type=MESH)
            send.start()
            recv = pltpu.make_async_remote_copy(src_ref=v_ref, dst_ref=out_ref.at[peer], send_sem=send_sems.at[p - 1],
                                                recv_sem=recv_sems.at[p - 1], device_id=(px, py, pc), device_id_type=MESH)
            copies.append((send, recv))
        for send, recv in copies:
            recv.wait_recv()
        for send, recv in copies:
            send.wait_send()
        if with_sum:
            s = out_ref[0]
            for k in range(1, N_DEV):
                s = s + out_ref[k]
            sum_ref[...] = s

    vm = pl.BlockSpec(memory_space=pltpu.VMEM)
    out_shape = [jax.ShapeDtypeStruct((N_DEV, R, L), F32)]
    if with_sum:
        out_shape.append(jax.ShapeDtypeStruct((R, L), F32))
    return _pcall(
        body, name=name, in_specs=[vm], out_specs=[vm] * len(out_shape), out_shape=out_shape,
        scratch_shapes=[pltpu.SemaphoreType.DMA((N_DEV - 1,)), pltpu.SemaphoreType.DMA((N_DEV - 1,))],
        compiler_params=_params(),
    )(v)


def allgather_chips(v, name):
    R, L = v.shape

    def body(v_ref, out_ref, send_sems, recv_sems):
        x, y, c = _me()
        chip = 2 * x + y
        out_ref[chip] = v_ref[...]
        copies = []
        for p in range(1, N_CHIPS):
            px, py = _flip(x, y, p)
            send = pltpu.make_async_remote_copy(src_ref=v_ref, dst_ref=out_ref.at[chip], send_sem=send_sems.at[p - 1],
                                                recv_sem=recv_sems.at[p - 1], device_id=(px, py, c), device_id_type=MESH)
            send.start()
            recv = pltpu.make_async_remote_copy(src_ref=v_ref, dst_ref=out_ref.at[2 * px + py], send_sem=send_sems.at[p - 1],
                                                recv_sem=recv_sems.at[p - 1], device_id=(px, py, c), device_id_type=MESH)
            copies.append((send, recv))
        for send, recv in copies:
            recv.wait_recv()
        for send, recv in copies:
            send.wait_send()

    vm = pl.BlockSpec(memory_space=pltpu.VMEM)
    return _pcall(
        body, name=name, in_specs=[vm], out_specs=vm, out_shape=jax.ShapeDtypeStruct((N_CHIPS, R, L), F32),
        scratch_shapes=[pltpu.SemaphoreType.DMA((N_CHIPS - 1,)), pltpu.SemaphoreType.DMA((N_CHIPS - 1,))],
        compiler_params=_params(),
    )(v)


def _shard_window(ref, kind, shard_shape, chip, half):
    r, c = shard_shape
    hr = r // 2
    if kind == "col":
        return ref.at[pl.ds(pl.multiple_of(half * hr, hr), hr), pl.ds(pl.multiple_of(chip * c, c), c)]
    return ref.at[pl.ds(pl.multiple_of(chip * r + half * hr, hr), hr), :]


def allgather_weights(shards, kinds, name):
    n = len(shards)
    fulls = []
    for s, kind in zip(shards, kinds):
        r, c = s.shape
        fulls.append(jax.ShapeDtypeStruct((r, N_CHIPS * c) if kind == "col" else (N_CHIPS * r, c), s.dtype))

    def body(*refs):
        srcs, outs = refs[:n], refs[n:2 * n]
        send_sems, recv_sems, fsend_sems, frecv_sems, local_sems = refs[2 * n:]
        x, y, c = _me()
        chip = 2 * x + y
        sib = (x, y, 1 - c)
        locals_, sends, fwds = [], [], []
        for i in range(n):
            shp = srcs[i].shape
            hr = shp[0] // 2
            win_all = (outs[i].at[:, pl.ds(pl.multiple_of(chip * shp[1], shp[1]), shp[1])] if kinds[i] == "col"
                       else outs[i].at[pl.ds(pl.multiple_of(chip * shp[0], shp[0]), shp[0]), :])
            lc = pltpu.make_async_copy(srcs[i], win_all, local_sems.at[i])
            lc.start()
            locals_.append(lc)
            my_half = srcs[i].at[pl.ds(pl.multiple_of(c * hr, hr), hr), :]
            for p in range(1, N_CHIPS):
                px, py = _flip(x, y, p)
                k = i * (N_CHIPS - 1) + p - 1
                cp = pltpu.make_async_remote_copy(src_ref=my_half, dst_ref=_shard_window(outs[i], kinds[i], shp, chip, c),
                                                  send_sem=send_sems.at[k], recv_sem=recv_sems.at[k],
                                                  device_id=(px, py, c), device_id_type=MESH)
                cp.start()
                sends.append(cp)
        for i in range(n):
            shp = srcs[i].shape
            for p in range(1, N_CHIPS):
                px, py = _flip(x, y, p)
                k = i * (N_CHIPS - 1) + p - 1
                landed = _shard_window(outs[i], kinds[i], shp, 2 * px + py, c)
                pltpu.make_async_remote_copy(src_ref=landed, dst_ref=landed, send_sem=send_sems.at[k], recv_sem=recv_sems.at[k],
                                             device_id=(px, py, c), device_id_type=MESH).wait_recv()
                fw = pltpu.make_async_remote_copy(src_ref=landed, dst_ref=landed, send_sem=fsend_sems.at[k],
                                                  recv_sem=frecv_sems.at[k], device_id=sib, device_id_type=MESH)
                fw.start()
                fwds.append(fw)
        for i in range(n):
            shp = srcs[i].shape
            for p in range(1, N_CHIPS):
                px, py = _flip(x, y, p)
                k = i * (N_CHIPS - 1) + p - 1
                other = _shard_window(outs[i], kinds[i], shp, 2 * px + py, 1 - c)
                pltpu.make_async_remote_copy(src_ref=other, dst_ref=other, send_sem=fsend_sems.at[k], recv_sem=frecv_sems.at[k],
                                             device_id=sib, device_id_type=MESH).wait_recv()
        for cp in sends + fwds:
            cp.wait_send()
        for lc in locals_:
            lc.wait()

    nk = n * (N_CHIPS - 1)
    anyspec = pl.BlockSpec(memory_space=pl.ANY)
    return _pcall(
        body, name=name, in_specs=[anyspec] * n, out_specs=[anyspec] * n, out_shape=fulls,
        scratch_shapes=[pltpu.SemaphoreType.DMA((nk,)), pltpu.SemaphoreType.DMA((nk,)), pltpu.SemaphoreType.DMA((nk,)),
                        pltpu.SemaphoreType.DMA((nk,)), pltpu.SemaphoreType.DMA((n,))],
        compiler_params=_params(),
    )(*shards)


def _as_halves(g, kind, shard_shape):
    r, c = shard_shape
    if kind == "col":
        return g.reshape(2, r // 2, N_CHIPS * c)
    return g.reshape(N_CHIPS, 2, r // 2, c)


def exchange_sibling_halves(grads, kinds, shard_shapes, name):
    n = len(grads)
    views = [_as_halves(g, k, s) for g, k, s in zip(grads, kinds, shard_shapes)]
    outs = []
    for k, (r, c) in zip(kinds, shard_shapes):
        outs.append(jax.ShapeDtypeStruct((r // 2, N_CHIPS * c) if k == "col" else (N_CHIPS, r // 2, c), F32))

    def body(*refs):
        srcs, dsts = refs[:n], refs[n:2 * n]
        send_sems, recv_sems = refs[2 * n:]
        x, y, c = _me()
        cps = []
        for i in range(n):
            src = srcs[i].at[1 - c] if kinds[i] == "col" else srcs[i].at[:, 1 - c]
            cp = pltpu.make_async_remote_copy(src_ref=src, dst_ref=dsts[i], send_sem=send_sems.at[i], recv_sem=recv_sems.at[i],
                                              device_id=(x, y, 1 - c), device_id_type=MESH)
            cp.start()
            cps.append(cp)
        for cp in cps:
            cp.wait_recv()
        for cp in cps:
            cp.wait_send()

    anyspec = pl.BlockSpec(memory_space=pl.ANY)
    return _pcall(
        body, name=name, in_specs=[anyspec] * n, out_specs=[anyspec] * n, out_shape=outs,
        scratch_shapes=[pltpu.SemaphoreType.DMA((n,)), pltpu.SemaphoreType.DMA((n,))], compiler_params=_params(),
    )(*views)


def add_sibling_half(g, recv, kind, shard_shape, cidx, name):
    r, c = shard_shape
    hr = r // 2
    gv = _as_halves(g, kind, shard_shape)
    tr = hr if hr <= 512 else (256 if hr % 256 == 0 else hr // 2)
    assert hr % tr == 0

    def body(ci_ref, g_ref, r_ref, h_ref, hb_ref):
        s = g_ref[...] + r_ref[...]
        h_ref[...] = s
        hb_ref[...] = s.astype(BF)

    if kind == "col":
        grid = (hr // tr, N_CHIPS)
        g_spec = pl.BlockSpec((None, tr, c), lambda i, k, ci: (ci[0], i, k))
        o_spec = pl.BlockSpec((tr, c), lambda i, k, ci: (i, k))
    else:
        grid = (hr // tr, N_CHIPS)
        g_spec = pl.BlockSpec((None, None, tr, c), lambda i, k, ci: (k, ci[0], i, 0))
        o_spec = pl.BlockSpec((None, tr, c), lambda i, k, ci: (k, i, 0))
    gs = pltpu.PrefetchScalarGridSpec(num_scalar_prefetch=1, grid=grid, in_specs=[g_spec, o_spec], out_specs=[o_spec, o_spec])
    return _pcall(
        body, name=name, grid_spec=gs,
        out_shape=[jax.ShapeDtypeStruct(recv.shape, F32), jax.ShapeDtypeStruct(recv.shape, BF)],
        compiler_params=_params(("parallel", "parallel")),
    )(cidx, gv, recv)


def exchange_chip_pieces(hbs, kinds, shard_shapes, name):
    n = len(hbs)
    outs = [jax.ShapeDtypeStruct((N_CHIPS - 1, r // 2, c), BF) for (r, c) in shard_shapes]

    def body(*refs):
        srcs, dsts = refs[:n], refs[n:2 * n]
        send_sems, recv_sems = refs[2 * n:]
        x, y, c = _me()
        cps = []
        for i in range(n):
            cc = shard_shapes[i][1]
            for p in range(1, N_CHIPS):
                px, py = _flip(x, y, p)
                pchip = 2 * px + py
                src = (srcs[i].at[:, pl.ds(pl.multiple_of(pchip * cc, cc), cc)] if kinds[i] == "col" else srcs[i].at[pchip])
                k = i * (N_CHIPS - 1) + p - 1
                cp = pltpu.make_async_remote_copy(src_ref=src, dst_ref=dsts[i].at[p - 1], send_sem=send_sems.at[k],
                                                  recv_sem=recv_sems.at[k], device_id=(px, py, c), device_id_type=MESH)
                cp.start()
                cps.append(cp)
        for cp in cps:
            cp.wait_recv()
        for cp in cps:
            cp.wait_send()

    nk = n * (N_CHIPS - 1)
    anyspec = pl.BlockSpec(memory_space=pl.ANY)
    return _pcall(
        body, name=name, in_specs=[anyspec] * n, out_specs=[anyspec] * n, out_shape=outs,
        scratch_shapes=[pltpu.SemaphoreType.DMA((nk,)), pltpu.SemaphoreType.DMA((nk,))], compiler_params=_params(),
    )(*hbs)


def sum_chip_pieces(h, pieces, kind, shard_shape, chip_core, name):
    r, c = shard_shape
    hr = r // 2
    tr = hr if hr <= 512 else (256 if hr % 256 == 0 else hr // 2)
    assert hr % tr == 0
    nrb = hr // tr

    def body(ci_ref, h_ref, p_ref, q_ref):
        q_ref[...] = ((h_ref[...] + p_ref[0].astype(F32)) + p_ref[1].astype(F32)) + p_ref[2].astype(F32)

    if kind == "col":
        h_spec = pl.BlockSpec((tr, c), lambda i, ci: (i, ci[0]))
    else:
        h_spec = pl.BlockSpec((None, tr, c), lambda i, ci: (ci[0], i, 0))
    gs = pltpu.PrefetchScalarGridSpec(
        num_scalar_prefetch=1, grid=(nrb,),
        in_specs=[h_spec, pl.BlockSpec((N_CHIPS - 1, tr, c), lambda i, ci: (0, i, 0))],
        out_specs=pl.BlockSpec((tr, c), lambda i, ci: (ci[1] * nrb + i, 0)))
    return _pcall(body, name=name, grid_spec=gs, out_shape=jax.ShapeDtypeStruct((r, c), F32),
                  compiler_params=_params(("parallel",)))(chip_core, h, pieces)


def exchange_reduced_halves(qs, name):
    n = len(qs)

    def body(*refs):
        bufs = refs[n:2 * n]
        send_sems, recv_sems = refs[2 * n:]
        x, y, c = _me()
        cps = []
        for i in range(n):
            hr = bufs[i].shape[0] // 2
            mine = bufs[i].at[pl.ds(pl.multiple_of(c * hr, hr), hr), :]
            other = bufs[i].at[pl.ds(pl.multiple_of((1 - c) * hr, hr), hr), :]
            cp = pltpu.make_async_remote_copy(src_ref=mine, dst_ref=mine, send_sem=send_sems.at[i], recv_sem=recv_sems.at[i],
                                              device_id=(x, y, 1 - c), device_id_type=MESH)
            cp.start()
            cps.append((cp, pltpu.make_async_remote_copy(src_ref=other, dst_ref=other, send_sem=send_sems.at[i],
                                                         recv_sem=recv_sems.at[i], device_id=(x, y, 1 - c), device_id_type=MESH)))
        for cp, rv in cps:
            rv.wait_recv()
        for cp, rv in cps:
            cp.wait_send()

    anyspec = pl.BlockSpec(memory_space=pl.ANY)
    return _pcall(
        body, name=name, in_specs=[anyspec] * n, out_specs=[anyspec] * n,
        out_shape=[jax.ShapeDtypeStruct(q.shape, F32) for q in qs], input_output_aliases={i: i for i in range(n)},
        scratch_shapes=[pltpu.SemaphoreType.DMA((n,)), pltpu.SemaphoreType.DMA((n,))],
        compiler_params=_params(),
    )(*qs)


def _rows128(a):
    return a.reshape(-1, LANES)


def _block_diag(w):
    H, d, _ = w.shape
    eye = jnp.eye(H, dtype=w.dtype)
    return jnp.einsum("hde,hg->hdge", w, eye).reshape(H * d, H * d)


def _diag_blocks(g4, H, d):
    nb = g4.shape[0]
    per = LANES // d
    g = g4.reshape(nb, per, d, per, d)
    return jnp.stack([g[:, j, :, j, :] for j in range(per)], axis=1).reshape(H, d, d)


def kernel(x, c, w_mod, b_mod, g_ffn1, w_ffn1_in, w_ffn1_out, g_mix, w_in, conv_w, conv_b, ln_g, ln_b, rnn_conv_w, rnn_conv_b, w_a, b_a, w_i, b_i, lru_lambda, w_out, g_ffn2, w_ffn2_in, w_ffn2_out, w_fmod, b_fmod, g_final, loss_target, m_w_mod, m_b_mod, m_g_ffn1, m_w_ffn1_in, m_w_ffn1_out, m_g_mix, m_w_in, m_conv_w, m_conv_b, m_ln_g, m_ln_b, m_rnn_conv_w, m_rnn_conv_b, m_w_a, m_b_a, m_w_i, m_b_i, m_lru_lambda, m_w_out, m_g_ffn2, m_w_ffn2_in, m_w_ffn2_out, m_w_fmod, m_b_fmod, m_g_final, v_w_mod, v_b_mod, v_g_ffn1, v_w_ffn1_in, v_w_ffn1_out, v_g_mix, v_w_in, v_conv_w, v_conv_b, v_ln_g, v_ln_b, v_rnn_conv_w, v_rnn_conv_b, v_w_a, v_b_a, v_w_i, v_b_i, v_lru_lambda, v_w_out, v_g_ffn2, v_w_ffn2_in, v_w_ffn2_out, v_w_fmod, v_b_fmod, v_g_final):
    S, D = x.shape[1], x.shape[2]
    M = conv_b.shape[1]
    H, HD = w_a.shape[1], w_a.shape[2]
    nb = M // LANES
    ix, iy, ic = lax.axis_index("x"), lax.axis_index("y"), lax.axis_index("c")
    chip = 2 * ix + iy
    dev = 2 * chip + ic
    cidx = jnp.reshape(ic, (1,)).astype(I32)
    chip_core = jnp.stack([chip, ic]).astype(I32)
    xs = x[0]
    tgt = loss_target[0]

    c_all = allgather_devices(_rows128(c), "gather_c")[0].reshape(N_DEV, D)
    mod_cols = cond_matmul(c_all, w_mod[0], "mod_proj")
    fmod_cols = cond_matmul(c_all, w_fmod, "fmod_proj")
    convw_pad = jnp.pad(conv_w[0], ((0, 32 - CONV_WIDTH), (0, 0)))
    rnnw_pad = jnp.pad(rnn_conv_w[0], ((0, SUBLANES - RNN_CONV_WIDTH), (0, 0)))
    n_mod, n_fmod = mod_cols.shape[1], fmod_cols.shape[1]
    small = jnp.concatenate([_rows128(mod_cols), _rows128(fmod_cols), convw_pad, rnnw_pad], axis=0)
    small4 = allgather_chips(small, "gather_cond")
    r0 = N_DEV * n_mod // LANES
    r1 = r0 + N_DEV * n_fmod // LANES
    mod_all = small4[:, :r0].reshape(N_CHIPS, N_DEV, n_mod)
    fmod_all = small4[:, r0:r1].reshape(N_CHIPS, N_DEV, n_fmod)
    convw4 = small4[:, r1:r1 + 32]
    rnnw4 = small4[:, r1 + 32:r1 + 32 + SUBLANES]
    mod_row = lax.dynamic_index_in_dim(mod_all, dev, axis=1, keepdims=False).reshape(1, N_CHIPS * n_mod) + b_mod
    fmod_row = lax.dynamic_index_in_dim(fmod_all, dev, axis=1, keepdims=False).reshape(1, N_CHIPS * n_fmod) + b_fmod[None, :]
    vecs = jnp.concatenate([mod_row.reshape(9, D), fmod_row.reshape(2, D), g_ffn1, g_mix, g_ffn2, g_final[None, :],
                            jnp.zeros((1, D), F32)], axis=0)
    lnv = jnp.concatenate([ln_g, ln_b, jnp.zeros((SUBLANES - 2, M), F32)], axis=0)
    bda = _block_diag(w_a[0]).astype(BF)
    bdi = _block_diag(w_i[0]).astype(BF)

    kinds = ["col", "row", "col", "row", "col", "row"]
    big_w = [w_ffn1_in[0], w_ffn1_out[0], w_in[0], w_out[0], w_ffn2_in[0], w_ffn2_out[0]]
    shard_shapes = [w.shape for w in big_w]
    wi1, wo1, win, wout, wi2, wo2 = allgather_weights([w.astype(BF) for w in big_w], kinds, "gather_weights")

    rows1 = (R_SH1, R_SC1, R_GT1, R_G1)
    rows3 = (R_SH3, R_SC3, R_GT3, R_G3)
    x1, g1s, u1s, y1 = ffn_fwd(xs, vecs, wi1, wo1, rows1, "ffn1_fwd")
    proj = norm_matmul(x1, vecs, win, (R_SH2, R_SC2, R_G2), "mix_in_proj")
    cq = conv_fwd(proj, convw4, conv_b, "conv_fwd")
    xr, ra, ii, hh = rnn_fwd(proj, rnnw4, rnn_conv_b, bda, bdi, b_a, b_i, lru_lambda, "rnn_fwd")
    x2, ym, ycat = mix_out(cq, proj, hh, x1, vecs, lnv, wout, "mix_out")
    x3, g2s, u2s, y2 = ffn_fwd(x2, vecs, wi2, wo2, rows3, "ffn2_fwd")
    dx3, vgf = final_fwd_bwd(x3, tgt, vecs, "final_loss")

    Fd = wo1.shape[0]
    tk = min(512, S)
    dx2, act2, dg2, du2, h3b, dy2b, vg3 = ffn_bwd(dx3, x2, vecs, g2s, u2s, y2, wi2, wo2, rows3, "ffn2_bwd")
    gwo2 = matmul(act2, dy2b, "tn", tm=Fd // 2, tn=D, tk=tk, name="ffn2_dwo")
    gwi2 = matmul(h3b, dg2, "tn", tm=D, tn=Fd // 2, tk=tk, name="ffn2_dwg", out_cols=2 * Fd)
    gwi2 = matmul(h3b, du2, "tn", tm=D, tn=Fd // 2, tk=tk, name="ffn2_dwu", out_cols=2 * Fd, col_off=Fd, prev=gwi2)
    dcq, dhout, duy, dymb, vgd, vgm = mix_out_bwd(dx2, ym, vecs, wout, cq, lnv, proj, hh, "mix_out_bwd")
    gwout = matmul(ycat, dymb, "tn", tm=2 * M, tn=D, tk=tk, name="mix_dwout")
    duv, dug, dconvw4, dconvb = conv_bwd(dcq, proj, convw4, "conv_bwd")
    dux, dwa4, dwi4, drnnw4, rvec = rnn_bwd(dhout, hh, xr, ra, ii, proj, rnnw4, bda, bdi, lru_lambda, "rnn_bwd")
    dx1, h2b, dpb, vg2 = mix_in_bwd((duv, dug, dux, duy), x1, dx2, vecs, win, "mix_in_bwd")
    gwin = matmul(h2b, dpb, "tn", tm=D, tn=1024, tk=tk, name="mix_dwin")
    dx0, act1, dg1, du1, h1b, dy1b, vg1 = ffn_bwd(dx1, xs, vecs, g1s, u1s, y1, wi1, wo1, rows1, "ffn1_bwd")
    gwo1 = matmul(act1, dy1b, "tn", tm=Fd // 2, tn=D, tk=tk, name="ffn1_dwo")
    gwi1 = matmul(h1b, dg1, "tn", tm=D, tn=Fd // 2, tk=tk, name="ffn1_dwg", out_cols=2 * Fd)
    gwi1 = matmul(h1b, du1, "tn", tm=D, tn=Fd // 2, tk=tk, name="ffn1_dwu", out_cols=2 * Fd, col_off=Fd, prev=gwi1)

    dmod_row = jnp.concatenate([vg1[1:3], vg1[0:1], vg2[0:2], vgd[0:1], vg3[1:3], vg3[0:1]], axis=0)
    gains = jnp.concatenate([vg1[3:4], vg2[2:3], vg3[3:4], vgf[2:4]], axis=0)
    mvecs = jnp.concatenate([dconvb, vgm[0:2], rvec[0:4], jnp.zeros((1, M), F32)], axis=0)
    parts = [_rows128(dmod_row), _rows128(vgf[0:2]), _rows128(gains), _rows128(mvecs),
             _rows128(dconvw4), _rows128(drnnw4), _rows128(_diag_blocks(dwa4, H, HD)), _rows128(_diag_blocks(dwi4, H, HD))]
    sizes = [p.shape[0] for p in parts]
    packed = jnp.concatenate(parts, axis=0)
    gathered, summed = allgather_devices(packed, "reduce_small", with_sum=True)
    offs = [0]
    for s in sizes:
        offs.append(offs[-1] + s)
    seg = lambda k: summed[offs[k]:offs[k + 1]]
    g_b_mod = seg(0).reshape(1, 9 * D)
    g_b_fmod = seg(1).reshape(1, 2 * D)
    gsum = seg(2).reshape(5, D)
    loss = (0.5 / D) * jnp.sum(gsum[4])
    msum = seg(3).reshape(SUBLANES, M)
    g_conv_w = lax.dynamic_index_in_dim(seg(4).reshape(nb, 32, LANES), chip, axis=0, keepdims=False)[:CONV_WIDTH]
    g_rnn_w = lax.dynamic_index_in_dim(seg(5).reshape(nb, SUBLANES, LANES), chip, axis=0, keepdims=False)[:RNN_CONV_WIDTH]
    g_w_a = seg(6).reshape(H, HD, HD)
    g_w_i = seg(7).reshape(H, HD, HD)
    dmod_all = gathered[:, offs[0]:offs[1]].reshape(N_DEV, 9 * D)
    dfmod_all = gathered[:, offs[1]:offs[2]].reshape(N_DEV, 2 * D)
    dmod_cols = lax.dynamic_slice_in_dim(dmod_all, chip * n_mod, n_mod, axis=1)
    dfmod_cols = lax.dynamic_slice_in_dim(dfmod_all, chip * n_fmod, n_fmod, axis=1)

    big_g = [gwi1, gwo1, gwin, gwout, gwi2, gwo2]
    names = ["ffn1_in", "ffn1_out", "w_in", "w_out", "ffn2_in", "ffn2_out"]
    recv1 = exchange_sibling_halves(big_g, kinds, shard_shapes, "reduce_stage1")
    hs, hbs = [], []
    for g, r1_, k, s, nm in zip(big_g, recv1, kinds, shard_shapes, names):
        h_, hb_ = add_sibling_half(g, r1_, k, s, cidx, "add_sibling_" + nm)
        hs.append(h_)
        hbs.append(hb_)
    recv2 = exchange_chip_pieces(hbs, kinds, shard_shapes, "reduce_stage2")
    qs = [sum_chip_pieces(h_, p_, k, s, chip_core, "sum_chips_" + nm)
          for h_, p_, k, s, nm in zip(hs, recv2, kinds, shard_shapes, names)]
    g_big = exchange_reduced_halves(qs, "reduce_stage3")

    big_m = [m_w_ffn1_in[0], m_w_ffn1_out[0], m_w_in[0], m_w_out[0], m_w_ffn2_in[0], m_w_ffn2_out[0]]
    big_v = [v_w_ffn1_in[0], v_w_ffn1_out[0], v_w_in[0], v_w_out[0], v_w_ffn2_in[0], v_w_ffn2_out[0]]
    big_out = [adam_big(w, g, m, v, "adam_" + nm) for w, g, m, v, nm in zip(big_w, g_big, big_m, big_v, names)]
    g_w_mod, d_w_mod, nm_w_mod, nv_w_mod = adam_cond(c_all, dmod_cols, w_mod[0], m_w_mod[0], v_w_mod[0], "adam_w_mod")
    g_w_fmod, d_w_fmod, nm_w_fmod, nv_w_fmod = adam_cond(c_all, dfmod_cols, w_fmod, m_w_fmod, v_w_fmod, "adam_w_fmod")

    flat2 = lambda a: a.reshape(-1, a.shape[-1])
    small_names = ["b_mod", "g_ffn1", "g_mix", "conv_w", "conv_b", "ln_g", "ln_b", "rnn_conv_w", "rnn_conv_b", "w_a", "b_a",
                   "w_i", "b_i", "lru_lambda", "g_ffn2", "b_fmod", "g_final"]
    small_w = [b_mod, g_ffn1, g_mix, conv_w, conv_b, ln_g, ln_b, rnn_conv_w, rnn_conv_b, w_a, b_a, w_i, b_i, lru_lambda,
               g_ffn2, b_fmod, g_final]
    small_m = [m_b_mod, m_g_ffn1, m_g_mix, m_conv_w, m_conv_b, m_ln_g, m_ln_b, m_rnn_conv_w, m_rnn_conv_b, m_w_a, m_b_a,
               m_w_i, m_b_i, m_lru_lambda, m_g_ffn2, m_b_fmod, m_g_final]
    small_v = [v_b_mod, v_g_ffn1, v_g_mix, v_conv_w, v_conv_b, v_ln_g, v_ln_b, v_rnn_conv_w, v_rnn_conv_b, v_w_a, v_b_a,
               v_w_i, v_b_i, v_lru_lambda, v_g_ffn2, v_b_fmod, v_g_final]
    small_g = [g_b_mod, gsum[0:1], gsum[1:2], g_conv_w, msum[0:1], msum[1:2], msum[2:3], g_rnn_w, msum[3:4], g_w_a, msum[4:5],
               g_w_i, msum[5:6], msum[6:7], gsum[2:3], g_b_fmod, gsum[3:4]]
    small_g = [g.reshape(w.shape) for g, w in zip(small_g, small_w)]
    two_d = lambda a: a.reshape(1, -1) if a.ndim == 1 else flat2(a)
    sd, sm, sv = adam_small([two_d(a) for a in small_w], [two_d(a) for a in small_g], [two_d(a) for a in small_m],
                            [two_d(a) for a in small_v], "adam_small")
    small = {}
    for k, nm in enumerate(small_names):
        shp = small_w[k].shape
        small[nm] = (small_g[k], sd[k].reshape(shp), sm[k].reshape(shp), sv[k].reshape(shp))

    big = {"w_mod": tuple(a[None] for a in (g_w_mod, d_w_mod, nm_w_mod, nv_w_mod)),
           "w_fmod": (g_w_fmod, d_w_fmod, nm_w_fmod, nv_w_fmod)}
    for nm, full, g, (d, nmm, nvv) in zip(["w_ffn1_in", "w_ffn1_out", "w_in", "w_out", "w_ffn2_in", "w_ffn2_out"],
                                         big_w, g_big, big_out):
        big[nm] = tuple(a[None] for a in (g, d, nmm, nvv))
    order = ["w_mod", "b_mod", "g_ffn1", "w_ffn1_in", "w_ffn1_out", "g_mix", "w_in", "conv_w", "conv_b", "ln_g", "ln_b",
             "rnn_conv_w", "rnn_conv_b", "w_a", "b_a", "w_i", "b_i", "lru_lambda", "w_out", "g_ffn2", "w_ffn2_in",
             "w_ffn2_out", "w_fmod", "b_fmod", "g_final"]
    table = {**small, **big}
    outs = [loss, dx0[None]]
    for kind_ in range(4):
        outs.extend(table[nm][kind_] for nm in order)
    return tuple(outs)
```

```python
import functools

import jax
import jax.numpy as jnp
from jax import lax
from jax.experimental import pallas as pl
from jax.experimental.pallas import tpu as pltpu
from jax.experimental.pallas import tpu_sc as plsc

F32 = jnp.float32
BF = jnp.bfloat16
I32 = jnp.int32
MESH = pl.DeviceIdType.MESH

EPS = 1e-6
RG_C = 8.0
MACARON_W = 0.5
CONV_WIDTH = 31
RNN_CONV_WIDTH = 4
ADAM_LR = 0.001
ADAM_B1 = 0.9
ADAM_B2 = 0.999
ADAM_EPS = 1e-08
ADAM_WD = 0.01
ADAM_STEP = 10

LANES = 128
SUBLANES = 8
VMEM_LIMIT = 48 * 1024 * 1024
N_CHIPS = 4
N_DEV = 8

R_SH1, R_SC1, R_GT1, R_SH2, R_SC2, R_GT2, R_SH3, R_SC3, R_GT3, R_FSH, R_FSC, R_G1, R_G2, R_G3, R_GF = range(15)

CONTRACT_LAST = (((1,), (1,)), ((), ()))
CONTRACT_FIRST = (((0,), (0,)), ((), ()))


def _pcall(body, **kw):
    return pl.pallas_call(body, **kw)


def _params(sem=None):
    if sem is None:
        return pltpu.CompilerParams(vmem_limit_bytes=VMEM_LIMIT)
    return pltpu.CompilerParams(dimension_semantics=sem, vmem_limit_bytes=VMEM_LIMIT)


def _row(ref, r):
    return ref[r:r + 1, :]


def _sigmoid(x):
    return 1.0 / (1.0 + jnp.exp(-x))


def _colsum(x):
    return jnp.sum(x, axis=0, keepdims=True)


def _rowmean(x):
    return jnp.mean(x, axis=-1, keepdims=True)


def matmul(a, b, mode, *, tm, tn, tk, name, out_dtype=F32, out_cols=None, col_off=0, prev=None):
    if mode == "nn":
        (M, K), (K2, N) = a.shape, b.shape
    elif mode == "nt":
        (M, K), (N, K2) = a.shape, b.shape
    else:
        (K, M), (K2, N) = a.shape, b.shape
    assert K == K2 and M % tm == 0 and N % tn == 0 and K % tk == 0 and col_off % tn == 0
    nk = K // tk
    out_cols = N if out_cols is None else out_cols
    off = col_off // tn

    def body(*refs):
        if prev is None:
            a_ref, b_ref, o_ref, acc = refs
        else:
            a_ref, b_ref, _, o_ref, acc = refs
        k = pl.program_id(2)

        @pl.when(k == 0)
        def _():
            acc[...] = jnp.zeros_like(acc)

        av = a_ref[...].astype(BF)
        bv = b_ref[...].astype(BF)
        if mode == "nn":
            acc[...] += jnp.dot(av, bv, preferred_element_type=F32)
        elif mode == "nt":
            acc[...] += lax.dot_general(av, bv, CONTRACT_LAST, preferred_element_type=F32)
        else:
            acc[...] += lax.dot_general(av, bv, CONTRACT_FIRST, preferred_element_type=F32)

        @pl.when(k == nk - 1)
        def _():
            o_ref[...] = acc[...].astype(out_dtype)

    if mode == "nn":
        a_spec = pl.BlockSpec((tm, tk), lambda m, n, k: (m, k))
        b_spec = pl.BlockSpec((tk, tn), lambda m, n, k: (k, n))
    elif mode == "nt":
        a_spec = pl.BlockSpec((tm, tk), lambda m, n, k: (m, k))
        b_spec = pl.BlockSpec((tn, tk), lambda m, n, k: (n, k))
    else:
        a_spec = pl.BlockSpec((tk, tm), lambda m, n, k: (k, m))
        b_spec = pl.BlockSpec((tk, tn), lambda m, n, k: (k, n))
    in_specs = [a_spec, b_spec]
    args = [a, b]
    aliases = {}
    if prev is not None:
        in_specs.append(pl.BlockSpec(memory_space=pl.ANY))
        args.append(prev)
        aliases = {2: 0}
    return _pcall(
        body, name=name, grid=(M // tm, N // tn, nk), in_specs=in_specs,
        out_specs=pl.BlockSpec((tm, tn), lambda m, n, k: (m, n + off)),
        out_shape=jax.ShapeDtypeStruct((M, out_cols), out_dtype),
        scratch_shapes=[pltpu.VMEM((tm, tn), F32)], input_output_aliases=aliases,
        compiler_params=_params(("parallel", "parallel", "arbitrary")),
    )(*args)


def cond_matmul(c_all, w, name):
    B, K = c_all.shape
    N = w.shape[1]
    tn = 256
    assert N % tn == 0

    def body(c_ref, w_ref, o_ref):
        cv = c_ref[...]
        ca = cv * _sigmoid(cv)
        o_ref[...] = jnp.dot(ca, w_ref[...], preferred_element_type=F32, precision=lax.Precision.HIGHEST)

    return _pcall(
        body, name=name, grid=(N // tn,),
        in_specs=[pl.BlockSpec((B, K), lambda n: (0, 0)), pl.BlockSpec((K, tn), lambda n: (0, n))],
        out_specs=pl.BlockSpec((B, tn), lambda n: (0, n)),
        out_shape=jax.ShapeDtypeStruct((B, N), F32), compiler_params=_params(("parallel",)),
    )(c_all, w)


FFN_TS = 512
FFN_CH = 256


def ffn_fwd(x, vecs, wi, wo, rows, name):
    r_sh, r_sc, r_gt, r_g = rows
    S, D = x.shape
    Fd = wo.shape[0]
    ts, ch = min(FFN_TS, S), FFN_CH
    ni, nj = S // ts, Fd // ch

    def body(x_ref, v_ref, wg_ref, wu_ref, wo_ref, xo_ref, g_ref, u_ref, y_ref, h_sc, acc):
        j = pl.program_id(1)

        @pl.when(j == 0)
        def _():
            xv = x_ref[...]
            r = lax.rsqrt(_rowmean(xv * xv) + EPS)
            gs = _row(v_ref, r_g) * (1.0 + _row(v_ref, r_sc))
            h_sc[...] = (xv * r * gs + _row(v_ref, r_sh)).astype(BF)
            acc[...] = jnp.zeros_like(acc)

        hb = h_sc[...]
        G = jnp.dot(hb, wg_ref[...], preferred_element_type=F32)
        U = jnp.dot(hb, wu_ref[...], preferred_element_type=F32)
        g_ref[...] = G.astype(BF)
        u_ref[...] = U.astype(BF)
        act = (G * _sigmoid(G) * U).astype(BF)
        acc[...] += jnp.dot(act, wo_ref[...], preferred_element_type=F32)

        @pl.when(j == nj - 1)
        def _():
            Y = acc[...]
            y_ref[...] = Y
            xo_ref[...] = x_ref[...] + (MACARON_W * _row(v_ref, r_gt)) * Y

    tok = pl.BlockSpec((ts, D), lambda i, j: (i, 0))
    hid = pl.BlockSpec((ts, ch), lambda i, j: (i, j))
    return _pcall(
        body, name=name, grid=(ni, nj),
        in_specs=[tok, pl.BlockSpec(vecs.shape, lambda i, j: (0, 0)),
                  pl.BlockSpec((D, ch), lambda i, j: (0, j)), pl.BlockSpec((D, ch), lambda i, j: (0, j + nj)),
                  pl.BlockSpec((ch, D), lambda i, j: (j, 0))],
        out_specs=[tok, hid, hid, tok],
        out_shape=[jax.ShapeDtypeStruct((S, D), F32), jax.ShapeDtypeStruct((S, Fd), BF),
                   jax.ShapeDtypeStruct((S, Fd), BF), jax.ShapeDtypeStruct((S, D), F32)],
        scratch_shapes=[pltpu.VMEM((ts, D), BF), pltpu.VMEM((ts, D), F32)],
        compiler_params=_params(("arbitrary", "arbitrary")),
    )(x, vecs, wi, wi, wo)


def ffn_bwd(dxo, x, vecs, gs_, us_, y, wi, wo, rows, name):
    r_sh, r_sc, r_gt, r_g = rows
    S, D = x.shape
    Fd = wo.shape[0]
    ts, ch = min(FFN_TS, S), FFN_CH
    ni, nj = S // ts, Fd // ch

    def body(dxo_ref, x_ref, v_ref, g_ref, u_ref, y_ref, wg_ref, wu_ref, wo_ref,
             dx_ref, act_ref, dg_ref, du_ref, hb_ref, dyb_ref, vg_ref, dyb_sc, dh_sc):
        i, j = pl.program_id(0), pl.program_id(1)

        @pl.when((i == 0) & (j == 0))
        def _():
            vg_ref[...] = jnp.zeros_like(vg_ref)

        @pl.when(j == 0)
        def _():
            dxo_v = dxo_ref[...]
            dyb = ((MACARON_W * _row(v_ref, r_gt)) * dxo_v).astype(BF)
            dyb_sc[...] = dyb
            dyb_ref[...] = dyb
            vg_ref[0:1, :] += MACARON_W * _colsum(dxo_v * y_ref[...])
            dh_sc[...] = jnp.zeros_like(dh_sc)

        dA = lax.dot_general(dyb_sc[...], wo_ref[...], CONTRACT_LAST, preferred_element_type=F32)
        G = g_ref[...].astype(F32)
        U = u_ref[...].astype(F32)
        sg = _sigmoid(G)
        sl = G * sg
        dU = (dA * sl).astype(BF)
        dG = (dA * U * (sg * (1.0 + G * (1.0 - sg)))).astype(BF)
        act_ref[...] = (sl * U).astype(BF)
        dg_ref[...] = dG
        du_ref[...] = dU
        dh_sc[...] += (lax.dot_general(dG, wg_ref[...], CONTRACT_LAST, preferred_element_type=F32)
                       + lax.dot_general(dU, wu_ref[...], CONTRACT_LAST, preferred_element_type=F32))

        @pl.when(j == nj - 1)
        def _():
            dh = dh_sc[...]
            xv = x_ref[...]
            r = lax.rsqrt(_rowmean(xv * xv) + EPS)
            n = xv * r
            g = _row(v_ref, r_g)
            sc1 = 1.0 + _row(v_ref, r_sc)
            gsc = g * sc1
            hb_ref[...] = (n * gsc + _row(v_ref, r_sh)).astype(BF)
            dhn = dh * n
            vg_ref[1:2, :] += _colsum(dh)
            vg_ref[2:3, :] += _colsum(dhn) * g
            vg_ref[3:4, :] += _colsum(dhn) * sc1
            dn = dh * gsc
            dx_ref[...] = dxo_ref[...] + r * (dn - n * _rowmean(dn * n))

    tok = pl.BlockSpec((ts, D), lambda i, j: (i, 0))
    hid = pl.BlockSpec((ts, ch), lambda i, j: (i, j))
    return _pcall(
        body, name=name, grid=(ni, nj),
        in_specs=[tok, tok, pl.BlockSpec(vecs.shape, lambda i, j: (0, 0)), hid, hid, tok,
                  pl.BlockSpec((D, ch), lambda i, j: (0, j)), pl.BlockSpec((D, ch), lambda i, j: (0, j + nj)),
                  pl.BlockSpec((ch, D), lambda i, j: (j, 0))],
        out_specs=[tok, hid, hid, hid, tok, tok, pl.BlockSpec((SUBLANES, D), lambda i, j: (0, 0))],
        out_shape=[jax.ShapeDtypeStruct((S, D), F32), jax.ShapeDtypeStruct((S, Fd), BF),
                   jax.ShapeDtypeStruct((S, Fd), BF), jax.ShapeDtypeStruct((S, Fd), BF),
                   jax.ShapeDtypeStruct((S, D), BF), jax.ShapeDtypeStruct((S, D), BF),
                   jax.ShapeDtypeStruct((SUBLANES, D), F32)],
        scratch_shapes=[pltpu.VMEM((ts, D), BF), pltpu.VMEM((ts, D), F32)],
        compiler_params=_params(("arbitrary", "arbitrary")),
    )(dxo, x, vecs, gs_, us_, y, wi, wi, wo)


def final_fwd_bwd(x, tgt, vecs, name):
    S, D = x.shape
    ts = min(512, S)

    def body(x_ref, t_ref, v_ref, dx_ref, vg_ref):
        @pl.when(pl.program_id(0) == 0)
        def _():
            vg_ref[...] = jnp.zeros_like(vg_ref)

        xv = x_ref[...]
        r = lax.rsqrt(_rowmean(xv * xv) + EPS)
        n = xv * r
        g = _row(v_ref, R_GF)
        sc1 = 1.0 + _row(v_ref, R_FSC)
        gsc = g * sc1
        e = n * gsc + _row(v_ref, R_FSH) - t_ref[...]
        vg_ref[3:4, :] += _colsum(e * e)
        dout = e * (1.0 / D)
        dn_ = dout * n
        vg_ref[0:1, :] += _colsum(dout)
        vg_ref[1:2, :] += _colsum(dn_) * g
        vg_ref[2:3, :] += _colsum(dn_) * sc1
        dn = dout * gsc
        dx_ref[...] = r * (dn - n * _rowmean(dn * n))

    tok = pl.BlockSpec((ts, D), lambda i: (i, 0))
    return _pcall(
        body, name=name, grid=(S // ts,),
        in_specs=[tok, tok, pl.BlockSpec(vecs.shape, lambda i: (0, 0))],
        out_specs=[tok, pl.BlockSpec((SUBLANES, D), lambda i: (0, 0))],
        out_shape=[jax.ShapeDtypeStruct((S, D), F32), jax.ShapeDtypeStruct((SUBLANES, D), F32)],
        compiler_params=_params(("arbitrary",)),
    )(x, tgt, vecs)


def norm_matmul(x, vecs, w, rows, name):
    r_sh, r_sc, r_g = rows
    S, D = x.shape
    N = w.shape[1]
    ts, tn = min(512, S), 512
    nn = N // tn

    def body(x_ref, v_ref, w_ref, o_ref, h_sc):
        @pl.when(pl.program_id(1) == 0)
        def _():
            xv = x_ref[...]
            r = lax.rsqrt(_rowmean(xv * xv) + EPS)
            gs = _row(v_ref, r_g) * (1.0 + _row(v_ref, r_sc))
            h_sc[...] = (xv * r * gs + _row(v_ref, r_sh)).astype(BF)

        o_ref[...] = jnp.dot(h_sc[...], w_ref[...], preferred_element_type=F32)

    return _pcall(
        body, name=name, grid=(S // ts, nn),
        in_specs=[pl.BlockSpec((ts, D), lambda i, n: (i, 0)), pl.BlockSpec(vecs.shape, lambda i, n: (0, 0)),
                  pl.BlockSpec((D, tn), lambda i, n: (0, n))],
        out_specs=pl.BlockSpec((ts, tn), lambda i, n: (i, n)),
        out_shape=jax.ShapeDtypeStruct((S, N), F32),
        scratch_shapes=[pltpu.VMEM((ts, D), BF)],
        compiler_params=_params(("arbitrary", "arbitrary")),
    )(x, vecs, w)


SEQ_TT = 256
CONV_PAD = 32


def conv_fwd(proj, convw4, conv_b, name):
    S = proj.shape[0]
    M = conv_b.shape[1]
    nb = M // LANES
    tt = min(SEQ_TT, S)

    def body(uv_ref, ug_ref, w_ref, b_ref, cq_ref, qp):
        qp[0:CONV_PAD, :] = jnp.zeros((CONV_PAD, LANES), F32)

        def step(t, carry):
            base = pl.multiple_of(t * tt, tt)
            qp[pl.ds(base + CONV_PAD, tt), :] = uv_ref[pl.ds(base, tt), :] * _sigmoid(ug_ref[pl.ds(base, tt), :])
            acc = jnp.broadcast_to(b_ref[...], (tt, LANES))
            for k in range(CONV_WIDTH):
                acc = acc + w_ref[k:k + 1, :] * qp[pl.ds(base + (CONV_PAD - CONV_WIDTH + 1) + k, tt), :]
            cq_ref[pl.ds(base, tt), :] = acc
            return carry

        lax.fori_loop(0, S // tt, step, 0)

    return _pcall(
        body, name=name, grid=(nb,),
        in_specs=[pl.BlockSpec((S, LANES), lambda c: (0, c)), pl.BlockSpec((S, LANES), lambda c: (0, c + nb)),
                  pl.BlockSpec((None, 32, LANES), lambda c: (c, 0, 0)), pl.BlockSpec((1, LANES), lambda c: (0, c))],
        out_specs=pl.BlockSpec((S, LANES), lambda c: (0, c)),
        out_shape=jax.ShapeDtypeStruct((S, M), F32),
        scratch_shapes=[pltpu.VMEM((S + CONV_PAD, LANES), F32)],
        compiler_params=_params(("arbitrary",)),
    )(proj, proj, convw4, conv_b)


def conv_bwd(dcq, proj, convw4, name):
    S, M = dcq.shape
    nb = M // LANES
    tt = min(SEQ_TT, S)
    off = CONV_PAD - CONV_WIDTH + 1

    def body(dcq_ref, uv_ref, ug_ref, w_ref, duv_ref, dug_ref, dw_ref, db_ref, qp, dp, dw8, db8):
        qp[0:CONV_PAD, :] = jnp.zeros((CONV_PAD, LANES), F32)
        dp[S:S + CONV_PAD, :] = jnp.zeros((CONV_PAD, LANES), F32)
        dw8[...] = jnp.zeros_like(dw8)
        db8[...] = jnp.zeros_like(db8)

        def fill(t, carry):
            base = pl.multiple_of(t * tt, tt)
            qp[pl.ds(base + CONV_PAD, tt), :] = uv_ref[pl.ds(base, tt), :] * _sigmoid(ug_ref[pl.ds(base, tt), :])
            dp[pl.ds(base, tt), :] = dcq_ref[pl.ds(base, tt), :]
            return carry

        lax.fori_loop(0, S // tt, fill, 0)

        def step(t, carry):
            base = pl.multiple_of(t * tt, tt)
            d_t = dcq_ref[pl.ds(base, tt), :]
            db8[...] += d_t.reshape(tt // SUBLANES, SUBLANES, LANES).sum(axis=0)
            dq = jnp.zeros((tt, LANES), F32)
            for k in range(CONV_WIDTH):
                prod = d_t * qp[pl.ds(base + off + k, tt), :]
                dw8[k] += prod.reshape(tt // SUBLANES, SUBLANES, LANES).sum(axis=0)
                dq = dq + w_ref[k:k + 1, :] * dp[pl.ds(base + (CONV_WIDTH - 1) - k, tt), :]
            uv = uv_ref[pl.ds(base, tt), :]
            sg = _sigmoid(ug_ref[pl.ds(base, tt), :])
            duv_ref[pl.ds(base, tt), :] = dq * sg
            dug_ref[pl.ds(base, tt), :] = dq * uv * sg * (1.0 - sg)
            return carry

        lax.fori_loop(0, S // tt, step, 0)
        dw_ref[...] = jnp.zeros_like(dw_ref)
        for k in range(CONV_WIDTH):
            dw_ref[k:k + 1, :] = _colsum(dw8[k])
        db_ref[...] = _colsum(db8[...])

    col = lambda o: pl.BlockSpec((S, LANES), lambda c: (0, c + o))
    return _pcall(
        body, name=name, grid=(nb,),
        in_specs=[col(0), col(0), col(nb), pl.BlockSpec((None, 32, LANES), lambda c: (c, 0, 0))],
        out_specs=[col(0), col(0), pl.BlockSpec((None, 32, LANES), lambda c: (c, 0, 0)),
                   pl.BlockSpec((1, LANES), lambda c: (0, c))],
        out_shape=[jax.ShapeDtypeStruct((S, M), F32), jax.ShapeDtypeStruct((S, M), F32),
                   jax.ShapeDtypeStruct((nb, 32, LANES), F32), jax.ShapeDtypeStruct((1, M), F32)],
        scratch_shapes=[pltpu.VMEM((S + CONV_PAD, LANES), F32), pltpu.VMEM((S + CONV_PAD, LANES), F32),
                        pltpu.VMEM((32, SUBLANES, LANES), F32), pltpu.VMEM((SUBLANES, LANES), F32)],
        compiler_params=_params(("arbitrary",)),
    )(dcq, proj, proj, convw4)


def _log_sigmoid(x):
    return jnp.minimum(x, 0.0) - jnp.log(1.0 + jnp.exp(-jnp.abs(x)))


def _rg_gate_terms(ra, ls):
    la = RG_C * ra * ls
    a = jnp.exp(la)
    th = jnp.tanh(la)
    mult = jnp.sqrt(-2.0 * th / (1.0 - th))
    return a, mult


def rnn_fwd(proj, rnnw4, rnn_b, bda, bdi, b_a, b_i, lam, name):
    S = proj.shape[0]
    M = rnn_b.shape[1]
    nb = M // LANES
    tt = min(SEQ_TT, S)
    KW = RNN_CONV_WIDTH

    def body(ux_ref, w_ref, rb_ref, bda_ref, bdi_ref, ba_ref, bi_ref, lam_ref,
             xr_ref, ra_ref, ii_ref, h_ref, uxp, a_sc, b_sc):
        uxp[0:SUBLANES, :] = jnp.zeros((SUBLANES, LANES), F32)
        ls = _log_sigmoid(lam_ref[...])

        def step(t, carry):
            base = pl.multiple_of(t * tt, tt)
            uxp[pl.ds(base + SUBLANES, tt), :] = ux_ref[pl.ds(base, tt), :]
            xr = jnp.broadcast_to(rb_ref[...], (tt, LANES))
            for k in range(KW):
                xr = xr + w_ref[k:k + 1, :] * uxp[pl.ds(base + (SUBLANES - KW + 1) + k, tt), :]
            xb = xr.astype(BF)
            ra = _sigmoid(jnp.dot(xb, bda_ref[...], preferred_element_type=F32) + ba_ref[...])
            ii = _sigmoid(jnp.dot(xb, bdi_ref[...], preferred_element_type=F32) + bi_ref[...])
            a, mult = _rg_gate_terms(ra, ls)
            xr_ref[pl.ds(base, tt), :] = xr
            ra_ref[pl.ds(base, tt), :] = ra
            ii_ref[pl.ds(base, tt), :] = ii
            a_sc[pl.ds(base, tt), :] = a
            b_sc[pl.ds(base, tt), :] = mult * (ii * xr)
            return carry

        lax.fori_loop(0, S // tt, step, 0)

        rows = lax.broadcasted_iota(I32, (SUBLANES, LANES), 0)

        def scan(t, hprev):
            base = pl.multiple_of(t * SUBLANES, SUBLANES)
            A = a_sc[pl.ds(base, SUBLANES), :]
            B = b_sc[pl.ds(base, SUBLANES), :]
            for d in (1, 2, 4):
                As = jnp.where(rows >= d, pltpu.roll(A, d, axis=0), 1.0)
                Bs = jnp.where(rows >= d, pltpu.roll(B, d, axis=0), 0.0)
                B = A * Bs + B
                A = A * As
            hh = B + A * hprev
            h_ref[pl.ds(base, SUBLANES), :] = hh
            return jnp.broadcast_to(hh[SUBLANES - 1:SUBLANES, :], (SUBLANES, LANES))

        lax.fori_loop(0, S // SUBLANES, scan, jnp.zeros((SUBLANES, LANES), F32))

    col = lambda o: pl.BlockSpec((S, LANES), lambda c: (0, c + o))
    vec = pl.BlockSpec((1, LANES), lambda c: (0, c))
    diag = pl.BlockSpec((LANES, LANES), lambda c: (c, c))
    return _pcall(
        body, name=name, grid=(nb,),
        in_specs=[col(2 * nb), pl.BlockSpec((None, SUBLANES, LANES), lambda c: (c, 0, 0)), vec, diag, diag, vec, vec, vec],
        out_specs=[col(0)] * 4,
        out_shape=[jax.ShapeDtypeStruct((S, M), F32)] * 4,
        scratch_shapes=[pltpu.VMEM((S + SUBLANES, LANES), F32), pltpu.VMEM((S, LANES), F32), pltpu.VMEM((S, LANES), F32)],
        compiler_params=_params(("arbitrary",)),
    )(proj, rnnw4, rnn_b, bda, bdi, b_a, b_i, lam)


def rnn_bwd(dhout, h, xr, ra, ii, proj, rnnw4, bda, bdi, lam, name):
    S, M = h.shape
    nb = M // LANES
    tt = min(SEQ_TT, S)
    KW = RNN_CONV_WIDTH
    SL = SUBLANES

    def body(dh_ref, h_ref, xr_ref, ra_ref, ii_ref, ux_ref, w_ref, bda_ref, bdi_ref, lam_ref,
             dux_ref, dwa_ref, dwi_ref, drw_ref, vec_ref,
             a_sc, hp, g_sc, dpa_sc, dpi_sc, dxp, uxp, acc8, drw8):
        zero8 = jnp.zeros((SL, LANES), F32)
        a_sc[S:S + SL, :] = zero8
        hp[0:SL, :] = zero8
        dxp[S:S + SL, :] = zero8
        uxp[0:SL, :] = zero8
        acc8[...] = jnp.zeros_like(acc8)
        drw8[...] = jnp.zeros_like(drw8)
        lamv = lam_ref[...]
        ls = _log_sigmoid(lamv)

        def fill(t, carry):
            base = pl.multiple_of(t * tt, tt)
            a_sc[pl.ds(base, tt), :] = jnp.exp(RG_C * ra_ref[pl.ds(base, tt), :] * ls)
            hp[pl.ds(base + SL, tt), :] = h_ref[pl.ds(base, tt), :]
            uxp[pl.ds(base + SL, tt), :] = ux_ref[pl.ds(base, tt), :]
            return carry

        lax.fori_loop(0, S // tt, fill, 0)

        rows = lax.broadcasted_iota(I32, (SL, LANES), 0)
        nt8 = S // SL

        def rscan(t, gnext):
            base = pl.multiple_of((nt8 - 1 - t) * SL, SL)
            A = a_sc[pl.ds(base + 1, SL), :]
            B = dh_ref[pl.ds(base, SL), :]
            for d in (1, 2, 4):
                As = jnp.where(rows < SL - d, pltpu.roll(A, SL - d, axis=0), 1.0)
                Bs = jnp.where(rows < SL - d, pltpu.roll(B, SL - d, axis=0), 0.0)
                B = A * Bs + B
                A = A * As
            g = B + A * gnext
            g_sc[pl.ds(base, SL), :] = g
            return jnp.broadcast_to(g[0:1, :], (SL, LANES))

        lax.fori_loop(0, nt8, rscan, zero8)

        def red8(v):
            return v.reshape(tt // SL, SL, LANES).sum(axis=0)

        def step(t, carry):
            base = pl.multiple_of(t * tt, tt)
            g = g_sc[pl.ds(base, tt), :]
            hprev = hp[pl.ds(base + SL - 1, tt), :]
            xr_t = xr_ref[pl.ds(base, tt), :]
            ra_t = ra_ref[pl.ds(base, tt), :]
            ii_t = ii_ref[pl.ds(base, tt), :]
            a, mult = _rg_gate_terms(ra_t, ls)
            gx = g * xr_t
            dmult = gx * ii_t
            dii = gx * mult
            dxr = g * (mult * ii_t)
            dla = g * hprev * a - dmult * (a * a) / mult
            acc8[3] += red8(dla * ra_t)
            dpa = dla * (RG_C * ls) * ra_t * (1.0 - ra_t)
            dpi = dii * ii_t * (1.0 - ii_t)
            dpab = dpa.astype(BF)
            dpib = dpi.astype(BF)
            dxr = dxr + (lax.dot_general(dpab, bda_ref[...], CONTRACT_LAST, preferred_element_type=F32)
                         + lax.dot_general(dpib, bdi_ref[...], CONTRACT_LAST, preferred_element_type=F32))
            dpa_sc[pl.ds(base, tt), :] = dpab
            dpi_sc[pl.ds(base, tt), :] = dpib
            dxp[pl.ds(base, tt), :] = dxr
            acc8[0] += red8(dxr)
            acc8[1] += red8(dpa)
            acc8[2] += red8(dpi)
            return carry

        lax.fori_loop(0, S // tt, step, 0)

        def convb(t, carry):
            base = pl.multiple_of(t * tt, tt)
            d_t = dxp[pl.ds(base, tt), :]
            dux = jnp.zeros((tt, LANES), F32)
            for k in range(KW):
                drw8[k] += red8(d_t * uxp[pl.ds(base + (SL - KW + 1) + k, tt), :])
                dux = dux + w_ref[k:k + 1, :] * dxp[pl.ds(base + (KW - 1) - k, tt), :]
            dux_ref[pl.ds(base, tt), :] = dux
            return carry

        lax.fori_loop(0, S // tt, convb, 0)

        xb = xr_ref[...].astype(BF)
        dwa_ref[...] = lax.dot_general(xb, dpa_sc[...], CONTRACT_FIRST, preferred_element_type=F32)
        dwi_ref[...] = lax.dot_general(xb, dpi_sc[...], CONTRACT_FIRST, preferred_element_type=F32)
        drw_ref[...] = jnp.zeros_like(drw_ref)
        vec_ref[...] = jnp.zeros_like(vec_ref)
        for k in range(KW):
            drw_ref[k:k + 1, :] = _colsum(drw8[k])
        for k in range(3):
            vec_ref[k:k + 1, :] = _colsum(acc8[k])
        vec_ref[3:4, :] = _colsum(acc8[3]) * (RG_C * _sigmoid(-lamv))

    col = lambda o: pl.BlockSpec((S, LANES), lambda c: (0, c + o))
    vec = pl.BlockSpec((1, LANES), lambda c: (0, c))
    diag = pl.BlockSpec((LANES, LANES), lambda c: (c, c))
    blk3 = lambda r: pl.BlockSpec((None, r, LANES), lambda c: (c, 0, 0))
    return _pcall(
        body, name=name, grid=(nb,),
        in_specs=[col(0), col(0), col(0), col(0), col(0), col(2 * nb), blk3(SL), diag, diag, vec],
        out_specs=[col(0), blk3(LANES), blk3(LANES), blk3(SL), pl.BlockSpec((SL, LANES), lambda c: (0, c))],
        out_shape=[jax.ShapeDtypeStruct((S, M), F32), jax.ShapeDtypeStruct((nb, LANES, LANES), F32),
                   jax.ShapeDtypeStruct((nb, LANES, LANES), F32), jax.ShapeDtypeStruct((nb, SL, LANES), F32),
                   jax.ShapeDtypeStruct((SL, M), F32)],
        scratch_shapes=[pltpu.VMEM((S + SL, LANES), F32), pltpu.VMEM((S + SL, LANES), F32), pltpu.VMEM((S, LANES), F32),
                        pltpu.VMEM((S, LANES), BF), pltpu.VMEM((S, LANES), BF), pltpu.VMEM((S + SL, LANES), F32),
                        pltpu.VMEM((S + SL, LANES), F32), pltpu.VMEM((SL, SL, LANES), F32), pltpu.VMEM((SL, SL, LANES), F32)],
        compiler_params=_params(("arbitrary",)),
    )(dhout, h, xr, ra, ii, proj, rnnw4, bda, bdi, lam)


GELU_K = 0.7978845608028654
GELU_C = 0.044715


def _layernorm_parts(cq):
    mu = _rowmean(cq)
    d = cq - mu
    rstd = lax.rsqrt(_rowmean(d * d) + EPS)
    return d * rstd, rstd


def mix_out(cq, proj, h, x, vecs, lnv, wout, name):
    S, D = x.shape
    M = cq.shape[1]
    ts = min(512, S)

    def body(cq_ref, uy_ref, h_ref, x_ref, v_ref, ln_ref, w_ref, xo_ref, ym_ref, yc_ref):
        z, _ = _layernorm_parts(cq_ref[...])
        l = z * _row(ln_ref, 0) + _row(ln_ref, 1)
        yc_ref[:, 0:M] = (l * _sigmoid(l)).astype(BF)
        uy = uy_ref[...]
        gelu = 0.5 * uy * (1.0 + jnp.tanh(GELU_K * (uy + GELU_C * uy * uy * uy)))
        yc_ref[:, M:2 * M] = (gelu * h_ref[...]).astype(BF)
        ym = jnp.dot(yc_ref[...], w_ref[...], preferred_element_type=F32)
        ym_ref[...] = ym
        xo_ref[...] = x_ref[...] + _row(v_ref, R_GT2) * ym

    tok = pl.BlockSpec((ts, D), lambda i: (i, 0))
    mtok = lambda o: pl.BlockSpec((ts, M), lambda i: (i, o))
    return _pcall(
        body, name=name, grid=(S // ts,),
        in_specs=[mtok(0), mtok(3), mtok(0), tok, pl.BlockSpec(vecs.shape, lambda i: (0, 0)),
                  pl.BlockSpec(lnv.shape, lambda i: (0, 0)), pl.BlockSpec(wout.shape, lambda i: (0, 0))],
        out_specs=[tok, tok, pl.BlockSpec((ts, 2 * M), lambda i: (i, 0))],
        out_shape=[jax.ShapeDtypeStruct((S, D), F32), jax.ShapeDtypeStruct((S, D), F32),
                   jax.ShapeDtypeStruct((S, 2 * M), BF)],
        compiler_params=_params(("arbitrary",)),
    )(cq, proj, h, x, vecs, lnv, wout)


def mix_out_bwd(dxo, ym, vecs, wout, cq, lnv, proj, h, name):
    S, D = dxo.shape
    M = cq.shape[1]
    ts = min(512, S)

    def body(dxo_ref, ym_ref, v_ref, w_ref, cq_ref, ln_ref, uy_ref, h_ref,
             dcq_ref, dh_ref, duy_ref, dyb_ref, vgd_ref, vgm_ref):
        @pl.when(pl.program_id(0) == 0)
        def _():
            vgd_ref[...] = jnp.zeros_like(vgd_ref)
            vgm_ref[...] = jnp.zeros_like(vgm_ref)

        dxo_v = dxo_ref[...]
        dyb = (_row(v_ref, R_GT2) * dxo_v).astype(BF)
        dyb_ref[...] = dyb
        vgd_ref[0:1, :] += _colsum(dxo_v * ym_ref[...])
        dycat = lax.dot_general(dyb, w_ref[...], CONTRACT_LAST, preferred_element_type=F32)
        dyc = dycat[:, 0:M]
        dyr = dycat[:, M:2 * M]
        z, rstd = _layernorm_parts(cq_ref[...])
        lng = _row(ln_ref, 0)
        l = z * lng + _row(ln_ref, 1)
        sl = _sigmoid(l)
        dl = dyc * (sl * (1.0 + l * (1.0 - sl)))
        vgm_ref[0:1, :] += _colsum(dl * z)
        vgm_ref[1:2, :] += _colsum(dl)
        dz = dl * lng
        dcq_ref[...] = rstd * (dz - _rowmean(dz) - z * _rowmean(dz * z))
        uy = uy_ref[...]
        u2 = uy * uy
        th = jnp.tanh(GELU_K * (uy + GELU_C * uy * u2))
        gelu = 0.5 * uy * (1.0 + th)
        dgelu = 0.5 * (1.0 + th) + 0.5 * uy * (1.0 - th * th) * (GELU_K * (1.0 + 3.0 * GELU_C * u2))
        dh_ref[...] = dyr * gelu
        duy_ref[...] = dyr * h_ref[...] * dgelu

    tok = pl.BlockSpec((ts, D), lambda i: (i, 0))
    mtok = lambda o: pl.BlockSpec((ts, M), lambda i: (i, o))
    return _pcall(
        body, name=name, grid=(S // ts,),
        in_specs=[tok, tok, pl.BlockSpec(vecs.shape, lambda i: (0, 0)), pl.BlockSpec(wout.shape, lambda i: (0, 0)),
                  mtok(0), pl.BlockSpec(lnv.shape, lambda i: (0, 0)), mtok(3), mtok(0)],
        out_specs=[mtok(0), mtok(0), mtok(0), tok, pl.BlockSpec((SUBLANES, D), lambda i: (0, 0)),
                   pl.BlockSpec((SUBLANES, M), lambda i: (0, 0))],
        out_shape=[jax.ShapeDtypeStruct((S, M), F32)] * 3 + [jax.ShapeDtypeStruct((S, D), BF),
                   jax.ShapeDtypeStruct((SUBLANES, D), F32), jax.ShapeDtypeStruct((SUBLANES, M), F32)],
        compiler_params=_params(("arbitrary",)),
    )(dxo, ym, vecs, wout, cq, lnv, proj, h)


def mix_in_bwd(dparts, x, dxo, vecs, win, name):
    S, D = x.shape
    M = dparts[0].shape[1]
    ts = min(512, S)

    def body(d0, d1, d2, d3, x_ref, dxo_ref, v_ref, w_ref, dx_ref, hb_ref, dp_ref, vg_ref):
        @pl.when(pl.program_id(0) == 0)
        def _():
            vg_ref[...] = jnp.zeros_like(vg_ref)

        for q, dref in enumerate((d0, d1, d2, d3)):
            dp_ref[:, q * M:(q + 1) * M] = dref[...].astype(BF)
        dh = lax.dot_general(dp_ref[...], w_ref[...], CONTRACT_LAST, preferred_element_type=F32)
        xv = x_ref[...]
        r = lax.rsqrt(_rowmean(xv * xv) + EPS)
        n = xv * r
        g = _row(v_ref, R_G2)
        sc1 = 1.0 + _row(v_ref, R_SC2)
        gsc = g * sc1
        hb_ref[...] = (n * gsc + _row(v_ref, R_SH2)).astype(BF)
        dhn = dh * n
        vg_ref[0:1, :] += _colsum(dh)
        vg_ref[1:2, :] += _colsum(dhn) * g
        vg_ref[2:3, :] += _colsum(dhn) * sc1
        dn = dh * gsc
        dx_ref[...] = dxo_ref[...] + r * (dn - n * _rowmean(dn * n))

    tok = pl.BlockSpec((ts, D), lambda i: (i, 0))
    mtok = pl.BlockSpec((ts, M), lambda i: (i, 0))
    return _pcall(
        body, name=name, grid=(S // ts,),
        in_specs=[mtok] * 4 + [tok, tok, pl.BlockSpec(vecs.shape, lambda i: (0, 0)), pl.BlockSpec(win.shape, lambda i: (0, 0))],
        out_specs=[tok, tok, pl.BlockSpec((ts, 4 * M), lambda i: (i, 0)), pl.BlockSpec((SUBLANES, D), lambda i: (0, 0))],
        out_shape=[jax.ShapeDtypeStruct((S, D), F32), jax.ShapeDtypeStruct((S, D), BF),
                   jax.ShapeDtypeStruct((S, 4 * M), BF), jax.ShapeDtypeStruct((SUBLANES, D), F32)],
        compiler_params=_params(("arbitrary",)),
    )(*dparts, x, dxo, vecs, win)


def _adamw(w, g, m, v):
    m = ADAM_B1 * m + (1.0 - ADAM_B1) * g
    v = ADAM_B2 * v + (1.0 - ADAM_B2) * (g * g)
    m_hat = m / (1.0 - ADAM_B1 ** ADAM_STEP)
    v_hat = v / (1.0 - ADAM_B2 ** ADAM_STEP)
    delta = -ADAM_LR * (m_hat / (jnp.sqrt(v_hat) + ADAM_EPS) + ADAM_WD * w)
    return delta, m, v


def adam_big(w, g, m, v, name):
    R, C = w.shape
    tr = 256 if R % 256 == 0 else R // 2 if (R // 2) % SUBLANES == 0 and R > 512 else R
    tc = C if C <= 1536 else (1152 if C % 1152 == 0 else 1024)
    assert R % tr == 0 and C % tc == 0

    def body(w_ref, g_ref, m_ref, v_ref, d_ref, nm_ref, nv_ref):
        d, nm, nv = _adamw(w_ref[...], g_ref[...], m_ref[...], v_ref[...])
        d_ref[...] = d
        nm_ref[...] = nm
        nv_ref[...] = nv

    blk = pl.BlockSpec((tr, tc), lambda i, j: (i, j))
    return _pcall(
        body, name=name, grid=(R // tr, C // tc), in_specs=[blk] * 4, out_specs=[blk] * 3,
        out_shape=[jax.ShapeDtypeStruct((R, C), F32)] * 3, compiler_params=_params(("parallel", "parallel")),
    )(w, g, m, v)


def adam_cond(c_all, dmod, w, m, v, name):
    B, Kin = c_all.shape
    N = w.shape[1]
    tn = 256
    assert N % tn == 0

    def body(c_ref, d_ref, w_ref, m_ref, v_ref, g_ref, dl_ref, nm_ref, nv_ref):
        cv = c_ref[...]
        ca = cv * _sigmoid(cv)
        g = lax.dot_general(ca, d_ref[...], CONTRACT_FIRST, preferred_element_type=F32, precision=lax.Precision.HIGHEST)
        d, nm, nv = _adamw(w_ref[...], g, m_ref[...], v_ref[...])
        g_ref[...] = g
        dl_ref[...] = d
        nm_ref[...] = nm
        nv_ref[...] = nv

    blk = pl.BlockSpec((Kin, tn), lambda n: (0, n))
    return _pcall(
        body, name=name, grid=(N // tn,),
        in_specs=[pl.BlockSpec((B, Kin), lambda n: (0, 0)), pl.BlockSpec((B, tn), lambda n: (0, n)), blk, blk, blk],
        out_specs=[blk] * 4, out_shape=[jax.ShapeDtypeStruct((Kin, N), F32)] * 4,
        compiler_params=_params(("parallel",)),
    )(c_all, dmod, w, m, v)


def adam_small(ws, gs, ms, vs, name):
    n = len(ws)

    def body(*refs):
        ins, outs = refs[:4 * n], refs[4 * n:]
        for k in range(n):
            d, nm, nv = _adamw(ins[k][...], ins[n + k][...], ins[2 * n + k][...], ins[3 * n + k][...])
            outs[k][...] = d
            outs[n + k][...] = nm
            outs[2 * n + k][...] = nv

    vm = pl.BlockSpec(memory_space=pltpu.VMEM)
    shapes = [jax.ShapeDtypeStruct(w.shape, F32) for w in ws]
    out = _pcall(body, name=name, in_specs=[vm] * (4 * n), out_specs=[vm] * (3 * n), out_shape=shapes * 3,
                 compiler_params=_params())(*ws, *gs, *ms, *vs)
    return out[:n], out[n:2 * n], out[2 * n:]


def _me():
    return lax.axis_index("x"), lax.axis_index("y"), lax.axis_index("c")


def _flip(x, y, p):
    return (x ^ (p >> 1) if (p >> 1) else x), (y ^ (p & 1) if (p & 1) else y)


def _handshake(peers):
    barrier = pltpu.get_barrier_semaphore()
    for peer in peers:
        pl.semaphore_signal(barrier, inc=1, device_id=peer, device_id_type=MESH)
    pl.semaphore_wait(barrier, len(peers))


def _seq_call(body, *, name, n_in, out_shape, sem_shapes, collective_id):
    del n_in
    return pl.kernel(body, out_type=out_shape, mesh=plsc.ScalarSubcoreMesh(axis_name="sq", num_cores=1), name=name,
                     scratch_types=sem_shapes, compiler_params=pltpu.CompilerParams(collective_id=collective_id))


def _hbm_comm_call(body, *, name, n_in, out_shape, sem_shapes, seq_id):
    if seq_id is not None:
        return _seq_call(body, name=name, n_in=n_in, out_shape=out_shape, sem_shapes=sem_shapes, collective_id=seq_id)
    anyspec = pl.BlockSpec(memory_space=pl.ANY)
    return _pcall(body, name=name, in_specs=[anyspec] * n_in, out_specs=[anyspec] * len(out_shape), out_shape=out_shape,
                  scratch_shapes=sem_shapes, compiler_params=_params())


def allgather_devices(v, name, with_sum=False):
    R, L = v.shape

    def body(v_ref, out_ref, *rest):
        if with_sum:
            sum_ref, send_sems, recv_sems = rest
        else:
            send_sems, recv_sems = rest
        x, y, c = _me()
        me = 4 * x + 2 * y + c
        out_ref[me] = v_ref[...]
        copies = []
        for p in range(1, N_DEV):
            px, py = _flip(x, y, p >> 1)
            pc = (1 - c) if (p & 1) else c
            peer = 4 * px + 2 * py + pc
            send = pltpu.make_async_remote_copy(src_ref=v_ref, dst_ref=out_ref.at[me], send_sem=send_sems.at[p - 1],
                                                recv_sem=recv_sems.at[p - 1], device_id=(px, py, pc), device_id_type=MESH)
            send.start()
            recv = pltpu.make_async_remote_copy(src_ref=v_ref, dst_ref=out_ref.at[peer], send_sem=send_sems.at[p - 1],
                                                recv_sem=recv_sems.at[p - 1], device_id=(px, py, pc), device_id_type=MESH)
            copies.append((send, recv))
        for send, recv in copies:
            recv.wait_recv()
        for send, recv in copies:
            send.wait_send()
        if with_sum:
            s = out_ref[0]
            for k in range(1, N_DEV):
                s = s + out_ref[k]
            sum_ref[...] = s

    vm = pl.BlockSpec(memory_space=pltpu.VMEM)
    out_shape = [jax.ShapeDtypeStruct((N_DEV, R, L), F32)]
    if with_sum:
        out_shape.append(jax.ShapeDtypeStruct((R, L), F32))
    return _pcall(
        body, name=name, in_specs=[vm], out_specs=[vm] * len(out_shape), out_shape=out_shape,
        scratch_shapes=[pltpu.SemaphoreType.DMA((N_DEV - 1,)), pltpu.SemaphoreType.DMA((N_DEV - 1,))],
        compiler_params=_params(),
    )(v)


def allgather_chips(v, name):
    R, L = v.shape

    def body(v_ref, out_ref, send_sems, recv_sems):
        x, y, c = _me()
        chip = 2 * x + y
        out_ref[chip] = v_ref[...]
        copies = []
        for p in range(1, N_CHIPS):
            px, py = _flip(x, y, p)
            send = pltpu.make_async_remote_copy(src_ref=v_ref, dst_ref=out_ref.at[chip], send_sem=send_sems.at[p - 1],
                                                recv_sem=recv_sems.at[p - 1], device_id=(px, py, c), device_id_type=MESH)
            send.start()
            recv = pltpu.make_async_remote_copy(src_ref=v_ref, dst_ref=out_ref.at[2 * px + py], send_sem=send_sems.at[p - 1],
                                                recv_sem=recv_sems.at[p - 1], device_id=(px, py, c), device_id_type=MESH)
            copies.append((send, recv))
        for send, recv in copies:
            recv.wait_recv()
        for send, recv in copies:
            send.wait_send()

    vm = pl.BlockSpec(memory_space=pltpu.VMEM)
    return _pcall(
        body, name=name, in_specs=[vm], out_specs=vm, out_shape=jax.ShapeDtypeStruct((N_CHIPS, R, L), F32),
        scratch_shapes=[pltpu.SemaphoreType.DMA((N_CHIPS - 1,)), pltpu.SemaphoreType.DMA((N_CHIPS - 1,))],
        compiler_params=_params(),
    )(v)


def _shard_window(ref, kind, shard_shape, chip, half):
    r, c = shard_shape
    hr = r // 2
    if kind == "col":
        return ref.at[pl.ds(pl.multiple_of(half * hr, hr), hr), pl.ds(pl.multiple_of(chip * c, c), c)]
    return ref.at[pl.ds(pl.multiple_of(chip * r + half * hr, hr), hr), :]


def allgather_weights(shards, kinds, name, seq_id=None):
    n = len(shards)
    fulls = []
    for s, kind in zip(shards, kinds):
        r, c = s.shape
        fulls.append(jax.ShapeDtypeStruct((r, N_CHIPS * c) if kind == "col" else (N_CHIPS * r, c), s.dtype))

    def body(*refs):
        srcs, outs = refs[:n], refs[n:2 * n]
        send_sems, recv_sems, fsend_sems, frecv_sems, local_sems = refs[2 * n:]
        x, y, c = _me()
        chip = 2 * x + y
        sib = (x, y, 1 - c)
        if seq_id is not None:
            _handshake([(*_flip(x, y, p), c) for p in range(1, N_CHIPS)] + [sib])
        locals_, sends, fwds = [], [], []
        for i in range(n):
            shp = srcs[i].shape
            hr = shp[0] // 2
            win_all = (outs[i].at[:, pl.ds(pl.multiple_of(chip * shp[1], shp[1]), shp[1])] if kinds[i] == "col"
                       else outs[i].at[pl.ds(pl.multiple_of(chip * shp[0], shp[0]), shp[0]), :])
            lc = pltpu.make_async_copy(srcs[i], win_all, local_sems.at[i])
            lc.start()
            locals_.append(lc)
            my_half = srcs[i].at[pl.ds(pl.multiple_of(c * hr, hr), hr), :]
            for p in range(1, N_CHIPS):
                px, py = _flip(x, y, p)
                k = i * (N_CHIPS - 1) + p - 1
                cp = pltpu.make_async_remote_copy(src_ref=my_half, dst_ref=_shard_window(outs[i], kinds[i], shp, chip, c),
                                                  send_sem=send_sems.at[k], recv_sem=recv_sems.at[k],
                                                  device_id=(px, py, c), device_id_type=MESH)
                cp.start()
                sends.append(cp)
        for i in range(n):
            shp = srcs[i].shape
            for p in range(1, N_CHIPS):
                px, py = _flip(x, y, p)
                k = i * (N_CHIPS - 1) + p - 1
                landed = _shard_window(outs[i], kinds[i], shp, 2 * px + py, c)
                pltpu.make_async_remote_copy(src_ref=landed, dst_ref=landed, send_sem=send_sems.at[k], recv_sem=recv_sems.at[k],
                                             device_id=(px, py, c), device_id_type=MESH).wait_recv()
                fw = pltpu.make_async_remote_copy(src_ref=landed, dst_ref=landed, send_sem=fsend_sems.at[k],
                                                  recv_sem=frecv_sems.at[k], device_id=sib, device_id_type=MESH)
                fw.start()
                fwds.append(fw)
        for i in range(n):
            shp = srcs[i].shape
            for p in range(1, N_CHIPS):
                px, py = _flip(x, y, p)
                k = i * (N_CHIPS - 1) + p - 1
                other = _shard_window(outs[i], kinds[i], shp, 2 * px + py, 1 - c)
                pltpu.make_async_remote_copy(src_ref=other, dst_ref=other, send_sem=fsend_sems.at[k], recv_sem=frecv_sems.at[k],
                                             device_id=sib, device_id_type=MESH).wait_recv()
        for cp in sends + fwds:
            cp.wait_send()
        for lc in locals_:
            lc.wait()

    nk = n * (N_CHIPS - 1)
    return _hbm_comm_call(
        body, name=name, n_in=n, out_shape=fulls, seq_id=seq_id,
        sem_shapes=[pltpu.SemaphoreType.DMA((nk,)), pltpu.SemaphoreType.DMA((nk,)), pltpu.SemaphoreType.DMA((nk,)),
                    pltpu.SemaphoreType.DMA((nk,)), pltpu.SemaphoreType.DMA((n,))],
    )(*shards)


def _as_halves(g, kind, shard_shape):
    r, c = shard_shape
    if kind == "col":
        return g.reshape(2, r // 2, N_CHIPS * c)
    return g.reshape(N_CHIPS, 2, r // 2, c)


def exchange_sibling_halves(grads, kinds, shard_shapes, name, seq_id=None):
    n = len(grads)
    views = [_as_halves(g, k, s) for g, k, s in zip(grads, kinds, shard_shapes)]
    outs = []
    for k, (r, c) in zip(kinds, shard_shapes):
        outs.append(jax.ShapeDtypeStruct((r // 2, N_CHIPS * c) if k == "col" else (N_CHIPS, r // 2, c), F32))

    def body(*refs):
        srcs, dsts = refs[:n], refs[n:2 * n]
        send_sems, recv_sems = refs[2 * n:]
        x, y, c = _me()
        if seq_id is not None:
            _handshake([(x, y, 1 - c)])
        cps = []
        for i in range(n):
            src = srcs[i].at[1 - c] if kinds[i] == "col" else srcs[i].at[:, 1 - c]
            cp = pltpu.make_async_remote_copy(src_ref=src, dst_ref=dsts[i], send_sem=send_sems.at[i], recv_sem=recv_sems.at[i],
                                              device_id=(x, y, 1 - c), device_id_type=MESH)
            cp.start()
            cps.append(cp)
        for cp in cps:
            cp.wait_recv()
        for cp in cps:
            cp.wait_send()

    return _hbm_comm_call(body, name=name, n_in=n, out_shape=outs, seq_id=seq_id,
                          sem_shapes=[pltpu.SemaphoreType.DMA((n,)), pltpu.SemaphoreType.DMA((n,))])(*views)


def add_sibling_half(g, recv, kind, shard_shape, cidx, name):
    r, c = shard_shape
    hr = r // 2
    gv = _as_halves(g, kind, shard_shape)
    tr = hr if hr <= 512 else (256 if hr % 256 == 0 else hr // 2)
    assert hr % tr == 0

    def body(ci_ref, g_ref, r_ref, h_ref, hb_ref):
        s = g_ref[...] + r_ref[...]
        h_ref[...] = s
        hb_ref[...] = s.astype(BF)

    if kind == "col":
        grid = (hr // tr, N_CHIPS)
        g_spec = pl.BlockSpec((None, tr, c), lambda i, k, ci: (ci[0], i, k))
        o_spec = pl.BlockSpec((tr, c), lambda i, k, ci: (i, k))
    else:
        grid = (hr // tr, N_CHIPS)
        g_spec = pl.BlockSpec((None, None, tr, c), lambda i, k, ci: (k, ci[0], i, 0))
        o_spec = pl.BlockSpec((None, tr, c), lambda i, k, ci: (k, i, 0))
    gs = pltpu.PrefetchScalarGridSpec(num_scalar_prefetch=1, grid=grid, in_specs=[g_spec, o_spec], out_specs=[o_spec, o_spec])
    return _pcall(
        body, name=name, grid_spec=gs,
        out_shape=[jax.ShapeDtypeStruct(recv.shape, F32), jax.ShapeDtypeStruct(recv.shape, BF)],
        compiler_params=_params(("parallel", "parallel")),
    )(cidx, gv, recv)


def exchange_chip_pieces(hbs, kinds, shard_shapes, name, seq_id=None):
    n = len(hbs)
    outs = [jax.ShapeDtypeStruct((N_CHIPS - 1, r // 2, c), BF) for (r, c) in shard_shapes]

    def body(*refs):
        srcs, dsts = refs[:n], refs[n:2 * n]
        send_sems, recv_sems = refs[2 * n:]
        x, y, c = _me()
        if seq_id is not None:
            _handshake([(*_flip(x, y, p), c) for p in range(1, N_CHIPS)])
        cps = []
        for i in range(n):
            cc = shard_shapes[i][1]
            for p in range(1, N_CHIPS):
                px, py = _flip(x, y, p)
                pchip = 2 * px + py
                src = (srcs[i].at[:, pl.ds(pl.multiple_of(pchip * cc, cc), cc)] if kinds[i] == "col" else srcs[i].at[pchip])
                k = i * (N_CHIPS - 1) + p - 1
                cp = pltpu.make_async_remote_copy(src_ref=src, dst_ref=dsts[i].at[p - 1], send_sem=send_sems.at[k],
                                                  recv_sem=recv_sems.at[k], device_id=(px, py, c), device_id_type=MESH)
                cp.start()
                cps.append(cp)
        for cp in cps:
            cp.wait_recv()
        for cp in cps:
            cp.wait_send()

    nk = n * (N_CHIPS - 1)
    return _hbm_comm_call(body, name=name, n_in=n, out_shape=outs, seq_id=seq_id,
                          sem_shapes=[pltpu.SemaphoreType.DMA((nk,)), pltpu.SemaphoreType.DMA((nk,))])(*hbs)


def sum_chip_pieces(h, pieces, kind, shard_shape, chip_core, name):
    r, c = shard_shape
    hr = r // 2
    tr = hr if hr <= 512 else (256 if hr % 256 == 0 else hr // 2)
    assert hr % tr == 0
    nrb = hr // tr

    def body(ci_ref, h_ref, p_ref, q_ref):
        q_ref[...] = ((h_ref[...] + p_ref[0].astype(F32)) + p_ref[1].astype(F32)) + p_ref[2].astype(F32)

    if kind == "col":
        h_spec = pl.BlockSpec((tr, c), lambda i, ci: (i, ci[0]))
    else:
        h_spec = pl.BlockSpec((None, tr, c), lambda i, ci: (ci[0], i, 0))
    gs = pltpu.PrefetchScalarGridSpec(
        num_scalar_prefetch=1, grid=(nrb,),
        in_specs=[h_spec, pl.BlockSpec((N_CHIPS - 1, tr, c), lambda i, ci: (0, i, 0))],
        out_specs=pl.BlockSpec((tr, c), lambda i, ci: (ci[1] * nrb + i, 0)))
    return _pcall(body, name=name, grid_spec=gs, out_shape=jax.ShapeDtypeStruct((r, c), F32),
                  compiler_params=_params(("parallel",)))(chip_core, h, pieces)


def exchange_reduced_halves(qs, name):
    n = len(qs)

    def body(*refs):
        bufs = refs[n:2 * n]
        send_sems, recv_sems = refs[2 * n:]
        x, y, c = _me()
        cps = []
        for i in range(n):
            hr = bufs[i].shape[0] // 2
            mine = bufs[i].at[pl.ds(pl.multiple_of(c * hr, hr), hr), :]
            other = bufs[i].at[pl.ds(pl.multiple_of((1 - c) * hr, hr), hr), :]
            cp = pltpu.make_async_remote_copy(src_ref=mine, dst_ref=mine, send_sem=send_sems.at[i], recv_sem=recv_sems.at[i],
                                              device_id=(x, y, 1 - c), device_id_type=MESH)
            cp.start()
            cps.append((cp, pltpu.make_async_remote_copy(src_ref=other, dst_ref=other, send_sem=send_sems.at[i],
                                                         recv_sem=recv_sems.at[i], device_id=(x, y, 1 - c), device_id_type=MESH)))
        for cp, rv in cps:
            rv.wait_recv()
        for cp, rv in cps:
            cp.wait_send()

    anyspec = pl.BlockSpec(memory_space=pl.ANY)
    return _pcall(
        body, name=name, in_specs=[anyspec] * n, out_specs=[anyspec] * n,
        out_shape=[jax.ShapeDtypeStruct(q.shape, F32) for q in qs], input_output_aliases={i: i for i in range(n)},
        scratch_shapes=[pltpu.SemaphoreType.DMA((n,)), pltpu.SemaphoreType.DMA((n,))],
        compiler_params=_params(),
    )(*qs)


def _rows128(a):
    return a.reshape(-1, LANES)


def _block_diag(w):
    H, d, _ = w.shape
    eye = jnp.eye(H, dtype=w.dtype)
    return jnp.einsum("hde,hg->hdge", w, eye).reshape(H * d, H * d)


def _diag_blocks(g4, H, d):
    nb = g4.shape[0]
    per = LANES // d
    g = g4.reshape(nb, per, d, per, d)
    return jnp.stack([g[:, j, :, j, :] for j in range(per)], axis=1).reshape(H, d, d)


def kernel(x, c, w_mod, b_mod, g_ffn1, w_ffn1_in, w_ffn1_out, g_mix, w_in, conv_w, conv_b, ln_g, ln_b, rnn_conv_w, rnn_conv_b, w_a, b_a, w_i, b_i, lru_lambda, w_out, g_ffn2, w_ffn2_in, w_ffn2_out, w_fmod, b_fmod, g_final, loss_target, m_w_mod, m_b_mod, m_g_ffn1, m_w_ffn1_in, m_w_ffn1_out, m_g_mix, m_w_in, m_conv_w, m_conv_b, m_ln_g, m_ln_b, m_rnn_conv_w, m_rnn_conv_b, m_w_a, m_b_a, m_w_i, m_b_i, m_lru_lambda, m_w_out, m_g_ffn2, m_w_ffn2_in, m_w_ffn2_out, m_w_fmod, m_b_fmod, m_g_final, v_w_mod, v_b_mod, v_g_ffn1, v_w_ffn1_in, v_w_ffn1_out, v_g_mix, v_w_in, v_conv_w, v_conv_b, v_ln_g, v_ln_b, v_rnn_conv_w, v_rnn_conv_b, v_w_a, v_b_a, v_w_i, v_b_i, v_lru_lambda, v_w_out, v_g_ffn2, v_w_ffn2_in, v_w_ffn2_out, v_w_fmod, v_b_fmod, v_g_final):
    S, D = x.shape[1], x.shape[2]
    M = conv_b.shape[1]
    H, HD = w_a.shape[1], w_a.shape[2]
    nb = M // LANES
    ix, iy, ic = lax.axis_index("x"), lax.axis_index("y"), lax.axis_index("c")
    chip = 2 * ix + iy
    dev = 2 * chip + ic
    cidx = jnp.reshape(ic, (1,)).astype(I32)
    chip_core = jnp.stack([chip, ic]).astype(I32)
    xs = x[0]
    tgt = loss_target[0]

    c_all = allgather_devices(_rows128(c), "gather_c")[0].reshape(N_DEV, D)
    mod_cols = cond_matmul(c_all, w_mod[0], "mod_proj")
    fmod_cols = cond_matmul(c_all, w_fmod, "fmod_proj")
    convw_pad = jnp.pad(conv_w[0], ((0, 32 - CONV_WIDTH), (0, 0)))
    rnnw_pad = jnp.pad(rnn_conv_w[0], ((0, SUBLANES - RNN_CONV_WIDTH), (0, 0)))
    n_mod, n_fmod = mod_cols.shape[1], fmod_cols.shape[1]
    small = jnp.concatenate([_rows128(mod_cols), _rows128(fmod_cols), convw_pad, rnnw_pad], axis=0)
    small4 = allgather_chips(small, "gather_cond")
    r0 = N_DEV * n_mod // LANES
    r1 = r0 + N_DEV * n_fmod // LANES
    mod_all = small4[:, :r0].reshape(N_CHIPS, N_DEV, n_mod)
    fmod_all = small4[:, r0:r1].reshape(N_CHIPS, N_DEV, n_fmod)
    convw4 = small4[:, r1:r1 + 32]
    rnnw4 = small4[:, r1 + 32:r1 + 32 + SUBLANES]
    mod_row = lax.dynamic_index_in_dim(mod_all, dev, axis=1, keepdims=False).reshape(1, N_CHIPS * n_mod) + b_mod
    fmod_row = lax.dynamic_index_in_dim(fmod_all, dev, axis=1, keepdims=False).reshape(1, N_CHIPS * n_fmod) + b_fmod[None, :]
    vecs = jnp.concatenate([mod_row.reshape(9, D), fmod_row.reshape(2, D), g_ffn1, g_mix, g_ffn2, g_final[None, :],
                            jnp.zeros((1, D), F32)], axis=0)
    lnv = jnp.concatenate([ln_g, ln_b, jnp.zeros((SUBLANES - 2, M), F32)], axis=0)
    bda = _block_diag(w_a[0]).astype(BF)
    bdi = _block_diag(w_i[0]).astype(BF)

    kinds = ["col", "row"]
    w_f1, w_mx, w_f2 = [w_ffn1_in[0], w_ffn1_out[0]], [w_in[0], w_out[0]], [w_ffn2_in[0], w_ffn2_out[0]]
    as_bf = lambda ws: [w.astype(BF) for w in ws]
    shapes_of = lambda ws: [w.shape for w in ws]
    wi1, wo1 = allgather_weights(as_bf(w_f1), kinds, "gather_ffn1")
    win, wout = allgather_weights(as_bf(w_mx), kinds, "gather_mix", seq_id=1)
    wi2, wo2 = allgather_weights(as_bf(w_f2), kinds, "gather_ffn2", seq_id=2)

    def reduce_add(gs, recv, ws, tag):
        pairs = [add_sibling_half(g, r_, k, w.shape, cidx, f"add_sibling_{tag}{j}")
                 for j, (g, r_, k, w) in enumerate(zip(gs, recv, kinds, ws))]
        return [p[0] for p in pairs], [p[1] for p in pairs]

    def reduce_sum(hs_, recv, ws, tag):
        return [sum_chip_pieces(h_, p_, k, w.shape, chip_core, f"sum_chips_{tag}{j}")
                for j, (h_, p_, k, w) in enumerate(zip(hs_, recv, kinds, ws))]

    rows1 = (R_SH1, R_SC1, R_GT1, R_G1)
    rows3 = (R_SH3, R_SC3, R_GT3, R_G3)
    x1, g1s, u1s, y1 = ffn_fwd(xs, vecs, wi1, wo1, rows1, "ffn1_fwd")
    proj = norm_matmul(x1, vecs, win, (R_SH2, R_SC2, R_G2), "mix_in_proj")
    cq = conv_fwd(proj, convw4, conv_b, "conv_fwd")
    xr, ra, ii, hh = rnn_fwd(proj, rnnw4, rnn_conv_b, bda, bdi, b_a, b_i, lru_lambda, "rnn_fwd")
    x2, ym, ycat = mix_out(cq, proj, hh, x1, vecs, lnv, wout, "mix_out")
    x3, g2s, u2s, y2 = ffn_fwd(x2, vecs, wi2, wo2, rows3, "ffn2_fwd")
    dx3, vgf = final_fwd_bwd(x3, tgt, vecs, "final_loss")

    Fd = wo1.shape[0]
    tk = min(512, S)
    dx2, act2, dg2, du2, h3b, dy2b, vg3 = ffn_bwd(dx3, x2, vecs, g2s, u2s, y2, wi2, wo2, rows3, "ffn2_bwd")
    gwo2 = matmul(act2, dy2b, "tn", tm=Fd // 2, tn=D, tk=tk, name="ffn2_dwo")
    gwi2 = matmul(h3b, dg2, "tn", tm=D, tn=Fd // 2, tk=tk, name="ffn2_dwg", out_cols=2 * Fd)
    gwi2 = matmul(h3b, du2, "tn", tm=D, tn=Fd // 2, tk=tk, name="ffn2_dwu", out_cols=2 * Fd, col_off=Fd, prev=gwi2)
    recv1_f2 = exchange_sibling_halves([gwi2, gwo2], kinds, shapes_of(w_f2), "reduce1_ffn2", seq_id=3)
    dcq, dhout, duy, dymb, vgd, vgm = mix_out_bwd(dx2, ym, vecs, wout, cq, lnv, proj, hh, "mix_out_bwd")
    gwout = matmul(ycat, dymb, "tn", tm=2 * M, tn=D, tk=tk, name="mix_dwout")
    duv, dug, dconvw4, dconvb = conv_bwd(dcq, proj, convw4, "conv_bwd")
    dux, dwa4, dwi4, drnnw4, rvec = rnn_bwd(dhout, hh, xr, ra, ii, proj, rnnw4, bda, bdi, lru_lambda, "rnn_bwd")
    dx1, h2b, dpb, vg2 = mix_in_bwd((duv, dug, dux, duy), x1, dx2, vecs, win, "mix_in_bwd")
    gwin = matmul(h2b, dpb, "tn", tm=D, tn=1024, tk=tk, name="mix_dwin")
    h_f2, hb_f2 = reduce_add([gwi2, gwo2], recv1_f2, w_f2, "ffn2_")
    recv2_f2 = exchange_chip_pieces(hb_f2, kinds, shapes_of(w_f2), "reduce2_ffn2", seq_id=4)
    recv1_mx = exchange_sibling_halves([gwin, gwout], kinds, shapes_of(w_mx), "reduce1_mix", seq_id=5)
    dx0, act1, dg1, du1, h1b, dy1b, vg1 = ffn_bwd(dx1, xs, vecs, g1s, u1s, y1, wi1, wo1, rows1, "ffn1_bwd")
    h_mx, hb_mx = reduce_add([gwin, gwout], recv1_mx, w_mx, "mix_")
    recv2_mx = exchange_chip_pieces(hb_mx, kinds, shapes_of(w_mx), "reduce2_mix", seq_id=6)
    gwo1 = matmul(act1, dy1b, "tn", tm=Fd // 2, tn=D, tk=tk, name="ffn1_dwo")
    gwi1 = matmul(h1b, dg1, "tn", tm=D, tn=Fd // 2, tk=tk, name="ffn1_dwg", out_cols=2 * Fd)
    gwi1 = matmul(h1b, du1, "tn", tm=D, tn=Fd // 2, tk=tk, name="ffn1_dwu", out_cols=2 * Fd, col_off=Fd, prev=gwi1)
    recv1_f1 = exchange_sibling_halves([gwi1, gwo1], kinds, shapes_of(w_f1), "reduce1_ffn1", seq_id=7)
    g_f2 = exchange_reduced_halves(reduce_sum(h_f2, recv2_f2, w_f2, "ffn2_"), "reduce3_ffn2")
    g_mx = exchange_reduced_halves(reduce_sum(h_mx, recv2_mx, w_mx, "mix_"), "reduce3_mix")

    dmod_row = jnp.concatenate([vg1[1:3], vg1[0:1], vg2[0:2], vgd[0:1], vg3[1:3], vg3[0:1]], axis=0)
    gains = jnp.concatenate([vg1[3:4], vg2[2:3], vg3[3:4], vgf[2:4]], axis=0)
    mvecs = jnp.concatenate([dconvb, vgm[0:2], rvec[0:4], jnp.zeros((1, M), F32)], axis=0)
    parts = [_rows128(dmod_row), _rows128(vgf[0:2]), _rows128(gains), _rows128(mvecs),
             _rows128(dconvw4), _rows128(drnnw4), _rows128(_diag_blocks(dwa4, H, HD)), _rows128(_diag_blocks(dwi4, H, HD))]
    sizes = [p.shape[0] for p in parts]
    packed = jnp.concatenate(parts, axis=0)
    gathered, summed = allgather_devices(packed, "reduce_small", with_sum=True)
    offs = [0]
    for s in sizes:
        offs.append(offs[-1] + s)
    seg = lambda k: summed[offs[k]:offs[k + 1]]
    g_b_mod = seg(0).reshape(1, 9 * D)
    g_b_fmod = seg(1).reshape(1, 2 * D)
    gsum = seg(2).reshape(5, D)
    loss = (0.5 / D) * jnp.sum(gsum[4])
    msum = seg(3).reshape(SUBLANES, M)
    g_conv_w = lax.dynamic_index_in_dim(seg(4).reshape(nb, 32, LANES), chip, axis=0, keepdims=False)[:CONV_WIDTH]
    g_rnn_w = lax.dynamic_index_in_dim(seg(5).reshape(nb, SUBLANES, LANES), chip, axis=0, keepdims=False)[:RNN_CONV_WIDTH]
    g_w_a = seg(6).reshape(H, HD, HD)
    g_w_i = seg(7).reshape(H, HD, HD)
    dmod_all = gathered[:, offs[0]:offs[1]].reshape(N_DEV, 9 * D)
    dfmod_all = gathered[:, offs[1]:offs[2]].reshape(N_DEV, 2 * D)
    dmod_cols = lax.dynamic_slice_in_dim(dmod_all, chip * n_mod, n_mod, axis=1)
    dfmod_cols = lax.dynamic_slice_in_dim(dfmod_all, chip * n_fmod, n_fmod, axis=1)

    h_f1, hb_f1 = reduce_add([gwi1, gwo1], recv1_f1, w_f1, "ffn1_")
    recv2_f1 = exchange_chip_pieces(hb_f1, kinds, shapes_of(w_f1), "reduce2_ffn1", seq_id=8)
    g_f1 = exchange_reduced_halves(reduce_sum(h_f1, recv2_f1, w_f1, "ffn1_"), "reduce3_ffn1")

    big_w = w_f1 + w_mx + w_f2
    g_big = g_f1 + g_mx + g_f2
    names = ["ffn1_in", "ffn1_out", "w_in", "w_out", "ffn2_in", "ffn2_out"]
    big_m = [m_w_ffn1_in[0], m_w_ffn1_out[0], m_w_in[0], m_w_out[0], m_w_ffn2_in[0], m_w_ffn2_out[0]]
    big_v = [v_w_ffn1_in[0], v_w_ffn1_out[0], v_w_in[0], v_w_out[0], v_w_ffn2_in[0], v_w_ffn2_out[0]]
    big_out = [adam_big(w, g, m, v, "adam_" + nm) for w, g, m, v, nm in zip(big_w, g_big, big_m, big_v, names)]
    g_w_mod, d_w_mod, nm_w_mod, nv_w_mod = adam_cond(c_all, dmod_cols, w_mod[0], m_w_mod[0], v_w_mod[0], "adam_w_mod")
    g_w_fmod, d_w_fmod, nm_w_fmod, nv_w_fmod = adam_cond(c_all, dfmod_cols, w_fmod, m_w_fmod, v_w_fmod, "adam_w_fmod")

    flat2 = lambda a: a.reshape(-1, a.shape[-1])
    small_names = ["b_mod", "g_ffn1", "g_mix", "conv_w", "conv_b", "ln_g", "ln_b", "rnn_conv_w", "rnn_conv_b", "w_a", "b_a",
                   "w_i", "b_i", "lru_lambda", "g_ffn2", "b_fmod", "g_final"]
    small_w = [b_mod, g_ffn1, g_mix, conv_w, conv_b, ln_g, ln_b, rnn_conv_w, rnn_conv_b, w_a, b_a, w_i, b_i, lru_lambda,
               g_ffn2, b_fmod, g_final]
    small_m = [m_b_mod, m_g_ffn1, m_g_mix, m_conv_w, m_conv_b, m_ln_g, m_ln_b, m_rnn_conv_w, m_rnn_conv_b, m_w_a, m_b_a,
               m_w_i, m_b_i, m_lru_lambda, m_g_ffn2, m_b_fmod, m_g_final]
    small_v = [v_b_mod, v_g_ffn1, v_g_mix, v_conv_w, v_conv_b, v_ln_g, v_ln_b, v_rnn_conv_w, v_rnn_conv_b, v_w_a, v_b_a,
               v_w_i, v_b_i, v_lru_lambda, v_g_ffn2, v_b_fmod, v_g_final]
    small_g = [g_b_mod, gsum[0:1], gsum[1:2], g_conv_w, msum[0:1], msum[1:2], msum[2:3], g_rnn_w, msum[3:4], g_w_a, msum[4:5],
               g_w_i, msum[5:6], msum[6:7], gsum[2:3], g_b_fmod, gsum[3:4]]
    small_g = [g.reshape(w.shape) for g, w in zip(small_g, small_w)]
    two_d = lambda a: a.reshape(1, -1) if a.ndim == 1 else flat2(a)
    sd, sm, sv = adam_small([two_d(a) for a in small_w], [two_d(a) for a in small_g], [two_d(a) for a in small_m],
                            [two_d(a) for a in small_v], "adam_small")
    small = {}
    for k, nm in enumerate(small_names):
        shp = small_w[k].shape
        small[nm] = (small_g[k], sd[k].reshape(shp), sm[k].reshape(shp), sv[k].reshape(shp))

    big = {"w_mod": tuple(a[None] for a in (g_w_mod, d_w_mod, nm_w_mod, nv_w_mod)),
           "w_fmod": (g_w_fmod, d_w_fmod, nm_w_fmod, nv_w_fmod)}
    for nm, full, g, (d, nmm, nvv) in zip(["w_ffn1_in", "w_ffn1_out", "w_in", "w_out", "w_ffn2_in", "w_ffn2_out"],
                                         big_w, g_big, big_out):
        big[nm] = tuple(a[None] for a in (g, d, nmm, nvv))
    order = ["w_mod", "b_mod", "g_ffn1", "w_ffn1_in", "w_ffn1_out", "g_mix", "w_in", "conv_w", "conv_b", "ln_g", "ln_b",
             "rnn_conv_w", "rnn_conv_b", "w_a", "b_a", "w_i", "b_i", "lru_lambda", "w_out", "g_ffn2", "w_ffn2_in",
             "w_ffn2_out", "w_fmod", "b_fmod", "g_final"]
    table = {**small, **big}
    outs = [loss, dx0[None]]
    for kind_ in range(4):
        outs.extend(table[nm][kind_] for nm in order)
    return tuple(outs)
```

```python
import functools

import jax
import jax.numpy as jnp
from jax import lax
from jax.experimental import pallas as pl
from jax.experimental.pallas import tpu as pltpu
from jax.experimental.pallas import tpu_sc as plsc

F32 = jnp.float32
BF = jnp.bfloat16
I32 = jnp.int32
MESH = pl.DeviceIdType.MESH

EPS = 1e-6
RG_C = 8.0
MACARON_W = 0.5
CONV_WIDTH = 31
RNN_CONV_WIDTH = 4
ADAM_LR = 0.001
ADAM_B1 = 0.9
ADAM_B2 = 0.999
ADAM_EPS = 1e-08
ADAM_WD = 0.01
ADAM_STEP = 10

LANES = 128
SUBLANES = 8
VMEM_LIMIT = 48 * 1024 * 1024
N_CHIPS = 4
N_DEV = 8

R_SH1, R_SC1, R_GT1, R_SH2, R_SC2, R_GT2, R_SH3, R_SC3, R_GT3, R_FSH, R_FSC, R_G1, R_G2, R_G3, R_GF = range(15)

CONTRACT_LAST = (((1,), (1,)), ((), ()))
CONTRACT_FIRST = (((0,), (0,)), ((), ()))


def _pcall(body, **kw):
    return pl.pallas_call(body, **kw)


def _params(sem=None):
    if sem is None:
        return pltpu.CompilerParams(vmem_limit_bytes=VMEM_LIMIT)
    return pltpu.CompilerParams(dimension_semantics=sem, vmem_limit_bytes=VMEM_LIMIT)


def _row(ref, r):
    return ref[r:r + 1, :]


def _sigmoid(x):
    return 1.0 / (1.0 + jnp.exp(-x))


def _colsum(x):
    return jnp.sum(x, axis=0, keepdims=True)


def _rowmean(x):
    return jnp.mean(x, axis=-1, keepdims=True)


def matmul(a, b, mode, *, tm, tn, tk, name, out_dtype=F32, out_cols=None, col_off=0, prev=None):
    if mode == "nn":
        (M, K), (K2, N) = a.shape, b.shape
    elif mode == "nt":
        (M, K), (N, K2) = a.shape, b.shape
    else:
        (K, M), (K2, N) = a.shape, b.shape
    assert K == K2 and M % tm == 0 and N % tn == 0 and K % tk == 0 and col_off % tn == 0
    nk = K // tk
    out_cols = N if out_cols is None else out_cols
    off = col_off // tn

    def body(*refs):
        if prev is None:
            a_ref, b_ref, o_ref, acc = refs
        else:
            a_ref, b_ref, _, o_ref, acc = refs
        k = pl.program_id(2)

        @pl.when(k == 0)
        def _():
            acc[...] = jnp.zeros_like(acc)

        av = a_ref[...].astype(BF)
        bv = b_ref[...].astype(BF)
        if mode == "nn":
            acc[...] += jnp.dot(av, bv, preferred_element_type=F32)
        elif mode == "nt":
            acc[...] += lax.dot_general(av, bv, CONTRACT_LAST, preferred_element_type=F32)
        else:
            acc[...] += lax.dot_general(av, bv, CONTRACT_FIRST, preferred_element_type=F32)

        @pl.when(k == nk - 1)
        def _():
            o_ref[...] = acc[...].astype(out_dtype)

    if mode == "nn":
        a_spec = pl.BlockSpec((tm, tk), lambda m, n, k: (m, k))
        b_spec = pl.BlockSpec((tk, tn), lambda m, n, k: (k, n))
    elif mode == "nt":
        a_spec = pl.BlockSpec((tm, tk), lambda m, n, k: (m, k))
        b_spec = pl.BlockSpec((tn, tk), lambda m, n, k: (n, k))
    else:
        a_spec = pl.BlockSpec((tk, tm), lambda m, n, k: (k, m))
        b_spec = pl.BlockSpec((tk, tn), lambda m, n, k: (k, n))
    in_specs = [a_spec, b_spec]
    args = [a, b]
    aliases = {}
    if prev is not None:
        in_specs.append(pl.BlockSpec(memory_space=pl.ANY))
        args.append(prev)
        aliases = {2: 0}
    return _pcall(
        body, name=name, grid=(M // tm, N // tn, nk), in_specs=in_specs,
        out_specs=pl.BlockSpec((tm, tn), lambda m, n, k: (m, n + off)),
        out_shape=jax.ShapeDtypeStruct((M, out_cols), out_dtype),
        scratch_shapes=[pltpu.VMEM((tm, tn), F32)], input_output_aliases=aliases,
        compiler_params=_params(("parallel", "parallel", "arbitrary")),
    )(*args)


def cond_matmul(c_all, w, name):
    B, K = c_all.shape
    N = w.shape[1]
    tn = 256
    assert N % tn == 0

    def body(c_ref, w_ref, o_ref):
        cv = c_ref[...]
        ca = cv * _sigmoid(cv)
        o_ref[...] = jnp.dot(ca, w_ref[...], preferred_element_type=F32, precision=lax.Precision.HIGHEST)

    return _pcall(
        body, name=name, grid=(N // tn,),
        in_specs=[pl.BlockSpec((B, K), lambda n: (0, 0)), pl.BlockSpec((K, tn), lambda n: (0, n))],
        out_specs=pl.BlockSpec((B, tn), lambda n: (0, n)),
        out_shape=jax.ShapeDtypeStruct((B, N), F32), compiler_params=_params(("parallel",)),
    )(c_all, w)


FFN_TS = 512
FFN_CH = 256


def ffn_fwd(x, vecs, wi, wo, rows, name):
    r_sh, r_sc, r_gt, r_g = rows
    S, D = x.shape
    Fd = wo.shape[0]
    ts, ch = min(FFN_TS, S), FFN_CH
    ni, nj = S // ts, Fd // ch

    def body(x_ref, v_ref, wg_ref, wu_ref, wo_ref, xo_ref, g_ref, u_ref, y_ref, h_sc, acc):
        j = pl.program_id(1)

        @pl.when(j == 0)
        def _():
            xv = x_ref[...]
            r = lax.rsqrt(_rowmean(xv * xv) + EPS)
            gs = _row(v_ref, r_g) * (1.0 + _row(v_ref, r_sc))
            h_sc[...] = (xv * r * gs + _row(v_ref, r_sh)).astype(BF)
            acc[...] = jnp.zeros_like(acc)

        hb = h_sc[...]
        G = jnp.dot(hb, wg_ref[...], preferred_element_type=F32)
        U = jnp.dot(hb, wu_ref[...], preferred_element_type=F32)
        g_ref[...] = G.astype(BF)
        u_ref[...] = U.astype(BF)
        act = (G * _sigmoid(G) * U).astype(BF)
        acc[...] += jnp.dot(act, wo_ref[...], preferred_element_type=F32)

        @pl.when(j == nj - 1)
        def _():
            Y = acc[...]
            y_ref[...] = Y
            xo_ref[...] = x_ref[...] + (MACARON_W * _row(v_ref, r_gt)) * Y

    tok = pl.BlockSpec((ts, D), lambda i, j: (i, 0))
    hid = pl.BlockSpec((ts, ch), lambda i, j: (i, j))
    return _pcall(
        body, name=name, grid=(ni, nj),
        in_specs=[tok, pl.BlockSpec(vecs.shape, lambda i, j: (0, 0)),
                  pl.BlockSpec((D, ch), lambda i, j: (0, j)), pl.BlockSpec((D, ch), lambda i, j: (0, j + nj)),
                  pl.BlockSpec((ch, D), lambda i, j: (j, 0))],
        out_specs=[tok, hid, hid, tok],
        out_shape=[jax.ShapeDtypeStruct((S, D), F32), jax.ShapeDtypeStruct((S, Fd), BF),
                   jax.ShapeDtypeStruct((S, Fd), BF), jax.ShapeDtypeStruct((S, D), F32)],
        scratch_shapes=[pltpu.VMEM((ts, D), BF), pltpu.VMEM((ts, D), F32)],
        compiler_params=_params(("arbitrary", "arbitrary")),
    )(x, vecs, wi, wi, wo)


def ffn_bwd(dxo, x, vecs, gs_, us_, y, wi, wo, rows, name):
    r_sh, r_sc, r_gt, r_g = rows
    S, D = x.shape
    Fd = wo.shape[0]
    ts, ch = min(FFN_TS, S), FFN_CH
    ni, nj = S // ts, Fd // ch

    def body(dxo_ref, x_ref, v_ref, g_ref, u_ref, y_ref, wg_ref, wu_ref, wo_ref,
             dx_ref, act_ref, dg_ref, du_ref, hb_ref, dyb_ref, vg_ref, dyb_sc, dh_sc):
        i, j = pl.program_id(0), pl.program_id(1)

        @pl.when((i == 0) & (j == 0))
        def _():
            vg_ref[...] = jnp.zeros_like(vg_ref)

        @pl.when(j == 0)
        def _():
            dxo_v = dxo_ref[...]
            dyb = ((MACARON_W * _row(v_ref, r_gt)) * dxo_v).astype(BF)
            dyb_sc[...] = dyb
            dyb_ref[...] = dyb
            vg_ref[0:1, :] += MACARON_W * _colsum(dxo_v * y_ref[...])
            dh_sc[...] = jnp.zeros_like(dh_sc)

        dA = lax.dot_general(dyb_sc[...], wo_ref[...], CONTRACT_LAST, preferred_element_type=F32)
        G = g_ref[...].astype(F32)
        U = u_ref[...].astype(F32)
        sg = _sigmoid(G)
        sl = G * sg
        dU = (dA * sl).astype(BF)
        dG = (dA * U * (sg * (1.0 + G * (1.0 - sg)))).astype(BF)
        act_ref[...] = (sl * U).astype(BF)
        dg_ref[...] = dG
        du_ref[...] = dU
        dh_sc[...] += (lax.dot_general(dG, wg_ref[...], CONTRACT_LAST, preferred_element_type=F32)
                       + lax.dot_general(dU, wu_ref[...], CONTRACT_LAST, preferred_element_type=F32))

        @pl.when(j == nj - 1)
        def _():
            dh = dh_sc[...]
            xv = x_ref[...]
            r = lax.rsqrt(_rowmean(xv * xv) + EPS)
            n = xv * r
            g = _row(v_ref, r_g)
            sc1 = 1.0 + _row(v_ref, r_sc)
            gsc = g * sc1
            hb_ref[...] = (n * gsc + _row(v_ref, r_sh)).astype(BF)
            dhn = dh * n
            vg_ref[1:2, :] += _colsum(dh)
            vg_ref[2:3, :] += _colsum(dhn) * g
            vg_ref[3:4, :] += _colsum(dhn) * sc1
            dn = dh * gsc
            dx_ref[...] = dxo_ref[...] + r * (dn - n * _rowmean(dn * n))

    tok = pl.BlockSpec((ts, D), lambda i, j: (i, 0))
    hid = pl.BlockSpec((ts, ch), lambda i, j: (i, j))
    return _pcall(
        body, name=name, grid=(ni, nj),
        in_specs=[tok, tok, pl.BlockSpec(vecs.shape, lambda i, j: (0, 0)), hid, hid, tok,
                  pl.BlockSpec((D, ch), lambda i, j: (0, j)), pl.BlockSpec((D, ch), lambda i, j: (0, j + nj)),
                  pl.BlockSpec((ch, D), lambda i, j: (j, 0))],
        out_specs=[tok, hid, hid, hid, tok, tok, pl.BlockSpec((SUBLANES, D), lambda i, j: (0, 0))],
        out_shape=[jax.ShapeDtypeStruct((S, D), F32), jax.ShapeDtypeStruct((S, Fd), BF),
                   jax.ShapeDtypeStruct((S, Fd), BF), jax.ShapeDtypeStruct((S, Fd), BF),
                   jax.ShapeDtypeStruct((S, D), BF), jax.ShapeDtypeStruct((S, D), BF),
                   jax.ShapeDtypeStruct((SUBLANES, D), F32)],
        scratch_shapes=[pltpu.VMEM((ts, D), BF), pltpu.VMEM((ts, D), F32)],
        compiler_params=_params(("arbitrary", "arbitrary")),
    )(dxo, x, vecs, gs_, us_, y, wi, wi, wo)


def final_fwd_bwd(x, tgt, vecs, name):
    S, D = x.shape
    ts = min(512, S)

    def body(x_ref, t_ref, v_ref, dx_ref, vg_ref):
        @pl.when(pl.program_id(0) == 0)
        def _():
            vg_ref[...] = jnp.zeros_like(vg_ref)

        xv = x_ref[...]
        r = lax.rsqrt(_rowmean(xv * xv) + EPS)
        n = xv * r
        g = _row(v_ref, R_GF)
        sc1 = 1.0 + _row(v_ref, R_FSC)
        gsc = g * sc1
        e = n * gsc + _row(v_ref, R_FSH) - t_ref[...]
        vg_ref[3:4, :] += _colsum(e * e)
        dout = e * (1.0 / D)
        dn_ = dout * n
        vg_ref[0:1, :] += _colsum(dout)
        vg_ref[1:2, :] += _colsum(dn_) * g
        vg_ref[2:3, :] += _colsum(dn_) * sc1
        dn = dout * gsc
        dx_ref[...] = r * (dn - n * _rowmean(dn * n))

    tok = pl.BlockSpec((ts, D), lambda i: (i, 0))
    return _pcall(
        body, name=name, grid=(S // ts,),
        in_specs=[tok, tok, pl.BlockSpec(vecs.shape, lambda i: (0, 0))],
        out_specs=[tok, pl.BlockSpec((SUBLANES, D), lambda i: (0, 0))],
        out_shape=[jax.ShapeDtypeStruct((S, D), F32), jax.ShapeDtypeStruct((SUBLANES, D), F32)],
        compiler_params=_params(("arbitrary",)),
    )(x, tgt, vecs)


def norm_matmul(x, vecs, w, rows, name):
    r_sh, r_sc, r_g = rows
    S, D = x.shape
    N = w.shape[1]
    ts, tn = min(512, S), 512
    nn = N // tn

    def body(x_ref, v_ref, w_ref, o_ref, h_sc):
        @pl.when(pl.program_id(1) == 0)
        def _():
            xv = x_ref[...]
            r = lax.rsqrt(_rowmean(xv * xv) + EPS)
            gs = _row(v_ref, r_g) * (1.0 + _row(v_ref, r_sc))
            h_sc[...] = (xv * r * gs + _row(v_ref, r_sh)).astype(BF)

        o_ref[...] = jnp.dot(h_sc[...], w_ref[...], preferred_element_type=F32)

    return _pcall(
        body, name=name, grid=(S // ts, nn),
        in_specs=[pl.BlockSpec((ts, D), lambda i, n: (i, 0)), pl.BlockSpec(vecs.shape, lambda i, n: (0, 0)),
                  pl.BlockSpec((D, tn), lambda i, n: (0, n))],
        out_specs=pl.BlockSpec((ts, tn), lambda i, n: (i, n)),
        out_shape=jax.ShapeDtypeStruct((S, N), F32),
        scratch_shapes=[pltpu.VMEM((ts, D), BF)],
        compiler_params=_params(("arbitrary", "arbitrary")),
    )(x, vecs, w)


SEQ_TT = 256
CONV_PAD = 32


def conv_fwd(proj, convw4, conv_b, name):
    S = proj.shape[0]
    M = conv_b.shape[1]
    nb = M // LANES
    tt = min(SEQ_TT, S)

    def body(uv_ref, ug_ref, w_ref, b_ref, cq_ref, qp):
        qp[0:CONV_PAD, :] = jnp.zeros((CONV_PAD, LANES), F32)

        def step(t, carry):
            base = pl.multiple_of(t * tt, tt)
            qp[pl.ds(base + CONV_PAD, tt), :] = uv_ref[pl.ds(base, tt), :] * _sigmoid(ug_ref[pl.ds(base, tt), :])
            acc = jnp.broadcast_to(b_ref[...], (tt, LANES))
            for k in range(CONV_WIDTH):
                acc = acc + w_ref[k:k + 1, :] * qp[pl.ds(base + (CONV_PAD - CONV_WIDTH + 1) + k, tt), :]
            cq_ref[pl.ds(base, tt), :] = acc
            return carry

        lax.fori_loop(0, S // tt, step, 0)

    return _pcall(
        body, name=name, grid=(nb,),
        in_specs=[pl.BlockSpec((S, LANES), lambda c: (0, c)), pl.BlockSpec((S, LANES), lambda c: (0, c + nb)),
                  pl.BlockSpec((None, 32, LANES), lambda c: (c, 0, 0)), pl.BlockSpec((1, LANES), lambda c: (0, c))],
        out_specs=pl.BlockSpec((S, LANES), lambda c: (0, c)),
        out_shape=jax.ShapeDtypeStruct((S, M), F32),
        scratch_shapes=[pltpu.VMEM((S + CONV_PAD, LANES), F32)],
        compiler_params=_params(("arbitrary",)),
    )(proj, proj, convw4, conv_b)


def conv_bwd(dcq, proj, convw4, name):
    S, M = dcq.shape
    nb = M // LANES
    tt = min(SEQ_TT, S)
    off = CONV_PAD - CONV_WIDTH + 1

    def body(dcq_ref, uv_ref, ug_ref, w_ref, duv_ref, dug_ref, dw_ref, db_ref, qp, dp, dw8, db8):
        qp[0:CONV_PAD, :] = jnp.zeros((CONV_PAD, LANES), F32)
        dp[S:S + CONV_PAD, :] = jnp.zeros((CONV_PAD, LANES), F32)
        dw8[...] = jnp.zeros_like(dw8)
        db8[...] = jnp.zeros_like(db8)

        def fill(t, carry):
            base = pl.multiple_of(t * tt, tt)
            qp[pl.ds(base + CONV_PAD, tt), :] = uv_ref[pl.ds(base, tt), :] * _sigmoid(ug_ref[pl.ds(base, tt), :])
            dp[pl.ds(base, tt), :] = dcq_ref[pl.ds(base, tt), :]
            return carry

        lax.fori_loop(0, S // tt, fill, 0)

        def step(t, carry):
            base = pl.multiple_of(t * tt, tt)
            d_t = dcq_ref[pl.ds(base, tt), :]
            db8[...] += d_t.reshape(tt // SUBLANES, SUBLANES, LANES).sum(axis=0)
            dq = jnp.zeros((tt, LANES), F32)
            for k in range(CONV_WIDTH):
                prod = d_t * qp[pl.ds(base + off + k, tt), :]
                dw8[k] += prod.reshape(tt // SUBLANES, SUBLANES, LANES).sum(axis=0)
                dq = dq + w_ref[k:k + 1, :] * dp[pl.ds(base + (CONV_WIDTH - 1) - k, tt), :]
            uv = uv_ref[pl.ds(base, tt), :]
            sg = _sigmoid(ug_ref[pl.ds(base, tt), :])
            duv_ref[pl.ds(base, tt), :] = dq * sg
            dug_ref[pl.ds(base, tt), :] = dq * uv * sg * (1.0 - sg)
            return carry

        lax.fori_loop(0, S // tt, step, 0)
        dw_ref[...] = jnp.zeros_like(dw_ref)
        for k in range(CONV_WIDTH):
            dw_ref[k:k + 1, :] = _colsum(dw8[k])
        db_ref[...] = _colsum(db8[...])

    col = lambda o: pl.BlockSpec((S, LANES), lambda c: (0, c + o))
    return _pcall(
        body, name=name, grid=(nb,),
        in_specs=[col(0), col(0), col(nb), pl.BlockSpec((None, 32, LANES), lambda c: (c, 0, 0))],
        out_specs=[col(0), col(0), pl.BlockSpec((None, 32, LANES), lambda c: (c, 0, 0)),
                   pl.BlockSpec((1, LANES), lambda c: (0, c))],
        out_shape=[jax.ShapeDtypeStruct((S, M), F32), jax.ShapeDtypeStruct((S, M), F32),
                   jax.ShapeDtypeStruct((nb, 32, LANES), F32), jax.ShapeDtypeStruct((1, M), F32)],
        scratch_shapes=[pltpu.VMEM((S + CONV_PAD, LANES), F32), pltpu.VMEM((S + CONV_PAD, LANES), F32),
                        pltpu.VMEM((32, SUBLANES, LANES), F32), pltpu.VMEM((SUBLANES, LANES), F32)],
        compiler_params=_params(("arbitrary",)),
    )(dcq, proj, proj, convw4)


def _log_sigmoid(x):
    return jnp.minimum(x, 0.0) - jnp.log(1.0 + jnp.exp(-jnp.abs(x)))


def _rg_gate_terms(ra, ls):
    la = RG_C * ra * ls
    a = jnp.exp(la)
    th = jnp.tanh(la)
    mult = jnp.sqrt(-2.0 * th / (1.0 - th))
    return a, mult


def rnn_fwd(proj, rnnw4, rnn_b, bda, bdi, b_a, b_i, lam, name):
    S = proj.shape[0]
    M = rnn_b.shape[1]
    nb = M // LANES
    tt = min(SEQ_TT, S)
    KW = RNN_CONV_WIDTH

    def body(ux_ref, w_ref, rb_ref, bda_ref, bdi_ref, ba_ref, bi_ref, lam_ref,
             xr_ref, ra_ref, ii_ref, h_ref, uxp, a_sc, b_sc):
        uxp[0:SUBLANES, :] = jnp.zeros((SUBLANES, LANES), F32)
        ls = _log_sigmoid(lam_ref[...])

        def step(t, carry):
            base = pl.multiple_of(t * tt, tt)
            uxp[pl.ds(base + SUBLANES, tt), :] = ux_ref[pl.ds(base, tt), :]
            xr = jnp.broadcast_to(rb_ref[...], (tt, LANES))
            for k in range(KW):
                xr = xr + w_ref[k:k + 1, :] * uxp[pl.ds(base + (SUBLANES - KW + 1) + k, tt), :]
            xb = xr.astype(BF)
            ra = _sigmoid(jnp.dot(xb, bda_ref[...], preferred_element_type=F32) + ba_ref[...])
            ii = _sigmoid(jnp.dot(xb, bdi_ref[...], preferred_element_type=F32) + bi_ref[...])
            a, mult = _rg_gate_terms(ra, ls)
            xr_ref[pl.ds(base, tt), :] = xr
            ra_ref[pl.ds(base, tt), :] = ra
            ii_ref[pl.ds(base, tt), :] = ii
            a_sc[pl.ds(base, tt), :] = a
            b_sc[pl.ds(base, tt), :] = mult * (ii * xr)
            return carry

        lax.fori_loop(0, S // tt, step, 0)

        rows = lax.broadcasted_iota(I32, (SUBLANES, LANES), 0)

        def scan(t, hprev):
            base = pl.multiple_of(t * SUBLANES, SUBLANES)
            A = a_sc[pl.ds(base, SUBLANES), :]
            B = b_sc[pl.ds(base, SUBLANES), :]
            for d in (1, 2, 4):
                As = jnp.where(rows >= d, pltpu.roll(A, d, axis=0), 1.0)
                Bs = jnp.where(rows >= d, pltpu.roll(B, d, axis=0), 0.0)
                B = A * Bs + B
                A = A * As
            hh = B + A * hprev
            h_ref[pl.ds(base, SUBLANES), :] = hh
            return jnp.broadcast_to(hh[SUBLANES - 1:SUBLANES, :], (SUBLANES, LANES))

        lax.fori_loop(0, S // SUBLANES, scan, jnp.zeros((SUBLANES, LANES), F32))

    col = lambda o: pl.BlockSpec((S, LANES), lambda c: (0, c + o))
    vec = pl.BlockSpec((1, LANES), lambda c: (0, c))
    diag = pl.BlockSpec((LANES, LANES), lambda c: (c, c))
    return _pcall(
        body, name=name, grid=(nb,),
        in_specs=[col(2 * nb), pl.BlockSpec((None, SUBLANES, LANES), lambda c: (c, 0, 0)), vec, diag, diag, vec, vec, vec],
        out_specs=[col(0)] * 4,
        out_shape=[jax.ShapeDtypeStruct((S, M), F32)] * 4,
        scratch_shapes=[pltpu.VMEM((S + SUBLANES, LANES), F32), pltpu.VMEM((S, LANES), F32), pltpu.VMEM((S, LANES), F32)],
        compiler_params=_params(("arbitrary",)),
    )(proj, rnnw4, rnn_b, bda, bdi, b_a, b_i, lam)


def rnn_bwd(dhout, h, xr, ra, ii, proj, rnnw4, bda, bdi, lam, name):
    S, M = h.shape
    nb = M // LANES
    tt = min(SEQ_TT, S)
    KW = RNN_CONV_WIDTH
    SL = SUBLANES

    def body(dh_ref, h_ref, xr_ref, ra_ref, ii_ref, ux_ref, w_ref, bda_ref, bdi_ref, lam_ref,
             dux_ref, dwa_ref, dwi_ref, drw_ref, vec_ref,
             a_sc, hp, g_sc, dpa_sc, dpi_sc, dxp, uxp, acc8, drw8):
        zero8 = jnp.zeros((SL, LANES), F32)
        a_sc[S:S + SL, :] = zero8
        hp[0:SL, :] = zero8
        dxp[S:S + SL, :] = zero8
        uxp[0:SL, :] = zero8
        acc8[...] = jnp.zeros_like(acc8)
        drw8[...] = jnp.zeros_like(drw8)
        lamv = lam_ref[...]
        ls = _log_sigmoid(lamv)

        def fill(t, carry):
            base = pl.multiple_of(t * tt, tt)
            a_sc[pl.ds(base, tt), :] = jnp.exp(RG_C * ra_ref[pl.ds(base, tt), :] * ls)
            hp[pl.ds(base + SL, tt), :] = h_ref[pl.ds(base, tt), :]
            uxp[pl.ds(base + SL, tt), :] = ux_ref[pl.ds(base, tt), :]
            return carry

        lax.fori_loop(0, S // tt, fill, 0)

        rows = lax.broadcasted_iota(I32, (SL, LANES), 0)
        nt8 = S // SL

        def rscan(t, gnext):
            base = pl.multiple_of((nt8 - 1 - t) * SL, SL)
            A = a_sc[pl.ds(base + 1, SL), :]
            B = dh_ref[pl.ds(base, SL), :]
            for d in (1, 2, 4):
                As = jnp.where(rows < SL - d, pltpu.roll(A, SL - d, axis=0), 1.0)
                Bs = jnp.where(rows < SL - d, pltpu.roll(B, SL - d, axis=0), 0.0)
                B = A * Bs + B
                A = A * As
            g = B + A * gnext
            g_sc[pl.ds(base, SL), :] = g
            return jnp.broadcast_to(g[0:1, :], (SL, LANES))

        lax.fori_loop(0, nt8, rscan, zero8)

        def red8(v):
            return v.reshape(tt // SL, SL, LANES).sum(axis=0)

        def step(t, carry):
            base = pl.multiple_of(t * tt, tt)
            g = g_sc[pl.ds(base, tt), :]
            hprev = hp[pl.ds(base + SL - 1, tt), :]
            xr_t = xr_ref[pl.ds(base, tt), :]
            ra_t = ra_ref[pl.ds(base, tt), :]
            ii_t = ii_ref[pl.ds(base, tt), :]
            a, mult = _rg_gate_terms(ra_t, ls)
            gx = g * xr_t
            dmult = gx * ii_t
            dii = gx * mult
            dxr = g * (mult * ii_t)
            dla = g * hprev * a - dmult * (a * a) / mult
            acc8[3] += red8(dla * ra_t)
            dpa = dla * (RG_C * ls) * ra_t * (1.0 - ra_t)
            dpi = dii * ii_t * (1.0 - ii_t)
            dpab = dpa.astype(BF)
            dpib = dpi.astype(BF)
            dxr = dxr + (lax.dot_general(dpab, bda_ref[...], CONTRACT_LAST, preferred_element_type=F32)
                         + lax.dot_general(dpib, bdi_ref[...], CONTRACT_LAST, preferred_element_type=F32))
            dpa_sc[pl.ds(base, tt), :] = dpab
            dpi_sc[pl.ds(base, tt), :] = dpib
            dxp[pl.ds(base, tt), :] = dxr
            acc8[0] += red8(dxr)
            acc8[1] += red8(dpa)
            acc8[2] += red8(dpi)
            return carry

        lax.fori_loop(0, S // tt, step, 0)

        def convb(t, carry):
            base = pl.multiple_of(t * tt, tt)
            d_t = dxp[pl.ds(base, tt), :]
            dux = jnp.zeros((tt, LANES), F32)
            for k in range(KW):
                drw8[k] += red8(d_t * uxp[pl.ds(base + (SL - KW + 1) + k, tt), :])
                dux = dux + w_ref[k:k + 1, :] * dxp[pl.ds(base + (KW - 1) - k, tt), :]
            dux_ref[pl.ds(base, tt), :] = dux
            return carry

        lax.fori_loop(0, S // tt, convb, 0)

        xb = xr_ref[...].astype(BF)
        dwa_ref[...] = lax.dot_general(xb, dpa_sc[...], CONTRACT_FIRST, preferred_element_type=F32)
        dwi_ref[...] = lax.dot_general(xb, dpi_sc[...], CONTRACT_FIRST, preferred_element_type=F32)
        drw_ref[...] = jnp.zeros_like(drw_ref)
        vec_ref[...] = jnp.zeros_like(vec_ref)
        for k in range(KW):
            drw_ref[k:k + 1, :] = _colsum(drw8[k])
        for k in range(3):
            vec_ref[k:k + 1, :] = _colsum(acc8[k])
        vec_ref[3:4, :] = _colsum(acc8[3]) * (RG_C * _sigmoid(-lamv))

    col = lambda o: pl.BlockSpec((S, LANES), lambda c: (0, c + o))
    vec = pl.BlockSpec((1, LANES), lambda c: (0, c))
    diag = pl.BlockSpec((LANES, LANES), lambda c: (c, c))
    blk3 = lambda r: pl.BlockSpec((None, r, LANES), lambda c: (c, 0, 0))
    return _pcall(
        body, name=name, grid=(nb,),
        in_specs=[col(0), col(0), col(0), col(0), col(0), col(2 * nb), blk3(SL), diag, diag, vec],
        out_specs=[col(0), blk3(LANES), blk3(LANES), blk3(SL), pl.BlockSpec((SL, LANES), lambda c: (0, c))],
        out_shape=[jax.ShapeDtypeStruct((S, M), F32), jax.ShapeDtypeStruct((nb, LANES, LANES), F32),
                   jax.ShapeDtypeStruct((nb, LANES, LANES), F32), jax.ShapeDtypeStruct((nb, SL, LANES), F32),
                   jax.ShapeDtypeStruct((SL, M), F32)],
        scratch_shapes=[pltpu.VMEM((S + SL, LANES), F32), pltpu.VMEM((S + SL, LANES), F32), pltpu.VMEM((S, LANES), F32),
                        pltpu.VMEM((S, LANES), BF), pltpu.VMEM((S, LANES), BF), pltpu.VMEM((S + SL, LANES), F32),
                        pltpu.VMEM((S + SL, LANES), F32), pltpu.VMEM((SL, SL, LANES), F32), pltpu.VMEM((SL, SL, LANES), F32)],
        compiler_params=_params(("arbitrary",)),
    )(dhout, h, xr, ra, ii, proj, rnnw4, bda, bdi, lam)


GELU_K = 0.7978845608028654
GELU_C = 0.044715


def _layernorm_parts(cq):
    mu = _rowmean(cq)
    d = cq - mu
    rstd = lax.rsqrt(_rowmean(d * d) + EPS)
    return d * rstd, rstd


def mix_out(cq, proj, h, x, vecs, lnv, wout, name):
    S, D = x.shape
    M = cq.shape[1]
    ts = min(512, S)

    def body(cq_ref, uy_ref, h_ref, x_ref, v_ref, ln_ref, w_ref, xo_ref, ym_ref, yc_ref):
        z, _ = _layernorm_parts(cq_ref[...])
        l = z * _row(ln_ref, 0) + _row(ln_ref, 1)
        yc_ref[:, 0:M] = (l * _sigmoid(l)).astype(BF)
        uy = uy_ref[...]
        gelu = 0.5 * uy * (1.0 + jnp.tanh(GELU_K * (uy + GELU_C * uy * uy * uy)))
        yc_ref[:, M:2 * M] = (gelu * h_ref[...]).astype(BF)
        ym = jnp.dot(yc_ref[...], w_ref[...], preferred_element_type=F32)
        ym_ref[...] = ym
        xo_ref[...] = x_ref[...] + _row(v_ref, R_GT2) * ym

    tok = pl.BlockSpec((ts, D), lambda i: (i, 0))
    mtok = lambda o: pl.BlockSpec((ts, M), lambda i: (i, o))
    return _pcall(
        body, name=name, grid=(S // ts,),
        in_specs=[mtok(0), mtok(3), mtok(0), tok, pl.BlockSpec(vecs.shape, lambda i: (0, 0)),
                  pl.BlockSpec(lnv.shape, lambda i: (0, 0)), pl.BlockSpec(wout.shape, lambda i: (0, 0))],
        out_specs=[tok, tok, pl.BlockSpec((ts, 2 * M), lambda i: (i, 0))],
        out_shape=[jax.ShapeDtypeStruct((S, D), F32), jax.ShapeDtypeStruct((S, D), F32),
                   jax.ShapeDtypeStruct((S, 2 * M), BF)],
        compiler_params=_params(("arbitrary",)),
    )(cq, proj, h, x, vecs, lnv, wout)


def mix_out_bwd(dxo, ym, vecs, wout, cq, lnv, proj, h, name):
    S, D = dxo.shape
    M = cq.shape[1]
    ts = min(512, S)

    def body(dxo_ref, ym_ref, v_ref, w_ref, cq_ref, ln_ref, uy_ref, h_ref,
             dcq_ref, dh_ref, duy_ref, dyb_ref, vgd_ref, vgm_ref):
        @pl.when(pl.program_id(0) == 0)
        def _():
            vgd_ref[...] = jnp.zeros_like(vgd_ref)
            vgm_ref[...] = jnp.zeros_like(vgm_ref)

        dxo_v = dxo_ref[...]
        dyb = (_row(v_ref, R_GT2) * dxo_v).astype(BF)
        dyb_ref[...] = dyb
        vgd_ref[0:1, :] += _colsum(dxo_v * ym_ref[...])
        dycat = lax.dot_general(dyb, w_ref[...], CONTRACT_LAST, preferred_element_type=F32)
        dyc = dycat[:, 0:M]
        dyr = dycat[:, M:2 * M]
        z, rstd = _layernorm_parts(cq_ref[...])
        lng = _row(ln_ref, 0)
        l = z * lng + _row(ln_ref, 1)
        sl = _sigmoid(l)
        dl = dyc * (sl * (1.0 + l * (1.0 - sl)))
        vgm_ref[0:1, :] += _colsum(dl * z)
        vgm_ref[1:2, :] += _colsum(dl)
        dz = dl * lng
        dcq_ref[...] = rstd * (dz - _rowmean(dz) - z * _rowmean(dz * z))
        uy = uy_ref[...]
        u2 = uy * uy
        th = jnp.tanh(GELU_K * (uy + GELU_C * uy * u2))
        gelu = 0.5 * uy * (1.0 + th)
        dgelu = 0.5 * (1.0 + th) + 0.5 * uy * (1.0 - th * th) * (GELU_K * (1.0 + 3.0 * GELU_C * u2))
        dh_ref[...] = dyr * gelu
        duy_ref[...] = dyr * h_ref[...] * dgelu

    tok = pl.BlockSpec((ts, D), lambda i: (i, 0))
    mtok = lambda o: pl.BlockSpec((ts, M), lambda i: (i, o))
    return _pcall(
        body, name=name, grid=(S // ts,),
        in_specs=[tok, tok, pl.BlockSpec(vecs.shape, lambda i: (0, 0)), pl.BlockSpec(wout.shape, lambda i: (0, 0)),
                  mtok(0), pl.BlockSpec(lnv.shape, lambda i: (0, 0)), mtok(3), mtok(0)],
        out_specs=[mtok(0), mtok(0), mtok(0), tok, pl.BlockSpec((SUBLANES, D), lambda i: (0, 0)),
                   pl.BlockSpec((SUBLANES, M), lambda i: (0, 0))],
        out_shape=[jax.ShapeDtypeStruct((S, M), F32)] * 3 + [jax.ShapeDtypeStruct((S, D), BF),
                   jax.ShapeDtypeStruct((SUBLANES, D), F32), jax.ShapeDtypeStruct((SUBLANES, M), F32)],
        compiler_params=_params(("arbitrary",)),
    )(dxo, ym, vecs, wout, cq, lnv, proj, h)


def mix_in_bwd(dparts, x, dxo, vecs, win, name):
    S, D = x.shape
    M = dparts[0].shape[1]
    ts = min(512, S)

    def body(d0, d1, d2, d3, x_ref, dxo_ref, v_ref, w_ref, dx_ref, hb_ref, dp_ref, vg_ref):
        @pl.when(pl.program_id(0) == 0)
        def _():
            vg_ref[...] = jnp.zeros_like(vg_ref)

        for q, dref in enumerate((d0, d1, d2, d3)):
            dp_ref[:, q * M:(q + 1) * M] = dref[...].astype(BF)
        dh = lax.dot_general(dp_ref[...], w_ref[...], CONTRACT_LAST, preferred_element_type=F32)
        xv = x_ref[...]
        r = lax.rsqrt(_rowmean(xv * xv) + EPS)
        n = xv * r
        g = _row(v_ref, R_G2)
        sc1 = 1.0 + _row(v_ref, R_SC2)
        gsc = g * sc1
        hb_ref[...] = (n * gsc + _row(v_ref, R_SH2)).astype(BF)
        dhn = dh * n
        vg_ref[0:1, :] += _colsum(dh)
        vg_ref[1:2, :] += _colsum(dhn) * g
        vg_ref[2:3, :] += _colsum(dhn) * sc1
        dn = dh * gsc
        dx_ref[...] = dxo_ref[...] + r * (dn - n * _rowmean(dn * n))

    tok = pl.BlockSpec((ts, D), lambda i: (i, 0))
    mtok = pl.BlockSpec((ts, M), lambda i: (i, 0))
    return _pcall(
        body, name=name, grid=(S // ts,),
        in_specs=[mtok] * 4 + [tok, tok, pl.BlockSpec(vecs.shape, lambda i: (0, 0)), pl.BlockSpec(win.shape, lambda i: (0, 0))],
        out_specs=[tok, tok, pl.BlockSpec((ts, 4 * M), lambda i: (i, 0)), pl.BlockSpec((SUBLANES, D), lambda i: (0, 0))],
        out_shape=[jax.ShapeDtypeStruct((S, D), F32), jax.ShapeDtypeStruct((S, D), BF),
                   jax.ShapeDtypeStruct((S, 4 * M), BF), jax.ShapeDtypeStruct((SUBLANES, D), F32)],
        compiler_params=_params(("arbitrary",)),
    )(*dparts, x, dxo, vecs, win)


def _adamw(w, g, m, v):
    m = ADAM_B1 * m + (1.0 - ADAM_B1) * g
    v = ADAM_B2 * v + (1.0 - ADAM_B2) * (g * g)
    m_hat = m / (1.0 - ADAM_B1 ** ADAM_STEP)
    v_hat = v / (1.0 - ADAM_B2 ** ADAM_STEP)
    delta = -ADAM_LR * (m_hat / (jnp.sqrt(v_hat) + ADAM_EPS) + ADAM_WD * w)
    return delta, m, v


def adam_big(w, g, m, v, name):
    R, C = w.shape
    tr = 256 if R % 256 == 0 else R // 2 if (R // 2) % SUBLANES == 0 and R > 512 else R
    tc = C if C <= 1536 else (1152 if C % 1152 == 0 else 1024)
    assert R % tr == 0 and C % tc == 0

    def body(w_ref, g_ref, m_ref, v_ref, d_ref, nm_ref, nv_ref):
        d, nm, nv = _adamw(w_ref[...], g_ref[...], m_ref[...], v_ref[...])
        d_ref[...] = d
        nm_ref[...] = nm
        nv_ref[...] = nv

    blk = pl.BlockSpec((tr, tc), lambda i, j: (i, j))
    return _pcall(
        body, name=name, grid=(R // tr, C // tc), in_specs=[blk] * 4, out_specs=[blk] * 3,
        out_shape=[jax.ShapeDtypeStruct((R, C), F32)] * 3, compiler_params=_params(("parallel", "parallel")),
    )(w, g, m, v)


def adam_cond(c_all, dmod, w, m, v, name):
    B, Kin = c_all.shape
    N = w.shape[1]
    tn = 256
    assert N % tn == 0

    def body(c_ref, d_ref, w_ref, m_ref, v_ref, g_ref, dl_ref, nm_ref, nv_ref):
        cv = c_ref[...]
        ca = cv * _sigmoid(cv)
        g = lax.dot_general(ca, d_ref[...], CONTRACT_FIRST, preferred_element_type=F32, precision=lax.Precision.HIGHEST)
        d, nm, nv = _adamw(w_ref[...], g, m_ref[...], v_ref[...])
        g_ref[...] = g
        dl_ref[...] = d
        nm_ref[...] = nm
        nv_ref[...] = nv

    blk = pl.BlockSpec((Kin, tn), lambda n: (0, n))
    return _pcall(
        body, name=name, grid=(N // tn,),
        in_specs=[pl.BlockSpec((B, Kin), lambda n: (0, 0)), pl.BlockSpec((B, tn), lambda n: (0, n)), blk, blk, blk],
        out_specs=[blk] * 4, out_shape=[jax.ShapeDtypeStruct((Kin, N), F32)] * 4,
        compiler_params=_params(("parallel",)),
    )(c_all, dmod, w, m, v)


def adam_small(ws, gs, ms, vs, name):
    n = len(ws)

    def body(*refs):
        ins, outs = refs[:4 * n], refs[4 * n:]
        for k in range(n):
            d, nm, nv = _adamw(ins[k][...], ins[n + k][...], ins[2 * n + k][...], ins[3 * n + k][...])
            outs[k][...] = d
            outs[n + k][...] = nm
            outs[2 * n + k][...] = nv

    vm = pl.BlockSpec(memory_space=pltpu.VMEM)
    shapes = [jax.ShapeDtypeStruct(w.shape, F32) for w in ws]
    out = _pcall(body, name=name, in_specs=[vm] * (4 * n), out_specs=[vm] * (3 * n), out_shape=shapes * 3,
                 compiler_params=_params())(*ws, *gs, *ms, *vs)
    return out[:n], out[n:2 * n], out[2 * n:]


def _me():
    return lax.axis_index("x"), lax.axis_index("y"), lax.axis_index("c")


def _flip(x, y, p):
    return (x ^ (p >> 1) if (p >> 1) else x), (y ^ (p & 1) if (p & 1) else y)


def _handshake(peers):
    barrier = pltpu.get_barrier_semaphore()
    for peer in peers:
        pl.semaphore_signal(barrier, inc=1, device_id=peer, device_id_type=MESH)
    pl.semaphore_wait(barrier, len(peers))


def _seq_call(body, *, name, n_in, out_shape, sem_shapes, collective_id):
    del n_in
    return pl.kernel(body, out_type=out_shape, mesh=plsc.ScalarSubcoreMesh(axis_name="sq", num_cores=1), name=name,
                     scratch_types=sem_shapes, compiler_params=pltpu.CompilerParams(collective_id=collective_id))


def _hbm_comm_call(body, *, name, n_in, out_shape, sem_shapes, seq_id):
    if seq_id is not None:
        return _seq_call(body, name=name, n_in=n_in, out_shape=out_shape, sem_shapes=sem_shapes, collective_id=seq_id)
    anyspec = pl.BlockSpec(memory_space=pl.ANY)
    return _pcall(body, name=name, in_specs=[anyspec] * n_in, out_specs=[anyspec] * len(out_shape), out_shape=out_shape,
                  scratch_shapes=sem_shapes, compiler_params=_params())


def allgather_devices(v, name, with_sum=False):
    R, L = v.shape

    def body(v_ref, out_ref, *rest):
        if with_sum:
            sum_ref, send_sems, recv_sems = rest
        else:
            send_sems, recv_sems = rest
        x, y, c = _me()
        me = 4 * x + 2 * y + c
        out_ref[me] = v_ref[...]
        copies = []
        for p in range(1, N_DEV):
            px, py = _flip(x, y, p >> 1)
            pc = (1 - c) if (p & 1) else c
            peer = 4 * px + 2 * py + pc
            send = pltpu.make_async_remote_copy(src_ref=v_ref, dst_ref=out_ref.at[me], send_sem=send_sems.at[p - 1],
                                                recv_sem=recv_sems.at[p - 1], device_id=(px, py, pc), device_id_type=MESH)
            send.start()
            recv = pltpu.make_async_remote_copy(src_ref=v_ref, dst_ref=out_ref.at[peer], send_sem=send_sems.at[p - 1],
                                                recv_sem=recv_sems.at[p - 1], device_id=(px, py, pc), device_id_type=MESH)
            copies.append((send, recv))
        for send, recv in copies:
            recv.wait_recv()
        for send, recv in copies:
            send.wait_send()
        if with_sum:
            s = out_ref[0]
            for k in range(1, N_DEV):
                s = s + out_ref[k]
            sum_ref[...] = s

    vm = pl.BlockSpec(memory_space=pltpu.VMEM)
    out_shape = [jax.ShapeDtypeStruct((N_DEV, R, L), F32)]
    if with_sum:
        out_shape.append(jax.ShapeDtypeStruct((R, L), F32))
    return _pcall(
        body, name=name, in_specs=[vm], out_specs=[vm] * len(out_shape), out_shape=out_shape,
        scratch_shapes=[pltpu.SemaphoreType.DMA((N_DEV - 1,)), pltpu.SemaphoreType.DMA((N_DEV - 1,))],
        compiler_params=_params(),
    )(v)


def allgather_devices_hbm(v, name, seq_id):
    R, L = v.shape

    def body(v_ref, out_ref, send_sems, recv_sems, local_sem):
        x, y, c = _me()
        me = 4 * x + 2 * y + c
        peers = []
        for p in range(1, N_DEV):
            px, py = _flip(x, y, p >> 1)
            peers.append((px, py, (1 - c) if (p & 1) else c))
        _handshake(peers)
        lc = pltpu.make_async_copy(v_ref, out_ref.at[me], local_sem)
        lc.start()
        copies = []
        for p, (px, py, pc) in enumerate(peers):
            send = pltpu.make_async_remote_copy(src_ref=v_ref, dst_ref=out_ref.at[me], send_sem=send_sems.at[p],
                                                recv_sem=recv_sems.at[p], device_id=(px, py, pc), device_id_type=MESH)
            send.start()
            recv = pltpu.make_async_remote_copy(src_ref=v_ref, dst_ref=out_ref.at[4 * px + 2 * py + pc], send_sem=send_sems.at[p],
                                                recv_sem=recv_sems.at[p], device_id=(px, py, pc), device_id_type=MESH)
            copies.append((send, recv))
        for send, recv in copies:
            recv.wait_recv()
        for send, recv in copies:
            send.wait_send()
        lc.wait()

    return _seq_call(body, name=name, n_in=1, out_shape=[jax.ShapeDtypeStruct((N_DEV, R, L), F32)],
                     sem_shapes=[pltpu.SemaphoreType.DMA((N_DEV - 1,)), pltpu.SemaphoreType.DMA((N_DEV - 1,)),
                                 pltpu.SemaphoreType.DMA], collective_id=seq_id)(v)[0]


def sum_slots(g, name):
    n, R, L = g.shape
    tr = 216 if R % 216 == 0 else R
    assert R % tr == 0 and tr % SUBLANES == 0

    def body(g_ref, o_ref):
        s = g_ref[0]
        for k in range(1, n):
            s = s + g_ref[k]
        o_ref[...] = s

    return _pcall(body, name=name, grid=(R // tr,), in_specs=[pl.BlockSpec((n, tr, L), lambda i: (0, i, 0))],
                  out_specs=pl.BlockSpec((tr, L), lambda i: (i, 0)), out_shape=jax.ShapeDtypeStruct((R, L), F32),
                  compiler_params=_params(("parallel",)))(g)


def allgather_chips(v, name):
    R, L = v.shape

    def body(v_ref, out_ref, send_sems, recv_sems):
        x, y, c = _me()
        chip = 2 * x + y
        out_ref[chip] = v_ref[...]
        copies = []
        for p in range(1, N_CHIPS):
            px, py = _flip(x, y, p)
            send = pltpu.make_async_remote_copy(src_ref=v_ref, dst_ref=out_ref.at[chip], send_sem=send_sems.at[p - 1],
                                                recv_sem=recv_sems.at[p - 1], device_id=(px, py, c), device_id_type=MESH)
            send.start()
            recv = pltpu.make_async_remote_copy(src_ref=v_ref, dst_ref=out_ref.at[2 * px + py], send_sem=send_sems.at[p - 1],
                                                recv_sem=recv_sems.at[p - 1], device_id=(px, py, c), device_id_type=MESH)
            copies.append((send, recv))
        for send, recv in copies:
            recv.wait_recv()
        for send, recv in copies:
            send.wait_send()

    vm = pl.BlockSpec(memory_space=pltpu.VMEM)
    return _pcall(
        body, name=name, in_specs=[vm], out_specs=vm, out_shape=jax.ShapeDtypeStruct((N_CHIPS, R, L), F32),
        scratch_shapes=[pltpu.SemaphoreType.DMA((N_CHIPS - 1,)), pltpu.SemaphoreType.DMA((N_CHIPS - 1,))],
        compiler_params=_params(),
    )(v)


def _shard_window(ref, kind, shard_shape, chip, half):
    r, c = shard_shape
    hr = r // 2
    if kind == "col":
        return ref.at[pl.ds(pl.multiple_of(half * hr, hr), hr), pl.ds(pl.multiple_of(chip * c, c), c)]
    return ref.at[pl.ds(pl.multiple_of(chip * r + half * hr, hr), hr), :]


def allgather_weights(shards, kinds, name, seq_id=None):
    n = len(shards)
    fulls = []
    for s, kind in zip(shards, kinds):
        r, c = s.shape
        fulls.append(jax.ShapeDtypeStruct((r, N_CHIPS * c) if kind == "col" else (N_CHIPS * r, c), s.dtype))

    def body(*refs):
        srcs, outs = refs[:n], refs[n:2 * n]
        send_sems, recv_sems, fsend_sems, frecv_sems, local_sems = refs[2 * n:]
        x, y, c = _me()
        chip = 2 * x + y
        sib = (x, y, 1 - c)
        if seq_id is not None:
            _handshake([(*_flip(x, y, p), c) for p in range(1, N_CHIPS)] + [sib])
        locals_, sends, fwds = [], [], []
        for i in range(n):
            shp = srcs[i].shape
            hr = shp[0] // 2
            win_all = (outs[i].at[:, pl.ds(pl.multiple_of(chip * shp[1], shp[1]), shp[1])] if kinds[i] == "col"
                       else outs[i].at[pl.ds(pl.multiple_of(chip * shp[0], shp[0]), shp[0]), :])
            lc = pltpu.make_async_copy(srcs[i], win_all, local_sems.at[i])
            lc.start()
            locals_.append(lc)
            my_half = srcs[i].at[pl.ds(pl.multiple_of(c * hr, hr), hr), :]
            for p in range(1, N_CHIPS):
                px, py = _flip(x, y, p)
                k = i * (N_CHIPS - 1) + p - 1
                cp = pltpu.make_async_remote_copy(src_ref=my_half, dst_ref=_shard_window(outs[i], kinds[i], shp, chip, c),
                                                  send_sem=send_sems.at[k], recv_sem=recv_sems.at[k],
                                                  device_id=(px, py, c), device_id_type=MESH)
                cp.start()
                sends.append(cp)
        for i in range(n):
            shp = srcs[i].shape
            for p in range(1, N_CHIPS):
                px, py = _flip(x, y, p)
                k = i * (N_CHIPS - 1) + p - 1
                landed = _shard_window(outs[i], kinds[i], shp, 2 * px + py, c)
                pltpu.make_async_remote_copy(src_ref=landed, dst_ref=landed, send_sem=send_sems.at[k], recv_sem=recv_sems.at[k],
                                             device_id=(px, py, c), device_id_type=MESH).wait_recv()
                fw = pltpu.make_async_remote_copy(src_ref=landed, dst_ref=landed, send_sem=fsend_sems.at[k],
                                                  recv_sem=frecv_sems.at[k], device_id=sib, device_id_type=MESH)
                fw.start()
                fwds.append(fw)
        for i in range(n):
            shp = srcs[i].shape
            for p in range(1, N_CHIPS):
                px, py = _flip(x, y, p)
                k = i * (N_CHIPS - 1) + p - 1
                other = _shard_window(outs[i], kinds[i], shp, 2 * px + py, 1 - c)
                pltpu.make_async_remote_copy(src_ref=other, dst_ref=other, send_sem=fsend_sems.at[k], recv_sem=frecv_sems.at[k],
                                             device_id=sib, device_id_type=MESH).wait_recv()
        for cp in sends + fwds:
            cp.wait_send()
        for lc in locals_:
            lc.wait()

    nk = n * (N_CHIPS - 1)
    return _hbm_comm_call(
        body, name=name, n_in=n, out_shape=fulls, seq_id=seq_id,
        sem_shapes=[pltpu.SemaphoreType.DMA((nk,)), pltpu.SemaphoreType.DMA((nk,)), pltpu.SemaphoreType.DMA((nk,)),
                    pltpu.SemaphoreType.DMA((nk,)), pltpu.SemaphoreType.DMA((n,))],
    )(*shards)


def _as_halves(g, kind, shard_shape):
    r, c = shard_shape
    if kind == "col":
        return g.reshape(2, r // 2, N_CHIPS * c)
    return g.reshape(N_CHIPS, 2, r // 2, c)


def exchange_sibling_halves(grads, kinds, shard_shapes, name, seq_id=None):
    n = len(grads)
    views = [_as_halves(g, k, s) for g, k, s in zip(grads, kinds, shard_shapes)]
    outs = []
    for k, (r, c) in zip(kinds, shard_shapes):
        outs.append(jax.ShapeDtypeStruct((r // 2, N_CHIPS * c) if k == "col" else (N_CHIPS, r // 2, c), F32))

    def body(*refs):
        srcs, dsts = refs[:n], refs[n:2 * n]
        send_sems, recv_sems = refs[2 * n:]
        x, y, c = _me()
        if seq_id is not None:
            _handshake([(x, y, 1 - c)])
        cps = []
        for i in range(n):
            src = srcs[i].at[1 - c] if kinds[i] == "col" else srcs[i].at[:, 1 - c]
            cp = pltpu.make_async_remote_copy(src_ref=src, dst_ref=dsts[i], send_sem=send_sems.at[i], recv_sem=recv_sems.at[i],
                                              device_id=(x, y, 1 - c), device_id_type=MESH)
            cp.start()
            cps.append(cp)
        for cp in cps:
            cp.wait_recv()
        for cp in cps:
            cp.wait_send()

    return _hbm_comm_call(body, name=name, n_in=n, out_shape=outs, seq_id=seq_id,
                          sem_shapes=[pltpu.SemaphoreType.DMA((n,)), pltpu.SemaphoreType.DMA((n,))])(*views)


def add_sibling_half(g, recv, kind, shard_shape, cidx, name):
    r, c = shard_shape
    hr = r // 2
    gv = _as_halves(g, kind, shard_shape)
    tr = hr if hr <= 512 else (256 if hr % 256 == 0 else hr // 2)
    assert hr % tr == 0

    def body(ci_ref, g_ref, r_ref, h_ref, hb_ref):
        s = g_ref[...] + r_ref[...]
        h_ref[...] = s
        hb_ref[...] = s.astype(BF)

    if kind == "col":
        grid = (hr // tr, N_CHIPS)
        g_spec = pl.BlockSpec((None, tr, c), lambda i, k, ci: (ci[0], i, k))
        o_spec = pl.BlockSpec((tr, c), lambda i, k, ci: (i, k))
    else:
        grid = (hr // tr, N_CHIPS)
        g_spec = pl.BlockSpec((None, None, tr, c), lambda i, k, ci: (k, ci[0], i, 0))
        o_spec = pl.BlockSpec((None, tr, c), lambda i, k, ci: (k, i, 0))
    gs = pltpu.PrefetchScalarGridSpec(num_scalar_prefetch=1, grid=grid, in_specs=[g_spec, o_spec], out_specs=[o_spec, o_spec])
    return _pcall(
        body, name=name, grid_spec=gs,
        out_shape=[jax.ShapeDtypeStruct(recv.shape, F32), jax.ShapeDtypeStruct(recv.shape, BF)],
        compiler_params=_params(("parallel", "parallel")),
    )(cidx, gv, recv)


def exchange_chip_pieces(hbs, kinds, shard_shapes, name, seq_id=None):
    n = len(hbs)
    outs = [jax.ShapeDtypeStruct((N_CHIPS - 1, r // 2, c), BF) for (r, c) in shard_shapes]

    def body(*refs):
        srcs, dsts = refs[:n], refs[n:2 * n]
        send_sems, recv_sems = refs[2 * n:]
        x, y, c = _me()
        if seq_id is not None:
            _handshake([(*_flip(x, y, p), c) for p in range(1, N_CHIPS)])
        cps = []
        for i in range(n):
            cc = shard_shapes[i][1]
            for p in range(1, N_CHIPS):
                px, py = _flip(x, y, p)
                pchip = 2 * px + py
                src = (srcs[i].at[:, pl.ds(pl.multiple_of(pchip * cc, cc), cc)] if kinds[i] == "col" else srcs[i].at[pchip])
                k = i * (N_CHIPS - 1) + p - 1
                cp = pltpu.make_async_remote_copy(src_ref=src, dst_ref=dsts[i].at[p - 1], send_sem=send_sems.at[k],
                                                  recv_sem=recv_sems.at[k], device_id=(px, py, c), device_id_type=MESH)
                cp.start()
                cps.append(cp)
        for cp in cps:
            cp.wait_recv()
        for cp in cps:
            cp.wait_send()

    nk = n * (N_CHIPS - 1)
    return _hbm_comm_call(body, name=name, n_in=n, out_shape=outs, seq_id=seq_id,
                          sem_shapes=[pltpu.SemaphoreType.DMA((nk,)), pltpu.SemaphoreType.DMA((nk,))])(*hbs)


def sum_chip_pieces(h, pieces, kind, shard_shape, chip_core, name):
    r, c = shard_shape
    hr = r // 2
    tr = hr if hr <= 512 else (256 if hr % 256 == 0 else hr // 2)
    assert hr % tr == 0
    nrb = hr // tr

    def body(ci_ref, h_ref, p_ref, q_ref):
        q_ref[...] = ((h_ref[...] + p_ref[0].astype(F32)) + p_ref[1].astype(F32)) + p_ref[2].astype(F32)

    if kind == "col":
        h_spec = pl.BlockSpec((tr, c), lambda i, ci: (i, ci[0]))
    else:
        h_spec = pl.BlockSpec((None, tr, c), lambda i, ci: (ci[0], i, 0))
    gs = pltpu.PrefetchScalarGridSpec(
        num_scalar_prefetch=1, grid=(nrb,),
        in_specs=[h_spec, pl.BlockSpec((N_CHIPS - 1, tr, c), lambda i, ci: (0, i, 0))],
        out_specs=pl.BlockSpec((tr, c), lambda i, ci: (ci[1] * nrb + i, 0)))
    return _pcall(body, name=name, grid_spec=gs, out_shape=jax.ShapeDtypeStruct((r, c), F32),
                  compiler_params=_params(("parallel",)))(chip_core, h, pieces)


def exchange_reduced_halves(qs, name):
    n = len(qs)

    def body(*refs):
        bufs = refs[n:2 * n]
        send_sems, recv_sems = refs[2 * n:]
        x, y, c = _me()
        cps = []
        for i in range(n):
            hr = bufs[i].shape[0] // 2
            mine = bufs[i].at[pl.ds(pl.multiple_of(c * hr, hr), hr), :]
            other = bufs[i].at[pl.ds(pl.multiple_of((1 - c) * hr, hr), hr), :]
            cp = pltpu.make_async_remote_copy(src_ref=mine, dst_ref=mine, send_sem=send_sems.at[i], recv_sem=recv_sems.at[i],
                                              device_id=(x, y, 1 - c), device_id_type=MESH)
            cp.start()
            cps.append((cp, pltpu.make_async_remote_copy(src_ref=other, dst_ref=other, send_sem=send_sems.at[i],
                                                         recv_sem=recv_sems.at[i], device_id=(x, y, 1 - c), device_id_type=MESH)))
        for cp, rv in cps:
            rv.wait_recv()
        for cp, rv in cps:
            cp.wait_send()

    anyspec = pl.BlockSpec(memory_space=pl.ANY)
    return _pcall(
        body, name=name, in_specs=[anyspec] * n, out_specs=[anyspec] * n,
        out_shape=[jax.ShapeDtypeStruct(q.shape, F32) for q in qs], input_output_aliases={i: i for i in range(n)},
        scratch_shapes=[pltpu.SemaphoreType.DMA((n,)), pltpu.SemaphoreType.DMA((n,))],
        compiler_params=_params(),
    )(*qs)


def _rows128(a):
    return a.reshape(-1, LANES)


def _block_diag(w):
    H, d, _ = w.shape
    eye = jnp.eye(H, dtype=w.dtype)
    return jnp.einsum("hde,hg->hdge", w, eye).reshape(H * d, H * d)


def _diag_blocks(g4, H, d):
    nb = g4.shape[0]
    per = LANES // d
    g = g4.reshape(nb, per, d, per, d)
    return jnp.stack([g[:, j, :, j, :] for j in range(per)], axis=1).reshape(H, d, d)


def kernel(x, c, w_mod, b_mod, g_ffn1, w_ffn1_in, w_ffn1_out, g_mix, w_in, conv_w, conv_b, ln_g, ln_b, rnn_conv_w, rnn_conv_b, w_a, b_a, w_i, b_i, lru_lambda, w_out, g_ffn2, w_ffn2_in, w_ffn2_out, w_fmod, b_fmod, g_final, loss_target, m_w_mod, m_b_mod, m_g_ffn1, m_w_ffn1_in, m_w_ffn1_out, m_g_mix, m_w_in, m_conv_w, m_conv_b, m_ln_g, m_ln_b, m_rnn_conv_w, m_rnn_conv_b, m_w_a, m_b_a, m_w_i, m_b_i, m_lru_lambda, m_w_out, m_g_ffn2, m_w_ffn2_in, m_w_ffn2_out, m_w_fmod, m_b_fmod, m_g_final, v_w_mod, v_b_mod, v_g_ffn1, v_w_ffn1_in, v_w_ffn1_out, v_g_mix, v_w_in, v_conv_w, v_conv_b, v_ln_g, v_ln_b, v_rnn_conv_w, v_rnn_conv_b, v_w_a, v_b_a, v_w_i, v_b_i, v_lru_lambda, v_w_out, v_g_ffn2, v_w_ffn2_in, v_w_ffn2_out, v_w_fmod, v_b_fmod, v_g_final):
    S, D = x.shape[1], x.shape[2]
    M = conv_b.shape[1]
    H, HD = w_a.shape[1], w_a.shape[2]
    nb = M // LANES
    ix, iy, ic = lax.axis_index("x"), lax.axis_index("y"), lax.axis_index("c")
    chip = 2 * ix + iy
    dev = 2 * chip + ic
    cidx = jnp.reshape(ic, (1,)).astype(I32)
    chip_core = jnp.stack([chip, ic]).astype(I32)
    xs = x[0]
    tgt = loss_target[0]

    kinds = ["col", "row"]
    w_f1, w_mx, w_f2 = [w_ffn1_in[0], w_ffn1_out[0]], [w_in[0], w_out[0]], [w_ffn2_in[0], w_ffn2_out[0]]
    as_bf = lambda ws: [w.astype(BF) for w in ws]
    shapes_of = lambda ws: [w.shape for w in ws]
    wi1, wo1 = allgather_weights(as_bf(w_f1), kinds, "gather_ffn1", seq_id=9)
    win, wout = allgather_weights(as_bf(w_mx), kinds, "gather_mix", seq_id=1)
    wi2, wo2 = allgather_weights(as_bf(w_f2), kinds, "gather_ffn2", seq_id=2)

    c_all =allgather_devices(_rows128(c), "gather_c")[0].reshape(N_DEV, D)
    mod_cols = cond_matmul(c_all, w_mod[0], "mod_proj")
    fmod_cols = cond_matmul(c_all, w_fmod, "fmod_proj")
    convw_pad = jnp.pad(conv_w[0], ((0, 32 - CONV_WIDTH), (0, 0)))
    rnnw_pad = jnp.pad(rnn_conv_w[0], ((0, SUBLANES - RNN_CONV_WIDTH), (0, 0)))
    n_mod, n_fmod = mod_cols.shape[1], fmod_cols.shape[1]
    small = jnp.concatenate([_rows128(mod_cols), _rows128(fmod_cols), convw_pad, rnnw_pad], axis=0)
    small4 = allgather_chips(small, "gather_cond")
    r0 = N_DEV * n_mod // LANES
    r1 = r0 + N_DEV * n_fmod // LANES
    mod_all = small4[:, :r0].reshape(N_CHIPS, N_DEV, n_mod)
    fmod_all = small4[:, r0:r1].reshape(N_CHIPS, N_DEV, n_fmod)
    convw4 = small4[:, r1:r1 + 32]
    rnnw4 = small4[:, r1 + 32:r1 + 32 + SUBLANES]
    mod_row = lax.dynamic_index_in_dim(mod_all, dev, axis=1, keepdims=False).reshape(1, N_CHIPS * n_mod) + b_mod
    fmod_row = lax.dynamic_index_in_dim(fmod_all, dev, axis=1, keepdims=False).reshape(1, N_CHIPS * n_fmod) + b_fmod[None, :]
    vecs = jnp.concatenate([mod_row.reshape(9, D), fmod_row.reshape(2, D), g_ffn1, g_mix, g_ffn2, g_final[None, :],
                            jnp.zeros((1, D), F32)], axis=0)
    lnv = jnp.concatenate([ln_g, ln_b, jnp.zeros((SUBLANES - 2, M), F32)], axis=0)
    bda = _block_diag(w_a[0]).astype(BF)
    bdi = _block_diag(w_i[0]).astype(BF)

    def reduce_add(gs, recv, ws, tag, kinds_=kinds):
        pairs = [add_sibling_half(g, r_, k, w.shape, cidx, f"add_sibling_{tag}{j}")
                 for j, (g, r_, k, w) in enumerate(zip(gs, recv, kinds_, ws))]
        return [p[0] for p in pairs], [p[1] for p in pairs]

    def reduce_sum(hs_, recv, ws, tag, kinds_=kinds):
        return [sum_chip_pieces(h_, p_, k, w.shape, chip_core, f"sum_chips_{tag}{j}")
                for j, (h_, p_, k, w) in enumerate(zip(hs_, recv, kinds_, ws))]

    rows1 = (R_SH1, R_SC1, R_GT1, R_G1)
    rows3 = (R_SH3, R_SC3, R_GT3, R_G3)
    x1, g1s, u1s, y1 = ffn_fwd(xs, vecs, wi1, wo1, rows1, "ffn1_fwd")
    proj = norm_matmul(x1, vecs, win, (R_SH2, R_SC2, R_G2), "mix_in_proj")
    cq = conv_fwd(proj, convw4, conv_b, "conv_fwd")
    xr, ra, ii, hh = rnn_fwd(proj, rnnw4, rnn_conv_b, bda, bdi, b_a, b_i, lru_lambda, "rnn_fwd")
    x2, ym, ycat = mix_out(cq, proj, hh, x1, vecs, lnv, wout, "mix_out")
    x3, g2s, u2s, y2 = ffn_fwd(x2, vecs, wi2, wo2, rows3, "ffn2_fwd")
    dx3, vgf = final_fwd_bwd(x3, tgt, vecs, "final_loss")

    Fd = wo1.shape[0]
    tk = min(512, S)
    dx2, act2, dg2, du2, h3b, dy2b, vg3 = ffn_bwd(dx3, x2, vecs, g2s, u2s, y2, wi2, wo2, rows3, "ffn2_bwd")
    gwo2 = matmul(act2, dy2b, "tn", tm=Fd // 2, tn=D, tk=tk, name="ffn2_dwo")
    gwi2 = matmul(h3b, dg2, "tn", tm=D, tn=Fd // 2, tk=tk, name="ffn2_dwg", out_cols=2 * Fd)
    gwi2 = matmul(h3b, du2, "tn", tm=D, tn=Fd // 2, tk=tk, name="ffn2_dwu", out_cols=2 * Fd, col_off=Fd, prev=gwi2)
    recv1_f2 = exchange_sibling_halves([gwi2, gwo2], kinds, shapes_of(w_f2), "reduce1_ffn2", seq_id=3)
    dcq, dhout, duy, dymb, vgd, vgm = mix_out_bwd(dx2, ym, vecs, wout, cq, lnv, proj, hh, "mix_out_bwd")
    gwout = matmul(ycat, dymb, "tn", tm=2 * M, tn=D, tk=tk, name="mix_dwout")
    duv, dug, dconvw4, dconvb = conv_bwd(dcq, proj, convw4, "conv_bwd")
    dux, dwa4, dwi4, drnnw4, rvec = rnn_bwd(dhout, hh, xr, ra, ii, proj, rnnw4, bda, bdi, lru_lambda, "rnn_bwd")
    dx1, h2b, dpb, vg2 = mix_in_bwd((duv, dug, dux, duy), x1, dx2, vecs, win, "mix_in_bwd")
    gwin = matmul(h2b, dpb, "tn", tm=D, tn=1024, tk=tk, name="mix_dwin")
    h_f2, hb_f2 = reduce_add([gwi2, gwo2], recv1_f2, w_f2, "ffn2_")
    recv2_f2 = exchange_chip_pieces(hb_f2, kinds, shapes_of(w_f2), "reduce2_ffn2", seq_id=4)
    recv1_mx = exchange_sibling_halves([gwin, gwout], kinds, shapes_of(w_mx), "reduce1_mix", seq_id=5)
    dx0, act1, dg1, du1, h1b, dy1b, vg1 = ffn_bwd(dx1, xs, vecs, g1s, u1s, y1, wi1, wo1, rows1, "ffn1_bwd")
    h_mx, hb_mx = reduce_add([gwin, gwout], recv1_mx, w_mx, "mix_")
    recv2_mx = exchange_chip_pieces(hb_mx, kinds, shapes_of(w_mx), "reduce2_mix", seq_id=6)
    gwo1 = matmul(act1, dy1b, "tn", tm=Fd // 2, tn=D, tk=tk, name="ffn1_dwo")
    w_f1o, w_f1i = w_f1[1:], w_f1[:1]
    recv1_f1o = exchange_sibling_halves([gwo1], ["row"], shapes_of(w_f1o), "reduce1_ffn1_out", seq_id=7)
    gwi1 = matmul(h1b, dg1, "tn", tm=D, tn=Fd // 2, tk=tk, name="ffn1_dwg", out_cols=2 * Fd)
    h_f1o, hb_f1o = reduce_add([gwo1], recv1_f1o, w_f1o, "ffn1_out", ["row"])
    recv2_f1o = exchange_chip_pieces(hb_f1o, ["row"], shapes_of(w_f1o), "reduce2_ffn1_out", seq_id=11)
    gwi1 = matmul(h1b, du1, "tn", tm=D, tn=Fd // 2, tk=tk, name="ffn1_dwu", out_cols=2 * Fd, col_off=Fd, prev=gwi1)
    recv1_f1i = exchange_sibling_halves([gwi1], ["col"], shapes_of(w_f1i), "reduce1_ffn1_in", seq_id=12)
    g_f2 = exchange_reduced_halves(reduce_sum(h_f2, recv2_f2, w_f2, "ffn2_"), "reduce3_ffn2")
    g_mx = exchange_reduced_halves(reduce_sum(h_mx, recv2_mx, w_mx, "mix_"), "reduce3_mix")

    dmod_row = jnp.concatenate([vg1[1:3], vg1[0:1], vg2[0:2], vgd[0:1], vg3[1:3], vg3[0:1]], axis=0)
    gains = jnp.concatenate([vg1[3:4], vg2[2:3], vg3[3:4], vgf[2:4]], axis=0)
    mvecs = jnp.concatenate([dconvb, vgm[0:2], rvec[0:4], jnp.zeros((1, M), F32)], axis=0)
    parts = [_rows128(dmod_row), _rows128(vgf[0:2]), _rows128(gains), _rows128(mvecs),
             _rows128(dconvw4), _rows128(drnnw4), _rows128(_diag_blocks(dwa4, H, HD)), _rows128(_diag_blocks(dwi4, H, HD))]
    sizes = [p.shape[0] for p in parts]
    packed = jnp.concatenate(parts, axis=0)
    gathered = allgather_devices_hbm(packed, "gather_small", seq_id=10)
    summed = sum_slots(gathered, "sum_small")
    offs = [0]
    for s in sizes:
        offs.append(offs[-1] + s)
    seg = lambda k: summed[offs[k]:offs[k + 1]]
    g_b_mod = seg(0).reshape(1, 9 * D)
    g_b_fmod = seg(1).reshape(1, 2 * D)
    gsum = seg(2).reshape(5, D)
    loss = (0.5 / D) * jnp.sum(gsum[4])
    msum = seg(3).reshape(SUBLANES, M)
    g_conv_w = lax.dynamic_index_in_dim(seg(4).reshape(nb, 32, LANES), chip, axis=0, keepdims=False)[:CONV_WIDTH]
    g_rnn_w = lax.dynamic_index_in_dim(seg(5).reshape(nb, SUBLANES, LANES), chip, axis=0, keepdims=False)[:RNN_CONV_WIDTH]
    g_w_a = seg(6).reshape(H, HD, HD)
    g_w_i = seg(7).reshape(H, HD, HD)
    dmod_all = gathered[:, offs[0]:offs[1]].reshape(N_DEV, 9 * D)
    dfmod_all = gathered[:, offs[1]:offs[2]].reshape(N_DEV, 2 * D)
    dmod_cols = lax.dynamic_slice_in_dim(dmod_all, chip * n_mod, n_mod, axis=1)
    dfmod_cols = lax.dynamic_slice_in_dim(dfmod_all, chip * n_fmod, n_fmod, axis=1)

    h_f1i, hb_f1i = reduce_add([gwi1], recv1_f1i, w_f1i, "ffn1_in", ["col"])
    recv2_f1i = exchange_chip_pieces(hb_f1i, ["col"], shapes_of(w_f1i), "reduce2_ffn1_in", seq_id=8)
    g_f1o = exchange_reduced_halves(reduce_sum(h_f1o, recv2_f1o, w_f1o, "ffn1_out", ["row"]), "reduce3_ffn1_out")
    g_f1i = exchange_reduced_halves(reduce_sum(h_f1i, recv2_f1i, w_f1i, "ffn1_in", ["col"]), "reduce3_ffn1_in")
    g_f1 = list(g_f1i) + list(g_f1o)

    big_w = w_f1 + w_mx + w_f2
    g_big = g_f1 + list(g_mx) + list(g_f2)
    names = ["ffn1_in", "ffn1_out", "w_in", "w_out", "ffn2_in", "ffn2_out"]
    big_m = [m_w_ffn1_in[0], m_w_ffn1_out[0], m_w_in[0], m_w_out[0], m_w_ffn2_in[0], m_w_ffn2_out[0]]
    big_v = [v_w_ffn1_in[0], v_w_ffn1_out[0], v_w_in[0], v_w_out[0], v_w_ffn2_in[0], v_w_ffn2_out[0]]
    big_out = [adam_big(w, g, m, v, "adam_" + nm) for w, g, m, v, nm in zip(big_w, g_big, big_m, big_v, names)]
    g_w_mod, d_w_mod, nm_w_mod, nv_w_mod = adam_cond(c_all, dmod_cols, w_mod[0], m_w_mod[0], v_w_mod[0], "adam_w_mod")
    g_w_fmod, d_w_fmod, nm_w_fmod, nv_w_fmod = adam_cond(c_all, dfmod_cols, w_fmod, m_w_fmod, v_w_fmod, "adam_w_fmod")

    flat2 = lambda a: a.reshape(-1, a.shape[-1])
    small_names = ["b_mod", "g_ffn1", "g_mix", "conv_w", "conv_b", "ln_g", "ln_b", "rnn_conv_w", "rnn_conv_b", "w_a", "b_a",
                   "w_i", "b_i", "lru_lambda", "g_ffn2", "b_fmod", "g_final"]
    small_w = [b_mod, g_ffn1, g_mix, conv_w, conv_b, ln_g, ln_b, rnn_conv_w, rnn_conv_b, w_a, b_a, w_i, b_i, lru_lambda,
               g_ffn2, b_fmod, g_final]
    small_m = [m_b_mod, m_g_ffn1, m_g_mix, m_conv_w, m_conv_b, m_ln_g, m_ln_b, m_rnn_conv_w, m_rnn_conv_b, m_w_a, m_b_a,
               m_w_i, m_b_i, m_lru_lambda, m_g_ffn2, m_b_fmod, m_g_final]
    small_v = [v_b_mod, v_g_ffn1, v_g_mix, v_conv_w, v_conv_b, v_ln_g, v_ln_b, v_rnn_conv_w, v_rnn_conv_b, v_w_a, v_b_a,
               v_w_i, v_b_i, v_lru_lambda, v_g_ffn2, v_b_fmod, v_g_final]
    small_g = [g_b_mod, gsum[0:1], gsum[1:2], g_conv_w, msum[0:1], msum[1:2], msum[2:3], g_rnn_w, msum[3:4], g_w_a, msum[4:5],
               g_w_i, msum[5:6], msum[6:7], gsum[2:3], g_b_fmod, gsum[3:4]]
    small_g = [g.reshape(w.shape) for g, w in zip(small_g, small_w)]
    two_d = lambda a: a.reshape(1, -1) if a.ndim == 1 else flat2(a)
    sd, sm, sv = adam_small([two_d(a) for a in small_w], [two_d(a) for a in small_g], [two_d(a) for a in small_m],
                            [two_d(a) for a in small_v], "adam_small")
    small = {}
    for k, nm in enumerate(small_names):
        shp = small_w[k].shape
        small[nm] = (small_g[k], sd[k].reshape(shp), sm[k].reshape(shp), sv[k].reshape(shp))

    big = {"w_mod": tuple(a[None] for a in (g_w_mod, d_w_mod, nm_w_mod, nv_w_mod)),
           "w_fmod": (g_w_fmod, d_w_fmod, nm_w_fmod, nv_w_fmod)}
    for nm, full, g, (d, nmm, nvv) in zip(["w_ffn1_in", "w_ffn1_out", "w_in", "w_out", "w_ffn2_in", "w_ffn2_out"],
                                         big_w, g_big, big_out):
        big[nm] = tuple(a[None] for a in (g, d, nmm, nvv))
    order = ["w_mod", "b_mod", "g_ffn1", "w_ffn1_in", "w_ffn1_out", "g_mix", "w_in", "conv_w", "conv_b", "ln_g", "ln_b",
             "rnn_conv_w", "rnn_conv_b", "w_a", "b_a", "w_i", "b_i", "lru_lambda", "w_out", "g_ffn2", "w_ffn2_in",
             "w_ffn2_out", "w_fmod", "b_fmod", "g_final"]
    table = {**small, **big}
    outs = [loss, dx0[None]]
    for kind_ in range(4):
        outs.extend(table[nm][kind_] for nm in order)
    return tuple(outs)
```

```python
import functools

import jax
import jax.numpy as jnp
from jax import lax
from jax.experimental import pallas as pl
from jax.experimental.pallas import tpu as pltpu
from jax.experimental.pallas import tpu_sc as plsc

F32 = jnp.float32
BF = jnp.bfloat16
I32 = jnp.int32
MESH = pl.DeviceIdType.MESH

EPS = 1e-6
RG_C = 8.0
MACARON_W = 0.5
CONV_WIDTH = 31
RNN_CONV_WIDTH = 4
ADAM_LR = 0.001
ADAM_B1 = 0.9
ADAM_B2 = 0.999
ADAM_EPS = 1e-08
ADAM_WD = 0.01
ADAM_STEP = 10

LANES = 128
SUBLANES = 8
VMEM_LIMIT = 48 * 1024 * 1024
N_CHIPS = 4
N_DEV = 8

R_SH1, R_SC1, R_GT1, R_SH2, R_SC2, R_GT2, R_SH3, R_SC3, R_GT3, R_FSH, R_FSC, R_G1, R_G2, R_G3, R_GF = range(15)

CONTRACT_LAST = (((1,), (1,)), ((), ()))
CONTRACT_FIRST = (((0,), (0,)), ((), ()))


def _pcall(body, **kw):
    return pl.pallas_call(body, **kw)


def _params(sem=None, vmem=VMEM_LIMIT):
    if sem is None:
        return pltpu.CompilerParams(vmem_limit_bytes=vmem)
    return pltpu.CompilerParams(dimension_semantics=sem, vmem_limit_bytes=vmem)


def _row(ref, r):
    return ref[r:r + 1, :]


def _sigmoid(x):
    return 1.0 / (1.0 + jnp.exp(-x))


def _colsum(x):
    return jnp.sum(x, axis=0, keepdims=True)


def _rowmean(x):
    return jnp.mean(x, axis=-1, keepdims=True)


def matmul(a, b, mode, *, tm, tn, tk, name, out_dtype=F32, out_cols=None, col_off=0, prev=None):
    if mode == "nn":
        (M, K), (K2, N) = a.shape, b.shape
    elif mode == "nt":
        (M, K), (N, K2) = a.shape, b.shape
    else:
        (K, M), (K2, N) = a.shape, b.shape
    assert K == K2 and M % tm == 0 and N % tn == 0 and K % tk == 0 and col_off % tn == 0
    nk = K // tk
    out_cols = N if out_cols is None else out_cols
    off = col_off // tn

    def body(*refs):
        if prev is None:
            a_ref, b_ref, o_ref, acc = refs
        else:
            a_ref, b_ref, _, o_ref, acc = refs
        k = pl.program_id(2)

        @pl.when(k == 0)
        def _():
            acc[...] = jnp.zeros_like(acc)

        av = a_ref[...].astype(BF)
        bv = b_ref[...].astype(BF)
        if mode == "nn":
            acc[...] += jnp.dot(av, bv, preferred_element_type=F32)
        elif mode == "nt":
            acc[...] += lax.dot_general(av, bv, CONTRACT_LAST, preferred_element_type=F32)
        else:
            acc[...] += lax.dot_general(av, bv, CONTRACT_FIRST, preferred_element_type=F32)

        @pl.when(k == nk - 1)
        def _():
            o_ref[...] = acc[...].astype(out_dtype)

    if mode == "nn":
        a_spec = pl.BlockSpec((tm, tk), lambda m, n, k: (m, k))
        b_spec = pl.BlockSpec((tk, tn), lambda m, n, k: (k, n))
    elif mode == "nt":
        a_spec = pl.BlockSpec((tm, tk), lambda m, n, k: (m, k))
        b_spec = pl.BlockSpec((tn, tk), lambda m, n, k: (n, k))
    else:
        a_spec = pl.BlockSpec((tk, tm), lambda m, n, k: (k, m))
        b_spec = pl.BlockSpec((tk, tn), lambda m, n, k: (k, n))
    in_specs = [a_spec, b_spec]
    args = [a, b]
    aliases = {}
    if prev is not None:
        in_specs.append(pl.BlockSpec(memory_space=pl.ANY))
        args.append(prev)
        aliases = {2: 0}
    return _pcall(
        body, name=name, grid=(M // tm, N // tn, nk), in_specs=in_specs,
        out_specs=pl.BlockSpec((tm, tn), lambda m, n, k: (m, n + off)),
        out_shape=jax.ShapeDtypeStruct((M, out_cols), out_dtype),
        scratch_shapes=[pltpu.VMEM((tm, tn), F32)], input_output_aliases=aliases,
        compiler_params=_params(("parallel", "parallel", "arbitrary")),
    )(*args)


def cond_matmul(c_all, w, name):
    B, K = c_all.shape
    N = w.shape[1]
    tn = 256
    assert N % tn == 0

    def body(c_ref, w_ref, o_ref):
        cv = c_ref[...]
        ca = cv * _sigmoid(cv)
        o_ref[...] = jnp.dot(ca, w_ref[...], preferred_element_type=F32, precision=lax.Precision.HIGHEST)

    return _pcall(
        body, name=name, grid=(N // tn,),
        in_specs=[pl.BlockSpec((B, K), lambda n: (0, 0)), pl.BlockSpec((K, tn), lambda n: (0, n))],
        out_specs=pl.BlockSpec((B, tn), lambda n: (0, n)),
        out_shape=jax.ShapeDtypeStruct((B, N), F32), compiler_params=_params(("parallel",)),
    )(c_all, w)


FFN_CH = 1408
FFN_FWD_TS = 512
FFN_BWD_TS = 256
FFN_FWD_VMEM = 56 * 1024 * 1024


def ffn_fwd(x, vecs, wi, wo, rows, name):
    r_sh, r_sc, r_gt, r_g = rows
    S, D = x.shape
    Fd = wo.shape[0]
    ts, ch = min(FFN_FWD_TS, S), FFN_CH
    ni, nj = S // ts, Fd // ch
    assert nj >= 2

    def body(x_ref, v_ref, wg_ref, wu_ref, wo_ref, xo_ref, g_ref, u_ref, y_ref, h_sc, acc):
        j = pl.program_id(1)

        @pl.when(j == 0)
        def _():
            xv = x_ref[...]
            r = lax.rsqrt(_rowmean(xv * xv) + EPS)
            gs = _row(v_ref, r_g) * (1.0 + _row(v_ref, r_sc))
            h_sc[...] = (xv * r * gs + _row(v_ref, r_sh)).astype(BF)

        hb = h_sc[...]
        G = jnp.dot(hb, wg_ref[...], preferred_element_type=F32)
        U = jnp.dot(hb, wu_ref[...], preferred_element_type=F32)
        g_ref[...] = G.astype(BF)
        u_ref[...] = U.astype(BF)
        act = (G * _sigmoid(G) * U).astype(BF)
        part = jnp.dot(act, wo_ref[...], preferred_element_type=F32)

        @pl.when(j == 0)
        def _():
            acc[...] = part

        @pl.when((j > 0) & (j < nj - 1))
        def _():
            acc[...] += part

        @pl.when(j == nj - 1)
        def _():
            Y = acc[...] + part
            y_ref[...] = Y
            xo_ref[...] = x_ref[...] + (MACARON_W * _row(v_ref, r_gt)) * Y

    tok = pl.BlockSpec((ts, D), lambda i, j: (i, 0))
    hid = pl.BlockSpec((ts, ch), lambda i, j: (i, j))
    return _pcall(
        body, name=name, grid=(ni, nj),
        in_specs=[tok, pl.BlockSpec(vecs.shape, lambda i, j: (0, 0)),
                  pl.BlockSpec((D, ch), lambda i, j: (0, j)), pl.BlockSpec((D, ch), lambda i, j: (0, j + nj)),
                  pl.BlockSpec((ch, D), lambda i, j: (j, 0))],
        out_specs=[tok, hid, hid, tok],
        out_shape=[jax.ShapeDtypeStruct((S, D), F32), jax.ShapeDtypeStruct((S, Fd), BF),
                   jax.ShapeDtypeStruct((S, Fd), BF), jax.ShapeDtypeStruct((S, D), F32)],
        scratch_shapes=[pltpu.VMEM((ts, D), BF), pltpu.VMEM((ts, D), F32)],
        compiler_params=_params(("arbitrary", "arbitrary"), FFN_FWD_VMEM),
    )(x, vecs, wi, wi, wo)


def ffn_bwd(dxo, x, vecs, gs_, us_, y, wi, wo, rows, name):
    r_sh, r_sc, r_gt, r_g = rows
    S, D = x.shape
    Fd = wo.shape[0]
    ts, ch = min(FFN_BWD_TS, S), FFN_CH
    ni, nj = S // ts, Fd // ch

    def body(dxo_ref, x_ref, v_ref, g_ref, u_ref, y_ref, wg_ref, wu_ref, wo_ref,
             dx_ref, act_ref, dg_ref, du_ref, hb_ref, dyb_ref, vg_ref, dyb_sc, dh_sc):
        i, j = pl.program_id(0), pl.program_id(1)

        @pl.when((i == 0) & (j == 0))
        def _():
            vg_ref[...] = jnp.zeros_like(vg_ref)

        @pl.when(j == 0)
        def _():
            dxo_v = dxo_ref[...]
            dyb = ((MACARON_W * _row(v_ref, r_gt)) * dxo_v).astype(BF)
            dyb_sc[...] = dyb
            dyb_ref[...] = dyb
            vg_ref[0:1, :] += MACARON_W * _colsum(dxo_v * y_ref[...])

        dA = lax.dot_general(dyb_sc[...], wo_ref[...], CONTRACT_LAST, preferred_element_type=F32)
        G = g_ref[...].astype(F32)
        U = u_ref[...].astype(F32)
        sg = _sigmoid(G)
        sl = G * sg
        dU = (dA * sl).astype(BF)
        dG = (dA * U * (sg * (1.0 + G * (1.0 - sg)))).astype(BF)
        act_ref[...] = (sl * U).astype(BF)
        dg_ref[...] = dG
        du_ref[...] = dU
        part = (lax.dot_general(dG, wg_ref[...], CONTRACT_LAST, preferred_element_type=F32)
                + lax.dot_general(dU, wu_ref[...], CONTRACT_LAST, preferred_element_type=F32))

        @pl.when(j == 0)
        def _():
            dh_sc[...] = part

        @pl.when((j > 0) & (j < nj - 1))
        def _():
            dh_sc[...] += part

        @pl.when(j == nj - 1)
        def _():
            dh = dh_sc[...] + part
            xv = x_ref[...]
            r = lax.rsqrt(_rowmean(xv * xv) + EPS)
            n = xv * r
            g = _row(v_ref, r_g)
            sc1 = 1.0 + _row(v_ref, r_sc)
            gsc = g * sc1
            hb_ref[...] = (n * gsc + _row(v_ref, r_sh)).astype(BF)
            dhn = dh * n
            vg_ref[1:2, :] += _colsum(dh)
            vg_ref[2:3, :] += _colsum(dhn) * g
            vg_ref[3:4, :] += _colsum(dhn) * sc1
            dn = dh * gsc
            dx_ref[...] = dxo_ref[...] + r * (dn - n * _rowmean(dn * n))

    tok = pl.BlockSpec((ts, D), lambda i, j: (i, 0))
    hid = pl.BlockSpec((ts, ch), lambda i, j: (i, j))
    return _pcall(
        body, name=name, grid=(ni, nj),
        in_specs=[tok, tok, pl.BlockSpec(vecs.shape, lambda i, j: (0, 0)), hid, hid, tok,
                  pl.BlockSpec((D, ch), lambda i, j: (0, j)), pl.BlockSpec((D, ch), lambda i, j: (0, j + nj)),
                  pl.BlockSpec((ch, D), lambda i, j: (j, 0))],
        out_specs=[tok, hid, hid, hid, tok, tok, pl.BlockSpec((SUBLANES, D), lambda i, j: (0, 0))],
        out_shape=[jax.ShapeDtypeStruct((S, D), F32), jax.ShapeDtypeStruct((S, Fd), BF),
                   jax.ShapeDtypeStruct((S, Fd), BF), jax.ShapeDtypeStruct((S, Fd), BF),
                   jax.ShapeDtypeStruct((S, D), BF), jax.ShapeDtypeStruct((S, D), BF),
                   jax.ShapeDtypeStruct((SUBLANES, D), F32)],
        scratch_shapes=[pltpu.VMEM((ts, D), BF), pltpu.VMEM((ts, D), F32)],
        compiler_params=_params(("arbitrary", "arbitrary")),
    )(dxo, x, vecs, gs_, us_, y, wi, wi, wo)


def final_fwd_bwd(x, tgt, vecs, name):
    S, D = x.shape
    ts = min(512, S)

    def body(x_ref, t_ref, v_ref, dx_ref, vg_ref):
        @pl.when(pl.program_id(0) == 0)
        def _():
            vg_ref[...] = jnp.zeros_like(vg_ref)

        xv = x_ref[...]
        r = lax.rsqrt(_rowmean(xv * xv) + EPS)
        n = xv * r
        g = _row(v_ref, R_GF)
        sc1 = 1.0 + _row(v_ref, R_FSC)
        gsc = g * sc1
        e = n * gsc + _row(v_ref, R_FSH) - t_ref[...]
        vg_ref[3:4, :] += _colsum(e * e)
        dout = e * (1.0 / D)
        dn_ = dout * n
        vg_ref[0:1, :] += _colsum(dout)
        vg_ref[1:2, :] += _colsum(dn_) * g
        vg_ref[2:3, :] += _colsum(dn_) * sc1
        dn = dout * gsc
        dx_ref[...] = r * (dn - n * _rowmean(dn * n))

    tok = pl.BlockSpec((ts, D), lambda i: (i, 0))
    return _pcall(
        body, name=name, grid=(S // ts,),
        in_specs=[tok, tok, pl.BlockSpec(vecs.shape, lambda i: (0, 0))],
        out_specs=[tok, pl.BlockSpec((SUBLANES, D), lambda i: (0, 0))],
        out_shape=[jax.ShapeDtypeStruct((S, D), F32), jax.ShapeDtypeStruct((SUBLANES, D), F32)],
        compiler_params=_params(("arbitrary",)),
    )(x, tgt, vecs)


def norm_matmul(x, vecs, w, rows, name):
    r_sh, r_sc, r_g = rows
    S, D = x.shape
    N = w.shape[1]
    ts, tn = min(512, S), 512
    nn = N // tn

    def body(x_ref, v_ref, w_ref, o_ref, h_sc):
        @pl.when(pl.program_id(1) == 0)
        def _():
            xv = x_ref[...]
            r = lax.rsqrt(_rowmean(xv * xv) + EPS)
            gs = _row(v_ref, r_g) * (1.0 + _row(v_ref, r_sc))
            h_sc[...] = (xv * r * gs + _row(v_ref, r_sh)).astype(BF)

        o_ref[...] = jnp.dot(h_sc[...], w_ref[...], preferred_element_type=F32)

    return _pcall(
        body, name=name, grid=(S // ts, nn),
        in_specs=[pl.BlockSpec((ts, D), lambda i, n: (i, 0)), pl.BlockSpec(vecs.shape, lambda i, n: (0, 0)),
                  pl.BlockSpec((D, tn), lambda i, n: (0, n))],
        out_specs=pl.BlockSpec((ts, tn), lambda i, n: (i, n)),
        out_shape=jax.ShapeDtypeStruct((S, N), F32),
        scratch_shapes=[pltpu.VMEM((ts, D), BF)],
        compiler_params=_params(("arbitrary", "arbitrary")),
    )(x, vecs, w)


SEQ_TT = 256
CONV_PAD = 32


def conv_fwd(proj, convw4, conv_b, name):
    S = proj.shape[0]
    M = conv_b.shape[1]
    nb = M // LANES
    tt = min(SEQ_TT, S)

    def body(uv_ref, ug_ref, w_ref, b_ref, cq_ref, qp):
        qp[0:CONV_PAD, :] = jnp.zeros((CONV_PAD, LANES), F32)

        def step(t, carry):
            base = pl.multiple_of(t * tt, tt)
            qp[pl.ds(base + CONV_PAD, tt), :] = uv_ref[pl.ds(base, tt), :] * _sigmoid(ug_ref[pl.ds(base, tt), :])
            acc = jnp.broadcast_to(b_ref[...], (tt, LANES))
            for k in range(CONV_WIDTH):
                acc = acc + w_ref[k:k + 1, :] * qp[pl.ds(base + (CONV_PAD - CONV_WIDTH + 1) + k, tt), :]
            cq_ref[pl.ds(base, tt), :] = acc
            return carry

        lax.fori_loop(0, S // tt, step, 0)

    return _pcall(
        body, name=name, grid=(nb,),
        in_specs=[pl.BlockSpec((S, LANES), lambda c: (0, c)), pl.BlockSpec((S, LANES), lambda c: (0, c + nb)),
                  pl.BlockSpec((None, 32, LANES), lambda c: (c, 0, 0)), pl.BlockSpec((1, LANES), lambda c: (0, c))],
        out_specs=pl.BlockSpec((S, LANES), lambda c: (0, c)),
        out_shape=jax.ShapeDtypeStruct((S, M), F32),
        scratch_shapes=[pltpu.VMEM((S + CONV_PAD, LANES), F32)],
        compiler_params=_params(("arbitrary",)),
    )(proj, proj, convw4, conv_b)


def conv_bwd(dcq, proj, convw4, name):
    S, M = dcq.shape
    nb = M // LANES
    tt = min(SEQ_TT, S)
    off = CONV_PAD - CONV_WIDTH + 1

    def body(dcq_ref, uv_ref, ug_ref, w_ref, duv_ref, dug_ref, dw_ref, db_ref, qp, dp, dw8, db8):
        qp[0:CONV_PAD, :] = jnp.zeros((CONV_PAD, LANES), F32)
        dp[S:S + CONV_PAD, :] = jnp.zeros((CONV_PAD, LANES), F32)
        dw8[...] = jnp.zeros_like(dw8)
        db8[...] = jnp.zeros_like(db8)

        def fill(t, carry):
            base = pl.multiple_of(t * tt, tt)
            qp[pl.ds(base + CONV_PAD, tt), :] = uv_ref[pl.ds(base, tt), :] * _sigmoid(ug_ref[pl.ds(base, tt), :])
            dp[pl.ds(base, tt), :] = dcq_ref[pl.ds(base, tt), :]
            return carry

        lax.fori_loop(0, S // tt, fill, 0)

        def step(t, carry):
            base = pl.multiple_of(t * tt, tt)
            d_t = dcq_ref[pl.ds(base, tt), :]
            db8[...] += d_t.reshape(tt // SUBLANES, SUBLANES, LANES).sum(axis=0)
            dq = jnp.zeros((tt, LANES), F32)
            for k in range(CONV_WIDTH):
                prod = d_t * qp[pl.ds(base + off + k, tt), :]
                dw8[k] += prod.reshape(tt // SUBLANES, SUBLANES, LANES).sum(axis=0)
                dq = dq + w_ref[k:k + 1, :] * dp[pl.ds(base + (CONV_WIDTH - 1) - k, tt), :]
            uv = uv_ref[pl.ds(base, tt), :]
            sg = _sigmoid(ug_ref[pl.ds(base, tt), :])
            duv_ref[pl.ds(base, tt), :] = dq * sg
            dug_ref[pl.ds(base, tt), :] = dq * uv * sg * (1.0 - sg)
            return carry

        lax.fori_loop(0, S // tt, step, 0)
        dw_ref[...] = jnp.zeros_like(dw_ref)
        for k in range(CONV_WIDTH):
            dw_ref[k:k + 1, :] = _colsum(dw8[k])
        db_ref[...] = _colsum(db8[...])

    col = lambda o: pl.BlockSpec((S, LANES), lambda c: (0, c + o))
    return _pcall(
        body, name=name, grid=(nb,),
        in_specs=[col(0), col(0), col(nb), pl.BlockSpec((None, 32, LANES), lambda c: (c, 0, 0))],
        out_specs=[col(0), col(0), pl.BlockSpec((None, 32, LANES), lambda c: (c, 0, 0)),
                   pl.BlockSpec((1, LANES), lambda c: (0, c))],
        out_shape=[jax.ShapeDtypeStruct((S, M), F32), jax.ShapeDtypeStruct((S, M), F32),
                   jax.ShapeDtypeStruct((nb, 32, LANES), F32), jax.ShapeDtypeStruct((1, M), F32)],
        scratch_shapes=[pltpu.VMEM((S + CONV_PAD, LANES), F32), pltpu.VMEM((S + CONV_PAD, LANES), F32),
                        pltpu.VMEM((32, SUBLANES, LANES), F32), pltpu.VMEM((SUBLANES, LANES), F32)],
        compiler_params=_params(("arbitrary",)),
    )(dcq, proj, proj, convw4)


def _log_sigmoid(x):
    return jnp.minimum(x, 0.0) - jnp.log(1.0 + jnp.exp(-jnp.abs(x)))


def _rg_gate_terms(ra, ls):
    la = RG_C * ra * ls
    a = jnp.exp(la)
    th = jnp.tanh(la)
    mult = jnp.sqrt(-2.0 * th / (1.0 - th))
    return a, mult


def rnn_fwd(proj, rnnw4, rnn_b, bda, bdi, b_a, b_i, lam, name):
    S = proj.shape[0]
    M = rnn_b.shape[1]
    nb = M // LANES
    tt = min(SEQ_TT, S)
    KW = RNN_CONV_WIDTH

    def body(ux_ref, w_ref, rb_ref, bda_ref, bdi_ref, ba_ref, bi_ref, lam_ref,
             xr_ref, ra_ref, ii_ref, h_ref, uxp, a_sc, b_sc):
        uxp[0:SUBLANES, :] = jnp.zeros((SUBLANES, LANES), F32)
        ls = _log_sigmoid(lam_ref[...])

        def step(t, carry):
            base = pl.multiple_of(t * tt, tt)
            uxp[pl.ds(base + SUBLANES, tt), :] = ux_ref[pl.ds(base, tt), :]
            xr = jnp.broadcast_to(rb_ref[...], (tt, LANES))
            for k in range(KW):
                xr = xr + w_ref[k:k + 1, :] * uxp[pl.ds(base + (SUBLANES - KW + 1) + k, tt), :]
            xb = xr.astype(BF)
            ra = _sigmoid(jnp.dot(xb, bda_ref[...], preferred_element_type=F32) + ba_ref[...])
            ii = _sigmoid(jnp.dot(xb, bdi_ref[...], preferred_element_type=F32) + bi_ref[...])
            a, mult = _rg_gate_terms(ra, ls)
            xr_ref[pl.ds(base, tt), :] = xr
            ra_ref[pl.ds(base, tt), :] = ra
            ii_ref[pl.ds(base, tt), :] = ii
            a_sc[pl.ds(base, tt), :] = a
            b_sc[pl.ds(base, tt), :] = mult * (ii * xr)
            return carry

        lax.fori_loop(0, S // tt, step, 0)

        rows = lax.broadcasted_iota(I32, (SUBLANES, LANES), 0)

        def scan(t, hprev):
            base = pl.multiple_of(t * SUBLANES, SUBLANES)
            A = a_sc[pl.ds(base, SUBLANES), :]
            B = b_sc[pl.ds(base, SUBLANES), :]
            for d in (1, 2, 4):
                As = jnp.where(rows >= d, pltpu.roll(A, d, axis=0), 1.0)
                Bs = jnp.where(rows >= d, pltpu.roll(B, d, axis=0), 0.0)
                B = A * Bs + B
                A = A * As
            hh = B + A * hprev
            h_ref[pl.ds(base, SUBLANES), :] = hh
            return jnp.broadcast_to(hh[SUBLANES - 1:SUBLANES, :], (SUBLANES, LANES))

        lax.fori_loop(0, S // SUBLANES, scan, jnp.zeros((SUBLANES, LANES), F32))

    col = lambda o: pl.BlockSpec((S, LANES), lambda c: (0, c + o))
    vec = pl.BlockSpec((1, LANES), lambda c: (0, c))
    diag = pl.BlockSpec((LANES, LANES), lambda c: (c, c))
    return _pcall(
        body, name=name, grid=(nb,),
        in_specs=[col(2 * nb), pl.BlockSpec((None, SUBLANES, LANES), lambda c: (c, 0, 0)), vec, diag, diag, vec, vec, vec],
        out_specs=[col(0)] * 4,
        out_shape=[jax.ShapeDtypeStruct((S, M), F32)] * 4,
        scratch_shapes=[pltpu.VMEM((S + SUBLANES, LANES), F32), pltpu.VMEM((S, LANES), F32), pltpu.VMEM((S, LANES), F32)],
        compiler_params=_params(("arbitrary",)),
    )(proj, rnnw4, rnn_b, bda, bdi, b_a, b_i, lam)


def rnn_bwd(dhout, h, xr, ra, ii, proj, rnnw4, bda, bdi, lam, name):
    S, M = h.shape
    nb = M // LANES
    tt = min(SEQ_TT, S)
    KW = RNN_CONV_WIDTH
    SL = SUBLANES

    def body(dh_ref, h_ref, xr_ref, ra_ref, ii_ref, ux_ref, w_ref, bda_ref, bdi_ref, lam_ref,
             dux_ref, dwa_ref, dwi_ref, drw_ref, vec_ref,
             a_sc, hp, g_sc, dpa_sc, dpi_sc, dxp, uxp, acc8, drw8):
        zero8 = jnp.zeros((SL, LANES), F32)
        a_sc[S:S + SL, :] = zero8
        hp[0:SL, :] = zero8
        dxp[S:S + SL, :] = zero8
        uxp[0:SL, :] = zero8
        acc8[...] = jnp.zeros_like(acc8)
        drw8[...] = jnp.zeros_like(drw8)
        lamv = lam_ref[...]
        ls = _log_sigmoid(lamv)

        def fill(t, carry):
            base = pl.multiple_of(t * tt, tt)
            a_sc[pl.ds(base, tt), :] = jnp.exp(RG_C * ra_ref[pl.ds(base, tt), :] * ls)
            hp[pl.ds(base + SL, tt), :] = h_ref[pl.ds(base, tt), :]
            uxp[pl.ds(base + SL, tt), :] = ux_ref[pl.ds(base, tt), :]
            return carry

        lax.fori_loop(0, S // tt, fill, 0)

        rows = lax.broadcasted_iota(I32, (SL, LANES), 0)
        nt8 = S // SL

        def rscan(t, gnext):
            base = pl.multiple_of((nt8 - 1 - t) * SL, SL)
            A = a_sc[pl.ds(base + 1, SL), :]
            B = dh_ref[pl.ds(base, SL), :]
            for d in (1, 2, 4):
                As = jnp.where(rows < SL - d, pltpu.roll(A, SL - d, axis=0), 1.0)
                Bs = jnp.where(rows < SL - d, pltpu.roll(B, SL - d, axis=0), 0.0)
                B = A * Bs + B
                A = A * As
            g = B + A * gnext
            g_sc[pl.ds(base, SL), :] = g
            return jnp.broadcast_to(g[0:1, :], (SL, LANES))

        lax.fori_loop(0, nt8, rscan, zero8)

        def red8(v):
            return v.reshape(tt // SL, SL, LANES).sum(axis=0)

        def step(t, carry):
            base = pl.multiple_of(t * tt, tt)
            g = g_sc[pl.ds(base, tt), :]
            hprev = hp[pl.ds(base + SL - 1, tt), :]
            xr_t = xr_ref[pl.ds(base, tt), :]
            ra_t = ra_ref[pl.ds(base, tt), :]
            ii_t = ii_ref[pl.ds(base, tt), :]
            a, mult = _rg_gate_terms(ra_t, ls)
            gx = g * xr_t
            dmult = gx * ii_t
            dii = gx * mult
            dxr = g * (mult * ii_t)
            dla = g * hprev * a - dmult * (a * a) / mult
            acc8[3] += red8(dla * ra_t)
            dpa = dla * (RG_C * ls) * ra_t * (1.0 - ra_t)
            dpi = dii * ii_t * (1.0 - ii_t)
            dpab = dpa.astype(BF)
            dpib = dpi.astype(BF)
            dxr = dxr + (lax.dot_general(dpab, bda_ref[...], CONTRACT_LAST, preferred_element_type=F32)
                         + lax.dot_general(dpib, bdi_ref[...], CONTRACT_LAST, preferred_element_type=F32))
            dpa_sc[pl.ds(base, tt), :] = dpab
            dpi_sc[pl.ds(base, tt), :] = dpib
            dxp[pl.ds(base, tt), :] = dxr
            acc8[0] += red8(dxr)
            acc8[1] += red8(dpa)
            acc8[2] += red8(dpi)
            return carry

        lax.fori_loop(0, S // tt, step, 0)

        def convb(t, carry):
            base = pl.multiple_of(t * tt, tt)
            d_t = dxp[pl.ds(base, tt), :]
            dux = jnp.zeros((tt, LANES), F32)
            for k in range(KW):
                drw8[k] += red8(d_t * uxp[pl.ds(base + (SL - KW + 1) + k, tt), :])
                dux = dux + w_ref[k:k + 1, :] * dxp[pl.ds(base + (KW - 1) - k, tt), :]
            dux_ref[pl.ds(base, tt), :] = dux
            return carry

        lax.fori_loop(0, S // tt, convb, 0)

        xb = xr_ref[...].astype(BF)
        dwa_ref[...] = lax.dot_general(xb, dpa_sc[...], CONTRACT_FIRST, preferred_element_type=F32)
        dwi_ref[...] = lax.dot_general(xb, dpi_sc[...], CONTRACT_FIRST, preferred_element_type=F32)
        drw_ref[...] = jnp.zeros_like(drw_ref)
        vec_ref[...] = jnp.zeros_like(vec_ref)
        for k in range(KW):
            drw_ref[k:k + 1, :] = _colsum(drw8[k])
        for k in range(3):
            vec_ref[k:k + 1, :] = _colsum(acc8[k])
        vec_ref[3:4, :] = _colsum(acc8[3]) * (RG_C * _sigmoid(-lamv))

    col = lambda o: pl.BlockSpec((S, LANES), lambda c: (0, c + o))
    vec = pl.BlockSpec((1, LANES), lambda c: (0, c))
    diag = pl.BlockSpec((LANES, LANES), lambda c: (c, c))
    blk3 = lambda r: pl.BlockSpec((None, r, LANES), lambda c: (c, 0, 0))
    return _pcall(
        body, name=name, grid=(nb,),
        in_specs=[col(0), col(0), col(0), col(0), col(0), col(2 * nb), blk3(SL), diag, diag, vec],
        out_specs=[col(0), blk3(LANES), blk3(LANES), blk3(SL), pl.BlockSpec((SL, LANES), lambda c: (0, c))],
        out_shape=[jax.ShapeDtypeStruct((S, M), F32), jax.ShapeDtypeStruct((nb, LANES, LANES), F32),
                   jax.ShapeDtypeStruct((nb, LANES, LANES), F32), jax.ShapeDtypeStruct((nb, SL, LANES), F32),
                   jax.ShapeDtypeStruct((SL, M), F32)],
        scratch_shapes=[pltpu.VMEM((S + SL, LANES), F32), pltpu.VMEM((S + SL, LANES), F32), pltpu.VMEM((S, LANES), F32),
                        pltpu.VMEM((S, LANES), BF), pltpu.VMEM((S, LANES), BF), pltpu.VMEM((S + SL, LANES), F32),
                        pltpu.VMEM((S + SL, LANES), F32), pltpu.VMEM((SL, SL, LANES), F32), pltpu.VMEM((SL, SL, LANES), F32)],
        compiler_params=_params(("arbitrary",)),
    )(dhout, h, xr, ra, ii, proj, rnnw4, bda, bdi, lam)


GELU_K = 0.7978845608028654
GELU_C = 0.044715


def _layernorm_parts(cq):
    mu = _rowmean(cq)
    d = cq - mu
    rstd = lax.rsqrt(_rowmean(d * d) + EPS)
    return d * rstd, rstd


def mix_out(cq, proj, h, x, vecs, lnv, wout, name):
    S, D = x.shape
    M = cq.shape[1]
    ts = min(512, S)

    def body(cq_ref, uy_ref, h_ref, x_ref, v_ref, ln_ref, w_ref, xo_ref, ym_ref, yc_ref):
        z, _ = _layernorm_parts(cq_ref[...])
        l = z * _row(ln_ref, 0) + _row(ln_ref, 1)
        yc_ref[:, 0:M] = (l * _sigmoid(l)).astype(BF)
        uy = uy_ref[...]
        gelu = 0.5 * uy * (1.0 + jnp.tanh(GELU_K * (uy + GELU_C * uy * uy * uy)))
        yc_ref[:, M:2 * M] = (gelu * h_ref[...]).astype(BF)
        ym = jnp.dot(yc_ref[...], w_ref[...], preferred_element_type=F32)
        ym_ref[...] = ym
        xo_ref[...] = x_ref[...] + _row(v_ref, R_GT2) * ym

    tok = pl.BlockSpec((ts, D), lambda i: (i, 0))
    mtok = lambda o: pl.BlockSpec((ts, M), lambda i: (i, o))
    return _pcall(
        body, name=name, grid=(S // ts,),
        in_specs=[mtok(0), mtok(3), mtok(0), tok, pl.BlockSpec(vecs.shape, lambda i: (0, 0)),
                  pl.BlockSpec(lnv.shape, lambda i: (0, 0)), pl.BlockSpec(wout.shape, lambda i: (0, 0))],
        out_specs=[tok, tok, pl.BlockSpec((ts, 2 * M), lambda i: (i, 0))],
        out_shape=[jax.ShapeDtypeStruct((S, D), F32), jax.ShapeDtypeStruct((S, D), F32),
                   jax.ShapeDtypeStruct((S, 2 * M), BF)],
        compiler_params=_params(("arbitrary",)),
    )(cq, proj, h, x, vecs, lnv, wout)


def mix_out_bwd(dxo, ym, vecs, wout, cq, lnv, proj, h, name):
    S, D = dxo.shape
    M = cq.shape[1]
    ts = min(512, S)

    def body(dxo_ref, ym_ref, v_ref, w_ref, cq_ref, ln_ref, uy_ref, h_ref,
             dcq_ref, dh_ref, duy_ref, dyb_ref, vgd_ref, vgm_ref):
        @pl.when(pl.program_id(0) == 0)
        def _():
            vgd_ref[...] = jnp.zeros_like(vgd_ref)
            vgm_ref[...] = jnp.zeros_like(vgm_ref)

        dxo_v = dxo_ref[...]
        dyb = (_row(v_ref, R_GT2) * dxo_v).astype(BF)
        dyb_ref[...] = dyb
        vgd_ref[0:1, :] += _colsum(dxo_v * ym_ref[...])
        dycat = lax.dot_general(dyb, w_ref[...], CONTRACT_LAST, preferred_element_type=F32)
        dyc = dycat[:, 0:M]
        dyr = dycat[:, M:2 * M]
        z, rstd = _layernorm_parts(cq_ref[...])
        lng = _row(ln_ref, 0)
        l = z * lng + _row(ln_ref, 1)
        sl = _sigmoid(l)
        dl = dyc * (sl * (1.0 + l * (1.0 - sl)))
        vgm_ref[0:1, :] += _colsum(dl * z)
        vgm_ref[1:2, :] += _colsum(dl)
        dz = dl * lng
        dcq_ref[...] = rstd * (dz - _rowmean(dz) - z * _rowmean(dz * z))
        uy = uy_ref[...]
        u2 = uy * uy
        th = jnp.tanh(GELU_K * (uy + GELU_C * uy * u2))
        gelu = 0.5 * uy * (1.0 + th)
        dgelu = 0.5 * (1.0 + th) + 0.5 * uy * (1.0 - th * th) * (GELU_K * (1.0 + 3.0 * GELU_C * u2))
        dh_ref[...] = dyr * gelu
        duy_ref[...] = dyr * h_ref[...] * dgelu

    tok = pl.BlockSpec((ts, D), lambda i: (i, 0))
    mtok = lambda o: pl.BlockSpec((ts, M), lambda i: (i, o))
    return _pcall(
        body, name=name, grid=(S // ts,),
        in_specs=[tok, tok, pl.BlockSpec(vecs.shape, lambda i: (0, 0)), pl.BlockSpec(wout.shape, lambda i: (0, 0)),
                  mtok(0), pl.BlockSpec(lnv.shape, lambda i: (0, 0)), mtok(3), mtok(0)],
        out_specs=[mtok(0), mtok(0), mtok(0), tok, pl.BlockSpec((SUBLANES, D), lambda i: (0, 0)),
                   pl.BlockSpec((SUBLANES, M), lambda i: (0, 0))],
        out_shape=[jax.ShapeDtypeStruct((S, M), F32)] * 3 + [jax.ShapeDtypeStruct((S, D), BF),
                   jax.ShapeDtypeStruct((SUBLANES, D), F32), jax.ShapeDtypeStruct((SUBLANES, M), F32)],
        compiler_params=_params(("arbitrary",)),
    )(dxo, ym, vecs, wout, cq, lnv, proj, h)


def mix_in_bwd(dparts, x, dxo, vecs, win, name):
    S, D = x.shape
    M = dparts[0].shape[1]
    ts = min(512, S)

    def body(d0, d1, d2, d3, x_ref, dxo_ref, v_ref, w_ref, dx_ref, hb_ref, dp_ref, vg_ref):
        @pl.when(pl.program_id(0) == 0)
        def _():
            vg_ref[...] = jnp.zeros_like(vg_ref)

        for q, dref in enumerate((d0, d1, d2, d3)):
            dp_ref[:, q * M:(q + 1) * M] = dref[...].astype(BF)
        dh = lax.dot_general(dp_ref[...], w_ref[...], CONTRACT_LAST, preferred_element_type=F32)
        xv = x_ref[...]
        r = lax.rsqrt(_rowmean(xv * xv) + EPS)
        n = xv * r
        g = _row(v_ref, R_G2)
        sc1 = 1.0 + _row(v_ref, R_SC2)
        gsc = g * sc1
        hb_ref[...] = (n * gsc + _row(v_ref, R_SH2)).astype(BF)
        dhn = dh * n
        vg_ref[0:1, :] += _colsum(dh)
        vg_ref[1:2, :] += _colsum(dhn) * g
        vg_ref[2:3, :] += _colsum(dhn) * sc1
        dn = dh * gsc
        dx_ref[...] = dxo_ref[...] + r * (dn - n * _rowmean(dn * n))

    tok = pl.BlockSpec((ts, D), lambda i: (i, 0))
    mtok = pl.BlockSpec((ts, M), lambda i: (i, 0))
    return _pcall(
        body, name=name, grid=(S // ts,),
        in_specs=[mtok] * 4 + [tok, tok, pl.BlockSpec(vecs.shape, lambda i: (0, 0)), pl.BlockSpec(win.shape, lambda i: (0, 0))],
        out_specs=[tok, tok, pl.BlockSpec((ts, 4 * M), lambda i: (i, 0)), pl.BlockSpec((SUBLANES, D), lambda i: (0, 0))],
        out_shape=[jax.ShapeDtypeStruct((S, D), F32), jax.ShapeDtypeStruct((S, D), BF),
                   jax.ShapeDtypeStruct((S, 4 * M), BF), jax.ShapeDtypeStruct((SUBLANES, D), F32)],
        compiler_params=_params(("arbitrary",)),
    )(*dparts, x, dxo, vecs, win)


def _adamw(w, g, m, v):
    m = ADAM_B1 * m + (1.0 - ADAM_B1) * g
    v = ADAM_B2 * v + (1.0 - ADAM_B2) * (g * g)
    m_hat = m / (1.0 - ADAM_B1 ** ADAM_STEP)
    v_hat = v / (1.0 - ADAM_B2 ** ADAM_STEP)
    delta = -ADAM_LR * (m_hat / (jnp.sqrt(v_hat) + ADAM_EPS) + ADAM_WD * w)
    return delta, m, v


def adam_big(w, g, m, v, name):
    R, C = w.shape
    tr = 256 if R % 256 == 0 else R // 2 if (R // 2) % SUBLANES == 0 and R > 512 else R
    tc = C if C <= 1536 else (1152 if C % 1152 == 0 else 1024)
    assert R % tr == 0 and C % tc == 0

    def body(w_ref, g_ref, m_ref, v_ref, d_ref, nm_ref, nv_ref):
        d, nm, nv = _adamw(w_ref[...], g_ref[...], m_ref[...], v_ref[...])
        d_ref[...] = d
        nm_ref[...] = nm
        nv_ref[...] = nv

    blk = pl.BlockSpec((tr, tc), lambda i, j: (i, j))
    return _pcall(
        body, name=name, grid=(R // tr, C // tc), in_specs=[blk] * 4, out_specs=[blk] * 3,
        out_shape=[jax.ShapeDtypeStruct((R, C), F32)] * 3, compiler_params=_params(("parallel", "parallel")),
    )(w, g, m, v)


def adam_cond(c_all, dmod, w, m, v, name):
    B, Kin = c_all.shape
    N = w.shape[1]
    tn = 256
    assert N % tn == 0

    def body(c_ref, d_ref, w_ref, m_ref, v_ref, g_ref, dl_ref, nm_ref, nv_ref):
        cv = c_ref[...]
        ca = cv * _sigmoid(cv)
        g = lax.dot_general(ca, d_ref[...], CONTRACT_FIRST, preferred_element_type=F32, precision=lax.Precision.HIGHEST)
        d, nm, nv = _adamw(w_ref[...], g, m_ref[...], v_ref[...])
        g_ref[...] = g
        dl_ref[...] = d
        nm_ref[...] = nm
        nv_ref[...] = nv

    blk = pl.BlockSpec((Kin, tn), lambda n: (0, n))
    return _pcall(
        body, name=name, grid=(N // tn,),
        in_specs=[pl.BlockSpec((B, Kin), lambda n: (0, 0)), pl.BlockSpec((B, tn), lambda n: (0, n)), blk, blk, blk],
        out_specs=[blk] * 4, out_shape=[jax.ShapeDtypeStruct((Kin, N), F32)] * 4,
        compiler_params=_params(("parallel",)),
    )(c_all, dmod, w, m, v)


def adam_small(ws, gs, ms, vs, name):
    n = len(ws)

    def body(*refs):
        ins, outs = refs[:4 * n], refs[4 * n:]
        for k in range(n):
            d, nm, nv = _adamw(ins[k][...], ins[n + k][...], ins[2 * n + k][...], ins[3 * n + k][...])
            outs[k][...] = d
            outs[n + k][...] = nm
            outs[2 * n + k][...] = nv

    vm = pl.BlockSpec(memory_space=pltpu.VMEM)
    shapes = [jax.ShapeDtypeStruct(w.shape, F32) for w in ws]
    out = _pcall(body, name=name, in_specs=[vm] * (4 * n), out_specs=[vm] * (3 * n), out_shape=shapes * 3,
                 compiler_params=_params())(*ws, *gs, *ms, *vs)
    return out[:n], out[n:2 * n], out[2 * n:]


def _me():
    return lax.axis_index("x"), lax.axis_index("y"), lax.axis_index("c")


def _flip(x, y, p):
    return (x ^ (p >> 1) if (p >> 1) else x), (y ^ (p & 1) if (p & 1) else y)


def _handshake(peers):
    barrier = pltpu.get_barrier_semaphore()
    for peer in peers:
        pl.semaphore_signal(barrier, inc=1, device_id=peer, device_id_type=MESH)
    pl.semaphore_wait(barrier, len(peers))


def _seq_call(body, *, name, n_in, out_shape, sem_shapes, collective_id):
    del n_in
    return pl.kernel(body, out_type=out_shape, mesh=plsc.ScalarSubcoreMesh(axis_name="sq", num_cores=1), name=name,
                     scratch_types=sem_shapes, compiler_params=pltpu.CompilerParams(collective_id=collective_id))


def _hbm_comm_call(body, *, name, n_in, out_shape, sem_shapes, seq_id):
    if seq_id is not None:
        return _seq_call(body, name=name, n_in=n_in, out_shape=out_shape, sem_shapes=sem_shapes, collective_id=seq_id)
    anyspec = pl.BlockSpec(memory_space=pl.ANY)
    return _pcall(body, name=name, in_specs=[anyspec] * n_in, out_specs=[anyspec] * len(out_shape), out_shape=out_shape,
                  scratch_shapes=sem_shapes, compiler_params=_params())


def allgather_devices(v, name, with_sum=False):
    R, L = v.shape

    def body(v_ref, out_ref, *rest):
        if with_sum:
            sum_ref, send_sems, recv_sems = rest
        else:
            send_sems, recv_sems = rest
        x, y, c = _me()
        me = 4 * x + 2 * y + c
        out_ref[me] = v_ref[...]
        copies = []
        for p in range(1, N_DEV):
            px, py = _flip(x, y, p >> 1)
            pc = (1 - c) if (p & 1) else c
            peer = 4 * px + 2 * py + pc
            send = pltpu.make_async_remote_copy(src_ref=v_ref, dst_ref=out_ref.at[me], send_sem=send_sems.at[p - 1],
                                                recv_sem=recv_sems.at[p - 1], device_id=(px, py, pc), device_id_type=MESH)
            send.start()
            recv = pltpu.make_async_remote_copy(src_ref=v_ref, dst_ref=out_ref.at[peer], send_sem=send_sems.at[p - 1],
                                                recv_sem=recv_sems.at[p - 1], device_id=(px, py, pc), device_id_type=MESH)
            copies.append((send, recv))
        for send, recv in copies:
            recv.wait_recv()
        for send, recv in copies:
            send.wait_send()
        if with_sum:
            s = out_ref[0]
            for k in range(1, N_DEV):
                s = s + out_ref[k]
            sum_ref[...] = s

    vm = pl.BlockSpec(memory_space=pltpu.VMEM)
    out_shape = [jax.ShapeDtypeStruct((N_DEV, R, L), F32)]
    if with_sum:
        out_shape.append(jax.ShapeDtypeStruct((R, L), F32))
    return _pcall(
        body, name=name, in_specs=[vm], out_specs=[vm] * len(out_shape), out_shape=out_shape,
        scratch_shapes=[pltpu.SemaphoreType.DMA((N_DEV - 1,)), pltpu.SemaphoreType.DMA((N_DEV - 1,))],
        compiler_params=_params(),
    )(v)


def allgather_devices_hbm(v, name, seq_id):
    R, L = v.shape

    def body(v_ref, out_ref, send_sems, recv_sems, local_sem):
        x, y, c = _me()
        me = 4 * x + 2 * y + c
        peers = []
        for p in range(1, N_DEV):
            px, py = _flip(x, y, p >> 1)
            peers.append((px, py, (1 - c) if (p & 1) else c))
        _handshake(peers)
        lc = pltpu.make_async_copy(v_ref, out_ref.at[me], local_sem)
        lc.start()
        copies = []
        for p, (px, py, pc) in enumerate(peers):
            send = pltpu.make_async_remote_copy(src_ref=v_ref, dst_ref=out_ref.at[me], send_sem=send_sems.at[p],
                                                recv_sem=recv_sems.at[p], device_id=(px, py, pc), device_id_type=MESH)
            send.start()
            recv = pltpu.make_async_remote_copy(src_ref=v_ref, dst_ref=out_ref.at[4 * px + 2 * py + pc], send_sem=send_sems.at[p],
                                                recv_sem=recv_sems.at[p], device_id=(px, py, pc), device_id_type=MESH)
            copies.append((send, recv))
        for send, recv in copies:
            recv.wait_recv()
        for send, recv in copies:
            send.wait_send()
        lc.wait()

    return _seq_call(body, name=name, n_in=1, out_shape=[jax.ShapeDtypeStruct((N_DEV, R, L), F32)],
                     sem_shapes=[pltpu.SemaphoreType.DMA((N_DEV - 1,)), pltpu.SemaphoreType.DMA((N_DEV - 1,)),
                                 pltpu.SemaphoreType.DMA], collective_id=seq_id)(v)[0]


def sum_slots(g, name):
    n, R, L = g.shape
    tr = 216 if R % 216 == 0 else R
    assert R % tr == 0 and tr % SUBLANES == 0

    def body(g_ref, o_ref):
        s = g_ref[0]
        for k in range(1, n):
            s = s + g_ref[k]
        o_ref[...] = s

    return _pcall(body, name=name, grid=(R // tr,), in_specs=[pl.BlockSpec((n, tr, L), lambda i: (0, i, 0))],
                  out_specs=pl.BlockSpec((tr, L), lambda i: (i, 0)), out_shape=jax.ShapeDtypeStruct((R, L), F32),
                  compiler_params=_params(("parallel",)))(g)


def allgather_chips(v, name):
    R, L = v.shape

    def body(v_ref, out_ref, send_sems, recv_sems):
        x, y, c = _me()
        chip = 2 * x + y
        out_ref[chip] = v_ref[...]
        copies = []
        for p in range(1, N_CHIPS):
            px, py = _flip(x, y, p)
            send = pltpu.make_async_remote_copy(src_ref=v_ref, dst_ref=out_ref.at[chip], send_sem=send_sems.at[p - 1],
                                                recv_sem=recv_sems.at[p - 1], device_id=(px, py, c), device_id_type=MESH)
            send.start()
            recv = pltpu.make_async_remote_copy(src_ref=v_ref, dst_ref=out_ref.at[2 * px + py], send_sem=send_sems.at[p - 1],
                                                recv_sem=recv_sems.at[p - 1], device_id=(px, py, c), device_id_type=MESH)
            copies.append((send, recv))
        for send, recv in copies:
            recv.wait_recv()
        for send, recv in copies:
            send.wait_send()

    vm = pl.BlockSpec(memory_space=pltpu.VMEM)
    return _pcall(
        body, name=name, in_specs=[vm], out_specs=vm, out_shape=jax.ShapeDtypeStruct((N_CHIPS, R, L), F32),
        scratch_shapes=[pltpu.SemaphoreType.DMA((N_CHIPS - 1,)), pltpu.SemaphoreType.DMA((N_CHIPS - 1,))],
        compiler_params=_params(),
    )(v)


def _shard_window(ref, kind, shard_shape, chip, half):
    r, c = shard_shape
    hr = r // 2
    if kind == "col":
        return ref.at[pl.ds(pl.multiple_of(half * hr, hr), hr), pl.ds(pl.multiple_of(chip * c, c), c)]
    return ref.at[pl.ds(pl.multiple_of(chip * r + half * hr, hr), hr), :]


def allgather_weights(shards, kinds, name, seq_id=None):
    n = len(shards)
    fulls = []
    for s, kind in zip(shards, kinds):
        r, c = s.shape
        fulls.append(jax.ShapeDtypeStruct((r, N_CHIPS * c) if kind == "col" else (N_CHIPS * r, c), s.dtype))

    def body(*refs):
        srcs, outs = refs[:n], refs[n:2 * n]
        send_sems, recv_sems, fsend_sems, frecv_sems, local_sems = refs[2 * n:]
        x, y, c = _me()
        chip = 2 * x + y
        sib = (x, y, 1 - c)
        if seq_id is not None:
            _handshake([(*_flip(x, y, p), c) for p in range(1, N_CHIPS)] + [sib])
        locals_, sends, fwds = [], [], []
        for i in range(n):
            shp = srcs[i].shape
            hr = shp[0] // 2
            win_all = (outs[i].at[:, pl.ds(pl.multiple_of(chip * shp[1], shp[1]), shp[1])] if kinds[i] == "col"
                       else outs[i].at[pl.ds(pl.multiple_of(chip * shp[0], shp[0]), shp[0]), :])
            lc = pltpu.make_async_copy(srcs[i], win_all, local_sems.at[i])
            lc.start()
            locals_.append(lc)
            my_half = srcs[i].at[pl.ds(pl.multiple_of(c * hr, hr), hr), :]
            for p in range(1, N_CHIPS):
                px, py = _flip(x, y, p)
                k = i * (N_CHIPS - 1) + p - 1
                cp = pltpu.make_async_remote_copy(src_ref=my_half, dst_ref=_shard_window(outs[i], kinds[i], shp, chip, c),
                                                  send_sem=send_sems.at[k], recv_sem=recv_sems.at[k],
                                                  device_id=(px, py, c), device_id_type=MESH)
                cp.start()
                sends.append(cp)
        for i in range(n):
            shp = srcs[i].shape
            for p in range(1, N_CHIPS):
                px, py = _flip(x, y, p)
                k = i * (N_CHIPS - 1) + p - 1
                landed = _shard_window(outs[i], kinds[i], shp, 2 * px + py, c)
                pltpu.make_async_remote_copy(src_ref=landed, dst_ref=landed, send_sem=send_sems.at[k], recv_sem=recv_sems.at[k],
                                             device_id=(px, py, c), device_id_type=MESH).wait_recv()
                fw = pltpu.make_async_remote_copy(src_ref=landed, dst_ref=landed, send_sem=fsend_sems.at[k],
                                                  recv_sem=frecv_sems.at[k], device_id=sib, device_id_type=MESH)
                fw.start()
                fwds.append(fw)
        for i in range(n):
            shp = srcs[i].shape
            for p in range(1, N_CHIPS):
                px, py = _flip(x, y, p)
                k = i * (N_CHIPS - 1) + p - 1
                other = _shard_window(outs[i], kinds[i], shp, 2 * px + py, 1 - c)
                pltpu.make_async_remote_copy(src_ref=other, dst_ref=other, send_sem=fsend_sems.at[k], recv_sem=frecv_sems.at[k],
                                             device_id=sib, device_id_type=MESH).wait_recv()
        for cp in sends + fwds:
            cp.wait_send()
        for lc in locals_:
            lc.wait()

    nk = n * (N_CHIPS - 1)
    return _hbm_comm_call(
        body, name=name, n_in=n, out_shape=fulls, seq_id=seq_id,
        sem_shapes=[pltpu.SemaphoreType.DMA((nk,)), pltpu.SemaphoreType.DMA((nk,)), pltpu.SemaphoreType.DMA((nk,)),
                    pltpu.SemaphoreType.DMA((nk,)), pltpu.SemaphoreType.DMA((n,))],
    )(*shards)


def _as_halves(g, kind, shard_shape):
    r, c = shard_shape
    if kind == "col":
        return g.reshape(2, r // 2, N_CHIPS * c)
    return g.reshape(N_CHIPS, 2, r // 2, c)


def exchange_sibling_halves(grads, kinds, shard_shapes, name, seq_id=None):
    n = len(grads)
    views = [_as_halves(g, k, s) for g, k, s in zip(grads, kinds, shard_shapes)]
    outs = []
    for k, (r, c) in zip(kinds, shard_shapes):
        outs.append(jax.ShapeDtypeStruct((r // 2, N_CHIPS * c) if k == "col" else (N_CHIPS, r // 2, c), F32))

    def body(*refs):
        srcs, dsts = refs[:n], refs[n:2 * n]
        send_sems, recv_sems = refs[2 * n:]
        x, y, c = _me()
        if seq_id is not None:
            _handshake([(x, y, 1 - c)])
        cps = []
        for i in range(n):
            src = srcs[i].at[1 - c] if kinds[i] == "col" else srcs[i].at[:, 1 - c]
            cp = pltpu.make_async_remote_copy(src_ref=src, dst_ref=dsts[i], send_sem=send_sems.at[i], recv_sem=recv_sems.at[i],
                                              device_id=(x, y, 1 - c), device_id_type=MESH)
            cp.start()
            cps.append(cp)
        for cp in cps:
            cp.wait_recv()
        for cp in cps:
            cp.wait_send()

    return _hbm_comm_call(body, name=name, n_in=n, out_shape=outs, seq_id=seq_id,
                          sem_shapes=[pltpu.SemaphoreType.DMA((n,)), pltpu.SemaphoreType.DMA((n,))])(*views)


def add_sibling_half(g, recv, kind, shard_shape, cidx, name):
    r, c = shard_shape
    hr = r // 2
    gv = _as_halves(g, kind, shard_shape)
    tr = hr if hr <= 512 else (256 if hr % 256 == 0 else hr // 2)
    assert hr % tr == 0

    def body(ci_ref, g_ref, r_ref, h_ref, hb_ref):
        s = g_ref[...] + r_ref[...]
        h_ref[...] = s
        hb_ref[...] = s.astype(BF)

    if kind == "col":
        grid = (hr // tr, N_CHIPS)
        g_spec = pl.BlockSpec((None, tr, c), lambda i, k, ci: (ci[0], i, k))
        o_spec = pl.BlockSpec((tr, c), lambda i, k, ci: (i, k))
    else:
        grid = (hr // tr, N_CHIPS)
        g_spec = pl.BlockSpec((None, None, tr, c), lambda i, k, ci: (k, ci[0], i, 0))
        o_spec = pl.BlockSpec((None, tr, c), lambda i, k, ci: (k, i, 0))
    gs = pltpu.PrefetchScalarGridSpec(num_scalar_prefetch=1, grid=grid, in_specs=[g_spec, o_spec], out_specs=[o_spec, o_spec])
    return _pcall(
        body, name=name, grid_spec=gs,
        out_shape=[jax.ShapeDtypeStruct(recv.shape, F32), jax.ShapeDtypeStruct(recv.shape, BF)],
        compiler_params=_params(("parallel", "parallel")),
    )(cidx, gv, recv)


def exchange_chip_pieces(hbs, kinds, shard_shapes, name, seq_id=None):
    n = len(hbs)
    outs = [jax.ShapeDtypeStruct((N_CHIPS - 1, r // 2, c), BF) for (r, c) in shard_shapes]

    def body(*refs):
        srcs, dsts = refs[:n], refs[n:2 * n]
        send_sems, recv_sems = refs[2 * n:]
        x, y, c = _me()
        if seq_id is not None:
            _handshake([(*_flip(x, y, p), c) for p in range(1, N_CHIPS)])
        cps = []
        for i in range(n):
            cc = shard_shapes[i][1]
            for p in range(1, N_CHIPS):
                px, py = _flip(x, y, p)
                pchip = 2 * px + py
                src = (srcs[i].at[:, pl.ds(pl.multiple_of(pchip * cc, cc), cc)] if kinds[i] == "col" else srcs[i].at[pchip])
                k = i * (N_CHIPS - 1) + p - 1
                cp = pltpu.make_async_remote_copy(src_ref=src, dst_ref=dsts[i].at[p - 1], send_sem=send_sems.at[k],
                                                  recv_sem=recv_sems.at[k], device_id=(px, py, c), device_id_type=MESH)
                cp.start()
                cps.append(cp)
        for cp in cps:
            cp.wait_recv()
        for cp in cps:
            cp.wait_send()

    nk = n * (N_CHIPS - 1)
    return _hbm_comm_call(body, name=name, n_in=n, out_shape=outs, seq_id=seq_id,
                          sem_shapes=[pltpu.SemaphoreType.DMA((nk,)), pltpu.SemaphoreType.DMA((nk,))])(*hbs)


def sum_chip_pieces(h, pieces, kind, shard_shape, chip_core, name):
    r, c = shard_shape
    hr = r // 2
    tr = hr if hr <= 512 else (256 if hr % 256 == 0 else hr // 2)
    assert hr % tr == 0
    nrb = hr // tr

    def body(ci_ref, h_ref, p_ref, q_ref):
        q_ref[...] = ((h_ref[...] + p_ref[0].astype(F32)) + p_ref[1].astype(F32)) + p_ref[2].astype(F32)

    if kind == "col":
        h_spec = pl.BlockSpec((tr, c), lambda i, ci: (i, ci[0]))
    else:
        h_spec = pl.BlockSpec((None, tr, c), lambda i, ci: (ci[0], i, 0))
    gs = pltpu.PrefetchScalarGridSpec(
        num_scalar_prefetch=1, grid=(nrb,),
        in_specs=[h_spec, pl.BlockSpec((N_CHIPS - 1, tr, c), lambda i, ci: (0, i, 0))],
        out_specs=pl.BlockSpec((tr, c), lambda i, ci: (ci[1] * nrb + i, 0)))
    return _pcall(body, name=name, grid_spec=gs, out_shape=jax.ShapeDtypeStruct((r, c), F32),
                  compiler_params=_params(("parallel",)))(chip_core, h, pieces)


def exchange_reduced_halves(qs, name):
    n = len(qs)

    def body(*refs):
        bufs = refs[n:2 * n]
        send_sems, recv_sems = refs[2 * n:]
        x, y, c = _me()
        cps = []
        for i in range(n):
            hr = bufs[i].shape[0] // 2
            mine = bufs[i].at[pl.ds(pl.multiple_of(c * hr, hr), hr), :]
            other = bufs[i].at[pl.ds(pl.multiple_of((1 - c) * hr, hr), hr), :]
            cp = pltpu.make_async_remote_copy(src_ref=mine, dst_ref=mine, send_sem=send_sems.at[i], recv_sem=recv_sems.at[i],
                                              device_id=(x, y, 1 - c), device_id_type=MESH)
            cp.start()
            cps.append((cp, pltpu.make_async_remote_copy(src_ref=other, dst_ref=other, send_sem=send_sems.at[i],
                                                         recv_sem=recv_sems.at[i], device_id=(x, y, 1 - c), device_id_type=MESH)))
        for cp, rv in cps:
            rv.wait_recv()
        for cp, rv in cps:
            cp.wait_send()

    anyspec = pl.BlockSpec(memory_space=pl.ANY)
    return _pcall(
        body, name=name, in_specs=[anyspec] * n, out_specs=[anyspec] * n,
        out_shape=[jax.ShapeDtypeStruct(q.shape, F32) for q in qs], input_output_aliases={i: i for i in range(n)},
        scratch_shapes=[pltpu.SemaphoreType.DMA((n,)), pltpu.SemaphoreType.DMA((n,))],
        compiler_params=_params(),
    )(*qs)


def _rows128(a):
    return a.reshape(-1, LANES)


def _after(xs, *deps):
    flat = []
    for d in deps:
        flat.extend(d if isinstance(d, (list, tuple)) else [d])
    return list(lax.optimization_barrier((tuple(xs), tuple(flat)))[0])


def _block_diag(w):
    H, d, _ = w.shape
    eye = jnp.eye(H, dtype=w.dtype)
    return jnp.einsum("hde,hg->hdge", w, eye).reshape(H * d, H * d)


def _diag_blocks(g4, H, d):
    nb = g4.shape[0]
    per = LANES // d
    g = g4.reshape(nb, per, d, per, d)
    return jnp.stack([g[:, j, :, j, :] for j in range(per)], axis=1).reshape(H, d, d)


def kernel(x, c, w_mod, b_mod, g_ffn1, w_ffn1_in, w_ffn1_out, g_mix, w_in, conv_w, conv_b, ln_g, ln_b, rnn_conv_w, rnn_conv_b, w_a, b_a, w_i, b_i, lru_lambda, w_out, g_ffn2, w_ffn2_in, w_ffn2_out, w_fmod, b_fmod, g_final, loss_target, m_w_mod, m_b_mod, m_g_ffn1, m_w_ffn1_in, m_w_ffn1_out, m_g_mix, m_w_in, m_conv_w, m_conv_b, m_ln_g, m_ln_b, m_rnn_conv_w, m_rnn_conv_b, m_w_a, m_b_a, m_w_i, m_b_i, m_lru_lambda, m_w_out, m_g_ffn2, m_w_ffn2_in, m_w_ffn2_out, m_w_fmod, m_b_fmod, m_g_final, v_w_mod, v_b_mod, v_g_ffn1, v_w_ffn1_in, v_w_ffn1_out, v_g_mix, v_w_in, v_conv_w, v_conv_b, v_ln_g, v_ln_b, v_rnn_conv_w, v_rnn_conv_b, v_w_a, v_b_a, v_w_i, v_b_i, v_lru_lambda, v_w_out, v_g_ffn2, v_w_ffn2_in, v_w_ffn2_out, v_w_fmod, v_b_fmod, v_g_final):
    S, D = x.shape[1], x.shape[2]
    M = conv_b.shape[1]
    H, HD = w_a.shape[1], w_a.shape[2]
    nb = M // LANES
    ix, iy, ic = lax.axis_index("x"), lax.axis_index("y"), lax.axis_index("c")
    chip = 2 * ix + iy
    dev = 2 * chip + ic
    cidx = jnp.reshape(ic, (1,)).astype(I32)
    chip_core = jnp.stack([chip, ic]).astype(I32)
    xs = x[0]
    tgt = loss_target[0]

    kinds = ["col", "row"]
    w_f1, w_mx, w_f2 = [w_ffn1_in[0], w_ffn1_out[0]], [w_in[0], w_out[0]], [w_ffn2_in[0], w_ffn2_out[0]]
    as_bf = lambda ws: [w.astype(BF) for w in ws]
    shapes_of = lambda ws: [w.shape for w in ws]
    wi1, wo1 = allgather_weights(as_bf(w_f1), kinds, "gather_ffn1", seq_id=9)
    win, wout = allgather_weights(as_bf(w_mx), kinds, "gather_mix", seq_id=1)
    wi2, wo2 = allgather_weights(as_bf(w_f2), kinds, "gather_ffn2", seq_id=2)

    c_all =allgather_devices(_rows128(c), "gather_c")[0].reshape(N_DEV, D)
    mod_cols = cond_matmul(c_all, w_mod[0], "mod_proj")
    fmod_cols = cond_matmul(c_all, w_fmod, "fmod_proj")
    convw_pad = jnp.pad(conv_w[0], ((0, 32 - CONV_WIDTH), (0, 0)))
    rnnw_pad = jnp.pad(rnn_conv_w[0], ((0, SUBLANES - RNN_CONV_WIDTH), (0, 0)))
    n_mod, n_fmod = mod_cols.shape[1], fmod_cols.shape[1]
    small = jnp.concatenate([_rows128(mod_cols), _rows128(fmod_cols), convw_pad, rnnw_pad], axis=0)
    small4 = allgather_chips(small, "gather_cond")
    r0 = N_DEV * n_mod // LANES
    r1 = r0 + N_DEV * n_fmod // LANES
    mod_all = small4[:, :r0].reshape(N_CHIPS, N_DEV, n_mod)
    fmod_all = small4[:, r0:r1].reshape(N_CHIPS, N_DEV, n_fmod)
    convw4 = small4[:, r1:r1 + 32]
    rnnw4 = small4[:, r1 + 32:r1 + 32 + SUBLANES]
    mod_row = lax.dynamic_index_in_dim(mod_all, dev, axis=1, keepdims=False).reshape(1, N_CHIPS * n_mod) + b_mod
    fmod_row = lax.dynamic_index_in_dim(fmod_all, dev, axis=1, keepdims=False).reshape(1, N_CHIPS * n_fmod) + b_fmod[None, :]
    vecs = jnp.concatenate([mod_row.reshape(9, D), fmod_row.reshape(2, D), g_ffn1, g_mix, g_ffn2, g_final[None, :],
                            jnp.zeros((1, D), F32)], axis=0)
    lnv = jnp.concatenate([ln_g, ln_b, jnp.zeros((SUBLANES - 2, M), F32)], axis=0)
    bda = _block_diag(w_a[0]).astype(BF)
    bdi = _block_diag(w_i[0]).astype(BF)

    def reduce_add(gs, recv, ws, tag, kinds_=kinds):
        pairs = [add_sibling_half(g, r_, k, w.shape, cidx, f"add_sibling_{tag}{j}")
                 for j, (g, r_, k, w) in enumerate(zip(gs, recv, kinds_, ws))]
        return [p[0] for p in pairs], [p[1] for p in pairs]

    def reduce_sum(hs_, recv, ws, tag, kinds_=kinds):
        return [sum_chip_pieces(h_, p_, k, w.shape, chip_core, f"sum_chips_{tag}{j}")
                for j, (h_, p_, k, w) in enumerate(zip(hs_, recv, kinds_, ws))]

    rows1 = (R_SH1, R_SC1, R_GT1, R_G1)
    rows3 = (R_SH3, R_SC3, R_GT3, R_G3)
    x1, g1s, u1s, y1 = ffn_fwd(xs, vecs, wi1, wo1, rows1, "ffn1_fwd")
    proj = norm_matmul(x1, vecs, win, (R_SH2, R_SC2, R_G2), "mix_in_proj")
    cq = conv_fwd(proj, convw4, conv_b, "conv_fwd")
    xr, ra, ii, hh = rnn_fwd(proj, rnnw4, rnn_conv_b, bda, bdi, b_a, b_i, lru_lambda, "rnn_fwd")
    x2, ym, ycat = mix_out(cq, proj, hh, x1, vecs, lnv, wout, "mix_out")
    x3, g2s, u2s, y2 = ffn_fwd(x2, vecs, wi2, wo2, rows3, "ffn2_fwd")
    dx3, vgf = final_fwd_bwd(x3, tgt, vecs, "final_loss")

    Fd = wo1.shape[0]
    tk = min(512, S)
    dx2, act2, dg2, du2, h3b, dy2b, vg3 = ffn_bwd(dx3, x2, vecs, g2s, u2s, y2, wi2, wo2, rows3, "ffn2_bwd")
    gwo2 = matmul(act2, dy2b, "tn", tm=Fd // 2, tn=D, tk=tk, name="ffn2_dwo")
    gwi2 = matmul(h3b, dg2, "tn", tm=D, tn=Fd // 2, tk=tk, name="ffn2_dwg", out_cols=2 * Fd)
    gwi2 = matmul(h3b, du2, "tn", tm=D, tn=Fd // 2, tk=tk, name="ffn2_dwu", out_cols=2 * Fd, col_off=Fd, prev=gwi2)
    recv1_f2 = exchange_sibling_halves([gwi2, gwo2], kinds, shapes_of(w_f2), "reduce1_ffn2", seq_id=3)
    dcq, dhout, duy, dymb, vgd, vgm = mix_out_bwd(dx2, ym, vecs, wout, cq, lnv, proj, hh, "mix_out_bwd")
    gwout = matmul(ycat, dymb, "tn", tm=2 * M, tn=D, tk=tk, name="mix_dwout")
    recv1_f2 = _after(recv1_f2, gwout)
    h_f2, hb_f2 = reduce_add([gwi2, gwo2], recv1_f2, w_f2, "ffn2_")
    recv2_f2 = exchange_chip_pieces(hb_f2, kinds, shapes_of(w_f2), "reduce2_ffn2", seq_id=4)
    duv, dug, dconvw4, dconvb = conv_bwd(_after([dcq], hb_f2)[0], proj, convw4, "conv_bwd")
    dux, dwa4, dwi4, drnnw4, rvec = rnn_bwd(dhout, hh, xr, ra, ii, proj, rnnw4, bda, bdi, lru_lambda, "rnn_bwd")
    dx1, h2b, dpb, vg2 = mix_in_bwd((duv, dug, dux, duy), x1, dx2, vecs, win, "mix_in_bwd")
    gwin = matmul(h2b, dpb, "tn", tm=D, tn=1024, tk=tk, name="mix_dwin")
    recv1_mx = exchange_sibling_halves([gwin, gwout], kinds, shapes_of(w_mx), "reduce1_mix", seq_id=5)
    g_f2 = exchange_reduced_halves(reduce_sum(_after(h_f2, gwin), recv2_f2, w_f2, "ffn2_"), "reduce3_ffn2")
    adam_f2 = [adam_big(w, g, m, v, "adam_" + nm) for w, g, m, v, nm in
               zip(w_f2, g_f2, [m_w_ffn2_in[0], m_w_ffn2_out[0]], [v_w_ffn2_in[0], v_w_ffn2_out[0]], ["ffn2_in", "ffn2_out"])]
    h_mx, hb_mx = reduce_add([gwin, gwout], _after(recv1_mx, adam_f2[0][0], adam_f2[1][0]), w_mx, "mix_")
    recv2_mx = exchange_chip_pieces(hb_mx, kinds, shapes_of(w_mx), "reduce2_mix", seq_id=6)
    dx0, act1, dg1, du1, h1b, dy1b, vg1 = ffn_bwd(_after([dx1], hb_mx)[0], xs, vecs, g1s, u1s, y1, wi1, wo1, rows1, "ffn1_bwd")
    gwo1 = matmul(act1, dy1b, "tn", tm=Fd // 2, tn=D, tk=tk, name="ffn1_dwo")
    w_f1o, w_f1i = w_f1[1:], w_f1[:1]
    recv1_f1o = exchange_sibling_halves([gwo1], ["row"], shapes_of(w_f1o), "reduce1_ffn1_out", seq_id=7)
    gwi1 = matmul(h1b, dg1, "tn", tm=D, tn=Fd // 2, tk=tk, name="ffn1_dwg", out_cols=2 * Fd)
    h_f1o, hb_f1o = reduce_add([gwo1], recv1_f1o, w_f1o, "ffn1_out", ["row"])
    recv2_f1o = exchange_chip_pieces(hb_f1o, ["row"], shapes_of(w_f1o), "reduce2_ffn1_out", seq_id=11)
    gwi1 = matmul(h1b, du1, "tn", tm=D, tn=Fd // 2, tk=tk, name="ffn1_dwu", out_cols=2 * Fd, col_off=Fd, prev=gwi1)
    recv1_f1i = exchange_sibling_halves([gwi1], ["col"], shapes_of(w_f1i), "reduce1_ffn1_in", seq_id=12)
    g_mx = exchange_reduced_halves(reduce_sum(h_mx, recv2_mx, w_mx, "mix_"), "reduce3_mix")

    dmod_row = jnp.concatenate([vg1[1:3], vg1[0:1], vg2[0:2], vgd[0:1], vg3[1:3], vg3[0:1]], axis=0)
    gains = jnp.concatenate([vg1[3:4], vg2[2:3], vg3[3:4], vgf[2:4]], axis=0)
    mvecs = jnp.concatenate([dconvb, vgm[0:2], rvec[0:4], jnp.zeros((1, M), F32)], axis=0)
    parts = [_rows128(dmod_row), _rows128(vgf[0:2]), _rows128(gains), _rows128(mvecs),
             _rows128(dconvw4), _rows128(drnnw4), _rows128(_diag_blocks(dwa4, H, HD)), _rows128(_diag_blocks(dwi4, H, HD))]
    sizes = [p.shape[0] for p in parts]
    packed = jnp.concatenate(parts, axis=0)
    gathered = allgather_devices_hbm(packed, "gather_small", seq_id=10)
    summed = sum_slots(gathered, "sum_small")
    offs = [0]
    for s in sizes:
        offs.append(offs[-1] + s)
    seg = lambda k: summed[offs[k]:offs[k + 1]]
    g_b_mod = seg(0).reshape(1, 9 * D)
    g_b_fmod = seg(1).reshape(1, 2 * D)
    gsum = seg(2).reshape(5, D)
    loss = (0.5 / D) * jnp.sum(gsum[4])
    msum = seg(3).reshape(SUBLANES, M)
    g_conv_w = lax.dynamic_index_in_dim(seg(4).reshape(nb, 32, LANES), chip, axis=0, keepdims=False)[:CONV_WIDTH]
    g_rnn_w = lax.dynamic_index_in_dim(seg(5).reshape(nb, SUBLANES, LANES), chip, axis=0, keepdims=False)[:RNN_CONV_WIDTH]
    g_w_a = seg(6).reshape(H, HD, HD)
    g_w_i = seg(7).reshape(H, HD, HD)
    dmod_all = gathered[:, offs[0]:offs[1]].reshape(N_DEV, 9 * D)
    dfmod_all = gathered[:, offs[1]:offs[2]].reshape(N_DEV, 2 * D)
    dmod_cols = lax.dynamic_slice_in_dim(dmod_all, chip * n_mod, n_mod, axis=1)
    dfmod_cols = lax.dynamic_slice_in_dim(dfmod_all, chip * n_fmod, n_fmod, axis=1)

    h_f1i, hb_f1i = reduce_add([gwi1], recv1_f1i, w_f1i, "ffn1_in", ["col"])
    recv2_f1i = exchange_chip_pieces(hb_f1i, ["col"], shapes_of(w_f1i), "reduce2_ffn1_in", seq_id=8)
    g_f1o = exchange_reduced_halves(reduce_sum(h_f1o, recv2_f1o, w_f1o, "ffn1_out", ["row"]), "reduce3_ffn1_out")
    g_f1i = exchange_reduced_halves(reduce_sum(h_f1i, recv2_f1i, w_f1i, "ffn1_in", ["col"]), "reduce3_ffn1_in")
    g_f1 = list(g_f1i) + list(g_f1o)

    big_w = w_f1 + w_mx + w_f2
    g_big = g_f1 + list(g_mx) + list(g_f2)
    names = ["ffn1_in", "ffn1_out", "w_in", "w_out", "ffn2_in", "ffn2_out"]
    big_m = [m_w_ffn1_in[0], m_w_ffn1_out[0], m_w_in[0], m_w_out[0]]
    big_v = [v_w_ffn1_in[0], v_w_ffn1_out[0], v_w_in[0], v_w_out[0]]
    big_out = [adam_big(w, g, m, v, "adam_" + nm) for w, g, m, v, nm in zip(big_w, g_big, big_m, big_v, names)] + adam_f2
    g_w_mod, d_w_mod, nm_w_mod, nv_w_mod = adam_cond(c_all, dmod_cols, w_mod[0], m_w_mod[0], v_w_mod[0], "adam_w_mod")
    g_w_fmod, d_w_fmod, nm_w_fmod, nv_w_fmod = adam_cond(c_all, dfmod_cols, w_fmod, m_w_fmod, v_w_fmod, "adam_w_fmod")

    flat2 = lambda a: a.reshape(-1, a.shape[-1])
    small_names = ["b_mod", "g_ffn1", "g_mix", "conv_w", "conv_b", "ln_g", "ln_b", "rnn_conv_w", "rnn_conv_b", "w_a", "b_a",
                   "w_i", "b_i", "lru_lambda", "g_ffn2", "b_fmod", "g_final"]
    small_w = [b_mod, g_ffn1, g_mix, conv_w, conv_b, ln_g, ln_b, rnn_conv_w, rnn_conv_b, w_a, b_a, w_i, b_i, lru_lambda,
               g_ffn2, b_fmod, g_final]
    small_m = [m_b_mod, m_g_ffn1, m_g_mix, m_conv_w, m_conv_b, m_ln_g, m_ln_b, m_rnn_conv_w, m_rnn_conv_b, m_w_a, m_b_a,
               m_w_i, m_b_i, m_lru_lambda, m_g_ffn2, m_b_fmod, m_g_final]
    small_v = [v_b_mod, v_g_ffn1, v_g_mix, v_conv_w, v_conv_b, v_ln_g, v_ln_b, v_rnn_conv_w, v_rnn_conv_b, v_w_a, v_b_a,
               v_w_i, v_b_i, v_lru_lambda, v_g_ffn2, v_b_fmod, v_g_final]
    small_g = [g_b_mod, gsum[0:1], gsum[1:2], g_conv_w, msum[0:1], msum[1:2], msum[2:3], g_rnn_w, msum[3:4], g_w_a, msum[4:5],
               g_w_i, msum[5:6], msum[6:7], gsum[2:3], g_b_fmod, gsum[3:4]]
    small_g = [g.reshape(w.shape) for g, w in zip(small_g, small_w)]
    two_d = lambda a: a.reshape(1, -1) if a.ndim == 1 else flat2(a)
    sd, sm, sv = adam_small([two_d(a) for a in small_w], [two_d(a) for a in small_g], [two_d(a) for a in small_m],
                            [two_d(a) for a in small_v], "adam_small")
    small = {}
    for k, nm in enumerate(small_names):
        shp = small_w[k].shape
        small[nm] = (small_g[k], sd[k].reshape(shp), sm[k].reshape(shp), sv[k].reshape(shp))

    big = {"w_mod": tuple(a[None] for a in (g_w_mod, d_w_mod, nm_w_mod, nv_w_mod)),
           "w_fmod": (g_w_fmod, d_w_fmod, nm_w_fmod, nv_w_fmod)}
    for nm, full, g, (d, nmm, nvv) in zip(["w_ffn1_in", "w_ffn1_out", "w_in", "w_out", "w_ffn2_in", "w_ffn2_out"],
                                         big_w, g_big, big_out):
        big[nm] = tuple(a[None] for a in (g, d, nmm, nvv))
    order = ["w_mod", "b_mod", "g_ffn1", "w_ffn1_in", "w_ffn1_out", "g_mix", "w_in", "conv_w", "conv_b", "ln_g", "ln_b",
             "rnn_conv_w", "rnn_conv_b", "w_a", "b_a", "w_i", "b_i", "lru_lambda", "w_out", "g_ffn2", "w_ffn2_in",
             "w_ffn2_out", "w_fmod", "b_fmod", "g_final"]
    table = {**small, **big}
    outs = [loss, dx0[None]]
    for kind_ in range(4):
        outs.extend(table[nm][kind_] for nm in order)
    return tuple(outs)
```

```python
import functools

import jax
import jax.numpy as jnp
from jax import lax
from jax.experimental import pallas as pl
from jax.experimental.pallas import tpu as pltpu
from jax.experimental.pallas import tpu_sc as plsc

F32 = jnp.float32
BF = jnp.bfloat16
I32 = jnp.int32
MESH = pl.DeviceIdType.MESH

EPS = 1e-6
RG_C = 8.0
MACARON_W = 0.5
CONV_WIDTH = 31
RNN_CONV_WIDTH = 4
ADAM_LR = 0.001
ADAM_B1 = 0.9
ADAM_B2 = 0.999
ADAM_EPS = 1e-08
ADAM_WD = 0.01
ADAM_STEP = 10

LANES = 128
SUBLANES = 8
VMEM_LIMIT = 48 * 1024 * 1024
N_CHIPS = 4
N_DEV = 8

R_SH1, R_SC1, R_GT1, R_SH2, R_SC2, R_GT2, R_SH3, R_SC3, R_GT3, R_FSH, R_FSC, R_G1, R_G2, R_G3, R_GF = range(15)

CONTRACT_LAST = (((1,), (1,)), ((), ()))
CONTRACT_FIRST = (((0,), (0,)), ((), ()))


def _pcall(body, **kw):
    return pl.pallas_call(body, **kw)


def _params(sem=None, vmem=VMEM_LIMIT):
    if sem is None:
        return pltpu.CompilerParams(vmem_limit_bytes=vmem)
    return pltpu.CompilerParams(dimension_semantics=sem, vmem_limit_bytes=vmem)


def _row(ref, r):
    return ref[r:r + 1, :]


def _sigmoid(x):
    return 1.0 / (1.0 + jnp.exp(-x))


def _colsum(x):
    return jnp.sum(x, axis=0, keepdims=True)


def _rowmean(x):
    return jnp.mean(x, axis=-1, keepdims=True)


def matmul(a, b, mode, *, tm, tn, tk, name, out_dtype=F32, out_cols=None, col_off=0, prev=None):
    if mode == "nn":
        (M, K), (K2, N) = a.shape, b.shape
    elif mode == "nt":
        (M, K), (N, K2) = a.shape, b.shape
    else:
        (K, M), (K2, N) = a.shape, b.shape
    assert K == K2 and M % tm == 0 and N % tn == 0 and K % tk == 0 and col_off % tn == 0
    nk = K // tk
    out_cols = N if out_cols is None else out_cols
    off = col_off // tn

    def body(*refs):
        if prev is None:
            a_ref, b_ref, o_ref, acc = refs
        else:
            a_ref, b_ref, _, o_ref, acc = refs
        k = pl.program_id(2)

        @pl.when(k == 0)
        def _():
            acc[...] = jnp.zeros_like(acc)

        av = a_ref[...].astype(BF)
        bv = b_ref[...].astype(BF)
        if mode == "nn":
            acc[...] += jnp.dot(av, bv, preferred_element_type=F32)
        elif mode == "nt":
            acc[...] += lax.dot_general(av, bv, CONTRACT_LAST, preferred_element_type=F32)
        else:
            acc[...] += lax.dot_general(av, bv, CONTRACT_FIRST, preferred_element_type=F32)

        @pl.when(k == nk - 1)
        def _():
            o_ref[...] = acc[...].astype(out_dtype)

    if mode == "nn":
        a_spec = pl.BlockSpec((tm, tk), lambda m, n, k: (m, k))
        b_spec = pl.BlockSpec((tk, tn), lambda m, n, k: (k, n))
    elif mode == "nt":
        a_spec = pl.BlockSpec((tm, tk), lambda m, n, k: (m, k))
        b_spec = pl.BlockSpec((tn, tk), lambda m, n, k: (n, k))
    else:
        a_spec = pl.BlockSpec((tk, tm), lambda m, n, k: (k, m))
        b_spec = pl.BlockSpec((tk, tn), lambda m, n, k: (k, n))
    in_specs = [a_spec, b_spec]
    args = [a, b]
    aliases = {}
    if prev is not None:
        in_specs.append(pl.BlockSpec(memory_space=pl.ANY))
        args.append(prev)
        aliases = {2: 0}
    return _pcall(
        body, name=name, grid=(M // tm, N // tn, nk), in_specs=in_specs,
        out_specs=pl.BlockSpec((tm, tn), lambda m, n, k: (m, n + off)),
        out_shape=jax.ShapeDtypeStruct((M, out_cols), out_dtype),
        scratch_shapes=[pltpu.VMEM((tm, tn), F32)], input_output_aliases=aliases,
        compiler_params=_params(("parallel", "parallel", "arbitrary")),
    )(*args)


def cond_matmul(c_all, w, name):
    B, K = c_all.shape
    N = w.shape[1]
    tn = 256
    assert N % tn == 0

    def body(c_ref, w_ref, o_ref):
        cv = c_ref[...]
        ca = cv * _sigmoid(cv)
        o_ref[...] = jnp.dot(ca, w_ref[...], preferred_element_type=F32, precision=lax.Precision.HIGHEST)

    return _pcall(
        body, name=name, grid=(N // tn,),
        in_specs=[pl.BlockSpec((B, K), lambda n: (0, 0)), pl.BlockSpec((K, tn), lambda n: (0, n))],
        out_specs=pl.BlockSpec((B, tn), lambda n: (0, n)),
        out_shape=jax.ShapeDtypeStruct((B, N), F32), compiler_params=_params(("parallel",)),
    )(c_all, w)


FFN_CH = 1408
FFN_FWD_TS = 512
FFN_BWD_TS = 256
FFN_FWD_VMEM = 56 * 1024 * 1024


def ffn_fwd(x, vecs, wi, wo, rows, name):
    r_sh, r_sc, r_gt, r_g = rows
    S, D = x.shape
    Fd = wo.shape[0]
    ts, ch = min(FFN_FWD_TS, S), FFN_CH
    ni, nj = S // ts, Fd // ch
    assert nj >= 2

    def body(x_ref, v_ref, wg_ref, wu_ref, wo_ref, xo_ref, g_ref, u_ref, y_ref, h_sc, acc):
        j = pl.program_id(1)

        @pl.when(j == 0)
        def _():
            xv = x_ref[...]
            r = lax.rsqrt(_rowmean(xv * xv) + EPS)
            gs = _row(v_ref, r_g) * (1.0 + _row(v_ref, r_sc))
            h_sc[...] = (xv * r * gs + _row(v_ref, r_sh)).astype(BF)

        hb = h_sc[...]
        G = jnp.dot(hb, wg_ref[...], preferred_element_type=F32)
        U = jnp.dot(hb, wu_ref[...], preferred_element_type=F32)
        g_ref[...] = G.astype(BF)
        u_ref[...] = U.astype(BF)
        act = (G * _sigmoid(G) * U).astype(BF)
        part = jnp.dot(act, wo_ref[...], preferred_element_type=F32)

        @pl.when(j == 0)
        def _():
            acc[...] = part

        @pl.when((j > 0) & (j < nj - 1))
        def _():
            acc[...] += part

        @pl.when(j == nj - 1)
        def _():
            Y = acc[...] + part
            y_ref[...] = Y
            xo_ref[...] = x_ref[...] + (MACARON_W * _row(v_ref, r_gt)) * Y

    tok = pl.BlockSpec((ts, D), lambda i, j: (i, 0))
    hid = pl.BlockSpec((ts, ch), lambda i, j: (i, j))
    return _pcall(
        body, name=name, grid=(ni, nj),
        in_specs=[tok, pl.BlockSpec(vecs.shape, lambda i, j: (0, 0)),
                  pl.BlockSpec((D, ch), lambda i, j: (0, j)), pl.BlockSpec((D, ch), lambda i, j: (0, j + nj)),
                  pl.BlockSpec((ch, D), lambda i, j: (j, 0))],
        out_specs=[tok, hid, hid, tok],
        out_shape=[jax.ShapeDtypeStruct((S, D), F32), jax.ShapeDtypeStruct((S, Fd), BF),
                   jax.ShapeDtypeStruct((S, Fd), BF), jax.ShapeDtypeStruct((S, D), F32)],
        scratch_shapes=[pltpu.VMEM((ts, D), BF), pltpu.VMEM((ts, D), F32)],
        compiler_params=_params(("arbitrary", "arbitrary"), FFN_FWD_VMEM),
    )(x, vecs, wi, wi, wo)


def ffn_bwd(dxo, x, vecs, gs_, us_, y, wi, wo, rows, name):
    r_sh, r_sc, r_gt, r_g = rows
    S, D = x.shape
    Fd = wo.shape[0]
    ts, ch = min(FFN_BWD_TS, S), FFN_CH
    ni, nj = S // ts, Fd // ch

    def body(dxo_ref, x_ref, v_ref, g_ref, u_ref, y_ref, wg_ref, wu_ref, wo_ref,
             dx_ref, act_ref, dg_ref, du_ref, hb_ref, dyb_ref, vg_ref, dyb_sc, dh_sc):
        i, j = pl.program_id(0), pl.program_id(1)

        @pl.when((i == 0) & (j == 0))
        def _():
            vg_ref[...] = jnp.zeros_like(vg_ref)

        @pl.when(j == 0)
        def _():
            dxo_v = dxo_ref[...]
            dyb = ((MACARON_W * _row(v_ref, r_gt)) * dxo_v).astype(BF)
            dyb_sc[...] = dyb
            dyb_ref[...] = dyb
            vg_ref[0:1, :] += MACARON_W * _colsum(dxo_v * y_ref[...])

        dA = lax.dot_general(dyb_sc[...], wo_ref[...], CONTRACT_LAST, preferred_element_type=F32)
        G = g_ref[...].astype(F32)
        U = u_ref[...].astype(F32)
        sg = _sigmoid(G)
        sl = G * sg
        dU = (dA * sl).astype(BF)
        dG = (dA * U * (sg * (1.0 + G * (1.0 - sg)))).astype(BF)
        act_ref[...] = (sl * U).astype(BF)
        dg_ref[...] = dG
        du_ref[...] = dU
        part = (lax.dot_general(dG, wg_ref[...], CONTRACT_LAST, preferred_element_type=F32)
                + lax.dot_general(dU, wu_ref[...], CONTRACT_LAST, preferred_element_type=F32))

        @pl.when(j == 0)
        def _():
            dh_sc[...] = part

        @pl.when((j > 0) & (j < nj - 1))
        def _():
            dh_sc[...] += part

        @pl.when(j == nj - 1)
        def _():
            dh = dh_sc[...] + part
            xv = x_ref[...]
            r = lax.rsqrt(_rowmean(xv * xv) + EPS)
            n = xv * r
            g = _row(v_ref, r_g)
            sc1 = 1.0 + _row(v_ref, r_sc)
            gsc = g * sc1
            hb_ref[...] = (n * gsc + _row(v_ref, r_sh)).astype(BF)
            dhn = dh * n
            vg_ref[1:2, :] += _colsum(dh)
            vg_ref[2:3, :] += _colsum(dhn) * g
            vg_ref[3:4, :] += _colsum(dhn) * sc1
            dn = dh * gsc
            dx_ref[...] = dxo_ref[...] + r * (dn - n * _rowmean(dn * n))

    tok = pl.BlockSpec((ts, D), lambda i, j: (i, 0))
    hid = pl.BlockSpec((ts, ch), lambda i, j: (i, j))
    return _pcall(
        body, name=name, grid=(ni, nj),
        in_specs=[tok, tok, pl.BlockSpec(vecs.shape, lambda i, j: (0, 0)), hid, hid, tok,
                  pl.BlockSpec((D, ch), lambda i, j: (0, j)), pl.BlockSpec((D, ch), lambda i, j: (0, j + nj)),
                  pl.BlockSpec((ch, D), lambda i, j: (j, 0))],
        out_specs=[tok, hid, hid, hid, tok, tok, pl.BlockSpec((SUBLANES, D), lambda i, j: (0, 0))],
        out_shape=[jax.ShapeDtypeStruct((S, D), F32), jax.ShapeDtypeStruct((S, Fd), BF),
                   jax.ShapeDtypeStruct((S, Fd), BF), jax.ShapeDtypeStruct((S, Fd), BF),
                   jax.ShapeDtypeStruct((S, D), BF), jax.ShapeDtypeStruct((S, D), BF),
                   jax.ShapeDtypeStruct((SUBLANES, D), F32)],
        scratch_shapes=[pltpu.VMEM((ts, D), BF), pltpu.VMEM((ts, D), F32)],
        compiler_params=_params(("arbitrary", "arbitrary")),
    )(dxo, x, vecs, gs_, us_, y, wi, wi, wo)


def final_fwd_bwd(x, tgt, vecs, name):
    S, D = x.shape
    ts = min(512, S)

    def body(x_ref, t_ref, v_ref, dx_ref, vg_ref):
        @pl.when(pl.program_id(0) == 0)
        def _():
            vg_ref[...] = jnp.zeros_like(vg_ref)

        xv = x_ref[...]
        r = lax.rsqrt(_rowmean(xv * xv) + EPS)
        n = xv * r
        g = _row(v_ref, R_GF)
        sc1 = 1.0 + _row(v_ref, R_FSC)
        gsc = g * sc1
        e = n * gsc + _row(v_ref, R_FSH) - t_ref[...]
        vg_ref[3:4, :] += _colsum(e * e)
        dout = e * (1.0 / D)
        dn_ = dout * n
        vg_ref[0:1, :] += _colsum(dout)
        vg_ref[1:2, :] += _colsum(dn_) * g
        vg_ref[2:3, :] += _colsum(dn_) * sc1
        dn = dout * gsc
        dx_ref[...] = r * (dn - n * _rowmean(dn * n))

    tok = pl.BlockSpec((ts, D), lambda i: (i, 0))
    return _pcall(
        body, name=name, grid=(S // ts,),
        in_specs=[tok, tok, pl.BlockSpec(vecs.shape, lambda i: (0, 0))],
        out_specs=[tok, pl.BlockSpec((SUBLANES, D), lambda i: (0, 0))],
        out_shape=[jax.ShapeDtypeStruct((S, D), F32), jax.ShapeDtypeStruct((SUBLANES, D), F32)],
        compiler_params=_params(("arbitrary",)),
    )(x, tgt, vecs)


def norm_matmul(x, vecs, w, rows, name):
    r_sh, r_sc, r_g = rows
    S, D = x.shape
    N = w.shape[1]
    ts, tn = min(512, S), 512
    nn = N // tn

    def body(x_ref, v_ref, w_ref, o_ref, h_sc):
        @pl.when(pl.program_id(1) == 0)
        def _():
            xv = x_ref[...]
            r = lax.rsqrt(_rowmean(xv * xv) + EPS)
            gs = _row(v_ref, r_g) * (1.0 + _row(v_ref, r_sc))
            h_sc[...] = (xv * r * gs + _row(v_ref, r_sh)).astype(BF)

        o_ref[...] = jnp.dot(h_sc[...], w_ref[...], preferred_element_type=F32)

    return _pcall(
        body, name=name, grid=(S // ts, nn),
        in_specs=[pl.BlockSpec((ts, D), lambda i, n: (i, 0)), pl.BlockSpec(vecs.shape, lambda i, n: (0, 0)),
                  pl.BlockSpec((D, tn), lambda i, n: (0, n))],
        out_specs=pl.BlockSpec((ts, tn), lambda i, n: (i, n)),
        out_shape=jax.ShapeDtypeStruct((S, N), F32),
        scratch_shapes=[pltpu.VMEM((ts, D), BF)],
        compiler_params=_params(("arbitrary", "arbitrary")),
    )(x, vecs, w)


SEQ_TT = 256
CONV_PAD = 32


def conv_fwd(proj, convw4, conv_b, name):
    S = proj.shape[0]
    M = conv_b.shape[1]
    nb = M // LANES
    tt = min(SEQ_TT, S)

    def body(uv_ref, ug_ref, w_ref, b_ref, cq_ref, qp):
        qp[0:CONV_PAD, :] = jnp.zeros((CONV_PAD, LANES), F32)

        def step(t, carry):
            base = pl.multiple_of(t * tt, tt)
            qp[pl.ds(base + CONV_PAD, tt), :] = uv_ref[pl.ds(base, tt), :] * _sigmoid(ug_ref[pl.ds(base, tt), :])
            acc = jnp.broadcast_to(b_ref[...], (tt, LANES))
            for k in range(CONV_WIDTH):
                acc = acc + w_ref[k:k + 1, :] * qp[pl.ds(base + (CONV_PAD - CONV_WIDTH + 1) + k, tt), :]
            cq_ref[pl.ds(base, tt), :] = acc
            return carry

        lax.fori_loop(0, S // tt, step, 0)

    return _pcall(
        body, name=name, grid=(nb,),
        in_specs=[pl.BlockSpec((S, LANES), lambda c: (0, c)), pl.BlockSpec((S, LANES), lambda c: (0, c + nb)),
                  pl.BlockSpec((None, 32, LANES), lambda c: (c, 0, 0)), pl.BlockSpec((1, LANES), lambda c: (0, c))],
        out_specs=pl.BlockSpec((S, LANES), lambda c: (0, c)),
        out_shape=jax.ShapeDtypeStruct((S, M), F32),
        scratch_shapes=[pltpu.VMEM((S + CONV_PAD, LANES), F32)],
        compiler_params=_params(("arbitrary",)),
    )(proj, proj, convw4, conv_b)


def conv_bwd(dcq, proj, convw4, name):
    S, M = dcq.shape
    nb = M // LANES
    tt = min(SEQ_TT, S)
    off = CONV_PAD - CONV_WIDTH + 1

    def body(dcq_ref, uv_ref, ug_ref, w_ref, duv_ref, dug_ref, dw_ref, db_ref, qp, dp, dw8, db8):
        qp[0:CONV_PAD, :] = jnp.zeros((CONV_PAD, LANES), F32)
        dp[S:S + CONV_PAD, :] = jnp.zeros((CONV_PAD, LANES), F32)
        dw8[...] = jnp.zeros_like(dw8)
        db8[...] = jnp.zeros_like(db8)

        def fill(t, carry):
            base = pl.multiple_of(t * tt, tt)
            qp[pl.ds(base + CONV_PAD, tt), :] = uv_ref[pl.ds(base, tt), :] * _sigmoid(ug_ref[pl.ds(base, tt), :])
            dp[pl.ds(base, tt), :] = dcq_ref[pl.ds(base, tt), :]
            return carry

        lax.fori_loop(0, S // tt, fill, 0)

        def step(t, carry):
            base = pl.multiple_of(t * tt, tt)
            d_t = dcq_ref[pl.ds(base, tt), :]
            db8[...] += d_t.reshape(tt // SUBLANES, SUBLANES, LANES).sum(axis=0)
            dq = jnp.zeros((tt, LANES), F32)
            for k in range(CONV_WIDTH):
                prod = d_t * qp[pl.ds(base + off + k, tt), :]
                dw8[k] += prod.reshape(tt // SUBLANES, SUBLANES, LANES).sum(axis=0)
                dq = dq + w_ref[k:k + 1, :] * dp[pl.ds(base + (CONV_WIDTH - 1) - k, tt), :]
            uv = uv_ref[pl.ds(base, tt), :]
            sg = _sigmoid(ug_ref[pl.ds(base, tt), :])
            duv_ref[pl.ds(base, tt), :] = dq * sg
            dug_ref[pl.ds(base, tt), :] = dq * uv * sg * (1.0 - sg)
            return carry

        lax.fori_loop(0, S // tt, step, 0)
        dw_ref[...] = jnp.zeros_like(dw_ref)
        for k in range(CONV_WIDTH):
            dw_ref[k:k + 1, :] = _colsum(dw8[k])
        db_ref[...] = _colsum(db8[...])

    col = lambda o: pl.BlockSpec((S, LANES), lambda c: (0, c + o))
    return _pcall(
        body, name=name, grid=(nb,),
        in_specs=[col(0), col(0), col(nb), pl.BlockSpec((None, 32, LANES), lambda c: (c, 0, 0))],
        out_specs=[col(0), col(0), pl.BlockSpec((None, 32, LANES), lambda c: (c, 0, 0)),
                   pl.BlockSpec((1, LANES), lambda c: (0, c))],
        out_shape=[jax.ShapeDtypeStruct((S, M), F32), jax.ShapeDtypeStruct((S, M), F32),
                   jax.ShapeDtypeStruct((nb, 32, LANES), F32), jax.ShapeDtypeStruct((1, M), F32)],
        scratch_shapes=[pltpu.VMEM((S + CONV_PAD, LANES), F32), pltpu.VMEM((S + CONV_PAD, LANES), F32),
                        pltpu.VMEM((32, SUBLANES, LANES), F32), pltpu.VMEM((SUBLANES, LANES), F32)],
        compiler_params=_params(("arbitrary",)),
    )(dcq, proj, proj, convw4)


def _log_sigmoid(x):
    return jnp.minimum(x, 0.0) - jnp.log(1.0 + jnp.exp(-jnp.abs(x)))


def _rg_gate_terms(ra, ls):
    la = RG_C * ra * ls
    a = jnp.exp(la)
    th = jnp.tanh(la)
    mult = jnp.sqrt(-2.0 * th / (1.0 - th))
    return a, mult


def rnn_fwd(proj, rnnw4, rnn_b, bda, bdi, b_a, b_i, lam, name):
    S = proj.shape[0]
    M = rnn_b.shape[1]
    nb = M // LANES
    tt = min(SEQ_TT, S)
    KW = RNN_CONV_WIDTH

    def body(ux_ref, w_ref, rb_ref, bda_ref, bdi_ref, ba_ref, bi_ref, lam_ref,
             xr_ref, ra_ref, ii_ref, h_ref, uxp, a_sc, b_sc):
        uxp[0:SUBLANES, :] = jnp.zeros((SUBLANES, LANES), F32)
        ls = _log_sigmoid(lam_ref[...])

        def step(t, carry):
            base = pl.multiple_of(t * tt, tt)
            uxp[pl.ds(base + SUBLANES, tt), :] = ux_ref[pl.ds(base, tt), :]
            xr = jnp.broadcast_to(rb_ref[...], (tt, LANES))
            for k in range(KW):
                xr = xr + w_ref[k:k + 1, :] * uxp[pl.ds(base + (SUBLANES - KW + 1) + k, tt), :]
            xb = xr.astype(BF)
            ra = _sigmoid(jnp.dot(xb, bda_ref[...], preferred_element_type=F32) + ba_ref[...])
            ii = _sigmoid(jnp.dot(xb, bdi_ref[...], preferred_element_type=F32) + bi_ref[...])
            a, mult = _rg_gate_terms(ra, ls)
            xr_ref[pl.ds(base, tt), :] = xr
            ra_ref[pl.ds(base, tt), :] = ra
            ii_ref[pl.ds(base, tt), :] = ii
            a_sc[pl.ds(base, tt), :] = a
            b_sc[pl.ds(base, tt), :] = mult * (ii * xr)
            return carry

        lax.fori_loop(0, S // tt, step, 0)

        rows = lax.broadcasted_iota(I32, (SUBLANES, LANES), 0)

        def scan(t, hprev):
            base = pl.multiple_of(t * SUBLANES, SUBLANES)
            A = a_sc[pl.ds(base, SUBLANES), :]
            B = b_sc[pl.ds(base, SUBLANES), :]
            for d in (1, 2, 4):
                As = jnp.where(rows >= d, pltpu.roll(A, d, axis=0), 1.0)
                Bs = jnp.where(rows >= d, pltpu.roll(B, d, axis=0), 0.0)
                B = A * Bs + B
                A = A * As
            hh = B + A * hprev
            h_ref[pl.ds(base, SUBLANES), :] = hh
            return jnp.broadcast_to(hh[SUBLANES - 1:SUBLANES, :], (SUBLANES, LANES))

        lax.fori_loop(0, S // SUBLANES, scan, jnp.zeros((SUBLANES, LANES), F32))

    col = lambda o: pl.BlockSpec((S, LANES), lambda c: (0, c + o))
    vec = pl.BlockSpec((1, LANES), lambda c: (0, c))
    diag = pl.BlockSpec((LANES, LANES), lambda c: (c, c))
    return _pcall(
        body, name=name, grid=(nb,),
        in_specs=[col(2 * nb), pl.BlockSpec((None, SUBLANES, LANES), lambda c: (c, 0, 0)), vec, diag, diag, vec, vec, vec],
        out_specs=[col(0)] * 4,
        out_shape=[jax.ShapeDtypeStruct((S, M), F32)] * 4,
        scratch_shapes=[pltpu.VMEM((S + SUBLANES, LANES), F32), pltpu.VMEM((S, LANES), F32), pltpu.VMEM((S, LANES), F32)],
        compiler_params=_params(("arbitrary",)),
    )(proj, rnnw4, rnn_b, bda, bdi, b_a, b_i, lam)


def rnn_bwd(dhout, h, xr, ra, ii, proj, rnnw4, bda, bdi, lam, name):
    S, M = h.shape
    nb = M // LANES
    tt = min(SEQ_TT, S)
    KW = RNN_CONV_WIDTH
    SL = SUBLANES

    def body(dh_ref, h_ref, xr_ref, ra_ref, ii_ref, ux_ref, w_ref, bda_ref, bdi_ref, lam_ref,
             dux_ref, dwa_ref, dwi_ref, drw_ref, vec_ref,
             a_sc, hp, g_sc, dpa_sc, dpi_sc, dxp, uxp, acc8, drw8):
        zero8 = jnp.zeros((SL, LANES), F32)
        a_sc[S:S + SL, :] = zero8
        hp[0:SL, :] = zero8
        dxp[S:S + SL, :] = zero8
        uxp[0:SL, :] = zero8
        acc8[...] = jnp.zeros_like(acc8)
        drw8[...] = jnp.zeros_like(drw8)
        lamv = lam_ref[...]
        ls = _log_sigmoid(lamv)

        def fill(t, carry):
            base = pl.multiple_of(t * tt, tt)
            a_sc[pl.ds(base, tt), :] = jnp.exp(RG_C * ra_ref[pl.ds(base, tt), :] * ls)
            hp[pl.ds(base + SL, tt), :] = h_ref[pl.ds(base, tt), :]
            uxp[pl.ds(base + SL, tt), :] = ux_ref[pl.ds(base, tt), :]
            return carry

        lax.fori_loop(0, S // tt, fill, 0)

        rows = lax.broadcasted_iota(I32, (SL, LANES), 0)
        nt8 = S // SL

        def rscan(t, gnext):
            base = pl.multiple_of((nt8 - 1 - t) * SL, SL)
            A = a_sc[pl.ds(base + 1, SL), :]
            B = dh_ref[pl.ds(base, SL), :]
            for d in (1, 2, 4):
                As = jnp.where(rows < SL - d, pltpu.roll(A, SL - d, axis=0), 1.0)
                Bs = jnp.where(rows < SL - d, pltpu.roll(B, SL - d, axis=0), 0.0)
                B = A * Bs + B
                A = A * As
            g = B + A * gnext
            g_sc[pl.ds(base, SL), :] = g
            return jnp.broadcast_to(g[0:1, :], (SL, LANES))

        lax.fori_loop(0, nt8, rscan, zero8)

        def red8(v):
            return v.reshape(tt // SL, SL, LANES).sum(axis=0)

        def step(t, carry):
            base = pl.multiple_of(t * tt, tt)
            g = g_sc[pl.ds(base, tt), :]
            hprev = hp[pl.ds(base + SL - 1, tt), :]
            xr_t = xr_ref[pl.ds(base, tt), :]
            ra_t = ra_ref[pl.ds(base, tt), :]
            ii_t = ii_ref[pl.ds(base, tt), :]
            a, mult = _rg_gate_terms(ra_t, ls)
            gx = g * xr_t
            dmult = gx * ii_t
            dii = gx * mult
            dxr = g * (mult * ii_t)
            dla = g * hprev * a - dmult * (a * a) / mult
            acc8[3] += red8(dla * ra_t)
            dpa = dla * (RG_C * ls) * ra_t * (1.0 - ra_t)
            dpi = dii * ii_t * (1.0 - ii_t)
            dpab = dpa.astype(BF)
            dpib = dpi.astype(BF)
            dxr = dxr + (lax.dot_general(dpab, bda_ref[...], CONTRACT_LAST, preferred_element_type=F32)
                         + lax.dot_general(dpib, bdi_ref[...], CONTRACT_LAST, preferred_element_type=F32))
            dpa_sc[pl.ds(base, tt), :] = dpab
            dpi_sc[pl.ds(base, tt), :] = dpib
            dxp[pl.ds(base, tt), :] = dxr
            acc8[0] += red8(dxr)
            acc8[1] += red8(dpa)
            acc8[2] += red8(dpi)
            return carry

        lax.fori_loop(0, S // tt, step, 0)

        def convb(t, carry):
            base = pl.multiple_of(t * tt, tt)
            d_t = dxp[pl.ds(base, tt), :]
            dux = jnp.zeros((tt, LANES), F32)
            for k in range(KW):
                drw8[k] += red8(d_t * uxp[pl.ds(base + (SL - KW + 1) + k, tt), :])
                dux = dux + w_ref[k:k + 1, :] * dxp[pl.ds(base + (KW - 1) - k, tt), :]
            dux_ref[pl.ds(base, tt), :] = dux
            return carry

        lax.fori_loop(0, S // tt, convb, 0)

        xb = xr_ref[...].astype(BF)
        dwa_ref[...] = lax.dot_general(xb, dpa_sc[...], CONTRACT_FIRST, preferred_element_type=F32)
        dwi_ref[...] = lax.dot_general(xb, dpi_sc[...], CONTRACT_FIRST, preferred_element_type=F32)
        drw_ref[...] = jnp.zeros_like(drw_ref)
        vec_ref[...] = jnp.zeros_like(vec_ref)
        for k in range(KW):
            drw_ref[k:k + 1, :] = _colsum(drw8[k])
        for k in range(3):
            vec_ref[k:k + 1, :] = _colsum(acc8[k])
        vec_ref[3:4, :] = _colsum(acc8[3]) * (RG_C * _sigmoid(-lamv))

    col = lambda o: pl.BlockSpec((S, LANES), lambda c: (0, c + o))
    vec = pl.BlockSpec((1, LANES), lambda c: (0, c))
    diag = pl.BlockSpec((LANES, LANES), lambda c: (c, c))
    blk3 = lambda r: pl.BlockSpec((None, r, LANES), lambda c: (c, 0, 0))
    return _pcall(
        body, name=name, grid=(nb,),
        in_specs=[col(0), col(0), col(0), col(0), col(0), col(2 * nb), blk3(SL), diag, diag, vec],
        out_specs=[col(0), blk3(LANES), blk3(LANES), blk3(SL), pl.BlockSpec((SL, LANES), lambda c: (0, c))],
        out_shape=[jax.ShapeDtypeStruct((S, M), F32), jax.ShapeDtypeStruct((nb, LANES, LANES), F32),
                   jax.ShapeDtypeStruct((nb, LANES, LANES), F32), jax.ShapeDtypeStruct((nb, SL, LANES), F32),
                   jax.ShapeDtypeStruct((SL, M), F32)],
        scratch_shapes=[pltpu.VMEM((S + SL, LANES), F32), pltpu.VMEM((S + SL, LANES), F32), pltpu.VMEM((S, LANES), F32),
                        pltpu.VMEM((S, LANES), BF), pltpu.VMEM((S, LANES), BF), pltpu.VMEM((S + SL, LANES), F32),
                        pltpu.VMEM((S + SL, LANES), F32), pltpu.VMEM((SL, SL, LANES), F32), pltpu.VMEM((SL, SL, LANES), F32)],
        compiler_params=_params(("arbitrary",)),
    )(dhout, h, xr, ra, ii, proj, rnnw4, bda, bdi, lam)


GELU_K = 0.7978845608028654
GELU_C = 0.044715


def _layernorm_parts(cq):
    mu = _rowmean(cq)
    d = cq - mu
    rstd = lax.rsqrt(_rowmean(d * d) + EPS)
    return d * rstd, rstd


def mix_out(cq, proj, h, x, vecs, lnv, wout, name):
    S, D = x.shape
    M = cq.shape[1]
    ts = min(512, S)

    def body(cq_ref, uy_ref, h_ref, x_ref, v_ref, ln_ref, w_ref, xo_ref, ym_ref, yc_ref):
        z, _ = _layernorm_parts(cq_ref[...])
        l = z * _row(ln_ref, 0) + _row(ln_ref, 1)
        yc_ref[:, 0:M] = (l * _sigmoid(l)).astype(BF)
        uy = uy_ref[...]
        gelu = 0.5 * uy * (1.0 + jnp.tanh(GELU_K * (uy + GELU_C * uy * uy * uy)))
        yc_ref[:, M:2 * M] = (gelu * h_ref[...]).astype(BF)
        ym = jnp.dot(yc_ref[...], w_ref[...], preferred_element_type=F32)
        ym_ref[...] = ym
        xo_ref[...] = x_ref[...] + _row(v_ref, R_GT2) * ym

    tok = pl.BlockSpec((ts, D), lambda i: (i, 0))
    mtok = lambda o: pl.BlockSpec((ts, M), lambda i: (i, o))
    return _pcall(
        body, name=name, grid=(S // ts,),
        in_specs=[mtok(0), mtok(3), mtok(0), tok, pl.BlockSpec(vecs.shape, lambda i: (0, 0)),
                  pl.BlockSpec(lnv.shape, lambda i: (0, 0)), pl.BlockSpec(wout.shape, lambda i: (0, 0))],
        out_specs=[tok, tok, pl.BlockSpec((ts, 2 * M), lambda i: (i, 0))],
        out_shape=[jax.ShapeDtypeStruct((S, D), F32), jax.ShapeDtypeStruct((S, D), F32),
                   jax.ShapeDtypeStruct((S, 2 * M), BF)],
        compiler_params=_params(("arbitrary",)),
    )(cq, proj, h, x, vecs, lnv, wout)


def mix_out_bwd(dxo, ym, vecs, wout, cq, lnv, proj, h, name):
    S, D = dxo.shape
    M = cq.shape[1]
    ts = min(512, S)

    def body(dxo_ref, ym_ref, v_ref, w_ref, cq_ref, ln_ref, uy_ref, h_ref,
             dcq_ref, dh_ref, duy_ref, dyb_ref, vgd_ref, vgm_ref):
        @pl.when(pl.program_id(0) == 0)
        def _():
            vgd_ref[...] = jnp.zeros_like(vgd_ref)
            vgm_ref[...] = jnp.zeros_like(vgm_ref)

        dxo_v = dxo_ref[...]
        dyb = (_row(v_ref, R_GT2) * dxo_v).astype(BF)
        dyb_ref[...] = dyb
        vgd_ref[0:1, :] += _colsum(dxo_v * ym_ref[...])
        dycat = lax.dot_general(dyb, w_ref[...], CONTRACT_LAST, preferred_element_type=F32)
        dyc = dycat[:, 0:M]
        dyr = dycat[:, M:2 * M]
        z, rstd = _layernorm_parts(cq_ref[...])
        lng = _row(ln_ref, 0)
        l = z * lng + _row(ln_ref, 1)
        sl = _sigmoid(l)
        dl = dyc * (sl * (1.0 + l * (1.0 - sl)))
        vgm_ref[0:1, :] += _colsum(dl * z)
        vgm_ref[1:2, :] += _colsum(dl)
        dz = dl * lng
        dcq_ref[...] = rstd * (dz - _rowmean(dz) - z * _rowmean(dz * z))
        uy = uy_ref[...]
        u2 = uy * uy
        th = jnp.tanh(GELU_K * (uy + GELU_C * uy * u2))
        gelu = 0.5 * uy * (1.0 + th)
        dgelu = 0.5 * (1.0 + th) + 0.5 * uy * (1.0 - th * th) * (GELU_K * (1.0 + 3.0 * GELU_C * u2))
        dh_ref[...] = dyr * gelu
        duy_ref[...] = dyr * h_ref[...] * dgelu

    tok = pl.BlockSpec((ts, D), lambda i: (i, 0))
    mtok = lambda o: pl.BlockSpec((ts, M), lambda i: (i, o))
    return _pcall(
        body, name=name, grid=(S // ts,),
        in_specs=[tok, tok, pl.BlockSpec(vecs.shape, lambda i: (0, 0)), pl.BlockSpec(wout.shape, lambda i: (0, 0)),
                  mtok(0), pl.BlockSpec(lnv.shape, lambda i: (0, 0)), mtok(3), mtok(0)],
        out_specs=[mtok(0), mtok(0), mtok(0), tok, pl.BlockSpec((SUBLANES, D), lambda i: (0, 0)),
                   pl.BlockSpec((SUBLANES, M), lambda i: (0, 0))],
        out_shape=[jax.ShapeDtypeStruct((S, M), F32)] * 3 + [jax.ShapeDtypeStruct((S, D), BF),
                   jax.ShapeDtypeStruct((SUBLANES, D), F32), jax.ShapeDtypeStruct((SUBLANES, M), F32)],
        compiler_params=_params(("arbitrary",)),
    )(dxo, ym, vecs, wout, cq, lnv, proj, h)


def mix_in_bwd(dparts, x, dxo, vecs, win, name):
    S, D = x.shape
    M = dparts[0].shape[1]
    ts = min(512, S)

    def body(d0, d1, d2, d3, x_ref, dxo_ref, v_ref, w_ref, dx_ref, hb_ref, dp_ref, vg_ref):
        @pl.when(pl.program_id(0) == 0)
        def _():
            vg_ref[...] = jnp.zeros_like(vg_ref)

        for q, dref in enumerate((d0, d1, d2, d3)):
            dp_ref[:, q * M:(q + 1) * M] = dref[...].astype(BF)
        dh = lax.dot_general(dp_ref[...], w_ref[...], CONTRACT_LAST, preferred_element_type=F32)
        xv = x_ref[...]
        r = lax.rsqrt(_rowmean(xv * xv) + EPS)
        n = xv * r
        g = _row(v_ref, R_G2)
        sc1 = 1.0 + _row(v_ref, R_SC2)
        gsc = g * sc1
        hb_ref[...] = (n * gsc + _row(v_ref, R_SH2)).astype(BF)
        dhn = dh * n
        vg_ref[0:1, :] += _colsum(dh)
        vg_ref[1:2, :] += _colsum(dhn) * g
        vg_ref[2:3, :] += _colsum(dhn) * sc1
        dn = dh * gsc
        dx_ref[...] = dxo_ref[...] + r * (dn - n * _rowmean(dn * n))

    tok = pl.BlockSpec((ts, D), lambda i: (i, 0))
    mtok = pl.BlockSpec((ts, M), lambda i: (i, 0))
    return _pcall(
        body, name=name, grid=(S // ts,),
        in_specs=[mtok] * 4 + [tok, tok, pl.BlockSpec(vecs.shape, lambda i: (0, 0)), pl.BlockSpec(win.shape, lambda i: (0, 0))],
        out_specs=[tok, tok, pl.BlockSpec((ts, 4 * M), lambda i: (i, 0)), pl.BlockSpec((SUBLANES, D), lambda i: (0, 0))],
        out_shape=[jax.ShapeDtypeStruct((S, D), F32), jax.ShapeDtypeStruct((S, D), BF),
                   jax.ShapeDtypeStruct((S, 4 * M), BF), jax.ShapeDtypeStruct((SUBLANES, D), F32)],
        compiler_params=_params(("arbitrary",)),
    )(*dparts, x, dxo, vecs, win)


def _adamw(w, g, m, v):
    m = ADAM_B1 * m + (1.0 - ADAM_B1) * g
    v = ADAM_B2 * v + (1.0 - ADAM_B2) * (g * g)
    m_hat = m / (1.0 - ADAM_B1 ** ADAM_STEP)
    v_hat = v / (1.0 - ADAM_B2 ** ADAM_STEP)
    delta = -ADAM_LR * (m_hat / (jnp.sqrt(v_hat) + ADAM_EPS) + ADAM_WD * w)
    return delta, m, v


def adam_big(w, g, m, v, name):
    R, C = w.shape
    tr = 256 if R % 256 == 0 else R // 2 if (R // 2) % SUBLANES == 0 and R > 512 else R
    tc = C if C <= 1536 else (1152 if C % 1152 == 0 else 1024)
    assert R % tr == 0 and C % tc == 0

    def body(w_ref, g_ref, m_ref, v_ref, d_ref, nm_ref, nv_ref):
        d, nm, nv = _adamw(w_ref[...], g_ref[...], m_ref[...], v_ref[...])
        d_ref[...] = d
        nm_ref[...] = nm
        nv_ref[...] = nv

    blk = pl.BlockSpec((tr, tc), lambda i, j: (i, j))
    return _pcall(
        body, name=name, grid=(R // tr, C // tc), in_specs=[blk] * 4, out_specs=[blk] * 3,
        out_shape=[jax.ShapeDtypeStruct((R, C), F32)] * 3, compiler_params=_params(("parallel", "parallel")),
    )(w, g, m, v)


def adam_cond(c_all, dmod, w, m, v, name):
    B, Kin = c_all.shape
    N = w.shape[1]
    tn = 256
    assert N % tn == 0

    def body(c_ref, d_ref, w_ref, m_ref, v_ref, g_ref, dl_ref, nm_ref, nv_ref):
        cv = c_ref[...]
        ca = cv * _sigmoid(cv)
        g = lax.dot_general(ca, d_ref[...], CONTRACT_FIRST, preferred_element_type=F32, precision=lax.Precision.HIGHEST)
        d, nm, nv = _adamw(w_ref[...], g, m_ref[...], v_ref[...])
        g_ref[...] = g
        dl_ref[...] = d
        nm_ref[...] = nm
        nv_ref[...] = nv

    blk = pl.BlockSpec((Kin, tn), lambda n: (0, n))
    return _pcall(
        body, name=name, grid=(N // tn,),
        in_specs=[pl.BlockSpec((B, Kin), lambda n: (0, 0)), pl.BlockSpec((B, tn), lambda n: (0, n)), blk, blk, blk],
        out_specs=[blk] * 4, out_shape=[jax.ShapeDtypeStruct((Kin, N), F32)] * 4,
        compiler_params=_params(("parallel",)),
    )(c_all, dmod, w, m, v)


def adam_small(ws, gs, ms, vs, name):
    n = len(ws)

    def body(*refs):
        ins, outs = refs[:4 * n], refs[4 * n:]
        for k in range(n):
            d, nm, nv = _adamw(ins[k][...], ins[n + k][...], ins[2 * n + k][...], ins[3 * n + k][...])
            outs[k][...] = d
            outs[n + k][...] = nm
            outs[2 * n + k][...] = nv

    vm = pl.BlockSpec(memory_space=pltpu.VMEM)
    shapes = [jax.ShapeDtypeStruct(w.shape, F32) for w in ws]
    out = _pcall(body, name=name, in_specs=[vm] * (4 * n), out_specs=[vm] * (3 * n), out_shape=shapes * 3,
                 compiler_params=_params())(*ws, *gs, *ms, *vs)
    return out[:n], out[n:2 * n], out[2 * n:]


def _me():
    return lax.axis_index("x"), lax.axis_index("y"), lax.axis_index("c")


def _flip(x, y, p):
    return (x ^ (p >> 1) if (p >> 1) else x), (y ^ (p & 1) if (p & 1) else y)


def _handshake(peers):
    barrier = pltpu.get_barrier_semaphore()
    for peer in peers:
        pl.semaphore_signal(barrier, inc=1, device_id=peer, device_id_type=MESH)
    pl.semaphore_wait(barrier, len(peers))


def _seq_call(body, *, name, n_in, out_shape, sem_shapes, collective_id):
    del n_in
    return pl.kernel(body, out_type=out_shape, mesh=plsc.ScalarSubcoreMesh(axis_name="sq", num_cores=1), name=name,
                     scratch_types=sem_shapes, compiler_params=pltpu.CompilerParams(collective_id=collective_id))


def _hbm_comm_call(body, *, name, n_in, out_shape, sem_shapes, seq_id):
    if seq_id is not None:
        return _seq_call(body, name=name, n_in=n_in, out_shape=out_shape, sem_shapes=sem_shapes, collective_id=seq_id)
    anyspec = pl.BlockSpec(memory_space=pl.ANY)
    return _pcall(body, name=name, in_specs=[anyspec] * n_in, out_specs=[anyspec] * len(out_shape), out_shape=out_shape,
                  scratch_shapes=sem_shapes, compiler_params=_params())


def allgather_devices(v, name, with_sum=False):
    R, L = v.shape

    def body(v_ref, out_ref, *rest):
        if with_sum:
            sum_ref, send_sems, recv_sems = rest
        else:
            send_sems, recv_sems = rest
        x, y, c = _me()
        me = 4 * x + 2 * y + c
        out_ref[me] = v_ref[...]
        copies = []
        for p in range(1, N_DEV):
            px, py = _flip(x, y, p >> 1)
            pc = (1 - c) if (p & 1) else c
            peer = 4 * px + 2 * py + pc
            send = pltpu.make_async_remote_copy(src_ref=v_ref, dst_ref=out_ref.at[me], send_sem=send_sems.at[p - 1],
                                                recv_sem=recv_sems.at[p - 1], device_id=(px, py, pc), device_id_type=MESH)
            send.start()
            recv = pltpu.make_async_remote_copy(src_ref=v_ref, dst_ref=out_ref.at[peer], send_sem=send_sems.at[p - 1],
                                                recv_sem=recv_sems.at[p - 1], device_id=(px, py, pc), device_id_type=MESH)
            copies.append((send, recv))
        for send, recv in copies:
            recv.wait_recv()
        for send, recv in copies:
            send.wait_send()
        if with_sum:
            s = out_ref[0]
            for k in range(1, N_DEV):
                s = s + out_ref[k]
            sum_ref[...] = s

    vm = pl.BlockSpec(memory_space=pltpu.VMEM)
    out_shape = [jax.ShapeDtypeStruct((N_DEV, R, L), F32)]
    if with_sum:
        out_shape.append(jax.ShapeDtypeStruct((R, L), F32))
    return _pcall(
        body, name=name, in_specs=[vm], out_specs=[vm] * len(out_shape), out_shape=out_shape,
        scratch_shapes=[pltpu.SemaphoreType.DMA((N_DEV - 1,)), pltpu.SemaphoreType.DMA((N_DEV - 1,))],
        compiler_params=_params(),
    )(v)


def allgather_devices_hbm(v, name, seq_id):
    R, L = v.shape

    def body(v_ref, out_ref, send_sems, recv_sems, local_sem):
        x, y, c = _me()
        me = 4 * x + 2 * y + c
        peers = []
        for p in range(1, N_DEV):
            px, py = _flip(x, y, p >> 1)
            peers.append((px, py, (1 - c) if (p & 1) else c))
        _handshake(peers)
        lc = pltpu.make_async_copy(v_ref, out_ref.at[me], local_sem)
        lc.start()
        copies = []
        for p, (px, py, pc) in enumerate(peers):
            send = pltpu.make_async_remote_copy(src_ref=v_ref, dst_ref=out_ref.at[me], send_sem=send_sems.at[p],
                                                recv_sem=recv_sems.at[p], device_id=(px, py, pc), device_id_type=MESH)
            send.start()
            recv = pltpu.make_async_remote_copy(src_ref=v_ref, dst_ref=out_ref.at[4 * px + 2 * py + pc], send_sem=send_sems.at[p],
                                                recv_sem=recv_sems.at[p], device_id=(px, py, pc), device_id_type=MESH)
            copies.append((send, recv))
        for send, recv in copies:
            recv.wait_recv()
        for send, recv in copies:
            send.wait_send()
        lc.wait()

    return _seq_call(body, name=name, n_in=1, out_shape=[jax.ShapeDtypeStruct((N_DEV, R, L), F32)],
                     sem_shapes=[pltpu.SemaphoreType.DMA((N_DEV - 1,)), pltpu.SemaphoreType.DMA((N_DEV - 1,)),
                                 pltpu.SemaphoreType.DMA], collective_id=seq_id)(v)[0]


def sum_slots(g, name):
    n, R, L = g.shape
    tr = 216 if R % 216 == 0 else R
    assert R % tr == 0 and tr % SUBLANES == 0

    def body(g_ref, o_ref):
        s = g_ref[0]
        for k in range(1, n):
            s = s + g_ref[k]
        o_ref[...] = s

    return _pcall(body, name=name, grid=(R // tr,), in_specs=[pl.BlockSpec((n, tr, L), lambda i: (0, i, 0))],
                  out_specs=pl.BlockSpec((tr, L), lambda i: (i, 0)), out_shape=jax.ShapeDtypeStruct((R, L), F32),
                  compiler_params=_params(("parallel",)))(g)


def allgather_chips(v, name):
    R, L = v.shape

    def body(v_ref, out_ref, send_sems, recv_sems):
        x, y, c = _me()
        chip = 2 * x + y
        out_ref[chip] = v_ref[...]
        copies = []
        for p in range(1, N_CHIPS):
            px, py = _flip(x, y, p)
            send = pltpu.make_async_remote_copy(src_ref=v_ref, dst_ref=out_ref.at[chip], send_sem=send_sems.at[p - 1],
                                                recv_sem=recv_sems.at[p - 1], device_id=(px, py, c), device_id_type=MESH)
            send.start()
            recv = pltpu.make_async_remote_copy(src_ref=v_ref, dst_ref=out_ref.at[2 * px + py], send_sem=send_sems.at[p - 1],
                                                recv_sem=recv_sems.at[p - 1], device_id=(px, py, c), device_id_type=MESH)
            copies.append((send, recv))
        for send, recv in copies:
            recv.wait_recv()
        for send, recv in copies:
            send.wait_send()

    vm = pl.BlockSpec(memory_space=pltpu.VMEM)
    return _pcall(
        body, name=name, in_specs=[vm], out_specs=vm, out_shape=jax.ShapeDtypeStruct((N_CHIPS, R, L), F32),
        scratch_shapes=[pltpu.SemaphoreType.DMA((N_CHIPS - 1,)), pltpu.SemaphoreType.DMA((N_CHIPS - 1,))],
        compiler_params=_params(),
    )(v)


def _shard_window(ref, kind, shard_shape, chip, half):
    r, c = shard_shape
    hr = r // 2
    if kind == "col":
        return ref.at[pl.ds(pl.multiple_of(half * hr, hr), hr), pl.ds(pl.multiple_of(chip * c, c), c)]
    return ref.at[pl.ds(pl.multiple_of(chip * r + half * hr, hr), hr), :]


def allgather_weights(shards, kinds, name, seq_id=None):
    n = len(shards)
    fulls = []
    for s, kind in zip(shards, kinds):
        r, c = s.shape
        fulls.append(jax.ShapeDtypeStruct((r, N_CHIPS * c) if kind == "col" else (N_CHIPS * r, c), s.dtype))

    def body(*refs):
        srcs, outs = refs[:n], refs[n:2 * n]
        send_sems, recv_sems, fsend_sems, frecv_sems, local_sems = refs[2 * n:]
        x, y, c = _me()
        chip = 2 * x + y
        sib = (x, y, 1 - c)
        if seq_id is not None:
            _handshake([(*_flip(x, y, p), c) for p in range(1, N_CHIPS)] + [sib])
        locals_, sends, fwds = [], [], []
        for i in range(n):
            shp = srcs[i].shape
            hr = shp[0] // 2
            win_all = (outs[i].at[:, pl.ds(pl.multiple_of(chip * shp[1], shp[1]), shp[1])] if kinds[i] == "col"
                       else outs[i].at[pl.ds(pl.multiple_of(chip * shp[0], shp[0]), shp[0]), :])
            lc = pltpu.make_async_copy(srcs[i], win_all, local_sems.at[i])
            lc.start()
            locals_.append(lc)
            my_half = srcs[i].at[pl.ds(pl.multiple_of(c * hr, hr), hr), :]
            for p in range(1, N_CHIPS):
                px, py = _flip(x, y, p)
                k = i * (N_CHIPS - 1) + p - 1
                cp = pltpu.make_async_remote_copy(src_ref=my_half, dst_ref=_shard_window(outs[i], kinds[i], shp, chip, c),
                                                  send_sem=send_sems.at[k], recv_sem=recv_sems.at[k],
                                                  device_id=(px, py, c), device_id_type=MESH)
                cp.start()
                sends.append(cp)
        for i in range(n):
            shp = srcs[i].shape
            for p in range(1, N_CHIPS):
                px, py = _flip(x, y, p)
                k = i * (N_CHIPS - 1) + p - 1
                landed = _shard_window(outs[i], kinds[i], shp, 2 * px + py, c)
                pltpu.make_async_remote_copy(src_ref=landed, dst_ref=landed, send_sem=send_sems.at[k], recv_sem=recv_sems.at[k],
                                             device_id=(px, py, c), device_id_type=MESH).wait_recv()
                fw = pltpu.make_async_remote_copy(src_ref=landed, dst_ref=landed, send_sem=fsend_sems.at[k],
                                                  recv_sem=frecv_sems.at[k], device_id=sib, device_id_type=MESH)
                fw.start()
                fwds.append(fw)
        for i in range(n):
            shp = srcs[i].shape
            for p in range(1, N_CHIPS):
                px, py = _flip(x, y, p)
                k = i * (N_CHIPS - 1) + p - 1
                other = _shard_window(outs[i], kinds[i], shp, 2 * px + py, 1 - c)
                pltpu.make_async_remote_copy(src_ref=other, dst_ref=other, send_sem=fsend_sems.at[k], recv_sem=frecv_sems.at[k],
                                             device_id=sib, device_id_type=MESH).wait_recv()
        for cp in sends + fwds:
            cp.wait_send()
        for lc in locals_:
            lc.wait()

    nk = n * (N_CHIPS - 1)
    return _hbm_comm_call(
        body, name=name, n_in=n, out_shape=fulls, seq_id=seq_id,
        sem_shapes=[pltpu.SemaphoreType.DMA((nk,)), pltpu.SemaphoreType.DMA((nk,)), pltpu.SemaphoreType.DMA((nk,)),
                    pltpu.SemaphoreType.DMA((nk,)), pltpu.SemaphoreType.DMA((n,))],
    )(*shards)


def _as_halves(g, kind, shard_shape):
    r, c = shard_shape
    if kind == "col":
        return g.reshape(2, r // 2, N_CHIPS * c)
    return g.reshape(N_CHIPS, 2, r // 2, c)


def exchange_sibling_halves(grads, kinds, shard_shapes, name, seq_id=None):
    n = len(grads)
    views = [_as_halves(g, k, s) for g, k, s in zip(grads, kinds, shard_shapes)]
    outs = []
    for k, (r, c) in zip(kinds, shard_shapes):
        outs.append(jax.ShapeDtypeStruct((r // 2, N_CHIPS * c) if k == "col" else (N_CHIPS, r // 2, c), F32))

    def body(*refs):
        srcs, dsts = refs[:n], refs[n:2 * n]
        send_sems, recv_sems = refs[2 * n:]
        x, y, c = _me()
        if seq_id is not None:
            _handshake([(x, y, 1 - c)])
        cps = []
        for i in range(n):
            src = srcs[i].at[1 - c] if kinds[i] == "col" else srcs[i].at[:, 1 - c]
            cp = pltpu.make_async_remote_copy(src_ref=src, dst_ref=dsts[i], send_sem=send_sems.at[i], recv_sem=recv_sems.at[i],
                                              device_id=(x, y, 1 - c), device_id_type=MESH)
            cp.start()
            cps.append(cp)
        for cp in cps:
            cp.wait_recv()
        for cp in cps:
            cp.wait_send()

    return _hbm_comm_call(body, name=name, n_in=n, out_shape=outs, seq_id=seq_id,
                          sem_shapes=[pltpu.SemaphoreType.DMA((n,)), pltpu.SemaphoreType.DMA((n,))])(*views)


def add_sibling_half(g, recv, kind, shard_shape, cidx, name):
    r, c = shard_shape
    hr = r // 2
    gv = _as_halves(g, kind, shard_shape)
    tr = hr if hr <= 512 else (256 if hr % 256 == 0 else hr // 2)
    assert hr % tr == 0

    def body(ci_ref, g_ref, r_ref, h_ref, hb_ref):
        s = g_ref[...] + r_ref[...]
        h_ref[...] = s
        hb_ref[...] = s.astype(BF)

    if kind == "col":
        grid = (hr // tr, N_CHIPS)
        g_spec = pl.BlockSpec((None, tr, c), lambda i, k, ci: (ci[0], i, k))
        o_spec = pl.BlockSpec((tr, c), lambda i, k, ci: (i, k))
    else:
        grid = (hr // tr, N_CHIPS)
        g_spec = pl.BlockSpec((None, None, tr, c), lambda i, k, ci: (k, ci[0], i, 0))
        o_spec = pl.BlockSpec((None, tr, c), lambda i, k, ci: (k, i, 0))
    gs = pltpu.PrefetchScalarGridSpec(num_scalar_prefetch=1, grid=grid, in_specs=[g_spec, o_spec], out_specs=[o_spec, o_spec])
    return _pcall(
        body, name=name, grid_spec=gs,
        out_shape=[jax.ShapeDtypeStruct(recv.shape, F32), jax.ShapeDtypeStruct(recv.shape, BF)],
        compiler_params=_params(("parallel", "parallel")),
    )(cidx, gv, recv)


def exchange_chip_pieces(hbs, kinds, shard_shapes, name, seq_id=None):
    n = len(hbs)
    outs = [jax.ShapeDtypeStruct((N_CHIPS - 1, r // 2, c), BF) for (r, c) in shard_shapes]

    def body(*refs):
        srcs, dsts = refs[:n], refs[n:2 * n]
        send_sems, recv_sems = refs[2 * n:]
        x, y, c = _me()
        if seq_id is not None:
            _handshake([(*_flip(x, y, p), c) for p in range(1, N_CHIPS)])
        cps = []
        for i in range(n):
            cc = shard_shapes[i][1]
            for p in range(1, N_CHIPS):
                px, py = _flip(x, y, p)
                pchip = 2 * px + py
                src = (srcs[i].at[:, pl.ds(pl.multiple_of(pchip * cc, cc), cc)] if kinds[i] == "col" else srcs[i].at[pchip])
                k = i * (N_CHIPS - 1) + p - 1
                cp = pltpu.make_async_remote_copy(src_ref=src, dst_ref=dsts[i].at[p - 1], send_sem=send_sems.at[k],
                                                  recv_sem=recv_sems.at[k], device_id=(px, py, c), device_id_type=MESH)
                cp.start()
                cps.append(cp)
        for cp in cps:
            cp.wait_recv()
        for cp in cps:
            cp.wait_send()

    nk = n * (N_CHIPS - 1)
    return _hbm_comm_call(body, name=name, n_in=n, out_shape=outs, seq_id=seq_id,
                          sem_shapes=[pltpu.SemaphoreType.DMA((nk,)), pltpu.SemaphoreType.DMA((nk,))])(*hbs)


def sum_chip_pieces(h, pieces, kind, shard_shape, chip_core, name):
    r, c = shard_shape
    hr = r // 2
    tr = hr if hr <= 512 else (256 if hr % 256 == 0 else hr // 2)
    assert hr % tr == 0
    nrb = hr // tr

    def body(ci_ref, h_ref, p_ref, q_ref):
        q_ref[...] = ((h_ref[...] + p_ref[0].astype(F32)) + p_ref[1].astype(F32)) + p_ref[2].astype(F32)

    if kind == "col":
        h_spec = pl.BlockSpec((tr, c), lambda i, ci: (i, ci[0]))
    else:
        h_spec = pl.BlockSpec((None, tr, c), lambda i, ci: (ci[0], i, 0))
    gs = pltpu.PrefetchScalarGridSpec(
        num_scalar_prefetch=1, grid=(nrb,),
        in_specs=[h_spec, pl.BlockSpec((N_CHIPS - 1, tr, c), lambda i, ci: (0, i, 0))],
        out_specs=pl.BlockSpec((tr, c), lambda i, ci: (ci[1] * nrb + i, 0)))
    return _pcall(body, name=name, grid_spec=gs, out_shape=jax.ShapeDtypeStruct((r, c), F32),
                  compiler_params=_params(("parallel",)))(chip_core, h, pieces)


def exchange_reduced_halves(qs, name):
    n = len(qs)

    def body(*refs):
        bufs = refs[n:2 * n]
        send_sems, recv_sems = refs[2 * n:]
        x, y, c = _me()
        cps = []
        for i in range(n):
            hr = bufs[i].shape[0] // 2
            mine = bufs[i].at[pl.ds(pl.multiple_of(c * hr, hr), hr), :]
            other = bufs[i].at[pl.ds(pl.multiple_of((1 - c) * hr, hr), hr), :]
            cp = pltpu.make_async_remote_copy(src_ref=mine, dst_ref=mine, send_sem=send_sems.at[i], recv_sem=recv_sems.at[i],
                                              device_id=(x, y, 1 - c), device_id_type=MESH)
            cp.start()
            cps.append((cp, pltpu.make_async_remote_copy(src_ref=other, dst_ref=other, send_sem=send_sems.at[i],
                                                         recv_sem=recv_sems.at[i], device_id=(x, y, 1 - c), device_id_type=MESH)))
        for cp, rv in cps:
            rv.wait_recv()
        for cp, rv in cps:
            cp.wait_send()

    anyspec = pl.BlockSpec(memory_space=pl.ANY)
    return _pcall(
        body, name=name, in_specs=[anyspec] * n, out_specs=[anyspec] * n,
        out_shape=[jax.ShapeDtypeStruct(q.shape, F32) for q in qs], input_output_aliases={i: i for i in range(n)},
        scratch_shapes=[pltpu.SemaphoreType.DMA((n,)), pltpu.SemaphoreType.DMA((n,))],
        compiler_params=_params(),
    )(*qs)


def _rows128(a):
    return a.reshape(-1, LANES)


def _after(xs, *deps):
    flat = []
    for d in deps:
        flat.extend(d if isinstance(d, (list, tuple)) else [d])
    return list(lax.optimization_barrier((tuple(xs), tuple(flat)))[0])


def _block_diag(w):
    H, d, _ = w.shape
    eye = jnp.eye(H, dtype=w.dtype)
    return jnp.einsum("hde,hg->hdge", w, eye).reshape(H * d, H * d)


def _diag_blocks(g4, H, d):
    nb = g4.shape[0]
    per = LANES // d
    g = g4.reshape(nb, per, d, per, d)
    return jnp.stack([g[:, j, :, j, :] for j in range(per)], axis=1).reshape(H, d, d)


def kernel(x, c, w_mod, b_mod, g_ffn1, w_ffn1_in, w_ffn1_out, g_mix, w_in, conv_w, conv_b, ln_g, ln_b, rnn_conv_w, rnn_conv_b, w_a, b_a, w_i, b_i, lru_lambda, w_out, g_ffn2, w_ffn2_in, w_ffn2_out, w_fmod, b_fmod, g_final, loss_target, m_w_mod, m_b_mod, m_g_ffn1, m_w_ffn1_in, m_w_ffn1_out, m_g_mix, m_w_in, m_conv_w, m_conv_b, m_ln_g, m_ln_b, m_rnn_conv_w, m_rnn_conv_b, m_w_a, m_b_a, m_w_i, m_b_i, m_lru_lambda, m_w_out, m_g_ffn2, m_w_ffn2_in, m_w_ffn2_out, m_w_fmod, m_b_fmod, m_g_final, v_w_mod, v_b_mod, v_g_ffn1, v_w_ffn1_in, v_w_ffn1_out, v_g_mix, v_w_in, v_conv_w, v_conv_b, v_ln_g, v_ln_b, v_rnn_conv_w, v_rnn_conv_b, v_w_a, v_b_a, v_w_i, v_b_i, v_lru_lambda, v_w_out, v_g_ffn2, v_w_ffn2_in, v_w_ffn2_out, v_w_fmod, v_b_fmod, v_g_final):
    S, D = x.shape[1], x.shape[2]
    M = conv_b.shape[1]
    H, HD = w_a.shape[1], w_a.shape[2]
    nb = M // LANES
    ix, iy, ic = lax.axis_index("x"), lax.axis_index("y"), lax.axis_index("c")
    chip = 2 * ix + iy
    dev = 2 * chip + ic
    cidx = jnp.reshape(ic, (1,)).astype(I32)
    chip_core = jnp.stack([chip, ic]).astype(I32)
    xs = x[0]
    tgt = loss_target[0]

    kinds = ["col", "row"]
    w_f1, w_mx, w_f2 = [w_ffn1_in[0], w_ffn1_out[0]], [w_in[0], w_out[0]], [w_ffn2_in[0], w_ffn2_out[0]]
    as_bf = lambda ws: [w.astype(BF) for w in ws]
    shapes_of = lambda ws: [w.shape for w in ws]
    wi1, wo1 = allgather_weights(as_bf(w_f1), kinds, "gather_ffn1", seq_id=9)
    win, wout = allgather_weights(as_bf(w_mx), kinds, "gather_mix", seq_id=1)
    wi2, wo2 = allgather_weights(as_bf(w_f2), kinds, "gather_ffn2", seq_id=2)

    c_all =allgather_devices(_rows128(c), "gather_c")[0].reshape(N_DEV, D)
    mod_cols = cond_matmul(c_all, w_mod[0], "mod_proj")
    fmod_cols = cond_matmul(c_all, w_fmod, "fmod_proj")
    convw_pad = jnp.pad(conv_w[0], ((0, 32 - CONV_WIDTH), (0, 0)))
    rnnw_pad = jnp.pad(rnn_conv_w[0], ((0, SUBLANES - RNN_CONV_WIDTH), (0, 0)))
    n_mod, n_fmod = mod_cols.shape[1], fmod_cols.shape[1]
    small = jnp.concatenate([_rows128(mod_cols), _rows128(fmod_cols), convw_pad, rnnw_pad], axis=0)
    small4 = allgather_chips(small, "gather_cond")
    r0 = N_DEV * n_mod // LANES
    r1 = r0 + N_DEV * n_fmod // LANES
    mod_all = small4[:, :r0].reshape(N_CHIPS, N_DEV, n_mod)
    fmod_all = small4[:, r0:r1].reshape(N_CHIPS, N_DEV, n_fmod)
    convw4 = small4[:, r1:r1 + 32]
    rnnw4 = small4[:, r1 + 32:r1 + 32 + SUBLANES]
    mod_row = lax.dynamic_index_in_dim(mod_all, dev, axis=1, keepdims=False).reshape(1, N_CHIPS * n_mod) + b_mod
    fmod_row = lax.dynamic_index_in_dim(fmod_all, dev, axis=1, keepdims=False).reshape(1, N_CHIPS * n_fmod) + b_fmod[None, :]
    vecs = jnp.concatenate([mod_row.reshape(9, D), fmod_row.reshape(2, D), g_ffn1, g_mix, g_ffn2, g_final[None, :],
                            jnp.zeros((1, D), F32)], axis=0)
    lnv = jnp.concatenate([ln_g, ln_b, jnp.zeros((SUBLANES - 2, M), F32)], axis=0)
    bda = _block_diag(w_a[0]).astype(BF)
    bdi = _block_diag(w_i[0]).astype(BF)

    def reduce_add(gs, recv, ws, tag, kinds_=kinds):
        pairs = [add_sibling_half(g, r_, k, w.shape, cidx, f"add_sibling_{tag}{j}")
                 for j, (g, r_, k, w) in enumerate(zip(gs, recv, kinds_, ws))]
        return [p[0] for p in pairs], [p[1] for p in pairs]

    def reduce_sum(hs_, recv, ws, tag, kinds_=kinds):
        return [sum_chip_pieces(h_, p_, k, w.shape, chip_core, f"sum_chips_{tag}{j}")
                for j, (h_, p_, k, w) in enumerate(zip(hs_, recv, kinds_, ws))]

    rows1 = (R_SH1, R_SC1, R_GT1, R_G1)
    rows3 = (R_SH3, R_SC3, R_GT3, R_G3)
    x1, g1s, u1s, y1 = ffn_fwd(xs, vecs, wi1, wo1, rows1, "ffn1_fwd")
    proj = norm_matmul(x1, vecs, win, (R_SH2, R_SC2, R_G2), "mix_in_proj")
    cq = conv_fwd(proj, convw4, conv_b, "conv_fwd")
    xr, ra, ii, hh = rnn_fwd(proj, rnnw4, rnn_conv_b, bda, bdi, b_a, b_i, lru_lambda, "rnn_fwd")
    x2, ym, ycat = mix_out(cq, proj, hh, x1, vecs, lnv, wout, "mix_out")
    x3, g2s, u2s, y2 = ffn_fwd(x2, vecs, wi2, wo2, rows3, "ffn2_fwd")
    dx3, vgf = final_fwd_bwd(x3, tgt, vecs, "final_loss")

    Fd = wo1.shape[0]
    tk = min(512, S)
    dx2, act2, dg2, du2, h3b, dy2b, vg3 = ffn_bwd(dx3, x2, vecs, g2s, u2s, y2, wi2, wo2, rows3, "ffn2_bwd")
    gwo2 = matmul(act2, dy2b, "tn", tm=Fd // 2, tn=D, tk=tk, name="ffn2_dwo")
    gwi2 = matmul(h3b, dg2, "tn", tm=D, tn=Fd // 2, tk=tk, name="ffn2_dwg", out_cols=2 * Fd)
    gwi2 = matmul(h3b, du2, "tn", tm=D, tn=Fd // 2, tk=tk, name="ffn2_dwu", out_cols=2 * Fd, col_off=Fd, prev=gwi2)
    recv1_f2 = exchange_sibling_halves([gwi2, gwo2], kinds, shapes_of(w_f2), "reduce1_ffn2", seq_id=3)
    dcq, dhout, duy, dymb, vgd, vgm = mix_out_bwd(dx2, ym, vecs, wout, cq, lnv, proj, hh, "mix_out_bwd")
    gwout = matmul(ycat, dymb, "tn", tm=2 * M, tn=D, tk=tk, name="mix_dwout")
    recv1_f2 = _after(recv1_f2, gwout)
    h_f2, hb_f2 = reduce_add([gwi2, gwo2], recv1_f2, w_f2, "ffn2_")
    recv2_f2 = exchange_chip_pieces(hb_f2, kinds, shapes_of(w_f2), "reduce2_ffn2", seq_id=4)
    duv, dug, dconvw4, dconvb = conv_bwd(_after([dcq], hb_f2)[0], proj, convw4, "conv_bwd")
    dux, dwa4, dwi4, drnnw4, rvec = rnn_bwd(dhout, hh, xr, ra, ii, proj, rnnw4, bda, bdi, lru_lambda, "rnn_bwd")
    dx1, h2b, dpb, vg2 = mix_in_bwd((duv, dug, dux, duy), x1, dx2, vecs, win, "mix_in_bwd")
    gwin = matmul(h2b, dpb, "tn", tm=D, tn=1024, tk=tk, name="mix_dwin")
    recv1_mx = exchange_sibling_halves([gwin, gwout], kinds, shapes_of(w_mx), "reduce1_mix", seq_id=5)
    g_f2 = exchange_reduced_halves(reduce_sum(_after(h_f2, gwin), recv2_f2, w_f2, "ffn2_"), "reduce3_ffn2")
    adam_f2 = [adam_big(w, g, m, v, "adam_" + nm) for w, g, m, v, nm in
               zip(w_f2, g_f2, [m_w_ffn2_in[0], m_w_ffn2_out[0]], [v_w_ffn2_in[0], v_w_ffn2_out[0]], ["ffn2_in", "ffn2_out"])]
    h_mx, hb_mx = reduce_add([gwin, gwout], _after(recv1_mx, adam_f2[0][0], adam_f2[1][0]), w_mx, "mix_")
    recv2_mx = exchange_chip_pieces(hb_mx, kinds, shapes_of(w_mx), "reduce2_mix", seq_id=6)
    dx0, act1, dg1, du1, h1b, dy1b, vg1 = ffn_bwd(_after([dx1], hb_mx)[0], xs, vecs, g1s, u1s, y1, wi1, wo1, rows1, "ffn1_bwd")
    dmod_row = jnp.concatenate([vg1[1:3], vg1[0:1], vg2[0:2], vgd[0:1], vg3[1:3], vg3[0:1]], axis=0)
    gains = jnp.concatenate([vg1[3:4], vg2[2:3], vg3[3:4], vgf[2:4]], axis=0)
    mvecs = jnp.concatenate([dconvb, vgm[0:2], rvec[0:4], jnp.zeros((1, M), F32)], axis=0)
    parts = [_rows128(dmod_row), _rows128(vgf[0:2]), _rows128(gains), _rows128(mvecs),
             _rows128(dconvw4), _rows128(drnnw4), _rows128(_diag_blocks(dwa4, H, HD)), _rows128(_diag_blocks(dwi4, H, HD))]
    sizes = [p.shape[0] for p in parts]
    packed = jnp.concatenate(parts, axis=0)
    gathered = allgather_devices_hbm(packed, "gather_small", seq_id=10)

    gwo1 = matmul(_after([act1], recv2_mx, packed)[0], dy1b, "tn", tm=Fd // 2, tn=D, tk=tk, name="ffn1_dwo")
    w_f1o, w_f1i = w_f1[1:], w_f1[:1]
    recv1_f1o = exchange_sibling_halves([gwo1], ["row"], shapes_of(w_f1o), "reduce1_ffn1_out", seq_id=7)
    gwi1 = matmul(h1b, dg1, "tn", tm=D, tn=Fd // 2, tk=tk, name="ffn1_dwg", out_cols=2 * Fd)
    h_f1o, hb_f1o = reduce_add([gwo1], _after(recv1_f1o, gwi1), w_f1o, "ffn1_out", ["row"])
    recv2_f1o = exchange_chip_pieces(hb_f1o, ["row"], shapes_of(w_f1o), "reduce2_ffn1_out", seq_id=11)
    gwi1 = matmul(h1b, _after([du1], hb_f1o)[0], "tn", tm=D, tn=Fd // 2, tk=tk, name="ffn1_dwu", out_cols=2 * Fd, col_off=Fd,
                  prev=gwi1)
    recv1_f1i = exchange_sibling_halves([gwi1], ["col"], shapes_of(w_f1i), "reduce1_ffn1_in", seq_id=12)
    g_mx = exchange_reduced_halves(reduce_sum(h_mx, recv2_mx, w_mx, "mix_"), "reduce3_mix")
    summed = sum_slots(gathered, "sum_small")
    offs = [0]
    for s in sizes:
        offs.append(offs[-1] + s)
    seg = lambda k: summed[offs[k]:offs[k + 1]]
    g_b_mod = seg(0).reshape(1, 9 * D)
    g_b_fmod = seg(1).reshape(1, 2 * D)
    gsum = seg(2).reshape(5, D)
    loss = (0.5 / D) * jnp.sum(gsum[4])
    msum = seg(3).reshape(SUBLANES, M)
    g_conv_w = lax.dynamic_index_in_dim(seg(4).reshape(nb, 32, LANES), chip, axis=0, keepdims=False)[:CONV_WIDTH]
    g_rnn_w = lax.dynamic_index_in_dim(seg(5).reshape(nb, SUBLANES, LANES), chip, axis=0, keepdims=False)[:RNN_CONV_WIDTH]
    g_w_a = seg(6).reshape(H, HD, HD)
    g_w_i = seg(7).reshape(H, HD, HD)
    dmod_all = gathered[:, offs[0]:offs[1]].reshape(N_DEV, 9 * D)
    dfmod_all = gathered[:, offs[1]:offs[2]].reshape(N_DEV, 2 * D)
    dmod_cols = lax.dynamic_slice_in_dim(dmod_all, chip * n_mod, n_mod, axis=1)
    dfmod_cols = lax.dynamic_slice_in_dim(dfmod_all, chip * n_fmod, n_fmod, axis=1)

    h_f1i, hb_f1i = reduce_add([gwi1], recv1_f1i, w_f1i, "ffn1_in", ["col"])
    recv2_f1i = exchange_chip_pieces(hb_f1i, ["col"], shapes_of(w_f1i), "reduce2_ffn1_in", seq_id=8)
    g_f1o = exchange_reduced_halves(reduce_sum(h_f1o, recv2_f1o, w_f1o, "ffn1_out", ["row"]), "reduce3_ffn1_out")
    g_f1i = exchange_reduced_halves(reduce_sum(h_f1i, recv2_f1i, w_f1i, "ffn1_in", ["col"]), "reduce3_ffn1_in")
    g_f1 = list(g_f1i) + list(g_f1o)

    big_w = w_f1 + w_mx + w_f2
    g_big = g_f1 + list(g_mx) + list(g_f2)
    names = ["ffn1_in", "ffn1_out", "w_in", "w_out", "ffn2_in", "ffn2_out"]
    big_m = [m_w_ffn1_in[0], m_w_ffn1_out[0], m_w_in[0], m_w_out[0]]
    big_v = [v_w_ffn1_in[0], v_w_ffn1_out[0], v_w_in[0], v_w_out[0]]
    big_out = [adam_big(w, g, m, v, "adam_" + nm) for w, g, m, v, nm in zip(big_w, g_big, big_m, big_v, names)] + adam_f2
    g_w_mod, d_w_mod, nm_w_mod, nv_w_mod = adam_cond(c_all, dmod_cols, w_mod[0], m_w_mod[0], v_w_mod[0], "adam_w_mod")
    g_w_fmod, d_w_fmod, nm_w_fmod, nv_w_fmod = adam_cond(c_all, dfmod_cols, w_fmod, m_w_fmod, v_w_fmod, "adam_w_fmod")

    flat2 = lambda a: a.reshape(-1, a.shape[-1])
    small_names = ["b_mod", "g_ffn1", "g_mix", "conv_w", "conv_b", "ln_g", "ln_b", "rnn_conv_w", "rnn_conv_b", "w_a", "b_a",
                   "w_i", "b_i", "lru_lambda", "g_ffn2", "b_fmod", "g_final"]
    small_w = [b_mod, g_ffn1, g_mix, conv_w, conv_b, ln_g, ln_b, rnn_conv_w, rnn_conv_b, w_a, b_a, w_i, b_i, lru_lambda,
               g_ffn2, b_fmod, g_final]
    small_m = [m_b_mod, m_g_ffn1, m_g_mix, m_conv_w, m_conv_b, m_ln_g, m_ln_b, m_rnn_conv_w, m_rnn_conv_b, m_w_a, m_b_a,
               m_w_i, m_b_i, m_lru_lambda, m_g_ffn2, m_b_fmod, m_g_final]
    small_v = [v_b_mod, v_g_ffn1, v_g_mix, v_conv_w, v_conv_b, v_ln_g, v_ln_b, v_rnn_conv_w, v_rnn_conv_b, v_w_a, v_b_a,
               v_w_i, v_b_i, v_lru_lambda, v_g_ffn2, v_b_fmod, v_g_final]
    small_g = [g_b_mod, gsum[0:1], gsum[1:2], g_conv_w, msum[0:1], msum[1:2], msum[2:3], g_rnn_w, msum[3:4], g_w_a, msum[4:5],
               g_w_i, msum[5:6], msum[6:7], gsum[2:3], g_b_fmod, gsum[3:4]]
    small_g = [g.reshape(w.shape) for g, w in zip(small_g, small_w)]
    two_d = lambda a: a.reshape(1, -1) if a.ndim == 1 else flat2(a)
    sd, sm, sv = adam_small([two_d(a) for a in small_w], [two_d(a) for a in small_g], [two_d(a) for a in small_m],
                            [two_d(a) for a in small_v], "adam_small")
    small = {}
    for k, nm in enumerate(small_names):
        shp = small_w[k].shape
        small[nm] = (small_g[k], sd[k].reshape(shp), sm[k].reshape(shp), sv[k].reshape(shp))

    big = {"w_mod": tuple(a[None] for a in (g_w_mod, d_w_mod, nm_w_mod, nv_w_mod)),
           "w_fmod": (g_w_fmod, d_w_fmod, nm_w_fmod, nv_w_fmod)}
    for nm, full, g, (d, nmm, nvv) in zip(["w_ffn1_in", "w_ffn1_out", "w_in", "w_out", "w_ffn2_in", "w_ffn2_out"],
                                         big_w, g_big, big_out):
        big[nm] = tuple(a[None] for a in (g, d, nmm, nvv))
    order = ["w_mod", "b_mod", "g_ffn1", "w_ffn1_in", "w_ffn1_out", "g_mix", "w_in", "conv_w", "conv_b", "ln_g", "ln_b",
             "rnn_conv_w", "rnn_conv_b", "w_a", "b_a", "w_i", "b_i", "lru_lambda", "w_out", "g_ffn2", "w_ffn2_in",
             "w_ffn2_out", "w_fmod", "b_fmod", "g_final"]
    table = {**small, **big}
    outs = [loss, dx0[None]]
    for kind_ in range(4):
        outs.extend(table[nm][kind_] for nm in order)
    return tuple(outs)
```

```python
import functools

import jax
import jax.numpy as jnp
from jax import lax
from jax.experimental import pallas as pl
from jax.experimental.pallas import tpu as pltpu
from jax.experimental.pallas import tpu_sc as plsc

F32 = jnp.float32
BF = jnp.bfloat16
I32 = jnp.int32
MESH = pl.DeviceIdType.MESH

EPS = 1e-6
RG_C = 8.0
MACARON_W = 0.5
CONV_WIDTH = 31
RNN_CONV_WIDTH = 4
ADAM_LR = 0.001
ADAM_B1 = 0.9
ADAM_B2 = 0.999
ADAM_EPS = 1e-08
ADAM_WD = 0.01
ADAM_STEP = 10

LANES = 128
SUBLANES = 8
VMEM_LIMIT = 48 * 1024 * 1024
N_CHIPS = 4
N_DEV = 8

R_SH1, R_SC1, R_GT1, R_SH2, R_SC2, R_GT2, R_SH3, R_SC3, R_GT3, R_FSH, R_FSC, R_G1, R_G2, R_G3, R_GF = range(15)

CONTRACT_LAST = (((1,), (1,)), ((), ()))
CONTRACT_FIRST = (((0,), (0,)), ((), ()))


def _pcall(body, **kw):
    return pl.pallas_call(body, **kw)


def _params(sem=None, vmem=VMEM_LIMIT):
    if sem is None:
        return pltpu.CompilerParams(vmem_limit_bytes=vmem)
    return pltpu.CompilerParams(dimension_semantics=sem, vmem_limit_bytes=vmem)


def _row(ref, r):
    return ref[r:r + 1, :]


def _sigmoid(x):
    return 1.0 / (1.0 + jnp.exp(-x))


def _colsum(x):
    return jnp.sum(x, axis=0, keepdims=True)


def _rowmean(x):
    return jnp.mean(x, axis=-1, keepdims=True)


def matmul(a, b, mode, *, tm, tn, tk, name, out_dtype=F32, out_cols=None, col_off=0, prev=None):
    if mode == "nn":
        (M, K), (K2, N) = a.shape, b.shape
    elif mode == "nt":
        (M, K), (N, K2) = a.shape, b.shape
    else:
        (K, M), (K2, N) = a.shape, b.shape
    assert K == K2 and M % tm == 0 and N % tn == 0 and K % tk == 0 and col_off % tn == 0
    nk = K // tk
    out_cols = N if out_cols is None else out_cols
    off = col_off // tn

    def body(*refs):
        if prev is None:
            a_ref, b_ref, o_ref, acc = refs
        else:
            a_ref, b_ref, _, o_ref, acc = refs
        k = pl.program_id(2)

        @pl.when(k == 0)
        def _():
            acc[...] = jnp.zeros_like(acc)

        av = a_ref[...].astype(BF)
        bv = b_ref[...].astype(BF)
        if mode == "nn":
            acc[...] += jnp.dot(av, bv, preferred_element_type=F32)
        elif mode == "nt":
            acc[...] += lax.dot_general(av, bv, CONTRACT_LAST, preferred_element_type=F32)
        else:
            acc[...] += lax.dot_general(av, bv, CONTRACT_FIRST, preferred_element_type=F32)

        @pl.when(k == nk - 1)
        def _():
            o_ref[...] = acc[...].astype(out_dtype)

    if mode == "nn":
        a_spec = pl.BlockSpec((tm, tk), lambda m, n, k: (m, k))
        b_spec = pl.BlockSpec((tk, tn), lambda m, n, k: (k, n))
    elif mode == "nt":
        a_spec = pl.BlockSpec((tm, tk), lambda m, n, k: (m, k))
        b_spec = pl.BlockSpec((tn, tk), lambda m, n, k: (n, k))
    else:
        a_spec = pl.BlockSpec((tk, tm), lambda m, n, k: (k, m))
        b_spec = pl.BlockSpec((tk, tn), lambda m, n, k: (k, n))
    in_specs = [a_spec, b_spec]
    args = [a, b]
    aliases = {}
    if prev is not None:
        in_specs.append(pl.BlockSpec(memory_space=pl.ANY))
        args.append(prev)
        aliases = {2: 0}
    return _pcall(
        body, name=name, grid=(M // tm, N // tn, nk), in_specs=in_specs,
        out_specs=pl.BlockSpec((tm, tn), lambda m, n, k: (m, n + off)),
        out_shape=jax.ShapeDtypeStruct((M, out_cols), out_dtype),
        scratch_shapes=[pltpu.VMEM((tm, tn), F32)], input_output_aliases=aliases,
        compiler_params=_params(("parallel", "parallel", "arbitrary")),
    )(*args)


def cond_matmul(c_all, w, name):
    B, K = c_all.shape
    N = w.shape[1]
    tn = 256
    assert N % tn == 0

    def body(c_ref, w_ref, o_ref):
        cv = c_ref[...]
        ca = cv * _sigmoid(cv)
        o_ref[...] = jnp.dot(ca, w_ref[...], preferred_element_type=F32, precision=lax.Precision.HIGHEST)

    return _pcall(
        body, name=name, grid=(N // tn,),
        in_specs=[pl.BlockSpec((B, K), lambda n: (0, 0)), pl.BlockSpec((K, tn), lambda n: (0, n))],
        out_specs=pl.BlockSpec((B, tn), lambda n: (0, n)),
        out_shape=jax.ShapeDtypeStruct((B, N), F32), compiler_params=_params(("parallel",)),
    )(c_all, w)


FFN_CH = 1408
FFN_FWD_TS = 512
FFN_BWD_TS = 256
FFN_FWD_VMEM = 56 * 1024 * 1024


def ffn_fwd(x, vecs, wi, wo, rows, name):
    r_sh, r_sc, r_gt, r_g = rows
    S, D = x.shape
    Fd = wo.shape[0]
    ts, ch = min(FFN_FWD_TS, S), FFN_CH
    ni, nj = S // ts, Fd // ch
    assert nj >= 2

    def body(x_ref, v_ref, wg_ref, wu_ref, wo_ref, xo_ref, g_ref, u_ref, y_ref, h_sc, acc):
        j = pl.program_id(1)

        @pl.when(j == 0)
        def _():
            xv = x_ref[...]
            r = lax.rsqrt(_rowmean(xv * xv) + EPS)
            gs = _row(v_ref, r_g) * (1.0 + _row(v_ref, r_sc))
            h_sc[...] = (xv * r * gs + _row(v_ref, r_sh)).astype(BF)

        hb = h_sc[...]
        G = jnp.dot(hb, wg_ref[...], preferred_element_type=F32)
        U = jnp.dot(hb, wu_ref[...], preferred_element_type=F32)
        g_ref[...] = G.astype(BF)
        u_ref[...] = U.astype(BF)
        act = (G * _sigmoid(G) * U).astype(BF)
        part = jnp.dot(act, wo_ref[...], preferred_element_type=F32)

        @pl.when(j == 0)
        def _():
            acc[...] = part

        @pl.when((j > 0) & (j < nj - 1))
        def _():
            acc[...] += part

        @pl.when(j == nj - 1)
        def _():
            Y = acc[...] + part
            y_ref[...] = Y
            xo_ref[...] = x_ref[...] + (MACARON_W * _row(v_ref, r_gt)) * Y

    tok = pl.BlockSpec((ts, D), lambda i, j: (i, 0))
    hid = pl.BlockSpec((ts, ch), lambda i, j: (i, j))
    return _pcall(
        body, name=name, grid=(ni, nj),
        in_specs=[tok, pl.BlockSpec(vecs.shape, lambda i, j: (0, 0)),
                  pl.BlockSpec((D, ch), lambda i, j: (0, j)), pl.BlockSpec((D, ch), lambda i, j: (0, j + nj)),
                  pl.BlockSpec((ch, D), lambda i, j: (j, 0))],
        out_specs=[tok, hid, hid, tok],
        out_shape=[jax.ShapeDtypeStruct((S, D), F32), jax.ShapeDtypeStruct((S, Fd), BF),
                   jax.ShapeDtypeStruct((S, Fd), BF), jax.ShapeDtypeStruct((S, D), F32)],
        scratch_shapes=[pltpu.VMEM((ts, D), BF), pltpu.VMEM((ts, D), F32)],
        compiler_params=_params(("arbitrary", "arbitrary"), FFN_FWD_VMEM),
    )(x, vecs, wi, wi, wo)


def ffn_bwd(dxo, x, vecs, gs_, us_, y, wi, wo, rows, name):
    r_sh, r_sc, r_gt, r_g = rows
    S, D = x.shape
    Fd = wo.shape[0]
    ts, ch = min(FFN_BWD_TS, S), FFN_CH
    ni, nj = S // ts, Fd // ch

    def body(dxo_ref, x_ref, v_ref, g_ref, u_ref, y_ref, wg_ref, wu_ref, wo_ref,
             dx_ref, act_ref, dg_ref, du_ref, hb_ref, dyb_ref, vg_ref, dyb_sc, dh_sc):
        i, j = pl.program_id(0), pl.program_id(1)

        @pl.when((i == 0) & (j == 0))
        def _():
            vg_ref[...] = jnp.zeros_like(vg_ref)

        @pl.when(j == 0)
        def _():
            dxo_v = dxo_ref[...]
            dyb = ((MACARON_W * _row(v_ref, r_gt)) * dxo_v).astype(BF)
            dyb_sc[...] = dyb
            dyb_ref[...] = dyb
            vg_ref[0:1, :] += MACARON_W * _colsum(dxo_v * y_ref[...])

        dA = lax.dot_general(dyb_sc[...], wo_ref[...], CONTRACT_LAST, preferred_element_type=F32)
        G = g_ref[...].astype(F32)
        U = u_ref[...].astype(F32)
        sg = _sigmoid(G)
        sl = G * sg
        dU = (dA * sl).astype(BF)
        dG = (dA * U * (sg * (1.0 + G * (1.0 - sg)))).astype(BF)
        act_ref[...] = (sl * U).astype(BF)
        dg_ref[...] = dG
        du_ref[...] = dU
        part = (lax.dot_general(dG, wg_ref[...], CONTRACT_LAST, preferred_element_type=F32)
                + lax.dot_general(dU, wu_ref[...], CONTRACT_LAST, preferred_element_type=F32))

        @pl.when(j == 0)
        def _():
            dh_sc[...] = part

        @pl.when((j > 0) & (j < nj - 1))
        def _():
            dh_sc[...] += part

        @pl.when(j == nj - 1)
        def _():
            dh = dh_sc[...] + part
            xv = x_ref[...]
            r = lax.rsqrt(_rowmean(xv * xv) + EPS)
            n = xv * r
            g = _row(v_ref, r_g)
            sc1 = 1.0 + _row(v_ref, r_sc)
            gsc = g * sc1
            hb_ref[...] = (n * gsc + _row(v_ref, r_sh)).astype(BF)
            dhn = dh * n
            vg_ref[1:2, :] += _colsum(dh)
            vg_ref[2:3, :] += _colsum(dhn) * g
            vg_ref[3:4, :] += _colsum(dhn) * sc1
            dn = dh * gsc
            dx_ref[...] = dxo_ref[...] + r * (dn - n * _rowmean(dn * n))

    tok = pl.BlockSpec((ts, D), lambda i, j: (i, 0))
    hid = pl.BlockSpec((ts, ch), lambda i, j: (i, j))
    return _pcall(
        body, name=name, grid=(ni, nj),
        in_specs=[tok, tok, pl.BlockSpec(vecs.shape, lambda i, j: (0, 0)), hid, hid, tok,
                  pl.BlockSpec((D, ch), lambda i, j: (0, j)), pl.BlockSpec((D, ch), lambda i, j: (0, j + nj)),
                  pl.BlockSpec((ch, D), lambda i, j: (j, 0))],
        out_specs=[tok, hid, hid, hid, tok, tok, pl.BlockSpec((SUBLANES, D), lambda i, j: (0, 0))],
        out_shape=[jax.ShapeDtypeStruct((S, D), F32), jax.ShapeDtypeStruct((S, Fd), BF),
                   jax.ShapeDtypeStruct((S, Fd), BF), jax.ShapeDtypeStruct((S, Fd), BF),
                   jax.ShapeDtypeStruct((S, D), BF), jax.ShapeDtypeStruct((S, D), BF),
                   jax.ShapeDtypeStruct((SUBLANES, D), F32)],
        scratch_shapes=[pltpu.VMEM((ts, D), BF), pltpu.VMEM((ts, D), F32)],
        compiler_params=_params(("arbitrary", "arbitrary")),
    )(dxo, x, vecs, gs_, us_, y, wi, wi, wo)


def final_fwd_bwd(x, tgt, vecs, name):
    S, D = x.shape
    ts = min(512, S)

    def body(x_ref, t_ref, v_ref, dx_ref, vg_ref):
        @pl.when(pl.program_id(0) == 0)
        def _():
            vg_ref[...] = jnp.zeros_like(vg_ref)

        xv = x_ref[...]
        r = lax.rsqrt(_rowmean(xv * xv) + EPS)
        n = xv * r
        g = _row(v_ref, R_GF)
        sc1 = 1.0 + _row(v_ref, R_FSC)
        gsc = g * sc1
        e = n * gsc + _row(v_ref, R_FSH) - t_ref[...]
        vg_ref[3:4, :] += _colsum(e * e)
        dout = e * (1.0 / D)
        dn_ = dout * n
        vg_ref[0:1, :] += _colsum(dout)
        vg_ref[1:2, :] += _colsum(dn_) * g
        vg_ref[2:3, :] += _colsum(dn_) * sc1
        dn = dout * gsc
        dx_ref[...] = r * (dn - n * _rowmean(dn * n))

    tok = pl.BlockSpec((ts, D), lambda i: (i, 0))
    return _pcall(
        body, name=name, grid=(S // ts,),
        in_specs=[tok, tok, pl.BlockSpec(vecs.shape, lambda i: (0, 0))],
        out_specs=[tok, pl.BlockSpec((SUBLANES, D), lambda i: (0, 0))],
        out_shape=[jax.ShapeDtypeStruct((S, D), F32), jax.ShapeDtypeStruct((SUBLANES, D), F32)],
        compiler_params=_params(("arbitrary",)),
    )(x, tgt, vecs)


def norm_matmul(x, vecs, w, rows, name):
    r_sh, r_sc, r_g = rows
    S, D = x.shape
    N = w.shape[1]
    ts, tn = min(512, S), 512
    nn = N // tn

    def body(x_ref, v_ref, w_ref, o_ref, h_sc):
        @pl.when(pl.program_id(1) == 0)
        def _():
            xv = x_ref[...]
            r = lax.rsqrt(_rowmean(xv * xv) + EPS)
            gs = _row(v_ref, r_g) * (1.0 + _row(v_ref, r_sc))
            h_sc[...] = (xv * r * gs + _row(v_ref, r_sh)).astype(BF)

        o_ref[...] = jnp.dot(h_sc[...], w_ref[...], preferred_element_type=F32)

    return _pcall(
        body, name=name, grid=(S // ts, nn),
        in_specs=[pl.BlockSpec((ts, D), lambda i, n: (i, 0)), pl.BlockSpec(vecs.shape, lambda i, n: (0, 0)),
                  pl.BlockSpec((D, tn), lambda i, n: (0, n))],
        out_specs=pl.BlockSpec((ts, tn), lambda i, n: (i, n)),
        out_shape=jax.ShapeDtypeStruct((S, N), F32),
        scratch_shapes=[pltpu.VMEM((ts, D), BF)],
        compiler_params=_params(("arbitrary", "arbitrary")),
    )(x, vecs, w)


SEQ_TT = 256
CONV_PAD = 32


def conv_fwd(proj, convw4, conv_b, name):
    S = proj.shape[0]
    M = conv_b.shape[1]
    nb = M // LANES
    tt = min(SEQ_TT, S)

    def body(uv_ref, ug_ref, w_ref, b_ref, cq_ref, qp):
        qp[0:CONV_PAD, :] = jnp.zeros((CONV_PAD, LANES), F32)

        def step(t, carry):
            base = pl.multiple_of(t * tt, tt)
            qp[pl.ds(base + CONV_PAD, tt), :] = uv_ref[pl.ds(base, tt), :] * _sigmoid(ug_ref[pl.ds(base, tt), :])
            acc = jnp.broadcast_to(b_ref[...], (tt, LANES))
            for k in range(CONV_WIDTH):
                acc = acc + w_ref[k:k + 1, :] * qp[pl.ds(base + (CONV_PAD - CONV_WIDTH + 1) + k, tt), :]
            cq_ref[pl.ds(base, tt), :] = acc
            return carry

        lax.fori_loop(0, S // tt, step, 0)

    return _pcall(
        body, name=name, grid=(nb,),
        in_specs=[pl.BlockSpec((S, LANES), lambda c: (0, c)), pl.BlockSpec((S, LANES), lambda c: (0, c + nb)),
                  pl.BlockSpec((None, 32, LANES), lambda c: (c, 0, 0)), pl.BlockSpec((1, LANES), lambda c: (0, c))],
        out_specs=pl.BlockSpec((S, LANES), lambda c: (0, c)),
        out_shape=jax.ShapeDtypeStruct((S, M), F32),
        scratch_shapes=[pltpu.VMEM((S + CONV_PAD, LANES), F32)],
        compiler_params=_params(("arbitrary",)),
    )(proj, proj, convw4, conv_b)


def conv_bwd(dcq, proj, convw4, name):
    S, M = dcq.shape
    nb = M // LANES
    tt = min(SEQ_TT, S)
    off = CONV_PAD - CONV_WIDTH + 1

    def body(dcq_ref, uv_ref, ug_ref, w_ref, duv_ref, dug_ref, dw_ref, db_ref, qp, dp, dw8, db8):
        qp[0:CONV_PAD, :] = jnp.zeros((CONV_PAD, LANES), F32)
        dp[S:S + CONV_PAD, :] = jnp.zeros((CONV_PAD, LANES), F32)
        dw8[...] = jnp.zeros_like(dw8)
        db8[...] = jnp.zeros_like(db8)

        def fill(t, carry):
            base = pl.multiple_of(t * tt, tt)
            qp[pl.ds(base + CONV_PAD, tt), :] = uv_ref[pl.ds(base, tt), :] * _sigmoid(ug_ref[pl.ds(base, tt), :])
            dp[pl.ds(base, tt), :] = dcq_ref[pl.ds(base, tt), :]
            return carry

        lax.fori_loop(0, S // tt, fill, 0)

        def step(t, carry):
            base = pl.multiple_of(t * tt, tt)
            d_t = dcq_ref[pl.ds(base, tt), :]
            db8[...] += d_t.reshape(tt // SUBLANES, SUBLANES, LANES).sum(axis=0)
            dq = jnp.zeros((tt, LANES), F32)
            for k in range(CONV_WIDTH):
                prod = d_t * qp[pl.ds(base + off + k, tt), :]
                dw8[k] += prod.reshape(tt // SUBLANES, SUBLANES, LANES).sum(axis=0)
                dq = dq + w_ref[k:k + 1, :] * dp[pl.ds(base + (CONV_WIDTH - 1) - k, tt), :]
            uv = uv_ref[pl.ds(base, tt), :]
            sg = _sigmoid(ug_ref[pl.ds(base, tt), :])
            duv_ref[pl.ds(base, tt), :] = dq * sg
            dug_ref[pl.ds(base, tt), :] = dq * uv * sg * (1.0 - sg)
            return carry

        lax.fori_loop(0, S // tt, step, 0)
        dw_ref[...] = jnp.zeros_like(dw_ref)
        for k in range(CONV_WIDTH):
            dw_ref[k:k + 1, :] = _colsum(dw8[k])
        db_ref[...] = _colsum(db8[...])

    col = lambda o: pl.BlockSpec((S, LANES), lambda c: (0, c + o))
    return _pcall(
        body, name=name, grid=(nb,),
        in_specs=[col(0), col(0), col(nb), pl.BlockSpec((None, 32, LANES), lambda c: (c, 0, 0))],
        out_specs=[col(0), col(0), pl.BlockSpec((None, 32, LANES), lambda c: (c, 0, 0)),
                   pl.BlockSpec((1, LANES), lambda c: (0, c))],
        out_shape=[jax.ShapeDtypeStruct((S, M), F32), jax.ShapeDtypeStruct((S, M), F32),
                   jax.ShapeDtypeStruct((nb, 32, LANES), F32), jax.ShapeDtypeStruct((1, M), F32)],
        scratch_shapes=[pltpu.VMEM((S + CONV_PAD, LANES), F32), pltpu.VMEM((S + CONV_PAD, LANES), F32),
                        pltpu.VMEM((32, SUBLANES, LANES), F32), pltpu.VMEM((SUBLANES, LANES), F32)],
        compiler_params=_params(("arbitrary",)),
    )(dcq, proj, proj, convw4)


def _log_sigmoid(x):
    return jnp.minimum(x, 0.0) - jnp.log(1.0 + jnp.exp(-jnp.abs(x)))


def _rg_gate_terms(ra, ls):
    la = RG_C * ra * ls
    a = jnp.exp(la)
    th = jnp.tanh(la)
    mult = jnp.sqrt(-2.0 * th / (1.0 - th))
    return a, mult


def rnn_fwd(proj, rnnw4, rnn_b, bda, bdi, b_a, b_i, lam, name):
    S = proj.shape[0]
    M = rnn_b.shape[1]
    nb = M // LANES
    tt = min(SEQ_TT, S)
    KW = RNN_CONV_WIDTH

    def body(ux_ref, w_ref, rb_ref, bda_ref, bdi_ref, ba_ref, bi_ref, lam_ref,
             xr_ref, ra_ref, ii_ref, h_ref, uxp, a_sc, b_sc):
        uxp[0:SUBLANES, :] = jnp.zeros((SUBLANES, LANES), F32)
        ls = _log_sigmoid(lam_ref[...])

        def step(t, carry):
            base = pl.multiple_of(t * tt, tt)
            uxp[pl.ds(base + SUBLANES, tt), :] = ux_ref[pl.ds(base, tt), :]
            xr = jnp.broadcast_to(rb_ref[...], (tt, LANES))
            for k in range(KW):
                xr = xr + w_ref[k:k + 1, :] * uxp[pl.ds(base + (SUBLANES - KW + 1) + k, tt), :]
            xb = xr.astype(BF)
            ra = _sigmoid(jnp.dot(xb, bda_ref[...], preferred_element_type=F32) + ba_ref[...])
            ii = _sigmoid(jnp.dot(xb, bdi_ref[...], preferred_element_type=F32) + bi_ref[...])
            a, mult = _rg_gate_terms(ra, ls)
            xr_ref[pl.ds(base, tt), :] = xr
            ra_ref[pl.ds(base, tt), :] = ra
            ii_ref[pl.ds(base, tt), :] = ii
            a_sc[pl.ds(base, tt), :] = a
            b_sc[pl.ds(base, tt), :] = mult * (ii * xr)
            return carry

        lax.fori_loop(0, S // tt, step, 0)

        rows = lax.broadcasted_iota(I32, (SUBLANES, LANES), 0)

        def scan(t, hprev):
            base = pl.multiple_of(t * SUBLANES, SUBLANES)
            A = a_sc[pl.ds(base, SUBLANES), :]
            B = b_sc[pl.ds(base, SUBLANES), :]
            for d in (1, 2, 4):
                As = jnp.where(rows >= d, pltpu.roll(A, d, axis=0), 1.0)
                Bs = jnp.where(rows >= d, pltpu.roll(B, d, axis=0), 0.0)
                B = A * Bs + B
                A = A * As
            hh = B + A * hprev
            h_ref[pl.ds(base, SUBLANES), :] = hh
            return jnp.broadcast_to(hh[SUBLANES - 1:SUBLANES, :], (SUBLANES, LANES))

        lax.fori_loop(0, S // SUBLANES, scan, jnp.zeros((SUBLANES, LANES), F32))

    col = lambda o: pl.BlockSpec((S, LANES), lambda c: (0, c + o))
    vec = pl.BlockSpec((1, LANES), lambda c: (0, c))
    diag = pl.BlockSpec((LANES, LANES), lambda c: (c, c))
    return _pcall(
        body, name=name, grid=(nb,),
        in_specs=[col(2 * nb), pl.BlockSpec((None, SUBLANES, LANES), lambda c: (c, 0, 0)), vec, diag, diag, vec, vec, vec],
        out_specs=[col(0)] * 4,
        out_shape=[jax.ShapeDtypeStruct((S, M), F32)] * 4,
        scratch_shapes=[pltpu.VMEM((S + SUBLANES, LANES), F32), pltpu.VMEM((S, LANES), F32), pltpu.VMEM((S, LANES), F32)],
        compiler_params=_params(("arbitrary",)),
    )(proj, rnnw4, rnn_b, bda, bdi, b_a, b_i, lam)


def rnn_bwd(dhout, h, xr, ra, ii, proj, rnnw4, bda, bdi, lam, name):
    S, M = h.shape
    nb = M // LANES
    tt = min(SEQ_TT, S)
    KW = RNN_CONV_WIDTH
    SL = SUBLANES

    def body(dh_ref, h_ref, xr_ref, ra_ref, ii_ref, ux_ref, w_ref, bda_ref, bdi_ref, lam_ref,
             dux_ref, dwa_ref, dwi_ref, drw_ref, vec_ref,
             a_sc, hp, g_sc, dpa_sc, dpi_sc, dxp, uxp, acc8, drw8):
        zero8 = jnp.zeros((SL, LANES), F32)
        a_sc[S:S + SL, :] = zero8
        hp[0:SL, :] = zero8
        dxp[S:S + SL, :] = zero8
        uxp[0:SL, :] = zero8
        acc8[...] = jnp.zeros_like(acc8)
        drw8[...] = jnp.zeros_like(drw8)
        lamv = lam_ref[...]
        ls = _log_sigmoid(lamv)

        def fill(t, carry):
            base = pl.multiple_of(t * tt, tt)
            a_sc[pl.ds(base, tt), :] = jnp.exp(RG_C * ra_ref[pl.ds(base, tt), :] * ls)
            hp[pl.ds(base + SL, tt), :] = h_ref[pl.ds(base, tt), :]
            uxp[pl.ds(base + SL, tt), :] = ux_ref[pl.ds(base, tt), :]
            return carry

        lax.fori_loop(0, S // tt, fill, 0)

        rows = lax.broadcasted_iota(I32, (SL, LANES), 0)
        nt8 = S // SL

        def rscan(t, gnext):
            base = pl.multiple_of((nt8 - 1 - t) * SL, SL)
            A = a_sc[pl.ds(base + 1, SL), :]
            B = dh_ref[pl.ds(base, SL), :]
            for d in (1, 2, 4):
                As = jnp.where(rows < SL - d, pltpu.roll(A, SL - d, axis=0), 1.0)
                Bs = jnp.where(rows < SL - d, pltpu.roll(B, SL - d, axis=0), 0.0)
                B = A * Bs + B
                A = A * As
            g = B + A * gnext
            g_sc[pl.ds(base, SL), :] = g
            return jnp.broadcast_to(g[0:1, :], (SL, LANES))

        lax.fori_loop(0, nt8, rscan, zero8)

        def red8(v):
            return v.reshape(tt // SL, SL, LANES).sum(axis=0)

        def step(t, carry):
            base = pl.multiple_of(t * tt, tt)
            g = g_sc[pl.ds(base, tt), :]
            hprev = hp[pl.ds(base + SL - 1, tt), :]
            xr_t = xr_ref[pl.ds(base, tt), :]
            ra_t = ra_ref[pl.ds(base, tt), :]
            ii_t = ii_ref[pl.ds(base, tt), :]
            a, mult = _rg_gate_terms(ra_t, ls)
            gx = g * xr_t
            dmult = gx * ii_t
            dii = gx * mult
            dxr = g * (mult * ii_t)
            dla = g * hprev * a - dmult * (a * a) / mult
            acc8[3] += red8(dla * ra_t)
            dpa = dla * (RG_C * ls) * ra_t * (1.0 - ra_t)
            dpi = dii * ii_t * (1.0 - ii_t)
            dpab = dpa.astype(BF)
            dpib = dpi.astype(BF)
            dxr = dxr + (lax.dot_general(dpab, bda_ref[...], CONTRACT_LAST, preferred_element_type=F32)
                         + lax.dot_general(dpib, bdi_ref[...], CONTRACT_LAST, preferred_element_type=F32))
            dpa_sc[pl.ds(base, tt), :] = dpab
            dpi_sc[pl.ds(base, tt), :] = dpib
            dxp[pl.ds(base, tt), :] = dxr
            acc8[0] += red8(dxr)
            acc8[1] += red8(dpa)
            acc8[2] += red8(dpi)
            return carry

        lax.fori_loop(0, S // tt, step, 0)

        def convb(t, carry):
            base = pl.multiple_of(t * tt, tt)
            d_t = dxp[pl.ds(base, tt), :]
            dux = jnp.zeros((tt, LANES), F32)
            for k in range(KW):
                drw8[k] += red8(d_t * uxp[pl.ds(base + (SL - KW + 1) + k, tt), :])
                dux = dux + w_ref[k:k + 1, :] * dxp[pl.ds(base + (KW - 1) - k, tt), :]
            dux_ref[pl.ds(base, tt), :] = dux
            return carry

        lax.fori_loop(0, S // tt, convb, 0)

        xb = xr_ref[...].astype(BF)
        dwa_ref[...] = lax.dot_general(xb, dpa_sc[...], CONTRACT_FIRST, preferred_element_type=F32)
        dwi_ref[...] = lax.dot_general(xb, dpi_sc[...], CONTRACT_FIRST, preferred_element_type=F32)
        drw_ref[...] = jnp.zeros_like(drw_ref)
        vec_ref[...] = jnp.zeros_like(vec_ref)
        for k in range(KW):
            drw_ref[k:k + 1, :] = _colsum(drw8[k])
        for k in range(3):
            vec_ref[k:k + 1, :] = _colsum(acc8[k])
        vec_ref[3:4, :] = _colsum(acc8[3]) * (RG_C * _sigmoid(-lamv))

    col = lambda o: pl.BlockSpec((S, LANES), lambda c: (0, c + o))
    vec = pl.BlockSpec((1, LANES), lambda c: (0, c))
    diag = pl.BlockSpec((LANES, LANES), lambda c: (c, c))
    blk3 = lambda r: pl.BlockSpec((None, r, LANES), lambda c: (c, 0, 0))
    return _pcall(
        body, name=name, grid=(nb,),
        in_specs=[col(0), col(0), col(0), col(0), col(0), col(2 * nb), blk3(SL), diag, diag, vec],
        out_specs=[col(0), blk3(LANES), blk3(LANES), blk3(SL), pl.BlockSpec((SL, LANES), lambda c: (0, c))],
        out_shape=[jax.ShapeDtypeStruct((S, M), F32), jax.ShapeDtypeStruct((nb, LANES, LANES), F32),
                   jax.ShapeDtypeStruct((nb, LANES, LANES), F32), jax.ShapeDtypeStruct((nb, SL, LANES), F32),
                   jax.ShapeDtypeStruct((SL, M), F32)],
        scratch_shapes=[pltpu.VMEM((S + SL, LANES), F32), pltpu.VMEM((S + SL, LANES), F32), pltpu.VMEM((S, LANES), F32),
                        pltpu.VMEM((S, LANES), BF), pltpu.VMEM((S, LANES), BF), pltpu.VMEM((S + SL, LANES), F32),
                        pltpu.VMEM((S + SL, LANES), F32), pltpu.VMEM((SL, SL, LANES), F32), pltpu.VMEM((SL, SL, LANES), F32)],
        compiler_params=_params(("arbitrary",)),
    )(dhout, h, xr, ra, ii, proj, rnnw4, bda, bdi, lam)


GELU_K = 0.7978845608028654
GELU_C = 0.044715


def _layernorm_parts(cq):
    mu = _rowmean(cq)
    d = cq - mu
    rstd = lax.rsqrt(_rowmean(d * d) + EPS)
    return d * rstd, rstd


def mix_out(cq, proj, h, x, vecs, lnv, wout, name):
    S, D = x.shape
    M = cq.shape[1]
    ts = min(512, S)

    def body(cq_ref, uy_ref, h_ref, x_ref, v_ref, ln_ref, w_ref, xo_ref, ym_ref, yc_ref):
        z, _ = _layernorm_parts(cq_ref[...])
        l = z * _row(ln_ref, 0) + _row(ln_ref, 1)
        yc_ref[:, 0:M] = (l * _sigmoid(l)).astype(BF)
        uy = uy_ref[...]
        gelu = 0.5 * uy * (1.0 + jnp.tanh(GELU_K * (uy + GELU_C * uy * uy * uy)))
        yc_ref[:, M:2 * M] = (gelu * h_ref[...]).astype(BF)
        ym = jnp.dot(yc_ref[...], w_ref[...], preferred_element_type=F32)
        ym_ref[...] = ym
        xo_ref[...] = x_ref[...] + _row(v_ref, R_GT2) * ym

    tok = pl.BlockSpec((ts, D), lambda i: (i, 0))
    mtok = lambda o: pl.BlockSpec((ts, M), lambda i: (i, o))
    return _pcall(
        body, name=name, grid=(S // ts,),
        in_specs=[mtok(0), mtok(3), mtok(0), tok, pl.BlockSpec(vecs.shape, lambda i: (0, 0)),
                  pl.BlockSpec(lnv.shape, lambda i: (0, 0)), pl.BlockSpec(wout.shape, lambda i: (0, 0))],
        out_specs=[tok, tok, pl.BlockSpec((ts, 2 * M), lambda i: (i, 0))],
        out_shape=[jax.ShapeDtypeStruct((S, D), F32), jax.ShapeDtypeStruct((S, D), F32),
                   jax.ShapeDtypeStruct((S, 2 * M), BF)],
        compiler_params=_params(("arbitrary",)),
    )(cq, proj, h, x, vecs, lnv, wout)


def mix_out_bwd(dxo, ym, vecs, wout, cq, lnv, proj, h, name):
    S, D = dxo.shape
    M = cq.shape[1]
    ts = min(512, S)

    def body(dxo_ref, ym_ref, v_ref, w_ref, cq_ref, ln_ref, uy_ref, h_ref,
             dcq_ref, dh_ref, duy_ref, dyb_ref, vgd_ref, vgm_ref):
        @pl.when(pl.program_id(0) == 0)
        def _():
            vgd_ref[...] = jnp.zeros_like(vgd_ref)
            vgm_ref[...] = jnp.zeros_like(vgm_ref)

        dxo_v = dxo_ref[...]
        dyb = (_row(v_ref, R_GT2) * dxo_v).astype(BF)
        dyb_ref[...] = dyb
        vgd_ref[0:1, :] += _colsum(dxo_v * ym_ref[...])
        dycat = lax.dot_general(dyb, w_ref[...], CONTRACT_LAST, preferred_element_type=F32)
        dyc = dycat[:, 0:M]
        dyr = dycat[:, M:2 * M]
        z, rstd = _layernorm_parts(cq_ref[...])
        lng = _row(ln_ref, 0)
        l = z * lng + _row(ln_ref, 1)
        sl = _sigmoid(l)
        dl = dyc * (sl * (1.0 + l * (1.0 - sl)))
        vgm_ref[0:1, :] += _colsum(dl * z)
        vgm_ref[1:2, :] += _colsum(dl)
        dz = dl * lng
        dcq_ref[...] = rstd * (dz - _rowmean(dz) - z * _rowmean(dz * z))
        uy = uy_ref[...]
        u2 = uy * uy
        th = jnp.tanh(GELU_K * (uy + GELU_C * uy * u2))
        gelu = 0.5 * uy * (1.0 + th)
        dgelu = 0.5 * (1.0 + th) + 0.5 * uy * (1.0 - th * th) * (GELU_K * (1.0 + 3.0 * GELU_C * u2))
        dh_ref[...] = dyr * gelu
        duy_ref[...] = dyr * h_ref[...] * dgelu

    tok = pl.BlockSpec((ts, D), lambda i: (i, 0))
    mtok = lambda o: pl.BlockSpec((ts, M), lambda i: (i, o))
    return _pcall(
        body, name=name, grid=(S // ts,),
        in_specs=[tok, tok, pl.BlockSpec(vecs.shape, lambda i: (0, 0)), pl.BlockSpec(wout.shape, lambda i: (0, 0)),
                  mtok(0), pl.BlockSpec(lnv.shape, lambda i: (0, 0)), mtok(3), mtok(0)],
        out_specs=[mtok(0), mtok(0), mtok(0), tok, pl.BlockSpec((SUBLANES, D), lambda i: (0, 0)),
                   pl.BlockSpec((SUBLANES, M), lambda i: (0, 0))],
        out_shape=[jax.ShapeDtypeStruct((S, M), F32)] * 3 + [jax.ShapeDtypeStruct((S, D), BF),
                   jax.ShapeDtypeStruct((SUBLANES, D), F32), jax.ShapeDtypeStruct((SUBLANES, M), F32)],
        compiler_params=_params(("arbitrary",)),
    )(dxo, ym, vecs, wout, cq, lnv, proj, h)


def mix_in_bwd(dparts, x, dxo, vecs, win, name):
    S, D = x.shape
    M = dparts[0].shape[1]
    ts = min(512, S)

    def body(d0, d1, d2, d3, x_ref, dxo_ref, v_ref, w_ref, dx_ref, hb_ref, dp_ref, vg_ref):
        @pl.when(pl.program_id(0) == 0)
        def _():
            vg_ref[...] = jnp.zeros_like(vg_ref)

        for q, dref in enumerate((d0, d1, d2, d3)):
            dp_ref[:, q * M:(q + 1) * M] = dref[...].astype(BF)
        dh = lax.dot_general(dp_ref[...], w_ref[...], CONTRACT_LAST, preferred_element_type=F32)
        xv = x_ref[...]
        r = lax.rsqrt(_rowmean(xv * xv) + EPS)
        n = xv * r
        g = _row(v_ref, R_G2)
        sc1 = 1.0 + _row(v_ref, R_SC2)
        gsc = g * sc1
        hb_ref[...] = (n * gsc + _row(v_ref, R_SH2)).astype(BF)
        dhn = dh * n
        vg_ref[0:1, :] += _colsum(dh)
        vg_ref[1:2, :] += _colsum(dhn) * g
        vg_ref[2:3, :] += _colsum(dhn) * sc1
        dn = dh * gsc
        dx_ref[...] = dxo_ref[...] + r * (dn - n * _rowmean(dn * n))

    tok = pl.BlockSpec((ts, D), lambda i: (i, 0))
    mtok = pl.BlockSpec((ts, M), lambda i: (i, 0))
    return _pcall(
        body, name=name, grid=(S // ts,),
        in_specs=[mtok] * 4 + [tok, tok, pl.BlockSpec(vecs.shape, lambda i: (0, 0)), pl.BlockSpec(win.shape, lambda i: (0, 0))],
        out_specs=[tok, tok, pl.BlockSpec((ts, 4 * M), lambda i: (i, 0)), pl.BlockSpec((SUBLANES, D), lambda i: (0, 0))],
        out_shape=[jax.ShapeDtypeStruct((S, D), F32), jax.ShapeDtypeStruct((S, D), BF),
                   jax.ShapeDtypeStruct((S, 4 * M), BF), jax.ShapeDtypeStruct((SUBLANES, D), F32)],
        compiler_params=_params(("arbitrary",)),
    )(*dparts, x, dxo, vecs, win)


def _adamw(w, g, m, v):
    m = ADAM_B1 * m + (1.0 - ADAM_B1) * g
    v = ADAM_B2 * v + (1.0 - ADAM_B2) * (g * g)
    m_hat = m / (1.0 - ADAM_B1 ** ADAM_STEP)
    v_hat = v / (1.0 - ADAM_B2 ** ADAM_STEP)
    delta = -ADAM_LR * (m_hat / (jnp.sqrt(v_hat) + ADAM_EPS) + ADAM_WD * w)
    return delta, m, v


def adam_big(w, g, m, v, name):
    R, C = w.shape
    tr = 256 if R % 256 == 0 else R // 2 if (R // 2) % SUBLANES == 0 and R > 512 else R
    tc = C if C <= 1536 else (1152 if C % 1152 == 0 else 1024)
    assert R % tr == 0 and C % tc == 0

    def body(w_ref, g_ref, m_ref, v_ref, d_ref, nm_ref, nv_ref):
        d, nm, nv = _adamw(w_ref[...], g_ref[...], m_ref[...], v_ref[...])
        d_ref[...] = d
        nm_ref[...] = nm
        nv_ref[...] = nv

    blk = pl.BlockSpec((tr, tc), lambda i, j: (i, j))
    return _pcall(
        body, name=name, grid=(R // tr, C // tc), in_specs=[blk] * 4, out_specs=[blk] * 3,
        out_shape=[jax.ShapeDtypeStruct((R, C), F32)] * 3, compiler_params=_params(("parallel", "parallel")),
    )(w, g, m, v)


def adam_cond(c_all, dmod, w, m, v, name):
    B, Kin = c_all.shape
    N = w.shape[1]
    tn = 256
    assert N % tn == 0

    def body(c_ref, d_ref, w_ref, m_ref, v_ref, g_ref, dl_ref, nm_ref, nv_ref):
        cv = c_ref[...]
        ca = cv * _sigmoid(cv)
        g = lax.dot_general(ca, d_ref[...], CONTRACT_FIRST, preferred_element_type=F32, precision=lax.Precision.HIGHEST)
        d, nm, nv = _adamw(w_ref[...], g, m_ref[...], v_ref[...])
        g_ref[...] = g
        dl_ref[...] = d
        nm_ref[...] = nm
        nv_ref[...] = nv

    blk = pl.BlockSpec((Kin, tn), lambda n: (0, n))
    return _pcall(
        body, name=name, grid=(N // tn,),
        in_specs=[pl.BlockSpec((B, Kin), lambda n: (0, 0)), pl.BlockSpec((B, tn), lambda n: (0, n)), blk, blk, blk],
        out_specs=[blk] * 4, out_shape=[jax.ShapeDtypeStruct((Kin, N), F32)] * 4,
        compiler_params=_params(("parallel",)),
    )(c_all, dmod, w, m, v)


def adam_small(ws, gs, ms, vs, name):
    n = len(ws)

    def body(*refs):
        ins, outs = refs[:4 * n], refs[4 * n:]
        for k in range(n):
            d, nm, nv = _adamw(ins[k][...], ins[n + k][...], ins[2 * n + k][...], ins[3 * n + k][...])
            outs[k][...] = d
            outs[n + k][...] = nm
            outs[2 * n + k][...] = nv

    vm = pl.BlockSpec(memory_space=pltpu.VMEM)
    shapes = [jax.ShapeDtypeStruct(w.shape, F32) for w in ws]
    out = _pcall(body, name=name, in_specs=[vm] * (4 * n), out_specs=[vm] * (3 * n), out_shape=shapes * 3,
                 compiler_params=_params())(*ws, *gs, *ms, *vs)
    return out[:n], out[n:2 * n], out[2 * n:]


def _me():
    return lax.axis_index("x"), lax.axis_index("y"), lax.axis_index("c")


def _flip(x, y, p):
    return (x ^ (p >> 1) if (p >> 1) else x), (y ^ (p & 1) if (p & 1) else y)


def _handshake(peers):
    barrier = pltpu.get_barrier_semaphore()
    for peer in peers:
        pl.semaphore_signal(barrier, inc=1, device_id=peer, device_id_type=MESH)
    pl.semaphore_wait(barrier, len(peers))


def _seq_call(body, *, name, n_in, out_shape, sem_shapes, collective_id):
    del n_in
    return pl.kernel(body, out_type=out_shape, mesh=plsc.ScalarSubcoreMesh(axis_name="sq", num_cores=1), name=name,
                     scratch_types=sem_shapes, compiler_params=pltpu.CompilerParams(collective_id=collective_id))


def _hbm_comm_call(body, *, name, n_in, out_shape, sem_shapes, seq_id):
    if seq_id is not None:
        return _seq_call(body, name=name, n_in=n_in, out_shape=out_shape, sem_shapes=sem_shapes, collective_id=seq_id)
    anyspec = pl.BlockSpec(memory_space=pl.ANY)
    return _pcall(body, name=name, in_specs=[anyspec] * n_in, out_specs=[anyspec] * len(out_shape), out_shape=out_shape,
                  scratch_shapes=sem_shapes, compiler_params=_params())


def allgather_devices(v, name, with_sum=False):
    R, L = v.shape

    def body(v_ref, out_ref, *rest):
        if with_sum:
            sum_ref, send_sems, recv_sems = rest
        else:
            send_sems, recv_sems = rest
        x, y, c = _me()
        me = 4 * x + 2 * y + c
        out_ref[me] = v_ref[...]
        copies = []
        for p in range(1, N_DEV):
            px, py = _flip(x, y, p >> 1)
            pc = (1 - c) if (p & 1) else c
            peer = 4 * px + 2 * py + pc
            send = pltpu.make_async_remote_copy(src_ref=v_ref, dst_ref=out_ref.at[me], send_sem=send_sems.at[p - 1],
                                                recv_sem=recv_sems.at[p - 1], device_id=(px, py, pc), device_id_type=MESH)
            send.start()
            recv = pltpu.make_async_remote_copy(src_ref=v_ref, dst_ref=out_ref.at[peer], send_sem=send_sems.at[p - 1],
                                                recv_sem=recv_sems.at[p - 1], device_id=(px, py, pc), device_id_type=MESH)
            copies.append((send, recv))
        for send, recv in copies:
            recv.wait_recv()
        for send, recv in copies:
            send.wait_send()
        if with_sum:
            s = out_ref[0]
            for k in range(1, N_DEV):
                s = s + out_ref[k]
            sum_ref[...] = s

    vm = pl.BlockSpec(memory_space=pltpu.VMEM)
    out_shape = [jax.ShapeDtypeStruct((N_DEV, R, L), F32)]
    if with_sum:
        out_shape.append(jax.ShapeDtypeStruct((R, L), F32))
    return _pcall(
        body, name=name, in_specs=[vm], out_specs=[vm] * len(out_shape), out_shape=out_shape,
        scratch_shapes=[pltpu.SemaphoreType.DMA((N_DEV - 1,)), pltpu.SemaphoreType.DMA((N_DEV - 1,))],
        compiler_params=_params(),
    )(v)


def allgather_devices_hbm(v, name, seq_id):
    R, L = v.shape

    def body(v_ref, out_ref, send_sems, recv_sems, local_sem):
        x, y, c = _me()
        me = 4 * x + 2 * y + c
        peers = []
        for p in range(1, N_DEV):
            px, py = _flip(x, y, p >> 1)
            peers.append((px, py, (1 - c) if (p & 1) else c))
        _handshake(peers)
        lc = pltpu.make_async_copy(v_ref, out_ref.at[me], local_sem)
        lc.start()
        copies = []
        for p, (px, py, pc) in enumerate(peers):
            send = pltpu.make_async_remote_copy(src_ref=v_ref, dst_ref=out_ref.at[me], send_sem=send_sems.at[p],
                                                recv_sem=recv_sems.at[p], device_id=(px, py, pc), device_id_type=MESH)
            send.start()
            recv = pltpu.make_async_remote_copy(src_ref=v_ref, dst_ref=out_ref.at[4 * px + 2 * py + pc], send_sem=send_sems.at[p],
                                                recv_sem=recv_sems.at[p], device_id=(px, py, pc), device_id_type=MESH)
            copies.append((send, recv))
        for send, recv in copies:
            recv.wait_recv()
        for send, recv in copies:
            send.wait_send()
        lc.wait()

    return _seq_call(body, name=name, n_in=1, out_shape=[jax.ShapeDtypeStruct((N_DEV, R, L), F32)],
                     sem_shapes=[pltpu.SemaphoreType.DMA((N_DEV - 1,)), pltpu.SemaphoreType.DMA((N_DEV - 1,)),
                                 pltpu.SemaphoreType.DMA], collective_id=seq_id)(v)[0]


def sum_slots(g, name):
    n, R, L = g.shape
    tr = 216 if R % 216 == 0 else R
    assert R % tr == 0 and tr % SUBLANES == 0

    def body(g_ref, o_ref):
        s = g_ref[0]
        for k in range(1, n):
            s = s + g_ref[k]
        o_ref[...] = s

    return _pcall(body, name=name, grid=(R // tr,), in_specs=[pl.BlockSpec((n, tr, L), lambda i: (0, i, 0))],
                  out_specs=pl.BlockSpec((tr, L), lambda i: (i, 0)), out_shape=jax.ShapeDtypeStruct((R, L), F32),
                  compiler_params=_params(("parallel",)))(g)


def allgather_chips(v, name):
    R, L = v.shape

    def body(v_ref, out_ref, send_sems, recv_sems):
        x, y, c = _me()
        chip = 2 * x + y
        out_ref[chip] = v_ref[...]
        copies = []
        for p in range(1, N_CHIPS):
            px, py = _flip(x, y, p)
            send = pltpu.make_async_remote_copy(src_ref=v_ref, dst_ref=out_ref.at[chip], send_sem=send_sems.at[p - 1],
                                                recv_sem=recv_sems.at[p - 1], device_id=(px, py, c), device_id_type=MESH)
            send.start()
            recv = pltpu.make_async_remote_copy(src_ref=v_ref, dst_ref=out_ref.at[2 * px + py], send_sem=send_sems.at[p - 1],
                                                recv_sem=recv_sems.at[p - 1], device_id=(px, py, c), device_id_type=MESH)
            copies.append((send, recv))
        for send, recv in copies:
            recv.wait_recv()
        for send, recv in copies:
            send.wait_send()

    vm = pl.BlockSpec(memory_space=pltpu.VMEM)
    return _pcall(
        body, name=name, in_specs=[vm], out_specs=vm, out_shape=jax.ShapeDtypeStruct((N_CHIPS, R, L), F32),
        scratch_shapes=[pltpu.SemaphoreType.DMA((N_CHIPS - 1,)), pltpu.SemaphoreType.DMA((N_CHIPS - 1,))],
        compiler_params=_params(),
    )(v)


def _shard_window(ref, kind, shard_shape, chip, half):
    r, c = shard_shape
    hr = r // 2
    if kind == "col":
        return ref.at[pl.ds(pl.multiple_of(half * hr, hr), hr), pl.ds(pl.multiple_of(chip * c, c), c)]
    return ref.at[pl.ds(pl.multiple_of(chip * r + half * hr, hr), hr), :]


def allgather_weights(shards, kinds, name, seq_id=None):
    n = len(shards)
    fulls = []
    for s, kind in zip(shards, kinds):
        r, c = s.shape
        fulls.append(jax.ShapeDtypeStruct((r, N_CHIPS * c) if kind == "col" else (N_CHIPS * r, c), s.dtype))

    def body(*refs):
        srcs, outs = refs[:n], refs[n:2 * n]
        send_sems, recv_sems, fsend_sems, frecv_sems, local_sems = refs[2 * n:]
        x, y, c = _me()
        chip = 2 * x + y
        sib = (x, y, 1 - c)
        if seq_id is not None:
            _handshake([(*_flip(x, y, p), c) for p in range(1, N_CHIPS)] + [sib])
        locals_, sends, fwds = [], [], []
        for i in range(n):
            shp = srcs[i].shape
            hr = shp[0] // 2
            win_all = (outs[i].at[:, pl.ds(pl.multiple_of(chip * shp[1], shp[1]), shp[1])] if kinds[i] == "col"
                       else outs[i].at[pl.ds(pl.multiple_of(chip * shp[0], shp[0]), shp[0]), :])
            lc = pltpu.make_async_copy(srcs[i], win_all, local_sems.at[i])
            lc.start()
            locals_.append(lc)
            my_half = srcs[i].at[pl.ds(pl.multiple_of(c * hr, hr), hr), :]
            for p in range(1, N_CHIPS):
                px, py = _flip(x, y, p)
                k = i * (N_CHIPS - 1) + p - 1
                cp = pltpu.make_async_remote_copy(src_ref=my_half, dst_ref=_shard_window(outs[i], kinds[i], shp, chip, c),
                                                  send_sem=send_sems.at[k], recv_sem=recv_sems.at[k],
                                                  device_id=(px, py, c), device_id_type=MESH)
                cp.start()
                sends.append(cp)
        for i in range(n):
            shp = srcs[i].shape
            for p in range(1, N_CHIPS):
                px, py = _flip(x, y, p)
                k = i * (N_CHIPS - 1) + p - 1
                landed = _shard_window(outs[i], kinds[i], shp, 2 * px + py, c)
                pltpu.make_async_remote_copy(src_ref=landed, dst_ref=landed, send_sem=send_sems.at[k], recv_sem=recv_sems.at[k],
                                             device_id=(px, py, c), device_id_type=MESH).wait_recv()
                fw = pltpu.make_async_remote_copy(src_ref=landed, dst_ref=landed, send_sem=fsend_sems.at[k],
                                                  recv_sem=frecv_sems.at[k], device_id=sib, device_id_type=MESH)
                fw.start()
                fwds.append(fw)
        for i in range(n):
            shp = srcs[i].shape
            for p in range(1, N_CHIPS):
                px, py = _flip(x, y, p)
                k = i * (N_CHIPS - 1) + p - 1
                other = _shard_window(outs[i], kinds[i], shp, 2 * px + py, 1 - c)
                pltpu.make_async_remote_copy(src_ref=other, dst_ref=other, send_sem=fsend_sems.at[k], recv_sem=frecv_sems.at[k],
                                             device_id=sib, device_id_type=MESH).wait_recv()
        for cp in sends + fwds:
            cp.wait_send()
        for lc in locals_:
            lc.wait()

    nk = n * (N_CHIPS - 1)
    return _hbm_comm_call(
        body, name=name, n_in=n, out_shape=fulls, seq_id=seq_id,
        sem_shapes=[pltpu.SemaphoreType.DMA((nk,)), pltpu.SemaphoreType.DMA((nk,)), pltpu.SemaphoreType.DMA((nk,)),
                    pltpu.SemaphoreType.DMA((nk,)), pltpu.SemaphoreType.DMA((n,))],
    )(*shards)


def _as_halves(g, kind, shard_shape):
    r, c = shard_shape
    if kind == "col":
        return g.reshape(2, r // 2, N_CHIPS * c)
    return g.reshape(N_CHIPS, 2, r // 2, c)


def exchange_sibling_halves(grads, kinds, shard_shapes, name, seq_id=None):
    n = len(grads)
    views = [_as_halves(g, k, s) for g, k, s in zip(grads, kinds, shard_shapes)]
    outs = []
    for k, (r, c) in zip(kinds, shard_shapes):
        outs.append(jax.ShapeDtypeStruct((r // 2, N_CHIPS * c) if k == "col" else (N_CHIPS, r // 2, c), F32))

    def body(*refs):
        srcs, dsts = refs[:n], refs[n:2 * n]
        send_sems, recv_sems = refs[2 * n:]
        x, y, c = _me()
        if seq_id is not None:
            _handshake([(x, y, 1 - c)])
        cps = []
        for i in range(n):
            src = srcs[i].at[1 - c] if kinds[i] == "col" else srcs[i].at[:, 1 - c]
            cp = pltpu.make_async_remote_copy(src_ref=src, dst_ref=dsts[i], send_sem=send_sems.at[i], recv_sem=recv_sems.at[i],
                                              device_id=(x, y, 1 - c), device_id_type=MESH)
            cp.start()
            cps.append(cp)
        for cp in cps:
            cp.wait_recv()
        for cp in cps:
            cp.wait_send()

    return _hbm_comm_call(body, name=name, n_in=n, out_shape=outs, seq_id=seq_id,
                          sem_shapes=[pltpu.SemaphoreType.DMA((n,)), pltpu.SemaphoreType.DMA((n,))])(*views)


def add_sibling_half(g, recv, kind, shard_shape, cidx, name):
    r, c = shard_shape
    hr = r // 2
    gv = _as_halves(g, kind, shard_shape)
    tr = hr if hr <= 512 else (256 if hr % 256 == 0 else hr // 2)
    assert hr % tr == 0

    def body(ci_ref, g_ref, r_ref, h_ref, hb_ref):
        s = g_ref[...] + r_ref[...]
        h_ref[...] = s
        hb_ref[...] = s.astype(BF)

    if kind == "col":
        grid = (hr // tr, N_CHIPS)
        g_spec = pl.BlockSpec((None, tr, c), lambda i, k, ci: (ci[0], i, k))
        o_spec = pl.BlockSpec((tr, c), lambda i, k, ci: (i, k))
    else:
        grid = (hr // tr, N_CHIPS)
        g_spec = pl.BlockSpec((None, None, tr, c), lambda i, k, ci: (k, ci[0], i, 0))
        o_spec = pl.BlockSpec((None, tr, c), lambda i, k, ci: (k, i, 0))
    gs = pltpu.PrefetchScalarGridSpec(num_scalar_prefetch=1, grid=grid, in_specs=[g_spec, o_spec], out_specs=[o_spec, o_spec])
    return _pcall(
        body, name=name, grid_spec=gs,
        out_shape=[jax.ShapeDtypeStruct(recv.shape, F32), jax.ShapeDtypeStruct(recv.shape, BF)],
        compiler_params=_params(("parallel", "parallel")),
    )(cidx, gv, recv)


def exchange_chip_pieces(hbs, kinds, shard_shapes, name, seq_id=None):
    n = len(hbs)
    outs = [jax.ShapeDtypeStruct((N_CHIPS - 1, r // 2, c), BF) for (r, c) in shard_shapes]

    def body(*refs):
        srcs, dsts = refs[:n], refs[n:2 * n]
        send_sems, recv_sems = refs[2 * n:]
        x, y, c = _me()
        if seq_id is not None:
            _handshake([(*_flip(x, y, p), c) for p in range(1, N_CHIPS)])
        cps = []
        for i in range(n):
            cc = shard_shapes[i][1]
            for p in range(1, N_CHIPS):
                px, py = _flip(x, y, p)
                pchip = 2 * px + py
                src = (srcs[i].at[:, pl.ds(pl.multiple_of(pchip * cc, cc), cc)] if kinds[i] == "col" else srcs[i].at[pchip])
                k = i * (N_CHIPS - 1) + p - 1
                cp = pltpu.make_async_remote_copy(src_ref=src, dst_ref=dsts[i].at[p - 1], send_sem=send_sems.at[k],
                                                  recv_sem=recv_sems.at[k], device_id=(px, py, c), device_id_type=MESH)
                cp.start()
                cps.append(cp)
        for cp in cps:
            cp.wait_recv()
        for cp in cps:
            cp.wait_send()

    nk = n * (N_CHIPS - 1)
    return _hbm_comm_call(body, name=name, n_in=n, out_shape=outs, seq_id=seq_id,
                          sem_shapes=[pltpu.SemaphoreType.DMA((nk,)), pltpu.SemaphoreType.DMA((nk,))])(*hbs)


def sum_chip_pieces(h, pieces, kind, shard_shape, chip_core, name):
    r, c = shard_shape
    hr = r // 2
    tr = hr if hr <= 512 else (256 if hr % 256 == 0 else hr // 2)
    assert hr % tr == 0
    nrb = hr // tr

    def body(ci_ref, h_ref, p_ref, q_ref):
        q_ref[...] = ((h_ref[...] + p_ref[0].astype(F32)) + p_ref[1].astype(F32)) + p_ref[2].astype(F32)

    if kind == "col":
        h_spec = pl.BlockSpec((tr, c), lambda i, ci: (i, ci[0]))
    else:
        h_spec = pl.BlockSpec((None, tr, c), lambda i, ci: (ci[0], i, 0))
    gs = pltpu.PrefetchScalarGridSpec(
        num_scalar_prefetch=1, grid=(nrb,),
        in_specs=[h_spec, pl.BlockSpec((N_CHIPS - 1, tr, c), lambda i, ci: (0, i, 0))],
        out_specs=pl.BlockSpec((tr, c), lambda i, ci: (ci[1] * nrb + i, 0)))
    return _pcall(body, name=name, grid_spec=gs, out_shape=jax.ShapeDtypeStruct((r, c), F32),
                  compiler_params=_params(("parallel",)))(chip_core, h, pieces)


def exchange_reduced_halves(qs, name):
    n = len(qs)

    def body(*refs):
        bufs = refs[n:2 * n]
        send_sems, recv_sems = refs[2 * n:]
        x, y, c = _me()
        cps = []
        for i in range(n):
            hr = bufs[i].shape[0] // 2
            mine = bufs[i].at[pl.ds(pl.multiple_of(c * hr, hr), hr), :]
            other = bufs[i].at[pl.ds(pl.multiple_of((1 - c) * hr, hr), hr), :]
            cp = pltpu.make_async_remote_copy(src_ref=mine, dst_ref=mine, send_sem=send_sems.at[i], recv_sem=recv_sems.at[i],
                                              device_id=(x, y, 1 - c), device_id_type=MESH)
            cp.start()
            cps.append((cp, pltpu.make_async_remote_copy(src_ref=other, dst_ref=other, send_sem=send_sems.at[i],
                                                         recv_sem=recv_sems.at[i], device_id=(x, y, 1 - c), device_id_type=MESH)))
        for cp, rv in cps:
            rv.wait_recv()
        for cp, rv in cps:
            cp.wait_send()

    anyspec = pl.BlockSpec(memory_space=pl.ANY)
    return _pcall(
        body, name=name, in_specs=[anyspec] * n, out_specs=[anyspec] * n,
        out_shape=[jax.ShapeDtypeStruct(q.shape, F32) for q in qs], input_output_aliases={i: i for i in range(n)},
        scratch_shapes=[pltpu.SemaphoreType.DMA((n,)), pltpu.SemaphoreType.DMA((n,))],
        compiler_params=_params(),
    )(*qs)


def _rows128(a):
    return a.reshape(-1, LANES)


def _after(xs, *deps):
    flat = []
    for d in deps:
        flat.extend(d if isinstance(d, (list, tuple)) else [d])
    return list(lax.optimization_barrier((tuple(xs), tuple(flat)))[0])


def _block_diag(w):
    H, d, _ = w.shape
    eye = jnp.eye(H, dtype=w.dtype)
    return jnp.einsum("hde,hg->hdge", w, eye).reshape(H * d, H * d)


def _diag_blocks(g4, H, d):
    nb = g4.shape[0]
    per = LANES // d
    g = g4.reshape(nb, per, d, per, d)
    return jnp.stack([g[:, j, :, j, :] for j in range(per)], axis=1).reshape(H, d, d)


def kernel(x, c, w_mod, b_mod, g_ffn1, w_ffn1_in, w_ffn1_out, g_mix, w_in, conv_w, conv_b, ln_g, ln_b, rnn_conv_w, rnn_conv_b, w_a, b_a, w_i, b_i, lru_lambda, w_out, g_ffn2, w_ffn2_in, w_ffn2_out, w_fmod, b_fmod, g_final, loss_target, m_w_mod, m_b_mod, m_g_ffn1, m_w_ffn1_in, m_w_ffn1_out, m_g_mix, m_w_in, m_conv_w, m_conv_b, m_ln_g, m_ln_b, m_rnn_conv_w, m_rnn_conv_b, m_w_a, m_b_a, m_w_i, m_b_i, m_lru_lambda, m_w_out, m_g_ffn2, m_w_ffn2_in, m_w_ffn2_out, m_w_fmod, m_b_fmod, m_g_final, v_w_mod, v_b_mod, v_g_ffn1, v_w_ffn1_in, v_w_ffn1_out, v_g_mix, v_w_in, v_conv_w, v_conv_b, v_ln_g, v_ln_b, v_rnn_conv_w, v_rnn_conv_b, v_w_a, v_b_a, v_w_i, v_b_i, v_lru_lambda, v_w_out, v_g_ffn2, v_w_ffn2_in, v_w_ffn2_out, v_w_fmod, v_b_fmod, v_g_final):
    S, D = x.shape[1], x.shape[2]
    M = conv_b.shape[1]
    H, HD = w_a.shape[1], w_a.shape[2]
    nb = M // LANES
    ix, iy, ic = lax.axis_index("x"), lax.axis_index("y"), lax.axis_index("c")
    chip = 2 * ix + iy
    dev = 2 * chip + ic
    cidx = jnp.reshape(ic, (1,)).astype(I32)
    chip_core = jnp.stack([chip, ic]).astype(I32)
    xs = x[0]
    tgt = loss_target[0]

    kinds = ["col", "row"]
    w_f1, w_mx, w_f2 = [w_ffn1_in[0], w_ffn1_out[0]], [w_in[0], w_out[0]], [w_ffn2_in[0], w_ffn2_out[0]]
    as_bf = lambda ws: [w.astype(BF) for w in ws]
    shapes_of = lambda ws: [w.shape for w in ws]
    wi1, wo1 = allgather_weights(as_bf(w_f1), kinds, "gather_ffn1", seq_id=9)
    win, wout = allgather_weights(as_bf(w_mx), kinds, "gather_mix", seq_id=1)
    wi2, wo2 = allgather_weights(as_bf(w_f2), kinds, "gather_ffn2", seq_id=2)

    c_all =allgather_devices(_rows128(c), "gather_c")[0].reshape(N_DEV, D)
    mod_cols = cond_matmul(c_all, w_mod[0], "mod_proj")
    fmod_cols = cond_matmul(c_all, w_fmod, "fmod_proj")
    convw_pad = jnp.pad(conv_w[0], ((0, 32 - CONV_WIDTH), (0, 0)))
    rnnw_pad = jnp.pad(rnn_conv_w[0], ((0, SUBLANES - RNN_CONV_WIDTH), (0, 0)))
    n_mod, n_fmod = mod_cols.shape[1], fmod_cols.shape[1]
    small = jnp.concatenate([_rows128(mod_cols), _rows128(fmod_cols), convw_pad, rnnw_pad], axis=0)
    small4 = allgather_chips(small, "gather_cond")
    r0 = N_DEV * n_mod // LANES
    r1 = r0 + N_DEV * n_fmod // LANES
    mod_all = small4[:, :r0].reshape(N_CHIPS, N_DEV, n_mod)
    fmod_all = small4[:, r0:r1].reshape(N_CHIPS, N_DEV, n_fmod)
    convw4 = small4[:, r1:r1 + 32]
    rnnw4 = small4[:, r1 + 32:r1 + 32 + SUBLANES]
    mod_row = lax.dynamic_index_in_dim(mod_all, dev, axis=1, keepdims=False).reshape(1, N_CHIPS * n_mod) + b_mod
    fmod_row = lax.dynamic_index_in_dim(fmod_all, dev, axis=1, keepdims=False).reshape(1, N_CHIPS * n_fmod) + b_fmod[None, :]
    vecs = jnp.concatenate([mod_row.reshape(9, D), fmod_row.reshape(2, D), g_ffn1, g_mix, g_ffn2, g_final[None, :],
                            jnp.zeros((1, D), F32)], axis=0)
    lnv = jnp.concatenate([ln_g, ln_b, jnp.zeros((SUBLANES - 2, M), F32)], axis=0)
    bda = _block_diag(w_a[0]).astype(BF)
    bdi = _block_diag(w_i[0]).astype(BF)

    def reduce_add(gs, recv, ws, tag, kinds_=kinds):
        pairs = [add_sibling_half(g, r_, k, w.shape, cidx, f"add_sibling_{tag}{j}")
                 for j, (g, r_, k, w) in enumerate(zip(gs, recv, kinds_, ws))]
        return [p[0] for p in pairs], [p[1] for p in pairs]

    def reduce_sum(hs_, recv, ws, tag, kinds_=kinds):
        return [sum_chip_pieces(h_, p_, k, w.shape, chip_core, f"sum_chips_{tag}{j}")
                for j, (h_, p_, k, w) in enumerate(zip(hs_, recv, kinds_, ws))]

    rows1 = (R_SH1, R_SC1, R_GT1, R_G1)
    rows3 = (R_SH3, R_SC3, R_GT3, R_G3)
    x1, g1s, u1s, y1 = ffn_fwd(xs, vecs, wi1, wo1, rows1, "ffn1_fwd")
    proj = norm_matmul(x1, vecs, win, (R_SH2, R_SC2, R_G2), "mix_in_proj")
    cq = conv_fwd(proj, convw4, conv_b, "conv_fwd")
    xr, ra, ii, hh = rnn_fwd(proj, rnnw4, rnn_conv_b, bda, bdi, b_a, b_i, lru_lambda, "rnn_fwd")
    x2, ym, ycat = mix_out(cq, proj, hh, x1, vecs, lnv, wout, "mix_out")
    x3, g2s, u2s, y2 = ffn_fwd(x2, vecs, wi2, wo2, rows3, "ffn2_fwd")
    dx3, vgf = final_fwd_bwd(x3, tgt, vecs, "final_loss")

    Fd = wo1.shape[0]
    tk = min(512, S)
    dx2, act2, dg2, du2, h3b, dy2b, vg3 = ffn_bwd(dx3, x2, vecs, g2s, u2s, y2, wi2, wo2, rows3, "ffn2_bwd")
    gwo2 = matmul(act2, dy2b, "tn", tm=Fd // 2, tn=D, tk=tk, name="ffn2_dwo")
    gwi2 = matmul(h3b, dg2, "tn", tm=D, tn=Fd // 2, tk=tk, name="ffn2_dwg", out_cols=2 * Fd)
    gwi2 = matmul(h3b, du2, "tn", tm=D, tn=Fd // 2, tk=tk, name="ffn2_dwu", out_cols=2 * Fd, col_off=Fd, prev=gwi2)
    recv1_f2 = exchange_sibling_halves([gwi2, gwo2], kinds, shapes_of(w_f2), "reduce1_ffn2", seq_id=3)
    dcq, dhout, duy, dymb, vgd, vgm = mix_out_bwd(dx2, ym, vecs, wout, cq, lnv, proj, hh, "mix_out_bwd")
    gwout = matmul(ycat, dymb, "tn", tm=2 * M, tn=D, tk=tk, name="mix_dwout")
    recv1_f2 = _after(recv1_f2, gwout)
    h_f2, hb_f2 = reduce_add([gwi2, gwo2], recv1_f2, w_f2, "ffn2_")
    recv2_f2 = exchange_chip_pieces(hb_f2, kinds, shapes_of(w_f2), "reduce2_ffn2", seq_id=4)
    duv, dug, dconvw4, dconvb = conv_bwd(_after([dcq], hb_f2)[0], proj, convw4, "conv_bwd")
    dux, dwa4, dwi4, drnnw4, rvec = rnn_bwd(dhout, hh, xr, ra, ii, proj, rnnw4, bda, bdi, lru_lambda, "rnn_bwd")
    dx1, h2b, dpb, vg2 = mix_in_bwd((duv, dug, dux, duy), x1, dx2, vecs, win, "mix_in_bwd")
    gwin = matmul(h2b, dpb, "tn", tm=D, tn=1024, tk=tk, name="mix_dwin")
    recv1_mx = exchange_sibling_halves([gwin, gwout], kinds, shapes_of(w_mx), "reduce1_mix", seq_id=5)
    g_f2 = exchange_reduced_halves(reduce_sum(_after(h_f2, gwin), recv2_f2, w_f2, "ffn2_"), "reduce3_ffn2")
    adam_f2 = [adam_big(w, g, m, v, "adam_" + nm) for w, g, m, v, nm in
               zip(w_f2, g_f2, [m_w_ffn2_in[0], m_w_ffn2_out[0]], [v_w_ffn2_in[0], v_w_ffn2_out[0]], ["ffn2_in", "ffn2_out"])]
    h_mx, hb_mx = reduce_add([gwin, gwout], _after(recv1_mx, adam_f2[0][0], adam_f2[1][0]), w_mx, "mix_")
    recv2_mx = exchange_chip_pieces(hb_mx, kinds, shapes_of(w_mx), "reduce2_mix", seq_id=6)
    dx0, act1, dg1, du1, h1b, dy1b, vg1 = ffn_bwd(_after([dx1], hb_mx)[0], xs, vecs, g1s, u1s, y1, wi1, wo1, rows1, "ffn1_bwd")
    dmod_row = jnp.concatenate([vg1[1:3], vg1[0:1], vg2[0:2], vgd[0:1], vg3[1:3], vg3[0:1]], axis=0)
    gains = jnp.concatenate([vg1[3:4], vg2[2:3], vg3[3:4], vgf[2:4]], axis=0)
    mvecs = jnp.concatenate([dconvb, vgm[0:2], rvec[0:4], jnp.zeros((1, M), F32)], axis=0)
    parts = [_rows128(dmod_row), _rows128(vgf[0:2]), _rows128(gains), _rows128(mvecs),
             _rows128(dconvw4), _rows128(drnnw4), _rows128(_diag_blocks(dwa4, H, HD)), _rows128(_diag_blocks(dwi4, H, HD))]
    sizes = [p.shape[0] for p in parts]
    packed = jnp.concatenate(parts, axis=0)
    gathered = allgather_devices_hbm(packed, "gather_small", seq_id=10)

    gwo1 = matmul(_after([act1], recv2_mx, packed)[0], dy1b, "tn", tm=Fd // 2, tn=D, tk=tk, name="ffn1_dwo")
    w_f1o, w_f1i = w_f1[1:], w_f1[:1]
    recv1_f1o = exchange_sibling_halves([gwo1], ["row"], shapes_of(w_f1o), "reduce1_ffn1_out", seq_id=7)
    gwi1 = matmul(h1b, dg1, "tn", tm=D, tn=Fd // 2, tk=tk, name="ffn1_dwg", out_cols=2 * Fd)
    h_f1o, hb_f1o = reduce_add([gwo1], _after(recv1_f1o, gwi1), w_f1o, "ffn1_out", ["row"])
    recv2_f1o = exchange_chip_pieces(hb_f1o, ["row"], shapes_of(w_f1o), "reduce2_ffn1_out", seq_id=11)
    gwi1 = matmul(h1b, _after([du1], hb_f1o, gathered)[0], "tn", tm=D, tn=Fd // 2, tk=tk, name="ffn1_dwu", out_cols=2 * Fd,
                  col_off=Fd, prev=gwi1)
    recv1_f1i = exchange_sibling_halves([gwi1], ["col"], shapes_of(w_f1i), "reduce1_ffn1_in", seq_id=12)
    g_mx = exchange_reduced_halves(reduce_sum(h_mx, recv2_mx, w_mx, "mix_"), "reduce3_mix")
    summed = sum_slots(gathered, "sum_small")
    offs = [0]
    for s in sizes:
        offs.append(offs[-1] + s)
    seg = lambda k: summed[offs[k]:offs[k + 1]]
    g_b_mod = seg(0).reshape(1, 9 * D)
    g_b_fmod = seg(1).reshape(1, 2 * D)
    gsum = seg(2).reshape(5, D)
    loss = (0.5 / D) * jnp.sum(gsum[4])
    msum = seg(3).reshape(SUBLANES, M)
    g_conv_w = lax.dynamic_index_in_dim(seg(4).reshape(nb, 32, LANES), chip, axis=0, keepdims=False)[:CONV_WIDTH]
    g_rnn_w = lax.dynamic_index_in_dim(seg(5).reshape(nb, SUBLANES, LANES), chip, axis=0, keepdims=False)[:RNN_CONV_WIDTH]
    g_w_a = seg(6).reshape(H, HD, HD)
    g_w_i = seg(7).reshape(H, HD, HD)
    dmod_all = gathered[:, offs[0]:offs[1]].reshape(N_DEV, 9 * D)
    dfmod_all = gathered[:, offs[1]:offs[2]].reshape(N_DEV, 2 * D)
    dmod_cols = lax.dynamic_slice_in_dim(dmod_all, chip * n_mod, n_mod, axis=1)
    dfmod_cols = lax.dynamic_slice_in_dim(dfmod_all, chip * n_fmod, n_fmod, axis=1)

    h_f1i, hb_f1i = reduce_add([gwi1], _after(recv1_f1i, recv2_f1o), w_f1i, "ffn1_in", ["col"])
    recv2_f1i = exchange_chip_pieces(hb_f1i, ["col"], shapes_of(w_f1i), "reduce2_ffn1_in", seq_id=8)
    g_f1o = exchange_reduced_halves(reduce_sum(h_f1o, recv2_f1o, w_f1o, "ffn1_out", ["row"]), "reduce3_ffn1_out")
    g_f1i = exchange_reduced_halves(reduce_sum(h_f1i, recv2_f1i, w_f1i, "ffn1_in", ["col"]), "reduce3_ffn1_in")
    g_f1 = list(g_f1i) + list(g_f1o)

    big_w = w_f1 + w_mx + w_f2
    g_big = g_f1 + list(g_mx) + list(g_f2)
    names = ["ffn1_in", "ffn1_out", "w_in", "w_out", "ffn2_in", "ffn2_out"]
    big_m = [m_w_ffn1_in[0], m_w_ffn1_out[0], m_w_in[0], m_w_out[0]]
    big_v = [v_w_ffn1_in[0], v_w_ffn1_out[0], v_w_in[0], v_w_out[0]]
    big_out = [adam_big(w, g, m, v, "adam_" + nm) for w, g, m, v, nm in zip(big_w, g_big, big_m, big_v, names)] + adam_f2
    g_w_mod, d_w_mod, nm_w_mod, nv_w_mod = adam_cond(c_all, dmod_cols, w_mod[0], m_w_mod[0], v_w_mod[0], "adam_w_mod")
    g_w_fmod, d_w_fmod, nm_w_fmod, nv_w_fmod = adam_cond(c_all, dfmod_cols, w_fmod, m_w_fmod, v_w_fmod, "adam_w_fmod")

    flat2 = lambda a: a.reshape(-1, a.shape[-1])
    small_names = ["b_mod", "g_ffn1", "g_mix", "conv_w", "conv_b", "ln_g", "ln_b", "rnn_conv_w", "rnn_conv_b", "w_a", "b_a",
                   "w_i", "b_i", "lru_lambda", "g_ffn2", "b_fmod", "g_final"]
    small_w = [b_mod, g_ffn1, g_mix, conv_w, conv_b, ln_g, ln_b, rnn_conv_w, rnn_conv_b, w_a, b_a, w_i, b_i, lru_lambda,
               g_ffn2, b_fmod, g_final]
    small_m = [m_b_mod, m_g_ffn1, m_g_mix, m_conv_w, m_conv_b, m_ln_g, m_ln_b, m_rnn_conv_w, m_rnn_conv_b, m_w_a, m_b_a,
               m_w_i, m_b_i, m_lru_lambda, m_g_ffn2, m_b_fmod, m_g_final]
    small_v = [v_b_mod, v_g_ffn1, v_g_mix, v_conv_w, v_conv_b, v_ln_g, v_ln_b, v_rnn_conv_w, v_rnn_conv_b, v_w_a, v_b_a,
               v_w_i, v_b_i, v_lru_lambda, v_g_ffn2, v_b_fmod, v_g_final]
    small_g = [g_b_mod, gsum[0:1], gsum[1:2], g_conv_w, msum[0:1], msum[1:2], msum[2:3], g_rnn_w, msum[3:4], g_w_a, msum[4:5],
               g_w_i, msum[5:6], msum[6:7], gsum[2:3], g_b_fmod, gsum[3:4]]
    small_g = [g.reshape(w.shape) for g, w in zip(small_g, small_w)]
    two_d = lambda a: a.reshape(1, -1) if a.ndim == 1 else flat2(a)
    sd, sm, sv = adam_small([two_d(a) for a in small_w], [two_d(a) for a in small_g], [two_d(a) for a in small_m],
                            [two_d(a) for a in small_v], "adam_small")
    small = {}
    for k, nm in enumerate(small_names):
        shp = small_w[k].shape
        small[nm] = (small_g[k], sd[k].reshape(shp), sm[k].reshape(shp), sv[k].reshape(shp))

    big = {"w_mod": tuple(a[None] for a in (g_w_mod, d_w_mod, nm_w_mod, nv_w_mod)),
           "w_fmod": (g_w_fmod, d_w_fmod, nm_w_fmod, nv_w_fmod)}
    for nm, full, g, (d, nmm, nvv) in zip(["w_ffn1_in", "w_ffn1_out", "w_in", "w_out", "w_ffn2_in", "w_ffn2_out"],
                                         big_w, g_big, big_out):
        big[nm] = tuple(a[None] for a in (g, d, nmm, nvv))
    order = ["w_mod", "b_mod", "g_ffn1", "w_ffn1_in", "w_ffn1_out", "g_mix", "w_in", "conv_w", "conv_b", "ln_g", "ln_b",
             "rnn_conv_w", "rnn_conv_b", "w_a", "b_a", "w_i", "b_i", "lru_lambda", "w_out", "g_ffn2", "w_ffn2_in",
             "w_ffn2_out", "w_fmod", "b_fmod", "g_final"]
    table = {**small, **big}
    outs = [loss, dx0[None]]
    for kind_ in range(4):
        outs.extend(table[nm][kind_] for nm in order)
    return tuple(outs)
```

```python
import functools

import jax
import jax.numpy as jnp
from jax import lax
from jax.experimental import pallas as pl
from jax.experimental.pallas import tpu as pltpu
from jax.experimental.pallas import tpu_sc as plsc

F32 = jnp.float32
BF = jnp.bfloat16
I32 = jnp.int32
MESH = pl.DeviceIdType.MESH

EPS = 1e-6
RG_C = 8.0
MACARON_W = 0.5
CONV_WIDTH = 31
RNN_CONV_WIDTH = 4
ADAM_LR = 0.001
ADAM_B1 = 0.9
ADAM_B2 = 0.999
ADAM_EPS = 1e-08
ADAM_WD = 0.01
ADAM_STEP = 10

LANES = 128
SUBLANES = 8
VMEM_LIMIT = 48 * 1024 * 1024
N_CHIPS = 4
N_DEV = 8

R_SH1, R_SC1, R_GT1, R_SH2, R_SC2, R_GT2, R_SH3, R_SC3, R_GT3, R_FSH, R_FSC, R_G1, R_G2, R_G3, R_GF = range(15)

CONTRACT_LAST = (((1,), (1,)), ((), ()))
CONTRACT_FIRST = (((0,), (0,)), ((), ()))


def _pcall(body, **kw):
    return pl.pallas_call(body, **kw)


def _params(sem=None, vmem=VMEM_LIMIT):
    if sem is None:
        return pltpu.CompilerParams(vmem_limit_bytes=vmem)
    return pltpu.CompilerParams(dimension_semantics=sem, vmem_limit_bytes=vmem)


def _row(ref, r):
    return ref[r:r + 1, :]


def _sigmoid(x):
    return 1.0 / (1.0 + jnp.exp(-x))


def _colsum(x):
    return jnp.sum(x, axis=0, keepdims=True)


def _rowmean(x):
    return jnp.mean(x, axis=-1, keepdims=True)


def matmul(a, b, mode, *, tm, tn, tk, name, out_dtype=F32, out_cols=None, col_off=0, prev=None):
    if mode == "nn":
        (M, K), (K2, N) = a.shape, b.shape
    elif mode == "nt":
        (M, K), (N, K2) = a.shape, b.shape
    else:
        (K, M), (K2, N) = a.shape, b.shape
    assert K == K2 and M % tm == 0 and N % tn == 0 and K % tk == 0 and col_off % tn == 0
    nk = K // tk
    out_cols = N if out_cols is None else out_cols
    off = col_off // tn

    def body(*refs):
        if prev is None:
            a_ref, b_ref, o_ref, acc = refs
        else:
            a_ref, b_ref, _, o_ref, acc = refs
        k = pl.program_id(2)

        @pl.when(k == 0)
        def _():
            acc[...] = jnp.zeros_like(acc)

        av = a_ref[...].astype(BF)
        bv = b_ref[...].astype(BF)
        if mode == "nn":
            acc[...] += jnp.dot(av, bv, preferred_element_type=F32)
        elif mode == "nt":
            acc[...] += lax.dot_general(av, bv, CONTRACT_LAST, preferred_element_type=F32)
        else:
            acc[...] += lax.dot_general(av, bv, CONTRACT_FIRST, preferred_element_type=F32)

        @pl.when(k == nk - 1)
        def _():
            o_ref[...] = acc[...].astype(out_dtype)

    if mode == "nn":
        a_spec = pl.BlockSpec((tm, tk), lambda m, n, k: (m, k))
        b_spec = pl.BlockSpec((tk, tn), lambda m, n, k: (k, n))
    elif mode == "nt":
        a_spec = pl.BlockSpec((tm, tk), lambda m, n, k: (m, k))
        b_spec = pl.BlockSpec((tn, tk), lambda m, n, k: (n, k))
    else:
        a_spec = pl.BlockSpec((tk, tm), lambda m, n, k: (k, m))
        b_spec = pl.BlockSpec((tk, tn), lambda m, n, k: (k, n))
    in_specs = [a_spec, b_spec]
    args = [a, b]
    aliases = {}
    if prev is not None:
        in_specs.append(pl.BlockSpec(memory_space=pl.ANY))
        args.append(prev)
        aliases = {2: 0}
    return _pcall(
        body, name=name, grid=(M // tm, N // tn, nk), in_specs=in_specs,
        out_specs=pl.BlockSpec((tm, tn), lambda m, n, k: (m, n + off)),
        out_shape=jax.ShapeDtypeStruct((M, out_cols), out_dtype),
        scratch_shapes=[pltpu.VMEM((tm, tn), F32)], input_output_aliases=aliases,
        compiler_params=_params(("parallel", "parallel", "arbitrary")),
    )(*args)


def cond_matmul(c_all, w, name):
    B, K = c_all.shape
    N = w.shape[1]
    tn = 256
    assert N % tn == 0

    def body(c_ref, w_ref, o_ref):
        cv = c_ref[...]
        ca = cv * _sigmoid(cv)
        o_ref[...] = jnp.dot(ca, w_ref[...], preferred_element_type=F32, precision=lax.Precision.HIGHEST)

    return _pcall(
        body, name=name, grid=(N // tn,),
        in_specs=[pl.BlockSpec((B, K), lambda n: (0, 0)), pl.BlockSpec((K, tn), lambda n: (0, n))],
        out_specs=pl.BlockSpec((B, tn), lambda n: (0, n)),
        out_shape=jax.ShapeDtypeStruct((B, N), F32), compiler_params=_params(("parallel",)),
    )(c_all, w)


FFN_CH = 1408
FFN_FWD_TS = 512
FFN_BWD_TS = 256
FFN_FWD_VMEM = 56 * 1024 * 1024


def ffn_fwd(x, vecs, wi, wo, rows, name):
    r_sh, r_sc, r_gt, r_g = rows
    S, D = x.shape
    Fd = wo.shape[0]
    ts, ch = min(FFN_FWD_TS, S), FFN_CH
    ni, nj = S // ts, Fd // ch
    assert nj >= 2

    def body(x_ref, v_ref, wg_ref, wu_ref, wo_ref, xo_ref, g_ref, u_ref, y_ref, h_sc, acc):
        j = pl.program_id(1)

        @pl.when(j == 0)
        def _():
            xv = x_ref[...]
            r = lax.rsqrt(_rowmean(xv * xv) + EPS)
            gs = _row(v_ref, r_g) * (1.0 + _row(v_ref, r_sc))
            h_sc[...] = (xv * r * gs + _row(v_ref, r_sh)).astype(BF)

        hb = h_sc[...]
        G = jnp.dot(hb, wg_ref[...], preferred_element_type=F32)
        U = jnp.dot(hb, wu_ref[...], preferred_element_type=F32)
        g_ref[...] = G.astype(BF)
        u_ref[...] = U.astype(BF)
        act = (G * _sigmoid(G) * U).astype(BF)
        part = jnp.dot(act, wo_ref[...], preferred_element_type=F32)

        @pl.when(j == 0)
        def _():
            acc[...] = part

        @pl.when((j > 0) & (j < nj - 1))
        def _():
            acc[...] += part

        @pl.when(j == nj - 1)
        def _():
            Y = acc[...] + part
            y_ref[...] = Y
            xo_ref[...] = x_ref[...] + (MACARON_W * _row(v_ref, r_gt)) * Y

    tok = pl.BlockSpec((ts, D), lambda i, j: (i, 0))
    hid = pl.BlockSpec((ts, ch), lambda i, j: (i, j))
    return _pcall(
        body, name=name, grid=(ni, nj),
        in_specs=[tok, pl.BlockSpec(vecs.shape, lambda i, j: (0, 0)),
                  pl.BlockSpec((D, ch), lambda i, j: (0, j)), pl.BlockSpec((D, ch), lambda i, j: (0, j + nj)),
                  pl.BlockSpec((ch, D), lambda i, j: (j, 0))],
        out_specs=[tok, hid, hid, tok],
        out_shape=[jax.ShapeDtypeStruct((S, D), F32), jax.ShapeDtypeStruct((S, Fd), BF),
                   jax.ShapeDtypeStruct((S, Fd), BF), jax.ShapeDtypeStruct((S, D), F32)],
        scratch_shapes=[pltpu.VMEM((ts, D), BF), pltpu.VMEM((ts, D), F32)],
        compiler_params=_params(("arbitrary", "arbitrary"), FFN_FWD_VMEM),
    )(x, vecs, wi, wi, wo)


def ffn_bwd(dxo, x, vecs, gs_, us_, y, wi, wo, rows, name):
    r_sh, r_sc, r_gt, r_g = rows
    S, D = x.shape
    Fd = wo.shape[0]
    ts, ch = min(FFN_BWD_TS, S), FFN_CH
    ni, nj = S // ts, Fd // ch

    def body(dxo_ref, x_ref, v_ref, g_ref, u_ref, y_ref, wg_ref, wu_ref, wo_ref,
             dx_ref, act_ref, dg_ref, du_ref, hb_ref, dyb_ref, vg_ref, dyb_sc, dh_sc):
        i, j = pl.program_id(0), pl.program_id(1)

        @pl.when((i == 0) & (j == 0))
        def _():
            vg_ref[...] = jnp.zeros_like(vg_ref)

        @pl.when(j == 0)
        def _():
            dxo_v = dxo_ref[...]
            dyb = ((MACARON_W * _row(v_ref, r_gt)) * dxo_v).astype(BF)
            dyb_sc[...] = dyb
            dyb_ref[...] = dyb
            vg_ref[0:1, :] += MACARON_W * _colsum(dxo_v * y_ref[...])

        dA = lax.dot_general(dyb_sc[...], wo_ref[...], CONTRACT_LAST, preferred_element_type=F32)
        G = g_ref[...].astype(F32)
        U = u_ref[...].astype(F32)
        sg = _sigmoid(G)
        sl = G * sg
        dU = (dA * sl).astype(BF)
        dG = (dA * U * (sg * (1.0 + G * (1.0 - sg)))).astype(BF)
        act_ref[...] = (sl * U).astype(BF)
        dg_ref[...] = dG
        du_ref[...] = dU
        part = (lax.dot_general(dG, wg_ref[...], CONTRACT_LAST, preferred_element_type=F32)
                + lax.dot_general(dU, wu_ref[...], CONTRACT_LAST, preferred_element_type=F32))

        @pl.when(j == 0)
        def _():
            dh_sc[...] = part

        @pl.when((j > 0) & (j < nj - 1))
        def _():
            dh_sc[...] += part

        @pl.when(j == nj - 1)
        def _():
            dh = dh_sc[...] + part
            xv = x_ref[...]
            r = lax.rsqrt(_rowmean(xv * xv) + EPS)
            n = xv * r
            g = _row(v_ref, r_g)
            sc1 = 1.0 + _row(v_ref, r_sc)
            gsc = g * sc1
            hb_ref[...] = (n * gsc + _row(v_ref, r_sh)).astype(BF)
            dhn = dh * n
            vg_ref[1:2, :] += _colsum(dh)
            vg_ref[2:3, :] += _colsum(dhn) * g
            vg_ref[3:4, :] += _colsum(dhn) * sc1
            dn = dh * gsc
            dx_ref[...] = dxo_ref[...] + r * (dn - n * _rowmean(dn * n))

    tok = pl.BlockSpec((ts, D), lambda i, j: (i, 0))
    hid = pl.BlockSpec((ts, ch), lambda i, j: (i, j))
    return _pcall(
        body, name=name, grid=(ni, nj),
        in_specs=[tok, tok, pl.BlockSpec(vecs.shape, lambda i, j: (0, 0)), hid, hid, tok,
                  pl.BlockSpec((D, ch), lambda i, j: (0, j)), pl.BlockSpec((D, ch), lambda i, j: (0, j + nj)),
                  pl.BlockSpec((ch, D), lambda i, j: (j, 0))],
        out_specs=[tok, hid, hid, hid, tok, tok, pl.BlockSpec((SUBLANES, D), lambda i, j: (0, 0))],
        out_shape=[jax.ShapeDtypeStruct((S, D), F32), jax.ShapeDtypeStruct((S, Fd), BF),
                   jax.ShapeDtypeStruct((S, Fd), BF), jax.ShapeDtypeStruct((S, Fd), BF),
                   jax.ShapeDtypeStruct((S, D), BF), jax.ShapeDtypeStruct((S, D), BF),
                   jax.ShapeDtypeStruct((SUBLANES, D), F32)],
        scratch_shapes=[pltpu.VMEM((ts, D), BF), pltpu.VMEM((ts, D), F32)],
        compiler_params=_params(("arbitrary", "arbitrary")),
    )(dxo, x, vecs, gs_, us_, y, wi, wi, wo)


def final_fwd_bwd(x, tgt, vecs, name):
    S, D = x.shape
    ts = min(512, S)

    def body(x_ref, t_ref, v_ref, dx_ref, vg_ref):
        @pl.when(pl.program_id(0) == 0)
        def _():
            vg_ref[...] = jnp.zeros_like(vg_ref)

        xv = x_ref[...]
        r = lax.rsqrt(_rowmean(xv * xv) + EPS)
        n = xv * r
        g = _row(v_ref, R_GF)
        sc1 = 1.0 + _row(v_ref, R_FSC)
        gsc = g * sc1
        e = n * gsc + _row(v_ref, R_FSH) - t_ref[...]
        vg_ref[3:4, :] += _colsum(e * e)
        dout = e * (1.0 / D)
        dn_ = dout * n
        vg_ref[0:1, :] += _colsum(dout)
        vg_ref[1:2, :] += _colsum(dn_) * g
        vg_ref[2:3, :] += _colsum(dn_) * sc1
        dn = dout * gsc
        dx_ref[...] = r * (dn - n * _rowmean(dn * n))

    tok = pl.BlockSpec((ts, D), lambda i: (i, 0))
    return _pcall(
        body, name=name, grid=(S // ts,),
        in_specs=[tok, tok, pl.BlockSpec(vecs.shape, lambda i: (0, 0))],
        out_specs=[tok, pl.BlockSpec((SUBLANES, D), lambda i: (0, 0))],
        out_shape=[jax.ShapeDtypeStruct((S, D), F32), jax.ShapeDtypeStruct((SUBLANES, D), F32)],
        compiler_params=_params(("arbitrary",)),
    )(x, tgt, vecs)


def norm_matmul(x, vecs, w, rows, name):
    r_sh, r_sc, r_g = rows
    S, D = x.shape
    N = w.shape[1]
    ts, tn = min(512, S), 1024
    nn = N // tn

    def body(x_ref, v_ref, w_ref, o_ref, h_sc):
        @pl.when(pl.program_id(1) == 0)
        def _():
            xv = x_ref[...]
            r = lax.rsqrt(_rowmean(xv * xv) + EPS)
            gs = _row(v_ref, r_g) * (1.0 + _row(v_ref, r_sc))
            h_sc[...] = (xv * r * gs + _row(v_ref, r_sh)).astype(BF)

        o_ref[...] = jnp.dot(h_sc[...], w_ref[...], preferred_element_type=F32)

    return _pcall(
        body, name=name, grid=(S // ts, nn),
        in_specs=[pl.BlockSpec((ts, D), lambda i, n: (i, 0)), pl.BlockSpec(vecs.shape, lambda i, n: (0, 0)),
                  pl.BlockSpec((D, tn), lambda i, n: (0, n))],
        out_specs=pl.BlockSpec((ts, tn), lambda i, n: (i, n)),
        out_shape=jax.ShapeDtypeStruct((S, N), F32),
        scratch_shapes=[pltpu.VMEM((ts, D), BF)],
        compiler_params=_params(("arbitrary", "arbitrary")),
    )(x, vecs, w)


SEQ_TT = 256
CONV_PAD = 32


def conv_fwd(proj, convw4, conv_b, name):
    S = proj.shape[0]
    M = conv_b.shape[1]
    nb = M // LANES
    tt = min(SEQ_TT, S)

    def body(uv_ref, ug_ref, w_ref, b_ref, cq_ref, qp):
        qp[0:CONV_PAD, :] = jnp.zeros((CONV_PAD, LANES), F32)

        def step(t, carry):
            base = pl.multiple_of(t * tt, tt)
            qp[pl.ds(base + CONV_PAD, tt), :] = uv_ref[pl.ds(base, tt), :] * _sigmoid(ug_ref[pl.ds(base, tt), :])
            acc = jnp.broadcast_to(b_ref[...], (tt, LANES))
            for k in range(CONV_WIDTH):
                acc = acc + w_ref[k:k + 1, :] * qp[pl.ds(base + (CONV_PAD - CONV_WIDTH + 1) + k, tt), :]
            cq_ref[pl.ds(base, tt), :] = acc
            return carry

        lax.fori_loop(0, S // tt, step, 0)

    return _pcall(
        body, name=name, grid=(nb,),
        in_specs=[pl.BlockSpec((S, LANES), lambda c: (0, c)), pl.BlockSpec((S, LANES), lambda c: (0, c + nb)),
                  pl.BlockSpec((None, 32, LANES), lambda c: (c, 0, 0)), pl.BlockSpec((1, LANES), lambda c: (0, c))],
        out_specs=pl.BlockSpec((S, LANES), lambda c: (0, c)),
        out_shape=jax.ShapeDtypeStruct((S, M), F32),
        scratch_shapes=[pltpu.VMEM((S + CONV_PAD, LANES), F32)],
        compiler_params=_params(("arbitrary",)),
    )(proj, proj, convw4, conv_b)


def conv_bwd(dcq, proj, convw4, name):
    S, M = dcq.shape
    nb = M // LANES
    tt = min(SEQ_TT, S)
    off = CONV_PAD - CONV_WIDTH + 1

    def body(dcq_ref, uv_ref, ug_ref, w_ref, duv_ref, dug_ref, dw_ref, db_ref, qp, dp, dw8, db8):
        qp[0:CONV_PAD, :] = jnp.zeros((CONV_PAD, LANES), F32)
        dp[S:S + CONV_PAD, :] = jnp.zeros((CONV_PAD, LANES), F32)
        dw8[...] = jnp.zeros_like(dw8)
        db8[...] = jnp.zeros_like(db8)

        def fill(t, carry):
            base = pl.multiple_of(t * tt, tt)
            qp[pl.ds(base + CONV_PAD, tt), :] = uv_ref[pl.ds(base, tt), :] * _sigmoid(ug_ref[pl.ds(base, tt), :])
            dp[pl.ds(base, tt), :] = dcq_ref[pl.ds(base, tt), :]
            return carry

        lax.fori_loop(0, S // tt, fill, 0)

        def step(t, carry):
            base = pl.multiple_of(t * tt, tt)
            d_t = dcq_ref[pl.ds(base, tt), :]
            db8[...] += d_t.reshape(tt // SUBLANES, SUBLANES, LANES).sum(axis=0)
            dq = jnp.zeros((tt, LANES), F32)
            for k in range(CONV_WIDTH):
                prod = d_t * qp[pl.ds(base + off + k, tt), :]
                dw8[k] += prod.reshape(tt // SUBLANES, SUBLANES, LANES).sum(axis=0)
                dq = dq + w_ref[k:k + 1, :] * dp[pl.ds(base + (CONV_WIDTH - 1) - k, tt), :]
            uv = uv_ref[pl.ds(base, tt), :]
            sg = _sigmoid(ug_ref[pl.ds(base, tt), :])
            duv_ref[pl.ds(base, tt), :] = dq * sg
            dug_ref[pl.ds(base, tt), :] = dq * uv * sg * (1.0 - sg)
            return carry

        lax.fori_loop(0, S // tt, step, 0)
        dw_ref[...] = jnp.zeros_like(dw_ref)
        for k in range(CONV_WIDTH):
            dw_ref[k:k + 1, :] = _colsum(dw8[k])
        db_ref[...] = _colsum(db8[...])

    col = lambda o: pl.BlockSpec((S, LANES), lambda c: (0, c + o))
    return _pcall(
        body, name=name, grid=(nb,),
        in_specs=[col(0), col(0), col(nb), pl.BlockSpec((None, 32, LANES), lambda c: (c, 0, 0))],
        out_specs=[col(0), col(0), pl.BlockSpec((None, 32, LANES), lambda c: (c, 0, 0)),
                   pl.BlockSpec((1, LANES), lambda c: (0, c))],
        out_shape=[jax.ShapeDtypeStruct((S, M), F32), jax.ShapeDtypeStruct((S, M), F32),
                   jax.ShapeDtypeStruct((nb, 32, LANES), F32), jax.ShapeDtypeStruct((1, M), F32)],
        scratch_shapes=[pltpu.VMEM((S + CONV_PAD, LANES), F32), pltpu.VMEM((S + CONV_PAD, LANES), F32),
                        pltpu.VMEM((32, SUBLANES, LANES), F32), pltpu.VMEM((SUBLANES, LANES), F32)],
        compiler_params=_params(("arbitrary",)),
    )(dcq, proj, proj, convw4)


def _log_sigmoid(x):
    return jnp.minimum(x, 0.0) - jnp.log(1.0 + jnp.exp(-jnp.abs(x)))


def _rg_gate_terms(ra, ls):
    la = RG_C * ra * ls
    a = jnp.exp(la)
    th = jnp.tanh(la)
    mult = jnp.sqrt(-2.0 * th / (1.0 - th))
    return a, mult


def rnn_fwd(proj, rnnw4, rnn_b, bda, bdi, b_a, b_i, lam, name):
    S = proj.shape[0]
    M = rnn_b.shape[1]
    nb = M // LANES
    tt = min(SEQ_TT, S)
    KW = RNN_CONV_WIDTH

    def body(ux_ref, w_ref, rb_ref, bda_ref, bdi_ref, ba_ref, bi_ref, lam_ref,
             xr_ref, ra_ref, ii_ref, h_ref, uxp, a_sc, b_sc):
        uxp[0:SUBLANES, :] = jnp.zeros((SUBLANES, LANES), F32)
        ls = _log_sigmoid(lam_ref[...])

        def step(t, carry):
            base = pl.multiple_of(t * tt, tt)
            uxp[pl.ds(base + SUBLANES, tt), :] = ux_ref[pl.ds(base, tt), :]
            xr = jnp.broadcast_to(rb_ref[...], (tt, LANES))
            for k in range(KW):
                xr = xr + w_ref[k:k + 1, :] * uxp[pl.ds(base + (SUBLANES - KW + 1) + k, tt), :]
            xb = xr.astype(BF)
            ra = _sigmoid(jnp.dot(xb, bda_ref[...], preferred_element_type=F32) + ba_ref[...])
            ii = _sigmoid(jnp.dot(xb, bdi_ref[...], preferred_element_type=F32) + bi_ref[...])
            a, mult = _rg_gate_terms(ra, ls)
            xr_ref[pl.ds(base, tt), :] = xr
            ra_ref[pl.ds(base, tt), :] = ra
            ii_ref[pl.ds(base, tt), :] = ii
            a_sc[pl.ds(base, tt), :] = a
            b_sc[pl.ds(base, tt), :] = mult * (ii * xr)
            return carry

        lax.fori_loop(0, S // tt, step, 0)

        rows = lax.broadcasted_iota(I32, (SUBLANES, LANES), 0)

        def scan(t, hprev):
            base = pl.multiple_of(t * SUBLANES, SUBLANES)
            A = a_sc[pl.ds(base, SUBLANES), :]
            B = b_sc[pl.ds(base, SUBLANES), :]
            for d in (1, 2, 4):
                As = jnp.where(rows >= d, pltpu.roll(A, d, axis=0), 1.0)
                Bs = jnp.where(rows >= d, pltpu.roll(B, d, axis=0), 0.0)
                B = A * Bs + B
                A = A * As
            hh = B + A * hprev
            h_ref[pl.ds(base, SUBLANES), :] = hh
            return jnp.broadcast_to(hh[SUBLANES - 1:SUBLANES, :], (SUBLANES, LANES))

        lax.fori_loop(0, S // SUBLANES, scan, jnp.zeros((SUBLANES, LANES), F32))

    col = lambda o: pl.BlockSpec((S, LANES), lambda c: (0, c + o))
    vec = pl.BlockSpec((1, LANES), lambda c: (0, c))
    diag = pl.BlockSpec((LANES, LANES), lambda c: (c, c))
    return _pcall(
        body, name=name, grid=(nb,),
        in_specs=[col(2 * nb), pl.BlockSpec((None, SUBLANES, LANES), lambda c: (c, 0, 0)), vec, diag, diag, vec, vec, vec],
        out_specs=[col(0)] * 4,
        out_shape=[jax.ShapeDtypeStruct((S, M), F32)] * 4,
        scratch_shapes=[pltpu.VMEM((S + SUBLANES, LANES), F32), pltpu.VMEM((S, LANES), F32), pltpu.VMEM((S, LANES), F32)],
        compiler_params=_params(("arbitrary",)),
    )(proj, rnnw4, rnn_b, bda, bdi, b_a, b_i, lam)


def rnn_bwd(dhout, h, xr, ra, ii, proj, rnnw4, bda, bdi, lam, name):
    S, M = h.shape
    nb = M // LANES
    tt = min(SEQ_TT, S)
    KW = RNN_CONV_WIDTH
    SL = SUBLANES

    def body(dh_ref, h_ref, xr_ref, ra_ref, ii_ref, ux_ref, w_ref, bda_ref, bdi_ref, lam_ref,
             dux_ref, dwa_ref, dwi_ref, drw_ref, vec_ref,
             a_sc, hp, g_sc, dpa_sc, dpi_sc, dxp, uxp, acc8, drw8):
        zero8 = jnp.zeros((SL, LANES), F32)
        a_sc[S:S + SL, :] = zero8
        hp[0:SL, :] = zero8
        dxp[S:S + SL, :] = zero8
        uxp[0:SL, :] = zero8
        acc8[...] = jnp.zeros_like(acc8)
        drw8[...] = jnp.zeros_like(drw8)
        lamv = lam_ref[...]
        ls = _log_sigmoid(lamv)

        def fill(t, carry):
            base = pl.multiple_of(t * tt, tt)
            a_sc[pl.ds(base, tt), :] = jnp.exp(RG_C * ra_ref[pl.ds(base, tt), :] * ls)
            hp[pl.ds(base + SL, tt), :] = h_ref[pl.ds(base, tt), :]
            uxp[pl.ds(base + SL, tt), :] = ux_ref[pl.ds(base, tt), :]
            return carry

        lax.fori_loop(0, S // tt, fill, 0)

        rows = lax.broadcasted_iota(I32, (SL, LANES), 0)
        nt8 = S // SL

        def rscan(t, gnext):
            base = pl.multiple_of((nt8 - 1 - t) * SL, SL)
            A = a_sc[pl.ds(base + 1, SL), :]
            B = dh_ref[pl.ds(base, SL), :]
            for d in (1, 2, 4):
                As = jnp.where(rows < SL - d, pltpu.roll(A, SL - d, axis=0), 1.0)
                Bs = jnp.where(rows < SL - d, pltpu.roll(B, SL - d, axis=0), 0.0)
                B = A * Bs + B
                A = A * As
            g = B + A * gnext
            g_sc[pl.ds(base, SL), :] = g
            return jnp.broadcast_to(g[0:1, :], (SL, LANES))

        lax.fori_loop(0, nt8, rscan, zero8)

        def red8(v):
            return v.reshape(tt // SL, SL, LANES).sum(axis=0)

        def step(t, carry):
            base = pl.multiple_of(t * tt, tt)
            g = g_sc[pl.ds(base, tt), :]
            hprev = hp[pl.ds(base + SL - 1, tt), :]
            xr_t = xr_ref[pl.ds(base, tt), :]
            ra_t = ra_ref[pl.ds(base, tt), :]
            ii_t = ii_ref[pl.ds(base, tt), :]
            a, mult = _rg_gate_terms(ra_t, ls)
            gx = g * xr_t
            dmult = gx * ii_t
            dii = gx * mult
            dxr = g * (mult * ii_t)
            dla = g * hprev * a - dmult * (a * a) / mult
            acc8[3] += red8(dla * ra_t)
            dpa = dla * (RG_C * ls) * ra_t * (1.0 - ra_t)
            dpi = dii * ii_t * (1.0 - ii_t)
            dpab = dpa.astype(BF)
            dpib = dpi.astype(BF)
            dxr = dxr + (lax.dot_general(dpab, bda_ref[...], CONTRACT_LAST, preferred_element_type=F32)
                         + lax.dot_general(dpib, bdi_ref[...], CONTRACT_LAST, preferred_element_type=F32))
            dpa_sc[pl.ds(base, tt), :] = dpab
            dpi_sc[pl.ds(base, tt), :] = dpib
            dxp[pl.ds(base, tt), :] = dxr
            acc8[0] += red8(dxr)
            acc8[1] += red8(dpa)
            acc8[2] += red8(dpi)
            return carry

        lax.fori_loop(0, S // tt, step, 0)

        def convb(t, carry):
            base = pl.multiple_of(t * tt, tt)
            d_t = dxp[pl.ds(base, tt), :]
            dux = jnp.zeros((tt, LANES), F32)
            for k in range(KW):
                drw8[k] += red8(d_t * uxp[pl.ds(base + (SL - KW + 1) + k, tt), :])
                dux = dux + w_ref[k:k + 1, :] * dxp[pl.ds(base + (KW - 1) - k, tt), :]
            dux_ref[pl.ds(base, tt), :] = dux
            return carry

        lax.fori_loop(0, S // tt, convb, 0)

        xb = xr_ref[...].astype(BF)
        dwa_ref[...] = lax.dot_general(xb, dpa_sc[...], CONTRACT_FIRST, preferred_element_type=F32)
        dwi_ref[...] = lax.dot_general(xb, dpi_sc[...], CONTRACT_FIRST, preferred_element_type=F32)
        drw_ref[...] = jnp.zeros_like(drw_ref)
        vec_ref[...] = jnp.zeros_like(vec_ref)
        for k in range(KW):
            drw_ref[k:k + 1, :] = _colsum(drw8[k])
        for k in range(3):
            vec_ref[k:k + 1, :] = _colsum(acc8[k])
        vec_ref[3:4, :] = _colsum(acc8[3]) * (RG_C * _sigmoid(-lamv))

    col = lambda o: pl.BlockSpec((S, LANES), lambda c: (0, c + o))
    vec = pl.BlockSpec((1, LANES), lambda c: (0, c))
    diag = pl.BlockSpec((LANES, LANES), lambda c: (c, c))
    blk3 = lambda r: pl.BlockSpec((None, r, LANES), lambda c: (c, 0, 0))
    return _pcall(
        body, name=name, grid=(nb,),
        in_specs=[col(0), col(0), col(0), col(0), col(0), col(2 * nb), blk3(SL), diag, diag, vec],
        out_specs=[col(0), blk3(LANES), blk3(LANES), blk3(SL), pl.BlockSpec((SL, LANES), lambda c: (0, c))],
        out_shape=[jax.ShapeDtypeStruct((S, M), F32), jax.ShapeDtypeStruct((nb, LANES, LANES), F32),
                   jax.ShapeDtypeStruct((nb, LANES, LANES), F32), jax.ShapeDtypeStruct((nb, SL, LANES), F32),
                   jax.ShapeDtypeStruct((SL, M), F32)],
        scratch_shapes=[pltpu.VMEM((S + SL, LANES), F32), pltpu.VMEM((S + SL, LANES), F32), pltpu.VMEM((S, LANES), F32),
                        pltpu.VMEM((S, LANES), BF), pltpu.VMEM((S, LANES), BF), pltpu.VMEM((S + SL, LANES), F32),
                        pltpu.VMEM((S + SL, LANES), F32), pltpu.VMEM((SL, SL, LANES), F32), pltpu.VMEM((SL, SL, LANES), F32)],
        compiler_params=_params(("arbitrary",)),
    )(dhout, h, xr, ra, ii, proj, rnnw4, bda, bdi, lam)


GELU_K = 0.7978845608028654
GELU_C = 0.044715


def _layernorm_parts(cq):
    mu = _rowmean(cq)
    d = cq - mu
    rstd = lax.rsqrt(_rowmean(d * d) + EPS)
    return d * rstd, rstd


def mix_out(cq, proj, h, x, vecs, lnv, wout, name):
    S, D = x.shape
    M = cq.shape[1]
    ts = min(512, S)

    def body(cq_ref, uy_ref, h_ref, x_ref, v_ref, ln_ref, w_ref, xo_ref, ym_ref, yc_ref):
        z, _ = _layernorm_parts(cq_ref[...])
        l = z * _row(ln_ref, 0) + _row(ln_ref, 1)
        yc_ref[:, 0:M] = (l * _sigmoid(l)).astype(BF)
        uy = uy_ref[...]
        gelu = 0.5 * uy * (1.0 + jnp.tanh(GELU_K * (uy + GELU_C * uy * uy * uy)))
        yc_ref[:, M:2 * M] = (gelu * h_ref[...]).astype(BF)
        ym = jnp.dot(yc_ref[...], w_ref[...], preferred_element_type=F32)
        ym_ref[...] = ym
        xo_ref[...] = x_ref[...] + _row(v_ref, R_GT2) * ym

    tok = pl.BlockSpec((ts, D), lambda i: (i, 0))
    mtok = lambda o: pl.BlockSpec((ts, M), lambda i: (i, o))
    return _pcall(
        body, name=name, grid=(S // ts,),
        in_specs=[mtok(0), mtok(3), mtok(0), tok, pl.BlockSpec(vecs.shape, lambda i: (0, 0)),
                  pl.BlockSpec(lnv.shape, lambda i: (0, 0)), pl.BlockSpec(wout.shape, lambda i: (0, 0))],
        out_specs=[tok, tok, pl.BlockSpec((ts, 2 * M), lambda i: (i, 0))],
        out_shape=[jax.ShapeDtypeStruct((S, D), F32), jax.ShapeDtypeStruct((S, D), F32),
                   jax.ShapeDtypeStruct((S, 2 * M), BF)],
        compiler_params=_params(("arbitrary",)),
    )(cq, proj, h, x, vecs, lnv, wout)


def mix_out_bwd(dxo, ym, vecs, wout, cq, lnv, proj, h, name):
    S, D = dxo.shape
    M = cq.shape[1]
    ts = min(512, S)

    def body(dxo_ref, ym_ref, v_ref, w_ref, cq_ref, ln_ref, uy_ref, h_ref,
             dcq_ref, dh_ref, duy_ref, dyb_ref, vgd_ref, vgm_ref):
        @pl.when(pl.program_id(0) == 0)
        def _():
            vgd_ref[...] = jnp.zeros_like(vgd_ref)
            vgm_ref[...] = jnp.zeros_like(vgm_ref)

        dxo_v = dxo_ref[...]
        dyb = (_row(v_ref, R_GT2) * dxo_v).astype(BF)
        dyb_ref[...] = dyb
        vgd_ref[0:1, :] += _colsum(dxo_v * ym_ref[...])
        dycat = lax.dot_general(dyb, w_ref[...], CONTRACT_LAST, preferred_element_type=F32)
        dyc = dycat[:, 0:M]
        dyr = dycat[:, M:2 * M]
        z, rstd = _layernorm_parts(cq_ref[...])
        lng = _row(ln_ref, 0)
        l = z * lng + _row(ln_ref, 1)
        sl = _sigmoid(l)
        dl = dyc * (sl * (1.0 + l * (1.0 - sl)))
        vgm_ref[0:1, :] += _colsum(dl * z)
        vgm_ref[1:2, :] += _colsum(dl)
        dz = dl * lng
        dcq_ref[...] = rstd * (dz - _rowmean(dz) - z * _rowmean(dz * z))
        uy = uy_ref[...]
        u2 = uy * uy
        th = jnp.tanh(GELU_K * (uy + GELU_C * uy * u2))
        gelu = 0.5 * uy * (1.0 + th)
        dgelu = 0.5 * (1.0 + th) + 0.5 * uy * (1.0 - th * th) * (GELU_K * (1.0 + 3.0 * GELU_C * u2))
        dh_ref[...] = dyr * gelu
        duy_ref[...] = dyr * h_ref[...] * dgelu

    tok = pl.BlockSpec((ts, D), lambda i: (i, 0))
    mtok = lambda o: pl.BlockSpec((ts, M), lambda i: (i, o))
    return _pcall(
        body, name=name, grid=(S // ts,),
        in_specs=[tok, tok, pl.BlockSpec(vecs.shape, lambda i: (0, 0)), pl.BlockSpec(wout.shape, lambda i: (0, 0)),
                  mtok(0), pl.BlockSpec(lnv.shape, lambda i: (0, 0)), mtok(3), mtok(0)],
        out_specs=[mtok(0), mtok(0), mtok(0), tok, pl.BlockSpec((SUBLANES, D), lambda i: (0, 0)),
                   pl.BlockSpec((SUBLANES, M), lambda i: (0, 0))],
        out_shape=[jax.ShapeDtypeStruct((S, M), F32)] * 3 + [jax.ShapeDtypeStruct((S, D), BF),
                   jax.ShapeDtypeStruct((SUBLANES, D), F32), jax.ShapeDtypeStruct((SUBLANES, M), F32)],
        compiler_params=_params(("arbitrary",)),
    )(dxo, ym, vecs, wout, cq, lnv, proj, h)


def mix_in_bwd(dparts, x, dxo, vecs, win, name):
    S, D = x.shape
    M = dparts[0].shape[1]
    ts = min(512, S)

    def body(d0, d1, d2, d3, x_ref, dxo_ref, v_ref, w_ref, dx_ref, hb_ref, dp_ref, vg_ref):
        @pl.when(pl.program_id(0) == 0)
        def _():
            vg_ref[...] = jnp.zeros_like(vg_ref)

        for q, dref in enumerate((d0, d1, d2, d3)):
            dp_ref[:, q * M:(q + 1) * M] = dref[...].astype(BF)
        dh = lax.dot_general(dp_ref[...], w_ref[...], CONTRACT_LAST, preferred_element_type=F32)
        xv = x_ref[...]
        r = lax.rsqrt(_rowmean(xv * xv) + EPS)
        n = xv * r
        g = _row(v_ref, R_G2)
        sc1 = 1.0 + _row(v_ref, R_SC2)
        gsc = g * sc1
        hb_ref[...] = (n * gsc + _row(v_ref, R_SH2)).astype(BF)
        dhn = dh * n
        vg_ref[0:1, :] += _colsum(dh)
        vg_ref[1:2, :] += _colsum(dhn) * g
        vg_ref[2:3, :] += _colsum(dhn) * sc1
        dn = dh * gsc
        dx_ref[...] = dxo_ref[...] + r * (dn - n * _rowmean(dn * n))

    tok = pl.BlockSpec((ts, D), lambda i: (i, 0))
    mtok = pl.BlockSpec((ts, M), lambda i: (i, 0))
    return _pcall(
        body, name=name, grid=(S // ts,),
        in_specs=[mtok] * 4 + [tok, tok, pl.BlockSpec(vecs.shape, lambda i: (0, 0)), pl.BlockSpec(win.shape, lambda i: (0, 0))],
        out_specs=[tok, tok, pl.BlockSpec((ts, 4 * M), lambda i: (i, 0)), pl.BlockSpec((SUBLANES, D), lambda i: (0, 0))],
        out_shape=[jax.ShapeDtypeStruct((S, D), F32), jax.ShapeDtypeStruct((S, D), BF),
                   jax.ShapeDtypeStruct((S, 4 * M), BF), jax.ShapeDtypeStruct((SUBLANES, D), F32)],
        compiler_params=_params(("arbitrary",)),
    )(*dparts, x, dxo, vecs, win)


def _adamw(w, g, m, v):
    m = ADAM_B1 * m + (1.0 - ADAM_B1) * g
    v = ADAM_B2 * v + (1.0 - ADAM_B2) * (g * g)
    m_hat = m / (1.0 - ADAM_B1 ** ADAM_STEP)
    v_hat = v / (1.0 - ADAM_B2 ** ADAM_STEP)
    delta = -ADAM_LR * (m_hat / (jnp.sqrt(v_hat) + ADAM_EPS) + ADAM_WD * w)
    return delta, m, v


def adam_big(w, g, m, v, name):
    R, C = w.shape
    tr = 256 if R % 256 == 0 else R // 2 if (R // 2) % SUBLANES == 0 and R > 512 else R
    tc = C if C <= 1536 else (1152 if C % 1152 == 0 else 1024)
    assert R % tr == 0 and C % tc == 0

    def body(w_ref, g_ref, m_ref, v_ref, d_ref, nm_ref, nv_ref):
        d, nm, nv = _adamw(w_ref[...], g_ref[...], m_ref[...], v_ref[...])
        d_ref[...] = d
        nm_ref[...] = nm
        nv_ref[...] = nv

    blk = pl.BlockSpec((tr, tc), lambda i, j: (i, j))
    return _pcall(
        body, name=name, grid=(R // tr, C // tc), in_specs=[blk] * 4, out_specs=[blk] * 3,
        out_shape=[jax.ShapeDtypeStruct((R, C), F32)] * 3, compiler_params=_params(("parallel", "parallel")),
    )(w, g, m, v)


def adam_cond(c_all, dmod, w, m, v, name):
    B, Kin = c_all.shape
    N = w.shape[1]
    tn = 256
    assert N % tn == 0

    def body(c_ref, d_ref, w_ref, m_ref, v_ref, g_ref, dl_ref, nm_ref, nv_ref):
        cv = c_ref[...]
        ca = cv * _sigmoid(cv)
        g = lax.dot_general(ca, d_ref[...], CONTRACT_FIRST, preferred_element_type=F32, precision=lax.Precision.HIGHEST)
        d, nm, nv = _adamw(w_ref[...], g, m_ref[...], v_ref[...])
        g_ref[...] = g
        dl_ref[...] = d
        nm_ref[...] = nm
        nv_ref[...] = nv

    blk = pl.BlockSpec((Kin, tn), lambda n: (0, n))
    return _pcall(
        body, name=name, grid=(N // tn,),
        in_specs=[pl.BlockSpec((B, Kin), lambda n: (0, 0)), pl.BlockSpec((B, tn), lambda n: (0, n)), blk, blk, blk],
        out_specs=[blk] * 4, out_shape=[jax.ShapeDtypeStruct((Kin, N), F32)] * 4,
        compiler_params=_params(("parallel",)),
    )(c_all, dmod, w, m, v)


def adam_small(ws, gs, ms, vs, name):
    n = len(ws)

    def body(*refs):
        ins, outs = refs[:4 * n], refs[4 * n:]
        for k in range(n):
            d, nm, nv = _adamw(ins[k][...], ins[n + k][...], ins[2 * n + k][...], ins[3 * n + k][...])
            outs[k][...] = d
            outs[n + k][...] = nm
            outs[2 * n + k][...] = nv

    specs = [pl.BlockSpec(w.shape, lambda i: (0, 0)) for w in ws]
    shapes = [jax.ShapeDtypeStruct(w.shape, F32) for w in ws]
    out = _pcall(body, name=name, grid=(1,), in_specs=specs * 4, out_specs=specs * 3, out_shape=shapes * 3,
                 compiler_params=_params(("arbitrary",)))(*ws, *gs, *ms, *vs)
    return out[:n], out[n:2 * n], out[2 * n:]


def _me():
    return lax.axis_index("x"), lax.axis_index("y"), lax.axis_index("c")


def _flip(x, y, p):
    return (x ^ (p >> 1) if (p >> 1) else x), (y ^ (p & 1) if (p & 1) else y)


def _handshake(peers):
    barrier = pltpu.get_barrier_semaphore()
    for peer in peers:
        pl.semaphore_signal(barrier, inc=1, device_id=peer, device_id_type=MESH)
    pl.semaphore_wait(barrier, len(peers))


def _seq_call(body, *, name, n_in, out_shape, sem_shapes, collective_id):
    del n_in
    return pl.kernel(body, out_type=out_shape, mesh=plsc.ScalarSubcoreMesh(axis_name="sq", num_cores=1), name=name,
                     scratch_types=sem_shapes, compiler_params=pltpu.CompilerParams(collective_id=collective_id))


def _hbm_comm_call(body, *, name, n_in, out_shape, sem_shapes, seq_id):
    if seq_id is not None:
        return _seq_call(body, name=name, n_in=n_in, out_shape=out_shape, sem_shapes=sem_shapes, collective_id=seq_id)
    anyspec = pl.BlockSpec(memory_space=pl.ANY)
    return _pcall(body, name=name, in_specs=[anyspec] * n_in, out_specs=[anyspec] * len(out_shape), out_shape=out_shape,
                  scratch_shapes=sem_shapes, compiler_params=_params())


def allgather_devices(v, name, with_sum=False):
    R, L = v.shape

    def body(v_ref, out_ref, *rest):
        if with_sum:
            sum_ref, send_sems, recv_sems = rest
        else:
            send_sems, recv_sems = rest
        x, y, c = _me()
        me = 4 * x + 2 * y + c
        out_ref[me] = v_ref[...]
        copies = []
        for p in range(1, N_DEV):
            px, py = _flip(x, y, p >> 1)
            pc = (1 - c) if (p & 1) else c
            peer = 4 * px + 2 * py + pc
            send = pltpu.make_async_remote_copy(src_ref=v_ref, dst_ref=out_ref.at[me], send_sem=send_sems.at[p - 1],
                                                recv_sem=recv_sems.at[p - 1], device_id=(px, py, pc), device_id_type=MESH)
            send.start()
            recv = pltpu.make_async_remote_copy(src_ref=v_ref, dst_ref=out_ref.at[peer], send_sem=send_sems.at[p - 1],
                                                recv_sem=recv_sems.at[p - 1], device_id=(px, py, pc), device_id_type=MESH)
            copies.append((send, recv))
        for send, recv in copies:
            recv.wait_recv()
        for send, recv in copies:
            send.wait_send()
        if with_sum:
            s = out_ref[0]
            for k in range(1, N_DEV):
                s = s + out_ref[k]
            sum_ref[...] = s

    vm = pl.BlockSpec(memory_space=pltpu.VMEM)
    out_shape = [jax.ShapeDtypeStruct((N_DEV, R, L), F32)]
    if with_sum:
        out_shape.append(jax.ShapeDtypeStruct((R, L), F32))
    return _pcall(
        body, name=name, in_specs=[vm], out_specs=[vm] * len(out_shape), out_shape=out_shape,
        scratch_shapes=[pltpu.SemaphoreType.DMA((N_DEV - 1,)), pltpu.SemaphoreType.DMA((N_DEV - 1,))],
        compiler_params=_params(),
    )(v)


def allgather_devices_hbm(v, name, seq_id):
    R, L = v.shape

    def body(v_ref, out_ref, send_sems, recv_sems, local_sem):
        x, y, c = _me()
        me = 4 * x + 2 * y + c
        peers = []
        for p in range(1, N_DEV):
            px, py = _flip(x, y, p >> 1)
            peers.append((px, py, (1 - c) if (p & 1) else c))
        _handshake(peers)
        lc = pltpu.make_async_copy(v_ref, out_ref.at[me], local_sem)
        lc.start()
        copies = []
        for p, (px, py, pc) in enumerate(peers):
            send = pltpu.make_async_remote_copy(src_ref=v_ref, dst_ref=out_ref.at[me], send_sem=send_sems.at[p],
                                                recv_sem=recv_sems.at[p], device_id=(px, py, pc), device_id_type=MESH)
            send.start()
            recv = pltpu.make_async_remote_copy(src_ref=v_ref, dst_ref=out_ref.at[4 * px + 2 * py + pc], send_sem=send_sems.at[p],
                                                recv_sem=recv_sems.at[p], device_id=(px, py, pc), device_id_type=MESH)
            copies.append((send, recv))
        for send, recv in copies:
            recv.wait_recv()
        for send, recv in copies:
            send.wait_send()
        lc.wait()

    return _seq_call(body, name=name, n_in=1, out_shape=[jax.ShapeDtypeStruct((N_DEV, R, L), F32)],
                     sem_shapes=[pltpu.SemaphoreType.DMA((N_DEV - 1,)), pltpu.SemaphoreType.DMA((N_DEV - 1,)),
                                 pltpu.SemaphoreType.DMA], collective_id=seq_id)(v)[0]


def sum_slots(g, name):
    n, R, L = g.shape
    tr = 216 if R % 216 == 0 else R
    assert R % tr == 0 and tr % SUBLANES == 0

    def body(g_ref, o_ref):
        s = g_ref[0]
        for k in range(1, n):
            s = s + g_ref[k]
        o_ref[...] = s

    return _pcall(body, name=name, grid=(R // tr,), in_specs=[pl.BlockSpec((n, tr, L), lambda i: (0, i, 0))],
                  out_specs=pl.BlockSpec((tr, L), lambda i: (i, 0)), out_shape=jax.ShapeDtypeStruct((R, L), F32),
                  compiler_params=_params(("parallel",)))(g)


def allgather_chips(v, name):
    R, L = v.shape

    def body(v_ref, out_ref, send_sems, recv_sems):
        x, y, c = _me()
        chip = 2 * x + y
        out_ref[chip] = v_ref[...]
        copies = []
        for p in range(1, N_CHIPS):
            px, py = _flip(x, y, p)
            send = pltpu.make_async_remote_copy(src_ref=v_ref, dst_ref=out_ref.at[chip], send_sem=send_sems.at[p - 1],
                                                recv_sem=recv_sems.at[p - 1], device_id=(px, py, c), device_id_type=MESH)
            send.start()
            recv = pltpu.make_async_remote_copy(src_ref=v_ref, dst_ref=out_ref.at[2 * px + py], send_sem=send_sems.at[p - 1],
                                                recv_sem=recv_sems.at[p - 1], device_id=(px, py, c), device_id_type=MESH)
            copies.append((send, recv))
        for send, recv in copies:
            recv.wait_recv()
        for send, recv in copies:
            send.wait_send()

    vm = pl.BlockSpec(memory_space=pltpu.VMEM)
    return _pcall(
        body, name=name, in_specs=[vm], out_specs=vm, out_shape=jax.ShapeDtypeStruct((N_CHIPS, R, L), F32),
        scratch_shapes=[pltpu.SemaphoreType.DMA((N_CHIPS - 1,)), pltpu.SemaphoreType.DMA((N_CHIPS - 1,))],
        compiler_params=_params(),
    )(v)


def _shard_window(ref, kind, shard_shape, chip, half):
    r, c = shard_shape
    hr = r // 2
    if kind == "col":
        return ref.at[pl.ds(pl.multiple_of(half * hr, hr), hr), pl.ds(pl.multiple_of(chip * c, c), c)]
    return ref.at[pl.ds(pl.multiple_of(chip * r + half * hr, hr), hr), :]


def allgather_weights(shards, kinds, name, seq_id=None):
    n = len(shards)
    fulls = []
    for s, kind in zip(shards, kinds):
        r, c = s.shape
        fulls.append(jax.ShapeDtypeStruct((r, N_CHIPS * c) if kind == "col" else (N_CHIPS * r, c), s.dtype))

    def body(*refs):
        srcs, outs = refs[:n], refs[n:2 * n]
        send_sems, recv_sems, fsend_sems, frecv_sems = refs[2 * n:]
        x, y, c = _me()
        chip = 2 * x + y
        sib = (x, y, 1 - c)
        if seq_id is not None:
            _handshake([(*_flip(x, y, p), c) for p in range(1, N_CHIPS)] + [sib])
        sends, fwds = [], []
        for i in range(n):
            shp = srcs[i].shape
            hr = shp[0] // 2
            my_half = srcs[i].at[pl.ds(pl.multiple_of(c * hr, hr), hr), :]
            for p in range(1, N_CHIPS):
                px, py = _flip(x, y, p)
                k = i * (N_CHIPS - 1) + p - 1
                cp = pltpu.make_async_remote_copy(src_ref=my_half, dst_ref=_shard_window(outs[i], kinds[i], shp, chip, c),
                                                  send_sem=send_sems.at[k], recv_sem=recv_sems.at[k],
                                                  device_id=(px, py, c), device_id_type=MESH)
                cp.start()
                sends.append(cp)
        for i in range(n):
            shp = srcs[i].shape
            for p in range(1, N_CHIPS):
                px, py = _flip(x, y, p)
                k = i * (N_CHIPS - 1) + p - 1
                landed = _shard_window(outs[i], kinds[i], shp, 2 * px + py, c)
                pltpu.make_async_remote_copy(src_ref=landed, dst_ref=landed, send_sem=send_sems.at[k], recv_sem=recv_sems.at[k],
                                             device_id=(px, py, c), device_id_type=MESH).wait_recv()
                fw = pltpu.make_async_remote_copy(src_ref=landed, dst_ref=landed, send_sem=fsend_sems.at[k],
                                                  recv_sem=frecv_sems.at[k], device_id=sib, device_id_type=MESH)
                fw.start()
                fwds.append(fw)
        for i in range(n):
            shp = srcs[i].shape
            for p in range(1, N_CHIPS):
                px, py = _flip(x, y, p)
                k = i * (N_CHIPS - 1) + p - 1
                other = _shard_window(outs[i], kinds[i], shp, 2 * px + py, 1 - c)
                pltpu.make_async_remote_copy(src_ref=other, dst_ref=other, send_sem=fsend_sems.at[k], recv_sem=frecv_sems.at[k],
                                             device_id=sib, device_id_type=MESH).wait_recv()
        for cp in sends + fwds:
            cp.wait_send()

    nk = n * (N_CHIPS - 1)
    gathered = _hbm_comm_call(
        body, name=name, n_in=n, out_shape=fulls, seq_id=seq_id,
        sem_shapes=[pltpu.SemaphoreType.DMA((nk,)), pltpu.SemaphoreType.DMA((nk,)), pltpu.SemaphoreType.DMA((nk,)),
                    pltpu.SemaphoreType.DMA((nk,))],
    )(*shards)
    return place_local_shards(gathered, shards, kinds, name + "_local")


def place_local_shards(fulls, shards, kinds, name):
    n = len(shards)
    chip = jnp.reshape(2 * lax.axis_index("x") + lax.axis_index("y"), (1,)).astype(I32)

    def body(ci_ref, *refs):
        for i in range(n):
            refs[2 * n + i][...] = refs[i][...]

    in_specs = [pl.BlockSpec(s.shape, lambda i, ci: (0, 0)) for s in shards] + [pl.BlockSpec(memory_space=pl.ANY)] * n
    out_specs = [pl.BlockSpec(s.shape, (lambda i, ci: (0, ci[0])) if k == "col" else (lambda i, ci: (ci[0], 0)))
                 for s, k in zip(shards, kinds)]
    gs = pltpu.PrefetchScalarGridSpec(num_scalar_prefetch=1, grid=(1,), in_specs=in_specs, out_specs=out_specs)
    return _pcall(body, name=name, grid_spec=gs, out_shape=[jax.ShapeDtypeStruct(f.shape, f.dtype) for f in fulls],
                  input_output_aliases={1 + n + i: i for i in range(n)}, compiler_params=_params(("arbitrary",)))(chip, *shards, *fulls)


def _as_halves(g, kind, shard_shape):
    r, c = shard_shape
    if kind == "col":
        return g.reshape(2, r // 2, N_CHIPS * c)
    return g.reshape(N_CHIPS, 2, r // 2, c)


def exchange_sibling_halves(grads, kinds, shard_shapes, name, seq_id=None):
    n = len(grads)
    views = [_as_halves(g, k, s) for g, k, s in zip(grads, kinds, shard_shapes)]
    outs = []
    for k, (r, c) in zip(kinds, shard_shapes):
        outs.append(jax.ShapeDtypeStruct((r // 2, N_CHIPS * c) if k == "col" else (N_CHIPS, r // 2, c), F32))

    def body(*refs):
        srcs, dsts = refs[:n], refs[n:2 * n]
        send_sems, recv_sems = refs[2 * n:]
        x, y, c = _me()
        if seq_id is not None:
            _handshake([(x, y, 1 - c)])
        cps = []
        for i in range(n):
            src = srcs[i].at[1 - c] if kinds[i] == "col" else srcs[i].at[:, 1 - c]
            cp = pltpu.make_async_remote_copy(src_ref=src, dst_ref=dsts[i], send_sem=send_sems.at[i], recv_sem=recv_sems.at[i],
                                              device_id=(x, y, 1 - c), device_id_type=MESH)
            cp.start()
            cps.append(cp)
        for cp in cps:
            cp.wait_recv()
        for cp in cps:
            cp.wait_send()

    return _hbm_comm_call(body, name=name, n_in=n, out_shape=outs, seq_id=seq_id,
                          sem_shapes=[pltpu.SemaphoreType.DMA((n,)), pltpu.SemaphoreType.DMA((n,))])(*views)


def add_sibling_half(g, recv, kind, shard_shape, cidx, name):
    r, c = shard_shape
    hr = r // 2
    gv = _as_halves(g, kind, shard_shape)
    tr = hr if hr <= 512 else (256 if hr % 256 == 0 else hr // 2)
    assert hr % tr == 0

    def body(ci_ref, g_ref, r_ref, h_ref, hb_ref):
        s = g_ref[...] + r_ref[...]
        h_ref[...] = s
        hb_ref[...] = s.astype(BF)

    if kind == "col":
        grid = (hr // tr, N_CHIPS)
        g_spec = pl.BlockSpec((None, tr, c), lambda i, k, ci: (ci[0], i, k))
        o_spec = pl.BlockSpec((tr, c), lambda i, k, ci: (i, k))
    else:
        grid = (hr // tr, N_CHIPS)
        g_spec = pl.BlockSpec((None, None, tr, c), lambda i, k, ci: (k, ci[0], i, 0))
        o_spec = pl.BlockSpec((None, tr, c), lambda i, k, ci: (k, i, 0))
    gs = pltpu.PrefetchScalarGridSpec(num_scalar_prefetch=1, grid=grid, in_specs=[g_spec, o_spec], out_specs=[o_spec, o_spec])
    return _pcall(
        body, name=name, grid_spec=gs,
        out_shape=[jax.ShapeDtypeStruct(recv.shape, F32), jax.ShapeDtypeStruct(recv.shape, BF)],
        compiler_params=_params(("parallel", "parallel")),
    )(cidx, gv, recv)


def exchange_chip_pieces(hbs, kinds, shard_shapes, name, seq_id=None):
    n = len(hbs)
    outs = [jax.ShapeDtypeStruct((N_CHIPS - 1, r // 2, c), BF) for (r, c) in shard_shapes]

    def body(*refs):
        srcs, dsts = refs[:n], refs[n:2 * n]
        send_sems, recv_sems = refs[2 * n:]
        x, y, c = _me()
        if seq_id is not None:
            _handshake([(*_flip(x, y, p), c) for p in range(1, N_CHIPS)])
        cps = []
        for i in range(n):
            cc = shard_shapes[i][1]
            for p in range(1, N_CHIPS):
                px, py = _flip(x, y, p)
                pchip = 2 * px + py
                src = (srcs[i].at[:, pl.ds(pl.multiple_of(pchip * cc, cc), cc)] if kinds[i] == "col" else srcs[i].at[pchip])
                k = i * (N_CHIPS - 1) + p - 1
                cp = pltpu.make_async_remote_copy(src_ref=src, dst_ref=dsts[i].at[p - 1], send_sem=send_sems.at[k],
                                                  recv_sem=recv_sems.at[k], device_id=(px, py, c), device_id_type=MESH)
                cp.start()
                cps.append(cp)
        for cp in cps:
            cp.wait_recv()
        for cp in cps:
            cp.wait_send()

    nk = n * (N_CHIPS - 1)
    return _hbm_comm_call(body, name=name, n_in=n, out_shape=outs, seq_id=seq_id,
                          sem_shapes=[pltpu.SemaphoreType.DMA((nk,)), pltpu.SemaphoreType.DMA((nk,))])(*hbs)


def sum_chip_pieces(h, pieces, kind, shard_shape, chip_core, name):
    r, c = shard_shape
    hr = r // 2
    tr = hr if hr <= 512 else (256 if hr % 256 == 0 else hr // 2)
    assert hr % tr == 0
    nrb = hr // tr

    def body(ci_ref, h_ref, p_ref, q_ref):
        q_ref[...] = ((h_ref[...] + p_ref[0].astype(F32)) + p_ref[1].astype(F32)) + p_ref[2].astype(F32)

    if kind == "col":
        h_spec = pl.BlockSpec((tr, c), lambda i, ci: (i, ci[0]))
    else:
        h_spec = pl.BlockSpec((None, tr, c), lambda i, ci: (ci[0], i, 0))
    gs = pltpu.PrefetchScalarGridSpec(
        num_scalar_prefetch=1, grid=(nrb,),
        in_specs=[h_spec, pl.BlockSpec((N_CHIPS - 1, tr, c), lambda i, ci: (0, i, 0))],
        out_specs=pl.BlockSpec((tr, c), lambda i, ci: (ci[1] * nrb + i, 0)))
    return _pcall(body, name=name, grid_spec=gs, out_shape=jax.ShapeDtypeStruct((r, c), F32),
                  compiler_params=_params(("parallel",)))(chip_core, h, pieces)


def exchange_reduced_halves(qs, name):
    n = len(qs)

    def body(*refs):
        bufs = refs[n:2 * n]
        send_sems, recv_sems = refs[2 * n:]
        x, y, c = _me()
        cps = []
        for i in range(n):
            hr = bufs[i].shape[0] // 2
            mine = bufs[i].at[pl.ds(pl.multiple_of(c * hr, hr), hr), :]
            other = bufs[i].at[pl.ds(pl.multiple_of((1 - c) * hr, hr), hr), :]
            cp = pltpu.make_async_remote_copy(src_ref=mine, dst_ref=mine, send_sem=send_sems.at[i], recv_sem=recv_sems.at[i],
                                              device_id=(x, y, 1 - c), device_id_type=MESH)
            cp.start()
            cps.append((cp, pltpu.make_async_remote_copy(src_ref=other, dst_ref=other, send_sem=send_sems.at[i],
                                                         recv_sem=recv_sems.at[i], device_id=(x, y, 1 - c), device_id_type=MESH)))
        for cp, rv in cps:
            rv.wait_recv()
        for cp, rv in cps:
            cp.wait_send()

    anyspec = pl.BlockSpec(memory_space=pl.ANY)
    return _pcall(
        body, name=name, in_specs=[anyspec] * n, out_specs=[anyspec] * n,
        out_shape=[jax.ShapeDtypeStruct(q.shape, F32) for q in qs], input_output_aliases={i: i for i in range(n)},
        scratch_shapes=[pltpu.SemaphoreType.DMA((n,)), pltpu.SemaphoreType.DMA((n,))],
        compiler_params=_params(),
    )(*qs)


def _rows128(a):
    return a.reshape(-1, LANES)


def _after(xs, *deps):
    flat = []
    for d in deps:
        flat.extend(d if isinstance(d, (list, tuple)) else [d])
    return list(lax.optimization_barrier((tuple(xs), tuple(flat)))[0])


def _block_diag(w):
    H, d, _ = w.shape
    eye = jnp.eye(H, dtype=w.dtype)
    return jnp.einsum("hde,hg->hdge", w, eye).reshape(H * d, H * d)


def _diag_blocks(g4, H, d):
    nb = g4.shape[0]
    per = LANES // d
    g = g4.reshape(nb, per, d, per, d)
    return jnp.stack([g[:, j, :, j, :] for j in range(per)], axis=1).reshape(H, d, d)


def kernel(x, c, w_mod, b_mod, g_ffn1, w_ffn1_in, w_ffn1_out, g_mix, w_in, conv_w, conv_b, ln_g, ln_b, rnn_conv_w, rnn_conv_b, w_a, b_a, w_i, b_i, lru_lambda, w_out, g_ffn2, w_ffn2_in, w_ffn2_out, w_fmod, b_fmod, g_final, loss_target, m_w_mod, m_b_mod, m_g_ffn1, m_w_ffn1_in, m_w_ffn1_out, m_g_mix, m_w_in, m_conv_w, m_conv_b, m_ln_g, m_ln_b, m_rnn_conv_w, m_rnn_conv_b, m_w_a, m_b_a, m_w_i, m_b_i, m_lru_lambda, m_w_out, m_g_ffn2, m_w_ffn2_in, m_w_ffn2_out, m_w_fmod, m_b_fmod, m_g_final, v_w_mod, v_b_mod, v_g_ffn1, v_w_ffn1_in, v_w_ffn1_out, v_g_mix, v_w_in, v_conv_w, v_conv_b, v_ln_g, v_ln_b, v_rnn_conv_w, v_rnn_conv_b, v_w_a, v_b_a, v_w_i, v_b_i, v_lru_lambda, v_w_out, v_g_ffn2, v_w_ffn2_in, v_w_ffn2_out, v_w_fmod, v_b_fmod, v_g_final):
    S, D = x.shape[1], x.shape[2]
    M = conv_b.shape[1]
    H, HD = w_a.shape[1], w_a.shape[2]
    nb = M // LANES
    ix, iy, ic = lax.axis_index("x"), lax.axis_index("y"), lax.axis_index("c")
    chip = 2 * ix + iy
    dev = 2 * chip + ic
    cidx = jnp.reshape(ic, (1,)).astype(I32)
    chip_core = jnp.stack([chip, ic]).astype(I32)
    xs = x[0]
    tgt = loss_target[0]

    kinds = ["col", "row"]
    w_f1, w_mx, w_f2 = [w_ffn1_in[0], w_ffn1_out[0]], [w_in[0], w_out[0]], [w_ffn2_in[0], w_ffn2_out[0]]
    as_bf = lambda ws: [w.astype(BF) for w in ws]
    shapes_of = lambda ws: [w.shape for w in ws]
    wi1, wo1 = allgather_weights(as_bf(w_f1), kinds, "gather_ffn1", seq_id=9)
    win, wout = allgather_weights(as_bf(w_mx), kinds, "gather_mix", seq_id=1)
    wi2, wo2 = allgather_weights(as_bf(w_f2), kinds, "gather_ffn2", seq_id=2)

    c_all =allgather_devices(_rows128(c), "gather_c")[0].reshape(N_DEV, D)
    mod_cols = cond_matmul(c_all, w_mod[0], "mod_proj")
    fmod_cols = cond_matmul(c_all, w_fmod, "fmod_proj")
    convw_pad = jnp.pad(conv_w[0], ((0, 32 - CONV_WIDTH), (0, 0)))
    rnnw_pad = jnp.pad(rnn_conv_w[0], ((0, SUBLANES - RNN_CONV_WIDTH), (0, 0)))
    n_mod, n_fmod = mod_cols.shape[1], fmod_cols.shape[1]
    small = jnp.concatenate([_rows128(mod_cols), _rows128(fmod_cols), convw_pad, rnnw_pad], axis=0)
    small4 = allgather_chips(small, "gather_cond")
    r0 = N_DEV * n_mod // LANES
    r1 = r0 + N_DEV * n_fmod // LANES
    mod_all = small4[:, :r0].reshape(N_CHIPS, N_DEV, n_mod)
    fmod_all = small4[:, r0:r1].reshape(N_CHIPS, N_DEV, n_fmod)
    convw4 = small4[:, r1:r1 + 32]
    rnnw4 = small4[:, r1 + 32:r1 + 32 + SUBLANES]
    mod_row = lax.dynamic_index_in_dim(mod_all, dev, axis=1, keepdims=False).reshape(1, N_CHIPS * n_mod) + b_mod
    fmod_row = lax.dynamic_index_in_dim(fmod_all, dev, axis=1, keepdims=False).reshape(1, N_CHIPS * n_fmod) + b_fmod[None, :]
    vecs = jnp.concatenate([mod_row.reshape(9, D), fmod_row.reshape(2, D), g_ffn1, g_mix, g_ffn2, g_final[None, :],
                            jnp.zeros((1, D), F32)], axis=0)
    lnv = jnp.concatenate([ln_g, ln_b, jnp.zeros((SUBLANES - 2, M), F32)], axis=0)
    bda = _block_diag(w_a[0]).astype(BF)
    bdi = _block_diag(w_i[0]).astype(BF)

    def reduce_add(gs, recv, ws, tag, kinds_=kinds):
        pairs = [add_sibling_half(g, r_, k, w.shape, cidx, f"add_sibling_{tag}{j}")
                 for j, (g, r_, k, w) in enumerate(zip(gs, recv, kinds_, ws))]
        return [p[0] for p in pairs], [p[1] for p in pairs]

    def reduce_sum(hs_, recv, ws, tag, kinds_=kinds):
        return [sum_chip_pieces(h_, p_, k, w.shape, chip_core, f"sum_chips_{tag}{j}")
                for j, (h_, p_, k, w) in enumerate(zip(hs_, recv, kinds_, ws))]

    rows1 = (R_SH1, R_SC1, R_GT1, R_G1)
    rows3 = (R_SH3, R_SC3, R_GT3, R_G3)
    x1, g1s, u1s, y1 = ffn_fwd(xs, vecs, wi1, wo1, rows1, "ffn1_fwd")
    proj = norm_matmul(x1, vecs, win, (R_SH2, R_SC2, R_G2), "mix_in_proj")
    cq = conv_fwd(proj, convw4, conv_b, "conv_fwd")
    xr, ra, ii, hh = rnn_fwd(proj, rnnw4, rnn_conv_b, bda, bdi, b_a, b_i, lru_lambda, "rnn_fwd")
    x2, ym, ycat = mix_out(cq, proj, hh, x1, vecs, lnv, wout, "mix_out")
    x3, g2s, u2s, y2 = ffn_fwd(x2, vecs, wi2, wo2, rows3, "ffn2_fwd")
    dx3, vgf = final_fwd_bwd(x3, tgt, vecs, "final_loss")

    Fd = wo1.shape[0]
    tk = min(1024, S)
    dx2, act2, dg2, du2, h3b, dy2b, vg3 = ffn_bwd(dx3, x2, vecs, g2s, u2s, y2, wi2, wo2, rows3, "ffn2_bwd")
    gwo2 = matmul(act2, dy2b, "tn", tm=Fd // 2, tn=D, tk=tk, name="ffn2_dwo")
    gwi2 = matmul(h3b, dg2, "tn", tm=D, tn=Fd // 2, tk=tk, name="ffn2_dwg", out_cols=2 * Fd)
    gwi2 = matmul(h3b, du2, "tn", tm=D, tn=Fd // 2, tk=tk, name="ffn2_dwu", out_cols=2 * Fd, col_off=Fd, prev=gwi2)
    recv1_f2 = exchange_sibling_halves([gwi2, gwo2], kinds, shapes_of(w_f2), "reduce1_ffn2", seq_id=3)
    dcq, dhout, duy, dymb, vgd, vgm = mix_out_bwd(dx2, ym, vecs, wout, cq, lnv, proj, hh, "mix_out_bwd")
    gwout = matmul(ycat, dymb, "tn", tm=2 * M, tn=D, tk=tk, name="mix_dwout")
    recv1_f2 = _after(recv1_f2, gwout)
    h_f2, hb_f2 = reduce_add([gwi2, gwo2], recv1_f2, w_f2, "ffn2_")
    recv2_f2 = exchange_chip_pieces(hb_f2, kinds, shapes_of(w_f2), "reduce2_ffn2", seq_id=4)
    duv, dug, dconvw4, dconvb = conv_bwd(_after([dcq], hb_f2)[0], proj, convw4, "conv_bwd")
    dux, dwa4, dwi4, drnnw4, rvec = rnn_bwd(dhout, hh, xr, ra, ii, proj, rnnw4, bda, bdi, lru_lambda, "rnn_bwd")
    dx1, h2b, dpb, vg2 = mix_in_bwd((duv, dug, dux, duy), x1, dx2, vecs, win, "mix_in_bwd")
    gwin = matmul(h2b, dpb, "tn", tm=D, tn=1024, tk=tk, name="mix_dwin")
    recv1_mx = exchange_sibling_halves([gwin, gwout], kinds, shapes_of(w_mx), "reduce1_mix", seq_id=5)
    g_f2 = exchange_reduced_halves(reduce_sum(_after(h_f2, gwin), recv2_f2, w_f2, "ffn2_"), "reduce3_ffn2")
    adam_f2 = [adam_big(w, g, m, v, "adam_" + nm) for w, g, m, v, nm in
               zip(w_f2, g_f2, [m_w_ffn2_in[0], m_w_ffn2_out[0]], [v_w_ffn2_in[0], v_w_ffn2_out[0]], ["ffn2_in", "ffn2_out"])]
    h_mx, hb_mx = reduce_add([gwin, gwout], _after(recv1_mx, adam_f2[0][0], adam_f2[1][0]), w_mx, "mix_")
    recv2_mx = exchange_chip_pieces(hb_mx, kinds, shapes_of(w_mx), "reduce2_mix", seq_id=6)
    dx0, act1, dg1, du1, h1b, dy1b, vg1 = ffn_bwd(_after([dx1], hb_mx)[0], xs, vecs, g1s, u1s, y1, wi1, wo1, rows1, "ffn1_bwd")
    dmod_row = jnp.concatenate([vg1[1:3], vg1[0:1], vg2[0:2], vgd[0:1], vg3[1:3], vg3[0:1]], axis=0)
    gains = jnp.concatenate([vg1[3:4], vg2[2:3], vg3[3:4], vgf[2:4]], axis=0)
    mvecs = jnp.concatenate([dconvb, vgm[0:2], rvec[0:4], jnp.zeros((1, M), F32)], axis=0)
    parts = [_rows128(dmod_row), _rows128(vgf[0:2]), _rows128(gains), _rows128(mvecs),
             _rows128(dconvw4), _rows128(drnnw4), _rows128(_diag_blocks(dwa4, H, HD)), _rows128(_diag_blocks(dwi4, H, HD))]
    sizes = [p.shape[0] for p in parts]
    packed = jnp.concatenate(parts, axis=0)
    gathered = allgather_devices_hbm(packed, "gather_small", seq_id=10)

    gwo1 = matmul(_after([act1], recv2_mx, packed)[0], dy1b, "tn", tm=Fd // 2, tn=D, tk=tk, name="ffn1_dwo")
    w_f1o, w_f1i = w_f1[1:], w_f1[:1]
    recv1_f1o = exchange_sibling_halves([gwo1], ["row"], shapes_of(w_f1o), "reduce1_ffn1_out", seq_id=7)
    gwi1 = matmul(h1b, dg1, "tn", tm=D, tn=Fd // 2, tk=tk, name="ffn1_dwg", out_cols=2 * Fd)
    h_f1o, hb_f1o = reduce_add([gwo1], _after(recv1_f1o, gwi1), w_f1o, "ffn1_out", ["row"])
    recv2_f1o = exchange_chip_pieces(hb_f1o, ["row"], shapes_of(w_f1o), "reduce2_ffn1_out", seq_id=11)
    gwi1 = matmul(h1b, _after([du1], hb_f1o, gathered)[0], "tn", tm=D, tn=Fd // 2, tk=tk, name="ffn1_dwu", out_cols=2 * Fd,
                  col_off=Fd, prev=gwi1)
    recv1_f1i = exchange_sibling_halves([gwi1], ["col"], shapes_of(w_f1i), "reduce1_ffn1_in", seq_id=12)
    g_mx = exchange_reduced_halves(reduce_sum(h_mx, recv2_mx, w_mx, "mix_"), "reduce3_mix")
    summed = sum_slots(gathered, "sum_small")
    offs = [0]
    for s in sizes:
        offs.append(offs[-1] + s)
    seg = lambda k: summed[offs[k]:offs[k + 1]]
    g_b_mod = seg(0).reshape(1, 9 * D)
    g_b_fmod = seg(1).reshape(1, 2 * D)
    gsum = seg(2).reshape(5, D)
    loss = (0.5 / D) * jnp.sum(gsum[4])
    msum = seg(3).reshape(SUBLANES, M)
    g_conv_w = lax.dynamic_index_in_dim(seg(4).reshape(nb, 32, LANES), chip, axis=0, keepdims=False)[:CONV_WIDTH]
    g_rnn_w = lax.dynamic_index_in_dim(seg(5).reshape(nb, SUBLANES, LANES), chip, axis=0, keepdims=False)[:RNN_CONV_WIDTH]
    g_w_a = seg(6).reshape(H, HD, HD)
    g_w_i = seg(7).reshape(H, HD, HD)
    dmod_all = gathered[:, offs[0]:offs[1]].reshape(N_DEV, 9 * D)
    dfmod_all = gathered[:, offs[1]:offs[2]].reshape(N_DEV, 2 * D)
    dmod_cols = lax.dynamic_slice_in_dim(dmod_all, chip * n_mod, n_mod, axis=1)
    dfmod_cols = lax.dynamic_slice_in_dim(dfmod_all, chip * n_fmod, n_fmod, axis=1)

    h_f1i, hb_f1i = reduce_add([gwi1], _after(recv1_f1i, recv2_f1o), w_f1i, "ffn1_in", ["col"])
    recv2_f1i = exchange_chip_pieces(hb_f1i, ["col"], shapes_of(w_f1i), "reduce2_ffn1_in", seq_id=8)
    g_f1o = exchange_reduced_halves(reduce_sum(h_f1o, recv2_f1o, w_f1o, "ffn1_out", ["row"]), "reduce3_ffn1_out")
    g_f1i = exchange_reduced_halves(reduce_sum(h_f1i, recv2_f1i, w_f1i, "ffn1_in", ["col"]), "reduce3_ffn1_in")
    g_f1 = list(g_f1i) + list(g_f1o)

    big_w = w_f1 + w_mx + w_f2
    g_big = g_f1 + list(g_mx) + list(g_f2)
    names = ["ffn1_in", "ffn1_out", "w_in", "w_out", "ffn2_in", "ffn2_out"]
    big_m = [m_w_ffn1_in[0], m_w_ffn1_out[0], m_w_in[0], m_w_out[0]]
    big_v = [v_w_ffn1_in[0], v_w_ffn1_out[0], v_w_in[0], v_w_out[0]]
    big_out = [adam_big(w, g, m, v, "adam_" + nm) for w, g, m, v, nm in zip(big_w, g_big, big_m, big_v, names)] + adam_f2
    g_w_mod, d_w_mod, nm_w_mod, nv_w_mod = adam_cond(c_all, dmod_cols, w_mod[0], m_w_mod[0], v_w_mod[0], "adam_w_mod")
    g_w_fmod, d_w_fmod, nm_w_fmod, nv_w_fmod = adam_cond(c_all, dfmod_cols, w_fmod, m_w_fmod, v_w_fmod, "adam_w_fmod")

    flat2 = lambda a: a.reshape(-1, a.shape[-1])
    small_names = ["b_mod", "g_ffn1", "g_mix", "conv_w", "conv_b", "ln_g", "ln_b", "rnn_conv_w", "rnn_conv_b", "w_a", "b_a",
                   "w_i", "b_i", "lru_lambda", "g_ffn2", "b_fmod", "g_final"]
    small_w = [b_mod, g_ffn1, g_mix, conv_w, conv_b, ln_g, ln_b, rnn_conv_w, rnn_conv_b, w_a, b_a, w_i, b_i, lru_lambda,
               g_ffn2, b_fmod, g_final]
    small_m = [m_b_mod, m_g_ffn1, m_g_mix, m_conv_w, m_conv_b, m_ln_g, m_ln_b, m_rnn_conv_w, m_rnn_conv_b, m_w_a, m_b_a,
               m_w_i, m_b_i, m_lru_lambda, m_g_ffn2, m_b_fmod, m_g_final]
    small_v = [v_b_mod, v_g_ffn1, v_g_mix, v_conv_w, v_conv_b, v_ln_g, v_ln_b, v_rnn_conv_w, v_rnn_conv_b, v_w_a, v_b_a,
               v_w_i, v_b_i, v_lru_lambda, v_g_ffn2, v_b_fmod, v_g_final]
    small_g = [g_b_mod, gsum[0:1], gsum[1:2], g_conv_w, msum[0:1], msum[1:2], msum[2:3], g_rnn_w, msum[3:4], g_w_a, msum[4:5],
               g_w_i, msum[5:6], msum[6:7], gsum[2:3], g_b_fmod, gsum[3:4]]
    small_g = [g.reshape(w.shape) for g, w in zip(small_g, small_w)]
    two_d = lambda a: a.reshape(1, -1) if a.ndim == 1 else flat2(a)
    sd, sm, sv = adam_small([two_d(a) for a in small_w], [two_d(a) for a in small_g], [two_d(a) for a in small_m],
                            [two_d(a) for a in small_v], "adam_small")
    small = {}
    for k, nm in enumerate(small_names):
        shp = small_w[k].shape
        small[nm] = (small_g[k], sd[k].reshape(shp), sm[k].reshape(shp), sv[k].reshape(shp))

    big = {"w_mod": tuple(a[None] for a in (g_w_mod, d_w_mod, nm_w_mod, nv_w_mod)),
           "w_fmod": (g_w_fmod, d_w_fmod, nm_w_fmod, nv_w_fmod)}
    for nm, full, g, (d, nmm, nvv) in zip(["w_ffn1_in", "w_ffn1_out", "w_in", "w_out", "w_ffn2_in", "w_ffn2_out"],
                                         big_w, g_big, big_out):
        big[nm] = tuple(a[None] for a in (g, d, nmm, nvv))
    order = ["w_mod", "b_mod", "g_ffn1", "w_ffn1_in", "w_ffn1_out", "g_mix", "w_in", "conv_w", "conv_b", "ln_g", "ln_b",
             "rnn_conv_w", "rnn_conv_b", "w_a", "b_a", "w_i", "b_i", "lru_lambda", "w_out", "g_ffn2", "w_ffn2_in",
             "w_ffn2_out", "w_fmod", "b_fmod", "g_final"]
    table = {**small, **big}
    outs = [loss, dx0[None]]
    for kind_ in range(4):
        outs.extend(table[nm][kind_] for nm in order)
    return tuple(outs)
```

```python
import functools

import jax
import jax.numpy as jnp
from jax import lax
from jax.experimental import pallas as pl
from jax.experimental.pallas import tpu as pltpu
from jax.experimental.pallas import tpu_sc as plsc

F32 = jnp.float32
BF = jnp.bfloat16
I32 = jnp.int32
MESH = pl.DeviceIdType.MESH

EPS = 1e-6
RG_C = 8.0
MACARON_W = 0.5
CONV_WIDTH = 31
RNN_CONV_WIDTH = 4
ADAM_LR = 0.001
ADAM_B1 = 0.9
ADAM_B2 = 0.999
ADAM_EPS = 1e-08
ADAM_WD = 0.01
ADAM_STEP = 10

LANES = 128
SUBLANES = 8
VMEM_LIMIT = 62 * 1024 * 1024
N_CHIPS = 4
N_DEV = 8

R_SH1, R_SC1, R_GT1, R_SH2, R_SC2, R_GT2, R_SH3, R_SC3, R_GT3, R_FSH, R_FSC, R_G1, R_G2, R_G3, R_GF = range(15)

CONTRACT_LAST = (((1,), (1,)), ((), ()))
CONTRACT_FIRST = (((0,), (0,)), ((), ()))


def _pcall(body, **kw):
    return pl.pallas_call(body, **kw)


def _params(sem=None, vmem=VMEM_LIMIT):
    if sem is None:
        return pltpu.CompilerParams(vmem_limit_bytes=vmem)
    return pltpu.CompilerParams(dimension_semantics=sem, vmem_limit_bytes=vmem)


def _row(ref, r):
    return ref[r:r + 1, :]


def _sigmoid(x):
    return 1.0 / (1.0 + jnp.exp(-x))


def _colsum(x):
    return jnp.sum(x, axis=0, keepdims=True)


def _rowmean(x):
    return jnp.mean(x, axis=-1, keepdims=True)


def matmul(a, b, mode, *, tm, tn, tk, name, out_dtype=F32, out_cols=None, col_off=0, prev=None):
    if mode == "nn":
        (M, K), (K2, N) = a.shape, b.shape
    elif mode == "nt":
        (M, K), (N, K2) = a.shape, b.shape
    else:
        (K, M), (K2, N) = a.shape, b.shape
    assert K == K2 and M % tm == 0 and N % tn == 0 and K % tk == 0 and col_off % tn == 0
    nk = K // tk
    out_cols = N if out_cols is None else out_cols
    off = col_off // tn

    def body(*refs):
        if prev is None:
            a_ref, b_ref, o_ref, acc = refs
        else:
            a_ref, b_ref, _, o_ref, acc = refs
        k = pl.program_id(2)

        @pl.when(k == 0)
        def _():
            acc[...] = jnp.zeros_like(acc)

        av = a_ref[...].astype(BF)
        bv = b_ref[...].astype(BF)
        if mode == "nn":
            acc[...] += jnp.dot(av, bv, preferred_element_type=F32)
        elif mode == "nt":
            acc[...] += lax.dot_general(av, bv, CONTRACT_LAST, preferred_element_type=F32)
        else:
            acc[...] += lax.dot_general(av, bv, CONTRACT_FIRST, preferred_element_type=F32)

        @pl.when(k == nk - 1)
        def _():
            o_ref[...] = acc[...].astype(out_dtype)

    if mode == "nn":
        a_spec = pl.BlockSpec((tm, tk), lambda m, n, k: (m, k))
        b_spec = pl.BlockSpec((tk, tn), lambda m, n, k: (k, n))
    elif mode == "nt":
        a_spec = pl.BlockSpec((tm, tk), lambda m, n, k: (m, k))
        b_spec = pl.BlockSpec((tn, tk), lambda m, n, k: (n, k))
    else:
        a_spec = pl.BlockSpec((tk, tm), lambda m, n, k: (k, m))
        b_spec = pl.BlockSpec((tk, tn), lambda m, n, k: (k, n))
    in_specs = [a_spec, b_spec]
    args = [a, b]
    aliases = {}
    if prev is not None:
        in_specs.append(pl.BlockSpec(memory_space=pl.ANY))
        args.append(prev)
        aliases = {2: 0}
    return _pcall(
        body, name=name, grid=(M // tm, N // tn, nk), in_specs=in_specs,
        out_specs=pl.BlockSpec((tm, tn), lambda m, n, k: (m, n + off)),
        out_shape=jax.ShapeDtypeStruct((M, out_cols), out_dtype),
        scratch_shapes=[pltpu.VMEM((tm, tn), F32)], input_output_aliases=aliases,
        compiler_params=_params(("parallel", "parallel", "arbitrary")),
    )(*args)


def cond_matmul(c_all, w, name):
    B, K = c_all.shape
    N = w.shape[1]
    tn = 256
    assert N % tn == 0

    def body(c_ref, w_ref, o_ref):
        cv = c_ref[...]
        ca = cv * _sigmoid(cv)
        o_ref[...] = jnp.dot(ca, w_ref[...], preferred_element_type=F32, precision=lax.Precision.HIGHEST)

    return _pcall(
        body, name=name, grid=(N // tn,),
        in_specs=[pl.BlockSpec((B, K), lambda n: (0, 0)), pl.BlockSpec((K, tn), lambda n: (0, n))],
        out_specs=pl.BlockSpec((B, tn), lambda n: (0, n)),
        out_shape=jax.ShapeDtypeStruct((B, N), F32), compiler_params=_params(("parallel",)),
    )(c_all, w)


FFN_CH = 1408
FFN_FWD_TS = 512
FFN_BWD_TS = 256
FFN_FWD_VMEM = VMEM_LIMIT


def ffn_fwd(x, vecs, wi, wo, rows, name):
    r_sh, r_sc, r_gt, r_g = rows
    S, D = x.shape
    Fd = wo.shape[0]
    ts, ch = min(FFN_FWD_TS, S), FFN_CH
    ni, nj = S // ts, Fd // ch
    assert nj >= 2

    def body(x_ref, v_ref, wg_ref, wu_ref, wo_ref, xo_ref, g_ref, u_ref, y_ref, h_sc, acc):
        j = pl.program_id(1)

        @pl.when(j == 0)
        def _():
            xv = x_ref[...]
            r = lax.rsqrt(_rowmean(xv * xv) + EPS)
            gs = _row(v_ref, r_g) * (1.0 + _row(v_ref, r_sc))
            h_sc[...] = (xv * r * gs + _row(v_ref, r_sh)).astype(BF)

        hb = h_sc[...]
        G = jnp.dot(hb, wg_ref[...], preferred_element_type=F32)
        U = jnp.dot(hb, wu_ref[...], preferred_element_type=F32)
        g_ref[...] = G.astype(BF)
        u_ref[...] = U.astype(BF)
        act = (G * _sigmoid(G) * U).astype(BF)
        part = jnp.dot(act, wo_ref[...], preferred_element_type=F32)

        @pl.when(j == 0)
        def _():
            acc[...] = part

        @pl.when((j > 0) & (j < nj - 1))
        def _():
            acc[...] += part

        @pl.when(j == nj - 1)
        def _():
            Y = acc[...] + part
            y_ref[...] = Y
            xo_ref[...] = x_ref[...] + (MACARON_W * _row(v_ref, r_gt)) * Y

    tok = pl.BlockSpec((ts, D), lambda i, j: (i, 0))
    hid = pl.BlockSpec((ts, ch), lambda i, j: (i, j))
    return _pcall(
        body, name=name, grid=(ni, nj),
        in_specs=[tok, pl.BlockSpec(vecs.shape, lambda i, j: (0, 0)),
                  pl.BlockSpec((D, ch), lambda i, j: (0, j)), pl.BlockSpec((D, ch), lambda i, j: (0, j + nj)),
                  pl.BlockSpec((ch, D), lambda i, j: (j, 0))],
        out_specs=[tok, hid, hid, tok],
        out_shape=[jax.ShapeDtypeStruct((S, D), F32), jax.ShapeDtypeStruct((S, Fd), BF),
                   jax.ShapeDtypeStruct((S, Fd), BF), jax.ShapeDtypeStruct((S, D), F32)],
        scratch_shapes=[pltpu.VMEM((ts, D), BF), pltpu.VMEM((ts, D), F32)],
        compiler_params=_params(("arbitrary", "arbitrary"), FFN_FWD_VMEM),
    )(x, vecs, wi, wi, wo)


def ffn_bwd(dxo, x, vecs, gs_, us_, y, wi, wo, rows, name):
    r_sh, r_sc, r_gt, r_g = rows
    S, D = x.shape
    Fd = wo.shape[0]
    ts, ch = min(FFN_BWD_TS, S), FFN_CH
    ni, nj = S // ts, Fd // ch

    def body(dxo_ref, x_ref, v_ref, g_ref, u_ref, y_ref, wg_ref, wu_ref, wo_ref,
             dx_ref, act_ref, dg_ref, du_ref, hb_ref, dyb_ref, vg_ref, dyb_sc, dh_sc):
        i, j = pl.program_id(0), pl.program_id(1)

        @pl.when((i == 0) & (j == 0))
        def _():
            vg_ref[...] = jnp.zeros_like(vg_ref)

        @pl.when(j == 0)
        def _():
            dxo_v = dxo_ref[...]
            dyb = ((MACARON_W * _row(v_ref, r_gt)) * dxo_v).astype(BF)
            dyb_sc[...] = dyb
            dyb_ref[...] = dyb
            vg_ref[0:1, :] += MACARON_W * _colsum(dxo_v * y_ref[...])

        dA = lax.dot_general(dyb_sc[...], wo_ref[...], CONTRACT_LAST, preferred_element_type=F32)
        G = g_ref[...].astype(F32)
        U = u_ref[...].astype(F32)
        sg = _sigmoid(G)
        sl = G * sg
        dU = (dA * sl).astype(BF)
        dG = (dA * U * (sg * (1.0 + G * (1.0 - sg)))).astype(BF)
        act_ref[...] = (sl * U).astype(BF)
        dg_ref[...] = dG
        du_ref[...] = dU
        part = (lax.dot_general(dG, wg_ref[...], CONTRACT_LAST, preferred_element_type=F32)
                + lax.dot_general(dU, wu_ref[...], CONTRACT_LAST, preferred_element_type=F32))

        @pl.when(j == 0)
        def _():
            dh_sc[...] = part

        @pl.when((j > 0) & (j < nj - 1))
        def _():
            dh_sc[...] += part

        @pl.when(j == nj - 1)
        def _():
            dh = dh_sc[...] + part
            xv = x_ref[...]
            r = lax.rsqrt(_rowmean(xv * xv) + EPS)
            n = xv * r
            g = _row(v_ref, r_g)
            sc1 = 1.0 + _row(v_ref, r_sc)
            gsc = g * sc1
            hb_ref[...] = (n * gsc + _row(v_ref, r_sh)).astype(BF)
            dhn = dh * n
            vg_ref[1:2, :] += _colsum(dh)
            vg_ref[2:3, :] += _colsum(dhn) * g
            vg_ref[3:4, :] += _colsum(dhn) * sc1
            dn = dh * gsc
            dx_ref[...] = dxo_ref[...] + r * (dn - n * _rowmean(dn * n))

    tok = pl.BlockSpec((ts, D), lambda i, j: (i, 0))
    hid = pl.BlockSpec((ts, ch), lambda i, j: (i, j))
    return _pcall(
        body, name=name, grid=(ni, nj),
        in_specs=[tok, tok, pl.BlockSpec(vecs.shape, lambda i, j: (0, 0)), hid, hid, tok,
                  pl.BlockSpec((D, ch), lambda i, j: (0, j)), pl.BlockSpec((D, ch), lambda i, j: (0, j + nj)),
                  pl.BlockSpec((ch, D), lambda i, j: (j, 0))],
        out_specs=[tok, hid, hid, hid, tok, tok, pl.BlockSpec((SUBLANES, D), lambda i, j: (0, 0))],
        out_shape=[jax.ShapeDtypeStruct((S, D), F32), jax.ShapeDtypeStruct((S, Fd), BF),
                   jax.ShapeDtypeStruct((S, Fd), BF), jax.ShapeDtypeStruct((S, Fd), BF),
                   jax.ShapeDtypeStruct((S, D), BF), jax.ShapeDtypeStruct((S, D), BF),
                   jax.ShapeDtypeStruct((SUBLANES, D), F32)],
        scratch_shapes=[pltpu.VMEM((ts, D), BF), pltpu.VMEM((ts, D), F32)],
        compiler_params=_params(("arbitrary", "arbitrary")),
    )(dxo, x, vecs, gs_, us_, y, wi, wi, wo)


def final_fwd_bwd(x, tgt, vecs, name):
    S, D = x.shape
    ts = min(512, S)

    def body(x_ref, t_ref, v_ref, dx_ref, vg_ref):
        @pl.when(pl.program_id(0) == 0)
        def _():
            vg_ref[...] = jnp.zeros_like(vg_ref)

        xv = x_ref[...]
        r = lax.rsqrt(_rowmean(xv * xv) + EPS)
        n = xv * r
        g = _row(v_ref, R_GF)
        sc1 = 1.0 + _row(v_ref, R_FSC)
        gsc = g * sc1
        e = n * gsc + _row(v_ref, R_FSH) - t_ref[...]
        vg_ref[3:4, :] += _colsum(e * e)
        dout = e * (1.0 / D)
        dn_ = dout * n
        vg_ref[0:1, :] += _colsum(dout)
        vg_ref[1:2, :] += _colsum(dn_) * g
        vg_ref[2:3, :] += _colsum(dn_) * sc1
        dn = dout * gsc
        dx_ref[...] = r * (dn - n * _rowmean(dn * n))

    tok = pl.BlockSpec((ts, D), lambda i: (i, 0))
    return _pcall(
        body, name=name, grid=(S // ts,),
        in_specs=[tok, tok, pl.BlockSpec(vecs.shape, lambda i: (0, 0))],
        out_specs=[tok, pl.BlockSpec((SUBLANES, D), lambda i: (0, 0))],
        out_shape=[jax.ShapeDtypeStruct((S, D), F32), jax.ShapeDtypeStruct((SUBLANES, D), F32)],
        compiler_params=_params(("arbitrary",)),
    )(x, tgt, vecs)


def norm_matmul(x, vecs, w, rows, name):
    r_sh, r_sc, r_g = rows
    S, D = x.shape
    N = w.shape[1]
    ts, tn = min(512, S), 1024
    nn = N // tn

    def body(x_ref, v_ref, w_ref, o_ref, h_sc):
        @pl.when(pl.program_id(1) == 0)
        def _():
            xv = x_ref[...]
            r = lax.rsqrt(_rowmean(xv * xv) + EPS)
            gs = _row(v_ref, r_g) * (1.0 + _row(v_ref, r_sc))
            h_sc[...] = (xv * r * gs + _row(v_ref, r_sh)).astype(BF)

        o_ref[...] = jnp.dot(h_sc[...], w_ref[...], preferred_element_type=F32)

    return _pcall(
        body, name=name, grid=(S // ts, nn),
        in_specs=[pl.BlockSpec((ts, D), lambda i, n: (i, 0)), pl.BlockSpec(vecs.shape, lambda i, n: (0, 0)),
                  pl.BlockSpec((D, tn), lambda i, n: (0, n))],
        out_specs=pl.BlockSpec((ts, tn), lambda i, n: (i, n)),
        out_shape=jax.ShapeDtypeStruct((S, N), F32),
        scratch_shapes=[pltpu.VMEM((ts, D), BF)],
        compiler_params=_params(("arbitrary", "arbitrary")),
    )(x, vecs, w)


SEQ_TT = 256
CONV_PAD = 32


def conv_fwd(proj, convw4, conv_b, name):
    S = proj.shape[0]
    M = conv_b.shape[1]
    nb = M // LANES
    tt = min(SEQ_TT, S)

    def body(uv_ref, ug_ref, w_ref, b_ref, cq_ref, qp):
        qp[0:CONV_PAD, :] = jnp.zeros((CONV_PAD, LANES), F32)

        def step(t, carry):
            base = pl.multiple_of(t * tt, tt)
            qp[pl.ds(base + CONV_PAD, tt), :] = uv_ref[pl.ds(base, tt), :] * _sigmoid(ug_ref[pl.ds(base, tt), :])
            acc = jnp.broadcast_to(b_ref[...], (tt, LANES))
            for k in range(CONV_WIDTH):
                acc = acc + w_ref[k:k + 1, :] * qp[pl.ds(base + (CONV_PAD - CONV_WIDTH + 1) + k, tt), :]
            cq_ref[pl.ds(base, tt), :] = acc
            return carry

        lax.fori_loop(0, S // tt, step, 0)

    return _pcall(
        body, name=name, grid=(nb,),
        in_specs=[pl.BlockSpec((S, LANES), lambda c: (0, c)), pl.BlockSpec((S, LANES), lambda c: (0, c + nb)),
                  pl.BlockSpec((None, 32, LANES), lambda c: (c, 0, 0)), pl.BlockSpec((1, LANES), lambda c: (0, c))],
        out_specs=pl.BlockSpec((S, LANES), lambda c: (0, c)),
        out_shape=jax.ShapeDtypeStruct((S, M), F32),
        scratch_shapes=[pltpu.VMEM((S + CONV_PAD, LANES), F32)],
        compiler_params=_params(("arbitrary",)),
    )(proj, proj, convw4, conv_b)


def conv_bwd(dcq, proj, convw4, name):
    S, M = dcq.shape
    nb = M // LANES
    tt = min(SEQ_TT, S)
    off = CONV_PAD - CONV_WIDTH + 1

    def body(dcq_ref, uv_ref, ug_ref, w_ref, duv_ref, dug_ref, dw_ref, db_ref, qp, dp, dw8, db8):
        qp[0:CONV_PAD, :] = jnp.zeros((CONV_PAD, LANES), F32)
        dp[S:S + CONV_PAD, :] = jnp.zeros((CONV_PAD, LANES), F32)
        dw8[...] = jnp.zeros_like(dw8)
        db8[...] = jnp.zeros_like(db8)

        def fill(t, carry):
            base = pl.multiple_of(t * tt, tt)
            qp[pl.ds(base + CONV_PAD, tt), :] = uv_ref[pl.ds(base, tt), :] * _sigmoid(ug_ref[pl.ds(base, tt), :])
            dp[pl.ds(base, tt), :] = dcq_ref[pl.ds(base, tt), :]
            return carry

        lax.fori_loop(0, S // tt, fill, 0)

        def step(t, carry):
            base = pl.multiple_of(t * tt, tt)
            d_t = dcq_ref[pl.ds(base, tt), :]
            db8[...] += d_t.reshape(tt // SUBLANES, SUBLANES, LANES).sum(axis=0)
            dq = jnp.zeros((tt, LANES), F32)
            for k in range(CONV_WIDTH):
                prod = d_t * qp[pl.ds(base + off + k, tt), :]
                dw8[k] += prod.reshape(tt // SUBLANES, SUBLANES, LANES).sum(axis=0)
                dq = dq + w_ref[k:k + 1, :] * dp[pl.ds(base + (CONV_WIDTH - 1) - k, tt), :]
            uv = uv_ref[pl.ds(base, tt), :]
            sg = _sigmoid(ug_ref[pl.ds(base, tt), :])
            duv_ref[pl.ds(base, tt), :] = dq * sg
            dug_ref[pl.ds(base, tt), :] = dq * uv * sg * (1.0 - sg)
            return carry

        lax.fori_loop(0, S // tt, step, 0)
        dw_ref[...] = jnp.zeros_like(dw_ref)
        for k in range(CONV_WIDTH):
            dw_ref[k:k + 1, :] = _colsum(dw8[k])
        db_ref[...] = _colsum(db8[...])

    col = lambda o: pl.BlockSpec((S, LANES), lambda c: (0, c + o))
    return _pcall(
        body, name=name, grid=(nb,),
        in_specs=[col(0), col(0), col(nb), pl.BlockSpec((None, 32, LANES), lambda c: (c, 0, 0))],
        out_specs=[col(0), col(0), pl.BlockSpec((None, 32, LANES), lambda c: (c, 0, 0)),
                   pl.BlockSpec((1, LANES), lambda c: (0, c))],
        out_shape=[jax.ShapeDtypeStruct((S, M), F32), jax.ShapeDtypeStruct((S, M), F32),
                   jax.ShapeDtypeStruct((nb, 32, LANES), F32), jax.ShapeDtypeStruct((1, M), F32)],
        scratch_shapes=[pltpu.VMEM((S + CONV_PAD, LANES), F32), pltpu.VMEM((S + CONV_PAD, LANES), F32),
                        pltpu.VMEM((32, SUBLANES, LANES), F32), pltpu.VMEM((SUBLANES, LANES), F32)],
        compiler_params=_params(("arbitrary",)),
    )(dcq, proj, proj, convw4)


def _log_sigmoid(x):
    return jnp.minimum(x, 0.0) - jnp.log(1.0 + jnp.exp(-jnp.abs(x)))


def _rg_gate_terms(ra, ls):
    la = RG_C * ra * ls
    a = jnp.exp(la)
    th = jnp.tanh(la)
    mult = jnp.sqrt(-2.0 * th / (1.0 - th))
    return a, mult


def rnn_fwd(proj, rnnw4, rnn_b, bda, bdi, b_a, b_i, lam, name):
    S = proj.shape[0]
    M = rnn_b.shape[1]
    nb = M // LANES
    tt = min(SEQ_TT, S)
    KW = RNN_CONV_WIDTH

    def body(ux_ref, w_ref, rb_ref, bda_ref, bdi_ref, ba_ref, bi_ref, lam_ref,
             xr_ref, ra_ref, ii_ref, h_ref, uxp, a_sc, b_sc):
        uxp[0:SUBLANES, :] = jnp.zeros((SUBLANES, LANES), F32)
        ls = _log_sigmoid(lam_ref[...])

        def step(t, carry):
            base = pl.multiple_of(t * tt, tt)
            uxp[pl.ds(base + SUBLANES, tt), :] = ux_ref[pl.ds(base, tt), :]
            xr = jnp.broadcast_to(rb_ref[...], (tt, LANES))
            for k in range(KW):
                xr = xr + w_ref[k:k + 1, :] * uxp[pl.ds(base + (SUBLANES - KW + 1) + k, tt), :]
            xb = xr.astype(BF)
            ra = _sigmoid(jnp.dot(xb, bda_ref[...], preferred_element_type=F32) + ba_ref[...])
            ii = _sigmoid(jnp.dot(xb, bdi_ref[...], preferred_element_type=F32) + bi_ref[...])
            a, mult = _rg_gate_terms(ra, ls)
            xr_ref[pl.ds(base, tt), :] = xr
            ra_ref[pl.ds(base, tt), :] = ra
            ii_ref[pl.ds(base, tt), :] = ii
            a_sc[pl.ds(base, tt), :] = a
            b_sc[pl.ds(base, tt), :] = mult * (ii * xr)
            return carry

        lax.fori_loop(0, S // tt, step, 0)

        rows = lax.broadcasted_iota(I32, (SUBLANES, LANES), 0)

        def scan(t, hprev):
            base = pl.multiple_of(t * SUBLANES, SUBLANES)
            A = a_sc[pl.ds(base, SUBLANES), :]
            B = b_sc[pl.ds(base, SUBLANES), :]
            for d in (1, 2, 4):
                As = jnp.where(rows >= d, pltpu.roll(A, d, axis=0), 1.0)
                Bs = jnp.where(rows >= d, pltpu.roll(B, d, axis=0), 0.0)
                B = A * Bs + B
                A = A * As
            hh = B + A * hprev
            h_ref[pl.ds(base, SUBLANES), :] = hh
            return jnp.broadcast_to(hh[SUBLANES - 1:SUBLANES, :], (SUBLANES, LANES))

        lax.fori_loop(0, S // SUBLANES, scan, jnp.zeros((SUBLANES, LANES), F32))

    col = lambda o: pl.BlockSpec((S, LANES), lambda c: (0, c + o))
    vec = pl.BlockSpec((1, LANES), lambda c: (0, c))
    diag = pl.BlockSpec((LANES, LANES), lambda c: (c, c))
    return _pcall(
        body, name=name, grid=(nb,),
        in_specs=[col(2 * nb), pl.BlockSpec((None, SUBLANES, LANES), lambda c: (c, 0, 0)), vec, diag, diag, vec, vec, vec],
        out_specs=[col(0)] * 4,
        out_shape=[jax.ShapeDtypeStruct((S, M), F32)] * 4,
        scratch_shapes=[pltpu.VMEM((S + SUBLANES, LANES), F32), pltpu.VMEM((S, LANES), F32), pltpu.VMEM((S, LANES), F32)],
        compiler_params=_params(("arbitrary",)),
    )(proj, rnnw4, rnn_b, bda, bdi, b_a, b_i, lam)


def rnn_bwd(dhout, h, xr, ra, ii, proj, rnnw4, bda, bdi, lam, name):
    S, M = h.shape
    nb = M // LANES
    tt = min(SEQ_TT, S)
    KW = RNN_CONV_WIDTH
    SL = SUBLANES

    def body(dh_ref, h_ref, xr_ref, ra_ref, ii_ref, ux_ref, w_ref, bda_ref, bdi_ref, lam_ref,
             dux_ref, dwa_ref, dwi_ref, drw_ref, vec_ref,
             a_sc, hp, g_sc, dpa_sc, dpi_sc, dxp, uxp, acc8, drw8):
        zero8 = jnp.zeros((SL, LANES), F32)
        a_sc[S:S + SL, :] = zero8
        hp[0:SL, :] = zero8
        dxp[S:S + SL, :] = zero8
        uxp[0:SL, :] = zero8
        acc8[...] = jnp.zeros_like(acc8)
        drw8[...] = jnp.zeros_like(drw8)
        lamv = lam_ref[...]
        ls = _log_sigmoid(lamv)

        def fill(t, carry):
            base = pl.multiple_of(t * tt, tt)
            a_sc[pl.ds(base, tt), :] = jnp.exp(RG_C * ra_ref[pl.ds(base, tt), :] * ls)
            hp[pl.ds(base + SL, tt), :] = h_ref[pl.ds(base, tt), :]
            uxp[pl.ds(base + SL, tt), :] = ux_ref[pl.ds(base, tt), :]
            return carry

        lax.fori_loop(0, S // tt, fill, 0)

        rows = lax.broadcasted_iota(I32, (SL, LANES), 0)
        nt8 = S // SL

        def rscan(t, gnext):
            base = pl.multiple_of((nt8 - 1 - t) * SL, SL)
            A = a_sc[pl.ds(base + 1, SL), :]
            B = dh_ref[pl.ds(base, SL), :]
            for d in (1, 2, 4):
                As = jnp.where(rows < SL - d, pltpu.roll(A, SL - d, axis=0), 1.0)
                Bs = jnp.where(rows < SL - d, pltpu.roll(B, SL - d, axis=0), 0.0)
                B = A * Bs + B
                A = A * As
            g = B + A * gnext
            g_sc[pl.ds(base, SL), :] = g
            return jnp.broadcast_to(g[0:1, :], (SL, LANES))

        lax.fori_loop(0, nt8, rscan, zero8)

        def red8(v):
            return v.reshape(tt // SL, SL, LANES).sum(axis=0)

        def step(t, carry):
            base = pl.multiple_of(t * tt, tt)
            g = g_sc[pl.ds(base, tt), :]
            hprev = hp[pl.ds(base + SL - 1, tt), :]
            xr_t = xr_ref[pl.ds(base, tt), :]
            ra_t = ra_ref[pl.ds(base, tt), :]
            ii_t = ii_ref[pl.ds(base, tt), :]
            a, mult = _rg_gate_terms(ra_t, ls)
            gx = g * xr_t
            dmult = gx * ii_t
            dii = gx * mult
            dxr = g * (mult * ii_t)
            dla = g * hprev * a - dmult * (a * a) / mult
            acc8[3] += red8(dla * ra_t)
            dpa = dla * (RG_C * ls) * ra_t * (1.0 - ra_t)
            dpi = dii * ii_t * (1.0 - ii_t)
            dpab = dpa.astype(BF)
            dpib = dpi.astype(BF)
            dxr = dxr + (lax.dot_general(dpab, bda_ref[...], CONTRACT_LAST, preferred_element_type=F32)
                         + lax.dot_general(dpib, bdi_ref[...], CONTRACT_LAST, preferred_element_type=F32))
            dpa_sc[pl.ds(base, tt), :] = dpab
            dpi_sc[pl.ds(base, tt), :] = dpib
            dxp[pl.ds(base, tt), :] = dxr
            acc8[0] += red8(dxr)
            acc8[1] += red8(dpa)
            acc8[2] += red8(dpi)
            return carry

        lax.fori_loop(0, S // tt, step, 0)

        def convb(t, carry):
            base = pl.multiple_of(t * tt, tt)
            d_t = dxp[pl.ds(base, tt), :]
            dux = jnp.zeros((tt, LANES), F32)
            for k in range(KW):
                drw8[k] += red8(d_t * uxp[pl.ds(base + (SL - KW + 1) + k, tt), :])
                dux = dux + w_ref[k:k + 1, :] * dxp[pl.ds(base + (KW - 1) - k, tt), :]
            dux_ref[pl.ds(base, tt), :] = dux
            return carry

        lax.fori_loop(0, S // tt, convb, 0)

        xb = xr_ref[...].astype(BF)
        dwa_ref[...] = lax.dot_general(xb, dpa_sc[...], CONTRACT_FIRST, preferred_element_type=F32)
        dwi_ref[...] = lax.dot_general(xb, dpi_sc[...], CONTRACT_FIRST, preferred_element_type=F32)
        drw_ref[...] = jnp.zeros_like(drw_ref)
        vec_ref[...] = jnp.zeros_like(vec_ref)
        for k in range(KW):
            drw_ref[k:k + 1, :] = _colsum(drw8[k])
        for k in range(3):
            vec_ref[k:k + 1, :] = _colsum(acc8[k])
        vec_ref[3:4, :] = _colsum(acc8[3]) * (RG_C * _sigmoid(-lamv))

    col = lambda o: pl.BlockSpec((S, LANES), lambda c: (0, c + o))
    vec = pl.BlockSpec((1, LANES), lambda c: (0, c))
    diag = pl.BlockSpec((LANES, LANES), lambda c: (c, c))
    blk3 = lambda r: pl.BlockSpec((None, r, LANES), lambda c: (c, 0, 0))
    return _pcall(
        body, name=name, grid=(nb,),
        in_specs=[col(0), col(0), col(0), col(0), col(0), col(2 * nb), blk3(SL), diag, diag, vec],
        out_specs=[col(0), blk3(LANES), blk3(LANES), blk3(SL), pl.BlockSpec((SL, LANES), lambda c: (0, c))],
        out_shape=[jax.ShapeDtypeStruct((S, M), F32), jax.ShapeDtypeStruct((nb, LANES, LANES), F32),
                   jax.ShapeDtypeStruct((nb, LANES, LANES), F32), jax.ShapeDtypeStruct((nb, SL, LANES), F32),
                   jax.ShapeDtypeStruct((SL, M), F32)],
        scratch_shapes=[pltpu.VMEM((S + SL, LANES), F32), pltpu.VMEM((S + SL, LANES), F32), pltpu.VMEM((S, LANES), F32),
                        pltpu.VMEM((S, LANES), BF), pltpu.VMEM((S, LANES), BF), pltpu.VMEM((S + SL, LANES), F32),
                        pltpu.VMEM((S + SL, LANES), F32), pltpu.VMEM((SL, SL, LANES), F32), pltpu.VMEM((SL, SL, LANES), F32)],
        compiler_params=_params(("arbitrary",)),
    )(dhout, h, xr, ra, ii, proj, rnnw4, bda, bdi, lam)


GELU_K = 0.7978845608028654
GELU_C = 0.044715


def _layernorm_parts(cq):
    mu = _rowmean(cq)
    d = cq - mu
    rstd = lax.rsqrt(_rowmean(d * d) + EPS)
    return d * rstd, rstd


def mix_out(cq, proj, h, x, vecs, lnv, wout, name):
    S, D = x.shape
    M = cq.shape[1]
    ts = min(512, S)

    def body(cq_ref, uy_ref, h_ref, x_ref, v_ref, ln_ref, w_ref, xo_ref, ym_ref, yc_ref):
        z, _ = _layernorm_parts(cq_ref[...])
        l = z * _row(ln_ref, 0) + _row(ln_ref, 1)
        yc_ref[:, 0:M] = (l * _sigmoid(l)).astype(BF)
        uy = uy_ref[...]
        gelu = 0.5 * uy * (1.0 + jnp.tanh(GELU_K * (uy + GELU_C * uy * uy * uy)))
        yc_ref[:, M:2 * M] = (gelu * h_ref[...]).astype(BF)
        ym = jnp.dot(yc_ref[...], w_ref[...], preferred_element_type=F32)
        ym_ref[...] = ym
        xo_ref[...] = x_ref[...] + _row(v_ref, R_GT2) * ym

    tok = pl.BlockSpec((ts, D), lambda i: (i, 0))
    mtok = lambda o: pl.BlockSpec((ts, M), lambda i: (i, o))
    return _pcall(
        body, name=name, grid=(S // ts,),
        in_specs=[mtok(0), mtok(3), mtok(0), tok, pl.BlockSpec(vecs.shape, lambda i: (0, 0)),
                  pl.BlockSpec(lnv.shape, lambda i: (0, 0)), pl.BlockSpec(wout.shape, lambda i: (0, 0))],
        out_specs=[tok, tok, pl.BlockSpec((ts, 2 * M), lambda i: (i, 0))],
        out_shape=[jax.ShapeDtypeStruct((S, D), F32), jax.ShapeDtypeStruct((S, D), F32),
                   jax.ShapeDtypeStruct((S, 2 * M), BF)],
        compiler_params=_params(("arbitrary",)),
    )(cq, proj, h, x, vecs, lnv, wout)


def mix_out_bwd(dxo, ym, vecs, wout, cq, lnv, proj, h, name):
    S, D = dxo.shape
    M = cq.shape[1]
    ts = min(512, S)

    def body(dxo_ref, ym_ref, v_ref, w_ref, cq_ref, ln_ref, uy_ref, h_ref,
             dcq_ref, dh_ref, duy_ref, dyb_ref, vgd_ref, vgm_ref):
        @pl.when(pl.program_id(0) == 0)
        def _():
            vgd_ref[...] = jnp.zeros_like(vgd_ref)
            vgm_ref[...] = jnp.zeros_like(vgm_ref)

        dxo_v = dxo_ref[...]
        dyb = (_row(v_ref, R_GT2) * dxo_v).astype(BF)
        dyb_ref[...] = dyb
        vgd_ref[0:1, :] += _colsum(dxo_v * ym_ref[...])
        dycat = lax.dot_general(dyb, w_ref[...], CONTRACT_LAST, preferred_element_type=F32)
        dyc = dycat[:, 0:M]
        dyr = dycat[:, M:2 * M]
        z, rstd = _layernorm_parts(cq_ref[...])
        lng = _row(ln_ref, 0)
        l = z * lng + _row(ln_ref, 1)
        sl = _sigmoid(l)
        dl = dyc * (sl * (1.0 + l * (1.0 - sl)))
        vgm_ref[0:1, :] += _colsum(dl * z)
        vgm_ref[1:2, :] += _colsum(dl)
        dz = dl * lng
        dcq_ref[...] = rstd * (dz - _rowmean(dz) - z * _rowmean(dz * z))
        uy = uy_ref[...]
        u2 = uy * uy
        th = jnp.tanh(GELU_K * (uy + GELU_C * uy * u2))
        gelu = 0.5 * uy * (1.0 + th)
        dgelu = 0.5 * (1.0 + th) + 0.5 * uy * (1.0 - th * th) * (GELU_K * (1.0 + 3.0 * GELU_C * u2))
        dh_ref[...] = dyr * gelu
        duy_ref[...] = dyr * h_ref[...] * dgelu

    tok = pl.BlockSpec((ts, D), lambda i: (i, 0))
    mtok = lambda o: pl.BlockSpec((ts, M), lambda i: (i, o))
    return _pcall(
        body, name=name, grid=(S // ts,),
        in_specs=[tok, tok, pl.BlockSpec(vecs.shape, lambda i: (0, 0)), pl.BlockSpec(wout.shape, lambda i: (0, 0)),
                  mtok(0), pl.BlockSpec(lnv.shape, lambda i: (0, 0)), mtok(3), mtok(0)],
        out_specs=[mtok(0), mtok(0), mtok(0), tok, pl.BlockSpec((SUBLANES, D), lambda i: (0, 0)),
                   pl.BlockSpec((SUBLANES, M), lambda i: (0, 0))],
        out_shape=[jax.ShapeDtypeStruct((S, M), F32)] * 3 + [jax.ShapeDtypeStruct((S, D), BF),
                   jax.ShapeDtypeStruct((SUBLANES, D), F32), jax.ShapeDtypeStruct((SUBLANES, M), F32)],
        compiler_params=_params(("arbitrary",)),
    )(dxo, ym, vecs, wout, cq, lnv, proj, h)


def mix_in_bwd(dparts, x, dxo, vecs, win, name):
    S, D = x.shape
    M = dparts[0].shape[1]
    ts = min(512, S)

    def body(d0, d1, d2, d3, x_ref, dxo_ref, v_ref, w_ref, dx_ref, hb_ref, dp_ref, vg_ref):
        @pl.when(pl.program_id(0) == 0)
        def _():
            vg_ref[...] = jnp.zeros_like(vg_ref)

        for q, dref in enumerate((d0, d1, d2, d3)):
            dp_ref[:, q * M:(q + 1) * M] = dref[...].astype(BF)
        dh = lax.dot_general(dp_ref[...], w_ref[...], CONTRACT_LAST, preferred_element_type=F32)
        xv = x_ref[...]
        r = lax.rsqrt(_rowmean(xv * xv) + EPS)
        n = xv * r
        g = _row(v_ref, R_G2)
        sc1 = 1.0 + _row(v_ref, R_SC2)
        gsc = g * sc1
        hb_ref[...] = (n * gsc + _row(v_ref, R_SH2)).astype(BF)
        dhn = dh * n
        vg_ref[0:1, :] += _colsum(dh)
        vg_ref[1:2, :] += _colsum(dhn) * g
        vg_ref[2:3, :] += _colsum(dhn) * sc1
        dn = dh * gsc
        dx_ref[...] = dxo_ref[...] + r * (dn - n * _rowmean(dn * n))

    tok = pl.BlockSpec((ts, D), lambda i: (i, 0))
    mtok = pl.BlockSpec((ts, M), lambda i: (i, 0))
    return _pcall(
        body, name=name, grid=(S // ts,),
        in_specs=[mtok] * 4 + [tok, tok, pl.BlockSpec(vecs.shape, lambda i: (0, 0)), pl.BlockSpec(win.shape, lambda i: (0, 0))],
        out_specs=[tok, tok, pl.BlockSpec((ts, 4 * M), lambda i: (i, 0)), pl.BlockSpec((SUBLANES, D), lambda i: (0, 0))],
        out_shape=[jax.ShapeDtypeStruct((S, D), F32), jax.ShapeDtypeStruct((S, D), BF),
                   jax.ShapeDtypeStruct((S, 4 * M), BF), jax.ShapeDtypeStruct((SUBLANES, D), F32)],
        compiler_params=_params(("arbitrary",)),
    )(*dparts, x, dxo, vecs, win)


def _adamw(w, g, m, v):
    m = ADAM_B1 * m + (1.0 - ADAM_B1) * g
    v = ADAM_B2 * v + (1.0 - ADAM_B2) * (g * g)
    m_hat = m / (1.0 - ADAM_B1 ** ADAM_STEP)
    v_hat = v / (1.0 - ADAM_B2 ** ADAM_STEP)
    delta = -ADAM_LR * (m_hat / (jnp.sqrt(v_hat) + ADAM_EPS) + ADAM_WD * w)
    return delta, m, v


def adam_big(w, g, m, v, name):
    R, C = w.shape
    tr = 256 if R % 256 == 0 else R // 2 if (R // 2) % SUBLANES == 0 and R > 512 else R
    tc = C if C <= 1536 else (1152 if C % 1152 == 0 else 1024)
    assert R % tr == 0 and C % tc == 0

    def body(w_ref, g_ref, m_ref, v_ref, d_ref, nm_ref, nv_ref):
        d, nm, nv = _adamw(w_ref[...], g_ref[...], m_ref[...], v_ref[...])
        d_ref[...] = d
        nm_ref[...] = nm
        nv_ref[...] = nv

    blk = pl.BlockSpec((tr, tc), lambda i, j: (i, j))
    return _pcall(
        body, name=name, grid=(R // tr, C // tc), in_specs=[blk] * 4, out_specs=[blk] * 3,
        out_shape=[jax.ShapeDtypeStruct((R, C), F32)] * 3, compiler_params=_params(("parallel", "parallel")),
    )(w, g, m, v)


def adam_cond(c_all, dmod, w, m, v, name):
    B, Kin = c_all.shape
    N = w.shape[1]
    tn = 256
    assert N % tn == 0

    def body(c_ref, d_ref, w_ref, m_ref, v_ref, g_ref, dl_ref, nm_ref, nv_ref):
        cv = c_ref[...]
        ca = cv * _sigmoid(cv)
        g = lax.dot_general(ca, d_ref[...], CONTRACT_FIRST, preferred_element_type=F32, precision=lax.Precision.HIGHEST)
        d, nm, nv = _adamw(w_ref[...], g, m_ref[...], v_ref[...])
        g_ref[...] = g
        dl_ref[...] = d
        nm_ref[...] = nm
        nv_ref[...] = nv

    blk = pl.BlockSpec((Kin, tn), lambda n: (0, n))
    return _pcall(
        body, name=name, grid=(N // tn,),
        in_specs=[pl.BlockSpec((B, Kin), lambda n: (0, 0)), pl.BlockSpec((B, tn), lambda n: (0, n)), blk, blk, blk],
        out_specs=[blk] * 4, out_shape=[jax.ShapeDtypeStruct((Kin, N), F32)] * 4,
        compiler_params=_params(("parallel",)),
    )(c_all, dmod, w, m, v)


def adam_small(ws, gs, ms, vs, name):
    n = len(ws)

    def body(*refs):
        ins, outs = refs[:4 * n], refs[4 * n:]
        for k in range(n):
            d, nm, nv = _adamw(ins[k][...], ins[n + k][...], ins[2 * n + k][...], ins[3 * n + k][...])
            outs[k][...] = d
            outs[n + k][...] = nm
            outs[2 * n + k][...] = nv

    specs = [pl.BlockSpec(w.shape, lambda i: (0, 0)) for w in ws]
    shapes = [jax.ShapeDtypeStruct(w.shape, F32) for w in ws]
    out = _pcall(body, name=name, grid=(1,), in_specs=specs * 4, out_specs=specs * 3, out_shape=shapes * 3,
                 compiler_params=_params(("arbitrary",)))(*ws, *gs, *ms, *vs)
    return out[:n], out[n:2 * n], out[2 * n:]


def _me():
    return lax.axis_index("x"), lax.axis_index("y"), lax.axis_index("c")


def _flip(x, y, p):
    return (x ^ (p >> 1) if (p >> 1) else x), (y ^ (p & 1) if (p & 1) else y)


def _handshake(peers):
    barrier = pltpu.get_barrier_semaphore()
    for peer in peers:
        pl.semaphore_signal(barrier, inc=1, device_id=peer, device_id_type=MESH)
    pl.semaphore_wait(barrier, len(peers))


def _seq_call(body, *, name, n_in, out_shape, sem_shapes, collective_id):
    del n_in
    return pl.kernel(body, out_type=out_shape, mesh=plsc.ScalarSubcoreMesh(axis_name="sq", num_cores=1), name=name,
                     scratch_types=sem_shapes, compiler_params=pltpu.CompilerParams(collective_id=collective_id))


def _hbm_comm_call(body, *, name, n_in, out_shape, sem_shapes, seq_id):
    if seq_id is not None:
        return _seq_call(body, name=name, n_in=n_in, out_shape=out_shape, sem_shapes=sem_shapes, collective_id=seq_id)
    anyspec = pl.BlockSpec(memory_space=pl.ANY)
    return _pcall(body, name=name, in_specs=[anyspec] * n_in, out_specs=[anyspec] * len(out_shape), out_shape=out_shape,
                  scratch_shapes=sem_shapes, compiler_params=_params())


def allgather_devices(v, name, with_sum=False):
    R, L = v.shape

    def body(v_ref, out_ref, *rest):
        if with_sum:
            sum_ref, send_sems, recv_sems = rest
        else:
            send_sems, recv_sems = rest
        x, y, c = _me()
        me = 4 * x + 2 * y + c
        out_ref[me] = v_ref[...]
        copies = []
        for p in range(1, N_DEV):
            px, py = _flip(x, y, p >> 1)
            pc = (1 - c) if (p & 1) else c
            peer = 4 * px + 2 * py + pc
            send = pltpu.make_async_remote_copy(src_ref=v_ref, dst_ref=out_ref.at[me], send_sem=send_sems.at[p - 1],
                                                recv_sem=recv_sems.at[p - 1], device_id=(px, py, pc), device_id_type=MESH)
            send.start()
            recv = pltpu.make_async_remote_copy(src_ref=v_ref, dst_ref=out_ref.at[peer], send_sem=send_sems.at[p - 1],
                                                recv_sem=recv_sems.at[p - 1], device_id=(px, py, pc), device_id_type=MESH)
            copies.append((send, recv))
        for send, recv in copies:
            recv.wait_recv()
        for send, recv in copies:
            send.wait_send()
        if with_sum:
            s = out_ref[0]
            for k in range(1, N_DEV):
                s = s + out_ref[k]
            sum_ref[...] = s

    vm = pl.BlockSpec(memory_space=pltpu.VMEM)
    out_shape = [jax.ShapeDtypeStruct((N_DEV, R, L), F32)]
    if with_sum:
        out_shape.append(jax.ShapeDtypeStruct((R, L), F32))
    return _pcall(
        body, name=name, in_specs=[vm], out_specs=[vm] * len(out_shape), out_shape=out_shape,
        scratch_shapes=[pltpu.SemaphoreType.DMA((N_DEV - 1,)), pltpu.SemaphoreType.DMA((N_DEV - 1,))],
        compiler_params=_params(),
    )(v)


def allgather_devices_hbm(v, name, seq_id):
    R, L = v.shape

    def body(v_ref, out_ref, send_sems, recv_sems, local_sem):
        x, y, c = _me()
        me = 4 * x + 2 * y + c
        peers = []
        for p in range(1, N_DEV):
            px, py = _flip(x, y, p >> 1)
            peers.append((px, py, (1 - c) if (p & 1) else c))
        _handshake(peers)
        lc = pltpu.make_async_copy(v_ref, out_ref.at[me], local_sem)
        lc.start()
        copies = []
        for p, (px, py, pc) in enumerate(peers):
            send = pltpu.make_async_remote_copy(src_ref=v_ref, dst_ref=out_ref.at[me], send_sem=send_sems.at[p],
                                                recv_sem=recv_sems.at[p], device_id=(px, py, pc), device_id_type=MESH)
            send.start()
            recv = pltpu.make_async_remote_copy(src_ref=v_ref, dst_ref=out_ref.at[4 * px + 2 * py + pc], send_sem=send_sems.at[p],
                                                recv_sem=recv_sems.at[p], device_id=(px, py, pc), device_id_type=MESH)
            copies.append((send, recv))
        for send, recv in copies:
            recv.wait_recv()
        for send, recv in copies:
            send.wait_send()
        lc.wait()

    return _seq_call(body, name=name, n_in=1, out_shape=[jax.ShapeDtypeStruct((N_DEV, R, L), F32)],
                     sem_shapes=[pltpu.SemaphoreType.DMA((N_DEV - 1,)), pltpu.SemaphoreType.DMA((N_DEV - 1,)),
                                 pltpu.SemaphoreType.DMA], collective_id=seq_id)(v)[0]


def sum_slots(g, name):
    n, R, L = g.shape
    tr = 216 if R % 216 == 0 else R
    assert R % tr == 0 and tr % SUBLANES == 0

    def body(g_ref, o_ref):
        s = g_ref[0]
        for k in range(1, n):
            s = s + g_ref[k]
        o_ref[...] = s

    return _pcall(body, name=name, grid=(R // tr,), in_specs=[pl.BlockSpec((n, tr, L), lambda i: (0, i, 0))],
                  out_specs=pl.BlockSpec((tr, L), lambda i: (i, 0)), out_shape=jax.ShapeDtypeStruct((R, L), F32),
                  compiler_params=_params(("parallel",)))(g)


def allgather_chips(v, name):
    R, L = v.shape

    def body(v_ref, out_ref, send_sems, recv_sems):
        x, y, c = _me()
        chip = 2 * x + y
        out_ref[chip] = v_ref[...]
        copies = []
        for p in range(1, N_CHIPS):
            px, py = _flip(x, y, p)
            send = pltpu.make_async_remote_copy(src_ref=v_ref, dst_ref=out_ref.at[chip], send_sem=send_sems.at[p - 1],
                                                recv_sem=recv_sems.at[p - 1], device_id=(px, py, c), device_id_type=MESH)
            send.start()
            recv = pltpu.make_async_remote_copy(src_ref=v_ref, dst_ref=out_ref.at[2 * px + py], send_sem=send_sems.at[p - 1],
                                                recv_sem=recv_sems.at[p - 1], device_id=(px, py, c), device_id_type=MESH)
            copies.append((send, recv))
        for send, recv in copies:
            recv.wait_recv()
        for send, recv in copies:
            send.wait_send()

    vm = pl.BlockSpec(memory_space=pltpu.VMEM)
    return _pcall(
        body, name=name, in_specs=[vm], out_specs=vm, out_shape=jax.ShapeDtypeStruct((N_CHIPS, R, L), F32),
        scratch_shapes=[pltpu.SemaphoreType.DMA((N_CHIPS - 1,)), pltpu.SemaphoreType.DMA((N_CHIPS - 1,))],
        compiler_params=_params(),
    )(v)


def _shard_window(ref, kind, shard_shape, chip, half):
    r, c = shard_shape
    hr = r // 2
    if kind == "col":
        return ref.at[pl.ds(pl.multiple_of(half * hr, hr), hr), pl.ds(pl.multiple_of(chip * c, c), c)]
    return ref.at[pl.ds(pl.multiple_of(chip * r + half * hr, hr), hr), :]


def allgather_weights(shards, kinds, name, seq_id=None):
    n = len(shards)
    fulls = []
    for s, kind in zip(shards, kinds):
        r, c = s.shape
        fulls.append(jax.ShapeDtypeStruct((r, N_CHIPS * c) if kind == "col" else (N_CHIPS * r, c), s.dtype))

    def body(*refs):
        srcs, outs = refs[:n], refs[n:2 * n]
        send_sems, recv_sems, fsend_sems, frecv_sems = refs[2 * n:]
        x, y, c = _me()
        chip = 2 * x + y
        sib = (x, y, 1 - c)
        if seq_id is not None:
            _handshake([(*_flip(x, y, p), c) for p in range(1, N_CHIPS)] + [sib])
        sends, fwds = [], []
        for i in range(n):
            shp = srcs[i].shape
            hr = shp[0] // 2
            my_half = srcs[i].at[pl.ds(pl.multiple_of(c * hr, hr), hr), :]
            for p in range(1, N_CHIPS):
                px, py = _flip(x, y, p)
                k = i * (N_CHIPS - 1) + p - 1
                cp = pltpu.make_async_remote_copy(src_ref=my_half, dst_ref=_shard_window(outs[i], kinds[i], shp, chip, c),
                                                  send_sem=send_sems.at[k], recv_sem=recv_sems.at[k],
                                                  device_id=(px, py, c), device_id_type=MESH)
                cp.start()
                sends.append(cp)
        for i in range(n):
            shp = srcs[i].shape
            for p in range(1, N_CHIPS):
                px, py = _flip(x, y, p)
                k = i * (N_CHIPS - 1) + p - 1
                landed = _shard_window(outs[i], kinds[i], shp, 2 * px + py, c)
                pltpu.make_async_remote_copy(src_ref=landed, dst_ref=landed, send_sem=send_sems.at[k], recv_sem=recv_sems.at[k],
                                             device_id=(px, py, c), device_id_type=MESH).wait_recv()
                fw = pltpu.make_async_remote_copy(src_ref=landed, dst_ref=landed, send_sem=fsend_sems.at[k],
                                                  recv_sem=frecv_sems.at[k], device_id=sib, device_id_type=MESH)
                fw.start()
                fwds.append(fw)
        for i in range(n):
            shp = srcs[i].shape
            for p in range(1, N_CHIPS):
                px, py = _flip(x, y, p)
                k = i * (N_CHIPS - 1) + p - 1
                other = _shard_window(outs[i], kinds[i], shp, 2 * px + py, 1 - c)
                pltpu.make_async_remote_copy(src_ref=other, dst_ref=other, send_sem=fsend_sems.at[k], recv_sem=frecv_sems.at[k],
                                             device_id=sib, device_id_type=MESH).wait_recv()
        for cp in sends + fwds:
            cp.wait_send()

    nk = n * (N_CHIPS - 1)
    gathered = _hbm_comm_call(
        body, name=name, n_in=n, out_shape=fulls, seq_id=seq_id,
        sem_shapes=[pltpu.SemaphoreType.DMA((nk,)), pltpu.SemaphoreType.DMA((nk,)), pltpu.SemaphoreType.DMA((nk,)),
                    pltpu.SemaphoreType.DMA((nk,))],
    )(*shards)
    return gathered


def place_local_shards(fulls, shards, kinds, name):
    n = len(shards)
    chip = jnp.reshape(2 * lax.axis_index("x") + lax.axis_index("y"), (1,)).astype(I32)

    def body(ci_ref, *refs):
        for i in range(n):
            refs[2 * n + i][...] = refs[i][...]

    in_specs = [pl.BlockSpec(s.shape, lambda i, ci: (0, 0)) for s in shards] + [pl.BlockSpec(memory_space=pl.ANY)] * n
    out_specs = [pl.BlockSpec(s.shape, (lambda i, ci: (0, ci[0])) if k == "col" else (lambda i, ci: (ci[0], 0)))
                 for s, k in zip(shards, kinds)]
    gs = pltpu.PrefetchScalarGridSpec(num_scalar_prefetch=1, grid=(1,), in_specs=in_specs, out_specs=out_specs)
    return _pcall(body, name=name, grid_spec=gs, out_shape=[jax.ShapeDtypeStruct(f.shape, f.dtype) for f in fulls],
                  input_output_aliases={1 + n + i: i for i in range(n)}, compiler_params=_params(("arbitrary",)))(chip, *shards, *fulls)


def _as_halves(g, kind, shard_shape):
    r, c = shard_shape
    if kind == "col":
        return g.reshape(2, r // 2, N_CHIPS * c)
    return g.reshape(N_CHIPS, 2, r // 2, c)


def exchange_sibling_halves(grads, kinds, shard_shapes, name, seq_id=None):
    n = len(grads)
    views = [_as_halves(g, k, s) for g, k, s in zip(grads, kinds, shard_shapes)]
    outs = []
    for k, (r, c) in zip(kinds, shard_shapes):
        outs.append(jax.ShapeDtypeStruct((r // 2, N_CHIPS * c) if k == "col" else (N_CHIPS, r // 2, c), F32))

    def body(*refs):
        srcs, dsts = refs[:n], refs[n:2 * n]
        send_sems, recv_sems = refs[2 * n:]
        x, y, c = _me()
        if seq_id is not None:
            _handshake([(x, y, 1 - c)])
        cps = []
        for i in range(n):
            src = srcs[i].at[1 - c] if kinds[i] == "col" else srcs[i].at[:, 1 - c]
            cp = pltpu.make_async_remote_copy(src_ref=src, dst_ref=dsts[i], send_sem=send_sems.at[i], recv_sem=recv_sems.at[i],
                                              device_id=(x, y, 1 - c), device_id_type=MESH)
            cp.start()
            cps.append(cp)
        for cp in cps:
            cp.wait_recv()
        for cp in cps:
            cp.wait_send()

    return _hbm_comm_call(body, name=name, n_in=n, out_shape=outs, seq_id=seq_id,
                          sem_shapes=[pltpu.SemaphoreType.DMA((n,)), pltpu.SemaphoreType.DMA((n,))])(*views)


def add_sibling_half(g, recv, kind, shard_shape, cidx, name):
    r, c = shard_shape
    hr = r // 2
    gv = _as_halves(g, kind, shard_shape)
    tr = hr if hr <= 512 else (256 if hr % 256 == 0 else hr // 2)
    assert hr % tr == 0

    def body(ci_ref, g_ref, r_ref, h_ref, hb_ref):
        s = g_ref[...] + r_ref[...]
        h_ref[...] = s
        hb_ref[...] = s.astype(BF)

    if kind == "col":
        grid = (hr // tr, N_CHIPS)
        g_spec = pl.BlockSpec((None, tr, c), lambda i, k, ci: (ci[0], i, k))
        o_spec = pl.BlockSpec((tr, c), lambda i, k, ci: (i, k))
    else:
        grid = (hr // tr, N_CHIPS)
        g_spec = pl.BlockSpec((None, None, tr, c), lambda i, k, ci: (k, ci[0], i, 0))
        o_spec = pl.BlockSpec((None, tr, c), lambda i, k, ci: (k, i, 0))
    gs = pltpu.PrefetchScalarGridSpec(num_scalar_prefetch=1, grid=grid, in_specs=[g_spec, o_spec], out_specs=[o_spec, o_spec])
    return _pcall(
        body, name=name, grid_spec=gs,
        out_shape=[jax.ShapeDtypeStruct(recv.shape, F32), jax.ShapeDtypeStruct(recv.shape, BF)],
        compiler_params=_params(("parallel", "parallel")),
    )(cidx, gv, recv)


def exchange_chip_pieces(hbs, kinds, shard_shapes, name, seq_id=None):
    n = len(hbs)
    outs = [jax.ShapeDtypeStruct((N_CHIPS - 1, r // 2, c), BF) for (r, c) in shard_shapes]

    def body(*refs):
        srcs, dsts = refs[:n], refs[n:2 * n]
        send_sems, recv_sems = refs[2 * n:]
        x, y, c = _me()
        if seq_id is not None:
            _handshake([(*_flip(x, y, p), c) for p in range(1, N_CHIPS)])
        cps = []
        for i in range(n):
            cc = shard_shapes[i][1]
            for p in range(1, N_CHIPS):
                px, py = _flip(x, y, p)
                pchip = 2 * px + py
                src = (srcs[i].at[:, pl.ds(pl.multiple_of(pchip * cc, cc), cc)] if kinds[i] == "col" else srcs[i].at[pchip])
                k = i * (N_CHIPS - 1) + p - 1
                cp = pltpu.make_async_remote_copy(src_ref=src, dst_ref=dsts[i].at[p - 1], send_sem=send_sems.at[k],
                                                  recv_sem=recv_sems.at[k], device_id=(px, py, c), device_id_type=MESH)
                cp.start()
                cps.append(cp)
        for cp in cps:
            cp.wait_recv()
        for cp in cps:
            cp.wait_send()

    nk = n * (N_CHIPS - 1)
    return _hbm_comm_call(body, name=name, n_in=n, out_shape=outs, seq_id=seq_id,
                          sem_shapes=[pltpu.SemaphoreType.DMA((nk,)), pltpu.SemaphoreType.DMA((nk,))])(*hbs)


def sum_chip_pieces(h, pieces, kind, shard_shape, chip_core, name):
    r, c = shard_shape
    hr = r // 2
    tr = hr if hr <= 512 else (256 if hr % 256 == 0 else hr // 2)
    assert hr % tr == 0
    nrb = hr // tr

    def body(ci_ref, h_ref, p_ref, q_ref):
        q_ref[...] = ((h_ref[...] + p_ref[0].astype(F32)) + p_ref[1].astype(F32)) + p_ref[2].astype(F32)

    if kind == "col":
        h_spec = pl.BlockSpec((tr, c), lambda i, ci: (i, ci[0]))
    else:
        h_spec = pl.BlockSpec((None, tr, c), lambda i, ci: (ci[0], i, 0))
    gs = pltpu.PrefetchScalarGridSpec(
        num_scalar_prefetch=1, grid=(nrb,),
        in_specs=[h_spec, pl.BlockSpec((N_CHIPS - 1, tr, c), lambda i, ci: (0, i, 0))],
        out_specs=pl.BlockSpec((tr, c), lambda i, ci: (ci[1] * nrb + i, 0)))
    return _pcall(body, name=name, grid_spec=gs, out_shape=jax.ShapeDtypeStruct((r, c), F32),
                  compiler_params=_params(("parallel",)))(chip_core, h, pieces)


def exchange_reduced_halves(qs, name):
    n = len(qs)

    def body(*refs):
        bufs = refs[n:2 * n]
        send_sems, recv_sems = refs[2 * n:]
        x, y, c = _me()
        cps = []
        for i in range(n):
            hr = bufs[i].shape[0] // 2
            mine = bufs[i].at[pl.ds(pl.multiple_of(c * hr, hr), hr), :]
            other = bufs[i].at[pl.ds(pl.multiple_of((1 - c) * hr, hr), hr), :]
            cp = pltpu.make_async_remote_copy(src_ref=mine, dst_ref=mine, send_sem=send_sems.at[i], recv_sem=recv_sems.at[i],
                                              device_id=(x, y, 1 - c), device_id_type=MESH)
            cp.start()
            cps.append((cp, pltpu.make_async_remote_copy(src_ref=other, dst_ref=other, send_sem=send_sems.at[i],
                                                         recv_sem=recv_sems.at[i], device_id=(x, y, 1 - c), device_id_type=MESH)))
        for cp, rv in cps:
            rv.wait_recv()
        for cp, rv in cps:
            cp.wait_send()

    anyspec = pl.BlockSpec(memory_space=pl.ANY)
    return _pcall(
        body, name=name, in_specs=[anyspec] * n, out_specs=[anyspec] * n,
        out_shape=[jax.ShapeDtypeStruct(q.shape, F32) for q in qs], input_output_aliases={i: i for i in range(n)},
        scratch_shapes=[pltpu.SemaphoreType.DMA((n,)), pltpu.SemaphoreType.DMA((n,))],
        compiler_params=_params(),
    )(*qs)


def _rows128(a):
    return a.reshape(-1, LANES)


def _after(xs, *deps):
    flat = []
    for d in deps:
        flat.extend(d if isinstance(d, (list, tuple)) else [d])
    return list(lax.optimization_barrier((tuple(xs), tuple(flat)))[0])


def _block_diag(w):
    H, d, _ = w.shape
    eye = jnp.eye(H, dtype=w.dtype)
    return jnp.einsum("hde,hg->hdge", w, eye).reshape(H * d, H * d)


def _diag_blocks(g4, H, d):
    nb = g4.shape[0]
    per = LANES // d
    g = g4.reshape(nb, per, d, per, d)
    return jnp.stack([g[:, j, :, j, :] for j in range(per)], axis=1).reshape(H, d, d)


def kernel(x, c, w_mod, b_mod, g_ffn1, w_ffn1_in, w_ffn1_out, g_mix, w_in, conv_w, conv_b, ln_g, ln_b, rnn_conv_w, rnn_conv_b, w_a, b_a, w_i, b_i, lru_lambda, w_out, g_ffn2, w_ffn2_in, w_ffn2_out, w_fmod, b_fmod, g_final, loss_target, m_w_mod, m_b_mod, m_g_ffn1, m_w_ffn1_in, m_w_ffn1_out, m_g_mix, m_w_in, m_conv_w, m_conv_b, m_ln_g, m_ln_b, m_rnn_conv_w, m_rnn_conv_b, m_w_a, m_b_a, m_w_i, m_b_i, m_lru_lambda, m_w_out, m_g_ffn2, m_w_ffn2_in, m_w_ffn2_out, m_w_fmod, m_b_fmod, m_g_final, v_w_mod, v_b_mod, v_g_ffn1, v_w_ffn1_in, v_w_ffn1_out, v_g_mix, v_w_in, v_conv_w, v_conv_b, v_ln_g, v_ln_b, v_rnn_conv_w, v_rnn_conv_b, v_w_a, v_b_a, v_w_i, v_b_i, v_lru_lambda, v_w_out, v_g_ffn2, v_w_ffn2_in, v_w_ffn2_out, v_w_fmod, v_b_fmod, v_g_final):
    S, D = x.shape[1], x.shape[2]
    M = conv_b.shape[1]
    H, HD = w_a.shape[1], w_a.shape[2]
    nb = M // LANES
    ix, iy, ic = lax.axis_index("x"), lax.axis_index("y"), lax.axis_index("c")
    chip = 2 * ix + iy
    dev = 2 * chip + ic
    cidx = jnp.reshape(ic, (1,)).astype(I32)
    chip_core = jnp.stack([chip, ic]).astype(I32)
    xs = x[0]
    tgt = loss_target[0]

    kinds = ["col", "row"]
    w_f1, w_mx, w_f2 = [w_ffn1_in[0], w_ffn1_out[0]], [w_in[0], w_out[0]], [w_ffn2_in[0], w_ffn2_out[0]]
    as_bf = lambda ws: [w.astype(BF) for w in ws]
    shapes_of = lambda ws: [w.shape for w in ws]
    b_f1, b_mx, b_f2 = as_bf(w_f1), as_bf(w_mx), as_bf(w_f2)
    got_f1 = allgather_weights(b_f1, kinds, "gather_ffn1", seq_id=9)
    got_mx = allgather_weights(b_mx, kinds, "gather_mix", seq_id=1)
    got_f2 = allgather_weights(b_f2, kinds, "gather_ffn2", seq_id=2)

    c_all =allgather_devices(_rows128(c), "gather_c")[0].reshape(N_DEV, D)
    mod_cols = cond_matmul(c_all, w_mod[0], "mod_proj")
    fmod_cols = cond_matmul(c_all, w_fmod, "fmod_proj")
    convw_pad = jnp.pad(conv_w[0], ((0, 32 - CONV_WIDTH), (0, 0)))
    rnnw_pad = jnp.pad(rnn_conv_w[0], ((0, SUBLANES - RNN_CONV_WIDTH), (0, 0)))
    n_mod, n_fmod = mod_cols.shape[1], fmod_cols.shape[1]
    small = jnp.concatenate([_rows128(mod_cols), _rows128(fmod_cols), convw_pad, rnnw_pad], axis=0)
    small4 = allgather_chips(small, "gather_cond")
    r0 = N_DEV * n_mod // LANES
    r1 = r0 + N_DEV * n_fmod // LANES
    mod_all = small4[:, :r0].reshape(N_CHIPS, N_DEV, n_mod)
    fmod_all = small4[:, r0:r1].reshape(N_CHIPS, N_DEV, n_fmod)
    convw4 = small4[:, r1:r1 + 32]
    rnnw4 = small4[:, r1 + 32:r1 + 32 + SUBLANES]
    mod_row = lax.dynamic_index_in_dim(mod_all, dev, axis=1, keepdims=False).reshape(1, N_CHIPS * n_mod) + b_mod
    fmod_row = lax.dynamic_index_in_dim(fmod_all, dev, axis=1, keepdims=False).reshape(1, N_CHIPS * n_fmod) + b_fmod[None, :]
    vecs = jnp.concatenate([mod_row.reshape(9, D), fmod_row.reshape(2, D), g_ffn1, g_mix, g_ffn2, g_final[None, :],
                            jnp.zeros((1, D), F32)], axis=0)
    lnv = jnp.concatenate([ln_g, ln_b, jnp.zeros((SUBLANES - 2, M), F32)], axis=0)
    bda = _block_diag(w_a[0]).astype(BF)
    bdi = _block_diag(w_i[0]).astype(BF)

    def reduce_add(gs, recv, ws, tag, kinds_=kinds):
        pairs = [add_sibling_half(g, r_, k, w.shape, cidx, f"add_sibling_{tag}{j}")
                 for j, (g, r_, k, w) in enumerate(zip(gs, recv, kinds_, ws))]
        return [p[0] for p in pairs], [p[1] for p in pairs]

    def reduce_sum(hs_, recv, ws, tag, kinds_=kinds):
        return [sum_chip_pieces(h_, p_, k, w.shape, chip_core, f"sum_chips_{tag}{j}")
                for j, (h_, p_, k, w) in enumerate(zip(hs_, recv, kinds_, ws))]

    rows1 = (R_SH1, R_SC1, R_GT1, R_G1)
    rows3 = (R_SH3, R_SC3, R_GT3, R_G3)
    wi1, wo1 = place_local_shards(got_f1, b_f1, kinds, "place_ffn1")
    x1, g1s, u1s, y1 = ffn_fwd(xs, vecs, wi1, wo1, rows1, "ffn1_fwd")
    win, wout = place_local_shards(_after(got_mx, x1), b_mx, kinds, "place_mix")
    proj = norm_matmul(x1, vecs, win, (R_SH2, R_SC2, R_G2), "mix_in_proj")
    cq = conv_fwd(proj, convw4, conv_b, "conv_fwd")
    xr, ra, ii, hh = rnn_fwd(proj, rnnw4, rnn_conv_b, bda, bdi, b_a, b_i, lru_lambda, "rnn_fwd")
    x2, ym, ycat = mix_out(cq, proj, hh, x1, vecs, lnv, wout, "mix_out")
    wi2, wo2 = place_local_shards(_after(got_f2, x2), b_f2, kinds, "place_ffn2")
    x3, g2s, u2s, y2 = ffn_fwd(x2, vecs, wi2, wo2, rows3, "ffn2_fwd")
    dx3, vgf = final_fwd_bwd(x3, tgt, vecs, "final_loss")

    Fd = wo1.shape[0]
    tk = min(1024, S)
    dx2, act2, dg2, du2, h3b, dy2b, vg3 = ffn_bwd(dx3, x2, vecs, g2s, u2s, y2, wi2, wo2, rows3, "ffn2_bwd")
    gwo2 = matmul(act2, dy2b, "tn", tm=Fd // 2, tn=D, tk=tk, name="ffn2_dwo")
    gwi2 = matmul(h3b, dg2, "tn", tm=D, tn=Fd // 2, tk=tk, name="ffn2_dwg", out_cols=2 * Fd)
    gwi2 = matmul(h3b, du2, "tn", tm=D, tn=Fd // 2, tk=tk, name="ffn2_dwu", out_cols=2 * Fd, col_off=Fd, prev=gwi2)
    recv1_f2 = exchange_sibling_halves([gwi2, gwo2], kinds, shapes_of(w_f2), "reduce1_ffn2", seq_id=3)
    dcq, dhout, duy, dymb, vgd, vgm = mix_out_bwd(dx2, ym, vecs, wout, cq, lnv, proj, hh, "mix_out_bwd")
    gwout = matmul(ycat, dymb, "tn", tm=2 * M, tn=D, tk=tk, name="mix_dwout")
    recv1_f2 = _after(recv1_f2, gwout)
    h_f2, hb_f2 = reduce_add([gwi2, gwo2], recv1_f2, w_f2, "ffn2_")
    recv2_f2 = exchange_chip_pieces(hb_f2, kinds, shapes_of(w_f2), "reduce2_ffn2", seq_id=4)
    duv, dug, dconvw4, dconvb = conv_bwd(_after([dcq], hb_f2)[0], proj, convw4, "conv_bwd")
    dux, dwa4, dwi4, drnnw4, rvec = rnn_bwd(dhout, hh, xr, ra, ii, proj, rnnw4, bda, bdi, lru_lambda, "rnn_bwd")
    dx1, h2b, dpb, vg2 = mix_in_bwd((duv, dug, dux, duy), x1, dx2, vecs, win, "mix_in_bwd")
    gwin = matmul(h2b, dpb, "tn", tm=D, tn=1024, tk=tk, name="mix_dwin")
    recv1_mx = exchange_sibling_halves([gwin, gwout], kinds, shapes_of(w_mx), "reduce1_mix", seq_id=5)
    g_f2 = exchange_reduced_halves(reduce_sum(_after(h_f2, gwin), recv2_f2, w_f2, "ffn2_"), "reduce3_ffn2")
    adam_f2 = [adam_big(w, g, m, v, "adam_" + nm) for w, g, m, v, nm in
               zip(w_f2, g_f2, [m_w_ffn2_in[0], m_w_ffn2_out[0]], [v_w_ffn2_in[0], v_w_ffn2_out[0]], ["ffn2_in", "ffn2_out"])]
    h_mx, hb_mx = reduce_add([gwin, gwout], _after(recv1_mx, adam_f2[0][0], adam_f2[1][0]), w_mx, "mix_")
    recv2_mx = exchange_chip_pieces(hb_mx, kinds, shapes_of(w_mx), "reduce2_mix", seq_id=6)
    dx0, act1, dg1, du1, h1b, dy1b, vg1 = ffn_bwd(_after([dx1], hb_mx)[0], xs, vecs, g1s, u1s, y1, wi1, wo1, rows1, "ffn1_bwd")
    dmod_row = jnp.concatenate([vg1[1:3], vg1[0:1], vg2[0:2], vgd[0:1], vg3[1:3], vg3[0:1]], axis=0)
    gains = jnp.concatenate([vg1[3:4], vg2[2:3], vg3[3:4], vgf[2:4]], axis=0)
    mvecs = jnp.concatenate([dconvb, vgm[0:2], rvec[0:4], jnp.zeros((1, M), F32)], axis=0)
    parts = [_rows128(dmod_row), _rows128(vgf[0:2]), _rows128(gains), _rows128(mvecs),
             _rows128(dconvw4), _rows128(drnnw4), _rows128(_diag_blocks(dwa4, H, HD)), _rows128(_diag_blocks(dwi4, H, HD))]
    sizes = [p.shape[0] for p in parts]
    packed = jnp.concatenate(parts, axis=0)
    gathered = allgather_devices_hbm(packed, "gather_small", seq_id=10)

    gwo1 = matmul(_after([act1], recv2_mx, packed)[0], dy1b, "tn", tm=Fd // 2, tn=D, tk=tk, name="ffn1_dwo")
    w_f1o, w_f1i = w_f1[1:], w_f1[:1]
    recv1_f1o = exchange_sibling_halves([gwo1], ["row"], shapes_of(w_f1o), "reduce1_ffn1_out", seq_id=7)
    gwi1 = matmul(h1b, dg1, "tn", tm=D, tn=Fd // 2, tk=tk, name="ffn1_dwg", out_cols=2 * Fd)
    h_f1o, hb_f1o = reduce_add([gwo1], _after(recv1_f1o, gwi1), w_f1o, "ffn1_out", ["row"])
    recv2_f1o = exchange_chip_pieces(hb_f1o, ["row"], shapes_of(w_f1o), "reduce2_ffn1_out", seq_id=11)
    gwi1 = matmul(h1b, _after([du1], hb_f1o, gathered)[0], "tn", tm=D, tn=Fd // 2, tk=tk, name="ffn1_dwu", out_cols=2 * Fd,
                  col_off=Fd, prev=gwi1)
    recv1_f1i = exchange_sibling_halves([gwi1], ["col"], shapes_of(w_f1i), "reduce1_ffn1_in", seq_id=12)
    g_mx = exchange_reduced_halves(reduce_sum(h_mx, recv2_mx, w_mx, "mix_"), "reduce3_mix")
    summed = sum_slots(gathered, "sum_small")
    offs = [0]
    for s in sizes:
        offs.append(offs[-1] + s)
    seg = lambda k: summed[offs[k]:offs[k + 1]]
    g_b_mod = seg(0).reshape(1, 9 * D)
    g_b_fmod = seg(1).reshape(1, 2 * D)
    gsum = seg(2).reshape(5, D)
    loss = (0.5 / D) * jnp.sum(gsum[4])
    msum = seg(3).reshape(SUBLANES, M)
    g_conv_w = lax.dynamic_index_in_dim(seg(4).reshape(nb, 32, LANES), chip, axis=0, keepdims=False)[:CONV_WIDTH]
    g_rnn_w = lax.dynamic_index_in_dim(seg(5).reshape(nb, SUBLANES, LANES), chip, axis=0, keepdims=False)[:RNN_CONV_WIDTH]
    g_w_a = seg(6).reshape(H, HD, HD)
    g_w_i = seg(7).reshape(H, HD, HD)
    dmod_all = gathered[:, offs[0]:offs[1]].reshape(N_DEV, 9 * D)
    dfmod_all = gathered[:, offs[1]:offs[2]].reshape(N_DEV, 2 * D)
    dmod_cols = lax.dynamic_slice_in_dim(dmod_all, chip * n_mod, n_mod, axis=1)
    dfmod_cols = lax.dynamic_slice_in_dim(dfmod_all, chip * n_fmod, n_fmod, axis=1)

    h_f1i, hb_f1i = reduce_add([gwi1], _after(recv1_f1i, recv2_f1o), w_f1i, "ffn1_in", ["col"])
    recv2_f1i = exchange_chip_pieces(hb_f1i, ["col"], shapes_of(w_f1i), "reduce2_ffn1_in", seq_id=8)
    g_f1o = exchange_reduced_halves(reduce_sum(h_f1o, recv2_f1o, w_f1o, "ffn1_out", ["row"]), "reduce3_ffn1_out")
    g_f1i = exchange_reduced_halves(reduce_sum(h_f1i, recv2_f1i, w_f1i, "ffn1_in", ["col"]), "reduce3_ffn1_in")
    g_f1 = list(g_f1i) + list(g_f1o)

    big_w = w_f1 + w_mx + w_f2
    g_big = g_f1 + list(g_mx) + list(g_f2)
    names = ["ffn1_in", "ffn1_out", "w_in", "w_out", "ffn2_in", "ffn2_out"]
    big_m = [m_w_ffn1_in[0], m_w_ffn1_out[0], m_w_in[0], m_w_out[0]]
    big_v = [v_w_ffn1_in[0], v_w_ffn1_out[0], v_w_in[0], v_w_out[0]]
    big_out = [adam_big(w, g, m, v, "adam_" + nm) for w, g, m, v, nm in zip(big_w, g_big, big_m, big_v, names)] + adam_f2
    g_w_mod, d_w_mod, nm_w_mod, nv_w_mod = adam_cond(c_all, dmod_cols, w_mod[0], m_w_mod[0], v_w_mod[0], "adam_w_mod")
    g_w_fmod, d_w_fmod, nm_w_fmod, nv_w_fmod = adam_cond(c_all, dfmod_cols, w_fmod, m_w_fmod, v_w_fmod, "adam_w_fmod")

    flat2 = lambda a: a.reshape(-1, a.shape[-1])
    small_names = ["b_mod", "g_ffn1", "g_mix", "conv_w", "conv_b", "ln_g", "ln_b", "rnn_conv_w", "rnn_conv_b", "w_a", "b_a",
                   "w_i", "b_i", "lru_lambda", "g_ffn2", "b_fmod", "g_final"]
    small_w = [b_mod, g_ffn1, g_mix, conv_w, conv_b, ln_g, ln_b, rnn_conv_w, rnn_conv_b, w_a, b_a, w_i, b_i, lru_lambda,
               g_ffn2, b_fmod, g_final]
    small_m = [m_b_mod, m_g_ffn1, m_g_mix, m_conv_w, m_conv_b, m_ln_g, m_ln_b, m_rnn_conv_w, m_rnn_conv_b, m_w_a, m_b_a,
               m_w_i, m_b_i, m_lru_lambda, m_g_ffn2, m_b_fmod, m_g_final]
    small_v = [v_b_mod, v_g_ffn1, v_g_mix, v_conv_w, v_conv_b, v_ln_g, v_ln_b, v_rnn_conv_w, v_rnn_conv_b, v_w_a, v_b_a,
               v_w_i, v_b_i, v_lru_lambda, v_g_ffn2, v_b_fmod, v_g_final]
    small_g = [g_b_mod, gsum[0:1], gsum[1:2], g_conv_w, msum[0:1], msum[1:2], msum[2:3], g_rnn_w, msum[3:4], g_w_a, msum[4:5],
               g_w_i, msum[5:6], msum[6:7], gsum[2:3], g_b_fmod, gsum[3:4]]
    small_g = [g.reshape(w.shape) for g, w in zip(small_g, small_w)]
    two_d = lambda a: a.reshape(1, -1) if a.ndim == 1 else flat2(a)
    sd, sm, sv = adam_small([two_d(a) for a in small_w], [two_d(a) for a in small_g], [two_d(a) for a in small_m],
                            [two_d(a) for a in small_v], "adam_small")
    small = {}
    for k, nm in enumerate(small_names):
        shp = small_w[k].shape
        small[nm] = (small_g[k], sd[k].reshape(shp), sm[k].reshape(shp), sv[k].reshape(shp))

    big = {"w_mod": tuple(a[None] for a in (g_w_mod, d_w_mod, nm_w_mod, nv_w_mod)),
           "w_fmod": (g_w_fmod, d_w_fmod, nm_w_fmod, nv_w_fmod)}
    for nm, full, g, (d, nmm, nvv) in zip(["w_ffn1_in", "w_ffn1_out", "w_in", "w_out", "w_ffn2_in", "w_ffn2_out"],
                                         big_w, g_big, big_out):
        big[nm] = tuple(a[None] for a in (g, d, nmm, nvv))
    order = ["w_mod", "b_mod", "g_ffn1", "w_ffn1_in", "w_ffn1_out", "g_mix", "w_in", "conv_w", "conv_b", "ln_g", "ln_b",
             "rnn_conv_w", "rnn_conv_b", "w_a", "b_a", "w_i", "b_i", "lru_lambda", "w_out", "g_ffn2", "w_ffn2_in",
             "w_ffn2_out", "w_fmod", "b_fmod", "g_final"]
    table = {**small, **big}
    outs = [loss, dx0[None]]
    for kind_ in range(4):
        outs.extend(table[nm][kind_] for nm in order)
    return tuple(outs)
```

```python
import functools

import jax
import jax.numpy as jnp
from jax import lax
from jax.experimental import pallas as pl
from jax.experimental.pallas import tpu as pltpu
from jax.experimental.pallas import tpu_sc as plsc

F32 = jnp.float32
BF = jnp.bfloat16
I32 = jnp.int32
MESH = pl.DeviceIdType.MESH

EPS = 1e-6
RG_C = 8.0
MACARON_W = 0.5
CONV_WIDTH = 31
RNN_CONV_WIDTH = 4
ADAM_LR = 0.001
ADAM_B1 = 0.9
ADAM_B2 = 0.999
ADAM_EPS = 1e-08
ADAM_WD = 0.01
ADAM_STEP = 10

LANES = 128
SUBLANES = 8
VMEM_LIMIT = 62 * 1024 * 1024
N_CHIPS = 4
N_DEV = 8

R_SH1, R_SC1, R_GT1, R_SH2, R_SC2, R_GT2, R_SH3, R_SC3, R_GT3, R_FSH, R_FSC, R_G1, R_G2, R_G3, R_GF = range(15)

CONTRACT_LAST = (((1,), (1,)), ((), ()))
CONTRACT_FIRST = (((0,), (0,)), ((), ()))


def _pcall(body, **kw):
    return pl.pallas_call(body, **kw)


def _params(sem=None, vmem=VMEM_LIMIT):
    if sem is None:
        return pltpu.CompilerParams(vmem_limit_bytes=vmem)
    return pltpu.CompilerParams(dimension_semantics=sem, vmem_limit_bytes=vmem)


def _row(ref, r):
    return ref[r:r + 1, :]


def _sigmoid(x):
    return 1.0 / (1.0 + jnp.exp(-x))


def _colsum(x):
    return jnp.sum(x, axis=0, keepdims=True)


def _rowmean(x):
    return jnp.mean(x, axis=-1, keepdims=True)


def matmul(a, b, mode, *, tm, tn, tk, name, out_dtype=F32, out_cols=None, col_off=0, prev=None):
    if mode == "nn":
        (M, K), (K2, N) = a.shape, b.shape
    elif mode == "nt":
        (M, K), (N, K2) = a.shape, b.shape
    else:
        (K, M), (K2, N) = a.shape, b.shape
    assert K == K2 and M % tm == 0 and N % tn == 0 and K % tk == 0 and col_off % tn == 0
    nk = K // tk
    out_cols = N if out_cols is None else out_cols
    off = col_off // tn

    def body(*refs):
        if prev is None:
            a_ref, b_ref, o_ref, acc = refs
        else:
            a_ref, b_ref, _, o_ref, acc = refs
        k = pl.program_id(2)

        @pl.when(k == 0)
        def _():
            acc[...] = jnp.zeros_like(acc)

        av = a_ref[...].astype(BF)
        bv = b_ref[...].astype(BF)
        if mode == "nn":
            acc[...] += jnp.dot(av, bv, preferred_element_type=F32)
        elif mode == "nt":
            acc[...] += lax.dot_general(av, bv, CONTRACT_LAST, preferred_element_type=F32)
        else:
            acc[...] += lax.dot_general(av, bv, CONTRACT_FIRST, preferred_element_type=F32)

        @pl.when(k == nk - 1)
        def _():
            o_ref[...] = acc[...].astype(out_dtype)

    if mode == "nn":
        a_spec = pl.BlockSpec((tm, tk), lambda m, n, k: (m, k))
        b_spec = pl.BlockSpec((tk, tn), lambda m, n, k: (k, n))
    elif mode == "nt":
        a_spec = pl.BlockSpec((tm, tk), lambda m, n, k: (m, k))
        b_spec = pl.BlockSpec((tn, tk), lambda m, n, k: (n, k))
    else:
        a_spec = pl.BlockSpec((tk, tm), lambda m, n, k: (k, m))
        b_spec = pl.BlockSpec((tk, tn), lambda m, n, k: (k, n))
    in_specs = [a_spec, b_spec]
    args = [a, b]
    aliases = {}
    if prev is not None:
        in_specs.append(pl.BlockSpec(memory_space=pl.ANY))
        args.append(prev)
        aliases = {2: 0}
    return _pcall(
        body, name=name, grid=(M // tm, N // tn, nk), in_specs=in_specs,
        out_specs=pl.BlockSpec((tm, tn), lambda m, n, k: (m, n + off)),
        out_shape=jax.ShapeDtypeStruct((M, out_cols), out_dtype),
        scratch_shapes=[pltpu.VMEM((tm, tn), F32)], input_output_aliases=aliases,
        compiler_params=_params(("parallel", "parallel", "arbitrary")),
    )(*args)


def cond_matmul(c_all, w, name):
    B, K = c_all.shape
    N = w.shape[1]
    tn = 256
    assert N % tn == 0

    def body(c_ref, w_ref, o_ref):
        cv = c_ref[...]
        ca = cv * _sigmoid(cv)
        o_ref[...] = jnp.dot(ca, w_ref[...], preferred_element_type=F32, precision=lax.Precision.HIGHEST)

    return _pcall(
        body, name=name, grid=(N // tn,),
        in_specs=[pl.BlockSpec((B, K), lambda n: (0, 0)), pl.BlockSpec((K, tn), lambda n: (0, n))],
        out_specs=pl.BlockSpec((B, tn), lambda n: (0, n)),
        out_shape=jax.ShapeDtypeStruct((B, N), F32), compiler_params=_params(("parallel",)),
    )(c_all, w)


FFN_FWD_TS = 512
FFN_BWD_TS = 256


def _resident(shape, index_map):
    return pl.BlockSpec(shape, index_map, pipeline_mode=pl.Buffered(1))


def ffn_fwd(x, vecs, wi, wo, rows, name):
    r_sh, r_sc, r_gt, r_g = rows
    S, D = x.shape
    Fd = wo.shape[0]
    ts = min(FFN_FWD_TS, S)

    def body(x_ref, v_ref, wg_ref, wu_ref, wo_ref, xo_ref, g_ref, u_ref, y_ref):
        xv = x_ref[...]
        r = lax.rsqrt(_rowmean(xv * xv) + EPS)
        gs = _row(v_ref, r_g) * (1.0 + _row(v_ref, r_sc))
        hb = (xv * r * gs + _row(v_ref, r_sh)).astype(BF)
        G = jnp.dot(hb, wg_ref[...], preferred_element_type=F32)
        U = jnp.dot(hb, wu_ref[...], preferred_element_type=F32)
        g_ref[...] = G.astype(BF)
        u_ref[...] = U.astype(BF)
        act = (G * _sigmoid(G) * U).astype(BF)
        Y = jnp.dot(act, wo_ref[...], preferred_element_type=F32)
        y_ref[...] = Y
        xo_ref[...] = xv + (MACARON_W * _row(v_ref, r_gt)) * Y

    tok = pl.BlockSpec((ts, D), lambda i: (i, 0))
    hid = pl.BlockSpec((ts, Fd), lambda i: (i, 0))
    return _pcall(
        body, name=name, grid=(S // ts,),
        in_specs=[tok, pl.BlockSpec(vecs.shape, lambda i: (0, 0)), _resident((D, Fd), lambda i: (0, 0)),
                  _resident((D, Fd), lambda i: (0, 1)), _resident((Fd, D), lambda i: (0, 0))],
        out_specs=[tok, hid, hid, tok],
        out_shape=[jax.ShapeDtypeStruct((S, D), F32), jax.ShapeDtypeStruct((S, Fd), BF),
                   jax.ShapeDtypeStruct((S, Fd), BF), jax.ShapeDtypeStruct((S, D), F32)],
        compiler_params=_params(("arbitrary",)),
    )(x, vecs, wi, wi, wo)


def ffn_bwd(dxo, x, vecs, gs_, us_, y, wi, wo, rows, name):
    r_sh, r_sc, r_gt, r_g = rows
    S, D = x.shape
    Fd = wo.shape[0]
    ts = min(FFN_BWD_TS, S)

    def body(dxo_ref, x_ref, v_ref, g_ref, u_ref, y_ref, wg_ref, wu_ref, wo_ref,
             dx_ref, act_ref, dg_ref, du_ref, hb_ref, dyb_ref, vg_ref):
        @pl.when(pl.program_id(0) == 0)
        def _():
            vg_ref[...] = jnp.zeros_like(vg_ref)

        dxo_v = dxo_ref[...]
        dyb = ((MACARON_W * _row(v_ref, r_gt)) * dxo_v).astype(BF)
        dyb_ref[...] = dyb
        vg_ref[0:1, :] += MACARON_W * _colsum(dxo_v * y_ref[...])
        dA = lax.dot_general(dyb, wo_ref[...], CONTRACT_LAST, preferred_element_type=F32)
        G = g_ref[...].astype(F32)
        U = u_ref[...].astype(F32)
        sg = _sigmoid(G)
        sl = G * sg
        dU = (dA * sl).astype(BF)
        dG = (dA * U * (sg * (1.0 + G * (1.0 - sg)))).astype(BF)
        act_ref[...] = (sl * U).astype(BF)
        dg_ref[...] = dG
        du_ref[...] = dU
        dh = (lax.dot_general(dG, wg_ref[...], CONTRACT_LAST, preferred_element_type=F32)
              + lax.dot_general(dU, wu_ref[...], CONTRACT_LAST, preferred_element_type=F32))
        xv = x_ref[...]
        r = lax.rsqrt(_rowmean(xv * xv) + EPS)
        n = xv * r
        g = _row(v_ref, r_g)
        sc1 = 1.0 + _row(v_ref, r_sc)
        gsc = g * sc1
        hb_ref[...] = (n * gsc + _row(v_ref, r_sh)).astype(BF)
        dhn = dh * n
        vg_ref[1:2, :] += _colsum(dh)
        vg_ref[2:3, :] += _colsum(dhn) * g
        vg_ref[3:4, :] += _colsum(dhn) * sc1
        dn = dh * gsc
        dx_ref[...] = dxo_v + r * (dn - n * _rowmean(dn * n))

    tok = pl.BlockSpec((ts, D), lambda i: (i, 0))
    hid = pl.BlockSpec((ts, Fd), lambda i: (i, 0))
    return _pcall(
        body, name=name, grid=(S // ts,),
        in_specs=[tok, tok, pl.BlockSpec(vecs.shape, lambda i: (0, 0)), hid, hid, tok, _resident((D, Fd), lambda i: (0, 0)),
                  _resident((D, Fd), lambda i: (0, 1)), _resident((Fd, D), lambda i: (0, 0))],
        out_specs=[tok, hid, hid, hid, tok, tok, pl.BlockSpec((SUBLANES, D), lambda i: (0, 0))],
        out_shape=[jax.ShapeDtypeStruct((S, D), F32), jax.ShapeDtypeStruct((S, Fd), BF),
                   jax.ShapeDtypeStruct((S, Fd), BF), jax.ShapeDtypeStruct((S, Fd), BF),
                   jax.ShapeDtypeStruct((S, D), BF), jax.ShapeDtypeStruct((S, D), BF),
                   jax.ShapeDtypeStruct((SUBLANES, D), F32)],
        compiler_params=_params(("arbitrary",)),
    )(dxo, x, vecs, gs_, us_, y, wi, wi, wo)


def final_fwd_bwd(x, tgt, vecs, name):
    S, D = x.shape
    ts = min(512, S)

    def body(x_ref, t_ref, v_ref, dx_ref, vg_ref):
        @pl.when(pl.program_id(0) == 0)
        def _():
            vg_ref[...] = jnp.zeros_like(vg_ref)

        xv = x_ref[...]
        r = lax.rsqrt(_rowmean(xv * xv) + EPS)
        n = xv * r
        g = _row(v_ref, R_GF)
        sc1 = 1.0 + _row(v_ref, R_FSC)
        gsc = g * sc1
        e = n * gsc + _row(v_ref, R_FSH) - t_ref[...]
        vg_ref[3:4, :] += _colsum(e * e)
        dout = e * (1.0 / D)
        dn_ = dout * n
        vg_ref[0:1, :] += _colsum(dout)
        vg_ref[1:2, :] += _colsum(dn_) * g
        vg_ref[2:3, :] += _colsum(dn_) * sc1
        dn = dout * gsc
        dx_ref[...] = r * (dn - n * _rowmean(dn * n))

    tok = pl.BlockSpec((ts, D), lambda i: (i, 0))
    return _pcall(
        body, name=name, grid=(S // ts,),
        in_specs=[tok, tok, pl.BlockSpec(vecs.shape, lambda i: (0, 0))],
        out_specs=[tok, pl.BlockSpec((SUBLANES, D), lambda i: (0, 0))],
        out_shape=[jax.ShapeDtypeStruct((S, D), F32), jax.ShapeDtypeStruct((SUBLANES, D), F32)],
        compiler_params=_params(("arbitrary",)),
    )(x, tgt, vecs)


def norm_matmul(x, vecs, w, rows, name):
    r_sh, r_sc, r_g = rows
    S, D = x.shape
    N = w.shape[1]
    ts, tn = min(512, S), 1024
    nn = N // tn

    def body(x_ref, v_ref, w_ref, o_ref, h_sc):
        @pl.when(pl.program_id(1) == 0)
        def _():
            xv = x_ref[...]
            r = lax.rsqrt(_rowmean(xv * xv) + EPS)
            gs = _row(v_ref, r_g) * (1.0 + _row(v_ref, r_sc))
            h_sc[...] = (xv * r * gs + _row(v_ref, r_sh)).astype(BF)

        o_ref[...] = jnp.dot(h_sc[...], w_ref[...], preferred_element_type=F32)

    return _pcall(
        body, name=name, grid=(S // ts, nn),
        in_specs=[pl.BlockSpec((ts, D), lambda i, n: (i, 0)), pl.BlockSpec(vecs.shape, lambda i, n: (0, 0)),
                  pl.BlockSpec((D, tn), lambda i, n: (0, n))],
        out_specs=pl.BlockSpec((ts, tn), lambda i, n: (i, n)),
        out_shape=jax.ShapeDtypeStruct((S, N), F32),
        scratch_shapes=[pltpu.VMEM((ts, D), BF)],
        compiler_params=_params(("arbitrary", "arbitrary")),
    )(x, vecs, w)


SEQ_TT = 256
CONV_PAD = 32


def conv_fwd(proj, convw4, conv_b, name):
    S = proj.shape[0]
    M = conv_b.shape[1]
    nb = M // LANES
    tt = min(SEQ_TT, S)

    def body(uv_ref, ug_ref, w_ref, b_ref, cq_ref, qp):
        qp[0:CONV_PAD, :] = jnp.zeros((CONV_PAD, LANES), F32)

        def step(t, carry):
            base = pl.multiple_of(t * tt, tt)
            qp[pl.ds(base + CONV_PAD, tt), :] = uv_ref[pl.ds(base, tt), :] * _sigmoid(ug_ref[pl.ds(base, tt), :])
            acc = jnp.broadcast_to(b_ref[...], (tt, LANES))
            for k in range(CONV_WIDTH):
                acc = acc + w_ref[k:k + 1, :] * qp[pl.ds(base + (CONV_PAD - CONV_WIDTH + 1) + k, tt), :]
            cq_ref[pl.ds(base, tt), :] = acc
            return carry

        lax.fori_loop(0, S // tt, step, 0)

    return _pcall(
        body, name=name, grid=(nb,),
        in_specs=[pl.BlockSpec((S, LANES), lambda c: (0, c)), pl.BlockSpec((S, LANES), lambda c: (0, c + nb)),
                  pl.BlockSpec((None, 32, LANES), lambda c: (c, 0, 0)), pl.BlockSpec((1, LANES), lambda c: (0, c))],
        out_specs=pl.BlockSpec((S, LANES), lambda c: (0, c)),
        out_shape=jax.ShapeDtypeStruct((S, M), F32),
        scratch_shapes=[pltpu.VMEM((S + CONV_PAD, LANES), F32)],
        compiler_params=_params(("arbitrary",)),
    )(proj, proj, convw4, conv_b)


def conv_bwd(dcq, proj, convw4, name):
    S, M = dcq.shape
    nb = M // LANES
    tt = min(SEQ_TT, S)
    off = CONV_PAD - CONV_WIDTH + 1

    def body(dcq_ref, uv_ref, ug_ref, w_ref, duv_ref, dug_ref, dw_ref, db_ref, qp, dp, dw8, db8):
        qp[0:CONV_PAD, :] = jnp.zeros((CONV_PAD, LANES), F32)
        dp[S:S + CONV_PAD, :] = jnp.zeros((CONV_PAD, LANES), F32)
        dw8[...] = jnp.zeros_like(dw8)
        db8[...] = jnp.zeros_like(db8)

        def fill(t, carry):
            base = pl.multiple_of(t * tt, tt)
            qp[pl.ds(base + CONV_PAD, tt), :] = uv_ref[pl.ds(base, tt), :] * _sigmoid(ug_ref[pl.ds(base, tt), :])
            dp[pl.ds(base, tt), :] = dcq_ref[pl.ds(base, tt), :]
            return carry

        lax.fori_loop(0, S // tt, fill, 0)

        def step(t, carry):
            base = pl.multiple_of(t * tt, tt)
            d_t = dcq_ref[pl.ds(base, tt), :]
            db8[...] += d_t.reshape(tt // SUBLANES, SUBLANES, LANES).sum(axis=0)
            dq = jnp.zeros((tt, LANES), F32)
            for k in range(CONV_WIDTH):
                prod = d_t * qp[pl.ds(base + off + k, tt), :]
                dw8[k] += prod.reshape(tt // SUBLANES, SUBLANES, LANES).sum(axis=0)
                dq = dq + w_ref[k:k + 1, :] * dp[pl.ds(base + (CONV_WIDTH - 1) - k, tt), :]
            uv = uv_ref[pl.ds(base, tt), :]
            sg = _sigmoid(ug_ref[pl.ds(base, tt), :])
            duv_ref[pl.ds(base, tt), :] = dq * sg
            dug_ref[pl.ds(base, tt), :] = dq * uv * sg * (1.0 - sg)
            return carry

        lax.fori_loop(0, S // tt, step, 0)
        dw_ref[...] = jnp.zeros_like(dw_ref)
        for k in range(CONV_WIDTH):
            dw_ref[k:k + 1, :] = _colsum(dw8[k])
        db_ref[...] = _colsum(db8[...])

    col = lambda o: pl.BlockSpec((S, LANES), lambda c: (0, c + o))
    return _pcall(
        body, name=name, grid=(nb,),
        in_specs=[col(0), col(0), col(nb), pl.BlockSpec((None, 32, LANES), lambda c: (c, 0, 0))],
        out_specs=[col(0), col(0), pl.BlockSpec((None, 32, LANES), lambda c: (c, 0, 0)),
                   pl.BlockSpec((1, LANES), lambda c: (0, c))],
        out_shape=[jax.ShapeDtypeStruct((S, M), F32), jax.ShapeDtypeStruct((S, M), F32),
                   jax.ShapeDtypeStruct((nb, 32, LANES), F32), jax.ShapeDtypeStruct((1, M), F32)],
        scratch_shapes=[pltpu.VMEM((S + CONV_PAD, LANES), F32), pltpu.VMEM((S + CONV_PAD, LANES), F32),
                        pltpu.VMEM((32, SUBLANES, LANES), F32), pltpu.VMEM((SUBLANES, LANES), F32)],
        compiler_params=_params(("arbitrary",)),
    )(dcq, proj, proj, convw4)


def _log_sigmoid(x):
    return jnp.minimum(x, 0.0) - jnp.log(1.0 + jnp.exp(-jnp.abs(x)))


def _rg_gate_terms(ra, ls):
    la = RG_C * ra * ls
    a = jnp.exp(la)
    th = jnp.tanh(la)
    mult = jnp.sqrt(-2.0 * th / (1.0 - th))
    return a, mult


def rnn_fwd(proj, rnnw4, rnn_b, bda, bdi, b_a, b_i, lam, name):
    S = proj.shape[0]
    M = rnn_b.shape[1]
    nb = M // LANES
    tt = min(SEQ_TT, S)
    KW = RNN_CONV_WIDTH

    def body(ux_ref, w_ref, rb_ref, bda_ref, bdi_ref, ba_ref, bi_ref, lam_ref,
             xr_ref, ra_ref, ii_ref, h_ref, uxp, a_sc, b_sc):
        uxp[0:SUBLANES, :] = jnp.zeros((SUBLANES, LANES), F32)
        ls = _log_sigmoid(lam_ref[...])

        def step(t, carry):
            base = pl.multiple_of(t * tt, tt)
            uxp[pl.ds(base + SUBLANES, tt), :] = ux_ref[pl.ds(base, tt), :]
            xr = jnp.broadcast_to(rb_ref[...], (tt, LANES))
            for k in range(KW):
                xr = xr + w_ref[k:k + 1, :] * uxp[pl.ds(base + (SUBLANES - KW + 1) + k, tt), :]
            xb = xr.astype(BF)
            ra = _sigmoid(jnp.dot(xb, bda_ref[...], preferred_element_type=F32) + ba_ref[...])
            ii = _sigmoid(jnp.dot(xb, bdi_ref[...], preferred_element_type=F32) + bi_ref[...])
            a, mult = _rg_gate_terms(ra, ls)
            xr_ref[pl.ds(base, tt), :] = xr
            ra_ref[pl.ds(base, tt), :] = ra
            ii_ref[pl.ds(base, tt), :] = ii
            a_sc[pl.ds(base, tt), :] = a
            b_sc[pl.ds(base, tt), :] = mult * (ii * xr)
            return carry

        lax.fori_loop(0, S // tt, step, 0)

        rows = lax.broadcasted_iota(I32, (SUBLANES, LANES), 0)

        def scan(t, hprev):
            base = pl.multiple_of(t * SUBLANES, SUBLANES)
            A = a_sc[pl.ds(base, SUBLANES), :]
            B = b_sc[pl.ds(base, SUBLANES), :]
            for d in (1, 2, 4):
                As = jnp.where(rows >= d, pltpu.roll(A, d, axis=0), 1.0)
                Bs = jnp.where(rows >= d, pltpu.roll(B, d, axis=0), 0.0)
                B = A * Bs + B
                A = A * As
            hh = B + A * hprev
            h_ref[pl.ds(base, SUBLANES), :] = hh
            return jnp.broadcast_to(hh[SUBLANES - 1:SUBLANES, :], (SUBLANES, LANES))

        lax.fori_loop(0, S // SUBLANES, scan, jnp.zeros((SUBLANES, LANES), F32))

    col = lambda o: pl.BlockSpec((S, LANES), lambda c: (0, c + o))
    vec = pl.BlockSpec((1, LANES), lambda c: (0, c))
    diag = pl.BlockSpec((LANES, LANES), lambda c: (c, c))
    return _pcall(
        body, name=name, grid=(nb,),
        in_specs=[col(2 * nb), pl.BlockSpec((None, SUBLANES, LANES), lambda c: (c, 0, 0)), vec, diag, diag, vec, vec, vec],
        out_specs=[col(0)] * 4,
        out_shape=[jax.ShapeDtypeStruct((S, M), F32)] * 4,
        scratch_shapes=[pltpu.VMEM((S + SUBLANES, LANES), F32), pltpu.VMEM((S, LANES), F32), pltpu.VMEM((S, LANES), F32)],
        compiler_params=_params(("arbitrary",)),
    )(proj, rnnw4, rnn_b, bda, bdi, b_a, b_i, lam)


def rnn_bwd(dhout, h, xr, ra, ii, proj, rnnw4, bda, bdi, lam, name):
    S, M = h.shape
    nb = M // LANES
    tt = min(SEQ_TT, S)
    KW = RNN_CONV_WIDTH
    SL = SUBLANES

    def body(dh_ref, h_ref, xr_ref, ra_ref, ii_ref, ux_ref, w_ref, bda_ref, bdi_ref, lam_ref,
             dux_ref, dwa_ref, dwi_ref, drw_ref, vec_ref,
             a_sc, hp, g_sc, dpa_sc, dpi_sc, dxp, uxp, acc8, drw8):
        zero8 = jnp.zeros((SL, LANES), F32)
        a_sc[S:S + SL, :] = zero8
        hp[0:SL, :] = zero8
        dxp[S:S + SL, :] = zero8
        uxp[0:SL, :] = zero8
        acc8[...] = jnp.zeros_like(acc8)
        drw8[...] = jnp.zeros_like(drw8)
        lamv = lam_ref[...]
        ls = _log_sigmoid(lamv)

        def fill(t, carry):
            base = pl.multiple_of(t * tt, tt)
            a_sc[pl.ds(base, tt), :] = jnp.exp(RG_C * ra_ref[pl.ds(base, tt), :] * ls)
            hp[pl.ds(base + SL, tt), :] = h_ref[pl.ds(base, tt), :]
            uxp[pl.ds(base + SL, tt), :] = ux_ref[pl.ds(base, tt), :]
            return carry

        lax.fori_loop(0, S // tt, fill, 0)

        rows = lax.broadcasted_iota(I32, (SL, LANES), 0)
        nt8 = S // SL

        def rscan(t, gnext):
            base = pl.multiple_of((nt8 - 1 - t) * SL, SL)
            A = a_sc[pl.ds(base + 1, SL), :]
            B = dh_ref[pl.ds(base, SL), :]
            for d in (1, 2, 4):
                As = jnp.where(rows < SL - d, pltpu.roll(A, SL - d, axis=0), 1.0)
                Bs = jnp.where(rows < SL - d, pltpu.roll(B, SL - d, axis=0), 0.0)
                B = A * Bs + B
                A = A * As
            g = B + A * gnext
            g_sc[pl.ds(base, SL), :] = g
            return jnp.broadcast_to(g[0:1, :], (SL, LANES))

        lax.fori_loop(0, nt8, rscan, zero8)

        def red8(v):
            return v.reshape(tt // SL, SL, LANES).sum(axis=0)

        def step(t, carry):
            base = pl.multiple_of(t * tt, tt)
            g = g_sc[pl.ds(base, tt), :]
            hprev = hp[pl.ds(base + SL - 1, tt), :]
            xr_t = xr_ref[pl.ds(base, tt), :]
            ra_t = ra_ref[pl.ds(base, tt), :]
            ii_t = ii_ref[pl.ds(base, tt), :]
            a, mult = _rg_gate_terms(ra_t, ls)
            gx = g * xr_t
            dmult = gx * ii_t
            dii = gx * mult
            dxr = g * (mult * ii_t)
            dla = g * hprev * a - dmult * (a * a) / mult
            acc8[3] += red8(dla * ra_t)
            dpa = dla * (RG_C * ls) * ra_t * (1.0 - ra_t)
            dpi = dii * ii_t * (1.0 - ii_t)
            dpab = dpa.astype(BF)
            dpib = dpi.astype(BF)
            dxr = dxr + (lax.dot_general(dpab, bda_ref[...], CONTRACT_LAST, preferred_element_type=F32)
                         + lax.dot_general(dpib, bdi_ref[...], CONTRACT_LAST, preferred_element_type=F32))
            dpa_sc[pl.ds(base, tt), :] = dpab
            dpi_sc[pl.ds(base, tt), :] = dpib
            dxp[pl.ds(base, tt), :] = dxr
            acc8[0] += red8(dxr)
            acc8[1] += red8(dpa)
            acc8[2] += red8(dpi)
            return carry

        lax.fori_loop(0, S // tt, step, 0)

        def convb(t, carry):
            base = pl.multiple_of(t * tt, tt)
            d_t = dxp[pl.ds(base, tt), :]
            dux = jnp.zeros((tt, LANES), F32)
            for k in range(KW):
                drw8[k] += red8(d_t * uxp[pl.ds(base + (SL - KW + 1) + k, tt), :])
                dux = dux + w_ref[k:k + 1, :] * dxp[pl.ds(base + (KW - 1) - k, tt), :]
            dux_ref[pl.ds(base, tt), :] = dux
            return carry

        lax.fori_loop(0, S // tt, convb, 0)

        xb = xr_ref[...].astype(BF)
        dwa_ref[...] = lax.dot_general(xb, dpa_sc[...], CONTRACT_FIRST, preferred_element_type=F32)
        dwi_ref[...] = lax.dot_general(xb, dpi_sc[...], CONTRACT_FIRST, preferred_element_type=F32)
        drw_ref[...] = jnp.zeros_like(drw_ref)
        vec_ref[...] = jnp.zeros_like(vec_ref)
        for k in range(KW):
            drw_ref[k:k + 1, :] = _colsum(drw8[k])
        for k in range(3):
            vec_ref[k:k + 1, :] = _colsum(acc8[k])
        vec_ref[3:4, :] = _colsum(acc8[3]) * (RG_C * _sigmoid(-lamv))

    col = lambda o: pl.BlockSpec((S, LANES), lambda c: (0, c + o))
    vec = pl.BlockSpec((1, LANES), lambda c: (0, c))
    diag = pl.BlockSpec((LANES, LANES), lambda c: (c, c))
    blk3 = lambda r: pl.BlockSpec((None, r, LANES), lambda c: (c, 0, 0))
    return _pcall(
        body, name=name, grid=(nb,),
        in_specs=[col(0), col(0), col(0), col(0), col(0), col(2 * nb), blk3(SL), diag, diag, vec],
        out_specs=[col(0), blk3(LANES), blk3(LANES), blk3(SL), pl.BlockSpec((SL, LANES), lambda c: (0, c))],
        out_shape=[jax.ShapeDtypeStruct((S, M), F32), jax.ShapeDtypeStruct((nb, LANES, LANES), F32),
                   jax.ShapeDtypeStruct((nb, LANES, LANES), F32), jax.ShapeDtypeStruct((nb, SL, LANES), F32),
                   jax.ShapeDtypeStruct((SL, M), F32)],
        scratch_shapes=[pltpu.VMEM((S + SL, LANES), F32), pltpu.VMEM((S + SL, LANES), F32), pltpu.VMEM((S, LANES), F32),
                        pltpu.VMEM((S, LANES), BF), pltpu.VMEM((S, LANES), BF), pltpu.VMEM((S + SL, LANES), F32),
                        pltpu.VMEM((S + SL, LANES), F32), pltpu.VMEM((SL, SL, LANES), F32), pltpu.VMEM((SL, SL, LANES), F32)],
        compiler_params=_params(("arbitrary",)),
    )(dhout, h, xr, ra, ii, proj, rnnw4, bda, bdi, lam)


GELU_K = 0.7978845608028654
GELU_C = 0.044715


def _layernorm_parts(cq):
    mu = _rowmean(cq)
    d = cq - mu
    rstd = lax.rsqrt(_rowmean(d * d) + EPS)
    return d * rstd, rstd


def mix_out(cq, proj, h, x, vecs, lnv, wout, name):
    S, D = x.shape
    M = cq.shape[1]
    ts = min(512, S)

    def body(cq_ref, uy_ref, h_ref, x_ref, v_ref, ln_ref, w_ref, xo_ref, ym_ref, yc_ref):
        z, _ = _layernorm_parts(cq_ref[...])
        l = z * _row(ln_ref, 0) + _row(ln_ref, 1)
        yc_ref[:, 0:M] = (l * _sigmoid(l)).astype(BF)
        uy = uy_ref[...]
        gelu = 0.5 * uy * (1.0 + jnp.tanh(GELU_K * (uy + GELU_C * uy * uy * uy)))
        yc_ref[:, M:2 * M] = (gelu * h_ref[...]).astype(BF)
        ym = jnp.dot(yc_ref[...], w_ref[...], preferred_element_type=F32)
        ym_ref[...] = ym
        xo_ref[...] = x_ref[...] + _row(v_ref, R_GT2) * ym

    tok = pl.BlockSpec((ts, D), lambda i: (i, 0))
    mtok = lambda o: pl.BlockSpec((ts, M), lambda i: (i, o))
    return _pcall(
        body, name=name, grid=(S // ts,),
        in_specs=[mtok(0), mtok(3), mtok(0), tok, pl.BlockSpec(vecs.shape, lambda i: (0, 0)),
                  pl.BlockSpec(lnv.shape, lambda i: (0, 0)), pl.BlockSpec(wout.shape, lambda i: (0, 0))],
        out_specs=[tok, tok, pl.BlockSpec((ts, 2 * M), lambda i: (i, 0))],
        out_shape=[jax.ShapeDtypeStruct((S, D), F32), jax.ShapeDtypeStruct((S, D), F32),
                   jax.ShapeDtypeStruct((S, 2 * M), BF)],
        compiler_params=_params(("arbitrary",)),
    )(cq, proj, h, x, vecs, lnv, wout)


def mix_out_bwd(dxo, ym, vecs, wout, cq, lnv, proj, h, name):
    S, D = dxo.shape
    M = cq.shape[1]
    ts = min(512, S)

    def body(dxo_ref, ym_ref, v_ref, w_ref, cq_ref, ln_ref, uy_ref, h_ref,
             dcq_ref, dh_ref, duy_ref, dyb_ref, vgd_ref, vgm_ref):
        @pl.when(pl.program_id(0) == 0)
        def _():
            vgd_ref[...] = jnp.zeros_like(vgd_ref)
            vgm_ref[...] = jnp.zeros_like(vgm_ref)

        dxo_v = dxo_ref[...]
        dyb = (_row(v_ref, R_GT2) * dxo_v).astype(BF)
        dyb_ref[...] = dyb
        vgd_ref[0:1, :] += _colsum(dxo_v * ym_ref[...])
        dycat = lax.dot_general(dyb, w_ref[...], CONTRACT_LAST, preferred_element_type=F32)
        dyc = dycat[:, 0:M]
        dyr = dycat[:, M:2 * M]
        z, rstd = _layernorm_parts(cq_ref[...])
        lng = _row(ln_ref, 0)
        l = z * lng + _row(ln_ref, 1)
        sl = _sigmoid(l)
        dl = dyc * (sl * (1.0 + l * (1.0 - sl)))
        vgm_ref[0:1, :] += _colsum(dl * z)
        vgm_ref[1:2, :] += _colsum(dl)
        dz = dl * lng
        dcq_ref[...] = rstd * (dz - _rowmean(dz) - z * _rowmean(dz * z))
        uy = uy_ref[...]
        u2 = uy * uy
        th = jnp.tanh(GELU_K * (uy + GELU_C * uy * u2))
        gelu = 0.5 * uy * (1.0 + th)
        dgelu = 0.5 * (1.0 + th) + 0.5 * uy * (1.0 - th * th) * (GELU_K * (1.0 + 3.0 * GELU_C * u2))
        dh_ref[...] = dyr * gelu
        duy_ref[...] = dyr * h_ref[...] * dgelu

    tok = pl.BlockSpec((ts, D), lambda i: (i, 0))
    mtok = lambda o: pl.BlockSpec((ts, M), lambda i: (i, o))
    return _pcall(
        body, name=name, grid=(S // ts,),
        in_specs=[tok, tok, pl.BlockSpec(vecs.shape, lambda i: (0, 0)), pl.BlockSpec(wout.shape, lambda i: (0, 0)),
                  mtok(0), pl.BlockSpec(lnv.shape, lambda i: (0, 0)), mtok(3), mtok(0)],
        out_specs=[mtok(0), mtok(0), mtok(0), tok, pl.BlockSpec((SUBLANES, D), lambda i: (0, 0)),
                   pl.BlockSpec((SUBLANES, M), lambda i: (0, 0))],
        out_shape=[jax.ShapeDtypeStruct((S, M), F32)] * 3 + [jax.ShapeDtypeStruct((S, D), BF),
                   jax.ShapeDtypeStruct((SUBLANES, D), F32), jax.ShapeDtypeStruct((SUBLANES, M), F32)],
        compiler_params=_params(("arbitrary",)),
    )(dxo, ym, vecs, wout, cq, lnv, proj, h)


def mix_in_bwd(dparts, x, dxo, vecs, win, name):
    S, D = x.shape
    M = dparts[0].shape[1]
    ts = min(512, S)

    def body(d0, d1, d2, d3, x_ref, dxo_ref, v_ref, w_ref, dx_ref, hb_ref, dp_ref, vg_ref):
        @pl.when(pl.program_id(0) == 0)
        def _():
            vg_ref[...] = jnp.zeros_like(vg_ref)

        for q, dref in enumerate((d0, d1, d2, d3)):
            dp_ref[:, q * M:(q + 1) * M] = dref[...].astype(BF)
        dh = lax.dot_general(dp_ref[...], w_ref[...], CONTRACT_LAST, preferred_element_type=F32)
        xv = x_ref[...]
        r = lax.rsqrt(_rowmean(xv * xv) + EPS)
        n = xv * r
        g = _row(v_ref, R_G2)
        sc1 = 1.0 + _row(v_ref, R_SC2)
        gsc = g * sc1
        hb_ref[...] = (n * gsc + _row(v_ref, R_SH2)).astype(BF)
        dhn = dh * n
        vg_ref[0:1, :] += _colsum(dh)
        vg_ref[1:2, :] += _colsum(dhn) * g
        vg_ref[2:3, :] += _colsum(dhn) * sc1
        dn = dh * gsc
        dx_ref[...] = dxo_ref[...] + r * (dn - n * _rowmean(dn * n))

    tok = pl.BlockSpec((ts, D), lambda i: (i, 0))
    mtok = pl.BlockSpec((ts, M), lambda i: (i, 0))
    return _pcall(
        body, name=name, grid=(S // ts,),
        in_specs=[mtok] * 4 + [tok, tok, pl.BlockSpec(vecs.shape, lambda i: (0, 0)), pl.BlockSpec(win.shape, lambda i: (0, 0))],
        out_specs=[tok, tok, pl.BlockSpec((ts, 4 * M), lambda i: (i, 0)), pl.BlockSpec((SUBLANES, D), lambda i: (0, 0))],
        out_shape=[jax.ShapeDtypeStruct((S, D), F32), jax.ShapeDtypeStruct((S, D), BF),
                   jax.ShapeDtypeStruct((S, 4 * M), BF), jax.ShapeDtypeStruct((SUBLANES, D), F32)],
        compiler_params=_params(("arbitrary",)),
    )(*dparts, x, dxo, vecs, win)


def _adamw(w, g, m, v):
    m = ADAM_B1 * m + (1.0 - ADAM_B1) * g
    v = ADAM_B2 * v + (1.0 - ADAM_B2) * (g * g)
    m_hat = m / (1.0 - ADAM_B1 ** ADAM_STEP)
    v_hat = v / (1.0 - ADAM_B2 ** ADAM_STEP)
    delta = -ADAM_LR * (m_hat / (jnp.sqrt(v_hat) + ADAM_EPS) + ADAM_WD * w)
    return delta, m, v


def adam_big(w, g, m, v, name):
    R, C = w.shape
    tr = 256 if R % 256 == 0 else R // 2 if (R // 2) % SUBLANES == 0 and R > 512 else R
    tc = C if C <= 1536 else (1152 if C % 1152 == 0 else 1024)
    assert R % tr == 0 and C % tc == 0

    def body(w_ref, g_ref, m_ref, v_ref, d_ref, nm_ref, nv_ref):
        d, nm, nv = _adamw(w_ref[...], g_ref[...], m_ref[...], v_ref[...])
        d_ref[...] = d
        nm_ref[...] = nm
        nv_ref[...] = nv

    blk = pl.BlockSpec((tr, tc), lambda i, j: (i, j))
    return _pcall(
        body, name=name, grid=(R // tr, C // tc), in_specs=[blk] * 4, out_specs=[blk] * 3,
        out_shape=[jax.ShapeDtypeStruct((R, C), F32)] * 3, compiler_params=_params(("parallel", "parallel")),
    )(w, g, m, v)


def adam_cond(c_all, dmod, w, m, v, name):
    B, Kin = c_all.shape
    N = w.shape[1]
    tn = 256
    assert N % tn == 0

    def body(c_ref, d_ref, w_ref, m_ref, v_ref, g_ref, dl_ref, nm_ref, nv_ref):
        cv = c_ref[...]
        ca = cv * _sigmoid(cv)
        g = lax.dot_general(ca, d_ref[...], CONTRACT_FIRST, preferred_element_type=F32, precision=lax.Precision.HIGHEST)
        d, nm, nv = _adamw(w_ref[...], g, m_ref[...], v_ref[...])
        g_ref[...] = g
        dl_ref[...] = d
        nm_ref[...] = nm
        nv_ref[...] = nv

    blk = pl.BlockSpec((Kin, tn), lambda n: (0, n))
    return _pcall(
        body, name=name, grid=(N // tn,),
        in_specs=[pl.BlockSpec((B, Kin), lambda n: (0, 0)), pl.BlockSpec((B, tn), lambda n: (0, n)), blk, blk, blk],
        out_specs=[blk] * 4, out_shape=[jax.ShapeDtypeStruct((Kin, N), F32)] * 4,
        compiler_params=_params(("parallel",)),
    )(c_all, dmod, w, m, v)


def adam_small(ws, gs, ms, vs, name):
    n = len(ws)

    def body(*refs):
        ins, outs = refs[:4 * n], refs[4 * n:]
        for k in range(n):
            d, nm, nv = _adamw(ins[k][...], ins[n + k][...], ins[2 * n + k][...], ins[3 * n + k][...])
            outs[k][...] = d
            outs[n + k][...] = nm
            outs[2 * n + k][...] = nv

    specs = [pl.BlockSpec(w.shape, lambda i: (0, 0)) for w in ws]
    shapes = [jax.ShapeDtypeStruct(w.shape, F32) for w in ws]
    out = _pcall(body, name=name, grid=(1,), in_specs=specs * 4, out_specs=specs * 3, out_shape=shapes * 3,
                 compiler_params=_params(("arbitrary",)))(*ws, *gs, *ms, *vs)
    return out[:n], out[n:2 * n], out[2 * n:]


def _me():
    return lax.axis_index("x"), lax.axis_index("y"), lax.axis_index("c")


def _flip(x, y, p):
    return (x ^ (p >> 1) if (p >> 1) else x), (y ^ (p & 1) if (p & 1) else y)


def _handshake(peers):
    barrier = pltpu.get_barrier_semaphore()
    for peer in peers:
        pl.semaphore_signal(barrier, inc=1, device_id=peer, device_id_type=MESH)
    pl.semaphore_wait(barrier, len(peers))


def _seq_call(body, *, name, n_in, out_shape, sem_shapes, collective_id):
    del n_in
    return pl.kernel(body, out_type=out_shape, mesh=plsc.ScalarSubcoreMesh(axis_name="sq", num_cores=1), name=name,
                     scratch_types=sem_shapes, compiler_params=pltpu.CompilerParams(collective_id=collective_id))


def _hbm_comm_call(body, *, name, n_in, out_shape, sem_shapes, seq_id):
    if seq_id is not None:
        return _seq_call(body, name=name, n_in=n_in, out_shape=out_shape, sem_shapes=sem_shapes, collective_id=seq_id)
    anyspec = pl.BlockSpec(memory_space=pl.ANY)
    return _pcall(body, name=name, in_specs=[anyspec] * n_in, out_specs=[anyspec] * len(out_shape), out_shape=out_shape,
                  scratch_shapes=sem_shapes, compiler_params=_params())


def allgather_devices(v, name, with_sum=False):
    R, L = v.shape

    def body(v_ref, out_ref, *rest):
        if with_sum:
            sum_ref, send_sems, recv_sems = rest
        else:
            send_sems, recv_sems = rest
        x, y, c = _me()
        me = 4 * x + 2 * y + c
        out_ref[me] = v_ref[...]
        copies = []
        for p in range(1, N_DEV):
            px, py = _flip(x, y, p >> 1)
            pc = (1 - c) if (p & 1) else c
            peer = 4 * px + 2 * py + pc
            send = pltpu.make_async_remote_copy(src_ref=v_ref, dst_ref=out_ref.at[me], send_sem=send_sems.at[p - 1],
                                                recv_sem=recv_sems.at[p - 1], device_id=(px, py, pc), device_id_type=MESH)
            send.start()
            recv = pltpu.make_async_remote_copy(src_ref=v_ref, dst_ref=out_ref.at[peer], send_sem=send_sems.at[p - 1],
                                                recv_sem=recv_sems.at[p - 1], device_id=(px, py, pc), device_id_type=MESH)
            copies.append((send, recv))
        for send, recv in copies:
            recv.wait_recv()
        for send, recv in copies:
            send.wait_send()
        if with_sum:
            s = out_ref[0]
            for k in range(1, N_DEV):
                s = s + out_ref[k]
            sum_ref[...] = s

    vm = pl.BlockSpec(memory_space=pltpu.VMEM)
    out_shape = [jax.ShapeDtypeStruct((N_DEV, R, L), F32)]
    if with_sum:
        out_shape.append(jax.ShapeDtypeStruct((R, L), F32))
    return _pcall(
        body, name=name, in_specs=[vm], out_specs=[vm] * len(out_shape), out_shape=out_shape,
        scratch_shapes=[pltpu.SemaphoreType.DMA((N_DEV - 1,)), pltpu.SemaphoreType.DMA((N_DEV - 1,))],
        compiler_params=_params(),
    )(v)


def allgather_devices_hbm(v, name, seq_id):
    R, L = v.shape

    def body(v_ref, out_ref, send_sems, recv_sems, local_sem):
        x, y, c = _me()
        me = 4 * x + 2 * y + c
        peers = []
        for p in range(1, N_DEV):
            px, py = _flip(x, y, p >> 1)
            peers.append((px, py, (1 - c) if (p & 1) else c))
        _handshake(peers)
        lc = pltpu.make_async_copy(v_ref, out_ref.at[me], local_sem)
        lc.start()
        copies = []
        for p, (px, py, pc) in enumerate(peers):
            send = pltpu.make_async_remote_copy(src_ref=v_ref, dst_ref=out_ref.at[me], send_sem=send_sems.at[p],
                                                recv_sem=recv_sems.at[p], device_id=(px, py, pc), device_id_type=MESH)
            send.start()
            recv = pltpu.make_async_remote_copy(src_ref=v_ref, dst_ref=out_ref.at[4 * px + 2 * py + pc], send_sem=send_sems.at[p],
                                                recv_sem=recv_sems.at[p], device_id=(px, py, pc), device_id_type=MESH)
            copies.append((send, recv))
        for send, recv in copies:
            recv.wait_recv()
        for send, recv in copies:
            send.wait_send()
        lc.wait()

    return _seq_call(body, name=name, n_in=1, out_shape=[jax.ShapeDtypeStruct((N_DEV, R, L), F32)],
                     sem_shapes=[pltpu.SemaphoreType.DMA((N_DEV - 1,)), pltpu.SemaphoreType.DMA((N_DEV - 1,)),
                                 pltpu.SemaphoreType.DMA], collective_id=seq_id)(v)[0]


def sum_slots(g, name):
    n, R, L = g.shape
    tr = 216 if R % 216 == 0 else R
    assert R % tr == 0 and tr % SUBLANES == 0

    def body(g_ref, o_ref):
        s = g_ref[0]
        for k in range(1, n):
            s = s + g_ref[k]
        o_ref[...] = s

    return _pcall(body, name=name, grid=(R // tr,), in_specs=[pl.BlockSpec((n, tr, L), lambda i: (0, i, 0))],
                  out_specs=pl.BlockSpec((tr, L), lambda i: (i, 0)), out_shape=jax.ShapeDtypeStruct((R, L), F32),
                  compiler_params=_params(("parallel",)))(g)


def allgather_chips(v, name):
    R, L = v.shape

    def body(v_ref, out_ref, send_sems, recv_sems):
        x, y, c = _me()
        chip = 2 * x + y
        out_ref[chip] = v_ref[...]
        copies = []
        for p in range(1, N_CHIPS):
            px, py = _flip(x, y, p)
            send = pltpu.make_async_remote_copy(src_ref=v_ref, dst_ref=out_ref.at[chip], send_sem=send_sems.at[p - 1],
                                                recv_sem=recv_sems.at[p - 1], device_id=(px, py, c), device_id_type=MESH)
            send.start()
            recv = pltpu.make_async_remote_copy(src_ref=v_ref, dst_ref=out_ref.at[2 * px + py], send_sem=send_sems.at[p - 1],
                                                recv_sem=recv_sems.at[p - 1], device_id=(px, py, c), device_id_type=MESH)
            copies.append((send, recv))
        for send, recv in copies:
            recv.wait_recv()
        for send, recv in copies:
            send.wait_send()

    vm = pl.BlockSpec(memory_space=pltpu.VMEM)
    return _pcall(
        body, name=name, in_specs=[vm], out_specs=vm, out_shape=jax.ShapeDtypeStruct((N_CHIPS, R, L), F32),
        scratch_shapes=[pltpu.SemaphoreType.DMA((N_CHIPS - 1,)), pltpu.SemaphoreType.DMA((N_CHIPS - 1,))],
        compiler_params=_params(),
    )(v)


def _shard_window(ref, kind, shard_shape, chip, half):
    r, c = shard_shape
    hr = r // 2
    if kind == "col":
        return ref.at[pl.ds(pl.multiple_of(half * hr, hr), hr), pl.ds(pl.multiple_of(chip * c, c), c)]
    return ref.at[pl.ds(pl.multiple_of(chip * r + half * hr, hr), hr), :]


def allgather_weights(shards, kinds, name, seq_id=None):
    n = len(shards)
    fulls = []
    for s, kind in zip(shards, kinds):
        r, c = s.shape
        fulls.append(jax.ShapeDtypeStruct((r, N_CHIPS * c) if kind == "col" else (N_CHIPS * r, c), s.dtype))

    def body(*refs):
        srcs, outs = refs[:n], refs[n:2 * n]
        send_sems, recv_sems, fsend_sems, frecv_sems = refs[2 * n:]
        x, y, c = _me()
        chip = 2 * x + y
        sib = (x, y, 1 - c)
        if seq_id is not None:
            _handshake([(*_flip(x, y, p), c) for p in range(1, N_CHIPS)] + [sib])
        sends, fwds = [], []
        for i in range(n):
            shp = srcs[i].shape
            hr = shp[0] // 2
            my_half = srcs[i].at[pl.ds(pl.multiple_of(c * hr, hr), hr), :]
            for p in range(1, N_CHIPS):
                px, py = _flip(x, y, p)
                k = i * (N_CHIPS - 1) + p - 1
                cp = pltpu.make_async_remote_copy(src_ref=my_half, dst_ref=_shard_window(outs[i], kinds[i], shp, chip, c),
                                                  send_sem=send_sems.at[k], recv_sem=recv_sems.at[k],
                                                  device_id=(px, py, c), device_id_type=MESH)
                cp.start()
                sends.append(cp)
        for i in range(n):
            shp = srcs[i].shape
            for p in range(1, N_CHIPS):
                px, py = _flip(x, y, p)
                k = i * (N_CHIPS - 1) + p - 1
                landed = _shard_window(outs[i], kinds[i], shp, 2 * px + py, c)
                pltpu.make_async_remote_copy(src_ref=landed, dst_ref=landed, send_sem=send_sems.at[k], recv_sem=recv_sems.at[k],
                                             device_id=(px, py, c), device_id_type=MESH).wait_recv()
                fw = pltpu.make_async_remote_copy(src_ref=landed, dst_ref=landed, send_sem=fsend_sems.at[k],
                                                  recv_sem=frecv_sems.at[k], device_id=sib, device_id_type=MESH)
                fw.start()
                fwds.append(fw)
        for i in range(n):
            shp = srcs[i].shape
            for p in range(1, N_CHIPS):
                px, py = _flip(x, y, p)
                k = i * (N_CHIPS - 1) + p - 1
                other = _shard_window(outs[i], kinds[i], shp, 2 * px + py, 1 - c)
                pltpu.make_async_remote_copy(src_ref=other, dst_ref=other, send_sem=fsend_sems.at[k], recv_sem=frecv_sems.at[k],
                                             device_id=sib, device_id_type=MESH).wait_recv()
        for cp in sends + fwds:
            cp.wait_send()

    nk = n * (N_CHIPS - 1)
    gathered = _hbm_comm_call(
        body, name=name, n_in=n, out_shape=fulls, seq_id=seq_id,
        sem_shapes=[pltpu.SemaphoreType.DMA((nk,)), pltpu.SemaphoreType.DMA((nk,)), pltpu.SemaphoreType.DMA((nk,)),
                    pltpu.SemaphoreType.DMA((nk,))],
    )(*shards)
    return gathered


def place_local_shards(fulls, shards, kinds, name):
    n = len(shards)
    chip = jnp.reshape(2 * lax.axis_index("x") + lax.axis_index("y"), (1,)).astype(I32)

    def body(ci_ref, *refs):
        for i in range(n):
            refs[2 * n + i][...] = refs[i][...]

    in_specs = [pl.BlockSpec(s.shape, lambda i, ci: (0, 0)) for s in shards] + [pl.BlockSpec(memory_space=pl.ANY)] * n
    out_specs = [pl.BlockSpec(s.shape, (lambda i, ci: (0, ci[0])) if k == "col" else (lambda i, ci: (ci[0], 0)))
                 for s, k in zip(shards, kinds)]
    gs = pltpu.PrefetchScalarGridSpec(num_scalar_prefetch=1, grid=(1,), in_specs=in_specs, out_specs=out_specs)
    return _pcall(body, name=name, grid_spec=gs, out_shape=[jax.ShapeDtypeStruct(f.shape, f.dtype) for f in fulls],
                  input_output_aliases={1 + n + i: i for i in range(n)}, compiler_params=_params(("arbitrary",)))(chip, *shards, *fulls)


def _as_halves(g, kind, shard_shape):
    r, c = shard_shape
    if kind == "col":
        return g.reshape(2, r // 2, N_CHIPS * c)
    return g.reshape(N_CHIPS, 2, r // 2, c)


def exchange_sibling_halves(grads, kinds, shard_shapes, name, seq_id=None):
    n = len(grads)
    views = [_as_halves(g, k, s) for g, k, s in zip(grads, kinds, shard_shapes)]
    outs = []
    for k, (r, c) in zip(kinds, shard_shapes):
        outs.append(jax.ShapeDtypeStruct((r // 2, N_CHIPS * c) if k == "col" else (N_CHIPS, r // 2, c), F32))

    def body(*refs):
        srcs, dsts = refs[:n], refs[n:2 * n]
        send_sems, recv_sems = refs[2 * n:]
        x, y, c = _me()
        if seq_id is not None:
            _handshake([(x, y, 1 - c)])
        cps = []
        for i in range(n):
            src = srcs[i].at[1 - c] if kinds[i] == "col" else srcs[i].at[:, 1 - c]
            cp = pltpu.make_async_remote_copy(src_ref=src, dst_ref=dsts[i], send_sem=send_sems.at[i], recv_sem=recv_sems.at[i],
                                              device_id=(x, y, 1 - c), device_id_type=MESH)
            cp.start()
            cps.append(cp)
        for cp in cps:
            cp.wait_recv()
        for cp in cps:
            cp.wait_send()

    return _hbm_comm_call(body, name=name, n_in=n, out_shape=outs, seq_id=seq_id,
                          sem_shapes=[pltpu.SemaphoreType.DMA((n,)), pltpu.SemaphoreType.DMA((n,))])(*views)


def add_sibling_half(g, recv, kind, shard_shape, cidx, name):
    r, c = shard_shape
    hr = r // 2
    gv = _as_halves(g, kind, shard_shape)
    tr = hr if hr <= 512 else (256 if hr % 256 == 0 else hr // 2)
    assert hr % tr == 0

    def body(ci_ref, g_ref, r_ref, h_ref, hb_ref):
        s = g_ref[...] + r_ref[...]
        h_ref[...] = s
        hb_ref[...] = s.astype(BF)

    if kind == "col":
        grid = (hr // tr, N_CHIPS)
        g_spec = pl.BlockSpec((None, tr, c), lambda i, k, ci: (ci[0], i, k))
        o_spec = pl.BlockSpec((tr, c), lambda i, k, ci: (i, k))
    else:
        grid = (hr // tr, N_CHIPS)
        g_spec = pl.BlockSpec((None, None, tr, c), lambda i, k, ci: (k, ci[0], i, 0))
        o_spec = pl.BlockSpec((None, tr, c), lambda i, k, ci: (k, i, 0))
    gs = pltpu.PrefetchScalarGridSpec(num_scalar_prefetch=1, grid=grid, in_specs=[g_spec, o_spec], out_specs=[o_spec, o_spec])
    return _pcall(
        body, name=name, grid_spec=gs,
        out_shape=[jax.ShapeDtypeStruct(recv.shape, F32), jax.ShapeDtypeStruct(recv.shape, BF)],
        compiler_params=_params(("parallel", "parallel")),
    )(cidx, gv, recv)


def exchange_chip_pieces(hbs, kinds, shard_shapes, name, seq_id=None):
    n = len(hbs)
    outs = [jax.ShapeDtypeStruct((N_CHIPS - 1, r // 2, c), BF) for (r, c) in shard_shapes]

    def body(*refs):
        srcs, dsts = refs[:n], refs[n:2 * n]
        send_sems, recv_sems = refs[2 * n:]
        x, y, c = _me()
        if seq_id is not None:
            _handshake([(*_flip(x, y, p), c) for p in range(1, N_CHIPS)])
        cps = []
        for i in range(n):
            cc = shard_shapes[i][1]
            for p in range(1, N_CHIPS):
                px, py = _flip(x, y, p)
                pchip = 2 * px + py
                src = (srcs[i].at[:, pl.ds(pl.multiple_of(pchip * cc, cc), cc)] if kinds[i] == "col" else srcs[i].at[pchip])
                k = i * (N_CHIPS - 1) + p - 1
                cp = pltpu.make_async_remote_copy(src_ref=src, dst_ref=dsts[i].at[p - 1], send_sem=send_sems.at[k],
                                                  recv_sem=recv_sems.at[k], device_id=(px, py, c), device_id_type=MESH)
                cp.start()
                cps.append(cp)
        for cp in cps:
            cp.wait_recv()
        for cp in cps:
            cp.wait_send()

    nk = n * (N_CHIPS - 1)
    return _hbm_comm_call(body, name=name, n_in=n, out_shape=outs, seq_id=seq_id,
                          sem_shapes=[pltpu.SemaphoreType.DMA((nk,)), pltpu.SemaphoreType.DMA((nk,))])(*hbs)


def sum_chip_pieces(h, pieces, kind, shard_shape, chip_core, name):
    r, c = shard_shape
    hr = r // 2
    tr = hr if hr <= 512 else (256 if hr % 256 == 0 else hr // 2)
    assert hr % tr == 0
    nrb = hr // tr

    def body(ci_ref, h_ref, p_ref, q_ref):
        q_ref[...] = ((h_ref[...] + p_ref[0].astype(F32)) + p_ref[1].astype(F32)) + p_ref[2].astype(F32)

    if kind == "col":
        h_spec = pl.BlockSpec((tr, c), lambda i, ci: (i, ci[0]))
    else:
        h_spec = pl.BlockSpec((None, tr, c), lambda i, ci: (ci[0], i, 0))
    gs = pltpu.PrefetchScalarGridSpec(
        num_scalar_prefetch=1, grid=(nrb,),
        in_specs=[h_spec, pl.BlockSpec((N_CHIPS - 1, tr, c), lambda i, ci: (0, i, 0))],
        out_specs=pl.BlockSpec((tr, c), lambda i, ci: (ci[1] * nrb + i, 0)))
    return _pcall(body, name=name, grid_spec=gs, out_shape=jax.ShapeDtypeStruct((r, c), F32),
                  compiler_params=_params(("parallel",)))(chip_core, h, pieces)


def exchange_reduced_halves(qs, name):
    n = len(qs)

    def body(*refs):
        bufs = refs[n:2 * n]
        send_sems, recv_sems = refs[2 * n:]
        x, y, c = _me()
        cps = []
        for i in range(n):
            hr = bufs[i].shape[0] // 2
            mine = bufs[i].at[pl.ds(pl.multiple_of(c * hr, hr), hr), :]
            other = bufs[i].at[pl.ds(pl.multiple_of((1 - c) * hr, hr), hr), :]
            cp = pltpu.make_async_remote_copy(src_ref=mine, dst_ref=mine, send_sem=send_sems.at[i], recv_sem=recv_sems.at[i],
                                              device_id=(x, y, 1 - c), device_id_type=MESH)
            cp.start()
            cps.append((cp, pltpu.make_async_remote_copy(src_ref=other, dst_ref=other, send_sem=send_sems.at[i],
                                                         recv_sem=recv_sems.at[i], device_id=(x, y, 1 - c), device_id_type=MESH)))
        for cp, rv in cps:
            rv.wait_recv()
        for cp, rv in cps:
            cp.wait_send()

    anyspec = pl.BlockSpec(memory_space=pl.ANY)
    return _pcall(
        body, name=name, in_specs=[anyspec] * n, out_specs=[anyspec] * n,
        out_shape=[jax.ShapeDtypeStruct(q.shape, F32) for q in qs], input_output_aliases={i: i for i in range(n)},
        scratch_shapes=[pltpu.SemaphoreType.DMA((n,)), pltpu.SemaphoreType.DMA((n,))],
        compiler_params=_params(),
    )(*qs)


def _rows128(a):
    return a.reshape(-1, LANES)


def _after(xs, *deps):
    flat = []
    for d in deps:
        flat.extend(d if isinstance(d, (list, tuple)) else [d])
    return list(lax.optimization_barrier((tuple(xs), tuple(flat)))[0])


def _block_diag(w):
    H, d, _ = w.shape
    eye = jnp.eye(H, dtype=w.dtype)
    return jnp.einsum("hde,hg->hdge", w, eye).reshape(H * d, H * d)


def _diag_blocks(g4, H, d):
    nb = g4.shape[0]
    per = LANES // d
    g = g4.reshape(nb, per, d, per, d)
    return jnp.stack([g[:, j, :, j, :] for j in range(per)], axis=1).reshape(H, d, d)


def kernel(x, c, w_mod, b_mod, g_ffn1, w_ffn1_in, w_ffn1_out, g_mix, w_in, conv_w, conv_b, ln_g, ln_b, rnn_conv_w, rnn_conv_b, w_a, b_a, w_i, b_i, lru_lambda, w_out, g_ffn2, w_ffn2_in, w_ffn2_out, w_fmod, b_fmod, g_final, loss_target, m_w_mod, m_b_mod, m_g_ffn1, m_w_ffn1_in, m_w_ffn1_out, m_g_mix, m_w_in, m_conv_w, m_conv_b, m_ln_g, m_ln_b, m_rnn_conv_w, m_rnn_conv_b, m_w_a, m_b_a, m_w_i, m_b_i, m_lru_lambda, m_w_out, m_g_ffn2, m_w_ffn2_in, m_w_ffn2_out, m_w_fmod, m_b_fmod, m_g_final, v_w_mod, v_b_mod, v_g_ffn1, v_w_ffn1_in, v_w_ffn1_out, v_g_mix, v_w_in, v_conv_w, v_conv_b, v_ln_g, v_ln_b, v_rnn_conv_w, v_rnn_conv_b, v_w_a, v_b_a, v_w_i, v_b_i, v_lru_lambda, v_w_out, v_g_ffn2, v_w_ffn2_in, v_w_ffn2_out, v_w_fmod, v_b_fmod, v_g_final):
    S, D = x.shape[1], x.shape[2]
    M = conv_b.shape[1]
    H, HD = w_a.shape[1], w_a.shape[2]
    nb = M // LANES
    ix, iy, ic = lax.axis_index("x"), lax.axis_index("y"), lax.axis_index("c")
    chip = 2 * ix + iy
    dev = 2 * chip + ic
    cidx = jnp.reshape(ic, (1,)).astype(I32)
    chip_core = jnp.stack([chip, ic]).astype(I32)
    xs = x[0]
    tgt = loss_target[0]

    kinds = ["col", "row"]
    w_f1, w_mx, w_f2 = [w_ffn1_in[0], w_ffn1_out[0]], [w_in[0], w_out[0]], [w_ffn2_in[0], w_ffn2_out[0]]
    as_bf = lambda ws: [w.astype(BF) for w in ws]
    shapes_of = lambda ws: [w.shape for w in ws]
    b_f1, b_mx, b_f2 = as_bf(w_f1), as_bf(w_mx), as_bf(w_f2)
    got_f1 = allgather_weights(b_f1, kinds, "gather_ffn1", seq_id=9)
    got_mx = allgather_weights(b_mx, kinds, "gather_mix", seq_id=1)
    got_f2 = allgather_weights(b_f2, kinds, "gather_ffn2", seq_id=2)

    c_all =allgather_devices(_rows128(c), "gather_c")[0].reshape(N_DEV, D)
    mod_cols = cond_matmul(c_all, w_mod[0], "mod_proj")
    fmod_cols = cond_matmul(c_all, w_fmod, "fmod_proj")
    convw_pad = jnp.pad(conv_w[0], ((0, 32 - CONV_WIDTH), (0, 0)))
    rnnw_pad = jnp.pad(rnn_conv_w[0], ((0, SUBLANES - RNN_CONV_WIDTH), (0, 0)))
    n_mod, n_fmod = mod_cols.shape[1], fmod_cols.shape[1]
    small = jnp.concatenate([_rows128(mod_cols), _rows128(fmod_cols), convw_pad, rnnw_pad], axis=0)
    small4 = allgather_chips(small, "gather_cond")
    r0 = N_DEV * n_mod // LANES
    r1 = r0 + N_DEV * n_fmod // LANES
    mod_all = small4[:, :r0].reshape(N_CHIPS, N_DEV, n_mod)
    fmod_all = small4[:, r0:r1].reshape(N_CHIPS, N_DEV, n_fmod)
    convw4 = small4[:, r1:r1 + 32]
    rnnw4 = small4[:, r1 + 32:r1 + 32 + SUBLANES]
    mod_row = lax.dynamic_index_in_dim(mod_all, dev, axis=1, keepdims=False).reshape(1, N_CHIPS * n_mod) + b_mod
    fmod_row = lax.dynamic_index_in_dim(fmod_all, dev, axis=1, keepdims=False).reshape(1, N_CHIPS * n_fmod) + b_fmod[None, :]
    vecs = jnp.concatenate([mod_row.reshape(9, D), fmod_row.reshape(2, D), g_ffn1, g_mix, g_ffn2, g_final[None, :],
                            jnp.zeros((1, D), F32)], axis=0)
    lnv = jnp.concatenate([ln_g, ln_b, jnp.zeros((SUBLANES - 2, M), F32)], axis=0)
    bda = _block_diag(w_a[0]).astype(BF)
    bdi = _block_diag(w_i[0]).astype(BF)

    def reduce_add(gs, recv, ws, tag, kinds_=kinds):
        pairs = [add_sibling_half(g, r_, k, w.shape, cidx, f"add_sibling_{tag}{j}")
                 for j, (g, r_, k, w) in enumerate(zip(gs, recv, kinds_, ws))]
        return [p[0] for p in pairs], [p[1] for p in pairs]

    def reduce_sum(hs_, recv, ws, tag, kinds_=kinds):
        return [sum_chip_pieces(h_, p_, k, w.shape, chip_core, f"sum_chips_{tag}{j}")
                for j, (h_, p_, k, w) in enumerate(zip(hs_, recv, kinds_, ws))]

    rows1 = (R_SH1, R_SC1, R_GT1, R_G1)
    rows3 = (R_SH3, R_SC3, R_GT3, R_G3)
    wi1, wo1 = place_local_shards(got_f1, b_f1, kinds, "place_ffn1")
    x1, g1s, u1s, y1 = ffn_fwd(xs, vecs, wi1, wo1, rows1, "ffn1_fwd")
    win, wout = place_local_shards(_after(got_mx, x1), b_mx, kinds, "place_mix")
    proj = norm_matmul(x1, vecs, win, (R_SH2, R_SC2, R_G2), "mix_in_proj")
    cq = conv_fwd(proj, convw4, conv_b, "conv_fwd")
    xr, ra, ii, hh = rnn_fwd(proj, rnnw4, rnn_conv_b, bda, bdi, b_a, b_i, lru_lambda, "rnn_fwd")
    x2, ym, ycat = mix_out(cq, proj, hh, x1, vecs, lnv, wout, "mix_out")
    wi2, wo2 = place_local_shards(_after(got_f2, x2), b_f2, kinds, "place_ffn2")
    x3, g2s, u2s, y2 = ffn_fwd(x2, vecs, wi2, wo2, rows3, "ffn2_fwd")
    dx3, vgf = final_fwd_bwd(x3, tgt, vecs, "final_loss")

    Fd = wo1.shape[0]
    tk = min(1024, S)
    dx2, act2, dg2, du2, h3b, dy2b, vg3 = ffn_bwd(dx3, x2, vecs, g2s, u2s, y2, wi2, wo2, rows3, "ffn2_bwd")
    gwo2 = matmul(act2, dy2b, "tn", tm=Fd // 2, tn=D, tk=tk, name="ffn2_dwo")
    gwi2 = matmul(h3b, dg2, "tn", tm=D, tn=Fd // 2, tk=tk, name="ffn2_dwg", out_cols=2 * Fd)
    gwi2 = matmul(h3b, du2, "tn", tm=D, tn=Fd // 2, tk=tk, name="ffn2_dwu", out_cols=2 * Fd, col_off=Fd, prev=gwi2)
    recv1_f2 = exchange_sibling_halves([gwi2, gwo2], kinds, shapes_of(w_f2), "reduce1_ffn2", seq_id=3)
    dcq, dhout, duy, dymb, vgd, vgm = mix_out_bwd(dx2, ym, vecs, wout, cq, lnv, proj, hh, "mix_out_bwd")
    gwout = matmul(ycat, dymb, "tn", tm=2 * M, tn=D, tk=tk, name="mix_dwout")
    recv1_f2 = _after(recv1_f2, gwout)
    h_f2, hb_f2 = reduce_add([gwi2, gwo2], recv1_f2, w_f2, "ffn2_")
    recv2_f2 = exchange_chip_pieces(hb_f2, kinds, shapes_of(w_f2), "reduce2_ffn2", seq_id=4)
    duv, dug, dconvw4, dconvb = conv_bwd(_after([dcq], hb_f2)[0], proj, convw4, "conv_bwd")
    dux, dwa4, dwi4, drnnw4, rvec = rnn_bwd(dhout, hh, xr, ra, ii, proj, rnnw4, bda, bdi, lru_lambda, "rnn_bwd")
    dx1, h2b, dpb, vg2 = mix_in_bwd((duv, dug, dux, duy), x1, dx2, vecs, win, "mix_in_bwd")
    gwin = matmul(h2b, dpb, "tn", tm=D, tn=1024, tk=tk, name="mix_dwin")
    recv1_mx = exchange_sibling_halves([gwin, gwout], kinds, shapes_of(w_mx), "reduce1_mix", seq_id=5)
    g_f2 = exchange_reduced_halves(reduce_sum(_after(h_f2, gwin), recv2_f2, w_f2, "ffn2_"), "reduce3_ffn2")
    adam_f2 = [adam_big(w, g, m, v, "adam_" + nm) for w, g, m, v, nm in
               zip(w_f2, g_f2, [m_w_ffn2_in[0], m_w_ffn2_out[0]], [v_w_ffn2_in[0], v_w_ffn2_out[0]], ["ffn2_in", "ffn2_out"])]
    h_mx, hb_mx = reduce_add([gwin, gwout], _after(recv1_mx, adam_f2[0][0], adam_f2[1][0]), w_mx, "mix_")
    recv2_mx = exchange_chip_pieces(hb_mx, kinds, shapes_of(w_mx), "reduce2_mix", seq_id=6)
    dx0, act1, dg1, du1, h1b, dy1b, vg1 = ffn_bwd(_after([dx1], hb_mx)[0], xs, vecs, g1s, u1s, y1, wi1, wo1, rows1, "ffn1_bwd")
    dmod_row = jnp.concatenate([vg1[1:3], vg1[0:1], vg2[0:2], vgd[0:1], vg3[1:3], vg3[0:1]], axis=0)
    gains = jnp.concatenate([vg1[3:4], vg2[2:3], vg3[3:4], vgf[2:4]], axis=0)
    mvecs = jnp.concatenate([dconvb, vgm[0:2], rvec[0:4], jnp.zeros((1, M), F32)], axis=0)
    parts = [_rows128(dmod_row), _rows128(vgf[0:2]), _rows128(gains), _rows128(mvecs),
             _rows128(dconvw4), _rows128(drnnw4), _rows128(_diag_blocks(dwa4, H, HD)), _rows128(_diag_blocks(dwi4, H, HD))]
    sizes = [p.shape[0] for p in parts]
    packed = jnp.concatenate(parts, axis=0)
    gathered = allgather_devices_hbm(packed, "gather_small", seq_id=10)

    gwo1 = matmul(_after([act1], recv2_mx, packed)[0], dy1b, "tn", tm=Fd // 2, tn=D, tk=tk, name="ffn1_dwo")
    w_f1o, w_f1i = w_f1[1:], w_f1[:1]
    recv1_f1o = exchange_sibling_halves([gwo1], ["row"], shapes_of(w_f1o), "reduce1_ffn1_out", seq_id=7)
    gwi1 = matmul(h1b, dg1, "tn", tm=D, tn=Fd // 2, tk=tk, name="ffn1_dwg", out_cols=2 * Fd)
    h_f1o, hb_f1o = reduce_add([gwo1], _after(recv1_f1o, gwi1), w_f1o, "ffn1_out", ["row"])
    recv2_f1o = exchange_chip_pieces(hb_f1o, ["row"], shapes_of(w_f1o), "reduce2_ffn1_out", seq_id=11)
    gwi1 = matmul(h1b, _after([du1], hb_f1o, gathered)[0], "tn", tm=D, tn=Fd // 2, tk=tk, name="ffn1_dwu", out_cols=2 * Fd,
                  col_off=Fd, prev=gwi1)
    recv1_f1i = exchange_sibling_halves([gwi1], ["col"], shapes_of(w_f1i), "reduce1_ffn1_in", seq_id=12)
    g_mx = exchange_reduced_halves(reduce_sum(h_mx, recv2_mx, w_mx, "mix_"), "reduce3_mix")
    summed = sum_slots(gathered, "sum_small")
    offs = [0]
    for s in sizes:
        offs.append(offs[-1] + s)
    seg = lambda k: summed[offs[k]:offs[k + 1]]
    g_b_mod = seg(0).reshape(1, 9 * D)
    g_b_fmod = seg(1).reshape(1, 2 * D)
    gsum = seg(2).reshape(5, D)
    loss = (0.5 / D) * jnp.sum(gsum[4])
    msum = seg(3).reshape(SUBLANES, M)
    g_conv_w = lax.dynamic_index_in_dim(seg(4).reshape(nb, 32, LANES), chip, axis=0, keepdims=False)[:CONV_WIDTH]
    g_rnn_w = lax.dynamic_index_in_dim(seg(5).reshape(nb, SUBLANES, LANES), chip, axis=0, keepdims=False)[:RNN_CONV_WIDTH]
    g_w_a = seg(6).reshape(H, HD, HD)
    g_w_i = seg(7).reshape(H, HD, HD)
    dmod_all = gathered[:, offs[0]:offs[1]].reshape(N_DEV, 9 * D)
    dfmod_all = gathered[:, offs[1]:offs[2]].reshape(N_DEV, 2 * D)
    dmod_cols = lax.dynamic_slice_in_dim(dmod_all, chip * n_mod, n_mod, axis=1)
    dfmod_cols = lax.dynamic_slice_in_dim(dfmod_all, chip * n_fmod, n_fmod, axis=1)

    g_w_mod, d_w_mod, nm_w_mod, nv_w_mod = adam_cond(c_all, dmod_cols, w_mod[0], m_w_mod[0], v_w_mod[0], "adam_w_mod")
    g_w_fmod, d_w_fmod, nm_w_fmod, nv_w_fmod = adam_cond(c_all, dfmod_cols, w_fmod, m_w_fmod, v_w_fmod, "adam_w_fmod")

    h_f1i, hb_f1i = reduce_add([gwi1], _after(recv1_f1i, recv2_f1o, g_w_mod, g_w_fmod), w_f1i, "ffn1_in", ["col"])
    recv2_f1i = exchange_chip_pieces(hb_f1i, ["col"], shapes_of(w_f1i), "reduce2_ffn1_in", seq_id=8)
    g_f1o = exchange_reduced_halves(reduce_sum(h_f1o, recv2_f1o, w_f1o, "ffn1_out", ["row"]), "reduce3_ffn1_out")
    g_f1i = exchange_reduced_halves(reduce_sum(h_f1i, recv2_f1i, w_f1i, "ffn1_in", ["col"]), "reduce3_ffn1_in")
    g_f1 = list(g_f1i) + list(g_f1o)

    big_w = w_f1 + w_mx + w_f2
    g_big = g_f1 + list(g_mx) + list(g_f2)
    names = ["ffn1_in", "ffn1_out", "w_in", "w_out", "ffn2_in", "ffn2_out"]
    big_m = [m_w_ffn1_in[0], m_w_ffn1_out[0], m_w_in[0], m_w_out[0]]
    big_v = [v_w_ffn1_in[0], v_w_ffn1_out[0], v_w_in[0], v_w_out[0]]
    big_out = [adam_big(w, g, m, v, "adam_" + nm) for w, g, m, v, nm in zip(big_w, g_big, big_m, big_v, names)] + adam_f2

    flat2 = lambda a: a.reshape(-1, a.shape[-1])
    small_names = ["b_mod", "g_ffn1", "g_mix", "conv_w", "conv_b", "ln_g", "ln_b", "rnn_conv_w", "rnn_conv_b", "w_a", "b_a",
                   "w_i", "b_i", "lru_lambda", "g_ffn2", "b_fmod", "g_final"]
    small_w = [b_mod, g_ffn1, g_mix, conv_w, conv_b, ln_g, ln_b, rnn_conv_w, rnn_conv_b, w_a, b_a, w_i, b_i, lru_lambda,
               g_ffn2, b_fmod, g_final]
    small_m = [m_b_mod, m_g_ffn1, m_g_mix, m_conv_w, m_conv_b, m_ln_g, m_ln_b, m_rnn_conv_w, m_rnn_conv_b, m_w_a, m_b_a,
               m_w_i, m_b_i, m_lru_lambda, m_g_ffn2, m_b_fmod, m_g_final]
    small_v = [v_b_mod, v_g_ffn1, v_g_mix, v_conv_w, v_conv_b, v_ln_g, v_ln_b, v_rnn_conv_w, v_rnn_conv_b, v_w_a, v_b_a,
               v_w_i, v_b_i, v_lru_lambda, v_g_ffn2, v_b_fmod, v_g_final]
    small_g = [g_b_mod, gsum[0:1], gsum[1:2], g_conv_w, msum[0:1], msum[1:2], msum[2:3], g_rnn_w, msum[3:4], g_w_a, msum[4:5],
               g_w_i, msum[5:6], msum[6:7], gsum[2:3], g_b_fmod, gsum[3:4]]
    small_g = [g.reshape(w.shape) for g, w in zip(small_g, small_w)]
    two_d = lambda a: a.reshape(1, -1) if a.ndim == 1 else flat2(a)
    sd, sm, sv = adam_small([two_d(a) for a in small_w], [two_d(a) for a in small_g], [two_d(a) for a in small_m],
                            [two_d(a) for a in small_v], "adam_small")
    small = {}
    for k, nm in enumerate(small_names):
        shp = small_w[k].shape
        small[nm] = (small_g[k], sd[k].reshape(shp), sm[k].reshape(shp), sv[k].reshape(shp))

    big = {"w_mod": tuple(a[None] for a in (g_w_mod, d_w_mod, nm_w_mod, nv_w_mod)),
           "w_fmod": (g_w_fmod, d_w_fmod, nm_w_fmod, nv_w_fmod)}
    for nm, full, g, (d, nmm, nvv) in zip(["w_ffn1_in", "w_ffn1_out", "w_in", "w_out", "w_ffn2_in", "w_ffn2_out"],
                                         big_w, g_big, big_out):
        big[nm] = tuple(a[None] for a in (g, d, nmm, nvv))
    order = ["w_mod", "b_mod", "g_ffn1", "w_ffn1_in", "w_ffn1_out", "g_mix", "w_in", "conv_w", "conv_b", "ln_g", "ln_b",
             "rnn_conv_w", "rnn_conv_b", "w_a", "b_a", "w_i", "b_i", "lru_lambda", "w_out", "g_ffn2", "w_ffn2_in",
             "w_ffn2_out", "w_fmod", "b_fmod", "g_final"]
    table = {**small, **big}
    outs = [loss, dx0[None]]
    for kind_ in range(4):
        outs.extend(table[nm][kind_] for nm in order)
    return tuple(outs)
```

```python
import functools

import jax
import jax.numpy as jnp
from jax import lax
from jax.experimental import pallas as pl
from jax.experimental.pallas import tpu as pltpu
from jax.experimental.pallas import tpu_sc as plsc

F32 = jnp.float32
BF = jnp.bfloat16
I32 = jnp.int32
MESH = pl.DeviceIdType.MESH

EPS = 1e-6
RG_C = 8.0
MACARON_W = 0.5
CONV_WIDTH = 31
RNN_CONV_WIDTH = 4
ADAM_LR = 0.001
ADAM_B1 = 0.9
ADAM_B2 = 0.999
ADAM_EPS = 1e-08
ADAM_WD = 0.01
ADAM_STEP = 10

LANES = 128
SUBLANES = 8
VMEM_LIMIT = 62 * 1024 * 1024
N_CHIPS = 4
N_DEV = 8

R_SH1, R_SC1, R_GT1, R_SH2, R_SC2, R_GT2, R_SH3, R_SC3, R_GT3, R_FSH, R_FSC, R_G1, R_G2, R_G3, R_GF = range(15)

CONTRACT_LAST = (((1,), (1,)), ((), ()))
CONTRACT_FIRST = (((0,), (0,)), ((), ()))


def _pcall(body, **kw):
    return pl.pallas_call(body, **kw)


def _params(sem=None, vmem=VMEM_LIMIT):
    if sem is None:
        return pltpu.CompilerParams(vmem_limit_bytes=vmem)
    return pltpu.CompilerParams(dimension_semantics=sem, vmem_limit_bytes=vmem)


def _row(ref, r):
    return ref[r:r + 1, :]


def _sigmoid(x):
    return 1.0 / (1.0 + jnp.exp(-x))


def _colsum(x):
    return jnp.sum(x, axis=0, keepdims=True)


def _rowmean(x):
    return jnp.mean(x, axis=-1, keepdims=True)


def matmul(a, b, mode, *, tm, tn, tk, name, out_dtype=F32, out_cols=None, col_off=0, prev=None):
    if mode == "nn":
        (M, K), (K2, N) = a.shape, b.shape
    elif mode == "nt":
        (M, K), (N, K2) = a.shape, b.shape
    else:
        (K, M), (K2, N) = a.shape, b.shape
    assert K == K2 and M % tm == 0 and N % tn == 0 and K % tk == 0 and col_off % tn == 0
    nk = K // tk
    out_cols = N if out_cols is None else out_cols
    off = col_off // tn

    def body(*refs):
        if prev is None:
            a_ref, b_ref, o_ref, acc = refs
        else:
            a_ref, b_ref, _, o_ref, acc = refs
        k = pl.program_id(2)

        @pl.when(k == 0)
        def _():
            acc[...] = jnp.zeros_like(acc)

        av = a_ref[...].astype(BF)
        bv = b_ref[...].astype(BF)
        if mode == "nn":
            acc[...] += jnp.dot(av, bv, preferred_element_type=F32)
        elif mode == "nt":
            acc[...] += lax.dot_general(av, bv, CONTRACT_LAST, preferred_element_type=F32)
        else:
            acc[...] += lax.dot_general(av, bv, CONTRACT_FIRST, preferred_element_type=F32)

        @pl.when(k == nk - 1)
        def _():
            o_ref[...] = acc[...].astype(out_dtype)

    if mode == "nn":
        a_spec = pl.BlockSpec((tm, tk), lambda m, n, k: (m, k))
        b_spec = pl.BlockSpec((tk, tn), lambda m, n, k: (k, n))
    elif mode == "nt":
        a_spec = pl.BlockSpec((tm, tk), lambda m, n, k: (m, k))
        b_spec = pl.BlockSpec((tn, tk), lambda m, n, k: (n, k))
    else:
        a_spec = pl.BlockSpec((tk, tm), lambda m, n, k: (k, m))
        b_spec = pl.BlockSpec((tk, tn), lambda m, n, k: (k, n))
    in_specs = [a_spec, b_spec]
    args = [a, b]
    aliases = {}
    if prev is not None:
        in_specs.append(pl.BlockSpec(memory_space=pl.ANY))
        args.append(prev)
        aliases = {2: 0}
    return _pcall(
        body, name=name, grid=(M // tm, N // tn, nk), in_specs=in_specs,
        out_specs=pl.BlockSpec((tm, tn), lambda m, n, k: (m, n + off)),
        out_shape=jax.ShapeDtypeStruct((M, out_cols), out_dtype),
        scratch_shapes=[pltpu.VMEM((tm, tn), F32)], input_output_aliases=aliases,
        compiler_params=_params(("parallel", "parallel", "arbitrary")),
    )(*args)


def cond_matmul(c_all, w, name):
    B, K = c_all.shape
    N = w.shape[1]
    tn = 256
    assert N % tn == 0

    def body(c_ref, w_ref, o_ref):
        cv = c_ref[...]
        ca = cv * _sigmoid(cv)
        o_ref[...] = jnp.dot(ca, w_ref[...], preferred_element_type=F32, precision=lax.Precision.HIGHEST)

    return _pcall(
        body, name=name, grid=(N // tn,),
        in_specs=[pl.BlockSpec((B, K), lambda n: (0, 0)), pl.BlockSpec((K, tn), lambda n: (0, n))],
        out_specs=pl.BlockSpec((B, tn), lambda n: (0, n)),
        out_shape=jax.ShapeDtypeStruct((B, N), F32), compiler_params=_params(("parallel",)),
    )(c_all, w)


FFN_FWD_TS = 512
FFN_BWD_TS = 256


def _resident(shape, index_map):
    return pl.BlockSpec(shape, index_map, pipeline_mode=pl.Buffered(1))


def ffn_fwd(x, vecs, wi, wo, rows, name):
    r_sh, r_sc, r_gt, r_g = rows
    S, D = x.shape
    Fd = wo.shape[0]
    ts = min(FFN_FWD_TS, S)

    def body(x_ref, v_ref, wg_ref, wu_ref, wo_ref, xo_ref, g_ref, u_ref, y_ref):
        xv = x_ref[...]
        r = lax.rsqrt(_rowmean(xv * xv) + EPS)
        gs = _row(v_ref, r_g) * (1.0 + _row(v_ref, r_sc))
        hb = (xv * r * gs + _row(v_ref, r_sh)).astype(BF)
        G = jnp.dot(hb, wg_ref[...], preferred_element_type=F32)
        U = jnp.dot(hb, wu_ref[...], preferred_element_type=F32)
        g_ref[...] = G.astype(BF)
        u_ref[...] = U.astype(BF)
        act = (G * _sigmoid(G) * U).astype(BF)
        Y = jnp.dot(act, wo_ref[...], preferred_element_type=F32)
        y_ref[...] = Y
        xo_ref[...] = xv + (MACARON_W * _row(v_ref, r_gt)) * Y

    tok = pl.BlockSpec((ts, D), lambda i: (i, 0))
    hid = pl.BlockSpec((ts, Fd), lambda i: (i, 0))
    return _pcall(
        body, name=name, grid=(S // ts,),
        in_specs=[tok, pl.BlockSpec(vecs.shape, lambda i: (0, 0)), _resident((D, Fd), lambda i: (0, 0)),
                  _resident((D, Fd), lambda i: (0, 1)), _resident((Fd, D), lambda i: (0, 0))],
        out_specs=[tok, hid, hid, tok],
        out_shape=[jax.ShapeDtypeStruct((S, D), F32), jax.ShapeDtypeStruct((S, Fd), BF),
                   jax.ShapeDtypeStruct((S, Fd), BF), jax.ShapeDtypeStruct((S, D), F32)],
        compiler_params=_params(("arbitrary",)),
    )(x, vecs, wi, wi, wo)


def ffn_bwd(dxo, x, vecs, gs_, us_, y, wi, wo, rows, name):
    r_sh, r_sc, r_gt, r_g = rows
    S, D = x.shape
    Fd = wo.shape[0]
    ts = min(FFN_BWD_TS, S)

    def body(dxo_ref, x_ref, v_ref, g_ref, u_ref, y_ref, wg_ref, wu_ref, wo_ref,
             dx_ref, act_ref, dg_ref, du_ref, hb_ref, dyb_ref, vg_ref):
        @pl.when(pl.program_id(0) == 0)
        def _():
            vg_ref[...] = jnp.zeros_like(vg_ref)

        dxo_v = dxo_ref[...]
        dyb = ((MACARON_W * _row(v_ref, r_gt)) * dxo_v).astype(BF)
        dyb_ref[...] = dyb
        vg_ref[0:1, :] += MACARON_W * _colsum(dxo_v * y_ref[...])
        dA = lax.dot_general(dyb, wo_ref[...], CONTRACT_LAST, preferred_element_type=F32)
        G = g_ref[...].astype(F32)
        U = u_ref[...].astype(F32)
        sg = _sigmoid(G)
        sl = G * sg
        dU = (dA * sl).astype(BF)
        dG = (dA * U * (sg * (1.0 + G * (1.0 - sg)))).astype(BF)
        act_ref[...] = (sl * U).astype(BF)
        dg_ref[...] = dG
        du_ref[...] = dU
        dh = (lax.dot_general(dG, wg_ref[...], CONTRACT_LAST, preferred_element_type=F32)
              + lax.dot_general(dU, wu_ref[...], CONTRACT_LAST, preferred_element_type=F32))
        xv = x_ref[...]
        r = lax.rsqrt(_rowmean(xv * xv) + EPS)
        n = xv * r
        g = _row(v_ref, r_g)
        sc1 = 1.0 + _row(v_ref, r_sc)
        gsc = g * sc1
        hb_ref[...] = (n * gsc + _row(v_ref, r_sh)).astype(BF)
        dhn = dh * n
        vg_ref[1:2, :] += _colsum(dh)
        vg_ref[2:3, :] += _colsum(dhn) * g
        vg_ref[3:4, :] += _colsum(dhn) * sc1
        dn = dh * gsc
        dx_ref[...] = dxo_v + r * (dn - n * _rowmean(dn * n))

    tok = pl.BlockSpec((ts, D), lambda i: (i, 0))
    hid = pl.BlockSpec((ts, Fd), lambda i: (i, 0))
    return _pcall(
        body, name=name, grid=(S // ts,),
        in_specs=[tok, tok, pl.BlockSpec(vecs.shape, lambda i: (0, 0)), hid, hid, tok, _resident((D, Fd), lambda i: (0, 0)),
                  _resident((D, Fd), lambda i: (0, 1)), _resident((Fd, D), lambda i: (0, 0))],
        out_specs=[tok, hid, hid, hid, tok, tok, pl.BlockSpec((SUBLANES, D), lambda i: (0, 0))],
        out_shape=[jax.ShapeDtypeStruct((S, D), F32), jax.ShapeDtypeStruct((S, Fd), BF),
                   jax.ShapeDtypeStruct((S, Fd), BF), jax.ShapeDtypeStruct((S, Fd), BF),
                   jax.ShapeDtypeStruct((S, D), BF), jax.ShapeDtypeStruct((S, D), BF),
                   jax.ShapeDtypeStruct((SUBLANES, D), F32)],
        compiler_params=_params(("arbitrary",)),
    )(dxo, x, vecs, gs_, us_, y, wi, wi, wo)


def final_fwd_bwd(x, tgt, vecs, name):
    S, D = x.shape
    ts = min(512, S)

    def body(x_ref, t_ref, v_ref, dx_ref, vg_ref):
        @pl.when(pl.program_id(0) == 0)
        def _():
            vg_ref[...] = jnp.zeros_like(vg_ref)

        xv = x_ref[...]
        r = lax.rsqrt(_rowmean(xv * xv) + EPS)
        n = xv * r
        g = _row(v_ref, R_GF)
        sc1 = 1.0 + _row(v_ref, R_FSC)
        gsc = g * sc1
        e = n * gsc + _row(v_ref, R_FSH) - t_ref[...]
        vg_ref[3:4, :] += _colsum(e * e)
        dout = e * (1.0 / D)
        dn_ = dout * n
        vg_ref[0:1, :] += _colsum(dout)
        vg_ref[1:2, :] += _colsum(dn_) * g
        vg_ref[2:3, :] += _colsum(dn_) * sc1
        dn = dout * gsc
        dx_ref[...] = r * (dn - n * _rowmean(dn * n))

    tok = pl.BlockSpec((ts, D), lambda i: (i, 0))
    return _pcall(
        body, name=name, grid=(S // ts,),
        in_specs=[tok, tok, pl.BlockSpec(vecs.shape, lambda i: (0, 0))],
        out_specs=[tok, pl.BlockSpec((SUBLANES, D), lambda i: (0, 0))],
        out_shape=[jax.ShapeDtypeStruct((S, D), F32), jax.ShapeDtypeStruct((SUBLANES, D), F32)],
        compiler_params=_params(("arbitrary",)),
    )(x, tgt, vecs)


def norm_matmul(x, vecs, w, rows, name):
    r_sh, r_sc, r_g = rows
    S, D = x.shape
    N = w.shape[1]
    ts = min(512, S)

    def body(x_ref, v_ref, w_ref, o_ref):
        xv = x_ref[...]
        r = lax.rsqrt(_rowmean(xv * xv) + EPS)
        gs = _row(v_ref, r_g) * (1.0 + _row(v_ref, r_sc))
        hb = (xv * r * gs + _row(v_ref, r_sh)).astype(BF)
        o_ref[...] = jnp.dot(hb, w_ref[...], preferred_element_type=F32)

    return _pcall(
        body, name=name, grid=(S // ts,),
        in_specs=[pl.BlockSpec((ts, D), lambda i: (i, 0)), pl.BlockSpec(vecs.shape, lambda i: (0, 0)),
                  _resident((D, N), lambda i: (0, 0))],
        out_specs=pl.BlockSpec((ts, N), lambda i: (i, 0)),
        out_shape=jax.ShapeDtypeStruct((S, N), F32),
        compiler_params=_params(("arbitrary",)),
    )(x, vecs, w)


SEQ_TT = 256
SCAN_SEGMENTS = 4
CONV_PAD = 32


def conv_fwd(proj, convw4, conv_b, name):
    S = proj.shape[0]
    M = conv_b.shape[1]
    nb = M // LANES
    tt = min(SEQ_TT, S)

    def body(uv_ref, ug_ref, w_ref, b_ref, cq_ref, qp):
        qp[0:CONV_PAD, :] = jnp.zeros((CONV_PAD, LANES), F32)

        def step(t, carry):
            base = pl.multiple_of(t * tt, tt)
            qp[pl.ds(base + CONV_PAD, tt), :] = uv_ref[pl.ds(base, tt), :] * _sigmoid(ug_ref[pl.ds(base, tt), :])
            acc = jnp.broadcast_to(b_ref[...], (tt, LANES))
            for k in range(CONV_WIDTH):
                acc = acc + w_ref[k:k + 1, :] * qp[pl.ds(base + (CONV_PAD - CONV_WIDTH + 1) + k, tt), :]
            cq_ref[pl.ds(base, tt), :] = acc
            return carry

        lax.fori_loop(0, S // tt, step, 0)

    return _pcall(
        body, name=name, grid=(nb,),
        in_specs=[pl.BlockSpec((S, LANES), lambda c: (0, c)), pl.BlockSpec((S, LANES), lambda c: (0, c + nb)),
                  pl.BlockSpec((None, 32, LANES), lambda c: (c, 0, 0)), pl.BlockSpec((1, LANES), lambda c: (0, c))],
        out_specs=pl.BlockSpec((S, LANES), lambda c: (0, c)),
        out_shape=jax.ShapeDtypeStruct((S, M), F32),
        scratch_shapes=[pltpu.VMEM((S + CONV_PAD, LANES), F32)],
        compiler_params=_params(("arbitrary",)),
    )(proj, proj, convw4, conv_b)


def conv_bwd(dcq, proj, convw4, name):
    S, M = dcq.shape
    nb = M // LANES
    tt = min(SEQ_TT, S)
    off = CONV_PAD - CONV_WIDTH + 1

    def body(dcq_ref, uv_ref, ug_ref, w_ref, duv_ref, dug_ref, dw_ref, db_ref, qp, dp, dw8, db8):
        qp[0:CONV_PAD, :] = jnp.zeros((CONV_PAD, LANES), F32)
        dp[S:S + CONV_PAD, :] = jnp.zeros((CONV_PAD, LANES), F32)
        dw8[...] = jnp.zeros_like(dw8)
        db8[...] = jnp.zeros_like(db8)

        def fill(t, carry):
            base = pl.multiple_of(t * tt, tt)
            qp[pl.ds(base + CONV_PAD, tt), :] = uv_ref[pl.ds(base, tt), :] * _sigmoid(ug_ref[pl.ds(base, tt), :])
            dp[pl.ds(base, tt), :] = dcq_ref[pl.ds(base, tt), :]
            return carry

        lax.fori_loop(0, S // tt, fill, 0)

        def step(t, carry):
            base = pl.multiple_of(t * tt, tt)
            d_t = dcq_ref[pl.ds(base, tt), :]
            db8[...] += d_t.reshape(tt // SUBLANES, SUBLANES, LANES).sum(axis=0)
            dq = jnp.zeros((tt, LANES), F32)
            for k in range(CONV_WIDTH):
                prod = d_t * qp[pl.ds(base + off + k, tt), :]
                dw8[k] += prod.reshape(tt // SUBLANES, SUBLANES, LANES).sum(axis=0)
                dq = dq + w_ref[k:k + 1, :] * dp[pl.ds(base + (CONV_WIDTH - 1) - k, tt), :]
            uv = uv_ref[pl.ds(base, tt), :]
            sg = _sigmoid(ug_ref[pl.ds(base, tt), :])
            duv_ref[pl.ds(base, tt), :] = (dq * sg).astype(BF)
            dug_ref[pl.ds(base, tt), :] = (dq * uv * sg * (1.0 - sg)).astype(BF)
            return carry

        lax.fori_loop(0, S // tt, step, 0)
        dw_ref[...] = jnp.zeros_like(dw_ref)
        for k in range(CONV_WIDTH):
            dw_ref[k:k + 1, :] = _colsum(dw8[k])
        db_ref[...] = _colsum(db8[...])

    col = lambda o: pl.BlockSpec((S, LANES), lambda c: (0, c + o))
    return _pcall(
        body, name=name, grid=(nb,),
        in_specs=[col(0), col(0), col(nb), pl.BlockSpec((None, 32, LANES), lambda c: (c, 0, 0))],
        out_specs=[col(0), col(0), pl.BlockSpec((None, 32, LANES), lambda c: (c, 0, 0)),
                   pl.BlockSpec((1, LANES), lambda c: (0, c))],
        out_shape=[jax.ShapeDtypeStruct((S, M), BF), jax.ShapeDtypeStruct((S, M), BF),
                   jax.ShapeDtypeStruct((nb, 32, LANES), F32), jax.ShapeDtypeStruct((1, M), F32)],
        scratch_shapes=[pltpu.VMEM((S + CONV_PAD, LANES), F32), pltpu.VMEM((S + CONV_PAD, LANES), F32),
                        pltpu.VMEM((32, SUBLANES, LANES), F32), pltpu.VMEM((SUBLANES, LANES), F32)],
        compiler_params=_params(("arbitrary",)),
    )(dcq, proj, proj, convw4)


def _log_sigmoid(x):
    return jnp.minimum(x, 0.0) - jnp.log(1.0 + jnp.exp(-jnp.abs(x)))


def _rg_gate_terms(ra, ls):
    la = RG_C * ra * ls
    a = jnp.exp(la)
    th = jnp.tanh(la)
    mult = jnp.sqrt(-2.0 * th / (1.0 - th))
    return a, mult


def rnn_fwd(proj, rnnw4, rnn_b, bda, bdi, b_a, b_i, lam, name):
    S = proj.shape[0]
    M = rnn_b.shape[1]
    nb = M // LANES
    tt = min(SEQ_TT, S)
    KW = RNN_CONV_WIDTH
    nseg = SCAN_SEGMENTS if S % (SCAN_SEGMENTS * tt) == 0 else 1

    def body(ux_ref, w_ref, rb_ref, bda_ref, bdi_ref, ba_ref, bi_ref, lam_ref,
             xr_ref, ra_ref, ii_ref, h_ref, uxp, a_sc, b_sc):
        uxp[0:SUBLANES, :] = jnp.zeros((SUBLANES, LANES), F32)
        ls = _log_sigmoid(lam_ref[...])

        def step(t, carry):
            base = pl.multiple_of(t * tt, tt)
            uxp[pl.ds(base + SUBLANES, tt), :] = ux_ref[pl.ds(base, tt), :]
            xr = jnp.broadcast_to(rb_ref[...], (tt, LANES))
            for k in range(KW):
                xr = xr + w_ref[k:k + 1, :] * uxp[pl.ds(base + (SUBLANES - KW + 1) + k, tt), :]
            xb = xr.astype(BF)
            ra = _sigmoid(jnp.dot(xb, bda_ref[...], preferred_element_type=F32) + ba_ref[...])
            ii = _sigmoid(jnp.dot(xb, bdi_ref[...], preferred_element_type=F32) + bi_ref[...])
            a, mult = _rg_gate_terms(ra, ls)
            xr_ref[pl.ds(base, tt), :] = xr
            ra_ref[pl.ds(base, tt), :] = ra
            ii_ref[pl.ds(base, tt), :] = ii
            a_sc[pl.ds(base, tt), :] = a
            b_sc[pl.ds(base, tt), :] = mult * (ii * xr)
            return carry

        lax.fori_loop(0, S // tt, step, 0)

        rows = lax.broadcasted_iota(I32, (SUBLANES, LANES), 0)
        seg = S // nseg
        last = lambda v: jnp.broadcast_to(v[SUBLANES - 1:SUBLANES, :], (SUBLANES, LANES))

        def scan(t, carry):
            hs, ps = carry
            new_h, new_p = [], []
            for s in range(nseg):
                base = pl.multiple_of(s * seg + t * SUBLANES, SUBLANES)
                A = a_sc[pl.ds(base, SUBLANES), :]
                B = b_sc[pl.ds(base, SUBLANES), :]
                for d in (1, 2, 4):
                    As = jnp.where(rows >= d, pltpu.roll(A, d, axis=0), 1.0)
                    Bs = jnp.where(rows >= d, pltpu.roll(B, d, axis=0), 0.0)
                    B = A * Bs + B
                    A = A * As
                hh = B + A * hs[s]
                h_ref[pl.ds(base, SUBLANES), :] = hh
                pp = A * ps[s]
                if s > 0:
                    a_sc[pl.ds(base, SUBLANES), :] = pp
                new_h.append(last(hh))
                new_p.append(last(pp))
            return tuple(new_h), tuple(new_p)

        zero8 = jnp.zeros((SUBLANES, LANES), F32)
        one8 = jnp.ones((SUBLANES, LANES), F32)
        hs, ps = lax.fori_loop(0, seg // SUBLANES, scan, ((zero8,) * nseg, (one8,) * nseg))
        carry_in = hs[0]
        for s in range(1, nseg):
            c_row = carry_in[0:1, :]

            def fix(t, c, s=s, c_row=c_row):
                base = pl.multiple_of(s * seg + t * tt, tt)
                h_ref[pl.ds(base, tt), :] = h_ref[pl.ds(base, tt), :] + a_sc[pl.ds(base, tt), :] * c_row
                return c

            lax.fori_loop(0, seg // tt, fix, 0)
            carry_in = hs[s] + ps[s] * carry_in

    col = lambda o: pl.BlockSpec((S, LANES), lambda c: (0, c + o))
    vec = pl.BlockSpec((1, LANES), lambda c: (0, c))
    diag = pl.BlockSpec((LANES, LANES), lambda c: (c, c))
    return _pcall(
        body, name=name, grid=(nb,),
        in_specs=[col(2 * nb), pl.BlockSpec((None, SUBLANES, LANES), lambda c: (c, 0, 0)), vec, diag, diag, vec, vec, vec],
        out_specs=[col(0)] * 4,
        out_shape=[jax.ShapeDtypeStruct((S, M), F32)] * 4,
        scratch_shapes=[pltpu.VMEM((S + SUBLANES, LANES), F32), pltpu.VMEM((S, LANES), F32), pltpu.VMEM((S, LANES), F32)],
        compiler_params=_params(("arbitrary",)),
    )(proj, rnnw4, rnn_b, bda, bdi, b_a, b_i, lam)


def rnn_bwd(dhout, h, xr, ra, ii, proj, rnnw4, bda, bdi, lam, name):
    S, M = h.shape
    nb = M // LANES
    tt = min(SEQ_TT, S)
    KW = RNN_CONV_WIDTH
    SL = SUBLANES
    nseg = SCAN_SEGMENTS if S % (SCAN_SEGMENTS * tt) == 0 else 1

    def body(dh_ref, h_ref, xr_ref, ra_ref, ii_ref, ux_ref, w_ref, bda_ref, bdi_ref, lam_ref,
             dux_ref, dwa_ref, dwi_ref, drw_ref, vec_ref,
             a_sc, hp, g_sc, dpa_sc, dpi_sc, dxp, uxp, acc8, drw8, p_sc):
        zero8 = jnp.zeros((SL, LANES), F32)
        a_sc[S:S + SL, :] = zero8
        hp[0:SL, :] = zero8
        dxp[S:S + SL, :] = zero8
        uxp[0:SL, :] = zero8
        acc8[...] = jnp.zeros_like(acc8)
        drw8[...] = jnp.zeros_like(drw8)
        lamv = lam_ref[...]
        ls = _log_sigmoid(lamv)

        def fill(t, carry):
            base = pl.multiple_of(t * tt, tt)
            a_sc[pl.ds(base, tt), :] = jnp.exp(RG_C * ra_ref[pl.ds(base, tt), :] * ls)
            hp[pl.ds(base + SL, tt), :] = h_ref[pl.ds(base, tt), :]
            uxp[pl.ds(base + SL, tt), :] = ux_ref[pl.ds(base, tt), :]
            return carry

        lax.fori_loop(0, S // tt, fill, 0)

        rows = lax.broadcasted_iota(I32, (SL, LANES), 0)
        seg = S // nseg
        nt8 = seg // SL
        first = lambda v: jnp.broadcast_to(v[0:1, :], (SL, LANES))

        def rscan(t, carry):
            gs, ps = carry
            new_g, new_p = [], []
            for s in range(nseg):
                base = pl.multiple_of(s * seg + (nt8 - 1 - t) * SL, SL)
                A = a_sc[pl.ds(base + 1, SL), :]
                B = dh_ref[pl.ds(base, SL), :]
                for d in (1, 2, 4):
                    As = jnp.where(rows < SL - d, pltpu.roll(A, SL - d, axis=0), 1.0)
                    Bs = jnp.where(rows < SL - d, pltpu.roll(B, SL - d, axis=0), 0.0)
                    B = A * Bs + B
                    A = A * As
                g = B + A * gs[s]
                g_sc[pl.ds(base, SL), :] = g
                pp = A * ps[s]
                if s < nseg - 1:
                    p_sc[pl.ds(base, SL), :] = pp
                new_g.append(first(g))
                new_p.append(first(pp))
            return tuple(new_g), tuple(new_p)

        one8 = jnp.ones((SL, LANES), F32)
        gs, ps = lax.fori_loop(0, nt8, rscan, ((zero8,) * nseg, (one8,) * nseg))
        carry_in = gs[nseg - 1]
        for s in range(nseg - 2, -1, -1):
            c_row = carry_in[0:1, :]

            def fix(t, c, s=s, c_row=c_row):
                base = pl.multiple_of(s * seg + t * tt, tt)
                g_sc[pl.ds(base, tt), :] = g_sc[pl.ds(base, tt), :] + p_sc[pl.ds(base, tt), :] * c_row
                return c

            lax.fori_loop(0, seg // tt, fix, 0)
            carry_in = gs[s] + ps[s] * carry_in

        def red8(v):
            return v.reshape(tt // SL, SL, LANES).sum(axis=0)

        def step(t, carry):
            base = pl.multiple_of(t * tt, tt)
            g = g_sc[pl.ds(base, tt), :]
            hprev = hp[pl.ds(base + SL - 1, tt), :]
            xr_t = xr_ref[pl.ds(base, tt), :]
            ra_t = ra_ref[pl.ds(base, tt), :]
            ii_t = ii_ref[pl.ds(base, tt), :]
            a, mult = _rg_gate_terms(ra_t, ls)
            gx = g * xr_t
            dmult = gx * ii_t
            dii = gx * mult
            dxr = g * (mult * ii_t)
            dla = g * hprev * a - dmult * (a * a) / mult
            acc8[3] += red8(dla * ra_t)
            dpa = dla * (RG_C * ls) * ra_t * (1.0 - ra_t)
            dpi = dii * ii_t * (1.0 - ii_t)
            dpab = dpa.astype(BF)
            dpib = dpi.astype(BF)
            dxr = dxr + (lax.dot_general(dpab, bda_ref[...], CONTRACT_LAST, preferred_element_type=F32)
                         + lax.dot_general(dpib, bdi_ref[...], CONTRACT_LAST, preferred_element_type=F32))
            dpa_sc[pl.ds(base, tt), :] = dpab
            dpi_sc[pl.ds(base, tt), :] = dpib
            dxp[pl.ds(base, tt), :] = dxr
            acc8[0] += red8(dxr)
            acc8[1] += red8(dpa)
            acc8[2] += red8(dpi)
            return carry

        lax.fori_loop(0, S // tt, step, 0)

        def convb(t, carry):
            base = pl.multiple_of(t * tt, tt)
            d_t = dxp[pl.ds(base, tt), :]
            dux = jnp.zeros((tt, LANES), F32)
            for k in range(KW):
                drw8[k] += red8(d_t * uxp[pl.ds(base + (SL - KW + 1) + k, tt), :])
                dux = dux + w_ref[k:k + 1, :] * dxp[pl.ds(base + (KW - 1) - k, tt), :]
            dux_ref[pl.ds(base, tt), :] = dux.astype(BF)
            return carry

        lax.fori_loop(0, S // tt, convb, 0)

        xb = xr_ref[...].astype(BF)
        dwa_ref[...] = lax.dot_general(xb, dpa_sc[...], CONTRACT_FIRST, preferred_element_type=F32)
        dwi_ref[...] = lax.dot_general(xb, dpi_sc[...], CONTRACT_FIRST, preferred_element_type=F32)
        drw_ref[...] = jnp.zeros_like(drw_ref)
        vec_ref[...] = jnp.zeros_like(vec_ref)
        for k in range(KW):
            drw_ref[k:k + 1, :] = _colsum(drw8[k])
        for k in range(3):
            vec_ref[k:k + 1, :] = _colsum(acc8[k])
        vec_ref[3:4, :] = _colsum(acc8[3]) * (RG_C * _sigmoid(-lamv))

    col = lambda o: pl.BlockSpec((S, LANES), lambda c: (0, c + o))
    vec = pl.BlockSpec((1, LANES), lambda c: (0, c))
    diag = pl.BlockSpec((LANES, LANES), lambda c: (c, c))
    blk3 = lambda r: pl.BlockSpec((None, r, LANES), lambda c: (c, 0, 0))
    return _pcall(
        body, name=name, grid=(nb,),
        in_specs=[col(0), col(0), col(0), col(0), col(0), col(2 * nb), blk3(SL), diag, diag, vec],
        out_specs=[col(0), blk3(LANES), blk3(LANES), blk3(SL), pl.BlockSpec((SL, LANES), lambda c: (0, c))],
        out_shape=[jax.ShapeDtypeStruct((S, M), BF), jax.ShapeDtypeStruct((nb, LANES, LANES), F32),
                   jax.ShapeDtypeStruct((nb, LANES, LANES), F32), jax.ShapeDtypeStruct((nb, SL, LANES), F32),
                   jax.ShapeDtypeStruct((SL, M), F32)],
        scratch_shapes=[pltpu.VMEM((S + SL, LANES), F32), pltpu.VMEM((S + SL, LANES), F32), pltpu.VMEM((S, LANES), F32),
                        pltpu.VMEM((S, LANES), BF), pltpu.VMEM((S, LANES), BF), pltpu.VMEM((S + SL, LANES), F32),
                        pltpu.VMEM((S + SL, LANES), F32), pltpu.VMEM((SL, SL, LANES), F32), pltpu.VMEM((SL, SL, LANES), F32),
                        pltpu.VMEM((S, LANES), F32)],
        compiler_params=_params(("arbitrary",)),
    )(dhout, h, xr, ra, ii, proj, rnnw4, bda, bdi, lam)


GELU_K = 0.7978845608028654
GELU_C = 0.044715


def _layernorm_parts(cq):
    mu = _rowmean(cq)
    d = cq - mu
    rstd = lax.rsqrt(_rowmean(d * d) + EPS)
    return d * rstd, rstd


def mix_out(cq, proj, h, x, vecs, lnv, wout, name):
    S, D = x.shape
    M = cq.shape[1]
    ts = min(512, S)

    def body(cq_ref, uy_ref, h_ref, x_ref, v_ref, ln_ref, w_ref, xo_ref, ym_ref, yc_ref):
        z, _ = _layernorm_parts(cq_ref[...])
        l = z * _row(ln_ref, 0) + _row(ln_ref, 1)
        yc_ref[:, 0:M] = (l * _sigmoid(l)).astype(BF)
        uy = uy_ref[...]
        gelu = 0.5 * uy * (1.0 + jnp.tanh(GELU_K * (uy + GELU_C * uy * uy * uy)))
        yc_ref[:, M:2 * M] = (gelu * h_ref[...]).astype(BF)
        ym = jnp.dot(yc_ref[...], w_ref[...], preferred_element_type=F32)
        ym_ref[...] = ym
        xo_ref[...] = x_ref[...] + _row(v_ref, R_GT2) * ym

    tok = pl.BlockSpec((ts, D), lambda i: (i, 0))
    mtok = lambda o: pl.BlockSpec((ts, M), lambda i: (i, o))
    return _pcall(
        body, name=name, grid=(S // ts,),
        in_specs=[mtok(0), mtok(3), mtok(0), tok, pl.BlockSpec(vecs.shape, lambda i: (0, 0)),
                  pl.BlockSpec(lnv.shape, lambda i: (0, 0)), pl.BlockSpec(wout.shape, lambda i: (0, 0))],
        out_specs=[tok, tok, pl.BlockSpec((ts, 2 * M), lambda i: (i, 0))],
        out_shape=[jax.ShapeDtypeStruct((S, D), F32), jax.ShapeDtypeStruct((S, D), F32),
                   jax.ShapeDtypeStruct((S, 2 * M), BF)],
        compiler_params=_params(("arbitrary",)),
    )(cq, proj, h, x, vecs, lnv, wout)


def mix_out_bwd(dxo, ym, vecs, wout, cq, lnv, proj, h, name):
    S, D = dxo.shape
    M = cq.shape[1]
    ts = min(512, S)

    def body(dxo_ref, ym_ref, v_ref, w_ref, cq_ref, ln_ref, uy_ref, h_ref,
             dcq_ref, dh_ref, duy_ref, dyb_ref, vgd_ref, vgm_ref):
        @pl.when(pl.program_id(0) == 0)
        def _():
            vgd_ref[...] = jnp.zeros_like(vgd_ref)
            vgm_ref[...] = jnp.zeros_like(vgm_ref)

        dxo_v = dxo_ref[...]
        dyb = (_row(v_ref, R_GT2) * dxo_v).astype(BF)
        dyb_ref[...] = dyb
        vgd_ref[0:1, :] += _colsum(dxo_v * ym_ref[...])
        dycat = lax.dot_general(dyb, w_ref[...], CONTRACT_LAST, preferred_element_type=F32)
        dyc = dycat[:, 0:M]
        dyr = dycat[:, M:2 * M]
        z, rstd = _layernorm_parts(cq_ref[...])
        lng = _row(ln_ref, 0)
        l = z * lng + _row(ln_ref, 1)
        sl = _sigmoid(l)
        dl = dyc * (sl * (1.0 + l * (1.0 - sl)))
        vgm_ref[0:1, :] += _colsum(dl * z)
        vgm_ref[1:2, :] += _colsum(dl)
        dz = dl * lng
        dcq_ref[...] = rstd * (dz - _rowmean(dz) - z * _rowmean(dz * z))
        uy = uy_ref[...]
        u2 = uy * uy
        th = jnp.tanh(GELU_K * (uy + GELU_C * uy * u2))
        gelu = 0.5 * uy * (1.0 + th)
        dgelu = 0.5 * (1.0 + th) + 0.5 * uy * (1.0 - th * th) * (GELU_K * (1.0 + 3.0 * GELU_C * u2))
        dh_ref[...] = dyr * gelu
        duy_ref[...] = (dyr * h_ref[...] * dgelu).astype(BF)

    tok = pl.BlockSpec((ts, D), lambda i: (i, 0))
    mtok = lambda o: pl.BlockSpec((ts, M), lambda i: (i, o))
    return _pcall(
        body, name=name, grid=(S // ts,),
        in_specs=[tok, tok, pl.BlockSpec(vecs.shape, lambda i: (0, 0)), pl.BlockSpec(wout.shape, lambda i: (0, 0)),
                  mtok(0), pl.BlockSpec(lnv.shape, lambda i: (0, 0)), mtok(3), mtok(0)],
        out_specs=[mtok(0), mtok(0), mtok(0), tok, pl.BlockSpec((SUBLANES, D), lambda i: (0, 0)),
                   pl.BlockSpec((SUBLANES, M), lambda i: (0, 0))],
        out_shape=[jax.ShapeDtypeStruct((S, M), F32)] * 2 + [jax.ShapeDtypeStruct((S, M), BF), jax.ShapeDtypeStruct((S, D), BF),
                   jax.ShapeDtypeStruct((SUBLANES, D), F32), jax.ShapeDtypeStruct((SUBLANES, M), F32)],
        compiler_params=_params(("arbitrary",)),
    )(dxo, ym, vecs, wout, cq, lnv, proj, h)


def mix_in_bwd(dparts, x, dxo, vecs, win, name):
    S, D = x.shape
    M = dparts[0].shape[1]
    ts = min(512, S)

    def body(d0, d1, d2, d3, x_ref, dxo_ref, v_ref, w_ref, dx_ref, hb_ref, dp_ref, vg_ref):
        @pl.when(pl.program_id(0) == 0)
        def _():
            vg_ref[...] = jnp.zeros_like(vg_ref)

        for q, dref in enumerate((d0, d1, d2, d3)):
            dp_ref[:, q * M:(q + 1) * M] = dref[...].astype(BF)
        dh = lax.dot_general(dp_ref[...], w_ref[...], CONTRACT_LAST, preferred_element_type=F32)
        xv = x_ref[...]
        r = lax.rsqrt(_rowmean(xv * xv) + EPS)
        n = xv * r
        g = _row(v_ref, R_G2)
        sc1 = 1.0 + _row(v_ref, R_SC2)
        gsc = g * sc1
        hb_ref[...] = (n * gsc + _row(v_ref, R_SH2)).astype(BF)
        dhn = dh * n
        vg_ref[0:1, :] += _colsum(dh)
        vg_ref[1:2, :] += _colsum(dhn) * g
        vg_ref[2:3, :] += _colsum(dhn) * sc1
        dn = dh * gsc
        dx_ref[...] = dxo_ref[...] + r * (dn - n * _rowmean(dn * n))

    tok = pl.BlockSpec((ts, D), lambda i: (i, 0))
    mtok = pl.BlockSpec((ts, M), lambda i: (i, 0))
    return _pcall(
        body, name=name, grid=(S // ts,),
        in_specs=[mtok] * 4 + [tok, tok, pl.BlockSpec(vecs.shape, lambda i: (0, 0)), pl.BlockSpec(win.shape, lambda i: (0, 0))],
        out_specs=[tok, tok, pl.BlockSpec((ts, 4 * M), lambda i: (i, 0)), pl.BlockSpec((SUBLANES, D), lambda i: (0, 0))],
        out_shape=[jax.ShapeDtypeStruct((S, D), F32), jax.ShapeDtypeStruct((S, D), BF),
                   jax.ShapeDtypeStruct((S, 4 * M), BF), jax.ShapeDtypeStruct((SUBLANES, D), F32)],
        compiler_params=_params(("arbitrary",)),
    )(*dparts, x, dxo, vecs, win)


def _adamw(w, g, m, v):
    m = ADAM_B1 * m + (1.0 - ADAM_B1) * g
    v = ADAM_B2 * v + (1.0 - ADAM_B2) * (g * g)
    m_hat = m / (1.0 - ADAM_B1 ** ADAM_STEP)
    v_hat = v / (1.0 - ADAM_B2 ** ADAM_STEP)
    delta = -ADAM_LR * (m_hat / (jnp.sqrt(v_hat) + ADAM_EPS) + ADAM_WD * w)
    return delta, m, v


def adam_big(w, g, m, v, name):
    R, C = w.shape
    tr = 256 if R % 256 == 0 else R // 2 if (R // 2) % SUBLANES == 0 and R > 512 else R
    tc = C if C <= 1536 else (1152 if C % 1152 == 0 else 1024)
    assert R % tr == 0 and C % tc == 0

    def body(w_ref, g_ref, m_ref, v_ref, d_ref, nm_ref, nv_ref):
        d, nm, nv = _adamw(w_ref[...], g_ref[...], m_ref[...], v_ref[...])
        d_ref[...] = d
        nm_ref[...] = nm
        nv_ref[...] = nv

    blk = pl.BlockSpec((tr, tc), lambda i, j: (i, j))
    return _pcall(
        body, name=name, grid=(R // tr, C // tc), in_specs=[blk] * 4, out_specs=[blk] * 3,
        out_shape=[jax.ShapeDtypeStruct((R, C), F32)] * 3, compiler_params=_params(("parallel", "parallel")),
    )(w, g, m, v)


def adam_cond(c_all, dmod, w, m, v, name):
    B, Kin = c_all.shape
    N = w.shape[1]
    tn = 256
    assert N % tn == 0

    def body(c_ref, d_ref, w_ref, m_ref, v_ref, g_ref, dl_ref, nm_ref, nv_ref):
        cv = c_ref[...]
        ca = cv * _sigmoid(cv)
        g = lax.dot_general(ca, d_ref[...], CONTRACT_FIRST, preferred_element_type=F32, precision=lax.Precision.HIGHEST)
        d, nm, nv = _adamw(w_ref[...], g, m_ref[...], v_ref[...])
        g_ref[...] = g
        dl_ref[...] = d
        nm_ref[...] = nm
        nv_ref[...] = nv

    blk = pl.BlockSpec((Kin, tn), lambda n: (0, n))
    return _pcall(
        body, name=name, grid=(N // tn,),
        in_specs=[pl.BlockSpec((B, Kin), lambda n: (0, 0)), pl.BlockSpec((B, tn), lambda n: (0, n)), blk, blk, blk],
        out_specs=[blk] * 4, out_shape=[jax.ShapeDtypeStruct((Kin, N), F32)] * 4,
        compiler_params=_params(("parallel",)),
    )(c_all, dmod, w, m, v)


def adam_small(ws, gs, ms, vs, name):
    n = len(ws)

    def body(*refs):
        ins, outs = refs[:4 * n], refs[4 * n:]
        for k in range(n):
            d, nm, nv = _adamw(ins[k][...], ins[n + k][...], ins[2 * n + k][...], ins[3 * n + k][...])
            outs[k][...] = d
            outs[n + k][...] = nm
            outs[2 * n + k][...] = nv

    specs = [pl.BlockSpec(w.shape, lambda i: (0, 0)) for w in ws]
    shapes = [jax.ShapeDtypeStruct(w.shape, F32) for w in ws]
    out = _pcall(body, name=name, grid=(1,), in_specs=specs * 4, out_specs=specs * 3, out_shape=shapes * 3,
                 compiler_params=_params(("arbitrary",)))(*ws, *gs, *ms, *vs)
    return out[:n], out[n:2 * n], out[2 * n:]


def _me():
    return lax.axis_index("x"), lax.axis_index("y"), lax.axis_index("c")


def _flip(x, y, p):
    return (x ^ (p >> 1) if (p >> 1) else x), (y ^ (p & 1) if (p & 1) else y)


def _handshake(peers):
    barrier = pltpu.get_barrier_semaphore()
    for peer in peers:
        pl.semaphore_signal(barrier, inc=1, device_id=peer, device_id_type=MESH)
    pl.semaphore_wait(barrier, len(peers))


def _seq_call(body, *, name, n_in, out_shape, sem_shapes, collective_id):
    del n_in
    return pl.kernel(body, out_type=out_shape, mesh=plsc.ScalarSubcoreMesh(axis_name="sq", num_cores=1), name=name,
                     scratch_types=sem_shapes, compiler_params=pltpu.CompilerParams(collective_id=collective_id))


def _hbm_comm_call(body, *, name, n_in, out_shape, sem_shapes, seq_id):
    if seq_id is not None:
        return _seq_call(body, name=name, n_in=n_in, out_shape=out_shape, sem_shapes=sem_shapes, collective_id=seq_id)
    anyspec = pl.BlockSpec(memory_space=pl.ANY)
    return _pcall(body, name=name, in_specs=[anyspec] * n_in, out_specs=[anyspec] * len(out_shape), out_shape=out_shape,
                  scratch_shapes=sem_shapes, compiler_params=_params())


def allgather_devices(v, name, with_sum=False):
    R, L = v.shape

    def body(v_ref, out_ref, *rest):
        if with_sum:
            sum_ref, send_sems, recv_sems = rest
        else:
            send_sems, recv_sems = rest
        x, y, c = _me()
        me = 4 * x + 2 * y + c
        out_ref[me] = v_ref[...]
        copies = []
        for p in range(1, N_DEV):
            px, py = _flip(x, y, p >> 1)
            pc = (1 - c) if (p & 1) else c
            peer = 4 * px + 2 * py + pc
            send = pltpu.make_async_remote_copy(src_ref=v_ref, dst_ref=out_ref.at[me], send_sem=send_sems.at[p - 1],
                                                recv_sem=recv_sems.at[p - 1], device_id=(px, py, pc), device_id_type=MESH)
            send.start()
            recv = pltpu.make_async_remote_copy(src_ref=v_ref, dst_ref=out_ref.at[peer], send_sem=send_sems.at[p - 1],
                                                recv_sem=recv_sems.at[p - 1], device_id=(px, py, pc), device_id_type=MESH)
            copies.append((send, recv))
        for send, recv in copies:
            recv.wait_recv()
        for send, recv in copies:
            send.wait_send()
        if with_sum:
            s = out_ref[0]
            for k in range(1, N_DEV):
                s = s + out_ref[k]
            sum_ref[...] = s

    vm = pl.BlockSpec(memory_space=pltpu.VMEM)
    out_shape = [jax.ShapeDtypeStruct((N_DEV, R, L), F32)]
    if with_sum:
        out_shape.append(jax.ShapeDtypeStruct((R, L), F32))
    return _pcall(
        body, name=name, in_specs=[vm], out_specs=[vm] * len(out_shape), out_shape=out_shape,
        scratch_shapes=[pltpu.SemaphoreType.DMA((N_DEV - 1,)), pltpu.SemaphoreType.DMA((N_DEV - 1,))],
        compiler_params=_params(),
    )(v)


def allgather_devices_hbm(v, name, seq_id):
    R, L = v.shape

    def body(v_ref, out_ref, send_sems, recv_sems, local_sem):
        x, y, c = _me()
        me = 4 * x + 2 * y + c
        peers = []
        for p in range(1, N_DEV):
            px, py = _flip(x, y, p >> 1)
            peers.append((px, py, (1 - c) if (p & 1) else c))
        _handshake(peers)
        lc = pltpu.make_async_copy(v_ref, out_ref.at[me], local_sem)
        lc.start()
        copies = []
        for p, (px, py, pc) in enumerate(peers):
            send = pltpu.make_async_remote_copy(src_ref=v_ref, dst_ref=out_ref.at[me], send_sem=send_sems.at[p],
                                                recv_sem=recv_sems.at[p], device_id=(px, py, pc), device_id_type=MESH)
            send.start()
            recv = pltpu.make_async_remote_copy(src_ref=v_ref, dst_ref=out_ref.at[4 * px + 2 * py + pc], send_sem=send_sems.at[p],
                                                recv_sem=recv_sems.at[p], device_id=(px, py, pc), device_id_type=MESH)
            copies.append((send, recv))
        for send, recv in copies:
            recv.wait_recv()
        for send, recv in copies:
            send.wait_send()
        lc.wait()

    return _seq_call(body, name=name, n_in=1, out_shape=[jax.ShapeDtypeStruct((N_DEV, R, L), F32)],
                     sem_shapes=[pltpu.SemaphoreType.DMA((N_DEV - 1,)), pltpu.SemaphoreType.DMA((N_DEV - 1,)),
                                 pltpu.SemaphoreType.DMA], collective_id=seq_id)(v)[0]


def sum_slots(g, name):
    n, R, L = g.shape
    tr = 216 if R % 216 == 0 else R
    assert R % tr == 0 and tr % SUBLANES == 0

    def body(g_ref, o_ref):
        s = g_ref[0]
        for k in range(1, n):
            s = s + g_ref[k]
        o_ref[...] = s

    return _pcall(body, name=name, grid=(R // tr,), in_specs=[pl.BlockSpec((n, tr, L), lambda i: (0, i, 0))],
                  out_specs=pl.BlockSpec((tr, L), lambda i: (i, 0)), out_shape=jax.ShapeDtypeStruct((R, L), F32),
                  compiler_params=_params(("parallel",)))(g)


def allgather_chips(v, name):
    R, L = v.shape

    def body(v_ref, out_ref, send_sems, recv_sems):
        x, y, c = _me()
        chip = 2 * x + y
        out_ref[chip] = v_ref[...]
        copies = []
        for p in range(1, N_CHIPS):
            px, py = _flip(x, y, p)
            send = pltpu.make_async_remote_copy(src_ref=v_ref, dst_ref=out_ref.at[chip], send_sem=send_sems.at[p - 1],
                                                recv_sem=recv_sems.at[p - 1], device_id=(px, py, c), device_id_type=MESH)
            send.start()
            recv = pltpu.make_async_remote_copy(src_ref=v_ref, dst_ref=out_ref.at[2 * px + py], send_sem=send_sems.at[p - 1],
                                                recv_sem=recv_sems.at[p - 1], device_id=(px, py, c), device_id_type=MESH)
            copies.append((send, recv))
        for send, recv in copies:
            recv.wait_recv()
        for send, recv in copies:
            send.wait_send()

    vm = pl.BlockSpec(memory_space=pltpu.VMEM)
    return _pcall(
        body, name=name, in_specs=[vm], out_specs=vm, out_shape=jax.ShapeDtypeStruct((N_CHIPS, R, L), F32),
        scratch_shapes=[pltpu.SemaphoreType.DMA((N_CHIPS - 1,)), pltpu.SemaphoreType.DMA((N_CHIPS - 1,))],
        compiler_params=_params(),
    )(v)


def _shard_window(ref, kind, shard_shape, chip, half):
    r, c = shard_shape
    hr = r // 2
    if kind == "col":
        return ref.at[pl.ds(pl.multiple_of(half * hr, hr), hr), pl.ds(pl.multiple_of(chip * c, c), c)]
    return ref.at[pl.ds(pl.multiple_of(chip * r + half * hr, hr), hr), :]


def allgather_weights(shards, kinds, name, seq_id=None):
    n = len(shards)
    fulls = []
    for s, kind in zip(shards, kinds):
        r, c = s.shape
        fulls.append(jax.ShapeDtypeStruct((r, N_CHIPS * c) if kind == "col" else (N_CHIPS * r, c), s.dtype))

    def body(*refs):
        srcs, outs = refs[:n], refs[n:2 * n]
        send_sems, recv_sems, fsend_sems, frecv_sems = refs[2 * n:]
        x, y, c = _me()
        chip = 2 * x + y
        sib = (x, y, 1 - c)
        if seq_id is not None:
            _handshake([(*_flip(x, y, p), c) for p in range(1, N_CHIPS)] + [sib])
        sends, fwds = [], []
        for i in range(n):
            shp = srcs[i].shape
            hr = shp[0] // 2
            my_half = srcs[i].at[pl.ds(pl.multiple_of(c * hr, hr), hr), :]
            for p in range(1, N_CHIPS):
                px, py = _flip(x, y, p)
                k = i * (N_CHIPS - 1) + p - 1
                cp = pltpu.make_async_remote_copy(src_ref=my_half, dst_ref=_shard_window(outs[i], kinds[i], shp, chip, c),
                                                  send_sem=send_sems.at[k], recv_sem=recv_sems.at[k],
                                                  device_id=(px, py, c), device_id_type=MESH)
                cp.start()
                sends.append(cp)
        for i in range(n):
            shp = srcs[i].shape
            for p in range(1, N_CHIPS):
                px, py = _flip(x, y, p)
                k = i * (N_CHIPS - 1) + p - 1
                landed = _shard_window(outs[i], kinds[i], shp, 2 * px + py, c)
                pltpu.make_async_remote_copy(src_ref=landed, dst_ref=landed, send_sem=send_sems.at[k], recv_sem=recv_sems.at[k],
                                             device_id=(px, py, c), device_id_type=MESH).wait_recv()
                fw = pltpu.make_async_remote_copy(src_ref=landed, dst_ref=landed, send_sem=fsend_sems.at[k],
                                                  recv_sem=frecv_sems.at[k], device_id=sib, device_id_type=MESH)
                fw.start()
                fwds.append(fw)
        for i in range(n):
            shp = srcs[i].shape
            for p in range(1, N_CHIPS):
                px, py = _flip(x, y, p)
                k = i * (N_CHIPS - 1) + p - 1
                other = _shard_window(outs[i], kinds[i], shp, 2 * px + py, 1 - c)
                pltpu.make_async_remote_copy(src_ref=other, dst_ref=other, send_sem=fsend_sems.at[k], recv_sem=frecv_sems.at[k],
                                             device_id=sib, device_id_type=MESH).wait_recv()
        for cp in sends + fwds:
            cp.wait_send()

    nk = n * (N_CHIPS - 1)
    gathered = _hbm_comm_call(
        body, name=name, n_in=n, out_shape=fulls, seq_id=seq_id,
        sem_shapes=[pltpu.SemaphoreType.DMA((nk,)), pltpu.SemaphoreType.DMA((nk,)), pltpu.SemaphoreType.DMA((nk,)),
                    pltpu.SemaphoreType.DMA((nk,))],
    )(*shards)
    return gathered


def place_local_shards(fulls, shards, kinds, name):
    n = len(shards)
    chip = jnp.reshape(2 * lax.axis_index("x") + lax.axis_index("y"), (1,)).astype(I32)

    def body(ci_ref, *refs):
        for i in range(n):
            refs[2 * n + i][...] = refs[i][...]

    in_specs = [pl.BlockSpec(s.shape, lambda i, ci: (0, 0)) for s in shards] + [pl.BlockSpec(memory_space=pl.ANY)] * n
    out_specs = [pl.BlockSpec(s.shape, (lambda i, ci: (0, ci[0])) if k == "col" else (lambda i, ci: (ci[0], 0)))
                 for s, k in zip(shards, kinds)]
    gs = pltpu.PrefetchScalarGridSpec(num_scalar_prefetch=1, grid=(1,), in_specs=in_specs, out_specs=out_specs)
    return _pcall(body, name=name, grid_spec=gs, out_shape=[jax.ShapeDtypeStruct(f.shape, f.dtype) for f in fulls],
                  input_output_aliases={1 + n + i: i for i in range(n)}, compiler_params=_params(("arbitrary",)))(chip, *shards, *fulls)


def _as_halves(g, kind, shard_shape):
    r, c = shard_shape
    if kind == "col":
        return g.reshape(2, r // 2, N_CHIPS * c)
    return g.reshape(N_CHIPS, 2, r // 2, c)


def exchange_sibling_halves(grads, kinds, shard_shapes, name, seq_id=None):
    n = len(grads)
    views = [_as_halves(g, k, s) for g, k, s in zip(grads, kinds, shard_shapes)]
    outs = []
    for k, (r, c) in zip(kinds, shard_shapes):
        outs.append(jax.ShapeDtypeStruct((r // 2, N_CHIPS * c) if k == "col" else (N_CHIPS, r // 2, c), F32))

    def body(*refs):
        srcs, dsts = refs[:n], refs[n:2 * n]
        send_sems, recv_sems = refs[2 * n:]
        x, y, c = _me()
        if seq_id is not None:
            _handshake([(x, y, 1 - c)])
        cps = []
        for i in range(n):
            src = srcs[i].at[1 - c] if kinds[i] == "col" else srcs[i].at[:, 1 - c]
            cp = pltpu.make_async_remote_copy(src_ref=src, dst_ref=dsts[i], send_sem=send_sems.at[i], recv_sem=recv_sems.at[i],
                                              device_id=(x, y, 1 - c), device_id_type=MESH)
            cp.start()
            cps.append(cp)
        for cp in cps:
            cp.wait_recv()
        for cp in cps:
            cp.wait_send()

    return _hbm_comm_call(body, name=name, n_in=n, out_shape=outs, seq_id=seq_id,
                          sem_shapes=[pltpu.SemaphoreType.DMA((n,)), pltpu.SemaphoreType.DMA((n,))])(*views)


def add_sibling_half(g, recv, kind, shard_shape, cidx, name):
    r, c = shard_shape
    hr = r // 2
    gv = _as_halves(g, kind, shard_shape)
    tr = hr if hr <= 512 else (256 if hr % 256 == 0 else hr // 2)
    assert hr % tr == 0

    def body(ci_ref, g_ref, r_ref, h_ref, hb_ref):
        s = g_ref[...] + r_ref[...]
        h_ref[...] = s
        hb_ref[...] = s.astype(BF)

    if kind == "col":
        grid = (hr // tr, N_CHIPS)
        g_spec = pl.BlockSpec((None, tr, c), lambda i, k, ci: (ci[0], i, k))
        o_spec = pl.BlockSpec((tr, c), lambda i, k, ci: (i, k))
    else:
        grid = (hr // tr, N_CHIPS)
        g_spec = pl.BlockSpec((None, None, tr, c), lambda i, k, ci: (k, ci[0], i, 0))
        o_spec = pl.BlockSpec((None, tr, c), lambda i, k, ci: (k, i, 0))
    gs = pltpu.PrefetchScalarGridSpec(num_scalar_prefetch=1, grid=grid, in_specs=[g_spec, o_spec], out_specs=[o_spec, o_spec])
    return _pcall(
        body, name=name, grid_spec=gs,
        out_shape=[jax.ShapeDtypeStruct(recv.shape, F32), jax.ShapeDtypeStruct(recv.shape, BF)],
        compiler_params=_params(("parallel", "parallel")),
    )(cidx, gv, recv)


def exchange_chip_pieces(hbs, kinds, shard_shapes, name, seq_id=None):
    n = len(hbs)
    outs = [jax.ShapeDtypeStruct((N_CHIPS - 1, r // 2, c), BF) for (r, c) in shard_shapes]

    def body(*refs):
        srcs, dsts = refs[:n], refs[n:2 * n]
        send_sems, recv_sems = refs[2 * n:]
        x, y, c = _me()
        if seq_id is not None:
            _handshake([(*_flip(x, y, p), c) for p in range(1, N_CHIPS)])
        cps = []
        for i in range(n):
            cc = shard_shapes[i][1]
            for p in range(1, N_CHIPS):
                px, py = _flip(x, y, p)
                pchip = 2 * px + py
                src = (srcs[i].at[:, pl.ds(pl.multiple_of(pchip * cc, cc), cc)] if kinds[i] == "col" else srcs[i].at[pchip])
                k = i * (N_CHIPS - 1) + p - 1
                cp = pltpu.make_async_remote_copy(src_ref=src, dst_ref=dsts[i].at[p - 1], send_sem=send_sems.at[k],
                                                  recv_sem=recv_sems.at[k], device_id=(px, py, c), device_id_type=MESH)
                cp.start()
                cps.append(cp)
        for cp in cps:
            cp.wait_recv()
        for cp in cps:
            cp.wait_send()

    nk = n * (N_CHIPS - 1)
    return _hbm_comm_call(body, name=name, n_in=n, out_shape=outs, seq_id=seq_id,
                          sem_shapes=[pltpu.SemaphoreType.DMA((nk,)), pltpu.SemaphoreType.DMA((nk,))])(*hbs)


def sum_chip_pieces(h, pieces, kind, shard_shape, chip_core, name):
    r, c = shard_shape
    hr = r // 2
    tr = hr if hr <= 512 else (256 if hr % 256 == 0 else hr // 2)
    assert hr % tr == 0
    nrb = hr // tr

    def body(ci_ref, h_ref, p_ref, q_ref):
        q_ref[...] = ((h_ref[...] + p_ref[0].astype(F32)) + p_ref[1].astype(F32)) + p_ref[2].astype(F32)

    if kind == "col":
        h_spec = pl.BlockSpec((tr, c), lambda i, ci: (i, ci[0]))
    else:
        h_spec = pl.BlockSpec((None, tr, c), lambda i, ci: (ci[0], i, 0))
    gs = pltpu.PrefetchScalarGridSpec(
        num_scalar_prefetch=1, grid=(nrb,),
        in_specs=[h_spec, pl.BlockSpec((N_CHIPS - 1, tr, c), lambda i, ci: (0, i, 0))],
        out_specs=pl.BlockSpec((tr, c), lambda i, ci: (ci[1] * nrb + i, 0)))
    return _pcall(body, name=name, grid_spec=gs, out_shape=jax.ShapeDtypeStruct((r, c), F32),
                  compiler_params=_params(("parallel",)))(chip_core, h, pieces)


def exchange_reduced_halves(qs, name):
    n = len(qs)

    def body(*refs):
        bufs = refs[n:2 * n]
        send_sems, recv_sems = refs[2 * n:]
        x, y, c = _me()
        cps = []
        for i in range(n):
            hr = bufs[i].shape[0] // 2
            mine = bufs[i].at[pl.ds(pl.multiple_of(c * hr, hr), hr), :]
            other = bufs[i].at[pl.ds(pl.multiple_of((1 - c) * hr, hr), hr), :]
            cp = pltpu.make_async_remote_copy(src_ref=mine, dst_ref=mine, send_sem=send_sems.at[i], recv_sem=recv_sems.at[i],
                                              device_id=(x, y, 1 - c), device_id_type=MESH)
            cp.start()
            cps.append((cp, pltpu.make_async_remote_copy(src_ref=other, dst_ref=other, send_sem=send_sems.at[i],
                                                         recv_sem=recv_sems.at[i], device_id=(x, y, 1 - c), device_id_type=MESH)))
        for cp, rv in cps:
            rv.wait_recv()
        for cp, rv in cps:
            cp.wait_send()

    anyspec = pl.BlockSpec(memory_space=pl.ANY)
    return _pcall(
        body, name=name, in_specs=[anyspec] * n, out_specs=[anyspec] * n,
        out_shape=[jax.ShapeDtypeStruct(q.shape, F32) for q in qs], input_output_aliases={i: i for i in range(n)},
        scratch_shapes=[pltpu.SemaphoreType.DMA((n,)), pltpu.SemaphoreType.DMA((n,))],
        compiler_params=_params(),
    )(*qs)


def _rows128(a):
    return a.reshape(-1, LANES)


def _after(xs, *deps):
    flat = []
    for d in deps:
        flat.extend(d if isinstance(d, (list, tuple)) else [d])
    return list(lax.optimization_barrier((tuple(xs), tuple(flat)))[0])


def _block_diag(w):
    H, d, _ = w.shape
    eye = jnp.eye(H, dtype=w.dtype)
    return jnp.einsum("hde,hg->hdge", w, eye).reshape(H * d, H * d)


def _diag_blocks(g4, H, d):
    nb = g4.shape[0]
    per = LANES // d
    g = g4.reshape(nb, per, d, per, d)
    return jnp.stack([g[:, j, :, j, :] for j in range(per)], axis=1).reshape(H, d, d)


def kernel(x, c, w_mod, b_mod, g_ffn1, w_ffn1_in, w_ffn1_out, g_mix, w_in, conv_w, conv_b, ln_g, ln_b, rnn_conv_w, rnn_conv_b, w_a, b_a, w_i, b_i, lru_lambda, w_out, g_ffn2, w_ffn2_in, w_ffn2_out, w_fmod, b_fmod, g_final, loss_target, m_w_mod, m_b_mod, m_g_ffn1, m_w_ffn1_in, m_w_ffn1_out, m_g_mix, m_w_in, m_conv_w, m_conv_b, m_ln_g, m_ln_b, m_rnn_conv_w, m_rnn_conv_b, m_w_a, m_b_a, m_w_i, m_b_i, m_lru_lambda, m_w_out, m_g_ffn2, m_w_ffn2_in, m_w_ffn2_out, m_w_fmod, m_b_fmod, m_g_final, v_w_mod, v_b_mod, v_g_ffn1, v_w_ffn1_in, v_w_ffn1_out, v_g_mix, v_w_in, v_conv_w, v_conv_b, v_ln_g, v_ln_b, v_rnn_conv_w, v_rnn_conv_b, v_w_a, v_b_a, v_w_i, v_b_i, v_lru_lambda, v_w_out, v_g_ffn2, v_w_ffn2_in, v_w_ffn2_out, v_w_fmod, v_b_fmod, v_g_final):
    S, D = x.shape[1], x.shape[2]
    M = conv_b.shape[1]
    H, HD = w_a.shape[1], w_a.shape[2]
    nb = M // LANES
    ix, iy, ic = lax.axis_index("x"), lax.axis_index("y"), lax.axis_index("c")
    chip = 2 * ix + iy
    dev = 2 * chip + ic
    cidx = jnp.reshape(ic, (1,)).astype(I32)
    chip_core = jnp.stack([chip, ic]).astype(I32)
    xs = x[0]
    tgt = loss_target[0]

    kinds = ["col", "row"]
    w_f1, w_mx, w_f2 = [w_ffn1_in[0], w_ffn1_out[0]], [w_in[0], w_out[0]], [w_ffn2_in[0], w_ffn2_out[0]]
    as_bf = lambda ws: [w.astype(BF) for w in ws]
    shapes_of = lambda ws: [w.shape for w in ws]
    b_f1, b_mx, b_f2 = as_bf(w_f1), as_bf(w_mx), as_bf(w_f2)
    got_f1 = allgather_weights(b_f1, kinds, "gather_ffn1", seq_id=9)
    got_mx = allgather_weights(b_mx, kinds, "gather_mix", seq_id=1)
    got_f2 = allgather_weights(b_f2, kinds, "gather_ffn2", seq_id=2)

    c_all =allgather_devices(_rows128(c), "gather_c")[0].reshape(N_DEV, D)
    mod_cols = cond_matmul(c_all, w_mod[0], "mod_proj")
    fmod_cols = cond_matmul(c_all, w_fmod, "fmod_proj")
    convw_pad = jnp.pad(conv_w[0], ((0, 32 - CONV_WIDTH), (0, 0)))
    rnnw_pad = jnp.pad(rnn_conv_w[0], ((0, SUBLANES - RNN_CONV_WIDTH), (0, 0)))
    n_mod, n_fmod = mod_cols.shape[1], fmod_cols.shape[1]
    small = jnp.concatenate([_rows128(mod_cols), _rows128(fmod_cols), convw_pad, rnnw_pad], axis=0)
    small4 = allgather_chips(small, "gather_cond")
    r0 = N_DEV * n_mod // LANES
    r1 = r0 + N_DEV * n_fmod // LANES
    mod_all = small4[:, :r0].reshape(N_CHIPS, N_DEV, n_mod)
    fmod_all = small4[:, r0:r1].reshape(N_CHIPS, N_DEV, n_fmod)
    convw4 = small4[:, r1:r1 + 32]
    rnnw4 = small4[:, r1 + 32:r1 + 32 + SUBLANES]
    mod_row = lax.dynamic_index_in_dim(mod_all, dev, axis=1, keepdims=False).reshape(1, N_CHIPS * n_mod) + b_mod
    fmod_row = lax.dynamic_index_in_dim(fmod_all, dev, axis=1, keepdims=False).reshape(1, N_CHIPS * n_fmod) + b_fmod[None, :]
    vecs = jnp.concatenate([mod_row.reshape(9, D), fmod_row.reshape(2, D), g_ffn1, g_mix, g_ffn2, g_final[None, :],
                            jnp.zeros((1, D), F32)], axis=0)
    lnv = jnp.concatenate([ln_g, ln_b, jnp.zeros((SUBLANES - 2, M), F32)], axis=0)
    bda = _block_diag(w_a[0]).astype(BF)
    bdi = _block_diag(w_i[0]).astype(BF)

    def reduce_add(gs, recv, ws, tag, kinds_=kinds):
        pairs = [add_sibling_half(g, r_, k, w.shape, cidx, f"add_sibling_{tag}{j}")
                 for j, (g, r_, k, w) in enumerate(zip(gs, recv, kinds_, ws))]
        return [p[0] for p in pairs], [p[1] for p in pairs]

    def reduce_sum(hs_, recv, ws, tag, kinds_=kinds):
        return [sum_chip_pieces(h_, p_, k, w.shape, chip_core, f"sum_chips_{tag}{j}")
                for j, (h_, p_, k, w) in enumerate(zip(hs_, recv, kinds_, ws))]

    rows1 = (R_SH1, R_SC1, R_GT1, R_G1)
    rows3 = (R_SH3, R_SC3, R_GT3, R_G3)
    wi1, wo1 = place_local_shards(got_f1, b_f1, kinds, "place_ffn1")
    x1, g1s, u1s, y1 = ffn_fwd(xs, vecs, wi1, wo1, rows1, "ffn1_fwd")
    win, wout = place_local_shards(_after(got_mx, x1), b_mx, kinds, "place_mix")
    proj = norm_matmul(x1, vecs, win, (R_SH2, R_SC2, R_G2), "mix_in_proj")
    cq = conv_fwd(proj, convw4, conv_b, "conv_fwd")
    xr, ra, ii, hh = rnn_fwd(proj, rnnw4, rnn_conv_b, bda, bdi, b_a, b_i, lru_lambda, "rnn_fwd")
    x2, ym, ycat = mix_out(cq, proj, hh, x1, vecs, lnv, wout, "mix_out")
    wi2, wo2 = place_local_shards(_after(got_f2, x2), b_f2, kinds, "place_ffn2")
    x3, g2s, u2s, y2 = ffn_fwd(x2, vecs, wi2, wo2, rows3, "ffn2_fwd")
    dx3, vgf = final_fwd_bwd(x3, tgt, vecs, "final_loss")

    Fd = wo1.shape[0]
    tk = min(1024, S)
    dx2, act2, dg2, du2, h3b, dy2b, vg3 = ffn_bwd(dx3, x2, vecs, g2s, u2s, y2, wi2, wo2, rows3, "ffn2_bwd")
    gwo2 = matmul(act2, dy2b, "tn", tm=Fd // 2, tn=D, tk=tk, name="ffn2_dwo")
    gwi2 = matmul(h3b, dg2, "tn", tm=D, tn=Fd // 2, tk=tk, name="ffn2_dwg", out_cols=2 * Fd)
    gwi2 = matmul(h3b, du2, "tn", tm=D, tn=Fd // 2, tk=tk, name="ffn2_dwu", out_cols=2 * Fd, col_off=Fd, prev=gwi2)
    recv1_f2 = exchange_sibling_halves([gwi2, gwo2], kinds, shapes_of(w_f2), "reduce1_ffn2", seq_id=3)
    dcq, dhout, duy, dymb, vgd, vgm = mix_out_bwd(dx2, ym, vecs, wout, cq, lnv, proj, hh, "mix_out_bwd")
    gwout = matmul(ycat, dymb, "tn", tm=2 * M, tn=D, tk=tk, name="mix_dwout")
    recv1_f2 = _after(recv1_f2, gwout)
    h_f2, hb_f2 = reduce_add([gwi2, gwo2], recv1_f2, w_f2, "ffn2_")
    recv2_f2 = exchange_chip_pieces(hb_f2, kinds, shapes_of(w_f2), "reduce2_ffn2", seq_id=4)
    duv, dug, dconvw4, dconvb = conv_bwd(_after([dcq], hb_f2)[0], proj, convw4, "conv_bwd")
    dux, dwa4, dwi4, drnnw4, rvec = rnn_bwd(dhout, hh, xr, ra, ii, proj, rnnw4, bda, bdi, lru_lambda, "rnn_bwd")
    dx1, h2b, dpb, vg2 = mix_in_bwd((duv, dug, dux, duy), x1, dx2, vecs, win, "mix_in_bwd")
    gwin = matmul(h2b, dpb, "tn", tm=D, tn=1024, tk=tk, name="mix_dwin")
    recv1_mx = exchange_sibling_halves([gwin, gwout], kinds, shapes_of(w_mx), "reduce1_mix", seq_id=5)
    g_f2 = exchange_reduced_halves(reduce_sum(_after(h_f2, gwin), recv2_f2, w_f2, "ffn2_"), "reduce3_ffn2")
    adam_f2 = [adam_big(w, g, m, v, "adam_" + nm) for w, g, m, v, nm in
               zip(w_f2, g_f2, [m_w_ffn2_in[0], m_w_ffn2_out[0]], [v_w_ffn2_in[0], v_w_ffn2_out[0]], ["ffn2_in", "ffn2_out"])]
    h_mx, hb_mx = reduce_add([gwin, gwout], _after(recv1_mx, adam_f2[0][0], adam_f2[1][0]), w_mx, "mix_")
    recv2_mx = exchange_chip_pieces(hb_mx, kinds, shapes_of(w_mx), "reduce2_mix", seq_id=6)
    dx0, act1, dg1, du1, h1b, dy1b, vg1 = ffn_bwd(_after([dx1], hb_mx)[0], xs, vecs, g1s, u1s, y1, wi1, wo1, rows1, "ffn1_bwd")
    dmod_row = jnp.concatenate([vg1[1:3], vg1[0:1], vg2[0:2], vgd[0:1], vg3[1:3], vg3[0:1]], axis=0)
    gains = jnp.concatenate([vg1[3:4], vg2[2:3], vg3[3:4], vgf[2:4]], axis=0)
    mvecs = jnp.concatenate([dconvb, vgm[0:2], rvec[0:4], jnp.zeros((1, M), F32)], axis=0)
    parts = [_rows128(dmod_row), _rows128(vgf[0:2]), _rows128(gains), _rows128(mvecs),
             _rows128(dconvw4), _rows128(drnnw4), _rows128(_diag_blocks(dwa4, H, HD)), _rows128(_diag_blocks(dwi4, H, HD))]
    sizes = [p.shape[0] for p in parts]
    packed = jnp.concatenate(parts, axis=0)
    gathered = allgather_devices_hbm(packed, "gather_small", seq_id=10)

    gwo1 = matmul(_after([act1], recv2_mx, packed)[0], dy1b, "tn", tm=Fd // 2, tn=D, tk=tk, name="ffn1_dwo")
    w_f1o, w_f1i = w_f1[1:], w_f1[:1]
    recv1_f1o = exchange_sibling_halves([gwo1], ["row"], shapes_of(w_f1o), "reduce1_ffn1_out", seq_id=7)
    gwi1 = matmul(h1b, dg1, "tn", tm=D, tn=Fd // 2, tk=tk, name="ffn1_dwg", out_cols=2 * Fd)
    h_f1o, hb_f1o = reduce_add([gwo1], _after(recv1_f1o, gwi1), w_f1o, "ffn1_out", ["row"])
    recv2_f1o = exchange_chip_pieces(hb_f1o, ["row"], shapes_of(w_f1o), "reduce2_ffn1_out", seq_id=11)
    gwi1 = matmul(h1b, _after([du1], hb_f1o, gathered)[0], "tn", tm=D, tn=Fd // 2, tk=tk, name="ffn1_dwu", out_cols=2 * Fd,
                  col_off=Fd, prev=gwi1)
    recv1_f1i = exchange_sibling_halves([gwi1], ["col"], shapes_of(w_f1i), "reduce1_ffn1_in", seq_id=12)
    g_mx = exchange_reduced_halves(reduce_sum(h_mx, recv2_mx, w_mx, "mix_"), "reduce3_mix")
    summed = sum_slots(gathered, "sum_small")
    offs = [0]
    for s in sizes:
        offs.append(offs[-1] + s)
    seg = lambda k: summed[offs[k]:offs[k + 1]]
    g_b_mod = seg(0).reshape(1, 9 * D)
    g_b_fmod = seg(1).reshape(1, 2 * D)
    gsum = seg(2).reshape(5, D)
    loss = (0.5 / D) * jnp.sum(gsum[4])
    msum = seg(3).reshape(SUBLANES, M)
    g_conv_w = lax.dynamic_index_in_dim(seg(4).reshape(nb, 32, LANES), chip, axis=0, keepdims=False)[:CONV_WIDTH]
    g_rnn_w = lax.dynamic_index_in_dim(seg(5).reshape(nb, SUBLANES, LANES), chip, axis=0, keepdims=False)[:RNN_CONV_WIDTH]
    g_w_a = seg(6).reshape(H, HD, HD)
    g_w_i = seg(7).reshape(H, HD, HD)
    dmod_all = gathered[:, offs[0]:offs[1]].reshape(N_DEV, 9 * D)
    dfmod_all = gathered[:, offs[1]:offs[2]].reshape(N_DEV, 2 * D)
    dmod_cols = lax.dynamic_slice_in_dim(dmod_all, chip * n_mod, n_mod, axis=1)
    dfmod_cols = lax.dynamic_slice_in_dim(dfmod_all, chip * n_fmod, n_fmod, axis=1)

    g_w_mod, d_w_mod, nm_w_mod, nv_w_mod = adam_cond(c_all, dmod_cols, w_mod[0], m_w_mod[0], v_w_mod[0], "adam_w_mod")
    g_w_fmod, d_w_fmod, nm_w_fmod, nv_w_fmod = adam_cond(c_all, dfmod_cols, w_fmod, m_w_fmod, v_w_fmod, "adam_w_fmod")

    h_f1i, hb_f1i = reduce_add([gwi1], _after(recv1_f1i, recv2_f1o, g_w_mod, g_w_fmod), w_f1i, "ffn1_in", ["col"])
    recv2_f1i = exchange_chip_pieces(hb_f1i, ["col"], shapes_of(w_f1i), "reduce2_ffn1_in", seq_id=8)
    g_f1o = exchange_reduced_halves(reduce_sum(h_f1o, recv2_f1o, w_f1o, "ffn1_out", ["row"]), "reduce3_ffn1_out")
    g_f1i = exchange_reduced_halves(reduce_sum(h_f1i, recv2_f1i, w_f1i, "ffn1_in", ["col"]), "reduce3_ffn1_in")
    g_f1 = list(g_f1i) + list(g_f1o)

    big_w = w_f1 + w_mx + w_f2
    g_big = g_f1 + list(g_mx) + list(g_f2)
    names = ["ffn1_in", "ffn1_out", "w_in", "w_out", "ffn2_in", "ffn2_out"]
    big_m = [m_w_ffn1_in[0], m_w_ffn1_out[0], m_w_in[0], m_w_out[0]]
    big_v = [v_w_ffn1_in[0], v_w_ffn1_out[0], v_w_in[0], v_w_out[0]]
    big_out = [adam_big(w, g, m, v, "adam_" + nm) for w, g, m, v, nm in zip(big_w, g_big, big_m, big_v, names)] + adam_f2

    flat2 = lambda a: a.reshape(-1, a.shape[-1])
    small_names = ["b_mod", "g_ffn1", "g_mix", "conv_w", "conv_b", "ln_g", "ln_b", "rnn_conv_w", "rnn_conv_b", "w_a", "b_a",
                   "w_i", "b_i", "lru_lambda", "g_ffn2", "b_fmod", "g_final"]
    small_w = [b_mod, g_ffn1, g_mix, conv_w, conv_b, ln_g, ln_b, rnn_conv_w, rnn_conv_b, w_a, b_a, w_i, b_i, lru_lambda,
               g_ffn2, b_fmod, g_final]
    small_m = [m_b_mod, m_g_ffn1, m_g_mix, m_conv_w, m_conv_b, m_ln_g, m_ln_b, m_rnn_conv_w, m_rnn_conv_b, m_w_a, m_b_a,
               m_w_i, m_b_i, m_lru_lambda, m_g_ffn2, m_b_fmod, m_g_final]
    small_v = [v_b_mod, v_g_ffn1, v_g_mix, v_conv_w, v_conv_b, v_ln_g, v_ln_b, v_rnn_conv_w, v_rnn_conv_b, v_w_a, v_b_a,
               v_w_i, v_b_i, v_lru_lambda, v_g_ffn2, v_b_fmod, v_g_final]
    small_g = [g_b_mod, gsum[0:1], gsum[1:2], g_conv_w, msum[0:1], msum[1:2], msum[2:3], g_rnn_w, msum[3:4], g_w_a, msum[4:5],
               g_w_i, msum[5:6], msum[6:7], gsum[2:3], g_b_fmod, gsum[3:4]]
    small_g = [g.reshape(w.shape) for g, w in zip(small_g, small_w)]
    two_d = lambda a: a.reshape(1, -1) if a.ndim == 1 else flat2(a)
    sd, sm, sv = adam_small([two_d(a) for a in small_w], [two_d(a) for a in small_g], [two_d(a) for a in small_m],
                            [two_d(a) for a in small_v], "adam_small")
    small = {}
    for k, nm in enumerate(small_names):
        shp = small_w[k].shape
        small[nm] = (small_g[k], sd[k].reshape(shp), sm[k].reshape(shp), sv[k].reshape(shp))

    big = {"w_mod": tuple(a[None] for a in (g_w_mod, d_w_mod, nm_w_mod, nv_w_mod)),
           "w_fmod": (g_w_fmod, d_w_fmod, nm_w_fmod, nv_w_fmod)}
    for nm, full, g, (d, nmm, nvv) in zip(["w_ffn1_in", "w_ffn1_out", "w_in", "w_out", "w_ffn2_in", "w_ffn2_out"],
                                         big_w, g_big, big_out):
        big[nm] = tuple(a[None] for a in (g, d, nmm, nvv))
    order = ["w_mod", "b_mod", "g_ffn1", "w_ffn1_in", "w_ffn1_out", "g_mix", "w_in", "conv_w", "conv_b", "ln_g", "ln_b",
             "rnn_conv_w", "rnn_conv_b", "w_a", "b_a", "w_i", "b_i", "lru_lambda", "w_out", "g_ffn2", "w_ffn2_in",
             "w_ffn2_out", "w_fmod", "b_fmod", "g_final"]
    table = {**small, **big}
    outs = [loss, dx0[None]]
    for kind_ in range(4):
        outs.extend(table[nm][kind_] for nm in order)
    return tuple(outs)
```

```python
import functools

import jax
import jax.numpy as jnp
from jax import lax
from jax.experimental import pallas as pl
from jax.experimental.pallas import tpu as pltpu
from jax.experimental.pallas import tpu_sc as plsc

F32 = jnp.float32
BF = jnp.bfloat16
I32 = jnp.int32
MESH = pl.DeviceIdType.MESH

EPS = 1e-6
RG_C = 8.0
MACARON_W = 0.5
CONV_WIDTH = 31
RNN_CONV_WIDTH = 4
ADAM_LR = 0.001
ADAM_B1 = 0.9
ADAM_B2 = 0.999
ADAM_EPS = 1e-08
ADAM_WD = 0.01
ADAM_STEP = 10

LANES = 128
SUBLANES = 8
VMEM_LIMIT = 62 * 1024 * 1024
N_CHIPS = 4
N_DEV = 8

R_SH1, R_SC1, R_GT1, R_SH2, R_SC2, R_GT2, R_SH3, R_SC3, R_GT3, R_FSH, R_FSC, R_G1, R_G2, R_G3, R_GF = range(15)

CONTRACT_LAST = (((1,), (1,)), ((), ()))
CONTRACT_FIRST = (((0,), (0,)), ((), ()))


def _pcall(body, **kw):
    return pl.pallas_call(body, **kw)


def _params(sem=None, vmem=VMEM_LIMIT):
    if sem is None:
        return pltpu.CompilerParams(vmem_limit_bytes=vmem)
    return pltpu.CompilerParams(dimension_semantics=sem, vmem_limit_bytes=vmem)


def _row(ref, r):
    return ref[r:r + 1, :]


def _sigmoid(x):
    return 1.0 / (1.0 + jnp.exp(-x))


def _colsum(x):
    return jnp.sum(x, axis=0, keepdims=True)


def _rowmean(x):
    return jnp.mean(x, axis=-1, keepdims=True)


def matmul(a, b, mode, *, tm, tn, tk, name, out_dtype=F32, out_cols=None, col_off=0, prev=None):
    if mode == "nn":
        (M, K), (K2, N) = a.shape, b.shape
    elif mode == "nt":
        (M, K), (N, K2) = a.shape, b.shape
    else:
        (K, M), (K2, N) = a.shape, b.shape
    assert K == K2 and M % tm == 0 and N % tn == 0 and K % tk == 0 and col_off % tn == 0
    nk = K // tk
    out_cols = N if out_cols is None else out_cols
    off = col_off // tn

    def body(*refs):
        if prev is None:
            a_ref, b_ref, o_ref, acc = refs
        else:
            a_ref, b_ref, _, o_ref, acc = refs
        k = pl.program_id(2)
        av = a_ref[...].astype(BF)
        bv = b_ref[...].astype(BF)
        if mode == "nn":
            part = jnp.dot(av, bv, preferred_element_type=F32)
        elif mode == "nt":
            part = lax.dot_general(av, bv, CONTRACT_LAST, preferred_element_type=F32)
        else:
            part = lax.dot_general(av, bv, CONTRACT_FIRST, preferred_element_type=F32)
        if nk == 1:
            o_ref[...] = part.astype(out_dtype)
            return

        @pl.when(k == 0)
        def _():
            acc[...] = part

        @pl.when((k > 0) & (k < nk - 1))
        def _():
            acc[...] += part

        @pl.when(k == nk - 1)
        def _():
            o_ref[...] = (acc[...] + part).astype(out_dtype)

    if mode == "nn":
        a_spec = pl.BlockSpec((tm, tk), lambda m, n, k: (m, k))
        b_spec = pl.BlockSpec((tk, tn), lambda m, n, k: (k, n))
    elif mode == "nt":
        a_spec = pl.BlockSpec((tm, tk), lambda m, n, k: (m, k))
        b_spec = pl.BlockSpec((tn, tk), lambda m, n, k: (n, k))
    else:
        a_spec = pl.BlockSpec((tk, tm), lambda m, n, k: (k, m))
        b_spec = pl.BlockSpec((tk, tn), lambda m, n, k: (k, n))
    in_specs = [a_spec, b_spec]
    args = [a, b]
    aliases = {}
    if prev is not None:
        in_specs.append(pl.BlockSpec(memory_space=pl.ANY))
        args.append(prev)
        aliases = {2: 0}
    return _pcall(
        body, name=name, grid=(M // tm, N // tn, nk), in_specs=in_specs,
        out_specs=pl.BlockSpec((tm, tn), lambda m, n, k: (m, n + off)),
        out_shape=jax.ShapeDtypeStruct((M, out_cols), out_dtype),
        scratch_shapes=[pltpu.VMEM((tm, tn), F32)], input_output_aliases=aliases,
        compiler_params=_params(("parallel", "parallel", "arbitrary")),
    )(*args)


def cond_matmul(c_all, w, name):
    B, K = c_all.shape
    N = w.shape[1]
    tn = 256
    assert N % tn == 0

    def body(c_ref, w_ref, o_ref):
        cv = c_ref[...]
        ca = cv * _sigmoid(cv)
        o_ref[...] = jnp.dot(ca, w_ref[...], preferred_element_type=F32, precision=lax.Precision.HIGHEST)

    return _pcall(
        body, name=name, grid=(N // tn,),
        in_specs=[pl.BlockSpec((B, K), lambda n: (0, 0)), pl.BlockSpec((K, tn), lambda n: (0, n))],
        out_specs=pl.BlockSpec((B, tn), lambda n: (0, n)),
        out_shape=jax.ShapeDtypeStruct((B, N), F32), compiler_params=_params(("parallel",)),
    )(c_all, w)


FFN_FWD_TS = 512
FFN_BWD_TS = 256


def _resident(shape, index_map):
    return pl.BlockSpec(shape, index_map, pipeline_mode=pl.Buffered(1))


def ffn_fwd(x, vecs, wi, wo, rows, name):
    r_sh, r_sc, r_gt, r_g = rows
    S, D = x.shape
    Fd = wo.shape[0]
    ts = min(FFN_FWD_TS, S)

    def body(x_ref, v_ref, wg_ref, wu_ref, wo_ref, xo_ref, g_ref, u_ref, y_ref):
        xv = x_ref[...]
        r = lax.rsqrt(_rowmean(xv * xv) + EPS)
        gs = _row(v_ref, r_g) * (1.0 + _row(v_ref, r_sc))
        hb = (xv * r * gs + _row(v_ref, r_sh)).astype(BF)
        G = jnp.dot(hb, wg_ref[...], preferred_element_type=F32)
        U = jnp.dot(hb, wu_ref[...], preferred_element_type=F32)
        g_ref[...] = G.astype(BF)
        u_ref[...] = U.astype(BF)
        act = (G * _sigmoid(G) * U).astype(BF)
        Y = jnp.dot(act, wo_ref[...], preferred_element_type=F32)
        y_ref[...] = Y
        xo_ref[...] = xv + (MACARON_W * _row(v_ref, r_gt)) * Y

    tok = pl.BlockSpec((ts, D), lambda i: (i, 0))
    hid = pl.BlockSpec((ts, Fd), lambda i: (i, 0))
    return _pcall(
        body, name=name, grid=(S // ts,),
        in_specs=[tok, pl.BlockSpec(vecs.shape, lambda i: (0, 0)), _resident((D, Fd), lambda i: (0, 0)),
                  _resident((D, Fd), lambda i: (0, 1)), _resident((Fd, D), lambda i: (0, 0))],
        out_specs=[tok, hid, hid, tok],
        out_shape=[jax.ShapeDtypeStruct((S, D), F32), jax.ShapeDtypeStruct((S, Fd), BF),
                   jax.ShapeDtypeStruct((S, Fd), BF), jax.ShapeDtypeStruct((S, D), F32)],
        compiler_params=_params(("arbitrary",)),
    )(x, vecs, wi, wi, wo)


def ffn_fwd_in(x, vecs, wi, rows, name):
    r_sh, r_sc, r_gt, r_g = rows
    S, D = x.shape
    Fd = wi.shape[1] // 2
    ts = min(FFN_FWD_TS, S)

    def body(x_ref, v_ref, wg_ref, wu_ref, g_ref, u_ref, a_ref):
        xv = x_ref[...]
        r = lax.rsqrt(_rowmean(xv * xv) + EPS)
        gs = _row(v_ref, r_g) * (1.0 + _row(v_ref, r_sc))
        hb = (xv * r * gs + _row(v_ref, r_sh)).astype(BF)
        G = jnp.dot(hb, wg_ref[...], preferred_element_type=F32)
        U = jnp.dot(hb, wu_ref[...], preferred_element_type=F32)
        g_ref[...] = G.astype(BF)
        u_ref[...] = U.astype(BF)
        a_ref[...] = (G * _sigmoid(G) * U).astype(BF)

    hid = pl.BlockSpec((ts, Fd), lambda i: (i, 0))
    return _pcall(
        body, name=name, grid=(S // ts,),
        in_specs=[pl.BlockSpec((ts, D), lambda i: (i, 0)), pl.BlockSpec(vecs.shape, lambda i: (0, 0)),
                  _resident((D, Fd), lambda i: (0, 0)), _resident((D, Fd), lambda i: (0, 1))],
        out_specs=[hid, hid, hid], out_shape=[jax.ShapeDtypeStruct((S, Fd), BF)] * 3,
        compiler_params=_params(("arbitrary",)),
    )(x, vecs, wi, wi)


def ffn_fwd_out(act, x, vecs, wo, rows, name):
    r_sh, r_sc, r_gt, r_g = rows
    S, D = x.shape
    Fd = wo.shape[0]
    ts = min(FFN_FWD_TS, S)

    def body(a_ref, x_ref, v_ref, wo_ref, xo_ref, y_ref):
        Y = jnp.dot(a_ref[...], wo_ref[...], preferred_element_type=F32)
        y_ref[...] = Y
        xo_ref[...] = x_ref[...] + (MACARON_W * _row(v_ref, r_gt)) * Y

    tok = pl.BlockSpec((ts, D), lambda i: (i, 0))
    return _pcall(
        body, name=name, grid=(S // ts,),
        in_specs=[pl.BlockSpec((ts, Fd), lambda i: (i, 0)), tok, pl.BlockSpec(vecs.shape, lambda i: (0, 0)),
                  _resident((Fd, D), lambda i: (0, 0))],
        out_specs=[tok, tok], out_shape=[jax.ShapeDtypeStruct((S, D), F32)] * 2,
        compiler_params=_params(("arbitrary",)),
    )(act, x, vecs, wo)


def ffn_bwd(dxo, x, vecs, gs_, us_, y, wi, wo, rows, name):
    r_sh, r_sc, r_gt, r_g = rows
    S, D = x.shape
    Fd = wo.shape[0]
    ts = min(FFN_BWD_TS, S)

    def body(dxo_ref, x_ref, v_ref, g_ref, u_ref, y_ref, wg_ref, wu_ref, wo_ref,
             dx_ref, act_ref, dg_ref, du_ref, hb_ref, dyb_ref, vg_ref):
        @pl.when(pl.program_id(0) == 0)
        def _():
            vg_ref[...] = jnp.zeros_like(vg_ref)

        dxo_v = dxo_ref[...]
        dyb = ((MACARON_W * _row(v_ref, r_gt)) * dxo_v).astype(BF)
        dyb_ref[...] = dyb
        vg_ref[0:1, :] += MACARON_W * _colsum(dxo_v * y_ref[...])
        dA = lax.dot_general(dyb, wo_ref[...], CONTRACT_LAST, preferred_element_type=F32)
        G = g_ref[...].astype(F32)
        U = u_ref[...].astype(F32)
        sg = _sigmoid(G)
        sl = G * sg
        dU = (dA * sl).astype(BF)
        dG = (dA * U * (sg * (1.0 + G * (1.0 - sg)))).astype(BF)
        act_ref[...] = (sl * U).astype(BF)
        dg_ref[...] = dG
        du_ref[...] = dU
        dh = (lax.dot_general(dG, wg_ref[...], CONTRACT_LAST, preferred_element_type=F32)
              + lax.dot_general(dU, wu_ref[...], CONTRACT_LAST, preferred_element_type=F32))
        xv = x_ref[...]
        r = lax.rsqrt(_rowmean(xv * xv) + EPS)
        n = xv * r
        g = _row(v_ref, r_g)
        sc1 = 1.0 + _row(v_ref, r_sc)
        gsc = g * sc1
        hb_ref[...] = (n * gsc + _row(v_ref, r_sh)).astype(BF)
        dhn = dh * n
        vg_ref[1:2, :] += _colsum(dh)
        vg_ref[2:3, :] += _colsum(dhn) * g
        vg_ref[3:4, :] += _colsum(dhn) * sc1
        dn = dh * gsc
        dx_ref[...] = dxo_v + r * (dn - n * _rowmean(dn * n))

    tok = pl.BlockSpec((ts, D), lambda i: (i, 0))
    hid = pl.BlockSpec((ts, Fd), lambda i: (i, 0))
    return _pcall(
        body, name=name, grid=(S // ts,),
        in_specs=[tok, tok, pl.BlockSpec(vecs.shape, lambda i: (0, 0)), hid, hid, tok, _resident((D, Fd), lambda i: (0, 0)),
                  _resident((D, Fd), lambda i: (0, 1)), _resident((Fd, D), lambda i: (0, 0))],
        out_specs=[tok, hid, hid, hid, tok, tok, pl.BlockSpec((SUBLANES, D), lambda i: (0, 0))],
        out_shape=[jax.ShapeDtypeStruct((S, D), F32), jax.ShapeDtypeStruct((S, Fd), BF),
                   jax.ShapeDtypeStruct((S, Fd), BF), jax.ShapeDtypeStruct((S, Fd), BF),
                   jax.ShapeDtypeStruct((S, D), BF), jax.ShapeDtypeStruct((S, D), BF),
                   jax.ShapeDtypeStruct((SUBLANES, D), F32)],
        compiler_params=_params(("arbitrary",)),
    )(dxo, x, vecs, gs_, us_, y, wi, wi, wo)


def final_fwd_bwd(x, tgt, vecs, name):
    S, D = x.shape
    ts = min(512, S)

    def body(x_ref, t_ref, v_ref, dx_ref, vg_ref):
        @pl.when(pl.program_id(0) == 0)
        def _():
            vg_ref[...] = jnp.zeros_like(vg_ref)

        xv = x_ref[...]
        r = lax.rsqrt(_rowmean(xv * xv) + EPS)
        n = xv * r
        g = _row(v_ref, R_GF)
        sc1 = 1.0 + _row(v_ref, R_FSC)
        gsc = g * sc1
        e = n * gsc + _row(v_ref, R_FSH) - t_ref[...]
        vg_ref[3:4, :] += _colsum(e * e)
        dout = e * (1.0 / D)
        dn_ = dout * n
        vg_ref[0:1, :] += _colsum(dout)
        vg_ref[1:2, :] += _colsum(dn_) * g
        vg_ref[2:3, :] += _colsum(dn_) * sc1
        dn = dout * gsc
        dx_ref[...] = r * (dn - n * _rowmean(dn * n))

    tok = pl.BlockSpec((ts, D), lambda i: (i, 0))
    return _pcall(
        body, name=name, grid=(S // ts,),
        in_specs=[tok, tok, pl.BlockSpec(vecs.shape, lambda i: (0, 0))],
        out_specs=[tok, pl.BlockSpec((SUBLANES, D), lambda i: (0, 0))],
        out_shape=[jax.ShapeDtypeStruct((S, D), F32), jax.ShapeDtypeStruct((SUBLANES, D), F32)],
        compiler_params=_params(("arbitrary",)),
    )(x, tgt, vecs)


def norm_matmul(x, vecs, w, rows, name):
    r_sh, r_sc, r_g = rows
    S, D = x.shape
    N = w.shape[1]
    ts = min(512, S)

    def body(x_ref, v_ref, w_ref, o_ref):
        xv = x_ref[...]
        r = lax.rsqrt(_rowmean(xv * xv) + EPS)
        gs = _row(v_ref, r_g) * (1.0 + _row(v_ref, r_sc))
        hb = (xv * r * gs + _row(v_ref, r_sh)).astype(BF)
        o_ref[...] = jnp.dot(hb, w_ref[...], preferred_element_type=F32)

    return _pcall(
        body, name=name, grid=(S // ts,),
        in_specs=[pl.BlockSpec((ts, D), lambda i: (i, 0)), pl.BlockSpec(vecs.shape, lambda i: (0, 0)),
                  _resident((D, N), lambda i: (0, 0))],
        out_specs=pl.BlockSpec((ts, N), lambda i: (i, 0)),
        out_shape=jax.ShapeDtypeStruct((S, N), F32),
        compiler_params=_params(("arbitrary",)),
    )(x, vecs, w)


SEQ_TT = 256
SCAN_SEGMENTS = 4
CONV_PAD = 32


def conv_fwd(proj, convw4, conv_b, name):
    S = proj.shape[0]
    M = conv_b.shape[1]
    nb = M // LANES
    tt = min(SEQ_TT, S)

    def body(uv_ref, ug_ref, w_ref, b_ref, cq_ref, qp):
        qp[0:CONV_PAD, :] = jnp.zeros((CONV_PAD, LANES), F32)

        def step(t, carry):
            base = pl.multiple_of(t * tt, tt)
            qp[pl.ds(base + CONV_PAD, tt), :] = uv_ref[pl.ds(base, tt), :] * _sigmoid(ug_ref[pl.ds(base, tt), :])
            acc = jnp.broadcast_to(b_ref[...], (tt, LANES))
            for k in range(CONV_WIDTH):
                acc = acc + w_ref[k:k + 1, :] * qp[pl.ds(base + (CONV_PAD - CONV_WIDTH + 1) + k, tt), :]
            cq_ref[pl.ds(base, tt), :] = acc
            return carry

        lax.fori_loop(0, S // tt, step, 0)

    return _pcall(
        body, name=name, grid=(nb,),
        in_specs=[pl.BlockSpec((S, LANES), lambda c: (0, c)), pl.BlockSpec((S, LANES), lambda c: (0, c + nb)),
                  pl.BlockSpec((None, 32, LANES), lambda c: (c, 0, 0)), pl.BlockSpec((1, LANES), lambda c: (0, c))],
        out_specs=pl.BlockSpec((S, LANES), lambda c: (0, c)),
        out_shape=jax.ShapeDtypeStruct((S, M), F32),
        scratch_shapes=[pltpu.VMEM((S + CONV_PAD, LANES), F32)],
        compiler_params=_params(("arbitrary",)),
    )(proj, proj, convw4, conv_b)


def conv_bwd(dcq, proj, convw4, name):
    S, M = dcq.shape
    nb = M // LANES
    tt = min(SEQ_TT, S)
    off = CONV_PAD - CONV_WIDTH + 1

    def body(dcq_ref, uv_ref, ug_ref, w_ref, duv_ref, dug_ref, dw_ref, db_ref, qp, dp, dw8, db8):
        qp[0:CONV_PAD, :] = jnp.zeros((CONV_PAD, LANES), F32)
        dp[S:S + CONV_PAD, :] = jnp.zeros((CONV_PAD, LANES), F32)
        dw8[...] = jnp.zeros_like(dw8)
        db8[...] = jnp.zeros_like(db8)

        def fill(t, carry):
            base = pl.multiple_of(t * tt, tt)
            qp[pl.ds(base + CONV_PAD, tt), :] = uv_ref[pl.ds(base, tt), :] * _sigmoid(ug_ref[pl.ds(base, tt), :])
            dp[pl.ds(base, tt), :] = dcq_ref[pl.ds(base, tt), :]
            return carry

        lax.fori_loop(0, S // tt, fill, 0)

        def step(t, carry):
            base = pl.multiple_of(t * tt, tt)
            d_t = dcq_ref[pl.ds(base, tt), :]
            db8[...] += d_t.reshape(tt // SUBLANES, SUBLANES, LANES).sum(axis=0)
            dq = jnp.zeros((tt, LANES), F32)
            for k in range(CONV_WIDTH):
                prod = d_t * qp[pl.ds(base + off + k, tt), :]
                dw8[k] += prod.reshape(tt // SUBLANES, SUBLANES, LANES).sum(axis=0)
                dq = dq + w_ref[k:k + 1, :] * dp[pl.ds(base + (CONV_WIDTH - 1) - k, tt), :]
            uv = uv_ref[pl.ds(base, tt), :]
            sg = _sigmoid(ug_ref[pl.ds(base, tt), :])
            duv_ref[pl.ds(base, tt), :] = (dq * sg).astype(BF)
            dug_ref[pl.ds(base, tt), :] = (dq * uv * sg * (1.0 - sg)).astype(BF)
            return carry

        lax.fori_loop(0, S // tt, step, 0)
        dw_ref[...] = jnp.zeros_like(dw_ref)
        for k in range(CONV_WIDTH):
            dw_ref[k:k + 1, :] = _colsum(dw8[k])
        db_ref[...] = _colsum(db8[...])

    col = lambda o: pl.BlockSpec((S, LANES), lambda c: (0, c + o))
    return _pcall(
        body, name=name, grid=(nb,),
        in_specs=[col(0), col(0), col(nb), pl.BlockSpec((None, 32, LANES), lambda c: (c, 0, 0))],
        out_specs=[col(0), col(0), pl.BlockSpec((None, 32, LANES), lambda c: (c, 0, 0)),
                   pl.BlockSpec((1, LANES), lambda c: (0, c))],
        out_shape=[jax.ShapeDtypeStruct((S, M), BF), jax.ShapeDtypeStruct((S, M), BF),
                   jax.ShapeDtypeStruct((nb, 32, LANES), F32), jax.ShapeDtypeStruct((1, M), F32)],
        scratch_shapes=[pltpu.VMEM((S + CONV_PAD, LANES), F32), pltpu.VMEM((S + CONV_PAD, LANES), F32),
                        pltpu.VMEM((32, SUBLANES, LANES), F32), pltpu.VMEM((SUBLANES, LANES), F32)],
        compiler_params=_params(("arbitrary",)),
    )(dcq, proj, proj, convw4)


def _log_sigmoid(x):
    return jnp.minimum(x, 0.0) - jnp.log(1.0 + jnp.exp(-jnp.abs(x)))


def _rg_gate_terms(ra, ls):
    la = RG_C * ra * ls
    a = jnp.exp(la)
    th = jnp.tanh(la)
    mult = jnp.sqrt(-2.0 * th / (1.0 - th))
    return a, mult


def rnn_fwd(proj, rnnw4, rnn_b, bda, bdi, b_a, b_i, lam, name):
    S = proj.shape[0]
    M = rnn_b.shape[1]
    nb = M // LANES
    tt = min(SEQ_TT, S)
    KW = RNN_CONV_WIDTH
    nseg = SCAN_SEGMENTS if S % (SCAN_SEGMENTS * tt) == 0 else 1

    def body(ux_ref, w_ref, rb_ref, bda_ref, bdi_ref, ba_ref, bi_ref, lam_ref,
             xr_ref, ra_ref, ii_ref, h_ref, uxp, a_sc, b_sc):
        uxp[0:SUBLANES, :] = jnp.zeros((SUBLANES, LANES), F32)
        ls = _log_sigmoid(lam_ref[...])

        def step(t, carry):
            base = pl.multiple_of(t * tt, tt)
            uxp[pl.ds(base + SUBLANES, tt), :] = ux_ref[pl.ds(base, tt), :]
            xr = jnp.broadcast_to(rb_ref[...], (tt, LANES))
            for k in range(KW):
                xr = xr + w_ref[k:k + 1, :] * uxp[pl.ds(base + (SUBLANES - KW + 1) + k, tt), :]
            xb = xr.astype(BF)
            ra = _sigmoid(jnp.dot(xb, bda_ref[...], preferred_element_type=F32) + ba_ref[...])
            ii = _sigmoid(jnp.dot(xb, bdi_ref[...], preferred_element_type=F32) + bi_ref[...])
            a, mult = _rg_gate_terms(ra, ls)
            xr_ref[pl.ds(base, tt), :] = xr
            ra_ref[pl.ds(base, tt), :] = ra
            ii_ref[pl.ds(base, tt), :] = ii
            a_sc[pl.ds(base, tt), :] = a
            b_sc[pl.ds(base, tt), :] = mult * (ii * xr)
            return carry

        lax.fori_loop(0, S // tt, step, 0)

        rows = lax.broadcasted_iota(I32, (SUBLANES, LANES), 0)
        seg = S // nseg
        last = lambda v: jnp.broadcast_to(v[SUBLANES - 1:SUBLANES, :], (SUBLANES, LANES))

        def scan(t, carry):
            hs, ps = carry
            new_h, new_p = [], []
            for s in range(nseg):
                base = pl.multiple_of(s * seg + t * SUBLANES, SUBLANES)
                A = a_sc[pl.ds(base, SUBLANES), :]
                B = b_sc[pl.ds(base, SUBLANES), :]
                for d in (1, 2, 4):
                    As = jnp.where(rows >= d, pltpu.roll(A, d, axis=0), 1.0)
                    Bs = jnp.where(rows >= d, pltpu.roll(B, d, axis=0), 0.0)
                    B = A * Bs + B
                    A = A * As
                hh = B + A * hs[s]
                h_ref[pl.ds(base, SUBLANES), :] = hh
                pp = A * ps[s]
                if s > 0:
                    a_sc[pl.ds(base, SUBLANES), :] = pp
                new_h.append(last(hh))
                new_p.append(last(pp))
            return tuple(new_h), tuple(new_p)

        zero8 = jnp.zeros((SUBLANES, LANES), F32)
        one8 = jnp.ones((SUBLANES, LANES), F32)
        hs, ps = lax.fori_loop(0, seg // SUBLANES, scan, ((zero8,) * nseg, (one8,) * nseg))
        carry_in = hs[0]
        for s in range(1, nseg):
            c_row = carry_in[0:1, :]

            def fix(t, c, s=s, c_row=c_row):
                base = pl.multiple_of(s * seg + t * tt, tt)
                h_ref[pl.ds(base, tt), :] = h_ref[pl.ds(base, tt), :] + a_sc[pl.ds(base, tt), :] * c_row
                return c

            lax.fori_loop(0, seg // tt, fix, 0)
            carry_in = hs[s] + ps[s] * carry_in

    col = lambda o: pl.BlockSpec((S, LANES), lambda c: (0, c + o))
    vec = pl.BlockSpec((1, LANES), lambda c: (0, c))
    diag = pl.BlockSpec((LANES, LANES), lambda c: (c, c))
    return _pcall(
        body, name=name, grid=(nb,),
        in_specs=[col(2 * nb), pl.BlockSpec((None, SUBLANES, LANES), lambda c: (c, 0, 0)), vec, diag, diag, vec, vec, vec],
        out_specs=[col(0)] * 4,
        out_shape=[jax.ShapeDtypeStruct((S, M), F32)] * 4,
        scratch_shapes=[pltpu.VMEM((S + SUBLANES, LANES), F32), pltpu.VMEM((S, LANES), F32), pltpu.VMEM((S, LANES), F32)],
        compiler_params=_params(("arbitrary",)),
    )(proj, rnnw4, rnn_b, bda, bdi, b_a, b_i, lam)


def rnn_bwd(dhout, h, xr, ra, ii, proj, rnnw4, bda, bdi, lam, name):
    S, M = h.shape
    nb = M // LANES
    tt = min(SEQ_TT, S)
    KW = RNN_CONV_WIDTH
    SL = SUBLANES
    nseg = SCAN_SEGMENTS if S % (SCAN_SEGMENTS * tt) == 0 else 1

    def body(dh_ref, h_ref, xr_ref, ra_ref, ii_ref, ux_ref, w_ref, bda_ref, bdi_ref, lam_ref,
             dux_ref, dwa_ref, dwi_ref, drw_ref, vec_ref,
             a_sc, hp, g_sc, dpa_sc, dpi_sc, dxp, uxp, acc8, drw8, p_sc):
        zero8 = jnp.zeros((SL, LANES), F32)
        a_sc[S:S + SL, :] = zero8
        hp[0:SL, :] = zero8
        dxp[S:S + SL, :] = zero8
        uxp[0:SL, :] = zero8
        acc8[...] = jnp.zeros_like(acc8)
        drw8[...] = jnp.zeros_like(drw8)
        lamv = lam_ref[...]
        ls = _log_sigmoid(lamv)

        def fill(t, carry):
            base = pl.multiple_of(t * tt, tt)
            a_sc[pl.ds(base, tt), :] = jnp.exp(RG_C * ra_ref[pl.ds(base, tt), :] * ls)
            hp[pl.ds(base + SL, tt), :] = h_ref[pl.ds(base, tt), :]
            uxp[pl.ds(base + SL, tt), :] = ux_ref[pl.ds(base, tt), :]
            return carry

        lax.fori_loop(0, S // tt, fill, 0)

        rows = lax.broadcasted_iota(I32, (SL, LANES), 0)
        seg = S // nseg
        nt8 = seg // SL
        first = lambda v: jnp.broadcast_to(v[0:1, :], (SL, LANES))

        def rscan(t, carry):
            gs, ps = carry
            new_g, new_p = [], []
            for s in range(nseg):
                base = pl.multiple_of(s * seg + (nt8 - 1 - t) * SL, SL)
                A = a_sc[pl.ds(base + 1, SL), :]
                B = dh_ref[pl.ds(base, SL), :]
                for d in (1, 2, 4):
                    As = jnp.where(rows < SL - d, pltpu.roll(A, SL - d, axis=0), 1.0)
                    Bs = jnp.where(rows < SL - d, pltpu.roll(B, SL - d, axis=0), 0.0)
                    B = A * Bs + B
                    A = A * As
                g = B + A * gs[s]
                g_sc[pl.ds(base, SL), :] = g
                pp = A * ps[s]
                if s < nseg - 1:
                    p_sc[pl.ds(base, SL), :] = pp
                new_g.append(first(g))
                new_p.append(first(pp))
            return tuple(new_g), tuple(new_p)

        one8 = jnp.ones((SL, LANES), F32)
        gs, ps = lax.fori_loop(0, nt8, rscan, ((zero8,) * nseg, (one8,) * nseg))
        carry_in = gs[nseg - 1]
        for s in range(nseg - 2, -1, -1):
            c_row = carry_in[0:1, :]

            def fix(t, c, s=s, c_row=c_row):
                base = pl.multiple_of(s * seg + t * tt, tt)
                g_sc[pl.ds(base, tt), :] = g_sc[pl.ds(base, tt), :] + p_sc[pl.ds(base, tt), :] * c_row
                return c

            lax.fori_loop(0, seg // tt, fix, 0)
            carry_in = gs[s] + ps[s] * carry_in

        def red8(v):
            return v.reshape(tt // SL, SL, LANES).sum(axis=0)

        def step(t, carry):
            base = pl.multiple_of(t * tt, tt)
            g = g_sc[pl.ds(base, tt), :]
            hprev = hp[pl.ds(base + SL - 1, tt), :]
            xr_t = xr_ref[pl.ds(base, tt), :]
            ra_t = ra_ref[pl.ds(base, tt), :]
            ii_t = ii_ref[pl.ds(base, tt), :]
            a, mult = _rg_gate_terms(ra_t, ls)
            gx = g * xr_t
            dmult = gx * ii_t
            dii = gx * mult
            dxr = g * (mult * ii_t)
            dla = g * hprev * a - dmult * (a * a) / mult
            acc8[3] += red8(dla * ra_t)
            dpa = dla * (RG_C * ls) * ra_t * (1.0 - ra_t)
            dpi = dii * ii_t * (1.0 - ii_t)
            dpab = dpa.astype(BF)
            dpib = dpi.astype(BF)
            dxr = dxr + (lax.dot_general(dpab, bda_ref[...], CONTRACT_LAST, preferred_element_type=F32)
                         + lax.dot_general(dpib, bdi_ref[...], CONTRACT_LAST, preferred_element_type=F32))
            dpa_sc[pl.ds(base, tt), :] = dpab
            dpi_sc[pl.ds(base, tt), :] = dpib
            dxp[pl.ds(base, tt), :] = dxr
            acc8[0] += red8(dxr)
            acc8[1] += red8(dpa)
            acc8[2] += red8(dpi)
            return carry

        lax.fori_loop(0, S // tt, step, 0)

        def convb(t, carry):
            base = pl.multiple_of(t * tt, tt)
            d_t = dxp[pl.ds(base, tt), :]
            dux = jnp.zeros((tt, LANES), F32)
            for k in range(KW):
                drw8[k] += red8(d_t * uxp[pl.ds(base + (SL - KW + 1) + k, tt), :])
                dux = dux + w_ref[k:k + 1, :] * dxp[pl.ds(base + (KW - 1) - k, tt), :]
            dux_ref[pl.ds(base, tt), :] = dux.astype(BF)
            return carry

        lax.fori_loop(0, S // tt, convb, 0)

        xb = xr_ref[...].astype(BF)
        dwa_ref[...] = lax.dot_general(xb, dpa_sc[...], CONTRACT_FIRST, preferred_element_type=F32)
        dwi_ref[...] = lax.dot_general(xb, dpi_sc[...], CONTRACT_FIRST, preferred_element_type=F32)
        drw_ref[...] = jnp.zeros_like(drw_ref)
        vec_ref[...] = jnp.zeros_like(vec_ref)
        for k in range(KW):
            drw_ref[k:k + 1, :] = _colsum(drw8[k])
        for k in range(3):
            vec_ref[k:k + 1, :] = _colsum(acc8[k])
        vec_ref[3:4, :] = _colsum(acc8[3]) * (RG_C * _sigmoid(-lamv))

    col = lambda o: pl.BlockSpec((S, LANES), lambda c: (0, c + o))
    vec = pl.BlockSpec((1, LANES), lambda c: (0, c))
    diag = pl.BlockSpec((LANES, LANES), lambda c: (c, c))
    blk3 = lambda r: pl.BlockSpec((None, r, LANES), lambda c: (c, 0, 0))
    return _pcall(
        body, name=name, grid=(nb,),
        in_specs=[col(0), col(0), col(0), col(0), col(0), col(2 * nb), blk3(SL), diag, diag, vec],
        out_specs=[col(0), blk3(LANES), blk3(LANES), blk3(SL), pl.BlockSpec((SL, LANES), lambda c: (0, c))],
        out_shape=[jax.ShapeDtypeStruct((S, M), BF), jax.ShapeDtypeStruct((nb, LANES, LANES), F32),
                   jax.ShapeDtypeStruct((nb, LANES, LANES), F32), jax.ShapeDtypeStruct((nb, SL, LANES), F32),
                   jax.ShapeDtypeStruct((SL, M), F32)],
        scratch_shapes=[pltpu.VMEM((S + SL, LANES), F32), pltpu.VMEM((S + SL, LANES), F32), pltpu.VMEM((S, LANES), F32),
                        pltpu.VMEM((S, LANES), BF), pltpu.VMEM((S, LANES), BF), pltpu.VMEM((S + SL, LANES), F32),
                        pltpu.VMEM((S + SL, LANES), F32), pltpu.VMEM((SL, SL, LANES), F32), pltpu.VMEM((SL, SL, LANES), F32),
                        pltpu.VMEM((S, LANES), F32)],
        compiler_params=_params(("arbitrary",)),
    )(dhout, h, xr, ra, ii, proj, rnnw4, bda, bdi, lam)


GELU_K = 0.7978845608028654
GELU_C = 0.044715


def _layernorm_parts(cq):
    mu = _rowmean(cq)
    d = cq - mu
    rstd = lax.rsqrt(_rowmean(d * d) + EPS)
    return d * rstd, rstd


def mix_out(cq, proj, h, x, vecs, lnv, wout, name):
    S, D = x.shape
    M = cq.shape[1]
    ts = min(512, S)

    def body(cq_ref, uy_ref, h_ref, x_ref, v_ref, ln_ref, w_ref, xo_ref, ym_ref, yc_ref):
        z, _ = _layernorm_parts(cq_ref[...])
        l = z * _row(ln_ref, 0) + _row(ln_ref, 1)
        yc_ref[:, 0:M] = (l * _sigmoid(l)).astype(BF)
        uy = uy_ref[...]
        gelu = 0.5 * uy * (1.0 + jnp.tanh(GELU_K * (uy + GELU_C * uy * uy * uy)))
        yc_ref[:, M:2 * M] = (gelu * h_ref[...]).astype(BF)
        ym = jnp.dot(yc_ref[...], w_ref[...], preferred_element_type=F32)
        ym_ref[...] = ym
        xo_ref[...] = x_ref[...] + _row(v_ref, R_GT2) * ym

    tok = pl.BlockSpec((ts, D), lambda i: (i, 0))
    mtok = lambda o: pl.BlockSpec((ts, M), lambda i: (i, o))
    return _pcall(
        body, name=name, grid=(S // ts,),
        in_specs=[mtok(0), mtok(3), mtok(0), tok, pl.BlockSpec(vecs.shape, lambda i: (0, 0)),
                  pl.BlockSpec(lnv.shape, lambda i: (0, 0)), pl.BlockSpec(wout.shape, lambda i: (0, 0))],
        out_specs=[tok, tok, pl.BlockSpec((ts, 2 * M), lambda i: (i, 0))],
        out_shape=[jax.ShapeDtypeStruct((S, D), F32), jax.ShapeDtypeStruct((S, D), F32),
                   jax.ShapeDtypeStruct((S, 2 * M), BF)],
        compiler_params=_params(("arbitrary",)),
    )(cq, proj, h, x, vecs, lnv, wout)


def mix_out_bwd(dxo, ym, vecs, wout, cq, lnv, proj, h, name):
    S, D = dxo.shape
    M = cq.shape[1]
    ts = min(512, S)

    def body(dxo_ref, ym_ref, v_ref, w_ref, cq_ref, ln_ref, uy_ref, h_ref,
             dcq_ref, dh_ref, duy_ref, dyb_ref, vgd_ref, vgm_ref):
        @pl.when(pl.program_id(0) == 0)
        def _():
            vgd_ref[...] = jnp.zeros_like(vgd_ref)
            vgm_ref[...] = jnp.zeros_like(vgm_ref)

        dxo_v = dxo_ref[...]
        dyb = (_row(v_ref, R_GT2) * dxo_v).astype(BF)
        dyb_ref[...] = dyb
        vgd_ref[0:1, :] += _colsum(dxo_v * ym_ref[...])
        dycat = lax.dot_general(dyb, w_ref[...], CONTRACT_LAST, preferred_element_type=F32)
        dyc = dycat[:, 0:M]
        dyr = dycat[:, M:2 * M]
        z, rstd = _layernorm_parts(cq_ref[...])
        lng = _row(ln_ref, 0)
        l = z * lng + _row(ln_ref, 1)
        sl = _sigmoid(l)
        dl = dyc * (sl * (1.0 + l * (1.0 - sl)))
        vgm_ref[0:1, :] += _colsum(dl * z)
        vgm_ref[1:2, :] += _colsum(dl)
        dz = dl * lng
        dcq_ref[...] = rstd * (dz - _rowmean(dz) - z * _rowmean(dz * z))
        uy = uy_ref[...]
        u2 = uy * uy
        th = jnp.tanh(GELU_K * (uy + GELU_C * uy * u2))
        gelu = 0.5 * uy * (1.0 + th)
        dgelu = 0.5 * (1.0 + th) + 0.5 * uy * (1.0 - th * th) * (GELU_K * (1.0 + 3.0 * GELU_C * u2))
        dh_ref[...] = dyr * gelu
        duy_ref[...] = (dyr * h_ref[...] * dgelu).astype(BF)

    tok = pl.BlockSpec((ts, D), lambda i: (i, 0))
    mtok = lambda o: pl.BlockSpec((ts, M), lambda i: (i, o))
    return _pcall(
        body, name=name, grid=(S // ts,),
        in_specs=[tok, tok, pl.BlockSpec(vecs.shape, lambda i: (0, 0)), pl.BlockSpec(wout.shape, lambda i: (0, 0)),
                  mtok(0), pl.BlockSpec(lnv.shape, lambda i: (0, 0)), mtok(3), mtok(0)],
        out_specs=[mtok(0), mtok(0), mtok(0), tok, pl.BlockSpec((SUBLANES, D), lambda i: (0, 0)),
                   pl.BlockSpec((SUBLANES, M), lambda i: (0, 0))],
        out_shape=[jax.ShapeDtypeStruct((S, M), F32)] * 2 + [jax.ShapeDtypeStruct((S, M), BF), jax.ShapeDtypeStruct((S, D), BF),
                   jax.ShapeDtypeStruct((SUBLANES, D), F32), jax.ShapeDtypeStruct((SUBLANES, M), F32)],
        compiler_params=_params(("arbitrary",)),
    )(dxo, ym, vecs, wout, cq, lnv, proj, h)


def mix_in_bwd(dparts, x, dxo, vecs, win, name):
    S, D = x.shape
    M = dparts[0].shape[1]
    ts = min(512, S)

    def body(d0, d1, d2, d3, x_ref, dxo_ref, v_ref, w_ref, dx_ref, hb_ref, dp_ref, vg_ref):
        @pl.when(pl.program_id(0) == 0)
        def _():
            vg_ref[...] = jnp.zeros_like(vg_ref)

        for q, dref in enumerate((d0, d1, d2, d3)):
            dp_ref[:, q * M:(q + 1) * M] = dref[...].astype(BF)
        dh = lax.dot_general(dp_ref[...], w_ref[...], CONTRACT_LAST, preferred_element_type=F32)
        xv = x_ref[...]
        r = lax.rsqrt(_rowmean(xv * xv) + EPS)
        n = xv * r
        g = _row(v_ref, R_G2)
        sc1 = 1.0 + _row(v_ref, R_SC2)
        gsc = g * sc1
        hb_ref[...] = (n * gsc + _row(v_ref, R_SH2)).astype(BF)
        dhn = dh * n
        vg_ref[0:1, :] += _colsum(dh)
        vg_ref[1:2, :] += _colsum(dhn) * g
        vg_ref[2:3, :] += _colsum(dhn) * sc1
        dn = dh * gsc
        dx_ref[...] = dxo_ref[...] + r * (dn - n * _rowmean(dn * n))

    tok = pl.BlockSpec((ts, D), lambda i: (i, 0))
    mtok = pl.BlockSpec((ts, M), lambda i: (i, 0))
    return _pcall(
        body, name=name, grid=(S // ts,),
        in_specs=[mtok] * 4 + [tok, tok, pl.BlockSpec(vecs.shape, lambda i: (0, 0)), pl.BlockSpec(win.shape, lambda i: (0, 0))],
        out_specs=[tok, tok, pl.BlockSpec((ts, 4 * M), lambda i: (i, 0)), pl.BlockSpec((SUBLANES, D), lambda i: (0, 0))],
        out_shape=[jax.ShapeDtypeStruct((S, D), F32), jax.ShapeDtypeStruct((S, D), BF),
                   jax.ShapeDtypeStruct((S, 4 * M), BF), jax.ShapeDtypeStruct((SUBLANES, D), F32)],
        compiler_params=_params(("arbitrary",)),
    )(*dparts, x, dxo, vecs, win)


def _adamw(w, g, m, v):
    m = ADAM_B1 * m + (1.0 - ADAM_B1) * g
    v = ADAM_B2 * v + (1.0 - ADAM_B2) * (g * g)
    m_hat = m / (1.0 - ADAM_B1 ** ADAM_STEP)
    v_hat = v / (1.0 - ADAM_B2 ** ADAM_STEP)
    delta = -ADAM_LR * (m_hat / (jnp.sqrt(v_hat) + ADAM_EPS) + ADAM_WD * w)
    return delta, m, v


def adam_big(w, g, m, v, name):
    R, C = w.shape
    tr = 256 if R % 256 == 0 else R // 2 if (R // 2) % SUBLANES == 0 and R > 512 else R
    tc = C if C <= 1536 else (1152 if C % 1152 == 0 else 1024)
    assert R % tr == 0 and C % tc == 0

    def body(w_ref, g_ref, m_ref, v_ref, d_ref, nm_ref, nv_ref):
        d, nm, nv = _adamw(w_ref[...], g_ref[...], m_ref[...], v_ref[...])
        d_ref[...] = d
        nm_ref[...] = nm
        nv_ref[...] = nv

    blk = pl.BlockSpec((tr, tc), lambda i, j: (i, j))
    return _pcall(
        body, name=name, grid=(R // tr, C // tc), in_specs=[blk] * 4, out_specs=[blk] * 3,
        out_shape=[jax.ShapeDtypeStruct((R, C), F32)] * 3, compiler_params=_params(("parallel", "parallel")),
    )(w, g, m, v)


def adam_cond(c_all, dmod, w, m, v, name):
    B, Kin = c_all.shape
    N = w.shape[1]
    tn = 768 if N % 768 == 0 else 256
    assert N % tn == 0

    def body(c_ref, d_ref, w_ref, m_ref, v_ref, g_ref, dl_ref, nm_ref, nv_ref):
        cv = c_ref[...]
        ca = cv * _sigmoid(cv)
        g = lax.dot_general(ca, d_ref[...], CONTRACT_FIRST, preferred_element_type=F32, precision=lax.Precision.HIGHEST)
        d, nm, nv = _adamw(w_ref[...], g, m_ref[...], v_ref[...])
        g_ref[...] = g
        dl_ref[...] = d
        nm_ref[...] = nm
        nv_ref[...] = nv

    blk = pl.BlockSpec((Kin, tn), lambda n: (0, n))
    return _pcall(
        body, name=name, grid=(N // tn,),
        in_specs=[pl.BlockSpec((B, Kin), lambda n: (0, 0)), pl.BlockSpec((B, tn), lambda n: (0, n)), blk, blk, blk],
        out_specs=[blk] * 4, out_shape=[jax.ShapeDtypeStruct((Kin, N), F32)] * 4,
        compiler_params=_params(("parallel",)),
    )(c_all, dmod, w, m, v)


def adam_small(ws, gs, ms, vs, name):
    n = len(ws)

    def body(*refs):
        ins, outs = refs[:4 * n], refs[4 * n:]
        for k in range(n):
            d, nm, nv = _adamw(ins[k][...], ins[n + k][...], ins[2 * n + k][...], ins[3 * n + k][...])
            outs[k][...] = d
            outs[n + k][...] = nm
            outs[2 * n + k][...] = nv

    specs = [pl.BlockSpec(w.shape, lambda i: (0, 0)) for w in ws]
    shapes = [jax.ShapeDtypeStruct(w.shape, F32) for w in ws]
    out = _pcall(body, name=name, grid=(1,), in_specs=specs * 4, out_specs=specs * 3, out_shape=shapes * 3,
                 compiler_params=_params(("arbitrary",)))(*ws, *gs, *ms, *vs)
    return out[:n], out[n:2 * n], out[2 * n:]


def _me():
    return lax.axis_index("x"), lax.axis_index("y"), lax.axis_index("c")


def _flip(x, y, p):
    return (x ^ (p >> 1) if (p >> 1) else x), (y ^ (p & 1) if (p & 1) else y)


def _handshake(peers):
    barrier = pltpu.get_barrier_semaphore()
    for peer in peers:
        pl.semaphore_signal(barrier, inc=1, device_id=peer, device_id_type=MESH)
    pl.semaphore_wait(barrier, len(peers))


def _seq_call(body, *, name, n_in, out_shape, sem_shapes, collective_id):
    del n_in
    return pl.kernel(body, out_type=out_shape, mesh=plsc.ScalarSubcoreMesh(axis_name="sq", num_cores=1), name=name,
                     scratch_types=sem_shapes, compiler_params=pltpu.CompilerParams(collective_id=collective_id))


def _hbm_comm_call(body, *, name, n_in, out_shape, sem_shapes, seq_id):
    if seq_id is not None:
        return _seq_call(body, name=name, n_in=n_in, out_shape=out_shape, sem_shapes=sem_shapes, collective_id=seq_id)
    anyspec = pl.BlockSpec(memory_space=pl.ANY)
    return _pcall(body, name=name, in_specs=[anyspec] * n_in, out_specs=[anyspec] * len(out_shape), out_shape=out_shape,
                  scratch_shapes=sem_shapes, compiler_params=_params())


def allgather_devices(v, name, with_sum=False):
    R, L = v.shape

    def body(v_ref, out_ref, *rest):
        if with_sum:
            sum_ref, send_sems, recv_sems = rest
        else:
            send_sems, recv_sems = rest
        x, y, c = _me()
        me = 4 * x + 2 * y + c
        out_ref[me] = v_ref[...]
        copies = []
        for p in range(1, N_DEV):
            px, py = _flip(x, y, p >> 1)
            pc = (1 - c) if (p & 1) else c
            peer = 4 * px + 2 * py + pc
            send = pltpu.make_async_remote_copy(src_ref=v_ref, dst_ref=out_ref.at[me], send_sem=send_sems.at[p - 1],
                                                recv_sem=recv_sems.at[p - 1], device_id=(px, py, pc), device_id_type=MESH)
            send.start()
            recv = pltpu.make_async_remote_copy(src_ref=v_ref, dst_ref=out_ref.at[peer], send_sem=send_sems.at[p - 1],
                                                recv_sem=recv_sems.at[p - 1], device_id=(px, py, pc), device_id_type=MESH)
            copies.append((send, recv))
        for send, recv in copies:
            recv.wait_recv()
        for send, recv in copies:
            send.wait_send()
        if with_sum:
            s = out_ref[0]
            for k in range(1, N_DEV):
                s = s + out_ref[k]
            sum_ref[...] = s

    vm = pl.BlockSpec(memory_space=pltpu.VMEM)
    out_shape = [jax.ShapeDtypeStruct((N_DEV, R, L), F32)]
    if with_sum:
        out_shape.append(jax.ShapeDtypeStruct((R, L), F32))
    return _pcall(
        body, name=name, in_specs=[vm], out_specs=[vm] * len(out_shape), out_shape=out_shape,
        scratch_shapes=[pltpu.SemaphoreType.DMA((N_DEV - 1,)), pltpu.SemaphoreType.DMA((N_DEV - 1,))],
        compiler_params=_params(),
    )(v)


def allgather_devices_hbm(v, name, seq_id):
    R, L = v.shape

    def body(v_ref, out_ref, send_sems, recv_sems, local_sem):
        x, y, c = _me()
        me = 4 * x + 2 * y + c
        peers = []
        for p in range(1, N_DEV):
            px, py = _flip(x, y, p >> 1)
            peers.append((px, py, (1 - c) if (p & 1) else c))
        _handshake(peers)
        lc = pltpu.make_async_copy(v_ref, out_ref.at[me], local_sem)
        lc.start()
        copies = []
        for p, (px, py, pc) in enumerate(peers):
            send = pltpu.make_async_remote_copy(src_ref=v_ref, dst_ref=out_ref.at[me], send_sem=send_sems.at[p],
                                                recv_sem=recv_sems.at[p], device_id=(px, py, pc), device_id_type=MESH)
            send.start()
            recv = pltpu.make_async_remote_copy(src_ref=v_ref, dst_ref=out_ref.at[4 * px + 2 * py + pc], send_sem=send_sems.at[p],
                                                recv_sem=recv_sems.at[p], device_id=(px, py, pc), device_id_type=MESH)
            copies.append((send, recv))
        for send, recv in copies:
            recv.wait_recv()
        for send, recv in copies:
            send.wait_send()
        lc.wait()

    return _seq_call(body, name=name, n_in=1, out_shape=[jax.ShapeDtypeStruct((N_DEV, R, L), F32)],
                     sem_shapes=[pltpu.SemaphoreType.DMA((N_DEV - 1,)), pltpu.SemaphoreType.DMA((N_DEV - 1,)),
                                 pltpu.SemaphoreType.DMA], collective_id=seq_id)(v)[0]


def sum_slots(g, name):
    n, R, L = g.shape
    tr = 216 if R % 216 == 0 else R
    assert R % tr == 0 and tr % SUBLANES == 0

    def body(g_ref, o_ref):
        s = g_ref[0]
        for k in range(1, n):
            s = s + g_ref[k]
        o_ref[...] = s

    return _pcall(body, name=name, grid=(R // tr,), in_specs=[pl.BlockSpec((n, tr, L), lambda i: (0, i, 0))],
                  out_specs=pl.BlockSpec((tr, L), lambda i: (i, 0)), out_shape=jax.ShapeDtypeStruct((R, L), F32),
                  compiler_params=_params(("parallel",)))(g)


def allgather_chips(v, name):
    R, L = v.shape

    def body(v_ref, out_ref, send_sems, recv_sems):
        x, y, c = _me()
        chip = 2 * x + y
        out_ref[chip] = v_ref[...]
        copies = []
        for p in range(1, N_CHIPS):
            px, py = _flip(x, y, p)
            send = pltpu.make_async_remote_copy(src_ref=v_ref, dst_ref=out_ref.at[chip], send_sem=send_sems.at[p - 1],
                                                recv_sem=recv_sems.at[p - 1], device_id=(px, py, c), device_id_type=MESH)
            send.start()
            recv = pltpu.make_async_remote_copy(src_ref=v_ref, dst_ref=out_ref.at[2 * px + py], send_sem=send_sems.at[p - 1],
                                                recv_sem=recv_sems.at[p - 1], device_id=(px, py, c), device_id_type=MESH)
            copies.append((send, recv))
        for send, recv in copies:
            recv.wait_recv()
        for send, recv in copies:
            send.wait_send()

    vm = pl.BlockSpec(memory_space=pltpu.VMEM)
    return _pcall(
        body, name=name, in_specs=[vm], out_specs=vm, out_shape=jax.ShapeDtypeStruct((N_CHIPS, R, L), F32),
        scratch_shapes=[pltpu.SemaphoreType.DMA((N_CHIPS - 1,)), pltpu.SemaphoreType.DMA((N_CHIPS - 1,))],
        compiler_params=_params(),
    )(v)


def _shard_window(ref, kind, shard_shape, chip, half):
    r, c = shard_shape
    hr = r // 2
    if kind == "col":
        return ref.at[pl.ds(pl.multiple_of(half * hr, hr), hr), pl.ds(pl.multiple_of(chip * c, c), c)]
    return ref.at[pl.ds(pl.multiple_of(chip * r + half * hr, hr), hr), :]


def allgather_weights(shards, kinds, name, seq_id=None):
    n = len(shards)
    fulls = []
    for s, kind in zip(shards, kinds):
        r, c = s.shape
        fulls.append(jax.ShapeDtypeStruct((r, N_CHIPS * c) if kind == "col" else (N_CHIPS * r, c), s.dtype))

    def body(*refs):
        srcs, outs = refs[:n], refs[n:2 * n]
        send_sems, recv_sems, fsend_sems, frecv_sems = refs[2 * n:]
        x, y, c = _me()
        chip = 2 * x + y
        sib = (x, y, 1 - c)
        if seq_id is not None:
            _handshake([(*_flip(x, y, p), c) for p in range(1, N_CHIPS)] + [sib])
        sends, fwds = [], []
        for i in range(n):
            shp = srcs[i].shape
            hr = shp[0] // 2
            my_half = srcs[i].at[pl.ds(pl.multiple_of(c * hr, hr), hr), :]
            for p in range(1, N_CHIPS):
                px, py = _flip(x, y, p)
                k = i * (N_CHIPS - 1) + p - 1
                cp = pltpu.make_async_remote_copy(src_ref=my_half, dst_ref=_shard_window(outs[i], kinds[i], shp, chip, c),
                                                  send_sem=send_sems.at[k], recv_sem=recv_sems.at[k],
                                                  device_id=(px, py, c), device_id_type=MESH)
                cp.start()
                sends.append(cp)
        for i in range(n):
            shp = srcs[i].shape
            for p in range(1, N_CHIPS):
                px, py = _flip(x, y, p)
                k = i * (N_CHIPS - 1) + p - 1
                landed = _shard_window(outs[i], kinds[i], shp, 2 * px + py, c)
                pltpu.make_async_remote_copy(src_ref=landed, dst_ref=landed, send_sem=send_sems.at[k], recv_sem=recv_sems.at[k],
                                             device_id=(px, py, c), device_id_type=MESH).wait_recv()
                fw = pltpu.make_async_remote_copy(src_ref=landed, dst_ref=landed, send_sem=fsend_sems.at[k],
                                                  recv_sem=frecv_sems.at[k], device_id=sib, device_id_type=MESH)
                fw.start()
                fwds.append(fw)
        for i in range(n):
            shp = srcs[i].shape
            for p in range(1, N_CHIPS):
                px, py = _flip(x, y, p)
                k = i * (N_CHIPS - 1) + p - 1
                other = _shard_window(outs[i], kinds[i], shp, 2 * px + py, 1 - c)
                pltpu.make_async_remote_copy(src_ref=other, dst_ref=other, send_sem=fsend_sems.at[k], recv_sem=frecv_sems.at[k],
                                             device_id=sib, device_id_type=MESH).wait_recv()
        for cp in sends + fwds:
            cp.wait_send()

    nk = n * (N_CHIPS - 1)
    gathered = _hbm_comm_call(
        body, name=name, n_in=n, out_shape=fulls, seq_id=seq_id,
        sem_shapes=[pltpu.SemaphoreType.DMA((nk,)), pltpu.SemaphoreType.DMA((nk,)), pltpu.SemaphoreType.DMA((nk,)),
                    pltpu.SemaphoreType.DMA((nk,))],
    )(*shards)
    return gathered


def place_local_shards(fulls, shards, kinds, name):
    n = len(shards)
    chip = jnp.reshape(2 * lax.axis_index("x") + lax.axis_index("y"), (1,)).astype(I32)

    def body(ci_ref, *refs):
        for i in range(n):
            refs[2 * n + i][...] = refs[i][...]

    in_specs = [pl.BlockSpec(s.shape, lambda i, ci: (0, 0)) for s in shards] + [pl.BlockSpec(memory_space=pl.ANY)] * n
    out_specs = [pl.BlockSpec(s.shape, (lambda i, ci: (0, ci[0])) if k == "col" else (lambda i, ci: (ci[0], 0)))
                 for s, k in zip(shards, kinds)]
    gs = pltpu.PrefetchScalarGridSpec(num_scalar_prefetch=1, grid=(1,), in_specs=in_specs, out_specs=out_specs)
    return _pcall(body, name=name, grid_spec=gs, out_shape=[jax.ShapeDtypeStruct(f.shape, f.dtype) for f in fulls],
                  input_output_aliases={1 + n + i: i for i in range(n)}, compiler_params=_params(("arbitrary",)))(chip, *shards, *fulls)


def _as_halves(g, kind, shard_shape):
    r, c = shard_shape
    if kind == "col":
        return g.reshape(2, r // 2, N_CHIPS * c)
    return g.reshape(N_CHIPS, 2, r // 2, c)


def exchange_sibling_halves(grads, kinds, shard_shapes, name, seq_id=None):
    n = len(grads)
    views = [_as_halves(g, k, s) for g, k, s in zip(grads, kinds, shard_shapes)]
    outs = []
    for k, (r, c) in zip(kinds, shard_shapes):
        outs.append(jax.ShapeDtypeStruct((r // 2, N_CHIPS * c) if k == "col" else (N_CHIPS, r // 2, c), F32))

    def body(*refs):
        srcs, dsts = refs[:n], refs[n:2 * n]
        send_sems, recv_sems = refs[2 * n:]
        x, y, c = _me()
        if seq_id is not None:
            _handshake([(x, y, 1 - c)])
        cps = []
        for i in range(n):
            src = srcs[i].at[1 - c] if kinds[i] == "col" else srcs[i].at[:, 1 - c]
            cp = pltpu.make_async_remote_copy(src_ref=src, dst_ref=dsts[i], send_sem=send_sems.at[i], recv_sem=recv_sems.at[i],
                                              device_id=(x, y, 1 - c), device_id_type=MESH)
            cp.start()
            cps.append(cp)
        for cp in cps:
            cp.wait_recv()
        for cp in cps:
            cp.wait_send()

    return _hbm_comm_call(body, name=name, n_in=n, out_shape=outs, seq_id=seq_id,
                          sem_shapes=[pltpu.SemaphoreType.DMA((n,)), pltpu.SemaphoreType.DMA((n,))])(*views)


def add_sibling_half(g, recv, kind, shard_shape, cidx, name):
    r, c = shard_shape
    hr = r // 2
    gv = _as_halves(g, kind, shard_shape)
    tr = hr if hr <= 512 else (256 if hr % 256 == 0 else hr // 2)
    assert hr % tr == 0

    def body(ci_ref, g_ref, r_ref, h_ref, hb_ref):
        s = g_ref[...] + r_ref[...]
        h_ref[...] = s
        hb_ref[...] = s.astype(BF)

    if kind == "col":
        grid = (hr // tr, N_CHIPS)
        g_spec = pl.BlockSpec((None, tr, c), lambda i, k, ci: (ci[0], i, k))
        o_spec = pl.BlockSpec((tr, c), lambda i, k, ci: (i, k))
    else:
        grid = (hr // tr, N_CHIPS)
        g_spec = pl.BlockSpec((None, None, tr, c), lambda i, k, ci: (k, ci[0], i, 0))
        o_spec = pl.BlockSpec((None, tr, c), lambda i, k, ci: (k, i, 0))
    gs = pltpu.PrefetchScalarGridSpec(num_scalar_prefetch=1, grid=grid, in_specs=[g_spec, o_spec], out_specs=[o_spec, o_spec])
    return _pcall(
        body, name=name, grid_spec=gs,
        out_shape=[jax.ShapeDtypeStruct(recv.shape, F32), jax.ShapeDtypeStruct(recv.shape, BF)],
        compiler_params=_params(("parallel", "parallel")),
    )(cidx, gv, recv)


def exchange_chip_pieces(hbs, kinds, shard_shapes, name, seq_id=None):
    n = len(hbs)
    outs = [jax.ShapeDtypeStruct((N_CHIPS - 1, r // 2, c), BF) for (r, c) in shard_shapes]

    def body(*refs):
        srcs, dsts = refs[:n], refs[n:2 * n]
        send_sems, recv_sems = refs[2 * n:]
        x, y, c = _me()
        if seq_id is not None:
            _handshake([(*_flip(x, y, p), c) for p in range(1, N_CHIPS)])
        cps = []
        for i in range(n):
            cc = shard_shapes[i][1]
            for p in range(1, N_CHIPS):
                px, py = _flip(x, y, p)
                pchip = 2 * px + py
                src = (srcs[i].at[:, pl.ds(pl.multiple_of(pchip * cc, cc), cc)] if kinds[i] == "col" else srcs[i].at[pchip])
                k = i * (N_CHIPS - 1) + p - 1
                cp = pltpu.make_async_remote_copy(src_ref=src, dst_ref=dsts[i].at[p - 1], send_sem=send_sems.at[k],
                                                  recv_sem=recv_sems.at[k], device_id=(px, py, c), device_id_type=MESH)
                cp.start()
                cps.append(cp)
        for cp in cps:
            cp.wait_recv()
        for cp in cps:
            cp.wait_send()

    nk = n * (N_CHIPS - 1)
    return _hbm_comm_call(body, name=name, n_in=n, out_shape=outs, seq_id=seq_id,
                          sem_shapes=[pltpu.SemaphoreType.DMA((nk,)), pltpu.SemaphoreType.DMA((nk,))])(*hbs)


def sum_chip_pieces(h, pieces, kind, shard_shape, chip_core, name):
    r, c = shard_shape
    hr = r // 2
    tr = hr if hr <= 512 else (256 if hr % 256 == 0 else hr // 2)
    assert hr % tr == 0
    nrb = hr // tr

    def body(ci_ref, h_ref, p_ref, q_ref):
        q_ref[...] = ((h_ref[...] + p_ref[0].astype(F32)) + p_ref[1].astype(F32)) + p_ref[2].astype(F32)

    if kind == "col":
        h_spec = pl.BlockSpec((tr, c), lambda i, ci: (i, ci[0]))
    else:
        h_spec = pl.BlockSpec((None, tr, c), lambda i, ci: (ci[0], i, 0))
    gs = pltpu.PrefetchScalarGridSpec(
        num_scalar_prefetch=1, grid=(nrb,),
        in_specs=[h_spec, pl.BlockSpec((N_CHIPS - 1, tr, c), lambda i, ci: (0, i, 0))],
        out_specs=pl.BlockSpec((tr, c), lambda i, ci: (ci[1] * nrb + i, 0)))
    return _pcall(body, name=name, grid_spec=gs, out_shape=jax.ShapeDtypeStruct((r, c), F32),
                  compiler_params=_params(("parallel",)))(chip_core, h, pieces)


def exchange_reduced_halves(qs, name):
    n = len(qs)

    def body(*refs):
        bufs = refs[n:2 * n]
        send_sems, recv_sems = refs[2 * n:]
        x, y, c = _me()
        cps = []
        for i in range(n):
            hr = bufs[i].shape[0] // 2
            mine = bufs[i].at[pl.ds(pl.multiple_of(c * hr, hr), hr), :]
            other = bufs[i].at[pl.ds(pl.multiple_of((1 - c) * hr, hr), hr), :]
            cp = pltpu.make_async_remote_copy(src_ref=mine, dst_ref=mine, send_sem=send_sems.at[i], recv_sem=recv_sems.at[i],
                                              device_id=(x, y, 1 - c), device_id_type=MESH)
            cp.start()
            cps.append((cp, pltpu.make_async_remote_copy(src_ref=other, dst_ref=other, send_sem=send_sems.at[i],
                                                         recv_sem=recv_sems.at[i], device_id=(x, y, 1 - c), device_id_type=MESH)))
        for cp, rv in cps:
            rv.wait_recv()
        for cp, rv in cps:
            cp.wait_send()

    anyspec = pl.BlockSpec(memory_space=pl.ANY)
    return _pcall(
        body, name=name, in_specs=[anyspec] * n, out_specs=[anyspec] * n,
        out_shape=[jax.ShapeDtypeStruct(q.shape, F32) for q in qs], input_output_aliases={i: i for i in range(n)},
        scratch_shapes=[pltpu.SemaphoreType.DMA((n,)), pltpu.SemaphoreType.DMA((n,))],
        compiler_params=_params(),
    )(*qs)


def _rows128(a):
    return a.reshape(-1, LANES)


def _after(xs, *deps):
    flat = []
    for d in deps:
        flat.extend(d if isinstance(d, (list, tuple)) else [d])
    return list(lax.optimization_barrier((tuple(xs), tuple(flat)))[0])


def _block_diag(w):
    H, d, _ = w.shape
    eye = jnp.eye(H, dtype=w.dtype)
    return jnp.einsum("hde,hg->hdge", w, eye).reshape(H * d, H * d)


def _diag_blocks(g4, H, d):
    nb = g4.shape[0]
    per = LANES // d
    g = g4.reshape(nb, per, d, per, d)
    return jnp.stack([g[:, j, :, j, :] for j in range(per)], axis=1).reshape(H, d, d)


def kernel(x, c, w_mod, b_mod, g_ffn1, w_ffn1_in, w_ffn1_out, g_mix, w_in, conv_w, conv_b, ln_g, ln_b, rnn_conv_w, rnn_conv_b, w_a, b_a, w_i, b_i, lru_lambda, w_out, g_ffn2, w_ffn2_in, w_ffn2_out, w_fmod, b_fmod, g_final, loss_target, m_w_mod, m_b_mod, m_g_ffn1, m_w_ffn1_in, m_w_ffn1_out, m_g_mix, m_w_in, m_conv_w, m_conv_b, m_ln_g, m_ln_b, m_rnn_conv_w, m_rnn_conv_b, m_w_a, m_b_a, m_w_i, m_b_i, m_lru_lambda, m_w_out, m_g_ffn2, m_w_ffn2_in, m_w_ffn2_out, m_w_fmod, m_b_fmod, m_g_final, v_w_mod, v_b_mod, v_g_ffn1, v_w_ffn1_in, v_w_ffn1_out, v_g_mix, v_w_in, v_conv_w, v_conv_b, v_ln_g, v_ln_b, v_rnn_conv_w, v_rnn_conv_b, v_w_a, v_b_a, v_w_i, v_b_i, v_lru_lambda, v_w_out, v_g_ffn2, v_w_ffn2_in, v_w_ffn2_out, v_w_fmod, v_b_fmod, v_g_final):
    S, D = x.shape[1], x.shape[2]
    M = conv_b.shape[1]
    H, HD = w_a.shape[1], w_a.shape[2]
    nb = M // LANES
    ix, iy, ic = lax.axis_index("x"), lax.axis_index("y"), lax.axis_index("c")
    chip = 2 * ix + iy
    dev = 2 * chip + ic
    cidx = jnp.reshape(ic, (1,)).astype(I32)
    chip_core = jnp.stack([chip, ic]).astype(I32)
    xs = x[0]
    tgt = loss_target[0]

    kinds = ["col", "row"]
    w_f1, w_mx, w_f2 = [w_ffn1_in[0], w_ffn1_out[0]], [w_in[0], w_out[0]], [w_ffn2_in[0], w_ffn2_out[0]]
    as_bf = lambda ws: [w.astype(BF) for w in ws]
    shapes_of = lambda ws: [w.shape for w in ws]
    b_f1, b_mx, b_f2 = as_bf(w_f1), as_bf(w_mx), as_bf(w_f2)
    got_f1i = allgather_weights(b_f1[:1], kinds[:1], "gather_ffn1_in", seq_id=9)
    got_f1o = allgather_weights(b_f1[1:], kinds[1:], "gather_ffn1_out", seq_id=13)
    got_mx = allgather_weights(b_mx, kinds, "gather_mix", seq_id=1)
    got_f2 = allgather_weights(b_f2, kinds, "gather_ffn2", seq_id=2)

    c_all =allgather_devices(_rows128(c), "gather_c")[0].reshape(N_DEV, D)
    mod_cols = cond_matmul(c_all, w_mod[0], "mod_proj")
    fmod_cols = cond_matmul(c_all, w_fmod, "fmod_proj")
    convw_pad = jnp.pad(conv_w[0], ((0, 32 - CONV_WIDTH), (0, 0)))
    rnnw_pad = jnp.pad(rnn_conv_w[0], ((0, SUBLANES - RNN_CONV_WIDTH), (0, 0)))
    n_mod, n_fmod = mod_cols.shape[1], fmod_cols.shape[1]
    small = jnp.concatenate([_rows128(mod_cols), _rows128(fmod_cols), convw_pad, rnnw_pad], axis=0)
    small4 = allgather_chips(small, "gather_cond")
    r0 = N_DEV * n_mod // LANES
    r1 = r0 + N_DEV * n_fmod // LANES
    mod_all = small4[:, :r0].reshape(N_CHIPS, N_DEV, n_mod)
    fmod_all = small4[:, r0:r1].reshape(N_CHIPS, N_DEV, n_fmod)
    convw4 = small4[:, r1:r1 + 32]
    rnnw4 = small4[:, r1 + 32:r1 + 32 + SUBLANES]
    mod_row = lax.dynamic_index_in_dim(mod_all, dev, axis=1, keepdims=False).reshape(1, N_CHIPS * n_mod) + b_mod
    fmod_row = lax.dynamic_index_in_dim(fmod_all, dev, axis=1, keepdims=False).reshape(1, N_CHIPS * n_fmod) + b_fmod[None, :]
    vecs = jnp.concatenate([mod_row.reshape(9, D), fmod_row.reshape(2, D), g_ffn1, g_mix, g_ffn2, g_final[None, :],
                            jnp.zeros((1, D), F32)], axis=0)
    lnv = jnp.concatenate([ln_g, ln_b, jnp.zeros((SUBLANES - 2, M), F32)], axis=0)
    bda = _block_diag(w_a[0]).astype(BF)
    bdi = _block_diag(w_i[0]).astype(BF)

    def reduce_add(gs, recv, ws, tag, kinds_=kinds):
        pairs = [add_sibling_half(g, r_, k, w.shape, cidx, f"add_sibling_{tag}{j}")
                 for j, (g, r_, k, w) in enumerate(zip(gs, recv, kinds_, ws))]
        return [p[0] for p in pairs], [p[1] for p in pairs]

    def reduce_sum(hs_, recv, ws, tag, kinds_=kinds):
        return [sum_chip_pieces(h_, p_, k, w.shape, chip_core, f"sum_chips_{tag}{j}")
                for j, (h_, p_, k, w) in enumerate(zip(hs_, recv, kinds_, ws))]

    rows1 = (R_SH1, R_SC1, R_GT1, R_G1)
    rows3 = (R_SH3, R_SC3, R_GT3, R_G3)
    (wi1,) = place_local_shards(got_f1i, b_f1[:1], kinds[:1], "place_ffn1_in")
    g1s, u1s, a1s = ffn_fwd_in(xs, vecs, wi1, rows1, "ffn1_fwd_in")
    (wo1,) = place_local_shards(_after(got_f1o, a1s), b_f1[1:], kinds[1:], "place_ffn1_out")
    x1, y1 = ffn_fwd_out(a1s, xs, vecs, wo1, rows1, "ffn1_fwd_out")
    win, wout = place_local_shards(_after(got_mx, x1), b_mx, kinds, "place_mix")
    proj = norm_matmul(x1, vecs, win, (R_SH2, R_SC2, R_G2), "mix_in_proj")
    cq = conv_fwd(proj, convw4, conv_b, "conv_fwd")
    xr, ra, ii, hh = rnn_fwd(proj, rnnw4, rnn_conv_b, bda, bdi, b_a, b_i, lru_lambda, "rnn_fwd")
    x2, ym, ycat = mix_out(cq, proj, hh, x1, vecs, lnv, wout, "mix_out")
    wi2, wo2 = place_local_shards(_after(got_f2, x2), b_f2, kinds, "place_ffn2")
    x3, g2s, u2s, y2 = ffn_fwd(x2, vecs, wi2, wo2, rows3, "ffn2_fwd")
    dx3, vgf = final_fwd_bwd(x3, tgt, vecs, "final_loss")

    Fd = wo1.shape[0]
    tk = min(2048, S)
    dx2, act2, dg2, du2, h3b, dy2b, vg3 = ffn_bwd(dx3, x2, vecs, g2s, u2s, y2, wi2, wo2, rows3, "ffn2_bwd")
    gwo2 = matmul(act2, dy2b, "tn", tm=Fd // 2, tn=D, tk=tk, name="ffn2_dwo")
    gwi2 = matmul(h3b, dg2, "tn", tm=D, tn=Fd // 2, tk=tk, name="ffn2_dwg", out_cols=2 * Fd)
    gwi2 = matmul(h3b, du2, "tn", tm=D, tn=Fd // 2, tk=tk, name="ffn2_dwu", out_cols=2 * Fd, col_off=Fd, prev=gwi2)
    recv1_f2 = exchange_sibling_halves([gwi2, gwo2], kinds, shapes_of(w_f2), "reduce1_ffn2", seq_id=3)
    dcq, dhout, duy, dymb, vgd, vgm = mix_out_bwd(dx2, ym, vecs, wout, cq, lnv, proj, hh, "mix_out_bwd")
    gwout = matmul(ycat, dymb, "tn", tm=2 * M, tn=D, tk=tk, name="mix_dwout")
    recv1_f2 = _after(recv1_f2, gwout)
    h_f2, hb_f2 = reduce_add([gwi2, gwo2], recv1_f2, w_f2, "ffn2_")
    recv2_f2 = exchange_chip_pieces(hb_f2, kinds, shapes_of(w_f2), "reduce2_ffn2", seq_id=4)
    duv, dug, dconvw4, dconvb = conv_bwd(_after([dcq], hb_f2)[0], proj, convw4, "conv_bwd")
    dux, dwa4, dwi4, drnnw4, rvec = rnn_bwd(dhout, hh, xr, ra, ii, proj, rnnw4, bda, bdi, lru_lambda, "rnn_bwd")
    dx1, h2b, dpb, vg2 = mix_in_bwd((duv, dug, dux, duy), x1, dx2, vecs, win, "mix_in_bwd")
    gwin = matmul(h2b, dpb, "tn", tm=D, tn=1024, tk=tk, name="mix_dwin")
    recv1_mx = exchange_sibling_halves([gwin, gwout], kinds, shapes_of(w_mx), "reduce1_mix", seq_id=5)
    g_f2 = exchange_reduced_halves(reduce_sum(_after(h_f2, gwin), recv2_f2, w_f2, "ffn2_"), "reduce3_ffn2")
    adam_f2 = [adam_big(w, g, m, v, "adam_" + nm) for w, g, m, v, nm in
               zip(w_f2, g_f2, [m_w_ffn2_in[0], m_w_ffn2_out[0]], [v_w_ffn2_in[0], v_w_ffn2_out[0]], ["ffn2_in", "ffn2_out"])]
    h_mx, hb_mx = reduce_add([gwin, gwout], _after(recv1_mx, adam_f2[0][0], adam_f2[1][0]), w_mx, "mix_")
    recv2_mx = exchange_chip_pieces(hb_mx, kinds, shapes_of(w_mx), "reduce2_mix", seq_id=6)
    dx0, act1, dg1, du1, h1b, dy1b, vg1 = ffn_bwd(_after([dx1], hb_mx)[0], xs, vecs, g1s, u1s, y1, wi1, wo1, rows1, "ffn1_bwd")
    dmod_row = jnp.concatenate([vg1[1:3], vg1[0:1], vg2[0:2], vgd[0:1], vg3[1:3], vg3[0:1]], axis=0)
    gains = jnp.concatenate([vg1[3:4], vg2[2:3], vg3[3:4], vgf[2:4]], axis=0)
    mvecs = jnp.concatenate([dconvb, vgm[0:2], rvec[0:4], jnp.zeros((1, M), F32)], axis=0)
    parts = [_rows128(dmod_row), _rows128(vgf[0:2]), _rows128(gains), _rows128(mvecs),
             _rows128(dconvw4), _rows128(drnnw4), _rows128(_diag_blocks(dwa4, H, HD)), _rows128(_diag_blocks(dwi4, H, HD))]
    sizes = [p.shape[0] for p in parts]
    packed = jnp.concatenate(parts, axis=0)
    gathered = allgather_devices_hbm(packed, "gather_small", seq_id=10)

    gwo1 = matmul(_after([act1], recv2_mx, packed)[0], dy1b, "tn", tm=Fd // 2, tn=D, tk=tk, name="ffn1_dwo")
    w_f1o, w_f1i = w_f1[1:], w_f1[:1]
    recv1_f1o = exchange_sibling_halves([gwo1], ["row"], shapes_of(w_f1o), "reduce1_ffn1_out", seq_id=7)
    gwi1 = matmul(h1b, dg1, "tn", tm=D, tn=Fd // 2, tk=tk, name="ffn1_dwg", out_cols=2 * Fd)
    h_f1o, hb_f1o = reduce_add([gwo1], _after(recv1_f1o, gwi1), w_f1o, "ffn1_out", ["row"])
    recv2_f1o = exchange_chip_pieces(hb_f1o, ["row"], shapes_of(w_f1o), "reduce2_ffn1_out", seq_id=11)
    gwi1 = matmul(h1b, _after([du1], hb_f1o, gathered)[0], "tn", tm=D, tn=Fd // 2, tk=tk, name="ffn1_dwu", out_cols=2 * Fd,
                  col_off=Fd, prev=gwi1)
    recv1_f1i = exchange_sibling_halves([gwi1], ["col"], shapes_of(w_f1i), "reduce1_ffn1_in", seq_id=12)
    g_mx = exchange_reduced_halves(reduce_sum(h_mx, recv2_mx, w_mx, "mix_"), "reduce3_mix")
    summed = sum_slots(gathered, "sum_small")
    offs = [0]
    for s in sizes:
        offs.append(offs[-1] + s)
    seg = lambda k: summed[offs[k]:offs[k + 1]]
    g_b_mod = seg(0).reshape(1, 9 * D)
    g_b_fmod = seg(1).reshape(1, 2 * D)
    gsum = seg(2).reshape(5, D)
    loss = (0.5 / D) * jnp.sum(gsum[4])
    msum = seg(3).reshape(SUBLANES, M)
    g_conv_w = lax.dynamic_index_in_dim(seg(4).reshape(nb, 32, LANES), chip, axis=0, keepdims=False)[:CONV_WIDTH]
    g_rnn_w = lax.dynamic_index_in_dim(seg(5).reshape(nb, SUBLANES, LANES), chip, axis=0, keepdims=False)[:RNN_CONV_WIDTH]
    g_w_a = seg(6).reshape(H, HD, HD)
    g_w_i = seg(7).reshape(H, HD, HD)
    dmod_all = gathered[:, offs[0]:offs[1]].reshape(N_DEV, 9 * D)
    dfmod_all = gathered[:, offs[1]:offs[2]].reshape(N_DEV, 2 * D)
    dmod_cols = lax.dynamic_slice_in_dim(dmod_all, chip * n_mod, n_mod, axis=1)
    dfmod_cols = lax.dynamic_slice_in_dim(dfmod_all, chip * n_fmod, n_fmod, axis=1)

    g_w_mod, d_w_mod, nm_w_mod, nv_w_mod = adam_cond(c_all, dmod_cols, w_mod[0], m_w_mod[0], v_w_mod[0], "adam_w_mod")
    g_w_fmod, d_w_fmod, nm_w_fmod, nv_w_fmod = adam_cond(c_all, dfmod_cols, w_fmod, m_w_fmod, v_w_fmod, "adam_w_fmod")

    h_f1i, hb_f1i = reduce_add([gwi1], _after(recv1_f1i, recv2_f1o, g_w_mod, g_w_fmod), w_f1i, "ffn1_in", ["col"])
    recv2_f1i = exchange_chip_pieces(hb_f1i, ["col"], shapes_of(w_f1i), "reduce2_ffn1_in", seq_id=8)
    g_f1o = exchange_reduced_halves(reduce_sum(h_f1o, recv2_f1o, w_f1o, "ffn1_out", ["row"]), "reduce3_ffn1_out")
    g_f1i = exchange_reduced_halves(reduce_sum(h_f1i, recv2_f1i, w_f1i, "ffn1_in", ["col"]), "reduce3_ffn1_in")
    g_f1 = list(g_f1i) + list(g_f1o)

    big_w = w_f1 + w_mx + w_f2
    g_big = g_f1 + list(g_mx) + list(g_f2)
    names = ["ffn1_in", "ffn1_out", "w_in", "w_out", "ffn2_in", "ffn2_out"]
    big_m = [m_w_ffn1_in[0], m_w_ffn1_out[0], m_w_in[0], m_w_out[0]]
    big_v = [v_w_ffn1_in[0], v_w_ffn1_out[0], v_w_in[0], v_w_out[0]]
    big_out = [adam_big(w, g, m, v, "adam_" + nm) for w, g, m, v, nm in zip(big_w, g_big, big_m, big_v, names)] + adam_f2

    flat2 = lambda a: a.reshape(-1, a.shape[-1])
    small_names = ["b_mod", "g_ffn1", "g_mix", "conv_w", "conv_b", "ln_g", "ln_b", "rnn_conv_w", "rnn_conv_b", "w_a", "b_a",
                   "w_i", "b_i", "lru_lambda", "g_ffn2", "b_fmod", "g_final"]
    small_w = [b_mod, g_ffn1, g_mix, conv_w, conv_b, ln_g, ln_b, rnn_conv_w, rnn_conv_b, w_a, b_a, w_i, b_i, lru_lambda,
               g_ffn2, b_fmod, g_final]
    small_m = [m_b_mod, m_g_ffn1, m_g_mix, m_conv_w, m_conv_b, m_ln_g, m_ln_b, m_rnn_conv_w, m_rnn_conv_b, m_w_a, m_b_a,
               m_w_i, m_b_i, m_lru_lambda, m_g_ffn2, m_b_fmod, m_g_final]
    small_v = [v_b_mod, v_g_ffn1, v_g_mix, v_conv_w, v_conv_b, v_ln_g, v_ln_b, v_rnn_conv_w, v_rnn_conv_b, v_w_a, v_b_a,
               v_w_i, v_b_i, v_lru_lambda, v_g_ffn2, v_b_fmod, v_g_final]
    small_g = [g_b_mod, gsum[0:1], gsum[1:2], g_conv_w, msum[0:1], msum[1:2], msum[2:3], g_rnn_w, msum[3:4], g_w_a, msum[4:5],
               g_w_i, msum[5:6], msum[6:7], gsum[2:3], g_b_fmod, gsum[3:4]]
    small_g = [g.reshape(w.shape) for g, w in zip(small_g, small_w)]
    two_d = lambda a: a.reshape(1, -1) if a.ndim == 1 else flat2(a)
    sd, sm, sv = adam_small([two_d(a) for a in small_w], [two_d(a) for a in small_g], [two_d(a) for a in small_m],
                            [two_d(a) for a in small_v], "adam_small")
    small = {}
    for k, nm in enumerate(small_names):
        shp = small_w[k].shape
        small[nm] = (small_g[k], sd[k].reshape(shp), sm[k].reshape(shp), sv[k].reshape(shp))

    big = {"w_mod": tuple(a[None] for a in (g_w_mod, d_w_mod, nm_w_mod, nv_w_mod)),
           "w_fmod": (g_w_fmod, d_w_fmod, nm_w_fmod, nv_w_fmod)}
    for nm, full, g, (d, nmm, nvv) in zip(["w_ffn1_in", "w_ffn1_out", "w_in", "w_out", "w_ffn2_in", "w_ffn2_out"],
                                         big_w, g_big, big_out):
        big[nm] = tuple(a[None] for a in (g, d, nmm, nvv))
    order = ["w_mod", "b_mod", "g_ffn1", "w_ffn1_in", "w_ffn1_out", "g_mix", "w_in", "conv_w", "conv_b", "ln_g", "ln_b",
             "rnn_conv_w", "rnn_conv_b", "w_a", "b_a", "w_i", "b_i", "lru_lambda", "w_out", "g_ffn2", "w_ffn2_in",
             "w_ffn2_out", "w_fmod", "b_fmod", "g_final"]
    table = {**small, **big}
    outs = [loss, dx0[None]]
    for kind_ in range(4):
        outs.extend(table[nm][kind_] for nm in order)
    return tuple(outs)
```

```python
import functools

import jax
import jax.numpy as jnp
from jax import lax
from jax.experimental import pallas as pl
from jax.experimental.pallas import tpu as pltpu
from jax.experimental.pallas import tpu_sc as plsc

F32 = jnp.float32
BF = jnp.bfloat16
I32 = jnp.int32
MESH = pl.DeviceIdType.MESH

EPS = 1e-6
RG_C = 8.0
MACARON_W = 0.5
CONV_WIDTH = 31
RNN_CONV_WIDTH = 4
ADAM_LR = 0.001
ADAM_B1 = 0.9
ADAM_B2 = 0.999
ADAM_EPS = 1e-08
ADAM_WD = 0.01
ADAM_STEP = 10

LANES = 128
SUBLANES = 8
VMEM_LIMIT = 62 * 1024 * 1024
N_CHIPS = 4
N_DEV = 8

R_SH1, R_SC1, R_GT1, R_SH2, R_SC2, R_GT2, R_SH3, R_SC3, R_GT3, R_FSH, R_FSC, R_G1, R_G2, R_G3, R_GF = range(15)

CONTRACT_LAST = (((1,), (1,)), ((), ()))
CONTRACT_FIRST = (((0,), (0,)), ((), ()))


def _pcall(body, **kw):
    return pl.pallas_call(body, **kw)


def _params(sem=None, vmem=VMEM_LIMIT):
    if sem is None:
        return pltpu.CompilerParams(vmem_limit_bytes=vmem)
    return pltpu.CompilerParams(dimension_semantics=sem, vmem_limit_bytes=vmem)


def _row(ref, r):
    return ref[r:r + 1, :]


def _sigmoid(x):
    return 1.0 / (1.0 + jnp.exp(-x))


def _colsum(x):
    return jnp.sum(x, axis=0, keepdims=True)


def _rowmean(x):
    return jnp.mean(x, axis=-1, keepdims=True)


def matmul(a, b, mode, *, tm, tn, tk, name, out_dtype=F32, out_cols=None, col_off=0, prev=None):
    if mode == "nn":
        (M, K), (K2, N) = a.shape, b.shape
    elif mode == "nt":
        (M, K), (N, K2) = a.shape, b.shape
    else:
        (K, M), (K2, N) = a.shape, b.shape
    assert K == K2 and M % tm == 0 and N % tn == 0 and K % tk == 0 and col_off % tn == 0
    nk = K // tk
    out_cols = N if out_cols is None else out_cols
    off = col_off // tn

    def body(*refs):
        if prev is None:
            a_ref, b_ref, o_ref, acc = refs
        else:
            a_ref, b_ref, _, o_ref, acc = refs
        k = pl.program_id(2)
        av = a_ref[...].astype(BF)
        bv = b_ref[...].astype(BF)
        if mode == "nn":
            part = jnp.dot(av, bv, preferred_element_type=F32)
        elif mode == "nt":
            part = lax.dot_general(av, bv, CONTRACT_LAST, preferred_element_type=F32)
        else:
            part = lax.dot_general(av, bv, CONTRACT_FIRST, preferred_element_type=F32)
        if nk == 1:
            o_ref[...] = part.astype(out_dtype)
            return

        @pl.when(k == 0)
        def _():
            acc[...] = part

        @pl.when((k > 0) & (k < nk - 1))
        def _():
            acc[...] += part

        @pl.when(k == nk - 1)
        def _():
            o_ref[...] = (acc[...] + part).astype(out_dtype)

    if mode == "nn":
        a_spec = pl.BlockSpec((tm, tk), lambda m, n, k: (m, k))
        b_spec = pl.BlockSpec((tk, tn), lambda m, n, k: (k, n))
    elif mode == "nt":
        a_spec = pl.BlockSpec((tm, tk), lambda m, n, k: (m, k))
        b_spec = pl.BlockSpec((tn, tk), lambda m, n, k: (n, k))
    else:
        a_spec = pl.BlockSpec((tk, tm), lambda m, n, k: (k, m))
        b_spec = pl.BlockSpec((tk, tn), lambda m, n, k: (k, n))
    in_specs = [a_spec, b_spec]
    args = [a, b]
    aliases = {}
    if prev is not None:
        in_specs.append(pl.BlockSpec(memory_space=pl.ANY))
        args.append(prev)
        aliases = {2: 0}
    return _pcall(
        body, name=name, grid=(M // tm, N // tn, nk), in_specs=in_specs,
        out_specs=pl.BlockSpec((tm, tn), lambda m, n, k: (m, n + off)),
        out_shape=jax.ShapeDtypeStruct((M, out_cols), out_dtype),
        scratch_shapes=[pltpu.VMEM((tm, tn), F32)], input_output_aliases=aliases,
        compiler_params=_params(("parallel", "parallel", "arbitrary")),
    )(*args)


def cond_matmul(c_all, w, name):
    B, K = c_all.shape
    N = w.shape[1]
    tn = 256
    assert N % tn == 0

    def body(c_ref, w_ref, o_ref):
        cv = c_ref[...]
        ca = cv * _sigmoid(cv)
        o_ref[...] = jnp.dot(ca.astype(BF), w_ref[...].astype(BF), preferred_element_type=F32)

    return _pcall(
        body, name=name, grid=(N // tn,),
        in_specs=[pl.BlockSpec((B, K), lambda n: (0, 0)), pl.BlockSpec((K, tn), lambda n: (0, n))],
        out_specs=pl.BlockSpec((B, tn), lambda n: (0, n)),
        out_shape=jax.ShapeDtypeStruct((B, N), F32), compiler_params=_params(("parallel",)),
    )(c_all, w)


FFN_FWD_TS = 512
FFN_BWD_TS = 256


def _resident(shape, index_map):
    return pl.BlockSpec(shape, index_map, pipeline_mode=pl.Buffered(1))


def _final_norm_loss_grad(xv, t, v_ref, vg_ref):
    D = xv.shape[-1]
    r = lax.rsqrt(_rowmean(xv * xv) + EPS)
    n = xv * r
    g = _row(v_ref, R_GF)
    sc1 = 1.0 + _row(v_ref, R_FSC)
    gsc = g * sc1
    e = n * gsc + _row(v_ref, R_FSH) - t
    vg_ref[3:4, :] += _colsum(e * e)
    dout = e * (1.0 / D)
    dn_ = dout * n
    vg_ref[0:1, :] += _colsum(dout)
    vg_ref[1:2, :] += _colsum(dn_) * g
    vg_ref[2:3, :] += _colsum(dn_) * sc1
    dn = dout * gsc
    return r * (dn - n * _rowmean(dn * n))


def ffn_fwd(x, vecs, wi, wo, rows, name, final_tgt=None):
    r_sh, r_sc, r_gt, r_g = rows
    S, D = x.shape
    Fd = wo.shape[0]
    ts = min(FFN_FWD_TS, S)
    with_final = final_tgt is not None

    def body(*refs):
        if with_final:
            x_ref, v_ref, wg_ref, wu_ref, wo_ref, t_ref, xo_ref, g_ref, u_ref, y_ref, vg_ref = refs
        else:
            x_ref, v_ref, wg_ref, wu_ref, wo_ref, xo_ref, g_ref, u_ref, y_ref = refs
        xv = x_ref[...]
        r = lax.rsqrt(_rowmean(xv * xv) + EPS)
        gs = _row(v_ref, r_g) * (1.0 + _row(v_ref, r_sc))
        hb = (xv * r * gs + _row(v_ref, r_sh)).astype(BF)
        G = jnp.dot(hb, wg_ref[...], preferred_element_type=F32)
        U = jnp.dot(hb, wu_ref[...], preferred_element_type=F32)
        g_ref[...] = G.astype(BF)
        u_ref[...] = U.astype(BF)
        act = (G * _sigmoid(G) * U).astype(BF)
        Y = jnp.dot(act, wo_ref[...], preferred_element_type=F32)
        y_ref[...] = Y
        xo = xv + (MACARON_W * _row(v_ref, r_gt)) * Y
        if with_final:
            @pl.when(pl.program_id(0) == 0)
            def _():
                vg_ref[...] = jnp.zeros_like(vg_ref)

            xo_ref[...] = _final_norm_loss_grad(xo, t_ref[...], v_ref, vg_ref)
        else:
            xo_ref[...] = xo

    tok = pl.BlockSpec((ts, D), lambda i: (i, 0))
    hid = pl.BlockSpec((ts, Fd), lambda i: (i, 0))
    in_specs = [tok, pl.BlockSpec(vecs.shape, lambda i: (0, 0)), _resident((D, Fd), lambda i: (0, 0)),
                _resident((D, Fd), lambda i: (0, 1)), _resident((Fd, D), lambda i: (0, 0))]
    out_specs = [tok, hid, hid, tok]
    out_shape = [jax.ShapeDtypeStruct((S, D), F32), jax.ShapeDtypeStruct((S, Fd), BF),
                 jax.ShapeDtypeStruct((S, Fd), BF), jax.ShapeDtypeStruct((S, D), F32)]
    args = [x, vecs, wi, wi, wo]
    if with_final:
        in_specs.append(tok)
        args.append(final_tgt)
        out_specs.append(pl.BlockSpec((SUBLANES, D), lambda i: (0, 0)))
        out_shape.append(jax.ShapeDtypeStruct((SUBLANES, D), F32))
    return _pcall(body, name=name, grid=(S // ts,), in_specs=in_specs, out_specs=out_specs, out_shape=out_shape,
                  compiler_params=_params(("arbitrary",)))(*args)


def ffn_fwd_in(x, vecs, wi, rows, name):
    r_sh, r_sc, r_gt, r_g = rows
    S, D = x.shape
    Fd = wi.shape[1] // 2
    ts = min(FFN_FWD_TS, S)

    def body(x_ref, v_ref, wg_ref, wu_ref, g_ref, u_ref, a_ref):
        xv = x_ref[...]
        r = lax.rsqrt(_rowmean(xv * xv) + EPS)
        gs = _row(v_ref, r_g) * (1.0 + _row(v_ref, r_sc))
        hb = (xv * r * gs + _row(v_ref, r_sh)).astype(BF)
        G = jnp.dot(hb, wg_ref[...], preferred_element_type=F32)
        U = jnp.dot(hb, wu_ref[...], preferred_element_type=F32)
        g_ref[...] = G.astype(BF)
        u_ref[...] = U.astype(BF)
        a_ref[...] = (G * _sigmoid(G) * U).astype(BF)

    hid = pl.BlockSpec((ts, Fd), lambda i: (i, 0))
    return _pcall(
        body, name=name, grid=(S // ts,),
        in_specs=[pl.BlockSpec((ts, D), lambda i: (i, 0)), pl.BlockSpec(vecs.shape, lambda i: (0, 0)),
                  _resident((D, Fd), lambda i: (0, 0)), _resident((D, Fd), lambda i: (0, 1))],
        out_specs=[hid, hid, hid], out_shape=[jax.ShapeDtypeStruct((S, Fd), BF)] * 3,
        compiler_params=_params(("arbitrary",)),
    )(x, vecs, wi, wi)


def ffn_fwd_out(act, x, vecs, wo, rows, name):
    r_sh, r_sc, r_gt, r_g = rows
    S, D = x.shape
    Fd = wo.shape[0]
    ts = min(FFN_FWD_TS, S)

    def body(a_ref, x_ref, v_ref, wo_ref, xo_ref, y_ref):
        Y = jnp.dot(a_ref[...], wo_ref[...], preferred_element_type=F32)
        y_ref[...] = Y
        xo_ref[...] = x_ref[...] + (MACARON_W * _row(v_ref, r_gt)) * Y

    tok = pl.BlockSpec((ts, D), lambda i: (i, 0))
    return _pcall(
        body, name=name, grid=(S // ts,),
        in_specs=[pl.BlockSpec((ts, Fd), lambda i: (i, 0)), tok, pl.BlockSpec(vecs.shape, lambda i: (0, 0)),
                  _resident((Fd, D), lambda i: (0, 0))],
        out_specs=[tok, tok], out_shape=[jax.ShapeDtypeStruct((S, D), F32)] * 2,
        compiler_params=_params(("arbitrary",)),
    )(act, x, vecs, wo)


def ffn_bwd(dxo, x, vecs, gs_, us_, y, wi, wo, rows, name):
    r_sh, r_sc, r_gt, r_g = rows
    S, D = x.shape
    Fd = wo.shape[0]
    ts = min(FFN_BWD_TS, S)

    def body(dxo_ref, x_ref, v_ref, g_ref, u_ref, y_ref, wg_ref, wu_ref, wo_ref,
             dx_ref, act_ref, dg_ref, du_ref, hb_ref, dyb_ref, vg_ref):
        @pl.when(pl.program_id(0) == 0)
        def _():
            vg_ref[...] = jnp.zeros_like(vg_ref)

        dxo_v = dxo_ref[...]
        dyb = ((MACARON_W * _row(v_ref, r_gt)) * dxo_v).astype(BF)
        dyb_ref[...] = dyb
        vg_ref[0:1, :] += MACARON_W * _colsum(dxo_v * y_ref[...])
        dA = lax.dot_general(dyb, wo_ref[...], CONTRACT_LAST, preferred_element_type=F32)
        G = g_ref[...].astype(F32)
        U = u_ref[...].astype(F32)
        sg = _sigmoid(G)
        sl = G * sg
        dU = (dA * sl).astype(BF)
        dG = (dA * U * (sg * (1.0 + G * (1.0 - sg)))).astype(BF)
        act_ref[...] = (sl * U).astype(BF)
        dg_ref[...] = dG
        du_ref[...] = dU
        dh = (lax.dot_general(dG, wg_ref[...], CONTRACT_LAST, preferred_element_type=F32)
              + lax.dot_general(dU, wu_ref[...], CONTRACT_LAST, preferred_element_type=F32))
        xv = x_ref[...]
        r = lax.rsqrt(_rowmean(xv * xv) + EPS)
        n = xv * r
        g = _row(v_ref, r_g)
        sc1 = 1.0 + _row(v_ref, r_sc)
        gsc = g * sc1
        hb_ref[...] = (n * gsc + _row(v_ref, r_sh)).astype(BF)
        dhn = dh * n
        vg_ref[1:2, :] += _colsum(dh)
        vg_ref[2:3, :] += _colsum(dhn) * g
        vg_ref[3:4, :] += _colsum(dhn) * sc1
        dn = dh * gsc
        dx_ref[...] = dxo_v + r * (dn - n * _rowmean(dn * n))

    tok = pl.BlockSpec((ts, D), lambda i: (i, 0))
    hid = pl.BlockSpec((ts, Fd), lambda i: (i, 0))
    return _pcall(
        body, name=name, grid=(S // ts,),
        in_specs=[tok, tok, pl.BlockSpec(vecs.shape, lambda i: (0, 0)), hid, hid, tok, _resident((D, Fd), lambda i: (0, 0)),
                  _resident((D, Fd), lambda i: (0, 1)), _resident((Fd, D), lambda i: (0, 0))],
        out_specs=[tok, hid, hid, hid, tok, tok, pl.BlockSpec((SUBLANES, D), lambda i: (0, 0))],
        out_shape=[jax.ShapeDtypeStruct((S, D), F32), jax.ShapeDtypeStruct((S, Fd), BF),
                   jax.ShapeDtypeStruct((S, Fd), BF), jax.ShapeDtypeStruct((S, Fd), BF),
                   jax.ShapeDtypeStruct((S, D), BF), jax.ShapeDtypeStruct((S, D), BF),
                   jax.ShapeDtypeStruct((SUBLANES, D), F32)],
        compiler_params=_params(("arbitrary",)),
    )(dxo, x, vecs, gs_, us_, y, wi, wi, wo)


def norm_matmul(x, vecs, w, rows, name):
    r_sh, r_sc, r_g = rows
    S, D = x.shape
    N = w.shape[1]
    ts = min(512, S)

    def body(x_ref, v_ref, w_ref, o_ref):
        xv = x_ref[...]
        r = lax.rsqrt(_rowmean(xv * xv) + EPS)
        gs = _row(v_ref, r_g) * (1.0 + _row(v_ref, r_sc))
        hb = (xv * r * gs + _row(v_ref, r_sh)).astype(BF)
        o_ref[...] = jnp.dot(hb, w_ref[...], preferred_element_type=F32)

    return _pcall(
        body, name=name, grid=(S // ts,),
        in_specs=[pl.BlockSpec((ts, D), lambda i: (i, 0)), pl.BlockSpec(vecs.shape, lambda i: (0, 0)),
                  _resident((D, N), lambda i: (0, 0))],
        out_specs=pl.BlockSpec((ts, N), lambda i: (i, 0)),
        out_shape=jax.ShapeDtypeStruct((S, N), F32),
        compiler_params=_params(("arbitrary",)),
    )(x, vecs, w)


SEQ_TT = 256
SCAN_SEGMENTS = 4
CONV_PAD = 32


def conv_fwd(proj, convw4, conv_b, name):
    S = proj.shape[0]
    M = conv_b.shape[1]
    nb = M // LANES
    tt = min(SEQ_TT, S)

    def body(uv_ref, ug_ref, w_ref, b_ref, cq_ref, qp):
        qp[0:CONV_PAD, :] = jnp.zeros((CONV_PAD, LANES), F32)

        def step(t, carry):
            base = pl.multiple_of(t * tt, tt)
            qp[pl.ds(base + CONV_PAD, tt), :] = uv_ref[pl.ds(base, tt), :] * _sigmoid(ug_ref[pl.ds(base, tt), :])
            acc = jnp.broadcast_to(b_ref[...], (tt, LANES))
            for k in range(CONV_WIDTH):
                acc = acc + w_ref[k:k + 1, :] * qp[pl.ds(base + (CONV_PAD - CONV_WIDTH + 1) + k, tt), :]
            cq_ref[pl.ds(base, tt), :] = acc
            return carry

        lax.fori_loop(0, S // tt, step, 0)

    return _pcall(
        body, name=name, grid=(nb,),
        in_specs=[pl.BlockSpec((S, LANES), lambda c: (0, c)), pl.BlockSpec((S, LANES), lambda c: (0, c + nb)),
                  pl.BlockSpec((None, 32, LANES), lambda c: (c, 0, 0)), pl.BlockSpec((1, LANES), lambda c: (0, c))],
        out_specs=pl.BlockSpec((S, LANES), lambda c: (0, c)),
        out_shape=jax.ShapeDtypeStruct((S, M), F32),
        scratch_shapes=[pltpu.VMEM((S + CONV_PAD, LANES), F32)],
        compiler_params=_params(("arbitrary",)),
    )(proj, proj, convw4, conv_b)


def conv_bwd(dcq, proj, convw4, name):
    S, M = dcq.shape
    nb = M // LANES
    tt = min(SEQ_TT, S)
    off = CONV_PAD - CONV_WIDTH + 1

    def body(dcq_ref, uv_ref, ug_ref, w_ref, duv_ref, dug_ref, dw_ref, db_ref, qp, dp, dw8, db8):
        qp[0:CONV_PAD, :] = jnp.zeros((CONV_PAD, LANES), F32)
        dp[S:S + CONV_PAD, :] = jnp.zeros((CONV_PAD, LANES), F32)
        dw8[...] = jnp.zeros_like(dw8)
        db8[...] = jnp.zeros_like(db8)

        def fill(t, carry):
            base = pl.multiple_of(t * tt, tt)
            qp[pl.ds(base + CONV_PAD, tt), :] = uv_ref[pl.ds(base, tt), :] * _sigmoid(ug_ref[pl.ds(base, tt), :])
            dp[pl.ds(base, tt), :] = dcq_ref[pl.ds(base, tt), :]
            return carry

        lax.fori_loop(0, S // tt, fill, 0)

        def step(t, carry):
            base = pl.multiple_of(t * tt, tt)
            d_t = dcq_ref[pl.ds(base, tt), :]
            db8[...] += d_t.reshape(tt // SUBLANES, SUBLANES, LANES).sum(axis=0)
            dq = jnp.zeros((tt, LANES), F32)
            for k in range(CONV_WIDTH):
                prod = d_t * qp[pl.ds(base + off + k, tt), :]
                dw8[k] += prod.reshape(tt // SUBLANES, SUBLANES, LANES).sum(axis=0)
                dq = dq + w_ref[k:k + 1, :] * dp[pl.ds(base + (CONV_WIDTH - 1) - k, tt), :]
            uv = uv_ref[pl.ds(base, tt), :]
            sg = _sigmoid(ug_ref[pl.ds(base, tt), :])
            duv_ref[pl.ds(base, tt), :] = (dq * sg).astype(BF)
            dug_ref[pl.ds(base, tt), :] = (dq * uv * sg * (1.0 - sg)).astype(BF)
            return carry

        lax.fori_loop(0, S // tt, step, 0)
        dw_ref[...] = jnp.zeros_like(dw_ref)
        for k in range(CONV_WIDTH):
            dw_ref[k:k + 1, :] = _colsum(dw8[k])
        db_ref[...] = _colsum(db8[...])

    col = lambda o: pl.BlockSpec((S, LANES), lambda c: (0, c + o))
    return _pcall(
        body, name=name, grid=(nb,),
        in_specs=[col(0), col(0), col(nb), pl.BlockSpec((None, 32, LANES), lambda c: (c, 0, 0))],
        out_specs=[col(0), col(0), pl.BlockSpec((None, 32, LANES), lambda c: (c, 0, 0)),
                   pl.BlockSpec((1, LANES), lambda c: (0, c))],
        out_shape=[jax.ShapeDtypeStruct((S, M), BF), jax.ShapeDtypeStruct((S, M), BF),
                   jax.ShapeDtypeStruct((nb, 32, LANES), F32), jax.ShapeDtypeStruct((1, M), F32)],
        scratch_shapes=[pltpu.VMEM((S + CONV_PAD, LANES), F32), pltpu.VMEM((S + CONV_PAD, LANES), F32),
                        pltpu.VMEM((32, SUBLANES, LANES), F32), pltpu.VMEM((SUBLANES, LANES), F32)],
        compiler_params=_params(("arbitrary",)),
    )(dcq, proj, proj, convw4)


def _log_sigmoid(x):
    return jnp.minimum(x, 0.0) - jnp.log(1.0 + jnp.exp(-jnp.abs(x)))


def _rg_gate_terms(ra, ls):
    la = RG_C * ra * ls
    a = jnp.exp(la)
    th = jnp.tanh(la)
    mult = jnp.sqrt(-2.0 * th / (1.0 - th))
    return a, mult


def rnn_fwd(proj, rnnw4, rnn_b, bda, bdi, b_a, b_i, lam, name):
    S = proj.shape[0]
    M = rnn_b.shape[1]
    nb = M // LANES
    tt = min(SEQ_TT, S)
    KW = RNN_CONV_WIDTH
    nseg = SCAN_SEGMENTS if S % (SCAN_SEGMENTS * tt) == 0 else 1

    def body(ux_ref, w_ref, rb_ref, bda_ref, bdi_ref, ba_ref, bi_ref, lam_ref,
             xr_ref, ra_ref, ii_ref, h_ref, uxp, a_sc, b_sc):
        uxp[0:SUBLANES, :] = jnp.zeros((SUBLANES, LANES), F32)
        ls = _log_sigmoid(lam_ref[...])

        def step(t, carry):
            base = pl.multiple_of(t * tt, tt)
            uxp[pl.ds(base + SUBLANES, tt), :] = ux_ref[pl.ds(base, tt), :]
            xr = jnp.broadcast_to(rb_ref[...], (tt, LANES))
            for k in range(KW):
                xr = xr + w_ref[k:k + 1, :] * uxp[pl.ds(base + (SUBLANES - KW + 1) + k, tt), :]
            xb = xr.astype(BF)
            ra = _sigmoid(jnp.dot(xb, bda_ref[...], preferred_element_type=F32) + ba_ref[...])
            ii = _sigmoid(jnp.dot(xb, bdi_ref[...], preferred_element_type=F32) + bi_ref[...])
            a, mult = _rg_gate_terms(ra, ls)
            xr_ref[pl.ds(base, tt), :] = xr
            ra_ref[pl.ds(base, tt), :] = ra
            ii_ref[pl.ds(base, tt), :] = ii
            a_sc[pl.ds(base, tt), :] = a
            b_sc[pl.ds(base, tt), :] = mult * (ii * xr)
            return carry

        lax.fori_loop(0, S // tt, step, 0)

        rows = lax.broadcasted_iota(I32, (SUBLANES, LANES), 0)
        seg = S // nseg
        last = lambda v: jnp.broadcast_to(v[SUBLANES - 1:SUBLANES, :], (SUBLANES, LANES))

        def scan(t, carry):
            hs, ps = carry
            new_h, new_p = [], []
            for s in range(nseg):
                base = pl.multiple_of(s * seg + t * SUBLANES, SUBLANES)
                A = a_sc[pl.ds(base, SUBLANES), :]
                B = b_sc[pl.ds(base, SUBLANES), :]
                for d in (1, 2, 4):
                    As = jnp.where(rows >= d, pltpu.roll(A, d, axis=0), 1.0)
                    Bs = jnp.where(rows >= d, pltpu.roll(B, d, axis=0), 0.0)
                    B = A * Bs + B
                    A = A * As
                hh = B + A * hs[s]
                h_ref[pl.ds(base, SUBLANES), :] = hh
                pp = A * ps[s]
                if s > 0:
                    a_sc[pl.ds(base, SUBLANES), :] = pp
                new_h.append(last(hh))
                new_p.append(last(pp))
            return tuple(new_h), tuple(new_p)

        zero8 = jnp.zeros((SUBLANES, LANES), F32)
        one8 = jnp.ones((SUBLANES, LANES), F32)
        hs, ps = lax.fori_loop(0, seg // SUBLANES, scan, ((zero8,) * nseg, (one8,) * nseg))
        carry_in = hs[0]
        for s in range(1, nseg):
            c_row = carry_in[0:1, :]

            def fix(t, c, s=s, c_row=c_row):
                base = pl.multiple_of(s * seg + t * tt, tt)
                h_ref[pl.ds(base, tt), :] = h_ref[pl.ds(base, tt), :] + a_sc[pl.ds(base, tt), :] * c_row
                return c

            lax.fori_loop(0, seg // tt, fix, 0)
            carry_in = hs[s] + ps[s] * carry_in

    col = lambda o: pl.BlockSpec((S, LANES), lambda c: (0, c + o))
    vec = pl.BlockSpec((1, LANES), lambda c: (0, c))
    diag = pl.BlockSpec((LANES, LANES), lambda c: (c, c))
    return _pcall(
        body, name=name, grid=(nb,),
        in_specs=[col(2 * nb), pl.BlockSpec((None, SUBLANES, LANES), lambda c: (c, 0, 0)), vec, diag, diag, vec, vec, vec],
        out_specs=[col(0)] * 4,
        out_shape=[jax.ShapeDtypeStruct((S, M), F32)] * 4,
        scratch_shapes=[pltpu.VMEM((S + SUBLANES, LANES), F32), pltpu.VMEM((S, LANES), F32), pltpu.VMEM((S, LANES), F32)],
        compiler_params=_params(("arbitrary",)),
    )(proj, rnnw4, rnn_b, bda, bdi, b_a, b_i, lam)


def rnn_bwd(dhout, h, xr, ra, ii, proj, rnnw4, bda, bdi, lam, name):
    S, M = h.shape
    nb = M // LANES
    tt = min(SEQ_TT, S)
    KW = RNN_CONV_WIDTH
    SL = SUBLANES
    nseg = SCAN_SEGMENTS if S % (SCAN_SEGMENTS * tt) == 0 else 1

    def body(dh_ref, h_ref, xr_ref, ra_ref, ii_ref, ux_ref, w_ref, bda_ref, bdi_ref, lam_ref,
             dux_ref, dwa_ref, dwi_ref, drw_ref, vec_ref,
             a_sc, hp, g_sc, dpa_sc, dpi_sc, dxp, uxp, acc8, drw8, p_sc):
        zero8 = jnp.zeros((SL, LANES), F32)
        a_sc[S:S + SL, :] = zero8
        hp[0:SL, :] = zero8
        dxp[S:S + SL, :] = zero8
        uxp[0:SL, :] = zero8
        acc8[...] = jnp.zeros_like(acc8)
        drw8[...] = jnp.zeros_like(drw8)
        lamv = lam_ref[...]
        ls = _log_sigmoid(lamv)

        def fill(t, carry):
            base = pl.multiple_of(t * tt, tt)
            a_sc[pl.ds(base, tt), :] = jnp.exp(RG_C * ra_ref[pl.ds(base, tt), :] * ls)
            hp[pl.ds(base + SL, tt), :] = h_ref[pl.ds(base, tt), :]
            uxp[pl.ds(base + SL, tt), :] = ux_ref[pl.ds(base, tt), :]
            return carry

        lax.fori_loop(0, S // tt, fill, 0)

        rows = lax.broadcasted_iota(I32, (SL, LANES), 0)
        seg = S // nseg
        nt8 = seg // SL
        first = lambda v: jnp.broadcast_to(v[0:1, :], (SL, LANES))

        def rscan(t, carry):
            gs, ps = carry
            new_g, new_p = [], []
            for s in range(nseg):
                base = pl.multiple_of(s * seg + (nt8 - 1 - t) * SL, SL)
                A = a_sc[pl.ds(base + 1, SL), :]
                B = dh_ref[pl.ds(base, SL), :]
                for d in (1, 2, 4):
                    As = jnp.where(rows < SL - d, pltpu.roll(A, SL - d, axis=0), 1.0)
                    Bs = jnp.where(rows < SL - d, pltpu.roll(B, SL - d, axis=0), 0.0)
                    B = A * Bs + B
                    A = A * As
                g = B + A * gs[s]
                g_sc[pl.ds(base, SL), :] = g
                pp = A * ps[s]
                if s < nseg - 1:
                    p_sc[pl.ds(base, SL), :] = pp
                new_g.append(first(g))
                new_p.append(first(pp))
            return tuple(new_g), tuple(new_p)

        one8 = jnp.ones((SL, LANES), F32)
        gs, ps = lax.fori_loop(0, nt8, rscan, ((zero8,) * nseg, (one8,) * nseg))
        carry_in = gs[nseg - 1]
        for s in range(nseg - 2, -1, -1):
            c_row = carry_in[0:1, :]

            def fix(t, c, s=s, c_row=c_row):
                base = pl.multiple_of(s * seg + t * tt, tt)
                g_sc[pl.ds(base, tt), :] = g_sc[pl.ds(base, tt), :] + p_sc[pl.ds(base, tt), :] * c_row
                return c

            lax.fori_loop(0, seg // tt, fix, 0)
            carry_in = gs[s] + ps[s] * carry_in

        def red8(v):
            return v.reshape(tt // SL, SL, LANES).sum(axis=0)

        def step(t, carry):
            base = pl.multiple_of(t * tt, tt)
            g = g_sc[pl.ds(base, tt), :]
            hprev = hp[pl.ds(base + SL - 1, tt), :]
            xr_t = xr_ref[pl.ds(base, tt), :]
            ra_t = ra_ref[pl.ds(base, tt), :]
            ii_t = ii_ref[pl.ds(base, tt), :]
            a, mult = _rg_gate_terms(ra_t, ls)
            gx = g * xr_t
            dmult = gx * ii_t
            dii = gx * mult
            dxr = g * (mult * ii_t)
            dla = g * hprev * a - dmult * (a * a) / mult
            acc8[3] += red8(dla * ra_t)
            dpa = dla * (RG_C * ls) * ra_t * (1.0 - ra_t)
            dpi = dii * ii_t * (1.0 - ii_t)
            dpab = dpa.astype(BF)
            dpib = dpi.astype(BF)
            dxr = dxr + (lax.dot_general(dpab, bda_ref[...], CONTRACT_LAST, preferred_element_type=F32)
                         + lax.dot_general(dpib, bdi_ref[...], CONTRACT_LAST, preferred_element_type=F32))
            dpa_sc[pl.ds(base, tt), :] = dpab
            dpi_sc[pl.ds(base, tt), :] = dpib
            dxp[pl.ds(base, tt), :] = dxr
            acc8[0] += red8(dxr)
            acc8[1] += red8(dpa)
            acc8[2] += red8(dpi)
            return carry

        lax.fori_loop(0, S // tt, step, 0)

        def convb(t, carry):
            base = pl.multiple_of(t * tt, tt)
            d_t = dxp[pl.ds(base, tt), :]
            dux = jnp.zeros((tt, LANES), F32)
            for k in range(KW):
                drw8[k] += red8(d_t * uxp[pl.ds(base + (SL - KW + 1) + k, tt), :])
                dux = dux + w_ref[k:k + 1, :] * dxp[pl.ds(base + (KW - 1) - k, tt), :]
            dux_ref[pl.ds(base, tt), :] = dux.astype(BF)
            return carry

        lax.fori_loop(0, S // tt, convb, 0)

        xb = xr_ref[...].astype(BF)
        dwa_ref[...] = lax.dot_general(xb, dpa_sc[...], CONTRACT_FIRST, preferred_element_type=F32)
        dwi_ref[...] = lax.dot_general(xb, dpi_sc[...], CONTRACT_FIRST, preferred_element_type=F32)
        drw_ref[...] = jnp.zeros_like(drw_ref)
        vec_ref[...] = jnp.zeros_like(vec_ref)
        for k in range(KW):
            drw_ref[k:k + 1, :] = _colsum(drw8[k])
        for k in range(3):
            vec_ref[k:k + 1, :] = _colsum(acc8[k])
        vec_ref[3:4, :] = _colsum(acc8[3]) * (RG_C * _sigmoid(-lamv))

    col = lambda o: pl.BlockSpec((S, LANES), lambda c: (0, c + o))
    vec = pl.BlockSpec((1, LANES), lambda c: (0, c))
    diag = pl.BlockSpec((LANES, LANES), lambda c: (c, c))
    blk3 = lambda r: pl.BlockSpec((None, r, LANES), lambda c: (c, 0, 0))
    return _pcall(
        body, name=name, grid=(nb,),
        in_specs=[col(0), col(0), col(0), col(0), col(0), col(2 * nb), blk3(SL), diag, diag, vec],
        out_specs=[col(0), blk3(LANES), blk3(LANES), blk3(SL), pl.BlockSpec((SL, LANES), lambda c: (0, c))],
        out_shape=[jax.ShapeDtypeStruct((S, M), BF), jax.ShapeDtypeStruct((nb, LANES, LANES), F32),
                   jax.ShapeDtypeStruct((nb, LANES, LANES), F32), jax.ShapeDtypeStruct((nb, SL, LANES), F32),
                   jax.ShapeDtypeStruct((SL, M), F32)],
        scratch_shapes=[pltpu.VMEM((S + SL, LANES), F32), pltpu.VMEM((S + SL, LANES), F32), pltpu.VMEM((S, LANES), F32),
                        pltpu.VMEM((S, LANES), BF), pltpu.VMEM((S, LANES), BF), pltpu.VMEM((S + SL, LANES), F32),
                        pltpu.VMEM((S + SL, LANES), F32), pltpu.VMEM((SL, SL, LANES), F32), pltpu.VMEM((SL, SL, LANES), F32),
                        pltpu.VMEM((S, LANES), F32)],
        compiler_params=_params(("arbitrary",)),
    )(dhout, h, xr, ra, ii, proj, rnnw4, bda, bdi, lam)


GELU_K = 0.7978845608028654
GELU_C = 0.044715


def _layernorm_parts(cq):
    mu = _rowmean(cq)
    d = cq - mu
    rstd = lax.rsqrt(_rowmean(d * d) + EPS)
    return d * rstd, rstd


def mix_out(cq, proj, h, x, vecs, lnv, wout, name):
    S, D = x.shape
    M = cq.shape[1]
    ts = min(512, S)

    def body(cq_ref, uy_ref, h_ref, x_ref, v_ref, ln_ref, w_ref, xo_ref, ym_ref, yc_ref):
        z, _ = _layernorm_parts(cq_ref[...])
        l = z * _row(ln_ref, 0) + _row(ln_ref, 1)
        yc_ref[:, 0:M] = (l * _sigmoid(l)).astype(BF)
        uy = uy_ref[...]
        gelu = 0.5 * uy * (1.0 + jnp.tanh(GELU_K * (uy + GELU_C * uy * uy * uy)))
        yc_ref[:, M:2 * M] = (gelu * h_ref[...]).astype(BF)
        ym = jnp.dot(yc_ref[...], w_ref[...], preferred_element_type=F32)
        ym_ref[...] = ym
        xo_ref[...] = x_ref[...] + _row(v_ref, R_GT2) * ym

    tok = pl.BlockSpec((ts, D), lambda i: (i, 0))
    mtok = lambda o: pl.BlockSpec((ts, M), lambda i: (i, o))
    return _pcall(
        body, name=name, grid=(S // ts,),
        in_specs=[mtok(0), mtok(3), mtok(0), tok, pl.BlockSpec(vecs.shape, lambda i: (0, 0)),
                  pl.BlockSpec(lnv.shape, lambda i: (0, 0)), pl.BlockSpec(wout.shape, lambda i: (0, 0))],
        out_specs=[tok, tok, pl.BlockSpec((ts, 2 * M), lambda i: (i, 0))],
        out_shape=[jax.ShapeDtypeStruct((S, D), F32), jax.ShapeDtypeStruct((S, D), F32),
                   jax.ShapeDtypeStruct((S, 2 * M), BF)],
        compiler_params=_params(("arbitrary",)),
    )(cq, proj, h, x, vecs, lnv, wout)


def mix_out_bwd(dxo, ym, vecs, wout, cq, lnv, proj, h, name):
    S, D = dxo.shape
    M = cq.shape[1]
    ts = min(512, S)

    def body(dxo_ref, ym_ref, v_ref, w_ref, cq_ref, ln_ref, uy_ref, h_ref,
             dcq_ref, dh_ref, duy_ref, dyb_ref, vgd_ref, vgm_ref):
        @pl.when(pl.program_id(0) == 0)
        def _():
            vgd_ref[...] = jnp.zeros_like(vgd_ref)
            vgm_ref[...] = jnp.zeros_like(vgm_ref)

        dxo_v = dxo_ref[...]
        dyb = (_row(v_ref, R_GT2) * dxo_v).astype(BF)
        dyb_ref[...] = dyb
        vgd_ref[0:1, :] += _colsum(dxo_v * ym_ref[...])
        dycat = lax.dot_general(dyb, w_ref[...], CONTRACT_LAST, preferred_element_type=F32)
        dyc = dycat[:, 0:M]
        dyr = dycat[:, M:2 * M]
        z, rstd = _layernorm_parts(cq_ref[...])
        lng = _row(ln_ref, 0)
        l = z * lng + _row(ln_ref, 1)
        sl = _sigmoid(l)
        dl = dyc * (sl * (1.0 + l * (1.0 - sl)))
        vgm_ref[0:1, :] += _colsum(dl * z)
        vgm_ref[1:2, :] += _colsum(dl)
        dz = dl * lng
        dcq_ref[...] = rstd * (dz - _rowmean(dz) - z * _rowmean(dz * z))
        uy = uy_ref[...]
        u2 = uy * uy
        th = jnp.tanh(GELU_K * (uy + GELU_C * uy * u2))
        gelu = 0.5 * uy * (1.0 + th)
        dgelu = 0.5 * (1.0 + th) + 0.5 * uy * (1.0 - th * th) * (GELU_K * (1.0 + 3.0 * GELU_C * u2))
        dh_ref[...] = dyr * gelu
        duy_ref[...] = (dyr * h_ref[...] * dgelu).astype(BF)

    tok = pl.BlockSpec((ts, D), lambda i: (i, 0))
    mtok = lambda o: pl.BlockSpec((ts, M), lambda i: (i, o))
    return _pcall(
        body, name=name, grid=(S // ts,),
        in_specs=[tok, tok, pl.BlockSpec(vecs.shape, lambda i: (0, 0)), pl.BlockSpec(wout.shape, lambda i: (0, 0)),
                  mtok(0), pl.BlockSpec(lnv.shape, lambda i: (0, 0)), mtok(3), mtok(0)],
        out_specs=[mtok(0), mtok(0), mtok(0), tok, pl.BlockSpec((SUBLANES, D), lambda i: (0, 0)),
                   pl.BlockSpec((SUBLANES, M), lambda i: (0, 0))],
        out_shape=[jax.ShapeDtypeStruct((S, M), F32)] * 2 + [jax.ShapeDtypeStruct((S, M), BF), jax.ShapeDtypeStruct((S, D), BF),
                   jax.ShapeDtypeStruct((SUBLANES, D), F32), jax.ShapeDtypeStruct((SUBLANES, M), F32)],
        compiler_params=_params(("arbitrary",)),
    )(dxo, ym, vecs, wout, cq, lnv, proj, h)


def mix_in_bwd(dparts, x, dxo, vecs, win, name):
    S, D = x.shape
    M = dparts[0].shape[1]
    ts = min(512, S)

    def body(d0, d1, d2, d3, x_ref, dxo_ref, v_ref, w_ref, dx_ref, hb_ref, dp_ref, vg_ref):
        @pl.when(pl.program_id(0) == 0)
        def _():
            vg_ref[...] = jnp.zeros_like(vg_ref)

        for q, dref in enumerate((d0, d1, d2, d3)):
            dp_ref[:, q * M:(q + 1) * M] = dref[...].astype(BF)
        dh = lax.dot_general(dp_ref[...], w_ref[...], CONTRACT_LAST, preferred_element_type=F32)
        xv = x_ref[...]
        r = lax.rsqrt(_rowmean(xv * xv) + EPS)
        n = xv * r
        g = _row(v_ref, R_G2)
        sc1 = 1.0 + _row(v_ref, R_SC2)
        gsc = g * sc1
        hb_ref[...] = (n * gsc + _row(v_ref, R_SH2)).astype(BF)
        dhn = dh * n
        vg_ref[0:1, :] += _colsum(dh)
        vg_ref[1:2, :] += _colsum(dhn) * g
        vg_ref[2:3, :] += _colsum(dhn) * sc1
        dn = dh * gsc
        dx_ref[...] = dxo_ref[...] + r * (dn - n * _rowmean(dn * n))

    tok = pl.BlockSpec((ts, D), lambda i: (i, 0))
    mtok = pl.BlockSpec((ts, M), lambda i: (i, 0))
    return _pcall(
        body, name=name, grid=(S // ts,),
        in_specs=[mtok] * 4 + [tok, tok, pl.BlockSpec(vecs.shape, lambda i: (0, 0)), pl.BlockSpec(win.shape, lambda i: (0, 0))],
        out_specs=[tok, tok, pl.BlockSpec((ts, 4 * M), lambda i: (i, 0)), pl.BlockSpec((SUBLANES, D), lambda i: (0, 0))],
        out_shape=[jax.ShapeDtypeStruct((S, D), F32), jax.ShapeDtypeStruct((S, D), BF),
                   jax.ShapeDtypeStruct((S, 4 * M), BF), jax.ShapeDtypeStruct((SUBLANES, D), F32)],
        compiler_params=_params(("arbitrary",)),
    )(*dparts, x, dxo, vecs, win)


def _adamw(w, g, m, v):
    m = ADAM_B1 * m + (1.0 - ADAM_B1) * g
    v = ADAM_B2 * v + (1.0 - ADAM_B2) * (g * g)
    m_hat = m / (1.0 - ADAM_B1 ** ADAM_STEP)
    v_hat = v / (1.0 - ADAM_B2 ** ADAM_STEP)
    delta = -ADAM_LR * (m_hat / (jnp.sqrt(v_hat) + ADAM_EPS) + ADAM_WD * w)
    return delta, m, v


def adam_big(w, g, m, v, name):
    R, C = w.shape
    tr = 256 if R % 256 == 0 else R // 2 if (R // 2) % SUBLANES == 0 and R > 512 else R
    tc = C if C <= 1536 else (1152 if C % 1152 == 0 else 1024)
    assert R % tr == 0 and C % tc == 0

    def body(w_ref, g_ref, m_ref, v_ref, d_ref, nm_ref, nv_ref):
        d, nm, nv = _adamw(w_ref[...], g_ref[...], m_ref[...], v_ref[...])
        d_ref[...] = d
        nm_ref[...] = nm
        nv_ref[...] = nv

    blk = pl.BlockSpec((tr, tc), lambda i, j: (i, j))
    return _pcall(
        body, name=name, grid=(R // tr, C // tc), in_specs=[blk] * 4, out_specs=[blk] * 3,
        out_shape=[jax.ShapeDtypeStruct((R, C), F32)] * 3, compiler_params=_params(("parallel", "parallel")),
    )(w, g, m, v)


def adam_cond(c_all, dmod, w, m, v, name):
    B, Kin = c_all.shape
    N = w.shape[1]
    tn = 768 if N % 768 == 0 else 256
    assert N % tn == 0

    def body(c_ref, d_ref, w_ref, m_ref, v_ref, g_ref, dl_ref, nm_ref, nv_ref):
        cv = c_ref[...]
        ca = cv * _sigmoid(cv)
        g = lax.dot_general(ca.astype(BF), d_ref[...].astype(BF), CONTRACT_FIRST, preferred_element_type=F32)
        d, nm, nv = _adamw(w_ref[...], g, m_ref[...], v_ref[...])
        g_ref[...] = g
        dl_ref[...] = d
        nm_ref[...] = nm
        nv_ref[...] = nv

    blk = pl.BlockSpec((Kin, tn), lambda n: (0, n))
    return _pcall(
        body, name=name, grid=(N // tn,),
        in_specs=[pl.BlockSpec((B, Kin), lambda n: (0, 0)), pl.BlockSpec((B, tn), lambda n: (0, n)), blk, blk, blk],
        out_specs=[blk] * 4, out_shape=[jax.ShapeDtypeStruct((Kin, N), F32)] * 4,
        compiler_params=_params(("parallel",)),
    )(c_all, dmod, w, m, v)


def adam_small(ws, gs, ms, vs, name):
    n = len(ws)

    def body(*refs):
        ins, outs = refs[:4 * n], refs[4 * n:]
        for k in range(n):
            d, nm, nv = _adamw(ins[k][...], ins[n + k][...], ins[2 * n + k][...], ins[3 * n + k][...])
            outs[k][...] = d
            outs[n + k][...] = nm
            outs[2 * n + k][...] = nv

    specs = [pl.BlockSpec(w.shape, lambda i: (0, 0)) for w in ws]
    shapes = [jax.ShapeDtypeStruct(w.shape, F32) for w in ws]
    out = _pcall(body, name=name, grid=(1,), in_specs=specs * 4, out_specs=specs * 3, out_shape=shapes * 3,
                 compiler_params=_params(("arbitrary",)))(*ws, *gs, *ms, *vs)
    return out[:n], out[n:2 * n], out[2 * n:]


def _me():
    return lax.axis_index("x"), lax.axis_index("y"), lax.axis_index("c")


def _flip(x, y, p):
    return (x ^ (p >> 1) if (p >> 1) else x), (y ^ (p & 1) if (p & 1) else y)


def _handshake(peers):
    barrier = pltpu.get_barrier_semaphore()
    for peer in peers:
        pl.semaphore_signal(barrier, inc=1, device_id=peer, device_id_type=MESH)
    pl.semaphore_wait(barrier, len(peers))


def _seq_call(body, *, name, n_in, out_shape, sem_shapes, collective_id):
    del n_in
    return pl.kernel(body, out_type=out_shape, mesh=plsc.ScalarSubcoreMesh(axis_name="sq", num_cores=1), name=name,
                     scratch_types=sem_shapes, compiler_params=pltpu.CompilerParams(collective_id=collective_id))


def _hbm_comm_call(body, *, name, n_in, out_shape, sem_shapes, seq_id):
    if seq_id is not None:
        return _seq_call(body, name=name, n_in=n_in, out_shape=out_shape, sem_shapes=sem_shapes, collective_id=seq_id)
    anyspec = pl.BlockSpec(memory_space=pl.ANY)
    return _pcall(body, name=name, in_specs=[anyspec] * n_in, out_specs=[anyspec] * len(out_shape), out_shape=out_shape,
                  scratch_shapes=sem_shapes, compiler_params=_params())


def allgather_devices(v, name, with_sum=False):
    R, L = v.shape

    def body(v_ref, out_ref, *rest):
        if with_sum:
            sum_ref, send_sems, recv_sems = rest
        else:
            send_sems, recv_sems = rest
        x, y, c = _me()
        me = 4 * x + 2 * y + c
        out_ref[me] = v_ref[...]
        copies = []
        for p in range(1, N_DEV):
            px, py = _flip(x, y, p >> 1)
            pc = (1 - c) if (p & 1) else c
            peer = 4 * px + 2 * py + pc
            send = pltpu.make_async_remote_copy(src_ref=v_ref, dst_ref=out_ref.at[me], send_sem=send_sems.at[p - 1],
                                                recv_sem=recv_sems.at[p - 1], device_id=(px, py, pc), device_id_type=MESH)
            send.start()
            recv = pltpu.make_async_remote_copy(src_ref=v_ref, dst_ref=out_ref.at[peer], send_sem=send_sems.at[p - 1],
                                                recv_sem=recv_sems.at[p - 1], device_id=(px, py, pc), device_id_type=MESH)
            copies.append((send, recv))
        for send, recv in copies:
            recv.wait_recv()
        for send, recv in copies:
            send.wait_send()
        if with_sum:
            s = out_ref[0]
            for k in range(1, N_DEV):
                s = s + out_ref[k]
            sum_ref[...] = s

    vm = pl.BlockSpec(memory_space=pltpu.VMEM)
    out_shape = [jax.ShapeDtypeStruct((N_DEV, R, L), F32)]
    if with_sum:
        out_shape.append(jax.ShapeDtypeStruct((R, L), F32))
    return _pcall(
        body, name=name, in_specs=[vm], out_specs=[vm] * len(out_shape), out_shape=out_shape,
        scratch_shapes=[pltpu.SemaphoreType.DMA((N_DEV - 1,)), pltpu.SemaphoreType.DMA((N_DEV - 1,))],
        compiler_params=_params(),
    )(v)


def allgather_devices_hbm(v, name, seq_id):
    R, L = v.shape

    def body(v_ref, out_ref, send_sems, recv_sems, local_sem):
        x, y, c = _me()
        me = 4 * x + 2 * y + c
        peers = []
        for p in range(1, N_DEV):
            px, py = _flip(x, y, p >> 1)
            peers.append((px, py, (1 - c) if (p & 1) else c))
        _handshake(peers)
        lc = pltpu.make_async_copy(v_ref, out_ref.at[me], local_sem)
        lc.start()
        copies = []
        for p, (px, py, pc) in enumerate(peers):
            send = pltpu.make_async_remote_copy(src_ref=v_ref, dst_ref=out_ref.at[me], send_sem=send_sems.at[p],
                                                recv_sem=recv_sems.at[p], device_id=(px, py, pc), device_id_type=MESH)
            send.start()
            recv = pltpu.make_async_remote_copy(src_ref=v_ref, dst_ref=out_ref.at[4 * px + 2 * py + pc], send_sem=send_sems.at[p],
                                                recv_sem=recv_sems.at[p], device_id=(px, py, pc), device_id_type=MESH)
            copies.append((send, recv))
        for send, recv in copies:
            recv.wait_recv()
        for send, recv in copies:
            send.wait_send()
        lc.wait()

    return _seq_call(body, name=name, n_in=1, out_shape=[jax.ShapeDtypeStruct((N_DEV, R, L), F32)],
                     sem_shapes=[pltpu.SemaphoreType.DMA((N_DEV - 1,)), pltpu.SemaphoreType.DMA((N_DEV - 1,)),
                                 pltpu.SemaphoreType.DMA], collective_id=seq_id)(v)[0]


def sum_slots(g, name):
    n, R, L = g.shape
    tr = 216 if R % 216 == 0 else R
    assert R % tr == 0 and tr % SUBLANES == 0

    def body(g_ref, o_ref):
        s = g_ref[0]
        for k in range(1, n):
            s = s + g_ref[k]
        o_ref[...] = s

    return _pcall(body, name=name, grid=(R // tr,), in_specs=[pl.BlockSpec((n, tr, L), lambda i: (0, i, 0))],
                  out_specs=pl.BlockSpec((tr, L), lambda i: (i, 0)), out_shape=jax.ShapeDtypeStruct((R, L), F32),
                  compiler_params=_params(("parallel",)))(g)


def allgather_chips(v, name):
    R, L = v.shape

    def body(v_ref, out_ref, send_sems, recv_sems):
        x, y, c = _me()
        chip = 2 * x + y
        out_ref[chip] = v_ref[...]
        copies = []
        for p in range(1, N_CHIPS):
            px, py = _flip(x, y, p)
            send = pltpu.make_async_remote_copy(src_ref=v_ref, dst_ref=out_ref.at[chip], send_sem=send_sems.at[p - 1],
                                                recv_sem=recv_sems.at[p - 1], device_id=(px, py, c), device_id_type=MESH)
            send.start()
            recv = pltpu.make_async_remote_copy(src_ref=v_ref, dst_ref=out_ref.at[2 * px + py], send_sem=send_sems.at[p - 1],
                                                recv_sem=recv_sems.at[p - 1], device_id=(px, py, c), device_id_type=MESH)
            copies.append((send, recv))
        for send, recv in copies:
            recv.wait_recv()
        for send, recv in copies:
            send.wait_send()

    vm = pl.BlockSpec(memory_space=pltpu.VMEM)
    return _pcall(
        body, name=name, in_specs=[vm], out_specs=vm, out_shape=jax.ShapeDtypeStruct((N_CHIPS, R, L), F32),
        scratch_shapes=[pltpu.SemaphoreType.DMA((N_CHIPS - 1,)), pltpu.SemaphoreType.DMA((N_CHIPS - 1,))],
        compiler_params=_params(),
    )(v)


def _shard_window(ref, kind, shard_shape, chip, half):
    r, c = shard_shape
    hr = r // 2
    if kind == "col":
        return ref.at[pl.ds(pl.multiple_of(half * hr, hr), hr), pl.ds(pl.multiple_of(chip * c, c), c)]
    return ref.at[pl.ds(pl.multiple_of(chip * r + half * hr, hr), hr), :]


def allgather_weights(shards, kinds, name, seq_id=None):
    n = len(shards)
    fulls = []
    for s, kind in zip(shards, kinds):
        r, c = s.shape
        fulls.append(jax.ShapeDtypeStruct((r, N_CHIPS * c) if kind == "col" else (N_CHIPS * r, c), s.dtype))

    def body(*refs):
        srcs, outs = refs[:n], refs[n:2 * n]
        send_sems, recv_sems, fsend_sems, frecv_sems = refs[2 * n:]
        x, y, c = _me()
        chip = 2 * x + y
        sib = (x, y, 1 - c)
        if seq_id is not None:
            _handshake([(*_flip(x, y, p), c) for p in range(1, N_CHIPS)] + [sib])
        sends, fwds = [], []
        for i in range(n):
            shp = srcs[i].shape
            hr = shp[0] // 2
            my_half = srcs[i].at[pl.ds(pl.multiple_of(c * hr, hr), hr), :]
            for p in range(1, N_CHIPS):
                px, py = _flip(x, y, p)
                k = i * (N_CHIPS - 1) + p - 1
                cp = pltpu.make_async_remote_copy(src_ref=my_half, dst_ref=_shard_window(outs[i], kinds[i], shp, chip, c),
                                                  send_sem=send_sems.at[k], recv_sem=recv_sems.at[k],
                                                  device_id=(px, py, c), device_id_type=MESH)
                cp.start()
                sends.append(cp)
        for i in range(n):
            shp = srcs[i].shape
            for p in range(1, N_CHIPS):
                px, py = _flip(x, y, p)
                k = i * (N_CHIPS - 1) + p - 1
                landed = _shard_window(outs[i], kinds[i], shp, 2 * px + py, c)
                pltpu.make_async_remote_copy(src_ref=landed, dst_ref=landed, send_sem=send_sems.at[k], recv_sem=recv_sems.at[k],
                                             device_id=(px, py, c), device_id_type=MESH).wait_recv()
                fw = pltpu.make_async_remote_copy(src_ref=landed, dst_ref=landed, send_sem=fsend_sems.at[k],
                                                  recv_sem=frecv_sems.at[k], device_id=sib, device_id_type=MESH)
                fw.start()
                fwds.append(fw)
        for i in range(n):
            shp = srcs[i].shape
            for p in range(1, N_CHIPS):
                px, py = _flip(x, y, p)
                k = i * (N_CHIPS - 1) + p - 1
                other = _shard_window(outs[i], kinds[i], shp, 2 * px + py, 1 - c)
                pltpu.make_async_remote_copy(src_ref=other, dst_ref=other, send_sem=fsend_sems.at[k], recv_sem=frecv_sems.at[k],
                                             device_id=sib, device_id_type=MESH).wait_recv()
        for cp in sends + fwds:
            cp.wait_send()

    nk = n * (N_CHIPS - 1)
    gathered = _hbm_comm_call(
        body, name=name, n_in=n, out_shape=fulls, seq_id=seq_id,
        sem_shapes=[pltpu.SemaphoreType.DMA((nk,)), pltpu.SemaphoreType.DMA((nk,)), pltpu.SemaphoreType.DMA((nk,)),
                    pltpu.SemaphoreType.DMA((nk,))],
    )(*shards)
    return gathered


def place_local_shards(fulls, shards, kinds, name):
    n = len(shards)
    chip = jnp.reshape(2 * lax.axis_index("x") + lax.axis_index("y"), (1,)).astype(I32)

    def body(ci_ref, *refs):
        for i in range(n):
            refs[2 * n + i][...] = refs[i][...]

    in_specs = [pl.BlockSpec(s.shape, lambda i, ci: (0, 0)) for s in shards] + [pl.BlockSpec(memory_space=pl.ANY)] * n
    out_specs = [pl.BlockSpec(s.shape, (lambda i, ci: (0, ci[0])) if k == "col" else (lambda i, ci: (ci[0], 0)))
                 for s, k in zip(shards, kinds)]
    gs = pltpu.PrefetchScalarGridSpec(num_scalar_prefetch=1, grid=(1,), in_specs=in_specs, out_specs=out_specs)
    return _pcall(body, name=name, grid_spec=gs, out_shape=[jax.ShapeDtypeStruct(f.shape, f.dtype) for f in fulls],
                  input_output_aliases={1 + n + i: i for i in range(n)}, compiler_params=_params(("arbitrary",)))(chip, *shards, *fulls)


def _as_halves(g, kind, shard_shape):
    r, c = shard_shape
    if kind == "col":
        return g.reshape(2, r // 2, N_CHIPS * c)
    return g.reshape(N_CHIPS, 2, r // 2, c)


def exchange_sibling_halves(grads, kinds, shard_shapes, name, seq_id=None):
    n = len(grads)
    views = [_as_halves(g, k, s) for g, k, s in zip(grads, kinds, shard_shapes)]
    outs = []
    for k, (r, c) in zip(kinds, shard_shapes):
        outs.append(jax.ShapeDtypeStruct((r // 2, N_CHIPS * c) if k == "col" else (N_CHIPS, r // 2, c), F32))

    def body(*refs):
        srcs, dsts = refs[:n], refs[n:2 * n]
        send_sems, recv_sems = refs[2 * n:]
        x, y, c = _me()
        if seq_id is not None:
            _handshake([(x, y, 1 - c)])
        cps = []
        for i in range(n):
            src = srcs[i].at[1 - c] if kinds[i] == "col" else srcs[i].at[:, 1 - c]
            cp = pltpu.make_async_remote_copy(src_ref=src, dst_ref=dsts[i], send_sem=send_sems.at[i], recv_sem=recv_sems.at[i],
                                              device_id=(x, y, 1 - c), device_id_type=MESH)
            cp.start()
            cps.append(cp)
        for cp in cps:
            cp.wait_recv()
        for cp in cps:
            cp.wait_send()

    return _hbm_comm_call(body, name=name, n_in=n, out_shape=outs, seq_id=seq_id,
                          sem_shapes=[pltpu.SemaphoreType.DMA((n,)), pltpu.SemaphoreType.DMA((n,))])(*views)


def add_sibling_half(g, recv, kind, shard_shape, cidx, name):
    r, c = shard_shape
    hr = r // 2
    gv = _as_halves(g, kind, shard_shape)
    tr = hr if hr <= 512 else (256 if hr % 256 == 0 else hr // 2)
    assert hr % tr == 0

    def body(ci_ref, g_ref, r_ref, h_ref, hb_ref):
        s = g_ref[...] + r_ref[...]
        h_ref[...] = s
        hb_ref[...] = s.astype(BF)

    if kind == "col":
        grid = (hr // tr, N_CHIPS)
        g_spec = pl.BlockSpec((None, tr, c), lambda i, k, ci: (ci[0], i, k))
        o_spec = pl.BlockSpec((tr, c), lambda i, k, ci: (i, k))
    else:
        grid = (hr // tr, N_CHIPS)
        g_spec = pl.BlockSpec((None, None, tr, c), lambda i, k, ci: (k, ci[0], i, 0))
        o_spec = pl.BlockSpec((None, tr, c), lambda i, k, ci: (k, i, 0))
    gs = pltpu.PrefetchScalarGridSpec(num_scalar_prefetch=1, grid=grid, in_specs=[g_spec, o_spec], out_specs=[o_spec, o_spec])
    return _pcall(
        body, name=name, grid_spec=gs,
        out_shape=[jax.ShapeDtypeStruct(recv.shape, F32), jax.ShapeDtypeStruct(recv.shape, BF)],
        compiler_params=_params(("parallel", "parallel")),
    )(cidx, gv, recv)


def exchange_chip_pieces(hbs, kinds, shard_shapes, name, seq_id=None):
    n = len(hbs)
    outs = [jax.ShapeDtypeStruct((N_CHIPS - 1, r // 2, c), BF) for (r, c) in shard_shapes]

    def body(*refs):
        srcs, dsts = refs[:n], refs[n:2 * n]
        send_sems, recv_sems = refs[2 * n:]
        x, y, c = _me()
        if seq_id is not None:
            _handshake([(*_flip(x, y, p), c) for p in range(1, N_CHIPS)])
        cps = []
        for i in range(n):
            cc = shard_shapes[i][1]
            for p in range(1, N_CHIPS):
                px, py = _flip(x, y, p)
                pchip = 2 * px + py
                src = (srcs[i].at[:, pl.ds(pl.multiple_of(pchip * cc, cc), cc)] if kinds[i] == "col" else srcs[i].at[pchip])
                k = i * (N_CHIPS - 1) + p - 1
                cp = pltpu.make_async_remote_copy(src_ref=src, dst_ref=dsts[i].at[p - 1], send_sem=send_sems.at[k],
                                                  recv_sem=recv_sems.at[k], device_id=(px, py, c), device_id_type=MESH)
                cp.start()
                cps.append(cp)
        for cp in cps:
            cp.wait_recv()
        for cp in cps:
            cp.wait_send()

    nk = n * (N_CHIPS - 1)
    return _hbm_comm_call(body, name=name, n_in=n, out_shape=outs, seq_id=seq_id,
                          sem_shapes=[pltpu.SemaphoreType.DMA((nk,)), pltpu.SemaphoreType.DMA((nk,))])(*hbs)


def sum_chip_pieces(h, pieces, kind, shard_shape, chip_core, name):
    r, c = shard_shape
    hr = r // 2
    tr = hr if hr <= 512 else (256 if hr % 256 == 0 else hr // 2)
    assert hr % tr == 0
    nrb = hr // tr

    def body(ci_ref, h_ref, p_ref, q_ref):
        q_ref[...] = ((h_ref[...] + p_ref[0].astype(F32)) + p_ref[1].astype(F32)) + p_ref[2].astype(F32)

    if kind == "col":
        h_spec = pl.BlockSpec((tr, c), lambda i, ci: (i, ci[0]))
    else:
        h_spec = pl.BlockSpec((None, tr, c), lambda i, ci: (ci[0], i, 0))
    gs = pltpu.PrefetchScalarGridSpec(
        num_scalar_prefetch=1, grid=(nrb,),
        in_specs=[h_spec, pl.BlockSpec((N_CHIPS - 1, tr, c), lambda i, ci: (0, i, 0))],
        out_specs=pl.BlockSpec((tr, c), lambda i, ci: (ci[1] * nrb + i, 0)))
    return _pcall(body, name=name, grid_spec=gs, out_shape=jax.ShapeDtypeStruct((r, c), F32),
                  compiler_params=_params(("parallel",)))(chip_core, h, pieces)


def exchange_reduced_halves(qs, name):
    n = len(qs)

    def body(*refs):
        bufs = refs[n:2 * n]
        send_sems, recv_sems = refs[2 * n:]
        x, y, c = _me()
        cps = []
        for i in range(n):
            hr = bufs[i].shape[0] // 2
            mine = bufs[i].at[pl.ds(pl.multiple_of(c * hr, hr), hr), :]
            other = bufs[i].at[pl.ds(pl.multiple_of((1 - c) * hr, hr), hr), :]
            cp = pltpu.make_async_remote_copy(src_ref=mine, dst_ref=mine, send_sem=send_sems.at[i], recv_sem=recv_sems.at[i],
                                              device_id=(x, y, 1 - c), device_id_type=MESH)
            cp.start()
            cps.append((cp, pltpu.make_async_remote_copy(src_ref=other, dst_ref=other, send_sem=send_sems.at[i],
                                                         recv_sem=recv_sems.at[i], device_id=(x, y, 1 - c), device_id_type=MESH)))
        for cp, rv in cps:
            rv.wait_recv()
        for cp, rv in cps:
            cp.wait_send()

    anyspec = pl.BlockSpec(memory_space=pl.ANY)
    return _pcall(
        body, name=name, in_specs=[anyspec] * n, out_specs=[anyspec] * n,
        out_shape=[jax.ShapeDtypeStruct(q.shape, F32) for q in qs], input_output_aliases={i: i for i in range(n)},
        scratch_shapes=[pltpu.SemaphoreType.DMA((n,)), pltpu.SemaphoreType.DMA((n,))],
        compiler_params=_params(),
    )(*qs)


def _rows128(a):
    return a.reshape(-1, LANES)


def _after(xs, *deps):
    flat = []
    for d in deps:
        flat.extend(d if isinstance(d, (list, tuple)) else [d])
    return list(lax.optimization_barrier((tuple(xs), tuple(flat)))[0])


def _block_diag(w):
    H, d, _ = w.shape
    eye = jnp.eye(H, dtype=w.dtype)
    return jnp.einsum("hde,hg->hdge", w, eye).reshape(H * d, H * d)


def _diag_blocks(g4, H, d):
    nb = g4.shape[0]
    per = LANES // d
    g = g4.reshape(nb, per, d, per, d)
    return jnp.stack([g[:, j, :, j, :] for j in range(per)], axis=1).reshape(H, d, d)


def kernel(x, c, w_mod, b_mod, g_ffn1, w_ffn1_in, w_ffn1_out, g_mix, w_in, conv_w, conv_b, ln_g, ln_b, rnn_conv_w, rnn_conv_b, w_a, b_a, w_i, b_i, lru_lambda, w_out, g_ffn2, w_ffn2_in, w_ffn2_out, w_fmod, b_fmod, g_final, loss_target, m_w_mod, m_b_mod, m_g_ffn1, m_w_ffn1_in, m_w_ffn1_out, m_g_mix, m_w_in, m_conv_w, m_conv_b, m_ln_g, m_ln_b, m_rnn_conv_w, m_rnn_conv_b, m_w_a, m_b_a, m_w_i, m_b_i, m_lru_lambda, m_w_out, m_g_ffn2, m_w_ffn2_in, m_w_ffn2_out, m_w_fmod, m_b_fmod, m_g_final, v_w_mod, v_b_mod, v_g_ffn1, v_w_ffn1_in, v_w_ffn1_out, v_g_mix, v_w_in, v_conv_w, v_conv_b, v_ln_g, v_ln_b, v_rnn_conv_w, v_rnn_conv_b, v_w_a, v_b_a, v_w_i, v_b_i, v_lru_lambda, v_w_out, v_g_ffn2, v_w_ffn2_in, v_w_ffn2_out, v_w_fmod, v_b_fmod, v_g_final):
    S, D = x.shape[1], x.shape[2]
    M = conv_b.shape[1]
    H, HD = w_a.shape[1], w_a.shape[2]
    nb = M // LANES
    ix, iy, ic = lax.axis_index("x"), lax.axis_index("y"), lax.axis_index("c")
    chip = 2 * ix + iy
    dev = 2 * chip + ic
    cidx = jnp.reshape(ic, (1,)).astype(I32)
    chip_core = jnp.stack([chip, ic]).astype(I32)
    xs = x[0]
    tgt = loss_target[0]

    kinds = ["col", "row"]
    w_f1, w_mx, w_f2 = [w_ffn1_in[0], w_ffn1_out[0]], [w_in[0], w_out[0]], [w_ffn2_in[0], w_ffn2_out[0]]
    as_bf = lambda ws: [w.astype(BF) for w in ws]
    shapes_of = lambda ws: [w.shape for w in ws]
    b_f1, b_mx, b_f2 = as_bf(w_f1), as_bf(w_mx), as_bf(w_f2)
    got_f1i = allgather_weights(b_f1[:1], kinds[:1], "gather_ffn1_in", seq_id=9)
    got_f1o = allgather_weights(b_f1[1:], kinds[1:], "gather_ffn1_out", seq_id=13)
    got_mx = allgather_weights(b_mx, kinds, "gather_mix", seq_id=1)
    got_f2 = allgather_weights(b_f2, kinds, "gather_ffn2", seq_id=2)

    c_all =allgather_devices(_rows128(c), "gather_c")[0].reshape(N_DEV, D)
    mod_cols = cond_matmul(c_all, w_mod[0], "mod_proj")
    fmod_cols = cond_matmul(c_all, w_fmod, "fmod_proj")
    convw_pad = jnp.pad(conv_w[0], ((0, 32 - CONV_WIDTH), (0, 0)))
    rnnw_pad = jnp.pad(rnn_conv_w[0], ((0, SUBLANES - RNN_CONV_WIDTH), (0, 0)))
    n_mod, n_fmod = mod_cols.shape[1], fmod_cols.shape[1]
    small = jnp.concatenate([_rows128(mod_cols), _rows128(fmod_cols), convw_pad, rnnw_pad], axis=0)
    small4 = allgather_chips(small, "gather_cond")
    r0 = N_DEV * n_mod // LANES
    r1 = r0 + N_DEV * n_fmod // LANES
    mod_all = small4[:, :r0].reshape(N_CHIPS, N_DEV, n_mod)
    fmod_all = small4[:, r0:r1].reshape(N_CHIPS, N_DEV, n_fmod)
    convw4 = small4[:, r1:r1 + 32]
    rnnw4 = small4[:, r1 + 32:r1 + 32 + SUBLANES]
    mod_row = lax.dynamic_index_in_dim(mod_all, dev, axis=1, keepdims=False).reshape(1, N_CHIPS * n_mod) + b_mod
    fmod_row = lax.dynamic_index_in_dim(fmod_all, dev, axis=1, keepdims=False).reshape(1, N_CHIPS * n_fmod) + b_fmod[None, :]
    vecs = jnp.concatenate([mod_row.reshape(9, D), fmod_row.reshape(2, D), g_ffn1, g_mix, g_ffn2, g_final[None, :],
                            jnp.zeros((1, D), F32)], axis=0)
    lnv = jnp.concatenate([ln_g, ln_b, jnp.zeros((SUBLANES - 2, M), F32)], axis=0)
    bda = _block_diag(w_a[0]).astype(BF)
    bdi = _block_diag(w_i[0]).astype(BF)

    def reduce_add(gs, recv, ws, tag, kinds_=kinds):
        pairs = [add_sibling_half(g, r_, k, w.shape, cidx, f"add_sibling_{tag}{j}")
                 for j, (g, r_, k, w) in enumerate(zip(gs, recv, kinds_, ws))]
        return [p[0] for p in pairs], [p[1] for p in pairs]

    def reduce_sum(hs_, recv, ws, tag, kinds_=kinds):
        return [sum_chip_pieces(h_, p_, k, w.shape, chip_core, f"sum_chips_{tag}{j}")
                for j, (h_, p_, k, w) in enumerate(zip(hs_, recv, kinds_, ws))]

    rows1 = (R_SH1, R_SC1, R_GT1, R_G1)
    rows3 = (R_SH3, R_SC3, R_GT3, R_G3)
    (wi1,) = place_local_shards(got_f1i, b_f1[:1], kinds[:1], "place_ffn1_in")
    g1s, u1s, a1s = ffn_fwd_in(xs, vecs, wi1, rows1, "ffn1_fwd_in")
    (wo1,) = place_local_shards(_after(got_f1o, a1s), b_f1[1:], kinds[1:], "place_ffn1_out")
    x1, y1 = ffn_fwd_out(a1s, xs, vecs, wo1, rows1, "ffn1_fwd_out")
    win, wout = place_local_shards(_after(got_mx, x1), b_mx, kinds, "place_mix")
    proj = norm_matmul(x1, vecs, win, (R_SH2, R_SC2, R_G2), "mix_in_proj")
    cq = conv_fwd(proj, convw4, conv_b, "conv_fwd")
    xr, ra, ii, hh = rnn_fwd(proj, rnnw4, rnn_conv_b, bda, bdi, b_a, b_i, lru_lambda, "rnn_fwd")
    x2, ym, ycat = mix_out(cq, proj, hh, x1, vecs, lnv, wout, "mix_out")
    wi2, wo2 = place_local_shards(_after(got_f2, x2), b_f2, kinds, "place_ffn2")
    dx3, g2s, u2s, y2, vgf = ffn_fwd(x2, vecs, wi2, wo2, rows3, "ffn2_fwd", final_tgt=tgt)

    Fd = wo1.shape[0]
    tk = min(2048, S)
    dx2, act2, dg2, du2, h3b, dy2b, vg3 = ffn_bwd(dx3, x2, vecs, g2s, u2s, y2, wi2, wo2, rows3, "ffn2_bwd")
    gwo2 = matmul(act2, dy2b, "tn", tm=Fd // 2, tn=D, tk=tk, name="ffn2_dwo")
    gwi2 = matmul(h3b, dg2, "tn", tm=D, tn=Fd // 2, tk=tk, name="ffn2_dwg", out_cols=2 * Fd)
    gwi2 = matmul(h3b, du2, "tn", tm=D, tn=Fd // 2, tk=tk, name="ffn2_dwu", out_cols=2 * Fd, col_off=Fd, prev=gwi2)
    recv1_f2 = exchange_sibling_halves([gwi2, gwo2], kinds, shapes_of(w_f2), "reduce1_ffn2", seq_id=3)
    dcq, dhout, duy, dymb, vgd, vgm = mix_out_bwd(dx2, ym, vecs, wout, cq, lnv, proj, hh, "mix_out_bwd")
    gwout = matmul(ycat, dymb, "tn", tm=2 * M, tn=D, tk=tk, name="mix_dwout")
    recv1_f2 = _after(recv1_f2, gwout)
    h_f2, hb_f2 = reduce_add([gwi2, gwo2], recv1_f2, w_f2, "ffn2_")
    recv2_f2 = exchange_chip_pieces(hb_f2, kinds, shapes_of(w_f2), "reduce2_ffn2", seq_id=4)
    duv, dug, dconvw4, dconvb = conv_bwd(_after([dcq], hb_f2)[0], proj, convw4, "conv_bwd")
    dux, dwa4, dwi4, drnnw4, rvec = rnn_bwd(dhout, hh, xr, ra, ii, proj, rnnw4, bda, bdi, lru_lambda, "rnn_bwd")
    dx1, h2b, dpb, vg2 = mix_in_bwd((duv, dug, dux, duy), x1, dx2, vecs, win, "mix_in_bwd")
    gwin = matmul(h2b, dpb, "tn", tm=D, tn=1024, tk=tk, name="mix_dwin")
    recv1_mx = exchange_sibling_halves([gwin, gwout], kinds, shapes_of(w_mx), "reduce1_mix", seq_id=5)
    g_f2 = exchange_reduced_halves(reduce_sum(_after(h_f2, gwin), recv2_f2, w_f2, "ffn2_"), "reduce3_ffn2")
    adam_f2 = [adam_big(w, g, m, v, "adam_" + nm) for w, g, m, v, nm in
               zip(w_f2, g_f2, [m_w_ffn2_in[0], m_w_ffn2_out[0]], [v_w_ffn2_in[0], v_w_ffn2_out[0]], ["ffn2_in", "ffn2_out"])]
    h_mx, hb_mx = reduce_add([gwin, gwout], _after(recv1_mx, adam_f2[0][0], adam_f2[1][0]), w_mx, "mix_")
    recv2_mx = exchange_chip_pieces(hb_mx, kinds, shapes_of(w_mx), "reduce2_mix", seq_id=6)
    dx0, act1, dg1, du1, h1b, dy1b, vg1 = ffn_bwd(_after([dx1], hb_mx)[0], xs, vecs, g1s, u1s, y1, wi1, wo1, rows1, "ffn1_bwd")
    dmod_row = jnp.concatenate([vg1[1:3], vg1[0:1], vg2[0:2], vgd[0:1], vg3[1:3], vg3[0:1]], axis=0)
    gains = jnp.concatenate([vg1[3:4], vg2[2:3], vg3[3:4], vgf[2:4]], axis=0)
    mvecs = jnp.concatenate([dconvb, vgm[0:2], rvec[0:4], jnp.zeros((1, M), F32)], axis=0)
    parts = [_rows128(dmod_row), _rows128(vgf[0:2]), _rows128(gains), _rows128(mvecs),
             _rows128(dconvw4), _rows128(drnnw4), _rows128(_diag_blocks(dwa4, H, HD)), _rows128(_diag_blocks(dwi4, H, HD))]
    sizes = [p.shape[0] for p in parts]
    packed = jnp.concatenate(parts, axis=0)
    gathered = allgather_devices_hbm(packed, "gather_small", seq_id=10)

    gwo1 = matmul(_after([act1], recv2_mx, packed)[0], dy1b, "tn", tm=Fd // 2, tn=D, tk=tk, name="ffn1_dwo")
    w_f1o, w_f1i = w_f1[1:], w_f1[:1]
    recv1_f1o = exchange_sibling_halves([gwo1], ["row"], shapes_of(w_f1o), "reduce1_ffn1_out", seq_id=7)
    gwi1 = matmul(h1b, dg1, "tn", tm=D, tn=Fd // 2, tk=tk, name="ffn1_dwg", out_cols=2 * Fd)
    h_f1o, hb_f1o = reduce_add([gwo1], _after(recv1_f1o, gwi1), w_f1o, "ffn1_out", ["row"])
    recv2_f1o = exchange_chip_pieces(hb_f1o, ["row"], shapes_of(w_f1o), "reduce2_ffn1_out", seq_id=11)
    gwi1 = matmul(h1b, _after([du1], hb_f1o, gathered)[0], "tn", tm=D, tn=Fd // 2, tk=tk, name="ffn1_dwu", out_cols=2 * Fd,
                  col_off=Fd, prev=gwi1)
    recv1_f1i = exchange_sibling_halves([gwi1], ["col"], shapes_of(w_f1i), "reduce1_ffn1_in", seq_id=12)
    g_mx = exchange_reduced_halves(reduce_sum(h_mx, recv2_mx, w_mx, "mix_"), "reduce3_mix")
    summed = sum_slots(gathered, "sum_small")
    offs = [0]
    for s in sizes:
        offs.append(offs[-1] + s)
    seg = lambda k: summed[offs[k]:offs[k + 1]]
    g_b_mod = seg(0).reshape(1, 9 * D)
    g_b_fmod = seg(1).reshape(1, 2 * D)
    gsum = seg(2).reshape(5, D)
    loss = (0.5 / D) * jnp.sum(gsum[4])
    msum = seg(3).reshape(SUBLANES, M)
    g_conv_w = lax.dynamic_index_in_dim(seg(4).reshape(nb, 32, LANES), chip, axis=0, keepdims=False)[:CONV_WIDTH]
    g_rnn_w = lax.dynamic_index_in_dim(seg(5).reshape(nb, SUBLANES, LANES), chip, axis=0, keepdims=False)[:RNN_CONV_WIDTH]
    g_w_a = seg(6).reshape(H, HD, HD)
    g_w_i = seg(7).reshape(H, HD, HD)
    dmod_all = gathered[:, offs[0]:offs[1]].reshape(N_DEV, 9 * D)
    dfmod_all = gathered[:, offs[1]:offs[2]].reshape(N_DEV, 2 * D)
    dmod_cols = lax.dynamic_slice_in_dim(dmod_all, chip * n_mod, n_mod, axis=1)
    dfmod_cols = lax.dynamic_slice_in_dim(dfmod_all, chip * n_fmod, n_fmod, axis=1)

    g_w_mod, d_w_mod, nm_w_mod, nv_w_mod = adam_cond(c_all, dmod_cols, w_mod[0], m_w_mod[0], v_w_mod[0], "adam_w_mod")
    g_w_fmod, d_w_fmod, nm_w_fmod, nv_w_fmod = adam_cond(c_all, dfmod_cols, w_fmod, m_w_fmod, v_w_fmod, "adam_w_fmod")

    h_f1i, hb_f1i = reduce_add([gwi1], _after(recv1_f1i, recv2_f1o, g_w_mod, g_w_fmod), w_f1i, "ffn1_in", ["col"])
    recv2_f1i = exchange_chip_pieces(hb_f1i, ["col"], shapes_of(w_f1i), "reduce2_ffn1_in", seq_id=8)
    g_f1o = exchange_reduced_halves(reduce_sum(h_f1o, recv2_f1o, w_f1o, "ffn1_out", ["row"]), "reduce3_ffn1_out")
    g_f1i = exchange_reduced_halves(reduce_sum(h_f1i, recv2_f1i, w_f1i, "ffn1_in", ["col"]), "reduce3_ffn1_in")
    g_f1 = list(g_f1i) + list(g_f1o)

    big_w = w_f1 + w_mx + w_f2
    g_big = g_f1 + list(g_mx) + list(g_f2)
    names = ["ffn1_in", "ffn1_out", "w_in", "w_out", "ffn2_in", "ffn2_out"]
    big_m = [m_w_ffn1_in[0], m_w_ffn1_out[0], m_w_in[0], m_w_out[0]]
    big_v = [v_w_ffn1_in[0], v_w_ffn1_out[0], v_w_in[0], v_w_out[0]]
    big_out = [adam_big(w, g, m, v, "adam_" + nm) for w, g, m, v, nm in zip(big_w, g_big, big_m, big_v, names)] + adam_f2

    flat2 = lambda a: a.reshape(-1, a.shape[-1])
    small_names = ["b_mod", "g_ffn1", "g_mix", "conv_w", "conv_b", "ln_g", "ln_b", "rnn_conv_w", "rnn_conv_b", "w_a", "b_a",
                   "w_i", "b_i", "lru_lambda", "g_ffn2", "b_fmod", "g_final"]
    small_w = [b_mod, g_ffn1, g_mix, conv_w, conv_b, ln_g, ln_b, rnn_conv_w, rnn_conv_b, w_a, b_a, w_i, b_i, lru_lambda,
               g_ffn2, b_fmod, g_final]
    small_m = [m_b_mod, m_g_ffn1, m_g_mix, m_conv_w, m_conv_b, m_ln_g, m_ln_b, m_rnn_conv_w, m_rnn_conv_b, m_w_a, m_b_a,
               m_w_i, m_b_i, m_lru_lambda, m_g_ffn2, m_b_fmod, m_g_final]
    small_v = [v_b_mod, v_g_ffn1, v_g_mix, v_conv_w, v_conv_b, v_ln_g, v_ln_b, v_rnn_conv_w, v_rnn_conv_b, v_w_a, v_b_a,
               v_w_i, v_b_i, v_lru_lambda, v_g_ffn2, v_b_fmod, v_g_final]
    small_g = [g_b_mod, gsum[0:1], gsum[1:2], g_conv_w, msum[0:1], msum[1:2], msum[2:3], g_rnn_w, msum[3:4], g_w_a, msum[4:5],
               g_w_i, msum[5:6], msum[6:7], gsum[2:3], g_b_fmod, gsum[3:4]]
    small_g = [g.reshape(w.shape) for g, w in zip(small_g, small_w)]
    two_d = lambda a: a.reshape(1, -1) if a.ndim == 1 else flat2(a)
    sd, sm, sv = adam_small([two_d(a) for a in small_w], [two_d(a) for a in small_g], [two_d(a) for a in small_m],
                            [two_d(a) for a in small_v], "adam_small")
    small = {}
    for k, nm in enumerate(small_names):
        shp = small_w[k].shape
        small[nm] = (small_g[k], sd[k].reshape(shp), sm[k].reshape(shp), sv[k].reshape(shp))

    big = {"w_mod": tuple(a[None] for a in (g_w_mod, d_w_mod, nm_w_mod, nv_w_mod)),
           "w_fmod": (g_w_fmod, d_w_fmod, nm_w_fmod, nv_w_fmod)}
    for nm, full, g, (d, nmm, nvv) in zip(["w_ffn1_in", "w_ffn1_out", "w_in", "w_out", "w_ffn2_in", "w_ffn2_out"],
                                         big_w, g_big, big_out):
        big[nm] = tuple(a[None] for a in (g, d, nmm, nvv))
    order = ["w_mod", "b_mod", "g_ffn1", "w_ffn1_in", "w_ffn1_out", "g_mix", "w_in", "conv_w", "conv_b", "ln_g", "ln_b",
             "rnn_conv_w", "rnn_conv_b", "w_a", "b_a", "w_i", "b_i", "lru_lambda", "w_out", "g_ffn2", "w_ffn2_in",
             "w_ffn2_out", "w_fmod", "b_fmod", "g_final"]
    table = {**small, **big}
    outs = [loss, dx0[None]]
    for kind_ in range(4):
        outs.extend(table[nm][kind_] for nm in order)
    return tuple(outs)
```

```python
import functools

import jax
import jax.numpy as jnp
from jax import lax
from jax.experimental import pallas as pl
from jax.experimental.pallas import tpu as pltpu
from jax.experimental.pallas import tpu_sc as plsc

F32 = jnp.float32
BF = jnp.bfloat16
I32 = jnp.int32
MESH = pl.DeviceIdType.MESH

EPS = 1e-6
RG_C = 8.0
MACARON_W = 0.5
CONV_WIDTH = 31
RNN_CONV_WIDTH = 4
ADAM_LR = 0.001
ADAM_B1 = 0.9
ADAM_B2 = 0.999
ADAM_EPS = 1e-08
ADAM_WD = 0.01
ADAM_STEP = 10

LANES = 128
SUBLANES = 8
VMEM_LIMIT = 62 * 1024 * 1024
N_CHIPS = 4
N_DEV = 8

R_SH1, R_SC1, R_GT1, R_SH2, R_SC2, R_GT2, R_SH3, R_SC3, R_GT3, R_FSH, R_FSC, R_G1, R_G2, R_G3, R_GF = range(15)

CONTRACT_LAST = (((1,), (1,)), ((), ()))
CONTRACT_FIRST = (((0,), (0,)), ((), ()))


def _pcall(body, **kw):
    return pl.pallas_call(body, **kw)


def _params(sem=None, vmem=VMEM_LIMIT):
    if sem is None:
        return pltpu.CompilerParams(vmem_limit_bytes=vmem)
    return pltpu.CompilerParams(dimension_semantics=sem, vmem_limit_bytes=vmem)


def _row(ref, r):
    return ref[r:r + 1, :]


def _sigmoid(x):
    return 1.0 / (1.0 + jnp.exp(-x))


def _colsum(x):
    return jnp.sum(x, axis=0, keepdims=True)


def _rowmean(x):
    return jnp.mean(x, axis=-1, keepdims=True)


def matmul(a, b, mode, *, tm, tn, tk, name, out_dtype=F32, out_cols=None, col_off=0, prev=None):
    if mode == "nn":
        (M, K), (K2, N) = a.shape, b.shape
    elif mode == "nt":
        (M, K), (N, K2) = a.shape, b.shape
    else:
        (K, M), (K2, N) = a.shape, b.shape
    assert K == K2 and M % tm == 0 and N % tn == 0 and K % tk == 0 and col_off % tn == 0
    nk = K // tk
    out_cols = N if out_cols is None else out_cols
    off = col_off // tn

    def body(*refs):
        if prev is None:
            a_ref, b_ref, o_ref, acc = refs
        else:
            a_ref, b_ref, _, o_ref, acc = refs
        k = pl.program_id(2)
        av = a_ref[...].astype(BF)
        bv = b_ref[...].astype(BF)
        if mode == "nn":
            part = jnp.dot(av, bv, preferred_element_type=F32)
        elif mode == "nt":
            part = lax.dot_general(av, bv, CONTRACT_LAST, preferred_element_type=F32)
        else:
            part = lax.dot_general(av, bv, CONTRACT_FIRST, preferred_element_type=F32)
        if nk == 1:
            o_ref[...] = part.astype(out_dtype)
            return

        @pl.when(k == 0)
        def _():
            acc[...] = part

        @pl.when((k > 0) & (k < nk - 1))
        def _():
            acc[...] += part

        @pl.when(k == nk - 1)
        def _():
            o_ref[...] = (acc[...] + part).astype(out_dtype)

    if mode == "nn":
        a_spec = pl.BlockSpec((tm, tk), lambda m, n, k: (m, k))
        b_spec = pl.BlockSpec((tk, tn), lambda m, n, k: (k, n))
    elif mode == "nt":
        a_spec = pl.BlockSpec((tm, tk), lambda m, n, k: (m, k))
        b_spec = pl.BlockSpec((tn, tk), lambda m, n, k: (n, k))
    else:
        a_spec = pl.BlockSpec((tk, tm), lambda m, n, k: (k, m))
        b_spec = pl.BlockSpec((tk, tn), lambda m, n, k: (k, n))
    in_specs = [a_spec, b_spec]
    args = [a, b]
    aliases = {}
    if prev is not None:
        in_specs.append(pl.BlockSpec(memory_space=pl.ANY))
        args.append(prev)
        aliases = {2: 0}
    return _pcall(
        body, name=name, grid=(M // tm, N // tn, nk), in_specs=in_specs,
        out_specs=pl.BlockSpec((tm, tn), lambda m, n, k: (m, n + off)),
        out_shape=jax.ShapeDtypeStruct((M, out_cols), out_dtype),
        scratch_shapes=[pltpu.VMEM((tm, tn), F32)], input_output_aliases=aliases,
        compiler_params=_params(("parallel", "parallel", "arbitrary")),
    )(*args)


def cond_matmul(c_all, w, name):
    B, K = c_all.shape
    N = w.shape[1]
    tn = 256
    assert N % tn == 0

    def body(c_ref, w_ref, o_ref):
        cv = c_ref[...]
        ca = cv * _sigmoid(cv)
        o_ref[...] = jnp.dot(ca.astype(BF), w_ref[...].astype(BF), preferred_element_type=F32)

    return _pcall(
        body, name=name, grid=(N // tn,),
        in_specs=[pl.BlockSpec((B, K), lambda n: (0, 0)), pl.BlockSpec((K, tn), lambda n: (0, n))],
        out_specs=pl.BlockSpec((B, tn), lambda n: (0, n)),
        out_shape=jax.ShapeDtypeStruct((B, N), F32), compiler_params=_params(("parallel",)),
    )(c_all, w)


FFN_FWD_TS = 512
FFN_BWD_TS = 256


def _resident(shape, index_map):
    return pl.BlockSpec(shape, index_map, pipeline_mode=pl.Buffered(1))


def _final_norm_loss_grad(xv, t, v_ref, vg_ref):
    D = xv.shape[-1]
    r = lax.rsqrt(_rowmean(xv * xv) + EPS)
    n = xv * r
    g = _row(v_ref, R_GF)
    sc1 = 1.0 + _row(v_ref, R_FSC)
    gsc = g * sc1
    e = n * gsc + _row(v_ref, R_FSH) - t
    vg_ref[3:4, :] += _colsum(e * e)
    dout = e * (1.0 / D)
    dn_ = dout * n
    vg_ref[0:1, :] += _colsum(dout)
    vg_ref[1:2, :] += _colsum(dn_) * g
    vg_ref[2:3, :] += _colsum(dn_) * sc1
    dn = dout * gsc
    return r * (dn - n * _rowmean(dn * n))


def ffn_fwd(x, vecs, wi, wo, rows, name, final_tgt=None):
    r_sh, r_sc, r_gt, r_g = rows
    S, D = x.shape
    Fd = wo.shape[0]
    ts = min(FFN_FWD_TS, S)
    with_final = final_tgt is not None

    def body(*refs):
        if with_final:
            x_ref, v_ref, wg_ref, wu_ref, wo_ref, t_ref, xo_ref, g_ref, u_ref, y_ref, vg_ref = refs
        else:
            x_ref, v_ref, wg_ref, wu_ref, wo_ref, xo_ref, g_ref, u_ref, y_ref = refs
        xv = x_ref[...]
        r = lax.rsqrt(_rowmean(xv * xv) + EPS)
        gs = _row(v_ref, r_g) * (1.0 + _row(v_ref, r_sc))
        hb = (xv * r * gs + _row(v_ref, r_sh)).astype(BF)
        G = jnp.dot(hb, wg_ref[...], preferred_element_type=F32)
        U = jnp.dot(hb, wu_ref[...], preferred_element_type=F32)
        g_ref[...] = G.astype(BF)
        u_ref[...] = U.astype(BF)
        act = (G * _sigmoid(G) * U).astype(BF)
        Y = jnp.dot(act, wo_ref[...], preferred_element_type=F32)
        y_ref[...] = Y
        xo = xv + (MACARON_W * _row(v_ref, r_gt)) * Y
        if with_final:
            @pl.when(pl.program_id(0) == 0)
            def _():
                vg_ref[...] = jnp.zeros_like(vg_ref)

            xo_ref[...] = _final_norm_loss_grad(xo, t_ref[...], v_ref, vg_ref)
        else:
            xo_ref[...] = xo

    tok = pl.BlockSpec((ts, D), lambda i: (i, 0))
    hid = pl.BlockSpec((ts, Fd), lambda i: (i, 0))
    in_specs = [tok, pl.BlockSpec(vecs.shape, lambda i: (0, 0)), _resident((D, Fd), lambda i: (0, 0)),
                _resident((D, Fd), lambda i: (0, 1)), _resident((Fd, D), lambda i: (0, 0))]
    out_specs = [tok, hid, hid, tok]
    out_shape = [jax.ShapeDtypeStruct((S, D), F32), jax.ShapeDtypeStruct((S, Fd), BF),
                 jax.ShapeDtypeStruct((S, Fd), BF), jax.ShapeDtypeStruct((S, D), F32)]
    args = [x, vecs, wi, wi, wo]
    if with_final:
        in_specs.append(tok)
        args.append(final_tgt)
        out_specs.append(pl.BlockSpec((SUBLANES, D), lambda i: (0, 0)))
        out_shape.append(jax.ShapeDtypeStruct((SUBLANES, D), F32))
    return _pcall(body, name=name, grid=(S // ts,), in_specs=in_specs, out_specs=out_specs, out_shape=out_shape,
                  compiler_params=_params(("arbitrary",)))(*args)


def ffn_fwd_in(x, vecs, wi, rows, name):
    r_sh, r_sc, r_gt, r_g = rows
    S, D = x.shape
    Fd = wi.shape[1] // 2
    ts = min(FFN_FWD_TS, S)

    def body(x_ref, v_ref, wg_ref, wu_ref, g_ref, u_ref, a_ref):
        xv = x_ref[...]
        r = lax.rsqrt(_rowmean(xv * xv) + EPS)
        gs = _row(v_ref, r_g) * (1.0 + _row(v_ref, r_sc))
        hb = (xv * r * gs + _row(v_ref, r_sh)).astype(BF)
        G = jnp.dot(hb, wg_ref[...], preferred_element_type=F32)
        U = jnp.dot(hb, wu_ref[...], preferred_element_type=F32)
        g_ref[...] = G.astype(BF)
        u_ref[...] = U.astype(BF)
        a_ref[...] = (G * _sigmoid(G) * U).astype(BF)

    hid = pl.BlockSpec((ts, Fd), lambda i: (i, 0))
    return _pcall(
        body, name=name, grid=(S // ts,),
        in_specs=[pl.BlockSpec((ts, D), lambda i: (i, 0)), pl.BlockSpec(vecs.shape, lambda i: (0, 0)),
                  _resident((D, Fd), lambda i: (0, 0)), _resident((D, Fd), lambda i: (0, 1))],
        out_specs=[hid, hid, hid], out_shape=[jax.ShapeDtypeStruct((S, Fd), BF)] * 3,
        compiler_params=_params(("arbitrary",)),
    )(x, vecs, wi, wi)


def ffn_fwd_out(act, x, vecs, wo, rows, name):
    r_sh, r_sc, r_gt, r_g = rows
    S, D = x.shape
    Fd = wo.shape[0]
    ts = min(FFN_FWD_TS, S)

    def body(a_ref, x_ref, v_ref, wo_ref, xo_ref, y_ref):
        Y = jnp.dot(a_ref[...], wo_ref[...], preferred_element_type=F32)
        y_ref[...] = Y
        xo_ref[...] = x_ref[...] + (MACARON_W * _row(v_ref, r_gt)) * Y

    tok = pl.BlockSpec((ts, D), lambda i: (i, 0))
    return _pcall(
        body, name=name, grid=(S // ts,),
        in_specs=[pl.BlockSpec((ts, Fd), lambda i: (i, 0)), tok, pl.BlockSpec(vecs.shape, lambda i: (0, 0)),
                  _resident((Fd, D), lambda i: (0, 0))],
        out_specs=[tok, tok], out_shape=[jax.ShapeDtypeStruct((S, D), F32)] * 2,
        compiler_params=_params(("arbitrary",)),
    )(act, x, vecs, wo)


def ffn_bwd(dxo, x, vecs, gs_, us_, y, wi, wo, rows, name):
    r_sh, r_sc, r_gt, r_g = rows
    S, D = x.shape
    Fd = wo.shape[0]
    ts = min(FFN_BWD_TS, S)

    def body(dxo_ref, x_ref, v_ref, g_ref, u_ref, y_ref, wg_ref, wu_ref, wo_ref,
             dx_ref, act_ref, dg_ref, du_ref, hb_ref, dyb_ref, vg_ref):
        @pl.when(pl.program_id(0) == 0)
        def _():
            vg_ref[...] = jnp.zeros_like(vg_ref)

        dxo_v = dxo_ref[...]
        dyb = ((MACARON_W * _row(v_ref, r_gt)) * dxo_v).astype(BF)
        dyb_ref[...] = dyb
        vg_ref[0:1, :] += MACARON_W * _colsum(dxo_v * y_ref[...])
        dA = lax.dot_general(dyb, wo_ref[...], CONTRACT_LAST, preferred_element_type=F32)
        G = g_ref[...].astype(F32)
        U = u_ref[...].astype(F32)
        sg = _sigmoid(G)
        sl = G * sg
        dU = (dA * sl).astype(BF)
        dG = (dA * U * (sg * (1.0 + G * (1.0 - sg)))).astype(BF)
        act_ref[...] = (sl * U).astype(BF)
        dg_ref[...] = dG
        du_ref[...] = dU
        dh = (lax.dot_general(dG, wg_ref[...], CONTRACT_LAST, preferred_element_type=F32)
              + lax.dot_general(dU, wu_ref[...], CONTRACT_LAST, preferred_element_type=F32))
        xv = x_ref[...]
        r = lax.rsqrt(_rowmean(xv * xv) + EPS)
        n = xv * r
        g = _row(v_ref, r_g)
        sc1 = 1.0 + _row(v_ref, r_sc)
        gsc = g * sc1
        hb_ref[...] = (n * gsc + _row(v_ref, r_sh)).astype(BF)
        dhn = dh * n
        vg_ref[1:2, :] += _colsum(dh)
        vg_ref[2:3, :] += _colsum(dhn) * g
        vg_ref[3:4, :] += _colsum(dhn) * sc1
        dn = dh * gsc
        dx_ref[...] = dxo_v + r * (dn - n * _rowmean(dn * n))

    tok = pl.BlockSpec((ts, D), lambda i: (i, 0))
    hid = pl.BlockSpec((ts, Fd), lambda i: (i, 0))
    return _pcall(
        body, name=name, grid=(S // ts,),
        in_specs=[tok, tok, pl.BlockSpec(vecs.shape, lambda i: (0, 0)), hid, hid, tok, _resident((D, Fd), lambda i: (0, 0)),
                  _resident((D, Fd), lambda i: (0, 1)), _resident((Fd, D), lambda i: (0, 0))],
        out_specs=[tok, hid, hid, hid, tok, tok, pl.BlockSpec((SUBLANES, D), lambda i: (0, 0))],
        out_shape=[jax.ShapeDtypeStruct((S, D), F32), jax.ShapeDtypeStruct((S, Fd), BF),
                   jax.ShapeDtypeStruct((S, Fd), BF), jax.ShapeDtypeStruct((S, Fd), BF),
                   jax.ShapeDtypeStruct((S, D), BF), jax.ShapeDtypeStruct((S, D), BF),
                   jax.ShapeDtypeStruct((SUBLANES, D), F32)],
        compiler_params=_params(("arbitrary",)),
    )(dxo, x, vecs, gs_, us_, y, wi, wi, wo)


def norm_matmul(x, vecs, w, rows, name):
    r_sh, r_sc, r_g = rows
    S, D = x.shape
    N = w.shape[1]
    ts = min(512, S)

    def body(x_ref, v_ref, w_ref, o_ref):
        xv = x_ref[...]
        r = lax.rsqrt(_rowmean(xv * xv) + EPS)
        gs = _row(v_ref, r_g) * (1.0 + _row(v_ref, r_sc))
        hb = (xv * r * gs + _row(v_ref, r_sh)).astype(BF)
        o_ref[...] = jnp.dot(hb, w_ref[...], preferred_element_type=F32)

    return _pcall(
        body, name=name, grid=(S // ts,),
        in_specs=[pl.BlockSpec((ts, D), lambda i: (i, 0)), pl.BlockSpec(vecs.shape, lambda i: (0, 0)),
                  _resident((D, N), lambda i: (0, 0))],
        out_specs=pl.BlockSpec((ts, N), lambda i: (i, 0)),
        out_shape=jax.ShapeDtypeStruct((S, N), F32),
        compiler_params=_params(("arbitrary",)),
    )(x, vecs, w)


SEQ_TT = 256
SCAN_SEGMENTS = 4
CONV_PAD = 32


def conv_fwd(proj, convw4, conv_b, name):
    S = proj.shape[0]
    M = conv_b.shape[1]
    nb = M // LANES
    tt = min(SEQ_TT, S)

    def body(uv_ref, ug_ref, w_ref, b_ref, cq_ref, qp):
        qp[0:CONV_PAD, :] = jnp.zeros((CONV_PAD, LANES), F32)

        def step(t, carry):
            base = pl.multiple_of(t * tt, tt)
            qp[pl.ds(base + CONV_PAD, tt), :] = uv_ref[pl.ds(base, tt), :] * _sigmoid(ug_ref[pl.ds(base, tt), :])
            acc = jnp.broadcast_to(b_ref[...], (tt, LANES))
            for k in range(CONV_WIDTH):
                acc = acc + w_ref[k:k + 1, :] * qp[pl.ds(base + (CONV_PAD - CONV_WIDTH + 1) + k, tt), :]
            cq_ref[pl.ds(base, tt), :] = acc
            return carry

        lax.fori_loop(0, S // tt, step, 0)

    return _pcall(
        body, name=name, grid=(nb,),
        in_specs=[pl.BlockSpec((S, LANES), lambda c: (0, c)), pl.BlockSpec((S, LANES), lambda c: (0, c + nb)),
                  pl.BlockSpec((None, 32, LANES), lambda c: (c, 0, 0)), pl.BlockSpec((1, LANES), lambda c: (0, c))],
        out_specs=pl.BlockSpec((S, LANES), lambda c: (0, c)),
        out_shape=jax.ShapeDtypeStruct((S, M), F32),
        scratch_shapes=[pltpu.VMEM((S + CONV_PAD, LANES), F32)],
        compiler_params=_params(("arbitrary",)),
    )(proj, proj, convw4, conv_b)


def conv_bwd(dcq, proj, convw4, name):
    S, M = dcq.shape
    nb = M // LANES
    tt = min(SEQ_TT, S)
    off = CONV_PAD - CONV_WIDTH + 1

    def body(dcq_ref, uv_ref, ug_ref, w_ref, duv_ref, dug_ref, dw_ref, db_ref, qp, dp, dw8, db8):
        qp[0:CONV_PAD, :] = jnp.zeros((CONV_PAD, LANES), F32)
        dp[S:S + CONV_PAD, :] = jnp.zeros((CONV_PAD, LANES), F32)
        dw8[...] = jnp.zeros_like(dw8)
        db8[...] = jnp.zeros_like(db8)

        def fill(t, carry):
            base = pl.multiple_of(t * tt, tt)
            qp[pl.ds(base + CONV_PAD, tt), :] = uv_ref[pl.ds(base, tt), :] * _sigmoid(ug_ref[pl.ds(base, tt), :])
            dp[pl.ds(base, tt), :] = dcq_ref[pl.ds(base, tt), :]
            return carry

        lax.fori_loop(0, S // tt, fill, 0)

        def step(t, carry):
            base = pl.multiple_of(t * tt, tt)
            d_t = dcq_ref[pl.ds(base, tt), :]
            db8[...] += d_t.reshape(tt // SUBLANES, SUBLANES, LANES).sum(axis=0)
            dq = jnp.zeros((tt, LANES), F32)
            for k in range(CONV_WIDTH):
                prod = d_t * qp[pl.ds(base + off + k, tt), :]
                dw8[k] += prod.reshape(tt // SUBLANES, SUBLANES, LANES).sum(axis=0)
                dq = dq + w_ref[k:k + 1, :] * dp[pl.ds(base + (CONV_WIDTH - 1) - k, tt), :]
            uv = uv_ref[pl.ds(base, tt), :]
            sg = _sigmoid(ug_ref[pl.ds(base, tt), :])
            duv_ref[pl.ds(base, tt), :] = (dq * sg).astype(BF)
            dug_ref[pl.ds(base, tt), :] = (dq * uv * sg * (1.0 - sg)).astype(BF)
            return carry

        lax.fori_loop(0, S // tt, step, 0)
        dw_ref[...] = jnp.zeros_like(dw_ref)
        for k in range(CONV_WIDTH):
            dw_ref[k:k + 1, :] = _colsum(dw8[k])
        db_ref[...] = _colsum(db8[...])

    col = lambda o: pl.BlockSpec((S, LANES), lambda c: (0, c + o))
    return _pcall(
        body, name=name, grid=(nb,),
        in_specs=[col(0), col(0), col(nb), pl.BlockSpec((None, 32, LANES), lambda c: (c, 0, 0))],
        out_specs=[col(0), col(0), pl.BlockSpec((None, 32, LANES), lambda c: (c, 0, 0)),
                   pl.BlockSpec((1, LANES), lambda c: (0, c))],
        out_shape=[jax.ShapeDtypeStruct((S, M), BF), jax.ShapeDtypeStruct((S, M), BF),
                   jax.ShapeDtypeStruct((nb, 32, LANES), F32), jax.ShapeDtypeStruct((1, M), F32)],
        scratch_shapes=[pltpu.VMEM((S + CONV_PAD, LANES), F32), pltpu.VMEM((S + CONV_PAD, LANES), F32),
                        pltpu.VMEM((32, SUBLANES, LANES), F32), pltpu.VMEM((SUBLANES, LANES), F32)],
        compiler_params=_params(("arbitrary",)),
    )(dcq, proj, proj, convw4)


def _log_sigmoid(x):
    return jnp.minimum(x, 0.0) - jnp.log(1.0 + jnp.exp(-jnp.abs(x)))


def _rg_gate_terms(ra, ls):
    la = RG_C * ra * ls
    a = jnp.exp(la)
    th = jnp.tanh(la)
    mult = jnp.sqrt(-2.0 * th / (1.0 - th))
    return a, mult


def rnn_fwd(proj, rnnw4, rnn_b, bda, bdi, b_a, b_i, lam, name):
    S = proj.shape[0]
    M = rnn_b.shape[1]
    nb = M // LANES
    tt = min(SEQ_TT, S)
    KW = RNN_CONV_WIDTH
    nseg = SCAN_SEGMENTS if S % (SCAN_SEGMENTS * tt) == 0 else 1

    def body(ux_ref, w_ref, rb_ref, bda_ref, bdi_ref, ba_ref, bi_ref, lam_ref,
             xr_ref, ra_ref, ii_ref, h_ref, uxp, a_sc, b_sc):
        uxp[0:SUBLANES, :] = jnp.zeros((SUBLANES, LANES), F32)
        ls = _log_sigmoid(lam_ref[...])

        def step(t, carry):
            base = pl.multiple_of(t * tt, tt)
            uxp[pl.ds(base + SUBLANES, tt), :] = ux_ref[pl.ds(base, tt), :]
            xr = jnp.broadcast_to(rb_ref[...], (tt, LANES))
            for k in range(KW):
                xr = xr + w_ref[k:k + 1, :] * uxp[pl.ds(base + (SUBLANES - KW + 1) + k, tt), :]
            xb = xr.astype(BF)
            ra = _sigmoid(jnp.dot(xb, bda_ref[...], preferred_element_type=F32) + ba_ref[...])
            ii = _sigmoid(jnp.dot(xb, bdi_ref[...], preferred_element_type=F32) + bi_ref[...])
            a, mult = _rg_gate_terms(ra, ls)
            xr_ref[pl.ds(base, tt), :] = xr
            ra_ref[pl.ds(base, tt), :] = ra
            ii_ref[pl.ds(base, tt), :] = ii
            a_sc[pl.ds(base, tt), :] = a
            b_sc[pl.ds(base, tt), :] = mult * (ii * xr)
            return carry

        lax.fori_loop(0, S // tt, step, 0)

        rows = lax.broadcasted_iota(I32, (SUBLANES, LANES), 0)
        seg = S // nseg
        last = lambda v: jnp.broadcast_to(v[SUBLANES - 1:SUBLANES, :], (SUBLANES, LANES))

        def scan(t, carry):
            hs, ps = carry
            new_h, new_p = [], []
            for s in range(nseg):
                base = pl.multiple_of(s * seg + t * SUBLANES, SUBLANES)
                A = a_sc[pl.ds(base, SUBLANES), :]
                B = b_sc[pl.ds(base, SUBLANES), :]
                for d in (1, 2, 4):
                    As = jnp.where(rows >= d, pltpu.roll(A, d, axis=0), 1.0)
                    Bs = jnp.where(rows >= d, pltpu.roll(B, d, axis=0), 0.0)
                    B = A * Bs + B
                    A = A * As
                hh = B + A * hs[s]
                h_ref[pl.ds(base, SUBLANES), :] = hh
                pp = A * ps[s]
                if s > 0:
                    a_sc[pl.ds(base, SUBLANES), :] = pp
                new_h.append(last(hh))
                new_p.append(last(pp))
            return tuple(new_h), tuple(new_p)

        zero8 = jnp.zeros((SUBLANES, LANES), F32)
        one8 = jnp.ones((SUBLANES, LANES), F32)
        hs, ps = lax.fori_loop(0, seg // SUBLANES, scan, ((zero8,) * nseg, (one8,) * nseg))
        carry_in = hs[0]
        for s in range(1, nseg):
            c_row = carry_in[0:1, :]

            def fix(t, c, s=s, c_row=c_row):
                base = pl.multiple_of(s * seg + t * tt, tt)
                h_ref[pl.ds(base, tt), :] = h_ref[pl.ds(base, tt), :] + a_sc[pl.ds(base, tt), :] * c_row
                return c

            lax.fori_loop(0, seg // tt, fix, 0)
            carry_in = hs[s] + ps[s] * carry_in

    col = lambda o: pl.BlockSpec((S, LANES), lambda c: (0, c + o))
    vec = pl.BlockSpec((1, LANES), lambda c: (0, c))
    diag = pl.BlockSpec((LANES, LANES), lambda c: (c, c))
    return _pcall(
        body, name=name, grid=(nb,),
        in_specs=[col(2 * nb), pl.BlockSpec((None, SUBLANES, LANES), lambda c: (c, 0, 0)), vec, diag, diag, vec, vec, vec],
        out_specs=[col(0)] * 4,
        out_shape=[jax.ShapeDtypeStruct((S, M), F32)] * 4,
        scratch_shapes=[pltpu.VMEM((S + SUBLANES, LANES), F32), pltpu.VMEM((S, LANES), F32), pltpu.VMEM((S, LANES), F32)],
        compiler_params=_params(("arbitrary",)),
    )(proj, rnnw4, rnn_b, bda, bdi, b_a, b_i, lam)


def rnn_bwd(dhout, h, xr, ra, ii, proj, rnnw4, bda, bdi, lam, name):
    S, M = h.shape
    nb = M // LANES
    tt = min(SEQ_TT, S)
    KW = RNN_CONV_WIDTH
    SL = SUBLANES
    nseg = SCAN_SEGMENTS if S % (SCAN_SEGMENTS * tt) == 0 else 1

    def body(dh_ref, h_ref, xr_ref, ra_ref, ii_ref, ux_ref, w_ref, bda_ref, bdi_ref, lam_ref,
             dux_ref, dwa_ref, dwi_ref, drw_ref, vec_ref,
             a_sc, hp, g_sc, dpa_sc, dpi_sc, dxp, uxp, acc8, drw8, p_sc):
        zero8 = jnp.zeros((SL, LANES), F32)
        a_sc[S:S + SL, :] = zero8
        hp[0:SL, :] = zero8
        dxp[S:S + SL, :] = zero8
        uxp[0:SL, :] = zero8
        acc8[...] = jnp.zeros_like(acc8)
        drw8[...] = jnp.zeros_like(drw8)
        lamv = lam_ref[...]
        ls = _log_sigmoid(lamv)

        def fill(t, carry):
            base = pl.multiple_of(t * tt, tt)
            a_sc[pl.ds(base, tt), :] = jnp.exp(RG_C * ra_ref[pl.ds(base, tt), :] * ls)
            hp[pl.ds(base + SL, tt), :] = h_ref[pl.ds(base, tt), :]
            uxp[pl.ds(base + SL, tt), :] = ux_ref[pl.ds(base, tt), :]
            return carry

        lax.fori_loop(0, S // tt, fill, 0)

        rows = lax.broadcasted_iota(I32, (SL, LANES), 0)
        seg = S // nseg
        nt8 = seg // SL
        first = lambda v: jnp.broadcast_to(v[0:1, :], (SL, LANES))

        def rscan(t, carry):
            gs, ps = carry
            new_g, new_p = [], []
            for s in range(nseg):
                base = pl.multiple_of(s * seg + (nt8 - 1 - t) * SL, SL)
                A = a_sc[pl.ds(base + 1, SL), :]
                B = dh_ref[pl.ds(base, SL), :]
                for d in (1, 2, 4):
                    As = jnp.where(rows < SL - d, pltpu.roll(A, SL - d, axis=0), 1.0)
                    Bs = jnp.where(rows < SL - d, pltpu.roll(B, SL - d, axis=0), 0.0)
                    B = A * Bs + B
                    A = A * As
                g = B + A * gs[s]
                g_sc[pl.ds(base, SL), :] = g
                pp = A * ps[s]
                if s < nseg - 1:
                    p_sc[pl.ds(base, SL), :] = pp
                new_g.append(first(g))
                new_p.append(first(pp))
            return tuple(new_g), tuple(new_p)

        one8 = jnp.ones((SL, LANES), F32)
        gs, ps = lax.fori_loop(0, nt8, rscan, ((zero8,) * nseg, (one8,) * nseg))
        carry_in = gs[nseg - 1]
        for s in range(nseg - 2, -1, -1):
            c_row = carry_in[0:1, :]

            def fix(t, c, s=s, c_row=c_row):
                base = pl.multiple_of(s * seg + t * tt, tt)
                g_sc[pl.ds(base, tt), :] = g_sc[pl.ds(base, tt), :] + p_sc[pl.ds(base, tt), :] * c_row
                return c

            lax.fori_loop(0, seg // tt, fix, 0)
            carry_in = gs[s] + ps[s] * carry_in

        def red8(v):
            return v.reshape(tt // SL, SL, LANES).sum(axis=0)

        def step(t, carry):
            base = pl.multiple_of(t * tt, tt)
            g = g_sc[pl.ds(base, tt), :]
            hprev = hp[pl.ds(base + SL - 1, tt), :]
            xr_t = xr_ref[pl.ds(base, tt), :]
            ra_t = ra_ref[pl.ds(base, tt), :]
            ii_t = ii_ref[pl.ds(base, tt), :]
            a, mult = _rg_gate_terms(ra_t, ls)
            gx = g * xr_t
            dmult = gx * ii_t
            dii = gx * mult
            dxr = g * (mult * ii_t)
            dla = g * hprev * a - dmult * (a * a) / mult
            acc8[3] += red8(dla * ra_t)
            dpa = dla * (RG_C * ls) * ra_t * (1.0 - ra_t)
            dpi = dii * ii_t * (1.0 - ii_t)
            dpab = dpa.astype(BF)
            dpib = dpi.astype(BF)
            dxr = dxr + (lax.dot_general(dpab, bda_ref[...], CONTRACT_LAST, preferred_element_type=F32)
                         + lax.dot_general(dpib, bdi_ref[...], CONTRACT_LAST, preferred_element_type=F32))
            dpa_sc[pl.ds(base, tt), :] = dpab
            dpi_sc[pl.ds(base, tt), :] = dpib
            dxp[pl.ds(base, tt), :] = dxr
            acc8[0] += red8(dxr)
            acc8[1] += red8(dpa)
            acc8[2] += red8(dpi)
            return carry

        lax.fori_loop(0, S // tt, step, 0)

        def convb(t, carry):
            base = pl.multiple_of(t * tt, tt)
            d_t = dxp[pl.ds(base, tt), :]
            dux = jnp.zeros((tt, LANES), F32)
            for k in range(KW):
                drw8[k] += red8(d_t * uxp[pl.ds(base + (SL - KW + 1) + k, tt), :])
                dux = dux + w_ref[k:k + 1, :] * dxp[pl.ds(base + (KW - 1) - k, tt), :]
            dux_ref[pl.ds(base, tt), :] = dux.astype(BF)
            return carry

        lax.fori_loop(0, S // tt, convb, 0)

        xb = xr_ref[...].astype(BF)
        dwa_ref[...] = lax.dot_general(xb, dpa_sc[...], CONTRACT_FIRST, preferred_element_type=F32)
        dwi_ref[...] = lax.dot_general(xb, dpi_sc[...], CONTRACT_FIRST, preferred_element_type=F32)
        drw_ref[...] = jnp.zeros_like(drw_ref)
        vec_ref[...] = jnp.zeros_like(vec_ref)
        for k in range(KW):
            drw_ref[k:k + 1, :] = _colsum(drw8[k])
        for k in range(3):
            vec_ref[k:k + 1, :] = _colsum(acc8[k])
        vec_ref[3:4, :] = _colsum(acc8[3]) * (RG_C * _sigmoid(-lamv))

    col = lambda o: pl.BlockSpec((S, LANES), lambda c: (0, c + o))
    vec = pl.BlockSpec((1, LANES), lambda c: (0, c))
    diag = pl.BlockSpec((LANES, LANES), lambda c: (c, c))
    blk3 = lambda r: pl.BlockSpec((None, r, LANES), lambda c: (c, 0, 0))
    return _pcall(
        body, name=name, grid=(nb,),
        in_specs=[col(0), col(0), col(0), col(0), col(0), col(2 * nb), blk3(SL), diag, diag, vec],
        out_specs=[col(0), blk3(LANES), blk3(LANES), blk3(SL), pl.BlockSpec((SL, LANES), lambda c: (0, c))],
        out_shape=[jax.ShapeDtypeStruct((S, M), BF), jax.ShapeDtypeStruct((nb, LANES, LANES), F32),
                   jax.ShapeDtypeStruct((nb, LANES, LANES), F32), jax.ShapeDtypeStruct((nb, SL, LANES), F32),
                   jax.ShapeDtypeStruct((SL, M), F32)],
        scratch_shapes=[pltpu.VMEM((S + SL, LANES), F32), pltpu.VMEM((S + SL, LANES), F32), pltpu.VMEM((S, LANES), F32),
                        pltpu.VMEM((S, LANES), BF), pltpu.VMEM((S, LANES), BF), pltpu.VMEM((S + SL, LANES), F32),
                        pltpu.VMEM((S + SL, LANES), F32), pltpu.VMEM((SL, SL, LANES), F32), pltpu.VMEM((SL, SL, LANES), F32),
                        pltpu.VMEM((S, LANES), F32)],
        compiler_params=_params(("arbitrary",)),
    )(dhout, h, xr, ra, ii, proj, rnnw4, bda, bdi, lam)


GELU_K = 0.7978845608028654
GELU_C = 0.044715


def _layernorm_parts(cq):
    mu = _rowmean(cq)
    d = cq - mu
    rstd = lax.rsqrt(_rowmean(d * d) + EPS)
    return d * rstd, rstd


def mix_out(cq, proj, h, x, vecs, lnv, wout, name):
    S, D = x.shape
    M = cq.shape[1]
    ts = min(512, S)

    def body(cq_ref, uy_ref, h_ref, x_ref, v_ref, ln_ref, w_ref, xo_ref, ym_ref, yc_ref):
        z, _ = _layernorm_parts(cq_ref[...])
        l = z * _row(ln_ref, 0) + _row(ln_ref, 1)
        yc_ref[:, 0:M] = (l * _sigmoid(l)).astype(BF)
        uy = uy_ref[...]
        gelu = 0.5 * uy * (1.0 + jnp.tanh(GELU_K * (uy + GELU_C * uy * uy * uy)))
        yc_ref[:, M:2 * M] = (gelu * h_ref[...]).astype(BF)
        ym = jnp.dot(yc_ref[...], w_ref[...], preferred_element_type=F32)
        ym_ref[...] = ym
        xo_ref[...] = x_ref[...] + _row(v_ref, R_GT2) * ym

    tok = pl.BlockSpec((ts, D), lambda i: (i, 0))
    mtok = lambda o: pl.BlockSpec((ts, M), lambda i: (i, o))
    return _pcall(
        body, name=name, grid=(S // ts,),
        in_specs=[mtok(0), mtok(3), mtok(0), tok, pl.BlockSpec(vecs.shape, lambda i: (0, 0)),
                  pl.BlockSpec(lnv.shape, lambda i: (0, 0)), pl.BlockSpec(wout.shape, lambda i: (0, 0))],
        out_specs=[tok, tok, pl.BlockSpec((ts, 2 * M), lambda i: (i, 0))],
        out_shape=[jax.ShapeDtypeStruct((S, D), F32), jax.ShapeDtypeStruct((S, D), F32),
                   jax.ShapeDtypeStruct((S, 2 * M), BF)],
        compiler_params=_params(("arbitrary",)),
    )(cq, proj, h, x, vecs, lnv, wout)


def mix_out_bwd(dxo, ym, vecs, wout, cq, lnv, proj, h, name):
    S, D = dxo.shape
    M = cq.shape[1]
    ts = min(512, S)

    def body(dxo_ref, ym_ref, v_ref, w_ref, cq_ref, ln_ref, uy_ref, h_ref,
             dcq_ref, dh_ref, duy_ref, dyb_ref, vgd_ref, vgm_ref):
        @pl.when(pl.program_id(0) == 0)
        def _():
            vgd_ref[...] = jnp.zeros_like(vgd_ref)
            vgm_ref[...] = jnp.zeros_like(vgm_ref)

        dxo_v = dxo_ref[...]
        dyb = (_row(v_ref, R_GT2) * dxo_v).astype(BF)
        dyb_ref[...] = dyb
        vgd_ref[0:1, :] += _colsum(dxo_v * ym_ref[...])
        dycat = lax.dot_general(dyb, w_ref[...], CONTRACT_LAST, preferred_element_type=F32)
        dyc = dycat[:, 0:M]
        dyr = dycat[:, M:2 * M]
        z, rstd = _layernorm_parts(cq_ref[...])
        lng = _row(ln_ref, 0)
        l = z * lng + _row(ln_ref, 1)
        sl = _sigmoid(l)
        dl = dyc * (sl * (1.0 + l * (1.0 - sl)))
        vgm_ref[0:1, :] += _colsum(dl * z)
        vgm_ref[1:2, :] += _colsum(dl)
        dz = dl * lng
        dcq_ref[...] = rstd * (dz - _rowmean(dz) - z * _rowmean(dz * z))
        uy = uy_ref[...]
        u2 = uy * uy
        th = jnp.tanh(GELU_K * (uy + GELU_C * uy * u2))
        gelu = 0.5 * uy * (1.0 + th)
        dgelu = 0.5 * (1.0 + th) + 0.5 * uy * (1.0 - th * th) * (GELU_K * (1.0 + 3.0 * GELU_C * u2))
        dh_ref[...] = dyr * gelu
        duy_ref[...] = (dyr * h_ref[...] * dgelu).astype(BF)

    tok = pl.BlockSpec((ts, D), lambda i: (i, 0))
    mtok = lambda o: pl.BlockSpec((ts, M), lambda i: (i, o))
    return _pcall(
        body, name=name, grid=(S // ts,),
        in_specs=[tok, tok, pl.BlockSpec(vecs.shape, lambda i: (0, 0)), pl.BlockSpec(wout.shape, lambda i: (0, 0)),
                  mtok(0), pl.BlockSpec(lnv.shape, lambda i: (0, 0)), mtok(3), mtok(0)],
        out_specs=[mtok(0), mtok(0), mtok(0), tok, pl.BlockSpec((SUBLANES, D), lambda i: (0, 0)),
                   pl.BlockSpec((SUBLANES, M), lambda i: (0, 0))],
        out_shape=[jax.ShapeDtypeStruct((S, M), F32)] * 2 + [jax.ShapeDtypeStruct((S, M), BF), jax.ShapeDtypeStruct((S, D), BF),
                   jax.ShapeDtypeStruct((SUBLANES, D), F32), jax.ShapeDtypeStruct((SUBLANES, M), F32)],
        compiler_params=_params(("arbitrary",)),
    )(dxo, ym, vecs, wout, cq, lnv, proj, h)


def mix_in_bwd(dparts, x, dxo, vecs, win, name):
    S, D = x.shape
    M = dparts[0].shape[1]
    ts = min(512, S)

    def body(d0, d1, d2, d3, x_ref, dxo_ref, v_ref, w_ref, dx_ref, hb_ref, dp_ref, vg_ref):
        @pl.when(pl.program_id(0) == 0)
        def _():
            vg_ref[...] = jnp.zeros_like(vg_ref)

        for q, dref in enumerate((d0, d1, d2, d3)):
            dp_ref[:, q * M:(q + 1) * M] = dref[...].astype(BF)
        dh = lax.dot_general(dp_ref[...], w_ref[...], CONTRACT_LAST, preferred_element_type=F32)
        xv = x_ref[...]
        r = lax.rsqrt(_rowmean(xv * xv) + EPS)
        n = xv * r
        g = _row(v_ref, R_G2)
        sc1 = 1.0 + _row(v_ref, R_SC2)
        gsc = g * sc1
        hb_ref[...] = (n * gsc + _row(v_ref, R_SH2)).astype(BF)
        dhn = dh * n
        vg_ref[0:1, :] += _colsum(dh)
        vg_ref[1:2, :] += _colsum(dhn) * g
        vg_ref[2:3, :] += _colsum(dhn) * sc1
        dn = dh * gsc
        dx_ref[...] = dxo_ref[...] + r * (dn - n * _rowmean(dn * n))

    tok = pl.BlockSpec((ts, D), lambda i: (i, 0))
    mtok = pl.BlockSpec((ts, M), lambda i: (i, 0))
    return _pcall(
        body, name=name, grid=(S // ts,),
        in_specs=[mtok] * 4 + [tok, tok, pl.BlockSpec(vecs.shape, lambda i: (0, 0)), pl.BlockSpec(win.shape, lambda i: (0, 0))],
        out_specs=[tok, tok, pl.BlockSpec((ts, 4 * M), lambda i: (i, 0)), pl.BlockSpec((SUBLANES, D), lambda i: (0, 0))],
        out_shape=[jax.ShapeDtypeStruct((S, D), F32), jax.ShapeDtypeStruct((S, D), BF),
                   jax.ShapeDtypeStruct((S, 4 * M), BF), jax.ShapeDtypeStruct((SUBLANES, D), F32)],
        compiler_params=_params(("arbitrary",)),
    )(*dparts, x, dxo, vecs, win)


def _adamw(w, g, m, v):
    m = ADAM_B1 * m + (1.0 - ADAM_B1) * g
    v = ADAM_B2 * v + (1.0 - ADAM_B2) * (g * g)
    m_hat = m / (1.0 - ADAM_B1 ** ADAM_STEP)
    v_hat = v / (1.0 - ADAM_B2 ** ADAM_STEP)
    delta = -ADAM_LR * (m_hat / (jnp.sqrt(v_hat) + ADAM_EPS) + ADAM_WD * w)
    return delta, m, v


def adam_big(w, g, m, v, name):
    R, C = w.shape
    tr = 256 if R % 256 == 0 else R // 2 if (R // 2) % SUBLANES == 0 and R > 512 else R
    tc = C if C <= 1536 else (1152 if C % 1152 == 0 else 1024)
    assert R % tr == 0 and C % tc == 0

    def body(w_ref, g_ref, m_ref, v_ref, d_ref, nm_ref, nv_ref):
        d, nm, nv = _adamw(w_ref[...], g_ref[...], m_ref[...], v_ref[...])
        d_ref[...] = d
        nm_ref[...] = nm
        nv_ref[...] = nv

    blk = pl.BlockSpec((tr, tc), lambda i, j: (i, j))
    return _pcall(
        body, name=name, grid=(R // tr, C // tc), in_specs=[blk] * 4, out_specs=[blk] * 3,
        out_shape=[jax.ShapeDtypeStruct((R, C), F32)] * 3, compiler_params=_params(("parallel", "parallel")),
    )(w, g, m, v)


def adam_cond(c_all, dmod, w, m, v, name):
    B, Kin = c_all.shape
    N = w.shape[1]
    tn = 768 if N % 768 == 0 else 256
    assert N % tn == 0

    def body(c_ref, d_ref, w_ref, m_ref, v_ref, g_ref, dl_ref, nm_ref, nv_ref):
        cv = c_ref[...]
        ca = cv * _sigmoid(cv)
        g = lax.dot_general(ca.astype(BF), d_ref[...].astype(BF), CONTRACT_FIRST, preferred_element_type=F32)
        d, nm, nv = _adamw(w_ref[...], g, m_ref[...], v_ref[...])
        g_ref[...] = g
        dl_ref[...] = d
        nm_ref[...] = nm
        nv_ref[...] = nv

    blk = pl.BlockSpec((Kin, tn), lambda n: (0, n))
    return _pcall(
        body, name=name, grid=(N // tn,),
        in_specs=[pl.BlockSpec((B, Kin), lambda n: (0, 0)), pl.BlockSpec((B, tn), lambda n: (0, n)), blk, blk, blk],
        out_specs=[blk] * 4, out_shape=[jax.ShapeDtypeStruct((Kin, N), F32)] * 4,
        compiler_params=_params(("parallel",)),
    )(c_all, dmod, w, m, v)


def adam_small(ws, gs, ms, vs, name):
    n = len(ws)

    def body(*refs):
        ins, outs = refs[:4 * n], refs[4 * n:]
        for k in range(n):
            d, nm, nv = _adamw(ins[k][...], ins[n + k][...], ins[2 * n + k][...], ins[3 * n + k][...])
            outs[k][...] = d
            outs[n + k][...] = nm
            outs[2 * n + k][...] = nv

    specs = [pl.BlockSpec(w.shape, lambda i: (0, 0)) for w in ws]
    shapes = [jax.ShapeDtypeStruct(w.shape, F32) for w in ws]
    out = _pcall(body, name=name, grid=(1,), in_specs=specs * 4, out_specs=specs * 3, out_shape=shapes * 3,
                 compiler_params=_params(("arbitrary",)))(*ws, *gs, *ms, *vs)
    return out[:n], out[n:2 * n], out[2 * n:]


def _me():
    return lax.axis_index("x"), lax.axis_index("y"), lax.axis_index("c")


def _flip(x, y, p):
    return (x ^ (p >> 1) if (p >> 1) else x), (y ^ (p & 1) if (p & 1) else y)


def _handshake(peers):
    barrier = pltpu.get_barrier_semaphore()
    for peer in peers:
        pl.semaphore_signal(barrier, inc=1, device_id=peer, device_id_type=MESH)
    pl.semaphore_wait(barrier, len(peers))


def _seq_call(body, *, name, n_in, out_shape, sem_shapes, collective_id):
    del n_in
    return pl.kernel(body, out_type=out_shape, mesh=plsc.ScalarSubcoreMesh(axis_name="sq", num_cores=1), name=name,
                     scratch_types=sem_shapes, compiler_params=pltpu.CompilerParams(collective_id=collective_id))


def _hbm_comm_call(body, *, name, n_in, out_shape, sem_shapes, seq_id):
    if seq_id is not None:
        return _seq_call(body, name=name, n_in=n_in, out_shape=out_shape, sem_shapes=sem_shapes, collective_id=seq_id)
    anyspec = pl.BlockSpec(memory_space=pl.ANY)
    return _pcall(body, name=name, in_specs=[anyspec] * n_in, out_specs=[anyspec] * len(out_shape), out_shape=out_shape,
                  scratch_shapes=sem_shapes, compiler_params=_params())


def allgather_devices(v, name, with_sum=False):
    R, L = v.shape

    def body(v_ref, out_ref, *rest):
        if with_sum:
            sum_ref, send_sems, recv_sems = rest
        else:
            send_sems, recv_sems = rest
        x, y, c = _me()
        me = 4 * x + 2 * y + c
        out_ref[me] = v_ref[...]
        copies = []
        for p in range(1, N_DEV):
            px, py = _flip(x, y, p >> 1)
            pc = (1 - c) if (p & 1) else c
            peer = 4 * px + 2 * py + pc
            send = pltpu.make_async_remote_copy(src_ref=v_ref, dst_ref=out_ref.at[me], send_sem=send_sems.at[p - 1],
                                                recv_sem=recv_sems.at[p - 1], device_id=(px, py, pc), device_id_type=MESH)
            send.start()
            recv = pltpu.make_async_remote_copy(src_ref=v_ref, dst_ref=out_ref.at[peer], send_sem=send_sems.at[p - 1],
                                                recv_sem=recv_sems.at[p - 1], device_id=(px, py, pc), device_id_type=MESH)
            copies.append((send, recv))
        for send, recv in copies:
            recv.wait_recv()
        for send, recv in copies:
            send.wait_send()
        if with_sum:
            s = out_ref[0]
            for k in range(1, N_DEV):
                s = s + out_ref[k]
            sum_ref[...] = s

    vm = pl.BlockSpec(memory_space=pltpu.VMEM)
    out_shape = [jax.ShapeDtypeStruct((N_DEV, R, L), F32)]
    if with_sum:
        out_shape.append(jax.ShapeDtypeStruct((R, L), F32))
    return _pcall(
        body, name=name, in_specs=[vm], out_specs=[vm] * len(out_shape), out_shape=out_shape,
        scratch_shapes=[pltpu.SemaphoreType.DMA((N_DEV - 1,)), pltpu.SemaphoreType.DMA((N_DEV - 1,))],
        compiler_params=_params(),
    )(v)


def allgather_devices_hbm(v, name, seq_id):
    R, L = v.shape

    def body(v_ref, out_ref, send_sems, recv_sems, local_sem):
        x, y, c = _me()
        me = 4 * x + 2 * y + c
        peers = []
        for p in range(1, N_DEV):
            px, py = _flip(x, y, p >> 1)
            peers.append((px, py, (1 - c) if (p & 1) else c))
        _handshake(peers)
        lc = pltpu.make_async_copy(v_ref, out_ref.at[me], local_sem)
        lc.start()
        copies = []
        for p, (px, py, pc) in enumerate(peers):
            send = pltpu.make_async_remote_copy(src_ref=v_ref, dst_ref=out_ref.at[me], send_sem=send_sems.at[p],
                                                recv_sem=recv_sems.at[p], device_id=(px, py, pc), device_id_type=MESH)
            send.start()
            recv = pltpu.make_async_remote_copy(src_ref=v_ref, dst_ref=out_ref.at[4 * px + 2 * py + pc], send_sem=send_sems.at[p],
                                                recv_sem=recv_sems.at[p], device_id=(px, py, pc), device_id_type=MESH)
            copies.append((send, recv))
        for send, recv in copies:
            recv.wait_recv()
        for send, recv in copies:
            send.wait_send()
        lc.wait()

    return _seq_call(body, name=name, n_in=1, out_shape=[jax.ShapeDtypeStruct((N_DEV, R, L), F32)],
                     sem_shapes=[pltpu.SemaphoreType.DMA((N_DEV - 1,)), pltpu.SemaphoreType.DMA((N_DEV - 1,)),
                                 pltpu.SemaphoreType.DMA], collective_id=seq_id)(v)[0]


def sum_slots(g, name):
    n, R, L = g.shape
    tr = 216 if R % 216 == 0 else R
    assert R % tr == 0 and tr % SUBLANES == 0

    def body(g_ref, o_ref):
        s = g_ref[0]
        for k in range(1, n):
            s = s + g_ref[k]
        o_ref[...] = s

    return _pcall(body, name=name, grid=(R // tr,), in_specs=[pl.BlockSpec((n, tr, L), lambda i: (0, i, 0))],
                  out_specs=pl.BlockSpec((tr, L), lambda i: (i, 0)), out_shape=jax.ShapeDtypeStruct((R, L), F32),
                  compiler_params=_params(("parallel",)))(g)


def allgather_chips(v, name):
    R, L = v.shape

    def body(v_ref, out_ref, send_sems, recv_sems):
        x, y, c = _me()
        chip = 2 * x + y
        out_ref[chip] = v_ref[...]
        copies = []
        for p in range(1, N_CHIPS):
            px, py = _flip(x, y, p)
            send = pltpu.make_async_remote_copy(src_ref=v_ref, dst_ref=out_ref.at[chip], send_sem=send_sems.at[p - 1],
                                                recv_sem=recv_sems.at[p - 1], device_id=(px, py, c), device_id_type=MESH)
            send.start()
            recv = pltpu.make_async_remote_copy(src_ref=v_ref, dst_ref=out_ref.at[2 * px + py], send_sem=send_sems.at[p - 1],
                                                recv_sem=recv_sems.at[p - 1], device_id=(px, py, c), device_id_type=MESH)
            copies.append((send, recv))
        for send, recv in copies:
            recv.wait_recv()
        for send, recv in copies:
            send.wait_send()

    vm = pl.BlockSpec(memory_space=pltpu.VMEM)
    return _pcall(
        body, name=name, in_specs=[vm], out_specs=vm, out_shape=jax.ShapeDtypeStruct((N_CHIPS, R, L), F32),
        scratch_shapes=[pltpu.SemaphoreType.DMA((N_CHIPS - 1,)), pltpu.SemaphoreType.DMA((N_CHIPS - 1,))],
        compiler_params=_params(),
    )(v)


def _shard_window(ref, kind, shard_shape, chip, half):
    r, c = shard_shape
    hr = r // 2
    if kind == "col":
        return ref.at[pl.ds(pl.multiple_of(half * hr, hr), hr), pl.ds(pl.multiple_of(chip * c, c), c)]
    return ref.at[pl.ds(pl.multiple_of(chip * r + half * hr, hr), hr), :]


def allgather_weights(shards, kinds, name, seq_id=None):
    n = len(shards)
    fulls = []
    for s, kind in zip(shards, kinds):
        r, c = s.shape
        fulls.append(jax.ShapeDtypeStruct((r, N_CHIPS * c) if kind == "col" else (N_CHIPS * r, c), s.dtype))

    def body(*refs):
        srcs, outs = refs[:n], refs[n:2 * n]
        send_sems, recv_sems, fsend_sems, frecv_sems = refs[2 * n:]
        x, y, c = _me()
        chip = 2 * x + y
        sib = (x, y, 1 - c)
        if seq_id is not None:
            _handshake([(*_flip(x, y, p), c) for p in range(1, N_CHIPS)] + [sib])
        sends, fwds = [], []
        for i in range(n):
            shp = srcs[i].shape
            hr = shp[0] // 2
            my_half = srcs[i].at[pl.ds(pl.multiple_of(c * hr, hr), hr), :]
            for p in range(1, N_CHIPS):
                px, py = _flip(x, y, p)
                k = i * (N_CHIPS - 1) + p - 1
                cp = pltpu.make_async_remote_copy(src_ref=my_half, dst_ref=_shard_window(outs[i], kinds[i], shp, chip, c),
                                                  send_sem=send_sems.at[k], recv_sem=recv_sems.at[k],
                                                  device_id=(px, py, c), device_id_type=MESH)
                cp.start()
                sends.append(cp)
        for i in range(n):
            shp = srcs[i].shape
            for p in range(1, N_CHIPS):
                px, py = _flip(x, y, p)
                k = i * (N_CHIPS - 1) + p - 1
                landed = _shard_window(outs[i], kinds[i], shp, 2 * px + py, c)
                pltpu.make_async_remote_copy(src_ref=landed, dst_ref=landed, send_sem=send_sems.at[k], recv_sem=recv_sems.at[k],
                                             device_id=(px, py, c), device_id_type=MESH).wait_recv()
                fw = pltpu.make_async_remote_copy(src_ref=landed, dst_ref=landed, send_sem=fsend_sems.at[k],
                                                  recv_sem=frecv_sems.at[k], device_id=sib, device_id_type=MESH)
                fw.start()
                fwds.append(fw)
        for i in range(n):
            shp = srcs[i].shape
            for p in range(1, N_CHIPS):
                px, py = _flip(x, y, p)
                k = i * (N_CHIPS - 1) + p - 1
                other = _shard_window(outs[i], kinds[i], shp, 2 * px + py, 1 - c)
                pltpu.make_async_remote_copy(src_ref=other, dst_ref=other, send_sem=fsend_sems.at[k], recv_sem=frecv_sems.at[k],
                                             device_id=sib, device_id_type=MESH).wait_recv()
        for cp in sends + fwds:
            cp.wait_send()

    nk = n * (N_CHIPS - 1)
    gathered = _hbm_comm_call(
        body, name=name, n_in=n, out_shape=fulls, seq_id=seq_id,
        sem_shapes=[pltpu.SemaphoreType.DMA((nk,)), pltpu.SemaphoreType.DMA((nk,)), pltpu.SemaphoreType.DMA((nk,)),
                    pltpu.SemaphoreType.DMA((nk,))],
    )(*shards)
    return gathered


def place_local_shards(fulls, shards, kinds, name):
    n = len(shards)
    chip = jnp.reshape(2 * lax.axis_index("x") + lax.axis_index("y"), (1,)).astype(I32)

    def body(ci_ref, *refs):
        for i in range(n):
            refs[2 * n + i][...] = refs[i][...]

    in_specs = [pl.BlockSpec(s.shape, lambda i, ci: (0, 0)) for s in shards] + [pl.BlockSpec(memory_space=pl.ANY)] * n
    out_specs = [pl.BlockSpec(s.shape, (lambda i, ci: (0, ci[0])) if k == "col" else (lambda i, ci: (ci[0], 0)))
                 for s, k in zip(shards, kinds)]
    gs = pltpu.PrefetchScalarGridSpec(num_scalar_prefetch=1, grid=(1,), in_specs=in_specs, out_specs=out_specs)
    return _pcall(body, name=name, grid_spec=gs, out_shape=[jax.ShapeDtypeStruct(f.shape, f.dtype) for f in fulls],
                  input_output_aliases={1 + n + i: i for i in range(n)}, compiler_params=_params(("arbitrary",)))(chip, *shards, *fulls)


def _as_halves(g, kind, shard_shape):
    r, c = shard_shape
    if kind == "col":
        return g.reshape(2, r // 2, N_CHIPS * c)
    return g.reshape(N_CHIPS, 2, r // 2, c)


def exchange_sibling_halves(grads, kinds, shard_shapes, name, seq_id=None):
    n = len(grads)
    views = [_as_halves(g, k, s) for g, k, s in zip(grads, kinds, shard_shapes)]
    outs = []
    for k, (r, c) in zip(kinds, shard_shapes):
        outs.append(jax.ShapeDtypeStruct((r // 2, N_CHIPS * c) if k == "col" else (N_CHIPS, r // 2, c), grads[0].dtype))

    def body(*refs):
        srcs, dsts = refs[:n], refs[n:2 * n]
        send_sems, recv_sems = refs[2 * n:]
        x, y, c = _me()
        if seq_id is not None:
            _handshake([(x, y, 1 - c)])
        cps = []
        for i in range(n):
            src = srcs[i].at[1 - c] if kinds[i] == "col" else srcs[i].at[:, 1 - c]
            cp = pltpu.make_async_remote_copy(src_ref=src, dst_ref=dsts[i], send_sem=send_sems.at[i], recv_sem=recv_sems.at[i],
                                              device_id=(x, y, 1 - c), device_id_type=MESH)
            cp.start()
            cps.append(cp)
        for cp in cps:
            cp.wait_recv()
        for cp in cps:
            cp.wait_send()

    return _hbm_comm_call(body, name=name, n_in=n, out_shape=outs, seq_id=seq_id,
                          sem_shapes=[pltpu.SemaphoreType.DMA((n,)), pltpu.SemaphoreType.DMA((n,))])(*views)


def add_sibling_half(g, recv, kind, shard_shape, cidx, name):
    r, c = shard_shape
    hr = r // 2
    gv = _as_halves(g, kind, shard_shape)
    tr = hr if hr <= 512 else (256 if hr % 256 == 0 else hr // 2)
    assert hr % tr == 0

    def body(ci_ref, g_ref, r_ref, h_ref, hb_ref):
        s = g_ref[...].astype(F32) + r_ref[...].astype(F32)
        h_ref[...] = s
        hb_ref[...] = s.astype(BF)

    if kind == "col":
        grid = (hr // tr, N_CHIPS)
        g_spec = pl.BlockSpec((None, tr, c), lambda i, k, ci: (ci[0], i, k))
        o_spec = pl.BlockSpec((tr, c), lambda i, k, ci: (i, k))
    else:
        grid = (hr // tr, N_CHIPS)
        g_spec = pl.BlockSpec((None, None, tr, c), lambda i, k, ci: (k, ci[0], i, 0))
        o_spec = pl.BlockSpec((None, tr, c), lambda i, k, ci: (k, i, 0))
    gs = pltpu.PrefetchScalarGridSpec(num_scalar_prefetch=1, grid=grid, in_specs=[g_spec, o_spec], out_specs=[o_spec, o_spec])
    return _pcall(
        body, name=name, grid_spec=gs,
        out_shape=[jax.ShapeDtypeStruct(recv.shape, F32), jax.ShapeDtypeStruct(recv.shape, BF)],
        compiler_params=_params(("parallel", "parallel")),
    )(cidx, gv, recv)


def exchange_chip_pieces(hbs, kinds, shard_shapes, name, seq_id=None):
    n = len(hbs)
    outs = [jax.ShapeDtypeStruct((N_CHIPS - 1, r // 2, c), BF) for (r, c) in shard_shapes]

    def body(*refs):
        srcs, dsts = refs[:n], refs[n:2 * n]
        send_sems, recv_sems = refs[2 * n:]
        x, y, c = _me()
        if seq_id is not None:
            _handshake([(*_flip(x, y, p), c) for p in range(1, N_CHIPS)])
        cps = []
        for i in range(n):
            cc = shard_shapes[i][1]
            for p in range(1, N_CHIPS):
                px, py = _flip(x, y, p)
                pchip = 2 * px + py
                src = (srcs[i].at[:, pl.ds(pl.multiple_of(pchip * cc, cc), cc)] if kinds[i] == "col" else srcs[i].at[pchip])
                k = i * (N_CHIPS - 1) + p - 1
                cp = pltpu.make_async_remote_copy(src_ref=src, dst_ref=dsts[i].at[p - 1], send_sem=send_sems.at[k],
                                                  recv_sem=recv_sems.at[k], device_id=(px, py, c), device_id_type=MESH)
                cp.start()
                cps.append(cp)
        for cp in cps:
            cp.wait_recv()
        for cp in cps:
            cp.wait_send()

    nk = n * (N_CHIPS - 1)
    return _hbm_comm_call(body, name=name, n_in=n, out_shape=outs, seq_id=seq_id,
                          sem_shapes=[pltpu.SemaphoreType.DMA((nk,)), pltpu.SemaphoreType.DMA((nk,))])(*hbs)


def sum_chip_pieces(h, pieces, kind, shard_shape, chip_core, name):
    r, c = shard_shape
    hr = r // 2
    tr = hr if hr <= 512 else (256 if hr % 256 == 0 else hr // 2)
    assert hr % tr == 0
    nrb = hr // tr

    def body(ci_ref, h_ref, p_ref, q_ref):
        q_ref[...] = ((h_ref[...] + p_ref[0].astype(F32)) + p_ref[1].astype(F32)) + p_ref[2].astype(F32)

    if kind == "col":
        h_spec = pl.BlockSpec((tr, c), lambda i, ci: (i, ci[0]))
    else:
        h_spec = pl.BlockSpec((None, tr, c), lambda i, ci: (ci[0], i, 0))
    gs = pltpu.PrefetchScalarGridSpec(
        num_scalar_prefetch=1, grid=(nrb,),
        in_specs=[h_spec, pl.BlockSpec((N_CHIPS - 1, tr, c), lambda i, ci: (0, i, 0))],
        out_specs=pl.BlockSpec((tr, c), lambda i, ci: (ci[1] * nrb + i, 0)))
    return _pcall(body, name=name, grid_spec=gs, out_shape=jax.ShapeDtypeStruct((r, c), F32),
                  compiler_params=_params(("parallel",)))(chip_core, h, pieces)


def exchange_reduced_halves(qs, name):
    n = len(qs)

    def body(*refs):
        bufs = refs[n:2 * n]
        send_sems, recv_sems = refs[2 * n:]
        x, y, c = _me()
        cps = []
        for i in range(n):
            hr = bufs[i].shape[0] // 2
            mine = bufs[i].at[pl.ds(pl.multiple_of(c * hr, hr), hr), :]
            other = bufs[i].at[pl.ds(pl.multiple_of((1 - c) * hr, hr), hr), :]
            cp = pltpu.make_async_remote_copy(src_ref=mine, dst_ref=mine, send_sem=send_sems.at[i], recv_sem=recv_sems.at[i],
                                              device_id=(x, y, 1 - c), device_id_type=MESH)
            cp.start()
            cps.append((cp, pltpu.make_async_remote_copy(src_ref=other, dst_ref=other, send_sem=send_sems.at[i],
                                                         recv_sem=recv_sems.at[i], device_id=(x, y, 1 - c), device_id_type=MESH)))
        for cp, rv in cps:
            rv.wait_recv()
        for cp, rv in cps:
            cp.wait_send()

    anyspec = pl.BlockSpec(memory_space=pl.ANY)
    return _pcall(
        body, name=name, in_specs=[anyspec] * n, out_specs=[anyspec] * n,
        out_shape=[jax.ShapeDtypeStruct(q.shape, F32) for q in qs], input_output_aliases={i: i for i in range(n)},
        scratch_shapes=[pltpu.SemaphoreType.DMA((n,)), pltpu.SemaphoreType.DMA((n,))],
        compiler_params=_params(),
    )(*qs)


def _rows128(a):
    return a.reshape(-1, LANES)


def _after(xs, *deps):
    flat = []
    for d in deps:
        flat.extend(d if isinstance(d, (list, tuple)) else [d])
    return list(lax.optimization_barrier((tuple(xs), tuple(flat)))[0])


def _block_diag(w):
    H, d, _ = w.shape
    eye = jnp.eye(H, dtype=w.dtype)
    return jnp.einsum("hde,hg->hdge", w, eye).reshape(H * d, H * d)


def _diag_blocks(g4, H, d):
    nb = g4.shape[0]
    per = LANES // d
    g = g4.reshape(nb, per, d, per, d)
    return jnp.stack([g[:, j, :, j, :] for j in range(per)], axis=1).reshape(H, d, d)


def kernel(x, c, w_mod, b_mod, g_ffn1, w_ffn1_in, w_ffn1_out, g_mix, w_in, conv_w, conv_b, ln_g, ln_b, rnn_conv_w, rnn_conv_b, w_a, b_a, w_i, b_i, lru_lambda, w_out, g_ffn2, w_ffn2_in, w_ffn2_out, w_fmod, b_fmod, g_final, loss_target, m_w_mod, m_b_mod, m_g_ffn1, m_w_ffn1_in, m_w_ffn1_out, m_g_mix, m_w_in, m_conv_w, m_conv_b, m_ln_g, m_ln_b, m_rnn_conv_w, m_rnn_conv_b, m_w_a, m_b_a, m_w_i, m_b_i, m_lru_lambda, m_w_out, m_g_ffn2, m_w_ffn2_in, m_w_ffn2_out, m_w_fmod, m_b_fmod, m_g_final, v_w_mod, v_b_mod, v_g_ffn1, v_w_ffn1_in, v_w_ffn1_out, v_g_mix, v_w_in, v_conv_w, v_conv_b, v_ln_g, v_ln_b, v_rnn_conv_w, v_rnn_conv_b, v_w_a, v_b_a, v_w_i, v_b_i, v_lru_lambda, v_w_out, v_g_ffn2, v_w_ffn2_in, v_w_ffn2_out, v_w_fmod, v_b_fmod, v_g_final):
    S, D = x.shape[1], x.shape[2]
    M = conv_b.shape[1]
    H, HD = w_a.shape[1], w_a.shape[2]
    nb = M // LANES
    ix, iy, ic = lax.axis_index("x"), lax.axis_index("y"), lax.axis_index("c")
    chip = 2 * ix + iy
    dev = 2 * chip + ic
    cidx = jnp.reshape(ic, (1,)).astype(I32)
    chip_core = jnp.stack([chip, ic]).astype(I32)
    xs = x[0]
    tgt = loss_target[0]

    kinds = ["col", "row"]
    w_f1, w_mx, w_f2 = [w_ffn1_in[0], w_ffn1_out[0]], [w_in[0], w_out[0]], [w_ffn2_in[0], w_ffn2_out[0]]
    as_bf = lambda ws: [w.astype(BF) for w in ws]
    shapes_of = lambda ws: [w.shape for w in ws]
    b_f1, b_mx, b_f2 = as_bf(w_f1), as_bf(w_mx), as_bf(w_f2)
    got_f1i = allgather_weights(b_f1[:1], kinds[:1], "gather_ffn1_in", seq_id=9)
    got_f1o = allgather_weights(b_f1[1:], kinds[1:], "gather_ffn1_out", seq_id=13)
    got_mx = allgather_weights(b_mx, kinds, "gather_mix", seq_id=1)
    got_f2 = allgather_weights(b_f2, kinds, "gather_ffn2", seq_id=2)

    c_all =allgather_devices(_rows128(c), "gather_c")[0].reshape(N_DEV, D)
    mod_cols = cond_matmul(c_all, w_mod[0], "mod_proj")
    fmod_cols = cond_matmul(c_all, w_fmod, "fmod_proj")
    convw_pad = jnp.pad(conv_w[0], ((0, 32 - CONV_WIDTH), (0, 0)))
    rnnw_pad = jnp.pad(rnn_conv_w[0], ((0, SUBLANES - RNN_CONV_WIDTH), (0, 0)))
    n_mod, n_fmod = mod_cols.shape[1], fmod_cols.shape[1]
    small = jnp.concatenate([_rows128(mod_cols), _rows128(fmod_cols), convw_pad, rnnw_pad], axis=0)
    small4 = allgather_chips(small, "gather_cond")
    r0 = N_DEV * n_mod // LANES
    r1 = r0 + N_DEV * n_fmod // LANES
    mod_all = small4[:, :r0].reshape(N_CHIPS, N_DEV, n_mod)
    fmod_all = small4[:, r0:r1].reshape(N_CHIPS, N_DEV, n_fmod)
    convw4 = small4[:, r1:r1 + 32]
    rnnw4 = small4[:, r1 + 32:r1 + 32 + SUBLANES]
    mod_row = lax.dynamic_index_in_dim(mod_all, dev, axis=1, keepdims=False).reshape(1, N_CHIPS * n_mod) + b_mod
    fmod_row = lax.dynamic_index_in_dim(fmod_all, dev, axis=1, keepdims=False).reshape(1, N_CHIPS * n_fmod) + b_fmod[None, :]
    vecs = jnp.concatenate([mod_row.reshape(9, D), fmod_row.reshape(2, D), g_ffn1, g_mix, g_ffn2, g_final[None, :],
                            jnp.zeros((1, D), F32)], axis=0)
    lnv = jnp.concatenate([ln_g, ln_b, jnp.zeros((SUBLANES - 2, M), F32)], axis=0)
    bda = _block_diag(w_a[0]).astype(BF)
    bdi = _block_diag(w_i[0]).astype(BF)

    def reduce_add(gs, recv, ws, tag, kinds_=kinds):
        pairs = [add_sibling_half(g, r_, k, w.shape, cidx, f"add_sibling_{tag}{j}")
                 for j, (g, r_, k, w) in enumerate(zip(gs, recv, kinds_, ws))]
        return [p[0] for p in pairs], [p[1] for p in pairs]

    def reduce_sum(hs_, recv, ws, tag, kinds_=kinds):
        return [sum_chip_pieces(h_, p_, k, w.shape, chip_core, f"sum_chips_{tag}{j}")
                for j, (h_, p_, k, w) in enumerate(zip(hs_, recv, kinds_, ws))]

    rows1 = (R_SH1, R_SC1, R_GT1, R_G1)
    rows3 = (R_SH3, R_SC3, R_GT3, R_G3)
    (wi1,) = place_local_shards(got_f1i, b_f1[:1], kinds[:1], "place_ffn1_in")
    g1s, u1s, a1s = ffn_fwd_in(xs, vecs, wi1, rows1, "ffn1_fwd_in")
    (wo1,) = place_local_shards(_after(got_f1o, a1s), b_f1[1:], kinds[1:], "place_ffn1_out")
    x1, y1 = ffn_fwd_out(a1s, xs, vecs, wo1, rows1, "ffn1_fwd_out")
    win, wout = place_local_shards(_after(got_mx, x1), b_mx, kinds, "place_mix")
    proj = norm_matmul(x1, vecs, win, (R_SH2, R_SC2, R_G2), "mix_in_proj")
    cq = conv_fwd(proj, convw4, conv_b, "conv_fwd")
    xr, ra, ii, hh = rnn_fwd(proj, rnnw4, rnn_conv_b, bda, bdi, b_a, b_i, lru_lambda, "rnn_fwd")
    x2, ym, ycat = mix_out(cq, proj, hh, x1, vecs, lnv, wout, "mix_out")
    wi2, wo2 = place_local_shards(_after(got_f2, x2), b_f2, kinds, "place_ffn2")
    dx3, g2s, u2s, y2, vgf = ffn_fwd(x2, vecs, wi2, wo2, rows3, "ffn2_fwd", final_tgt=tgt)

    Fd = wo1.shape[0]
    tk = min(2048, S)
    dx2, act2, dg2, du2, h3b, dy2b, vg3 = ffn_bwd(dx3, x2, vecs, g2s, u2s, y2, wi2, wo2, rows3, "ffn2_bwd")
    gwo2 = matmul(act2, dy2b, "tn", tm=Fd // 2, tn=D, tk=tk, out_dtype=BF, name="ffn2_dwo")
    gwi2 = matmul(h3b, dg2, "tn", tm=D, tn=Fd // 2, tk=tk, out_dtype=BF, name="ffn2_dwg", out_cols=2 * Fd)
    gwi2 = matmul(h3b, du2, "tn", tm=D, tn=Fd // 2, tk=tk, out_dtype=BF, name="ffn2_dwu", out_cols=2 * Fd, col_off=Fd, prev=gwi2)
    recv1_f2 = exchange_sibling_halves([gwi2, gwo2], kinds, shapes_of(w_f2), "reduce1_ffn2", seq_id=3)
    dcq, dhout, duy, dymb, vgd, vgm = mix_out_bwd(dx2, ym, vecs, wout, cq, lnv, proj, hh, "mix_out_bwd")
    gwout = matmul(ycat, dymb, "tn", tm=2 * M, tn=D, tk=tk, out_dtype=BF, name="mix_dwout")
    recv1_f2 = _after(recv1_f2, gwout)
    h_f2, hb_f2 = reduce_add([gwi2, gwo2], recv1_f2, w_f2, "ffn2_")
    recv2_f2 = exchange_chip_pieces(hb_f2, kinds, shapes_of(w_f2), "reduce2_ffn2", seq_id=4)
    duv, dug, dconvw4, dconvb = conv_bwd(_after([dcq], hb_f2)[0], proj, convw4, "conv_bwd")
    dux, dwa4, dwi4, drnnw4, rvec = rnn_bwd(dhout, hh, xr, ra, ii, proj, rnnw4, bda, bdi, lru_lambda, "rnn_bwd")
    dx1, h2b, dpb, vg2 = mix_in_bwd((duv, dug, dux, duy), x1, dx2, vecs, win, "mix_in_bwd")
    gwin = matmul(h2b, dpb, "tn", tm=D, tn=1024, tk=tk, out_dtype=BF, name="mix_dwin")
    recv1_mx = exchange_sibling_halves([gwin, gwout], kinds, shapes_of(w_mx), "reduce1_mix", seq_id=5)
    g_f2 = exchange_reduced_halves(reduce_sum(_after(h_f2, gwin), recv2_f2, w_f2, "ffn2_"), "reduce3_ffn2")
    adam_f2 = [adam_big(w, g, m, v, "adam_" + nm) for w, g, m, v, nm in
               zip(w_f2, g_f2, [m_w_ffn2_in[0], m_w_ffn2_out[0]], [v_w_ffn2_in[0], v_w_ffn2_out[0]], ["ffn2_in", "ffn2_out"])]
    h_mx, hb_mx = reduce_add([gwin, gwout], _after(recv1_mx, adam_f2[0][0], adam_f2[1][0]), w_mx, "mix_")
    recv2_mx = exchange_chip_pieces(hb_mx, kinds, shapes_of(w_mx), "reduce2_mix", seq_id=6)
    dx0, act1, dg1, du1, h1b, dy1b, vg1 = ffn_bwd(_after([dx1], hb_mx)[0], xs, vecs, g1s, u1s, y1, wi1, wo1, rows1, "ffn1_bwd")
    dmod_row = jnp.concatenate([vg1[1:3], vg1[0:1], vg2[0:2], vgd[0:1], vg3[1:3], vg3[0:1]], axis=0)
    gains = jnp.concatenate([vg1[3:4], vg2[2:3], vg3[3:4], vgf[2:4]], axis=0)
    mvecs = jnp.concatenate([dconvb, vgm[0:2], rvec[0:4], jnp.zeros((1, M), F32)], axis=0)
    parts = [_rows128(dmod_row), _rows128(vgf[0:2]), _rows128(gains), _rows128(mvecs),
             _rows128(dconvw4), _rows128(drnnw4), _rows128(_diag_blocks(dwa4, H, HD)), _rows128(_diag_blocks(dwi4, H, HD))]
    sizes = [p.shape[0] for p in parts]
    packed = jnp.concatenate(parts, axis=0)
    gathered = allgather_devices_hbm(packed, "gather_small", seq_id=10)

    gwo1 = matmul(_after([act1], recv2_mx, packed)[0], dy1b, "tn", tm=Fd // 2, tn=D, tk=tk, out_dtype=BF, name="ffn1_dwo")
    w_f1o, w_f1i = w_f1[1:], w_f1[:1]
    recv1_f1o = exchange_sibling_halves([gwo1], ["row"], shapes_of(w_f1o), "reduce1_ffn1_out", seq_id=7)
    gwi1 = matmul(h1b, dg1, "tn", tm=D, tn=Fd // 2, tk=tk, out_dtype=BF, name="ffn1_dwg", out_cols=2 * Fd)
    h_f1o, hb_f1o = reduce_add([gwo1], _after(recv1_f1o, gwi1), w_f1o, "ffn1_out", ["row"])
    recv2_f1o = exchange_chip_pieces(hb_f1o, ["row"], shapes_of(w_f1o), "reduce2_ffn1_out", seq_id=11)
    gwi1 = matmul(h1b, _after([du1], hb_f1o, gathered)[0], "tn", tm=D, tn=Fd // 2, tk=tk, out_dtype=BF, name="ffn1_dwu", out_cols=2 * Fd,
                  col_off=Fd, prev=gwi1)
    recv1_f1i = exchange_sibling_halves([gwi1], ["col"], shapes_of(w_f1i), "reduce1_ffn1_in", seq_id=12)
    g_mx = exchange_reduced_halves(reduce_sum(h_mx, recv2_mx, w_mx, "mix_"), "reduce3_mix")
    summed = sum_slots(gathered, "sum_small")
    offs = [0]
    for s in sizes:
        offs.append(offs[-1] + s)
    seg = lambda k: summed[offs[k]:offs[k + 1]]
    g_b_mod = seg(0).reshape(1, 9 * D)
    g_b_fmod = seg(1).reshape(1, 2 * D)
    gsum = seg(2).reshape(5, D)
    loss = (0.5 / D) * jnp.sum(gsum[4])
    msum = seg(3).reshape(SUBLANES, M)
    g_conv_w = lax.dynamic_index_in_dim(seg(4).reshape(nb, 32, LANES), chip, axis=0, keepdims=False)[:CONV_WIDTH]
    g_rnn_w = lax.dynamic_index_in_dim(seg(5).reshape(nb, SUBLANES, LANES), chip, axis=0, keepdims=False)[:RNN_CONV_WIDTH]
    g_w_a = seg(6).reshape(H, HD, HD)
    g_w_i = seg(7).reshape(H, HD, HD)
    dmod_all = gathered[:, offs[0]:offs[1]].reshape(N_DEV, 9 * D)
    dfmod_all = gathered[:, offs[1]:offs[2]].reshape(N_DEV, 2 * D)
    dmod_cols = lax.dynamic_slice_in_dim(dmod_all, chip * n_mod, n_mod, axis=1)
    dfmod_cols = lax.dynamic_slice_in_dim(dfmod_all, chip * n_fmod, n_fmod, axis=1)

    g_w_mod, d_w_mod, nm_w_mod, nv_w_mod = adam_cond(c_all, dmod_cols, w_mod[0], m_w_mod[0], v_w_mod[0], "adam_w_mod")
    g_w_fmod, d_w_fmod, nm_w_fmod, nv_w_fmod = adam_cond(c_all, dfmod_cols, w_fmod, m_w_fmod, v_w_fmod, "adam_w_fmod")

    h_f1i, hb_f1i = reduce_add([gwi1], _after(recv1_f1i, recv2_f1o, g_w_mod, g_w_fmod), w_f1i, "ffn1_in", ["col"])
    recv2_f1i = exchange_chip_pieces(hb_f1i, ["col"], shapes_of(w_f1i), "reduce2_ffn1_in", seq_id=8)
    g_f1o = exchange_reduced_halves(reduce_sum(h_f1o, recv2_f1o, w_f1o, "ffn1_out", ["row"]), "reduce3_ffn1_out")
    g_f1i = exchange_reduced_halves(reduce_sum(h_f1i, recv2_f1i, w_f1i, "ffn1_in", ["col"]), "reduce3_ffn1_in")
    g_f1 = list(g_f1i) + list(g_f1o)

    big_w = w_f1 + w_mx + w_f2
    g_big = g_f1 + list(g_mx) + list(g_f2)
    names = ["ffn1_in", "ffn1_out", "w_in", "w_out", "ffn2_in", "ffn2_out"]
    big_m = [m_w_ffn1_in[0], m_w_ffn1_out[0], m_w_in[0], m_w_out[0]]
    big_v = [v_w_ffn1_in[0], v_w_ffn1_out[0], v_w_in[0], v_w_out[0]]
    big_out = [adam_big(w, g, m, v, "adam_" + nm) for w, g, m, v, nm in zip(big_w, g_big, big_m, big_v, names)] + adam_f2

    flat2 = lambda a: a.reshape(-1, a.shape[-1])
    small_names = ["b_mod", "g_ffn1", "g_mix", "conv_w", "conv_b", "ln_g", "ln_b", "rnn_conv_w", "rnn_conv_b", "w_a", "b_a",
                   "w_i", "b_i", "lru_lambda", "g_ffn2", "b_fmod", "g_final"]
    small_w = [b_mod, g_ffn1, g_mix, conv_w, conv_b, ln_g, ln_b, rnn_conv_w, rnn_conv_b, w_a, b_a, w_i, b_i, lru_lambda,
               g_ffn2, b_fmod, g_final]
    small_m = [m_b_mod, m_g_ffn1, m_g_mix, m_conv_w, m_conv_b, m_ln_g, m_ln_b, m_rnn_conv_w, m_rnn_conv_b, m_w_a, m_b_a,
               m_w_i, m_b_i, m_lru_lambda, m_g_ffn2, m_b_fmod, m_g_final]
    small_v = [v_b_mod, v_g_ffn1, v_g_mix, v_conv_w, v_conv_b, v_ln_g, v_ln_b, v_rnn_conv_w, v_rnn_conv_b, v_w_a, v_b_a,
               v_w_i, v_b_i, v_lru_lambda, v_g_ffn2, v_b_fmod, v_g_final]
    small_g = [g_b_mod, gsum[0:1], gsum[1:2], g_conv_w, msum[0:1], msum[1:2], msum[2:3], g_rnn_w, msum[3:4], g_w_a, msum[4:5],
               g_w_i, msum[5:6], msum[6:7], gsum[2:3], g_b_fmod, gsum[3:4]]
    small_g = [g.reshape(w.shape) for g, w in zip(small_g, small_w)]
    two_d = lambda a: a.reshape(1, -1) if a.ndim == 1 else flat2(a)
    sd, sm, sv = adam_small([two_d(a) for a in small_w], [two_d(a) for a in small_g], [two_d(a) for a in small_m],
                            [two_d(a) for a in small_v], "adam_small")
    small = {}
    for k, nm in enumerate(small_names):
        shp = small_w[k].shape
        small[nm] = (small_g[k], sd[k].reshape(shp), sm[k].reshape(shp), sv[k].reshape(shp))

    big = {"w_mod": tuple(a[None] for a in (g_w_mod, d_w_mod, nm_w_mod, nv_w_mod)),
           "w_fmod": (g_w_fmod, d_w_fmod, nm_w_fmod, nv_w_fmod)}
    for nm, full, g, (d, nmm, nvv) in zip(["w_ffn1_in", "w_ffn1_out", "w_in", "w_out", "w_ffn2_in", "w_ffn2_out"],
                                         big_w, g_big, big_out):
        big[nm] = tuple(a[None] for a in (g, d, nmm, nvv))
    order = ["w_mod", "b_mod", "g_ffn1", "w_ffn1_in", "w_ffn1_out", "g_mix", "w_in", "conv_w", "conv_b", "ln_g", "ln_b",
             "rnn_conv_w", "rnn_conv_b", "w_a", "b_a", "w_i", "b_i", "lru_lambda", "w_out", "g_ffn2", "w_ffn2_in",
             "w_ffn2_out", "w_fmod", "b_fmod", "g_final"]
    table = {**small, **big}
    outs = [loss, dx0[None]]
    for kind_ in range(4):
        outs.extend(table[nm][kind_] for nm in order)
    return tuple(outs)
```

```python
import functools

import jax
import jax.numpy as jnp
from jax import lax
from jax.experimental import pallas as pl
from jax.experimental.pallas import tpu as pltpu
from jax.experimental.pallas import tpu_sc as plsc

F32 = jnp.float32
BF = jnp.bfloat16
I32 = jnp.int32
MESH = pl.DeviceIdType.MESH

EPS = 1e-6
RG_C = 8.0
MACARON_W = 0.5
CONV_WIDTH = 31
RNN_CONV_WIDTH = 4
ADAM_LR = 0.001
ADAM_B1 = 0.9
ADAM_B2 = 0.999
ADAM_EPS = 1e-08
ADAM_WD = 0.01
ADAM_STEP = 10

LANES = 128
SUBLANES = 8
VMEM_LIMIT = 62 * 1024 * 1024
N_CHIPS = 4
N_DEV = 8

R_SH1, R_SC1, R_GT1, R_SH2, R_SC2, R_GT2, R_SH3, R_SC3, R_GT3, R_FSH, R_FSC, R_G1, R_G2, R_G3, R_GF = range(15)

CONTRACT_LAST = (((1,), (1,)), ((), ()))
CONTRACT_FIRST = (((0,), (0,)), ((), ()))


def _pcall(body, **kw):
    return pl.pallas_call(body, **kw)


def _params(sem=None, vmem=VMEM_LIMIT):
    if sem is None:
        return pltpu.CompilerParams(vmem_limit_bytes=vmem)
    return pltpu.CompilerParams(dimension_semantics=sem, vmem_limit_bytes=vmem)


def _row(ref, r):
    return ref[r:r + 1, :]


def _sigmoid(x):
    return 1.0 / (1.0 + jnp.exp(-x))


def _colsum(x):
    return jnp.sum(x, axis=0, keepdims=True)


def _rowmean(x):
    return jnp.mean(x, axis=-1, keepdims=True)


def matmul(a, b, mode, *, tm, tn, tk, name, out_dtype=F32, out_cols=None, col_off=0, prev=None):
    if mode == "nn":
        (M, K), (K2, N) = a.shape, b.shape
    elif mode == "nt":
        (M, K), (N, K2) = a.shape, b.shape
    else:
        (K, M), (K2, N) = a.shape, b.shape
    assert K == K2 and M % tm == 0 and N % tn == 0 and K % tk == 0 and col_off % tn == 0
    nk = K // tk
    out_cols = N if out_cols is None else out_cols
    off = col_off // tn

    def body(*refs):
        if prev is None:
            a_ref, b_ref, o_ref, acc = refs
        else:
            a_ref, b_ref, _, o_ref, acc = refs
        k = pl.program_id(2)
        av = a_ref[...].astype(BF)
        bv = b_ref[...].astype(BF)
        if mode == "nn":
            part = jnp.dot(av, bv, preferred_element_type=F32)
        elif mode == "nt":
            part = lax.dot_general(av, bv, CONTRACT_LAST, preferred_element_type=F32)
        else:
            part = lax.dot_general(av, bv, CONTRACT_FIRST, preferred_element_type=F32)
        if nk == 1:
            o_ref[...] = part.astype(out_dtype)
            return

        @pl.when(k == 0)
        def _():
            acc[...] = part

        @pl.when((k > 0) & (k < nk - 1))
        def _():
            acc[...] += part

        @pl.when(k == nk - 1)
        def _():
            o_ref[...] = (acc[...] + part).astype(out_dtype)

    if mode == "nn":
        a_spec = pl.BlockSpec((tm, tk), lambda m, n, k: (m, k))
        b_spec = pl.BlockSpec((tk, tn), lambda m, n, k: (k, n))
    elif mode == "nt":
        a_spec = pl.BlockSpec((tm, tk), lambda m, n, k: (m, k))
        b_spec = pl.BlockSpec((tn, tk), lambda m, n, k: (n, k))
    else:
        a_spec = pl.BlockSpec((tk, tm), lambda m, n, k: (k, m))
        b_spec = pl.BlockSpec((tk, tn), lambda m, n, k: (k, n))
    in_specs = [a_spec, b_spec]
    args = [a, b]
    aliases = {}
    if prev is not None:
        in_specs.append(pl.BlockSpec(memory_space=pl.ANY))
        args.append(prev)
        aliases = {2: 0}
    return _pcall(
        body, name=name, grid=(M // tm, N // tn, nk), in_specs=in_specs,
        out_specs=pl.BlockSpec((tm, tn), lambda m, n, k: (m, n + off)),
        out_shape=jax.ShapeDtypeStruct((M, out_cols), out_dtype),
        scratch_shapes=[pltpu.VMEM((tm, tn), F32)], input_output_aliases=aliases,
        compiler_params=_params(("parallel", "parallel", "arbitrary")),
    )(*args)


def cond_matmul(c_all, w, name):
    B, K = c_all.shape
    N = w.shape[1]
    tn = 256
    assert N % tn == 0

    def body(c_ref, w_ref, o_ref):
        cv = c_ref[...]
        ca = cv * _sigmoid(cv)
        o_ref[...] = jnp.dot(ca.astype(BF), w_ref[...].astype(BF), preferred_element_type=F32)

    return _pcall(
        body, name=name, grid=(N // tn,),
        in_specs=[pl.BlockSpec((B, K), lambda n: (0, 0)), pl.BlockSpec((K, tn), lambda n: (0, n))],
        out_specs=pl.BlockSpec((B, tn), lambda n: (0, n)),
        out_shape=jax.ShapeDtypeStruct((B, N), F32), compiler_params=_params(("parallel",)),
    )(c_all, w)


FFN_FWD_TS = 512
FFN_BWD_TS = 256


def _resident(shape, index_map):
    return pl.BlockSpec(shape, index_map, pipeline_mode=pl.Buffered(1))


def _final_norm_loss_grad(xv, t, v_ref, vg_ref):
    D = xv.shape[-1]
    r = lax.rsqrt(_rowmean(xv * xv) + EPS)
    n = xv * r
    g = _row(v_ref, R_GF)
    sc1 = 1.0 + _row(v_ref, R_FSC)
    gsc = g * sc1
    e = n * gsc + _row(v_ref, R_FSH) - t
    vg_ref[3:4, :] += _colsum(e * e)
    dout = e * (1.0 / D)
    dn_ = dout * n
    vg_ref[0:1, :] += _colsum(dout)
    vg_ref[1:2, :] += _colsum(dn_) * g
    vg_ref[2:3, :] += _colsum(dn_) * sc1
    dn = dout * gsc
    return r * (dn - n * _rowmean(dn * n))


def ffn_fwd(x, vecs, wi, wo, rows, name, final_tgt=None):
    r_sh, r_sc, r_gt, r_g = rows
    S, D = x.shape
    Fd = wo.shape[0]
    ts = min(FFN_FWD_TS, S)
    with_final = final_tgt is not None

    def body(*refs):
        if with_final:
            x_ref, v_ref, wg_ref, wu_ref, wo_ref, t_ref, xo_ref, g_ref, u_ref, y_ref, vg_ref = refs
        else:
            x_ref, v_ref, wg_ref, wu_ref, wo_ref, xo_ref, g_ref, u_ref, y_ref = refs
        xv = x_ref[...]
        r = lax.rsqrt(_rowmean(xv * xv) + EPS)
        gs = _row(v_ref, r_g) * (1.0 + _row(v_ref, r_sc))
        hb = (xv * r * gs + _row(v_ref, r_sh)).astype(BF)
        G = jnp.dot(hb, wg_ref[...], preferred_element_type=F32)
        U = jnp.dot(hb, wu_ref[...], preferred_element_type=F32)
        g_ref[...] = G.astype(BF)
        u_ref[...] = U.astype(BF)
        act = (G * _sigmoid(G) * U).astype(BF)
        Y = jnp.dot(act, wo_ref[...], preferred_element_type=F32)
        y_ref[...] = Y.astype(BF)
        xo = xv + (MACARON_W * _row(v_ref, r_gt)) * Y
        if with_final:
            @pl.when(pl.program_id(0) == 0)
            def _():
                vg_ref[...] = jnp.zeros_like(vg_ref)

            xo_ref[...] = _final_norm_loss_grad(xo, t_ref[...], v_ref, vg_ref)
        else:
            xo_ref[...] = xo

    tok = pl.BlockSpec((ts, D), lambda i: (i, 0))
    hid = pl.BlockSpec((ts, Fd), lambda i: (i, 0))
    in_specs = [tok, pl.BlockSpec(vecs.shape, lambda i: (0, 0)), _resident((D, Fd), lambda i: (0, 0)),
                _resident((D, Fd), lambda i: (0, 1)), _resident((Fd, D), lambda i: (0, 0))]
    out_specs = [tok, hid, hid, tok]
    out_shape = [jax.ShapeDtypeStruct((S, D), F32), jax.ShapeDtypeStruct((S, Fd), BF),
                 jax.ShapeDtypeStruct((S, Fd), BF), jax.ShapeDtypeStruct((S, D), BF)]
    args = [x, vecs, wi, wi, wo]
    if with_final:
        in_specs.append(tok)
        args.append(final_tgt)
        out_specs.append(pl.BlockSpec((SUBLANES, D), lambda i: (0, 0)))
        out_shape.append(jax.ShapeDtypeStruct((SUBLANES, D), F32))
    return _pcall(body, name=name, grid=(S // ts,), in_specs=in_specs, out_specs=out_specs, out_shape=out_shape,
                  compiler_params=_params(("arbitrary",)))(*args)


def ffn_fwd_in(x, vecs, wi, rows, name):
    r_sh, r_sc, r_gt, r_g = rows
    S, D = x.shape
    Fd = wi.shape[1] // 2
    ts = min(FFN_FWD_TS, S)

    def body(x_ref, v_ref, wg_ref, wu_ref, g_ref, u_ref, a_ref):
        xv = x_ref[...]
        r = lax.rsqrt(_rowmean(xv * xv) + EPS)
        gs = _row(v_ref, r_g) * (1.0 + _row(v_ref, r_sc))
        hb = (xv * r * gs + _row(v_ref, r_sh)).astype(BF)
        G = jnp.dot(hb, wg_ref[...], preferred_element_type=F32)
        U = jnp.dot(hb, wu_ref[...], preferred_element_type=F32)
        g_ref[...] = G.astype(BF)
        u_ref[...] = U.astype(BF)
        a_ref[...] = (G * _sigmoid(G) * U).astype(BF)

    hid = pl.BlockSpec((ts, Fd), lambda i: (i, 0))
    return _pcall(
        body, name=name, grid=(S // ts,),
        in_specs=[pl.BlockSpec((ts, D), lambda i: (i, 0)), pl.BlockSpec(vecs.shape, lambda i: (0, 0)),
                  _resident((D, Fd), lambda i: (0, 0)), _resident((D, Fd), lambda i: (0, 1))],
        out_specs=[hid, hid, hid], out_shape=[jax.ShapeDtypeStruct((S, Fd), BF)] * 3,
        compiler_params=_params(("arbitrary",)),
    )(x, vecs, wi, wi)


def ffn_fwd_out(act, x, vecs, wo, rows, name):
    r_sh, r_sc, r_gt, r_g = rows
    S, D = x.shape
    Fd = wo.shape[0]
    ts = min(FFN_FWD_TS, S)

    def body(a_ref, x_ref, v_ref, wo_ref, xo_ref, y_ref):
        Y = jnp.dot(a_ref[...], wo_ref[...], preferred_element_type=F32)
        y_ref[...] = Y.astype(BF)
        xo_ref[...] = x_ref[...] + (MACARON_W * _row(v_ref, r_gt)) * Y

    tok = pl.BlockSpec((ts, D), lambda i: (i, 0))
    return _pcall(
        body, name=name, grid=(S // ts,),
        in_specs=[pl.BlockSpec((ts, Fd), lambda i: (i, 0)), tok, pl.BlockSpec(vecs.shape, lambda i: (0, 0)),
                  _resident((Fd, D), lambda i: (0, 0))],
        out_specs=[tok, tok], out_shape=[jax.ShapeDtypeStruct((S, D), F32), jax.ShapeDtypeStruct((S, D), BF)],
        compiler_params=_params(("arbitrary",)),
    )(act, x, vecs, wo)


def ffn_bwd(dxo, x, vecs, gs_, us_, y, wi, wo, rows, name):
    r_sh, r_sc, r_gt, r_g = rows
    S, D = x.shape
    Fd = wo.shape[0]
    ts = min(FFN_BWD_TS, S)

    def body(dxo_ref, x_ref, v_ref, g_ref, u_ref, y_ref, wg_ref, wu_ref, wo_ref,
             dx_ref, act_ref, dg_ref, du_ref, hb_ref, dyb_ref, vg_ref):
        @pl.when(pl.program_id(0) == 0)
        def _():
            vg_ref[...] = jnp.zeros_like(vg_ref)

        dxo_v = dxo_ref[...]
        dyb = ((MACARON_W * _row(v_ref, r_gt)) * dxo_v).astype(BF)
        dyb_ref[...] = dyb
        vg_ref[0:1, :] += MACARON_W * _colsum(dxo_v * y_ref[...].astype(F32))
        dA = lax.dot_general(dyb, wo_ref[...], CONTRACT_LAST, preferred_element_type=F32)
        G = g_ref[...].astype(F32)
        U = u_ref[...].astype(F32)
        sg = _sigmoid(G)
        sl = G * sg
        dU = (dA * sl).astype(BF)
        dG = (dA * U * (sg * (1.0 + G * (1.0 - sg)))).astype(BF)
        act_ref[...] = (sl * U).astype(BF)
        dg_ref[...] = dG
        du_ref[...] = dU
        dh = (lax.dot_general(dG, wg_ref[...], CONTRACT_LAST, preferred_element_type=F32)
              + lax.dot_general(dU, wu_ref[...], CONTRACT_LAST, preferred_element_type=F32))
        xv = x_ref[...]
        r = lax.rsqrt(_rowmean(xv * xv) + EPS)
        n = xv * r
        g = _row(v_ref, r_g)
        sc1 = 1.0 + _row(v_ref, r_sc)
        gsc = g * sc1
        hb_ref[...] = (n * gsc + _row(v_ref, r_sh)).astype(BF)
        dhn = dh * n
        vg_ref[1:2, :] += _colsum(dh)
        vg_ref[2:3, :] += _colsum(dhn) * g
        vg_ref[3:4, :] += _colsum(dhn) * sc1
        dn = dh * gsc
        dx_ref[...] = dxo_v + r * (dn - n * _rowmean(dn * n))

    tok = pl.BlockSpec((ts, D), lambda i: (i, 0))
    hid = pl.BlockSpec((ts, Fd), lambda i: (i, 0))
    return _pcall(
        body, name=name, grid=(S // ts,),
        in_specs=[tok, tok, pl.BlockSpec(vecs.shape, lambda i: (0, 0)), hid, hid, tok, _resident((D, Fd), lambda i: (0, 0)),
                  _resident((D, Fd), lambda i: (0, 1)), _resident((Fd, D), lambda i: (0, 0))],
        out_specs=[tok, hid, hid, hid, tok, tok, pl.BlockSpec((SUBLANES, D), lambda i: (0, 0))],
        out_shape=[jax.ShapeDtypeStruct((S, D), F32), jax.ShapeDtypeStruct((S, Fd), BF),
                   jax.ShapeDtypeStruct((S, Fd), BF), jax.ShapeDtypeStruct((S, Fd), BF),
                   jax.ShapeDtypeStruct((S, D), BF), jax.ShapeDtypeStruct((S, D), BF),
                   jax.ShapeDtypeStruct((SUBLANES, D), F32)],
        compiler_params=_params(("arbitrary",)),
    )(dxo, x, vecs, gs_, us_, y, wi, wi, wo)


def norm_matmul(x, vecs, w, rows, name):
    r_sh, r_sc, r_g = rows
    S, D = x.shape
    N = w.shape[1]
    ts = min(512, S)

    def body(x_ref, v_ref, w_ref, o_ref):
        xv = x_ref[...]
        r = lax.rsqrt(_rowmean(xv * xv) + EPS)
        gs = _row(v_ref, r_g) * (1.0 + _row(v_ref, r_sc))
        hb = (xv * r * gs + _row(v_ref, r_sh)).astype(BF)
        o_ref[...] = jnp.dot(hb, w_ref[...], preferred_element_type=F32).astype(BF)

    return _pcall(
        body, name=name, grid=(S // ts,),
        in_specs=[pl.BlockSpec((ts, D), lambda i: (i, 0)), pl.BlockSpec(vecs.shape, lambda i: (0, 0)),
                  _resident((D, N), lambda i: (0, 0))],
        out_specs=pl.BlockSpec((ts, N), lambda i: (i, 0)),
        out_shape=jax.ShapeDtypeStruct((S, N), BF),
        compiler_params=_params(("arbitrary",)),
    )(x, vecs, w)


SEQ_TT = 256
SCAN_SEGMENTS = 4
CONV_PAD = 32


def conv_fwd(proj, convw4, conv_b, name):
    S = proj.shape[0]
    M = conv_b.shape[1]
    nb = M // LANES
    tt = min(SEQ_TT, S)

    def body(uv_ref, ug_ref, w_ref, b_ref, cq_ref, qp):
        qp[0:CONV_PAD, :] = jnp.zeros((CONV_PAD, LANES), F32)

        def step(t, carry):
            base = pl.multiple_of(t * tt, tt)
            qp[pl.ds(base + CONV_PAD, tt), :] = uv_ref[pl.ds(base, tt), :].astype(F32) * _sigmoid(ug_ref[pl.ds(base, tt), :].astype(F32))
            acc = jnp.broadcast_to(b_ref[...], (tt, LANES))
            for k in range(CONV_WIDTH):
                acc = acc + w_ref[k:k + 1, :] * qp[pl.ds(base + (CONV_PAD - CONV_WIDTH + 1) + k, tt), :]
            cq_ref[pl.ds(base, tt), :] = acc
            return carry

        lax.fori_loop(0, S // tt, step, 0)

    return _pcall(
        body, name=name, grid=(nb,),
        in_specs=[pl.BlockSpec((S, LANES), lambda c: (0, c)), pl.BlockSpec((S, LANES), lambda c: (0, c + nb)),
                  pl.BlockSpec((None, 32, LANES), lambda c: (c, 0, 0)), pl.BlockSpec((1, LANES), lambda c: (0, c))],
        out_specs=pl.BlockSpec((S, LANES), lambda c: (0, c)),
        out_shape=jax.ShapeDtypeStruct((S, M), F32),
        scratch_shapes=[pltpu.VMEM((S + CONV_PAD, LANES), F32)],
        compiler_params=_params(("arbitrary",)),
    )(proj, proj, convw4, conv_b)


def conv_bwd(dcq, proj, convw4, name):
    S, M = dcq.shape
    nb = M // LANES
    tt = min(SEQ_TT, S)
    off = CONV_PAD - CONV_WIDTH + 1

    def body(dcq_ref, uv_ref, ug_ref, w_ref, duv_ref, dug_ref, dw_ref, db_ref, qp, dp, dw8, db8):
        qp[0:CONV_PAD, :] = jnp.zeros((CONV_PAD, LANES), F32)
        dp[S:S + CONV_PAD, :] = jnp.zeros((CONV_PAD, LANES), F32)
        dw8[...] = jnp.zeros_like(dw8)
        db8[...] = jnp.zeros_like(db8)

        def fill(t, carry):
            base = pl.multiple_of(t * tt, tt)
            qp[pl.ds(base + CONV_PAD, tt), :] = uv_ref[pl.ds(base, tt), :].astype(F32) * _sigmoid(ug_ref[pl.ds(base, tt), :].astype(F32))
            dp[pl.ds(base, tt), :] = dcq_ref[pl.ds(base, tt), :].astype(F32)
            return carry

        lax.fori_loop(0, S // tt, fill, 0)

        def step(t, carry):
            base = pl.multiple_of(t * tt, tt)
            d_t = dcq_ref[pl.ds(base, tt), :].astype(F32)
            db8[...] += d_t.reshape(tt // SUBLANES, SUBLANES, LANES).sum(axis=0)
            dq = jnp.zeros((tt, LANES), F32)
            for k in range(CONV_WIDTH):
                prod = d_t * qp[pl.ds(base + off + k, tt), :]
                dw8[k] += prod.reshape(tt // SUBLANES, SUBLANES, LANES).sum(axis=0)
                dq = dq + w_ref[k:k + 1, :] * dp[pl.ds(base + (CONV_WIDTH - 1) - k, tt), :]
            uv = uv_ref[pl.ds(base, tt), :].astype(F32)
            sg = _sigmoid(ug_ref[pl.ds(base, tt), :].astype(F32))
            duv_ref[pl.ds(base, tt), :] = (dq * sg).astype(BF)
            dug_ref[pl.ds(base, tt), :] = (dq * uv * sg * (1.0 - sg)).astype(BF)
            return carry

        lax.fori_loop(0, S // tt, step, 0)
        dw_ref[...] = jnp.zeros_like(dw_ref)
        for k in range(CONV_WIDTH):
            dw_ref[k:k + 1, :] = _colsum(dw8[k])
        db_ref[...] = _colsum(db8[...])

    col = lambda o: pl.BlockSpec((S, LANES), lambda c: (0, c + o))
    return _pcall(
        body, name=name, grid=(nb,),
        in_specs=[col(0), col(0), col(nb), pl.BlockSpec((None, 32, LANES), lambda c: (c, 0, 0))],
        out_specs=[col(0), col(0), pl.BlockSpec((None, 32, LANES), lambda c: (c, 0, 0)),
                   pl.BlockSpec((1, LANES), lambda c: (0, c))],
        out_shape=[jax.ShapeDtypeStruct((S, M), BF), jax.ShapeDtypeStruct((S, M), BF),
                   jax.ShapeDtypeStruct((nb, 32, LANES), F32), jax.ShapeDtypeStruct((1, M), F32)],
        scratch_shapes=[pltpu.VMEM((S + CONV_PAD, LANES), F32), pltpu.VMEM((S + CONV_PAD, LANES), F32),
                        pltpu.VMEM((32, SUBLANES, LANES), F32), pltpu.VMEM((SUBLANES, LANES), F32)],
        compiler_params=_params(("arbitrary",)),
    )(dcq, proj, proj, convw4)


def _log_sigmoid(x):
    return jnp.minimum(x, 0.0) - jnp.log(1.0 + jnp.exp(-jnp.abs(x)))


def _rg_gate_terms(ra, ls):
    la = RG_C * ra * ls
    a = jnp.exp(la)
    th = jnp.tanh(la)
    mult = jnp.sqrt(-2.0 * th / (1.0 - th))
    return a, mult


def rnn_fwd(proj, rnnw4, rnn_b, bda, bdi, b_a, b_i, lam, name):
    S = proj.shape[0]
    M = rnn_b.shape[1]
    nb = M // LANES
    tt = min(SEQ_TT, S)
    KW = RNN_CONV_WIDTH
    nseg = SCAN_SEGMENTS if S % (SCAN_SEGMENTS * tt) == 0 else 1

    def body(ux_ref, w_ref, rb_ref, bda_ref, bdi_ref, ba_ref, bi_ref, lam_ref,
             xr_ref, ra_ref, ii_ref, h_ref, uxp, a_sc, b_sc):
        uxp[0:SUBLANES, :] = jnp.zeros((SUBLANES, LANES), F32)
        ls = _log_sigmoid(lam_ref[...])

        def step(t, carry):
            base = pl.multiple_of(t * tt, tt)
            uxp[pl.ds(base + SUBLANES, tt), :] = ux_ref[pl.ds(base, tt), :].astype(F32)
            xr = jnp.broadcast_to(rb_ref[...], (tt, LANES))
            for k in range(KW):
                xr = xr + w_ref[k:k + 1, :] * uxp[pl.ds(base + (SUBLANES - KW + 1) + k, tt), :]
            xb = xr.astype(BF)
            ra = _sigmoid(jnp.dot(xb, bda_ref[...], preferred_element_type=F32) + ba_ref[...])
            ii = _sigmoid(jnp.dot(xb, bdi_ref[...], preferred_element_type=F32) + bi_ref[...])
            a, mult = _rg_gate_terms(ra, ls)
            xr_ref[pl.ds(base, tt), :] = xr
            ra_ref[pl.ds(base, tt), :] = ra
            ii_ref[pl.ds(base, tt), :] = ii
            a_sc[pl.ds(base, tt), :] = a
            b_sc[pl.ds(base, tt), :] = mult * (ii * xr)
            return carry

        lax.fori_loop(0, S // tt, step, 0)

        rows = lax.broadcasted_iota(I32, (SUBLANES, LANES), 0)
        seg = S // nseg
        last = lambda v: jnp.broadcast_to(v[SUBLANES - 1:SUBLANES, :], (SUBLANES, LANES))

        def scan(t, carry):
            hs, ps = carry
            new_h, new_p = [], []
            for s in range(nseg):
                base = pl.multiple_of(s * seg + t * SUBLANES, SUBLANES)
                A = a_sc[pl.ds(base, SUBLANES), :]
                B = b_sc[pl.ds(base, SUBLANES), :]
                for d in (1, 2, 4):
                    As = jnp.where(rows >= d, pltpu.roll(A, d, axis=0), 1.0)
                    Bs = jnp.where(rows >= d, pltpu.roll(B, d, axis=0), 0.0)
                    B = A * Bs + B
                    A = A * As
                hh = B + A * hs[s]
                h_ref[pl.ds(base, SUBLANES), :] = hh
                pp = A * ps[s]
                if s > 0:
                    a_sc[pl.ds(base, SUBLANES), :] = pp
                new_h.append(last(hh))
                new_p.append(last(pp))
            return tuple(new_h), tuple(new_p)

        zero8 = jnp.zeros((SUBLANES, LANES), F32)
        one8 = jnp.ones((SUBLANES, LANES), F32)
        hs, ps = lax.fori_loop(0, seg // SUBLANES, scan, ((zero8,) * nseg, (one8,) * nseg))
        carry_in = hs[0]
        for s in range(1, nseg):
            c_row = carry_in[0:1, :]

            def fix(t, c, s=s, c_row=c_row):
                base = pl.multiple_of(s * seg + t * tt, tt)
                h_ref[pl.ds(base, tt), :] = h_ref[pl.ds(base, tt), :] + a_sc[pl.ds(base, tt), :] * c_row
                return c

            lax.fori_loop(0, seg // tt, fix, 0)
            carry_in = hs[s] + ps[s] * carry_in

    col = lambda o: pl.BlockSpec((S, LANES), lambda c: (0, c + o))
    vec = pl.BlockSpec((1, LANES), lambda c: (0, c))
    diag = pl.BlockSpec((LANES, LANES), lambda c: (c, c))
    return _pcall(
        body, name=name, grid=(nb,),
        in_specs=[col(2 * nb), pl.BlockSpec((None, SUBLANES, LANES), lambda c: (c, 0, 0)), vec, diag, diag, vec, vec, vec],
        out_specs=[col(0)] * 4,
        out_shape=[jax.ShapeDtypeStruct((S, M), F32)] * 4,
        scratch_shapes=[pltpu.VMEM((S + SUBLANES, LANES), F32), pltpu.VMEM((S, LANES), F32), pltpu.VMEM((S, LANES), F32)],
        compiler_params=_params(("arbitrary",)),
    )(proj, rnnw4, rnn_b, bda, bdi, b_a, b_i, lam)


def rnn_bwd(dhout, h, xr, ra, ii, proj, rnnw4, bda, bdi, lam, name):
    S, M = h.shape
    nb = M // LANES
    tt = min(SEQ_TT, S)
    KW = RNN_CONV_WIDTH
    SL = SUBLANES
    nseg = SCAN_SEGMENTS if S % (SCAN_SEGMENTS * tt) == 0 else 1

    def body(dh_ref, h_ref, xr_ref, ra_ref, ii_ref, ux_ref, w_ref, bda_ref, bdi_ref, lam_ref,
             dux_ref, dwa_ref, dwi_ref, drw_ref, vec_ref,
             a_sc, hp, g_sc, dpa_sc, dpi_sc, dxp, uxp, acc8, drw8, p_sc):
        zero8 = jnp.zeros((SL, LANES), F32)
        a_sc[S:S + SL, :] = zero8
        hp[0:SL, :] = zero8
        dxp[S:S + SL, :] = zero8
        uxp[0:SL, :] = zero8
        acc8[...] = jnp.zeros_like(acc8)
        drw8[...] = jnp.zeros_like(drw8)
        lamv = lam_ref[...]
        ls = _log_sigmoid(lamv)

        def fill(t, carry):
            base = pl.multiple_of(t * tt, tt)
            a_sc[pl.ds(base, tt), :] = jnp.exp(RG_C * ra_ref[pl.ds(base, tt), :] * ls)
            hp[pl.ds(base + SL, tt), :] = h_ref[pl.ds(base, tt), :]
            uxp[pl.ds(base + SL, tt), :] = ux_ref[pl.ds(base, tt), :].astype(F32)
            return carry

        lax.fori_loop(0, S // tt, fill, 0)

        rows = lax.broadcasted_iota(I32, (SL, LANES), 0)
        seg = S // nseg
        nt8 = seg // SL
        first = lambda v: jnp.broadcast_to(v[0:1, :], (SL, LANES))

        def rscan(t, carry):
            gs, ps = carry
            new_g, new_p = [], []
            for s in range(nseg):
                base = pl.multiple_of(s * seg + (nt8 - 1 - t) * SL, SL)
                A = a_sc[pl.ds(base + 1, SL), :]
                B = dh_ref[pl.ds(base, SL), :]
                for d in (1, 2, 4):
                    As = jnp.where(rows < SL - d, pltpu.roll(A, SL - d, axis=0), 1.0)
                    Bs = jnp.where(rows < SL - d, pltpu.roll(B, SL - d, axis=0), 0.0)
                    B = A * Bs + B
                    A = A * As
                g = B + A * gs[s]
                g_sc[pl.ds(base, SL), :] = g
                pp = A * ps[s]
                if s < nseg - 1:
                    p_sc[pl.ds(base, SL), :] = pp
                new_g.append(first(g))
                new_p.append(first(pp))
            return tuple(new_g), tuple(new_p)

        one8 = jnp.ones((SL, LANES), F32)
        gs, ps = lax.fori_loop(0, nt8, rscan, ((zero8,) * nseg, (one8,) * nseg))
        carry_in = gs[nseg - 1]
        for s in range(nseg - 2, -1, -1):
            c_row = carry_in[0:1, :]

            def fix(t, c, s=s, c_row=c_row):
                base = pl.multiple_of(s * seg + t * tt, tt)
                g_sc[pl.ds(base, tt), :] = g_sc[pl.ds(base, tt), :] + p_sc[pl.ds(base, tt), :] * c_row
                return c

            lax.fori_loop(0, seg // tt, fix, 0)
            carry_in = gs[s] + ps[s] * carry_in

        def red8(v):
            return v.reshape(tt // SL, SL, LANES).sum(axis=0)

        def step(t, carry):
            base = pl.multiple_of(t * tt, tt)
            g = g_sc[pl.ds(base, tt), :]
            hprev = hp[pl.ds(base + SL - 1, tt), :]
            xr_t = xr_ref[pl.ds(base, tt), :]
            ra_t = ra_ref[pl.ds(base, tt), :]
            ii_t = ii_ref[pl.ds(base, tt), :]
            a, mult = _rg_gate_terms(ra_t, ls)
            gx = g * xr_t
            dmult = gx * ii_t
            dii = gx * mult
            dxr = g * (mult * ii_t)
            dla = g * hprev * a - dmult * (a * a) / mult
            acc8[3] += red8(dla * ra_t)
            dpa = dla * (RG_C * ls) * ra_t * (1.0 - ra_t)
            dpi = dii * ii_t * (1.0 - ii_t)
            dpab = dpa.astype(BF)
            dpib = dpi.astype(BF)
            dxr = dxr + (lax.dot_general(dpab, bda_ref[...], CONTRACT_LAST, preferred_element_type=F32)
                         + lax.dot_general(dpib, bdi_ref[...], CONTRACT_LAST, preferred_element_type=F32))
            dpa_sc[pl.ds(base, tt), :] = dpab
            dpi_sc[pl.ds(base, tt), :] = dpib
            dxp[pl.ds(base, tt), :] = dxr
            acc8[0] += red8(dxr)
            acc8[1] += red8(dpa)
            acc8[2] += red8(dpi)
            return carry

        lax.fori_loop(0, S // tt, step, 0)

        def convb(t, carry):
            base = pl.multiple_of(t * tt, tt)
            d_t = dxp[pl.ds(base, tt), :]
            dux = jnp.zeros((tt, LANES), F32)
            for k in range(KW):
                drw8[k] += red8(d_t * uxp[pl.ds(base + (SL - KW + 1) + k, tt), :])
                dux = dux + w_ref[k:k + 1, :] * dxp[pl.ds(base + (KW - 1) - k, tt), :]
            dux_ref[pl.ds(base, tt), :] = dux.astype(BF)
            return carry

        lax.fori_loop(0, S // tt, convb, 0)

        xb = xr_ref[...].astype(BF)
        dwa_ref[...] = lax.dot_general(xb, dpa_sc[...], CONTRACT_FIRST, preferred_element_type=F32)
        dwi_ref[...] = lax.dot_general(xb, dpi_sc[...], CONTRACT_FIRST, preferred_element_type=F32)
        drw_ref[...] = jnp.zeros_like(drw_ref)
        vec_ref[...] = jnp.zeros_like(vec_ref)
        for k in range(KW):
            drw_ref[k:k + 1, :] = _colsum(drw8[k])
        for k in range(3):
            vec_ref[k:k + 1, :] = _colsum(acc8[k])
        vec_ref[3:4, :] = _colsum(acc8[3]) * (RG_C * _sigmoid(-lamv))

    col = lambda o: pl.BlockSpec((S, LANES), lambda c: (0, c + o))
    vec = pl.BlockSpec((1, LANES), lambda c: (0, c))
    diag = pl.BlockSpec((LANES, LANES), lambda c: (c, c))
    blk3 = lambda r: pl.BlockSpec((None, r, LANES), lambda c: (c, 0, 0))
    return _pcall(
        body, name=name, grid=(nb,),
        in_specs=[col(0), col(0), col(0), col(0), col(0), col(2 * nb), blk3(SL), diag, diag, vec],
        out_specs=[col(0), blk3(LANES), blk3(LANES), blk3(SL), pl.BlockSpec((SL, LANES), lambda c: (0, c))],
        out_shape=[jax.ShapeDtypeStruct((S, M), BF), jax.ShapeDtypeStruct((nb, LANES, LANES), F32),
                   jax.ShapeDtypeStruct((nb, LANES, LANES), F32), jax.ShapeDtypeStruct((nb, SL, LANES), F32),
                   jax.ShapeDtypeStruct((SL, M), F32)],
        scratch_shapes=[pltpu.VMEM((S + SL, LANES), F32), pltpu.VMEM((S + SL, LANES), F32), pltpu.VMEM((S, LANES), F32),
                        pltpu.VMEM((S, LANES), BF), pltpu.VMEM((S, LANES), BF), pltpu.VMEM((S + SL, LANES), F32),
                        pltpu.VMEM((S + SL, LANES), F32), pltpu.VMEM((SL, SL, LANES), F32), pltpu.VMEM((SL, SL, LANES), F32),
                        pltpu.VMEM((S, LANES), F32)],
        compiler_params=_params(("arbitrary",)),
    )(dhout, h, xr, ra, ii, proj, rnnw4, bda, bdi, lam)


GELU_K = 0.7978845608028654
GELU_C = 0.044715


def _layernorm_parts(cq):
    mu = _rowmean(cq)
    d = cq - mu
    rstd = lax.rsqrt(_rowmean(d * d) + EPS)
    return d * rstd, rstd


def mix_out(cq, proj, h, x, vecs, lnv, wout, name):
    S, D = x.shape
    M = cq.shape[1]
    ts = min(512, S)

    def body(cq_ref, uy_ref, h_ref, x_ref, v_ref, ln_ref, w_ref, xo_ref, ym_ref, yc_ref):
        z, _ = _layernorm_parts(cq_ref[...])
        l = z * _row(ln_ref, 0) + _row(ln_ref, 1)
        yc_ref[:, 0:M] = (l * _sigmoid(l)).astype(BF)
        uy = uy_ref[...].astype(F32)
        gelu = 0.5 * uy * (1.0 + jnp.tanh(GELU_K * (uy + GELU_C * uy * uy * uy)))
        yc_ref[:, M:2 * M] = (gelu * h_ref[...]).astype(BF)
        ym = jnp.dot(yc_ref[...], w_ref[...], preferred_element_type=F32)
        ym_ref[...] = ym.astype(BF)
        xo_ref[...] = x_ref[...] + _row(v_ref, R_GT2) * ym

    tok = pl.BlockSpec((ts, D), lambda i: (i, 0))
    mtok = lambda o: pl.BlockSpec((ts, M), lambda i: (i, o))
    return _pcall(
        body, name=name, grid=(S // ts,),
        in_specs=[mtok(0), mtok(3), mtok(0), tok, pl.BlockSpec(vecs.shape, lambda i: (0, 0)),
                  pl.BlockSpec(lnv.shape, lambda i: (0, 0)), pl.BlockSpec(wout.shape, lambda i: (0, 0))],
        out_specs=[tok, tok, pl.BlockSpec((ts, 2 * M), lambda i: (i, 0))],
        out_shape=[jax.ShapeDtypeStruct((S, D), F32), jax.ShapeDtypeStruct((S, D), BF),
                   jax.ShapeDtypeStruct((S, 2 * M), BF)],
        compiler_params=_params(("arbitrary",)),
    )(cq, proj, h, x, vecs, lnv, wout)


def mix_out_bwd(dxo, ym, vecs, wout, cq, lnv, proj, h, name):
    S, D = dxo.shape
    M = cq.shape[1]
    ts = min(512, S)

    def body(dxo_ref, ym_ref, v_ref, w_ref, cq_ref, ln_ref, uy_ref, h_ref,
             dcq_ref, dh_ref, duy_ref, dyb_ref, vgd_ref, vgm_ref):
        @pl.when(pl.program_id(0) == 0)
        def _():
            vgd_ref[...] = jnp.zeros_like(vgd_ref)
            vgm_ref[...] = jnp.zeros_like(vgm_ref)

        dxo_v = dxo_ref[...]
        dyb = (_row(v_ref, R_GT2) * dxo_v).astype(BF)
        dyb_ref[...] = dyb
        vgd_ref[0:1, :] += _colsum(dxo_v * ym_ref[...].astype(F32))
        dycat = lax.dot_general(dyb, w_ref[...], CONTRACT_LAST, preferred_element_type=F32)
        dyc = dycat[:, 0:M]
        dyr = dycat[:, M:2 * M]
        z, rstd = _layernorm_parts(cq_ref[...])
        lng = _row(ln_ref, 0)
        l = z * lng + _row(ln_ref, 1)
        sl = _sigmoid(l)
        dl = dyc * (sl * (1.0 + l * (1.0 - sl)))
        vgm_ref[0:1, :] += _colsum(dl * z)
        vgm_ref[1:2, :] += _colsum(dl)
        dz = dl * lng
        dcq_ref[...] = (rstd * (dz - _rowmean(dz) - z * _rowmean(dz * z))).astype(BF)
        uy = uy_ref[...].astype(F32)
        u2 = uy * uy
        th = jnp.tanh(GELU_K * (uy + GELU_C * uy * u2))
        gelu = 0.5 * uy * (1.0 + th)
        dgelu = 0.5 * (1.0 + th) + 0.5 * uy * (1.0 - th * th) * (GELU_K * (1.0 + 3.0 * GELU_C * u2))
        dh_ref[...] = dyr * gelu
        duy_ref[...] = (dyr * h_ref[...] * dgelu).astype(BF)

    tok = pl.BlockSpec((ts, D), lambda i: (i, 0))
    mtok = lambda o: pl.BlockSpec((ts, M), lambda i: (i, o))
    return _pcall(
        body, name=name, grid=(S // ts,),
        in_specs=[tok, tok, pl.BlockSpec(vecs.shape, lambda i: (0, 0)), pl.BlockSpec(wout.shape, lambda i: (0, 0)),
                  mtok(0), pl.BlockSpec(lnv.shape, lambda i: (0, 0)), mtok(3), mtok(0)],
        out_specs=[mtok(0), mtok(0), mtok(0), tok, pl.BlockSpec((SUBLANES, D), lambda i: (0, 0)),
                   pl.BlockSpec((SUBLANES, M), lambda i: (0, 0))],
        out_shape=[jax.ShapeDtypeStruct((S, M), BF), jax.ShapeDtypeStruct((S, M), F32), jax.ShapeDtypeStruct((S, M), BF),
                   jax.ShapeDtypeStruct((S, D), BF),
                   jax.ShapeDtypeStruct((SUBLANES, D), F32), jax.ShapeDtypeStruct((SUBLANES, M), F32)],
        compiler_params=_params(("arbitrary",)),
    )(dxo, ym, vecs, wout, cq, lnv, proj, h)


def mix_in_bwd(dparts, x, dxo, vecs, win, name):
    S, D = x.shape
    M = dparts[0].shape[1]
    ts = min(512, S)

    def body(d0, d1, d2, d3, x_ref, dxo_ref, v_ref, w_ref, dx_ref, hb_ref, dp_ref, vg_ref):
        @pl.when(pl.program_id(0) == 0)
        def _():
            vg_ref[...] = jnp.zeros_like(vg_ref)

        for q, dref in enumerate((d0, d1, d2, d3)):
            dp_ref[:, q * M:(q + 1) * M] = dref[...].astype(BF)
        dh = lax.dot_general(dp_ref[...], w_ref[...], CONTRACT_LAST, preferred_element_type=F32)
        xv = x_ref[...]
        r = lax.rsqrt(_rowmean(xv * xv) + EPS)
        n = xv * r
        g = _row(v_ref, R_G2)
        sc1 = 1.0 + _row(v_ref, R_SC2)
        gsc = g * sc1
        hb_ref[...] = (n * gsc + _row(v_ref, R_SH2)).astype(BF)
        dhn = dh * n
        vg_ref[0:1, :] += _colsum(dh)
        vg_ref[1:2, :] += _colsum(dhn) * g
        vg_ref[2:3, :] += _colsum(dhn) * sc1
        dn = dh * gsc
        dx_ref[...] = dxo_ref[...] + r * (dn - n * _rowmean(dn * n))

    tok = pl.BlockSpec((ts, D), lambda i: (i, 0))
    mtok = pl.BlockSpec((ts, M), lambda i: (i, 0))
    return _pcall(
        body, name=name, grid=(S // ts,),
        in_specs=[mtok] * 4 + [tok, tok, pl.BlockSpec(vecs.shape, lambda i: (0, 0)), pl.BlockSpec(win.shape, lambda i: (0, 0))],
        out_specs=[tok, tok, pl.BlockSpec((ts, 4 * M), lambda i: (i, 0)), pl.BlockSpec((SUBLANES, D), lambda i: (0, 0))],
        out_shape=[jax.ShapeDtypeStruct((S, D), F32), jax.ShapeDtypeStruct((S, D), BF),
                   jax.ShapeDtypeStruct((S, 4 * M), BF), jax.ShapeDtypeStruct((SUBLANES, D), F32)],
        compiler_params=_params(("arbitrary",)),
    )(*dparts, x, dxo, vecs, win)


def _adamw(w, g, m, v):
    m = ADAM_B1 * m + (1.0 - ADAM_B1) * g
    v = ADAM_B2 * v + (1.0 - ADAM_B2) * (g * g)
    m_hat = m / (1.0 - ADAM_B1 ** ADAM_STEP)
    v_hat = v / (1.0 - ADAM_B2 ** ADAM_STEP)
    delta = -ADAM_LR * (m_hat / (jnp.sqrt(v_hat) + ADAM_EPS) + ADAM_WD * w)
    return delta, m, v


def adam_big(w, g, m, v, name):
    R, C = w.shape
    tr = 256 if R % 256 == 0 else R // 2 if (R // 2) % SUBLANES == 0 and R > 512 else R
    tc = C if C <= 1536 else (1152 if C % 1152 == 0 else 1024)
    assert R % tr == 0 and C % tc == 0

    def body(w_ref, g_ref, m_ref, v_ref, d_ref, nm_ref, nv_ref):
        d, nm, nv = _adamw(w_ref[...], g_ref[...], m_ref[...], v_ref[...])
        d_ref[...] = d
        nm_ref[...] = nm
        nv_ref[...] = nv

    blk = pl.BlockSpec((tr, tc), lambda i, j: (i, j))
    return _pcall(
        body, name=name, grid=(R // tr, C // tc), in_specs=[blk] * 4, out_specs=[blk] * 3,
        out_shape=[jax.ShapeDtypeStruct((R, C), F32)] * 3, compiler_params=_params(("parallel", "parallel")),
    )(w, g, m, v)


def adam_cond(c_all, dmod, w, m, v, name):
    B, Kin = c_all.shape
    N = w.shape[1]
    tn = 768 if N % 768 == 0 else 256
    assert N % tn == 0

    def body(c_ref, d_ref, w_ref, m_ref, v_ref, g_ref, dl_ref, nm_ref, nv_ref):
        cv = c_ref[...]
        ca = cv * _sigmoid(cv)
        g = lax.dot_general(ca.astype(BF), d_ref[...].astype(BF), CONTRACT_FIRST, preferred_element_type=F32)
        d, nm, nv = _adamw(w_ref[...], g, m_ref[...], v_ref[...])
        g_ref[...] = g
        dl_ref[...] = d
        nm_ref[...] = nm
        nv_ref[...] = nv

    blk = pl.BlockSpec((Kin, tn), lambda n: (0, n))
    return _pcall(
        body, name=name, grid=(N // tn,),
        in_specs=[pl.BlockSpec((B, Kin), lambda n: (0, 0)), pl.BlockSpec((B, tn), lambda n: (0, n)), blk, blk, blk],
        out_specs=[blk] * 4, out_shape=[jax.ShapeDtypeStruct((Kin, N), F32)] * 4,
        compiler_params=_params(("parallel",)),
    )(c_all, dmod, w, m, v)


def adam_small(ws, gs, ms, vs, name):
    n = len(ws)

    def body(*refs):
        ins, outs = refs[:4 * n], refs[4 * n:]
        for k in range(n):
            d, nm, nv = _adamw(ins[k][...], ins[n + k][...], ins[2 * n + k][...], ins[3 * n + k][...])
            outs[k][...] = d
            outs[n + k][...] = nm
            outs[2 * n + k][...] = nv

    specs = [pl.BlockSpec(w.shape, lambda i: (0, 0)) for w in ws]
    shapes = [jax.ShapeDtypeStruct(w.shape, F32) for w in ws]
    out = _pcall(body, name=name, grid=(1,), in_specs=specs * 4, out_specs=specs * 3, out_shape=shapes * 3,
                 compiler_params=_params(("arbitrary",)))(*ws, *gs, *ms, *vs)
    return out[:n], out[n:2 * n], out[2 * n:]


def _me():
    return lax.axis_index("x"), lax.axis_index("y"), lax.axis_index("c")


def _flip(x, y, p):
    return (x ^ (p >> 1) if (p >> 1) else x), (y ^ (p & 1) if (p & 1) else y)


def _handshake(peers):
    barrier = pltpu.get_barrier_semaphore()
    for peer in peers:
        pl.semaphore_signal(barrier, inc=1, device_id=peer, device_id_type=MESH)
    pl.semaphore_wait(barrier, len(peers))


def _seq_call(body, *, name, n_in, out_shape, sem_shapes, collective_id):
    del n_in
    return pl.kernel(body, out_type=out_shape, mesh=plsc.ScalarSubcoreMesh(axis_name="sq", num_cores=1), name=name,
                     scratch_types=sem_shapes, compiler_params=pltpu.CompilerParams(collective_id=collective_id))


def _hbm_comm_call(body, *, name, n_in, out_shape, sem_shapes, seq_id):
    if seq_id is not None:
        return _seq_call(body, name=name, n_in=n_in, out_shape=out_shape, sem_shapes=sem_shapes, collective_id=seq_id)
    anyspec = pl.BlockSpec(memory_space=pl.ANY)
    return _pcall(body, name=name, in_specs=[anyspec] * n_in, out_specs=[anyspec] * len(out_shape), out_shape=out_shape,
                  scratch_shapes=sem_shapes, compiler_params=_params())


def allgather_devices(v, name, with_sum=False):
    R, L = v.shape

    def body(v_ref, out_ref, *rest):
        if with_sum:
            sum_ref, send_sems, recv_sems = rest
        else:
            send_sems, recv_sems = rest
        x, y, c = _me()
        me = 4 * x + 2 * y + c
        out_ref[me] = v_ref[...]
        copies = []
        for p in range(1, N_DEV):
            px, py = _flip(x, y, p >> 1)
            pc = (1 - c) if (p & 1) else c
            peer = 4 * px + 2 * py + pc
            send = pltpu.make_async_remote_copy(src_ref=v_ref, dst_ref=out_ref.at[me], send_sem=send_sems.at[p - 1],
                                                recv_sem=recv_sems.at[p - 1], device_id=(px, py, pc), device_id_type=MESH)
            send.start()
            recv = pltpu.make_async_remote_copy(src_ref=v_ref, dst_ref=out_ref.at[peer], send_sem=send_sems.at[p - 1],
                                                recv_sem=recv_sems.at[p - 1], device_id=(px, py, pc), device_id_type=MESH)
            copies.append((send, recv))
        for send, recv in copies:
            recv.wait_recv()
        for send, recv in copies:
            send.wait_send()
        if with_sum:
            s = out_ref[0]
            for k in range(1, N_DEV):
                s = s + out_ref[k]
            sum_ref[...] = s

    vm = pl.BlockSpec(memory_space=pltpu.VMEM)
    out_shape = [jax.ShapeDtypeStruct((N_DEV, R, L), F32)]
    if with_sum:
        out_shape.append(jax.ShapeDtypeStruct((R, L), F32))
    return _pcall(
        body, name=name, in_specs=[vm], out_specs=[vm] * len(out_shape), out_shape=out_shape,
        scratch_shapes=[pltpu.SemaphoreType.DMA((N_DEV - 1,)), pltpu.SemaphoreType.DMA((N_DEV - 1,))],
        compiler_params=_params(),
    )(v)


def allgather_devices_hbm(v, name, seq_id):
    R, L = v.shape

    def body(v_ref, out_ref, send_sems, recv_sems, local_sem):
        x, y, c = _me()
        me = 4 * x + 2 * y + c
        peers = []
        for p in range(1, N_DEV):
            px, py = _flip(x, y, p >> 1)
            peers.append((px, py, (1 - c) if (p & 1) else c))
        _handshake(peers)
        lc = pltpu.make_async_copy(v_ref, out_ref.at[me], local_sem)
        lc.start()
        copies = []
        for p, (px, py, pc) in enumerate(peers):
            send = pltpu.make_async_remote_copy(src_ref=v_ref, dst_ref=out_ref.at[me], send_sem=send_sems.at[p],
                                                recv_sem=recv_sems.at[p], device_id=(px, py, pc), device_id_type=MESH)
            send.start()
            recv = pltpu.make_async_remote_copy(src_ref=v_ref, dst_ref=out_ref.at[4 * px + 2 * py + pc], send_sem=send_sems.at[p],
                                                recv_sem=recv_sems.at[p], device_id=(px, py, pc), device_id_type=MESH)
            copies.append((send, recv))
        for send, recv in copies:
            recv.wait_recv()
        for send, recv in copies:
            send.wait_send()
        lc.wait()

    return _seq_call(body, name=name, n_in=1, out_shape=[jax.ShapeDtypeStruct((N_DEV, R, L), F32)],
                     sem_shapes=[pltpu.SemaphoreType.DMA((N_DEV - 1,)), pltpu.SemaphoreType.DMA((N_DEV - 1,)),
                                 pltpu.SemaphoreType.DMA], collective_id=seq_id)(v)[0]


def sum_slots(g, name):
    n, R, L = g.shape
    tr = 216 if R % 216 == 0 else R
    assert R % tr == 0 and tr % SUBLANES == 0

    def body(g_ref, o_ref):
        s = g_ref[0]
        for k in range(1, n):
            s = s + g_ref[k]
        o_ref[...] = s

    return _pcall(body, name=name, grid=(R // tr,), in_specs=[pl.BlockSpec((n, tr, L), lambda i: (0, i, 0))],
                  out_specs=pl.BlockSpec((tr, L), lambda i: (i, 0)), out_shape=jax.ShapeDtypeStruct((R, L), F32),
                  compiler_params=_params(("parallel",)))(g)


def allgather_chips(v, name):
    R, L = v.shape

    def body(v_ref, out_ref, send_sems, recv_sems):
        x, y, c = _me()
        chip = 2 * x + y
        out_ref[chip] = v_ref[...]
        copies = []
        for p in range(1, N_CHIPS):
            px, py = _flip(x, y, p)
            send = pltpu.make_async_remote_copy(src_ref=v_ref, dst_ref=out_ref.at[chip], send_sem=send_sems.at[p - 1],
                                                recv_sem=recv_sems.at[p - 1], device_id=(px, py, c), device_id_type=MESH)
            send.start()
            recv = pltpu.make_async_remote_copy(src_ref=v_ref, dst_ref=out_ref.at[2 * px + py], send_sem=send_sems.at[p - 1],
                                                recv_sem=recv_sems.at[p - 1], device_id=(px, py, c), device_id_type=MESH)
            copies.append((send, recv))
        for send, recv in copies:
            recv.wait_recv()
        for send, recv in copies:
            send.wait_send()

    vm = pl.BlockSpec(memory_space=pltpu.VMEM)
    return _pcall(
        body, name=name, in_specs=[vm], out_specs=vm, out_shape=jax.ShapeDtypeStruct((N_CHIPS, R, L), F32),
        scratch_shapes=[pltpu.SemaphoreType.DMA((N_CHIPS - 1,)), pltpu.SemaphoreType.DMA((N_CHIPS - 1,))],
        compiler_params=_params(),
    )(v)


def _shard_window(ref, kind, shard_shape, chip, half):
    r, c = shard_shape
    hr = r // 2
    if kind == "col":
        return ref.at[pl.ds(pl.multiple_of(half * hr, hr), hr), pl.ds(pl.multiple_of(chip * c, c), c)]
    return ref.at[pl.ds(pl.multiple_of(chip * r + half * hr, hr), hr), :]


def allgather_weights(shards, kinds, name, seq_id=None):
    n = len(shards)
    fulls = []
    for s, kind in zip(shards, kinds):
        r, c = s.shape
        fulls.append(jax.ShapeDtypeStruct((r, N_CHIPS * c) if kind == "col" else (N_CHIPS * r, c), s.dtype))

    def body(*refs):
        srcs, outs = refs[:n], refs[n:2 * n]
        send_sems, recv_sems, fsend_sems, frecv_sems = refs[2 * n:]
        x, y, c = _me()
        chip = 2 * x + y
        sib = (x, y, 1 - c)
        if seq_id is not None:
            _handshake([(*_flip(x, y, p), c) for p in range(1, N_CHIPS)] + [sib])
        sends, fwds = [], []
        for i in range(n):
            shp = srcs[i].shape
            hr = shp[0] // 2
            my_half = srcs[i].at[pl.ds(pl.multiple_of(c * hr, hr), hr), :]
            for p in range(1, N_CHIPS):
                px, py = _flip(x, y, p)
                k = i * (N_CHIPS - 1) + p - 1
                cp = pltpu.make_async_remote_copy(src_ref=my_half, dst_ref=_shard_window(outs[i], kinds[i], shp, chip, c),
                                                  send_sem=send_sems.at[k], recv_sem=recv_sems.at[k],
                                                  device_id=(px, py, c), device_id_type=MESH)
                cp.start()
                sends.append(cp)
        for i in range(n):
            shp = srcs[i].shape
            for p in range(1, N_CHIPS):
                px, py = _flip(x, y, p)
                k = i * (N_CHIPS - 1) + p - 1
                landed = _shard_window(outs[i], kinds[i], shp, 2 * px + py, c)
                pltpu.make_async_remote_copy(src_ref=landed, dst_ref=landed, send_sem=send_sems.at[k], recv_sem=recv_sems.at[k],
                                             device_id=(px, py, c), device_id_type=MESH).wait_recv()
                fw = pltpu.make_async_remote_copy(src_ref=landed, dst_ref=landed, send_sem=fsend_sems.at[k],
                                                  recv_sem=frecv_sems.at[k], device_id=sib, device_id_type=MESH)
                fw.start()
                fwds.append(fw)
        for i in range(n):
            shp = srcs[i].shape
            for p in range(1, N_CHIPS):
                px, py = _flip(x, y, p)
                k = i * (N_CHIPS - 1) + p - 1
                other = _shard_window(outs[i], kinds[i], shp, 2 * px + py, 1 - c)
                pltpu.make_async_remote_copy(src_ref=other, dst_ref=other, send_sem=fsend_sems.at[k], recv_sem=frecv_sems.at[k],
                                             device_id=sib, device_id_type=MESH).wait_recv()
        for cp in sends + fwds:
            cp.wait_send()

    nk = n * (N_CHIPS - 1)
    gathered = _hbm_comm_call(
        body, name=name, n_in=n, out_shape=fulls, seq_id=seq_id,
        sem_shapes=[pltpu.SemaphoreType.DMA((nk,)), pltpu.SemaphoreType.DMA((nk,)), pltpu.SemaphoreType.DMA((nk,)),
                    pltpu.SemaphoreType.DMA((nk,))],
    )(*shards)
    return gathered


def place_local_shards(fulls, shards, kinds, name):
    n = len(shards)
    chip = jnp.reshape(2 * lax.axis_index("x") + lax.axis_index("y"), (1,)).astype(I32)

    def body(ci_ref, *refs):
        for i in range(n):
            refs[2 * n + i][...] = refs[i][...]

    in_specs = [pl.BlockSpec(s.shape, lambda i, ci: (0, 0)) for s in shards] + [pl.BlockSpec(memory_space=pl.ANY)] * n
    out_specs = [pl.BlockSpec(s.shape, (lambda i, ci: (0, ci[0])) if k == "col" else (lambda i, ci: (ci[0], 0)))
                 for s, k in zip(shards, kinds)]
    gs = pltpu.PrefetchScalarGridSpec(num_scalar_prefetch=1, grid=(1,), in_specs=in_specs, out_specs=out_specs)
    return _pcall(body, name=name, grid_spec=gs, out_shape=[jax.ShapeDtypeStruct(f.shape, f.dtype) for f in fulls],
                  input_output_aliases={1 + n + i: i for i in range(n)}, compiler_params=_params(("arbitrary",)))(chip, *shards, *fulls)


def _as_halves(g, kind, shard_shape):
    r, c = shard_shape
    if kind == "col":
        return g.reshape(2, r // 2, N_CHIPS * c)
    return g.reshape(N_CHIPS, 2, r // 2, c)


def exchange_sibling_halves(grads, kinds, shard_shapes, name, seq_id=None):
    n = len(grads)
    views = [_as_halves(g, k, s) for g, k, s in zip(grads, kinds, shard_shapes)]
    outs = []
    for k, (r, c) in zip(kinds, shard_shapes):
        outs.append(jax.ShapeDtypeStruct((r // 2, N_CHIPS * c) if k == "col" else (N_CHIPS, r // 2, c), grads[0].dtype))

    def body(*refs):
        srcs, dsts = refs[:n], refs[n:2 * n]
        send_sems, recv_sems = refs[2 * n:]
        x, y, c = _me()
        if seq_id is not None:
            _handshake([(x, y, 1 - c)])
        cps = []
        for i in range(n):
            src = srcs[i].at[1 - c] if kinds[i] == "col" else srcs[i].at[:, 1 - c]
            cp = pltpu.make_async_remote_copy(src_ref=src, dst_ref=dsts[i], send_sem=send_sems.at[i], recv_sem=recv_sems.at[i],
                                              device_id=(x, y, 1 - c), device_id_type=MESH)
            cp.start()
            cps.append(cp)
        for cp in cps:
            cp.wait_recv()
        for cp in cps:
            cp.wait_send()

    return _hbm_comm_call(body, name=name, n_in=n, out_shape=outs, seq_id=seq_id,
                          sem_shapes=[pltpu.SemaphoreType.DMA((n,)), pltpu.SemaphoreType.DMA((n,))])(*views)


def add_sibling_half(g, recv, kind, shard_shape, cidx, name):
    r, c = shard_shape
    hr = r // 2
    gv = _as_halves(g, kind, shard_shape)
    tr = hr if hr <= 512 else (256 if hr % 256 == 0 else hr // 2)
    assert hr % tr == 0

    def body(ci_ref, g_ref, r_ref, h_ref, hb_ref):
        s = g_ref[...].astype(F32) + r_ref[...].astype(F32)
        h_ref[...] = s
        hb_ref[...] = s.astype(BF)

    if kind == "col":
        grid = (hr // tr, N_CHIPS)
        g_spec = pl.BlockSpec((None, tr, c), lambda i, k, ci: (ci[0], i, k))
        o_spec = pl.BlockSpec((tr, c), lambda i, k, ci: (i, k))
    else:
        grid = (hr // tr, N_CHIPS)
        g_spec = pl.BlockSpec((None, None, tr, c), lambda i, k, ci: (k, ci[0], i, 0))
        o_spec = pl.BlockSpec((None, tr, c), lambda i, k, ci: (k, i, 0))
    gs = pltpu.PrefetchScalarGridSpec(num_scalar_prefetch=1, grid=grid, in_specs=[g_spec, o_spec], out_specs=[o_spec, o_spec])
    return _pcall(
        body, name=name, grid_spec=gs,
        out_shape=[jax.ShapeDtypeStruct(recv.shape, F32), jax.ShapeDtypeStruct(recv.shape, BF)],
        compiler_params=_params(("parallel", "parallel")),
    )(cidx, gv, recv)


def exchange_chip_pieces(hbs, kinds, shard_shapes, name, seq_id=None):
    n = len(hbs)
    outs = [jax.ShapeDtypeStruct((N_CHIPS - 1, r // 2, c), BF) for (r, c) in shard_shapes]

    def body(*refs):
        srcs, dsts = refs[:n], refs[n:2 * n]
        send_sems, recv_sems = refs[2 * n:]
        x, y, c = _me()
        if seq_id is not None:
            _handshake([(*_flip(x, y, p), c) for p in range(1, N_CHIPS)])
        cps = []
        for i in range(n):
            cc = shard_shapes[i][1]
            for p in range(1, N_CHIPS):
                px, py = _flip(x, y, p)
                pchip = 2 * px + py
                src = (srcs[i].at[:, pl.ds(pl.multiple_of(pchip * cc, cc), cc)] if kinds[i] == "col" else srcs[i].at[pchip])
                k = i * (N_CHIPS - 1) + p - 1
                cp = pltpu.make_async_remote_copy(src_ref=src, dst_ref=dsts[i].at[p - 1], send_sem=send_sems.at[k],
                                                  recv_sem=recv_sems.at[k], device_id=(px, py, c), device_id_type=MESH)
                cp.start()
                cps.append(cp)
        for cp in cps:
            cp.wait_recv()
        for cp in cps:
            cp.wait_send()

    nk = n * (N_CHIPS - 1)
    return _hbm_comm_call(body, name=name, n_in=n, out_shape=outs, seq_id=seq_id,
                          sem_shapes=[pltpu.SemaphoreType.DMA((nk,)), pltpu.SemaphoreType.DMA((nk,))])(*hbs)


def sum_chip_pieces(h, pieces, kind, shard_shape, chip_core, name):
    r, c = shard_shape
    hr = r // 2
    tr = hr if hr <= 512 else (256 if hr % 256 == 0 else hr // 2)
    assert hr % tr == 0
    nrb = hr // tr

    def body(ci_ref, h_ref, p_ref, q_ref):
        q_ref[...] = ((h_ref[...] + p_ref[0].astype(F32)) + p_ref[1].astype(F32)) + p_ref[2].astype(F32)

    if kind == "col":
        h_spec = pl.BlockSpec((tr, c), lambda i, ci: (i, ci[0]))
    else:
        h_spec = pl.BlockSpec((None, tr, c), lambda i, ci: (ci[0], i, 0))
    gs = pltpu.PrefetchScalarGridSpec(
        num_scalar_prefetch=1, grid=(nrb,),
        in_specs=[h_spec, pl.BlockSpec((N_CHIPS - 1, tr, c), lambda i, ci: (0, i, 0))],
        out_specs=pl.BlockSpec((tr, c), lambda i, ci: (ci[1] * nrb + i, 0)))
    return _pcall(body, name=name, grid_spec=gs, out_shape=jax.ShapeDtypeStruct((r, c), F32),
                  compiler_params=_params(("parallel",)))(chip_core, h, pieces)


def exchange_reduced_halves(qs, name):
    n = len(qs)

    def body(*refs):
        bufs = refs[n:2 * n]
        send_sems, recv_sems = refs[2 * n:]
        x, y, c = _me()
        cps = []
        for i in range(n):
            hr = bufs[i].shape[0] // 2
            mine = bufs[i].at[pl.ds(pl.multiple_of(c * hr, hr), hr), :]
            other = bufs[i].at[pl.ds(pl.multiple_of((1 - c) * hr, hr), hr), :]
            cp = pltpu.make_async_remote_copy(src_ref=mine, dst_ref=mine, send_sem=send_sems.at[i], recv_sem=recv_sems.at[i],
                                              device_id=(x, y, 1 - c), device_id_type=MESH)
            cp.start()
            cps.append((cp, pltpu.make_async_remote_copy(src_ref=other, dst_ref=other, send_sem=send_sems.at[i],
                                                         recv_sem=recv_sems.at[i], device_id=(x, y, 1 - c), device_id_type=MESH)))
        for cp, rv in cps:
            rv.wait_recv()
        for cp, rv in cps:
            cp.wait_send()

    anyspec = pl.BlockSpec(memory_space=pl.ANY)
    return _pcall(
        body, name=name, in_specs=[anyspec] * n, out_specs=[anyspec] * n,
        out_shape=[jax.ShapeDtypeStruct(q.shape, F32) for q in qs], input_output_aliases={i: i for i in range(n)},
        scratch_shapes=[pltpu.SemaphoreType.DMA((n,)), pltpu.SemaphoreType.DMA((n,))],
        compiler_params=_params(),
    )(*qs)


def _rows128(a):
    return a.reshape(-1, LANES)


def _after(xs, *deps):
    flat = []
    for d in deps:
        flat.extend(d if isinstance(d, (list, tuple)) else [d])
    return list(lax.optimization_barrier((tuple(xs), tuple(flat)))[0])


def _block_diag(w):
    H, d, _ = w.shape
    eye = jnp.eye(H, dtype=w.dtype)
    return jnp.einsum("hde,hg->hdge", w, eye).reshape(H * d, H * d)


def _diag_blocks(g4, H, d):
    nb = g4.shape[0]
    per = LANES // d
    g = g4.reshape(nb, per, d, per, d)
    return jnp.stack([g[:, j, :, j, :] for j in range(per)], axis=1).reshape(H, d, d)


def kernel(x, c, w_mod, b_mod, g_ffn1, w_ffn1_in, w_ffn1_out, g_mix, w_in, conv_w, conv_b, ln_g, ln_b, rnn_conv_w, rnn_conv_b, w_a, b_a, w_i, b_i, lru_lambda, w_out, g_ffn2, w_ffn2_in, w_ffn2_out, w_fmod, b_fmod, g_final, loss_target, m_w_mod, m_b_mod, m_g_ffn1, m_w_ffn1_in, m_w_ffn1_out, m_g_mix, m_w_in, m_conv_w, m_conv_b, m_ln_g, m_ln_b, m_rnn_conv_w, m_rnn_conv_b, m_w_a, m_b_a, m_w_i, m_b_i, m_lru_lambda, m_w_out, m_g_ffn2, m_w_ffn2_in, m_w_ffn2_out, m_w_fmod, m_b_fmod, m_g_final, v_w_mod, v_b_mod, v_g_ffn1, v_w_ffn1_in, v_w_ffn1_out, v_g_mix, v_w_in, v_conv_w, v_conv_b, v_ln_g, v_ln_b, v_rnn_conv_w, v_rnn_conv_b, v_w_a, v_b_a, v_w_i, v_b_i, v_lru_lambda, v_w_out, v_g_ffn2, v_w_ffn2_in, v_w_ffn2_out, v_w_fmod, v_b_fmod, v_g_final):
    S, D = x.shape[1], x.shape[2]
    M = conv_b.shape[1]
    H, HD = w_a.shape[1], w_a.shape[2]
    nb = M // LANES
    ix, iy, ic = lax.axis_index("x"), lax.axis_index("y"), lax.axis_index("c")
    chip = 2 * ix + iy
    dev = 2 * chip + ic
    cidx = jnp.reshape(ic, (1,)).astype(I32)
    chip_core = jnp.stack([chip, ic]).astype(I32)
    xs = x[0]
    tgt = loss_target[0]

    kinds = ["col", "row"]
    w_f1, w_mx, w_f2 = [w_ffn1_in[0], w_ffn1_out[0]], [w_in[0], w_out[0]], [w_ffn2_in[0], w_ffn2_out[0]]
    as_bf = lambda ws: [w.astype(BF) for w in ws]
    shapes_of = lambda ws: [w.shape for w in ws]
    b_f1, b_mx, b_f2 = as_bf(w_f1), as_bf(w_mx), as_bf(w_f2)
    got_f1i = allgather_weights(b_f1[:1], kinds[:1], "gather_ffn1_in", seq_id=9)
    got_f1o = allgather_weights(b_f1[1:], kinds[1:], "gather_ffn1_out", seq_id=13)
    got_mx = allgather_weights(b_mx, kinds, "gather_mix", seq_id=1)
    got_f2 = allgather_weights(b_f2, kinds, "gather_ffn2", seq_id=2)

    c_all =allgather_devices(_rows128(c), "gather_c")[0].reshape(N_DEV, D)
    mod_cols = cond_matmul(c_all, w_mod[0], "mod_proj")
    fmod_cols = cond_matmul(c_all, w_fmod, "fmod_proj")
    convw_pad = jnp.pad(conv_w[0], ((0, 32 - CONV_WIDTH), (0, 0)))
    rnnw_pad = jnp.pad(rnn_conv_w[0], ((0, SUBLANES - RNN_CONV_WIDTH), (0, 0)))
    n_mod, n_fmod = mod_cols.shape[1], fmod_cols.shape[1]
    small = jnp.concatenate([_rows128(mod_cols), _rows128(fmod_cols), convw_pad, rnnw_pad], axis=0)
    small4 = allgather_chips(small, "gather_cond")
    r0 = N_DEV * n_mod // LANES
    r1 = r0 + N_DEV * n_fmod // LANES
    mod_all = small4[:, :r0].reshape(N_CHIPS, N_DEV, n_mod)
    fmod_all = small4[:, r0:r1].reshape(N_CHIPS, N_DEV, n_fmod)
    convw4 = small4[:, r1:r1 + 32]
    rnnw4 = small4[:, r1 + 32:r1 + 32 + SUBLANES]
    mod_row = lax.dynamic_index_in_dim(mod_all, dev, axis=1, keepdims=False).reshape(1, N_CHIPS * n_mod) + b_mod
    fmod_row = lax.dynamic_index_in_dim(fmod_all, dev, axis=1, keepdims=False).reshape(1, N_CHIPS * n_fmod) + b_fmod[None, :]
    vecs = jnp.concatenate([mod_row.reshape(9, D), fmod_row.reshape(2, D), g_ffn1, g_mix, g_ffn2, g_final[None, :],
                            jnp.zeros((1, D), F32)], axis=0)
    lnv = jnp.concatenate([ln_g, ln_b, jnp.zeros((SUBLANES - 2, M), F32)], axis=0)
    bda = _block_diag(w_a[0]).astype(BF)
    bdi = _block_diag(w_i[0]).astype(BF)

    def reduce_add(gs, recv, ws, tag, kinds_=kinds):
        pairs = [add_sibling_half(g, r_, k, w.shape, cidx, f"add_sibling_{tag}{j}")
                 for j, (g, r_, k, w) in enumerate(zip(gs, recv, kinds_, ws))]
        return [p[0] for p in pairs], [p[1] for p in pairs]

    def reduce_sum(hs_, recv, ws, tag, kinds_=kinds):
        return [sum_chip_pieces(h_, p_, k, w.shape, chip_core, f"sum_chips_{tag}{j}")
                for j, (h_, p_, k, w) in enumerate(zip(hs_, recv, kinds_, ws))]

    rows1 = (R_SH1, R_SC1, R_GT1, R_G1)
    rows3 = (R_SH3, R_SC3, R_GT3, R_G3)
    (wi1,) = place_local_shards(got_f1i, b_f1[:1], kinds[:1], "place_ffn1_in")
    g1s, u1s, a1s = ffn_fwd_in(xs, vecs, wi1, rows1, "ffn1_fwd_in")
    (wo1,) = place_local_shards(_after(got_f1o, a1s), b_f1[1:], kinds[1:], "place_ffn1_out")
    x1, y1 = ffn_fwd_out(a1s, xs, vecs, wo1, rows1, "ffn1_fwd_out")
    win, wout = place_local_shards(_after(got_mx, x1), b_mx, kinds, "place_mix")
    proj = norm_matmul(x1, vecs, win, (R_SH2, R_SC2, R_G2), "mix_in_proj")
    cq = conv_fwd(proj, convw4, conv_b, "conv_fwd")
    xr, ra, ii, hh = rnn_fwd(proj, rnnw4, rnn_conv_b, bda, bdi, b_a, b_i, lru_lambda, "rnn_fwd")
    x2, ym, ycat = mix_out(cq, proj, hh, x1, vecs, lnv, wout, "mix_out")
    wi2, wo2 = place_local_shards(_after(got_f2, x2), b_f2, kinds, "place_ffn2")
    dx3, g2s, u2s, y2, vgf = ffn_fwd(x2, vecs, wi2, wo2, rows3, "ffn2_fwd", final_tgt=tgt)

    Fd = wo1.shape[0]
    tk = min(2048, S)
    dx2, act2, dg2, du2, h3b, dy2b, vg3 = ffn_bwd(dx3, x2, vecs, g2s, u2s, y2, wi2, wo2, rows3, "ffn2_bwd")
    gwo2 = matmul(act2, dy2b, "tn", tm=Fd // 2, tn=D, tk=tk, out_dtype=BF, name="ffn2_dwo")
    gwi2 = matmul(h3b, dg2, "tn", tm=D, tn=Fd // 2, tk=tk, out_dtype=BF, name="ffn2_dwg", out_cols=2 * Fd)
    gwi2 = matmul(h3b, du2, "tn", tm=D, tn=Fd // 2, tk=tk, out_dtype=BF, name="ffn2_dwu", out_cols=2 * Fd, col_off=Fd, prev=gwi2)
    recv1_f2 = exchange_sibling_halves([gwi2, gwo2], kinds, shapes_of(w_f2), "reduce1_ffn2", seq_id=3)
    dcq, dhout, duy, dymb, vgd, vgm = mix_out_bwd(dx2, ym, vecs, wout, cq, lnv, proj, hh, "mix_out_bwd")
    gwout = matmul(ycat, dymb, "tn", tm=2 * M, tn=D, tk=tk, out_dtype=BF, name="mix_dwout")
    recv1_f2 = _after(recv1_f2, gwout)
    h_f2, hb_f2 = reduce_add([gwi2, gwo2], recv1_f2, w_f2, "ffn2_")
    recv2_f2 = exchange_chip_pieces(hb_f2, kinds, shapes_of(w_f2), "reduce2_ffn2", seq_id=4)
    duv, dug, dconvw4, dconvb = conv_bwd(_after([dcq], hb_f2)[0], proj, convw4, "conv_bwd")
    dux, dwa4, dwi4, drnnw4, rvec = rnn_bwd(dhout, hh, xr, ra, ii, proj, rnnw4, bda, bdi, lru_lambda, "rnn_bwd")
    dx1, h2b, dpb, vg2 = mix_in_bwd((duv, dug, dux, duy), x1, dx2, vecs, win, "mix_in_bwd")
    gwin = matmul(h2b, dpb, "tn", tm=D, tn=1024, tk=tk, out_dtype=BF, name="mix_dwin")
    recv1_mx = exchange_sibling_halves([gwin, gwout], kinds, shapes_of(w_mx), "reduce1_mix", seq_id=5)
    g_f2 = exchange_reduced_halves(reduce_sum(_after(h_f2, gwin), recv2_f2, w_f2, "ffn2_"), "reduce3_ffn2")
    adam_f2 = [adam_big(w, g, m, v, "adam_" + nm) for w, g, m, v, nm in
               zip(w_f2, g_f2, [m_w_ffn2_in[0], m_w_ffn2_out[0]], [v_w_ffn2_in[0], v_w_ffn2_out[0]], ["ffn2_in", "ffn2_out"])]
    h_mx, hb_mx = reduce_add([gwin, gwout], _after(recv1_mx, adam_f2[0][0], adam_f2[1][0]), w_mx, "mix_")
    recv2_mx = exchange_chip_pieces(hb_mx, kinds, shapes_of(w_mx), "reduce2_mix", seq_id=6)
    dx0, act1, dg1, du1, h1b, dy1b, vg1 = ffn_bwd(_after([dx1], hb_mx)[0], xs, vecs, g1s, u1s, y1, wi1, wo1, rows1, "ffn1_bwd")
    dmod_row = jnp.concatenate([vg1[1:3], vg1[0:1], vg2[0:2], vgd[0:1], vg3[1:3], vg3[0:1]], axis=0)
    gains = jnp.concatenate([vg1[3:4], vg2[2:3], vg3[3:4], vgf[2:4]], axis=0)
    mvecs = jnp.concatenate([dconvb, vgm[0:2], rvec[0:4], jnp.zeros((1, M), F32)], axis=0)
    parts = [_rows128(dmod_row), _rows128(vgf[0:2]), _rows128(gains), _rows128(mvecs),
             _rows128(dconvw4), _rows128(drnnw4), _rows128(_diag_blocks(dwa4, H, HD)), _rows128(_diag_blocks(dwi4, H, HD))]
    sizes = [p.shape[0] for p in parts]
    packed = jnp.concatenate(parts, axis=0)
    gathered = allgather_devices_hbm(packed, "gather_small", seq_id=10)

    gwo1 = matmul(_after([act1], recv2_mx, packed)[0], dy1b, "tn", tm=Fd // 2, tn=D, tk=tk, out_dtype=BF, name="ffn1_dwo")
    w_f1o, w_f1i = w_f1[1:], w_f1[:1]
    recv1_f1o = exchange_sibling_halves([gwo1], ["row"], shapes_of(w_f1o), "reduce1_ffn1_out", seq_id=7)
    gwi1 = matmul(h1b, dg1, "tn", tm=D, tn=Fd // 2, tk=tk, out_dtype=BF, name="ffn1_dwg", out_cols=2 * Fd)
    h_f1o, hb_f1o = reduce_add([gwo1], _after(recv1_f1o, gwi1), w_f1o, "ffn1_out", ["row"])
    recv2_f1o = exchange_chip_pieces(hb_f1o, ["row"], shapes_of(w_f1o), "reduce2_ffn1_out", seq_id=11)
    gwi1 = matmul(h1b, _after([du1], hb_f1o, gathered)[0], "tn", tm=D, tn=Fd // 2, tk=tk, out_dtype=BF, name="ffn1_dwu", out_cols=2 * Fd,
                  col_off=Fd, prev=gwi1)
    recv1_f1i = exchange_sibling_halves([gwi1], ["col"], shapes_of(w_f1i), "reduce1_ffn1_in", seq_id=12)
    g_mx = exchange_reduced_halves(reduce_sum(h_mx, recv2_mx, w_mx, "mix_"), "reduce3_mix")
    summed = sum_slots(gathered, "sum_small")
    offs = [0]
    for s in sizes:
        offs.append(offs[-1] + s)
    seg = lambda k: summed[offs[k]:offs[k + 1]]
    g_b_mod = seg(0).reshape(1, 9 * D)
    g_b_fmod = seg(1).reshape(1, 2 * D)
    gsum = seg(2).reshape(5, D)
    loss = (0.5 / D) * jnp.sum(gsum[4])
    msum = seg(3).reshape(SUBLANES, M)
    g_conv_w = lax.dynamic_index_in_dim(seg(4).reshape(nb, 32, LANES), chip, axis=0, keepdims=False)[:CONV_WIDTH]
    g_rnn_w = lax.dynamic_index_in_dim(seg(5).reshape(nb, SUBLANES, LANES), chip, axis=0, keepdims=False)[:RNN_CONV_WIDTH]
    g_w_a = seg(6).reshape(H, HD, HD)
    g_w_i = seg(7).reshape(H, HD, HD)
    dmod_all = gathered[:, offs[0]:offs[1]].reshape(N_DEV, 9 * D)
    dfmod_all = gathered[:, offs[1]:offs[2]].reshape(N_DEV, 2 * D)
    dmod_cols = lax.dynamic_slice_in_dim(dmod_all, chip * n_mod, n_mod, axis=1)
    dfmod_cols = lax.dynamic_slice_in_dim(dfmod_all, chip * n_fmod, n_fmod, axis=1)

    g_w_mod, d_w_mod, nm_w_mod, nv_w_mod = adam_cond(c_all, dmod_cols, w_mod[0], m_w_mod[0], v_w_mod[0], "adam_w_mod")
    g_w_fmod, d_w_fmod, nm_w_fmod, nv_w_fmod = adam_cond(c_all, dfmod_cols, w_fmod, m_w_fmod, v_w_fmod, "adam_w_fmod")

    h_f1i, hb_f1i = reduce_add([gwi1], _after(recv1_f1i, recv2_f1o, g_w_mod, g_w_fmod), w_f1i, "ffn1_in", ["col"])
    recv2_f1i = exchange_chip_pieces(hb_f1i, ["col"], shapes_of(w_f1i), "reduce2_ffn1_in", seq_id=8)
    g_f1o = exchange_reduced_halves(reduce_sum(h_f1o, recv2_f1o, w_f1o, "ffn1_out", ["row"]), "reduce3_ffn1_out")
    g_f1i = exchange_reduced_halves(reduce_sum(h_f1i, recv2_f1i, w_f1i, "ffn1_in", ["col"]), "reduce3_ffn1_in")
    g_f1 = list(g_f1i) + list(g_f1o)

    big_w = w_f1 + w_mx + w_f2
    g_big = g_f1 + list(g_mx) + list(g_f2)
    names = ["ffn1_in", "ffn1_out", "w_in", "w_out", "ffn2_in", "ffn2_out"]
    big_m = [m_w_ffn1_in[0], m_w_ffn1_out[0], m_w_in[0], m_w_out[0]]
    big_v = [v_w_ffn1_in[0], v_w_ffn1_out[0], v_w_in[0], v_w_out[0]]
    big_out = [adam_big(w, g, m, v, "adam_" + nm) for w, g, m, v, nm in zip(big_w, g_big, big_m, big_v, names)] + adam_f2

    flat2 = lambda a: a.reshape(-1, a.shape[-1])
    small_names = ["b_mod", "g_ffn1", "g_mix", "conv_w", "conv_b", "ln_g", "ln_b", "rnn_conv_w", "rnn_conv_b", "w_a", "b_a",
                   "w_i", "b_i", "lru_lambda", "g_ffn2", "b_fmod", "g_final"]
    small_w = [b_mod, g_ffn1, g_mix, conv_w, conv_b, ln_g, ln_b, rnn_conv_w, rnn_conv_b, w_a, b_a, w_i, b_i, lru_lambda,
               g_ffn2, b_fmod, g_final]
    small_m = [m_b_mod, m_g_ffn1, m_g_mix, m_conv_w, m_conv_b, m_ln_g, m_ln_b, m_rnn_conv_w, m_rnn_conv_b, m_w_a, m_b_a,
               m_w_i, m_b_i, m_lru_lambda, m_g_ffn2, m_b_fmod, m_g_final]
    small_v = [v_b_mod, v_g_ffn1, v_g_mix, v_conv_w, v_conv_b, v_ln_g, v_ln_b, v_rnn_conv_w, v_rnn_conv_b, v_w_a, v_b_a,
               v_w_i, v_b_i, v_lru_lambda, v_g_ffn2, v_b_fmod, v_g_final]
    small_g = [g_b_mod, gsum[0:1], gsum[1:2], g_conv_w, msum[0:1], msum[1:2], msum[2:3], g_rnn_w, msum[3:4], g_w_a, msum[4:5],
               g_w_i, msum[5:6], msum[6:7], gsum[2:3], g_b_fmod, gsum[3:4]]
    small_g = [g.reshape(w.shape) for g, w in zip(small_g, small_w)]
    two_d = lambda a: a.reshape(1, -1) if a.ndim == 1 else flat2(a)
    sd, sm, sv = adam_small([two_d(a) for a in small_w], [two_d(a) for a in small_g], [two_d(a) for a in small_m],
                            [two_d(a) for a in small_v], "adam_small")
    small = {}
    for k, nm in enumerate(small_names):
        shp = small_w[k].shape
        small[nm] = (small_g[k], sd[k].reshape(shp), sm[k].reshape(shp), sv[k].reshape(shp))

    big = {"w_mod": tuple(a[None] for a in (g_w_mod, d_w_mod, nm_w_mod, nv_w_mod)),
           "w_fmod": (g_w_fmod, d_w_fmod, nm_w_fmod, nv_w_fmod)}
    for nm, full, g, (d, nmm, nvv) in zip(["w_ffn1_in", "w_ffn1_out", "w_in", "w_out", "w_ffn2_in", "w_ffn2_out"],
                                         big_w, g_big, big_out):
        big[nm] = tuple(a[None] for a in (g, d, nmm, nvv))
    order = ["w_mod", "b_mod", "g_ffn1", "w_ffn1_in", "w_ffn1_out", "g_mix", "w_in", "conv_w", "conv_b", "ln_g", "ln_b",
             "rnn_conv_w", "rnn_conv_b", "w_a", "b_a", "w_i", "b_i", "lru_lambda", "w_out", "g_ffn2", "w_ffn2_in",
             "w_ffn2_out", "w_fmod", "b_fmod", "g_final"]
    table = {**small, **big}
    outs = [loss, dx0[None]]
    for kind_ in range(4):
        outs.extend(table[nm][kind_] for nm in order)
    return tuple(outs)
```

```python
import functools

import jax
import jax.numpy as jnp
from jax import lax
from jax.experimental import pallas as pl
from jax.experimental.pallas import tpu as pltpu
from jax.experimental.pallas import tpu_sc as plsc

F32 = jnp.float32
BF = jnp.bfloat16
I32 = jnp.int32
MESH = pl.DeviceIdType.MESH

EPS = 1e-6
RG_C = 8.0
MACARON_W = 0.5
CONV_WIDTH = 31
RNN_CONV_WIDTH = 4
ADAM_LR = 0.001
ADAM_B1 = 0.9
ADAM_B2 = 0.999
ADAM_EPS = 1e-08
ADAM_WD = 0.01
ADAM_STEP = 10

LANES = 128
SUBLANES = 8
VMEM_LIMIT = 62 * 1024 * 1024
N_CHIPS = 4
N_DEV = 8

R_SH1, R_SC1, R_GT1, R_SH2, R_SC2, R_GT2, R_SH3, R_SC3, R_GT3, R_FSH, R_FSC, R_G1, R_G2, R_G3, R_GF = range(15)

CONTRACT_LAST = (((1,), (1,)), ((), ()))
CONTRACT_FIRST = (((0,), (0,)), ((), ()))


def _pcall(body, **kw):
    return pl.pallas_call(body, **kw)


def _params(sem=None, vmem=VMEM_LIMIT):
    if sem is None:
        return pltpu.CompilerParams(vmem_limit_bytes=vmem)
    return pltpu.CompilerParams(dimension_semantics=sem, vmem_limit_bytes=vmem)


def _row(ref, r):
    return ref[r:r + 1, :]


def _sigmoid(x):
    return 1.0 / (1.0 + jnp.exp(-x))


def _colsum(x):
    return jnp.sum(x, axis=0, keepdims=True)


def _rowmean(x):
    return jnp.mean(x, axis=-1, keepdims=True)


def matmul(a, b, mode, *, tm, tn, tk, name, out_dtype=F32, out_cols=None, col_off=0, prev=None):
    if mode == "nn":
        (M, K), (K2, N) = a.shape, b.shape
    elif mode == "nt":
        (M, K), (N, K2) = a.shape, b.shape
    else:
        (K, M), (K2, N) = a.shape, b.shape
    assert K == K2 and M % tm == 0 and N % tn == 0 and K % tk == 0 and col_off % tn == 0
    nk = K // tk
    out_cols = N if out_cols is None else out_cols
    off = col_off // tn

    def body(*refs):
        if prev is None:
            a_ref, b_ref, o_ref, acc = refs
        else:
            a_ref, b_ref, _, o_ref, acc = refs
        k = pl.program_id(2)
        av = a_ref[...].astype(BF)
        bv = b_ref[...].astype(BF)
        if mode == "nn":
            part = jnp.dot(av, bv, preferred_element_type=F32)
        elif mode == "nt":
            part = lax.dot_general(av, bv, CONTRACT_LAST, preferred_element_type=F32)
        else:
            part = lax.dot_general(av, bv, CONTRACT_FIRST, preferred_element_type=F32)
        if nk == 1:
            o_ref[...] = part.astype(out_dtype)
            return

        @pl.when(k == 0)
        def _():
            acc[...] = part

        @pl.when((k > 0) & (k < nk - 1))
        def _():
            acc[...] += part

        @pl.when(k == nk - 1)
        def _():
            o_ref[...] = (acc[...] + part).astype(out_dtype)

    if mode == "nn":
        a_spec = pl.BlockSpec((tm, tk), lambda m, n, k: (m, k))
        b_spec = pl.BlockSpec((tk, tn), lambda m, n, k: (k, n))
    elif mode == "nt":
        a_spec = pl.BlockSpec((tm, tk), lambda m, n, k: (m, k))
        b_spec = pl.BlockSpec((tn, tk), lambda m, n, k: (n, k))
    else:
        a_spec = pl.BlockSpec((tk, tm), lambda m, n, k: (k, m))
        b_spec = pl.BlockSpec((tk, tn), lambda m, n, k: (k, n))
    in_specs = [a_spec, b_spec]
    args = [a, b]
    aliases = {}
    if prev is not None:
        in_specs.append(pl.BlockSpec(memory_space=pl.ANY))
        args.append(prev)
        aliases = {2: 0}
    return _pcall(
        body, name=name, grid=(M // tm, N // tn, nk), in_specs=in_specs,
        out_specs=pl.BlockSpec((tm, tn), lambda m, n, k: (m, n + off)),
        out_shape=jax.ShapeDtypeStruct((M, out_cols), out_dtype),
        scratch_shapes=[pltpu.VMEM((tm, tn), F32)], input_output_aliases=aliases,
        compiler_params=_params(("parallel", "parallel", "arbitrary")),
    )(*args)


def cond_matmul(c_all, w, name):
    B, K = c_all.shape
    N = w.shape[1]
    tn = 256
    assert N % tn == 0

    def body(c_ref, w_ref, o_ref):
        cv = c_ref[...]
        ca = cv * _sigmoid(cv)
        o_ref[...] = jnp.dot(ca.astype(BF), w_ref[...].astype(BF), preferred_element_type=F32)

    return _pcall(
        body, name=name, grid=(N // tn,),
        in_specs=[pl.BlockSpec((B, K), lambda n: (0, 0)), pl.BlockSpec((K, tn), lambda n: (0, n))],
        out_specs=pl.BlockSpec((B, tn), lambda n: (0, n)),
        out_shape=jax.ShapeDtypeStruct((B, N), F32), compiler_params=_params(("parallel",)),
    )(c_all, w)


FFN_FWD_TS = 512
FFN_BWD_TS = 256


def _resident(shape, index_map):
    return pl.BlockSpec(shape, index_map, pipeline_mode=pl.Buffered(1))


def _final_norm_loss_grad(xv, t, v_ref, vg_ref):
    D = xv.shape[-1]
    r = lax.rsqrt(_rowmean(xv * xv) + EPS)
    n = xv * r
    g = _row(v_ref, R_GF)
    sc1 = 1.0 + _row(v_ref, R_FSC)
    gsc = g * sc1
    e = n * gsc + _row(v_ref, R_FSH) - t
    vg_ref[3:4, :] += _colsum(e * e)
    dout = e * (1.0 / D)
    dn_ = dout * n
    vg_ref[0:1, :] += _colsum(dout)
    vg_ref[1:2, :] += _colsum(dn_) * g
    vg_ref[2:3, :] += _colsum(dn_) * sc1
    dn = dout * gsc
    return r * (dn - n * _rowmean(dn * n))


def ffn_fwd(x, vecs, wi, wo, rows, name, final_tgt=None):
    r_sh, r_sc, r_gt, r_g = rows
    S, D = x.shape
    Fd = wo.shape[0]
    ts = min(FFN_FWD_TS, S)
    with_final = final_tgt is not None

    def body(*refs):
        if with_final:
            x_ref, v_ref, wg_ref, wu_ref, wo_ref, t_ref, xo_ref, g_ref, u_ref, y_ref, vg_ref = refs
        else:
            x_ref, v_ref, wg_ref, wu_ref, wo_ref, xo_ref, g_ref, u_ref, y_ref = refs
        xv = x_ref[...]
        r = lax.rsqrt(_rowmean(xv * xv) + EPS)
        gs = _row(v_ref, r_g) * (1.0 + _row(v_ref, r_sc))
        hb = (xv * r * gs + _row(v_ref, r_sh)).astype(BF)
        G = jnp.dot(hb, wg_ref[...], preferred_element_type=F32)
        U = jnp.dot(hb, wu_ref[...], preferred_element_type=F32)
        g_ref[...] = G.astype(BF)
        u_ref[...] = U.astype(BF)
        act = (G * _sigmoid(G) * U).astype(BF)
        Y = jnp.dot(act, wo_ref[...], preferred_element_type=F32)
        y_ref[...] = Y.astype(BF)
        xo = xv + (MACARON_W * _row(v_ref, r_gt)) * Y
        if with_final:
            @pl.when(pl.program_id(0) == 0)
            def _():
                vg_ref[...] = jnp.zeros_like(vg_ref)

            xo_ref[...] = _final_norm_loss_grad(xo, t_ref[...], v_ref, vg_ref)
        else:
            xo_ref[...] = xo

    tok = pl.BlockSpec((ts, D), lambda i: (i, 0))
    hid = pl.BlockSpec((ts, Fd), lambda i: (i, 0))
    in_specs = [tok, pl.BlockSpec(vecs.shape, lambda i: (0, 0)), _resident((D, Fd), lambda i: (0, 0)),
                _resident((D, Fd), lambda i: (0, 1)), _resident((Fd, D), lambda i: (0, 0))]
    out_specs = [tok, hid, hid, tok]
    out_shape = [jax.ShapeDtypeStruct((S, D), F32), jax.ShapeDtypeStruct((S, Fd), BF),
                 jax.ShapeDtypeStruct((S, Fd), BF), jax.ShapeDtypeStruct((S, D), BF)]
    args = [x, vecs, wi, wi, wo]
    if with_final:
        in_specs.append(tok)
        args.append(final_tgt)
        out_specs.append(pl.BlockSpec((SUBLANES, D), lambda i: (0, 0)))
        out_shape.append(jax.ShapeDtypeStruct((SUBLANES, D), F32))
    return _pcall(body, name=name, grid=(S // ts,), in_specs=in_specs, out_specs=out_specs, out_shape=out_shape,
                  compiler_params=_params(("arbitrary",)))(*args)


def ffn_fwd_in(x, vecs, wi, rows, name):
    r_sh, r_sc, r_gt, r_g = rows
    S, D = x.shape
    Fd = wi.shape[1] // 2
    ts = min(FFN_FWD_TS, S)

    def body(x_ref, v_ref, wg_ref, wu_ref, g_ref, u_ref, a_ref):
        xv = x_ref[...]
        r = lax.rsqrt(_rowmean(xv * xv) + EPS)
        gs = _row(v_ref, r_g) * (1.0 + _row(v_ref, r_sc))
        hb = (xv * r * gs + _row(v_ref, r_sh)).astype(BF)
        G = jnp.dot(hb, wg_ref[...], preferred_element_type=F32)
        U = jnp.dot(hb, wu_ref[...], preferred_element_type=F32)
        g_ref[...] = G.astype(BF)
        u_ref[...] = U.astype(BF)
        a_ref[...] = (G * _sigmoid(G) * U).astype(BF)

    hid = pl.BlockSpec((ts, Fd), lambda i: (i, 0))
    return _pcall(
        body, name=name, grid=(S // ts,),
        in_specs=[pl.BlockSpec((ts, D), lambda i: (i, 0)), pl.BlockSpec(vecs.shape, lambda i: (0, 0)),
                  _resident((D, Fd), lambda i: (0, 0)), _resident((D, Fd), lambda i: (0, 1))],
        out_specs=[hid, hid, hid], out_shape=[jax.ShapeDtypeStruct((S, Fd), BF)] * 3,
        compiler_params=_params(("arbitrary",)),
    )(x, vecs, wi, wi)


def ffn_fwd_out(act, x, vecs, wo, rows, name):
    r_sh, r_sc, r_gt, r_g = rows
    S, D = x.shape
    Fd = wo.shape[0]
    ts = min(FFN_FWD_TS, S)

    def body(a_ref, x_ref, v_ref, wo_ref, xo_ref, y_ref):
        Y = jnp.dot(a_ref[...], wo_ref[...], preferred_element_type=F32)
        y_ref[...] = Y.astype(BF)
        xo_ref[...] = x_ref[...] + (MACARON_W * _row(v_ref, r_gt)) * Y

    tok = pl.BlockSpec((ts, D), lambda i: (i, 0))
    return _pcall(
        body, name=name, grid=(S // ts,),
        in_specs=[pl.BlockSpec((ts, Fd), lambda i: (i, 0)), tok, pl.BlockSpec(vecs.shape, lambda i: (0, 0)),
                  _resident((Fd, D), lambda i: (0, 0))],
        out_specs=[tok, tok], out_shape=[jax.ShapeDtypeStruct((S, D), F32), jax.ShapeDtypeStruct((S, D), BF)],
        compiler_params=_params(("arbitrary",)),
    )(act, x, vecs, wo)


def ffn_bwd(dxo, x, vecs, gs_, us_, y, wi, wo, rows, name):
    r_sh, r_sc, r_gt, r_g = rows
    S, D = x.shape
    Fd = wo.shape[0]
    ts = min(FFN_BWD_TS, S)

    def body(dxo_ref, x_ref, v_ref, g_ref, u_ref, y_ref, wg_ref, wu_ref, wo_ref,
             dx_ref, act_ref, dg_ref, du_ref, hb_ref, dyb_ref, vg_ref):
        @pl.when(pl.program_id(0) == 0)
        def _():
            vg_ref[...] = jnp.zeros_like(vg_ref)

        dxo_v = dxo_ref[...]
        dyb = ((MACARON_W * _row(v_ref, r_gt)) * dxo_v).astype(BF)
        dyb_ref[...] = dyb
        vg_ref[0:1, :] += MACARON_W * _colsum(dxo_v * y_ref[...].astype(F32))
        dA = lax.dot_general(dyb, wo_ref[...], CONTRACT_LAST, preferred_element_type=F32)
        G = g_ref[...].astype(F32)
        U = u_ref[...].astype(F32)
        sg = _sigmoid(G)
        sl = G * sg
        dU = (dA * sl).astype(BF)
        dG = (dA * U * (sg * (1.0 + G * (1.0 - sg)))).astype(BF)
        act_ref[...] = (sl * U).astype(BF)
        dg_ref[...] = dG
        du_ref[...] = dU
        dh = (lax.dot_general(dG, wg_ref[...], CONTRACT_LAST, preferred_element_type=F32)
              + lax.dot_general(dU, wu_ref[...], CONTRACT_LAST, preferred_element_type=F32))
        xv = x_ref[...]
        r = lax.rsqrt(_rowmean(xv * xv) + EPS)
        n = xv * r
        g = _row(v_ref, r_g)
        sc1 = 1.0 + _row(v_ref, r_sc)
        gsc = g * sc1
        hb_ref[...] = (n * gsc + _row(v_ref, r_sh)).astype(BF)
        dhn = dh * n
        vg_ref[1:2, :] += _colsum(dh)
        vg_ref[2:3, :] += _colsum(dhn) * g
        vg_ref[3:4, :] += _colsum(dhn) * sc1
        dn = dh * gsc
        dx_ref[...] = dxo_v + r * (dn - n * _rowmean(dn * n))

    tok = pl.BlockSpec((ts, D), lambda i: (i, 0))
    hid = pl.BlockSpec((ts, Fd), lambda i: (i, 0))
    return _pcall(
        body, name=name, grid=(S // ts,),
        in_specs=[tok, tok, pl.BlockSpec(vecs.shape, lambda i: (0, 0)), hid, hid, tok, _resident((D, Fd), lambda i: (0, 0)),
                  _resident((D, Fd), lambda i: (0, 1)), _resident((Fd, D), lambda i: (0, 0))],
        out_specs=[tok, hid, hid, hid, tok, tok, pl.BlockSpec((SUBLANES, D), lambda i: (0, 0))],
        out_shape=[jax.ShapeDtypeStruct((S, D), F32), jax.ShapeDtypeStruct((S, Fd), BF),
                   jax.ShapeDtypeStruct((S, Fd), BF), jax.ShapeDtypeStruct((S, Fd), BF),
                   jax.ShapeDtypeStruct((S, D), BF), jax.ShapeDtypeStruct((S, D), BF),
                   jax.ShapeDtypeStruct((SUBLANES, D), F32)],
        compiler_params=_params(("arbitrary",)),
    )(dxo, x, vecs, gs_, us_, y, wi, wi, wo)


def norm_matmul(x, vecs, w, rows, name):
    r_sh, r_sc, r_g = rows
    S, D = x.shape
    N = w.shape[1]
    ts = min(512, S)

    def body(x_ref, v_ref, w_ref, o_ref):
        xv = x_ref[...]
        r = lax.rsqrt(_rowmean(xv * xv) + EPS)
        gs = _row(v_ref, r_g) * (1.0 + _row(v_ref, r_sc))
        hb = (xv * r * gs + _row(v_ref, r_sh)).astype(BF)
        o_ref[...] = jnp.dot(hb, w_ref[...], preferred_element_type=F32).astype(BF)

    return _pcall(
        body, name=name, grid=(S // ts,),
        in_specs=[pl.BlockSpec((ts, D), lambda i: (i, 0)), pl.BlockSpec(vecs.shape, lambda i: (0, 0)),
                  _resident((D, N), lambda i: (0, 0))],
        out_specs=pl.BlockSpec((ts, N), lambda i: (i, 0)),
        out_shape=jax.ShapeDtypeStruct((S, N), BF),
        compiler_params=_params(("arbitrary",)),
    )(x, vecs, w)


SEQ_TT = 256
SCAN_SEGMENTS = 4
CONV_PAD = 32


def conv_fwd(proj, convw4, conv_b, name):
    S = proj.shape[0]
    M = conv_b.shape[1]
    nb = M // LANES
    tt = min(SEQ_TT, S)

    def body(uv_ref, ug_ref, w_ref, b_ref, cq_ref, qp):
        qp[0:CONV_PAD, :] = jnp.zeros((CONV_PAD, LANES), F32)

        def step(t, carry):
            base = pl.multiple_of(t * tt, tt)
            qp[pl.ds(base + CONV_PAD, tt), :] = uv_ref[pl.ds(base, tt), :].astype(F32) * _sigmoid(ug_ref[pl.ds(base, tt), :].astype(F32))
            acc = jnp.broadcast_to(b_ref[...], (tt, LANES))
            for k in range(CONV_WIDTH):
                acc = acc + w_ref[k:k + 1, :] * qp[pl.ds(base + (CONV_PAD - CONV_WIDTH + 1) + k, tt), :]
            cq_ref[pl.ds(base, tt), :] = acc
            return carry

        lax.fori_loop(0, S // tt, step, 0)

    return _pcall(
        body, name=name, grid=(nb,),
        in_specs=[pl.BlockSpec((S, LANES), lambda c: (0, c)), pl.BlockSpec((S, LANES), lambda c: (0, c + nb)),
                  pl.BlockSpec((None, 32, LANES), lambda c: (c, 0, 0)), pl.BlockSpec((1, LANES), lambda c: (0, c))],
        out_specs=pl.BlockSpec((S, LANES), lambda c: (0, c)),
        out_shape=jax.ShapeDtypeStruct((S, M), F32),
        scratch_shapes=[pltpu.VMEM((S + CONV_PAD, LANES), F32)],
        compiler_params=_params(("arbitrary",)),
    )(proj, proj, convw4, conv_b)


def conv_bwd(dcq, proj, convw4, name):
    S, M = dcq.shape
    nb = M // LANES
    tt = min(SEQ_TT, S)
    off = CONV_PAD - CONV_WIDTH + 1

    def body(dcq_ref, uv_ref, ug_ref, w_ref, duv_ref, dug_ref, dw_ref, db_ref, qp, dp, dw8, db8):
        qp[0:CONV_PAD, :] = jnp.zeros((CONV_PAD, LANES), F32)
        dp[S:S + CONV_PAD, :] = jnp.zeros((CONV_PAD, LANES), F32)
        dw8[...] = jnp.zeros_like(dw8)
        db8[...] = jnp.zeros_like(db8)

        def fill(t, carry):
            base = pl.multiple_of(t * tt, tt)
            qp[pl.ds(base + CONV_PAD, tt), :] = uv_ref[pl.ds(base, tt), :].astype(F32) * _sigmoid(ug_ref[pl.ds(base, tt), :].astype(F32))
            dp[pl.ds(base, tt), :] = dcq_ref[pl.ds(base, tt), :].astype(F32)
            return carry

        lax.fori_loop(0, S // tt, fill, 0)

        def step(t, carry):
            base = pl.multiple_of(t * tt, tt)
            d_t = dcq_ref[pl.ds(base, tt), :].astype(F32)
            db8[...] += d_t.reshape(tt // SUBLANES, SUBLANES, LANES).sum(axis=0)
            dq = jnp.zeros((tt, LANES), F32)
            for k in range(CONV_WIDTH):
                prod = d_t * qp[pl.ds(base + off + k, tt), :]
                dw8[k] += prod.reshape(tt // SUBLANES, SUBLANES, LANES).sum(axis=0)
                dq = dq + w_ref[k:k + 1, :] * dp[pl.ds(base + (CONV_WIDTH - 1) - k, tt), :]
            uv = uv_ref[pl.ds(base, tt), :].astype(F32)
            sg = _sigmoid(ug_ref[pl.ds(base, tt), :].astype(F32))
            duv_ref[pl.ds(base, tt), :] = (dq * sg).astype(BF)
            dug_ref[pl.ds(base, tt), :] = (dq * uv * sg * (1.0 - sg)).astype(BF)
            return carry

        lax.fori_loop(0, S // tt, step, 0)
        dw_ref[...] = jnp.zeros_like(dw_ref)
        for k in range(CONV_WIDTH):
            dw_ref[k:k + 1, :] = _colsum(dw8[k])
        db_ref[...] = _colsum(db8[...])

    col = lambda o: pl.BlockSpec((S, LANES), lambda c: (0, c + o))
    return _pcall(
        body, name=name, grid=(nb,),
        in_specs=[col(0), col(0), col(nb), pl.BlockSpec((None, 32, LANES), lambda c: (c, 0, 0))],
        out_specs=[col(0), col(0), pl.BlockSpec((None, 32, LANES), lambda c: (c, 0, 0)),
                   pl.BlockSpec((1, LANES), lambda c: (0, c))],
        out_shape=[jax.ShapeDtypeStruct((S, M), BF), jax.ShapeDtypeStruct((S, M), BF),
                   jax.ShapeDtypeStruct((nb, 32, LANES), F32), jax.ShapeDtypeStruct((1, M), F32)],
        scratch_shapes=[pltpu.VMEM((S + CONV_PAD, LANES), F32), pltpu.VMEM((S + CONV_PAD, LANES), F32),
                        pltpu.VMEM((32, SUBLANES, LANES), F32), pltpu.VMEM((SUBLANES, LANES), F32)],
        compiler_params=_params(("arbitrary",)),
    )(dcq, proj, proj, convw4)


def _log_sigmoid(x):
    return jnp.minimum(x, 0.0) - jnp.log(1.0 + jnp.exp(-jnp.abs(x)))


def _rg_gate_terms(ra, ls):
    la = RG_C * ra * ls
    a = jnp.exp(la)
    th = jnp.tanh(la)
    mult = jnp.sqrt(-2.0 * th / (1.0 - th))
    return a, mult


def rnn_fwd(proj, rnnw4, rnn_b, bda, bdi, b_a, b_i, lam, name):
    S = proj.shape[0]
    M = rnn_b.shape[1]
    nb = M // LANES
    tt = min(SEQ_TT, S)
    KW = RNN_CONV_WIDTH
    nseg = SCAN_SEGMENTS if S % (SCAN_SEGMENTS * tt) == 0 else 1

    def body(ux_ref, w_ref, rb_ref, bda_ref, bdi_ref, ba_ref, bi_ref, lam_ref,
             xr_ref, ra_ref, ii_ref, h_ref, uxp, a_sc, b_sc):
        uxp[0:SUBLANES, :] = jnp.zeros((SUBLANES, LANES), F32)
        ls = _log_sigmoid(lam_ref[...])

        def step(t, carry):
            base = pl.multiple_of(t * tt, tt)
            uxp[pl.ds(base + SUBLANES, tt), :] = ux_ref[pl.ds(base, tt), :].astype(F32)
            xr = jnp.broadcast_to(rb_ref[...], (tt, LANES))
            for k in range(KW):
                xr = xr + w_ref[k:k + 1, :] * uxp[pl.ds(base + (SUBLANES - KW + 1) + k, tt), :]
            xb = xr.astype(BF)
            ra = _sigmoid(jnp.dot(xb, bda_ref[...], preferred_element_type=F32) + ba_ref[...])
            ii = _sigmoid(jnp.dot(xb, bdi_ref[...], preferred_element_type=F32) + bi_ref[...])
            a, mult = _rg_gate_terms(ra, ls)
            xr_ref[pl.ds(base, tt), :] = xr
            ra_ref[pl.ds(base, tt), :] = ra
            ii_ref[pl.ds(base, tt), :] = ii
            a_sc[pl.ds(base, tt), :] = a
            b_sc[pl.ds(base, tt), :] = mult * (ii * xr)
            return carry

        lax.fori_loop(0, S // tt, step, 0)

        rows = lax.broadcasted_iota(I32, (SUBLANES, LANES), 0)
        seg = S // nseg
        last = lambda v: jnp.broadcast_to(v[SUBLANES - 1:SUBLANES, :], (SUBLANES, LANES))

        def scan(t, carry):
            hs, ps = carry
            new_h, new_p = [], []
            for s in range(nseg):
                base = pl.multiple_of(s * seg + t * SUBLANES, SUBLANES)
                A = a_sc[pl.ds(base, SUBLANES), :]
                B = b_sc[pl.ds(base, SUBLANES), :]
                for d in (1, 2, 4):
                    As = jnp.where(rows >= d, pltpu.roll(A, d, axis=0), 1.0)
                    Bs = jnp.where(rows >= d, pltpu.roll(B, d, axis=0), 0.0)
                    B = A * Bs + B
                    A = A * As
                hh = B + A * hs[s]
                h_ref[pl.ds(base, SUBLANES), :] = hh
                pp = A * ps[s]
                if s > 0:
                    a_sc[pl.ds(base, SUBLANES), :] = pp
                new_h.append(last(hh))
                new_p.append(last(pp))
            return tuple(new_h), tuple(new_p)

        zero8 = jnp.zeros((SUBLANES, LANES), F32)
        one8 = jnp.ones((SUBLANES, LANES), F32)
        hs, ps = lax.fori_loop(0, seg // SUBLANES, scan, ((zero8,) * nseg, (one8,) * nseg))
        carry_in = hs[0]
        for s in range(1, nseg):
            c_row = carry_in[0:1, :]

            def fix(t, c, s=s, c_row=c_row):
                base = pl.multiple_of(s * seg + t * tt, tt)
                h_ref[pl.ds(base, tt), :] = h_ref[pl.ds(base, tt), :] + a_sc[pl.ds(base, tt), :] * c_row
                return c

            lax.fori_loop(0, seg // tt, fix, 0)
            carry_in = hs[s] + ps[s] * carry_in

    col = lambda o: pl.BlockSpec((S, LANES), lambda c: (0, c + o))
    vec = pl.BlockSpec((1, LANES), lambda c: (0, c))
    diag = pl.BlockSpec((LANES, LANES), lambda c: (c, c))
    return _pcall(
        body, name=name, grid=(nb,),
        in_specs=[col(2 * nb), pl.BlockSpec((None, SUBLANES, LANES), lambda c: (c, 0, 0)), vec, diag, diag, vec, vec, vec],
        out_specs=[col(0)] * 4,
        out_shape=[jax.ShapeDtypeStruct((S, M), F32)] * 4,
        scratch_shapes=[pltpu.VMEM((S + SUBLANES, LANES), F32), pltpu.VMEM((S, LANES), F32), pltpu.VMEM((S, LANES), F32)],
        compiler_params=_params(("arbitrary",)),
    )(proj, rnnw4, rnn_b, bda, bdi, b_a, b_i, lam)


def rnn_bwd(dhout, h, xr, ra, ii, proj, rnnw4, bda, bdi, lam, name):
    S, M = h.shape
    nb = M // LANES
    tt = min(SEQ_TT, S)
    KW = RNN_CONV_WIDTH
    SL = SUBLANES
    nseg = SCAN_SEGMENTS if S % (SCAN_SEGMENTS * tt) == 0 else 1

    def body(dh_ref, h_ref, xr_ref, ra_ref, ii_ref, ux_ref, w_ref, bda_ref, bdi_ref, lam_ref,
             dux_ref, dwa_ref, dwi_ref, drw_ref, vec_ref,
             a_sc, hp, g_sc, dpa_sc, dpi_sc, dxp, uxp, acc8, drw8, p_sc):
        zero8 = jnp.zeros((SL, LANES), F32)
        a_sc[S:S + SL, :] = zero8
        hp[0:SL, :] = zero8
        dxp[S:S + SL, :] = zero8
        uxp[0:SL, :] = zero8
        acc8[...] = jnp.zeros_like(acc8)
        drw8[...] = jnp.zeros_like(drw8)
        lamv = lam_ref[...]
        ls = _log_sigmoid(lamv)

        def fill(t, carry):
            base = pl.multiple_of(t * tt, tt)
            a_sc[pl.ds(base, tt), :] = jnp.exp(RG_C * ra_ref[pl.ds(base, tt), :] * ls)
            hp[pl.ds(base + SL, tt), :] = h_ref[pl.ds(base, tt), :]
            uxp[pl.ds(base + SL, tt), :] = ux_ref[pl.ds(base, tt), :].astype(F32)
            return carry

        lax.fori_loop(0, S // tt, fill, 0)

        rows = lax.broadcasted_iota(I32, (SL, LANES), 0)
        seg = S // nseg
        nt8 = seg // SL
        first = lambda v: jnp.broadcast_to(v[0:1, :], (SL, LANES))

        def rscan(t, carry):
            gs, ps = carry
            new_g, new_p = [], []
            for s in range(nseg):
                base = pl.multiple_of(s * seg + (nt8 - 1 - t) * SL, SL)
                A = a_sc[pl.ds(base + 1, SL), :]
                B = dh_ref[pl.ds(base, SL), :]
                for d in (1, 2, 4):
                    As = jnp.where(rows < SL - d, pltpu.roll(A, SL - d, axis=0), 1.0)
                    Bs = jnp.where(rows < SL - d, pltpu.roll(B, SL - d, axis=0), 0.0)
                    B = A * Bs + B
                    A = A * As
                g = B + A * gs[s]
                g_sc[pl.ds(base, SL), :] = g
                pp = A * ps[s]
                if s < nseg - 1:
                    p_sc[pl.ds(base, SL), :] = pp
                new_g.append(first(g))
                new_p.append(first(pp))
            return tuple(new_g), tuple(new_p)

        one8 = jnp.ones((SL, LANES), F32)
        gs, ps = lax.fori_loop(0, nt8, rscan, ((zero8,) * nseg, (one8,) * nseg))
        carry_in = gs[nseg - 1]
        for s in range(nseg - 2, -1, -1):
            c_row = carry_in[0:1, :]

            def fix(t, c, s=s, c_row=c_row):
                base = pl.multiple_of(s * seg + t * tt, tt)
                g_sc[pl.ds(base, tt), :] = g_sc[pl.ds(base, tt), :] + p_sc[pl.ds(base, tt), :] * c_row
                return c

            lax.fori_loop(0, seg // tt, fix, 0)
            carry_in = gs[s] + ps[s] * carry_in

        def red8(v):
            return v.reshape(tt // SL, SL, LANES).sum(axis=0)

        def step(t, carry):
            base = pl.multiple_of(t * tt, tt)
            g = g_sc[pl.ds(base, tt), :]
            hprev = hp[pl.ds(base + SL - 1, tt), :]
            xr_t = xr_ref[pl.ds(base, tt), :]
            ra_t = ra_ref[pl.ds(base, tt), :]
            ii_t = ii_ref[pl.ds(base, tt), :]
            a, mult = _rg_gate_terms(ra_t, ls)
            gx = g * xr_t
            dmult = gx * ii_t
            dii = gx * mult
            dxr = g * (mult * ii_t)
            dla = g * hprev * a - dmult * (a * a) / mult
            acc8[3] += red8(dla * ra_t)
            dpa = dla * (RG_C * ls) * ra_t * (1.0 - ra_t)
            dpi = dii * ii_t * (1.0 - ii_t)
            dpab = dpa.astype(BF)
            dpib = dpi.astype(BF)
            dxr = dxr + (lax.dot_general(dpab, bda_ref[...], CONTRACT_LAST, preferred_element_type=F32)
                         + lax.dot_general(dpib, bdi_ref[...], CONTRACT_LAST, preferred_element_type=F32))
            dpa_sc[pl.ds(base, tt), :] = dpab
            dpi_sc[pl.ds(base, tt), :] = dpib
            dxp[pl.ds(base, tt), :] = dxr
            acc8[0] += red8(dxr)
            acc8[1] += red8(dpa)
            acc8[2] += red8(dpi)
            return carry

        lax.fori_loop(0, S // tt, step, 0)

        def convb(t, carry):
            base = pl.multiple_of(t * tt, tt)
            d_t = dxp[pl.ds(base, tt), :]
            dux = jnp.zeros((tt, LANES), F32)
            for k in range(KW):
                drw8[k] += red8(d_t * uxp[pl.ds(base + (SL - KW + 1) + k, tt), :])
                dux = dux + w_ref[k:k + 1, :] * dxp[pl.ds(base + (KW - 1) - k, tt), :]
            dux_ref[pl.ds(base, tt), :] = dux.astype(BF)
            return carry

        lax.fori_loop(0, S // tt, convb, 0)

        xb = xr_ref[...].astype(BF)
        dwa_ref[...] = lax.dot_general(xb, dpa_sc[...], CONTRACT_FIRST, preferred_element_type=F32)
        dwi_ref[...] = lax.dot_general(xb, dpi_sc[...], CONTRACT_FIRST, preferred_element_type=F32)
        drw_ref[...] = jnp.zeros_like(drw_ref)
        vec_ref[...] = jnp.zeros_like(vec_ref)
        for k in range(KW):
            drw_ref[k:k + 1, :] = _colsum(drw8[k])
        for k in range(3):
            vec_ref[k:k + 1, :] = _colsum(acc8[k])
        vec_ref[3:4, :] = _colsum(acc8[3]) * (RG_C * _sigmoid(-lamv))

    col = lambda o: pl.BlockSpec((S, LANES), lambda c: (0, c + o))
    vec = pl.BlockSpec((1, LANES), lambda c: (0, c))
    diag = pl.BlockSpec((LANES, LANES), lambda c: (c, c))
    blk3 = lambda r: pl.BlockSpec((None, r, LANES), lambda c: (c, 0, 0))
    return _pcall(
        body, name=name, grid=(nb,),
        in_specs=[col(0), col(0), col(0), col(0), col(0), col(2 * nb), blk3(SL), diag, diag, vec],
        out_specs=[col(0), blk3(LANES), blk3(LANES), blk3(SL), pl.BlockSpec((SL, LANES), lambda c: (0, c))],
        out_shape=[jax.ShapeDtypeStruct((S, M), BF), jax.ShapeDtypeStruct((nb, LANES, LANES), F32),
                   jax.ShapeDtypeStruct((nb, LANES, LANES), F32), jax.ShapeDtypeStruct((nb, SL, LANES), F32),
                   jax.ShapeDtypeStruct((SL, M), F32)],
        scratch_shapes=[pltpu.VMEM((S + SL, LANES), F32), pltpu.VMEM((S + SL, LANES), F32), pltpu.VMEM((S, LANES), F32),
                        pltpu.VMEM((S, LANES), BF), pltpu.VMEM((S, LANES), BF), pltpu.VMEM((S + SL, LANES), F32),
                        pltpu.VMEM((S + SL, LANES), F32), pltpu.VMEM((SL, SL, LANES), F32), pltpu.VMEM((SL, SL, LANES), F32),
                        pltpu.VMEM((S, LANES), F32)],
        compiler_params=_params(("arbitrary",)),
    )(dhout, h, xr, ra, ii, proj, rnnw4, bda, bdi, lam)


GELU_K = 0.7978845608028654
GELU_C = 0.044715


def _layernorm_parts(cq):
    mu = _rowmean(cq)
    d = cq - mu
    rstd = lax.rsqrt(_rowmean(d * d) + EPS)
    return d * rstd, rstd


def mix_out(cq, proj, h, x, vecs, lnv, wout, name):
    S, D = x.shape
    M = cq.shape[1]
    ts = min(512, S)

    def body(cq_ref, uy_ref, h_ref, x_ref, v_ref, ln_ref, w_ref, xo_ref, ym_ref, yc_ref):
        z, _ = _layernorm_parts(cq_ref[...])
        l = z * _row(ln_ref, 0) + _row(ln_ref, 1)
        yc_ref[:, 0:M] = (l * _sigmoid(l)).astype(BF)
        uy = uy_ref[...].astype(F32)
        gelu = 0.5 * uy * (1.0 + jnp.tanh(GELU_K * (uy + GELU_C * uy * uy * uy)))
        yc_ref[:, M:2 * M] = (gelu * h_ref[...]).astype(BF)
        ym = jnp.dot(yc_ref[...], w_ref[...], preferred_element_type=F32)
        ym_ref[...] = ym.astype(BF)
        xo_ref[...] = x_ref[...] + _row(v_ref, R_GT2) * ym

    tok = pl.BlockSpec((ts, D), lambda i: (i, 0))
    mtok = lambda o: pl.BlockSpec((ts, M), lambda i: (i, o))
    return _pcall(
        body, name=name, grid=(S // ts,),
        in_specs=[mtok(0), mtok(3), mtok(0), tok, pl.BlockSpec(vecs.shape, lambda i: (0, 0)),
                  pl.BlockSpec(lnv.shape, lambda i: (0, 0)), pl.BlockSpec(wout.shape, lambda i: (0, 0))],
        out_specs=[tok, tok, pl.BlockSpec((ts, 2 * M), lambda i: (i, 0))],
        out_shape=[jax.ShapeDtypeStruct((S, D), F32), jax.ShapeDtypeStruct((S, D), BF),
                   jax.ShapeDtypeStruct((S, 2 * M), BF)],
        compiler_params=_params(("arbitrary",)),
    )(cq, proj, h, x, vecs, lnv, wout)


def mix_out_bwd(dxo, ym, vecs, wout, cq, lnv, proj, h, name):
    S, D = dxo.shape
    M = cq.shape[1]
    ts = min(512, S)

    def body(dxo_ref, ym_ref, v_ref, w_ref, cq_ref, ln_ref, uy_ref, h_ref,
             dcq_ref, dh_ref, duy_ref, dyb_ref, vgd_ref, vgm_ref):
        @pl.when(pl.program_id(0) == 0)
        def _():
            vgd_ref[...] = jnp.zeros_like(vgd_ref)
            vgm_ref[...] = jnp.zeros_like(vgm_ref)

        dxo_v = dxo_ref[...]
        dyb = (_row(v_ref, R_GT2) * dxo_v).astype(BF)
        dyb_ref[...] = dyb
        vgd_ref[0:1, :] += _colsum(dxo_v * ym_ref[...].astype(F32))
        dycat = lax.dot_general(dyb, w_ref[...], CONTRACT_LAST, preferred_element_type=F32)
        dyc = dycat[:, 0:M]
        dyr = dycat[:, M:2 * M]
        z, rstd = _layernorm_parts(cq_ref[...])
        lng = _row(ln_ref, 0)
        l = z * lng + _row(ln_ref, 1)
        sl = _sigmoid(l)
        dl = dyc * (sl * (1.0 + l * (1.0 - sl)))
        vgm_ref[0:1, :] += _colsum(dl * z)
        vgm_ref[1:2, :] += _colsum(dl)
        dz = dl * lng
        dcq_ref[...] = (rstd * (dz - _rowmean(dz) - z * _rowmean(dz * z))).astype(BF)
        uy = uy_ref[...].astype(F32)
        u2 = uy * uy
        th = jnp.tanh(GELU_K * (uy + GELU_C * uy * u2))
        gelu = 0.5 * uy * (1.0 + th)
        dgelu = 0.5 * (1.0 + th) + 0.5 * uy * (1.0 - th * th) * (GELU_K * (1.0 + 3.0 * GELU_C * u2))
        dh_ref[...] = dyr * gelu
        duy_ref[...] = (dyr * h_ref[...] * dgelu).astype(BF)

    tok = pl.BlockSpec((ts, D), lambda i: (i, 0))
    mtok = lambda o: pl.BlockSpec((ts, M), lambda i: (i, o))
    return _pcall(
        body, name=name, grid=(S // ts,),
        in_specs=[tok, tok, pl.BlockSpec(vecs.shape, lambda i: (0, 0)), pl.BlockSpec(wout.shape, lambda i: (0, 0)),
                  mtok(0), pl.BlockSpec(lnv.shape, lambda i: (0, 0)), mtok(3), mtok(0)],
        out_specs=[mtok(0), mtok(0), mtok(0), tok, pl.BlockSpec((SUBLANES, D), lambda i: (0, 0)),
                   pl.BlockSpec((SUBLANES, M), lambda i: (0, 0))],
        out_shape=[jax.ShapeDtypeStruct((S, M), BF), jax.ShapeDtypeStruct((S, M), F32), jax.ShapeDtypeStruct((S, M), BF),
                   jax.ShapeDtypeStruct((S, D), BF),
                   jax.ShapeDtypeStruct((SUBLANES, D), F32), jax.ShapeDtypeStruct((SUBLANES, M), F32)],
        compiler_params=_params(("arbitrary",)),
    )(dxo, ym, vecs, wout, cq, lnv, proj, h)


def mix_in_bwd(dparts, x, dxo, vecs, win, name):
    S, D = x.shape
    M = dparts[0].shape[1]
    ts = min(512, S)

    def body(d0, d1, d2, d3, x_ref, dxo_ref, v_ref, w_ref, dx_ref, hb_ref, dp_ref, vg_ref):
        @pl.when(pl.program_id(0) == 0)
        def _():
            vg_ref[...] = jnp.zeros_like(vg_ref)

        for q, dref in enumerate((d0, d1, d2, d3)):
            dp_ref[:, q * M:(q + 1) * M] = dref[...].astype(BF)
        dh = lax.dot_general(dp_ref[...], w_ref[...], CONTRACT_LAST, preferred_element_type=F32)
        xv = x_ref[...]
        r = lax.rsqrt(_rowmean(xv * xv) + EPS)
        n = xv * r
        g = _row(v_ref, R_G2)
        sc1 = 1.0 + _row(v_ref, R_SC2)
        gsc = g * sc1
        hb_ref[...] = (n * gsc + _row(v_ref, R_SH2)).astype(BF)
        dhn = dh * n
        vg_ref[0:1, :] += _colsum(dh)
        vg_ref[1:2, :] += _colsum(dhn) * g
        vg_ref[2:3, :] += _colsum(dhn) * sc1
        dn = dh * gsc
        dx_ref[...] = dxo_ref[...] + r * (dn - n * _rowmean(dn * n))

    tok = pl.BlockSpec((ts, D), lambda i: (i, 0))
    mtok = pl.BlockSpec((ts, M), lambda i: (i, 0))
    return _pcall(
        body, name=name, grid=(S // ts,),
        in_specs=[mtok] * 4 + [tok, tok, pl.BlockSpec(vecs.shape, lambda i: (0, 0)), pl.BlockSpec(win.shape, lambda i: (0, 0))],
        out_specs=[tok, tok, pl.BlockSpec((ts, 4 * M), lambda i: (i, 0)), pl.BlockSpec((SUBLANES, D), lambda i: (0, 0))],
        out_shape=[jax.ShapeDtypeStruct((S, D), F32), jax.ShapeDtypeStruct((S, D), BF),
                   jax.ShapeDtypeStruct((S, 4 * M), BF), jax.ShapeDtypeStruct((SUBLANES, D), F32)],
        compiler_params=_params(("arbitrary",)),
    )(*dparts, x, dxo, vecs, win)


def _adamw(w, g, m, v):
    m = ADAM_B1 * m + (1.0 - ADAM_B1) * g
    v = ADAM_B2 * v + (1.0 - ADAM_B2) * (g * g)
    m_hat = m / (1.0 - ADAM_B1 ** ADAM_STEP)
    v_hat = v / (1.0 - ADAM_B2 ** ADAM_STEP)
    delta = -ADAM_LR * (m_hat / (jnp.sqrt(v_hat) + ADAM_EPS) + ADAM_WD * w)
    return delta, m, v


def adam_big(w, g_mine, g_sib, m, v, cidx, name):
    R, C = w.shape
    hr = R // 2
    tr = 256 if hr % 256 == 0 else hr
    tc = C if C <= 1536 else (1152 if C % 1152 == 0 else 1024)
    assert hr % tr == 0 and C % tc == 0 and g_mine.shape == (hr, C)
    nrb = hr // tr

    def body(ci_ref, w_ref, gm_ref, gs_ref, m_ref, v_ref, g_ref, d_ref, nm_ref, nv_ref):
        mine = (pl.program_id(0) // nrb) == ci_ref[0]
        g = jnp.where(mine, gm_ref[...], gs_ref[...])
        d, nm, nv = _adamw(w_ref[...], g, m_ref[...], v_ref[...])
        g_ref[...] = g
        d_ref[...] = d
        nm_ref[...] = nm
        nv_ref[...] = nv

    blk = pl.BlockSpec((tr, tc), lambda i, j, ci: (i, j))
    half = pl.BlockSpec((tr, tc), lambda i, j, ci: (i % nrb, j))
    gs = pltpu.PrefetchScalarGridSpec(num_scalar_prefetch=1, grid=(R // tr, C // tc), in_specs=[blk, half, half, blk, blk],
                                      out_specs=[blk] * 4)
    return _pcall(body, name=name, grid_spec=gs, out_shape=[jax.ShapeDtypeStruct((R, C), F32)] * 4,
                  compiler_params=_params(("parallel", "parallel")))(cidx, w, g_mine, g_sib, m, v)


def adam_cond(c_all, dmod, w, m, v, name):
    B, Kin = c_all.shape
    N = w.shape[1]
    tn = 768 if N % 768 == 0 else 256
    assert N % tn == 0

    def body(c_ref, d_ref, w_ref, m_ref, v_ref, g_ref, dl_ref, nm_ref, nv_ref):
        cv = c_ref[...]
        ca = cv * _sigmoid(cv)
        g = lax.dot_general(ca.astype(BF), d_ref[...].astype(BF), CONTRACT_FIRST, preferred_element_type=F32)
        d, nm, nv = _adamw(w_ref[...], g, m_ref[...], v_ref[...])
        g_ref[...] = g
        dl_ref[...] = d
        nm_ref[...] = nm
        nv_ref[...] = nv

    blk = pl.BlockSpec((Kin, tn), lambda n: (0, n))
    return _pcall(
        body, name=name, grid=(N // tn,),
        in_specs=[pl.BlockSpec((B, Kin), lambda n: (0, 0)), pl.BlockSpec((B, tn), lambda n: (0, n)), blk, blk, blk],
        out_specs=[blk] * 4, out_shape=[jax.ShapeDtypeStruct((Kin, N), F32)] * 4,
        compiler_params=_params(("parallel",)),
    )(c_all, dmod, w, m, v)


def adam_small(ws, gs, ms, vs, name):
    n = len(ws)

    def body(*refs):
        ins, outs = refs[:4 * n], refs[4 * n:]
        for k in range(n):
            d, nm, nv = _adamw(ins[k][...], ins[n + k][...], ins[2 * n + k][...], ins[3 * n + k][...])
            outs[k][...] = d
            outs[n + k][...] = nm
            outs[2 * n + k][...] = nv

    specs = [pl.BlockSpec(w.shape, lambda i: (0, 0)) for w in ws]
    shapes = [jax.ShapeDtypeStruct(w.shape, F32) for w in ws]
    out = _pcall(body, name=name, grid=(1,), in_specs=specs * 4, out_specs=specs * 3, out_shape=shapes * 3,
                 compiler_params=_params(("arbitrary",)))(*ws, *gs, *ms, *vs)
    return out[:n], out[n:2 * n], out[2 * n:]


def _me():
    return lax.axis_index("x"), lax.axis_index("y"), lax.axis_index("c")


def _flip(x, y, p):
    return (x ^ (p >> 1) if (p >> 1) else x), (y ^ (p & 1) if (p & 1) else y)


def _handshake(peers):
    barrier = pltpu.get_barrier_semaphore()
    for peer in peers:
        pl.semaphore_signal(barrier, inc=1, device_id=peer, device_id_type=MESH)
    pl.semaphore_wait(barrier, len(peers))


def _seq_call(body, *, name, n_in, out_shape, sem_shapes, collective_id):
    del n_in
    return pl.kernel(body, out_type=out_shape, mesh=plsc.ScalarSubcoreMesh(axis_name="sq", num_cores=1), name=name,
                     scratch_types=sem_shapes, compiler_params=pltpu.CompilerParams(collective_id=collective_id))


def _hbm_comm_call(body, *, name, n_in, out_shape, sem_shapes, seq_id):
    if seq_id is not None:
        return _seq_call(body, name=name, n_in=n_in, out_shape=out_shape, sem_shapes=sem_shapes, collective_id=seq_id)
    anyspec = pl.BlockSpec(memory_space=pl.ANY)
    return _pcall(body, name=name, in_specs=[anyspec] * n_in, out_specs=[anyspec] * len(out_shape), out_shape=out_shape,
                  scratch_shapes=sem_shapes, compiler_params=_params())


def allgather_devices(v, name, with_sum=False):
    R, L = v.shape

    def body(v_ref, out_ref, *rest):
        if with_sum:
            sum_ref, send_sems, recv_sems = rest
        else:
            send_sems, recv_sems = rest
        x, y, c = _me()
        me = 4 * x + 2 * y + c
        out_ref[me] = v_ref[...]
        copies = []
        for p in range(1, N_DEV):
            px, py = _flip(x, y, p >> 1)
            pc = (1 - c) if (p & 1) else c
            peer = 4 * px + 2 * py + pc
            send = pltpu.make_async_remote_copy(src_ref=v_ref, dst_ref=out_ref.at[me], send_sem=send_sems.at[p - 1],
                                                recv_sem=recv_sems.at[p - 1], device_id=(px, py, pc), device_id_type=MESH)
            send.start()
            recv = pltpu.make_async_remote_copy(src_ref=v_ref, dst_ref=out_ref.at[peer], send_sem=send_sems.at[p - 1],
                                                recv_sem=recv_sems.at[p - 1], device_id=(px, py, pc), device_id_type=MESH)
            copies.append((send, recv))
        for send, recv in copies:
            recv.wait_recv()
        for send, recv in copies:
            send.wait_send()
        if with_sum:
            s = out_ref[0]
            for k in range(1, N_DEV):
                s = s + out_ref[k]
            sum_ref[...] = s

    vm = pl.BlockSpec(memory_space=pltpu.VMEM)
    out_shape = [jax.ShapeDtypeStruct((N_DEV, R, L), F32)]
    if with_sum:
        out_shape.append(jax.ShapeDtypeStruct((R, L), F32))
    return _pcall(
        body, name=name, in_specs=[vm], out_specs=[vm] * len(out_shape), out_shape=out_shape,
        scratch_shapes=[pltpu.SemaphoreType.DMA((N_DEV - 1,)), pltpu.SemaphoreType.DMA((N_DEV - 1,))],
        compiler_params=_params(),
    )(v)


def allgather_devices_hbm(v, name, seq_id):
    R, L = v.shape

    def body(v_ref, out_ref, send_sems, recv_sems, local_sem):
        x, y, c = _me()
        me = 4 * x + 2 * y + c
        peers = []
        for p in range(1, N_DEV):
            px, py = _flip(x, y, p >> 1)
            peers.append((px, py, (1 - c) if (p & 1) else c))
        _handshake(peers)
        lc = pltpu.make_async_copy(v_ref, out_ref.at[me], local_sem)
        lc.start()
        copies = []
        for p, (px, py, pc) in enumerate(peers):
            send = pltpu.make_async_remote_copy(src_ref=v_ref, dst_ref=out_ref.at[me], send_sem=send_sems.at[p],
                                                recv_sem=recv_sems.at[p], device_id=(px, py, pc), device_id_type=MESH)
            send.start()
            recv = pltpu.make_async_remote_copy(src_ref=v_ref, dst_ref=out_ref.at[4 * px + 2 * py + pc], send_sem=send_sems.at[p],
                                                recv_sem=recv_sems.at[p], device_id=(px, py, pc), device_id_type=MESH)
            copies.append((send, recv))
        for send, recv in copies:
            recv.wait_recv()
        for send, recv in copies:
            send.wait_send()
        lc.wait()

    return _seq_call(body, name=name, n_in=1, out_shape=[jax.ShapeDtypeStruct((N_DEV, R, L), F32)],
                     sem_shapes=[pltpu.SemaphoreType.DMA((N_DEV - 1,)), pltpu.SemaphoreType.DMA((N_DEV - 1,)),
                                 pltpu.SemaphoreType.DMA], collective_id=seq_id)(v)[0]


def sum_slots(g, name):
    n, R, L = g.shape
    tr = 216 if R % 216 == 0 else R
    assert R % tr == 0 and tr % SUBLANES == 0

    def body(g_ref, o_ref):
        s = g_ref[0]
        for k in range(1, n):
            s = s + g_ref[k]
        o_ref[...] = s

    return _pcall(body, name=name, grid=(R // tr,), in_specs=[pl.BlockSpec((n, tr, L), lambda i: (0, i, 0))],
                  out_specs=pl.BlockSpec((tr, L), lambda i: (i, 0)), out_shape=jax.ShapeDtypeStruct((R, L), F32),
                  compiler_params=_params(("parallel",)))(g)


def allgather_chips(v, name):
    R, L = v.shape

    def body(v_ref, out_ref, send_sems, recv_sems):
        x, y, c = _me()
        chip = 2 * x + y
        out_ref[chip] = v_ref[...]
        copies = []
        for p in range(1, N_CHIPS):
            px, py = _flip(x, y, p)
            send = pltpu.make_async_remote_copy(src_ref=v_ref, dst_ref=out_ref.at[chip], send_sem=send_sems.at[p - 1],
                                                recv_sem=recv_sems.at[p - 1], device_id=(px, py, c), device_id_type=MESH)
            send.start()
            recv = pltpu.make_async_remote_copy(src_ref=v_ref, dst_ref=out_ref.at[2 * px + py], send_sem=send_sems.at[p - 1],
                                                recv_sem=recv_sems.at[p - 1], device_id=(px, py, c), device_id_type=MESH)
            copies.append((send, recv))
        for send, recv in copies:
            recv.wait_recv()
        for send, recv in copies:
            send.wait_send()

    vm = pl.BlockSpec(memory_space=pltpu.VMEM)
    return _pcall(
        body, name=name, in_specs=[vm], out_specs=vm, out_shape=jax.ShapeDtypeStruct((N_CHIPS, R, L), F32),
        scratch_shapes=[pltpu.SemaphoreType.DMA((N_CHIPS - 1,)), pltpu.SemaphoreType.DMA((N_CHIPS - 1,))],
        compiler_params=_params(),
    )(v)


def _shard_window(ref, kind, shard_shape, chip, half):
    r, c = shard_shape
    hr = r // 2
    if kind == "col":
        return ref.at[pl.ds(pl.multiple_of(half * hr, hr), hr), pl.ds(pl.multiple_of(chip * c, c), c)]
    return ref.at[pl.ds(pl.multiple_of(chip * r + half * hr, hr), hr), :]


def allgather_weights(shards, kinds, name, seq_id=None):
    n = len(shards)
    fulls = []
    for s, kind in zip(shards, kinds):
        r, c = s.shape
        fulls.append(jax.ShapeDtypeStruct((r, N_CHIPS * c) if kind == "col" else (N_CHIPS * r, c), s.dtype))

    def body(*refs):
        srcs, outs = refs[:n], refs[n:2 * n]
        send_sems, recv_sems, fsend_sems, frecv_sems = refs[2 * n:]
        x, y, c = _me()
        chip = 2 * x + y
        sib = (x, y, 1 - c)
        if seq_id is not None:
            _handshake([(*_flip(x, y, p), c) for p in range(1, N_CHIPS)] + [sib])
        sends, fwds = [], []
        for i in range(n):
            shp = srcs[i].shape
            hr = shp[0] // 2
            my_half = srcs[i].at[pl.ds(pl.multiple_of(c * hr, hr), hr), :]
            for p in range(1, N_CHIPS):
                px, py = _flip(x, y, p)
                k = i * (N_CHIPS - 1) + p - 1
                cp = pltpu.make_async_remote_copy(src_ref=my_half, dst_ref=_shard_window(outs[i], kinds[i], shp, chip, c),
                                                  send_sem=send_sems.at[k], recv_sem=recv_sems.at[k],
                                                  device_id=(px, py, c), device_id_type=MESH)
                cp.start()
                sends.append(cp)
        for i in range(n):
            shp = srcs[i].shape
            for p in range(1, N_CHIPS):
                px, py = _flip(x, y, p)
                k = i * (N_CHIPS - 1) + p - 1
                landed = _shard_window(outs[i], kinds[i], shp, 2 * px + py, c)
                pltpu.make_async_remote_copy(src_ref=landed, dst_ref=landed, send_sem=send_sems.at[k], recv_sem=recv_sems.at[k],
                                             device_id=(px, py, c), device_id_type=MESH).wait_recv()
                fw = pltpu.make_async_remote_copy(src_ref=landed, dst_ref=landed, send_sem=fsend_sems.at[k],
                                                  recv_sem=frecv_sems.at[k], device_id=sib, device_id_type=MESH)
                fw.start()
                fwds.append(fw)
        for i in range(n):
            shp = srcs[i].shape
            for p in range(1, N_CHIPS):
                px, py = _flip(x, y, p)
                k = i * (N_CHIPS - 1) + p - 1
                other = _shard_window(outs[i], kinds[i], shp, 2 * px + py, 1 - c)
                pltpu.make_async_remote_copy(src_ref=other, dst_ref=other, send_sem=fsend_sems.at[k], recv_sem=frecv_sems.at[k],
                                             device_id=sib, device_id_type=MESH).wait_recv()
        for cp in sends + fwds:
            cp.wait_send()

    nk = n * (N_CHIPS - 1)
    gathered = _hbm_comm_call(
        body, name=name, n_in=n, out_shape=fulls, seq_id=seq_id,
        sem_shapes=[pltpu.SemaphoreType.DMA((nk,)), pltpu.SemaphoreType.DMA((nk,)), pltpu.SemaphoreType.DMA((nk,)),
                    pltpu.SemaphoreType.DMA((nk,))],
    )(*shards)
    return gathered


def place_local_shards(fulls, shards, kinds, name):
    n = len(shards)
    chip = jnp.reshape(2 * lax.axis_index("x") + lax.axis_index("y"), (1,)).astype(I32)

    def body(ci_ref, *refs):
        for i in range(n):
            refs[2 * n + i][...] = refs[i][...]

    in_specs = [pl.BlockSpec(s.shape, lambda i, ci: (0, 0)) for s in shards] + [pl.BlockSpec(memory_space=pl.ANY)] * n
    out_specs = [pl.BlockSpec(s.shape, (lambda i, ci: (0, ci[0])) if k == "col" else (lambda i, ci: (ci[0], 0)))
                 for s, k in zip(shards, kinds)]
    gs = pltpu.PrefetchScalarGridSpec(num_scalar_prefetch=1, grid=(1,), in_specs=in_specs, out_specs=out_specs)
    return _pcall(body, name=name, grid_spec=gs, out_shape=[jax.ShapeDtypeStruct(f.shape, f.dtype) for f in fulls],
                  input_output_aliases={1 + n + i: i for i in range(n)}, compiler_params=_params(("arbitrary",)))(chip, *shards, *fulls)


def _as_halves(g, kind, shard_shape):
    r, c = shard_shape
    if kind == "col":
        return g.reshape(2, r // 2, N_CHIPS * c)
    return g.reshape(N_CHIPS, 2, r // 2, c)


def exchange_sibling_halves(grads, kinds, shard_shapes, name, seq_id=None):
    n = len(grads)
    views = [_as_halves(g, k, s) for g, k, s in zip(grads, kinds, shard_shapes)]
    outs = []
    for k, (r, c) in zip(kinds, shard_shapes):
        outs.append(jax.ShapeDtypeStruct((r // 2, N_CHIPS * c) if k == "col" else (N_CHIPS, r // 2, c), grads[0].dtype))

    def body(*refs):
        srcs, dsts = refs[:n], refs[n:2 * n]
        send_sems, recv_sems = refs[2 * n:]
        x, y, c = _me()
        if seq_id is not None:
            _handshake([(x, y, 1 - c)])
        cps = []
        for i in range(n):
            src = srcs[i].at[1 - c] if kinds[i] == "col" else srcs[i].at[:, 1 - c]
            cp = pltpu.make_async_remote_copy(src_ref=src, dst_ref=dsts[i], send_sem=send_sems.at[i], recv_sem=recv_sems.at[i],
                                              device_id=(x, y, 1 - c), device_id_type=MESH)
            cp.start()
            cps.append(cp)
        for cp in cps:
            cp.wait_recv()
        for cp in cps:
            cp.wait_send()

    return _hbm_comm_call(body, name=name, n_in=n, out_shape=outs, seq_id=seq_id,
                          sem_shapes=[pltpu.SemaphoreType.DMA((n,)), pltpu.SemaphoreType.DMA((n,))])(*views)


def add_sibling_half(g, recv, kind, shard_shape, cidx, name):
    r, c = shard_shape
    hr = r // 2
    gv = _as_halves(g, kind, shard_shape)
    tr = hr if hr <= 512 else (256 if hr % 256 == 0 else hr // 2)
    assert hr % tr == 0

    def body(ci_ref, g_ref, r_ref, h_ref, hb_ref):
        s = g_ref[...].astype(F32) + r_ref[...].astype(F32)
        h_ref[...] = s
        hb_ref[...] = s.astype(BF)

    if kind == "col":
        grid = (hr // tr, N_CHIPS)
        g_spec = pl.BlockSpec((None, tr, c), lambda i, k, ci: (ci[0], i, k))
        o_spec = pl.BlockSpec((tr, c), lambda i, k, ci: (i, k))
    else:
        grid = (hr // tr, N_CHIPS)
        g_spec = pl.BlockSpec((None, None, tr, c), lambda i, k, ci: (k, ci[0], i, 0))
        o_spec = pl.BlockSpec((None, tr, c), lambda i, k, ci: (k, i, 0))
    gs = pltpu.PrefetchScalarGridSpec(num_scalar_prefetch=1, grid=grid, in_specs=[g_spec, o_spec], out_specs=[o_spec, o_spec])
    return _pcall(
        body, name=name, grid_spec=gs,
        out_shape=[jax.ShapeDtypeStruct(recv.shape, F32), jax.ShapeDtypeStruct(recv.shape, BF)],
        compiler_params=_params(("parallel", "parallel")),
    )(cidx, gv, recv)


def exchange_chip_pieces(hbs, kinds, shard_shapes, name, seq_id=None):
    n = len(hbs)
    outs = [jax.ShapeDtypeStruct((N_CHIPS - 1, r // 2, c), BF) for (r, c) in shard_shapes]

    def body(*refs):
        srcs, dsts = refs[:n], refs[n:2 * n]
        send_sems, recv_sems = refs[2 * n:]
        x, y, c = _me()
        if seq_id is not None:
            _handshake([(*_flip(x, y, p), c) for p in range(1, N_CHIPS)])
        cps = []
        for i in range(n):
            cc = shard_shapes[i][1]
            for p in range(1, N_CHIPS):
                px, py = _flip(x, y, p)
                pchip = 2 * px + py
                src = (srcs[i].at[:, pl.ds(pl.multiple_of(pchip * cc, cc), cc)] if kinds[i] == "col" else srcs[i].at[pchip])
                k = i * (N_CHIPS - 1) + p - 1
                cp = pltpu.make_async_remote_copy(src_ref=src, dst_ref=dsts[i].at[p - 1], send_sem=send_sems.at[k],
                                                  recv_sem=recv_sems.at[k], device_id=(px, py, c), device_id_type=MESH)
                cp.start()
                cps.append(cp)
        for cp in cps:
            cp.wait_recv()
        for cp in cps:
            cp.wait_send()

    nk = n * (N_CHIPS - 1)
    return _hbm_comm_call(body, name=name, n_in=n, out_shape=outs, seq_id=seq_id,
                          sem_shapes=[pltpu.SemaphoreType.DMA((nk,)), pltpu.SemaphoreType.DMA((nk,))])(*hbs)


def sum_chip_pieces(h, pieces, kind, shard_shape, chip_core, name):
    r, c = shard_shape
    hr = r // 2
    tr = hr if hr <= 512 else (256 if hr % 256 == 0 else hr // 2)
    assert hr % tr == 0

    def body(ci_ref, h_ref, p_ref, q_ref):
        q_ref[...] = ((h_ref[...] + p_ref[0].astype(F32)) + p_ref[1].astype(F32)) + p_ref[2].astype(F32)

    if kind == "col":
        h_spec = pl.BlockSpec((tr, c), lambda i, ci: (i, ci[0]))
    else:
        h_spec = pl.BlockSpec((None, tr, c), lambda i, ci: (ci[0], i, 0))
    gs = pltpu.PrefetchScalarGridSpec(
        num_scalar_prefetch=1, grid=(hr // tr,),
        in_specs=[h_spec, pl.BlockSpec((N_CHIPS - 1, tr, c), lambda i, ci: (0, i, 0))],
        out_specs=pl.BlockSpec((tr, c), lambda i, ci: (i, 0)))
    return _pcall(body, name=name, grid_spec=gs, out_shape=jax.ShapeDtypeStruct((hr, c), F32),
                  compiler_params=_params(("parallel",)))(chip_core, h, pieces)


def exchange_reduced_halves(qs, name, seq_id):
    n = len(qs)

    def body(*refs):
        srcs, dsts = refs[:n], refs[n:2 * n]
        send_sems, recv_sems = refs[2 * n:]
        x, y, c = _me()
        _handshake([(x, y, 1 - c)])
        cps = []
        for i in range(n):
            cp = pltpu.make_async_remote_copy(src_ref=srcs[i], dst_ref=dsts[i], send_sem=send_sems.at[i], recv_sem=recv_sems.at[i],
                                              device_id=(x, y, 1 - c), device_id_type=MESH)
            cp.start()
            cps.append(cp)
        for cp in cps:
            cp.wait_recv()
        for cp in cps:
            cp.wait_send()

    return _seq_call(body, name=name, n_in=n, out_shape=[jax.ShapeDtypeStruct(q.shape, F32) for q in qs],
                     sem_shapes=[pltpu.SemaphoreType.DMA((n,)), pltpu.SemaphoreType.DMA((n,))], collective_id=seq_id)(*qs)


def _rows128(a):
    return a.reshape(-1, LANES)


def _after(xs, *deps):
    flat = []
    for d in deps:
        flat.extend(d if isinstance(d, (list, tuple)) else [d])
    return list(lax.optimization_barrier((tuple(xs), tuple(flat)))[0])


def _block_diag(w):
    H, d, _ = w.shape
    eye = jnp.eye(H, dtype=w.dtype)
    return jnp.einsum("hde,hg->hdge", w, eye).reshape(H * d, H * d)


def _diag_blocks(g4, H, d):
    nb = g4.shape[0]
    per = LANES // d
    g = g4.reshape(nb, per, d, per, d)
    return jnp.stack([g[:, j, :, j, :] for j in range(per)], axis=1).reshape(H, d, d)


def kernel(x, c, w_mod, b_mod, g_ffn1, w_ffn1_in, w_ffn1_out, g_mix, w_in, conv_w, conv_b, ln_g, ln_b, rnn_conv_w, rnn_conv_b, w_a, b_a, w_i, b_i, lru_lambda, w_out, g_ffn2, w_ffn2_in, w_ffn2_out, w_fmod, b_fmod, g_final, loss_target, m_w_mod, m_b_mod, m_g_ffn1, m_w_ffn1_in, m_w_ffn1_out, m_g_mix, m_w_in, m_conv_w, m_conv_b, m_ln_g, m_ln_b, m_rnn_conv_w, m_rnn_conv_b, m_w_a, m_b_a, m_w_i, m_b_i, m_lru_lambda, m_w_out, m_g_ffn2, m_w_ffn2_in, m_w_ffn2_out, m_w_fmod, m_b_fmod, m_g_final, v_w_mod, v_b_mod, v_g_ffn1, v_w_ffn1_in, v_w_ffn1_out, v_g_mix, v_w_in, v_conv_w, v_conv_b, v_ln_g, v_ln_b, v_rnn_conv_w, v_rnn_conv_b, v_w_a, v_b_a, v_w_i, v_b_i, v_lru_lambda, v_w_out, v_g_ffn2, v_w_ffn2_in, v_w_ffn2_out, v_w_fmod, v_b_fmod, v_g_final):
    S, D = x.shape[1], x.shape[2]
    M = conv_b.shape[1]
    H, HD = w_a.shape[1], w_a.shape[2]
    nb = M // LANES
    ix, iy, ic = lax.axis_index("x"), lax.axis_index("y"), lax.axis_index("c")
    chip = 2 * ix + iy
    dev = 2 * chip + ic
    cidx = jnp.reshape(ic, (1,)).astype(I32)
    chip_core = jnp.stack([chip, ic]).astype(I32)
    xs = x[0]
    tgt = loss_target[0]

    kinds = ["col", "row"]
    w_f1, w_mx, w_f2 = [w_ffn1_in[0], w_ffn1_out[0]], [w_in[0], w_out[0]], [w_ffn2_in[0], w_ffn2_out[0]]
    as_bf = lambda ws: [w.astype(BF) for w in ws]
    shapes_of = lambda ws: [w.shape for w in ws]
    b_f1, b_mx, b_f2 = as_bf(w_f1), as_bf(w_mx), as_bf(w_f2)
    got_f1i = allgather_weights(b_f1[:1], kinds[:1], "gather_ffn1_in", seq_id=9)
    got_f1o = allgather_weights(b_f1[1:], kinds[1:], "gather_ffn1_out", seq_id=13)
    got_mx = allgather_weights(b_mx, kinds, "gather_mix", seq_id=1)
    got_f2 = allgather_weights(b_f2, kinds, "gather_ffn2", seq_id=2)

    c_all =allgather_devices(_rows128(c), "gather_c")[0].reshape(N_DEV, D)
    mod_cols = cond_matmul(c_all, w_mod[0], "mod_proj")
    fmod_cols = cond_matmul(c_all, w_fmod, "fmod_proj")
    convw_pad = jnp.pad(conv_w[0], ((0, 32 - CONV_WIDTH), (0, 0)))
    rnnw_pad = jnp.pad(rnn_conv_w[0], ((0, SUBLANES - RNN_CONV_WIDTH), (0, 0)))
    n_mod, n_fmod = mod_cols.shape[1], fmod_cols.shape[1]
    small = jnp.concatenate([_rows128(mod_cols), _rows128(fmod_cols), convw_pad, rnnw_pad], axis=0)
    small4 = allgather_chips(small, "gather_cond")
    r0 = N_DEV * n_mod // LANES
    r1 = r0 + N_DEV * n_fmod // LANES
    mod_all = small4[:, :r0].reshape(N_CHIPS, N_DEV, n_mod)
    fmod_all = small4[:, r0:r1].reshape(N_CHIPS, N_DEV, n_fmod)
    convw4 = small4[:, r1:r1 + 32]
    rnnw4 = small4[:, r1 + 32:r1 + 32 + SUBLANES]
    mod_row = lax.dynamic_index_in_dim(mod_all, dev, axis=1, keepdims=False).reshape(1, N_CHIPS * n_mod) + b_mod
    fmod_row = lax.dynamic_index_in_dim(fmod_all, dev, axis=1, keepdims=False).reshape(1, N_CHIPS * n_fmod) + b_fmod[None, :]
    vecs = jnp.concatenate([mod_row.reshape(9, D), fmod_row.reshape(2, D), g_ffn1, g_mix, g_ffn2, g_final[None, :],
                            jnp.zeros((1, D), F32)], axis=0)
    lnv = jnp.concatenate([ln_g, ln_b, jnp.zeros((SUBLANES - 2, M), F32)], axis=0)
    bda = _block_diag(w_a[0]).astype(BF)
    bdi = _block_diag(w_i[0]).astype(BF)

    def reduce_add(gs, recv, ws, tag, kinds_=kinds):
        pairs = [add_sibling_half(g, r_, k, w.shape, cidx, f"add_sibling_{tag}{j}")
                 for j, (g, r_, k, w) in enumerate(zip(gs, recv, kinds_, ws))]
        return [p[0] for p in pairs], [p[1] for p in pairs]

    def reduce_sum(hs_, recv, ws, tag, kinds_=kinds):
        return [sum_chip_pieces(h_, p_, k, w.shape, chip_core, f"sum_chips_{tag}{j}")
                for j, (h_, p_, k, w) in enumerate(zip(hs_, recv, kinds_, ws))]

    rows1 = (R_SH1, R_SC1, R_GT1, R_G1)
    rows3 = (R_SH3, R_SC3, R_GT3, R_G3)
    (wi1,) = place_local_shards(got_f1i, b_f1[:1], kinds[:1], "place_ffn1_in")
    g1s, u1s, a1s = ffn_fwd_in(xs, vecs, wi1, rows1, "ffn1_fwd_in")
    (wo1,) = place_local_shards(_after(got_f1o, a1s), b_f1[1:], kinds[1:], "place_ffn1_out")
    x1, y1 = ffn_fwd_out(a1s, xs, vecs, wo1, rows1, "ffn1_fwd_out")
    win, wout = place_local_shards(_after(got_mx, x1), b_mx, kinds, "place_mix")
    proj = norm_matmul(x1, vecs, win, (R_SH2, R_SC2, R_G2), "mix_in_proj")
    cq = conv_fwd(proj, convw4, conv_b, "conv_fwd")
    xr, ra, ii, hh = rnn_fwd(proj, rnnw4, rnn_conv_b, bda, bdi, b_a, b_i, lru_lambda, "rnn_fwd")
    x2, ym, ycat = mix_out(cq, proj, hh, x1, vecs, lnv, wout, "mix_out")
    wi2, wo2 = place_local_shards(_after(got_f2, x2), b_f2, kinds, "place_ffn2")
    dx3, g2s, u2s, y2, vgf = ffn_fwd(x2, vecs, wi2, wo2, rows3, "ffn2_fwd", final_tgt=tgt)

    Fd = wo1.shape[0]
    tk = min(2048, S)
    dx2, act2, dg2, du2, h3b, dy2b, vg3 = ffn_bwd(dx3, x2, vecs, g2s, u2s, y2, wi2, wo2, rows3, "ffn2_bwd")
    gwo2 = matmul(act2, dy2b, "tn", tm=Fd // 2, tn=D, tk=tk, out_dtype=BF, name="ffn2_dwo")
    gwi2 = matmul(h3b, dg2, "tn", tm=D, tn=Fd // 2, tk=tk, out_dtype=BF, name="ffn2_dwg", out_cols=2 * Fd)
    gwi2 = matmul(h3b, du2, "tn", tm=D, tn=Fd // 2, tk=tk, out_dtype=BF, name="ffn2_dwu", out_cols=2 * Fd, col_off=Fd, prev=gwi2)
    recv1_f2 = exchange_sibling_halves([gwi2, gwo2], kinds, shapes_of(w_f2), "reduce1_ffn2", seq_id=3)
    dcq, dhout, duy, dymb, vgd, vgm = mix_out_bwd(dx2, ym, vecs, wout, cq, lnv, proj, hh, "mix_out_bwd")
    gwout = matmul(ycat, dymb, "tn", tm=2 * M, tn=D, tk=tk, out_dtype=BF, name="mix_dwout")
    recv1_f2 = _after(recv1_f2, gwout)
    h_f2, hb_f2 = reduce_add([gwi2, gwo2], recv1_f2, w_f2, "ffn2_")
    recv2_f2 = exchange_chip_pieces(hb_f2, kinds, shapes_of(w_f2), "reduce2_ffn2", seq_id=4)
    duv, dug, dconvw4, dconvb = conv_bwd(_after([dcq], hb_f2)[0], proj, convw4, "conv_bwd")
    dux, dwa4, dwi4, drnnw4, rvec = rnn_bwd(dhout, hh, xr, ra, ii, proj, rnnw4, bda, bdi, lru_lambda, "rnn_bwd")
    dx1, h2b, dpb, vg2 = mix_in_bwd((duv, dug, dux, duy), x1, dx2, vecs, win, "mix_in_bwd")
    gwin = matmul(h2b, dpb, "tn", tm=D, tn=1024, tk=tk, out_dtype=BF, name="mix_dwin")
    recv1_mx = exchange_sibling_halves([gwin, gwout], kinds, shapes_of(w_mx), "reduce1_mix", seq_id=5)
    q_f2 = reduce_sum(_after(h_f2, gwin), recv2_f2, w_f2, "ffn2_")
    r_f2 = exchange_reduced_halves(q_f2, "reduce3_ffn2", seq_id=14)
    h_mx, hb_mx = reduce_add([gwin, gwout], _after(recv1_mx, q_f2), w_mx, "mix_")
    recv2_mx = exchange_chip_pieces(hb_mx, kinds, shapes_of(w_mx), "reduce2_mix", seq_id=6)
    dx0, act1, dg1, du1, h1b, dy1b, vg1 = ffn_bwd(_after([dx1], hb_mx)[0], xs, vecs, g1s, u1s, y1, wi1, wo1, rows1, "ffn1_bwd")
    dmod_row = jnp.concatenate([vg1[1:3], vg1[0:1], vg2[0:2], vgd[0:1], vg3[1:3], vg3[0:1]], axis=0)
    gains = jnp.concatenate([vg1[3:4], vg2[2:3], vg3[3:4], vgf[2:4]], axis=0)
    mvecs = jnp.concatenate([dconvb, vgm[0:2], rvec[0:4], jnp.zeros((1, M), F32)], axis=0)
    parts = [_rows128(dmod_row), _rows128(vgf[0:2]), _rows128(gains), _rows128(mvecs),
             _rows128(dconvw4), _rows128(drnnw4), _rows128(_diag_blocks(dwa4, H, HD)), _rows128(_diag_blocks(dwi4, H, HD))]
    sizes = [p.shape[0] for p in parts]
    packed = jnp.concatenate(parts, axis=0)
    gathered = allgather_devices_hbm(packed, "gather_small", seq_id=10)

    gwo1 = matmul(_after([act1], recv2_mx, packed)[0], dy1b, "tn", tm=Fd // 2, tn=D, tk=tk, out_dtype=BF, name="ffn1_dwo")
    w_f1o, w_f1i = w_f1[1:], w_f1[:1]
    recv1_f1o = exchange_sibling_halves([gwo1], ["row"], shapes_of(w_f1o), "reduce1_ffn1_out", seq_id=7)
    q_mx = reduce_sum(_after(h_mx, gwo1), recv2_mx, w_mx, "mix_")
    r_mx = exchange_reduced_halves(q_mx, "reduce3_mix", seq_id=15)
    gwi1 = matmul(_after([h1b], q_mx)[0], dg1, "tn", tm=D, tn=Fd // 2, tk=tk, out_dtype=BF, name="ffn1_dwg", out_cols=2 * Fd)
    h_f1o, hb_f1o = reduce_add([gwo1], _after(recv1_f1o, gwi1), w_f1o, "ffn1_out", ["row"])
    recv2_f1o = exchange_chip_pieces(hb_f1o, ["row"], shapes_of(w_f1o), "reduce2_ffn1_out", seq_id=11)
    gwi1 = matmul(h1b, _after([du1], hb_f1o, gathered)[0], "tn", tm=D, tn=Fd // 2, tk=tk, out_dtype=BF, name="ffn1_dwu", out_cols=2 * Fd,
                  col_off=Fd, prev=gwi1)
    recv1_f1i = exchange_sibling_halves([gwi1], ["col"], shapes_of(w_f1i), "reduce1_ffn1_in", seq_id=12)
    q_f1o = reduce_sum(_after(h_f1o, gwi1), recv2_f1o, w_f1o, "ffn1_out", ["row"])
    r_f1o = exchange_reduced_halves(q_f1o, "reduce3_ffn1_out", seq_id=16)
    summed = sum_slots(gathered, "sum_small")
    offs = [0]
    for s in sizes:
        offs.append(offs[-1] + s)
    seg = lambda k: summed[offs[k]:offs[k + 1]]
    g_b_mod = seg(0).reshape(1, 9 * D)
    g_b_fmod = seg(1).reshape(1, 2 * D)
    gsum = seg(2).reshape(5, D)
    loss = (0.5 / D) * jnp.sum(gsum[4])
    msum = seg(3).reshape(SUBLANES, M)
    g_conv_w = lax.dynamic_index_in_dim(seg(4).reshape(nb, 32, LANES), chip, axis=0, keepdims=False)[:CONV_WIDTH]
    g_rnn_w = lax.dynamic_index_in_dim(seg(5).reshape(nb, SUBLANES, LANES), chip, axis=0, keepdims=False)[:RNN_CONV_WIDTH]
    g_w_a = seg(6).reshape(H, HD, HD)
    g_w_i = seg(7).reshape(H, HD, HD)
    dmod_all = gathered[:, offs[0]:offs[1]].reshape(N_DEV, 9 * D)
    dfmod_all = gathered[:, offs[1]:offs[2]].reshape(N_DEV, 2 * D)
    dmod_cols = lax.dynamic_slice_in_dim(dmod_all, chip * n_mod, n_mod, axis=1)
    dfmod_cols = lax.dynamic_slice_in_dim(dfmod_all, chip * n_fmod, n_fmod, axis=1)

    g_w_mod, d_w_mod, nm_w_mod, nv_w_mod = adam_cond(c_all, dmod_cols, w_mod[0], m_w_mod[0], v_w_mod[0], "adam_w_mod")
    g_w_fmod, d_w_fmod, nm_w_fmod, nv_w_fmod = adam_cond(c_all, dfmod_cols, w_fmod, m_w_fmod, v_w_fmod, "adam_w_fmod")

    h_f1i, hb_f1i = reduce_add([gwi1], _after(recv1_f1i, q_f1o, g_w_mod, g_w_fmod), w_f1i, "ffn1_in", ["col"])
    recv2_f1i = exchange_chip_pieces(hb_f1i, ["col"], shapes_of(w_f1i), "reduce2_ffn1_in", seq_id=8)

    def adam_group(ws, qs_, rs_, ms, vs, tags, after):
        qs_ = _after(list(qs_), *after) if after else list(qs_)
        return [adam_big(w, q_, r_, m, v, cidx, "adam_" + t) for w, q_, r_, m, v, t in zip(ws, qs_, rs_, ms, vs, tags)]

    ad_f2 = adam_group(w_f2, q_f2, r_f2, [m_w_ffn2_in[0], m_w_ffn2_out[0]], [v_w_ffn2_in[0], v_w_ffn2_out[0]],
                       ["ffn2_in", "ffn2_out"], [hb_f1i])
    ad_mx = adam_group(w_mx, q_mx, r_mx, [m_w_in[0], m_w_out[0]], [v_w_in[0], v_w_out[0]], ["w_in", "w_out"], [hb_f1i])
    ad_f1o = adam_group(w_f1o, q_f1o, r_f1o, [m_w_ffn1_out[0]], [v_w_ffn1_out[0]], ["ffn1_out"], [hb_f1i])
    q_f1i = reduce_sum(_after(h_f1i, ad_f2[0][0], ad_mx[0][0], ad_f1o[0][0]), recv2_f1i, w_f1i, "ffn1_in", ["col"])
    r_f1i = exchange_reduced_halves(q_f1i, "reduce3_ffn1_in", seq_id=17)
    ad_f1i = adam_group(w_f1i, q_f1i, r_f1i, [m_w_ffn1_in[0]], [v_w_ffn1_in[0]], ["ffn1_in"], [])
    big_out = ad_f1i + ad_f1o + ad_mx + ad_f2

    flat2 = lambda a: a.reshape(-1, a.shape[-1])
    small_names = ["b_mod", "g_ffn1", "g_mix", "conv_w", "conv_b", "ln_g", "ln_b", "rnn_conv_w", "rnn_conv_b", "w_a", "b_a",
                   "w_i", "b_i", "lru_lambda", "g_ffn2", "b_fmod", "g_final"]
    small_w = [b_mod, g_ffn1, g_mix, conv_w, conv_b, ln_g, ln_b, rnn_conv_w, rnn_conv_b, w_a, b_a, w_i, b_i, lru_lambda,
               g_ffn2, b_fmod, g_final]
    small_m = [m_b_mod, m_g_ffn1, m_g_mix, m_conv_w, m_conv_b, m_ln_g, m_ln_b, m_rnn_conv_w, m_rnn_conv_b, m_w_a, m_b_a,
               m_w_i, m_b_i, m_lru_lambda, m_g_ffn2, m_b_fmod, m_g_final]
    small_v = [v_b_mod, v_g_ffn1, v_g_mix, v_conv_w, v_conv_b, v_ln_g, v_ln_b, v_rnn_conv_w, v_rnn_conv_b, v_w_a, v_b_a,
               v_w_i, v_b_i, v_lru_lambda, v_g_ffn2, v_b_fmod, v_g_final]
    small_g = [g_b_mod, gsum[0:1], gsum[1:2], g_conv_w, msum[0:1], msum[1:2], msum[2:3], g_rnn_w, msum[3:4], g_w_a, msum[4:5],
               g_w_i, msum[5:6], msum[6:7], gsum[2:3], g_b_fmod, gsum[3:4]]
    small_g = [g.reshape(w.shape) for g, w in zip(small_g, small_w)]
    two_d = lambda a: a.reshape(1, -1) if a.ndim == 1 else flat2(a)
    sd, sm, sv = adam_small([two_d(a) for a in small_w], [two_d(a) for a in small_g], [two_d(a) for a in small_m],
                            [two_d(a) for a in small_v], "adam_small")
    small = {}
    for k, nm in enumerate(small_names):
        shp = small_w[k].shape
        small[nm] = (small_g[k], sd[k].reshape(shp), sm[k].reshape(shp), sv[k].reshape(shp))

    big = {"w_mod": tuple(a[None] for a in (g_w_mod, d_w_mod, nm_w_mod, nv_w_mod)),
           "w_fmod": (g_w_fmod, d_w_fmod, nm_w_fmod, nv_w_fmod)}
    for nm, res in zip(["w_ffn1_in", "w_ffn1_out", "w_in", "w_out", "w_ffn2_in", "w_ffn2_out"], big_out):
        big[nm] = tuple(a[None] for a in res)
    order = ["w_mod", "b_mod", "g_ffn1", "w_ffn1_in", "w_ffn1_out", "g_mix", "w_in", "conv_w", "conv_b", "ln_g", "ln_b",
             "rnn_conv_w", "rnn_conv_b", "w_a", "b_a", "w_i", "b_i", "lru_lambda", "w_out", "g_ffn2", "w_ffn2_in",
             "w_ffn2_out", "w_fmod", "b_fmod", "g_final"]
    table = {**small, **big}
    outs = [loss, dx0[None]]
    for kind_ in range(4):
        outs.extend(table[nm][kind_] for nm in order)
    return tuple(outs)
```

```python
import functools

import jax
import jax.numpy as jnp
from jax import lax
from jax.experimental import pallas as pl
from jax.experimental.pallas import tpu as pltpu
from jax.experimental.pallas import tpu_sc as plsc

F32 = jnp.float32
BF = jnp.bfloat16
I32 = jnp.int32
MESH = pl.DeviceIdType.MESH

EPS = 1e-6
RG_C = 8.0
MACARON_W = 0.5
CONV_WIDTH = 31
RNN_CONV_WIDTH = 4
ADAM_LR = 0.001
ADAM_B1 = 0.9
ADAM_B2 = 0.999
ADAM_EPS = 1e-08
ADAM_WD = 0.01
ADAM_STEP = 10

LANES = 128
SUBLANES = 8
VMEM_LIMIT = 62 * 1024 * 1024
N_CHIPS = 4
N_DEV = 8

R_SH1, R_SC1, R_GT1, R_SH2, R_SC2, R_GT2, R_SH3, R_SC3, R_GT3, R_FSH, R_FSC, R_G1, R_G2, R_G3, R_GF = range(15)

CONTRACT_LAST = (((1,), (1,)), ((), ()))
CONTRACT_FIRST = (((0,), (0,)), ((), ()))


def _pcall(body, **kw):
    return pl.pallas_call(body, **kw)


def _params(sem=None, vmem=VMEM_LIMIT):
    if sem is None:
        return pltpu.CompilerParams(vmem_limit_bytes=vmem)
    return pltpu.CompilerParams(dimension_semantics=sem, vmem_limit_bytes=vmem)


def _row(ref, r):
    return ref[r:r + 1, :]


def _sigmoid(x):
    return 1.0 / (1.0 + jnp.exp(-x))


def _colsum(x):
    return jnp.sum(x, axis=0, keepdims=True)


def _rowmean(x):
    return jnp.mean(x, axis=-1, keepdims=True)


def matmul(a, b, mode, *, tm, tn, tk, name, out_dtype=F32, out_cols=None, col_off=0, prev=None):
    if mode == "nn":
        (M, K), (K2, N) = a.shape, b.shape
    elif mode == "nt":
        (M, K), (N, K2) = a.shape, b.shape
    else:
        (K, M), (K2, N) = a.shape, b.shape
    assert K == K2 and M % tm == 0 and N % tn == 0 and K % tk == 0 and col_off % tn == 0
    nk = K // tk
    out_cols = N if out_cols is None else out_cols
    off = col_off // tn

    def body(*refs):
        if prev is None:
            a_ref, b_ref, o_ref, acc = refs
        else:
            a_ref, b_ref, _, o_ref, acc = refs
        k = pl.program_id(2)
        av = a_ref[...].astype(BF)
        bv = b_ref[...].astype(BF)
        if mode == "nn":
            part = jnp.dot(av, bv, preferred_element_type=F32)
        elif mode == "nt":
            part = lax.dot_general(av, bv, CONTRACT_LAST, preferred_element_type=F32)
        else:
            part = lax.dot_general(av, bv, CONTRACT_FIRST, preferred_element_type=F32)
        if nk == 1:
            o_ref[...] = part.astype(out_dtype)
            return

        @pl.when(k == 0)
        def _():
            acc[...] = part

        @pl.when((k > 0) & (k < nk - 1))
        def _():
            acc[...] += part

        @pl.when(k == nk - 1)
        def _():
            o_ref[...] = (acc[...] + part).astype(out_dtype)

    if mode == "nn":
        a_spec = pl.BlockSpec((tm, tk), lambda m, n, k: (m, k))
        b_spec = pl.BlockSpec((tk, tn), lambda m, n, k: (k, n))
    elif mode == "nt":
        a_spec = pl.BlockSpec((tm, tk), lambda m, n, k: (m, k))
        b_spec = pl.BlockSpec((tn, tk), lambda m, n, k: (n, k))
    else:
        a_spec = pl.BlockSpec((tk, tm), lambda m, n, k: (k, m))
        b_spec = pl.BlockSpec((tk, tn), lambda m, n, k: (k, n))
    in_specs = [a_spec, b_spec]
    args = [a, b]
    aliases = {}
    if prev is not None:
        in_specs.append(pl.BlockSpec(memory_space=pl.ANY))
        args.append(prev)
        aliases = {2: 0}
    return _pcall(
        body, name=name, grid=(M // tm, N // tn, nk), in_specs=in_specs,
        out_specs=pl.BlockSpec((tm, tn), lambda m, n, k: (m, n + off)),
        out_shape=jax.ShapeDtypeStruct((M, out_cols), out_dtype),
        scratch_shapes=[pltpu.VMEM((tm, tn), F32)], input_output_aliases=aliases,
        compiler_params=_params(("parallel", "parallel", "arbitrary")),
    )(*args)


def cond_matmul(c_all, w, name):
    B, K = c_all.shape
    N = w.shape[1]
    tn = 256
    assert N % tn == 0

    def body(c_ref, w_ref, o_ref):
        cv = c_ref[...]
        ca = cv * _sigmoid(cv)
        o_ref[...] = jnp.dot(ca.astype(BF), w_ref[...].astype(BF), preferred_element_type=F32)

    return _pcall(
        body, name=name, grid=(N // tn,),
        in_specs=[pl.BlockSpec((B, K), lambda n: (0, 0)), pl.BlockSpec((K, tn), lambda n: (0, n))],
        out_specs=pl.BlockSpec((B, tn), lambda n: (0, n)),
        out_shape=jax.ShapeDtypeStruct((B, N), F32), compiler_params=_params(("parallel",)),
    )(c_all, w)


FFN_FWD_TS = 512
FFN_BWD_TS = 256


def _resident(shape, index_map):
    return pl.BlockSpec(shape, index_map, pipeline_mode=pl.Buffered(1))


def _final_norm_loss_grad(xv, t, v_ref, vg_ref):
    D = xv.shape[-1]
    r = lax.rsqrt(_rowmean(xv * xv) + EPS)
    n = xv * r
    g = _row(v_ref, R_GF)
    sc1 = 1.0 + _row(v_ref, R_FSC)
    gsc = g * sc1
    e = n * gsc + _row(v_ref, R_FSH) - t
    vg_ref[3:4, :] += _colsum(e * e)
    dout = e * (1.0 / D)
    dn_ = dout * n
    vg_ref[0:1, :] += _colsum(dout)
    vg_ref[1:2, :] += _colsum(dn_) * g
    vg_ref[2:3, :] += _colsum(dn_) * sc1
    dn = dout * gsc
    return r * (dn - n * _rowmean(dn * n))


def ffn_fwd(x, vecs, wi, wo, rows, name, final_tgt=None):
    r_sh, r_sc, r_gt, r_g = rows
    S, D = x.shape
    Fd = wo.shape[0]
    ts = min(FFN_FWD_TS, S)
    with_final = final_tgt is not None

    def body(*refs):
        if with_final:
            x_ref, v_ref, wg_ref, wu_ref, wo_ref, t_ref, xo_ref, g_ref, u_ref, y_ref, vg_ref = refs
        else:
            x_ref, v_ref, wg_ref, wu_ref, wo_ref, xo_ref, g_ref, u_ref, y_ref = refs
        xv = x_ref[...]
        r = lax.rsqrt(_rowmean(xv * xv) + EPS)
        gs = _row(v_ref, r_g) * (1.0 + _row(v_ref, r_sc))
        hb = (xv * r * gs + _row(v_ref, r_sh)).astype(BF)
        G = jnp.dot(hb, wg_ref[...], preferred_element_type=F32)
        U = jnp.dot(hb, wu_ref[...], preferred_element_type=F32)
        g_ref[...] = G.astype(BF)
        u_ref[...] = U.astype(BF)
        act = (G * _sigmoid(G) * U).astype(BF)
        Y = jnp.dot(act, wo_ref[...], preferred_element_type=F32)
        y_ref[...] = Y.astype(BF)
        xo = xv + (MACARON_W * _row(v_ref, r_gt)) * Y
        if with_final:
            @pl.when(pl.program_id(0) == 0)
            def _():
                vg_ref[...] = jnp.zeros_like(vg_ref)

            xo_ref[...] = _final_norm_loss_grad(xo, t_ref[...], v_ref, vg_ref)
        else:
            xo_ref[...] = xo

    tok = pl.BlockSpec((ts, D), lambda i: (i, 0))
    hid = pl.BlockSpec((ts, Fd), lambda i: (i, 0))
    in_specs = [tok, pl.BlockSpec(vecs.shape, lambda i: (0, 0)), _resident((D, Fd), lambda i: (0, 0)),
                _resident((D, Fd), lambda i: (0, 1)), _resident((Fd, D), lambda i: (0, 0))]
    out_specs = [tok, hid, hid, tok]
    out_shape = [jax.ShapeDtypeStruct((S, D), F32), jax.ShapeDtypeStruct((S, Fd), BF),
                 jax.ShapeDtypeStruct((S, Fd), BF), jax.ShapeDtypeStruct((S, D), BF)]
    args = [x, vecs, wi, wi, wo]
    if with_final:
        in_specs.append(tok)
        args.append(final_tgt)
        out_specs.append(pl.BlockSpec((SUBLANES, D), lambda i: (0, 0)))
        out_shape.append(jax.ShapeDtypeStruct((SUBLANES, D), F32))
    return _pcall(body, name=name, grid=(S // ts,), in_specs=in_specs, out_specs=out_specs, out_shape=out_shape,
                  compiler_params=_params(("arbitrary",)))(*args)


def ffn_fwd_in(x, vecs, wi, rows, name):
    r_sh, r_sc, r_gt, r_g = rows
    S, D = x.shape
    Fd = wi.shape[1] // 2
    ts = min(FFN_FWD_TS, S)

    def body(x_ref, v_ref, wg_ref, wu_ref, g_ref, u_ref, a_ref):
        xv = x_ref[...]
        r = lax.rsqrt(_rowmean(xv * xv) + EPS)
        gs = _row(v_ref, r_g) * (1.0 + _row(v_ref, r_sc))
        hb = (xv * r * gs + _row(v_ref, r_sh)).astype(BF)
        G = jnp.dot(hb, wg_ref[...], preferred_element_type=F32)
        U = jnp.dot(hb, wu_ref[...], preferred_element_type=F32)
        g_ref[...] = G.astype(BF)
        u_ref[...] = U.astype(BF)
        a_ref[...] = (G * _sigmoid(G) * U).astype(BF)

    hid = pl.BlockSpec((ts, Fd), lambda i: (i, 0))
    return _pcall(
        body, name=name, grid=(S // ts,),
        in_specs=[pl.BlockSpec((ts, D), lambda i: (i, 0)), pl.BlockSpec(vecs.shape, lambda i: (0, 0)),
                  _resident((D, Fd), lambda i: (0, 0)), _resident((D, Fd), lambda i: (0, 1))],
        out_specs=[hid, hid, hid], out_shape=[jax.ShapeDtypeStruct((S, Fd), BF)] * 3,
        compiler_params=_params(("arbitrary",)),
    )(x, vecs, wi, wi)


def ffn_fwd_out(act, x, vecs, wo, rows, name):
    r_sh, r_sc, r_gt, r_g = rows
    S, D = x.shape
    Fd = wo.shape[0]
    ts = min(FFN_FWD_TS, S)

    def body(a_ref, x_ref, v_ref, wo_ref, xo_ref, y_ref):
        Y = jnp.dot(a_ref[...], wo_ref[...], preferred_element_type=F32)
        y_ref[...] = Y.astype(BF)
        xo_ref[...] = x_ref[...] + (MACARON_W * _row(v_ref, r_gt)) * Y

    tok = pl.BlockSpec((ts, D), lambda i: (i, 0))
    return _pcall(
        body, name=name, grid=(S // ts,),
        in_specs=[pl.BlockSpec((ts, Fd), lambda i: (i, 0)), tok, pl.BlockSpec(vecs.shape, lambda i: (0, 0)),
                  _resident((Fd, D), lambda i: (0, 0))],
        out_specs=[tok, tok], out_shape=[jax.ShapeDtypeStruct((S, D), F32), jax.ShapeDtypeStruct((S, D), BF)],
        compiler_params=_params(("arbitrary",)),
    )(act, x, vecs, wo)


def ffn_bwd(dxo, x, vecs, gs_, us_, y, wi, wo, rows, name):
    r_sh, r_sc, r_gt, r_g = rows
    S, D = x.shape
    Fd = wo.shape[0]
    ts = min(FFN_BWD_TS, S)

    def body(dxo_ref, x_ref, v_ref, g_ref, u_ref, y_ref, wg_ref, wu_ref, wo_ref,
             dx_ref, act_ref, dg_ref, du_ref, hb_ref, dyb_ref, vg_ref):
        @pl.when(pl.program_id(0) == 0)
        def _():
            vg_ref[...] = jnp.zeros_like(vg_ref)

        dxo_v = dxo_ref[...]
        dyb = ((MACARON_W * _row(v_ref, r_gt)) * dxo_v).astype(BF)
        dyb_ref[...] = dyb
        vg_ref[0:1, :] += MACARON_W * _colsum(dxo_v * y_ref[...].astype(F32))
        dA = lax.dot_general(dyb, wo_ref[...], CONTRACT_LAST, preferred_element_type=F32)
        G = g_ref[...].astype(F32)
        U = u_ref[...].astype(F32)
        sg = _sigmoid(G)
        sl = G * sg
        dU = (dA * sl).astype(BF)
        dG = (dA * U * (sg * (1.0 + G * (1.0 - sg)))).astype(BF)
        act_ref[...] = (sl * U).astype(BF)
        dg_ref[...] = dG
        du_ref[...] = dU
        dh = (lax.dot_general(dG, wg_ref[...], CONTRACT_LAST, preferred_element_type=F32)
              + lax.dot_general(dU, wu_ref[...], CONTRACT_LAST, preferred_element_type=F32))
        xv = x_ref[...]
        r = lax.rsqrt(_rowmean(xv * xv) + EPS)
        n = xv * r
        g = _row(v_ref, r_g)
        sc1 = 1.0 + _row(v_ref, r_sc)
        gsc = g * sc1
        hb_ref[...] = (n * gsc + _row(v_ref, r_sh)).astype(BF)
        dhn = dh * n
        vg_ref[1:2, :] += _colsum(dh)
        vg_ref[2:3, :] += _colsum(dhn) * g
        vg_ref[3:4, :] += _colsum(dhn) * sc1
        dn = dh * gsc
        dx_ref[...] = dxo_v + r * (dn - n * _rowmean(dn * n))

    tok = pl.BlockSpec((ts, D), lambda i: (i, 0))
    hid = pl.BlockSpec((ts, Fd), lambda i: (i, 0))
    return _pcall(
        body, name=name, grid=(S // ts,),
        in_specs=[tok, tok, pl.BlockSpec(vecs.shape, lambda i: (0, 0)), hid, hid, tok, _resident((D, Fd), lambda i: (0, 0)),
                  _resident((D, Fd), lambda i: (0, 1)), _resident((Fd, D), lambda i: (0, 0))],
        out_specs=[tok, hid, hid, hid, tok, tok, pl.BlockSpec((SUBLANES, D), lambda i: (0, 0))],
        out_shape=[jax.ShapeDtypeStruct((S, D), F32), jax.ShapeDtypeStruct((S, Fd), BF),
                   jax.ShapeDtypeStruct((S, Fd), BF), jax.ShapeDtypeStruct((S, Fd), BF),
                   jax.ShapeDtypeStruct((S, D), BF), jax.ShapeDtypeStruct((S, D), BF),
                   jax.ShapeDtypeStruct((SUBLANES, D), F32)],
        compiler_params=_params(("arbitrary",)),
    )(dxo, x, vecs, gs_, us_, y, wi, wi, wo)


def norm_matmul(x, vecs, w, rows, name):
    r_sh, r_sc, r_g = rows
    S, D = x.shape
    N = w.shape[1]
    ts = min(512, S)

    def body(x_ref, v_ref, w_ref, o_ref):
        xv = x_ref[...]
        r = lax.rsqrt(_rowmean(xv * xv) + EPS)
        gs = _row(v_ref, r_g) * (1.0 + _row(v_ref, r_sc))
        hb = (xv * r * gs + _row(v_ref, r_sh)).astype(BF)
        o_ref[...] = jnp.dot(hb, w_ref[...], preferred_element_type=F32).astype(BF)

    return _pcall(
        body, name=name, grid=(S // ts,),
        in_specs=[pl.BlockSpec((ts, D), lambda i: (i, 0)), pl.BlockSpec(vecs.shape, lambda i: (0, 0)),
                  _resident((D, N), lambda i: (0, 0))],
        out_specs=pl.BlockSpec((ts, N), lambda i: (i, 0)),
        out_shape=jax.ShapeDtypeStruct((S, N), BF),
        compiler_params=_params(("arbitrary",)),
    )(x, vecs, w)


SEQ_TT = 256
SCAN_SEGMENTS = 4
CONV_PAD = 32


def conv_fwd(proj, convw4, conv_b, name):
    S = proj.shape[0]
    M = conv_b.shape[1]
    nb = M // LANES
    tt = min(SEQ_TT, S)

    def body(uv_ref, ug_ref, w_ref, b_ref, cq_ref, qp):
        qp[0:CONV_PAD, :] = jnp.zeros((CONV_PAD, LANES), F32)

        def step(t, carry):
            base = pl.multiple_of(t * tt, tt)
            qp[pl.ds(base + CONV_PAD, tt), :] = uv_ref[pl.ds(base, tt), :].astype(F32) * _sigmoid(ug_ref[pl.ds(base, tt), :].astype(F32))
            acc = jnp.broadcast_to(b_ref[...], (tt, LANES))
            for k in range(CONV_WIDTH):
                acc = acc + w_ref[k:k + 1, :] * qp[pl.ds(base + (CONV_PAD - CONV_WIDTH + 1) + k, tt), :]
            cq_ref[pl.ds(base, tt), :] = acc
            return carry

        lax.fori_loop(0, S // tt, step, 0)

    return _pcall(
        body, name=name, grid=(nb,),
        in_specs=[pl.BlockSpec((S, LANES), lambda c: (0, c)), pl.BlockSpec((S, LANES), lambda c: (0, c + nb)),
                  pl.BlockSpec((None, 32, LANES), lambda c: (c, 0, 0)), pl.BlockSpec((1, LANES), lambda c: (0, c))],
        out_specs=pl.BlockSpec((S, LANES), lambda c: (0, c)),
        out_shape=jax.ShapeDtypeStruct((S, M), F32),
        scratch_shapes=[pltpu.VMEM((S + CONV_PAD, LANES), F32)],
        compiler_params=_params(("arbitrary",)),
    )(proj, proj, convw4, conv_b)


def conv_bwd(dcq, proj, convw4, name):
    S, M = dcq.shape
    nb = M // LANES
    tt = min(SEQ_TT, S)
    off = CONV_PAD - CONV_WIDTH + 1

    def body(dcq_ref, uv_ref, ug_ref, w_ref, duv_ref, dug_ref, dw_ref, db_ref, qp, dp, dw8, db8):
        qp[0:CONV_PAD, :] = jnp.zeros((CONV_PAD, LANES), F32)
        dp[S:S + CONV_PAD, :] = jnp.zeros((CONV_PAD, LANES), F32)
        dw8[...] = jnp.zeros_like(dw8)
        db8[...] = jnp.zeros_like(db8)

        def fill(t, carry):
            base = pl.multiple_of(t * tt, tt)
            qp[pl.ds(base + CONV_PAD, tt), :] = uv_ref[pl.ds(base, tt), :].astype(F32) * _sigmoid(ug_ref[pl.ds(base, tt), :].astype(F32))
            dp[pl.ds(base, tt), :] = dcq_ref[pl.ds(base, tt), :].astype(F32)
            return carry

        lax.fori_loop(0, S // tt, fill, 0)

        def step(t, carry):
            base = pl.multiple_of(t * tt, tt)
            d_t = dcq_ref[pl.ds(base, tt), :].astype(F32)
            db8[...] += d_t.reshape(tt // SUBLANES, SUBLANES, LANES).sum(axis=0)
            dq = jnp.zeros((tt, LANES), F32)
            for k in range(CONV_WIDTH):
                prod = d_t * qp[pl.ds(base + off + k, tt), :]
                dw8[k] += prod.reshape(tt // SUBLANES, SUBLANES, LANES).sum(axis=0)
                dq = dq + w_ref[k:k + 1, :] * dp[pl.ds(base + (CONV_WIDTH - 1) - k, tt), :]
            uv = uv_ref[pl.ds(base, tt), :].astype(F32)
            sg = _sigmoid(ug_ref[pl.ds(base, tt), :].astype(F32))
            duv_ref[pl.ds(base, tt), :] = (dq * sg).astype(BF)
            dug_ref[pl.ds(base, tt), :] = (dq * uv * sg * (1.0 - sg)).astype(BF)
            return carry

        lax.fori_loop(0, S // tt, step, 0)
        dw_ref[...] = jnp.zeros_like(dw_ref)
        for k in range(CONV_WIDTH):
            dw_ref[k:k + 1, :] = _colsum(dw8[k])
        db_ref[...] = _colsum(db8[...])

    col = lambda o: pl.BlockSpec((S, LANES), lambda c: (0, c + o))
    return _pcall(
        body, name=name, grid=(nb,),
        in_specs=[col(0), col(0), col(nb), pl.BlockSpec((None, 32, LANES), lambda c: (c, 0, 0))],
        out_specs=[col(0), col(0), pl.BlockSpec((None, 32, LANES), lambda c: (c, 0, 0)),
                   pl.BlockSpec((1, LANES), lambda c: (0, c))],
        out_shape=[jax.ShapeDtypeStruct((S, M), BF), jax.ShapeDtypeStruct((S, M), BF),
                   jax.ShapeDtypeStruct((nb, 32, LANES), F32), jax.ShapeDtypeStruct((1, M), F32)],
        scratch_shapes=[pltpu.VMEM((S + CONV_PAD, LANES), F32), pltpu.VMEM((S + CONV_PAD, LANES), F32),
                        pltpu.VMEM((32, SUBLANES, LANES), F32), pltpu.VMEM((SUBLANES, LANES), F32)],
        compiler_params=_params(("arbitrary",)),
    )(dcq, proj, proj, convw4)


def _log_sigmoid(x):
    return jnp.minimum(x, 0.0) - jnp.log(1.0 + jnp.exp(-jnp.abs(x)))


def _rg_gate_terms(ra, ls):
    la = RG_C * ra * ls
    a = jnp.exp(la)
    th = jnp.tanh(la)
    mult = jnp.sqrt(-2.0 * th / (1.0 - th))
    return a, mult


def rnn_fwd(proj, rnnw4, rnn_b, bda, bdi, b_a, b_i, lam, name):
    S = proj.shape[0]
    M = rnn_b.shape[1]
    nb = M // LANES
    tt = min(SEQ_TT, S)
    KW = RNN_CONV_WIDTH
    nseg = SCAN_SEGMENTS if S % (SCAN_SEGMENTS * tt) == 0 else 1

    def body(ux_ref, w_ref, rb_ref, bda_ref, bdi_ref, ba_ref, bi_ref, lam_ref,
             xr_ref, ra_ref, ii_ref, h_ref, uxp, a_sc, b_sc):
        uxp[0:SUBLANES, :] = jnp.zeros((SUBLANES, LANES), F32)
        ls = _log_sigmoid(lam_ref[...])

        def step(t, carry):
            base = pl.multiple_of(t * tt, tt)
            uxp[pl.ds(base + SUBLANES, tt), :] = ux_ref[pl.ds(base, tt), :].astype(F32)
            xr = jnp.broadcast_to(rb_ref[...], (tt, LANES))
            for k in range(KW):
                xr = xr + w_ref[k:k + 1, :] * uxp[pl.ds(base + (SUBLANES - KW + 1) + k, tt), :]
            xb = xr.astype(BF)
            ra = _sigmoid(jnp.dot(xb, bda_ref[...], preferred_element_type=F32) + ba_ref[...])
            ii = _sigmoid(jnp.dot(xb, bdi_ref[...], preferred_element_type=F32) + bi_ref[...])
            a, mult = _rg_gate_terms(ra, ls)
            xr_ref[pl.ds(base, tt), :] = xr
            ra_ref[pl.ds(base, tt), :] = ra
            ii_ref[pl.ds(base, tt), :] = ii
            a_sc[pl.ds(base, tt), :] = a
            b_sc[pl.ds(base, tt), :] = mult * (ii * xr)
            return carry

        lax.fori_loop(0, S // tt, step, 0)

        rows = lax.broadcasted_iota(I32, (SUBLANES, LANES), 0)
        seg = S // nseg
        last = lambda v: jnp.broadcast_to(v[SUBLANES - 1:SUBLANES, :], (SUBLANES, LANES))

        def scan(t, carry):
            hs, ps = carry
            new_h, new_p = [], []
            for s in range(nseg):
                base = pl.multiple_of(s * seg + t * SUBLANES, SUBLANES)
                A = a_sc[pl.ds(base, SUBLANES), :]
                B = b_sc[pl.ds(base, SUBLANES), :]
                for d in (1, 2, 4):
                    As = jnp.where(rows >= d, pltpu.roll(A, d, axis=0), 1.0)
                    Bs = jnp.where(rows >= d, pltpu.roll(B, d, axis=0), 0.0)
                    B = A * Bs + B
                    A = A * As
                hh = B + A * hs[s]
                h_ref[pl.ds(base, SUBLANES), :] = hh
                pp = A * ps[s]
                if s > 0:
                    a_sc[pl.ds(base, SUBLANES), :] = pp
                new_h.append(last(hh))
                new_p.append(last(pp))
            return tuple(new_h), tuple(new_p)

        zero8 = jnp.zeros((SUBLANES, LANES), F32)
        one8 = jnp.ones((SUBLANES, LANES), F32)
        hs, ps = lax.fori_loop(0, seg // SUBLANES, scan, ((zero8,) * nseg, (one8,) * nseg))
        carry_in = hs[0]
        for s in range(1, nseg):
            c_row = carry_in[0:1, :]

            def fix(t, c, s=s, c_row=c_row):
                base = pl.multiple_of(s * seg + t * tt, tt)
                h_ref[pl.ds(base, tt), :] = h_ref[pl.ds(base, tt), :] + a_sc[pl.ds(base, tt), :] * c_row
                return c

            lax.fori_loop(0, seg // tt, fix, 0)
            carry_in = hs[s] + ps[s] * carry_in

    col = lambda o: pl.BlockSpec((S, LANES), lambda c: (0, c + o))
    vec = pl.BlockSpec((1, LANES), lambda c: (0, c))
    diag = pl.BlockSpec((LANES, LANES), lambda c: (c, c))
    return _pcall(
        body, name=name, grid=(nb,),
        in_specs=[col(2 * nb), pl.BlockSpec((None, SUBLANES, LANES), lambda c: (c, 0, 0)), vec, diag, diag, vec, vec, vec],
        out_specs=[col(0)] * 4,
        out_shape=[jax.ShapeDtypeStruct((S, M), F32)] * 4,
        scratch_shapes=[pltpu.VMEM((S + SUBLANES, LANES), F32), pltpu.VMEM((S, LANES), F32), pltpu.VMEM((S, LANES), F32)],
        compiler_params=_params(("arbitrary",)),
    )(proj, rnnw4, rnn_b, bda, bdi, b_a, b_i, lam)


def rnn_bwd(dhout, h, xr, ra, ii, proj, rnnw4, bda, bdi, lam, name):
    S, M = h.shape
    nb = M // LANES
    tt = min(SEQ_TT, S)
    KW = RNN_CONV_WIDTH
    SL = SUBLANES
    nseg = SCAN_SEGMENTS if S % (SCAN_SEGMENTS * tt) == 0 else 1

    def body(dh_ref, h_ref, xr_ref, ra_ref, ii_ref, ux_ref, w_ref, bda_ref, bdi_ref, lam_ref,
             dux_ref, dwa_ref, dwi_ref, drw_ref, vec_ref,
             a_sc, hp, g_sc, dpa_sc, dpi_sc, dxp, uxp, acc8, drw8, p_sc):
        zero8 = jnp.zeros((SL, LANES), F32)
        a_sc[S:S + SL, :] = zero8
        hp[0:SL, :] = zero8
        dxp[S:S + SL, :] = zero8
        uxp[0:SL, :] = zero8
        acc8[...] = jnp.zeros_like(acc8)
        drw8[...] = jnp.zeros_like(drw8)
        lamv = lam_ref[...]
        ls = _log_sigmoid(lamv)

        def fill(t, carry):
            base = pl.multiple_of(t * tt, tt)
            a_sc[pl.ds(base, tt), :] = jnp.exp(RG_C * ra_ref[pl.ds(base, tt), :] * ls)
            hp[pl.ds(base + SL, tt), :] = h_ref[pl.ds(base, tt), :]
            uxp[pl.ds(base + SL, tt), :] = ux_ref[pl.ds(base, tt), :].astype(F32)
            return carry

        lax.fori_loop(0, S // tt, fill, 0)

        rows = lax.broadcasted_iota(I32, (SL, LANES), 0)
        seg = S // nseg
        nt8 = seg // SL
        first = lambda v: jnp.broadcast_to(v[0:1, :], (SL, LANES))

        def rscan(t, carry):
            gs, ps = carry
            new_g, new_p = [], []
            for s in range(nseg):
                base = pl.multiple_of(s * seg + (nt8 - 1 - t) * SL, SL)
                A = a_sc[pl.ds(base + 1, SL), :]
                B = dh_ref[pl.ds(base, SL), :]
                for d in (1, 2, 4):
                    As = jnp.where(rows < SL - d, pltpu.roll(A, SL - d, axis=0), 1.0)
                    Bs = jnp.where(rows < SL - d, pltpu.roll(B, SL - d, axis=0), 0.0)
                    B = A * Bs + B
                    A = A * As
                g = B + A * gs[s]
                g_sc[pl.ds(base, SL), :] = g
                pp = A * ps[s]
                if s < nseg - 1:
                    p_sc[pl.ds(base, SL), :] = pp
                new_g.append(first(g))
                new_p.append(first(pp))
            return tuple(new_g), tuple(new_p)

        one8 = jnp.ones((SL, LANES), F32)
        gs, ps = lax.fori_loop(0, nt8, rscan, ((zero8,) * nseg, (one8,) * nseg))
        carry_in = gs[nseg - 1]
        for s in range(nseg - 2, -1, -1):
            c_row = carry_in[0:1, :]

            def fix(t, c, s=s, c_row=c_row):
                base = pl.multiple_of(s * seg + t * tt, tt)
                g_sc[pl.ds(base, tt), :] = g_sc[pl.ds(base, tt), :] + p_sc[pl.ds(base, tt), :] * c_row
                return c

            lax.fori_loop(0, seg // tt, fix, 0)
            carry_in = gs[s] + ps[s] * carry_in

        def red8(v):
            return v.reshape(tt // SL, SL, LANES).sum(axis=0)

        def step(t, carry):
            base = pl.multiple_of(t * tt, tt)
            g = g_sc[pl.ds(base, tt), :]
            hprev = hp[pl.ds(base + SL - 1, tt), :]
            xr_t = xr_ref[pl.ds(base, tt), :]
            ra_t = ra_ref[pl.ds(base, tt), :]
            ii_t = ii_ref[pl.ds(base, tt), :]
            a, mult = _rg_gate_terms(ra_t, ls)
            gx = g * xr_t
            dmult = gx * ii_t
            dii = gx * mult
            dxr = g * (mult * ii_t)
            dla = g * hprev * a - dmult * (a * a) / mult
            acc8[3] += red8(dla * ra_t)
            dpa = dla * (RG_C * ls) * ra_t * (1.0 - ra_t)
            dpi = dii * ii_t * (1.0 - ii_t)
            dpab = dpa.astype(BF)
            dpib = dpi.astype(BF)
            dxr = dxr + (lax.dot_general(dpab, bda_ref[...], CONTRACT_LAST, preferred_element_type=F32)
                         + lax.dot_general(dpib, bdi_ref[...], CONTRACT_LAST, preferred_element_type=F32))
            dpa_sc[pl.ds(base, tt), :] = dpab
            dpi_sc[pl.ds(base, tt), :] = dpib
            dxp[pl.ds(base, tt), :] = dxr
            acc8[0] += red8(dxr)
            acc8[1] += red8(dpa)
            acc8[2] += red8(dpi)
            return carry

        lax.fori_loop(0, S // tt, step, 0)

        def convb(t, carry):
            base = pl.multiple_of(t * tt, tt)
            d_t = dxp[pl.ds(base, tt), :]
            dux = jnp.zeros((tt, LANES), F32)
            for k in range(KW):
                drw8[k] += red8(d_t * uxp[pl.ds(base + (SL - KW + 1) + k, tt), :])
                dux = dux + w_ref[k:k + 1, :] * dxp[pl.ds(base + (KW - 1) - k, tt), :]
            dux_ref[pl.ds(base, tt), :] = dux.astype(BF)
            return carry

        lax.fori_loop(0, S // tt, convb, 0)

        xb = xr_ref[...].astype(BF)
        dwa_ref[...] = lax.dot_general(xb, dpa_sc[...], CONTRACT_FIRST, preferred_element_type=F32)
        dwi_ref[...] = lax.dot_general(xb, dpi_sc[...], CONTRACT_FIRST, preferred_element_type=F32)
        drw_ref[...] = jnp.zeros_like(drw_ref)
        vec_ref[...] = jnp.zeros_like(vec_ref)
        for k in range(KW):
            drw_ref[k:k + 1, :] = _colsum(drw8[k])
        for k in range(3):
            vec_ref[k:k + 1, :] = _colsum(acc8[k])
        vec_ref[3:4, :] = _colsum(acc8[3]) * (RG_C * _sigmoid(-lamv))

    col = lambda o: pl.BlockSpec((S, LANES), lambda c: (0, c + o))
    vec = pl.BlockSpec((1, LANES), lambda c: (0, c))
    diag = pl.BlockSpec((LANES, LANES), lambda c: (c, c))
    blk3 = lambda r: pl.BlockSpec((None, r, LANES), lambda c: (c, 0, 0))
    return _pcall(
        body, name=name, grid=(nb,),
        in_specs=[col(0), col(0), col(0), col(0), col(0), col(2 * nb), blk3(SL), diag, diag, vec],
        out_specs=[col(0), blk3(LANES), blk3(LANES), blk3(SL), pl.BlockSpec((SL, LANES), lambda c: (0, c))],
        out_shape=[jax.ShapeDtypeStruct((S, M), BF), jax.ShapeDtypeStruct((nb, LANES, LANES), F32),
                   jax.ShapeDtypeStruct((nb, LANES, LANES), F32), jax.ShapeDtypeStruct((nb, SL, LANES), F32),
                   jax.ShapeDtypeStruct((SL, M), F32)],
        scratch_shapes=[pltpu.VMEM((S + SL, LANES), F32), pltpu.VMEM((S + SL, LANES), F32), pltpu.VMEM((S, LANES), F32),
                        pltpu.VMEM((S, LANES), BF), pltpu.VMEM((S, LANES), BF), pltpu.VMEM((S + SL, LANES), F32),
                        pltpu.VMEM((S + SL, LANES), F32), pltpu.VMEM((SL, SL, LANES), F32), pltpu.VMEM((SL, SL, LANES), F32),
                        pltpu.VMEM((S, LANES), F32)],
        compiler_params=_params(("arbitrary",)),
    )(dhout, h, xr, ra, ii, proj, rnnw4, bda, bdi, lam)


GELU_K = 0.7978845608028654
GELU_C = 0.044715


def _layernorm_parts(cq):
    mu = _rowmean(cq)
    d = cq - mu
    rstd = lax.rsqrt(_rowmean(d * d) + EPS)
    return d * rstd, rstd


def mix_out(cq, proj, h, x, vecs, lnv, wout, name):
    S, D = x.shape
    M = cq.shape[1]
    ts = min(512, S)

    def body(cq_ref, uy_ref, h_ref, x_ref, v_ref, ln_ref, w_ref, xo_ref, ym_ref, yc_ref):
        z, _ = _layernorm_parts(cq_ref[...])
        l = z * _row(ln_ref, 0) + _row(ln_ref, 1)
        yc_ref[:, 0:M] = (l * _sigmoid(l)).astype(BF)
        uy = uy_ref[...].astype(F32)
        gelu = 0.5 * uy * (1.0 + jnp.tanh(GELU_K * (uy + GELU_C * uy * uy * uy)))
        yc_ref[:, M:2 * M] = (gelu * h_ref[...]).astype(BF)
        ym = jnp.dot(yc_ref[...], w_ref[...], preferred_element_type=F32)
        ym_ref[...] = ym.astype(BF)
        xo_ref[...] = x_ref[...] + _row(v_ref, R_GT2) * ym

    tok = pl.BlockSpec((ts, D), lambda i: (i, 0))
    mtok = lambda o: pl.BlockSpec((ts, M), lambda i: (i, o))
    return _pcall(
        body, name=name, grid=(S // ts,),
        in_specs=[mtok(0), mtok(3), mtok(0), tok, pl.BlockSpec(vecs.shape, lambda i: (0, 0)),
                  pl.BlockSpec(lnv.shape, lambda i: (0, 0)), pl.BlockSpec(wout.shape, lambda i: (0, 0))],
        out_specs=[tok, tok, pl.BlockSpec((ts, 2 * M), lambda i: (i, 0))],
        out_shape=[jax.ShapeDtypeStruct((S, D), F32), jax.ShapeDtypeStruct((S, D), BF),
                   jax.ShapeDtypeStruct((S, 2 * M), BF)],
        compiler_params=_params(("arbitrary",)),
    )(cq, proj, h, x, vecs, lnv, wout)


def mix_out_bwd(dxo, ym, vecs, wout, cq, lnv, proj, h, name):
    S, D = dxo.shape
    M = cq.shape[1]
    ts = min(512, S)

    def body(dxo_ref, ym_ref, v_ref, w_ref, cq_ref, ln_ref, uy_ref, h_ref,
             dcq_ref, dh_ref, duy_ref, dyb_ref, vgd_ref, vgm_ref):
        @pl.when(pl.program_id(0) == 0)
        def _():
            vgd_ref[...] = jnp.zeros_like(vgd_ref)
            vgm_ref[...] = jnp.zeros_like(vgm_ref)

        dxo_v = dxo_ref[...]
        dyb = (_row(v_ref, R_GT2) * dxo_v).astype(BF)
        dyb_ref[...] = dyb
        vgd_ref[0:1, :] += _colsum(dxo_v * ym_ref[...].astype(F32))
        dycat = lax.dot_general(dyb, w_ref[...], CONTRACT_LAST, preferred_element_type=F32)
        dyc = dycat[:, 0:M]
        dyr = dycat[:, M:2 * M]
        z, rstd = _layernorm_parts(cq_ref[...])
        lng = _row(ln_ref, 0)
        l = z * lng + _row(ln_ref, 1)
        sl = _sigmoid(l)
        dl = dyc * (sl * (1.0 + l * (1.0 - sl)))
        vgm_ref[0:1, :] += _colsum(dl * z)
        vgm_ref[1:2, :] += _colsum(dl)
        dz = dl * lng
        dcq_ref[...] = (rstd * (dz - _rowmean(dz) - z * _rowmean(dz * z))).astype(BF)
        uy = uy_ref[...].astype(F32)
        u2 = uy * uy
        th = jnp.tanh(GELU_K * (uy + GELU_C * uy * u2))
        gelu = 0.5 * uy * (1.0 + th)
        dgelu = 0.5 * (1.0 + th) + 0.5 * uy * (1.0 - th * th) * (GELU_K * (1.0 + 3.0 * GELU_C * u2))
        dh_ref[...] = dyr * gelu
        duy_ref[...] = (dyr * h_ref[...] * dgelu).astype(BF)

    tok = pl.BlockSpec((ts, D), lambda i: (i, 0))
    mtok = lambda o: pl.BlockSpec((ts, M), lambda i: (i, o))
    return _pcall(
        body, name=name, grid=(S // ts,),
        in_specs=[tok, tok, pl.BlockSpec(vecs.shape, lambda i: (0, 0)), pl.BlockSpec(wout.shape, lambda i: (0, 0)),
                  mtok(0), pl.BlockSpec(lnv.shape, lambda i: (0, 0)), mtok(3), mtok(0)],
        out_specs=[mtok(0), mtok(0), mtok(0), tok, pl.BlockSpec((SUBLANES, D), lambda i: (0, 0)),
                   pl.BlockSpec((SUBLANES, M), lambda i: (0, 0))],
        out_shape=[jax.ShapeDtypeStruct((S, M), BF), jax.ShapeDtypeStruct((S, M), F32), jax.ShapeDtypeStruct((S, M), BF),
                   jax.ShapeDtypeStruct((S, D), BF),
                   jax.ShapeDtypeStruct((SUBLANES, D), F32), jax.ShapeDtypeStruct((SUBLANES, M), F32)],
        compiler_params=_params(("arbitrary",)),
    )(dxo, ym, vecs, wout, cq, lnv, proj, h)


def mix_in_bwd(dparts, x, dxo, vecs, win, name):
    S, D = x.shape
    M = dparts[0].shape[1]
    ts = min(512, S)

    def body(d0, d1, d2, d3, x_ref, dxo_ref, v_ref, w_ref, dx_ref, hb_ref, dp_ref, vg_ref):
        @pl.when(pl.program_id(0) == 0)
        def _():
            vg_ref[...] = jnp.zeros_like(vg_ref)

        for q, dref in enumerate((d0, d1, d2, d3)):
            dp_ref[:, q * M:(q + 1) * M] = dref[...].astype(BF)
        dh = lax.dot_general(dp_ref[...], w_ref[...], CONTRACT_LAST, preferred_element_type=F32)
        xv = x_ref[...]
        r = lax.rsqrt(_rowmean(xv * xv) + EPS)
        n = xv * r
        g = _row(v_ref, R_G2)
        sc1 = 1.0 + _row(v_ref, R_SC2)
        gsc = g * sc1
        hb_ref[...] = (n * gsc + _row(v_ref, R_SH2)).astype(BF)
        dhn = dh * n
        vg_ref[0:1, :] += _colsum(dh)
        vg_ref[1:2, :] += _colsum(dhn) * g
        vg_ref[2:3, :] += _colsum(dhn) * sc1
        dn = dh * gsc
        dx_ref[...] = dxo_ref[...] + r * (dn - n * _rowmean(dn * n))

    tok = pl.BlockSpec((ts, D), lambda i: (i, 0))
    mtok = pl.BlockSpec((ts, M), lambda i: (i, 0))
    return _pcall(
        body, name=name, grid=(S // ts,),
        in_specs=[mtok] * 4 + [tok, tok, pl.BlockSpec(vecs.shape, lambda i: (0, 0)), pl.BlockSpec(win.shape, lambda i: (0, 0))],
        out_specs=[tok, tok, pl.BlockSpec((ts, 4 * M), lambda i: (i, 0)), pl.BlockSpec((SUBLANES, D), lambda i: (0, 0))],
        out_shape=[jax.ShapeDtypeStruct((S, D), F32), jax.ShapeDtypeStruct((S, D), BF),
                   jax.ShapeDtypeStruct((S, 4 * M), BF), jax.ShapeDtypeStruct((SUBLANES, D), F32)],
        compiler_params=_params(("arbitrary",)),
    )(*dparts, x, dxo, vecs, win)


def _adamw(w, g, m, v):
    m = ADAM_B1 * m + (1.0 - ADAM_B1) * g
    v = ADAM_B2 * v + (1.0 - ADAM_B2) * (g * g)
    m_hat = m / (1.0 - ADAM_B1 ** ADAM_STEP)
    v_hat = v / (1.0 - ADAM_B2 ** ADAM_STEP)
    delta = -ADAM_LR * (m_hat / (jnp.sqrt(v_hat) + ADAM_EPS) + ADAM_WD * w)
    return delta, m, v


def adam_big(w, g_mine, g_sib, m, v, cidx, name):
    R, C = w.shape
    hr = R // 2
    tr = 256 if hr % 256 == 0 else hr
    tc = C if C <= 1536 else (1152 if C % 1152 == 0 else 1024)
    assert hr % tr == 0 and C % tc == 0 and g_mine.shape == (hr, C)
    nrb = hr // tr

    def body(ci_ref, w_ref, gm_ref, gs_ref, m_ref, v_ref, g_ref, d_ref, nm_ref, nv_ref):
        mine = (pl.program_id(0) // nrb) == ci_ref[0]
        g = jnp.where(mine, gm_ref[...], gs_ref[...])
        d, nm, nv = _adamw(w_ref[...], g, m_ref[...], v_ref[...])
        g_ref[...] = g
        d_ref[...] = d
        nm_ref[...] = nm
        nv_ref[...] = nv

    blk = pl.BlockSpec((tr, tc), lambda i, j, ci: (i, j))
    half = pl.BlockSpec((tr, tc), lambda i, j, ci: (i % nrb, j))
    gs = pltpu.PrefetchScalarGridSpec(num_scalar_prefetch=1, grid=(R // tr, C // tc), in_specs=[blk, half, half, blk, blk],
                                      out_specs=[blk] * 4)
    return _pcall(body, name=name, grid_spec=gs, out_shape=[jax.ShapeDtypeStruct((R, C), F32)] * 4,
                  compiler_params=_params(("parallel", "parallel")))(cidx, w, g_mine, g_sib, m, v)


def adam_cond(c_all, dmod, w, m, v, name):
    B, Kin = c_all.shape
    N = w.shape[1]
    tn = 768 if N % 768 == 0 else 256
    assert N % tn == 0

    def body(c_ref, d_ref, w_ref, m_ref, v_ref, g_ref, dl_ref, nm_ref, nv_ref):
        cv = c_ref[...]
        ca = cv * _sigmoid(cv)
        g = lax.dot_general(ca.astype(BF), d_ref[...].astype(BF), CONTRACT_FIRST, preferred_element_type=F32)
        d, nm, nv = _adamw(w_ref[...], g, m_ref[...], v_ref[...])
        g_ref[...] = g
        dl_ref[...] = d
        nm_ref[...] = nm
        nv_ref[...] = nv

    blk = pl.BlockSpec((Kin, tn), lambda n: (0, n))
    return _pcall(
        body, name=name, grid=(N // tn,),
        in_specs=[pl.BlockSpec((B, Kin), lambda n: (0, 0)), pl.BlockSpec((B, tn), lambda n: (0, n)), blk, blk, blk],
        out_specs=[blk] * 4, out_shape=[jax.ShapeDtypeStruct((Kin, N), F32)] * 4,
        compiler_params=_params(("parallel",)),
    )(c_all, dmod, w, m, v)


def adam_small(ws, gs, ms, vs, name):
    n = len(ws)

    def body(*refs):
        ins, outs = refs[:4 * n], refs[4 * n:]
        for k in range(n):
            d, nm, nv = _adamw(ins[k][...], ins[n + k][...], ins[2 * n + k][...], ins[3 * n + k][...])
            outs[k][...] = d
            outs[n + k][...] = nm
            outs[2 * n + k][...] = nv

    specs = [pl.BlockSpec(w.shape, lambda i: (0, 0)) for w in ws]
    shapes = [jax.ShapeDtypeStruct(w.shape, F32) for w in ws]
    out = _pcall(body, name=name, grid=(1,), in_specs=specs * 4, out_specs=specs * 3, out_shape=shapes * 3,
                 compiler_params=_params(("arbitrary",)))(*ws, *gs, *ms, *vs)
    return out[:n], out[n:2 * n], out[2 * n:]


def _me():
    return lax.axis_index("x"), lax.axis_index("y"), lax.axis_index("c")


def _flip(x, y, p):
    return (x ^ (p >> 1) if (p >> 1) else x), (y ^ (p & 1) if (p & 1) else y)


def _handshake(peers):
    barrier = pltpu.get_barrier_semaphore()
    for peer in peers:
        pl.semaphore_signal(barrier, inc=1, device_id=peer, device_id_type=MESH)
    pl.semaphore_wait(barrier, len(peers))


def _seq_call(body, *, name, n_in, out_shape, sem_shapes, collective_id):
    del n_in
    return pl.kernel(body, out_type=out_shape, mesh=plsc.ScalarSubcoreMesh(axis_name="sq", num_cores=1), name=name,
                     scratch_types=sem_shapes, compiler_params=pltpu.CompilerParams(collective_id=collective_id))


def _hbm_comm_call(body, *, name, n_in, out_shape, sem_shapes, seq_id):
    if seq_id is not None:
        return _seq_call(body, name=name, n_in=n_in, out_shape=out_shape, sem_shapes=sem_shapes, collective_id=seq_id)
    anyspec = pl.BlockSpec(memory_space=pl.ANY)
    return _pcall(body, name=name, in_specs=[anyspec] * n_in, out_specs=[anyspec] * len(out_shape), out_shape=out_shape,
                  scratch_shapes=sem_shapes, compiler_params=_params())


def allgather_devices(v, name, with_sum=False):
    R, L = v.shape

    def body(v_ref, out_ref, *rest):
        if with_sum:
            sum_ref, send_sems, recv_sems = rest
        else:
            send_sems, recv_sems = rest
        x, y, c = _me()
        me = 4 * x + 2 * y + c
        out_ref[me] = v_ref[...]
        copies = []
        for p in range(1, N_DEV):
            px, py = _flip(x, y, p >> 1)
            pc = (1 - c) if (p & 1) else c
            peer = 4 * px + 2 * py + pc
            send = pltpu.make_async_remote_copy(src_ref=v_ref, dst_ref=out_ref.at[me], send_sem=send_sems.at[p - 1],
                                                recv_sem=recv_sems.at[p - 1], device_id=(px, py, pc), device_id_type=MESH)
            send.start()
            recv = pltpu.make_async_remote_copy(src_ref=v_ref, dst_ref=out_ref.at[peer], send_sem=send_sems.at[p - 1],
                                                recv_sem=recv_sems.at[p - 1], device_id=(px, py, pc), device_id_type=MESH)
            copies.append((send, recv))
        for send, recv in copies:
            recv.wait_recv()
        for send, recv in copies:
            send.wait_send()
        if with_sum:
            s = out_ref[0]
            for k in range(1, N_DEV):
                s = s + out_ref[k]
            sum_ref[...] = s

    vm = pl.BlockSpec(memory_space=pltpu.VMEM)
    out_shape = [jax.ShapeDtypeStruct((N_DEV, R, L), F32)]
    if with_sum:
        out_shape.append(jax.ShapeDtypeStruct((R, L), F32))
    return _pcall(
        body, name=name, in_specs=[vm], out_specs=[vm] * len(out_shape), out_shape=out_shape,
        scratch_shapes=[pltpu.SemaphoreType.DMA((N_DEV - 1,)), pltpu.SemaphoreType.DMA((N_DEV - 1,))],
        compiler_params=_params(),
    )(v)


def allgather_devices_hbm(v, name, seq_id):
    R, L = v.shape

    def body(v_ref, out_ref, send_sems, recv_sems, local_sem):
        x, y, c = _me()
        me = 4 * x + 2 * y + c
        peers = []
        for p in range(1, N_DEV):
            px, py = _flip(x, y, p >> 1)
            peers.append((px, py, (1 - c) if (p & 1) else c))
        _handshake(peers)
        lc = pltpu.make_async_copy(v_ref, out_ref.at[me], local_sem)
        lc.start()
        copies = []
        for p, (px, py, pc) in enumerate(peers):
            send = pltpu.make_async_remote_copy(src_ref=v_ref, dst_ref=out_ref.at[me], send_sem=send_sems.at[p],
                                                recv_sem=recv_sems.at[p], device_id=(px, py, pc), device_id_type=MESH)
            send.start()
            recv = pltpu.make_async_remote_copy(src_ref=v_ref, dst_ref=out_ref.at[4 * px + 2 * py + pc], send_sem=send_sems.at[p],
                                                recv_sem=recv_sems.at[p], device_id=(px, py, pc), device_id_type=MESH)
            copies.append((send, recv))
        for send, recv in copies:
            recv.wait_recv()
        for send, recv in copies:
            send.wait_send()
        lc.wait()

    return _seq_call(body, name=name, n_in=1, out_shape=[jax.ShapeDtypeStruct((N_DEV, R, L), F32)],
                     sem_shapes=[pltpu.SemaphoreType.DMA((N_DEV - 1,)), pltpu.SemaphoreType.DMA((N_DEV - 1,)),
                                 pltpu.SemaphoreType.DMA], collective_id=seq_id)(v)[0]


def sum_slots(g, name):
    n, R, L = g.shape
    tr = 216 if R % 216 == 0 else R
    assert R % tr == 0 and tr % SUBLANES == 0

    def body(g_ref, o_ref):
        s = g_ref[0]
        for k in range(1, n):
            s = s + g_ref[k]
        o_ref[...] = s

    return _pcall(body, name=name, grid=(R // tr,), in_specs=[pl.BlockSpec((n, tr, L), lambda i: (0, i, 0))],
                  out_specs=pl.BlockSpec((tr, L), lambda i: (i, 0)), out_shape=jax.ShapeDtypeStruct((R, L), F32),
                  compiler_params=_params(("parallel",)))(g)


def allgather_chips(v, name):
    R, L = v.shape

    def body(v_ref, out_ref, send_sems, recv_sems):
        x, y, c = _me()
        chip = 2 * x + y
        out_ref[chip] = v_ref[...]
        copies = []
        for p in range(1, N_CHIPS):
            px, py = _flip(x, y, p)
            send = pltpu.make_async_remote_copy(src_ref=v_ref, dst_ref=out_ref.at[chip], send_sem=send_sems.at[p - 1],
                                                recv_sem=recv_sems.at[p - 1], device_id=(px, py, c), device_id_type=MESH)
            send.start()
            recv = pltpu.make_async_remote_copy(src_ref=v_ref, dst_ref=out_ref.at[2 * px + py], send_sem=send_sems.at[p - 1],
                                                recv_sem=recv_sems.at[p - 1], device_id=(px, py, c), device_id_type=MESH)
            copies.append((send, recv))
        for send, recv in copies:
            recv.wait_recv()
        for send, recv in copies:
            send.wait_send()

    vm = pl.BlockSpec(memory_space=pltpu.VMEM)
    return _pcall(
        body, name=name, in_specs=[vm], out_specs=vm, out_shape=jax.ShapeDtypeStruct((N_CHIPS, R, L), F32),
        scratch_shapes=[pltpu.SemaphoreType.DMA((N_CHIPS - 1,)), pltpu.SemaphoreType.DMA((N_CHIPS - 1,))],
        compiler_params=_params(),
    )(v)


def _shard_window(ref, kind, shard_shape, chip, half):
    r, c = shard_shape
    hr = r // 2
    if kind == "col":
        return ref.at[pl.ds(pl.multiple_of(half * hr, hr), hr), pl.ds(pl.multiple_of(chip * c, c), c)]
    return ref.at[pl.ds(pl.multiple_of(chip * r + half * hr, hr), hr), :]


def allgather_weights(shards, kinds, name, seq_id=None):
    n = len(shards)
    fulls = []
    for s, kind in zip(shards, kinds):
        r, c = s.shape
        fulls.append(jax.ShapeDtypeStruct((r, N_CHIPS * c) if kind == "col" else (N_CHIPS * r, c), s.dtype))

    def body(*refs):
        srcs, outs = refs[:n], refs[n:2 * n]
        send_sems, recv_sems, fsend_sems, frecv_sems = refs[2 * n:]
        x, y, c = _me()
        chip = 2 * x + y
        sib = (x, y, 1 - c)
        if seq_id is not None:
            _handshake([(*_flip(x, y, p), c) for p in range(1, N_CHIPS)] + [sib])
        sends, fwds = [], []
        for i in range(n):
            shp = srcs[i].shape
            hr = shp[0] // 2
            my_half = srcs[i].at[pl.ds(pl.multiple_of(c * hr, hr), hr), :]
            for p in range(1, N_CHIPS):
                px, py = _flip(x, y, p)
                k = i * (N_CHIPS - 1) + p - 1
                cp = pltpu.make_async_remote_copy(src_ref=my_half, dst_ref=_shard_window(outs[i], kinds[i], shp, chip, c),
                                                  send_sem=send_sems.at[k], recv_sem=recv_sems.at[k],
                                                  device_id=(px, py, c), device_id_type=MESH)
                cp.start()
                sends.append(cp)
        for i in range(n):
            shp = srcs[i].shape
            for p in range(1, N_CHIPS):
                px, py = _flip(x, y, p)
                k = i * (N_CHIPS - 1) + p - 1
                landed = _shard_window(outs[i], kinds[i], shp, 2 * px + py, c)
                pltpu.make_async_remote_copy(src_ref=landed, dst_ref=landed, send_sem=send_sems.at[k], recv_sem=recv_sems.at[k],
                                             device_id=(px, py, c), device_id_type=MESH).wait_recv()
                fw = pltpu.make_async_remote_copy(src_ref=landed, dst_ref=landed, send_sem=fsend_sems.at[k],
                                                  recv_sem=frecv_sems.at[k], device_id=sib, device_id_type=MESH)
                fw.start()
                fwds.append(fw)
        for i in range(n):
            shp = srcs[i].shape
            for p in range(1, N_CHIPS):
                px, py = _flip(x, y, p)
                k = i * (N_CHIPS - 1) + p - 1
                other = _shard_window(outs[i], kinds[i], shp, 2 * px + py, 1 - c)
                pltpu.make_async_remote_copy(src_ref=other, dst_ref=other, send_sem=fsend_sems.at[k], recv_sem=frecv_sems.at[k],
                                             device_id=sib, device_id_type=MESH).wait_recv()
        for cp in sends + fwds:
            cp.wait_send()

    nk = n * (N_CHIPS - 1)
    gathered = _hbm_comm_call(
        body, name=name, n_in=n, out_shape=fulls, seq_id=seq_id,
        sem_shapes=[pltpu.SemaphoreType.DMA((nk,)), pltpu.SemaphoreType.DMA((nk,)), pltpu.SemaphoreType.DMA((nk,)),
                    pltpu.SemaphoreType.DMA((nk,))],
    )(*shards)
    return gathered


def place_local_shards(fulls, shards, kinds, name):
    n = len(shards)
    chip = jnp.reshape(2 * lax.axis_index("x") + lax.axis_index("y"), (1,)).astype(I32)

    def body(ci_ref, *refs):
        for i in range(n):
            refs[2 * n + i][...] = refs[i][...]

    in_specs = [pl.BlockSpec(s.shape, lambda i, ci: (0, 0)) for s in shards] + [pl.BlockSpec(memory_space=pl.ANY)] * n
    out_specs = [pl.BlockSpec(s.shape, (lambda i, ci: (0, ci[0])) if k == "col" else (lambda i, ci: (ci[0], 0)))
                 for s, k in zip(shards, kinds)]
    gs = pltpu.PrefetchScalarGridSpec(num_scalar_prefetch=1, grid=(1,), in_specs=in_specs, out_specs=out_specs)
    return _pcall(body, name=name, grid_spec=gs, out_shape=[jax.ShapeDtypeStruct(f.shape, f.dtype) for f in fulls],
                  input_output_aliases={1 + n + i: i for i in range(n)}, compiler_params=_params(("arbitrary",)))(chip, *shards, *fulls)


def _as_halves(g, kind, shard_shape):
    r, c = shard_shape
    if kind == "col":
        return g.reshape(2, r // 2, N_CHIPS * c)
    return g.reshape(N_CHIPS, 2, r // 2, c)


def exchange_sibling_halves(grads, kinds, shard_shapes, name, seq_id=None):
    n = len(grads)
    views = [_as_halves(g, k, s) for g, k, s in zip(grads, kinds, shard_shapes)]
    outs = []
    for k, (r, c) in zip(kinds, shard_shapes):
        outs.append(jax.ShapeDtypeStruct((r // 2, N_CHIPS * c) if k == "col" else (N_CHIPS, r // 2, c), grads[0].dtype))

    def body(*refs):
        srcs, dsts = refs[:n], refs[n:2 * n]
        send_sems, recv_sems = refs[2 * n:]
        x, y, c = _me()
        if seq_id is not None:
            _handshake([(x, y, 1 - c)])
        cps = []
        for i in range(n):
            src = srcs[i].at[1 - c] if kinds[i] == "col" else srcs[i].at[:, 1 - c]
            cp = pltpu.make_async_remote_copy(src_ref=src, dst_ref=dsts[i], send_sem=send_sems.at[i], recv_sem=recv_sems.at[i],
                                              device_id=(x, y, 1 - c), device_id_type=MESH)
            cp.start()
            cps.append(cp)
        for cp in cps:
            cp.wait_recv()
        for cp in cps:
            cp.wait_send()

    return _hbm_comm_call(body, name=name, n_in=n, out_shape=outs, seq_id=seq_id,
                          sem_shapes=[pltpu.SemaphoreType.DMA((n,)), pltpu.SemaphoreType.DMA((n,))])(*views)


def add_sibling_half(g, recv, kind, shard_shape, cidx, name):
    r, c = shard_shape
    hr = r // 2
    gv = _as_halves(g, kind, shard_shape)
    tr = hr if hr <= 512 else (256 if hr % 256 == 0 else hr // 2)
    assert hr % tr == 0

    def body(ci_ref, g_ref, r_ref, h_ref, hb_ref):
        s = g_ref[...].astype(F32) + r_ref[...].astype(F32)
        h_ref[...] = s
        hb_ref[...] = s.astype(BF)

    if kind == "col":
        grid = (hr // tr, N_CHIPS)
        g_spec = pl.BlockSpec((None, tr, c), lambda i, k, ci: (ci[0], i, k))
        o_spec = pl.BlockSpec((tr, c), lambda i, k, ci: (i, k))
    else:
        grid = (hr // tr, N_CHIPS)
        g_spec = pl.BlockSpec((None, None, tr, c), lambda i, k, ci: (k, ci[0], i, 0))
        o_spec = pl.BlockSpec((None, tr, c), lambda i, k, ci: (k, i, 0))
    gs = pltpu.PrefetchScalarGridSpec(num_scalar_prefetch=1, grid=grid, in_specs=[g_spec, o_spec], out_specs=[o_spec, o_spec])
    return _pcall(
        body, name=name, grid_spec=gs,
        out_shape=[jax.ShapeDtypeStruct(recv.shape, F32), jax.ShapeDtypeStruct(recv.shape, BF)],
        compiler_params=_params(("parallel", "parallel")),
    )(cidx, gv, recv)


def exchange_chip_pieces(hbs, kinds, shard_shapes, name, seq_id=None):
    n = len(hbs)
    outs = [jax.ShapeDtypeStruct((N_CHIPS - 1, r // 2, c), BF) for (r, c) in shard_shapes]

    def body(*refs):
        srcs, dsts = refs[:n], refs[n:2 * n]
        send_sems, recv_sems = refs[2 * n:]
        x, y, c = _me()
        if seq_id is not None:
            _handshake([(*_flip(x, y, p), c) for p in range(1, N_CHIPS)])
        cps = []
        for i in range(n):
            cc = shard_shapes[i][1]
            for p in range(1, N_CHIPS):
                px, py = _flip(x, y, p)
                pchip = 2 * px + py
                src = (srcs[i].at[:, pl.ds(pl.multiple_of(pchip * cc, cc), cc)] if kinds[i] == "col" else srcs[i].at[pchip])
                k = i * (N_CHIPS - 1) + p - 1
                cp = pltpu.make_async_remote_copy(src_ref=src, dst_ref=dsts[i].at[p - 1], send_sem=send_sems.at[k],
                                                  recv_sem=recv_sems.at[k], device_id=(px, py, c), device_id_type=MESH)
                cp.start()
                cps.append(cp)
        for cp in cps:
            cp.wait_recv()
        for cp in cps:
            cp.wait_send()

    nk = n * (N_CHIPS - 1)
    return _hbm_comm_call(body, name=name, n_in=n, out_shape=outs, seq_id=seq_id,
                          sem_shapes=[pltpu.SemaphoreType.DMA((nk,)), pltpu.SemaphoreType.DMA((nk,))])(*hbs)


def sum_chip_pieces(h, pieces, kind, shard_shape, chip_core, name):
    r, c = shard_shape
    hr = r // 2
    tr = hr if hr <= 512 else (256 if hr % 256 == 0 else hr // 2)
    assert hr % tr == 0

    def body(ci_ref, h_ref, p_ref, q_ref):
        q_ref[...] = ((h_ref[...] + p_ref[0].astype(F32)) + p_ref[1].astype(F32)) + p_ref[2].astype(F32)

    if kind == "col":
        h_spec = pl.BlockSpec((tr, c), lambda i, ci: (i, ci[0]))
    else:
        h_spec = pl.BlockSpec((None, tr, c), lambda i, ci: (ci[0], i, 0))
    gs = pltpu.PrefetchScalarGridSpec(
        num_scalar_prefetch=1, grid=(hr // tr,),
        in_specs=[h_spec, pl.BlockSpec((N_CHIPS - 1, tr, c), lambda i, ci: (0, i, 0))],
        out_specs=pl.BlockSpec((tr, c), lambda i, ci: (i, 0)))
    return _pcall(body, name=name, grid_spec=gs, out_shape=jax.ShapeDtypeStruct((hr, c), F32),
                  compiler_params=_params(("parallel",)))(chip_core, h, pieces)


def exchange_reduced_halves(qs, name, seq_id):
    n = len(qs)

    def body(*refs):
        srcs, dsts = refs[:n], refs[n:2 * n]
        send_sems, recv_sems = refs[2 * n:]
        x, y, c = _me()
        _handshake([(x, y, 1 - c)])
        cps = []
        for i in range(n):
            cp = pltpu.make_async_remote_copy(src_ref=srcs[i], dst_ref=dsts[i], send_sem=send_sems.at[i], recv_sem=recv_sems.at[i],
                                              device_id=(x, y, 1 - c), device_id_type=MESH)
            cp.start()
            cps.append(cp)
        for cp in cps:
            cp.wait_recv()
        for cp in cps:
            cp.wait_send()

    return _seq_call(body, name=name, n_in=n, out_shape=[jax.ShapeDtypeStruct(q.shape, F32) for q in qs],
                     sem_shapes=[pltpu.SemaphoreType.DMA((n,)), pltpu.SemaphoreType.DMA((n,))], collective_id=seq_id)(*qs)


def _rows128(a):
    return a.reshape(-1, LANES)


def _after(xs, *deps):
    flat = []
    for d in deps:
        flat.extend(d if isinstance(d, (list, tuple)) else [d])
    return list(lax.optimization_barrier((tuple(xs), tuple(flat)))[0])


def _block_diag(w):
    H, d, _ = w.shape
    eye = jnp.eye(H, dtype=w.dtype)
    return jnp.einsum("hde,hg->hdge", w, eye).reshape(H * d, H * d)


def _diag_blocks(g4, H, d):
    nb = g4.shape[0]
    per = LANES // d
    g = g4.reshape(nb, per, d, per, d)
    return jnp.stack([g[:, j, :, j, :] for j in range(per)], axis=1).reshape(H, d, d)


def kernel(x, c, w_mod, b_mod, g_ffn1, w_ffn1_in, w_ffn1_out, g_mix, w_in, conv_w, conv_b, ln_g, ln_b, rnn_conv_w, rnn_conv_b, w_a, b_a, w_i, b_i, lru_lambda, w_out, g_ffn2, w_ffn2_in, w_ffn2_out, w_fmod, b_fmod, g_final, loss_target, m_w_mod, m_b_mod, m_g_ffn1, m_w_ffn1_in, m_w_ffn1_out, m_g_mix, m_w_in, m_conv_w, m_conv_b, m_ln_g, m_ln_b, m_rnn_conv_w, m_rnn_conv_b, m_w_a, m_b_a, m_w_i, m_b_i, m_lru_lambda, m_w_out, m_g_ffn2, m_w_ffn2_in, m_w_ffn2_out, m_w_fmod, m_b_fmod, m_g_final, v_w_mod, v_b_mod, v_g_ffn1, v_w_ffn1_in, v_w_ffn1_out, v_g_mix, v_w_in, v_conv_w, v_conv_b, v_ln_g, v_ln_b, v_rnn_conv_w, v_rnn_conv_b, v_w_a, v_b_a, v_w_i, v_b_i, v_lru_lambda, v_w_out, v_g_ffn2, v_w_ffn2_in, v_w_ffn2_out, v_w_fmod, v_b_fmod, v_g_final):
    S, D = x.shape[1], x.shape[2]
    M = conv_b.shape[1]
    H, HD = w_a.shape[1], w_a.shape[2]
    nb = M // LANES
    ix, iy, ic = lax.axis_index("x"), lax.axis_index("y"), lax.axis_index("c")
    chip = 2 * ix + iy
    dev = 2 * chip + ic
    cidx = jnp.reshape(ic, (1,)).astype(I32)
    chip_core = jnp.stack([chip, ic]).astype(I32)
    xs = x[0]
    tgt = loss_target[0]

    kinds = ["col", "row"]
    w_f1, w_mx, w_f2 = [w_ffn1_in[0], w_ffn1_out[0]], [w_in[0], w_out[0]], [w_ffn2_in[0], w_ffn2_out[0]]
    as_bf = lambda ws: [w.astype(BF) for w in ws]
    shapes_of = lambda ws: [w.shape for w in ws]
    b_f1, b_mx, b_f2 = as_bf(w_f1), as_bf(w_mx), as_bf(w_f2)
    got_f1i = allgather_weights(b_f1[:1], kinds[:1], "gather_ffn1_in", seq_id=9)
    got_f1o = allgather_weights(b_f1[1:], kinds[1:], "gather_ffn1_out", seq_id=13)
    got_mx = allgather_weights(b_mx, kinds, "gather_mix", seq_id=1)
    got_f2 = allgather_weights(b_f2, kinds, "gather_ffn2", seq_id=2)

    c_all =allgather_devices(_rows128(c), "gather_c")[0].reshape(N_DEV, D)
    mod_cols = cond_matmul(c_all, w_mod[0], "mod_proj")
    fmod_cols = cond_matmul(c_all, w_fmod, "fmod_proj")
    convw_pad = jnp.pad(conv_w[0], ((0, 32 - CONV_WIDTH), (0, 0)))
    rnnw_pad = jnp.pad(rnn_conv_w[0], ((0, SUBLANES - RNN_CONV_WIDTH), (0, 0)))
    n_mod, n_fmod = mod_cols.shape[1], fmod_cols.shape[1]
    small = jnp.concatenate([_rows128(mod_cols), _rows128(fmod_cols), convw_pad, rnnw_pad], axis=0)
    small4 = allgather_chips(small, "gather_cond")
    r0 = N_DEV * n_mod // LANES
    r1 = r0 + N_DEV * n_fmod // LANES
    mod_all = small4[:, :r0].reshape(N_CHIPS, N_DEV, n_mod)
    fmod_all = small4[:, r0:r1].reshape(N_CHIPS, N_DEV, n_fmod)
    convw4 = small4[:, r1:r1 + 32]
    rnnw4 = small4[:, r1 + 32:r1 + 32 + SUBLANES]
    mod_row = lax.dynamic_index_in_dim(mod_all, dev, axis=1, keepdims=False).reshape(1, N_CHIPS * n_mod) + b_mod
    fmod_row = lax.dynamic_index_in_dim(fmod_all, dev, axis=1, keepdims=False).reshape(1, N_CHIPS * n_fmod) + b_fmod[None, :]
    vecs = jnp.concatenate([mod_row.reshape(9, D), fmod_row.reshape(2, D), g_ffn1, g_mix, g_ffn2, g_final[None, :],
                            jnp.zeros((1, D), F32)], axis=0)
    lnv = jnp.concatenate([ln_g, ln_b, jnp.zeros((SUBLANES - 2, M), F32)], axis=0)
    bda = _block_diag(w_a[0]).astype(BF)
    bdi = _block_diag(w_i[0]).astype(BF)

    def reduce_add(gs, recv, ws, tag, kinds_=kinds):
        pairs = [add_sibling_half(g, r_, k, w.shape, cidx, f"add_sibling_{tag}{j}")
                 for j, (g, r_, k, w) in enumerate(zip(gs, recv, kinds_, ws))]
        return [p[0] for p in pairs], [p[1] for p in pairs]

    def reduce_sum(hs_, recv, ws, tag, kinds_=kinds):
        return [sum_chip_pieces(h_, p_, k, w.shape, chip_core, f"sum_chips_{tag}{j}")
                for j, (h_, p_, k, w) in enumerate(zip(hs_, recv, kinds_, ws))]

    rows1 = (R_SH1, R_SC1, R_GT1, R_G1)
    rows3 = (R_SH3, R_SC3, R_GT3, R_G3)
    (wi1,) = place_local_shards(got_f1i, b_f1[:1], kinds[:1], "place_ffn1_in")
    g1s, u1s, a1s = ffn_fwd_in(xs, vecs, wi1, rows1, "ffn1_fwd_in")
    (wo1,) = place_local_shards(_after(got_f1o, a1s), b_f1[1:], kinds[1:], "place_ffn1_out")
    x1, y1 = ffn_fwd_out(a1s, xs, vecs, wo1, rows1, "ffn1_fwd_out")
    win, wout = place_local_shards(_after(got_mx, x1), b_mx, kinds, "place_mix")
    proj = norm_matmul(x1, vecs, win, (R_SH2, R_SC2, R_G2), "mix_in_proj")
    cq = conv_fwd(proj, convw4, conv_b, "conv_fwd")
    xr, ra, ii, hh = rnn_fwd(proj, rnnw4, rnn_conv_b, bda, bdi, b_a, b_i, lru_lambda, "rnn_fwd")
    x2, ym, ycat = mix_out(cq, proj, hh, x1, vecs, lnv, wout, "mix_out")
    wi2, wo2 = place_local_shards(_after(got_f2, x2), b_f2, kinds, "place_ffn2")
    dx3, g2s, u2s, y2, vgf = ffn_fwd(x2, vecs, wi2, wo2, rows3, "ffn2_fwd", final_tgt=tgt)

    Fd = wo1.shape[0]
    tk = min(2048, S)
    dx2, act2, dg2, du2, h3b, dy2b, vg3 = ffn_bwd(dx3, x2, vecs, g2s, u2s, y2, wi2, wo2, rows3, "ffn2_bwd")
    gwo2 = matmul(act2, dy2b, "tn", tm=Fd // 2, tn=D, tk=tk, out_dtype=BF, name="ffn2_dwo")
    gwi2 = matmul(h3b, dg2, "tn", tm=D, tn=Fd // 2, tk=tk, out_dtype=BF, name="ffn2_dwg", out_cols=2 * Fd)
    gwi2 = matmul(h3b, du2, "tn", tm=D, tn=Fd // 2, tk=tk, out_dtype=BF, name="ffn2_dwu", out_cols=2 * Fd, col_off=Fd, prev=gwi2)
    recv1_f2 = exchange_sibling_halves([gwi2, gwo2], kinds, shapes_of(w_f2), "reduce1_ffn2", seq_id=3)
    dcq, dhout, duy, dymb, vgd, vgm = mix_out_bwd(dx2, ym, vecs, wout, cq, lnv, proj, hh, "mix_out_bwd")
    gwout = matmul(ycat, dymb, "tn", tm=2 * M, tn=D, tk=tk, out_dtype=BF, name="mix_dwout")
    recv1_f2 = _after(recv1_f2, gwout)
    h_f2, hb_f2 = reduce_add([gwi2, gwo2], recv1_f2, w_f2, "ffn2_")
    recv2_f2 = exchange_chip_pieces(hb_f2, kinds, shapes_of(w_f2), "reduce2_ffn2", seq_id=4)
    duv, dug, dconvw4, dconvb = conv_bwd(_after([dcq], hb_f2)[0], proj, convw4, "conv_bwd")
    dux, dwa4, dwi4, drnnw4, rvec = rnn_bwd(dhout, hh, xr, ra, ii, proj, rnnw4, bda, bdi, lru_lambda, "rnn_bwd")
    dx1, h2b, dpb, vg2 = mix_in_bwd((duv, dug, dux, duy), x1, dx2, vecs, win, "mix_in_bwd")
    gwin = matmul(h2b, dpb, "tn", tm=D, tn=1024, tk=tk, out_dtype=BF, name="mix_dwin")
    recv1_mx = exchange_sibling_halves([gwin, gwout], kinds, shapes_of(w_mx), "reduce1_mix", seq_id=5)
    q_f2 = reduce_sum(_after(h_f2, gwin), recv2_f2, w_f2, "ffn2_")
    r_f2 = exchange_reduced_halves(q_f2, "reduce3_ffn2", seq_id=14)
    h_mx, hb_mx = reduce_add([gwin, gwout], _after(recv1_mx, q_f2), w_mx, "mix_")
    recv2_mx = exchange_chip_pieces(hb_mx, kinds, shapes_of(w_mx), "reduce2_mix", seq_id=6)
    dx0, act1, dg1, du1, h1b, dy1b, vg1 = ffn_bwd(_after([dx1], hb_mx)[0], xs, vecs, g1s, u1s, y1, wi1, wo1, rows1, "ffn1_bwd")
    dmod_row = jnp.concatenate([vg1[1:3], vg1[0:1], vg2[0:2], vgd[0:1], vg3[1:3], vg3[0:1]], axis=0)
    gains = jnp.concatenate([vg1[3:4], vg2[2:3], vg3[3:4], vgf[2:4]], axis=0)
    mvecs = jnp.concatenate([dconvb, vgm[0:2], rvec[0:4], jnp.zeros((1, M), F32)], axis=0)
    parts = [_rows128(dmod_row), _rows128(vgf[0:2]), _rows128(gains), _rows128(mvecs),
             _rows128(dconvw4), _rows128(drnnw4), _rows128(_diag_blocks(dwa4, H, HD)), _rows128(_diag_blocks(dwi4, H, HD))]
    sizes = [p.shape[0] for p in parts]
    packed = jnp.concatenate(parts, axis=0)
    gathered = allgather_devices_hbm(packed, "gather_small", seq_id=10)

    gwo1 = matmul(_after([act1], recv2_mx, packed)[0], dy1b, "tn", tm=Fd // 2, tn=D, tk=tk, out_dtype=BF, name="ffn1_dwo")
    w_f1o, w_f1i = w_f1[1:], w_f1[:1]
    recv1_f1o = exchange_sibling_halves([gwo1], ["row"], shapes_of(w_f1o), "reduce1_ffn1_out", seq_id=7)
    q_mx = reduce_sum(_after(h_mx, gwo1), recv2_mx, w_mx, "mix_")
    r_mx = exchange_reduced_halves(q_mx, "reduce3_mix", seq_id=15)
    gwi1 = matmul(_after([h1b], q_mx)[0], dg1, "tn", tm=D, tn=Fd // 2, tk=tk, out_dtype=BF, name="ffn1_dwg", out_cols=2 * Fd)
    h_f1o, hb_f1o = reduce_add([gwo1], _after(recv1_f1o, gwi1), w_f1o, "ffn1_out", ["row"])
    recv2_f1o = exchange_chip_pieces(hb_f1o, ["row"], shapes_of(w_f1o), "reduce2_ffn1_out", seq_id=11)
    gwi1 = matmul(h1b, _after([du1], hb_f1o, gathered)[0], "tn", tm=D, tn=Fd // 2, tk=tk, out_dtype=BF, name="ffn1_dwu", out_cols=2 * Fd,
                  col_off=Fd, prev=gwi1)
    recv1_f1i = exchange_sibling_halves([gwi1], ["col"], shapes_of(w_f1i), "reduce1_ffn1_in", seq_id=12)
    q_f1o = reduce_sum(_after(h_f1o, gwi1), recv2_f1o, w_f1o, "ffn1_out", ["row"])
    r_f1o = exchange_reduced_halves(q_f1o, "reduce3_ffn1_out", seq_id=16)
    summed = sum_slots(gathered, "sum_small")
    offs = [0]
    for s in sizes:
        offs.append(offs[-1] + s)
    seg = lambda k: summed[offs[k]:offs[k + 1]]
    g_b_mod = seg(0).reshape(1, 9 * D)
    g_b_fmod = seg(1).reshape(1, 2 * D)
    gsum = seg(2).reshape(5, D)
    loss = (0.5 / D) * jnp.sum(gsum[4])
    msum = seg(3).reshape(SUBLANES, M)
    g_conv_w = lax.dynamic_index_in_dim(seg(4).reshape(nb, 32, LANES), chip, axis=0, keepdims=False)[:CONV_WIDTH]
    g_rnn_w = lax.dynamic_index_in_dim(seg(5).reshape(nb, SUBLANES, LANES), chip, axis=0, keepdims=False)[:RNN_CONV_WIDTH]
    g_w_a = seg(6).reshape(H, HD, HD)
    g_w_i = seg(7).reshape(H, HD, HD)
    dmod_all = gathered[:, offs[0]:offs[1]].reshape(N_DEV, 9 * D)
    dfmod_all = gathered[:, offs[1]:offs[2]].reshape(N_DEV, 2 * D)
    dmod_cols = lax.dynamic_slice_in_dim(dmod_all, chip * n_mod, n_mod, axis=1)
    dfmod_cols = lax.dynamic_slice_in_dim(dfmod_all, chip * n_fmod, n_fmod, axis=1)

    h_f1i, hb_f1i = reduce_add([gwi1], _after(recv1_f1i, q_f1o), w_f1i, "ffn1_in", ["col"])
    recv2_f1i = exchange_chip_pieces(hb_f1i, ["col"], shapes_of(w_f1i), "reduce2_ffn1_in", seq_id=8)
    dmod_cols, dfmod_cols = _after([dmod_cols, dfmod_cols], hb_f1i)
    g_w_mod, d_w_mod, nm_w_mod, nv_w_mod = adam_cond(c_all, dmod_cols, w_mod[0], m_w_mod[0], v_w_mod[0], "adam_w_mod")
    g_w_fmod, d_w_fmod, nm_w_fmod, nv_w_fmod = adam_cond(c_all, dfmod_cols, w_fmod, m_w_fmod, v_w_fmod, "adam_w_fmod")

    def adam_group(ws, qs_, rs_, ms, vs, tags, after):
        qs_ = _after(list(qs_), *after) if after else list(qs_)
        return [adam_big(w, q_, r_, m, v, cidx, "adam_" + t) for w, q_, r_, m, v, t in zip(ws, qs_, rs_, ms, vs, tags)]

    ad_f2 = adam_group(w_f2, q_f2, r_f2, [m_w_ffn2_in[0], m_w_ffn2_out[0]], [v_w_ffn2_in[0], v_w_ffn2_out[0]],
                       ["ffn2_in", "ffn2_out"], [hb_f1i])
    ad_mx = adam_group(w_mx, q_mx, r_mx, [m_w_in[0], m_w_out[0]], [v_w_in[0], v_w_out[0]], ["w_in", "w_out"], [hb_f1i])
    ad_f1o = adam_group(w_f1o, q_f1o, r_f1o, [m_w_ffn1_out[0]], [v_w_ffn1_out[0]], ["ffn1_out"], [hb_f1i])
    q_f1i = reduce_sum(_after(h_f1i, ad_f2[0][0], ad_f2[1][0], ad_mx[0][0], ad_mx[1][0], ad_f1o[0][0], g_w_mod, g_w_fmod),
                       recv2_f1i, w_f1i, "ffn1_in", ["col"])
    r_f1i = exchange_reduced_halves(q_f1i, "reduce3_ffn1_in", seq_id=17)
    ad_f1i = adam_group(w_f1i, q_f1i, r_f1i, [m_w_ffn1_in[0]], [v_w_ffn1_in[0]], ["ffn1_in"], [])
    big_out = ad_f1i + ad_f1o + ad_mx + ad_f2

    flat2 = lambda a: a.reshape(-1, a.shape[-1])
    small_names = ["b_mod", "g_ffn1", "g_mix", "conv_w", "conv_b", "ln_g", "ln_b", "rnn_conv_w", "rnn_conv_b", "w_a", "b_a",
                   "w_i", "b_i", "lru_lambda", "g_ffn2", "b_fmod", "g_final"]
    small_w = [b_mod, g_ffn1, g_mix, conv_w, conv_b, ln_g, ln_b, rnn_conv_w, rnn_conv_b, w_a, b_a, w_i, b_i, lru_lambda,
               g_ffn2, b_fmod, g_final]
    small_m = [m_b_mod, m_g_ffn1, m_g_mix, m_conv_w, m_conv_b, m_ln_g, m_ln_b, m_rnn_conv_w, m_rnn_conv_b, m_w_a, m_b_a,
               m_w_i, m_b_i, m_lru_lambda, m_g_ffn2, m_b_fmod, m_g_final]
    small_v = [v_b_mod, v_g_ffn1, v_g_mix, v_conv_w, v_conv_b, v_ln_g, v_ln_b, v_rnn_conv_w, v_rnn_conv_b, v_w_a, v_b_a,
               v_w_i, v_b_i, v_lru_lambda, v_g_ffn2, v_b_fmod, v_g_final]
    small_g = [g_b_mod, gsum[0:1], gsum[1:2], g_conv_w, msum[0:1], msum[1:2], msum[2:3], g_rnn_w, msum[3:4], g_w_a, msum[4:5],
               g_w_i, msum[5:6], msum[6:7], gsum[2:3], g_b_fmod, gsum[3:4]]
    small_g = [g.reshape(w.shape) for g, w in zip(small_g, small_w)]
    two_d = lambda a: a.reshape(1, -1) if a.ndim == 1 else flat2(a)
    sd, sm, sv = adam_small([two_d(a) for a in small_w], [two_d(a) for a in small_g], [two_d(a) for a in small_m],
                            [two_d(a) for a in small_v], "adam_small")
    small = {}
    for k, nm in enumerate(small_names):
        shp = small_w[k].shape
        small[nm] = (small_g[k], sd[k].reshape(shp), sm[k].reshape(shp), sv[k].reshape(shp))

    big = {"w_mod": tuple(a[None] for a in (g_w_mod, d_w_mod, nm_w_mod, nv_w_mod)),
           "w_fmod": (g_w_fmod, d_w_fmod, nm_w_fmod, nv_w_fmod)}
    for nm, res in zip(["w_ffn1_in", "w_ffn1_out", "w_in", "w_out", "w_ffn2_in", "w_ffn2_out"], big_out):
        big[nm] = tuple(a[None] for a in res)
    order = ["w_mod", "b_mod", "g_ffn1", "w_ffn1_in", "w_ffn1_out", "g_mix", "w_in", "conv_w", "conv_b", "ln_g", "ln_b",
             "rnn_conv_w", "rnn_conv_b", "w_a", "b_a", "w_i", "b_i", "lru_lambda", "w_out", "g_ffn2", "w_ffn2_in",
             "w_ffn2_out", "w_fmod", "b_fmod", "g_final"]
    table = {**small, **big}
    outs = [loss, dx0[None]]
    for kind_ in range(4):
        outs.extend(table[nm][kind_] for nm in order)
    return tuple(outs)
```

```python
import functools

import jax
import jax.numpy as jnp
from jax import lax
from jax.experimental import pallas as pl
from jax.experimental.pallas import tpu as pltpu
from jax.experimental.pallas import tpu_sc as plsc

F32 = jnp.float32
BF = jnp.bfloat16
I32 = jnp.int32
MESH = pl.DeviceIdType.MESH

EPS = 1e-6
RG_C = 8.0
MACARON_W = 0.5
CONV_WIDTH = 31
RNN_CONV_WIDTH = 4
ADAM_LR = 0.001
ADAM_B1 = 0.9
ADAM_B2 = 0.999
ADAM_EPS = 1e-08
ADAM_WD = 0.01
ADAM_STEP = 10

LANES = 128
SUBLANES = 8
VMEM_LIMIT = 62 * 1024 * 1024
N_CHIPS = 4
N_DEV = 8

R_SH1, R_SC1, R_GT1, R_SH2, R_SC2, R_GT2, R_SH3, R_SC3, R_GT3, R_FSH, R_FSC, R_G1, R_G2, R_G3, R_GF = range(15)

CONTRACT_LAST = (((1,), (1,)), ((), ()))
CONTRACT_FIRST = (((0,), (0,)), ((), ()))


def _pcall(body, **kw):
    return pl.pallas_call(body, **kw)


def _params(sem=None, vmem=VMEM_LIMIT):
    if sem is None:
        return pltpu.CompilerParams(vmem_limit_bytes=vmem)
    return pltpu.CompilerParams(dimension_semantics=sem, vmem_limit_bytes=vmem)


def _row(ref, r):
    return ref[r:r + 1, :]


def _sigmoid(x):
    return 1.0 / (1.0 + jnp.exp(-x))


def _colsum(x):
    return jnp.sum(x, axis=0, keepdims=True)


def _rowmean(x):
    return jnp.mean(x, axis=-1, keepdims=True)


def matmul(a, b, mode, *, tm, tn, tk, name, out_dtype=F32, out_cols=None, col_off=0, prev=None):
    if mode == "nn":
        (M, K), (K2, N) = a.shape, b.shape
    elif mode == "nt":
        (M, K), (N, K2) = a.shape, b.shape
    else:
        (K, M), (K2, N) = a.shape, b.shape
    assert K == K2 and M % tm == 0 and N % tn == 0 and K % tk == 0 and col_off % tn == 0
    nk = K // tk
    out_cols = N if out_cols is None else out_cols
    off = col_off // tn

    def body(*refs):
        if prev is None:
            a_ref, b_ref, o_ref, acc = refs
        else:
            a_ref, b_ref, _, o_ref, acc = refs
        k = pl.program_id(2)
        av = a_ref[...].astype(BF)
        bv = b_ref[...].astype(BF)
        if mode == "nn":
            part = jnp.dot(av, bv, preferred_element_type=F32)
        elif mode == "nt":
            part = lax.dot_general(av, bv, CONTRACT_LAST, preferred_element_type=F32)
        else:
            part = lax.dot_general(av, bv, CONTRACT_FIRST, preferred_element_type=F32)
        if nk == 1:
            o_ref[...] = part.astype(out_dtype)
            return

        @pl.when(k == 0)
        def _():
            acc[...] = part

        @pl.when((k > 0) & (k < nk - 1))
        def _():
            acc[...] += part

        @pl.when(k == nk - 1)
        def _():
            o_ref[...] = (acc[...] + part).astype(out_dtype)

    if mode == "nn":
        a_spec = pl.BlockSpec((tm, tk), lambda m, n, k: (m, k))
        b_spec = pl.BlockSpec((tk, tn), lambda m, n, k: (k, n))
    elif mode == "nt":
        a_spec = pl.BlockSpec((tm, tk), lambda m, n, k: (m, k))
        b_spec = pl.BlockSpec((tn, tk), lambda m, n, k: (n, k))
    else:
        a_spec = pl.BlockSpec((tk, tm), lambda m, n, k: (k, m))
        b_spec = pl.BlockSpec((tk, tn), lambda m, n, k: (k, n))
    in_specs = [a_spec, b_spec]
    args = [a, b]
    aliases = {}
    if prev is not None:
        in_specs.append(pl.BlockSpec(memory_space=pl.ANY))
        args.append(prev)
        aliases = {2: 0}
    return _pcall(
        body, name=name, grid=(M // tm, N // tn, nk), in_specs=in_specs,
        out_specs=pl.BlockSpec((tm, tn), lambda m, n, k: (m, n + off)),
        out_shape=jax.ShapeDtypeStruct((M, out_cols), out_dtype),
        scratch_shapes=[pltpu.VMEM((tm, tn), F32)], input_output_aliases=aliases,
        compiler_params=_params(("parallel", "parallel", "arbitrary")),
    )(*args)


def cond_matmul(c_all, w, name):
    B, K = c_all.shape
    N = w.shape[1]
    tn = 256
    assert N % tn == 0

    def body(c_ref, w_ref, o_ref):
        cv = c_ref[...]
        ca = cv * _sigmoid(cv)
        o_ref[...] = jnp.dot(ca.astype(BF), w_ref[...].astype(BF), preferred_element_type=F32)

    return _pcall(
        body, name=name, grid=(N // tn,),
        in_specs=[pl.BlockSpec((B, K), lambda n: (0, 0)), pl.BlockSpec((K, tn), lambda n: (0, n))],
        out_specs=pl.BlockSpec((B, tn), lambda n: (0, n)),
        out_shape=jax.ShapeDtypeStruct((B, N), F32), compiler_params=_params(("parallel",)),
    )(c_all, w)


FFN_FWD_TS = 512
FFN_BWD_TS = 256


def _resident(shape, index_map):
    return pl.BlockSpec(shape, index_map, pipeline_mode=pl.Buffered(1))


def _final_norm_loss_grad(xv, t, v_ref, vg_ref):
    D = xv.shape[-1]
    r = lax.rsqrt(_rowmean(xv * xv) + EPS)
    n = xv * r
    g = _row(v_ref, R_GF)
    sc1 = 1.0 + _row(v_ref, R_FSC)
    gsc = g * sc1
    e = n * gsc + _row(v_ref, R_FSH) - t
    vg_ref[3:4, :] += _colsum(e * e)
    dout = e * (1.0 / D)
    dn_ = dout * n
    vg_ref[0:1, :] += _colsum(dout)
    vg_ref[1:2, :] += _colsum(dn_) * g
    vg_ref[2:3, :] += _colsum(dn_) * sc1
    dn = dout * gsc
    return r * (dn - n * _rowmean(dn * n))


def ffn_fwd(x, vecs, wi, wo, rows, name, final_tgt=None):
    r_sh, r_sc, r_gt, r_g = rows
    S, D = x.shape
    Fd = wo.shape[0]
    ts = min(FFN_FWD_TS, S)
    with_final = final_tgt is not None

    def body(*refs):
        if with_final:
            x_ref, v_ref, wg_ref, wu_ref, wo_ref, t_ref, xo_ref, g_ref, u_ref, y_ref, vg_ref = refs
        else:
            x_ref, v_ref, wg_ref, wu_ref, wo_ref, xo_ref, g_ref, u_ref, y_ref = refs
        xv = x_ref[...]
        r = lax.rsqrt(_rowmean(xv * xv) + EPS)
        gs = _row(v_ref, r_g) * (1.0 + _row(v_ref, r_sc))
        hb = (xv * r * gs + _row(v_ref, r_sh)).astype(BF)
        G = jnp.dot(hb, wg_ref[...], preferred_element_type=F32)
        U = jnp.dot(hb, wu_ref[...], preferred_element_type=F32)
        g_ref[...] = G.astype(BF)
        u_ref[...] = U.astype(BF)
        act = (G * _sigmoid(G) * U).astype(BF)
        Y = jnp.dot(act, wo_ref[...], preferred_element_type=F32)
        y_ref[...] = Y.astype(BF)
        xo = xv + (MACARON_W * _row(v_ref, r_gt)) * Y
        if with_final:
            @pl.when(pl.program_id(0) == 0)
            def _():
                vg_ref[...] = jnp.zeros_like(vg_ref)

            xo_ref[...] = _final_norm_loss_grad(xo, t_ref[...], v_ref, vg_ref)
        else:
            xo_ref[...] = xo

    tok = pl.BlockSpec((ts, D), lambda i: (i, 0))
    hid = pl.BlockSpec((ts, Fd), lambda i: (i, 0))
    in_specs = [tok, pl.BlockSpec(vecs.shape, lambda i: (0, 0)), _resident((D, Fd), lambda i: (0, 0)),
                _resident((D, Fd), lambda i: (0, 1)), _resident((Fd, D), lambda i: (0, 0))]
    out_specs = [tok, hid, hid, tok]
    out_shape = [jax.ShapeDtypeStruct((S, D), F32), jax.ShapeDtypeStruct((S, Fd), BF),
                 jax.ShapeDtypeStruct((S, Fd), BF), jax.ShapeDtypeStruct((S, D), BF)]
    args = [x, vecs, wi, wi, wo]
    if with_final:
        in_specs.append(tok)
        args.append(final_tgt)
        out_specs.append(pl.BlockSpec((SUBLANES, D), lambda i: (0, 0)))
        out_shape.append(jax.ShapeDtypeStruct((SUBLANES, D), F32))
    return _pcall(body, name=name, grid=(S // ts,), in_specs=in_specs, out_specs=out_specs, out_shape=out_shape,
                  compiler_params=_params(("arbitrary",)))(*args)


def ffn_fwd_in(x, vecs, wi, rows, name):
    r_sh, r_sc, r_gt, r_g = rows
    S, D = x.shape
    Fd = wi.shape[1] // 2
    ts = min(FFN_FWD_TS, S)

    def body(x_ref, v_ref, wg_ref, wu_ref, g_ref, u_ref, a_ref):
        xv = x_ref[...]
        r = lax.rsqrt(_rowmean(xv * xv) + EPS)
        gs = _row(v_ref, r_g) * (1.0 + _row(v_ref, r_sc))
        hb = (xv * r * gs + _row(v_ref, r_sh)).astype(BF)
        G = jnp.dot(hb, wg_ref[...], preferred_element_type=F32)
        U = jnp.dot(hb, wu_ref[...], preferred_element_type=F32)
        g_ref[...] = G.astype(BF)
        u_ref[...] = U.astype(BF)
        a_ref[...] = (G * _sigmoid(G) * U).astype(BF)

    hid = pl.BlockSpec((ts, Fd), lambda i: (i, 0))
    return _pcall(
        body, name=name, grid=(S // ts,),
        in_specs=[pl.BlockSpec((ts, D), lambda i: (i, 0)), pl.BlockSpec(vecs.shape, lambda i: (0, 0)),
                  _resident((D, Fd), lambda i: (0, 0)), _resident((D, Fd), lambda i: (0, 1))],
        out_specs=[hid, hid, hid], out_shape=[jax.ShapeDtypeStruct((S, Fd), BF)] * 3,
        compiler_params=_params(("arbitrary",)),
    )(x, vecs, wi, wi)


def ffn_fwd_out(act, x, vecs, wo, rows, name):
    r_sh, r_sc, r_gt, r_g = rows
    S, D = x.shape
    Fd = wo.shape[0]
    ts = min(FFN_FWD_TS, S)

    def body(a_ref, x_ref, v_ref, wo_ref, xo_ref, y_ref):
        Y = jnp.dot(a_ref[...], wo_ref[...], preferred_element_type=F32)
        y_ref[...] = Y.astype(BF)
        xo_ref[...] = x_ref[...] + (MACARON_W * _row(v_ref, r_gt)) * Y

    tok = pl.BlockSpec((ts, D), lambda i: (i, 0))
    return _pcall(
        body, name=name, grid=(S // ts,),
        in_specs=[pl.BlockSpec((ts, Fd), lambda i: (i, 0)), tok, pl.BlockSpec(vecs.shape, lambda i: (0, 0)),
                  _resident((Fd, D), lambda i: (0, 0))],
        out_specs=[tok, tok], out_shape=[jax.ShapeDtypeStruct((S, D), F32), jax.ShapeDtypeStruct((S, D), BF)],
        compiler_params=_params(("arbitrary",)),
    )(act, x, vecs, wo)


def ffn_bwd(dxo, x, vecs, gs_, us_, y, wi, wo, rows, name):
    r_sh, r_sc, r_gt, r_g = rows
    S, D = x.shape
    Fd = wo.shape[0]
    ts = min(FFN_BWD_TS, S)

    def body(dxo_ref, x_ref, v_ref, g_ref, u_ref, y_ref, wg_ref, wu_ref, wo_ref,
             dx_ref, act_ref, dg_ref, du_ref, hb_ref, dyb_ref, vg_ref):
        @pl.when(pl.program_id(0) == 0)
        def _():
            vg_ref[...] = jnp.zeros_like(vg_ref)

        dxo_v = dxo_ref[...]
        dyb = ((MACARON_W * _row(v_ref, r_gt)) * dxo_v).astype(BF)
        dyb_ref[...] = dyb
        vg_ref[0:1, :] += MACARON_W * _colsum(dxo_v * y_ref[...].astype(F32))
        dA = lax.dot_general(dyb, wo_ref[...], CONTRACT_LAST, preferred_element_type=F32)
        G = g_ref[...].astype(F32)
        U = u_ref[...].astype(F32)
        sg = _sigmoid(G)
        sl = G * sg
        dU = (dA * sl).astype(BF)
        dG = (dA * U * (sg * (1.0 + G * (1.0 - sg)))).astype(BF)
        act_ref[...] = (sl * U).astype(BF)
        dg_ref[...] = dG
        du_ref[...] = dU
        dh = (lax.dot_general(dG, wg_ref[...], CONTRACT_LAST, preferred_element_type=F32)
              + lax.dot_general(dU, wu_ref[...], CONTRACT_LAST, preferred_element_type=F32))
        xv = x_ref[...]
        r = lax.rsqrt(_rowmean(xv * xv) + EPS)
        n = xv * r
        g = _row(v_ref, r_g)
        sc1 = 1.0 + _row(v_ref, r_sc)
        gsc = g * sc1
        hb_ref[...] = (n * gsc + _row(v_ref, r_sh)).astype(BF)
        dhn = dh * n
        vg_ref[1:2, :] += _colsum(dh)
        vg_ref[2:3, :] += _colsum(dhn) * g
        vg_ref[3:4, :] += _colsum(dhn) * sc1
        dn = dh * gsc
        dx_ref[...] = dxo_v + r * (dn - n * _rowmean(dn * n))

    tok = pl.BlockSpec((ts, D), lambda i: (i, 0))
    hid = pl.BlockSpec((ts, Fd), lambda i: (i, 0))
    return _pcall(
        body, name=name, grid=(S // ts,),
        in_specs=[tok, tok, pl.BlockSpec(vecs.shape, lambda i: (0, 0)), hid, hid, tok, _resident((D, Fd), lambda i: (0, 0)),
                  _resident((D, Fd), lambda i: (0, 1)), _resident((Fd, D), lambda i: (0, 0))],
        out_specs=[tok, hid, hid, hid, tok, tok, pl.BlockSpec((SUBLANES, D), lambda i: (0, 0))],
        out_shape=[jax.ShapeDtypeStruct((S, D), F32), jax.ShapeDtypeStruct((S, Fd), BF),
                   jax.ShapeDtypeStruct((S, Fd), BF), jax.ShapeDtypeStruct((S, Fd), BF),
                   jax.ShapeDtypeStruct((S, D), BF), jax.ShapeDtypeStruct((S, D), BF),
                   jax.ShapeDtypeStruct((SUBLANES, D), F32)],
        compiler_params=_params(("arbitrary",)),
    )(dxo, x, vecs, gs_, us_, y, wi, wi, wo)


def norm_matmul(x, vecs, w, rows, name):
    r_sh, r_sc, r_g = rows
    S, D = x.shape
    N = w.shape[1]
    ts = min(512, S)

    def body(x_ref, v_ref, w_ref, o_ref):
        xv = x_ref[...]
        r = lax.rsqrt(_rowmean(xv * xv) + EPS)
        gs = _row(v_ref, r_g) * (1.0 + _row(v_ref, r_sc))
        hb = (xv * r * gs + _row(v_ref, r_sh)).astype(BF)
        o_ref[...] = jnp.dot(hb, w_ref[...], preferred_element_type=F32).astype(BF)

    return _pcall(
        body, name=name, grid=(S // ts,),
        in_specs=[pl.BlockSpec((ts, D), lambda i: (i, 0)), pl.BlockSpec(vecs.shape, lambda i: (0, 0)),
                  _resident((D, N), lambda i: (0, 0))],
        out_specs=pl.BlockSpec((ts, N), lambda i: (i, 0)),
        out_shape=jax.ShapeDtypeStruct((S, N), BF),
        compiler_params=_params(("arbitrary",)),
    )(x, vecs, w)


SEQ_TT = 256
SCAN_SEGMENTS = 4
CONV_PAD = 32


def conv_fwd(proj, convw4, conv_b, name):
    S = proj.shape[0]
    M = conv_b.shape[1]
    nb = M // LANES
    tt = min(SEQ_TT, S)

    def body(uv_ref, ug_ref, w_ref, b_ref, cq_ref, qp):
        qp[0:CONV_PAD, :] = jnp.zeros((CONV_PAD, LANES), F32)

        def step(t, carry):
            base = pl.multiple_of(t * tt, tt)
            qp[pl.ds(base + CONV_PAD, tt), :] = uv_ref[pl.ds(base, tt), :].astype(F32) * _sigmoid(ug_ref[pl.ds(base, tt), :].astype(F32))
            acc = jnp.broadcast_to(b_ref[...], (tt, LANES))
            for k in range(CONV_WIDTH):
                acc = acc + w_ref[k:k + 1, :] * qp[pl.ds(base + (CONV_PAD - CONV_WIDTH + 1) + k, tt), :]
            cq_ref[pl.ds(base, tt), :] = acc
            return carry

        lax.fori_loop(0, S // tt, step, 0)

    return _pcall(
        body, name=name, grid=(nb,),
        in_specs=[pl.BlockSpec((S, LANES), lambda c: (0, c)), pl.BlockSpec((S, LANES), lambda c: (0, c + nb)),
                  pl.BlockSpec((None, 32, LANES), lambda c: (c, 0, 0)), pl.BlockSpec((1, LANES), lambda c: (0, c))],
        out_specs=pl.BlockSpec((S, LANES), lambda c: (0, c)),
        out_shape=jax.ShapeDtypeStruct((S, M), F32),
        scratch_shapes=[pltpu.VMEM((S + CONV_PAD, LANES), F32)],
        compiler_params=_params(("arbitrary",)),
    )(proj, proj, convw4, conv_b)


def conv_bwd(dcq, proj, convw4, name):
    S, M = dcq.shape
    nb = M // LANES
    tt = min(SEQ_TT, S)
    off = CONV_PAD - CONV_WIDTH + 1

    def body(dcq_ref, uv_ref, ug_ref, w_ref, duv_ref, dug_ref, dw_ref, db_ref, qp, dp, dw8, db8):
        qp[0:CONV_PAD, :] = jnp.zeros((CONV_PAD, LANES), F32)
        dp[S:S + CONV_PAD, :] = jnp.zeros((CONV_PAD, LANES), F32)
        dw8[...] = jnp.zeros_like(dw8)
        db8[...] = jnp.zeros_like(db8)

        def fill(t, carry):
            base = pl.multiple_of(t * tt, tt)
            qp[pl.ds(base + CONV_PAD, tt), :] = uv_ref[pl.ds(base, tt), :].astype(F32) * _sigmoid(ug_ref[pl.ds(base, tt), :].astype(F32))
            dp[pl.ds(base, tt), :] = dcq_ref[pl.ds(base, tt), :].astype(F32)
            return carry

        lax.fori_loop(0, S // tt, fill, 0)

        def step(t, carry):
            base = pl.multiple_of(t * tt, tt)
            d_t = dcq_ref[pl.ds(base, tt), :].astype(F32)
            db8[...] += d_t.reshape(tt // SUBLANES, SUBLANES, LANES).sum(axis=0)
            dq = jnp.zeros((tt, LANES), F32)
            for k in range(CONV_WIDTH):
                prod = d_t * qp[pl.ds(base + off + k, tt), :]
                dw8[k] += prod.reshape(tt // SUBLANES, SUBLANES, LANES).sum(axis=0)
                dq = dq + w_ref[k:k + 1, :] * dp[pl.ds(base + (CONV_WIDTH - 1) - k, tt), :]
            uv = uv_ref[pl.ds(base, tt), :].astype(F32)
            sg = _sigmoid(ug_ref[pl.ds(base, tt), :].astype(F32))
            duv_ref[pl.ds(base, tt), :] = (dq * sg).astype(BF)
            dug_ref[pl.ds(base, tt), :] = (dq * uv * sg * (1.0 - sg)).astype(BF)
            return carry

        lax.fori_loop(0, S // tt, step, 0)
        dw_ref[...] = jnp.zeros_like(dw_ref)
        for k in range(CONV_WIDTH):
            dw_ref[k:k + 1, :] = _colsum(dw8[k])
        db_ref[...] = _colsum(db8[...])

    col = lambda o: pl.BlockSpec((S, LANES), lambda c: (0, c + o))
    return _pcall(
        body, name=name, grid=(nb,),
        in_specs=[col(0), col(0), col(nb), pl.BlockSpec((None, 32, LANES), lambda c: (c, 0, 0))],
        out_specs=[col(0), col(0), pl.BlockSpec((None, 32, LANES), lambda c: (c, 0, 0)),
                   pl.BlockSpec((1, LANES), lambda c: (0, c))],
        out_shape=[jax.ShapeDtypeStruct((S, M), BF), jax.ShapeDtypeStruct((S, M), BF),
                   jax.ShapeDtypeStruct((nb, 32, LANES), F32), jax.ShapeDtypeStruct((1, M), F32)],
        scratch_shapes=[pltpu.VMEM((S + CONV_PAD, LANES), F32), pltpu.VMEM((S + CONV_PAD, LANES), F32),
                        pltpu.VMEM((32, SUBLANES, LANES), F32), pltpu.VMEM((SUBLANES, LANES), F32)],
        compiler_params=_params(("arbitrary",)),
    )(dcq, proj, proj, convw4)


def _log_sigmoid(x):
    return jnp.minimum(x, 0.0) - jnp.log(1.0 + jnp.exp(-jnp.abs(x)))


def _rg_gate_terms(ra, ls):
    la = RG_C * ra * ls
    a = jnp.exp(la)
    th = jnp.tanh(la)
    mult = jnp.sqrt(-2.0 * th / (1.0 - th))
    return a, mult


def rnn_fwd(proj, rnnw4, rnn_b, bda, bdi, b_a, b_i, lam, name):
    S = proj.shape[0]
    M = rnn_b.shape[1]
    nb = M // LANES
    tt = min(SEQ_TT, S)
    KW = RNN_CONV_WIDTH
    nseg = SCAN_SEGMENTS if S % (SCAN_SEGMENTS * tt) == 0 else 1

    def body(ux_ref, w_ref, rb_ref, bda_ref, bdi_ref, ba_ref, bi_ref, lam_ref,
             xr_ref, ra_ref, ii_ref, h_ref, uxp, a_sc, b_sc):
        uxp[0:SUBLANES, :] = jnp.zeros((SUBLANES, LANES), F32)
        ls = _log_sigmoid(lam_ref[...])

        def step(t, carry):
            base = pl.multiple_of(t * tt, tt)
            uxp[pl.ds(base + SUBLANES, tt), :] = ux_ref[pl.ds(base, tt), :].astype(F32)
            xr = jnp.broadcast_to(rb_ref[...], (tt, LANES))
            for k in range(KW):
                xr = xr + w_ref[k:k + 1, :] * uxp[pl.ds(base + (SUBLANES - KW + 1) + k, tt), :]
            xb = xr.astype(BF)
            ra = _sigmoid(jnp.dot(xb, bda_ref[...], preferred_element_type=F32) + ba_ref[...])
            ii = _sigmoid(jnp.dot(xb, bdi_ref[...], preferred_element_type=F32) + bi_ref[...])
            a, mult = _rg_gate_terms(ra, ls)
            xr_ref[pl.ds(base, tt), :] = xr
            ra_ref[pl.ds(base, tt), :] = ra
            ii_ref[pl.ds(base, tt), :] = ii
            a_sc[pl.ds(base, tt), :] = a
            b_sc[pl.ds(base, tt), :] = mult * (ii * xr)
            return carry

        lax.fori_loop(0, S // tt, step, 0)

        rows = lax.broadcasted_iota(I32, (SUBLANES, LANES), 0)
        seg = S // nseg
        last = lambda v: jnp.broadcast_to(v[SUBLANES - 1:SUBLANES, :], (SUBLANES, LANES))

        def scan(t, carry):
            hs, ps = carry
            new_h, new_p = [], []
            for s in range(nseg):
                base = pl.multiple_of(s * seg + t * SUBLANES, SUBLANES)
                A = a_sc[pl.ds(base, SUBLANES), :]
                B = b_sc[pl.ds(base, SUBLANES), :]
                for d in (1, 2, 4):
                    As = jnp.where(rows >= d, pltpu.roll(A, d, axis=0), 1.0)
                    Bs = jnp.where(rows >= d, pltpu.roll(B, d, axis=0), 0.0)
                    B = A * Bs + B
                    A = A * As
                hh = B + A * hs[s]
                h_ref[pl.ds(base, SUBLANES), :] = hh
                pp = A * ps[s]
                if s > 0:
                    a_sc[pl.ds(base, SUBLANES), :] = pp
                new_h.append(last(hh))
                new_p.append(last(pp))
            return tuple(new_h), tuple(new_p)

        zero8 = jnp.zeros((SUBLANES, LANES), F32)
        one8 = jnp.ones((SUBLANES, LANES), F32)
        hs, ps = lax.fori_loop(0, seg // SUBLANES, scan, ((zero8,) * nseg, (one8,) * nseg))
        carry_in = hs[0]
        for s in range(1, nseg):
            c_row = carry_in[0:1, :]

            def fix(t, c, s=s, c_row=c_row):
                base = pl.multiple_of(s * seg + t * tt, tt)
                h_ref[pl.ds(base, tt), :] = h_ref[pl.ds(base, tt), :] + a_sc[pl.ds(base, tt), :] * c_row
                return c

            lax.fori_loop(0, seg // tt, fix, 0)
            carry_in = hs[s] + ps[s] * carry_in

    col = lambda o: pl.BlockSpec((S, LANES), lambda c: (0, c + o))
    vec = pl.BlockSpec((1, LANES), lambda c: (0, c))
    diag = pl.BlockSpec((LANES, LANES), lambda c: (c, c))
    return _pcall(
        body, name=name, grid=(nb,),
        in_specs=[col(2 * nb), pl.BlockSpec((None, SUBLANES, LANES), lambda c: (c, 0, 0)), vec, diag, diag, vec, vec, vec],
        out_specs=[col(0)] * 4,
        out_shape=[jax.ShapeDtypeStruct((S, M), F32)] * 4,
        scratch_shapes=[pltpu.VMEM((S + SUBLANES, LANES), F32), pltpu.VMEM((S, LANES), F32), pltpu.VMEM((S, LANES), F32)],
        compiler_params=_params(("arbitrary",)),
    )(proj, rnnw4, rnn_b, bda, bdi, b_a, b_i, lam)


def rnn_bwd(dhout, h, xr, ra, ii, proj, rnnw4, bda, bdi, lam, name):
    S, M = h.shape
    nb = M // LANES
    tt = min(SEQ_TT, S)
    KW = RNN_CONV_WIDTH
    SL = SUBLANES
    nseg = SCAN_SEGMENTS if S % (SCAN_SEGMENTS * tt) == 0 else 1

    def body(dh_ref, h_ref, xr_ref, ra_ref, ii_ref, ux_ref, w_ref, bda_ref, bdi_ref, lam_ref,
             dux_ref, dwa_ref, dwi_ref, drw_ref, vec_ref,
             a_sc, hp, g_sc, dpa_sc, dpi_sc, dxp, uxp, acc8, drw8, p_sc):
        zero8 = jnp.zeros((SL, LANES), F32)
        a_sc[S:S + SL, :] = zero8
        hp[0:SL, :] = zero8
        dxp[S:S + SL, :] = zero8
        uxp[0:SL, :] = zero8
        acc8[...] = jnp.zeros_like(acc8)
        drw8[...] = jnp.zeros_like(drw8)
        lamv = lam_ref[...]
        ls = _log_sigmoid(lamv)

        def fill(t, carry):
            base = pl.multiple_of(t * tt, tt)
            a_sc[pl.ds(base, tt), :] = jnp.exp(RG_C * ra_ref[pl.ds(base, tt), :] * ls)
            hp[pl.ds(base + SL, tt), :] = h_ref[pl.ds(base, tt), :]
            uxp[pl.ds(base + SL, tt), :] = ux_ref[pl.ds(base, tt), :].astype(F32)
            return carry

        lax.fori_loop(0, S // tt, fill, 0)

        rows = lax.broadcasted_iota(I32, (SL, LANES), 0)
        seg = S // nseg
        nt8 = seg // SL
        first = lambda v: jnp.broadcast_to(v[0:1, :], (SL, LANES))

        def rscan(t, carry):
            gs, ps = carry
            new_g, new_p = [], []
            for s in range(nseg):
                base = pl.multiple_of(s * seg + (nt8 - 1 - t) * SL, SL)
                A = a_sc[pl.ds(base + 1, SL), :]
                B = dh_ref[pl.ds(base, SL), :]
                for d in (1, 2, 4):
                    As = jnp.where(rows < SL - d, pltpu.roll(A, SL - d, axis=0), 1.0)
                    Bs = jnp.where(rows < SL - d, pltpu.roll(B, SL - d, axis=0), 0.0)
                    B = A * Bs + B
                    A = A * As
                g = B + A * gs[s]
                g_sc[pl.ds(base, SL), :] = g
                pp = A * ps[s]
                if s < nseg - 1:
                    p_sc[pl.ds(base, SL), :] = pp
                new_g.append(first(g))
                new_p.append(first(pp))
            return tuple(new_g), tuple(new_p)

        one8 = jnp.ones((SL, LANES), F32)
        gs, ps = lax.fori_loop(0, nt8, rscan, ((zero8,) * nseg, (one8,) * nseg))
        carry_in = gs[nseg - 1]
        for s in range(nseg - 2, -1, -1):
            c_row = carry_in[0:1, :]

            def fix(t, c, s=s, c_row=c_row):
                base = pl.multiple_of(s * seg + t * tt, tt)
                g_sc[pl.ds(base, tt), :] = g_sc[pl.ds(base, tt), :] + p_sc[pl.ds(base, tt), :] * c_row
                return c

            lax.fori_loop(0, seg // tt, fix, 0)
            carry_in = gs[s] + ps[s] * carry_in

        def red8(v):
            return v.reshape(tt // SL, SL, LANES).sum(axis=0)

        def step(t, carry):
            base = pl.multiple_of(t * tt, tt)
            g = g_sc[pl.ds(base, tt), :]
            hprev = hp[pl.ds(base + SL - 1, tt), :]
            xr_t = xr_ref[pl.ds(base, tt), :]
            ra_t = ra_ref[pl.ds(base, tt), :]
            ii_t = ii_ref[pl.ds(base, tt), :]
            a, mult = _rg_gate_terms(ra_t, ls)
            gx = g * xr_t
            dmult = gx * ii_t
            dii = gx * mult
            dxr = g * (mult * ii_t)
            dla = g * hprev * a - dmult * (a * a) / mult
            acc8[3] += red8(dla * ra_t)
            dpa = dla * (RG_C * ls) * ra_t * (1.0 - ra_t)
            dpi = dii * ii_t * (1.0 - ii_t)
            dpab = dpa.astype(BF)
            dpib = dpi.astype(BF)
            dxr = dxr + (lax.dot_general(dpab, bda_ref[...], CONTRACT_LAST, preferred_element_type=F32)
                         + lax.dot_general(dpib, bdi_ref[...], CONTRACT_LAST, preferred_element_type=F32))
            dpa_sc[pl.ds(base, tt), :] = dpab
            dpi_sc[pl.ds(base, tt), :] = dpib
            dxp[pl.ds(base, tt), :] = dxr
            acc8[0] += red8(dxr)
            acc8[1] += red8(dpa)
            acc8[2] += red8(dpi)
            return carry

        lax.fori_loop(0, S // tt, step, 0)

        def convb(t, carry):
            base = pl.multiple_of(t * tt, tt)
            d_t = dxp[pl.ds(base, tt), :]
            dux = jnp.zeros((tt, LANES), F32)
            for k in range(KW):
                drw8[k] += red8(d_t * uxp[pl.ds(base + (SL - KW + 1) + k, tt), :])
                dux = dux + w_ref[k:k + 1, :] * dxp[pl.ds(base + (KW - 1) - k, tt), :]
            dux_ref[pl.ds(base, tt), :] = dux.astype(BF)
            return carry

        lax.fori_loop(0, S // tt, convb, 0)

        xb = xr_ref[...].astype(BF)
        dwa_ref[...] = lax.dot_general(xb, dpa_sc[...], CONTRACT_FIRST, preferred_element_type=F32)
        dwi_ref[...] = lax.dot_general(xb, dpi_sc[...], CONTRACT_FIRST, preferred_element_type=F32)
        drw_ref[...] = jnp.zeros_like(drw_ref)
        vec_ref[...] = jnp.zeros_like(vec_ref)
        for k in range(KW):
            drw_ref[k:k + 1, :] = _colsum(drw8[k])
        for k in range(3):
            vec_ref[k:k + 1, :] = _colsum(acc8[k])
        vec_ref[3:4, :] = _colsum(acc8[3]) * (RG_C * _sigmoid(-lamv))

    col = lambda o: pl.BlockSpec((S, LANES), lambda c: (0, c + o))
    vec = pl.BlockSpec((1, LANES), lambda c: (0, c))
    diag = pl.BlockSpec((LANES, LANES), lambda c: (c, c))
    blk3 = lambda r: pl.BlockSpec((None, r, LANES), lambda c: (c, 0, 0))
    return _pcall(
        body, name=name, grid=(nb,),
        in_specs=[col(0), col(0), col(0), col(0), col(0), col(2 * nb), blk3(SL), diag, diag, vec],
        out_specs=[col(0), blk3(LANES), blk3(LANES), blk3(SL), pl.BlockSpec((SL, LANES), lambda c: (0, c))],
        out_shape=[jax.ShapeDtypeStruct((S, M), BF), jax.ShapeDtypeStruct((nb, LANES, LANES), F32),
                   jax.ShapeDtypeStruct((nb, LANES, LANES), F32), jax.ShapeDtypeStruct((nb, SL, LANES), F32),
                   jax.ShapeDtypeStruct((SL, M), F32)],
        scratch_shapes=[pltpu.VMEM((S + SL, LANES), F32), pltpu.VMEM((S + SL, LANES), F32), pltpu.VMEM((S, LANES), F32),
                        pltpu.VMEM((S, LANES), BF), pltpu.VMEM((S, LANES), BF), pltpu.VMEM((S + SL, LANES), F32),
                        pltpu.VMEM((S + SL, LANES), F32), pltpu.VMEM((SL, SL, LANES), F32), pltpu.VMEM((SL, SL, LANES), F32),
                        pltpu.VMEM((S, LANES), F32)],
        compiler_params=_params(("arbitrary",)),
    )(dhout, h, xr, ra, ii, proj, rnnw4, bda, bdi, lam)


GELU_K = 0.7978845608028654
GELU_C = 0.044715


def _layernorm_parts(cq):
    mu = _rowmean(cq)
    d = cq - mu
    rstd = lax.rsqrt(_rowmean(d * d) + EPS)
    return d * rstd, rstd


def mix_out(cq, proj, h, x, vecs, lnv, wout, name):
    S, D = x.shape
    M = cq.shape[1]
    ts = min(512, S)

    def body(cq_ref, uy_ref, h_ref, x_ref, v_ref, ln_ref, w_ref, xo_ref, ym_ref, yc_ref):
        z, _ = _layernorm_parts(cq_ref[...])
        l = z * _row(ln_ref, 0) + _row(ln_ref, 1)
        yc_ref[:, 0:M] = (l * _sigmoid(l)).astype(BF)
        uy = uy_ref[...].astype(F32)
        gelu = 0.5 * uy * (1.0 + jnp.tanh(GELU_K * (uy + GELU_C * uy * uy * uy)))
        yc_ref[:, M:2 * M] = (gelu * h_ref[...]).astype(BF)
        ym = jnp.dot(yc_ref[...], w_ref[...], preferred_element_type=F32)
        ym_ref[...] = ym.astype(BF)
        xo_ref[...] = x_ref[...] + _row(v_ref, R_GT2) * ym

    tok = pl.BlockSpec((ts, D), lambda i: (i, 0))
    mtok = lambda o: pl.BlockSpec((ts, M), lambda i: (i, o))
    return _pcall(
        body, name=name, grid=(S // ts,),
        in_specs=[mtok(0), mtok(3), mtok(0), tok, pl.BlockSpec(vecs.shape, lambda i: (0, 0)),
                  pl.BlockSpec(lnv.shape, lambda i: (0, 0)), pl.BlockSpec(wout.shape, lambda i: (0, 0))],
        out_specs=[tok, tok, pl.BlockSpec((ts, 2 * M), lambda i: (i, 0))],
        out_shape=[jax.ShapeDtypeStruct((S, D), F32), jax.ShapeDtypeStruct((S, D), BF),
                   jax.ShapeDtypeStruct((S, 2 * M), BF)],
        compiler_params=_params(("arbitrary",)),
    )(cq, proj, h, x, vecs, lnv, wout)


def mix_out_bwd(dxo, ym, vecs, wout, cq, lnv, proj, h, name):
    S, D = dxo.shape
    M = cq.shape[1]
    ts = min(512, S)

    def body(dxo_ref, ym_ref, v_ref, w_ref, cq_ref, ln_ref, uy_ref, h_ref,
             dcq_ref, dh_ref, duy_ref, dyb_ref, vgd_ref, vgm_ref):
        @pl.when(pl.program_id(0) == 0)
        def _():
            vgd_ref[...] = jnp.zeros_like(vgd_ref)
            vgm_ref[...] = jnp.zeros_like(vgm_ref)

        dxo_v = dxo_ref[...]
        dyb = (_row(v_ref, R_GT2) * dxo_v).astype(BF)
        dyb_ref[...] = dyb
        vgd_ref[0:1, :] += _colsum(dxo_v * ym_ref[...].astype(F32))
        dycat = lax.dot_general(dyb, w_ref[...], CONTRACT_LAST, preferred_element_type=F32)
        dyc = dycat[:, 0:M]
        dyr = dycat[:, M:2 * M]
        z, rstd = _layernorm_parts(cq_ref[...])
        lng = _row(ln_ref, 0)
        l = z * lng + _row(ln_ref, 1)
        sl = _sigmoid(l)
        dl = dyc * (sl * (1.0 + l * (1.0 - sl)))
        vgm_ref[0:1, :] += _colsum(dl * z)
        vgm_ref[1:2, :] += _colsum(dl)
        dz = dl * lng
        dcq_ref[...] = (rstd * (dz - _rowmean(dz) - z * _rowmean(dz * z))).astype(BF)
        uy = uy_ref[...].astype(F32)
        u2 = uy * uy
        th = jnp.tanh(GELU_K * (uy + GELU_C * uy * u2))
        gelu = 0.5 * uy * (1.0 + th)
        dgelu = 0.5 * (1.0 + th) + 0.5 * uy * (1.0 - th * th) * (GELU_K * (1.0 + 3.0 * GELU_C * u2))
        dh_ref[...] = dyr * gelu
        duy_ref[...] = (dyr * h_ref[...] * dgelu).astype(BF)

    tok = pl.BlockSpec((ts, D), lambda i: (i, 0))
    mtok = lambda o: pl.BlockSpec((ts, M), lambda i: (i, o))
    return _pcall(
        body, name=name, grid=(S // ts,),
        in_specs=[tok, tok, pl.BlockSpec(vecs.shape, lambda i: (0, 0)), pl.BlockSpec(wout.shape, lambda i: (0, 0)),
                  mtok(0), pl.BlockSpec(lnv.shape, lambda i: (0, 0)), mtok(3), mtok(0)],
        out_specs=[mtok(0), mtok(0), mtok(0), tok, pl.BlockSpec((SUBLANES, D), lambda i: (0, 0)),
                   pl.BlockSpec((SUBLANES, M), lambda i: (0, 0))],
        out_shape=[jax.ShapeDtypeStruct((S, M), BF), jax.ShapeDtypeStruct((S, M), F32), jax.ShapeDtypeStruct((S, M), BF),
                   jax.ShapeDtypeStruct((S, D), BF),
                   jax.ShapeDtypeStruct((SUBLANES, D), F32), jax.ShapeDtypeStruct((SUBLANES, M), F32)],
        compiler_params=_params(("arbitrary",)),
    )(dxo, ym, vecs, wout, cq, lnv, proj, h)


def mix_in_bwd(dparts, x, dxo, vecs, win, name):
    S, D = x.shape
    M = dparts[0].shape[1]
    ts = min(512, S)

    def body(d0, d1, d2, d3, x_ref, dxo_ref, v_ref, w_ref, dx_ref, hb_ref, dp_ref, vg_ref):
        @pl.when(pl.program_id(0) == 0)
        def _():
            vg_ref[...] = jnp.zeros_like(vg_ref)

        for q, dref in enumerate((d0, d1, d2, d3)):
            dp_ref[:, q * M:(q + 1) * M] = dref[...].astype(BF)
        dh = lax.dot_general(dp_ref[...], w_ref[...], CONTRACT_LAST, preferred_element_type=F32)
        xv = x_ref[...]
        r = lax.rsqrt(_rowmean(xv * xv) + EPS)
        n = xv * r
        g = _row(v_ref, R_G2)
        sc1 = 1.0 + _row(v_ref, R_SC2)
        gsc = g * sc1
        hb_ref[...] = (n * gsc + _row(v_ref, R_SH2)).astype(BF)
        dhn = dh * n
        vg_ref[0:1, :] += _colsum(dh)
        vg_ref[1:2, :] += _colsum(dhn) * g
        vg_ref[2:3, :] += _colsum(dhn) * sc1
        dn = dh * gsc
        dx_ref[...] = dxo_ref[...] + r * (dn - n * _rowmean(dn * n))

    tok = pl.BlockSpec((ts, D), lambda i: (i, 0))
    mtok = pl.BlockSpec((ts, M), lambda i: (i, 0))
    return _pcall(
        body, name=name, grid=(S // ts,),
        in_specs=[mtok] * 4 + [tok, tok, pl.BlockSpec(vecs.shape, lambda i: (0, 0)), pl.BlockSpec(win.shape, lambda i: (0, 0))],
        out_specs=[tok, tok, pl.BlockSpec((ts, 4 * M), lambda i: (i, 0)), pl.BlockSpec((SUBLANES, D), lambda i: (0, 0))],
        out_shape=[jax.ShapeDtypeStruct((S, D), F32), jax.ShapeDtypeStruct((S, D), BF),
                   jax.ShapeDtypeStruct((S, 4 * M), BF), jax.ShapeDtypeStruct((SUBLANES, D), F32)],
        compiler_params=_params(("arbitrary",)),
    )(*dparts, x, dxo, vecs, win)


def _adamw(w, g, m, v):
    m = ADAM_B1 * m + (1.0 - ADAM_B1) * g
    v = ADAM_B2 * v + (1.0 - ADAM_B2) * (g * g)
    m_hat = m / (1.0 - ADAM_B1 ** ADAM_STEP)
    v_hat = v / (1.0 - ADAM_B2 ** ADAM_STEP)
    delta = -ADAM_LR * (m_hat / (jnp.sqrt(v_hat) + ADAM_EPS) + ADAM_WD * w)
    return delta, m, v


def adam_big(w, g_mine, g_sib, m, v, cidx, name):
    R, C = w.shape
    hr = R // 2
    tr = 256 if hr % 256 == 0 else hr
    tc = C if C <= 1536 else (1152 if C % 1152 == 0 else 1024)
    assert hr % tr == 0 and C % tc == 0 and g_mine.shape == (hr, C)
    nrb = hr // tr

    def body(ci_ref, w_ref, gm_ref, gs_ref, m_ref, v_ref, g_ref, d_ref, nm_ref, nv_ref):
        mine = (pl.program_id(0) // nrb) == ci_ref[0]
        g = jnp.where(mine, gm_ref[...], gs_ref[...])
        d, nm, nv = _adamw(w_ref[...], g, m_ref[...], v_ref[...])
        g_ref[...] = g
        d_ref[...] = d
        nm_ref[...] = nm
        nv_ref[...] = nv

    blk = pl.BlockSpec((tr, tc), lambda i, j, ci: (i, j))
    mine_spec = pl.BlockSpec((tr, tc), lambda i, j, ci: (jnp.where(i // nrb == ci[0], i % nrb, 0), j))
    sib_spec = pl.BlockSpec((tr, tc), lambda i, j, ci: (jnp.where(i // nrb == ci[0], 0, i % nrb), j))
    gs = pltpu.PrefetchScalarGridSpec(num_scalar_prefetch=1, grid=(R // tr, C // tc),
                                      in_specs=[blk, mine_spec, sib_spec, blk, blk], out_specs=[blk] * 4)
    return _pcall(body, name=name, grid_spec=gs, out_shape=[jax.ShapeDtypeStruct((R, C), F32)] * 4,
                  compiler_params=_params(("parallel", "parallel")))(cidx, w, g_mine, g_sib, m, v)


def adam_cond(c_all, dmod, w, m, v, name):
    B, Kin = c_all.shape
    N = w.shape[1]
    tn = 768 if N % 768 == 0 else 256
    assert N % tn == 0

    def body(c_ref, d_ref, w_ref, m_ref, v_ref, g_ref, dl_ref, nm_ref, nv_ref):
        cv = c_ref[...]
        ca = cv * _sigmoid(cv)
        g = lax.dot_general(ca.astype(BF), d_ref[...].astype(BF), CONTRACT_FIRST, preferred_element_type=F32)
        d, nm, nv = _adamw(w_ref[...], g, m_ref[...], v_ref[...])
        g_ref[...] = g
        dl_ref[...] = d
        nm_ref[...] = nm
        nv_ref[...] = nv

    blk = pl.BlockSpec((Kin, tn), lambda n: (0, n))
    return _pcall(
        body, name=name, grid=(N // tn,),
        in_specs=[pl.BlockSpec((B, Kin), lambda n: (0, 0)), pl.BlockSpec((B, tn), lambda n: (0, n)), blk, blk, blk],
        out_specs=[blk] * 4, out_shape=[jax.ShapeDtypeStruct((Kin, N), F32)] * 4,
        compiler_params=_params(("parallel",)),
    )(c_all, dmod, w, m, v)


def adam_small(ws, gs, ms, vs, name):
    n = len(ws)

    def body(*refs):
        ins, outs = refs[:4 * n], refs[4 * n:]
        for k in range(n):
            d, nm, nv = _adamw(ins[k][...], ins[n + k][...], ins[2 * n + k][...], ins[3 * n + k][...])
            outs[k][...] = d
            outs[n + k][...] = nm
            outs[2 * n + k][...] = nv

    specs = [pl.BlockSpec(w.shape, lambda i: (0, 0)) for w in ws]
    shapes = [jax.ShapeDtypeStruct(w.shape, F32) for w in ws]
    out = _pcall(body, name=name, grid=(1,), in_specs=specs * 4, out_specs=specs * 3, out_shape=shapes * 3,
                 compiler_params=_params(("arbitrary",)))(*ws, *gs, *ms, *vs)
    return out[:n], out[n:2 * n], out[2 * n:]


def _me():
    return lax.axis_index("x"), lax.axis_index("y"), lax.axis_index("c")


def _flip(x, y, p):
    return (x ^ (p >> 1) if (p >> 1) else x), (y ^ (p & 1) if (p & 1) else y)


def _handshake(peers):
    barrier = pltpu.get_barrier_semaphore()
    for peer in peers:
        pl.semaphore_signal(barrier, inc=1, device_id=peer, device_id_type=MESH)
    pl.semaphore_wait(barrier, len(peers))


def _seq_call(body, *, name, n_in, out_shape, sem_shapes, collective_id):
    del n_in
    return pl.kernel(body, out_type=out_shape, mesh=plsc.ScalarSubcoreMesh(axis_name="sq", num_cores=1), name=name,
                     scratch_types=sem_shapes, compiler_params=pltpu.CompilerParams(collective_id=collective_id))


def _hbm_comm_call(body, *, name, n_in, out_shape, sem_shapes, seq_id):
    if seq_id is not None:
        return _seq_call(body, name=name, n_in=n_in, out_shape=out_shape, sem_shapes=sem_shapes, collective_id=seq_id)
    anyspec = pl.BlockSpec(memory_space=pl.ANY)
    return _pcall(body, name=name, in_specs=[anyspec] * n_in, out_specs=[anyspec] * len(out_shape), out_shape=out_shape,
                  scratch_shapes=sem_shapes, compiler_params=_params())


def allgather_devices(v, name, with_sum=False):
    R, L = v.shape

    def body(v_ref, out_ref, *rest):
        if with_sum:
            sum_ref, send_sems, recv_sems = rest
        else:
            send_sems, recv_sems = rest
        x, y, c = _me()
        me = 4 * x + 2 * y + c
        out_ref[me] = v_ref[...]
        copies = []
        for p in range(1, N_DEV):
            px, py = _flip(x, y, p >> 1)
            pc = (1 - c) if (p & 1) else c
            peer = 4 * px + 2 * py + pc
            send = pltpu.make_async_remote_copy(src_ref=v_ref, dst_ref=out_ref.at[me], send_sem=send_sems.at[p - 1],
                                                recv_sem=recv_sems.at[p - 1], device_id=(px, py, pc), device_id_type=MESH)
            send.start()
            recv = pltpu.make_async_remote_copy(src_ref=v_ref, dst_ref=out_ref.at[peer], send_sem=send_sems.at[p - 1],
                                                recv_sem=recv_sems.at[p - 1], device_id=(px, py, pc), device_id_type=MESH)
            copies.append((send, recv))
        for send, recv in copies:
            recv.wait_recv()
        for send, recv in copies:
            send.wait_send()
        if with_sum:
            s = out_ref[0]
            for k in range(1, N_DEV):
                s = s + out_ref[k]
            sum_ref[...] = s

    vm = pl.BlockSpec(memory_space=pltpu.VMEM)
    out_shape = [jax.ShapeDtypeStruct((N_DEV, R, L), F32)]
    if with_sum:
        out_shape.append(jax.ShapeDtypeStruct((R, L), F32))
    return _pcall(
        body, name=name, in_specs=[vm], out_specs=[vm] * len(out_shape), out_shape=out_shape,
        scratch_shapes=[pltpu.SemaphoreType.DMA((N_DEV - 1,)), pltpu.SemaphoreType.DMA((N_DEV - 1,))],
        compiler_params=_params(),
    )(v)


def allgather_devices_hbm(v, name, seq_id):
    R, L = v.shape

    def body(v_ref, out_ref, send_sems, recv_sems, local_sem):
        x, y, c = _me()
        me = 4 * x + 2 * y + c
        peers = []
        for p in range(1, N_DEV):
            px, py = _flip(x, y, p >> 1)
            peers.append((px, py, (1 - c) if (p & 1) else c))
        _handshake(peers)
        lc = pltpu.make_async_copy(v_ref, out_ref.at[me], local_sem)
        lc.start()
        copies = []
        for p, (px, py, pc) in enumerate(peers):
            send = pltpu.make_async_remote_copy(src_ref=v_ref, dst_ref=out_ref.at[me], send_sem=send_sems.at[p],
                                                recv_sem=recv_sems.at[p], device_id=(px, py, pc), device_id_type=MESH)
            send.start()
            recv = pltpu.make_async_remote_copy(src_ref=v_ref, dst_ref=out_ref.at[4 * px + 2 * py + pc], send_sem=send_sems.at[p],
                                                recv_sem=recv_sems.at[p], device_id=(px, py, pc), device_id_type=MESH)
            copies.append((send, recv))
        for send, recv in copies:
            recv.wait_recv()
        for send, recv in copies:
            send.wait_send()
        lc.wait()

    return _seq_call(body, name=name, n_in=1, out_shape=[jax.ShapeDtypeStruct((N_DEV, R, L), F32)],
                     sem_shapes=[pltpu.SemaphoreType.DMA((N_DEV - 1,)), pltpu.SemaphoreType.DMA((N_DEV - 1,)),
                                 pltpu.SemaphoreType.DMA], collective_id=seq_id)(v)[0]


def sum_slots(g, name):
    n, R, L = g.shape
    tr = 216 if R % 216 == 0 else R
    assert R % tr == 0 and tr % SUBLANES == 0

    def body(g_ref, o_ref):
        s = g_ref[0]
        for k in range(1, n):
            s = s + g_ref[k]
        o_ref[...] = s

    return _pcall(body, name=name, grid=(R // tr,), in_specs=[pl.BlockSpec((n, tr, L), lambda i: (0, i, 0))],
                  out_specs=pl.BlockSpec((tr, L), lambda i: (i, 0)), out_shape=jax.ShapeDtypeStruct((R, L), F32),
                  compiler_params=_params(("parallel",)))(g)


def allgather_chips(v, name):
    R, L = v.shape

    def body(v_ref, out_ref, send_sems, recv_sems):
        x, y, c = _me()
        chip = 2 * x + y
        out_ref[chip] = v_ref[...]
        copies = []
        for p in range(1, N_CHIPS):
            px, py = _flip(x, y, p)
            send = pltpu.make_async_remote_copy(src_ref=v_ref, dst_ref=out_ref.at[chip], send_sem=send_sems.at[p - 1],
                                                recv_sem=recv_sems.at[p - 1], device_id=(px, py, c), device_id_type=MESH)
            send.start()
            recv = pltpu.make_async_remote_copy(src_ref=v_ref, dst_ref=out_ref.at[2 * px + py], send_sem=send_sems.at[p - 1],
                                                recv_sem=recv_sems.at[p - 1], device_id=(px, py, c), device_id_type=MESH)
            copies.append((send, recv))
        for send, recv in copies:
            recv.wait_recv()
        for send, recv in copies:
            send.wait_send()

    vm = pl.BlockSpec(memory_space=pltpu.VMEM)
    return _pcall(
        body, name=name, in_specs=[vm], out_specs=vm, out_shape=jax.ShapeDtypeStruct((N_CHIPS, R, L), F32),
        scratch_shapes=[pltpu.SemaphoreType.DMA((N_CHIPS - 1,)), pltpu.SemaphoreType.DMA((N_CHIPS - 1,))],
        compiler_params=_params(),
    )(v)


def _shard_window(ref, kind, shard_shape, chip, half):
    r, c = shard_shape
    hr = r // 2
    if kind == "col":
        return ref.at[pl.ds(pl.multiple_of(half * hr, hr), hr), pl.ds(pl.multiple_of(chip * c, c), c)]
    return ref.at[pl.ds(pl.multiple_of(chip * r + half * hr, hr), hr), :]


def allgather_weights(shards, kinds, name, seq_id=None):
    n = len(shards)
    fulls = []
    for s, kind in zip(shards, kinds):
        r, c = s.shape
        fulls.append(jax.ShapeDtypeStruct((r, N_CHIPS * c) if kind == "col" else (N_CHIPS * r, c), s.dtype))

    def body(*refs):
        srcs, outs = refs[:n], refs[n:2 * n]
        send_sems, recv_sems, fsend_sems, frecv_sems = refs[2 * n:]
        x, y, c = _me()
        chip = 2 * x + y
        sib = (x, y, 1 - c)
        if seq_id is not None:
            _handshake([(*_flip(x, y, p), c) for p in range(1, N_CHIPS)] + [sib])
        sends, fwds = [], []
        for i in range(n):
            shp = srcs[i].shape
            hr = shp[0] // 2
            my_half = srcs[i].at[pl.ds(pl.multiple_of(c * hr, hr), hr), :]
            for p in range(1, N_CHIPS):
                px, py = _flip(x, y, p)
                k = i * (N_CHIPS - 1) + p - 1
                cp = pltpu.make_async_remote_copy(src_ref=my_half, dst_ref=_shard_window(outs[i], kinds[i], shp, chip, c),
                                                  send_sem=send_sems.at[k], recv_sem=recv_sems.at[k],
                                                  device_id=(px, py, c), device_id_type=MESH)
                cp.start()
                sends.append(cp)
        for i in range(n):
            shp = srcs[i].shape
            for p in range(1, N_CHIPS):
                px, py = _flip(x, y, p)
                k = i * (N_CHIPS - 1) + p - 1
                landed = _shard_window(outs[i], kinds[i], shp, 2 * px + py, c)
                pltpu.make_async_remote_copy(src_ref=landed, dst_ref=landed, send_sem=send_sems.at[k], recv_sem=recv_sems.at[k],
                                             device_id=(px, py, c), device_id_type=MESH).wait_recv()
                fw = pltpu.make_async_remote_copy(src_ref=landed, dst_ref=landed, send_sem=fsend_sems.at[k],
                                                  recv_sem=frecv_sems.at[k], device_id=sib, device_id_type=MESH)
                fw.start()
                fwds.append(fw)
        for i in range(n):
            shp = srcs[i].shape
            for p in range(1, N_CHIPS):
                px, py = _flip(x, y, p)
                k = i * (N_CHIPS - 1) + p - 1
                other = _shard_window(outs[i], kinds[i], shp, 2 * px + py, 1 - c)
                pltpu.make_async_remote_copy(src_ref=other, dst_ref=other, send_sem=fsend_sems.at[k], recv_sem=frecv_sems.at[k],
                                             device_id=sib, device_id_type=MESH).wait_recv()
        for cp in sends + fwds:
            cp.wait_send()

    nk = n * (N_CHIPS - 1)
    gathered = _hbm_comm_call(
        body, name=name, n_in=n, out_shape=fulls, seq_id=seq_id,
        sem_shapes=[pltpu.SemaphoreType.DMA((nk,)), pltpu.SemaphoreType.DMA((nk,)), pltpu.SemaphoreType.DMA((nk,)),
                    pltpu.SemaphoreType.DMA((nk,))],
    )(*shards)
    return gathered


def place_local_shards(fulls, shards, kinds, name):
    n = len(shards)
    chip = jnp.reshape(2 * lax.axis_index("x") + lax.axis_index("y"), (1,)).astype(I32)

    def body(ci_ref, *refs):
        for i in range(n):
            refs[2 * n + i][...] = refs[i][...]

    in_specs = [pl.BlockSpec(s.shape, lambda i, ci: (0, 0)) for s in shards] + [pl.BlockSpec(memory_space=pl.ANY)] * n
    out_specs = [pl.BlockSpec(s.shape, (lambda i, ci: (0, ci[0])) if k == "col" else (lambda i, ci: (ci[0], 0)))
                 for s, k in zip(shards, kinds)]
    gs = pltpu.PrefetchScalarGridSpec(num_scalar_prefetch=1, grid=(1,), in_specs=in_specs, out_specs=out_specs)
    return _pcall(body, name=name, grid_spec=gs, out_shape=[jax.ShapeDtypeStruct(f.shape, f.dtype) for f in fulls],
                  input_output_aliases={1 + n + i: i for i in range(n)}, compiler_params=_params(("arbitrary",)))(chip, *shards, *fulls)


def _as_halves(g, kind, shard_shape):
    r, c = shard_shape
    if kind == "col":
        return g.reshape(2, r // 2, N_CHIPS * c)
    return g.reshape(N_CHIPS, 2, r // 2, c)


def exchange_sibling_halves(grads, kinds, shard_shapes, name, seq_id=None):
    n = len(grads)
    views = [_as_halves(g, k, s) for g, k, s in zip(grads, kinds, shard_shapes)]
    outs = []
    for k, (r, c) in zip(kinds, shard_shapes):
        outs.append(jax.ShapeDtypeStruct((r // 2, N_CHIPS * c) if k == "col" else (N_CHIPS, r // 2, c), grads[0].dtype))

    def body(*refs):
        srcs, dsts = refs[:n], refs[n:2 * n]
        send_sems, recv_sems = refs[2 * n:]
        x, y, c = _me()
        if seq_id is not None:
            _handshake([(x, y, 1 - c)])
        cps = []
        for i in range(n):
            src = srcs[i].at[1 - c] if kinds[i] == "col" else srcs[i].at[:, 1 - c]
            cp = pltpu.make_async_remote_copy(src_ref=src, dst_ref=dsts[i], send_sem=send_sems.at[i], recv_sem=recv_sems.at[i],
                                              device_id=(x, y, 1 - c), device_id_type=MESH)
            cp.start()
            cps.append(cp)
        for cp in cps:
            cp.wait_recv()
        for cp in cps:
            cp.wait_send()

    return _hbm_comm_call(body, name=name, n_in=n, out_shape=outs, seq_id=seq_id,
                          sem_shapes=[pltpu.SemaphoreType.DMA((n,)), pltpu.SemaphoreType.DMA((n,))])(*views)


def add_sibling_half(g, recv, kind, shard_shape, core_chip, name):
    r, c = shard_shape
    hr = r // 2
    gv = _as_halves(g, kind, shard_shape)
    tr = hr if hr <= 512 else (256 if hr % 256 == 0 else hr // 2)
    assert hr % tr == 0

    def body(ci_ref, g_ref, r_ref, h_ref, hb_ref):
        s = g_ref[...].astype(F32) + r_ref[...].astype(F32)
        hb_ref[...] = s.astype(BF)

        @pl.when(pl.program_id(1) == ci_ref[1])
        def _():
            h_ref[...] = s

    grid = (hr // tr, N_CHIPS)
    if kind == "col":
        g_spec = pl.BlockSpec((None, tr, c), lambda i, k, ci: (ci[0], i, k))
        o_spec = pl.BlockSpec((tr, c), lambda i, k, ci: (i, k))
    else:
        g_spec = pl.BlockSpec((None, None, tr, c), lambda i, k, ci: (k, ci[0], i, 0))
        o_spec = pl.BlockSpec((None, tr, c), lambda i, k, ci: (k, i, 0))
    own_spec = pl.BlockSpec((tr, c), lambda i, k, ci: (i, 0))
    gs = pltpu.PrefetchScalarGridSpec(num_scalar_prefetch=1, grid=grid, in_specs=[g_spec, o_spec], out_specs=[own_spec, o_spec])
    return _pcall(
        body, name=name, grid_spec=gs,
        out_shape=[jax.ShapeDtypeStruct((hr, c), F32), jax.ShapeDtypeStruct(recv.shape, BF)],
        compiler_params=_params(("parallel", "arbitrary")),
    )(core_chip, gv, recv)


def exchange_chip_pieces(hbs, kinds, shard_shapes, name, seq_id=None):
    n = len(hbs)
    outs = [jax.ShapeDtypeStruct((N_CHIPS - 1, r // 2, c), BF) for (r, c) in shard_shapes]

    def body(*refs):
        srcs, dsts = refs[:n], refs[n:2 * n]
        send_sems, recv_sems = refs[2 * n:]
        x, y, c = _me()
        if seq_id is not None:
            _handshake([(*_flip(x, y, p), c) for p in range(1, N_CHIPS)])
        cps = []
        for i in range(n):
            cc = shard_shapes[i][1]
            for p in range(1, N_CHIPS):
                px, py = _flip(x, y, p)
                pchip = 2 * px + py
                src = (srcs[i].at[:, pl.ds(pl.multiple_of(pchip * cc, cc), cc)] if kinds[i] == "col" else srcs[i].at[pchip])
                k = i * (N_CHIPS - 1) + p - 1
                cp = pltpu.make_async_remote_copy(src_ref=src, dst_ref=dsts[i].at[p - 1], send_sem=send_sems.at[k],
                                                  recv_sem=recv_sems.at[k], device_id=(px, py, c), device_id_type=MESH)
                cp.start()
                cps.append(cp)
        for cp in cps:
            cp.wait_recv()
        for cp in cps:
            cp.wait_send()

    nk = n * (N_CHIPS - 1)
    return _hbm_comm_call(body, name=name, n_in=n, out_shape=outs, seq_id=seq_id,
                          sem_shapes=[pltpu.SemaphoreType.DMA((nk,)), pltpu.SemaphoreType.DMA((nk,))])(*hbs)


def sum_chip_pieces(h_own, pieces, name):
    hr, c = h_own.shape
    tr = hr if hr <= 512 else (256 if hr % 256 == 0 else hr // 2)
    assert hr % tr == 0

    def body(h_ref, p_ref, q_ref):
        q_ref[...] = ((h_ref[...] + p_ref[0].astype(F32)) + p_ref[1].astype(F32)) + p_ref[2].astype(F32)

    blk = pl.BlockSpec((tr, c), lambda i: (i, 0))
    return _pcall(body, name=name, grid=(hr // tr,), in_specs=[blk, pl.BlockSpec((N_CHIPS - 1, tr, c), lambda i: (0, i, 0))],
                  out_specs=blk, out_shape=jax.ShapeDtypeStruct((hr, c), F32), compiler_params=_params(("parallel",)))(h_own, pieces)


def exchange_reduced_halves(qs, name, seq_id):
    n = len(qs)

    def body(*refs):
        srcs, dsts = refs[:n], refs[n:2 * n]
        send_sems, recv_sems = refs[2 * n:]
        x, y, c = _me()
        _handshake([(x, y, 1 - c)])
        cps = []
        for i in range(n):
            cp = pltpu.make_async_remote_copy(src_ref=srcs[i], dst_ref=dsts[i], send_sem=send_sems.at[i], recv_sem=recv_sems.at[i],
                                              device_id=(x, y, 1 - c), device_id_type=MESH)
            cp.start()
            cps.append(cp)
        for cp in cps:
            cp.wait_recv()
        for cp in cps:
            cp.wait_send()

    return _seq_call(body, name=name, n_in=n, out_shape=[jax.ShapeDtypeStruct(q.shape, F32) for q in qs],
                     sem_shapes=[pltpu.SemaphoreType.DMA((n,)), pltpu.SemaphoreType.DMA((n,))], collective_id=seq_id)(*qs)


def _rows128(a):
    return a.reshape(-1, LANES)


def _after(xs, *deps):
    flat = []
    for d in deps:
        flat.extend(d if isinstance(d, (list, tuple)) else [d])
    return list(lax.optimization_barrier((tuple(xs), tuple(flat)))[0])


def _block_diag(w):
    H, d, _ = w.shape
    eye = jnp.eye(H, dtype=w.dtype)
    return jnp.einsum("hde,hg->hdge", w, eye).reshape(H * d, H * d)


def _diag_blocks(g4, H, d):
    nb = g4.shape[0]
    per = LANES // d
    g = g4.reshape(nb, per, d, per, d)
    return jnp.stack([g[:, j, :, j, :] for j in range(per)], axis=1).reshape(H, d, d)


def kernel(x, c, w_mod, b_mod, g_ffn1, w_ffn1_in, w_ffn1_out, g_mix, w_in, conv_w, conv_b, ln_g, ln_b, rnn_conv_w, rnn_conv_b, w_a, b_a, w_i, b_i, lru_lambda, w_out, g_ffn2, w_ffn2_in, w_ffn2_out, w_fmod, b_fmod, g_final, loss_target, m_w_mod, m_b_mod, m_g_ffn1, m_w_ffn1_in, m_w_ffn1_out, m_g_mix, m_w_in, m_conv_w, m_conv_b, m_ln_g, m_ln_b, m_rnn_conv_w, m_rnn_conv_b, m_w_a, m_b_a, m_w_i, m_b_i, m_lru_lambda, m_w_out, m_g_ffn2, m_w_ffn2_in, m_w_ffn2_out, m_w_fmod, m_b_fmod, m_g_final, v_w_mod, v_b_mod, v_g_ffn1, v_w_ffn1_in, v_w_ffn1_out, v_g_mix, v_w_in, v_conv_w, v_conv_b, v_ln_g, v_ln_b, v_rnn_conv_w, v_rnn_conv_b, v_w_a, v_b_a, v_w_i, v_b_i, v_lru_lambda, v_w_out, v_g_ffn2, v_w_ffn2_in, v_w_ffn2_out, v_w_fmod, v_b_fmod, v_g_final):
    S, D = x.shape[1], x.shape[2]
    M = conv_b.shape[1]
    H, HD = w_a.shape[1], w_a.shape[2]
    nb = M // LANES
    ix, iy, ic = lax.axis_index("x"), lax.axis_index("y"), lax.axis_index("c")
    chip = 2 * ix + iy
    dev = 2 * chip + ic
    core_chip = jnp.stack([ic, chip]).astype(I32)
    cidx = core_chip
    xs = x[0]
    tgt = loss_target[0]

    kinds = ["col", "row"]
    w_f1, w_mx, w_f2 = [w_ffn1_in[0], w_ffn1_out[0]], [w_in[0], w_out[0]], [w_ffn2_in[0], w_ffn2_out[0]]
    as_bf = lambda ws: [w.astype(BF) for w in ws]
    shapes_of = lambda ws: [w.shape for w in ws]
    b_f1, b_mx, b_f2 = as_bf(w_f1), as_bf(w_mx), as_bf(w_f2)
    got_f1i = allgather_weights(b_f1[:1], kinds[:1], "gather_ffn1_in", seq_id=9)
    got_f1o = allgather_weights(b_f1[1:], kinds[1:], "gather_ffn1_out", seq_id=13)
    got_mx = allgather_weights(b_mx, kinds, "gather_mix", seq_id=1)
    got_f2 = allgather_weights(b_f2, kinds, "gather_ffn2", seq_id=2)

    c_all =allgather_devices(_rows128(c), "gather_c")[0].reshape(N_DEV, D)
    mod_cols = cond_matmul(c_all, w_mod[0], "mod_proj")
    fmod_cols = cond_matmul(c_all, w_fmod, "fmod_proj")
    convw_pad = jnp.pad(conv_w[0], ((0, 32 - CONV_WIDTH), (0, 0)))
    rnnw_pad = jnp.pad(rnn_conv_w[0], ((0, SUBLANES - RNN_CONV_WIDTH), (0, 0)))
    n_mod, n_fmod = mod_cols.shape[1], fmod_cols.shape[1]
    small = jnp.concatenate([_rows128(mod_cols), _rows128(fmod_cols), convw_pad, rnnw_pad], axis=0)
    small4 = allgather_chips(small, "gather_cond")
    r0 = N_DEV * n_mod // LANES
    r1 = r0 + N_DEV * n_fmod // LANES
    mod_all = small4[:, :r0].reshape(N_CHIPS, N_DEV, n_mod)
    fmod_all = small4[:, r0:r1].reshape(N_CHIPS, N_DEV, n_fmod)
    convw4 = small4[:, r1:r1 + 32]
    rnnw4 = small4[:, r1 + 32:r1 + 32 + SUBLANES]
    mod_row = lax.dynamic_index_in_dim(mod_all, dev, axis=1, keepdims=False).reshape(1, N_CHIPS * n_mod) + b_mod
    fmod_row = lax.dynamic_index_in_dim(fmod_all, dev, axis=1, keepdims=False).reshape(1, N_CHIPS * n_fmod) + b_fmod[None, :]
    vecs = jnp.concatenate([mod_row.reshape(9, D), fmod_row.reshape(2, D), g_ffn1, g_mix, g_ffn2, g_final[None, :],
                            jnp.zeros((1, D), F32)], axis=0)
    lnv = jnp.concatenate([ln_g, ln_b, jnp.zeros((SUBLANES - 2, M), F32)], axis=0)
    bda = _block_diag(w_a[0]).astype(BF)
    bdi = _block_diag(w_i[0]).astype(BF)

    def reduce_add(gs, recv, ws, tag, kinds_=kinds):
        pairs = [add_sibling_half(g, r_, k, w.shape, core_chip, f"add_sibling_{tag}{j}")
                 for j, (g, r_, k, w) in enumerate(zip(gs, recv, kinds_, ws))]
        return [p[0] for p in pairs], [p[1] for p in pairs]

    def reduce_sum(hs_, recv, ws, tag, kinds_=kinds):
        return [sum_chip_pieces(h_, p_, f"sum_chips_{tag}{j}") for j, (h_, p_) in enumerate(zip(hs_, recv))]

    rows1 = (R_SH1, R_SC1, R_GT1, R_G1)
    rows3 = (R_SH3, R_SC3, R_GT3, R_G3)
    (wi1,) = place_local_shards(got_f1i, b_f1[:1], kinds[:1], "place_ffn1_in")
    g1s, u1s, a1s = ffn_fwd_in(xs, vecs, wi1, rows1, "ffn1_fwd_in")
    (wo1,) = place_local_shards(_after(got_f1o, a1s), b_f1[1:], kinds[1:], "place_ffn1_out")
    x1, y1 = ffn_fwd_out(a1s, xs, vecs, wo1, rows1, "ffn1_fwd_out")
    win, wout = place_local_shards(_after(got_mx, x1), b_mx, kinds, "place_mix")
    proj = norm_matmul(x1, vecs, win, (R_SH2, R_SC2, R_G2), "mix_in_proj")
    cq = conv_fwd(proj, convw4, conv_b, "conv_fwd")
    xr, ra, ii, hh = rnn_fwd(proj, rnnw4, rnn_conv_b, bda, bdi, b_a, b_i, lru_lambda, "rnn_fwd")
    x2, ym, ycat = mix_out(cq, proj, hh, x1, vecs, lnv, wout, "mix_out")
    wi2, wo2 = place_local_shards(_after(got_f2, x2), b_f2, kinds, "place_ffn2")
    dx3, g2s, u2s, y2, vgf = ffn_fwd(x2, vecs, wi2, wo2, rows3, "ffn2_fwd", final_tgt=tgt)

    Fd = wo1.shape[0]
    tk = min(2048, S)
    dx2, act2, dg2, du2, h3b, dy2b, vg3 = ffn_bwd(dx3, x2, vecs, g2s, u2s, y2, wi2, wo2, rows3, "ffn2_bwd")
    gwo2 = matmul(act2, dy2b, "tn", tm=Fd // 2, tn=D, tk=tk, out_dtype=BF, name="ffn2_dwo")
    gwi2 = matmul(h3b, dg2, "tn", tm=D, tn=Fd // 2, tk=tk, out_dtype=BF, name="ffn2_dwg", out_cols=2 * Fd)
    gwi2 = matmul(h3b, du2, "tn", tm=D, tn=Fd // 2, tk=tk, out_dtype=BF, name="ffn2_dwu", out_cols=2 * Fd, col_off=Fd, prev=gwi2)
    recv1_f2 = exchange_sibling_halves([gwi2, gwo2], kinds, shapes_of(w_f2), "reduce1_ffn2", seq_id=3)
    dcq, dhout, duy, dymb, vgd, vgm = mix_out_bwd(dx2, ym, vecs, wout, cq, lnv, proj, hh, "mix_out_bwd")
    gwout = matmul(ycat, dymb, "tn", tm=2 * M, tn=D, tk=tk, out_dtype=BF, name="mix_dwout")
    recv1_f2 = _after(recv1_f2, gwout)
    h_f2, hb_f2 = reduce_add([gwi2, gwo2], recv1_f2, w_f2, "ffn2_")
    recv2_f2 = exchange_chip_pieces(hb_f2, kinds, shapes_of(w_f2), "reduce2_ffn2", seq_id=4)
    duv, dug, dconvw4, dconvb = conv_bwd(_after([dcq], hb_f2)[0], proj, convw4, "conv_bwd")
    dux, dwa4, dwi4, drnnw4, rvec = rnn_bwd(dhout, hh, xr, ra, ii, proj, rnnw4, bda, bdi, lru_lambda, "rnn_bwd")
    dx1, h2b, dpb, vg2 = mix_in_bwd((duv, dug, dux, duy), x1, dx2, vecs, win, "mix_in_bwd")
    gwin = matmul(h2b, dpb, "tn", tm=D, tn=1024, tk=tk, out_dtype=BF, name="mix_dwin")
    recv1_mx = exchange_sibling_halves([gwin, gwout], kinds, shapes_of(w_mx), "reduce1_mix", seq_id=5)
    q_f2 = reduce_sum(_after(h_f2, gwin), recv2_f2, w_f2, "ffn2_")
    r_f2 = exchange_reduced_halves(q_f2, "reduce3_ffn2", seq_id=14)
    h_mx, hb_mx = reduce_add([gwin, gwout], _after(recv1_mx, q_f2), w_mx, "mix_")
    recv2_mx = exchange_chip_pieces(hb_mx, kinds, shapes_of(w_mx), "reduce2_mix", seq_id=6)
    dx0, act1, dg1, du1, h1b, dy1b, vg1 = ffn_bwd(_after([dx1], hb_mx)[0], xs, vecs, g1s, u1s, y1, wi1, wo1, rows1, "ffn1_bwd")
    dmod_row = jnp.concatenate([vg1[1:3], vg1[0:1], vg2[0:2], vgd[0:1], vg3[1:3], vg3[0:1]], axis=0)
    gains = jnp.concatenate([vg1[3:4], vg2[2:3], vg3[3:4], vgf[2:4]], axis=0)
    mvecs = jnp.concatenate([dconvb, vgm[0:2], rvec[0:4], jnp.zeros((1, M), F32)], axis=0)
    parts = [_rows128(dmod_row), _rows128(vgf[0:2]), _rows128(gains), _rows128(mvecs),
             _rows128(dconvw4), _rows128(drnnw4), _rows128(_diag_blocks(dwa4, H, HD)), _rows128(_diag_blocks(dwi4, H, HD))]
    sizes = [p.shape[0] for p in parts]
    packed = jnp.concatenate(parts, axis=0)
    gathered = allgather_devices_hbm(packed, "gather_small", seq_id=10)

    gwo1 = matmul(_after([act1], recv2_mx, packed)[0], dy1b, "tn", tm=Fd // 2, tn=D, tk=tk, out_dtype=BF, name="ffn1_dwo")
    w_f1o, w_f1i = w_f1[1:], w_f1[:1]
    recv1_f1o = exchange_sibling_halves([gwo1], ["row"], shapes_of(w_f1o), "reduce1_ffn1_out", seq_id=7)
    q_mx = reduce_sum(_after(h_mx, gwo1), recv2_mx, w_mx, "mix_")
    r_mx = exchange_reduced_halves(q_mx, "reduce3_mix", seq_id=15)
    gwi1 = matmul(_after([h1b], q_mx)[0], dg1, "tn", tm=D, tn=Fd // 2, tk=tk, out_dtype=BF, name="ffn1_dwg", out_cols=2 * Fd)
    h_f1o, hb_f1o = reduce_add([gwo1], _after(recv1_f1o, gwi1), w_f1o, "ffn1_out", ["row"])
    recv2_f1o = exchange_chip_pieces(hb_f1o, ["row"], shapes_of(w_f1o), "reduce2_ffn1_out", seq_id=11)
    gwi1 = matmul(h1b, _after([du1], hb_f1o, gathered)[0], "tn", tm=D, tn=Fd // 2, tk=tk, out_dtype=BF, name="ffn1_dwu", out_cols=2 * Fd,
                  col_off=Fd, prev=gwi1)
    recv1_f1i = exchange_sibling_halves([gwi1], ["col"], shapes_of(w_f1i), "reduce1_ffn1_in", seq_id=12)
    q_f1o = reduce_sum(_after(h_f1o, gwi1), recv2_f1o, w_f1o, "ffn1_out", ["row"])
    r_f1o = exchange_reduced_halves(q_f1o, "reduce3_ffn1_out", seq_id=16)
    summed = sum_slots(gathered, "sum_small")
    offs = [0]
    for s in sizes:
        offs.append(offs[-1] + s)
    seg = lambda k: summed[offs[k]:offs[k + 1]]
    g_b_mod = seg(0).reshape(1, 9 * D)
    g_b_fmod = seg(1).reshape(1, 2 * D)
    gsum = seg(2).reshape(5, D)
    loss = (0.5 / D) * jnp.sum(gsum[4])
    msum = seg(3).reshape(SUBLANES, M)
    g_conv_w = lax.dynamic_index_in_dim(seg(4).reshape(nb, 32, LANES), chip, axis=0, keepdims=False)[:CONV_WIDTH]
    g_rnn_w = lax.dynamic_index_in_dim(seg(5).reshape(nb, SUBLANES, LANES), chip, axis=0, keepdims=False)[:RNN_CONV_WIDTH]
    g_w_a = seg(6).reshape(H, HD, HD)
    g_w_i = seg(7).reshape(H, HD, HD)
    dmod_all = gathered[:, offs[0]:offs[1]].reshape(N_DEV, 9 * D)
    dfmod_all = gathered[:, offs[1]:offs[2]].reshape(N_DEV, 2 * D)
    dmod_cols = lax.dynamic_slice_in_dim(dmod_all, chip * n_mod, n_mod, axis=1)
    dfmod_cols = lax.dynamic_slice_in_dim(dfmod_all, chip * n_fmod, n_fmod, axis=1)

    h_f1i, hb_f1i = reduce_add([gwi1], _after(recv1_f1i, q_f1o), w_f1i, "ffn1_in", ["col"])
    recv2_f1i = exchange_chip_pieces(hb_f1i, ["col"], shapes_of(w_f1i), "reduce2_ffn1_in", seq_id=8)
    dmod_cols, dfmod_cols = _after([dmod_cols, dfmod_cols], hb_f1i)
    g_w_mod, d_w_mod, nm_w_mod, nv_w_mod = adam_cond(c_all, dmod_cols, w_mod[0], m_w_mod[0], v_w_mod[0], "adam_w_mod")
    g_w_fmod, d_w_fmod, nm_w_fmod, nv_w_fmod = adam_cond(c_all, dfmod_cols, w_fmod, m_w_fmod, v_w_fmod, "adam_w_fmod")

    def adam_group(ws, qs_, rs_, ms, vs, tags, after):
        qs_ = _after(list(qs_), *after) if after else list(qs_)
        return [adam_big(w, q_, r_, m, v, cidx, "adam_" + t) for w, q_, r_, m, v, t in zip(ws, qs_, rs_, ms, vs, tags)]

    ad_f2 = adam_group(w_f2, q_f2, r_f2, [m_w_ffn2_in[0], m_w_ffn2_out[0]], [v_w_ffn2_in[0], v_w_ffn2_out[0]],
                       ["ffn2_in", "ffn2_out"], [hb_f1i])
    ad_mx = adam_group(w_mx, q_mx, r_mx, [m_w_in[0], m_w_out[0]], [v_w_in[0], v_w_out[0]], ["w_in", "w_out"], [hb_f1i])
    ad_f1o = adam_group(w_f1o, q_f1o, r_f1o, [m_w_ffn1_out[0]], [v_w_ffn1_out[0]], ["ffn1_out"], [hb_f1i])
    q_f1i = reduce_sum(_after(h_f1i, ad_f2[0][0], ad_f2[1][0], ad_mx[0][0], ad_mx[1][0], ad_f1o[0][0], g_w_mod, g_w_fmod),
                       recv2_f1i, w_f1i, "ffn1_in", ["col"])
    r_f1i = exchange_reduced_halves(q_f1i, "reduce3_ffn1_in", seq_id=17)
    ad_f1i = adam_group(w_f1i, q_f1i, r_f1i, [m_w_ffn1_in[0]], [v_w_ffn1_in[0]], ["ffn1_in"], [])
    big_out = ad_f1i + ad_f1o + ad_mx + ad_f2

    flat2 = lambda a: a.reshape(-1, a.shape[-1])
    small_names = ["b_mod", "g_ffn1", "g_mix", "conv_w", "conv_b", "ln_g", "ln_b", "rnn_conv_w", "rnn_conv_b", "w_a", "b_a",
                   "w_i", "b_i", "lru_lambda", "g_ffn2", "b_fmod", "g_final"]
    small_w = [b_mod, g_ffn1, g_mix, conv_w, conv_b, ln_g, ln_b, rnn_conv_w, rnn_conv_b, w_a, b_a, w_i, b_i, lru_lambda,
               g_ffn2, b_fmod, g_final]
    small_m = [m_b_mod, m_g_ffn1, m_g_mix, m_conv_w, m_conv_b, m_ln_g, m_ln_b, m_rnn_conv_w, m_rnn_conv_b, m_w_a, m_b_a,
               m_w_i, m_b_i, m_lru_lambda, m_g_ffn2, m_b_fmod, m_g_final]
    small_v = [v_b_mod, v_g_ffn1, v_g_mix, v_conv_w, v_conv_b, v_ln_g, v_ln_b, v_rnn_conv_w, v_rnn_conv_b, v_w_a, v_b_a,
               v_w_i, v_b_i, v_lru_lambda, v_g_ffn2, v_b_fmod, v_g_final]
    small_g = [g_b_mod, gsum[0:1], gsum[1:2], g_conv_w, msum[0:1], msum[1:2], msum[2:3], g_rnn_w, msum[3:4], g_w_a, msum[4:5],
               g_w_i, msum[5:6], msum[6:7], gsum[2:3], g_b_fmod, gsum[3:4]]
    small_g = [g.reshape(w.shape) for g, w in zip(small_g, small_w)]
    two_d = lambda a: a.reshape(1, -1) if a.ndim == 1 else flat2(a)
    sd, sm, sv = adam_small([two_d(a) for a in small_w], [two_d(a) for a in small_g], [two_d(a) for a in small_m],
                            [two_d(a) for a in small_v], "adam_small")
    small = {}
    for k, nm in enumerate(small_names):
        shp = small_w[k].shape
        small[nm] = (small_g[k], sd[k].reshape(shp), sm[k].reshape(shp), sv[k].reshape(shp))

    big = {"w_mod": tuple(a[None] for a in (g_w_mod, d_w_mod, nm_w_mod, nv_w_mod)),
           "w_fmod": (g_w_fmod, d_w_fmod, nm_w_fmod, nv_w_fmod)}
    for nm, res in zip(["w_ffn1_in", "w_ffn1_out", "w_in", "w_out", "w_ffn2_in", "w_ffn2_out"], big_out):
        big[nm] = tuple(a[None] for a in res)
    order = ["w_mod", "b_mod", "g_ffn1", "w_ffn1_in", "w_ffn1_out", "g_mix", "w_in", "conv_w", "conv_b", "ln_g", "ln_b",
             "rnn_conv_w", "rnn_conv_b", "w_a", "b_a", "w_i", "b_i", "lru_lambda", "w_out", "g_ffn2", "w_ffn2_in",
             "w_ffn2_out", "w_fmod", "b_fmod", "g_final"]
    table = {**small, **big}
    outs = [loss, dx0[None]]
    for kind_ in range(4):
        outs.extend(table[nm][kind_] for nm in order)
    return tuple(outs)
```

```python
import functools

import jax
import jax.numpy as jnp
from jax import lax
from jax.experimental import pallas as pl
from jax.experimental.pallas import tpu as pltpu
from jax.experimental.pallas import tpu_sc as plsc

F32 = jnp.float32
BF = jnp.bfloat16
I32 = jnp.int32
MESH = pl.DeviceIdType.MESH

EPS = 1e-6
RG_C = 8.0
MACARON_W = 0.5
CONV_WIDTH = 31
RNN_CONV_WIDTH = 4
ADAM_LR = 0.001
ADAM_B1 = 0.9
ADAM_B2 = 0.999
ADAM_EPS = 1e-08
ADAM_WD = 0.01
ADAM_STEP = 10

LANES = 128
SUBLANES = 8
VMEM_LIMIT = 62 * 1024 * 1024
N_CHIPS = 4
N_DEV = 8

R_SH1, R_SC1, R_GT1, R_SH2, R_SC2, R_GT2, R_SH3, R_SC3, R_GT3, R_FSH, R_FSC, R_G1, R_G2, R_G3, R_GF = range(15)

CONTRACT_LAST = (((1,), (1,)), ((), ()))
CONTRACT_FIRST = (((0,), (0,)), ((), ()))


def _pcall(body, **kw):
    return pl.pallas_call(body, **kw)


def _params(sem=None, vmem=VMEM_LIMIT):
    if sem is None:
        return pltpu.CompilerParams(vmem_limit_bytes=vmem)
    return pltpu.CompilerParams(dimension_semantics=sem, vmem_limit_bytes=vmem)


def _row(ref, r):
    return ref[r:r + 1, :]


def _sigmoid(x):
    return 1.0 / (1.0 + jnp.exp(-x))


def _colsum(x):
    return jnp.sum(x, axis=0, keepdims=True)


def _rowmean(x):
    return jnp.mean(x, axis=-1, keepdims=True)


def matmul(a, b, mode, *, tm, tn, tk, name, out_dtype=F32, out_cols=None, col_off=0, prev=None):
    if mode == "nn":
        (M, K), (K2, N) = a.shape, b.shape
    elif mode == "nt":
        (M, K), (N, K2) = a.shape, b.shape
    else:
        (K, M), (K2, N) = a.shape, b.shape
    assert K == K2 and M % tm == 0 and N % tn == 0 and K % tk == 0 and col_off % tn == 0
    nk = K // tk
    out_cols = N if out_cols is None else out_cols
    off = col_off // tn

    def body(*refs):
        if prev is None:
            a_ref, b_ref, o_ref, acc = refs
        else:
            a_ref, b_ref, _, o_ref, acc = refs
        k = pl.program_id(2)
        av = a_ref[...].astype(BF)
        bv = b_ref[...].astype(BF)
        if mode == "nn":
            part = jnp.dot(av, bv, preferred_element_type=F32)
        elif mode == "nt":
            part = lax.dot_general(av, bv, CONTRACT_LAST, preferred_element_type=F32)
        else:
            part = lax.dot_general(av, bv, CONTRACT_FIRST, preferred_element_type=F32)
        if nk == 1:
            o_ref[...] = part.astype(out_dtype)
            return

        @pl.when(k == 0)
        def _():
            acc[...] = part

        @pl.when((k > 0) & (k < nk - 1))
        def _():
            acc[...] += part

        @pl.when(k == nk - 1)
        def _():
            o_ref[...] = (acc[...] + part).astype(out_dtype)

    if mode == "nn":
        a_spec = pl.BlockSpec((tm, tk), lambda m, n, k: (m, k))
        b_spec = pl.BlockSpec((tk, tn), lambda m, n, k: (k, n))
    elif mode == "nt":
        a_spec = pl.BlockSpec((tm, tk), lambda m, n, k: (m, k))
        b_spec = pl.BlockSpec((tn, tk), lambda m, n, k: (n, k))
    else:
        a_spec = pl.BlockSpec((tk, tm), lambda m, n, k: (k, m))
        b_spec = pl.BlockSpec((tk, tn), lambda m, n, k: (k, n))
    in_specs = [a_spec, b_spec]
    args = [a, b]
    aliases = {}
    if prev is not None:
        in_specs.append(pl.BlockSpec(memory_space=pl.ANY))
        args.append(prev)
        aliases = {2: 0}
    return _pcall(
        body, name=name, grid=(M // tm, N // tn, nk), in_specs=in_specs,
        out_specs=pl.BlockSpec((tm, tn), lambda m, n, k: (m, n + off)),
        out_shape=jax.ShapeDtypeStruct((M, out_cols), out_dtype),
        scratch_shapes=[pltpu.VMEM((tm, tn), F32)], input_output_aliases=aliases,
        compiler_params=_params(("parallel", "parallel", "arbitrary")),
    )(*args)


def cond_matmul(c_all, w, name):
    B, K = c_all.shape
    N = w.shape[1]
    tn = 256
    assert N % tn == 0

    def body(c_ref, w_ref, o_ref):
        cv = c_ref[...]
        ca = cv * _sigmoid(cv)
        o_ref[...] = jnp.dot(ca.astype(BF), w_ref[...].astype(BF), preferred_element_type=F32)

    return _pcall(
        body, name=name, grid=(N // tn,),
        in_specs=[pl.BlockSpec((B, K), lambda n: (0, 0)), pl.BlockSpec((K, tn), lambda n: (0, n))],
        out_specs=pl.BlockSpec((B, tn), lambda n: (0, n)),
        out_shape=jax.ShapeDtypeStruct((B, N), F32), compiler_params=_params(("parallel",)),
    )(c_all, w)


FFN_FWD_TS = 512
FFN_BWD_TS = 256


def _resident(shape, index_map):
    return pl.BlockSpec(shape, index_map, pipeline_mode=pl.Buffered(1))


def _final_norm_loss_grad(xv, t, v_ref, vg_ref):
    D = xv.shape[-1]
    r = lax.rsqrt(_rowmean(xv * xv) + EPS)
    n = xv * r
    g = _row(v_ref, R_GF)
    sc1 = 1.0 + _row(v_ref, R_FSC)
    gsc = g * sc1
    e = n * gsc + _row(v_ref, R_FSH) - t
    vg_ref[3:4, :] += _colsum(e * e)
    dout = e * (1.0 / D)
    dn_ = dout * n
    vg_ref[0:1, :] += _colsum(dout)
    vg_ref[1:2, :] += _colsum(dn_) * g
    vg_ref[2:3, :] += _colsum(dn_) * sc1
    dn = dout * gsc
    return r * (dn - n * _rowmean(dn * n))


def ffn_fwd(x, vecs, wi, wo, rows, name, final_tgt=None):
    r_sh, r_sc, r_gt, r_g = rows
    S, D = x.shape
    Fd = wo.shape[0]
    ts = min(FFN_FWD_TS, S)
    with_final = final_tgt is not None

    def body(*refs):
        if with_final:
            x_ref, v_ref, wg_ref, wu_ref, wo_ref, t_ref, xo_ref, g_ref, u_ref, y_ref, vg_ref = refs
        else:
            x_ref, v_ref, wg_ref, wu_ref, wo_ref, xo_ref, g_ref, u_ref, y_ref = refs
        xv = x_ref[...]
        r = lax.rsqrt(_rowmean(xv * xv) + EPS)
        gs = _row(v_ref, r_g) * (1.0 + _row(v_ref, r_sc))
        hb = (xv * r * gs + _row(v_ref, r_sh)).astype(BF)
        G = jnp.dot(hb, wg_ref[...], preferred_element_type=F32)
        U = jnp.dot(hb, wu_ref[...], preferred_element_type=F32)
        g_ref[...] = G.astype(BF)
        u_ref[...] = U.astype(BF)
        act = (G * _sigmoid(G) * U).astype(BF)
        Y = jnp.dot(act, wo_ref[...], preferred_element_type=F32)
        y_ref[...] = Y.astype(BF)
        xo = xv + (MACARON_W * _row(v_ref, r_gt)) * Y
        if with_final:
            @pl.when(pl.program_id(0) == 0)
            def _():
                vg_ref[...] = jnp.zeros_like(vg_ref)

            xo_ref[...] = _final_norm_loss_grad(xo, t_ref[...], v_ref, vg_ref)
        else:
            xo_ref[...] = xo

    tok = pl.BlockSpec((ts, D), lambda i: (i, 0))
    hid = pl.BlockSpec((ts, Fd), lambda i: (i, 0))
    in_specs = [tok, pl.BlockSpec(vecs.shape, lambda i: (0, 0)), _resident((D, Fd), lambda i: (0, 0)),
                _resident((D, Fd), lambda i: (0, 1)), _resident((Fd, D), lambda i: (0, 0))]
    out_specs = [tok, hid, hid, tok]
    out_shape = [jax.ShapeDtypeStruct((S, D), F32), jax.ShapeDtypeStruct((S, Fd), BF),
                 jax.ShapeDtypeStruct((S, Fd), BF), jax.ShapeDtypeStruct((S, D), BF)]
    args = [x, vecs, wi, wi, wo]
    if with_final:
        in_specs.append(tok)
        args.append(final_tgt)
        out_specs.append(pl.BlockSpec((SUBLANES, D), lambda i: (0, 0)))
        out_shape.append(jax.ShapeDtypeStruct((SUBLANES, D), F32))
    return _pcall(body, name=name, grid=(S // ts,), in_specs=in_specs, out_specs=out_specs, out_shape=out_shape,
                  compiler_params=_params(("arbitrary",)))(*args)


def ffn_fwd_in(x, vecs, wi, rows, name):
    r_sh, r_sc, r_gt, r_g = rows
    S, D = x.shape
    Fd = wi.shape[1] // 2
    ts = min(FFN_FWD_TS, S)

    def body(x_ref, v_ref, wg_ref, wu_ref, g_ref, u_ref, a_ref):
        xv = x_ref[...]
        r = lax.rsqrt(_rowmean(xv * xv) + EPS)
        gs = _row(v_ref, r_g) * (1.0 + _row(v_ref, r_sc))
        hb = (xv * r * gs + _row(v_ref, r_sh)).astype(BF)
        G = jnp.dot(hb, wg_ref[...], preferred_element_type=F32)
        U = jnp.dot(hb, wu_ref[...], preferred_element_type=F32)
        g_ref[...] = G.astype(BF)
        u_ref[...] = U.astype(BF)
        a_ref[...] = (G * _sigmoid(G) * U).astype(BF)

    hid = pl.BlockSpec((ts, Fd), lambda i: (i, 0))
    return _pcall(
        body, name=name, grid=(S // ts,),
        in_specs=[pl.BlockSpec((ts, D), lambda i: (i, 0)), pl.BlockSpec(vecs.shape, lambda i: (0, 0)),
                  _resident((D, Fd), lambda i: (0, 0)), _resident((D, Fd), lambda i: (0, 1))],
        out_specs=[hid, hid, hid], out_shape=[jax.ShapeDtypeStruct((S, Fd), BF)] * 3,
        compiler_params=_params(("arbitrary",)),
    )(x, vecs, wi, wi)


def ffn_fwd_out(act, x, vecs, wo, rows, name):
    r_sh, r_sc, r_gt, r_g = rows
    S, D = x.shape
    Fd = wo.shape[0]
    ts = min(FFN_FWD_TS, S)

    def body(a_ref, x_ref, v_ref, wo_ref, xo_ref, y_ref):
        Y = jnp.dot(a_ref[...], wo_ref[...], preferred_element_type=F32)
        y_ref[...] = Y.astype(BF)
        xo_ref[...] = x_ref[...] + (MACARON_W * _row(v_ref, r_gt)) * Y

    tok = pl.BlockSpec((ts, D), lambda i: (i, 0))
    return _pcall(
        body, name=name, grid=(S // ts,),
        in_specs=[pl.BlockSpec((ts, Fd), lambda i: (i, 0)), tok, pl.BlockSpec(vecs.shape, lambda i: (0, 0)),
                  _resident((Fd, D), lambda i: (0, 0))],
        out_specs=[tok, tok], out_shape=[jax.ShapeDtypeStruct((S, D), F32), jax.ShapeDtypeStruct((S, D), BF)],
        compiler_params=_params(("arbitrary",)),
    )(act, x, vecs, wo)


def ffn_bwd(dxo, x, vecs, gs_, us_, y, wi, wo, rows, name):
    r_sh, r_sc, r_gt, r_g = rows
    S, D = x.shape
    Fd = wo.shape[0]
    ts = min(FFN_BWD_TS, S)

    def body(dxo_ref, x_ref, v_ref, g_ref, u_ref, y_ref, wg_ref, wu_ref, wo_ref,
             dx_ref, act_ref, dg_ref, du_ref, hb_ref, dyb_ref, vg_ref):
        @pl.when(pl.program_id(0) == 0)
        def _():
            vg_ref[...] = jnp.zeros_like(vg_ref)

        dxo_v = dxo_ref[...]
        dyb = ((MACARON_W * _row(v_ref, r_gt)) * dxo_v).astype(BF)
        dyb_ref[...] = dyb
        vg_ref[0:1, :] += MACARON_W * _colsum(dxo_v * y_ref[...].astype(F32))
        dA = lax.dot_general(dyb, wo_ref[...], CONTRACT_LAST, preferred_element_type=F32)
        G = g_ref[...].astype(F32)
        U = u_ref[...].astype(F32)
        sg = _sigmoid(G)
        sl = G * sg
        dU = (dA * sl).astype(BF)
        dG = (dA * U * (sg * (1.0 + G * (1.0 - sg)))).astype(BF)
        act_ref[...] = (sl * U).astype(BF)
        dg_ref[...] = dG
        du_ref[...] = dU
        dh = (lax.dot_general(dG, wg_ref[...], CONTRACT_LAST, preferred_element_type=F32)
              + lax.dot_general(dU, wu_ref[...], CONTRACT_LAST, preferred_element_type=F32))
        xv = x_ref[...]
        r = lax.rsqrt(_rowmean(xv * xv) + EPS)
        n = xv * r
        g = _row(v_ref, r_g)
        sc1 = 1.0 + _row(v_ref, r_sc)
        gsc = g * sc1
        hb_ref[...] = (n * gsc + _row(v_ref, r_sh)).astype(BF)
        dhn = dh * n
        vg_ref[1:2, :] += _colsum(dh)
        vg_ref[2:3, :] += _colsum(dhn) * g
        vg_ref[3:4, :] += _colsum(dhn) * sc1
        dn = dh * gsc
        dx_ref[...] = dxo_v + r * (dn - n * _rowmean(dn * n))

    tok = pl.BlockSpec((ts, D), lambda i: (i, 0))
    hid = pl.BlockSpec((ts, Fd), lambda i: (i, 0))
    return _pcall(
        body, name=name, grid=(S // ts,),
        in_specs=[tok, tok, pl.BlockSpec(vecs.shape, lambda i: (0, 0)), hid, hid, tok, _resident((D, Fd), lambda i: (0, 0)),
                  _resident((D, Fd), lambda i: (0, 1)), _resident((Fd, D), lambda i: (0, 0))],
        out_specs=[tok, hid, hid, hid, tok, tok, pl.BlockSpec((SUBLANES, D), lambda i: (0, 0))],
        out_shape=[jax.ShapeDtypeStruct((S, D), F32), jax.ShapeDtypeStruct((S, Fd), BF),
                   jax.ShapeDtypeStruct((S, Fd), BF), jax.ShapeDtypeStruct((S, Fd), BF),
                   jax.ShapeDtypeStruct((S, D), BF), jax.ShapeDtypeStruct((S, D), BF),
                   jax.ShapeDtypeStruct((SUBLANES, D), F32)],
        compiler_params=_params(("arbitrary",)),
    )(dxo, x, vecs, gs_, us_, y, wi, wi, wo)


def norm_matmul(x, vecs, w, rows, name):
    r_sh, r_sc, r_g = rows
    S, D = x.shape
    N = w.shape[1]
    ts = min(512, S)

    def body(x_ref, v_ref, w_ref, o_ref):
        xv = x_ref[...]
        r = lax.rsqrt(_rowmean(xv * xv) + EPS)
        gs = _row(v_ref, r_g) * (1.0 + _row(v_ref, r_sc))
        hb = (xv * r * gs + _row(v_ref, r_sh)).astype(BF)
        o_ref[...] = jnp.dot(hb, w_ref[...], preferred_element_type=F32).astype(BF)

    return _pcall(
        body, name=name, grid=(S // ts,),
        in_specs=[pl.BlockSpec((ts, D), lambda i: (i, 0)), pl.BlockSpec(vecs.shape, lambda i: (0, 0)),
                  _resident((D, N), lambda i: (0, 0))],
        out_specs=pl.BlockSpec((ts, N), lambda i: (i, 0)),
        out_shape=jax.ShapeDtypeStruct((S, N), BF),
        compiler_params=_params(("arbitrary",)),
    )(x, vecs, w)


SEQ_TT = 256
SCAN_SEGMENTS = 4
CONV_PAD = 32


def conv_fwd(proj, convw4, conv_b, name):
    S = proj.shape[0]
    M = conv_b.shape[1]
    nb = M // LANES
    tt = min(SEQ_TT, S)

    def body(uv_ref, ug_ref, w_ref, b_ref, cq_ref, qp):
        qp[0:CONV_PAD, :] = jnp.zeros((CONV_PAD, LANES), F32)

        def step(t, carry):
            base = pl.multiple_of(t * tt, tt)
            qp[pl.ds(base + CONV_PAD, tt), :] = uv_ref[pl.ds(base, tt), :].astype(F32) * _sigmoid(ug_ref[pl.ds(base, tt), :].astype(F32))
            acc = jnp.broadcast_to(b_ref[...], (tt, LANES))
            for k in range(CONV_WIDTH):
                acc = acc + w_ref[k:k + 1, :] * qp[pl.ds(base + (CONV_PAD - CONV_WIDTH + 1) + k, tt), :]
            cq_ref[pl.ds(base, tt), :] = acc
            return carry

        lax.fori_loop(0, S // tt, step, 0)

    return _pcall(
        body, name=name, grid=(nb,),
        in_specs=[pl.BlockSpec((S, LANES), lambda c: (0, c)), pl.BlockSpec((S, LANES), lambda c: (0, c + nb)),
                  pl.BlockSpec((None, 32, LANES), lambda c: (c, 0, 0)), pl.BlockSpec((1, LANES), lambda c: (0, c))],
        out_specs=pl.BlockSpec((S, LANES), lambda c: (0, c)),
        out_shape=jax.ShapeDtypeStruct((S, M), F32),
        scratch_shapes=[pltpu.VMEM((S + CONV_PAD, LANES), F32)],
        compiler_params=_params(("arbitrary",)),
    )(proj, proj, convw4, conv_b)


def conv_bwd(dcq, proj, convw4, name):
    S, M = dcq.shape
    nb = M // LANES
    tt = min(SEQ_TT, S)

    def body(dcq_ref, uv_ref, ug_ref, w_ref, duv_ref, dug_ref, dw_ref, db_ref, dp, dw8, db8):
        dp[S:S + CONV_PAD, :] = jnp.zeros((CONV_PAD, LANES), F32)
        dw8[...] = jnp.zeros_like(dw8)
        db8[...] = jnp.zeros_like(db8)

        def fill(t, carry):
            base = pl.multiple_of(t * tt, tt)
            dp[pl.ds(base, tt), :] = dcq_ref[pl.ds(base, tt), :].astype(F32)
            return carry

        lax.fori_loop(0, S // tt, fill, 0)

        def step(t, carry):
            base = pl.multiple_of(t * tt, tt)
            uv = uv_ref[pl.ds(base, tt), :].astype(F32)
            sg = _sigmoid(ug_ref[pl.ds(base, tt), :].astype(F32))
            q_t = uv * sg
            db8[...] += dp[pl.ds(base, tt), :].reshape(tt // SUBLANES, SUBLANES, LANES).sum(axis=0)
            dq = jnp.zeros((tt, LANES), F32)
            for k in range(CONV_WIDTH):
                shifted = dp[pl.ds(base + (CONV_WIDTH - 1) - k, tt), :]
                dw8[k] += (shifted * q_t).reshape(tt // SUBLANES, SUBLANES, LANES).sum(axis=0)
                dq = dq + w_ref[k:k + 1, :] * shifted
            duv_ref[pl.ds(base, tt), :] = (dq * sg).astype(BF)
            dug_ref[pl.ds(base, tt), :] = (dq * uv * sg * (1.0 - sg)).astype(BF)
            return carry

        lax.fori_loop(0, S // tt, step, 0)
        dw_ref[...] = jnp.zeros_like(dw_ref)
        for k in range(CONV_WIDTH):
            dw_ref[k:k + 1, :] = _colsum(dw8[k])
        db_ref[...] = _colsum(db8[...])

    col = lambda o: pl.BlockSpec((S, LANES), lambda c: (0, c + o))
    return _pcall(
        body, name=name, grid=(nb,),
        in_specs=[col(0), col(0), col(nb), pl.BlockSpec((None, 32, LANES), lambda c: (c, 0, 0))],
        out_specs=[col(0), col(0), pl.BlockSpec((None, 32, LANES), lambda c: (c, 0, 0)),
                   pl.BlockSpec((1, LANES), lambda c: (0, c))],
        out_shape=[jax.ShapeDtypeStruct((S, M), BF), jax.ShapeDtypeStruct((S, M), BF),
                   jax.ShapeDtypeStruct((nb, 32, LANES), F32), jax.ShapeDtypeStruct((1, M), F32)],
        scratch_shapes=[pltpu.VMEM((S + CONV_PAD, LANES), F32),
                        pltpu.VMEM((32, SUBLANES, LANES), F32), pltpu.VMEM((SUBLANES, LANES), F32)],
        compiler_params=_params(("arbitrary",)),
    )(dcq, proj, proj, convw4)


def _log_sigmoid(x):
    return jnp.minimum(x, 0.0) - jnp.log(1.0 + jnp.exp(-jnp.abs(x)))


def _rg_gate_terms(ra, ls):
    la = RG_C * ra * ls
    a = jnp.exp(la)
    th = jnp.tanh(la)
    mult = jnp.sqrt(-2.0 * th / (1.0 - th))
    return a, mult


def rnn_fwd(proj, rnnw4, rnn_b, bda, bdi, b_a, b_i, lam, name):
    S = proj.shape[0]
    M = rnn_b.shape[1]
    nb = M // LANES
    tt = min(SEQ_TT, S)
    KW = RNN_CONV_WIDTH
    nseg = SCAN_SEGMENTS if S % (SCAN_SEGMENTS * tt) == 0 else 1

    def body(ux_ref, w_ref, rb_ref, bda_ref, bdi_ref, ba_ref, bi_ref, lam_ref,
             xr_ref, ra_ref, ii_ref, h_ref, uxp, a_sc, b_sc):
        uxp[0:SUBLANES, :] = jnp.zeros((SUBLANES, LANES), F32)
        ls = _log_sigmoid(lam_ref[...])

        def step(t, carry):
            base = pl.multiple_of(t * tt, tt)
            uxp[pl.ds(base + SUBLANES, tt), :] = ux_ref[pl.ds(base, tt), :].astype(F32)
            xr = jnp.broadcast_to(rb_ref[...], (tt, LANES))
            for k in range(KW):
                xr = xr + w_ref[k:k + 1, :] * uxp[pl.ds(base + (SUBLANES - KW + 1) + k, tt), :]
            xb = xr.astype(BF)
            ra = _sigmoid(jnp.dot(xb, bda_ref[...], preferred_element_type=F32) + ba_ref[...])
            ii = _sigmoid(jnp.dot(xb, bdi_ref[...], preferred_element_type=F32) + bi_ref[...])
            a, mult = _rg_gate_terms(ra, ls)
            xr_ref[pl.ds(base, tt), :] = xr
            ra_ref[pl.ds(base, tt), :] = ra
            ii_ref[pl.ds(base, tt), :] = ii
            a_sc[pl.ds(base, tt), :] = a
            b_sc[pl.ds(base, tt), :] = mult * (ii * xr)
            return carry

        lax.fori_loop(0, S // tt, step, 0)

        rows = lax.broadcasted_iota(I32, (SUBLANES, LANES), 0)
        seg = S // nseg
        last = lambda v: jnp.broadcast_to(v[SUBLANES - 1:SUBLANES, :], (SUBLANES, LANES))

        def scan(t, carry):
            hs, ps = carry
            new_h, new_p = [], []
            for s in range(nseg):
                base = pl.multiple_of(s * seg + t * SUBLANES, SUBLANES)
                A = a_sc[pl.ds(base, SUBLANES), :]
                B = b_sc[pl.ds(base, SUBLANES), :]
                for d in (1, 2, 4):
                    As = jnp.where(rows >= d, pltpu.roll(A, d, axis=0), 1.0)
                    Bs = jnp.where(rows >= d, pltpu.roll(B, d, axis=0), 0.0)
                    B = A * Bs + B
                    A = A * As
                hh = B + A * hs[s]
                h_ref[pl.ds(base, SUBLANES), :] = hh
                pp = A * ps[s]
                if s > 0:
                    a_sc[pl.ds(base, SUBLANES), :] = pp
                new_h.append(last(hh))
                new_p.append(last(pp))
            return tuple(new_h), tuple(new_p)

        zero8 = jnp.zeros((SUBLANES, LANES), F32)
        one8 = jnp.ones((SUBLANES, LANES), F32)
        hs, ps = lax.fori_loop(0, seg // SUBLANES, scan, ((zero8,) * nseg, (one8,) * nseg))
        carry_in = hs[0]
        for s in range(1, nseg):
            c_row = carry_in[0:1, :]

            def fix(t, c, s=s, c_row=c_row):
                base = pl.multiple_of(s * seg + t * tt, tt)
                h_ref[pl.ds(base, tt), :] = h_ref[pl.ds(base, tt), :] + a_sc[pl.ds(base, tt), :] * c_row
                return c

            lax.fori_loop(0, seg // tt, fix, 0)
            carry_in = hs[s] + ps[s] * carry_in

    col = lambda o: pl.BlockSpec((S, LANES), lambda c: (0, c + o))
    vec = pl.BlockSpec((1, LANES), lambda c: (0, c))
    diag = pl.BlockSpec((LANES, LANES), lambda c: (c, c))
    return _pcall(
        body, name=name, grid=(nb,),
        in_specs=[col(2 * nb), pl.BlockSpec((None, SUBLANES, LANES), lambda c: (c, 0, 0)), vec, diag, diag, vec, vec, vec],
        out_specs=[col(0)] * 4,
        out_shape=[jax.ShapeDtypeStruct((S, M), F32)] * 4,
        scratch_shapes=[pltpu.VMEM((S + SUBLANES, LANES), F32), pltpu.VMEM((S, LANES), F32), pltpu.VMEM((S, LANES), F32)],
        compiler_params=_params(("arbitrary",)),
    )(proj, rnnw4, rnn_b, bda, bdi, b_a, b_i, lam)


def rnn_bwd(dhout, h, xr, ra, ii, proj, rnnw4, bda, bdi, lam, name):
    S, M = h.shape
    nb = M // LANES
    tt = min(SEQ_TT, S)
    KW = RNN_CONV_WIDTH
    SL = SUBLANES
    nseg = SCAN_SEGMENTS if S % (SCAN_SEGMENTS * tt) == 0 else 1

    def body(dh_ref, h_ref, xr_ref, ra_ref, ii_ref, ux_ref, w_ref, bda_ref, bdi_ref, lam_ref,
             dux_ref, dwa_ref, dwi_ref, drw_ref, vec_ref,
             a_sc, hp, g_sc, dpa_sc, dpi_sc, dxp, uxp, acc8, drw8, p_sc):
        zero8 = jnp.zeros((SL, LANES), F32)
        a_sc[S:S + SL, :] = zero8
        hp[0:SL, :] = zero8
        dxp[S:S + SL, :] = zero8
        uxp[0:SL, :] = zero8
        acc8[...] = jnp.zeros_like(acc8)
        drw8[...] = jnp.zeros_like(drw8)
        lamv = lam_ref[...]
        ls = _log_sigmoid(lamv)

        def fill(t, carry):
            base = pl.multiple_of(t * tt, tt)
            a_sc[pl.ds(base, tt), :] = jnp.exp(RG_C * ra_ref[pl.ds(base, tt), :] * ls)
            hp[pl.ds(base + SL, tt), :] = h_ref[pl.ds(base, tt), :]
            uxp[pl.ds(base + SL, tt), :] = ux_ref[pl.ds(base, tt), :].astype(F32)
            return carry

        lax.fori_loop(0, S // tt, fill, 0)

        rows = lax.broadcasted_iota(I32, (SL, LANES), 0)
        seg = S // nseg
        nt8 = seg // SL
        first = lambda v: jnp.broadcast_to(v[0:1, :], (SL, LANES))

        def rscan(t, carry):
            gs, ps = carry
            new_g, new_p = [], []
            for s in range(nseg):
                base = pl.multiple_of(s * seg + (nt8 - 1 - t) * SL, SL)
                A = a_sc[pl.ds(base + 1, SL), :]
                B = dh_ref[pl.ds(base, SL), :]
                for d in (1, 2, 4):
                    As = jnp.where(rows < SL - d, pltpu.roll(A, SL - d, axis=0), 1.0)
                    Bs = jnp.where(rows < SL - d, pltpu.roll(B, SL - d, axis=0), 0.0)
                    B = A * Bs + B
                    A = A * As
                g = B + A * gs[s]
                g_sc[pl.ds(base, SL), :] = g
                pp = A * ps[s]
                if s < nseg - 1:
                    p_sc[pl.ds(base, SL), :] = pp
                new_g.append(first(g))
                new_p.append(first(pp))
            return tuple(new_g), tuple(new_p)

        one8 = jnp.ones((SL, LANES), F32)
        gs, ps = lax.fori_loop(0, nt8, rscan, ((zero8,) * nseg, (one8,) * nseg))
        carry_in = gs[nseg - 1]
        for s in range(nseg - 2, -1, -1):
            c_row = carry_in[0:1, :]

            def fix(t, c, s=s, c_row=c_row):
                base = pl.multiple_of(s * seg + t * tt, tt)
                g_sc[pl.ds(base, tt), :] = g_sc[pl.ds(base, tt), :] + p_sc[pl.ds(base, tt), :] * c_row
                return c

            lax.fori_loop(0, seg // tt, fix, 0)
            carry_in = gs[s] + ps[s] * carry_in

        def red8(v):
            return v.reshape(tt // SL, SL, LANES).sum(axis=0)

        def step(t, carry):
            base = pl.multiple_of(t * tt, tt)
            g = g_sc[pl.ds(base, tt), :]
            hprev = hp[pl.ds(base + SL - 1, tt), :]
            xr_t = xr_ref[pl.ds(base, tt), :]
            ra_t = ra_ref[pl.ds(base, tt), :]
            ii_t = ii_ref[pl.ds(base, tt), :]
            a, mult = _rg_gate_terms(ra_t, ls)
            gx = g * xr_t
            dmult = gx * ii_t
            dii = gx * mult
            dxr = g * (mult * ii_t)
            dla = g * hprev * a - dmult * (a * a) / mult
            acc8[3] += red8(dla * ra_t)
            dpa = dla * (RG_C * ls) * ra_t * (1.0 - ra_t)
            dpi = dii * ii_t * (1.0 - ii_t)
            dpab = dpa.astype(BF)
            dpib = dpi.astype(BF)
            dxr = dxr + (lax.dot_general(dpab, bda_ref[...], CONTRACT_LAST, preferred_element_type=F32)
                         + lax.dot_general(dpib, bdi_ref[...], CONTRACT_LAST, preferred_element_type=F32))
            dpa_sc[pl.ds(base, tt), :] = dpab
            dpi_sc[pl.ds(base, tt), :] = dpib
            dxp[pl.ds(base, tt), :] = dxr
            acc8[0] += red8(dxr)
            acc8[1] += red8(dpa)
            acc8[2] += red8(dpi)
            return carry

        lax.fori_loop(0, S // tt, step, 0)

        def convb(t, carry):
            base = pl.multiple_of(t * tt, tt)
            d_t = dxp[pl.ds(base, tt), :]
            dux = jnp.zeros((tt, LANES), F32)
            for k in range(KW):
                drw8[k] += red8(d_t * uxp[pl.ds(base + (SL - KW + 1) + k, tt), :])
                dux = dux + w_ref[k:k + 1, :] * dxp[pl.ds(base + (KW - 1) - k, tt), :]
            dux_ref[pl.ds(base, tt), :] = dux.astype(BF)
            return carry

        lax.fori_loop(0, S // tt, convb, 0)

        xb = xr_ref[...].astype(BF)
        dwa_ref[...] = lax.dot_general(xb, dpa_sc[...], CONTRACT_FIRST, preferred_element_type=F32)
        dwi_ref[...] = lax.dot_general(xb, dpi_sc[...], CONTRACT_FIRST, preferred_element_type=F32)
        drw_ref[...] = jnp.zeros_like(drw_ref)
        vec_ref[...] = jnp.zeros_like(vec_ref)
        for k in range(KW):
            drw_ref[k:k + 1, :] = _colsum(drw8[k])
        for k in range(3):
            vec_ref[k:k + 1, :] = _colsum(acc8[k])
        vec_ref[3:4, :] = _colsum(acc8[3]) * (RG_C * _sigmoid(-lamv))

    col = lambda o: pl.BlockSpec((S, LANES), lambda c: (0, c + o))
    vec = pl.BlockSpec((1, LANES), lambda c: (0, c))
    diag = pl.BlockSpec((LANES, LANES), lambda c: (c, c))
    blk3 = lambda r: pl.BlockSpec((None, r, LANES), lambda c: (c, 0, 0))
    return _pcall(
        body, name=name, grid=(nb,),
        in_specs=[col(0), col(0), col(0), col(0), col(0), col(2 * nb), blk3(SL), diag, diag, vec],
        out_specs=[col(0), blk3(LANES), blk3(LANES), blk3(SL), pl.BlockSpec((SL, LANES), lambda c: (0, c))],
        out_shape=[jax.ShapeDtypeStruct((S, M), BF), jax.ShapeDtypeStruct((nb, LANES, LANES), F32),
                   jax.ShapeDtypeStruct((nb, LANES, LANES), F32), jax.ShapeDtypeStruct((nb, SL, LANES), F32),
                   jax.ShapeDtypeStruct((SL, M), F32)],
        scratch_shapes=[pltpu.VMEM((S + SL, LANES), F32), pltpu.VMEM((S + SL, LANES), F32), pltpu.VMEM((S, LANES), F32),
                        pltpu.VMEM((S, LANES), BF), pltpu.VMEM((S, LANES), BF), pltpu.VMEM((S + SL, LANES), F32),
                        pltpu.VMEM((S + SL, LANES), F32), pltpu.VMEM((SL, SL, LANES), F32), pltpu.VMEM((SL, SL, LANES), F32),
                        pltpu.VMEM((S, LANES), F32)],
        compiler_params=_params(("arbitrary",)),
    )(dhout, h, xr, ra, ii, proj, rnnw4, bda, bdi, lam)


GELU_K = 0.7978845608028654
GELU_C = 0.044715


def _layernorm_parts(cq):
    mu = _rowmean(cq)
    d = cq - mu
    rstd = lax.rsqrt(_rowmean(d * d) + EPS)
    return d * rstd, rstd


def mix_out(cq, proj, h, x, vecs, lnv, wout, name):
    S, D = x.shape
    M = cq.shape[1]
    ts = min(512, S)

    def body(cq_ref, uy_ref, h_ref, x_ref, v_ref, ln_ref, w_ref, xo_ref, ym_ref, yc_ref):
        z, _ = _layernorm_parts(cq_ref[...])
        l = z * _row(ln_ref, 0) + _row(ln_ref, 1)
        yc_ref[:, 0:M] = (l * _sigmoid(l)).astype(BF)
        uy = uy_ref[...].astype(F32)
        gelu = 0.5 * uy * (1.0 + jnp.tanh(GELU_K * (uy + GELU_C * uy * uy * uy)))
        yc_ref[:, M:2 * M] = (gelu * h_ref[...]).astype(BF)
        ym = jnp.dot(yc_ref[...], w_ref[...], preferred_element_type=F32)
        ym_ref[...] = ym.astype(BF)
        xo_ref[...] = x_ref[...] + _row(v_ref, R_GT2) * ym

    tok = pl.BlockSpec((ts, D), lambda i: (i, 0))
    mtok = lambda o: pl.BlockSpec((ts, M), lambda i: (i, o))
    return _pcall(
        body, name=name, grid=(S // ts,),
        in_specs=[mtok(0), mtok(3), mtok(0), tok, pl.BlockSpec(vecs.shape, lambda i: (0, 0)),
                  pl.BlockSpec(lnv.shape, lambda i: (0, 0)), pl.BlockSpec(wout.shape, lambda i: (0, 0))],
        out_specs=[tok, tok, pl.BlockSpec((ts, 2 * M), lambda i: (i, 0))],
        out_shape=[jax.ShapeDtypeStruct((S, D), F32), jax.ShapeDtypeStruct((S, D), BF),
                   jax.ShapeDtypeStruct((S, 2 * M), BF)],
        compiler_params=_params(("arbitrary",)),
    )(cq, proj, h, x, vecs, lnv, wout)


def mix_out_bwd(dxo, ym, vecs, wout, cq, lnv, proj, h, name):
    S, D = dxo.shape
    M = cq.shape[1]
    ts = min(512, S)

    def body(dxo_ref, ym_ref, v_ref, w_ref, cq_ref, ln_ref, uy_ref, h_ref,
             dcq_ref, dh_ref, duy_ref, dyb_ref, vgd_ref, vgm_ref):
        @pl.when(pl.program_id(0) == 0)
        def _():
            vgd_ref[...] = jnp.zeros_like(vgd_ref)
            vgm_ref[...] = jnp.zeros_like(vgm_ref)

        dxo_v = dxo_ref[...]
        dyb = (_row(v_ref, R_GT2) * dxo_v).astype(BF)
        dyb_ref[...] = dyb
        vgd_ref[0:1, :] += _colsum(dxo_v * ym_ref[...].astype(F32))
        dycat = lax.dot_general(dyb, w_ref[...], CONTRACT_LAST, preferred_element_type=F32)
        dyc = dycat[:, 0:M]
        dyr = dycat[:, M:2 * M]
        z, rstd = _layernorm_parts(cq_ref[...])
        lng = _row(ln_ref, 0)
        l = z * lng + _row(ln_ref, 1)
        sl = _sigmoid(l)
        dl = dyc * (sl * (1.0 + l * (1.0 - sl)))
        vgm_ref[0:1, :] += _colsum(dl * z)
        vgm_ref[1:2, :] += _colsum(dl)
        dz = dl * lng
        dcq_ref[...] = (rstd * (dz - _rowmean(dz) - z * _rowmean(dz * z))).astype(BF)
        uy = uy_ref[...].astype(F32)
        u2 = uy * uy
        th = jnp.tanh(GELU_K * (uy + GELU_C * uy * u2))
        gelu = 0.5 * uy * (1.0 + th)
        dgelu = 0.5 * (1.0 + th) + 0.5 * uy * (1.0 - th * th) * (GELU_K * (1.0 + 3.0 * GELU_C * u2))
        dh_ref[...] = dyr * gelu
        duy_ref[...] = (dyr * h_ref[...] * dgelu).astype(BF)

    tok = pl.BlockSpec((ts, D), lambda i: (i, 0))
    mtok = lambda o: pl.BlockSpec((ts, M), lambda i: (i, o))
    return _pcall(
        body, name=name, grid=(S // ts,),
        in_specs=[tok, tok, pl.BlockSpec(vecs.shape, lambda i: (0, 0)), pl.BlockSpec(wout.shape, lambda i: (0, 0)),
                  mtok(0), pl.BlockSpec(lnv.shape, lambda i: (0, 0)), mtok(3), mtok(0)],
        out_specs=[mtok(0), mtok(0), mtok(0), tok, pl.BlockSpec((SUBLANES, D), lambda i: (0, 0)),
                   pl.BlockSpec((SUBLANES, M), lambda i: (0, 0))],
        out_shape=[jax.ShapeDtypeStruct((S, M), BF), jax.ShapeDtypeStruct((S, M), F32), jax.ShapeDtypeStruct((S, M), BF),
                   jax.ShapeDtypeStruct((S, D), BF),
                   jax.ShapeDtypeStruct((SUBLANES, D), F32), jax.ShapeDtypeStruct((SUBLANES, M), F32)],
        compiler_params=_params(("arbitrary",)),
    )(dxo, ym, vecs, wout, cq, lnv, proj, h)


def mix_in_bwd(dparts, x, dxo, vecs, win, name):
    S, D = x.shape
    M = dparts[0].shape[1]
    ts = min(512, S)

    def body(d0, d1, d2, d3, x_ref, dxo_ref, v_ref, w_ref, dx_ref, hb_ref, dp_ref, vg_ref):
        @pl.when(pl.program_id(0) == 0)
        def _():
            vg_ref[...] = jnp.zeros_like(vg_ref)

        for q, dref in enumerate((d0, d1, d2, d3)):
            dp_ref[:, q * M:(q + 1) * M] = dref[...].astype(BF)
        dh = lax.dot_general(dp_ref[...], w_ref[...], CONTRACT_LAST, preferred_element_type=F32)
        xv = x_ref[...]
        r = lax.rsqrt(_rowmean(xv * xv) + EPS)
        n = xv * r
        g = _row(v_ref, R_G2)
        sc1 = 1.0 + _row(v_ref, R_SC2)
        gsc = g * sc1
        hb_ref[...] = (n * gsc + _row(v_ref, R_SH2)).astype(BF)
        dhn = dh * n
        vg_ref[0:1, :] += _colsum(dh)
        vg_ref[1:2, :] += _colsum(dhn) * g
        vg_ref[2:3, :] += _colsum(dhn) * sc1
        dn = dh * gsc
        dx_ref[...] = dxo_ref[...] + r * (dn - n * _rowmean(dn * n))

    tok = pl.BlockSpec((ts, D), lambda i: (i, 0))
    mtok = pl.BlockSpec((ts, M), lambda i: (i, 0))
    return _pcall(
        body, name=name, grid=(S // ts,),
        in_specs=[mtok] * 4 + [tok, tok, pl.BlockSpec(vecs.shape, lambda i: (0, 0)), pl.BlockSpec(win.shape, lambda i: (0, 0))],
        out_specs=[tok, tok, pl.BlockSpec((ts, 4 * M), lambda i: (i, 0)), pl.BlockSpec((SUBLANES, D), lambda i: (0, 0))],
        out_shape=[jax.ShapeDtypeStruct((S, D), F32), jax.ShapeDtypeStruct((S, D), BF),
                   jax.ShapeDtypeStruct((S, 4 * M), BF), jax.ShapeDtypeStruct((SUBLANES, D), F32)],
        compiler_params=_params(("arbitrary",)),
    )(*dparts, x, dxo, vecs, win)


def _adamw(w, g, m, v):
    m = ADAM_B1 * m + (1.0 - ADAM_B1) * g
    v = ADAM_B2 * v + (1.0 - ADAM_B2) * (g * g)
    m_hat = m / (1.0 - ADAM_B1 ** ADAM_STEP)
    v_hat = v / (1.0 - ADAM_B2 ** ADAM_STEP)
    delta = -ADAM_LR * (m_hat / (jnp.sqrt(v_hat) + ADAM_EPS) + ADAM_WD * w)
    return delta, m, v


def adam_big(w, g_mine, g_sib, m, v, cidx, name):
    R, C = w.shape
    hr = R // 2
    tr = 256 if hr % 256 == 0 else hr
    tc = C if C <= 1536 else (1152 if C % 1152 == 0 else 1024)
    assert hr % tr == 0 and C % tc == 0 and g_mine.shape == (hr, C)
    nrb = hr // tr

    def body(ci_ref, w_ref, gm_ref, gs_ref, m_ref, v_ref, g_ref, d_ref, nm_ref, nv_ref):
        mine = (pl.program_id(0) // nrb) == ci_ref[0]
        g = jnp.where(mine, gm_ref[...], gs_ref[...])
        d, nm, nv = _adamw(w_ref[...], g, m_ref[...], v_ref[...])
        g_ref[...] = g
        d_ref[...] = d
        nm_ref[...] = nm
        nv_ref[...] = nv

    blk = pl.BlockSpec((tr, tc), lambda i, j, ci: (i, j))
    mine_spec = pl.BlockSpec((tr, tc), lambda i, j, ci: (jnp.where(i // nrb == ci[0], i % nrb, 0), j))
    sib_spec = pl.BlockSpec((tr, tc), lambda i, j, ci: (jnp.where(i // nrb == ci[0], 0, i % nrb), j))
    gs = pltpu.PrefetchScalarGridSpec(num_scalar_prefetch=1, grid=(R // tr, C // tc),
                                      in_specs=[blk, mine_spec, sib_spec, blk, blk], out_specs=[blk] * 4)
    return _pcall(body, name=name, grid_spec=gs, out_shape=[jax.ShapeDtypeStruct((R, C), F32)] * 4,
                  compiler_params=_params(("parallel", "parallel")))(cidx, w, g_mine, g_sib, m, v)


def adam_cond(c_all, dmod, w, m, v, name):
    B, Kin = c_all.shape
    N = w.shape[1]
    tn = 768 if N % 768 == 0 else 256
    assert N % tn == 0

    def body(c_ref, d_ref, w_ref, m_ref, v_ref, g_ref, dl_ref, nm_ref, nv_ref):
        cv = c_ref[...]
        ca = cv * _sigmoid(cv)
        g = lax.dot_general(ca.astype(BF), d_ref[...].astype(BF), CONTRACT_FIRST, preferred_element_type=F32)
        d, nm, nv = _adamw(w_ref[...], g, m_ref[...], v_ref[...])
        g_ref[...] = g
        dl_ref[...] = d
        nm_ref[...] = nm
        nv_ref[...] = nv

    blk = pl.BlockSpec((Kin, tn), lambda n: (0, n))
    return _pcall(
        body, name=name, grid=(N // tn,),
        in_specs=[pl.BlockSpec((B, Kin), lambda n: (0, 0)), pl.BlockSpec((B, tn), lambda n: (0, n)), blk, blk, blk],
        out_specs=[blk] * 4, out_shape=[jax.ShapeDtypeStruct((Kin, N), F32)] * 4,
        compiler_params=_params(("parallel",)),
    )(c_all, dmod, w, m, v)


def adam_small(ws, gs, ms, vs, name):
    n = len(ws)

    def body(*refs):
        ins, outs = refs[:4 * n], refs[4 * n:]
        for k in range(n):
            d, nm, nv = _adamw(ins[k][...], ins[n + k][...], ins[2 * n + k][...], ins[3 * n + k][...])
            outs[k][...] = d
            outs[n + k][...] = nm
            outs[2 * n + k][...] = nv

    specs = [pl.BlockSpec(w.shape, lambda i: (0, 0)) for w in ws]
    shapes = [jax.ShapeDtypeStruct(w.shape, F32) for w in ws]
    out = _pcall(body, name=name, grid=(1,), in_specs=specs * 4, out_specs=specs * 3, out_shape=shapes * 3,
                 compiler_params=_params(("arbitrary",)))(*ws, *gs, *ms, *vs)
    return out[:n], out[n:2 * n], out[2 * n:]


def _me():
    return lax.axis_index("x"), lax.axis_index("y"), lax.axis_index("c")


def _flip(x, y, p):
    return (x ^ (p >> 1) if (p >> 1) else x), (y ^ (p & 1) if (p & 1) else y)


def _handshake(peers):
    barrier = pltpu.get_barrier_semaphore()
    for peer in peers:
        pl.semaphore_signal(barrier, inc=1, device_id=peer, device_id_type=MESH)
    pl.semaphore_wait(barrier, len(peers))


def _seq_call(body, *, name, n_in, out_shape, sem_shapes, collective_id):
    del n_in
    return pl.kernel(body, out_type=out_shape, mesh=plsc.ScalarSubcoreMesh(axis_name="sq", num_cores=1), name=name,
                     scratch_types=sem_shapes, compiler_params=pltpu.CompilerParams(collective_id=collective_id))


def _hbm_comm_call(body, *, name, n_in, out_shape, sem_shapes, seq_id):
    if seq_id is not None:
        return _seq_call(body, name=name, n_in=n_in, out_shape=out_shape, sem_shapes=sem_shapes, collective_id=seq_id)
    anyspec = pl.BlockSpec(memory_space=pl.ANY)
    return _pcall(body, name=name, in_specs=[anyspec] * n_in, out_specs=[anyspec] * len(out_shape), out_shape=out_shape,
                  scratch_shapes=sem_shapes, compiler_params=_params())


def allgather_devices(v, name, with_sum=False):
    R, L = v.shape

    def body(v_ref, out_ref, *rest):
        if with_sum:
            sum_ref, send_sems, recv_sems = rest
        else:
            send_sems, recv_sems = rest
        x, y, c = _me()
        me = 4 * x + 2 * y + c
        out_ref[me] = v_ref[...]
        copies = []
        for p in range(1, N_DEV):
            px, py = _flip(x, y, p >> 1)
            pc = (1 - c) if (p & 1) else c
            peer = 4 * px + 2 * py + pc
            send = pltpu.make_async_remote_copy(src_ref=v_ref, dst_ref=out_ref.at[me], send_sem=send_sems.at[p - 1],
                                                recv_sem=recv_sems.at[p - 1], device_id=(px, py, pc), device_id_type=MESH)
            send.start()
            recv = pltpu.make_async_remote_copy(src_ref=v_ref, dst_ref=out_ref.at[peer], send_sem=send_sems.at[p - 1],
                                                recv_sem=recv_sems.at[p - 1], device_id=(px, py, pc), device_id_type=MESH)
            copies.append((send, recv))
        for send, recv in copies:
            recv.wait_recv()
        for send, recv in copies:
            send.wait_send()
        if with_sum:
            s = out_ref[0]
            for k in range(1, N_DEV):
                s = s + out_ref[k]
            sum_ref[...] = s

    vm = pl.BlockSpec(memory_space=pltpu.VMEM)
    out_shape = [jax.ShapeDtypeStruct((N_DEV, R, L), F32)]
    if with_sum:
        out_shape.append(jax.ShapeDtypeStruct((R, L), F32))
    return _pcall(
        body, name=name, in_specs=[vm], out_specs=[vm] * len(out_shape), out_shape=out_shape,
        scratch_shapes=[pltpu.SemaphoreType.DMA((N_DEV - 1,)), pltpu.SemaphoreType.DMA((N_DEV - 1,))],
        compiler_params=_params(),
    )(v)


def allgather_devices_hbm(v, name, seq_id):
    R, L = v.shape

    def body(v_ref, out_ref, send_sems, recv_sems, local_sem):
        x, y, c = _me()
        me = 4 * x + 2 * y + c
        peers = []
        for p in range(1, N_DEV):
            px, py = _flip(x, y, p >> 1)
            peers.append((px, py, (1 - c) if (p & 1) else c))
        _handshake(peers)
        lc = pltpu.make_async_copy(v_ref, out_ref.at[me], local_sem)
        lc.start()
        copies = []
        for p, (px, py, pc) in enumerate(peers):
            send = pltpu.make_async_remote_copy(src_ref=v_ref, dst_ref=out_ref.at[me], send_sem=send_sems.at[p],
                                                recv_sem=recv_sems.at[p], device_id=(px, py, pc), device_id_type=MESH)
            send.start()
            recv = pltpu.make_async_remote_copy(src_ref=v_ref, dst_ref=out_ref.at[4 * px + 2 * py + pc], send_sem=send_sems.at[p],
                                                recv_sem=recv_sems.at[p], device_id=(px, py, pc), device_id_type=MESH)
            copies.append((send, recv))
        for send, recv in copies:
            recv.wait_recv()
        for send, recv in copies:
            send.wait_send()
        lc.wait()

    return _seq_call(body, name=name, n_in=1, out_shape=[jax.ShapeDtypeStruct((N_DEV, R, L), F32)],
                     sem_shapes=[pltpu.SemaphoreType.DMA((N_DEV - 1,)), pltpu.SemaphoreType.DMA((N_DEV - 1,)),
                                 pltpu.SemaphoreType.DMA], collective_id=seq_id)(v)[0]


def sum_slots(g, name):
    n, R, L = g.shape
    tr = 216 if R % 216 == 0 else R
    assert R % tr == 0 and tr % SUBLANES == 0

    def body(g_ref, o_ref):
        s = g_ref[0]
        for k in range(1, n):
            s = s + g_ref[k]
        o_ref[...] = s

    return _pcall(body, name=name, grid=(R // tr,), in_specs=[pl.BlockSpec((n, tr, L), lambda i: (0, i, 0))],
                  out_specs=pl.BlockSpec((tr, L), lambda i: (i, 0)), out_shape=jax.ShapeDtypeStruct((R, L), F32),
                  compiler_params=_params(("parallel",)))(g)


def allgather_chips(v, name):
    R, L = v.shape

    def body(v_ref, out_ref, send_sems, recv_sems):
        x, y, c = _me()
        chip = 2 * x + y
        out_ref[chip] = v_ref[...]
        copies = []
        for p in range(1, N_CHIPS):
            px, py = _flip(x, y, p)
            send = pltpu.make_async_remote_copy(src_ref=v_ref, dst_ref=out_ref.at[chip], send_sem=send_sems.at[p - 1],
                                                recv_sem=recv_sems.at[p - 1], device_id=(px, py, c), device_id_type=MESH)
            send.start()
            recv = pltpu.make_async_remote_copy(src_ref=v_ref, dst_ref=out_ref.at[2 * px + py], send_sem=send_sems.at[p - 1],
                                                recv_sem=recv_sems.at[p - 1], device_id=(px, py, c), device_id_type=MESH)
            copies.append((send, recv))
        for send, recv in copies:
            recv.wait_recv()
        for send, recv in copies:
            send.wait_send()

    vm = pl.BlockSpec(memory_space=pltpu.VMEM)
    return _pcall(
        body, name=name, in_specs=[vm], out_specs=vm, out_shape=jax.ShapeDtypeStruct((N_CHIPS, R, L), F32),
        scratch_shapes=[pltpu.SemaphoreType.DMA((N_CHIPS - 1,)), pltpu.SemaphoreType.DMA((N_CHIPS - 1,))],
        compiler_params=_params(),
    )(v)


def _shard_window(ref, kind, shard_shape, chip, half):
    r, c = shard_shape
    hr = r // 2
    if kind == "col":
        return ref.at[pl.ds(pl.multiple_of(half * hr, hr), hr), pl.ds(pl.multiple_of(chip * c, c), c)]
    return ref.at[pl.ds(pl.multiple_of(chip * r + half * hr, hr), hr), :]


def allgather_weights(shards, kinds, name, seq_id=None):
    n = len(shards)
    fulls = []
    for s, kind in zip(shards, kinds):
        r, c = s.shape
        fulls.append(jax.ShapeDtypeStruct((r, N_CHIPS * c) if kind == "col" else (N_CHIPS * r, c), s.dtype))

    def body(*refs):
        srcs, outs = refs[:n], refs[n:2 * n]
        send_sems, recv_sems, fsend_sems, frecv_sems = refs[2 * n:]
        x, y, c = _me()
        chip = 2 * x + y
        sib = (x, y, 1 - c)
        if seq_id is not None:
            _handshake([(*_flip(x, y, p), c) for p in range(1, N_CHIPS)] + [sib])
        sends, fwds = [], []
        for i in range(n):
            shp = srcs[i].shape
            hr = shp[0] // 2
            my_half = srcs[i].at[pl.ds(pl.multiple_of(c * hr, hr), hr), :]
            for p in range(1, N_CHIPS):
                px, py = _flip(x, y, p)
                k = i * (N_CHIPS - 1) + p - 1
                cp = pltpu.make_async_remote_copy(src_ref=my_half, dst_ref=_shard_window(outs[i], kinds[i], shp, chip, c),
                                                  send_sem=send_sems.at[k], recv_sem=recv_sems.at[k],
                                                  device_id=(px, py, c), device_id_type=MESH)
                cp.start()
                sends.append(cp)
        for i in range(n):
            shp = srcs[i].shape
            for p in range(1, N_CHIPS):
                px, py = _flip(x, y, p)
                k = i * (N_CHIPS - 1) + p - 1
                landed = _shard_window(outs[i], kinds[i], shp, 2 * px + py, c)
                pltpu.make_async_remote_copy(src_ref=landed, dst_ref=landed, send_sem=send_sems.at[k], recv_sem=recv_sems.at[k],
                                             device_id=(px, py, c), device_id_type=MESH).wait_recv()
                fw = pltpu.make_async_remote_copy(src_ref=landed, dst_ref=landed, send_sem=fsend_sems.at[k],
                                                  recv_sem=frecv_sems.at[k], device_id=sib, device_id_type=MESH)
                fw.start()
                fwds.append(fw)
        for i in range(n):
            shp = srcs[i].shape
            for p in range(1, N_CHIPS):
                px, py = _flip(x, y, p)
                k = i * (N_CHIPS - 1) + p - 1
                other = _shard_window(outs[i], kinds[i], shp, 2 * px + py, 1 - c)
                pltpu.make_async_remote_copy(src_ref=other, dst_ref=other, send_sem=fsend_sems.at[k], recv_sem=frecv_sems.at[k],
                                             device_id=sib, device_id_type=MESH).wait_recv()
        for cp in sends + fwds:
            cp.wait_send()

    nk = n * (N_CHIPS - 1)
    gathered = _hbm_comm_call(
        body, name=name, n_in=n, out_shape=fulls, seq_id=seq_id,
        sem_shapes=[pltpu.SemaphoreType.DMA((nk,)), pltpu.SemaphoreType.DMA((nk,)), pltpu.SemaphoreType.DMA((nk,)),
                    pltpu.SemaphoreType.DMA((nk,))],
    )(*shards)
    return gathered


def place_local_shards(fulls, shards, kinds, name):
    n = len(shards)
    chip = jnp.reshape(2 * lax.axis_index("x") + lax.axis_index("y"), (1,)).astype(I32)

    def body(ci_ref, *refs):
        for i in range(n):
            refs[2 * n + i][...] = refs[i][...]

    in_specs = [pl.BlockSpec(s.shape, lambda i, ci: (0, 0)) for s in shards] + [pl.BlockSpec(memory_space=pl.ANY)] * n
    out_specs = [pl.BlockSpec(s.shape, (lambda i, ci: (0, ci[0])) if k == "col" else (lambda i, ci: (ci[0], 0)))
                 for s, k in zip(shards, kinds)]
    gs = pltpu.PrefetchScalarGridSpec(num_scalar_prefetch=1, grid=(1,), in_specs=in_specs, out_specs=out_specs)
    return _pcall(body, name=name, grid_spec=gs, out_shape=[jax.ShapeDtypeStruct(f.shape, f.dtype) for f in fulls],
                  input_output_aliases={1 + n + i: i for i in range(n)}, compiler_params=_params(("arbitrary",)))(chip, *shards, *fulls)


def _as_halves(g, kind, shard_shape):
    r, c = shard_shape
    if kind == "col":
        return g.reshape(2, r // 2, N_CHIPS * c)
    return g.reshape(N_CHIPS, 2, r // 2, c)


def exchange_sibling_halves(grads, kinds, shard_shapes, name, seq_id=None):
    n = len(grads)
    views = [_as_halves(g, k, s) for g, k, s in zip(grads, kinds, shard_shapes)]
    outs = []
    for k, (r, c) in zip(kinds, shard_shapes):
        outs.append(jax.ShapeDtypeStruct((r // 2, N_CHIPS * c) if k == "col" else (N_CHIPS, r // 2, c), grads[0].dtype))

    def body(*refs):
        srcs, dsts = refs[:n], refs[n:2 * n]
        send_sems, recv_sems = refs[2 * n:]
        x, y, c = _me()
        if seq_id is not None:
            _handshake([(x, y, 1 - c)])
        cps = []
        for i in range(n):
            src = srcs[i].at[1 - c] if kinds[i] == "col" else srcs[i].at[:, 1 - c]
            cp = pltpu.make_async_remote_copy(src_ref=src, dst_ref=dsts[i], send_sem=send_sems.at[i], recv_sem=recv_sems.at[i],
                                              device_id=(x, y, 1 - c), device_id_type=MESH)
            cp.start()
            cps.append(cp)
        for cp in cps:
            cp.wait_recv()
        for cp in cps:
            cp.wait_send()

    return _hbm_comm_call(body, name=name, n_in=n, out_shape=outs, seq_id=seq_id,
                          sem_shapes=[pltpu.SemaphoreType.DMA((n,)), pltpu.SemaphoreType.DMA((n,))])(*views)


def add_sibling_half(g, recv, kind, shard_shape, core_chip, name):
    r, c = shard_shape
    hr = r // 2
    gv = _as_halves(g, kind, shard_shape)
    tr = hr if hr <= 512 else (256 if hr % 256 == 0 else hr // 2)
    assert hr % tr == 0

    def body(ci_ref, g_ref, r_ref, h_ref, hb_ref):
        s = g_ref[...].astype(F32) + r_ref[...].astype(F32)
        hb_ref[...] = s.astype(BF)

        @pl.when(pl.program_id(1) == ci_ref[1])
        def _():
            h_ref[...] = s

    grid = (hr // tr, N_CHIPS)
    if kind == "col":
        g_spec = pl.BlockSpec((None, tr, c), lambda i, k, ci: (ci[0], i, k))
        o_spec = pl.BlockSpec((tr, c), lambda i, k, ci: (i, k))
    else:
        g_spec = pl.BlockSpec((None, None, tr, c), lambda i, k, ci: (k, ci[0], i, 0))
        o_spec = pl.BlockSpec((None, tr, c), lambda i, k, ci: (k, i, 0))
    own_spec = pl.BlockSpec((tr, c), lambda i, k, ci: (i, 0))
    gs = pltpu.PrefetchScalarGridSpec(num_scalar_prefetch=1, grid=grid, in_specs=[g_spec, o_spec], out_specs=[own_spec, o_spec])
    return _pcall(
        body, name=name, grid_spec=gs,
        out_shape=[jax.ShapeDtypeStruct((hr, c), F32), jax.ShapeDtypeStruct(recv.shape, BF)],
        compiler_params=_params(("parallel", "arbitrary")),
    )(core_chip, gv, recv)


def exchange_chip_pieces(hbs, kinds, shard_shapes, name, seq_id=None):
    n = len(hbs)
    outs = [jax.ShapeDtypeStruct((N_CHIPS - 1, r // 2, c), BF) for (r, c) in shard_shapes]

    def body(*refs):
        srcs, dsts = refs[:n], refs[n:2 * n]
        send_sems, recv_sems = refs[2 * n:]
        x, y, c = _me()
        if seq_id is not None:
            _handshake([(*_flip(x, y, p), c) for p in range(1, N_CHIPS)])
        cps = []
        for i in range(n):
            cc = shard_shapes[i][1]
            for p in range(1, N_CHIPS):
                px, py = _flip(x, y, p)
                pchip = 2 * px + py
                src = (srcs[i].at[:, pl.ds(pl.multiple_of(pchip * cc, cc), cc)] if kinds[i] == "col" else srcs[i].at[pchip])
                k = i * (N_CHIPS - 1) + p - 1
                cp = pltpu.make_async_remote_copy(src_ref=src, dst_ref=dsts[i].at[p - 1], send_sem=send_sems.at[k],
                                                  recv_sem=recv_sems.at[k], device_id=(px, py, c), device_id_type=MESH)
                cp.start()
                cps.append(cp)
        for cp in cps:
            cp.wait_recv()
        for cp in cps:
            cp.wait_send()

    nk = n * (N_CHIPS - 1)
    return _hbm_comm_call(body, name=name, n_in=n, out_shape=outs, seq_id=seq_id,
                          sem_shapes=[pltpu.SemaphoreType.DMA((nk,)), pltpu.SemaphoreType.DMA((nk,))])(*hbs)


def sum_chip_pieces(h_own, pieces, name):
    hr, c = h_own.shape
    tr = hr if hr <= 512 else (256 if hr % 256 == 0 else hr // 2)
    assert hr % tr == 0

    def body(h_ref, p_ref, q_ref):
        q_ref[...] = ((h_ref[...] + p_ref[0].astype(F32)) + p_ref[1].astype(F32)) + p_ref[2].astype(F32)

    blk = pl.BlockSpec((tr, c), lambda i: (i, 0))
    return _pcall(body, name=name, grid=(hr // tr,), in_specs=[blk, pl.BlockSpec((N_CHIPS - 1, tr, c), lambda i: (0, i, 0))],
                  out_specs=blk, out_shape=jax.ShapeDtypeStruct((hr, c), F32), compiler_params=_params(("parallel",)))(h_own, pieces)


def exchange_reduced_halves(qs, name, seq_id):
    n = len(qs)

    def body(*refs):
        srcs, dsts = refs[:n], refs[n:2 * n]
        send_sems, recv_sems = refs[2 * n:]
        x, y, c = _me()
        _handshake([(x, y, 1 - c)])
        cps = []
        for i in range(n):
            cp = pltpu.make_async_remote_copy(src_ref=srcs[i], dst_ref=dsts[i], send_sem=send_sems.at[i], recv_sem=recv_sems.at[i],
                                              device_id=(x, y, 1 - c), device_id_type=MESH)
            cp.start()
            cps.append(cp)
        for cp in cps:
            cp.wait_recv()
        for cp in cps:
            cp.wait_send()

    return _seq_call(body, name=name, n_in=n, out_shape=[jax.ShapeDtypeStruct(q.shape, F32) for q in qs],
                     sem_shapes=[pltpu.SemaphoreType.DMA((n,)), pltpu.SemaphoreType.DMA((n,))], collective_id=seq_id)(*qs)


def _rows128(a):
    return a.reshape(-1, LANES)


def _after(xs, *deps):
    flat = []
    for d in deps:
        flat.extend(d if isinstance(d, (list, tuple)) else [d])
    return list(lax.optimization_barrier((tuple(xs), tuple(flat)))[0])


def _block_diag(w):
    H, d, _ = w.shape
    eye = jnp.eye(H, dtype=w.dtype)
    return jnp.einsum("hde,hg->hdge", w, eye).reshape(H * d, H * d)


def _diag_blocks(g4, H, d):
    nb = g4.shape[0]
    per = LANES // d
    g = g4.reshape(nb, per, d, per, d)
    return jnp.stack([g[:, j, :, j, :] for j in range(per)], axis=1).reshape(H, d, d)


def kernel(x, c, w_mod, b_mod, g_ffn1, w_ffn1_in, w_ffn1_out, g_mix, w_in, conv_w, conv_b, ln_g, ln_b, rnn_conv_w, rnn_conv_b, w_a, b_a, w_i, b_i, lru_lambda, w_out, g_ffn2, w_ffn2_in, w_ffn2_out, w_fmod, b_fmod, g_final, loss_target, m_w_mod, m_b_mod, m_g_ffn1, m_w_ffn1_in, m_w_ffn1_out, m_g_mix, m_w_in, m_conv_w, m_conv_b, m_ln_g, m_ln_b, m_rnn_conv_w, m_rnn_conv_b, m_w_a, m_b_a, m_w_i, m_b_i, m_lru_lambda, m_w_out, m_g_ffn2, m_w_ffn2_in, m_w_ffn2_out, m_w_fmod, m_b_fmod, m_g_final, v_w_mod, v_b_mod, v_g_ffn1, v_w_ffn1_in, v_w_ffn1_out, v_g_mix, v_w_in, v_conv_w, v_conv_b, v_ln_g, v_ln_b, v_rnn_conv_w, v_rnn_conv_b, v_w_a, v_b_a, v_w_i, v_b_i, v_lru_lambda, v_w_out, v_g_ffn2, v_w_ffn2_in, v_w_ffn2_out, v_w_fmod, v_b_fmod, v_g_final):
    S, D = x.shape[1], x.shape[2]
    M = conv_b.shape[1]
    H, HD = w_a.shape[1], w_a.shape[2]
    nb = M // LANES
    ix, iy, ic = lax.axis_index("x"), lax.axis_index("y"), lax.axis_index("c")
    chip = 2 * ix + iy
    dev = 2 * chip + ic
    core_chip = jnp.stack([ic, chip]).astype(I32)
    cidx = core_chip
    xs = x[0]
    tgt = loss_target[0]

    kinds = ["col", "row"]
    w_f1, w_mx, w_f2 = [w_ffn1_in[0], w_ffn1_out[0]], [w_in[0], w_out[0]], [w_ffn2_in[0], w_ffn2_out[0]]
    as_bf = lambda ws: [w.astype(BF) for w in ws]
    shapes_of = lambda ws: [w.shape for w in ws]
    b_f1, b_mx, b_f2 = as_bf(w_f1), as_bf(w_mx), as_bf(w_f2)
    got_f1i = allgather_weights(b_f1[:1], kinds[:1], "gather_ffn1_in", seq_id=9)
    got_f1o = allgather_weights(b_f1[1:], kinds[1:], "gather_ffn1_out", seq_id=13)
    got_mx = allgather_weights(b_mx, kinds, "gather_mix", seq_id=1)
    got_f2 = allgather_weights(b_f2, kinds, "gather_ffn2", seq_id=2)

    c_all =allgather_devices(_rows128(c), "gather_c")[0].reshape(N_DEV, D)
    mod_cols = cond_matmul(c_all, w_mod[0], "mod_proj")
    fmod_cols = cond_matmul(c_all, w_fmod, "fmod_proj")
    convw_pad = jnp.pad(conv_w[0], ((0, 32 - CONV_WIDTH), (0, 0)))
    rnnw_pad = jnp.pad(rnn_conv_w[0], ((0, SUBLANES - RNN_CONV_WIDTH), (0, 0)))
    n_mod, n_fmod = mod_cols.shape[1], fmod_cols.shape[1]
    small = jnp.concatenate([_rows128(mod_cols), _rows128(fmod_cols), convw_pad, rnnw_pad], axis=0)
    small4 = allgather_chips(small, "gather_cond")
    r0 = N_DEV * n_mod // LANES
    r1 = r0 + N_DEV * n_fmod // LANES
    mod_all = small4[:, :r0].reshape(N_CHIPS, N_DEV, n_mod)
    fmod_all = small4[:, r0:r1].reshape(N_CHIPS, N_DEV, n_fmod)
    convw4 = small4[:, r1:r1 + 32]
    rnnw4 = small4[:, r1 + 32:r1 + 32 + SUBLANES]
    mod_row = lax.dynamic_index_in_dim(mod_all, dev, axis=1, keepdims=False).reshape(1, N_CHIPS * n_mod) + b_mod
    fmod_row = lax.dynamic_index_in_dim(fmod_all, dev, axis=1, keepdims=False).reshape(1, N_CHIPS * n_fmod) + b_fmod[None, :]
    vecs = jnp.concatenate([mod_row.reshape(9, D), fmod_row.reshape(2, D), g_ffn1, g_mix, g_ffn2, g_final[None, :],
                            jnp.zeros((1, D), F32)], axis=0)
    lnv = jnp.concatenate([ln_g, ln_b, jnp.zeros((SUBLANES - 2, M), F32)], axis=0)
    bda = _block_diag(w_a[0]).astype(BF)
    bdi = _block_diag(w_i[0]).astype(BF)

    def reduce_add(gs, recv, ws, tag, kinds_=kinds):
        pairs = [add_sibling_half(g, r_, k, w.shape, core_chip, f"add_sibling_{tag}{j}")
                 for j, (g, r_, k, w) in enumerate(zip(gs, recv, kinds_, ws))]
        return [p[0] for p in pairs], [p[1] for p in pairs]

    def reduce_sum(hs_, recv, ws, tag, kinds_=kinds):
        return [sum_chip_pieces(h_, p_, f"sum_chips_{tag}{j}") for j, (h_, p_) in enumerate(zip(hs_, recv))]

    rows1 = (R_SH1, R_SC1, R_GT1, R_G1)
    rows3 = (R_SH3, R_SC3, R_GT3, R_G3)
    (wi1,) = place_local_shards(got_f1i, b_f1[:1], kinds[:1], "place_ffn1_in")
    g1s, u1s, a1s = ffn_fwd_in(xs, vecs, wi1, rows1, "ffn1_fwd_in")
    (wo1,) = place_local_shards(_after(got_f1o, a1s), b_f1[1:], kinds[1:], "place_ffn1_out")
    x1, y1 = ffn_fwd_out(a1s, xs, vecs, wo1, rows1, "ffn1_fwd_out")
    win, wout = place_local_shards(_after(got_mx, x1), b_mx, kinds, "place_mix")
    proj = norm_matmul(x1, vecs, win, (R_SH2, R_SC2, R_G2), "mix_in_proj")
    cq = conv_fwd(proj, convw4, conv_b, "conv_fwd")
    xr, ra, ii, hh = rnn_fwd(proj, rnnw4, rnn_conv_b, bda, bdi, b_a, b_i, lru_lambda, "rnn_fwd")
    x2, ym, ycat = mix_out(cq, proj, hh, x1, vecs, lnv, wout, "mix_out")
    wi2, wo2 = place_local_shards(_after(got_f2, x2), b_f2, kinds, "place_ffn2")
    dx3, g2s, u2s, y2, vgf = ffn_fwd(x2, vecs, wi2, wo2, rows3, "ffn2_fwd", final_tgt=tgt)

    Fd = wo1.shape[0]
    tk = min(2048, S)
    dx2, act2, dg2, du2, h3b, dy2b, vg3 = ffn_bwd(dx3, x2, vecs, g2s, u2s, y2, wi2, wo2, rows3, "ffn2_bwd")
    gwo2 = matmul(act2, dy2b, "tn", tm=Fd // 2, tn=D, tk=tk, out_dtype=BF, name="ffn2_dwo")
    gwi2 = matmul(h3b, dg2, "tn", tm=D, tn=Fd // 2, tk=tk, out_dtype=BF, name="ffn2_dwg", out_cols=2 * Fd)
    gwi2 = matmul(h3b, du2, "tn", tm=D, tn=Fd // 2, tk=tk, out_dtype=BF, name="ffn2_dwu", out_cols=2 * Fd, col_off=Fd, prev=gwi2)
    recv1_f2 = exchange_sibling_halves([gwi2, gwo2], kinds, shapes_of(w_f2), "reduce1_ffn2", seq_id=3)
    dcq, dhout, duy, dymb, vgd, vgm = mix_out_bwd(dx2, ym, vecs, wout, cq, lnv, proj, hh, "mix_out_bwd")
    gwout = matmul(ycat, dymb, "tn", tm=2 * M, tn=D, tk=tk, out_dtype=BF, name="mix_dwout")
    recv1_f2 = _after(recv1_f2, gwout)
    h_f2, hb_f2 = reduce_add([gwi2, gwo2], recv1_f2, w_f2, "ffn2_")
    recv2_f2 = exchange_chip_pieces(hb_f2, kinds, shapes_of(w_f2), "reduce2_ffn2", seq_id=4)
    duv, dug, dconvw4, dconvb = conv_bwd(_after([dcq], hb_f2)[0], proj, convw4, "conv_bwd")
    dux, dwa4, dwi4, drnnw4, rvec = rnn_bwd(dhout, hh, xr, ra, ii, proj, rnnw4, bda, bdi, lru_lambda, "rnn_bwd")
    dx1, h2b, dpb, vg2 = mix_in_bwd((duv, dug, dux, duy), x1, dx2, vecs, win, "mix_in_bwd")
    gwin = matmul(h2b, dpb, "tn", tm=D, tn=1024, tk=tk, out_dtype=BF, name="mix_dwin")
    recv1_mx = exchange_sibling_halves([gwin, gwout], kinds, shapes_of(w_mx), "reduce1_mix", seq_id=5)
    q_f2 = reduce_sum(_after(h_f2, gwin), recv2_f2, w_f2, "ffn2_")
    r_f2 = exchange_reduced_halves(q_f2, "reduce3_ffn2", seq_id=14)
    h_mx, hb_mx = reduce_add([gwin, gwout], _after(recv1_mx, q_f2), w_mx, "mix_")
    recv2_mx = exchange_chip_pieces(hb_mx, kinds, shapes_of(w_mx), "reduce2_mix", seq_id=6)
    dx0, act1, dg1, du1, h1b, dy1b, vg1 = ffn_bwd(_after([dx1], hb_mx)[0], xs, vecs, g1s, u1s, y1, wi1, wo1, rows1, "ffn1_bwd")
    dmod_row = jnp.concatenate([vg1[1:3], vg1[0:1], vg2[0:2], vgd[0:1], vg3[1:3], vg3[0:1]], axis=0)
    gains = jnp.concatenate([vg1[3:4], vg2[2:3], vg3[3:4], vgf[2:4]], axis=0)
    mvecs = jnp.concatenate([dconvb, vgm[0:2], rvec[0:4], jnp.zeros((1, M), F32)], axis=0)
    parts = [_rows128(dmod_row), _rows128(vgf[0:2]), _rows128(gains), _rows128(mvecs),
             _rows128(dconvw4), _rows128(drnnw4), _rows128(_diag_blocks(dwa4, H, HD)), _rows128(_diag_blocks(dwi4, H, HD))]
    sizes = [p.shape[0] for p in parts]
    packed = jnp.concatenate(parts, axis=0)
    gathered = allgather_devices_hbm(packed, "gather_small", seq_id=10)

    gwo1 = matmul(_after([act1], recv2_mx, packed)[0], dy1b, "tn", tm=Fd // 2, tn=D, tk=tk, out_dtype=BF, name="ffn1_dwo")
    w_f1o, w_f1i = w_f1[1:], w_f1[:1]
    recv1_f1o = exchange_sibling_halves([gwo1], ["row"], shapes_of(w_f1o), "reduce1_ffn1_out", seq_id=7)
    q_mx = reduce_sum(_after(h_mx, gwo1), recv2_mx, w_mx, "mix_")
    r_mx = exchange_reduced_halves(q_mx, "reduce3_mix", seq_id=15)
    gwi1 = matmul(_after([h1b], q_mx)[0], dg1, "tn", tm=D, tn=Fd // 2, tk=tk, out_dtype=BF, name="ffn1_dwg", out_cols=2 * Fd)
    h_f1o, hb_f1o = reduce_add([gwo1], _after(recv1_f1o, gwi1), w_f1o, "ffn1_out", ["row"])
    recv2_f1o = exchange_chip_pieces(hb_f1o, ["row"], shapes_of(w_f1o), "reduce2_ffn1_out", seq_id=11)
    gwi1 = matmul(h1b, _after([du1], hb_f1o, gathered)[0], "tn", tm=D, tn=Fd // 2, tk=tk, out_dtype=BF, name="ffn1_dwu", out_cols=2 * Fd,
                  col_off=Fd, prev=gwi1)
    recv1_f1i = exchange_sibling_halves([gwi1], ["col"], shapes_of(w_f1i), "reduce1_ffn1_in", seq_id=12)
    q_f1o = reduce_sum(_after(h_f1o, gwi1), recv2_f1o, w_f1o, "ffn1_out", ["row"])
    r_f1o = exchange_reduced_halves(q_f1o, "reduce3_ffn1_out", seq_id=16)
    summed = sum_slots(gathered, "sum_small")
    offs = [0]
    for s in sizes:
        offs.append(offs[-1] + s)
    seg = lambda k: summed[offs[k]:offs[k + 1]]
    g_b_mod = seg(0).reshape(1, 9 * D)
    g_b_fmod = seg(1).reshape(1, 2 * D)
    gsum = seg(2).reshape(5, D)
    loss = (0.5 / D) * jnp.sum(gsum[4])
    msum = seg(3).reshape(SUBLANES, M)
    g_conv_w = lax.dynamic_index_in_dim(seg(4).reshape(nb, 32, LANES), chip, axis=0, keepdims=False)[:CONV_WIDTH]
    g_rnn_w = lax.dynamic_index_in_dim(seg(5).reshape(nb, SUBLANES, LANES), chip, axis=0, keepdims=False)[:RNN_CONV_WIDTH]
    g_w_a = seg(6).reshape(H, HD, HD)
    g_w_i = seg(7).reshape(H, HD, HD)
    dmod_all = gathered[:, offs[0]:offs[1]].reshape(N_DEV, 9 * D)
    dfmod_all = gathered[:, offs[1]:offs[2]].reshape(N_DEV, 2 * D)
    dmod_cols = lax.dynamic_slice_in_dim(dmod_all, chip * n_mod, n_mod, axis=1)
    dfmod_cols = lax.dynamic_slice_in_dim(dfmod_all, chip * n_fmod, n_fmod, axis=1)

    h_f1i, hb_f1i = reduce_add([gwi1], _after(recv1_f1i, q_f1o), w_f1i, "ffn1_in", ["col"])
    recv2_f1i = exchange_chip_pieces(hb_f1i, ["col"], shapes_of(w_f1i), "reduce2_ffn1_in", seq_id=8)
    dmod_cols, dfmod_cols = _after([dmod_cols, dfmod_cols], hb_f1i)
    g_w_mod, d_w_mod, nm_w_mod, nv_w_mod = adam_cond(c_all, dmod_cols, w_mod[0], m_w_mod[0], v_w_mod[0], "adam_w_mod")
    g_w_fmod, d_w_fmod, nm_w_fmod, nv_w_fmod = adam_cond(c_all, dfmod_cols, w_fmod, m_w_fmod, v_w_fmod, "adam_w_fmod")

    def adam_group(ws, qs_, rs_, ms, vs, tags, after):
        qs_ = _after(list(qs_), *after) if after else list(qs_)
        return [adam_big(w, q_, r_, m, v, cidx, "adam_" + t) for w, q_, r_, m, v, t in zip(ws, qs_, rs_, ms, vs, tags)]

    ad_f2 = adam_group(w_f2, q_f2, r_f2, [m_w_ffn2_in[0], m_w_ffn2_out[0]], [v_w_ffn2_in[0], v_w_ffn2_out[0]],
                       ["ffn2_in", "ffn2_out"], [hb_f1i])
    ad_mx = adam_group(w_mx, q_mx, r_mx, [m_w_in[0], m_w_out[0]], [v_w_in[0], v_w_out[0]], ["w_in", "w_out"], [hb_f1i])
    ad_f1o = adam_group(w_f1o, q_f1o, r_f1o, [m_w_ffn1_out[0]], [v_w_ffn1_out[0]], ["ffn1_out"], [hb_f1i])
    q_f1i = reduce_sum(_after(h_f1i, ad_f2[0][0], ad_f2[1][0], ad_mx[0][0], ad_mx[1][0], ad_f1o[0][0], g_w_mod, g_w_fmod),
                       recv2_f1i, w_f1i, "ffn1_in", ["col"])
    r_f1i = exchange_reduced_halves(q_f1i, "reduce3_ffn1_in", seq_id=17)
    ad_f1i = adam_group(w_f1i, q_f1i, r_f1i, [m_w_ffn1_in[0]], [v_w_ffn1_in[0]], ["ffn1_in"], [])
    big_out = ad_f1i + ad_f1o + ad_mx + ad_f2

    flat2 = lambda a: a.reshape(-1, a.shape[-1])
    small_names = ["b_mod", "g_ffn1", "g_mix", "conv_w", "conv_b", "ln_g", "ln_b", "rnn_conv_w", "rnn_conv_b", "w_a", "b_a",
                   "w_i", "b_i", "lru_lambda", "g_ffn2", "b_fmod", "g_final"]
    small_w = [b_mod, g_ffn1, g_mix, conv_w, conv_b, ln_g, ln_b, rnn_conv_w, rnn_conv_b, w_a, b_a, w_i, b_i, lru_lambda,
               g_ffn2, b_fmod, g_final]
    small_m = [m_b_mod, m_g_ffn1, m_g_mix, m_conv_w, m_conv_b, m_ln_g, m_ln_b, m_rnn_conv_w, m_rnn_conv_b, m_w_a, m_b_a,
               m_w_i, m_b_i, m_lru_lambda, m_g_ffn2, m_b_fmod, m_g_final]
    small_v = [v_b_mod, v_g_ffn1, v_g_mix, v_conv_w, v_conv_b, v_ln_g, v_ln_b, v_rnn_conv_w, v_rnn_conv_b, v_w_a, v_b_a,
               v_w_i, v_b_i, v_lru_lambda, v_g_ffn2, v_b_fmod, v_g_final]
    small_g = [g_b_mod, gsum[0:1], gsum[1:2], g_conv_w, msum[0:1], msum[1:2], msum[2:3], g_rnn_w, msum[3:4], g_w_a, msum[4:5],
               g_w_i, msum[5:6], msum[6:7], gsum[2:3], g_b_fmod, gsum[3:4]]
    small_g = [g.reshape(w.shape) for g, w in zip(small_g, small_w)]
    two_d = lambda a: a.reshape(1, -1) if a.ndim == 1 else flat2(a)
    sd, sm, sv = adam_small([two_d(a) for a in small_w], [two_d(a) for a in small_g], [two_d(a) for a in small_m],
                            [two_d(a) for a in small_v], "adam_small")
    small = {}
    for k, nm in enumerate(small_names):
        shp = small_w[k].shape
        small[nm] = (small_g[k], sd[k].reshape(shp), sm[k].reshape(shp), sv[k].reshape(shp))

    big = {"w_mod": tuple(a[None] for a in (g_w_mod, d_w_mod, nm_w_mod, nv_w_mod)),
           "w_fmod": (g_w_fmod, d_w_fmod, nm_w_fmod, nv_w_fmod)}
    for nm, res in zip(["w_ffn1_in", "w_ffn1_out", "w_in", "w_out", "w_ffn2_in", "w_ffn2_out"], big_out):
        big[nm] = tuple(a[None] for a in res)
    order = ["w_mod", "b_mod", "g_ffn1", "w_ffn1_in", "w_ffn1_out", "g_mix", "w_in", "conv_w", "conv_b", "ln_g", "ln_b",
             "rnn_conv_w", "rnn_conv_b", "w_a", "b_a", "w_i", "b_i", "lru_lambda", "w_out", "g_ffn2", "w_ffn2_in",
             "w_ffn2_out", "w_fmod", "b_fmod", "g_final"]
    table = {**small, **big}
    outs = [loss, dx0[None]]
    for kind_ in range(4):
        outs.extend(table[nm][kind_] for nm in order)
    return tuple(outs)
```

```python
import functools

import jax
import jax.numpy as jnp
from jax import lax
from jax.experimental import pallas as pl
from jax.experimental.pallas import tpu as pltpu
from jax.experimental.pallas import tpu_sc as plsc

F32 = jnp.float32
BF = jnp.bfloat16
I32 = jnp.int32
MESH = pl.DeviceIdType.MESH

EPS = 1e-6
RG_C = 8.0
MACARON_W = 0.5
CONV_WIDTH = 31
RNN_CONV_WIDTH = 4
ADAM_LR = 0.001
ADAM_B1 = 0.9
ADAM_B2 = 0.999
ADAM_EPS = 1e-08
ADAM_WD = 0.01
ADAM_STEP = 10

LANES = 128
SUBLANES = 8
VMEM_LIMIT = 62 * 1024 * 1024
N_CHIPS = 4
N_DEV = 8

R_SH1, R_SC1, R_GT1, R_SH2, R_SC2, R_GT2, R_SH3, R_SC3, R_GT3, R_FSH, R_FSC, R_G1, R_G2, R_G3, R_GF = range(15)

CONTRACT_LAST = (((1,), (1,)), ((), ()))
CONTRACT_FIRST = (((0,), (0,)), ((), ()))


def _pcall(body, **kw):
    return pl.pallas_call(body, **kw)


def _params(sem=None, vmem=VMEM_LIMIT):
    if sem is None:
        return pltpu.CompilerParams(vmem_limit_bytes=vmem)
    return pltpu.CompilerParams(dimension_semantics=sem, vmem_limit_bytes=vmem)


def _row(ref, r):
    return ref[r:r + 1, :]


def _sigmoid(x):
    return 1.0 / (1.0 + jnp.exp(-x))


def _colsum(x):
    return jnp.sum(x, axis=0, keepdims=True)


def _rowmean(x):
    return jnp.mean(x, axis=-1, keepdims=True)


def matmul(a, b, mode, *, tm, tn, tk, name, out_dtype=F32, out_cols=None, col_off=0, prev=None):
    if mode == "nn":
        (M, K), (K2, N) = a.shape, b.shape
    elif mode == "nt":
        (M, K), (N, K2) = a.shape, b.shape
    else:
        (K, M), (K2, N) = a.shape, b.shape
    assert K == K2 and M % tm == 0 and N % tn == 0 and K % tk == 0 and col_off % tn == 0
    nk = K // tk
    out_cols = N if out_cols is None else out_cols
    off = col_off // tn

    def body(*refs):
        if prev is None:
            a_ref, b_ref, o_ref, acc = refs
        else:
            a_ref, b_ref, _, o_ref, acc = refs
        k = pl.program_id(2)
        av = a_ref[...].astype(BF)
        bv = b_ref[...].astype(BF)
        if mode == "nn":
            part = jnp.dot(av, bv, preferred_element_type=F32)
        elif mode == "nt":
            part = lax.dot_general(av, bv, CONTRACT_LAST, preferred_element_type=F32)
        else:
            part = lax.dot_general(av, bv, CONTRACT_FIRST, preferred_element_type=F32)
        if nk == 1:
            o_ref[...] = part.astype(out_dtype)
            return

        @pl.when(k == 0)
        def _():
            acc[...] = part

        @pl.when((k > 0) & (k < nk - 1))
        def _():
            acc[...] += part

        @pl.when(k == nk - 1)
        def _():
            o_ref[...] = (acc[...] + part).astype(out_dtype)

    if mode == "nn":
        a_spec = pl.BlockSpec((tm, tk), lambda m, n, k: (m, k))
        b_spec = pl.BlockSpec((tk, tn), lambda m, n, k: (k, n))
    elif mode == "nt":
        a_spec = pl.BlockSpec((tm, tk), lambda m, n, k: (m, k))
        b_spec = pl.BlockSpec((tn, tk), lambda m, n, k: (n, k))
    else:
        a_spec = pl.BlockSpec((tk, tm), lambda m, n, k: (k, m))
        b_spec = pl.BlockSpec((tk, tn), lambda m, n, k: (k, n))
    in_specs = [a_spec, b_spec]
    args = [a, b]
    aliases = {}
    if prev is not None:
        in_specs.append(pl.BlockSpec(memory_space=pl.ANY))
        args.append(prev)
        aliases = {2: 0}
    return _pcall(
        body, name=name, grid=(M // tm, N // tn, nk), in_specs=in_specs,
        out_specs=pl.BlockSpec((tm, tn), lambda m, n, k: (m, n + off)),
        out_shape=jax.ShapeDtypeStruct((M, out_cols), out_dtype),
        scratch_shapes=[pltpu.VMEM((tm, tn), F32)], input_output_aliases=aliases,
        compiler_params=_params(("parallel", "parallel", "arbitrary")),
    )(*args)


def cond_matmul(c_all, w, name):
    B, K = c_all.shape
    N = w.shape[1]
    tn = 256
    assert N % tn == 0

    def body(c_ref, w_ref, o_ref):
        cv = c_ref[...]
        ca = cv * _sigmoid(cv)
        o_ref[...] = jnp.dot(ca.astype(BF), w_ref[...].astype(BF), preferred_element_type=F32)

    return _pcall(
        body, name=name, grid=(N // tn,),
        in_specs=[pl.BlockSpec((B, K), lambda n: (0, 0)), pl.BlockSpec((K, tn), lambda n: (0, n))],
        out_specs=pl.BlockSpec((B, tn), lambda n: (0, n)),
        out_shape=jax.ShapeDtypeStruct((B, N), F32), compiler_params=_params(("parallel",)),
    )(c_all, w)


FFN_FWD_TS = 512
FFN_BWD_TS = 256


def _resident(shape, index_map):
    return pl.BlockSpec(shape, index_map, pipeline_mode=pl.Buffered(1))


def _final_norm_loss_grad(xv, t, v_ref, vg_ref):
    D = xv.shape[-1]
    r = lax.rsqrt(_rowmean(xv * xv) + EPS)
    n = xv * r
    g = _row(v_ref, R_GF)
    sc1 = 1.0 + _row(v_ref, R_FSC)
    gsc = g * sc1
    e = n * gsc + _row(v_ref, R_FSH) - t
    vg_ref[3:4, :] += _colsum(e * e)
    dout = e * (1.0 / D)
    dn_ = dout * n
    vg_ref[0:1, :] += _colsum(dout)
    vg_ref[1:2, :] += _colsum(dn_) * g
    vg_ref[2:3, :] += _colsum(dn_) * sc1
    dn = dout * gsc
    return r * (dn - n * _rowmean(dn * n))


def ffn_fwd(x, vecs, wi, wo, rows, name, final_tgt=None):
    r_sh, r_sc, r_gt, r_g = rows
    S, D = x.shape
    Fd = wo.shape[0]
    ts = min(FFN_FWD_TS, S)
    with_final = final_tgt is not None

    def body(*refs):
        if with_final:
            x_ref, v_ref, wg_ref, wu_ref, wo_ref, t_ref, xo_ref, g_ref, u_ref, y_ref, vg_ref = refs
        else:
            x_ref, v_ref, wg_ref, wu_ref, wo_ref, xo_ref, g_ref, u_ref, y_ref = refs
        xv = x_ref[...]
        r = lax.rsqrt(_rowmean(xv * xv) + EPS)
        gs = _row(v_ref, r_g) * (1.0 + _row(v_ref, r_sc))
        hb = (xv * r * gs + _row(v_ref, r_sh)).astype(BF)
        G = jnp.dot(hb, wg_ref[...], preferred_element_type=F32)
        U = jnp.dot(hb, wu_ref[...], preferred_element_type=F32)
        g_ref[...] = G.astype(BF)
        u_ref[...] = U.astype(BF)
        act = (G * _sigmoid(G) * U).astype(BF)
        Y = jnp.dot(act, wo_ref[...], preferred_element_type=F32)
        y_ref[...] = Y.astype(BF)
        xo = xv + (MACARON_W * _row(v_ref, r_gt)) * Y
        if with_final:
            @pl.when(pl.program_id(0) == 0)
            def _():
                vg_ref[...] = jnp.zeros_like(vg_ref)

            xo_ref[...] = _final_norm_loss_grad(xo, t_ref[...], v_ref, vg_ref)
        else:
            xo_ref[...] = xo

    tok = pl.BlockSpec((ts, D), lambda i: (i, 0))
    hid = pl.BlockSpec((ts, Fd), lambda i: (i, 0))
    in_specs = [tok, pl.BlockSpec(vecs.shape, lambda i: (0, 0)), _resident((D, Fd), lambda i: (0, 0)),
                _resident((D, Fd), lambda i: (0, 1)), _resident((Fd, D), lambda i: (0, 0))]
    out_specs = [tok, hid, hid, tok]
    out_shape = [jax.ShapeDtypeStruct((S, D), F32), jax.ShapeDtypeStruct((S, Fd), BF),
                 jax.ShapeDtypeStruct((S, Fd), BF), jax.ShapeDtypeStruct((S, D), BF)]
    args = [x, vecs, wi, wi, wo]
    if with_final:
        in_specs.append(tok)
        args.append(final_tgt)
        out_specs.append(pl.BlockSpec((SUBLANES, D), lambda i: (0, 0)))
        out_shape.append(jax.ShapeDtypeStruct((SUBLANES, D), F32))
    return _pcall(body, name=name, grid=(S // ts,), in_specs=in_specs, out_specs=out_specs, out_shape=out_shape,
                  compiler_params=_params(("arbitrary",)))(*args)


def ffn_fwd_in(x, vecs, wi, rows, name):
    r_sh, r_sc, r_gt, r_g = rows
    S, D = x.shape
    Fd = wi.shape[1] // 2
    ts = min(FFN_FWD_TS, S)

    def body(x_ref, v_ref, wg_ref, wu_ref, g_ref, u_ref, a_ref):
        xv = x_ref[...]
        r = lax.rsqrt(_rowmean(xv * xv) + EPS)
        gs = _row(v_ref, r_g) * (1.0 + _row(v_ref, r_sc))
        hb = (xv * r * gs + _row(v_ref, r_sh)).astype(BF)
        G = jnp.dot(hb, wg_ref[...], preferred_element_type=F32)
        U = jnp.dot(hb, wu_ref[...], preferred_element_type=F32)
        g_ref[...] = G.astype(BF)
        u_ref[...] = U.astype(BF)
        a_ref[...] = (G * _sigmoid(G) * U).astype(BF)

    hid = pl.BlockSpec((ts, Fd), lambda i: (i, 0))
    return _pcall(
        body, name=name, grid=(S // ts,),
        in_specs=[pl.BlockSpec((ts, D), lambda i: (i, 0)), pl.BlockSpec(vecs.shape, lambda i: (0, 0)),
                  _resident((D, Fd), lambda i: (0, 0)), _resident((D, Fd), lambda i: (0, 1))],
        out_specs=[hid, hid, hid], out_shape=[jax.ShapeDtypeStruct((S, Fd), BF)] * 3,
        compiler_params=_params(("arbitrary",)),
    )(x, vecs, wi, wi)


def ffn_fwd_out(act, x, vecs, wo, rows, name):
    r_sh, r_sc, r_gt, r_g = rows
    S, D = x.shape
    Fd = wo.shape[0]
    ts = min(FFN_FWD_TS, S)

    def body(a_ref, x_ref, v_ref, wo_ref, xo_ref, y_ref):
        Y = jnp.dot(a_ref[...], wo_ref[...], preferred_element_type=F32)
        y_ref[...] = Y.astype(BF)
        xo_ref[...] = x_ref[...] + (MACARON_W * _row(v_ref, r_gt)) * Y

    tok = pl.BlockSpec((ts, D), lambda i: (i, 0))
    return _pcall(
        body, name=name, grid=(S // ts,),
        in_specs=[pl.BlockSpec((ts, Fd), lambda i: (i, 0)), tok, pl.BlockSpec(vecs.shape, lambda i: (0, 0)),
                  _resident((Fd, D), lambda i: (0, 0))],
        out_specs=[tok, tok], out_shape=[jax.ShapeDtypeStruct((S, D), F32), jax.ShapeDtypeStruct((S, D), BF)],
        compiler_params=_params(("arbitrary",)),
    )(act, x, vecs, wo)


def ffn_bwd(dxo, x, vecs, gs_, us_, y, wi, wo, rows, name):
    r_sh, r_sc, r_gt, r_g = rows
    S, D = x.shape
    Fd = wo.shape[0]
    ts = min(FFN_BWD_TS, S)

    def body(dxo_ref, x_ref, v_ref, g_ref, u_ref, y_ref, wg_ref, wu_ref, wo_ref,
             dx_ref, act_ref, dg_ref, du_ref, hb_ref, dyb_ref, vg_ref):
        @pl.when(pl.program_id(0) == 0)
        def _():
            vg_ref[...] = jnp.zeros_like(vg_ref)

        dxo_v = dxo_ref[...]
        dyb = ((MACARON_W * _row(v_ref, r_gt)) * dxo_v).astype(BF)
        dyb_ref[...] = dyb
        vg_ref[0:1, :] += MACARON_W * _colsum(dxo_v * y_ref[...].astype(F32))
        dA = lax.dot_general(dyb, wo_ref[...], CONTRACT_LAST, preferred_element_type=F32)
        G = g_ref[...].astype(F32)
        U = u_ref[...].astype(F32)
        sg = _sigmoid(G)
        sl = G * sg
        dU = (dA * sl).astype(BF)
        dG = (dA * U * (sg * (1.0 + G * (1.0 - sg)))).astype(BF)
        act_ref[...] = (sl * U).astype(BF)
        dg_ref[...] = dG
        du_ref[...] = dU
        dh = (lax.dot_general(dG, wg_ref[...], CONTRACT_LAST, preferred_element_type=F32)
              + lax.dot_general(dU, wu_ref[...], CONTRACT_LAST, preferred_element_type=F32))
        xv = x_ref[...]
        r = lax.rsqrt(_rowmean(xv * xv) + EPS)
        n = xv * r
        g = _row(v_ref, r_g)
        sc1 = 1.0 + _row(v_ref, r_sc)
        gsc = g * sc1
        hb_ref[...] = (n * gsc + _row(v_ref, r_sh)).astype(BF)
        dhn = dh * n
        vg_ref[1:2, :] += _colsum(dh)
        vg_ref[2:3, :] += _colsum(dhn) * g
        vg_ref[3:4, :] += _colsum(dhn) * sc1
        dn = dh * gsc
        dx_ref[...] = dxo_v + r * (dn - n * _rowmean(dn * n))

    tok = pl.BlockSpec((ts, D), lambda i: (i, 0))
    hid = pl.BlockSpec((ts, Fd), lambda i: (i, 0))
    return _pcall(
        body, name=name, grid=(S // ts,),
        in_specs=[tok, tok, pl.BlockSpec(vecs.shape, lambda i: (0, 0)), hid, hid, tok, _resident((D, Fd), lambda i: (0, 0)),
                  _resident((D, Fd), lambda i: (0, 1)), _resident((Fd, D), lambda i: (0, 0))],
        out_specs=[tok, hid, hid, hid, tok, tok, pl.BlockSpec((SUBLANES, D), lambda i: (0, 0))],
        out_shape=[jax.ShapeDtypeStruct((S, D), F32), jax.ShapeDtypeStruct((S, Fd), BF),
                   jax.ShapeDtypeStruct((S, Fd), BF), jax.ShapeDtypeStruct((S, Fd), BF),
                   jax.ShapeDtypeStruct((S, D), BF), jax.ShapeDtypeStruct((S, D), BF),
                   jax.ShapeDtypeStruct((SUBLANES, D), F32)],
        compiler_params=_params(("arbitrary",)),
    )(dxo, x, vecs, gs_, us_, y, wi, wi, wo)


def norm_matmul(x, vecs, w, rows, name):
    r_sh, r_sc, r_g = rows
    S, D = x.shape
    N = w.shape[1]
    ts = min(512, S)

    def body(x_ref, v_ref, w_ref, o_ref):
        xv = x_ref[...]
        r = lax.rsqrt(_rowmean(xv * xv) + EPS)
        gs = _row(v_ref, r_g) * (1.0 + _row(v_ref, r_sc))
        hb = (xv * r * gs + _row(v_ref, r_sh)).astype(BF)
        o_ref[...] = jnp.dot(hb, w_ref[...], preferred_element_type=F32).astype(BF)

    return _pcall(
        body, name=name, grid=(S // ts,),
        in_specs=[pl.BlockSpec((ts, D), lambda i: (i, 0)), pl.BlockSpec(vecs.shape, lambda i: (0, 0)),
                  _resident((D, N), lambda i: (0, 0))],
        out_specs=pl.BlockSpec((ts, N), lambda i: (i, 0)),
        out_shape=jax.ShapeDtypeStruct((S, N), BF),
        compiler_params=_params(("arbitrary",)),
    )(x, vecs, w)


SEQ_TT = 256
SCAN_SEGMENTS = 4
CONV_BWD_TT = 128
CONV_PAD = 32


def conv_fwd(proj, convw4, conv_b, name):
    S = proj.shape[0]
    M = conv_b.shape[1]
    nb = M // LANES
    tt = min(SEQ_TT, S)

    def body(uv_ref, ug_ref, w_ref, b_ref, cq_ref, qp):
        qp[0:CONV_PAD, :] = jnp.zeros((CONV_PAD, LANES), F32)

        def step(t, carry):
            base = pl.multiple_of(t * tt, tt)
            qp[pl.ds(base + CONV_PAD, tt), :] = uv_ref[pl.ds(base, tt), :].astype(F32) * _sigmoid(ug_ref[pl.ds(base, tt), :].astype(F32))
            acc = jnp.broadcast_to(b_ref[...], (tt, LANES))
            for k in range(CONV_WIDTH):
                acc = acc + w_ref[k:k + 1, :] * qp[pl.ds(base + (CONV_PAD - CONV_WIDTH + 1) + k, tt), :]
            cq_ref[pl.ds(base, tt), :] = acc
            return carry

        lax.fori_loop(0, S // tt, step, 0)

    return _pcall(
        body, name=name, grid=(nb,),
        in_specs=[pl.BlockSpec((S, LANES), lambda c: (0, c)), pl.BlockSpec((S, LANES), lambda c: (0, c + nb)),
                  pl.BlockSpec((None, 32, LANES), lambda c: (c, 0, 0)), pl.BlockSpec((1, LANES), lambda c: (0, c))],
        out_specs=pl.BlockSpec((S, LANES), lambda c: (0, c)),
        out_shape=jax.ShapeDtypeStruct((S, M), F32),
        scratch_shapes=[pltpu.VMEM((S + CONV_PAD, LANES), F32)],
        compiler_params=_params(("arbitrary",)),
    )(proj, proj, convw4, conv_b)


def conv_bwd(dcq, proj, convw4, name):
    S, M = dcq.shape
    nb = M // LANES
    tt = min(CONV_BWD_TT, S)

    def body(dcq_ref, uv_ref, ug_ref, w_ref, duv_ref, dug_ref, dw_ref, db_ref, dp, dw8, db8):
        dp[S:S + CONV_PAD, :] = jnp.zeros((CONV_PAD, LANES), F32)
        dw8[...] = jnp.zeros_like(dw8)
        db8[...] = jnp.zeros_like(db8)

        def fill(t, carry):
            base = pl.multiple_of(t * tt, tt)
            dp[pl.ds(base, tt), :] = dcq_ref[pl.ds(base, tt), :].astype(F32)
            return carry

        lax.fori_loop(0, S // tt, fill, 0)

        def step(t, carry):
            base = pl.multiple_of(t * tt, tt)
            uv = uv_ref[pl.ds(base, tt), :].astype(F32)
            sg = _sigmoid(ug_ref[pl.ds(base, tt), :].astype(F32))
            q_t = uv * sg
            db8[...] += dp[pl.ds(base, tt), :].reshape(tt // SUBLANES, SUBLANES, LANES).sum(axis=0)
            dq = jnp.zeros((tt, LANES), F32)
            for k in range(CONV_WIDTH):
                shifted = dp[pl.ds(base + (CONV_WIDTH - 1) - k, tt), :]
                dw8[k] += (shifted * q_t).reshape(tt // SUBLANES, SUBLANES, LANES).sum(axis=0)
                dq = dq + w_ref[k:k + 1, :] * shifted
            duv_ref[pl.ds(base, tt), :] = (dq * sg).astype(BF)
            dug_ref[pl.ds(base, tt), :] = (dq * uv * sg * (1.0 - sg)).astype(BF)
            return carry

        lax.fori_loop(0, S // tt, step, 0)
        dw_ref[...] = jnp.zeros_like(dw_ref)
        for k in range(CONV_WIDTH):
            dw_ref[k:k + 1, :] = _colsum(dw8[k])
        db_ref[...] = _colsum(db8[...])

    col = lambda o: pl.BlockSpec((S, LANES), lambda c: (0, c + o))
    return _pcall(
        body, name=name, grid=(nb,),
        in_specs=[col(0), col(0), col(nb), pl.BlockSpec((None, 32, LANES), lambda c: (c, 0, 0))],
        out_specs=[col(0), col(0), pl.BlockSpec((None, 32, LANES), lambda c: (c, 0, 0)),
                   pl.BlockSpec((1, LANES), lambda c: (0, c))],
        out_shape=[jax.ShapeDtypeStruct((S, M), BF), jax.ShapeDtypeStruct((S, M), BF),
                   jax.ShapeDtypeStruct((nb, 32, LANES), F32), jax.ShapeDtypeStruct((1, M), F32)],
        scratch_shapes=[pltpu.VMEM((S + CONV_PAD, LANES), F32),
                        pltpu.VMEM((32, SUBLANES, LANES), F32), pltpu.VMEM((SUBLANES, LANES), F32)],
        compiler_params=_params(("arbitrary",)),
    )(dcq, proj, proj, convw4)


def _log_sigmoid(x):
    return jnp.minimum(x, 0.0) - jnp.log(1.0 + jnp.exp(-jnp.abs(x)))


def _rg_gate_terms(ra, ls):
    la = RG_C * ra * ls
    a = jnp.exp(la)
    th = jnp.tanh(la)
    mult = jnp.sqrt(-2.0 * th / (1.0 - th))
    return a, mult


def rnn_fwd(proj, rnnw4, rnn_b, bda, bdi, b_a, b_i, lam, name):
    S = proj.shape[0]
    M = rnn_b.shape[1]
    nb = M // LANES
    tt = min(SEQ_TT, S)
    KW = RNN_CONV_WIDTH
    nseg = SCAN_SEGMENTS if S % (SCAN_SEGMENTS * tt) == 0 else 1

    def body(ux_ref, w_ref, rb_ref, bda_ref, bdi_ref, ba_ref, bi_ref, lam_ref,
             xr_ref, ra_ref, ii_ref, h_ref, uxp, a_sc, b_sc):
        uxp[0:SUBLANES, :] = jnp.zeros((SUBLANES, LANES), F32)
        ls = _log_sigmoid(lam_ref[...])

        def step(t, carry):
            base = pl.multiple_of(t * tt, tt)
            uxp[pl.ds(base + SUBLANES, tt), :] = ux_ref[pl.ds(base, tt), :].astype(F32)
            xr = jnp.broadcast_to(rb_ref[...], (tt, LANES))
            for k in range(KW):
                xr = xr + w_ref[k:k + 1, :] * uxp[pl.ds(base + (SUBLANES - KW + 1) + k, tt), :]
            xb = xr.astype(BF)
            ra = _sigmoid(jnp.dot(xb, bda_ref[...], preferred_element_type=F32) + ba_ref[...])
            ii = _sigmoid(jnp.dot(xb, bdi_ref[...], preferred_element_type=F32) + bi_ref[...])
            a, mult = _rg_gate_terms(ra, ls)
            xr_ref[pl.ds(base, tt), :] = xr
            ra_ref[pl.ds(base, tt), :] = ra
            ii_ref[pl.ds(base, tt), :] = ii
            a_sc[pl.ds(base, tt), :] = a
            b_sc[pl.ds(base, tt), :] = mult * (ii * xr)
            return carry

        lax.fori_loop(0, S // tt, step, 0)

        rows = lax.broadcasted_iota(I32, (SUBLANES, LANES), 0)
        seg = S // nseg
        last = lambda v: jnp.broadcast_to(v[SUBLANES - 1:SUBLANES, :], (SUBLANES, LANES))

        def scan(t, carry):
            hs, ps = carry
            new_h, new_p = [], []
            for s in range(nseg):
                base = pl.multiple_of(s * seg + t * SUBLANES, SUBLANES)
                A = a_sc[pl.ds(base, SUBLANES), :]
                B = b_sc[pl.ds(base, SUBLANES), :]
                for d in (1, 2, 4):
                    As = jnp.where(rows >= d, pltpu.roll(A, d, axis=0), 1.0)
                    Bs = jnp.where(rows >= d, pltpu.roll(B, d, axis=0), 0.0)
                    B = A * Bs + B
                    A = A * As
                hh = B + A * hs[s]
                h_ref[pl.ds(base, SUBLANES), :] = hh
                pp = A * ps[s]
                if s > 0:
                    a_sc[pl.ds(base, SUBLANES), :] = pp
                new_h.append(last(hh))
                new_p.append(last(pp))
            return tuple(new_h), tuple(new_p)

        zero8 = jnp.zeros((SUBLANES, LANES), F32)
        one8 = jnp.ones((SUBLANES, LANES), F32)
        hs, ps = lax.fori_loop(0, seg // SUBLANES, scan, ((zero8,) * nseg, (one8,) * nseg))
        carry_in = hs[0]
        for s in range(1, nseg):
            c_row = carry_in[0:1, :]

            def fix(t, c, s=s, c_row=c_row):
                base = pl.multiple_of(s * seg + t * tt, tt)
                h_ref[pl.ds(base, tt), :] = h_ref[pl.ds(base, tt), :] + a_sc[pl.ds(base, tt), :] * c_row
                return c

            lax.fori_loop(0, seg // tt, fix, 0)
            carry_in = hs[s] + ps[s] * carry_in

    col = lambda o: pl.BlockSpec((S, LANES), lambda c: (0, c + o))
    vec = pl.BlockSpec((1, LANES), lambda c: (0, c))
    diag = pl.BlockSpec((LANES, LANES), lambda c: (c, c))
    return _pcall(
        body, name=name, grid=(nb,),
        in_specs=[col(2 * nb), pl.BlockSpec((None, SUBLANES, LANES), lambda c: (c, 0, 0)), vec, diag, diag, vec, vec, vec],
        out_specs=[col(0)] * 4,
        out_shape=[jax.ShapeDtypeStruct((S, M), F32)] * 4,
        scratch_shapes=[pltpu.VMEM((S + SUBLANES, LANES), F32), pltpu.VMEM((S, LANES), F32), pltpu.VMEM((S, LANES), F32)],
        compiler_params=_params(("arbitrary",)),
    )(proj, rnnw4, rnn_b, bda, bdi, b_a, b_i, lam)


def rnn_bwd(dhout, h, xr, ra, ii, proj, rnnw4, bda, bdi, lam, name):
    S, M = h.shape
    nb = M // LANES
    tt = min(SEQ_TT, S)
    KW = RNN_CONV_WIDTH
    SL = SUBLANES
    nseg = SCAN_SEGMENTS if S % (SCAN_SEGMENTS * tt) == 0 else 1

    def body(dh_ref, h_ref, xr_ref, ra_ref, ii_ref, ux_ref, w_ref, bda_ref, bdi_ref, lam_ref,
             dux_ref, dwa_ref, dwi_ref, drw_ref, vec_ref,
             a_sc, hp, g_sc, dpa_sc, dpi_sc, dxp, uxp, acc8, drw8, p_sc):
        zero8 = jnp.zeros((SL, LANES), F32)
        a_sc[S:S + SL, :] = zero8
        hp[0:SL, :] = zero8
        dxp[S:S + SL, :] = zero8
        uxp[0:SL, :] = zero8
        acc8[...] = jnp.zeros_like(acc8)
        drw8[...] = jnp.zeros_like(drw8)
        lamv = lam_ref[...]
        ls = _log_sigmoid(lamv)

        def fill(t, carry):
            base = pl.multiple_of(t * tt, tt)
            a_sc[pl.ds(base, tt), :] = jnp.exp(RG_C * ra_ref[pl.ds(base, tt), :] * ls)
            hp[pl.ds(base + SL, tt), :] = h_ref[pl.ds(base, tt), :]
            uxp[pl.ds(base + SL, tt), :] = ux_ref[pl.ds(base, tt), :].astype(F32)
            return carry

        lax.fori_loop(0, S // tt, fill, 0)

        rows = lax.broadcasted_iota(I32, (SL, LANES), 0)
        seg = S // nseg
        nt8 = seg // SL
        first = lambda v: jnp.broadcast_to(v[0:1, :], (SL, LANES))

        def rscan(t, carry):
            gs, ps = carry
            new_g, new_p = [], []
            for s in range(nseg):
                base = pl.multiple_of(s * seg + (nt8 - 1 - t) * SL, SL)
                A = a_sc[pl.ds(base + 1, SL), :]
                B = dh_ref[pl.ds(base, SL), :]
                for d in (1, 2, 4):
                    As = jnp.where(rows < SL - d, pltpu.roll(A, SL - d, axis=0), 1.0)
                    Bs = jnp.where(rows < SL - d, pltpu.roll(B, SL - d, axis=0), 0.0)
                    B = A * Bs + B
                    A = A * As
                g = B + A * gs[s]
                g_sc[pl.ds(base, SL), :] = g
                pp = A * ps[s]
                if s < nseg - 1:
                    p_sc[pl.ds(base, SL), :] = pp
                new_g.append(first(g))
                new_p.append(first(pp))
            return tuple(new_g), tuple(new_p)

        one8 = jnp.ones((SL, LANES), F32)
        gs, ps = lax.fori_loop(0, nt8, rscan, ((zero8,) * nseg, (one8,) * nseg))
        carry_in = gs[nseg - 1]
        for s in range(nseg - 2, -1, -1):
            c_row = carry_in[0:1, :]

            def fix(t, c, s=s, c_row=c_row):
                base = pl.multiple_of(s * seg + t * tt, tt)
                g_sc[pl.ds(base, tt), :] = g_sc[pl.ds(base, tt), :] + p_sc[pl.ds(base, tt), :] * c_row
                return c

            lax.fori_loop(0, seg // tt, fix, 0)
            carry_in = gs[s] + ps[s] * carry_in

        def red8(v):
            return v.reshape(tt // SL, SL, LANES).sum(axis=0)

        def step(t, carry):
            base = pl.multiple_of(t * tt, tt)
            g = g_sc[pl.ds(base, tt), :]
            hprev = hp[pl.ds(base + SL - 1, tt), :]
            xr_t = xr_ref[pl.ds(base, tt), :]
            ra_t = ra_ref[pl.ds(base, tt), :]
            ii_t = ii_ref[pl.ds(base, tt), :]
            a, mult = _rg_gate_terms(ra_t, ls)
            gx = g * xr_t
            dmult = gx * ii_t
            dii = gx * mult
            dxr = g * (mult * ii_t)
            dla = g * hprev * a - dmult * (a * a) / mult
            acc8[3] += red8(dla * ra_t)
            dpa = dla * (RG_C * ls) * ra_t * (1.0 - ra_t)
            dpi = dii * ii_t * (1.0 - ii_t)
            dpab = dpa.astype(BF)
            dpib = dpi.astype(BF)
            dxr = dxr + (lax.dot_general(dpab, bda_ref[...], CONTRACT_LAST, preferred_element_type=F32)
                         + lax.dot_general(dpib, bdi_ref[...], CONTRACT_LAST, preferred_element_type=F32))
            dpa_sc[pl.ds(base, tt), :] = dpab
            dpi_sc[pl.ds(base, tt), :] = dpib
            dxp[pl.ds(base, tt), :] = dxr
            acc8[0] += red8(dxr)
            acc8[1] += red8(dpa)
            acc8[2] += red8(dpi)
            return carry

        lax.fori_loop(0, S // tt, step, 0)

        def convb(t, carry):
            base = pl.multiple_of(t * tt, tt)
            d_t = dxp[pl.ds(base, tt), :]
            dux = jnp.zeros((tt, LANES), F32)
            for k in range(KW):
                drw8[k] += red8(d_t * uxp[pl.ds(base + (SL - KW + 1) + k, tt), :])
                dux = dux + w_ref[k:k + 1, :] * dxp[pl.ds(base + (KW - 1) - k, tt), :]
            dux_ref[pl.ds(base, tt), :] = dux.astype(BF)
            return carry

        lax.fori_loop(0, S // tt, convb, 0)

        xb = xr_ref[...].astype(BF)
        dwa_ref[...] = lax.dot_general(xb, dpa_sc[...], CONTRACT_FIRST, preferred_element_type=F32)
        dwi_ref[...] = lax.dot_general(xb, dpi_sc[...], CONTRACT_FIRST, preferred_element_type=F32)
        drw_ref[...] = jnp.zeros_like(drw_ref)
        vec_ref[...] = jnp.zeros_like(vec_ref)
        for k in range(KW):
            drw_ref[k:k + 1, :] = _colsum(drw8[k])
        for k in range(3):
            vec_ref[k:k + 1, :] = _colsum(acc8[k])
        vec_ref[3:4, :] = _colsum(acc8[3]) * (RG_C * _sigmoid(-lamv))

    col = lambda o: pl.BlockSpec((S, LANES), lambda c: (0, c + o))
    vec = pl.BlockSpec((1, LANES), lambda c: (0, c))
    diag = pl.BlockSpec((LANES, LANES), lambda c: (c, c))
    blk3 = lambda r: pl.BlockSpec((None, r, LANES), lambda c: (c, 0, 0))
    return _pcall(
        body, name=name, grid=(nb,),
        in_specs=[col(0), col(0), col(0), col(0), col(0), col(2 * nb), blk3(SL), diag, diag, vec],
        out_specs=[col(0), blk3(LANES), blk3(LANES), blk3(SL), pl.BlockSpec((SL, LANES), lambda c: (0, c))],
        out_shape=[jax.ShapeDtypeStruct((S, M), BF), jax.ShapeDtypeStruct((nb, LANES, LANES), F32),
                   jax.ShapeDtypeStruct((nb, LANES, LANES), F32), jax.ShapeDtypeStruct((nb, SL, LANES), F32),
                   jax.ShapeDtypeStruct((SL, M), F32)],
        scratch_shapes=[pltpu.VMEM((S + SL, LANES), F32), pltpu.VMEM((S + SL, LANES), F32), pltpu.VMEM((S, LANES), F32),
                        pltpu.VMEM((S, LANES), BF), pltpu.VMEM((S, LANES), BF), pltpu.VMEM((S + SL, LANES), F32),
                        pltpu.VMEM((S + SL, LANES), F32), pltpu.VMEM((SL, SL, LANES), F32), pltpu.VMEM((SL, SL, LANES), F32),
                        pltpu.VMEM((S, LANES), F32)],
        compiler_params=_params(("arbitrary",)),
    )(dhout, h, xr, ra, ii, proj, rnnw4, bda, bdi, lam)


GELU_K = 0.7978845608028654
GELU_C = 0.044715


def _layernorm_parts(cq):
    mu = _rowmean(cq)
    d = cq - mu
    rstd = lax.rsqrt(_rowmean(d * d) + EPS)
    return d * rstd, rstd


def mix_out(cq, proj, h, x, vecs, lnv, wout, name):
    S, D = x.shape
    M = cq.shape[1]
    ts = min(512, S)

    def body(cq_ref, uy_ref, h_ref, x_ref, v_ref, ln_ref, w_ref, xo_ref, ym_ref, yc_ref):
        z, _ = _layernorm_parts(cq_ref[...])
        l = z * _row(ln_ref, 0) + _row(ln_ref, 1)
        yc_ref[:, 0:M] = (l * _sigmoid(l)).astype(BF)
        uy = uy_ref[...].astype(F32)
        gelu = 0.5 * uy * (1.0 + jnp.tanh(GELU_K * (uy + GELU_C * uy * uy * uy)))
        yc_ref[:, M:2 * M] = (gelu * h_ref[...]).astype(BF)
        ym = jnp.dot(yc_ref[...], w_ref[...], preferred_element_type=F32)
        ym_ref[...] = ym.astype(BF)
        xo_ref[...] = x_ref[...] + _row(v_ref, R_GT2) * ym

    tok = pl.BlockSpec((ts, D), lambda i: (i, 0))
    mtok = lambda o: pl.BlockSpec((ts, M), lambda i: (i, o))
    return _pcall(
        body, name=name, grid=(S // ts,),
        in_specs=[mtok(0), mtok(3), mtok(0), tok, pl.BlockSpec(vecs.shape, lambda i: (0, 0)),
                  pl.BlockSpec(lnv.shape, lambda i: (0, 0)), pl.BlockSpec(wout.shape, lambda i: (0, 0))],
        out_specs=[tok, tok, pl.BlockSpec((ts, 2 * M), lambda i: (i, 0))],
        out_shape=[jax.ShapeDtypeStruct((S, D), F32), jax.ShapeDtypeStruct((S, D), BF),
                   jax.ShapeDtypeStruct((S, 2 * M), BF)],
        compiler_params=_params(("arbitrary",)),
    )(cq, proj, h, x, vecs, lnv, wout)


def mix_out_bwd(dxo, ym, vecs, wout, cq, lnv, proj, h, name):
    S, D = dxo.shape
    M = cq.shape[1]
    ts = min(512, S)

    def body(dxo_ref, ym_ref, v_ref, w_ref, cq_ref, ln_ref, uy_ref, h_ref,
             dcq_ref, dh_ref, duy_ref, dyb_ref, vgd_ref, vgm_ref):
        @pl.when(pl.program_id(0) == 0)
        def _():
            vgd_ref[...] = jnp.zeros_like(vgd_ref)
            vgm_ref[...] = jnp.zeros_like(vgm_ref)

        dxo_v = dxo_ref[...]
        dyb = (_row(v_ref, R_GT2) * dxo_v).astype(BF)
        dyb_ref[...] = dyb
        vgd_ref[0:1, :] += _colsum(dxo_v * ym_ref[...].astype(F32))
        dycat = lax.dot_general(dyb, w_ref[...], CONTRACT_LAST, preferred_element_type=F32)
        dyc = dycat[:, 0:M]
        dyr = dycat[:, M:2 * M]
        z, rstd = _layernorm_parts(cq_ref[...])
        lng = _row(ln_ref, 0)
        l = z * lng + _row(ln_ref, 1)
        sl = _sigmoid(l)
        dl = dyc * (sl * (1.0 + l * (1.0 - sl)))
        vgm_ref[0:1, :] += _colsum(dl * z)
        vgm_ref[1:2, :] += _colsum(dl)
        dz = dl * lng
        dcq_ref[...] = (rstd * (dz - _rowmean(dz) - z * _rowmean(dz * z))).astype(BF)
        uy = uy_ref[...].astype(F32)
        u2 = uy * uy
        th = jnp.tanh(GELU_K * (uy + GELU_C * uy * u2))
        gelu = 0.5 * uy * (1.0 + th)
        dgelu = 0.5 * (1.0 + th) + 0.5 * uy * (1.0 - th * th) * (GELU_K * (1.0 + 3.0 * GELU_C * u2))
        dh_ref[...] = dyr * gelu
        duy_ref[...] = (dyr * h_ref[...] * dgelu).astype(BF)

    tok = pl.BlockSpec((ts, D), lambda i: (i, 0))
    mtok = lambda o: pl.BlockSpec((ts, M), lambda i: (i, o))
    return _pcall(
        body, name=name, grid=(S // ts,),
        in_specs=[tok, tok, pl.BlockSpec(vecs.shape, lambda i: (0, 0)), pl.BlockSpec(wout.shape, lambda i: (0, 0)),
                  mtok(0), pl.BlockSpec(lnv.shape, lambda i: (0, 0)), mtok(3), mtok(0)],
        out_specs=[mtok(0), mtok(0), mtok(0), tok, pl.BlockSpec((SUBLANES, D), lambda i: (0, 0)),
                   pl.BlockSpec((SUBLANES, M), lambda i: (0, 0))],
        out_shape=[jax.ShapeDtypeStruct((S, M), BF), jax.ShapeDtypeStruct((S, M), F32), jax.ShapeDtypeStruct((S, M), BF),
                   jax.ShapeDtypeStruct((S, D), BF),
                   jax.ShapeDtypeStruct((SUBLANES, D), F32), jax.ShapeDtypeStruct((SUBLANES, M), F32)],
        compiler_params=_params(("arbitrary",)),
    )(dxo, ym, vecs, wout, cq, lnv, proj, h)


def mix_in_bwd(dparts, x, dxo, vecs, win, name):
    S, D = x.shape
    M = dparts[0].shape[1]
    ts = min(512, S)

    def body(d0, d1, d2, d3, x_ref, dxo_ref, v_ref, w_ref, dx_ref, hb_ref, dp_ref, vg_ref):
        @pl.when(pl.program_id(0) == 0)
        def _():
            vg_ref[...] = jnp.zeros_like(vg_ref)

        for q, dref in enumerate((d0, d1, d2, d3)):
            dp_ref[:, q * M:(q + 1) * M] = dref[...].astype(BF)
        dh = lax.dot_general(dp_ref[...], w_ref[...], CONTRACT_LAST, preferred_element_type=F32)
        xv = x_ref[...]
        r = lax.rsqrt(_rowmean(xv * xv) + EPS)
        n = xv * r
        g = _row(v_ref, R_G2)
        sc1 = 1.0 + _row(v_ref, R_SC2)
        gsc = g * sc1
        hb_ref[...] = (n * gsc + _row(v_ref, R_SH2)).astype(BF)
        dhn = dh * n
        vg_ref[0:1, :] += _colsum(dh)
        vg_ref[1:2, :] += _colsum(dhn) * g
        vg_ref[2:3, :] += _colsum(dhn) * sc1
        dn = dh * gsc
        dx_ref[...] = dxo_ref[...] + r * (dn - n * _rowmean(dn * n))

    tok = pl.BlockSpec((ts, D), lambda i: (i, 0))
    mtok = pl.BlockSpec((ts, M), lambda i: (i, 0))
    return _pcall(
        body, name=name, grid=(S // ts,),
        in_specs=[mtok] * 4 + [tok, tok, pl.BlockSpec(vecs.shape, lambda i: (0, 0)), pl.BlockSpec(win.shape, lambda i: (0, 0))],
        out_specs=[tok, tok, pl.BlockSpec((ts, 4 * M), lambda i: (i, 0)), pl.BlockSpec((SUBLANES, D), lambda i: (0, 0))],
        out_shape=[jax.ShapeDtypeStruct((S, D), F32), jax.ShapeDtypeStruct((S, D), BF),
                   jax.ShapeDtypeStruct((S, 4 * M), BF), jax.ShapeDtypeStruct((SUBLANES, D), F32)],
        compiler_params=_params(("arbitrary",)),
    )(*dparts, x, dxo, vecs, win)


def _adamw(w, g, m, v):
    m = ADAM_B1 * m + (1.0 - ADAM_B1) * g
    v = ADAM_B2 * v + (1.0 - ADAM_B2) * (g * g)
    m_hat = m / (1.0 - ADAM_B1 ** ADAM_STEP)
    v_hat = v / (1.0 - ADAM_B2 ** ADAM_STEP)
    delta = -ADAM_LR * (m_hat / (jnp.sqrt(v_hat) + ADAM_EPS) + ADAM_WD * w)
    return delta, m, v


def adam_big(w, g_mine, g_sib, m, v, cidx, name):
    R, C = w.shape
    hr = R // 2
    tr = 256 if hr % 256 == 0 else hr
    tc = C if C <= 1536 else (1152 if C % 1152 == 0 else 1024)
    assert hr % tr == 0 and C % tc == 0 and g_mine.shape == (hr, C)
    nrb = hr // tr

    def body(ci_ref, w_ref, gm_ref, gs_ref, m_ref, v_ref, g_ref, d_ref, nm_ref, nv_ref):
        mine = (pl.program_id(0) // nrb) == ci_ref[0]
        g = jnp.where(mine, gm_ref[...], gs_ref[...])
        d, nm, nv = _adamw(w_ref[...], g, m_ref[...], v_ref[...])
        g_ref[...] = g
        d_ref[...] = d
        nm_ref[...] = nm
        nv_ref[...] = nv

    blk = pl.BlockSpec((tr, tc), lambda i, j, ci: (i, j))
    mine_spec = pl.BlockSpec((tr, tc), lambda i, j, ci: (jnp.where(i // nrb == ci[0], i % nrb, 0), j))
    sib_spec = pl.BlockSpec((tr, tc), lambda i, j, ci: (jnp.where(i // nrb == ci[0], 0, i % nrb), j))
    gs = pltpu.PrefetchScalarGridSpec(num_scalar_prefetch=1, grid=(R // tr, C // tc),
                                      in_specs=[blk, mine_spec, sib_spec, blk, blk], out_specs=[blk] * 4)
    return _pcall(body, name=name, grid_spec=gs, out_shape=[jax.ShapeDtypeStruct((R, C), F32)] * 4,
                  compiler_params=_params(("parallel", "parallel")))(cidx, w, g_mine, g_sib, m, v)


def adam_cond(c_all, dmod, w, m, v, name):
    B, Kin = c_all.shape
    N = w.shape[1]
    tn = 768 if N % 768 == 0 else 256
    assert N % tn == 0

    def body(c_ref, d_ref, w_ref, m_ref, v_ref, g_ref, dl_ref, nm_ref, nv_ref):
        cv = c_ref[...]
        ca = cv * _sigmoid(cv)
        g = lax.dot_general(ca.astype(BF), d_ref[...].astype(BF), CONTRACT_FIRST, preferred_element_type=F32)
        d, nm, nv = _adamw(w_ref[...], g, m_ref[...], v_ref[...])
        g_ref[...] = g
        dl_ref[...] = d
        nm_ref[...] = nm
        nv_ref[...] = nv

    blk = pl.BlockSpec((Kin, tn), lambda n: (0, n))
    return _pcall(
        body, name=name, grid=(N // tn,),
        in_specs=[pl.BlockSpec((B, Kin), lambda n: (0, 0)), pl.BlockSpec((B, tn), lambda n: (0, n)), blk, blk, blk],
        out_specs=[blk] * 4, out_shape=[jax.ShapeDtypeStruct((Kin, N), F32)] * 4,
        compiler_params=_params(("parallel",)),
    )(c_all, dmod, w, m, v)


def adam_small(ws, gs, ms, vs, name):
    n = len(ws)

    def body(*refs):
        ins, outs = refs[:4 * n], refs[4 * n:]
        for k in range(n):
            d, nm, nv = _adamw(ins[k][...], ins[n + k][...], ins[2 * n + k][...], ins[3 * n + k][...])
            outs[k][...] = d
            outs[n + k][...] = nm
            outs[2 * n + k][...] = nv

    specs = [pl.BlockSpec(w.shape, lambda i: (0, 0)) for w in ws]
    shapes = [jax.ShapeDtypeStruct(w.shape, F32) for w in ws]
    out = _pcall(body, name=name, grid=(1,), in_specs=specs * 4, out_specs=specs * 3, out_shape=shapes * 3,
                 compiler_params=_params(("arbitrary",)))(*ws, *gs, *ms, *vs)
    return out[:n], out[n:2 * n], out[2 * n:]


def _me():
    return lax.axis_index("x"), lax.axis_index("y"), lax.axis_index("c")


def _flip(x, y, p):
    return (x ^ (p >> 1) if (p >> 1) else x), (y ^ (p & 1) if (p & 1) else y)


def _handshake(peers):
    barrier = pltpu.get_barrier_semaphore()
    for peer in peers:
        pl.semaphore_signal(barrier, inc=1, device_id=peer, device_id_type=MESH)
    pl.semaphore_wait(barrier, len(peers))


def _seq_call(body, *, name, n_in, out_shape, sem_shapes, collective_id):
    del n_in
    return pl.kernel(body, out_type=out_shape, mesh=plsc.ScalarSubcoreMesh(axis_name="sq", num_cores=1), name=name,
                     scratch_types=sem_shapes, compiler_params=pltpu.CompilerParams(collective_id=collective_id))


def _hbm_comm_call(body, *, name, n_in, out_shape, sem_shapes, seq_id):
    if seq_id is not None:
        return _seq_call(body, name=name, n_in=n_in, out_shape=out_shape, sem_shapes=sem_shapes, collective_id=seq_id)
    anyspec = pl.BlockSpec(memory_space=pl.ANY)
    return _pcall(body, name=name, in_specs=[anyspec] * n_in, out_specs=[anyspec] * len(out_shape), out_shape=out_shape,
                  scratch_shapes=sem_shapes, compiler_params=_params())


def allgather_devices(v, name, with_sum=False):
    R, L = v.shape

    def body(v_ref, out_ref, *rest):
        if with_sum:
            sum_ref, send_sems, recv_sems = rest
        else:
            send_sems, recv_sems = rest
        x, y, c = _me()
        me = 4 * x + 2 * y + c
        out_ref[me] = v_ref[...]
        copies = []
        for p in range(1, N_DEV):
            px, py = _flip(x, y, p >> 1)
            pc = (1 - c) if (p & 1) else c
            peer = 4 * px + 2 * py + pc
            send = pltpu.make_async_remote_copy(src_ref=v_ref, dst_ref=out_ref.at[me], send_sem=send_sems.at[p - 1],
                                                recv_sem=recv_sems.at[p - 1], device_id=(px, py, pc), device_id_type=MESH)
            send.start()
            recv = pltpu.make_async_remote_copy(src_ref=v_ref, dst_ref=out_ref.at[peer], send_sem=send_sems.at[p - 1],
                                                recv_sem=recv_sems.at[p - 1], device_id=(px, py, pc), device_id_type=MESH)
            copies.append((send, recv))
        for send, recv in copies:
            recv.wait_recv()
        for send, recv in copies:
            send.wait_send()
        if with_sum:
            s = out_ref[0]
            for k in range(1, N_DEV):
                s = s + out_ref[k]
            sum_ref[...] = s

    vm = pl.BlockSpec(memory_space=pltpu.VMEM)
    out_shape = [jax.ShapeDtypeStruct((N_DEV, R, L), F32)]
    if with_sum:
        out_shape.append(jax.ShapeDtypeStruct((R, L), F32))
    return _pcall(
        body, name=name, in_specs=[vm], out_specs=[vm] * len(out_shape), out_shape=out_shape,
        scratch_shapes=[pltpu.SemaphoreType.DMA((N_DEV - 1,)), pltpu.SemaphoreType.DMA((N_DEV - 1,))],
        compiler_params=_params(),
    )(v)


def allgather_devices_hbm(v, name, seq_id):
    R, L = v.shape

    def body(v_ref, out_ref, send_sems, recv_sems, local_sem):
        x, y, c = _me()
        me = 4 * x + 2 * y + c
        peers = []
        for p in range(1, N_DEV):
            px, py = _flip(x, y, p >> 1)
            peers.append((px, py, (1 - c) if (p & 1) else c))
        _handshake(peers)
        lc = pltpu.make_async_copy(v_ref, out_ref.at[me], local_sem)
        lc.start()
        copies = []
        for p, (px, py, pc) in enumerate(peers):
            send = pltpu.make_async_remote_copy(src_ref=v_ref, dst_ref=out_ref.at[me], send_sem=send_sems.at[p],
                                                recv_sem=recv_sems.at[p], device_id=(px, py, pc), device_id_type=MESH)
            send.start()
            recv = pltpu.make_async_remote_copy(src_ref=v_ref, dst_ref=out_ref.at[4 * px + 2 * py + pc], send_sem=send_sems.at[p],
                                                recv_sem=recv_sems.at[p], device_id=(px, py, pc), device_id_type=MESH)
            copies.append((send, recv))
        for send, recv in copies:
            recv.wait_recv()
        for send, recv in copies:
            send.wait_send()
        lc.wait()

    return _seq_call(body, name=name, n_in=1, out_shape=[jax.ShapeDtypeStruct((N_DEV, R, L), F32)],
                     sem_shapes=[pltpu.SemaphoreType.DMA((N_DEV - 1,)), pltpu.SemaphoreType.DMA((N_DEV - 1,)),
                                 pltpu.SemaphoreType.DMA], collective_id=seq_id)(v)[0]


def sum_slots(g, name):
    n, R, L = g.shape
    tr = 216 if R % 216 == 0 else R
    assert R % tr == 0 and tr % SUBLANES == 0

    def body(g_ref, o_ref):
        s = g_ref[0]
        for k in range(1, n):
            s = s + g_ref[k]
        o_ref[...] = s

    return _pcall(body, name=name, grid=(R // tr,), in_specs=[pl.BlockSpec((n, tr, L), lambda i: (0, i, 0))],
                  out_specs=pl.BlockSpec((tr, L), lambda i: (i, 0)), out_shape=jax.ShapeDtypeStruct((R, L), F32),
                  compiler_params=_params(("parallel",)))(g)


def allgather_chips(v, name):
    R, L = v.shape

    def body(v_ref, out_ref, send_sems, recv_sems):
        x, y, c = _me()
        chip = 2 * x + y
        out_ref[chip] = v_ref[...]
        copies = []
        for p in range(1, N_CHIPS):
            px, py = _flip(x, y, p)
            send = pltpu.make_async_remote_copy(src_ref=v_ref, dst_ref=out_ref.at[chip], send_sem=send_sems.at[p - 1],
                                                recv_sem=recv_sems.at[p - 1], device_id=(px, py, c), device_id_type=MESH)
            send.start()
            recv = pltpu.make_async_remote_copy(src_ref=v_ref, dst_ref=out_ref.at[2 * px + py], send_sem=send_sems.at[p - 1],
                                                recv_sem=recv_sems.at[p - 1], device_id=(px, py, c), device_id_type=MESH)
            copies.append((send, recv))
        for send, recv in copies:
            recv.wait_recv()
        for send, recv in copies:
            send.wait_send()

    vm = pl.BlockSpec(memory_space=pltpu.VMEM)
    return _pcall(
        body, name=name, in_specs=[vm], out_specs=vm, out_shape=jax.ShapeDtypeStruct((N_CHIPS, R, L), F32),
        scratch_shapes=[pltpu.SemaphoreType.DMA((N_CHIPS - 1,)), pltpu.SemaphoreType.DMA((N_CHIPS - 1,))],
        compiler_params=_params(),
    )(v)


def _shard_window(ref, kind, shard_shape, chip, half):
    r, c = shard_shape
    hr = r // 2
    if kind == "col":
        return ref.at[pl.ds(pl.multiple_of(half * hr, hr), hr), pl.ds(pl.multiple_of(chip * c, c), c)]
    return ref.at[pl.ds(pl.multiple_of(chip * r + half * hr, hr), hr), :]


def allgather_weights(shards, kinds, name, seq_id=None):
    n = len(shards)
    fulls = []
    for s, kind in zip(shards, kinds):
        r, c = s.shape
        fulls.append(jax.ShapeDtypeStruct((r, N_CHIPS * c) if kind == "col" else (N_CHIPS * r, c), s.dtype))

    def body(*refs):
        srcs, outs = refs[:n], refs[n:2 * n]
        send_sems, recv_sems, fsend_sems, frecv_sems = refs[2 * n:]
        x, y, c = _me()
        chip = 2 * x + y
        sib = (x, y, 1 - c)
        if seq_id is not None:
            _handshake([(*_flip(x, y, p), c) for p in range(1, N_CHIPS)] + [sib])
        sends, fwds = [], []
        for i in range(n):
            shp = srcs[i].shape
            hr = shp[0] // 2
            my_half = srcs[i].at[pl.ds(pl.multiple_of(c * hr, hr), hr), :]
            for p in range(1, N_CHIPS):
                px, py = _flip(x, y, p)
                k = i * (N_CHIPS - 1) + p - 1
                cp = pltpu.make_async_remote_copy(src_ref=my_half, dst_ref=_shard_window(outs[i], kinds[i], shp, chip, c),
                                                  send_sem=send_sems.at[k], recv_sem=recv_sems.at[k],
                                                  device_id=(px, py, c), device_id_type=MESH)
                cp.start()
                sends.append(cp)
        for i in range(n):
            shp = srcs[i].shape
            for p in range(1, N_CHIPS):
                px, py = _flip(x, y, p)
                k = i * (N_CHIPS - 1) + p - 1
                landed = _shard_window(outs[i], kinds[i], shp, 2 * px + py, c)
                pltpu.make_async_remote_copy(src_ref=landed, dst_ref=landed, send_sem=send_sems.at[k], recv_sem=recv_sems.at[k],
                                             device_id=(px, py, c), device_id_type=MESH).wait_recv()
                fw = pltpu.make_async_remote_copy(src_ref=landed, dst_ref=landed, send_sem=fsend_sems.at[k],
                                                  recv_sem=frecv_sems.at[k], device_id=sib, device_id_type=MESH)
                fw.start()
                fwds.append(fw)
        for i in range(n):
            shp = srcs[i].shape
            for p in range(1, N_CHIPS):
                px, py = _flip(x, y, p)
                k = i * (N_CHIPS - 1) + p - 1
                other = _shard_window(outs[i], kinds[i], shp, 2 * px + py, 1 - c)
                pltpu.make_async_remote_copy(src_ref=other, dst_ref=other, send_sem=fsend_sems.at[k], recv_sem=frecv_sems.at[k],
                                             device_id=sib, device_id_type=MESH).wait_recv()
        for cp in sends + fwds:
            cp.wait_send()

    nk = n * (N_CHIPS - 1)
    gathered = _hbm_comm_call(
        body, name=name, n_in=n, out_shape=fulls, seq_id=seq_id,
        sem_shapes=[pltpu.SemaphoreType.DMA((nk,)), pltpu.SemaphoreType.DMA((nk,)), pltpu.SemaphoreType.DMA((nk,)),
                    pltpu.SemaphoreType.DMA((nk,))],
    )(*shards)
    return gathered


def place_local_shards(fulls, shards, kinds, name):
    n = len(shards)
    chip = jnp.reshape(2 * lax.axis_index("x") + lax.axis_index("y"), (1,)).astype(I32)

    def body(ci_ref, *refs):
        for i in range(n):
            refs[2 * n + i][...] = refs[i][...]

    in_specs = [pl.BlockSpec(s.shape, lambda i, ci: (0, 0)) for s in shards] + [pl.BlockSpec(memory_space=pl.ANY)] * n
    out_specs = [pl.BlockSpec(s.shape, (lambda i, ci: (0, ci[0])) if k == "col" else (lambda i, ci: (ci[0], 0)))
                 for s, k in zip(shards, kinds)]
    gs = pltpu.PrefetchScalarGridSpec(num_scalar_prefetch=1, grid=(1,), in_specs=in_specs, out_specs=out_specs)
    return _pcall(body, name=name, grid_spec=gs, out_shape=[jax.ShapeDtypeStruct(f.shape, f.dtype) for f in fulls],
                  input_output_aliases={1 + n + i: i for i in range(n)}, compiler_params=_params(("arbitrary",)))(chip, *shards, *fulls)


def _as_halves(g, kind, shard_shape):
    r, c = shard_shape
    if kind == "col":
        return g.reshape(2, r // 2, N_CHIPS * c)
    return g.reshape(N_CHIPS, 2, r // 2, c)


def exchange_sibling_halves(grads, kinds, shard_shapes, name, seq_id=None):
    n = len(grads)
    views = [_as_halves(g, k, s) for g, k, s in zip(grads, kinds, shard_shapes)]
    outs = []
    for k, (r, c) in zip(kinds, shard_shapes):
        outs.append(jax.ShapeDtypeStruct((r // 2, N_CHIPS * c) if k == "col" else (N_CHIPS, r // 2, c), grads[0].dtype))

    def body(*refs):
        srcs, dsts = refs[:n], refs[n:2 * n]
        send_sems, recv_sems = refs[2 * n:]
        x, y, c = _me()
        if seq_id is not None:
            _handshake([(x, y, 1 - c)])
        cps = []
        for i in range(n):
            src = srcs[i].at[1 - c] if kinds[i] == "col" else srcs[i].at[:, 1 - c]
            cp = pltpu.make_async_remote_copy(src_ref=src, dst_ref=dsts[i], send_sem=send_sems.at[i], recv_sem=recv_sems.at[i],
                                              device_id=(x, y, 1 - c), device_id_type=MESH)
            cp.start()
            cps.append(cp)
        for cp in cps:
            cp.wait_recv()
        for cp in cps:
            cp.wait_send()

    return _hbm_comm_call(body, name=name, n_in=n, out_shape=outs, seq_id=seq_id,
                          sem_shapes=[pltpu.SemaphoreType.DMA((n,)), pltpu.SemaphoreType.DMA((n,))])(*views)


def add_sibling_half(g, recv, kind, shard_shape, core_chip, name):
    r, c = shard_shape
    hr = r // 2
    gv = _as_halves(g, kind, shard_shape)
    tr = hr if hr <= 512 else (256 if hr % 256 == 0 else hr // 2)
    assert hr % tr == 0

    def body(ci_ref, g_ref, r_ref, h_ref, hb_ref):
        s = g_ref[...].astype(F32) + r_ref[...].astype(F32)
        hb_ref[...] = s.astype(BF)

        @pl.when(pl.program_id(1) == ci_ref[1])
        def _():
            h_ref[...] = s

    grid = (hr // tr, N_CHIPS)
    if kind == "col":
        g_spec = pl.BlockSpec((None, tr, c), lambda i, k, ci: (ci[0], i, k))
        o_spec = pl.BlockSpec((tr, c), lambda i, k, ci: (i, k))
    else:
        g_spec = pl.BlockSpec((None, None, tr, c), lambda i, k, ci: (k, ci[0], i, 0))
        o_spec = pl.BlockSpec((None, tr, c), lambda i, k, ci: (k, i, 0))
    own_spec = pl.BlockSpec((tr, c), lambda i, k, ci: (i, 0))
    gs = pltpu.PrefetchScalarGridSpec(num_scalar_prefetch=1, grid=grid, in_specs=[g_spec, o_spec], out_specs=[own_spec, o_spec])
    return _pcall(
        body, name=name, grid_spec=gs,
        out_shape=[jax.ShapeDtypeStruct((hr, c), F32), jax.ShapeDtypeStruct(recv.shape, BF)],
        compiler_params=_params(("parallel", "arbitrary")),
    )(core_chip, gv, recv)


def exchange_chip_pieces(hbs, kinds, shard_shapes, name, seq_id=None):
    n = len(hbs)
    outs = [jax.ShapeDtypeStruct((N_CHIPS - 1, r // 2, c), BF) for (r, c) in shard_shapes]

    def body(*refs):
        srcs, dsts = refs[:n], refs[n:2 * n]
        send_sems, recv_sems = refs[2 * n:]
        x, y, c = _me()
        if seq_id is not None:
            _handshake([(*_flip(x, y, p), c) for p in range(1, N_CHIPS)])
        cps = []
        for i in range(n):
            cc = shard_shapes[i][1]
            for p in range(1, N_CHIPS):
                px, py = _flip(x, y, p)
                pchip = 2 * px + py
                src = (srcs[i].at[:, pl.ds(pl.multiple_of(pchip * cc, cc), cc)] if kinds[i] == "col" else srcs[i].at[pchip])
                k = i * (N_CHIPS - 1) + p - 1
                cp = pltpu.make_async_remote_copy(src_ref=src, dst_ref=dsts[i].at[p - 1], send_sem=send_sems.at[k],
                                                  recv_sem=recv_sems.at[k], device_id=(px, py, c), device_id_type=MESH)
                cp.start()
                cps.append(cp)
        for cp in cps:
            cp.wait_recv()
        for cp in cps:
            cp.wait_send()

    nk = n * (N_CHIPS - 1)
    return _hbm_comm_call(body, name=name, n_in=n, out_shape=outs, seq_id=seq_id,
                          sem_shapes=[pltpu.SemaphoreType.DMA((nk,)), pltpu.SemaphoreType.DMA((nk,))])(*hbs)


def sum_chip_pieces(h_own, pieces, name):
    hr, c = h_own.shape
    tr = hr if hr <= 512 else (256 if hr % 256 == 0 else hr // 2)
    assert hr % tr == 0

    def body(h_ref, p_ref, q_ref):
        q_ref[...] = ((h_ref[...] + p_ref[0].astype(F32)) + p_ref[1].astype(F32)) + p_ref[2].astype(F32)

    blk = pl.BlockSpec((tr, c), lambda i: (i, 0))
    return _pcall(body, name=name, grid=(hr // tr,), in_specs=[blk, pl.BlockSpec((N_CHIPS - 1, tr, c), lambda i: (0, i, 0))],
                  out_specs=blk, out_shape=jax.ShapeDtypeStruct((hr, c), F32), compiler_params=_params(("parallel",)))(h_own, pieces)


def exchange_reduced_halves(qs, name, seq_id):
    n = len(qs)

    def body(*refs):
        srcs, dsts = refs[:n], refs[n:2 * n]
        send_sems, recv_sems = refs[2 * n:]
        x, y, c = _me()
        _handshake([(x, y, 1 - c)])
        cps = []
        for i in range(n):
            cp = pltpu.make_async_remote_copy(src_ref=srcs[i], dst_ref=dsts[i], send_sem=send_sems.at[i], recv_sem=recv_sems.at[i],
                                              device_id=(x, y, 1 - c), device_id_type=MESH)
            cp.start()
            cps.append(cp)
        for cp in cps:
            cp.wait_recv()
        for cp in cps:
            cp.wait_send()

    return _seq_call(body, name=name, n_in=n, out_shape=[jax.ShapeDtypeStruct(q.shape, F32) for q in qs],
                     sem_shapes=[pltpu.SemaphoreType.DMA((n,)), pltpu.SemaphoreType.DMA((n,))], collective_id=seq_id)(*qs)


def _rows128(a):
    return a.reshape(-1, LANES)


def _after(xs, *deps):
    flat = []
    for d in deps:
        flat.extend(d if isinstance(d, (list, tuple)) else [d])
    return list(lax.optimization_barrier((tuple(xs), tuple(flat)))[0])


def _block_diag(w):
    H, d, _ = w.shape
    eye = jnp.eye(H, dtype=w.dtype)
    return jnp.einsum("hde,hg->hdge", w, eye).reshape(H * d, H * d)


def _diag_blocks(g4, H, d):
    nb = g4.shape[0]
    per = LANES // d
    g = g4.reshape(nb, per, d, per, d)
    return jnp.stack([g[:, j, :, j, :] for j in range(per)], axis=1).reshape(H, d, d)


def kernel(x, c, w_mod, b_mod, g_ffn1, w_ffn1_in, w_ffn1_out, g_mix, w_in, conv_w, conv_b, ln_g, ln_b, rnn_conv_w, rnn_conv_b, w_a, b_a, w_i, b_i, lru_lambda, w_out, g_ffn2, w_ffn2_in, w_ffn2_out, w_fmod, b_fmod, g_final, loss_target, m_w_mod, m_b_mod, m_g_ffn1, m_w_ffn1_in, m_w_ffn1_out, m_g_mix, m_w_in, m_conv_w, m_conv_b, m_ln_g, m_ln_b, m_rnn_conv_w, m_rnn_conv_b, m_w_a, m_b_a, m_w_i, m_b_i, m_lru_lambda, m_w_out, m_g_ffn2, m_w_ffn2_in, m_w_ffn2_out, m_w_fmod, m_b_fmod, m_g_final, v_w_mod, v_b_mod, v_g_ffn1, v_w_ffn1_in, v_w_ffn1_out, v_g_mix, v_w_in, v_conv_w, v_conv_b, v_ln_g, v_ln_b, v_rnn_conv_w, v_rnn_conv_b, v_w_a, v_b_a, v_w_i, v_b_i, v_lru_lambda, v_w_out, v_g_ffn2, v_w_ffn2_in, v_w_ffn2_out, v_w_fmod, v_b_fmod, v_g_final):
    S, D = x.shape[1], x.shape[2]
    M = conv_b.shape[1]
    H, HD = w_a.shape[1], w_a.shape[2]
    nb = M // LANES
    ix, iy, ic = lax.axis_index("x"), lax.axis_index("y"), lax.axis_index("c")
    chip = 2 * ix + iy
    dev = 2 * chip + ic
    core_chip = jnp.stack([ic, chip]).astype(I32)
    cidx = core_chip
    xs = x[0]
    tgt = loss_target[0]

    kinds = ["col", "row"]
    w_f1, w_mx, w_f2 = [w_ffn1_in[0], w_ffn1_out[0]], [w_in[0], w_out[0]], [w_ffn2_in[0], w_ffn2_out[0]]
    as_bf = lambda ws: [w.astype(BF) for w in ws]
    shapes_of = lambda ws: [w.shape for w in ws]
    b_f1, b_mx, b_f2 = as_bf(w_f1), as_bf(w_mx), as_bf(w_f2)
    got_f1i = allgather_weights(b_f1[:1], kinds[:1], "gather_ffn1_in", seq_id=9)
    got_f1o = allgather_weights(b_f1[1:], kinds[1:], "gather_ffn1_out", seq_id=13)
    got_mx = allgather_weights(b_mx, kinds, "gather_mix", seq_id=1)
    got_f2 = allgather_weights(b_f2, kinds, "gather_ffn2", seq_id=2)

    c_all =allgather_devices(_rows128(c), "gather_c")[0].reshape(N_DEV, D)
    mod_cols = cond_matmul(c_all, w_mod[0], "mod_proj")
    fmod_cols = cond_matmul(c_all, w_fmod, "fmod_proj")
    convw_pad = jnp.pad(conv_w[0], ((0, 32 - CONV_WIDTH), (0, 0)))
    rnnw_pad = jnp.pad(rnn_conv_w[0], ((0, SUBLANES - RNN_CONV_WIDTH), (0, 0)))
    n_mod, n_fmod = mod_cols.shape[1], fmod_cols.shape[1]
    small = jnp.concatenate([_rows128(mod_cols), _rows128(fmod_cols), convw_pad, rnnw_pad], axis=0)
    small4 = allgather_chips(small, "gather_cond")
    r0 = N_DEV * n_mod // LANES
    r1 = r0 + N_DEV * n_fmod // LANES
    mod_all = small4[:, :r0].reshape(N_CHIPS, N_DEV, n_mod)
    fmod_all = small4[:, r0:r1].reshape(N_CHIPS, N_DEV, n_fmod)
    convw4 = small4[:, r1:r1 + 32]
    rnnw4 = small4[:, r1 + 32:r1 + 32 + SUBLANES]
    mod_row = lax.dynamic_index_in_dim(mod_all, dev, axis=1, keepdims=False).reshape(1, N_CHIPS * n_mod) + b_mod
    fmod_row = lax.dynamic_index_in_dim(fmod_all, dev, axis=1, keepdims=False).reshape(1, N_CHIPS * n_fmod) + b_fmod[None, :]
    vecs = jnp.concatenate([mod_row.reshape(9, D), fmod_row.reshape(2, D), g_ffn1, g_mix, g_ffn2, g_final[None, :],
                            jnp.zeros((1, D), F32)], axis=0)
    lnv = jnp.concatenate([ln_g, ln_b, jnp.zeros((SUBLANES - 2, M), F32)], axis=0)
    bda = _block_diag(w_a[0]).astype(BF)
    bdi = _block_diag(w_i[0]).astype(BF)

    def reduce_add(gs, recv, ws, tag, kinds_=kinds):
        pairs = [add_sibling_half(g, r_, k, w.shape, core_chip, f"add_sibling_{tag}{j}")
                 for j, (g, r_, k, w) in enumerate(zip(gs, recv, kinds_, ws))]
        return [p[0] for p in pairs], [p[1] for p in pairs]

    def reduce_sum(hs_, recv, ws, tag, kinds_=kinds):
        return [sum_chip_pieces(h_, p_, f"sum_chips_{tag}{j}") for j, (h_, p_) in enumerate(zip(hs_, recv))]

    rows1 = (R_SH1, R_SC1, R_GT1, R_G1)
    rows3 = (R_SH3, R_SC3, R_GT3, R_G3)
    (wi1,) = place_local_shards(got_f1i, b_f1[:1], kinds[:1], "place_ffn1_in")
    g1s, u1s, a1s = ffn_fwd_in(xs, vecs, wi1, rows1, "ffn1_fwd_in")
    (wo1,) = place_local_shards(_after(got_f1o, a1s), b_f1[1:], kinds[1:], "place_ffn1_out")
    x1, y1 = ffn_fwd_out(a1s, xs, vecs, wo1, rows1, "ffn1_fwd_out")
    win, wout = place_local_shards(_after(got_mx, x1), b_mx, kinds, "place_mix")
    proj = norm_matmul(x1, vecs, win, (R_SH2, R_SC2, R_G2), "mix_in_proj")
    cq = conv_fwd(proj, convw4, conv_b, "conv_fwd")
    xr, ra, ii, hh = rnn_fwd(proj, rnnw4, rnn_conv_b, bda, bdi, b_a, b_i, lru_lambda, "rnn_fwd")
    x2, ym, ycat = mix_out(cq, proj, hh, x1, vecs, lnv, wout, "mix_out")
    wi2, wo2 = place_local_shards(_after(got_f2, x2), b_f2, kinds, "place_ffn2")
    dx3, g2s, u2s, y2, vgf = ffn_fwd(x2, vecs, wi2, wo2, rows3, "ffn2_fwd", final_tgt=tgt)

    Fd = wo1.shape[0]
    tk = min(2048, S)
    dx2, act2, dg2, du2, h3b, dy2b, vg3 = ffn_bwd(dx3, x2, vecs, g2s, u2s, y2, wi2, wo2, rows3, "ffn2_bwd")
    gwo2 = matmul(act2, dy2b, "tn", tm=Fd // 2, tn=D, tk=tk, out_dtype=BF, name="ffn2_dwo")
    gwi2 = matmul(h3b, dg2, "tn", tm=D, tn=Fd // 2, tk=tk, out_dtype=BF, name="ffn2_dwg", out_cols=2 * Fd)
    gwi2 = matmul(h3b, du2, "tn", tm=D, tn=Fd // 2, tk=tk, out_dtype=BF, name="ffn2_dwu", out_cols=2 * Fd, col_off=Fd, prev=gwi2)
    recv1_f2 = exchange_sibling_halves([gwi2, gwo2], kinds, shapes_of(w_f2), "reduce1_ffn2", seq_id=3)
    dcq, dhout, duy, dymb, vgd, vgm = mix_out_bwd(dx2, ym, vecs, wout, cq, lnv, proj, hh, "mix_out_bwd")
    gwout = matmul(ycat, dymb, "tn", tm=2 * M, tn=D, tk=tk, out_dtype=BF, name="mix_dwout")
    recv1_f2 = _after(recv1_f2, gwout)
    h_f2, hb_f2 = reduce_add([gwi2, gwo2], recv1_f2, w_f2, "ffn2_")
    recv2_f2 = exchange_chip_pieces(hb_f2, kinds, shapes_of(w_f2), "reduce2_ffn2", seq_id=4)
    duv, dug, dconvw4, dconvb = conv_bwd(_after([dcq], hb_f2)[0], proj, convw4, "conv_bwd")
    dux, dwa4, dwi4, drnnw4, rvec = rnn_bwd(dhout, hh, xr, ra, ii, proj, rnnw4, bda, bdi, lru_lambda, "rnn_bwd")
    dx1, h2b, dpb, vg2 = mix_in_bwd((duv, dug, dux, duy), x1, dx2, vecs, win, "mix_in_bwd")
    gwin = matmul(h2b, dpb, "tn", tm=D, tn=1024, tk=tk, out_dtype=BF, name="mix_dwin")
    recv1_mx = exchange_sibling_halves([gwin, gwout], kinds, shapes_of(w_mx), "reduce1_mix", seq_id=5)
    q_f2 = reduce_sum(_after(h_f2, gwin), recv2_f2, w_f2, "ffn2_")
    r_f2 = exchange_reduced_halves(q_f2, "reduce3_ffn2", seq_id=14)
    h_mx, hb_mx = reduce_add([gwin, gwout], _after(recv1_mx, q_f2), w_mx, "mix_")
    recv2_mx = exchange_chip_pieces(hb_mx, kinds, shapes_of(w_mx), "reduce2_mix", seq_id=6)
    dx0, act1, dg1, du1, h1b, dy1b, vg1 = ffn_bwd(_after([dx1], hb_mx)[0], xs, vecs, g1s, u1s, y1, wi1, wo1, rows1, "ffn1_bwd")
    dmod_row = jnp.concatenate([vg1[1:3], vg1[0:1], vg2[0:2], vgd[0:1], vg3[1:3], vg3[0:1]], axis=0)
    gains = jnp.concatenate([vg1[3:4], vg2[2:3], vg3[3:4], vgf[2:4]], axis=0)
    mvecs = jnp.concatenate([dconvb, vgm[0:2], rvec[0:4], jnp.zeros((1, M), F32)], axis=0)
    parts = [_rows128(dmod_row), _rows128(vgf[0:2]), _rows128(gains), _rows128(mvecs),
             _rows128(dconvw4), _rows128(drnnw4), _rows128(_diag_blocks(dwa4, H, HD)), _rows128(_diag_blocks(dwi4, H, HD))]
    sizes = [p.shape[0] for p in parts]
    packed = jnp.concatenate(parts, axis=0)
    gathered = allgather_devices_hbm(packed, "gather_small", seq_id=10)

    gwo1 = matmul(_after([act1], recv2_mx, packed)[0], dy1b, "tn", tm=Fd // 2, tn=D, tk=tk, out_dtype=BF, name="ffn1_dwo")
    w_f1o, w_f1i = w_f1[1:], w_f1[:1]
    recv1_f1o = exchange_sibling_halves([gwo1], ["row"], shapes_of(w_f1o), "reduce1_ffn1_out", seq_id=7)
    q_mx = reduce_sum(_after(h_mx, gwo1), recv2_mx, w_mx, "mix_")
    r_mx = exchange_reduced_halves(q_mx, "reduce3_mix", seq_id=15)
    gwi1 = matmul(_after([h1b], q_mx)[0], dg1, "tn", tm=D, tn=Fd // 2, tk=tk, out_dtype=BF, name="ffn1_dwg", out_cols=2 * Fd)
    h_f1o, hb_f1o = reduce_add([gwo1], _after(recv1_f1o, gwi1), w_f1o, "ffn1_out", ["row"])
    recv2_f1o = exchange_chip_pieces(hb_f1o, ["row"], shapes_of(w_f1o), "reduce2_ffn1_out", seq_id=11)
    gwi1 = matmul(h1b, _after([du1], hb_f1o, gathered)[0], "tn", tm=D, tn=Fd // 2, tk=tk, out_dtype=BF, name="ffn1_dwu", out_cols=2 * Fd,
                  col_off=Fd, prev=gwi1)
    recv1_f1i = exchange_sibling_halves([gwi1], ["col"], shapes_of(w_f1i), "reduce1_ffn1_in", seq_id=12)
    q_f1o = reduce_sum(_after(h_f1o, gwi1), recv2_f1o, w_f1o, "ffn1_out", ["row"])
    r_f1o = exchange_reduced_halves(q_f1o, "reduce3_ffn1_out", seq_id=16)
    summed = sum_slots(gathered, "sum_small")
    offs = [0]
    for s in sizes:
        offs.append(offs[-1] + s)
    seg = lambda k: summed[offs[k]:offs[k + 1]]
    g_b_mod = seg(0).reshape(1, 9 * D)
    g_b_fmod = seg(1).reshape(1, 2 * D)
    gsum = seg(2).reshape(5, D)
    loss = (0.5 / D) * jnp.sum(gsum[4])
    msum = seg(3).reshape(SUBLANES, M)
    g_conv_w = lax.dynamic_index_in_dim(seg(4).reshape(nb, 32, LANES), chip, axis=0, keepdims=False)[:CONV_WIDTH]
    g_rnn_w = lax.dynamic_index_in_dim(seg(5).reshape(nb, SUBLANES, LANES), chip, axis=0, keepdims=False)[:RNN_CONV_WIDTH]
    g_w_a = seg(6).reshape(H, HD, HD)
    g_w_i = seg(7).reshape(H, HD, HD)
    dmod_all = gathered[:, offs[0]:offs[1]].reshape(N_DEV, 9 * D)
    dfmod_all = gathered[:, offs[1]:offs[2]].reshape(N_DEV, 2 * D)
    dmod_cols = lax.dynamic_slice_in_dim(dmod_all, chip * n_mod, n_mod, axis=1)
    dfmod_cols = lax.dynamic_slice_in_dim(dfmod_all, chip * n_fmod, n_fmod, axis=1)

    h_f1i, hb_f1i = reduce_add([gwi1], _after(recv1_f1i, q_f1o), w_f1i, "ffn1_in", ["col"])
    recv2_f1i = exchange_chip_pieces(hb_f1i, ["col"], shapes_of(w_f1i), "reduce2_ffn1_in", seq_id=8)
    dmod_cols, dfmod_cols = _after([dmod_cols, dfmod_cols], hb_f1i)
    g_w_mod, d_w_mod, nm_w_mod, nv_w_mod = adam_cond(c_all, dmod_cols, w_mod[0], m_w_mod[0], v_w_mod[0], "adam_w_mod")
    g_w_fmod, d_w_fmod, nm_w_fmod, nv_w_fmod = adam_cond(c_all, dfmod_cols, w_fmod, m_w_fmod, v_w_fmod, "adam_w_fmod")

    def adam_group(ws, qs_, rs_, ms, vs, tags, after):
        qs_ = _after(list(qs_), *after) if after else list(qs_)
        return [adam_big(w, q_, r_, m, v, cidx, "adam_" + t) for w, q_, r_, m, v, t in zip(ws, qs_, rs_, ms, vs, tags)]

    ad_f2 = adam_group(w_f2, q_f2, r_f2, [m_w_ffn2_in[0], m_w_ffn2_out[0]], [v_w_ffn2_in[0], v_w_ffn2_out[0]],
                       ["ffn2_in", "ffn2_out"], [hb_f1i])
    ad_mx = adam_group(w_mx, q_mx, r_mx, [m_w_in[0], m_w_out[0]], [v_w_in[0], v_w_out[0]], ["w_in", "w_out"], [hb_f1i])
    ad_f1o = adam_group(w_f1o, q_f1o, r_f1o, [m_w_ffn1_out[0]], [v_w_ffn1_out[0]], ["ffn1_out"], [hb_f1i])
    q_f1i = reduce_sum(_after(h_f1i, ad_f2[0][0], ad_f2[1][0], ad_mx[0][0], ad_mx[1][0], ad_f1o[0][0], g_w_mod, g_w_fmod),
                       recv2_f1i, w_f1i, "ffn1_in", ["col"])
    r_f1i = exchange_reduced_halves(q_f1i, "reduce3_ffn1_in", seq_id=17)
    ad_f1i = adam_group(w_f1i, q_f1i, r_f1i, [m_w_ffn1_in[0]], [v_w_ffn1_in[0]], ["ffn1_in"], [])
    big_out = ad_f1i + ad_f1o + ad_mx + ad_f2

    flat2 = lambda a: a.reshape(-1, a.shape[-1])
    small_names = ["b_mod", "g_ffn1", "g_mix", "conv_w", "conv_b", "ln_g", "ln_b", "rnn_conv_w", "rnn_conv_b", "w_a", "b_a",
                   "w_i", "b_i", "lru_lambda", "g_ffn2", "b_fmod", "g_final"]
    small_w = [b_mod, g_ffn1, g_mix, conv_w, conv_b, ln_g, ln_b, rnn_conv_w, rnn_conv_b, w_a, b_a, w_i, b_i, lru_lambda,
               g_ffn2, b_fmod, g_final]
    small_m = [m_b_mod, m_g_ffn1, m_g_mix, m_conv_w, m_conv_b, m_ln_g, m_ln_b, m_rnn_conv_w, m_rnn_conv_b, m_w_a, m_b_a,
               m_w_i, m_b_i, m_lru_lambda, m_g_ffn2, m_b_fmod, m_g_final]
    small_v = [v_b_mod, v_g_ffn1, v_g_mix, v_conv_w, v_conv_b, v_ln_g, v_ln_b, v_rnn_conv_w, v_rnn_conv_b, v_w_a, v_b_a,
               v_w_i, v_b_i, v_lru_lambda, v_g_ffn2, v_b_fmod, v_g_final]
    small_g = [g_b_mod, gsum[0:1], gsum[1:2], g_conv_w, msum[0:1], msum[1:2], msum[2:3], g_rnn_w, msum[3:4], g_w_a, msum[4:5],
               g_w_i, msum[5:6], msum[6:7], gsum[2:3], g_b_fmod, gsum[3:4]]
    small_g = [g.reshape(w.shape) for g, w in zip(small_g, small_w)]
    two_d = lambda a: a.reshape(1, -1) if a.ndim == 1 else flat2(a)
    sd, sm, sv = adam_small([two_d(a) for a in small_w], [two_d(a) for a in small_g], [two_d(a) for a in small_m],
                            [two_d(a) for a in small_v], "adam_small")
    small = {}
    for k, nm in enumerate(small_names):
        shp = small_w[k].shape
        small[nm] = (small_g[k], sd[k].reshape(shp), sm[k].reshape(shp), sv[k].reshape(shp))

    big = {"w_mod": tuple(a[None] for a in (g_w_mod, d_w_mod, nm_w_mod, nv_w_mod)),
           "w_fmod": (g_w_fmod, d_w_fmod, nm_w_fmod, nv_w_fmod)}
    for nm, res in zip(["w_ffn1_in", "w_ffn1_out", "w_in", "w_out", "w_ffn2_in", "w_ffn2_out"], big_out):
        big[nm] = tuple(a[None] for a in res)
    order = ["w_mod", "b_mod", "g_ffn1", "w_ffn1_in", "w_ffn1_out", "g_mix", "w_in", "conv_w", "conv_b", "ln_g", "ln_b",
             "rnn_conv_w", "rnn_conv_b", "w_a", "b_a", "w_i", "b_i", "lru_lambda", "w_out", "g_ffn2", "w_ffn2_in",
             "w_ffn2_out", "w_fmod", "b_fmod", "g_final"]
    table = {**small, **big}
    outs = [loss, dx0[None]]
    for kind_ in range(4):
        outs.extend(table[nm][kind_] for nm in order)
    return tuple(outs)
```

```python
import functools

import jax
import jax.numpy as jnp
from jax import lax
from jax.experimental import pallas as pl
from jax.experimental.pallas import tpu as pltpu
from jax.experimental.pallas import tpu_sc as plsc

F32 = jnp.float32
BF = jnp.bfloat16
I32 = jnp.int32
MESH = pl.DeviceIdType.MESH

EPS = 1e-6
RG_C = 8.0
MACARON_W = 0.5
CONV_WIDTH = 31
RNN_CONV_WIDTH = 4
ADAM_LR = 0.001
ADAM_B1 = 0.9
ADAM_B2 = 0.999
ADAM_EPS = 1e-08
ADAM_WD = 0.01
ADAM_STEP = 10

LANES = 128
SUBLANES = 8
VMEM_LIMIT = 62 * 1024 * 1024
N_CHIPS = 4
N_DEV = 8

R_SH1, R_SC1, R_GT1, R_SH2, R_SC2, R_GT2, R_SH3, R_SC3, R_GT3, R_FSH, R_FSC, R_G1, R_G2, R_G3, R_GF = range(15)

CONTRACT_LAST = (((1,), (1,)), ((), ()))
CONTRACT_FIRST = (((0,), (0,)), ((), ()))


def _pcall(body, **kw):
    return pl.pallas_call(body, **kw)


def _params(sem=None, vmem=VMEM_LIMIT):
    if sem is None:
        return pltpu.CompilerParams(vmem_limit_bytes=vmem)
    return pltpu.CompilerParams(dimension_semantics=sem, vmem_limit_bytes=vmem)


def _row(ref, r):
    return ref[r:r + 1, :]


def _sigmoid(x):
    return 0.5 * jnp.tanh(0.5 * x) + 0.5


def _colsum(x):
    return jnp.sum(x, axis=0, keepdims=True)


def _rowmean(x):
    return jnp.mean(x, axis=-1, keepdims=True)


def matmul(a, b, mode, *, tm, tn, tk, name, out_dtype=F32, out_cols=None, col_off=0, prev=None):
    if mode == "nn":
        (M, K), (K2, N) = a.shape, b.shape
    elif mode == "nt":
        (M, K), (N, K2) = a.shape, b.shape
    else:
        (K, M), (K2, N) = a.shape, b.shape
    assert K == K2 and M % tm == 0 and N % tn == 0 and K % tk == 0 and col_off % tn == 0
    nk = K // tk
    out_cols = N if out_cols is None else out_cols
    off = col_off // tn

    def body(*refs):
        if prev is None:
            a_ref, b_ref, o_ref, acc = refs
        else:
            a_ref, b_ref, _, o_ref, acc = refs
        k = pl.program_id(2)
        av = a_ref[...].astype(BF)
        bv = b_ref[...].astype(BF)
        if mode == "nn":
            part = jnp.dot(av, bv, preferred_element_type=F32)
        elif mode == "nt":
            part = lax.dot_general(av, bv, CONTRACT_LAST, preferred_element_type=F32)
        else:
            part = lax.dot_general(av, bv, CONTRACT_FIRST, preferred_element_type=F32)
        if nk == 1:
            o_ref[...] = part.astype(out_dtype)
            return

        @pl.when(k == 0)
        def _():
            acc[...] = part

        @pl.when((k > 0) & (k < nk - 1))
        def _():
            acc[...] += part

        @pl.when(k == nk - 1)
        def _():
            o_ref[...] = (acc[...] + part).astype(out_dtype)

    if mode == "nn":
        a_spec = pl.BlockSpec((tm, tk), lambda m, n, k: (m, k))
        b_spec = pl.BlockSpec((tk, tn), lambda m, n, k: (k, n))
    elif mode == "nt":
        a_spec = pl.BlockSpec((tm, tk), lambda m, n, k: (m, k))
        b_spec = pl.BlockSpec((tn, tk), lambda m, n, k: (n, k))
    else:
        a_spec = pl.BlockSpec((tk, tm), lambda m, n, k: (k, m))
        b_spec = pl.BlockSpec((tk, tn), lambda m, n, k: (k, n))
    in_specs = [a_spec, b_spec]
    args = [a, b]
    aliases = {}
    if prev is not None:
        in_specs.append(pl.BlockSpec(memory_space=pl.ANY))
        args.append(prev)
        aliases = {2: 0}
    return _pcall(
        body, name=name, grid=(M // tm, N // tn, nk), in_specs=in_specs,
        out_specs=pl.BlockSpec((tm, tn), lambda m, n, k: (m, n + off)),
        out_shape=jax.ShapeDtypeStruct((M, out_cols), out_dtype),
        scratch_shapes=[pltpu.VMEM((tm, tn), F32)], input_output_aliases=aliases,
        compiler_params=_params(("parallel", "parallel", "arbitrary")),
    )(*args)


def cond_matmul(c_all, w, name):
    B, K = c_all.shape
    N = w.shape[1]
    tn = 256
    assert N % tn == 0

    def body(c_ref, w_ref, o_ref):
        cv = c_ref[...]
        ca = cv * _sigmoid(cv)
        o_ref[...] = jnp.dot(ca.astype(BF), w_ref[...].astype(BF), preferred_element_type=F32)

    return _pcall(
        body, name=name, grid=(N // tn,),
        in_specs=[pl.BlockSpec((B, K), lambda n: (0, 0)), pl.BlockSpec((K, tn), lambda n: (0, n))],
        out_specs=pl.BlockSpec((B, tn), lambda n: (0, n)),
        out_shape=jax.ShapeDtypeStruct((B, N), F32), compiler_params=_params(("parallel",)),
    )(c_all, w)


FFN_FWD_TS = 512
FFN_BWD_TS = 256


def _resident(shape, index_map):
    return pl.BlockSpec(shape, index_map, pipeline_mode=pl.Buffered(1))


def _final_norm_loss_grad(xv, t, v_ref, vg_ref):
    D = xv.shape[-1]
    r = lax.rsqrt(_rowmean(xv * xv) + EPS)
    n = xv * r
    g = _row(v_ref, R_GF)
    sc1 = 1.0 + _row(v_ref, R_FSC)
    gsc = g * sc1
    e = n * gsc + _row(v_ref, R_FSH) - t
    vg_ref[3:4, :] += _colsum(e * e)
    dout = e * (1.0 / D)
    dn_ = dout * n
    vg_ref[0:1, :] += _colsum(dout)
    vg_ref[1:2, :] += _colsum(dn_) * g
    vg_ref[2:3, :] += _colsum(dn_) * sc1
    dn = dout * gsc
    return r * (dn - n * _rowmean(dn * n))


def ffn_fwd(x, vecs, wi, wo, rows, name, final_tgt=None):
    r_sh, r_sc, r_gt, r_g = rows
    S, D = x.shape
    Fd = wo.shape[0]
    ts = min(FFN_FWD_TS, S)
    with_final = final_tgt is not None

    def body(*refs):
        if with_final:
            x_ref, v_ref, wg_ref, wu_ref, wo_ref, t_ref, xo_ref, g_ref, u_ref, y_ref, vg_ref = refs
        else:
            x_ref, v_ref, wg_ref, wu_ref, wo_ref, xo_ref, g_ref, u_ref, y_ref = refs
        xv = x_ref[...]
        r = lax.rsqrt(_rowmean(xv * xv) + EPS)
        gs = _row(v_ref, r_g) * (1.0 + _row(v_ref, r_sc))
        hb = (xv * r * gs + _row(v_ref, r_sh)).astype(BF)
        G = jnp.dot(hb, wg_ref[...], preferred_element_type=F32)
        U = jnp.dot(hb, wu_ref[...], preferred_element_type=F32)
        g_ref[...] = G.astype(BF)
        u_ref[...] = U.astype(BF)
        act = (G * _sigmoid(G) * U).astype(BF)
        Y = jnp.dot(act, wo_ref[...], preferred_element_type=F32)
        y_ref[...] = Y.astype(BF)
        xo = xv + (MACARON_W * _row(v_ref, r_gt)) * Y
        if with_final:
            @pl.when(pl.program_id(0) == 0)
            def _():
                vg_ref[...] = jnp.zeros_like(vg_ref)

            xo_ref[...] = _final_norm_loss_grad(xo, t_ref[...], v_ref, vg_ref)
        else:
            xo_ref[...] = xo

    tok = pl.BlockSpec((ts, D), lambda i: (i, 0))
    hid = pl.BlockSpec((ts, Fd), lambda i: (i, 0))
    in_specs = [tok, pl.BlockSpec(vecs.shape, lambda i: (0, 0)), _resident((D, Fd), lambda i: (0, 0)),
                _resident((D, Fd), lambda i: (0, 1)), _resident((Fd, D), lambda i: (0, 0))]
    out_specs = [tok, hid, hid, tok]
    out_shape = [jax.ShapeDtypeStruct((S, D), F32), jax.ShapeDtypeStruct((S, Fd), BF),
                 jax.ShapeDtypeStruct((S, Fd), BF), jax.ShapeDtypeStruct((S, D), BF)]
    args = [x, vecs, wi, wi, wo]
    if with_final:
        in_specs.append(tok)
        args.append(final_tgt)
        out_specs.append(pl.BlockSpec((SUBLANES, D), lambda i: (0, 0)))
        out_shape.append(jax.ShapeDtypeStruct((SUBLANES, D), F32))
    return _pcall(body, name=name, grid=(S // ts,), in_specs=in_specs, out_specs=out_specs, out_shape=out_shape,
                  compiler_params=_params(("arbitrary",)))(*args)


def ffn_fwd_in(x, vecs, wi, rows, name):
    r_sh, r_sc, r_gt, r_g = rows
    S, D = x.shape
    Fd = wi.shape[1] // 2
    ts = min(FFN_FWD_TS, S)

    def body(x_ref, v_ref, wg_ref, wu_ref, g_ref, u_ref, a_ref):
        xv = x_ref[...]
        r = lax.rsqrt(_rowmean(xv * xv) + EPS)
        gs = _row(v_ref, r_g) * (1.0 + _row(v_ref, r_sc))
        hb = (xv * r * gs + _row(v_ref, r_sh)).astype(BF)
        G = jnp.dot(hb, wg_ref[...], preferred_element_type=F32)
        U = jnp.dot(hb, wu_ref[...], preferred_element_type=F32)
        g_ref[...] = G.astype(BF)
        u_ref[...] = U.astype(BF)
        a_ref[...] = (G * _sigmoid(G) * U).astype(BF)

    hid = pl.BlockSpec((ts, Fd), lambda i: (i, 0))
    return _pcall(
        body, name=name, grid=(S // ts,),
        in_specs=[pl.BlockSpec((ts, D), lambda i: (i, 0)), pl.BlockSpec(vecs.shape, lambda i: (0, 0)),
                  _resident((D, Fd), lambda i: (0, 0)), _resident((D, Fd), lambda i: (0, 1))],
        out_specs=[hid, hid, hid], out_shape=[jax.ShapeDtypeStruct((S, Fd), BF)] * 3,
        compiler_params=_params(("arbitrary",)),
    )(x, vecs, wi, wi)


def ffn_fwd_out(act, x, vecs, wo, rows, name):
    r_sh, r_sc, r_gt, r_g = rows
    S, D = x.shape
    Fd = wo.shape[0]
    ts = min(FFN_FWD_TS, S)

    def body(a_ref, x_ref, v_ref, wo_ref, xo_ref, y_ref):
        Y = jnp.dot(a_ref[...], wo_ref[...], preferred_element_type=F32)
        y_ref[...] = Y.astype(BF)
        xo_ref[...] = x_ref[...] + (MACARON_W * _row(v_ref, r_gt)) * Y

    tok = pl.BlockSpec((ts, D), lambda i: (i, 0))
    return _pcall(
        body, name=name, grid=(S // ts,),
        in_specs=[pl.BlockSpec((ts, Fd), lambda i: (i, 0)), tok, pl.BlockSpec(vecs.shape, lambda i: (0, 0)),
                  _resident((Fd, D), lambda i: (0, 0))],
        out_specs=[tok, tok], out_shape=[jax.ShapeDtypeStruct((S, D), F32), jax.ShapeDtypeStruct((S, D), BF)],
        compiler_params=_params(("arbitrary",)),
    )(act, x, vecs, wo)


def ffn_bwd(dxo, x, vecs, gs_, us_, y, wi, wo, rows, name):
    r_sh, r_sc, r_gt, r_g = rows
    S, D = x.shape
    Fd = wo.shape[0]
    ts = min(FFN_BWD_TS, S)

    def body(dxo_ref, x_ref, v_ref, g_ref, u_ref, y_ref, wg_ref, wu_ref, wo_ref,
             dx_ref, act_ref, dg_ref, du_ref, hb_ref, dyb_ref, vg_ref):
        @pl.when(pl.program_id(0) == 0)
        def _():
            vg_ref[...] = jnp.zeros_like(vg_ref)

        dxo_v = dxo_ref[...]
        dyb = ((MACARON_W * _row(v_ref, r_gt)) * dxo_v).astype(BF)
        dyb_ref[...] = dyb
        vg_ref[0:1, :] += MACARON_W * _colsum(dxo_v * y_ref[...].astype(F32))
        dA = lax.dot_general(dyb, wo_ref[...], CONTRACT_LAST, preferred_element_type=F32)
        G = g_ref[...].astype(F32)
        U = u_ref[...].astype(F32)
        sg = _sigmoid(G)
        sl = G * sg
        dU = (dA * sl).astype(BF)
        dG = (dA * U * (sg * (1.0 + G * (1.0 - sg)))).astype(BF)
        act_ref[...] = (sl * U).astype(BF)
        dg_ref[...] = dG
        du_ref[...] = dU
        dh = (lax.dot_general(dG, wg_ref[...], CONTRACT_LAST, preferred_element_type=F32)
              + lax.dot_general(dU, wu_ref[...], CONTRACT_LAST, preferred_element_type=F32))
        xv = x_ref[...]
        r = lax.rsqrt(_rowmean(xv * xv) + EPS)
        n = xv * r
        g = _row(v_ref, r_g)
        sc1 = 1.0 + _row(v_ref, r_sc)
        gsc = g * sc1
        hb_ref[...] = (n * gsc + _row(v_ref, r_sh)).astype(BF)
        dhn = dh * n
        vg_ref[1:2, :] += _colsum(dh)
        vg_ref[2:3, :] += _colsum(dhn) * g
        vg_ref[3:4, :] += _colsum(dhn) * sc1
        dn = dh * gsc
        dx_ref[...] = dxo_v + r * (dn - n * _rowmean(dn * n))

    tok = pl.BlockSpec((ts, D), lambda i: (i, 0))
    hid = pl.BlockSpec((ts, Fd), lambda i: (i, 0))
    return _pcall(
        body, name=name, grid=(S // ts,),
        in_specs=[tok, tok, pl.BlockSpec(vecs.shape, lambda i: (0, 0)), hid, hid, tok, _resident((D, Fd), lambda i: (0, 0)),
                  _resident((D, Fd), lambda i: (0, 1)), _resident((Fd, D), lambda i: (0, 0))],
        out_specs=[tok, hid, hid, hid, tok, tok, pl.BlockSpec((SUBLANES, D), lambda i: (0, 0))],
        out_shape=[jax.ShapeDtypeStruct((S, D), F32), jax.ShapeDtypeStruct((S, Fd), BF),
                   jax.ShapeDtypeStruct((S, Fd), BF), jax.ShapeDtypeStruct((S, Fd), BF),
                   jax.ShapeDtypeStruct((S, D), BF), jax.ShapeDtypeStruct((S, D), BF),
                   jax.ShapeDtypeStruct((SUBLANES, D), F32)],
        compiler_params=_params(("arbitrary",)),
    )(dxo, x, vecs, gs_, us_, y, wi, wi, wo)


def norm_matmul(x, vecs, w, rows, name):
    r_sh, r_sc, r_g = rows
    S, D = x.shape
    N = w.shape[1]
    ts = min(512, S)

    def body(x_ref, v_ref, w_ref, o_ref):
        xv = x_ref[...]
        r = lax.rsqrt(_rowmean(xv * xv) + EPS)
        gs = _row(v_ref, r_g) * (1.0 + _row(v_ref, r_sc))
        hb = (xv * r * gs + _row(v_ref, r_sh)).astype(BF)
        o_ref[...] = jnp.dot(hb, w_ref[...], preferred_element_type=F32).astype(BF)

    return _pcall(
        body, name=name, grid=(S // ts,),
        in_specs=[pl.BlockSpec((ts, D), lambda i: (i, 0)), pl.BlockSpec(vecs.shape, lambda i: (0, 0)),
                  _resident((D, N), lambda i: (0, 0))],
        out_specs=pl.BlockSpec((ts, N), lambda i: (i, 0)),
        out_shape=jax.ShapeDtypeStruct((S, N), BF),
        compiler_params=_params(("arbitrary",)),
    )(x, vecs, w)


SEQ_TT = 256
SCAN_SEGMENTS = 4
CONV_BWD_TT = 128
CONV_PAD = 32
CONV_TAPS = 32


def conv_fwd(proj, convw4, conv_b, name):
    S = proj.shape[0]
    M = conv_b.shape[1]
    nb = M // LANES
    tt = min(SEQ_TT, S)

    def body(uv_ref, ug_ref, w_ref, b_ref, cq_ref, qp):
        qp[0:CONV_PAD, :] = jnp.zeros((CONV_PAD, LANES), F32)

        def step(t, carry):
            base = pl.multiple_of(t * tt, tt)
            qp[pl.ds(base + CONV_PAD, tt), :] = uv_ref[pl.ds(base, tt), :].astype(F32) * _sigmoid(ug_ref[pl.ds(base, tt), :].astype(F32))
            acc = jnp.broadcast_to(b_ref[...], (tt, LANES))
            for k in range(CONV_WIDTH):
                acc = acc + w_ref[k:k + 1, :] * qp[pl.ds(base + (CONV_PAD - CONV_WIDTH + 1) + k, tt), :]
            cq_ref[pl.ds(base, tt), :] = acc
            return carry

        lax.fori_loop(0, S // tt, step, 0)

    return _pcall(
        body, name=name, grid=(nb,),
        in_specs=[pl.BlockSpec((S, LANES), lambda c: (0, c)), pl.BlockSpec((S, LANES), lambda c: (0, c + nb)),
                  pl.BlockSpec((None, CONV_TAPS, LANES), lambda c: (c, 0, 0)), pl.BlockSpec((1, LANES), lambda c: (0, c))],
        out_specs=pl.BlockSpec((S, LANES), lambda c: (0, c)),
        out_shape=jax.ShapeDtypeStruct((S, M), F32),
        scratch_shapes=[pltpu.VMEM((S + CONV_PAD, LANES), F32)],
        compiler_params=_params(("arbitrary",)),
    )(proj, proj, convw4, conv_b)


def conv_bwd(dcq, proj, convw4, name):
    S, M = dcq.shape
    nb = M // LANES
    tt = min(CONV_BWD_TT, S)

    def body(dcq_ref, uv_ref, ug_ref, w_ref, duv_ref, dug_ref, dw_ref, db_ref, dp, dw8, db8):
        dp[S:S + CONV_PAD, :] = jnp.zeros((CONV_PAD, LANES), F32)
        dw8[...] = jnp.zeros_like(dw8)
        db8[...] = jnp.zeros_like(db8)

        def fill(t, carry):
            base = pl.multiple_of(t * tt, tt)
            dp[pl.ds(base, tt), :] = dcq_ref[pl.ds(base, tt), :].astype(F32)
            return carry

        lax.fori_loop(0, S // tt, fill, 0)

        def step(t, carry):
            base = pl.multiple_of(t * tt, tt)
            uv = uv_ref[pl.ds(base, tt), :].astype(F32)
            sg = _sigmoid(ug_ref[pl.ds(base, tt), :].astype(F32))
            q_t = uv * sg
            db8[...] += dp[pl.ds(base, tt), :].reshape(tt // SUBLANES, SUBLANES, LANES).sum(axis=0)
            dq = jnp.zeros((tt, LANES), F32)
            for k in range(CONV_WIDTH):
                shifted = dp[pl.ds(base + (CONV_WIDTH - 1) - k, tt), :]
                dw8[k] += (shifted * q_t).reshape(tt // SUBLANES, SUBLANES, LANES).sum(axis=0)
                dq = dq + w_ref[k:k + 1, :] * shifted
            duv_ref[pl.ds(base, tt), :] = (dq * sg).astype(BF)
            dug_ref[pl.ds(base, tt), :] = (dq * uv * sg * (1.0 - sg)).astype(BF)
            return carry

        lax.fori_loop(0, S // tt, step, 0)
        dw_ref[...] = jnp.zeros_like(dw_ref)
        for k in range(CONV_WIDTH):
            dw_ref[k:k + 1, :] = _colsum(dw8[k])
        db_ref[...] = _colsum(db8[...])

    col = lambda o: pl.BlockSpec((S, LANES), lambda c: (0, c + o))
    return _pcall(
        body, name=name, grid=(nb,),
        in_specs=[col(0), col(0), col(nb), pl.BlockSpec((None, CONV_TAPS, LANES), lambda c: (c, 0, 0))],
        out_specs=[col(0), col(0), pl.BlockSpec((None, CONV_TAPS, LANES), lambda c: (c, 0, 0)),
                   pl.BlockSpec((1, LANES), lambda c: (0, c))],
        out_shape=[jax.ShapeDtypeStruct((S, M), BF), jax.ShapeDtypeStruct((S, M), BF),
                   jax.ShapeDtypeStruct((nb, CONV_TAPS, LANES), F32), jax.ShapeDtypeStruct((1, M), F32)],
        scratch_shapes=[pltpu.VMEM((S + CONV_PAD, LANES), F32),
                        pltpu.VMEM((CONV_TAPS, SUBLANES, LANES), F32), pltpu.VMEM((SUBLANES, LANES), F32)],
        compiler_params=_params(("arbitrary",)),
    )(dcq, proj, proj, convw4)


def _log_sigmoid(x):
    return jnp.minimum(x, 0.0) - jnp.log(1.0 + jnp.exp(-jnp.abs(x)))


def _rg_gate_terms(ra, ls):
    la = RG_C * ra * ls
    a = jnp.exp(la)
    th = jnp.tanh(la)
    mult = jnp.sqrt(-2.0 * th / (1.0 - th))
    return a, mult


def rnn_fwd(proj, rnnw4, rnn_b, bda, bdi, b_a, b_i, lam, name):
    S = proj.shape[0]
    M = rnn_b.shape[1]
    nb = M // LANES
    tt = min(SEQ_TT, S)
    KW = RNN_CONV_WIDTH
    nseg = SCAN_SEGMENTS if S % (SCAN_SEGMENTS * tt) == 0 else 1

    def body(ux_ref, w_ref, rb_ref, bda_ref, bdi_ref, ba_ref, bi_ref, lam_ref,
             xr_ref, ra_ref, ii_ref, h_ref, uxp, a_sc, b_sc):
        uxp[0:SUBLANES, :] = jnp.zeros((SUBLANES, LANES), F32)
        ls = _log_sigmoid(lam_ref[...])

        def step(t, carry):
            base = pl.multiple_of(t * tt, tt)
            uxp[pl.ds(base + SUBLANES, tt), :] = ux_ref[pl.ds(base, tt), :].astype(F32)
            xr = jnp.broadcast_to(rb_ref[...], (tt, LANES))
            for k in range(KW):
                xr = xr + w_ref[k:k + 1, :] * uxp[pl.ds(base + (SUBLANES - KW + 1) + k, tt), :]
            xb = xr.astype(BF)
            ra = _sigmoid(jnp.dot(xb, bda_ref[...], preferred_element_type=F32) + ba_ref[...])
            ii = _sigmoid(jnp.dot(xb, bdi_ref[...], preferred_element_type=F32) + bi_ref[...])
            a, mult = _rg_gate_terms(ra, ls)
            xr_ref[pl.ds(base, tt), :] = xr
            ra_ref[pl.ds(base, tt), :] = ra
            ii_ref[pl.ds(base, tt), :] = ii
            a_sc[pl.ds(base, tt), :] = a
            b_sc[pl.ds(base, tt), :] = mult * (ii * xr)
            return carry

        lax.fori_loop(0, S // tt, step, 0)

        rows = lax.broadcasted_iota(I32, (SUBLANES, LANES), 0)
        seg = S // nseg
        last = lambda v: jnp.broadcast_to(v[SUBLANES - 1:SUBLANES, :], (SUBLANES, LANES))

        def scan(t, carry):
            hs, ps = carry
            new_h, new_p = [], []
            for s in range(nseg):
                base = pl.multiple_of(s * seg + t * SUBLANES, SUBLANES)
                A = a_sc[pl.ds(base, SUBLANES), :]
                B = b_sc[pl.ds(base, SUBLANES), :]
                for d in (1, 2, 4):
                    As = jnp.where(rows >= d, pltpu.roll(A, d, axis=0), 1.0)
                    Bs = jnp.where(rows >= d, pltpu.roll(B, d, axis=0), 0.0)
                    B = A * Bs + B
                    A = A * As
                hh = B + A * hs[s]
                h_ref[pl.ds(base, SUBLANES), :] = hh
                pp = A * ps[s]
                if s > 0:
                    a_sc[pl.ds(base, SUBLANES), :] = pp
                new_h.append(last(hh))
                new_p.append(last(pp))
            return tuple(new_h), tuple(new_p)

        zero8 = jnp.zeros((SUBLANES, LANES), F32)
        one8 = jnp.ones((SUBLANES, LANES), F32)
        hs, ps = lax.fori_loop(0, seg // SUBLANES, scan, ((zero8,) * nseg, (one8,) * nseg))
        carry_in = hs[0]
        for s in range(1, nseg):
            c_row = carry_in[0:1, :]

            def fix(t, c, s=s, c_row=c_row):
                base = pl.multiple_of(s * seg + t * tt, tt)
                h_ref[pl.ds(base, tt), :] = h_ref[pl.ds(base, tt), :] + a_sc[pl.ds(base, tt), :] * c_row
                return c

            lax.fori_loop(0, seg // tt, fix, 0)
            carry_in = hs[s] + ps[s] * carry_in

    col = lambda o: pl.BlockSpec((S, LANES), lambda c: (0, c + o))
    vec = pl.BlockSpec((1, LANES), lambda c: (0, c))
    diag = pl.BlockSpec((LANES, LANES), lambda c: (c, c))
    return _pcall(
        body, name=name, grid=(nb,),
        in_specs=[col(2 * nb), pl.BlockSpec((None, SUBLANES, LANES), lambda c: (c, 0, 0)), vec, diag, diag, vec, vec, vec],
        out_specs=[col(0)] * 4,
        out_shape=[jax.ShapeDtypeStruct((S, M), F32)] * 4,
        scratch_shapes=[pltpu.VMEM((S + SUBLANES, LANES), F32), pltpu.VMEM((S, LANES), F32), pltpu.VMEM((S, LANES), F32)],
        compiler_params=_params(("arbitrary",)),
    )(proj, rnnw4, rnn_b, bda, bdi, b_a, b_i, lam)


def rnn_bwd(dhout, h, xr, ra, ii, proj, rnnw4, bda, bdi, lam, name):
    S, M = h.shape
    nb = M // LANES
    tt = min(SEQ_TT, S)
    KW = RNN_CONV_WIDTH
    SL = SUBLANES
    nseg = SCAN_SEGMENTS if S % (SCAN_SEGMENTS * tt) == 0 else 1

    def body(dh_ref, h_ref, xr_ref, ra_ref, ii_ref, ux_ref, w_ref, bda_ref, bdi_ref, lam_ref,
             dux_ref, dwa_ref, dwi_ref, drw_ref, vec_ref,
             a_sc, hp, g_sc, dpa_sc, dpi_sc, dxp, uxp, acc8, drw8, p_sc):
        zero8 = jnp.zeros((SL, LANES), F32)
        a_sc[S:S + SL, :] = zero8
        hp[0:SL, :] = zero8
        dxp[S:S + SL, :] = zero8
        uxp[0:SL, :] = zero8
        acc8[...] = jnp.zeros_like(acc8)
        drw8[...] = jnp.zeros_like(drw8)
        lamv = lam_ref[...]
        ls = _log_sigmoid(lamv)

        def fill(t, carry):
            base = pl.multiple_of(t * tt, tt)
            a_sc[pl.ds(base, tt), :] = jnp.exp(RG_C * ra_ref[pl.ds(base, tt), :] * ls)
            hp[pl.ds(base + SL, tt), :] = h_ref[pl.ds(base, tt), :]
            uxp[pl.ds(base + SL, tt), :] = ux_ref[pl.ds(base, tt), :].astype(F32)
            return carry

        lax.fori_loop(0, S // tt, fill, 0)

        rows = lax.broadcasted_iota(I32, (SL, LANES), 0)
        seg = S // nseg
        nt8 = seg // SL
        first = lambda v: jnp.broadcast_to(v[0:1, :], (SL, LANES))

        def rscan(t, carry):
            gs, ps = carry
            new_g, new_p = [], []
            for s in range(nseg):
                base = pl.multiple_of(s * seg + (nt8 - 1 - t) * SL, SL)
                A = a_sc[pl.ds(base + 1, SL), :]
                B = dh_ref[pl.ds(base, SL), :]
                for d in (1, 2, 4):
                    As = jnp.where(rows < SL - d, pltpu.roll(A, SL - d, axis=0), 1.0)
                    Bs = jnp.where(rows < SL - d, pltpu.roll(B, SL - d, axis=0), 0.0)
                    B = A * Bs + B
                    A = A * As
                g = B + A * gs[s]
                g_sc[pl.ds(base, SL), :] = g
                pp = A * ps[s]
                if s < nseg - 1:
                    p_sc[pl.ds(base, SL), :] = pp
                new_g.append(first(g))
                new_p.append(first(pp))
            return tuple(new_g), tuple(new_p)

        one8 = jnp.ones((SL, LANES), F32)
        gs, ps = lax.fori_loop(0, nt8, rscan, ((zero8,) * nseg, (one8,) * nseg))
        carry_in = gs[nseg - 1]
        for s in range(nseg - 2, -1, -1):
            c_row = carry_in[0:1, :]

            def fix(t, c, s=s, c_row=c_row):
                base = pl.multiple_of(s * seg + t * tt, tt)
                g_sc[pl.ds(base, tt), :] = g_sc[pl.ds(base, tt), :] + p_sc[pl.ds(base, tt), :] * c_row
                return c

            lax.fori_loop(0, seg // tt, fix, 0)
            carry_in = gs[s] + ps[s] * carry_in

        def red8(v):
            return v.reshape(tt // SL, SL, LANES).sum(axis=0)

        def step(t, carry):
            base = pl.multiple_of(t * tt, tt)
            g = g_sc[pl.ds(base, tt), :]
            hprev = hp[pl.ds(base + SL - 1, tt), :]
            xr_t = xr_ref[pl.ds(base, tt), :]
            ra_t = ra_ref[pl.ds(base, tt), :]
            ii_t = ii_ref[pl.ds(base, tt), :]
            a, mult = _rg_gate_terms(ra_t, ls)
            gx = g * xr_t
            dmult = gx * ii_t
            dii = gx * mult
            dxr = g * (mult * ii_t)
            dla = g * hprev * a - dmult * (a * a) / mult
            acc8[3] += red8(dla * ra_t)
            dpa = dla * (RG_C * ls) * ra_t * (1.0 - ra_t)
            dpi = dii * ii_t * (1.0 - ii_t)
            dpab = dpa.astype(BF)
            dpib = dpi.astype(BF)
            dxr = dxr + (lax.dot_general(dpab, bda_ref[...], CONTRACT_LAST, preferred_element_type=F32)
                         + lax.dot_general(dpib, bdi_ref[...], CONTRACT_LAST, preferred_element_type=F32))
            dpa_sc[pl.ds(base, tt), :] = dpab
            dpi_sc[pl.ds(base, tt), :] = dpib
            dxp[pl.ds(base, tt), :] = dxr
            acc8[0] += red8(dxr)
            acc8[1] += red8(dpa)
            acc8[2] += red8(dpi)
            return carry

        lax.fori_loop(0, S // tt, step, 0)

        def convb(t, carry):
            base = pl.multiple_of(t * tt, tt)
            d_t = dxp[pl.ds(base, tt), :]
            dux = jnp.zeros((tt, LANES), F32)
            for k in range(KW):
                drw8[k] += red8(d_t * uxp[pl.ds(base + (SL - KW + 1) + k, tt), :])
                dux = dux + w_ref[k:k + 1, :] * dxp[pl.ds(base + (KW - 1) - k, tt), :]
            dux_ref[pl.ds(base, tt), :] = dux.astype(BF)
            return carry

        lax.fori_loop(0, S // tt, convb, 0)

        xb = xr_ref[...].astype(BF)
        dwa_ref[...] = lax.dot_general(xb, dpa_sc[...], CONTRACT_FIRST, preferred_element_type=F32)
        dwi_ref[...] = lax.dot_general(xb, dpi_sc[...], CONTRACT_FIRST, preferred_element_type=F32)
        drw_ref[...] = jnp.zeros_like(drw_ref)
        vec_ref[...] = jnp.zeros_like(vec_ref)
        for k in range(KW):
            drw_ref[k:k + 1, :] = _colsum(drw8[k])
        for k in range(3):
            vec_ref[k:k + 1, :] = _colsum(acc8[k])
        vec_ref[3:4, :] = _colsum(acc8[3]) * (RG_C * _sigmoid(-lamv))

    col = lambda o: pl.BlockSpec((S, LANES), lambda c: (0, c + o))
    vec = pl.BlockSpec((1, LANES), lambda c: (0, c))
    diag = pl.BlockSpec((LANES, LANES), lambda c: (c, c))
    blk3 = lambda r: pl.BlockSpec((None, r, LANES), lambda c: (c, 0, 0))
    return _pcall(
        body, name=name, grid=(nb,),
        in_specs=[col(0), col(0), col(0), col(0), col(0), col(2 * nb), blk3(SL), diag, diag, vec],
        out_specs=[col(0), blk3(LANES), blk3(LANES), blk3(SL), pl.BlockSpec((SL, LANES), lambda c: (0, c))],
        out_shape=[jax.ShapeDtypeStruct((S, M), BF), jax.ShapeDtypeStruct((nb, LANES, LANES), F32),
                   jax.ShapeDtypeStruct((nb, LANES, LANES), F32), jax.ShapeDtypeStruct((nb, SL, LANES), F32),
                   jax.ShapeDtypeStruct((SL, M), F32)],
        scratch_shapes=[pltpu.VMEM((S + SL, LANES), F32), pltpu.VMEM((S + SL, LANES), F32), pltpu.VMEM((S, LANES), F32),
                        pltpu.VMEM((S, LANES), BF), pltpu.VMEM((S, LANES), BF), pltpu.VMEM((S + SL, LANES), F32),
                        pltpu.VMEM((S + SL, LANES), F32), pltpu.VMEM((SL, SL, LANES), F32), pltpu.VMEM((SL, SL, LANES), F32),
                        pltpu.VMEM((S, LANES), F32)],
        compiler_params=_params(("arbitrary",)),
    )(dhout, h, xr, ra, ii, proj, rnnw4, bda, bdi, lam)


GELU_K = 0.7978845608028654
GELU_C = 0.044715


def _layernorm_parts(cq):
    mu = _rowmean(cq)
    d = cq - mu
    rstd = lax.rsqrt(_rowmean(d * d) + EPS)
    return d * rstd, rstd


def mix_out(cq, proj, h, x, vecs, lnv, wout, name):
    S, D = x.shape
    M = cq.shape[1]
    ts = min(512, S)

    def body(cq_ref, uy_ref, h_ref, x_ref, v_ref, ln_ref, w_ref, xo_ref, ym_ref, yc_ref):
        z, _ = _layernorm_parts(cq_ref[...])
        l = z * _row(ln_ref, 0) + _row(ln_ref, 1)
        yc_ref[:, 0:M] = (l * _sigmoid(l)).astype(BF)
        uy = uy_ref[...].astype(F32)
        gelu = 0.5 * uy * (1.0 + jnp.tanh(GELU_K * (uy + GELU_C * uy * uy * uy)))
        yc_ref[:, M:2 * M] = (gelu * h_ref[...]).astype(BF)
        ym = jnp.dot(yc_ref[...], w_ref[...], preferred_element_type=F32)
        ym_ref[...] = ym.astype(BF)
        xo_ref[...] = x_ref[...] + _row(v_ref, R_GT2) * ym

    tok = pl.BlockSpec((ts, D), lambda i: (i, 0))
    mtok = lambda o: pl.BlockSpec((ts, M), lambda i: (i, o))
    return _pcall(
        body, name=name, grid=(S // ts,),
        in_specs=[mtok(0), mtok(3), mtok(0), tok, pl.BlockSpec(vecs.shape, lambda i: (0, 0)),
                  pl.BlockSpec(lnv.shape, lambda i: (0, 0)), pl.BlockSpec(wout.shape, lambda i: (0, 0))],
        out_specs=[tok, tok, pl.BlockSpec((ts, 2 * M), lambda i: (i, 0))],
        out_shape=[jax.ShapeDtypeStruct((S, D), F32), jax.ShapeDtypeStruct((S, D), BF),
                   jax.ShapeDtypeStruct((S, 2 * M), BF)],
        compiler_params=_params(("arbitrary",)),
    )(cq, proj, h, x, vecs, lnv, wout)


def mix_out_bwd(dxo, ym, vecs, wout, cq, lnv, proj, h, name):
    S, D = dxo.shape
    M = cq.shape[1]
    ts = min(512, S)

    def body(dxo_ref, ym_ref, v_ref, w_ref, cq_ref, ln_ref, uy_ref, h_ref,
             dcq_ref, dh_ref, duy_ref, dyb_ref, vgd_ref, vgm_ref):
        @pl.when(pl.program_id(0) == 0)
        def _():
            vgd_ref[...] = jnp.zeros_like(vgd_ref)
            vgm_ref[...] = jnp.zeros_like(vgm_ref)

        dxo_v = dxo_ref[...]
        dyb = (_row(v_ref, R_GT2) * dxo_v).astype(BF)
        dyb_ref[...] = dyb
        vgd_ref[0:1, :] += _colsum(dxo_v * ym_ref[...].astype(F32))
        dycat = lax.dot_general(dyb, w_ref[...], CONTRACT_LAST, preferred_element_type=F32)
        dyc = dycat[:, 0:M]
        dyr = dycat[:, M:2 * M]
        z, rstd = _layernorm_parts(cq_ref[...])
        lng = _row(ln_ref, 0)
        l = z * lng + _row(ln_ref, 1)
        sl = _sigmoid(l)
        dl = dyc * (sl * (1.0 + l * (1.0 - sl)))
        vgm_ref[0:1, :] += _colsum(dl * z)
        vgm_ref[1:2, :] += _colsum(dl)
        dz = dl * lng
        dcq_ref[...] = (rstd * (dz - _rowmean(dz) - z * _rowmean(dz * z))).astype(BF)
        uy = uy_ref[...].astype(F32)
        u2 = uy * uy
        th = jnp.tanh(GELU_K * (uy + GELU_C * uy * u2))
        gelu = 0.5 * uy * (1.0 + th)
        dgelu = 0.5 * (1.0 + th) + 0.5 * uy * (1.0 - th * th) * (GELU_K * (1.0 + 3.0 * GELU_C * u2))
        dh_ref[...] = dyr * gelu
        duy_ref[...] = (dyr * h_ref[...] * dgelu).astype(BF)

    tok = pl.BlockSpec((ts, D), lambda i: (i, 0))
    mtok = lambda o: pl.BlockSpec((ts, M), lambda i: (i, o))
    return _pcall(
        body, name=name, grid=(S // ts,),
        in_specs=[tok, tok, pl.BlockSpec(vecs.shape, lambda i: (0, 0)), pl.BlockSpec(wout.shape, lambda i: (0, 0)),
                  mtok(0), pl.BlockSpec(lnv.shape, lambda i: (0, 0)), mtok(3), mtok(0)],
        out_specs=[mtok(0), mtok(0), mtok(0), tok, pl.BlockSpec((SUBLANES, D), lambda i: (0, 0)),
                   pl.BlockSpec((SUBLANES, M), lambda i: (0, 0))],
        out_shape=[jax.ShapeDtypeStruct((S, M), BF), jax.ShapeDtypeStruct((S, M), F32), jax.ShapeDtypeStruct((S, M), BF),
                   jax.ShapeDtypeStruct((S, D), BF),
                   jax.ShapeDtypeStruct((SUBLANES, D), F32), jax.ShapeDtypeStruct((SUBLANES, M), F32)],
        compiler_params=_params(("arbitrary",)),
    )(dxo, ym, vecs, wout, cq, lnv, proj, h)


def mix_in_bwd(dparts, x, dxo, vecs, win, name):
    S, D = x.shape
    M = dparts[0].shape[1]
    ts = min(512, S)

    def body(d0, d1, d2, d3, x_ref, dxo_ref, v_ref, w_ref, dx_ref, hb_ref, dp_ref, vg_ref):
        @pl.when(pl.program_id(0) == 0)
        def _():
            vg_ref[...] = jnp.zeros_like(vg_ref)

        for q, dref in enumerate((d0, d1, d2, d3)):
            dp_ref[:, q * M:(q + 1) * M] = dref[...].astype(BF)
        dh = lax.dot_general(dp_ref[...], w_ref[...], CONTRACT_LAST, preferred_element_type=F32)
        xv = x_ref[...]
        r = lax.rsqrt(_rowmean(xv * xv) + EPS)
        n = xv * r
        g = _row(v_ref, R_G2)
        sc1 = 1.0 + _row(v_ref, R_SC2)
        gsc = g * sc1
        hb_ref[...] = (n * gsc + _row(v_ref, R_SH2)).astype(BF)
        dhn = dh * n
        vg_ref[0:1, :] += _colsum(dh)
        vg_ref[1:2, :] += _colsum(dhn) * g
        vg_ref[2:3, :] += _colsum(dhn) * sc1
        dn = dh * gsc
        dx_ref[...] = dxo_ref[...] + r * (dn - n * _rowmean(dn * n))

    tok = pl.BlockSpec((ts, D), lambda i: (i, 0))
    mtok = pl.BlockSpec((ts, M), lambda i: (i, 0))
    return _pcall(
        body, name=name, grid=(S // ts,),
        in_specs=[mtok] * 4 + [tok, tok, pl.BlockSpec(vecs.shape, lambda i: (0, 0)), pl.BlockSpec(win.shape, lambda i: (0, 0))],
        out_specs=[tok, tok, pl.BlockSpec((ts, 4 * M), lambda i: (i, 0)), pl.BlockSpec((SUBLANES, D), lambda i: (0, 0))],
        out_shape=[jax.ShapeDtypeStruct((S, D), F32), jax.ShapeDtypeStruct((S, D), BF),
                   jax.ShapeDtypeStruct((S, 4 * M), BF), jax.ShapeDtypeStruct((SUBLANES, D), F32)],
        compiler_params=_params(("arbitrary",)),
    )(*dparts, x, dxo, vecs, win)


def _adamw(w, g, m, v):
    m = ADAM_B1 * m + (1.0 - ADAM_B1) * g
    v = ADAM_B2 * v + (1.0 - ADAM_B2) * (g * g)
    m_hat = m / (1.0 - ADAM_B1 ** ADAM_STEP)
    v_hat = v / (1.0 - ADAM_B2 ** ADAM_STEP)
    delta = -ADAM_LR * (m_hat / (jnp.sqrt(v_hat) + ADAM_EPS) + ADAM_WD * w)
    return delta, m, v


def adam_big(w, g_mine, g_sib, m, v, cidx, name):
    R, C = w.shape
    hr = R // 2
    tr = 256 if hr % 256 == 0 else hr
    tc = C if C <= 1536 else (1152 if C % 1152 == 0 else 1024)
    assert hr % tr == 0 and C % tc == 0 and g_mine.shape == (hr, C)
    nrb = hr // tr

    def body(ci_ref, w_ref, gm_ref, gs_ref, m_ref, v_ref, g_ref, d_ref, nm_ref, nv_ref):
        mine = (pl.program_id(0) // nrb) == ci_ref[0]
        g = jnp.where(mine, gm_ref[...], gs_ref[...])
        d, nm, nv = _adamw(w_ref[...], g, m_ref[...], v_ref[...])
        g_ref[...] = g
        d_ref[...] = d
        nm_ref[...] = nm
        nv_ref[...] = nv

    blk = pl.BlockSpec((tr, tc), lambda i, j, ci: (i, j))
    mine_spec = pl.BlockSpec((tr, tc), lambda i, j, ci: (jnp.where(i // nrb == ci[0], i % nrb, 0), j))
    sib_spec = pl.BlockSpec((tr, tc), lambda i, j, ci: (jnp.where(i // nrb == ci[0], 0, i % nrb), j))
    gs = pltpu.PrefetchScalarGridSpec(num_scalar_prefetch=1, grid=(R // tr, C // tc),
                                      in_specs=[blk, mine_spec, sib_spec, blk, blk], out_specs=[blk] * 4)
    return _pcall(body, name=name, grid_spec=gs, out_shape=[jax.ShapeDtypeStruct((R, C), F32)] * 4,
                  compiler_params=_params(("parallel", "parallel")))(cidx, w, g_mine, g_sib, m, v)


def adam_cond(c_all, dmod, w, m, v, name):
    B, Kin = c_all.shape
    N = w.shape[1]
    tn = 768 if N % 768 == 0 else 256
    assert N % tn == 0

    def body(c_ref, d_ref, w_ref, m_ref, v_ref, g_ref, dl_ref, nm_ref, nv_ref):
        cv = c_ref[...]
        ca = cv * _sigmoid(cv)
        g = lax.dot_general(ca.astype(BF), d_ref[...].astype(BF), CONTRACT_FIRST, preferred_element_type=F32)
        d, nm, nv = _adamw(w_ref[...], g, m_ref[...], v_ref[...])
        g_ref[...] = g
        dl_ref[...] = d
        nm_ref[...] = nm
        nv_ref[...] = nv

    blk = pl.BlockSpec((Kin, tn), lambda n: (0, n))
    return _pcall(
        body, name=name, grid=(N // tn,),
        in_specs=[pl.BlockSpec((B, Kin), lambda n: (0, 0)), pl.BlockSpec((B, tn), lambda n: (0, n)), blk, blk, blk],
        out_specs=[blk] * 4, out_shape=[jax.ShapeDtypeStruct((Kin, N), F32)] * 4,
        compiler_params=_params(("parallel",)),
    )(c_all, dmod, w, m, v)


def adam_small(ws, gs, ms, vs, name):
    n = len(ws)

    def body(*refs):
        ins, outs = refs[:4 * n], refs[4 * n:]
        for k in range(n):
            d, nm, nv = _adamw(ins[k][...], ins[n + k][...], ins[2 * n + k][...], ins[3 * n + k][...])
            outs[k][...] = d
            outs[n + k][...] = nm
            outs[2 * n + k][...] = nv

    specs = [pl.BlockSpec(w.shape, lambda i: (0, 0)) for w in ws]
    shapes = [jax.ShapeDtypeStruct(w.shape, F32) for w in ws]
    out = _pcall(body, name=name, grid=(1,), in_specs=specs * 4, out_specs=specs * 3, out_shape=shapes * 3,
                 compiler_params=_params(("arbitrary",)))(*ws, *gs, *ms, *vs)
    return out[:n], out[n:2 * n], out[2 * n:]


def _me():
    return lax.axis_index("x"), lax.axis_index("y"), lax.axis_index("c")


def _flip(x, y, p):
    return (x ^ (p >> 1) if (p >> 1) else x), (y ^ (p & 1) if (p & 1) else y)


def _handshake(peers):
    barrier = pltpu.get_barrier_semaphore()
    for peer in peers:
        pl.semaphore_signal(barrier, inc=1, device_id=peer, device_id_type=MESH)
    pl.semaphore_wait(barrier, len(peers))


def _seq_call(body, *, name, n_in, out_shape, sem_shapes, collective_id):
    del n_in
    return pl.kernel(body, out_type=out_shape, mesh=plsc.ScalarSubcoreMesh(axis_name="sq", num_cores=1), name=name,
                     scratch_types=sem_shapes, compiler_params=pltpu.CompilerParams(collective_id=collective_id))


def _hbm_comm_call(body, *, name, n_in, out_shape, sem_shapes, seq_id):
    if seq_id is not None:
        return _seq_call(body, name=name, n_in=n_in, out_shape=out_shape, sem_shapes=sem_shapes, collective_id=seq_id)
    anyspec = pl.BlockSpec(memory_space=pl.ANY)
    return _pcall(body, name=name, in_specs=[anyspec] * n_in, out_specs=[anyspec] * len(out_shape), out_shape=out_shape,
                  scratch_shapes=sem_shapes, compiler_params=_params())


def allgather_devices(v, name, with_sum=False):
    R, L = v.shape

    def body(v_ref, out_ref, *rest):
        if with_sum:
            sum_ref, send_sems, recv_sems = rest
        else:
            send_sems, recv_sems = rest
        x, y, c = _me()
        me = 4 * x + 2 * y + c
        out_ref[me] = v_ref[...]
        copies = []
        for p in range(1, N_DEV):
            px, py = _flip(x, y, p >> 1)
            pc = (1 - c) if (p & 1) else c
            peer = 4 * px + 2 * py + pc
            send = pltpu.make_async_remote_copy(src_ref=v_ref, dst_ref=out_ref.at[me], send_sem=send_sems.at[p - 1],
                                                recv_sem=recv_sems.at[p - 1], device_id=(px, py, pc), device_id_type=MESH)
            send.start()
            recv = pltpu.make_async_remote_copy(src_ref=v_ref, dst_ref=out_ref.at[peer], send_sem=send_sems.at[p - 1],
                                                recv_sem=recv_sems.at[p - 1], device_id=(px, py, pc), device_id_type=MESH)
            copies.append((send, recv))
        for send, recv in copies:
            recv.wait_recv()
        for send, recv in copies:
            send.wait_send()
        if with_sum:
            s = out_ref[0]
            for k in range(1, N_DEV):
                s = s + out_ref[k]
            sum_ref[...] = s

    vm = pl.BlockSpec(memory_space=pltpu.VMEM)
    out_shape = [jax.ShapeDtypeStruct((N_DEV, R, L), F32)]
    if with_sum:
        out_shape.append(jax.ShapeDtypeStruct((R, L), F32))
    return _pcall(
        body, name=name, in_specs=[vm], out_specs=[vm] * len(out_shape), out_shape=out_shape,
        scratch_shapes=[pltpu.SemaphoreType.DMA((N_DEV - 1,)), pltpu.SemaphoreType.DMA((N_DEV - 1,))],
        compiler_params=_params(),
    )(v)


def allgather_devices_hbm(v, name, seq_id):
    R, L = v.shape

    def body(v_ref, out_ref, send_sems, recv_sems, local_sem):
        x, y, c = _me()
        me = 4 * x + 2 * y + c
        peers = []
        for p in range(1, N_DEV):
            px, py = _flip(x, y, p >> 1)
            peers.append((px, py, (1 - c) if (p & 1) else c))
        _handshake(peers)
        lc = pltpu.make_async_copy(v_ref, out_ref.at[me], local_sem)
        lc.start()
        copies = []
        for p, (px, py, pc) in enumerate(peers):
            send = pltpu.make_async_remote_copy(src_ref=v_ref, dst_ref=out_ref.at[me], send_sem=send_sems.at[p],
                                                recv_sem=recv_sems.at[p], device_id=(px, py, pc), device_id_type=MESH)
            send.start()
            recv = pltpu.make_async_remote_copy(src_ref=v_ref, dst_ref=out_ref.at[4 * px + 2 * py + pc], send_sem=send_sems.at[p],
                                                recv_sem=recv_sems.at[p], device_id=(px, py, pc), device_id_type=MESH)
            copies.append((send, recv))
        for send, recv in copies:
            recv.wait_recv()
        for send, recv in copies:
            send.wait_send()
        lc.wait()

    return _seq_call(body, name=name, n_in=1, out_shape=[jax.ShapeDtypeStruct((N_DEV, R, L), F32)],
                     sem_shapes=[pltpu.SemaphoreType.DMA((N_DEV - 1,)), pltpu.SemaphoreType.DMA((N_DEV - 1,)),
                                 pltpu.SemaphoreType.DMA], collective_id=seq_id)(v)[0]


def sum_slots(g, name):
    n, R, L = g.shape
    tr = 216 if R % 216 == 0 else R
    assert R % tr == 0 and tr % SUBLANES == 0

    def body(g_ref, o_ref):
        s = g_ref[0]
        for k in range(1, n):
            s = s + g_ref[k]
        o_ref[...] = s

    return _pcall(body, name=name, grid=(R // tr,), in_specs=[pl.BlockSpec((n, tr, L), lambda i: (0, i, 0))],
                  out_specs=pl.BlockSpec((tr, L), lambda i: (i, 0)), out_shape=jax.ShapeDtypeStruct((R, L), F32),
                  compiler_params=_params(("parallel",)))(g)


def allgather_chips(v, name):
    R, L = v.shape

    def body(v_ref, out_ref, send_sems, recv_sems):
        x, y, c = _me()
        chip = 2 * x + y
        out_ref[chip] = v_ref[...]
        copies = []
        for p in range(1, N_CHIPS):
            px, py = _flip(x, y, p)
            send = pltpu.make_async_remote_copy(src_ref=v_ref, dst_ref=out_ref.at[chip], send_sem=send_sems.at[p - 1],
                                                recv_sem=recv_sems.at[p - 1], device_id=(px, py, c), device_id_type=MESH)
            send.start()
            recv = pltpu.make_async_remote_copy(src_ref=v_ref, dst_ref=out_ref.at[2 * px + py], send_sem=send_sems.at[p - 1],
                                                recv_sem=recv_sems.at[p - 1], device_id=(px, py, c), device_id_type=MESH)
            copies.append((send, recv))
        for send, recv in copies:
            recv.wait_recv()
        for send, recv in copies:
            send.wait_send()

    vm = pl.BlockSpec(memory_space=pltpu.VMEM)
    return _pcall(
        body, name=name, in_specs=[vm], out_specs=vm, out_shape=jax.ShapeDtypeStruct((N_CHIPS, R, L), F32),
        scratch_shapes=[pltpu.SemaphoreType.DMA((N_CHIPS - 1,)), pltpu.SemaphoreType.DMA((N_CHIPS - 1,))],
        compiler_params=_params(),
    )(v)


def _shard_window(ref, kind, shard_shape, chip, half):
    r, c = shard_shape
    hr = r // 2
    if kind == "col":
        return ref.at[pl.ds(pl.multiple_of(half * hr, hr), hr), pl.ds(pl.multiple_of(chip * c, c), c)]
    return ref.at[pl.ds(pl.multiple_of(chip * r + half * hr, hr), hr), :]


def allgather_weights(shards, kinds, name, seq_id=None):
    n = len(shards)
    fulls = []
    for s, kind in zip(shards, kinds):
        r, c = s.shape
        fulls.append(jax.ShapeDtypeStruct((r, N_CHIPS * c) if kind == "col" else (N_CHIPS * r, c), s.dtype))

    def body(*refs):
        srcs, outs = refs[:n], refs[n:2 * n]
        send_sems, recv_sems, fsend_sems, frecv_sems = refs[2 * n:]
        x, y, c = _me()
        chip = 2 * x + y
        sib = (x, y, 1 - c)
        if seq_id is not None:
            _handshake([(*_flip(x, y, p), c) for p in range(1, N_CHIPS)] + [sib])
        sends, fwds = [], []
        for i in range(n):
            shp = srcs[i].shape
            hr = shp[0] // 2
            my_half = srcs[i].at[pl.ds(pl.multiple_of(c * hr, hr), hr), :]
            for p in range(1, N_CHIPS):
                px, py = _flip(x, y, p)
                k = i * (N_CHIPS - 1) + p - 1
                cp = pltpu.make_async_remote_copy(src_ref=my_half, dst_ref=_shard_window(outs[i], kinds[i], shp, chip, c),
                                                  send_sem=send_sems.at[k], recv_sem=recv_sems.at[k],
                                                  device_id=(px, py, c), device_id_type=MESH)
                cp.start()
                sends.append(cp)
        for i in range(n):
            shp = srcs[i].shape
            for p in range(1, N_CHIPS):
                px, py = _flip(x, y, p)
                k = i * (N_CHIPS - 1) + p - 1
                landed = _shard_window(outs[i], kinds[i], shp, 2 * px + py, c)
                pltpu.make_async_remote_copy(src_ref=landed, dst_ref=landed, send_sem=send_sems.at[k], recv_sem=recv_sems.at[k],
                                             device_id=(px, py, c), device_id_type=MESH).wait_recv()
                fw = pltpu.make_async_remote_copy(src_ref=landed, dst_ref=landed, send_sem=fsend_sems.at[k],
                                                  recv_sem=frecv_sems.at[k], device_id=sib, device_id_type=MESH)
                fw.start()
                fwds.append(fw)
        for i in range(n):
            shp = srcs[i].shape
            for p in range(1, N_CHIPS):
                px, py = _flip(x, y, p)
                k = i * (N_CHIPS - 1) + p - 1
                other = _shard_window(outs[i], kinds[i], shp, 2 * px + py, 1 - c)
                pltpu.make_async_remote_copy(src_ref=other, dst_ref=other, send_sem=fsend_sems.at[k], recv_sem=frecv_sems.at[k],
                                             device_id=sib, device_id_type=MESH).wait_recv()
        for cp in sends + fwds:
            cp.wait_send()

    nk = n * (N_CHIPS - 1)
    gathered = _hbm_comm_call(
        body, name=name, n_in=n, out_shape=fulls, seq_id=seq_id,
        sem_shapes=[pltpu.SemaphoreType.DMA((nk,)), pltpu.SemaphoreType.DMA((nk,)), pltpu.SemaphoreType.DMA((nk,)),
                    pltpu.SemaphoreType.DMA((nk,))],
    )(*shards)
    return gathered


def place_local_shards(fulls, shards, kinds, name):
    n = len(shards)
    chip = jnp.reshape(2 * lax.axis_index("x") + lax.axis_index("y"), (1,)).astype(I32)

    def body(ci_ref, *refs):
        for i in range(n):
            refs[2 * n + i][...] = refs[i][...]

    in_specs = [pl.BlockSpec(s.shape, lambda i, ci: (0, 0)) for s in shards] + [pl.BlockSpec(memory_space=pl.ANY)] * n
    out_specs = [pl.BlockSpec(s.shape, (lambda i, ci: (0, ci[0])) if k == "col" else (lambda i, ci: (ci[0], 0)))
                 for s, k in zip(shards, kinds)]
    gs = pltpu.PrefetchScalarGridSpec(num_scalar_prefetch=1, grid=(1,), in_specs=in_specs, out_specs=out_specs)
    return _pcall(body, name=name, grid_spec=gs, out_shape=[jax.ShapeDtypeStruct(f.shape, f.dtype) for f in fulls],
                  input_output_aliases={1 + n + i: i for i in range(n)}, compiler_params=_params(("arbitrary",)))(chip, *shards, *fulls)


def _as_halves(g, kind, shard_shape):
    r, c = shard_shape
    if kind == "col":
        return g.reshape(2, r // 2, N_CHIPS * c)
    return g.reshape(N_CHIPS, 2, r // 2, c)


def exchange_sibling_halves(grads, kinds, shard_shapes, name, seq_id=None):
    n = len(grads)
    views = [_as_halves(g, k, s) for g, k, s in zip(grads, kinds, shard_shapes)]
    outs = []
    for k, (r, c) in zip(kinds, shard_shapes):
        outs.append(jax.ShapeDtypeStruct((r // 2, N_CHIPS * c) if k == "col" else (N_CHIPS, r // 2, c), grads[0].dtype))

    def body(*refs):
        srcs, dsts = refs[:n], refs[n:2 * n]
        send_sems, recv_sems = refs[2 * n:]
        x, y, c = _me()
        if seq_id is not None:
            _handshake([(x, y, 1 - c)])
        cps = []
        for i in range(n):
            src = srcs[i].at[1 - c] if kinds[i] == "col" else srcs[i].at[:, 1 - c]
            cp = pltpu.make_async_remote_copy(src_ref=src, dst_ref=dsts[i], send_sem=send_sems.at[i], recv_sem=recv_sems.at[i],
                                              device_id=(x, y, 1 - c), device_id_type=MESH)
            cp.start()
            cps.append(cp)
        for cp in cps:
            cp.wait_recv()
        for cp in cps:
            cp.wait_send()

    return _hbm_comm_call(body, name=name, n_in=n, out_shape=outs, seq_id=seq_id,
                          sem_shapes=[pltpu.SemaphoreType.DMA((n,)), pltpu.SemaphoreType.DMA((n,))])(*views)


def add_sibling_half(g, recv, kind, shard_shape, core_chip, name):
    r, c = shard_shape
    hr = r // 2
    gv = _as_halves(g, kind, shard_shape)
    tr = hr if hr <= 512 else (256 if hr % 256 == 0 else hr // 2)
    assert hr % tr == 0

    def body(ci_ref, g_ref, r_ref, h_ref, hb_ref):
        s = g_ref[...].astype(F32) + r_ref[...].astype(F32)
        hb_ref[...] = s.astype(BF)

        @pl.when(pl.program_id(1) == ci_ref[1])
        def _():
            h_ref[...] = s

    grid = (hr // tr, N_CHIPS)
    if kind == "col":
        g_spec = pl.BlockSpec((None, tr, c), lambda i, k, ci: (ci[0], i, k))
        o_spec = pl.BlockSpec((tr, c), lambda i, k, ci: (i, k))
    else:
        g_spec = pl.BlockSpec((None, None, tr, c), lambda i, k, ci: (k, ci[0], i, 0))
        o_spec = pl.BlockSpec((None, tr, c), lambda i, k, ci: (k, i, 0))
    own_spec = pl.BlockSpec((tr, c), lambda i, k, ci: (i, 0))
    gs = pltpu.PrefetchScalarGridSpec(num_scalar_prefetch=1, grid=grid, in_specs=[g_spec, o_spec], out_specs=[own_spec, o_spec])
    return _pcall(
        body, name=name, grid_spec=gs,
        out_shape=[jax.ShapeDtypeStruct((hr, c), F32), jax.ShapeDtypeStruct(recv.shape, BF)],
        compiler_params=_params(("parallel", "arbitrary")),
    )(core_chip, gv, recv)


def exchange_chip_pieces(hbs, kinds, shard_shapes, name, seq_id=None):
    n = len(hbs)
    outs = [jax.ShapeDtypeStruct((N_CHIPS - 1, r // 2, c), BF) for (r, c) in shard_shapes]

    def body(*refs):
        srcs, dsts = refs[:n], refs[n:2 * n]
        send_sems, recv_sems = refs[2 * n:]
        x, y, c = _me()
        if seq_id is not None:
            _handshake([(*_flip(x, y, p), c) for p in range(1, N_CHIPS)])
        cps = []
        for i in range(n):
            cc = shard_shapes[i][1]
            for p in range(1, N_CHIPS):
                px, py = _flip(x, y, p)
                pchip = 2 * px + py
                src = (srcs[i].at[:, pl.ds(pl.multiple_of(pchip * cc, cc), cc)] if kinds[i] == "col" else srcs[i].at[pchip])
                k = i * (N_CHIPS - 1) + p - 1
                cp = pltpu.make_async_remote_copy(src_ref=src, dst_ref=dsts[i].at[p - 1], send_sem=send_sems.at[k],
                                                  recv_sem=recv_sems.at[k], device_id=(px, py, c), device_id_type=MESH)
                cp.start()
                cps.append(cp)
        for cp in cps:
            cp.wait_recv()
        for cp in cps:
            cp.wait_send()

    nk = n * (N_CHIPS - 1)
    return _hbm_comm_call(body, name=name, n_in=n, out_shape=outs, seq_id=seq_id,
                          sem_shapes=[pltpu.SemaphoreType.DMA((nk,)), pltpu.SemaphoreType.DMA((nk,))])(*hbs)


def sum_chip_pieces(h_own, pieces, name):
    hr, c = h_own.shape
    tr = hr if hr <= 512 else (256 if hr % 256 == 0 else hr // 2)
    assert hr % tr == 0

    def body(h_ref, p_ref, q_ref):
        q_ref[...] = ((h_ref[...] + p_ref[0].astype(F32)) + p_ref[1].astype(F32)) + p_ref[2].astype(F32)

    blk = pl.BlockSpec((tr, c), lambda i: (i, 0))
    return _pcall(body, name=name, grid=(hr // tr,), in_specs=[blk, pl.BlockSpec((N_CHIPS - 1, tr, c), lambda i: (0, i, 0))],
                  out_specs=blk, out_shape=jax.ShapeDtypeStruct((hr, c), F32), compiler_params=_params(("parallel",)))(h_own, pieces)


def exchange_reduced_halves(qs, name, seq_id):
    n = len(qs)

    def body(*refs):
        srcs, dsts = refs[:n], refs[n:2 * n]
        send_sems, recv_sems = refs[2 * n:]
        x, y, c = _me()
        _handshake([(x, y, 1 - c)])
        cps = []
        for i in range(n):
            cp = pltpu.make_async_remote_copy(src_ref=srcs[i], dst_ref=dsts[i], send_sem=send_sems.at[i], recv_sem=recv_sems.at[i],
                                              device_id=(x, y, 1 - c), device_id_type=MESH)
            cp.start()
            cps.append(cp)
        for cp in cps:
            cp.wait_recv()
        for cp in cps:
            cp.wait_send()

    return _seq_call(body, name=name, n_in=n, out_shape=[jax.ShapeDtypeStruct(q.shape, F32) for q in qs],
                     sem_shapes=[pltpu.SemaphoreType.DMA((n,)), pltpu.SemaphoreType.DMA((n,))], collective_id=seq_id)(*qs)


def _rows128(a):
    return a.reshape(-1, LANES)


def _after(xs, *deps):
    flat = []
    for d in deps:
        flat.extend(d if isinstance(d, (list, tuple)) else [d])
    return list(lax.optimization_barrier((tuple(xs), tuple(flat)))[0])


def _block_diag(w):
    H, d, _ = w.shape
    eye = jnp.eye(H, dtype=w.dtype)
    return jnp.einsum("hde,hg->hdge", w, eye).reshape(H * d, H * d)


def _diag_blocks(g4, H, d):
    nb = g4.shape[0]
    per = LANES // d
    g = g4.reshape(nb, per, d, per, d)
    return jnp.stack([g[:, j, :, j, :] for j in range(per)], axis=1).reshape(H, d, d)


def kernel(x, c, w_mod, b_mod, g_ffn1, w_ffn1_in, w_ffn1_out, g_mix, w_in, conv_w, conv_b, ln_g, ln_b, rnn_conv_w, rnn_conv_b, w_a, b_a, w_i, b_i, lru_lambda, w_out, g_ffn2, w_ffn2_in, w_ffn2_out, w_fmod, b_fmod, g_final, loss_target, m_w_mod, m_b_mod, m_g_ffn1, m_w_ffn1_in, m_w_ffn1_out, m_g_mix, m_w_in, m_conv_w, m_conv_b, m_ln_g, m_ln_b, m_rnn_conv_w, m_rnn_conv_b, m_w_a, m_b_a, m_w_i, m_b_i, m_lru_lambda, m_w_out, m_g_ffn2, m_w_ffn2_in, m_w_ffn2_out, m_w_fmod, m_b_fmod, m_g_final, v_w_mod, v_b_mod, v_g_ffn1, v_w_ffn1_in, v_w_ffn1_out, v_g_mix, v_w_in, v_conv_w, v_conv_b, v_ln_g, v_ln_b, v_rnn_conv_w, v_rnn_conv_b, v_w_a, v_b_a, v_w_i, v_b_i, v_lru_lambda, v_w_out, v_g_ffn2, v_w_ffn2_in, v_w_ffn2_out, v_w_fmod, v_b_fmod, v_g_final):
    S, D = x.shape[1], x.shape[2]
    M = conv_b.shape[1]
    H, HD = w_a.shape[1], w_a.shape[2]
    nb = M // LANES
    ix, iy, ic = lax.axis_index("x"), lax.axis_index("y"), lax.axis_index("c")
    chip = 2 * ix + iy
    dev = 2 * chip + ic
    core_chip = jnp.stack([ic, chip]).astype(I32)
    cidx = core_chip
    xs = x[0]
    tgt = loss_target[0]

    kinds = ["col", "row"]
    w_f1, w_mx, w_f2 = [w_ffn1_in[0], w_ffn1_out[0]], [w_in[0], w_out[0]], [w_ffn2_in[0], w_ffn2_out[0]]
    as_bf = lambda ws: [w.astype(BF) for w in ws]
    shapes_of = lambda ws: [w.shape for w in ws]
    b_f1, b_mx, b_f2 = as_bf(w_f1), as_bf(w_mx), as_bf(w_f2)
    got_f1i = allgather_weights(b_f1[:1], kinds[:1], "gather_ffn1_in", seq_id=9)
    got_f1o = allgather_weights(b_f1[1:], kinds[1:], "gather_ffn1_out", seq_id=13)
    got_mx = allgather_weights(b_mx, kinds, "gather_mix", seq_id=1)
    got_f2 = allgather_weights(b_f2, kinds, "gather_ffn2", seq_id=2)

    c_all =allgather_devices(_rows128(c), "gather_c")[0].reshape(N_DEV, D)
    mod_cols = cond_matmul(c_all, w_mod[0], "mod_proj")
    fmod_cols = cond_matmul(c_all, w_fmod, "fmod_proj")
    convw_pad = jnp.pad(conv_w[0], ((0, CONV_TAPS - CONV_WIDTH), (0, 0)))
    rnnw_pad = jnp.pad(rnn_conv_w[0], ((0, SUBLANES - RNN_CONV_WIDTH), (0, 0)))
    n_mod, n_fmod = mod_cols.shape[1], fmod_cols.shape[1]
    small = jnp.concatenate([_rows128(mod_cols), _rows128(fmod_cols), convw_pad, rnnw_pad], axis=0)
    small4 = allgather_chips(small, "gather_cond")
    r0 = N_DEV * n_mod // LANES
    r1 = r0 + N_DEV * n_fmod // LANES
    mod_all = small4[:, :r0].reshape(N_CHIPS, N_DEV, n_mod)
    fmod_all = small4[:, r0:r1].reshape(N_CHIPS, N_DEV, n_fmod)
    convw4 = small4[:, r1:r1 + CONV_TAPS]
    rnnw4 = small4[:, r1 + CONV_TAPS:r1 + CONV_TAPS + SUBLANES]
    mod_row = lax.dynamic_index_in_dim(mod_all, dev, axis=1, keepdims=False).reshape(1, N_CHIPS * n_mod) + b_mod
    fmod_row = lax.dynamic_index_in_dim(fmod_all, dev, axis=1, keepdims=False).reshape(1, N_CHIPS * n_fmod) + b_fmod[None, :]
    vecs = jnp.concatenate([mod_row.reshape(9, D), fmod_row.reshape(2, D), g_ffn1, g_mix, g_ffn2, g_final[None, :],
                            jnp.zeros((1, D), F32)], axis=0)
    lnv = jnp.concatenate([ln_g, ln_b, jnp.zeros((SUBLANES - 2, M), F32)], axis=0)
    bda = _block_diag(w_a[0]).astype(BF)
    bdi = _block_diag(w_i[0]).astype(BF)

    def reduce_add(gs, recv, ws, tag, kinds_=kinds):
        pairs = [add_sibling_half(g, r_, k, w.shape, core_chip, f"add_sibling_{tag}{j}")
                 for j, (g, r_, k, w) in enumerate(zip(gs, recv, kinds_, ws))]
        return [p[0] for p in pairs], [p[1] for p in pairs]

    def reduce_sum(hs_, recv, ws, tag, kinds_=kinds):
        return [sum_chip_pieces(h_, p_, f"sum_chips_{tag}{j}") for j, (h_, p_) in enumerate(zip(hs_, recv))]

    rows1 = (R_SH1, R_SC1, R_GT1, R_G1)
    rows3 = (R_SH3, R_SC3, R_GT3, R_G3)
    (wi1,) = place_local_shards(got_f1i, b_f1[:1], kinds[:1], "place_ffn1_in")
    g1s, u1s, a1s = ffn_fwd_in(xs, vecs, wi1, rows1, "ffn1_fwd_in")
    (wo1,) = place_local_shards(_after(got_f1o, a1s), b_f1[1:], kinds[1:], "place_ffn1_out")
    x1, y1 = ffn_fwd_out(a1s, xs, vecs, wo1, rows1, "ffn1_fwd_out")
    win, wout = place_local_shards(_after(got_mx, x1), b_mx, kinds, "place_mix")
    proj = norm_matmul(x1, vecs, win, (R_SH2, R_SC2, R_G2), "mix_in_proj")
    cq = conv_fwd(proj, convw4, conv_b, "conv_fwd")
    xr, ra, ii, hh = rnn_fwd(proj, rnnw4, rnn_conv_b, bda, bdi, b_a, b_i, lru_lambda, "rnn_fwd")
    x2, ym, ycat = mix_out(cq, proj, hh, x1, vecs, lnv, wout, "mix_out")
    wi2, wo2 = place_local_shards(_after(got_f2, x2), b_f2, kinds, "place_ffn2")
    dx3, g2s, u2s, y2, vgf = ffn_fwd(x2, vecs, wi2, wo2, rows3, "ffn2_fwd", final_tgt=tgt)

    Fd = wo1.shape[0]
    tk = min(2048, S)
    dx2, act2, dg2, du2, h3b, dy2b, vg3 = ffn_bwd(dx3, x2, vecs, g2s, u2s, y2, wi2, wo2, rows3, "ffn2_bwd")
    gwo2 = matmul(act2, dy2b, "tn", tm=Fd // 2, tn=D, tk=tk, out_dtype=BF, name="ffn2_dwo")
    gwi2 = matmul(h3b, dg2, "tn", tm=D, tn=Fd // 2, tk=tk, out_dtype=BF, name="ffn2_dwg", out_cols=2 * Fd)
    gwi2 = matmul(h3b, du2, "tn", tm=D, tn=Fd // 2, tk=tk, out_dtype=BF, name="ffn2_dwu", out_cols=2 * Fd, col_off=Fd, prev=gwi2)
    recv1_f2 = exchange_sibling_halves([gwi2, gwo2], kinds, shapes_of(w_f2), "reduce1_ffn2", seq_id=3)
    dcq, dhout, duy, dymb, vgd, vgm = mix_out_bwd(dx2, ym, vecs, wout, cq, lnv, proj, hh, "mix_out_bwd")
    gwout = matmul(ycat, dymb, "tn", tm=2 * M, tn=D, tk=tk, out_dtype=BF, name="mix_dwout")
    recv1_f2 = _after(recv1_f2, gwout)
    h_f2, hb_f2 = reduce_add([gwi2, gwo2], recv1_f2, w_f2, "ffn2_")
    recv2_f2 = exchange_chip_pieces(hb_f2, kinds, shapes_of(w_f2), "reduce2_ffn2", seq_id=4)
    duv, dug, dconvw4, dconvb = conv_bwd(_after([dcq], hb_f2)[0], proj, convw4, "conv_bwd")
    dux, dwa4, dwi4, drnnw4, rvec = rnn_bwd(dhout, hh, xr, ra, ii, proj, rnnw4, bda, bdi, lru_lambda, "rnn_bwd")
    dx1, h2b, dpb, vg2 = mix_in_bwd((duv, dug, dux, duy), x1, dx2, vecs, win, "mix_in_bwd")
    gwin = matmul(h2b, dpb, "tn", tm=D, tn=1024, tk=tk, out_dtype=BF, name="mix_dwin")
    recv1_mx = exchange_sibling_halves([gwin, gwout], kinds, shapes_of(w_mx), "reduce1_mix", seq_id=5)
    q_f2 = reduce_sum(_after(h_f2, gwin), recv2_f2, w_f2, "ffn2_")
    r_f2 = exchange_reduced_halves(q_f2, "reduce3_ffn2", seq_id=14)
    h_mx, hb_mx = reduce_add([gwin, gwout], _after(recv1_mx, q_f2), w_mx, "mix_")
    recv2_mx = exchange_chip_pieces(hb_mx, kinds, shapes_of(w_mx), "reduce2_mix", seq_id=6)
    dx0, act1, dg1, du1, h1b, dy1b, vg1 = ffn_bwd(_after([dx1], hb_mx)[0], xs, vecs, g1s, u1s, y1, wi1, wo1, rows1, "ffn1_bwd")
    dmod_row = jnp.concatenate([vg1[1:3], vg1[0:1], vg2[0:2], vgd[0:1], vg3[1:3], vg3[0:1]], axis=0)
    gains = jnp.concatenate([vg1[3:4], vg2[2:3], vg3[3:4], vgf[2:4]], axis=0)
    mvecs = jnp.concatenate([dconvb, vgm[0:2], rvec[0:4], jnp.zeros((1, M), F32)], axis=0)
    parts = [_rows128(dmod_row), _rows128(vgf[0:2]), _rows128(gains), _rows128(mvecs),
             _rows128(dconvw4), _rows128(drnnw4), _rows128(_diag_blocks(dwa4, H, HD)), _rows128(_diag_blocks(dwi4, H, HD))]
    sizes = [p.shape[0] for p in parts]
    packed = jnp.concatenate(parts, axis=0)
    gathered = allgather_devices_hbm(packed, "gather_small", seq_id=10)

    gwo1 = matmul(_after([act1], recv2_mx, packed)[0], dy1b, "tn", tm=Fd // 2, tn=D, tk=tk, out_dtype=BF, name="ffn1_dwo")
    w_f1o, w_f1i = w_f1[1:], w_f1[:1]
    recv1_f1o = exchange_sibling_halves([gwo1], ["row"], shapes_of(w_f1o), "reduce1_ffn1_out", seq_id=7)
    q_mx = reduce_sum(_after(h_mx, gwo1), recv2_mx, w_mx, "mix_")
    r_mx = exchange_reduced_halves(q_mx, "reduce3_mix", seq_id=15)
    gwi1 = matmul(_after([h1b], q_mx)[0], dg1, "tn", tm=D, tn=Fd // 2, tk=tk, out_dtype=BF, name="ffn1_dwg", out_cols=2 * Fd)
    h_f1o, hb_f1o = reduce_add([gwo1], _after(recv1_f1o, gwi1), w_f1o, "ffn1_out", ["row"])
    recv2_f1o = exchange_chip_pieces(hb_f1o, ["row"], shapes_of(w_f1o), "reduce2_ffn1_out", seq_id=11)
    gwi1 = matmul(h1b, _after([du1], hb_f1o, gathered)[0], "tn", tm=D, tn=Fd // 2, tk=tk, out_dtype=BF, name="ffn1_dwu", out_cols=2 * Fd,
                  col_off=Fd, prev=gwi1)
    recv1_f1i = exchange_sibling_halves([gwi1], ["col"], shapes_of(w_f1i), "reduce1_ffn1_in", seq_id=12)
    q_f1o = reduce_sum(_after(h_f1o, gwi1), recv2_f1o, w_f1o, "ffn1_out", ["row"])
    r_f1o = exchange_reduced_halves(q_f1o, "reduce3_ffn1_out", seq_id=16)
    summed = sum_slots(gathered, "sum_small")
    offs = [0]
    for s in sizes:
        offs.append(offs[-1] + s)
    seg = lambda k: summed[offs[k]:offs[k + 1]]
    g_b_mod = seg(0).reshape(1, 9 * D)
    g_b_fmod = seg(1).reshape(1, 2 * D)
    gsum = seg(2).reshape(5, D)
    loss = (0.5 / D) * jnp.sum(gsum[4])
    msum = seg(3).reshape(SUBLANES, M)
    g_conv_w = lax.dynamic_index_in_dim(seg(4).reshape(nb, CONV_TAPS, LANES), chip, axis=0, keepdims=False)[:CONV_WIDTH]
    g_rnn_w = lax.dynamic_index_in_dim(seg(5).reshape(nb, SUBLANES, LANES), chip, axis=0, keepdims=False)[:RNN_CONV_WIDTH]
    g_w_a = seg(6).reshape(H, HD, HD)
    g_w_i = seg(7).reshape(H, HD, HD)
    dmod_all = gathered[:, offs[0]:offs[1]].reshape(N_DEV, 9 * D)
    dfmod_all = gathered[:, offs[1]:offs[2]].reshape(N_DEV, 2 * D)
    dmod_cols = lax.dynamic_slice_in_dim(dmod_all, chip * n_mod, n_mod, axis=1)
    dfmod_cols = lax.dynamic_slice_in_dim(dfmod_all, chip * n_fmod, n_fmod, axis=1)

    h_f1i, hb_f1i = reduce_add([gwi1], _after(recv1_f1i, q_f1o), w_f1i, "ffn1_in", ["col"])
    recv2_f1i = exchange_chip_pieces(hb_f1i, ["col"], shapes_of(w_f1i), "reduce2_ffn1_in", seq_id=8)
    dmod_cols, dfmod_cols = _after([dmod_cols, dfmod_cols], hb_f1i)
    g_w_mod, d_w_mod, nm_w_mod, nv_w_mod = adam_cond(c_all, dmod_cols, w_mod[0], m_w_mod[0], v_w_mod[0], "adam_w_mod")
    g_w_fmod, d_w_fmod, nm_w_fmod, nv_w_fmod = adam_cond(c_all, dfmod_cols, w_fmod, m_w_fmod, v_w_fmod, "adam_w_fmod")

    def adam_group(ws, qs_, rs_, ms, vs, tags, after):
        qs_ = _after(list(qs_), *after) if after else list(qs_)
        return [adam_big(w, q_, r_, m, v, cidx, "adam_" + t) for w, q_, r_, m, v, t in zip(ws, qs_, rs_, ms, vs, tags)]

    ad_f2 = adam_group(w_f2, q_f2, r_f2, [m_w_ffn2_in[0], m_w_ffn2_out[0]], [v_w_ffn2_in[0], v_w_ffn2_out[0]],
                       ["ffn2_in", "ffn2_out"], [hb_f1i])
    ad_mx = adam_group(w_mx, q_mx, r_mx, [m_w_in[0], m_w_out[0]], [v_w_in[0], v_w_out[0]], ["w_in", "w_out"], [hb_f1i])
    ad_f1o = adam_group(w_f1o, q_f1o, r_f1o, [m_w_ffn1_out[0]], [v_w_ffn1_out[0]], ["ffn1_out"], [hb_f1i])
    q_f1i = reduce_sum(_after(h_f1i, ad_f2[0][0], ad_f2[1][0], ad_mx[0][0], ad_mx[1][0], ad_f1o[0][0], g_w_mod, g_w_fmod),
                       recv2_f1i, w_f1i, "ffn1_in", ["col"])
    r_f1i = exchange_reduced_halves(q_f1i, "reduce3_ffn1_in", seq_id=17)
    ad_f1i = adam_group(w_f1i, q_f1i, r_f1i, [m_w_ffn1_in[0]], [v_w_ffn1_in[0]], ["ffn1_in"], [])
    big_out = ad_f1i + ad_f1o + ad_mx + ad_f2

    flat2 = lambda a: a.reshape(-1, a.shape[-1])
    small_names = ["b_mod", "g_ffn1", "g_mix", "conv_w", "conv_b", "ln_g", "ln_b", "rnn_conv_w", "rnn_conv_b", "w_a", "b_a",
                   "w_i", "b_i", "lru_lambda", "g_ffn2", "b_fmod", "g_final"]
    small_w = [b_mod, g_ffn1, g_mix, conv_w, conv_b, ln_g, ln_b, rnn_conv_w, rnn_conv_b, w_a, b_a, w_i, b_i, lru_lambda,
               g_ffn2, b_fmod, g_final]
    small_m = [m_b_mod, m_g_ffn1, m_g_mix, m_conv_w, m_conv_b, m_ln_g, m_ln_b, m_rnn_conv_w, m_rnn_conv_b, m_w_a, m_b_a,
               m_w_i, m_b_i, m_lru_lambda, m_g_ffn2, m_b_fmod, m_g_final]
    small_v = [v_b_mod, v_g_ffn1, v_g_mix, v_conv_w, v_conv_b, v_ln_g, v_ln_b, v_rnn_conv_w, v_rnn_conv_b, v_w_a, v_b_a,
               v_w_i, v_b_i, v_lru_lambda, v_g_ffn2, v_b_fmod, v_g_final]
    small_g = [g_b_mod, gsum[0:1], gsum[1:2], g_conv_w, msum[0:1], msum[1:2], msum[2:3], g_rnn_w, msum[3:4], g_w_a, msum[4:5],
               g_w_i, msum[5:6], msum[6:7], gsum[2:3], g_b_fmod, gsum[3:4]]
    small_g = [g.reshape(w.shape) for g, w in zip(small_g, small_w)]
    two_d = lambda a: a.reshape(1, -1) if a.ndim == 1 else flat2(a)
    sd, sm, sv = adam_small([two_d(a) for a in small_w], [two_d(a) for a in small_g], [two_d(a) for a in small_m],
                            [two_d(a) for a in small_v], "adam_small")
    small = {}
    for k, nm in enumerate(small_names):
        shp = small_w[k].shape
        small[nm] = (small_g[k], sd[k].reshape(shp), sm[k].reshape(shp), sv[k].reshape(shp))

    big = {"w_mod": tuple(a[None] for a in (g_w_mod, d_w_mod, nm_w_mod, nv_w_mod)),
           "w_fmod": (g_w_fmod, d_w_fmod, nm_w_fmod, nv_w_fmod)}
    for nm, res in zip(["w_ffn1_in", "w_ffn1_out", "w_in", "w_out", "w_ffn2_in", "w_ffn2_out"], big_out):
        big[nm] = tuple(a[None] for a in res)
    order = ["w_mod", "b_mod", "g_ffn1", "w_ffn1_in", "w_ffn1_out", "g_mix", "w_in", "conv_w", "conv_b", "ln_g", "ln_b",
             "rnn_conv_w", "rnn_conv_b", "w_a", "b_a", "w_i", "b_i", "lru_lambda", "w_out", "g_ffn2", "w_ffn2_in",
             "w_ffn2_out", "w_fmod", "b_fmod", "g_final"]
    table = {**small, **big}
    outs = [loss, dx0[None]]
    for kind_ in range(4):
        outs.extend(table[nm][kind_] for nm in order)
    return tuple(outs)
```

```python
import functools

import jax
import jax.numpy as jnp
from jax import lax
from jax.experimental import pallas as pl
from jax.experimental.pallas import tpu as pltpu
from jax.experimental.pallas import tpu_sc as plsc

F32 = jnp.float32
BF = jnp.bfloat16
I32 = jnp.int32
MESH = pl.DeviceIdType.MESH

EPS = 1e-6
RG_C = 8.0
MACARON_W = 0.5
CONV_WIDTH = 31
RNN_CONV_WIDTH = 4
ADAM_LR = 0.001
ADAM_B1 = 0.9
ADAM_B2 = 0.999
ADAM_EPS = 1e-08
ADAM_WD = 0.01
ADAM_STEP = 10

LANES = 128
SUBLANES = 8
VMEM_LIMIT = 62 * 1024 * 1024
N_CHIPS = 4
N_DEV = 8

R_SH1, R_SC1, R_GT1, R_SH2, R_SC2, R_GT2, R_SH3, R_SC3, R_GT3, R_FSH, R_FSC, R_G1, R_G2, R_G3, R_GF = range(15)

CONTRACT_LAST = (((1,), (1,)), ((), ()))
CONTRACT_FIRST = (((0,), (0,)), ((), ()))


def _pcall(body, **kw):
    return pl.pallas_call(body, **kw)


def _params(sem=None, vmem=VMEM_LIMIT):
    if sem is None:
        return pltpu.CompilerParams(vmem_limit_bytes=vmem)
    return pltpu.CompilerParams(dimension_semantics=sem, vmem_limit_bytes=vmem)


def _row(ref, r):
    return ref[r:r + 1, :]


def _sigmoid(x):
    return 0.5 * jnp.tanh(0.5 * x) + 0.5


def _colsum(x):
    return jnp.sum(x, axis=0, keepdims=True)


def _rowmean(x):
    return jnp.mean(x, axis=-1, keepdims=True)


def matmul(a, b, mode, *, tm, tn, tk, name, out_dtype=F32, out_cols=None, col_off=0, prev=None):
    if mode == "nn":
        (M, K), (K2, N) = a.shape, b.shape
    elif mode == "nt":
        (M, K), (N, K2) = a.shape, b.shape
    else:
        (K, M), (K2, N) = a.shape, b.shape
    assert K == K2 and M % tm == 0 and N % tn == 0 and K % tk == 0 and col_off % tn == 0
    nk = K // tk
    out_cols = N if out_cols is None else out_cols
    off = col_off // tn

    def body(*refs):
        if prev is None:
            a_ref, b_ref, o_ref, acc = refs
        else:
            a_ref, b_ref, _, o_ref, acc = refs
        k = pl.program_id(2)
        av = a_ref[...].astype(BF)
        bv = b_ref[...].astype(BF)
        if mode == "nn":
            part = jnp.dot(av, bv, preferred_element_type=F32)
        elif mode == "nt":
            part = lax.dot_general(av, bv, CONTRACT_LAST, preferred_element_type=F32)
        else:
            part = lax.dot_general(av, bv, CONTRACT_FIRST, preferred_element_type=F32)
        if nk == 1:
            o_ref[...] = part.astype(out_dtype)
            return

        @pl.when(k == 0)
        def _():
            acc[...] = part

        @pl.when((k > 0) & (k < nk - 1))
        def _():
            acc[...] += part

        @pl.when(k == nk - 1)
        def _():
            o_ref[...] = (acc[...] + part).astype(out_dtype)

    if mode == "nn":
        a_spec = pl.BlockSpec((tm, tk), lambda m, n, k: (m, k))
        b_spec = pl.BlockSpec((tk, tn), lambda m, n, k: (k, n))
    elif mode == "nt":
        a_spec = pl.BlockSpec((tm, tk), lambda m, n, k: (m, k))
        b_spec = pl.BlockSpec((tn, tk), lambda m, n, k: (n, k))
    else:
        a_spec = pl.BlockSpec((tk, tm), lambda m, n, k: (k, m))
        b_spec = pl.BlockSpec((tk, tn), lambda m, n, k: (k, n))
    in_specs = [a_spec, b_spec]
    args = [a, b]
    aliases = {}
    if prev is not None:
        in_specs.append(pl.BlockSpec(memory_space=pl.ANY))
        args.append(prev)
        aliases = {2: 0}
    return _pcall(
        body, name=name, grid=(M // tm, N // tn, nk), in_specs=in_specs,
        out_specs=pl.BlockSpec((tm, tn), lambda m, n, k: (m, n + off)),
        out_shape=jax.ShapeDtypeStruct((M, out_cols), out_dtype),
        scratch_shapes=[pltpu.VMEM((tm, tn), F32)], input_output_aliases=aliases,
        compiler_params=_params(("parallel", "parallel", "arbitrary")),
    )(*args)


def cond_matmul(c_all, w, name):
    B, K = c_all.shape
    N = w.shape[1]
    tn = 256
    assert N % tn == 0

    def body(c_ref, w_ref, o_ref):
        cv = c_ref[...]
        ca = cv * _sigmoid(cv)
        o_ref[...] = jnp.dot(ca.astype(BF), w_ref[...].astype(BF), preferred_element_type=F32)

    return _pcall(
        body, name=name, grid=(N // tn,),
        in_specs=[pl.BlockSpec((B, K), lambda n: (0, 0)), pl.BlockSpec((K, tn), lambda n: (0, n))],
        out_specs=pl.BlockSpec((B, tn), lambda n: (0, n)),
        out_shape=jax.ShapeDtypeStruct((B, N), F32), compiler_params=_params(("parallel",)),
    )(c_all, w)


FFN_FWD_TS = 512
FFN_BWD_TS = 256


def _resident(shape, index_map):
    return pl.BlockSpec(shape, index_map, pipeline_mode=pl.Buffered(1))


def _final_norm_loss_grad(xv, t, v_ref, vg_ref):
    D = xv.shape[-1]
    r = lax.rsqrt(_rowmean(xv * xv) + EPS)
    n = xv * r
    g = _row(v_ref, R_GF)
    sc1 = 1.0 + _row(v_ref, R_FSC)
    gsc = g * sc1
    e = n * gsc + _row(v_ref, R_FSH) - t
    vg_ref[3:4, :] += _colsum(e * e)
    dout = e * (1.0 / D)
    dn_ = dout * n
    vg_ref[0:1, :] += _colsum(dout)
    vg_ref[1:2, :] += _colsum(dn_) * g
    vg_ref[2:3, :] += _colsum(dn_) * sc1
    dn = dout * gsc
    return r * (dn - n * _rowmean(dn * n))


def ffn_fwd(x, vecs, wi, wo, rows, name, final_tgt=None):
    r_sh, r_sc, r_gt, r_g = rows
    S, D = x.shape
    Fd = wo.shape[0]
    ts = min(FFN_FWD_TS, S)
    with_final = final_tgt is not None

    def body(*refs):
        if with_final:
            x_ref, v_ref, wg_ref, wu_ref, wo_ref, t_ref, xo_ref, g_ref, u_ref, y_ref, vg_ref = refs
        else:
            x_ref, v_ref, wg_ref, wu_ref, wo_ref, xo_ref, g_ref, u_ref, y_ref = refs
        xv = x_ref[...]
        r = lax.rsqrt(_rowmean(xv * xv) + EPS)
        gs = _row(v_ref, r_g) * (1.0 + _row(v_ref, r_sc))
        hb = (xv * r * gs + _row(v_ref, r_sh)).astype(BF)
        G = jnp.dot(hb, wg_ref[...], preferred_element_type=F32)
        U = jnp.dot(hb, wu_ref[...], preferred_element_type=F32)
        g_ref[...] = G.astype(BF)
        u_ref[...] = U.astype(BF)
        act = (G * _sigmoid(G) * U).astype(BF)
        Y = jnp.dot(act, wo_ref[...], preferred_element_type=F32)
        y_ref[...] = Y.astype(BF)
        xo = xv + (MACARON_W * _row(v_ref, r_gt)) * Y
        if with_final:
            @pl.when(pl.program_id(0) == 0)
            def _():
                vg_ref[...] = jnp.zeros_like(vg_ref)

            xo_ref[...] = _final_norm_loss_grad(xo, t_ref[...], v_ref, vg_ref)
        else:
            xo_ref[...] = xo

    tok = pl.BlockSpec((ts, D), lambda i: (i, 0))
    hid = pl.BlockSpec((ts, Fd), lambda i: (i, 0))
    in_specs = [tok, pl.BlockSpec(vecs.shape, lambda i: (0, 0)), _resident((D, Fd), lambda i: (0, 0)),
                _resident((D, Fd), lambda i: (0, 1)), _resident((Fd, D), lambda i: (0, 0))]
    out_specs = [tok, hid, hid, tok]
    out_shape = [jax.ShapeDtypeStruct((S, D), F32), jax.ShapeDtypeStruct((S, Fd), BF),
                 jax.ShapeDtypeStruct((S, Fd), BF), jax.ShapeDtypeStruct((S, D), BF)]
    args = [x, vecs, wi, wi, wo]
    if with_final:
        in_specs.append(tok)
        args.append(final_tgt)
        out_specs.append(pl.BlockSpec((SUBLANES, D), lambda i: (0, 0)))
        out_shape.append(jax.ShapeDtypeStruct((SUBLANES, D), F32))
    return _pcall(body, name=name, grid=(S // ts,), in_specs=in_specs, out_specs=out_specs, out_shape=out_shape,
                  compiler_params=_params(("arbitrary",)))(*args)


def ffn_fwd_in(x, vecs, wi, rows, name):
    r_sh, r_sc, r_gt, r_g = rows
    S, D = x.shape
    Fd = wi.shape[1] // 2
    ts = min(FFN_FWD_TS, S)

    def body(x_ref, v_ref, wg_ref, wu_ref, g_ref, u_ref, a_ref):
        xv = x_ref[...]
        r = lax.rsqrt(_rowmean(xv * xv) + EPS)
        gs = _row(v_ref, r_g) * (1.0 + _row(v_ref, r_sc))
        hb = (xv * r * gs + _row(v_ref, r_sh)).astype(BF)
        G = jnp.dot(hb, wg_ref[...], preferred_element_type=F32)
        U = jnp.dot(hb, wu_ref[...], preferred_element_type=F32)
        g_ref[...] = G.astype(BF)
        u_ref[...] = U.astype(BF)
        a_ref[...] = (G * _sigmoid(G) * U).astype(BF)

    hid = pl.BlockSpec((ts, Fd), lambda i: (i, 0))
    return _pcall(
        body, name=name, grid=(S // ts,),
        in_specs=[pl.BlockSpec((ts, D), lambda i: (i, 0)), pl.BlockSpec(vecs.shape, lambda i: (0, 0)),
                  _resident((D, Fd), lambda i: (0, 0)), _resident((D, Fd), lambda i: (0, 1))],
        out_specs=[hid, hid, hid], out_shape=[jax.ShapeDtypeStruct((S, Fd), BF)] * 3,
        compiler_params=_params(("arbitrary",)),
    )(x, vecs, wi, wi)


def ffn_fwd_out(act, x, vecs, wo, rows, name):
    r_sh, r_sc, r_gt, r_g = rows
    S, D = x.shape
    Fd = wo.shape[0]
    ts = min(FFN_FWD_TS, S)

    def body(a_ref, x_ref, v_ref, wo_ref, xo_ref, y_ref):
        Y = jnp.dot(a_ref[...], wo_ref[...], preferred_element_type=F32)
        y_ref[...] = Y.astype(BF)
        xo_ref[...] = x_ref[...] + (MACARON_W * _row(v_ref, r_gt)) * Y

    tok = pl.BlockSpec((ts, D), lambda i: (i, 0))
    return _pcall(
        body, name=name, grid=(S // ts,),
        in_specs=[pl.BlockSpec((ts, Fd), lambda i: (i, 0)), tok, pl.BlockSpec(vecs.shape, lambda i: (0, 0)),
                  _resident((Fd, D), lambda i: (0, 0))],
        out_specs=[tok, tok], out_shape=[jax.ShapeDtypeStruct((S, D), F32), jax.ShapeDtypeStruct((S, D), BF)],
        compiler_params=_params(("arbitrary",)),
    )(act, x, vecs, wo)


def ffn_bwd(dxo, x, vecs, gs_, us_, y, wi, wo, rows, name):
    r_sh, r_sc, r_gt, r_g = rows
    S, D = x.shape
    Fd = wo.shape[0]
    ts = min(FFN_BWD_TS, S)

    def body(dxo_ref, x_ref, v_ref, g_ref, u_ref, y_ref, wg_ref, wu_ref, wo_ref,
             dx_ref, act_ref, dg_ref, du_ref, hb_ref, dyb_ref, vg_ref):
        @pl.when(pl.program_id(0) == 0)
        def _():
            vg_ref[...] = jnp.zeros_like(vg_ref)

        dxo_v = dxo_ref[...]
        dyb = ((MACARON_W * _row(v_ref, r_gt)) * dxo_v).astype(BF)
        dyb_ref[...] = dyb
        vg_ref[0:1, :] += MACARON_W * _colsum(dxo_v * y_ref[...].astype(F32))
        dA = lax.dot_general(dyb, wo_ref[...], CONTRACT_LAST, preferred_element_type=F32)
        G = g_ref[...].astype(F32)
        U = u_ref[...].astype(F32)
        sg = _sigmoid(G)
        sl = G * sg
        dU = (dA * sl).astype(BF)
        dG = (dA * U * (sg * (1.0 + G * (1.0 - sg)))).astype(BF)
        act_ref[...] = (sl * U).astype(BF)
        dg_ref[...] = dG
        du_ref[...] = dU
        dh = (lax.dot_general(dG, wg_ref[...], CONTRACT_LAST, preferred_element_type=F32)
              + lax.dot_general(dU, wu_ref[...], CONTRACT_LAST, preferred_element_type=F32))
        xv = x_ref[...]
        r = lax.rsqrt(_rowmean(xv * xv) + EPS)
        n = xv * r
        g = _row(v_ref, r_g)
        sc1 = 1.0 + _row(v_ref, r_sc)
        gsc = g * sc1
        hb_ref[...] = (n * gsc + _row(v_ref, r_sh)).astype(BF)
        dhn = dh * n
        vg_ref[1:2, :] += _colsum(dh)
        vg_ref[2:3, :] += _colsum(dhn) * g
        vg_ref[3:4, :] += _colsum(dhn) * sc1
        dn = dh * gsc
        dx_ref[...] = dxo_v + r * (dn - n * _rowmean(dn * n))

    tok = pl.BlockSpec((ts, D), lambda i: (i, 0))
    hid = pl.BlockSpec((ts, Fd), lambda i: (i, 0))
    return _pcall(
        body, name=name, grid=(S // ts,),
        in_specs=[tok, tok, pl.BlockSpec(vecs.shape, lambda i: (0, 0)), hid, hid, tok, _resident((D, Fd), lambda i: (0, 0)),
                  _resident((D, Fd), lambda i: (0, 1)), _resident((Fd, D), lambda i: (0, 0))],
        out_specs=[tok, hid, hid, hid, tok, tok, pl.BlockSpec((SUBLANES, D), lambda i: (0, 0))],
        out_shape=[jax.ShapeDtypeStruct((S, D), F32), jax.ShapeDtypeStruct((S, Fd), BF),
                   jax.ShapeDtypeStruct((S, Fd), BF), jax.ShapeDtypeStruct((S, Fd), BF),
                   jax.ShapeDtypeStruct((S, D), BF), jax.ShapeDtypeStruct((S, D), BF),
                   jax.ShapeDtypeStruct((SUBLANES, D), F32)],
        compiler_params=_params(("arbitrary",)),
    )(dxo, x, vecs, gs_, us_, y, wi, wi, wo)


def norm_matmul(x, vecs, w, rows, name):
    r_sh, r_sc, r_g = rows
    S, D = x.shape
    N = w.shape[1]
    ts = min(512, S)

    def body(x_ref, v_ref, w_ref, o_ref):
        xv = x_ref[...]
        r = lax.rsqrt(_rowmean(xv * xv) + EPS)
        gs = _row(v_ref, r_g) * (1.0 + _row(v_ref, r_sc))
        hb = (xv * r * gs + _row(v_ref, r_sh)).astype(BF)
        o_ref[...] = jnp.dot(hb, w_ref[...], preferred_element_type=F32).astype(BF)

    return _pcall(
        body, name=name, grid=(S // ts,),
        in_specs=[pl.BlockSpec((ts, D), lambda i: (i, 0)), pl.BlockSpec(vecs.shape, lambda i: (0, 0)),
                  _resident((D, N), lambda i: (0, 0))],
        out_specs=pl.BlockSpec((ts, N), lambda i: (i, 0)),
        out_shape=jax.ShapeDtypeStruct((S, N), BF),
        compiler_params=_params(("arbitrary",)),
    )(x, vecs, w)


SEQ_TT = 256
SCAN_SEGMENTS = 4
CONV_BWD_TT = 128
CONV_PAD = 32
CONV_TAPS = 32


def conv_fwd(proj, convw4, conv_b, name):
    S = proj.shape[0]
    M = conv_b.shape[1]
    nb = M // LANES
    tt = min(SEQ_TT, S)

    def body(uv_ref, ug_ref, w_ref, b_ref, cq_ref, qp):
        qp[0:CONV_PAD, :] = jnp.zeros((CONV_PAD, LANES), F32)

        def step(t, carry):
            base = pl.multiple_of(t * tt, tt)
            qp[pl.ds(base + CONV_PAD, tt), :] = uv_ref[pl.ds(base, tt), :].astype(F32) * _sigmoid(ug_ref[pl.ds(base, tt), :].astype(F32))
            acc = jnp.broadcast_to(b_ref[...], (tt, LANES))
            for k in range(CONV_WIDTH):
                acc = acc + w_ref[k:k + 1, :] * qp[pl.ds(base + (CONV_PAD - CONV_WIDTH + 1) + k, tt), :]
            cq_ref[pl.ds(base, tt), :] = acc
            return carry

        lax.fori_loop(0, S // tt, step, 0)

    return _pcall(
        body, name=name, grid=(nb,),
        in_specs=[pl.BlockSpec((S, LANES), lambda c: (0, c)), pl.BlockSpec((S, LANES), lambda c: (0, c + nb)),
                  pl.BlockSpec((None, CONV_TAPS, LANES), lambda c: (c, 0, 0)), pl.BlockSpec((1, LANES), lambda c: (0, c))],
        out_specs=pl.BlockSpec((S, LANES), lambda c: (0, c)),
        out_shape=jax.ShapeDtypeStruct((S, M), F32),
        scratch_shapes=[pltpu.VMEM((S + CONV_PAD, LANES), F32)],
        compiler_params=_params(("arbitrary",)),
    )(proj, proj, convw4, conv_b)


def conv_bwd(dcq, proj, convw4, name):
    S, M = dcq.shape
    nb = M // LANES
    tt = min(CONV_BWD_TT, S)

    def body(dcq_ref, uv_ref, ug_ref, w_ref, duv_ref, dug_ref, dw_ref, db_ref, dp, dw8, db8):
        dp[S:S + CONV_PAD, :] = jnp.zeros((CONV_PAD, LANES), F32)
        dw8[...] = jnp.zeros_like(dw8)
        db8[...] = jnp.zeros_like(db8)

        def fill(t, carry):
            base = pl.multiple_of(t * tt, tt)
            dp[pl.ds(base, tt), :] = dcq_ref[pl.ds(base, tt), :].astype(F32)
            return carry

        lax.fori_loop(0, S // tt, fill, 0)

        def step(t, carry):
            base = pl.multiple_of(t * tt, tt)
            uv = uv_ref[pl.ds(base, tt), :].astype(F32)
            sg = _sigmoid(ug_ref[pl.ds(base, tt), :].astype(F32))
            q_t = uv * sg
            db8[...] += dp[pl.ds(base, tt), :].reshape(tt // SUBLANES, SUBLANES, LANES).sum(axis=0)
            dq = jnp.zeros((tt, LANES), F32)
            for k in range(CONV_WIDTH):
                shifted = dp[pl.ds(base + (CONV_WIDTH - 1) - k, tt), :]
                dw8[k] += (shifted * q_t).reshape(tt // SUBLANES, SUBLANES, LANES).sum(axis=0)
                dq = dq + w_ref[k:k + 1, :] * shifted
            duv_ref[pl.ds(base, tt), :] = (dq * sg).astype(BF)
            dug_ref[pl.ds(base, tt), :] = (dq * uv * sg * (1.0 - sg)).astype(BF)
            return carry

        lax.fori_loop(0, S // tt, step, 0)
        dw_ref[...] = jnp.zeros_like(dw_ref)
        for k in range(CONV_WIDTH):
            dw_ref[k:k + 1, :] = _colsum(dw8[k])
        db_ref[...] = _colsum(db8[...])

    col = lambda o: pl.BlockSpec((S, LANES), lambda c: (0, c + o))
    return _pcall(
        body, name=name, grid=(nb,),
        in_specs=[col(0), col(0), col(nb), pl.BlockSpec((None, CONV_TAPS, LANES), lambda c: (c, 0, 0))],
        out_specs=[col(0), col(0), pl.BlockSpec((None, CONV_TAPS, LANES), lambda c: (c, 0, 0)),
                   pl.BlockSpec((1, LANES), lambda c: (0, c))],
        out_shape=[jax.ShapeDtypeStruct((S, M), BF), jax.ShapeDtypeStruct((S, M), BF),
                   jax.ShapeDtypeStruct((nb, CONV_TAPS, LANES), F32), jax.ShapeDtypeStruct((1, M), F32)],
        scratch_shapes=[pltpu.VMEM((S + CONV_PAD, LANES), F32),
                        pltpu.VMEM((CONV_TAPS, SUBLANES, LANES), F32), pltpu.VMEM((SUBLANES, LANES), F32)],
        compiler_params=_params(("arbitrary",)),
    )(dcq, proj, proj, convw4)


def _log_sigmoid(x):
    return jnp.minimum(x, 0.0) - jnp.log(1.0 + jnp.exp(-jnp.abs(x)))


def _rg_gate_terms(ra, ls):
    la = RG_C * ra * ls
    a = jnp.exp(la)
    th = jnp.tanh(la)
    mult = jnp.sqrt(-2.0 * th / (1.0 - th))
    return a, mult


def rnn_fwd(proj, rnnw4, rnn_b, bda, bdi, b_a, b_i, lam, name):
    S = proj.shape[0]
    M = rnn_b.shape[1]
    nb = M // LANES
    tt = min(SEQ_TT, S)
    KW = RNN_CONV_WIDTH
    nseg = SCAN_SEGMENTS if S % (SCAN_SEGMENTS * tt) == 0 else 1

    def body(ux_ref, w_ref, rb_ref, bda_ref, bdi_ref, ba_ref, bi_ref, lam_ref,
             xr_ref, ra_ref, ii_ref, h_ref, uxp, a_sc, b_sc):
        uxp[0:SUBLANES, :] = jnp.zeros((SUBLANES, LANES), F32)
        ls = _log_sigmoid(lam_ref[...])

        def step(t, carry):
            base = pl.multiple_of(t * tt, tt)
            uxp[pl.ds(base + SUBLANES, tt), :] = ux_ref[pl.ds(base, tt), :].astype(F32)
            xr = jnp.broadcast_to(rb_ref[...], (tt, LANES))
            for k in range(KW):
                xr = xr + w_ref[k:k + 1, :] * uxp[pl.ds(base + (SUBLANES - KW + 1) + k, tt), :]
            xb = xr.astype(BF)
            ra = _sigmoid(jnp.dot(xb, bda_ref[...], preferred_element_type=F32) + ba_ref[...])
            ii = _sigmoid(jnp.dot(xb, bdi_ref[...], preferred_element_type=F32) + bi_ref[...])
            a, mult = _rg_gate_terms(ra, ls)
            xr_ref[pl.ds(base, tt), :] = xr
            ra_ref[pl.ds(base, tt), :] = ra
            ii_ref[pl.ds(base, tt), :] = ii
            a_sc[pl.ds(base, tt), :] = a
            b_sc[pl.ds(base, tt), :] = mult * (ii * xr)
            return carry

        lax.fori_loop(0, S // tt, step, 0)

        rows = lax.broadcasted_iota(I32, (SUBLANES, LANES), 0)
        seg = S // nseg
        last = lambda v: jnp.broadcast_to(v[SUBLANES - 1:SUBLANES, :], (SUBLANES, LANES))

        def scan(t, carry):
            hs, ps = carry
            new_h, new_p = [], []
            for s in range(nseg):
                base = pl.multiple_of(s * seg + t * SUBLANES, SUBLANES)
                A = a_sc[pl.ds(base, SUBLANES), :]
                B = b_sc[pl.ds(base, SUBLANES), :]
                for d in (1, 2, 4):
                    As = jnp.where(rows >= d, pltpu.roll(A, d, axis=0), 1.0)
                    Bs = jnp.where(rows >= d, pltpu.roll(B, d, axis=0), 0.0)
                    B = A * Bs + B
                    A = A * As
                hh = B + A * hs[s]
                h_ref[pl.ds(base, SUBLANES), :] = hh
                pp = A * ps[s]
                if s > 0:
                    a_sc[pl.ds(base, SUBLANES), :] = pp
                new_h.append(last(hh))
                new_p.append(last(pp))
            return tuple(new_h), tuple(new_p)

        zero8 = jnp.zeros((SUBLANES, LANES), F32)
        one8 = jnp.ones((SUBLANES, LANES), F32)
        hs, ps = lax.fori_loop(0, seg // SUBLANES, scan, ((zero8,) * nseg, (one8,) * nseg))
        carry_in = hs[0]
        for s in range(1, nseg):
            c_row = carry_in[0:1, :]

            def fix(t, c, s=s, c_row=c_row):
                base = pl.multiple_of(s * seg + t * tt, tt)
                h_ref[pl.ds(base, tt), :] = h_ref[pl.ds(base, tt), :] + a_sc[pl.ds(base, tt), :] * c_row
                return c

            lax.fori_loop(0, seg // tt, fix, 0)
            carry_in = hs[s] + ps[s] * carry_in

    col = lambda o: pl.BlockSpec((S, LANES), lambda c: (0, c + o))
    vec = pl.BlockSpec((1, LANES), lambda c: (0, c))
    diag = pl.BlockSpec((LANES, LANES), lambda c: (c, c))
    return _pcall(
        body, name=name, grid=(nb,),
        in_specs=[col(2 * nb), pl.BlockSpec((None, SUBLANES, LANES), lambda c: (c, 0, 0)), vec, diag, diag, vec, vec, vec],
        out_specs=[col(0)] * 4,
        out_shape=[jax.ShapeDtypeStruct((S, M), F32)] * 4,
        scratch_shapes=[pltpu.VMEM((S + SUBLANES, LANES), F32), pltpu.VMEM((S, LANES), F32), pltpu.VMEM((S, LANES), F32)],
        compiler_params=_params(("arbitrary",)),
    )(proj, rnnw4, rnn_b, bda, bdi, b_a, b_i, lam)


def rnn_bwd(dhout, h, xr, ra, ii, proj, rnnw4, bda, bdi, lam, name):
    S, M = h.shape
    nb = M // LANES
    tt = min(SEQ_TT, S)
    KW = RNN_CONV_WIDTH
    SL = SUBLANES
    nseg = SCAN_SEGMENTS if S % (SCAN_SEGMENTS * tt) == 0 else 1

    def body(dh_ref, h_ref, xr_ref, ra_ref, ii_ref, ux_ref, w_ref, bda_ref, bdi_ref, lam_ref,
             dux_ref, dwa_ref, dwi_ref, drw_ref, vec_ref,
             a_sc, hp, g_sc, dpa_sc, dpi_sc, dxp, uxp, acc8, drw8, p_sc):
        zero8 = jnp.zeros((SL, LANES), F32)
        a_sc[S:S + SL, :] = zero8
        hp[0:SL, :] = zero8
        dxp[S:S + SL, :] = zero8
        uxp[0:SL, :] = zero8
        acc8[...] = jnp.zeros_like(acc8)
        drw8[...] = jnp.zeros_like(drw8)
        lamv = lam_ref[...]
        ls = _log_sigmoid(lamv)

        def fill(t, carry):
            base = pl.multiple_of(t * tt, tt)
            a_sc[pl.ds(base, tt), :] = jnp.exp(RG_C * ra_ref[pl.ds(base, tt), :] * ls)
            hp[pl.ds(base + SL, tt), :] = h_ref[pl.ds(base, tt), :]
            uxp[pl.ds(base + SL, tt), :] = ux_ref[pl.ds(base, tt), :].astype(F32)
            return carry

        lax.fori_loop(0, S // tt, fill, 0)

        rows = lax.broadcasted_iota(I32, (SL, LANES), 0)
        seg = S // nseg
        nt8 = seg // SL
        first = lambda v: jnp.broadcast_to(v[0:1, :], (SL, LANES))

        def rscan(t, carry):
            gs, ps = carry
            new_g, new_p = [], []
            for s in range(nseg):
                base = pl.multiple_of(s * seg + (nt8 - 1 - t) * SL, SL)
                A = a_sc[pl.ds(base + 1, SL), :]
                B = dh_ref[pl.ds(base, SL), :]
                for d in (1, 2, 4):
                    As = jnp.where(rows < SL - d, pltpu.roll(A, SL - d, axis=0), 1.0)
                    Bs = jnp.where(rows < SL - d, pltpu.roll(B, SL - d, axis=0), 0.0)
                    B = A * Bs + B
                    A = A * As
                g = B + A * gs[s]
                g_sc[pl.ds(base, SL), :] = g
                pp = A * ps[s]
                if s < nseg - 1:
                    p_sc[pl.ds(base, SL), :] = pp
                new_g.append(first(g))
                new_p.append(first(pp))
            return tuple(new_g), tuple(new_p)

        one8 = jnp.ones((SL, LANES), F32)
        gs, ps = lax.fori_loop(0, nt8, rscan, ((zero8,) * nseg, (one8,) * nseg))
        carry_in = gs[nseg - 1]
        for s in range(nseg - 2, -1, -1):
            c_row = carry_in[0:1, :]

            def fix(t, c, s=s, c_row=c_row):
                base = pl.multiple_of(s * seg + t * tt, tt)
                g_sc[pl.ds(base, tt), :] = g_sc[pl.ds(base, tt), :] + p_sc[pl.ds(base, tt), :] * c_row
                return c

            lax.fori_loop(0, seg // tt, fix, 0)
            carry_in = gs[s] + ps[s] * carry_in

        def red8(v):
            return v.reshape(tt // SL, SL, LANES).sum(axis=0)

        def step(t, carry):
            base = pl.multiple_of(t * tt, tt)
            g = g_sc[pl.ds(base, tt), :]
            hprev = hp[pl.ds(base + SL - 1, tt), :]
            xr_t = xr_ref[pl.ds(base, tt), :]
            ra_t = ra_ref[pl.ds(base, tt), :]
            ii_t = ii_ref[pl.ds(base, tt), :]
            a, mult = _rg_gate_terms(ra_t, ls)
            gx = g * xr_t
            dmult = gx * ii_t
            dii = gx * mult
            dxr = g * (mult * ii_t)
            dla = g * hprev * a - dmult * (a * a) / mult
            acc8[3] += red8(dla * ra_t)
            dpa = dla * (RG_C * ls) * ra_t * (1.0 - ra_t)
            dpi = dii * ii_t * (1.0 - ii_t)
            dpab = dpa.astype(BF)
            dpib = dpi.astype(BF)
            dxr = dxr + (lax.dot_general(dpab, bda_ref[...], CONTRACT_LAST, preferred_element_type=F32)
                         + lax.dot_general(dpib, bdi_ref[...], CONTRACT_LAST, preferred_element_type=F32))
            dpa_sc[pl.ds(base, tt), :] = dpab
            dpi_sc[pl.ds(base, tt), :] = dpib
            dxp[pl.ds(base, tt), :] = dxr
            acc8[0] += red8(dxr)
            acc8[1] += red8(dpa)
            acc8[2] += red8(dpi)
            return carry

        lax.fori_loop(0, S // tt, step, 0)

        def convb(t, carry):
            base = pl.multiple_of(t * tt, tt)
            d_t = dxp[pl.ds(base, tt), :]
            dux = jnp.zeros((tt, LANES), F32)
            for k in range(KW):
                drw8[k] += red8(d_t * uxp[pl.ds(base + (SL - KW + 1) + k, tt), :])
                dux = dux + w_ref[k:k + 1, :] * dxp[pl.ds(base + (KW - 1) - k, tt), :]
            dux_ref[pl.ds(base, tt), :] = dux.astype(BF)
            return carry

        lax.fori_loop(0, S // tt, convb, 0)

        xb = xr_ref[...].astype(BF)
        dwa_ref[...] = lax.dot_general(xb, dpa_sc[...], CONTRACT_FIRST, preferred_element_type=F32)
        dwi_ref[...] = lax.dot_general(xb, dpi_sc[...], CONTRACT_FIRST, preferred_element_type=F32)
        drw_ref[...] = jnp.zeros_like(drw_ref)
        vec_ref[...] = jnp.zeros_like(vec_ref)
        for k in range(KW):
            drw_ref[k:k + 1, :] = _colsum(drw8[k])
        for k in range(3):
            vec_ref[k:k + 1, :] = _colsum(acc8[k])
        vec_ref[3:4, :] = _colsum(acc8[3]) * (RG_C * _sigmoid(-lamv))

    col = lambda o: pl.BlockSpec((S, LANES), lambda c: (0, c + o))
    vec = pl.BlockSpec((1, LANES), lambda c: (0, c))
    diag = pl.BlockSpec((LANES, LANES), lambda c: (c, c))
    blk3 = lambda r: pl.BlockSpec((None, r, LANES), lambda c: (c, 0, 0))
    return _pcall(
        body, name=name, grid=(nb,),
        in_specs=[col(0), col(0), col(0), col(0), col(0), col(2 * nb), blk3(SL), diag, diag, vec],
        out_specs=[col(0), blk3(LANES), blk3(LANES), blk3(SL), pl.BlockSpec((SL, LANES), lambda c: (0, c))],
        out_shape=[jax.ShapeDtypeStruct((S, M), BF), jax.ShapeDtypeStruct((nb, LANES, LANES), F32),
                   jax.ShapeDtypeStruct((nb, LANES, LANES), F32), jax.ShapeDtypeStruct((nb, SL, LANES), F32),
                   jax.ShapeDtypeStruct((SL, M), F32)],
        scratch_shapes=[pltpu.VMEM((S + SL, LANES), F32), pltpu.VMEM((S + SL, LANES), F32), pltpu.VMEM((S, LANES), F32),
                        pltpu.VMEM((S, LANES), BF), pltpu.VMEM((S, LANES), BF), pltpu.VMEM((S + SL, LANES), F32),
                        pltpu.VMEM((S + SL, LANES), F32), pltpu.VMEM((SL, SL, LANES), F32), pltpu.VMEM((SL, SL, LANES), F32),
                        pltpu.VMEM((S, LANES), F32)],
        compiler_params=_params(("arbitrary",)),
    )(dhout, h, xr, ra, ii, proj, rnnw4, bda, bdi, lam)


GELU_K = 0.7978845608028654
GELU_C = 0.044715


def _layernorm_parts(cq):
    mu = _rowmean(cq)
    d = cq - mu
    rstd = lax.rsqrt(_rowmean(d * d) + EPS)
    return d * rstd, rstd


def mix_out(cq, proj, h, x, vecs, lnv, wout, name):
    S, D = x.shape
    M = cq.shape[1]
    ts = min(512, S)

    def body(cq_ref, uy_ref, h_ref, x_ref, v_ref, ln_ref, w_ref, xo_ref, ym_ref, yc_ref):
        z, _ = _layernorm_parts(cq_ref[...])
        l = z * _row(ln_ref, 0) + _row(ln_ref, 1)
        yc_ref[:, 0:M] = (l * _sigmoid(l)).astype(BF)
        uy = uy_ref[...].astype(F32)
        gelu = 0.5 * uy * (1.0 + jnp.tanh(GELU_K * (uy + GELU_C * uy * uy * uy)))
        yc_ref[:, M:2 * M] = (gelu * h_ref[...]).astype(BF)
        ym = jnp.dot(yc_ref[...], w_ref[...], preferred_element_type=F32)
        ym_ref[...] = ym.astype(BF)
        xo_ref[...] = x_ref[...] + _row(v_ref, R_GT2) * ym

    tok = pl.BlockSpec((ts, D), lambda i: (i, 0))
    mtok = lambda o: pl.BlockSpec((ts, M), lambda i: (i, o))
    return _pcall(
        body, name=name, grid=(S // ts,),
        in_specs=[mtok(0), mtok(3), mtok(0), tok, pl.BlockSpec(vecs.shape, lambda i: (0, 0)),
                  pl.BlockSpec(lnv.shape, lambda i: (0, 0)), pl.BlockSpec(wout.shape, lambda i: (0, 0))],
        out_specs=[tok, tok, pl.BlockSpec((ts, 2 * M), lambda i: (i, 0))],
        out_shape=[jax.ShapeDtypeStruct((S, D), F32), jax.ShapeDtypeStruct((S, D), BF),
                   jax.ShapeDtypeStruct((S, 2 * M), BF)],
        compiler_params=_params(("arbitrary",)),
    )(cq, proj, h, x, vecs, lnv, wout)


def mix_out_bwd(dxo, ym, vecs, wout, cq, lnv, proj, h, name):
    S, D = dxo.shape
    M = cq.shape[1]
    ts = min(512, S)

    def body(dxo_ref, ym_ref, v_ref, w_ref, cq_ref, ln_ref, uy_ref, h_ref,
             dcq_ref, dh_ref, duy_ref, dyb_ref, vgd_ref, vgm_ref):
        @pl.when(pl.program_id(0) == 0)
        def _():
            vgd_ref[...] = jnp.zeros_like(vgd_ref)
            vgm_ref[...] = jnp.zeros_like(vgm_ref)

        dxo_v = dxo_ref[...]
        dyb = (_row(v_ref, R_GT2) * dxo_v).astype(BF)
        dyb_ref[...] = dyb
        vgd_ref[0:1, :] += _colsum(dxo_v * ym_ref[...].astype(F32))
        dycat = lax.dot_general(dyb, w_ref[...], CONTRACT_LAST, preferred_element_type=F32)
        dyc = dycat[:, 0:M]
        dyr = dycat[:, M:2 * M]
        z, rstd = _layernorm_parts(cq_ref[...])
        lng = _row(ln_ref, 0)
        l = z * lng + _row(ln_ref, 1)
        sl = _sigmoid(l)
        dl = dyc * (sl * (1.0 + l * (1.0 - sl)))
        vgm_ref[0:1, :] += _colsum(dl * z)
        vgm_ref[1:2, :] += _colsum(dl)
        dz = dl * lng
        dcq_ref[...] = (rstd * (dz - _rowmean(dz) - z * _rowmean(dz * z))).astype(BF)
        uy = uy_ref[...].astype(F32)
        u2 = uy * uy
        th = jnp.tanh(GELU_K * (uy + GELU_C * uy * u2))
        gelu = 0.5 * uy * (1.0 + th)
        dgelu = 0.5 * (1.0 + th) + 0.5 * uy * (1.0 - th * th) * (GELU_K * (1.0 + 3.0 * GELU_C * u2))
        dh_ref[...] = dyr * gelu
        duy_ref[...] = (dyr * h_ref[...] * dgelu).astype(BF)

    tok = pl.BlockSpec((ts, D), lambda i: (i, 0))
    mtok = lambda o: pl.BlockSpec((ts, M), lambda i: (i, o))
    return _pcall(
        body, name=name, grid=(S // ts,),
        in_specs=[tok, tok, pl.BlockSpec(vecs.shape, lambda i: (0, 0)), pl.BlockSpec(wout.shape, lambda i: (0, 0)),
                  mtok(0), pl.BlockSpec(lnv.shape, lambda i: (0, 0)), mtok(3), mtok(0)],
        out_specs=[mtok(0), mtok(0), mtok(0), tok, pl.BlockSpec((SUBLANES, D), lambda i: (0, 0)),
                   pl.BlockSpec((SUBLANES, M), lambda i: (0, 0))],
        out_shape=[jax.ShapeDtypeStruct((S, M), BF), jax.ShapeDtypeStruct((S, M), F32), jax.ShapeDtypeStruct((S, M), BF),
                   jax.ShapeDtypeStruct((S, D), BF),
                   jax.ShapeDtypeStruct((SUBLANES, D), F32), jax.ShapeDtypeStruct((SUBLANES, M), F32)],
        compiler_params=_params(("arbitrary",)),
    )(dxo, ym, vecs, wout, cq, lnv, proj, h)


def mix_in_bwd(dparts, x, dxo, vecs, win, name):
    S, D = x.shape
    M = dparts[0].shape[1]
    ts = min(512, S)

    def body(d0, d1, d2, d3, x_ref, dxo_ref, v_ref, w_ref, dx_ref, hb_ref, dp_ref, vg_ref):
        @pl.when(pl.program_id(0) == 0)
        def _():
            vg_ref[...] = jnp.zeros_like(vg_ref)

        for q, dref in enumerate((d0, d1, d2, d3)):
            dp_ref[:, q * M:(q + 1) * M] = dref[...].astype(BF)
        dh = lax.dot_general(dp_ref[...], w_ref[...], CONTRACT_LAST, preferred_element_type=F32)
        xv = x_ref[...]
        r = lax.rsqrt(_rowmean(xv * xv) + EPS)
        n = xv * r
        g = _row(v_ref, R_G2)
        sc1 = 1.0 + _row(v_ref, R_SC2)
        gsc = g * sc1
        hb_ref[...] = (n * gsc + _row(v_ref, R_SH2)).astype(BF)
        dhn = dh * n
        vg_ref[0:1, :] += _colsum(dh)
        vg_ref[1:2, :] += _colsum(dhn) * g
        vg_ref[2:3, :] += _colsum(dhn) * sc1
        dn = dh * gsc
        dx_ref[...] = dxo_ref[...] + r * (dn - n * _rowmean(dn * n))

    tok = pl.BlockSpec((ts, D), lambda i: (i, 0))
    mtok = pl.BlockSpec((ts, M), lambda i: (i, 0))
    return _pcall(
        body, name=name, grid=(S // ts,),
        in_specs=[mtok] * 4 + [tok, tok, pl.BlockSpec(vecs.shape, lambda i: (0, 0)), pl.BlockSpec(win.shape, lambda i: (0, 0))],
        out_specs=[tok, tok, pl.BlockSpec((ts, 4 * M), lambda i: (i, 0)), pl.BlockSpec((SUBLANES, D), lambda i: (0, 0))],
        out_shape=[jax.ShapeDtypeStruct((S, D), F32), jax.ShapeDtypeStruct((S, D), BF),
                   jax.ShapeDtypeStruct((S, 4 * M), BF), jax.ShapeDtypeStruct((SUBLANES, D), F32)],
        compiler_params=_params(("arbitrary",)),
    )(*dparts, x, dxo, vecs, win)


def _adamw(w, g, m, v):
    m = ADAM_B1 * m + (1.0 - ADAM_B1) * g
    v = ADAM_B2 * v + (1.0 - ADAM_B2) * (g * g)
    m_hat = m / (1.0 - ADAM_B1 ** ADAM_STEP)
    v_hat = v / (1.0 - ADAM_B2 ** ADAM_STEP)
    delta = -ADAM_LR * (m_hat / (jnp.sqrt(v_hat) + ADAM_EPS) + ADAM_WD * w)
    return delta, m, v


def adam_big(w, g_mine, g_sib, m, v, cidx, name):
    R, C = w.shape
    hr = R // 2
    tr = 256 if hr % 256 == 0 else hr
    tc = C if C <= 1536 else (1152 if C % 1152 == 0 else 1024)
    assert hr % tr == 0 and C % tc == 0 and g_mine.shape == (hr, C)
    nrb = hr // tr

    def body(ci_ref, w_ref, gm_ref, gs_ref, m_ref, v_ref, g_ref, d_ref, nm_ref, nv_ref):
        mine = (pl.program_id(0) // nrb) == ci_ref[0]
        g = jnp.where(mine, gm_ref[...], gs_ref[...])
        d, nm, nv = _adamw(w_ref[...], g, m_ref[...], v_ref[...])
        g_ref[...] = g
        d_ref[...] = d
        nm_ref[...] = nm
        nv_ref[...] = nv

    blk = pl.BlockSpec((tr, tc), lambda i, j, ci: (i, j))
    mine_spec = pl.BlockSpec((tr, tc), lambda i, j, ci: (jnp.where(i // nrb == ci[0], i % nrb, 0), j))
    sib_spec = pl.BlockSpec((tr, tc), lambda i, j, ci: (jnp.where(i // nrb == ci[0], 0, i % nrb), j))
    gs = pltpu.PrefetchScalarGridSpec(num_scalar_prefetch=1, grid=(R // tr, C // tc),
                                      in_specs=[blk, mine_spec, sib_spec, blk, blk], out_specs=[blk] * 4)
    return _pcall(body, name=name, grid_spec=gs, out_shape=[jax.ShapeDtypeStruct((R, C), F32)] * 4,
                  compiler_params=_params(("parallel", "parallel")))(cidx, w, g_mine, g_sib, m, v)


def adam_cond(c_all, dmod, w, m, v, name):
    B, Kin = c_all.shape
    N = w.shape[1]
    tn = 768 if N % 768 == 0 else 256
    assert N % tn == 0

    def body(c_ref, d_ref, w_ref, m_ref, v_ref, g_ref, dl_ref, nm_ref, nv_ref):
        cv = c_ref[...]
        ca = cv * _sigmoid(cv)
        g = lax.dot_general(ca.astype(BF), d_ref[...].astype(BF), CONTRACT_FIRST, preferred_element_type=F32)
        d, nm, nv = _adamw(w_ref[...], g, m_ref[...], v_ref[...])
        g_ref[...] = g
        dl_ref[...] = d
        nm_ref[...] = nm
        nv_ref[...] = nv

    blk = pl.BlockSpec((Kin, tn), lambda n: (0, n))
    return _pcall(
        body, name=name, grid=(N // tn,),
        in_specs=[pl.BlockSpec((B, Kin), lambda n: (0, 0)), pl.BlockSpec((B, tn), lambda n: (0, n)), blk, blk, blk],
        out_specs=[blk] * 4, out_shape=[jax.ShapeDtypeStruct((Kin, N), F32)] * 4,
        compiler_params=_params(("parallel",)),
    )(c_all, dmod, w, m, v)


def adam_small(ws, gs, ms, vs, name):
    n = len(ws)

    def body(*refs):
        ins, outs = refs[:4 * n], refs[4 * n:]
        for k in range(n):
            d, nm, nv = _adamw(ins[k][...], ins[n + k][...], ins[2 * n + k][...], ins[3 * n + k][...])
            outs[k][...] = d
            outs[n + k][...] = nm
            outs[2 * n + k][...] = nv

    specs = [pl.BlockSpec(w.shape, lambda i: (0, 0)) for w in ws]
    shapes = [jax.ShapeDtypeStruct(w.shape, F32) for w in ws]
    out = _pcall(body, name=name, grid=(1,), in_specs=specs * 4, out_specs=specs * 3, out_shape=shapes * 3,
                 compiler_params=_params(("arbitrary",)))(*ws, *gs, *ms, *vs)
    return out[:n], out[n:2 * n], out[2 * n:]


def _me():
    return lax.axis_index("x"), lax.axis_index("y"), lax.axis_index("c")


def _flip(x, y, p):
    return (x ^ (p >> 1) if (p >> 1) else x), (y ^ (p & 1) if (p & 1) else y)


def _handshake(peers):
    barrier = pltpu.get_barrier_semaphore()
    for peer in peers:
        pl.semaphore_signal(barrier, inc=1, device_id=peer, device_id_type=MESH)
    pl.semaphore_wait(barrier, len(peers))


def _seq_call(body, *, name, n_in, out_shape, sem_shapes, collective_id):
    del n_in
    return pl.kernel(body, out_type=out_shape, mesh=plsc.ScalarSubcoreMesh(axis_name="sq", num_cores=1), name=name,
                     scratch_types=sem_shapes, compiler_params=pltpu.CompilerParams(collective_id=collective_id))


def _hbm_comm_call(body, *, name, n_in, out_shape, sem_shapes, seq_id):
    if seq_id is not None:
        return _seq_call(body, name=name, n_in=n_in, out_shape=out_shape, sem_shapes=sem_shapes, collective_id=seq_id)
    anyspec = pl.BlockSpec(memory_space=pl.ANY)
    return _pcall(body, name=name, in_specs=[anyspec] * n_in, out_specs=[anyspec] * len(out_shape), out_shape=out_shape,
                  scratch_shapes=sem_shapes, compiler_params=_params())


def allgather_devices(v, name, with_sum=False):
    R, L = v.shape

    def body(v_ref, out_ref, *rest):
        if with_sum:
            sum_ref, send_sems, recv_sems = rest
        else:
            send_sems, recv_sems = rest
        x, y, c = _me()
        me = 4 * x + 2 * y + c
        out_ref[me] = v_ref[...]
        copies = []
        for p in range(1, N_DEV):
            px, py = _flip(x, y, p >> 1)
            pc = (1 - c) if (p & 1) else c
            peer = 4 * px + 2 * py + pc
            send = pltpu.make_async_remote_copy(src_ref=v_ref, dst_ref=out_ref.at[me], send_sem=send_sems.at[p - 1],
                                                recv_sem=recv_sems.at[p - 1], device_id=(px, py, pc), device_id_type=MESH)
            send.start()
            recv = pltpu.make_async_remote_copy(src_ref=v_ref, dst_ref=out_ref.at[peer], send_sem=send_sems.at[p - 1],
                                                recv_sem=recv_sems.at[p - 1], device_id=(px, py, pc), device_id_type=MESH)
            copies.append((send, recv))
        for send, recv in copies:
            recv.wait_recv()
        for send, recv in copies:
            send.wait_send()
        if with_sum:
            s = out_ref[0]
            for k in range(1, N_DEV):
                s = s + out_ref[k]
            sum_ref[...] = s

    vm = pl.BlockSpec(memory_space=pltpu.VMEM)
    out_shape = [jax.ShapeDtypeStruct((N_DEV, R, L), F32)]
    if with_sum:
        out_shape.append(jax.ShapeDtypeStruct((R, L), F32))
    return _pcall(
        body, name=name, in_specs=[vm], out_specs=[vm] * len(out_shape), out_shape=out_shape,
        scratch_shapes=[pltpu.SemaphoreType.DMA((N_DEV - 1,)), pltpu.SemaphoreType.DMA((N_DEV - 1,))],
        compiler_params=_params(),
    )(v)


def allgather_devices_hbm(v, name, seq_id):
    R, L = v.shape

    def body(v_ref, out_ref, send_sems, recv_sems, local_sem):
        x, y, c = _me()
        me = 4 * x + 2 * y + c
        peers = []
        for p in range(1, N_DEV):
            px, py = _flip(x, y, p >> 1)
            peers.append((px, py, (1 - c) if (p & 1) else c))
        _handshake(peers)
        lc = pltpu.make_async_copy(v_ref, out_ref.at[me], local_sem)
        lc.start()
        copies = []
        for p, (px, py, pc) in enumerate(peers):
            send = pltpu.make_async_remote_copy(src_ref=v_ref, dst_ref=out_ref.at[me], send_sem=send_sems.at[p],
                                                recv_sem=recv_sems.at[p], device_id=(px, py, pc), device_id_type=MESH)
            send.start()
            recv = pltpu.make_async_remote_copy(src_ref=v_ref, dst_ref=out_ref.at[4 * px + 2 * py + pc], send_sem=send_sems.at[p],
                                                recv_sem=recv_sems.at[p], device_id=(px, py, pc), device_id_type=MESH)
            copies.append((send, recv))
        for send, recv in copies:
            recv.wait_recv()
        for send, recv in copies:
            send.wait_send()
        lc.wait()

    return _seq_call(body, name=name, n_in=1, out_shape=[jax.ShapeDtypeStruct((N_DEV, R, L), F32)],
                     sem_shapes=[pltpu.SemaphoreType.DMA((N_DEV - 1,)), pltpu.SemaphoreType.DMA((N_DEV - 1,)),
                                 pltpu.SemaphoreType.DMA], collective_id=seq_id)(v)[0]


def sum_slots(g, name):
    n, R, L = g.shape
    tr = 216 if R % 216 == 0 else R
    assert R % tr == 0 and tr % SUBLANES == 0

    def body(g_ref, o_ref):
        s = g_ref[0]
        for k in range(1, n):
            s = s + g_ref[k]
        o_ref[...] = s

    return _pcall(body, name=name, grid=(R // tr,), in_specs=[pl.BlockSpec((n, tr, L), lambda i: (0, i, 0))],
                  out_specs=pl.BlockSpec((tr, L), lambda i: (i, 0)), out_shape=jax.ShapeDtypeStruct((R, L), F32),
                  compiler_params=_params(("parallel",)))(g)


def allgather_chips(v, name):
    R, L = v.shape

    def body(v_ref, out_ref, send_sems, recv_sems):
        x, y, c = _me()
        chip = 2 * x + y
        out_ref[chip] = v_ref[...]
        copies = []
        for p in range(1, N_CHIPS):
            px, py = _flip(x, y, p)
            send = pltpu.make_async_remote_copy(src_ref=v_ref, dst_ref=out_ref.at[chip], send_sem=send_sems.at[p - 1],
                                                recv_sem=recv_sems.at[p - 1], device_id=(px, py, c), device_id_type=MESH)
            send.start()
            recv = pltpu.make_async_remote_copy(src_ref=v_ref, dst_ref=out_ref.at[2 * px + py], send_sem=send_sems.at[p - 1],
                                                recv_sem=recv_sems.at[p - 1], device_id=(px, py, c), device_id_type=MESH)
            copies.append((send, recv))
        for send, recv in copies:
            recv.wait_recv()
        for send, recv in copies:
            send.wait_send()

    vm = pl.BlockSpec(memory_space=pltpu.VMEM)
    return _pcall(
        body, name=name, in_specs=[vm], out_specs=vm, out_shape=jax.ShapeDtypeStruct((N_CHIPS, R, L), F32),
        scratch_shapes=[pltpu.SemaphoreType.DMA((N_CHIPS - 1,)), pltpu.SemaphoreType.DMA((N_CHIPS - 1,))],
        compiler_params=_params(),
    )(v)


def _shard_window(ref, kind, shard_shape, chip, half):
    r, c = shard_shape
    hr = r // 2
    if kind == "col":
        return ref.at[pl.ds(pl.multiple_of(half * hr, hr), hr), pl.ds(pl.multiple_of(chip * c, c), c)]
    return ref.at[pl.ds(pl.multiple_of(chip * r + half * hr, hr), hr), :]


def allgather_weights(shards, kinds, name, seq_id=None):
    n = len(shards)
    fulls = []
    for s, kind in zip(shards, kinds):
        r, c = s.shape
        fulls.append(jax.ShapeDtypeStruct((r, N_CHIPS * c) if kind == "col" else (N_CHIPS * r, c), s.dtype))

    def body(*refs):
        srcs, outs = refs[:n], refs[n:2 * n]
        send_sems, recv_sems, fsend_sems, frecv_sems = refs[2 * n:]
        x, y, c = _me()
        chip = 2 * x + y
        sib = (x, y, 1 - c)
        if seq_id is not None:
            _handshake([(*_flip(x, y, p), c) for p in range(1, N_CHIPS)] + [sib])
        sends, fwds = [], []
        for i in range(n):
            shp = srcs[i].shape
            hr = shp[0] // 2
            my_half = srcs[i].at[pl.ds(pl.multiple_of(c * hr, hr), hr), :]
            for p in range(1, N_CHIPS):
                px, py = _flip(x, y, p)
                k = i * (N_CHIPS - 1) + p - 1
                cp = pltpu.make_async_remote_copy(src_ref=my_half, dst_ref=_shard_window(outs[i], kinds[i], shp, chip, c),
                                                  send_sem=send_sems.at[k], recv_sem=recv_sems.at[k],
                                                  device_id=(px, py, c), device_id_type=MESH)
                cp.start()
                sends.append(cp)
        for i in range(n):
            shp = srcs[i].shape
            for p in range(1, N_CHIPS):
                px, py = _flip(x, y, p)
                k = i * (N_CHIPS - 1) + p - 1
                landed = _shard_window(outs[i], kinds[i], shp, 2 * px + py, c)
                pltpu.make_async_remote_copy(src_ref=landed, dst_ref=landed, send_sem=send_sems.at[k], recv_sem=recv_sems.at[k],
                                             device_id=(px, py, c), device_id_type=MESH).wait_recv()
                fw = pltpu.make_async_remote_copy(src_ref=landed, dst_ref=landed, send_sem=fsend_sems.at[k],
                                                  recv_sem=frecv_sems.at[k], device_id=sib, device_id_type=MESH)
                fw.start()
                fwds.append(fw)
        for i in range(n):
            shp = srcs[i].shape
            for p in range(1, N_CHIPS):
                px, py = _flip(x, y, p)
                k = i * (N_CHIPS - 1) + p - 1
                other = _shard_window(outs[i], kinds[i], shp, 2 * px + py, 1 - c)
                pltpu.make_async_remote_copy(src_ref=other, dst_ref=other, send_sem=fsend_sems.at[k], recv_sem=frecv_sems.at[k],
                                             device_id=sib, device_id_type=MESH).wait_recv()
        for cp in sends + fwds:
            cp.wait_send()

    nk = n * (N_CHIPS - 1)
    gathered = _hbm_comm_call(
        body, name=name, n_in=n, out_shape=fulls, seq_id=seq_id,
        sem_shapes=[pltpu.SemaphoreType.DMA((nk,)), pltpu.SemaphoreType.DMA((nk,)), pltpu.SemaphoreType.DMA((nk,)),
                    pltpu.SemaphoreType.DMA((nk,))],
    )(*shards)
    return gathered


def place_local_shards(fulls, shards, kinds, name):
    n = len(shards)
    chip = jnp.reshape(2 * lax.axis_index("x") + lax.axis_index("y"), (1,)).astype(I32)

    def body(ci_ref, *refs):
        for i in range(n):
            refs[2 * n + i][...] = refs[i][...]

    in_specs = [pl.BlockSpec(s.shape, lambda i, ci: (0, 0)) for s in shards] + [pl.BlockSpec(memory_space=pl.ANY)] * n
    out_specs = [pl.BlockSpec(s.shape, (lambda i, ci: (0, ci[0])) if k == "col" else (lambda i, ci: (ci[0], 0)))
                 for s, k in zip(shards, kinds)]
    gs = pltpu.PrefetchScalarGridSpec(num_scalar_prefetch=1, grid=(1,), in_specs=in_specs, out_specs=out_specs)
    return _pcall(body, name=name, grid_spec=gs, out_shape=[jax.ShapeDtypeStruct(f.shape, f.dtype) for f in fulls],
                  input_output_aliases={1 + n + i: i for i in range(n)}, compiler_params=_params(("arbitrary",)))(chip, *shards, *fulls)


def _as_halves(g, kind, shard_shape):
    r, c = shard_shape
    if kind == "col":
        return g.reshape(2, r // 2, N_CHIPS * c)
    return g.reshape(N_CHIPS, 2, r // 2, c)


def exchange_sibling_halves(grads, kinds, shard_shapes, name, seq_id=None):
    n = len(grads)
    views = [_as_halves(g, k, s) for g, k, s in zip(grads, kinds, shard_shapes)]
    outs = []
    for k, (r, c) in zip(kinds, shard_shapes):
        outs.append(jax.ShapeDtypeStruct((r // 2, N_CHIPS * c) if k == "col" else (N_CHIPS, r // 2, c), grads[0].dtype))

    def body(*refs):
        srcs, dsts = refs[:n], refs[n:2 * n]
        send_sems, recv_sems = refs[2 * n:]
        x, y, c = _me()
        if seq_id is not None:
            _handshake([(x, y, 1 - c)])
        cps = []
        for i in range(n):
            src = srcs[i].at[1 - c] if kinds[i] == "col" else srcs[i].at[:, 1 - c]
            cp = pltpu.make_async_remote_copy(src_ref=src, dst_ref=dsts[i], send_sem=send_sems.at[i], recv_sem=recv_sems.at[i],
                                              device_id=(x, y, 1 - c), device_id_type=MESH)
            cp.start()
            cps.append(cp)
        for cp in cps:
            cp.wait_recv()
        for cp in cps:
            cp.wait_send()

    return _hbm_comm_call(body, name=name, n_in=n, out_shape=outs, seq_id=seq_id,
                          sem_shapes=[pltpu.SemaphoreType.DMA((n,)), pltpu.SemaphoreType.DMA((n,))])(*views)


def add_sibling_half(g, recv, kind, shard_shape, core_chip, name):
    r, c = shard_shape
    hr = r // 2
    gv = _as_halves(g, kind, shard_shape)
    tr = hr if hr <= 512 else (256 if hr % 256 == 0 else hr // 2)
    assert hr % tr == 0

    def body(ci_ref, g_ref, r_ref, h_ref, hb_ref):
        s = g_ref[...].astype(F32) + r_ref[...].astype(F32)
        hb_ref[...] = s.astype(BF)

        @pl.when(pl.program_id(1) == ci_ref[1])
        def _():
            h_ref[...] = s

    grid = (hr // tr, N_CHIPS)
    if kind == "col":
        g_spec = pl.BlockSpec((None, tr, c), lambda i, k, ci: (ci[0], i, k))
        o_spec = pl.BlockSpec((tr, c), lambda i, k, ci: (i, k))
    else:
        g_spec = pl.BlockSpec((None, None, tr, c), lambda i, k, ci: (k, ci[0], i, 0))
        o_spec = pl.BlockSpec((None, tr, c), lambda i, k, ci: (k, i, 0))
    own_spec = pl.BlockSpec((tr, c), lambda i, k, ci: (i, 0))
    gs = pltpu.PrefetchScalarGridSpec(num_scalar_prefetch=1, grid=grid, in_specs=[g_spec, o_spec], out_specs=[own_spec, o_spec])
    return _pcall(
        body, name=name, grid_spec=gs,
        out_shape=[jax.ShapeDtypeStruct((hr, c), F32), jax.ShapeDtypeStruct(recv.shape, BF)],
        compiler_params=_params(("parallel", "arbitrary")),
    )(core_chip, gv, recv)


def exchange_chip_pieces(hbs, kinds, shard_shapes, name, seq_id=None):
    n = len(hbs)
    outs = [jax.ShapeDtypeStruct((N_CHIPS - 1, r // 2, c), BF) for (r, c) in shard_shapes]

    def body(*refs):
        srcs, dsts = refs[:n], refs[n:2 * n]
        send_sems, recv_sems = refs[2 * n:]
        x, y, c = _me()
        if seq_id is not None:
            _handshake([(*_flip(x, y, p), c) for p in range(1, N_CHIPS)])
        cps = []
        for i in range(n):
            cc = shard_shapes[i][1]
            for p in range(1, N_CHIPS):
                px, py = _flip(x, y, p)
                pchip = 2 * px + py
                src = (srcs[i].at[:, pl.ds(pl.multiple_of(pchip * cc, cc), cc)] if kinds[i] == "col" else srcs[i].at[pchip])
                k = i * (N_CHIPS - 1) + p - 1
                cp = pltpu.make_async_remote_copy(src_ref=src, dst_ref=dsts[i].at[p - 1], send_sem=send_sems.at[k],
                                                  recv_sem=recv_sems.at[k], device_id=(px, py, c), device_id_type=MESH)
                cp.start()
                cps.append(cp)
        for cp in cps:
            cp.wait_recv()
        for cp in cps:
            cp.wait_send()

    nk = n * (N_CHIPS - 1)
    return _hbm_comm_call(body, name=name, n_in=n, out_shape=outs, seq_id=seq_id,
                          sem_shapes=[pltpu.SemaphoreType.DMA((nk,)), pltpu.SemaphoreType.DMA((nk,))])(*hbs)


def sum_chip_pieces(h_own, pieces, name):
    hr, c = h_own.shape
    tr = hr if hr <= 512 else (256 if hr % 256 == 0 else hr // 2)
    assert hr % tr == 0

    def body(h_ref, p_ref, q_ref):
        q_ref[...] = ((h_ref[...] + p_ref[0].astype(F32)) + p_ref[1].astype(F32)) + p_ref[2].astype(F32)

    blk = pl.BlockSpec((tr, c), lambda i: (i, 0))
    return _pcall(body, name=name, grid=(hr // tr,), in_specs=[blk, pl.BlockSpec((N_CHIPS - 1, tr, c), lambda i: (0, i, 0))],
                  out_specs=blk, out_shape=jax.ShapeDtypeStruct((hr, c), F32), compiler_params=_params(("parallel",)))(h_own, pieces)


def exchange_reduced_halves(qs, name, seq_id):
    n = len(qs)

    def body(*refs):
        srcs, dsts = refs[:n], refs[n:2 * n]
        send_sems, recv_sems = refs[2 * n:]
        x, y, c = _me()
        _handshake([(x, y, 1 - c)])
        cps = []
        for i in range(n):
            cp = pltpu.make_async_remote_copy(src_ref=srcs[i], dst_ref=dsts[i], send_sem=send_sems.at[i], recv_sem=recv_sems.at[i],
                                              device_id=(x, y, 1 - c), device_id_type=MESH)
            cp.start()
            cps.append(cp)
        for cp in cps:
            cp.wait_recv()
        for cp in cps:
            cp.wait_send()

    return _seq_call(body, name=name, n_in=n, out_shape=[jax.ShapeDtypeStruct(q.shape, F32) for q in qs],
                     sem_shapes=[pltpu.SemaphoreType.DMA((n,)), pltpu.SemaphoreType.DMA((n,))], collective_id=seq_id)(*qs)


def _rows128(a):
    return a.reshape(-1, LANES)


def _after(xs, *deps):
    flat = []
    for d in deps:
        flat.extend(d if isinstance(d, (list, tuple)) else [d])
    return list(lax.optimization_barrier((tuple(xs), tuple(flat)))[0])


def _block_diag(w):
    H, d, _ = w.shape
    eye = jnp.eye(H, dtype=w.dtype)
    return jnp.einsum("hde,hg->hdge", w, eye).reshape(H * d, H * d)


def _diag_blocks(g4, H, d):
    nb = g4.shape[0]
    per = LANES // d
    g = g4.reshape(nb, per, d, per, d)
    return jnp.stack([g[:, j, :, j, :] for j in range(per)], axis=1).reshape(H, d, d)


def kernel(x, c, w_mod, b_mod, g_ffn1, w_ffn1_in, w_ffn1_out, g_mix, w_in, conv_w, conv_b, ln_g, ln_b, rnn_conv_w, rnn_conv_b, w_a, b_a, w_i, b_i, lru_lambda, w_out, g_ffn2, w_ffn2_in, w_ffn2_out, w_fmod, b_fmod, g_final, loss_target, m_w_mod, m_b_mod, m_g_ffn1, m_w_ffn1_in, m_w_ffn1_out, m_g_mix, m_w_in, m_conv_w, m_conv_b, m_ln_g, m_ln_b, m_rnn_conv_w, m_rnn_conv_b, m_w_a, m_b_a, m_w_i, m_b_i, m_lru_lambda, m_w_out, m_g_ffn2, m_w_ffn2_in, m_w_ffn2_out, m_w_fmod, m_b_fmod, m_g_final, v_w_mod, v_b_mod, v_g_ffn1, v_w_ffn1_in, v_w_ffn1_out, v_g_mix, v_w_in, v_conv_w, v_conv_b, v_ln_g, v_ln_b, v_rnn_conv_w, v_rnn_conv_b, v_w_a, v_b_a, v_w_i, v_b_i, v_lru_lambda, v_w_out, v_g_ffn2, v_w_ffn2_in, v_w_ffn2_out, v_w_fmod, v_b_fmod, v_g_final):
    S, D = x.shape[1], x.shape[2]
    M = conv_b.shape[1]
    H, HD = w_a.shape[1], w_a.shape[2]
    nb = M // LANES
    ix, iy, ic = lax.axis_index("x"), lax.axis_index("y"), lax.axis_index("c")
    chip = 2 * ix + iy
    dev = 2 * chip + ic
    core_chip = jnp.stack([ic, chip]).astype(I32)
    cidx = core_chip
    xs = x[0]
    tgt = loss_target[0]

    kinds = ["col", "row"]
    w_f1, w_mx, w_f2 = [w_ffn1_in[0], w_ffn1_out[0]], [w_in[0], w_out[0]], [w_ffn2_in[0], w_ffn2_out[0]]
    as_bf = lambda ws: [w.astype(BF) for w in ws]
    shapes_of = lambda ws: [w.shape for w in ws]
    b_f1, b_mx, b_f2 = as_bf(w_f1), as_bf(w_mx), as_bf(w_f2)
    got_f1i = allgather_weights(b_f1[:1], kinds[:1], "gather_ffn1_in", seq_id=9)
    got_f1o = allgather_weights(b_f1[1:], kinds[1:], "gather_ffn1_out", seq_id=13)
    got_mx = allgather_weights(b_mx, kinds, "gather_mix", seq_id=1)
    got_f2 = allgather_weights(b_f2, kinds, "gather_ffn2", seq_id=2)

    c_all =allgather_devices(_rows128(c), "gather_c")[0].reshape(N_DEV, D)
    mod_cols = cond_matmul(c_all, w_mod[0], "mod_proj")
    fmod_cols = cond_matmul(c_all, w_fmod, "fmod_proj")
    convw_pad = jnp.pad(conv_w[0], ((0, CONV_TAPS - CONV_WIDTH), (0, 0)))
    rnnw_pad = jnp.pad(rnn_conv_w[0], ((0, SUBLANES - RNN_CONV_WIDTH), (0, 0)))
    n_mod, n_fmod = mod_cols.shape[1], fmod_cols.shape[1]
    small = jnp.concatenate([_rows128(mod_cols), _rows128(fmod_cols), convw_pad, rnnw_pad], axis=0)
    small4 = allgather_chips(small, "gather_cond")
    r0 = N_DEV * n_mod // LANES
    r1 = r0 + N_DEV * n_fmod // LANES
    mod_all = small4[:, :r0].reshape(N_CHIPS, N_DEV, n_mod)
    fmod_all = small4[:, r0:r1].reshape(N_CHIPS, N_DEV, n_fmod)
    convw4 = small4[:, r1:r1 + CONV_TAPS]
    rnnw4 = small4[:, r1 + CONV_TAPS:r1 + CONV_TAPS + SUBLANES]
    mod_row = lax.dynamic_index_in_dim(mod_all, dev, axis=1, keepdims=False).reshape(1, N_CHIPS * n_mod) + b_mod
    fmod_row = lax.dynamic_index_in_dim(fmod_all, dev, axis=1, keepdims=False).reshape(1, N_CHIPS * n_fmod) + b_fmod[None, :]
    vecs = jnp.concatenate([mod_row.reshape(9, D), fmod_row.reshape(2, D), g_ffn1, g_mix, g_ffn2, g_final[None, :],
                            jnp.zeros((1, D), F32)], axis=0)
    lnv = jnp.concatenate([ln_g, ln_b, jnp.zeros((SUBLANES - 2, M), F32)], axis=0)
    bda = _block_diag(w_a[0]).astype(BF)
    bdi = _block_diag(w_i[0]).astype(BF)

    def reduce_add(gs, recv, ws, tag, kinds_=kinds):
        pairs = [add_sibling_half(g, r_, k, w.shape, core_chip, f"add_sibling_{tag}{j}")
                 for j, (g, r_, k, w) in enumerate(zip(gs, recv, kinds_, ws))]
        return [p[0] for p in pairs], [p[1] for p in pairs]

    def reduce_sum(hs_, recv, ws, tag, kinds_=kinds):
        return [sum_chip_pieces(h_, p_, f"sum_chips_{tag}{j}") for j, (h_, p_) in enumerate(zip(hs_, recv))]

    rows1 = (R_SH1, R_SC1, R_GT1, R_G1)
    rows3 = (R_SH3, R_SC3, R_GT3, R_G3)
    (wi1,) = place_local_shards(got_f1i, b_f1[:1], kinds[:1], "place_ffn1_in")
    g1s, u1s, a1s = ffn_fwd_in(xs, vecs, wi1, rows1, "ffn1_fwd_in")
    (wo1,) = place_local_shards(_after(got_f1o, a1s), b_f1[1:], kinds[1:], "place_ffn1_out")
    x1, y1 = ffn_fwd_out(a1s, xs, vecs, wo1, rows1, "ffn1_fwd_out")
    win, wout = place_local_shards(_after(got_mx, x1), b_mx, kinds, "place_mix")
    proj = norm_matmul(x1, vecs, win, (R_SH2, R_SC2, R_G2), "mix_in_proj")
    cq = conv_fwd(proj, convw4, conv_b, "conv_fwd")
    xr, ra, ii, hh = rnn_fwd(proj, rnnw4, rnn_conv_b, bda, bdi, b_a, b_i, lru_lambda, "rnn_fwd")
    x2, ym, ycat = mix_out(cq, proj, hh, x1, vecs, lnv, wout, "mix_out")
    wi2, wo2 = place_local_shards(_after(got_f2, x2), b_f2, kinds, "place_ffn2")
    dx3, g2s, u2s, y2, vgf = ffn_fwd(x2, vecs, wi2, wo2, rows3, "ffn2_fwd", final_tgt=tgt)

    Fd = wo1.shape[0]
    tk = S
    dx2, act2, dg2, du2, h3b, dy2b, vg3 = ffn_bwd(dx3, x2, vecs, g2s, u2s, y2, wi2, wo2, rows3, "ffn2_bwd")
    gwo2 = matmul(act2, dy2b, "tn", tm=Fd // 2, tn=D, tk=tk, out_dtype=BF, name="ffn2_dwo")
    gwi2 = matmul(h3b, dg2, "tn", tm=D, tn=Fd // 2, tk=tk, out_dtype=BF, name="ffn2_dwg", out_cols=2 * Fd)
    gwi2 = matmul(h3b, du2, "tn", tm=D, tn=Fd // 2, tk=tk, out_dtype=BF, name="ffn2_dwu", out_cols=2 * Fd, col_off=Fd, prev=gwi2)
    recv1_f2 = exchange_sibling_halves([gwi2, gwo2], kinds, shapes_of(w_f2), "reduce1_ffn2", seq_id=3)
    dcq, dhout, duy, dymb, vgd, vgm = mix_out_bwd(dx2, ym, vecs, wout, cq, lnv, proj, hh, "mix_out_bwd")
    gwout = matmul(ycat, dymb, "tn", tm=2 * M, tn=D, tk=tk, out_dtype=BF, name="mix_dwout")
    recv1_f2 = _after(recv1_f2, gwout)
    h_f2, hb_f2 = reduce_add([gwi2, gwo2], recv1_f2, w_f2, "ffn2_")
    recv2_f2 = exchange_chip_pieces(hb_f2, kinds, shapes_of(w_f2), "reduce2_ffn2", seq_id=4)
    duv, dug, dconvw4, dconvb = conv_bwd(_after([dcq], hb_f2)[0], proj, convw4, "conv_bwd")
    dux, dwa4, dwi4, drnnw4, rvec = rnn_bwd(dhout, hh, xr, ra, ii, proj, rnnw4, bda, bdi, lru_lambda, "rnn_bwd")
    dx1, h2b, dpb, vg2 = mix_in_bwd((duv, dug, dux, duy), x1, dx2, vecs, win, "mix_in_bwd")
    gwin = matmul(h2b, dpb, "tn", tm=D, tn=1024, tk=tk, out_dtype=BF, name="mix_dwin")
    recv1_mx = exchange_sibling_halves([gwin, gwout], kinds, shapes_of(w_mx), "reduce1_mix", seq_id=5)
    q_f2 = reduce_sum(_after(h_f2, gwin), recv2_f2, w_f2, "ffn2_")
    r_f2 = exchange_reduced_halves(q_f2, "reduce3_ffn2", seq_id=14)
    h_mx, hb_mx = reduce_add([gwin, gwout], _after(recv1_mx, q_f2), w_mx, "mix_")
    recv2_mx = exchange_chip_pieces(hb_mx, kinds, shapes_of(w_mx), "reduce2_mix", seq_id=6)
    dx0, act1, dg1, du1, h1b, dy1b, vg1 = ffn_bwd(_after([dx1], hb_mx)[0], xs, vecs, g1s, u1s, y1, wi1, wo1, rows1, "ffn1_bwd")
    dmod_row = jnp.concatenate([vg1[1:3], vg1[0:1], vg2[0:2], vgd[0:1], vg3[1:3], vg3[0:1]], axis=0)
    gains = jnp.concatenate([vg1[3:4], vg2[2:3], vg3[3:4], vgf[2:4]], axis=0)
    mvecs = jnp.concatenate([dconvb, vgm[0:2], rvec[0:4], jnp.zeros((1, M), F32)], axis=0)
    parts = [_rows128(dmod_row), _rows128(vgf[0:2]), _rows128(gains), _rows128(mvecs),
             _rows128(dconvw4), _rows128(drnnw4), _rows128(_diag_blocks(dwa4, H, HD)), _rows128(_diag_blocks(dwi4, H, HD))]
    sizes = [p.shape[0] for p in parts]
    packed = jnp.concatenate(parts, axis=0)
    gathered = allgather_devices_hbm(packed, "gather_small", seq_id=10)

    gwo1 = matmul(_after([act1], recv2_mx, packed)[0], dy1b, "tn", tm=Fd // 2, tn=D, tk=tk, out_dtype=BF, name="ffn1_dwo")
    w_f1o, w_f1i = w_f1[1:], w_f1[:1]
    recv1_f1o = exchange_sibling_halves([gwo1], ["row"], shapes_of(w_f1o), "reduce1_ffn1_out", seq_id=7)
    q_mx = reduce_sum(_after(h_mx, gwo1), recv2_mx, w_mx, "mix_")
    r_mx = exchange_reduced_halves(q_mx, "reduce3_mix", seq_id=15)
    gwi1 = matmul(_after([h1b], q_mx)[0], dg1, "tn", tm=D, tn=Fd // 2, tk=tk, out_dtype=BF, name="ffn1_dwg", out_cols=2 * Fd)
    h_f1o, hb_f1o = reduce_add([gwo1], _after(recv1_f1o, gwi1), w_f1o, "ffn1_out", ["row"])
    recv2_f1o = exchange_chip_pieces(hb_f1o, ["row"], shapes_of(w_f1o), "reduce2_ffn1_out", seq_id=11)
    gwi1 = matmul(h1b, _after([du1], hb_f1o, gathered)[0], "tn", tm=D, tn=Fd // 2, tk=tk, out_dtype=BF, name="ffn1_dwu", out_cols=2 * Fd,
                  col_off=Fd, prev=gwi1)
    recv1_f1i = exchange_sibling_halves([gwi1], ["col"], shapes_of(w_f1i), "reduce1_ffn1_in", seq_id=12)
    q_f1o = reduce_sum(_after(h_f1o, gwi1), recv2_f1o, w_f1o, "ffn1_out", ["row"])
    r_f1o = exchange_reduced_halves(q_f1o, "reduce3_ffn1_out", seq_id=16)
    summed = sum_slots(gathered, "sum_small")
    offs = [0]
    for s in sizes:
        offs.append(offs[-1] + s)
    seg = lambda k: summed[offs[k]:offs[k + 1]]
    g_b_mod = seg(0).reshape(1, 9 * D)
    g_b_fmod = seg(1).reshape(1, 2 * D)
    gsum = seg(2).reshape(5, D)
    loss = (0.5 / D) * jnp.sum(gsum[4])
    msum = seg(3).reshape(SUBLANES, M)
    g_conv_w = lax.dynamic_index_in_dim(seg(4).reshape(nb, CONV_TAPS, LANES), chip, axis=0, keepdims=False)[:CONV_WIDTH]
    g_rnn_w = lax.dynamic_index_in_dim(seg(5).reshape(nb, SUBLANES, LANES), chip, axis=0, keepdims=False)[:RNN_CONV_WIDTH]
    g_w_a = seg(6).reshape(H, HD, HD)
    g_w_i = seg(7).reshape(H, HD, HD)
    dmod_all = gathered[:, offs[0]:offs[1]].reshape(N_DEV, 9 * D)
    dfmod_all = gathered[:, offs[1]:offs[2]].reshape(N_DEV, 2 * D)
    dmod_cols = lax.dynamic_slice_in_dim(dmod_all, chip * n_mod, n_mod, axis=1)
    dfmod_cols = lax.dynamic_slice_in_dim(dfmod_all, chip * n_fmod, n_fmod, axis=1)

    h_f1i, hb_f1i = reduce_add([gwi1], _after(recv1_f1i, q_f1o), w_f1i, "ffn1_in", ["col"])
    recv2_f1i = exchange_chip_pieces(hb_f1i, ["col"], shapes_of(w_f1i), "reduce2_ffn1_in", seq_id=8)
    dmod_cols, dfmod_cols = _after([dmod_cols, dfmod_cols], hb_f1i)
    g_w_mod, d_w_mod, nm_w_mod, nv_w_mod = adam_cond(c_all, dmod_cols, w_mod[0], m_w_mod[0], v_w_mod[0], "adam_w_mod")
    g_w_fmod, d_w_fmod, nm_w_fmod, nv_w_fmod = adam_cond(c_all, dfmod_cols, w_fmod, m_w_fmod, v_w_fmod, "adam_w_fmod")

    def adam_group(ws, qs_, rs_, ms, vs, tags, after):
        qs_ = _after(list(qs_), *after) if after else list(qs_)
        return [adam_big(w, q_, r_, m, v, cidx, "adam_" + t) for w, q_, r_, m, v, t in zip(ws, qs_, rs_, ms, vs, tags)]

    ad_f2 = adam_group(w_f2, q_f2, r_f2, [m_w_ffn2_in[0], m_w_ffn2_out[0]], [v_w_ffn2_in[0], v_w_ffn2_out[0]],
                       ["ffn2_in", "ffn2_out"], [hb_f1i])
    ad_mx = adam_group(w_mx, q_mx, r_mx, [m_w_in[0], m_w_out[0]], [v_w_in[0], v_w_out[0]], ["w_in", "w_out"], [hb_f1i])
    ad_f1o = adam_group(w_f1o, q_f1o, r_f1o, [m_w_ffn1_out[0]], [v_w_ffn1_out[0]], ["ffn1_out"], [hb_f1i])
    q_f1i = reduce_sum(_after(h_f1i, ad_f2[0][0], ad_f2[1][0], ad_mx[0][0], ad_mx[1][0], ad_f1o[0][0], g_w_mod, g_w_fmod),
                       recv2_f1i, w_f1i, "ffn1_in", ["col"])
    r_f1i = exchange_reduced_halves(q_f1i, "reduce3_ffn1_in", seq_id=17)
    ad_f1i = adam_group(w_f1i, q_f1i, r_f1i, [m_w_ffn1_in[0]], [v_w_ffn1_in[0]], ["ffn1_in"], [])
    big_out = ad_f1i + ad_f1o + ad_mx + ad_f2

    flat2 = lambda a: a.reshape(-1, a.shape[-1])
    small_names = ["b_mod", "g_ffn1", "g_mix", "conv_w", "conv_b", "ln_g", "ln_b", "rnn_conv_w", "rnn_conv_b", "w_a", "b_a",
                   "w_i", "b_i", "lru_lambda", "g_ffn2", "b_fmod", "g_final"]
    small_w = [b_mod, g_ffn1, g_mix, conv_w, conv_b, ln_g, ln_b, rnn_conv_w, rnn_conv_b, w_a, b_a, w_i, b_i, lru_lambda,
               g_ffn2, b_fmod, g_final]
    small_m = [m_b_mod, m_g_ffn1, m_g_mix, m_conv_w, m_conv_b, m_ln_g, m_ln_b, m_rnn_conv_w, m_rnn_conv_b, m_w_a, m_b_a,
               m_w_i, m_b_i, m_lru_lambda, m_g_ffn2, m_b_fmod, m_g_final]
    small_v = [v_b_mod, v_g_ffn1, v_g_mix, v_conv_w, v_conv_b, v_ln_g, v_ln_b, v_rnn_conv_w, v_rnn_conv_b, v_w_a, v_b_a,
               v_w_i, v_b_i, v_lru_lambda, v_g_ffn2, v_b_fmod, v_g_final]
    small_g = [g_b_mod, gsum[0:1], gsum[1:2], g_conv_w, msum[0:1], msum[1:2], msum[2:3], g_rnn_w, msum[3:4], g_w_a, msum[4:5],
               g_w_i, msum[5:6], msum[6:7], gsum[2:3], g_b_fmod, gsum[3:4]]
    small_g = [g.reshape(w.shape) for g, w in zip(small_g, small_w)]
    two_d = lambda a: a.reshape(1, -1) if a.ndim == 1 else flat2(a)
    sd, sm, sv = adam_small([two_d(a) for a in small_w], [two_d(a) for a in small_g], [two_d(a) for a in small_m],
                            [two_d(a) for a in small_v], "adam_small")
    small = {}
    for k, nm in enumerate(small_names):
        shp = small_w[k].shape
        small[nm] = (small_g[k], sd[k].reshape(shp), sm[k].reshape(shp), sv[k].reshape(shp))

    big = {"w_mod": tuple(a[None] for a in (g_w_mod, d_w_mod, nm_w_mod, nv_w_mod)),
           "w_fmod": (g_w_fmod, d_w_fmod, nm_w_fmod, nv_w_fmod)}
    for nm, res in zip(["w_ffn1_in", "w_ffn1_out", "w_in", "w_out", "w_ffn2_in", "w_ffn2_out"], big_out):
        big[nm] = tuple(a[None] for a in res)
    order = ["w_mod", "b_mod", "g_ffn1", "w_ffn1_in", "w_ffn1_out", "g_mix", "w_in", "conv_w", "conv_b", "ln_g", "ln_b",
             "rnn_conv_w", "rnn_conv_b", "w_a", "b_a", "w_i", "b_i", "lru_lambda", "w_out", "g_ffn2", "w_ffn2_in",
             "w_ffn2_out", "w_fmod", "b_fmod", "g_final"]
    table = {**small, **big}
    outs = [loss, dx0[None]]
    for kind_ in range(4):
        outs.extend(table[nm][kind_] for nm in order)
    return tuple(outs)
```

```python
import functools

import jax
import jax.numpy as jnp
from jax import lax
from jax.experimental import pallas as pl
from jax.experimental.pallas import tpu as pltpu
from jax.experimental.pallas import tpu_sc as plsc

F32 = jnp.float32
BF = jnp.bfloat16
I32 = jnp.int32
MESH = pl.DeviceIdType.MESH

EPS = 1e-6
RG_C = 8.0
MACARON_W = 0.5
CONV_WIDTH = 31
RNN_CONV_WIDTH = 4
ADAM_LR = 0.001
ADAM_B1 = 0.9
ADAM_B2 = 0.999
ADAM_EPS = 1e-08
ADAM_WD = 0.01
ADAM_STEP = 10

LANES = 128
SUBLANES = 8
VMEM_LIMIT = 62 * 1024 * 1024
N_CHIPS = 4
N_DEV = 8

R_SH1, R_SC1, R_GT1, R_SH2, R_SC2, R_GT2, R_SH3, R_SC3, R_GT3, R_FSH, R_FSC, R_G1, R_G2, R_G3, R_GF = range(15)

CONTRACT_LAST = (((1,), (1,)), ((), ()))
CONTRACT_FIRST = (((0,), (0,)), ((), ()))


def _pcall(body, **kw):
    return pl.pallas_call(body, **kw)


def _params(sem=None, vmem=VMEM_LIMIT):
    if sem is None:
        return pltpu.CompilerParams(vmem_limit_bytes=vmem)
    return pltpu.CompilerParams(dimension_semantics=sem, vmem_limit_bytes=vmem)


def _row(ref, r):
    return ref[r:r + 1, :]


def _sigmoid(x):
    return 0.5 * jnp.tanh(0.5 * x) + 0.5


def _colsum(x):
    return jnp.sum(x, axis=0, keepdims=True)


def _rowmean(x):
    return jnp.mean(x, axis=-1, keepdims=True)


def matmul(a, b, mode, *, tm, tn, tk, name, out_dtype=F32, out_cols=None, col_off=0, prev=None):
    if mode == "nn":
        (M, K), (K2, N) = a.shape, b.shape
    elif mode == "nt":
        (M, K), (N, K2) = a.shape, b.shape
    else:
        (K, M), (K2, N) = a.shape, b.shape
    assert K == K2 and M % tm == 0 and N % tn == 0 and K % tk == 0 and col_off % tn == 0
    nk = K // tk
    out_cols = N if out_cols is None else out_cols
    off = col_off // tn

    def body(*refs):
        if prev is None:
            a_ref, b_ref, o_ref, acc = refs
        else:
            a_ref, b_ref, _, o_ref, acc = refs
        k = pl.program_id(2)
        av = a_ref[...].astype(BF)
        bv = b_ref[...].astype(BF)
        if mode == "nn":
            part = jnp.dot(av, bv, preferred_element_type=F32)
        elif mode == "nt":
            part = lax.dot_general(av, bv, CONTRACT_LAST, preferred_element_type=F32)
        else:
            part = lax.dot_general(av, bv, CONTRACT_FIRST, preferred_element_type=F32)
        if nk == 1:
            o_ref[...] = part.astype(out_dtype)
            return

        @pl.when(k == 0)
        def _():
            acc[...] = part

        @pl.when((k > 0) & (k < nk - 1))
        def _():
            acc[...] += part

        @pl.when(k == nk - 1)
        def _():
            o_ref[...] = (acc[...] + part).astype(out_dtype)

    if mode == "nn":
        a_spec = pl.BlockSpec((tm, tk), lambda m, n, k: (m, k))
        b_spec = pl.BlockSpec((tk, tn), lambda m, n, k: (k, n))
    elif mode == "nt":
        a_spec = pl.BlockSpec((tm, tk), lambda m, n, k: (m, k))
        b_spec = pl.BlockSpec((tn, tk), lambda m, n, k: (n, k))
    else:
        a_spec = pl.BlockSpec((tk, tm), lambda m, n, k: (k, m))
        b_spec = pl.BlockSpec((tk, tn), lambda m, n, k: (k, n))
    in_specs = [a_spec, b_spec]
    args = [a, b]
    aliases = {}
    if prev is not None:
        in_specs.append(pl.BlockSpec(memory_space=pl.ANY))
        args.append(prev)
        aliases = {2: 0}
    return _pcall(
        body, name=name, grid=(M // tm, N // tn, nk), in_specs=in_specs,
        out_specs=pl.BlockSpec((tm, tn), lambda m, n, k: (m, n + off)),
        out_shape=jax.ShapeDtypeStruct((M, out_cols), out_dtype),
        scratch_shapes=[pltpu.VMEM((tm, tn), F32)], input_output_aliases=aliases,
        compiler_params=_params(("parallel", "parallel", "arbitrary")),
    )(*args)


def cond_matmul(c_all, w, name):
    B, K = c_all.shape
    N = w.shape[1]
    tn = 256
    assert N % tn == 0

    def body(c_ref, w_ref, o_ref):
        cv = c_ref[...]
        ca = cv * _sigmoid(cv)
        o_ref[...] = jnp.dot(ca.astype(BF), w_ref[...].astype(BF), preferred_element_type=F32)

    return _pcall(
        body, name=name, grid=(N // tn,),
        in_specs=[pl.BlockSpec((B, K), lambda n: (0, 0)), pl.BlockSpec((K, tn), lambda n: (0, n))],
        out_specs=pl.BlockSpec((B, tn), lambda n: (0, n)),
        out_shape=jax.ShapeDtypeStruct((B, N), F32), compiler_params=_params(("parallel",)),
    )(c_all, w)


FFN_FWD_TS = 512
FFN_BWD_TS = 256
DW_TILE = 256


def _resident(shape, index_map):
    return pl.BlockSpec(shape, index_map, pipeline_mode=pl.Buffered(1))


def _final_norm_loss_grad(xv, t, v_ref, vg_ref):
    D = xv.shape[-1]
    r = lax.rsqrt(_rowmean(xv * xv) + EPS)
    n = xv * r
    g = _row(v_ref, R_GF)
    sc1 = 1.0 + _row(v_ref, R_FSC)
    gsc = g * sc1
    e = n * gsc + _row(v_ref, R_FSH) - t
    vg_ref[3:4, :] += _colsum(e * e)
    dout = e * (1.0 / D)
    dn_ = dout * n
    vg_ref[0:1, :] += _colsum(dout)
    vg_ref[1:2, :] += _colsum(dn_) * g
    vg_ref[2:3, :] += _colsum(dn_) * sc1
    dn = dout * gsc
    return r * (dn - n * _rowmean(dn * n))


def ffn_fwd(x, vecs, wi, wo, rows, name, final_tgt=None):
    r_sh, r_sc, r_gt, r_g = rows
    S, D = x.shape
    Fd = wo.shape[0]
    ts = min(FFN_FWD_TS, S)
    with_final = final_tgt is not None

    def body(*refs):
        if with_final:
            x_ref, v_ref, wg_ref, wu_ref, wo_ref, t_ref, xo_ref, g_ref, u_ref, y_ref, vg_ref = refs
        else:
            x_ref, v_ref, wg_ref, wu_ref, wo_ref, xo_ref, g_ref, u_ref, y_ref = refs
        xv = x_ref[...]
        r = lax.rsqrt(_rowmean(xv * xv) + EPS)
        gs = _row(v_ref, r_g) * (1.0 + _row(v_ref, r_sc))
        hb = (xv * r * gs + _row(v_ref, r_sh)).astype(BF)
        G = jnp.dot(hb, wg_ref[...], preferred_element_type=F32)
        U = jnp.dot(hb, wu_ref[...], preferred_element_type=F32)
        g_ref[...] = G.astype(BF)
        u_ref[...] = U.astype(BF)
        act = (G * _sigmoid(G) * U).astype(BF)
        Y = jnp.dot(act, wo_ref[...], preferred_element_type=F32)
        y_ref[...] = Y.astype(BF)
        xo = xv + (MACARON_W * _row(v_ref, r_gt)) * Y
        if with_final:
            @pl.when(pl.program_id(0) == 0)
            def _():
                vg_ref[...] = jnp.zeros_like(vg_ref)

            xo_ref[...] = _final_norm_loss_grad(xo, t_ref[...], v_ref, vg_ref)
        else:
            xo_ref[...] = xo

    tok = pl.BlockSpec((ts, D), lambda i: (i, 0))
    hid = pl.BlockSpec((ts, Fd), lambda i: (i, 0))
    in_specs = [tok, pl.BlockSpec(vecs.shape, lambda i: (0, 0)), _resident((D, Fd), lambda i: (0, 0)),
                _resident((D, Fd), lambda i: (0, 1)), _resident((Fd, D), lambda i: (0, 0))]
    out_specs = [tok, hid, hid, tok]
    out_shape = [jax.ShapeDtypeStruct((S, D), F32), jax.ShapeDtypeStruct((S, Fd), BF),
                 jax.ShapeDtypeStruct((S, Fd), BF), jax.ShapeDtypeStruct((S, D), BF)]
    args = [x, vecs, wi, wi, wo]
    if with_final:
        in_specs.append(tok)
        args.append(final_tgt)
        out_specs.append(pl.BlockSpec((SUBLANES, D), lambda i: (0, 0)))
        out_shape.append(jax.ShapeDtypeStruct((SUBLANES, D), F32))
    return _pcall(body, name=name, grid=(S // ts,), in_specs=in_specs, out_specs=out_specs, out_shape=out_shape,
                  compiler_params=_params(("arbitrary",)))(*args)


def ffn_fwd_in(x, vecs, wi, rows, name):
    r_sh, r_sc, r_gt, r_g = rows
    S, D = x.shape
    Fd = wi.shape[1] // 2
    ts = min(FFN_FWD_TS, S)

    def body(x_ref, v_ref, wg_ref, wu_ref, g_ref, u_ref, a_ref):
        xv = x_ref[...]
        r = lax.rsqrt(_rowmean(xv * xv) + EPS)
        gs = _row(v_ref, r_g) * (1.0 + _row(v_ref, r_sc))
        hb = (xv * r * gs + _row(v_ref, r_sh)).astype(BF)
        G = jnp.dot(hb, wg_ref[...], preferred_element_type=F32)
        U = jnp.dot(hb, wu_ref[...], preferred_element_type=F32)
        g_ref[...] = G.astype(BF)
        u_ref[...] = U.astype(BF)
        a_ref[...] = (G * _sigmoid(G) * U).astype(BF)

    hid = pl.BlockSpec((ts, Fd), lambda i: (i, 0))
    return _pcall(
        body, name=name, grid=(S // ts,),
        in_specs=[pl.BlockSpec((ts, D), lambda i: (i, 0)), pl.BlockSpec(vecs.shape, lambda i: (0, 0)),
                  _resident((D, Fd), lambda i: (0, 0)), _resident((D, Fd), lambda i: (0, 1))],
        out_specs=[hid, hid, hid], out_shape=[jax.ShapeDtypeStruct((S, Fd), BF)] * 3,
        compiler_params=_params(("arbitrary",)),
    )(x, vecs, wi, wi)


def ffn_fwd_out(act, x, vecs, wo, rows, name):
    r_sh, r_sc, r_gt, r_g = rows
    S, D = x.shape
    Fd = wo.shape[0]
    ts = min(FFN_FWD_TS, S)

    def body(a_ref, x_ref, v_ref, wo_ref, xo_ref, y_ref):
        Y = jnp.dot(a_ref[...], wo_ref[...], preferred_element_type=F32)
        y_ref[...] = Y.astype(BF)
        xo_ref[...] = x_ref[...] + (MACARON_W * _row(v_ref, r_gt)) * Y

    tok = pl.BlockSpec((ts, D), lambda i: (i, 0))
    return _pcall(
        body, name=name, grid=(S // ts,),
        in_specs=[pl.BlockSpec((ts, Fd), lambda i: (i, 0)), tok, pl.BlockSpec(vecs.shape, lambda i: (0, 0)),
                  _resident((Fd, D), lambda i: (0, 0))],
        out_specs=[tok, tok], out_shape=[jax.ShapeDtypeStruct((S, D), F32), jax.ShapeDtypeStruct((S, D), BF)],
        compiler_params=_params(("arbitrary",)),
    )(act, x, vecs, wo)


def ffn_bwd(dxo, x, vecs, gs_, us_, y, wi, wo, rows, name):
    r_sh, r_sc, r_gt, r_g = rows
    S, D = x.shape
    Fd = wo.shape[0]
    ts = min(FFN_BWD_TS, S)

    def body(dxo_ref, x_ref, v_ref, g_ref, u_ref, y_ref, wg_ref, wu_ref, wo_ref,
             dx_ref, act_ref, dg_ref, du_ref, hb_ref, dyb_ref, vg_ref):
        @pl.when(pl.program_id(0) == 0)
        def _():
            vg_ref[...] = jnp.zeros_like(vg_ref)

        dxo_v = dxo_ref[...]
        dyb = ((MACARON_W * _row(v_ref, r_gt)) * dxo_v).astype(BF)
        dyb_ref[...] = dyb
        vg_ref[0:1, :] += MACARON_W * _colsum(dxo_v * y_ref[...].astype(F32))
        dA = lax.dot_general(dyb, wo_ref[...], CONTRACT_LAST, preferred_element_type=F32)
        G = g_ref[...].astype(F32)
        U = u_ref[...].astype(F32)
        sg = _sigmoid(G)
        sl = G * sg
        dU = (dA * sl).astype(BF)
        dG = (dA * U * (sg * (1.0 + G * (1.0 - sg)))).astype(BF)
        act_ref[...] = (sl * U).astype(BF)
        dg_ref[...] = dG
        du_ref[...] = dU
        dh = (lax.dot_general(dG, wg_ref[...], CONTRACT_LAST, preferred_element_type=F32)
              + lax.dot_general(dU, wu_ref[...], CONTRACT_LAST, preferred_element_type=F32))
        xv = x_ref[...]
        r = lax.rsqrt(_rowmean(xv * xv) + EPS)
        n = xv * r
        g = _row(v_ref, r_g)
        sc1 = 1.0 + _row(v_ref, r_sc)
        gsc = g * sc1
        hb_ref[...] = (n * gsc + _row(v_ref, r_sh)).astype(BF)
        dhn = dh * n
        vg_ref[1:2, :] += _colsum(dh)
        vg_ref[2:3, :] += _colsum(dhn) * g
        vg_ref[3:4, :] += _colsum(dhn) * sc1
        dn = dh * gsc
        dx_ref[...] = dxo_v + r * (dn - n * _rowmean(dn * n))

    tok = pl.BlockSpec((ts, D), lambda i: (i, 0))
    hid = pl.BlockSpec((ts, Fd), lambda i: (i, 0))
    return _pcall(
        body, name=name, grid=(S // ts,),
        in_specs=[tok, tok, pl.BlockSpec(vecs.shape, lambda i: (0, 0)), hid, hid, tok, _resident((D, Fd), lambda i: (0, 0)),
                  _resident((D, Fd), lambda i: (0, 1)), _resident((Fd, D), lambda i: (0, 0))],
        out_specs=[tok, hid, hid, hid, tok, tok, pl.BlockSpec((SUBLANES, D), lambda i: (0, 0))],
        out_shape=[jax.ShapeDtypeStruct((S, D), F32), jax.ShapeDtypeStruct((S, Fd), BF),
                   jax.ShapeDtypeStruct((S, Fd), BF), jax.ShapeDtypeStruct((S, Fd), BF),
                   jax.ShapeDtypeStruct((S, D), BF), jax.ShapeDtypeStruct((S, D), BF),
                   jax.ShapeDtypeStruct((SUBLANES, D), F32)],
        compiler_params=_params(("arbitrary",)),
    )(dxo, x, vecs, gs_, us_, y, wi, wi, wo)


def norm_matmul(x, vecs, w, rows, name):
    r_sh, r_sc, r_g = rows
    S, D = x.shape
    N = w.shape[1]
    ts = min(512, S)

    def body(x_ref, v_ref, w_ref, o_ref):
        xv = x_ref[...]
        r = lax.rsqrt(_rowmean(xv * xv) + EPS)
        gs = _row(v_ref, r_g) * (1.0 + _row(v_ref, r_sc))
        hb = (xv * r * gs + _row(v_ref, r_sh)).astype(BF)
        o_ref[...] = jnp.dot(hb, w_ref[...], preferred_element_type=F32).astype(BF)

    return _pcall(
        body, name=name, grid=(S // ts,),
        in_specs=[pl.BlockSpec((ts, D), lambda i: (i, 0)), pl.BlockSpec(vecs.shape, lambda i: (0, 0)),
                  _resident((D, N), lambda i: (0, 0))],
        out_specs=pl.BlockSpec((ts, N), lambda i: (i, 0)),
        out_shape=jax.ShapeDtypeStruct((S, N), BF),
        compiler_params=_params(("arbitrary",)),
    )(x, vecs, w)


SEQ_TT = 256
SCAN_SEGMENTS = 4
CONV_BWD_TT = 128
CONV_PAD = 32
CONV_TAPS = 32


def conv_fwd(proj, convw4, conv_b, name):
    S = proj.shape[0]
    M = conv_b.shape[1]
    nb = M // LANES
    tt = min(SEQ_TT, S)

    def body(uv_ref, ug_ref, w_ref, b_ref, cq_ref, qp):
        qp[0:CONV_PAD, :] = jnp.zeros((CONV_PAD, LANES), F32)

        def step(t, carry):
            base = pl.multiple_of(t * tt, tt)
            qp[pl.ds(base + CONV_PAD, tt), :] = uv_ref[pl.ds(base, tt), :].astype(F32) * _sigmoid(ug_ref[pl.ds(base, tt), :].astype(F32))
            acc = jnp.broadcast_to(b_ref[...], (tt, LANES))
            for k in range(CONV_WIDTH):
                acc = acc + w_ref[k:k + 1, :] * qp[pl.ds(base + (CONV_PAD - CONV_WIDTH + 1) + k, tt), :]
            cq_ref[pl.ds(base, tt), :] = acc
            return carry

        lax.fori_loop(0, S // tt, step, 0)

    return _pcall(
        body, name=name, grid=(nb,),
        in_specs=[pl.BlockSpec((S, LANES), lambda c: (0, c)), pl.BlockSpec((S, LANES), lambda c: (0, c + nb)),
                  pl.BlockSpec((None, CONV_TAPS, LANES), lambda c: (c, 0, 0)), pl.BlockSpec((1, LANES), lambda c: (0, c))],
        out_specs=pl.BlockSpec((S, LANES), lambda c: (0, c)),
        out_shape=jax.ShapeDtypeStruct((S, M), F32),
        scratch_shapes=[pltpu.VMEM((S + CONV_PAD, LANES), F32)],
        compiler_params=_params(("arbitrary",)),
    )(proj, proj, convw4, conv_b)


def conv_bwd(dcq, proj, convw4, name):
    S, M = dcq.shape
    nb = M // LANES
    tt = min(CONV_BWD_TT, S)

    def body(dcq_ref, uv_ref, ug_ref, w_ref, duv_ref, dug_ref, dw_ref, db_ref, dp, dw8, db8):
        dp[S:S + CONV_PAD, :] = jnp.zeros((CONV_PAD, LANES), F32)
        dw8[...] = jnp.zeros_like(dw8)
        db8[...] = jnp.zeros_like(db8)

        def fill(t, carry):
            base = pl.multiple_of(t * tt, tt)
            dp[pl.ds(base, tt), :] = dcq_ref[pl.ds(base, tt), :].astype(F32)
            return carry

        lax.fori_loop(0, S // tt, fill, 0)

        def step(t, carry):
            base = pl.multiple_of(t * tt, tt)
            uv = uv_ref[pl.ds(base, tt), :].astype(F32)
            sg = _sigmoid(ug_ref[pl.ds(base, tt), :].astype(F32))
            q_t = uv * sg
            db8[...] += dp[pl.ds(base, tt), :].reshape(tt // SUBLANES, SUBLANES, LANES).sum(axis=0)
            dq = jnp.zeros((tt, LANES), F32)
            for k in range(CONV_WIDTH):
                shifted = dp[pl.ds(base + (CONV_WIDTH - 1) - k, tt), :]
                dw8[k] += (shifted * q_t).reshape(tt // SUBLANES, SUBLANES, LANES).sum(axis=0)
                dq = dq + w_ref[k:k + 1, :] * shifted
            duv_ref[pl.ds(base, tt), :] = (dq * sg).astype(BF)
            dug_ref[pl.ds(base, tt), :] = (dq * uv * sg * (1.0 - sg)).astype(BF)
            return carry

        lax.fori_loop(0, S // tt, step, 0)
        dw_ref[...] = jnp.zeros_like(dw_ref)
        for k in range(CONV_WIDTH):
            dw_ref[k:k + 1, :] = _colsum(dw8[k])
        db_ref[...] = _colsum(db8[...])

    col = lambda o: pl.BlockSpec((S, LANES), lambda c: (0, c + o))
    return _pcall(
        body, name=name, grid=(nb,),
        in_specs=[col(0), col(0), col(nb), pl.BlockSpec((None, CONV_TAPS, LANES), lambda c: (c, 0, 0))],
        out_specs=[col(0), col(0), pl.BlockSpec((None, CONV_TAPS, LANES), lambda c: (c, 0, 0)),
                   pl.BlockSpec((1, LANES), lambda c: (0, c))],
        out_shape=[jax.ShapeDtypeStruct((S, M), BF), jax.ShapeDtypeStruct((S, M), BF),
                   jax.ShapeDtypeStruct((nb, CONV_TAPS, LANES), F32), jax.ShapeDtypeStruct((1, M), F32)],
        scratch_shapes=[pltpu.VMEM((S + CONV_PAD, LANES), F32),
                        pltpu.VMEM((CONV_TAPS, SUBLANES, LANES), F32), pltpu.VMEM((SUBLANES, LANES), F32)],
        compiler_params=_params(("arbitrary",)),
    )(dcq, proj, proj, convw4)


def _log_sigmoid(x):
    return jnp.minimum(x, 0.0) - jnp.log(1.0 + jnp.exp(-jnp.abs(x)))


def _rg_gate_terms(ra, ls):
    la = RG_C * ra * ls
    a = jnp.exp(la)
    th = jnp.tanh(la)
    mult = jnp.sqrt(-2.0 * th / (1.0 - th))
    return a, mult


def rnn_fwd(proj, rnnw4, rnn_b, bda, bdi, b_a, b_i, lam, name):
    S = proj.shape[0]
    M = rnn_b.shape[1]
    nb = M // LANES
    tt = min(SEQ_TT, S)
    KW = RNN_CONV_WIDTH
    nseg = SCAN_SEGMENTS if S % (SCAN_SEGMENTS * tt) == 0 else 1

    def body(ux_ref, w_ref, rb_ref, bda_ref, bdi_ref, ba_ref, bi_ref, lam_ref,
             xr_ref, ra_ref, ii_ref, h_ref, uxp, a_sc, b_sc):
        uxp[0:SUBLANES, :] = jnp.zeros((SUBLANES, LANES), F32)
        ls = _log_sigmoid(lam_ref[...])

        def step(t, carry):
            base = pl.multiple_of(t * tt, tt)
            uxp[pl.ds(base + SUBLANES, tt), :] = ux_ref[pl.ds(base, tt), :].astype(F32)
            xr = jnp.broadcast_to(rb_ref[...], (tt, LANES))
            for k in range(KW):
                xr = xr + w_ref[k:k + 1, :] * uxp[pl.ds(base + (SUBLANES - KW + 1) + k, tt), :]
            xb = xr.astype(BF)
            ra = _sigmoid(jnp.dot(xb, bda_ref[...], preferred_element_type=F32) + ba_ref[...])
            ii = _sigmoid(jnp.dot(xb, bdi_ref[...], preferred_element_type=F32) + bi_ref[...])
            a, mult = _rg_gate_terms(ra, ls)
            xr_ref[pl.ds(base, tt), :] = xr
            ra_ref[pl.ds(base, tt), :] = ra
            ii_ref[pl.ds(base, tt), :] = ii
            a_sc[pl.ds(base, tt), :] = a
            b_sc[pl.ds(base, tt), :] = mult * (ii * xr)
            return carry

        lax.fori_loop(0, S // tt, step, 0)

        rows = lax.broadcasted_iota(I32, (SUBLANES, LANES), 0)
        seg = S // nseg
        last = lambda v: jnp.broadcast_to(v[SUBLANES - 1:SUBLANES, :], (SUBLANES, LANES))

        def scan(t, carry):
            hs, ps = carry
            new_h, new_p = [], []
            for s in range(nseg):
                base = pl.multiple_of(s * seg + t * SUBLANES, SUBLANES)
                A = a_sc[pl.ds(base, SUBLANES), :]
                B = b_sc[pl.ds(base, SUBLANES), :]
                for d in (1, 2, 4):
                    As = jnp.where(rows >= d, pltpu.roll(A, d, axis=0), 1.0)
                    Bs = jnp.where(rows >= d, pltpu.roll(B, d, axis=0), 0.0)
                    B = A * Bs + B
                    A = A * As
                hh = B + A * hs[s]
                h_ref[pl.ds(base, SUBLANES), :] = hh
                pp = A * ps[s]
                if s > 0:
                    a_sc[pl.ds(base, SUBLANES), :] = pp
                new_h.append(last(hh))
                new_p.append(last(pp))
            return tuple(new_h), tuple(new_p)

        zero8 = jnp.zeros((SUBLANES, LANES), F32)
        one8 = jnp.ones((SUBLANES, LANES), F32)
        hs, ps = lax.fori_loop(0, seg // SUBLANES, scan, ((zero8,) * nseg, (one8,) * nseg))
        carry_in = hs[0]
        for s in range(1, nseg):
            c_row = carry_in[0:1, :]

            def fix(t, c, s=s, c_row=c_row):
                base = pl.multiple_of(s * seg + t * tt, tt)
                h_ref[pl.ds(base, tt), :] = h_ref[pl.ds(base, tt), :] + a_sc[pl.ds(base, tt), :] * c_row
                return c

            lax.fori_loop(0, seg // tt, fix, 0)
            carry_in = hs[s] + ps[s] * carry_in

    col = lambda o: pl.BlockSpec((S, LANES), lambda c: (0, c + o))
    vec = pl.BlockSpec((1, LANES), lambda c: (0, c))
    diag = pl.BlockSpec((LANES, LANES), lambda c: (c, c))
    return _pcall(
        body, name=name, grid=(nb,),
        in_specs=[col(2 * nb), pl.BlockSpec((None, SUBLANES, LANES), lambda c: (c, 0, 0)), vec, diag, diag, vec, vec, vec],
        out_specs=[col(0)] * 4,
        out_shape=[jax.ShapeDtypeStruct((S, M), F32)] * 4,
        scratch_shapes=[pltpu.VMEM((S + SUBLANES, LANES), F32), pltpu.VMEM((S, LANES), F32), pltpu.VMEM((S, LANES), F32)],
        compiler_params=_params(("arbitrary",)),
    )(proj, rnnw4, rnn_b, bda, bdi, b_a, b_i, lam)


def rnn_bwd(dhout, h, xr, ra, ii, proj, rnnw4, bda, bdi, lam, name):
    S, M = h.shape
    nb = M // LANES
    tt = min(SEQ_TT, S)
    KW = RNN_CONV_WIDTH
    SL = SUBLANES
    nseg = SCAN_SEGMENTS if S % (SCAN_SEGMENTS * tt) == 0 else 1

    def body(dh_ref, h_ref, xr_ref, ra_ref, ii_ref, ux_ref, w_ref, bda_ref, bdi_ref, lam_ref,
             dux_ref, dwa_ref, dwi_ref, drw_ref, vec_ref,
             a_sc, hp, g_sc, dpa_sc, dpi_sc, dxp, uxp, acc8, drw8, p_sc):
        zero8 = jnp.zeros((SL, LANES), F32)
        a_sc[S:S + SL, :] = zero8
        hp[0:SL, :] = zero8
        dxp[S:S + SL, :] = zero8
        uxp[0:SL, :] = zero8
        acc8[...] = jnp.zeros_like(acc8)
        drw8[...] = jnp.zeros_like(drw8)
        lamv = lam_ref[...]
        ls = _log_sigmoid(lamv)

        def fill(t, carry):
            base = pl.multiple_of(t * tt, tt)
            a_sc[pl.ds(base, tt), :] = jnp.exp(RG_C * ra_ref[pl.ds(base, tt), :] * ls)
            hp[pl.ds(base + SL, tt), :] = h_ref[pl.ds(base, tt), :]
            uxp[pl.ds(base + SL, tt), :] = ux_ref[pl.ds(base, tt), :].astype(F32)
            return carry

        lax.fori_loop(0, S // tt, fill, 0)

        rows = lax.broadcasted_iota(I32, (SL, LANES), 0)
        seg = S // nseg
        nt8 = seg // SL
        first = lambda v: jnp.broadcast_to(v[0:1, :], (SL, LANES))

        def rscan(t, carry):
            gs, ps = carry
            new_g, new_p = [], []
            for s in range(nseg):
                base = pl.multiple_of(s * seg + (nt8 - 1 - t) * SL, SL)
                A = a_sc[pl.ds(base + 1, SL), :]
                B = dh_ref[pl.ds(base, SL), :]
                for d in (1, 2, 4):
                    As = jnp.where(rows < SL - d, pltpu.roll(A, SL - d, axis=0), 1.0)
                    Bs = jnp.where(rows < SL - d, pltpu.roll(B, SL - d, axis=0), 0.0)
                    B = A * Bs + B
                    A = A * As
                g = B + A * gs[s]
                g_sc[pl.ds(base, SL), :] = g
                pp = A * ps[s]
                if s < nseg - 1:
                    p_sc[pl.ds(base, SL), :] = pp
                new_g.append(first(g))
                new_p.append(first(pp))
            return tuple(new_g), tuple(new_p)

        one8 = jnp.ones((SL, LANES), F32)
        gs, ps = lax.fori_loop(0, nt8, rscan, ((zero8,) * nseg, (one8,) * nseg))
        carry_in = gs[nseg - 1]
        for s in range(nseg - 2, -1, -1):
            c_row = carry_in[0:1, :]

            def fix(t, c, s=s, c_row=c_row):
                base = pl.multiple_of(s * seg + t * tt, tt)
                g_sc[pl.ds(base, tt), :] = g_sc[pl.ds(base, tt), :] + p_sc[pl.ds(base, tt), :] * c_row
                return c

            lax.fori_loop(0, seg // tt, fix, 0)
            carry_in = gs[s] + ps[s] * carry_in

        def red8(v):
            return v.reshape(tt // SL, SL, LANES).sum(axis=0)

        def step(t, carry):
            base = pl.multiple_of(t * tt, tt)
            g = g_sc[pl.ds(base, tt), :]
            hprev = hp[pl.ds(base + SL - 1, tt), :]
            xr_t = xr_ref[pl.ds(base, tt), :]
            ra_t = ra_ref[pl.ds(base, tt), :]
            ii_t = ii_ref[pl.ds(base, tt), :]
            a, mult = _rg_gate_terms(ra_t, ls)
            gx = g * xr_t
            dmult = gx * ii_t
            dii = gx * mult
            dxr = g * (mult * ii_t)
            dla = g * hprev * a - dmult * (a * a) / mult
            acc8[3] += red8(dla * ra_t)
            dpa = dla * (RG_C * ls) * ra_t * (1.0 - ra_t)
            dpi = dii * ii_t * (1.0 - ii_t)
            dpab = dpa.astype(BF)
            dpib = dpi.astype(BF)
            dxr = dxr + (lax.dot_general(dpab, bda_ref[...], CONTRACT_LAST, preferred_element_type=F32)
                         + lax.dot_general(dpib, bdi_ref[...], CONTRACT_LAST, preferred_element_type=F32))
            dpa_sc[pl.ds(base, tt), :] = dpab
            dpi_sc[pl.ds(base, tt), :] = dpib
            dxp[pl.ds(base, tt), :] = dxr
            acc8[0] += red8(dxr)
            acc8[1] += red8(dpa)
            acc8[2] += red8(dpi)
            return carry

        lax.fori_loop(0, S // tt, step, 0)

        def convb(t, carry):
            base = pl.multiple_of(t * tt, tt)
            d_t = dxp[pl.ds(base, tt), :]
            dux = jnp.zeros((tt, LANES), F32)
            for k in range(KW):
                drw8[k] += red8(d_t * uxp[pl.ds(base + (SL - KW + 1) + k, tt), :])
                dux = dux + w_ref[k:k + 1, :] * dxp[pl.ds(base + (KW - 1) - k, tt), :]
            dux_ref[pl.ds(base, tt), :] = dux.astype(BF)
            return carry

        lax.fori_loop(0, S // tt, convb, 0)

        xb = xr_ref[...].astype(BF)
        dwa_ref[...] = lax.dot_general(xb, dpa_sc[...], CONTRACT_FIRST, preferred_element_type=F32)
        dwi_ref[...] = lax.dot_general(xb, dpi_sc[...], CONTRACT_FIRST, preferred_element_type=F32)
        drw_ref[...] = jnp.zeros_like(drw_ref)
        vec_ref[...] = jnp.zeros_like(vec_ref)
        for k in range(KW):
            drw_ref[k:k + 1, :] = _colsum(drw8[k])
        for k in range(3):
            vec_ref[k:k + 1, :] = _colsum(acc8[k])
        vec_ref[3:4, :] = _colsum(acc8[3]) * (RG_C * _sigmoid(-lamv))

    col = lambda o: pl.BlockSpec((S, LANES), lambda c: (0, c + o))
    vec = pl.BlockSpec((1, LANES), lambda c: (0, c))
    diag = pl.BlockSpec((LANES, LANES), lambda c: (c, c))
    blk3 = lambda r: pl.BlockSpec((None, r, LANES), lambda c: (c, 0, 0))
    return _pcall(
        body, name=name, grid=(nb,),
        in_specs=[col(0), col(0), col(0), col(0), col(0), col(2 * nb), blk3(SL), diag, diag, vec],
        out_specs=[col(0), blk3(LANES), blk3(LANES), blk3(SL), pl.BlockSpec((SL, LANES), lambda c: (0, c))],
        out_shape=[jax.ShapeDtypeStruct((S, M), BF), jax.ShapeDtypeStruct((nb, LANES, LANES), F32),
                   jax.ShapeDtypeStruct((nb, LANES, LANES), F32), jax.ShapeDtypeStruct((nb, SL, LANES), F32),
                   jax.ShapeDtypeStruct((SL, M), F32)],
        scratch_shapes=[pltpu.VMEM((S + SL, LANES), F32), pltpu.VMEM((S + SL, LANES), F32), pltpu.VMEM((S, LANES), F32),
                        pltpu.VMEM((S, LANES), BF), pltpu.VMEM((S, LANES), BF), pltpu.VMEM((S + SL, LANES), F32),
                        pltpu.VMEM((S + SL, LANES), F32), pltpu.VMEM((SL, SL, LANES), F32), pltpu.VMEM((SL, SL, LANES), F32),
                        pltpu.VMEM((S, LANES), F32)],
        compiler_params=_params(("arbitrary",)),
    )(dhout, h, xr, ra, ii, proj, rnnw4, bda, bdi, lam)


GELU_K = 0.7978845608028654
GELU_C = 0.044715


def _layernorm_parts(cq):
    mu = _rowmean(cq)
    d = cq - mu
    rstd = lax.rsqrt(_rowmean(d * d) + EPS)
    return d * rstd, rstd


def mix_out(cq, proj, h, x, vecs, lnv, wout, name):
    S, D = x.shape
    M = cq.shape[1]
    ts = min(512, S)

    def body(cq_ref, uy_ref, h_ref, x_ref, v_ref, ln_ref, w_ref, xo_ref, ym_ref, yc_ref):
        z, _ = _layernorm_parts(cq_ref[...])
        l = z * _row(ln_ref, 0) + _row(ln_ref, 1)
        yc_ref[:, 0:M] = (l * _sigmoid(l)).astype(BF)
        uy = uy_ref[...].astype(F32)
        gelu = 0.5 * uy * (1.0 + jnp.tanh(GELU_K * (uy + GELU_C * uy * uy * uy)))
        yc_ref[:, M:2 * M] = (gelu * h_ref[...]).astype(BF)
        ym = jnp.dot(yc_ref[...], w_ref[...], preferred_element_type=F32)
        ym_ref[...] = ym.astype(BF)
        xo_ref[...] = x_ref[...] + _row(v_ref, R_GT2) * ym

    tok = pl.BlockSpec((ts, D), lambda i: (i, 0))
    mtok = lambda o: pl.BlockSpec((ts, M), lambda i: (i, o))
    return _pcall(
        body, name=name, grid=(S // ts,),
        in_specs=[mtok(0), mtok(3), mtok(0), tok, pl.BlockSpec(vecs.shape, lambda i: (0, 0)),
                  pl.BlockSpec(lnv.shape, lambda i: (0, 0)), pl.BlockSpec(wout.shape, lambda i: (0, 0))],
        out_specs=[tok, tok, pl.BlockSpec((ts, 2 * M), lambda i: (i, 0))],
        out_shape=[jax.ShapeDtypeStruct((S, D), F32), jax.ShapeDtypeStruct((S, D), BF),
                   jax.ShapeDtypeStruct((S, 2 * M), BF)],
        compiler_params=_params(("arbitrary",)),
    )(cq, proj, h, x, vecs, lnv, wout)


def mix_out_bwd(dxo, ym, vecs, wout, cq, lnv, proj, h, name):
    S, D = dxo.shape
    M = cq.shape[1]
    ts = min(512, S)

    def body(dxo_ref, ym_ref, v_ref, w_ref, cq_ref, ln_ref, uy_ref, h_ref,
             dcq_ref, dh_ref, duy_ref, dyb_ref, vgd_ref, vgm_ref):
        @pl.when(pl.program_id(0) == 0)
        def _():
            vgd_ref[...] = jnp.zeros_like(vgd_ref)
            vgm_ref[...] = jnp.zeros_like(vgm_ref)

        dxo_v = dxo_ref[...]
        dyb = (_row(v_ref, R_GT2) * dxo_v).astype(BF)
        dyb_ref[...] = dyb
        vgd_ref[0:1, :] += _colsum(dxo_v * ym_ref[...].astype(F32))
        dycat = lax.dot_general(dyb, w_ref[...], CONTRACT_LAST, preferred_element_type=F32)
        dyc = dycat[:, 0:M]
        dyr = dycat[:, M:2 * M]
        z, rstd = _layernorm_parts(cq_ref[...])
        lng = _row(ln_ref, 0)
        l = z * lng + _row(ln_ref, 1)
        sl = _sigmoid(l)
        dl = dyc * (sl * (1.0 + l * (1.0 - sl)))
        vgm_ref[0:1, :] += _colsum(dl * z)
        vgm_ref[1:2, :] += _colsum(dl)
        dz = dl * lng
        dcq_ref[...] = (rstd * (dz - _rowmean(dz) - z * _rowmean(dz * z))).astype(BF)
        uy = uy_ref[...].astype(F32)
        u2 = uy * uy
        th = jnp.tanh(GELU_K * (uy + GELU_C * uy * u2))
        gelu = 0.5 * uy * (1.0 + th)
        dgelu = 0.5 * (1.0 + th) + 0.5 * uy * (1.0 - th * th) * (GELU_K * (1.0 + 3.0 * GELU_C * u2))
        dh_ref[...] = dyr * gelu
        duy_ref[...] = (dyr * h_ref[...] * dgelu).astype(BF)

    tok = pl.BlockSpec((ts, D), lambda i: (i, 0))
    mtok = lambda o: pl.BlockSpec((ts, M), lambda i: (i, o))
    return _pcall(
        body, name=name, grid=(S // ts,),
        in_specs=[tok, tok, pl.BlockSpec(vecs.shape, lambda i: (0, 0)), pl.BlockSpec(wout.shape, lambda i: (0, 0)),
                  mtok(0), pl.BlockSpec(lnv.shape, lambda i: (0, 0)), mtok(3), mtok(0)],
        out_specs=[mtok(0), mtok(0), mtok(0), tok, pl.BlockSpec((SUBLANES, D), lambda i: (0, 0)),
                   pl.BlockSpec((SUBLANES, M), lambda i: (0, 0))],
        out_shape=[jax.ShapeDtypeStruct((S, M), BF), jax.ShapeDtypeStruct((S, M), F32), jax.ShapeDtypeStruct((S, M), BF),
                   jax.ShapeDtypeStruct((S, D), BF),
                   jax.ShapeDtypeStruct((SUBLANES, D), F32), jax.ShapeDtypeStruct((SUBLANES, M), F32)],
        compiler_params=_params(("arbitrary",)),
    )(dxo, ym, vecs, wout, cq, lnv, proj, h)


def mix_in_bwd(dparts, x, dxo, vecs, win, name):
    S, D = x.shape
    M = dparts[0].shape[1]
    ts = min(512, S)

    def body(d0, d1, d2, d3, x_ref, dxo_ref, v_ref, w_ref, dx_ref, hb_ref, dp_ref, vg_ref):
        @pl.when(pl.program_id(0) == 0)
        def _():
            vg_ref[...] = jnp.zeros_like(vg_ref)

        for q, dref in enumerate((d0, d1, d2, d3)):
            dp_ref[:, q * M:(q + 1) * M] = dref[...].astype(BF)
        dh = lax.dot_general(dp_ref[...], w_ref[...], CONTRACT_LAST, preferred_element_type=F32)
        xv = x_ref[...]
        r = lax.rsqrt(_rowmean(xv * xv) + EPS)
        n = xv * r
        g = _row(v_ref, R_G2)
        sc1 = 1.0 + _row(v_ref, R_SC2)
        gsc = g * sc1
        hb_ref[...] = (n * gsc + _row(v_ref, R_SH2)).astype(BF)
        dhn = dh * n
        vg_ref[0:1, :] += _colsum(dh)
        vg_ref[1:2, :] += _colsum(dhn) * g
        vg_ref[2:3, :] += _colsum(dhn) * sc1
        dn = dh * gsc
        dx_ref[...] = dxo_ref[...] + r * (dn - n * _rowmean(dn * n))

    tok = pl.BlockSpec((ts, D), lambda i: (i, 0))
    mtok = pl.BlockSpec((ts, M), lambda i: (i, 0))
    return _pcall(
        body, name=name, grid=(S // ts,),
        in_specs=[mtok] * 4 + [tok, tok, pl.BlockSpec(vecs.shape, lambda i: (0, 0)), pl.BlockSpec(win.shape, lambda i: (0, 0))],
        out_specs=[tok, tok, pl.BlockSpec((ts, 4 * M), lambda i: (i, 0)), pl.BlockSpec((SUBLANES, D), lambda i: (0, 0))],
        out_shape=[jax.ShapeDtypeStruct((S, D), F32), jax.ShapeDtypeStruct((S, D), BF),
                   jax.ShapeDtypeStruct((S, 4 * M), BF), jax.ShapeDtypeStruct((SUBLANES, D), F32)],
        compiler_params=_params(("arbitrary",)),
    )(*dparts, x, dxo, vecs, win)


def _adamw(w, g, m, v):
    m = ADAM_B1 * m + (1.0 - ADAM_B1) * g
    v = ADAM_B2 * v + (1.0 - ADAM_B2) * (g * g)
    m_hat = m / (1.0 - ADAM_B1 ** ADAM_STEP)
    v_hat = v / (1.0 - ADAM_B2 ** ADAM_STEP)
    delta = -ADAM_LR * (m_hat / (jnp.sqrt(v_hat) + ADAM_EPS) + ADAM_WD * w)
    return delta, m, v


def adam_big(w, g_mine, g_sib, m, v, cidx, name):
    R, C = w.shape
    hr = R // 2
    tr = 256 if hr % 256 == 0 else hr
    tc = C if C <= 1536 else (1152 if C % 1152 == 0 else 1024)
    assert hr % tr == 0 and C % tc == 0 and g_mine.shape == (hr, C)
    nrb = hr // tr

    def body(ci_ref, w_ref, gm_ref, gs_ref, m_ref, v_ref, g_ref, d_ref, nm_ref, nv_ref):
        mine = (pl.program_id(0) // nrb) == ci_ref[0]
        g = jnp.where(mine, gm_ref[...], gs_ref[...])
        d, nm, nv = _adamw(w_ref[...], g, m_ref[...], v_ref[...])
        g_ref[...] = g
        d_ref[...] = d
        nm_ref[...] = nm
        nv_ref[...] = nv

    blk = pl.BlockSpec((tr, tc), lambda i, j, ci: (i, j))
    mine_spec = pl.BlockSpec((tr, tc), lambda i, j, ci: (jnp.where(i // nrb == ci[0], i % nrb, 0), j))
    sib_spec = pl.BlockSpec((tr, tc), lambda i, j, ci: (jnp.where(i // nrb == ci[0], 0, i % nrb), j))
    gs = pltpu.PrefetchScalarGridSpec(num_scalar_prefetch=1, grid=(R // tr, C // tc),
                                      in_specs=[blk, mine_spec, sib_spec, blk, blk], out_specs=[blk] * 4)
    return _pcall(body, name=name, grid_spec=gs, out_shape=[jax.ShapeDtypeStruct((R, C), F32)] * 4,
                  compiler_params=_params(("parallel", "parallel")))(cidx, w, g_mine, g_sib, m, v)


def adam_cond(c_all, dmod, w, m, v, name):
    B, Kin = c_all.shape
    N = w.shape[1]
    tn = 768 if N % 768 == 0 else 256
    assert N % tn == 0

    def body(c_ref, d_ref, w_ref, m_ref, v_ref, g_ref, dl_ref, nm_ref, nv_ref):
        cv = c_ref[...]
        ca = cv * _sigmoid(cv)
        g = lax.dot_general(ca.astype(BF), d_ref[...].astype(BF), CONTRACT_FIRST, preferred_element_type=F32)
        d, nm, nv = _adamw(w_ref[...], g, m_ref[...], v_ref[...])
        g_ref[...] = g
        dl_ref[...] = d
        nm_ref[...] = nm
        nv_ref[...] = nv

    blk = pl.BlockSpec((Kin, tn), lambda n: (0, n))
    return _pcall(
        body, name=name, grid=(N // tn,),
        in_specs=[pl.BlockSpec((B, Kin), lambda n: (0, 0)), pl.BlockSpec((B, tn), lambda n: (0, n)), blk, blk, blk],
        out_specs=[blk] * 4, out_shape=[jax.ShapeDtypeStruct((Kin, N), F32)] * 4,
        compiler_params=_params(("parallel",)),
    )(c_all, dmod, w, m, v)


def adam_small(ws, gs, ms, vs, name):
    n = len(ws)

    def body(*refs):
        ins, outs = refs[:4 * n], refs[4 * n:]
        for k in range(n):
            d, nm, nv = _adamw(ins[k][...], ins[n + k][...], ins[2 * n + k][...], ins[3 * n + k][...])
            outs[k][...] = d
            outs[n + k][...] = nm
            outs[2 * n + k][...] = nv

    specs = [pl.BlockSpec(w.shape, lambda i: (0, 0)) for w in ws]
    shapes = [jax.ShapeDtypeStruct(w.shape, F32) for w in ws]
    out = _pcall(body, name=name, grid=(1,), in_specs=specs * 4, out_specs=specs * 3, out_shape=shapes * 3,
                 compiler_params=_params(("arbitrary",)))(*ws, *gs, *ms, *vs)
    return out[:n], out[n:2 * n], out[2 * n:]


def _me():
    return lax.axis_index("x"), lax.axis_index("y"), lax.axis_index("c")


def _flip(x, y, p):
    return (x ^ (p >> 1) if (p >> 1) else x), (y ^ (p & 1) if (p & 1) else y)


def _handshake(peers):
    barrier = pltpu.get_barrier_semaphore()
    for peer in peers:
        pl.semaphore_signal(barrier, inc=1, device_id=peer, device_id_type=MESH)
    pl.semaphore_wait(barrier, len(peers))


def _seq_call(body, *, name, n_in, out_shape, sem_shapes, collective_id):
    del n_in
    return pl.kernel(body, out_type=out_shape, mesh=plsc.ScalarSubcoreMesh(axis_name="sq", num_cores=1), name=name,
                     scratch_types=sem_shapes, compiler_params=pltpu.CompilerParams(collective_id=collective_id))


def _hbm_comm_call(body, *, name, n_in, out_shape, sem_shapes, seq_id):
    if seq_id is not None:
        return _seq_call(body, name=name, n_in=n_in, out_shape=out_shape, sem_shapes=sem_shapes, collective_id=seq_id)
    anyspec = pl.BlockSpec(memory_space=pl.ANY)
    return _pcall(body, name=name, in_specs=[anyspec] * n_in, out_specs=[anyspec] * len(out_shape), out_shape=out_shape,
                  scratch_shapes=sem_shapes, compiler_params=_params())


def allgather_devices(v, name, with_sum=False):
    R, L = v.shape

    def body(v_ref, out_ref, *rest):
        if with_sum:
            sum_ref, send_sems, recv_sems = rest
        else:
            send_sems, recv_sems = rest
        x, y, c = _me()
        me = 4 * x + 2 * y + c
        out_ref[me] = v_ref[...]
        copies = []
        for p in range(1, N_DEV):
            px, py = _flip(x, y, p >> 1)
            pc = (1 - c) if (p & 1) else c
            peer = 4 * px + 2 * py + pc
            send = pltpu.make_async_remote_copy(src_ref=v_ref, dst_ref=out_ref.at[me], send_sem=send_sems.at[p - 1],
                                                recv_sem=recv_sems.at[p - 1], device_id=(px, py, pc), device_id_type=MESH)
            send.start()
            recv = pltpu.make_async_remote_copy(src_ref=v_ref, dst_ref=out_ref.at[peer], send_sem=send_sems.at[p - 1],
                                                recv_sem=recv_sems.at[p - 1], device_id=(px, py, pc), device_id_type=MESH)
            copies.append((send, recv))
        for send, recv in copies:
            recv.wait_recv()
        for send, recv in copies:
            send.wait_send()
        if with_sum:
            s = out_ref[0]
            for k in range(1, N_DEV):
                s = s + out_ref[k]
            sum_ref[...] = s

    vm = pl.BlockSpec(memory_space=pltpu.VMEM)
    out_shape = [jax.ShapeDtypeStruct((N_DEV, R, L), F32)]
    if with_sum:
        out_shape.append(jax.ShapeDtypeStruct((R, L), F32))
    return _pcall(
        body, name=name, in_specs=[vm], out_specs=[vm] * len(out_shape), out_shape=out_shape,
        scratch_shapes=[pltpu.SemaphoreType.DMA((N_DEV - 1,)), pltpu.SemaphoreType.DMA((N_DEV - 1,))],
        compiler_params=_params(),
    )(v)


def allgather_devices_hbm(v, name, seq_id):
    R, L = v.shape

    def body(v_ref, out_ref, send_sems, recv_sems, local_sem):
        x, y, c = _me()
        me = 4 * x + 2 * y + c
        peers = []
        for p in range(1, N_DEV):
            px, py = _flip(x, y, p >> 1)
            peers.append((px, py, (1 - c) if (p & 1) else c))
        _handshake(peers)
        lc = pltpu.make_async_copy(v_ref, out_ref.at[me], local_sem)
        lc.start()
        copies = []
        for p, (px, py, pc) in enumerate(peers):
            send = pltpu.make_async_remote_copy(src_ref=v_ref, dst_ref=out_ref.at[me], send_sem=send_sems.at[p],
                                                recv_sem=recv_sems.at[p], device_id=(px, py, pc), device_id_type=MESH)
            send.start()
            recv = pltpu.make_async_remote_copy(src_ref=v_ref, dst_ref=out_ref.at[4 * px + 2 * py + pc], send_sem=send_sems.at[p],
                                                recv_sem=recv_sems.at[p], device_id=(px, py, pc), device_id_type=MESH)
            copies.append((send, recv))
        for send, recv in copies:
            recv.wait_recv()
        for send, recv in copies:
            send.wait_send()
        lc.wait()

    return _seq_call(body, name=name, n_in=1, out_shape=[jax.ShapeDtypeStruct((N_DEV, R, L), F32)],
                     sem_shapes=[pltpu.SemaphoreType.DMA((N_DEV - 1,)), pltpu.SemaphoreType.DMA((N_DEV - 1,)),
                                 pltpu.SemaphoreType.DMA], collective_id=seq_id)(v)[0]


def sum_slots(g, name):
    n, R, L = g.shape
    tr = 216 if R % 216 == 0 else R
    assert R % tr == 0 and tr % SUBLANES == 0

    def body(g_ref, o_ref):
        s = g_ref[0]
        for k in range(1, n):
            s = s + g_ref[k]
        o_ref[...] = s

    return _pcall(body, name=name, grid=(R // tr,), in_specs=[pl.BlockSpec((n, tr, L), lambda i: (0, i, 0))],
                  out_specs=pl.BlockSpec((tr, L), lambda i: (i, 0)), out_shape=jax.ShapeDtypeStruct((R, L), F32),
                  compiler_params=_params(("parallel",)))(g)


def allgather_chips(v, name):
    R, L = v.shape

    def body(v_ref, out_ref, send_sems, recv_sems):
        x, y, c = _me()
        chip = 2 * x + y
        out_ref[chip] = v_ref[...]
        copies = []
        for p in range(1, N_CHIPS):
            px, py = _flip(x, y, p)
            send = pltpu.make_async_remote_copy(src_ref=v_ref, dst_ref=out_ref.at[chip], send_sem=send_sems.at[p - 1],
                                                recv_sem=recv_sems.at[p - 1], device_id=(px, py, c), device_id_type=MESH)
            send.start()
            recv = pltpu.make_async_remote_copy(src_ref=v_ref, dst_ref=out_ref.at[2 * px + py], send_sem=send_sems.at[p - 1],
                                                recv_sem=recv_sems.at[p - 1], device_id=(px, py, c), device_id_type=MESH)
            copies.append((send, recv))
        for send, recv in copies:
            recv.wait_recv()
        for send, recv in copies:
            send.wait_send()

    vm = pl.BlockSpec(memory_space=pltpu.VMEM)
    return _pcall(
        body, name=name, in_specs=[vm], out_specs=vm, out_shape=jax.ShapeDtypeStruct((N_CHIPS, R, L), F32),
        scratch_shapes=[pltpu.SemaphoreType.DMA((N_CHIPS - 1,)), pltpu.SemaphoreType.DMA((N_CHIPS - 1,))],
        compiler_params=_params(),
    )(v)


def _shard_window(ref, kind, shard_shape, chip, half):
    r, c = shard_shape
    hr = r // 2
    if kind == "col":
        return ref.at[pl.ds(pl.multiple_of(half * hr, hr), hr), pl.ds(pl.multiple_of(chip * c, c), c)]
    return ref.at[pl.ds(pl.multiple_of(chip * r + half * hr, hr), hr), :]


def allgather_weights(shards, kinds, name, seq_id=None):
    n = len(shards)
    fulls = []
    for s, kind in zip(shards, kinds):
        r, c = s.shape
        fulls.append(jax.ShapeDtypeStruct((r, N_CHIPS * c) if kind == "col" else (N_CHIPS * r, c), s.dtype))

    def body(*refs):
        srcs, outs = refs[:n], refs[n:2 * n]
        send_sems, recv_sems, fsend_sems, frecv_sems = refs[2 * n:]
        x, y, c = _me()
        chip = 2 * x + y
        sib = (x, y, 1 - c)
        if seq_id is not None:
            _handshake([(*_flip(x, y, p), c) for p in range(1, N_CHIPS)] + [sib])
        sends, fwds = [], []
        for i in range(n):
            shp = srcs[i].shape
            hr = shp[0] // 2
            my_half = srcs[i].at[pl.ds(pl.multiple_of(c * hr, hr), hr), :]
            for p in range(1, N_CHIPS):
                px, py = _flip(x, y, p)
                k = i * (N_CHIPS - 1) + p - 1
                cp = pltpu.make_async_remote_copy(src_ref=my_half, dst_ref=_shard_window(outs[i], kinds[i], shp, chip, c),
                                                  send_sem=send_sems.at[k], recv_sem=recv_sems.at[k],
                                                  device_id=(px, py, c), device_id_type=MESH)
                cp.start()
                sends.append(cp)
        for i in range(n):
            shp = srcs[i].shape
            for p in range(1, N_CHIPS):
                px, py = _flip(x, y, p)
                k = i * (N_CHIPS - 1) + p - 1
                landed = _shard_window(outs[i], kinds[i], shp, 2 * px + py, c)
                pltpu.make_async_remote_copy(src_ref=landed, dst_ref=landed, send_sem=send_sems.at[k], recv_sem=recv_sems.at[k],
                                             device_id=(px, py, c), device_id_type=MESH).wait_recv()
                fw = pltpu.make_async_remote_copy(src_ref=landed, dst_ref=landed, send_sem=fsend_sems.at[k],
                                                  recv_sem=frecv_sems.at[k], device_id=sib, device_id_type=MESH)
                fw.start()
                fwds.append(fw)
        for i in range(n):
            shp = srcs[i].shape
            for p in range(1, N_CHIPS):
                px, py = _flip(x, y, p)
                k = i * (N_CHIPS - 1) + p - 1
                other = _shard_window(outs[i], kinds[i], shp, 2 * px + py, 1 - c)
                pltpu.make_async_remote_copy(src_ref=other, dst_ref=other, send_sem=fsend_sems.at[k], recv_sem=frecv_sems.at[k],
                                             device_id=sib, device_id_type=MESH).wait_recv()
        for cp in sends + fwds:
            cp.wait_send()

    nk = n * (N_CHIPS - 1)
    gathered = _hbm_comm_call(
        body, name=name, n_in=n, out_shape=fulls, seq_id=seq_id,
        sem_shapes=[pltpu.SemaphoreType.DMA((nk,)), pltpu.SemaphoreType.DMA((nk,)), pltpu.SemaphoreType.DMA((nk,)),
                    pltpu.SemaphoreType.DMA((nk,))],
    )(*shards)
    return gathered


def place_local_shards(fulls, shards, kinds, name):
    n = len(shards)
    chip = jnp.reshape(2 * lax.axis_index("x") + lax.axis_index("y"), (1,)).astype(I32)

    def body(ci_ref, *refs):
        for i in range(n):
            refs[2 * n + i][...] = refs[i][...]

    in_specs = [pl.BlockSpec(s.shape, lambda i, ci: (0, 0)) for s in shards] + [pl.BlockSpec(memory_space=pl.ANY)] * n
    out_specs = [pl.BlockSpec(s.shape, (lambda i, ci: (0, ci[0])) if k == "col" else (lambda i, ci: (ci[0], 0)))
                 for s, k in zip(shards, kinds)]
    gs = pltpu.PrefetchScalarGridSpec(num_scalar_prefetch=1, grid=(1,), in_specs=in_specs, out_specs=out_specs)
    return _pcall(body, name=name, grid_spec=gs, out_shape=[jax.ShapeDtypeStruct(f.shape, f.dtype) for f in fulls],
                  input_output_aliases={1 + n + i: i for i in range(n)}, compiler_params=_params(("arbitrary",)))(chip, *shards, *fulls)


def _as_halves(g, kind, shard_shape):
    r, c = shard_shape
    if kind == "col":
        return g.reshape(2, r // 2, N_CHIPS * c)
    return g.reshape(N_CHIPS, 2, r // 2, c)


def exchange_sibling_halves(grads, kinds, shard_shapes, name, seq_id=None):
    n = len(grads)
    views = [_as_halves(g, k, s) for g, k, s in zip(grads, kinds, shard_shapes)]
    outs = []
    for k, (r, c) in zip(kinds, shard_shapes):
        outs.append(jax.ShapeDtypeStruct((r // 2, N_CHIPS * c) if k == "col" else (N_CHIPS, r // 2, c), grads[0].dtype))

    def body(*refs):
        srcs, dsts = refs[:n], refs[n:2 * n]
        send_sems, recv_sems = refs[2 * n:]
        x, y, c = _me()
        if seq_id is not None:
            _handshake([(x, y, 1 - c)])
        cps = []
        for i in range(n):
            src = srcs[i].at[1 - c] if kinds[i] == "col" else srcs[i].at[:, 1 - c]
            cp = pltpu.make_async_remote_copy(src_ref=src, dst_ref=dsts[i], send_sem=send_sems.at[i], recv_sem=recv_sems.at[i],
                                              device_id=(x, y, 1 - c), device_id_type=MESH)
            cp.start()
            cps.append(cp)
        for cp in cps:
            cp.wait_recv()
        for cp in cps:
            cp.wait_send()

    return _hbm_comm_call(body, name=name, n_in=n, out_shape=outs, seq_id=seq_id,
                          sem_shapes=[pltpu.SemaphoreType.DMA((n,)), pltpu.SemaphoreType.DMA((n,))])(*views)


def add_sibling_half(g, recv, kind, shard_shape, core_chip, name):
    r, c = shard_shape
    hr = r // 2
    gv = _as_halves(g, kind, shard_shape)
    tr = hr if hr <= 512 else (256 if hr % 256 == 0 else hr // 2)
    assert hr % tr == 0

    def body(ci_ref, g_ref, r_ref, h_ref, hb_ref):
        s = g_ref[...].astype(F32) + r_ref[...].astype(F32)
        hb_ref[...] = s.astype(BF)

        @pl.when(pl.program_id(1) == ci_ref[1])
        def _():
            h_ref[...] = s

    grid = (hr // tr, N_CHIPS)
    if kind == "col":
        g_spec = pl.BlockSpec((None, tr, c), lambda i, k, ci: (ci[0], i, k))
        o_spec = pl.BlockSpec((tr, c), lambda i, k, ci: (i, k))
    else:
        g_spec = pl.BlockSpec((None, None, tr, c), lambda i, k, ci: (k, ci[0], i, 0))
        o_spec = pl.BlockSpec((None, tr, c), lambda i, k, ci: (k, i, 0))
    own_spec = pl.BlockSpec((tr, c), lambda i, k, ci: (i, 0))
    gs = pltpu.PrefetchScalarGridSpec(num_scalar_prefetch=1, grid=grid, in_specs=[g_spec, o_spec], out_specs=[own_spec, o_spec])
    return _pcall(
        body, name=name, grid_spec=gs,
        out_shape=[jax.ShapeDtypeStruct((hr, c), F32), jax.ShapeDtypeStruct(recv.shape, BF)],
        compiler_params=_params(("parallel", "arbitrary")),
    )(core_chip, gv, recv)


def exchange_chip_pieces(hbs, kinds, shard_shapes, name, seq_id=None):
    n = len(hbs)
    outs = [jax.ShapeDtypeStruct((N_CHIPS - 1, r // 2, c), BF) for (r, c) in shard_shapes]

    def body(*refs):
        srcs, dsts = refs[:n], refs[n:2 * n]
        send_sems, recv_sems = refs[2 * n:]
        x, y, c = _me()
        if seq_id is not None:
            _handshake([(*_flip(x, y, p), c) for p in range(1, N_CHIPS)])
        cps = []
        for i in range(n):
            cc = shard_shapes[i][1]
            for p in range(1, N_CHIPS):
                px, py = _flip(x, y, p)
                pchip = 2 * px + py
                src = (srcs[i].at[:, pl.ds(pl.multiple_of(pchip * cc, cc), cc)] if kinds[i] == "col" else srcs[i].at[pchip])
                k = i * (N_CHIPS - 1) + p - 1
                cp = pltpu.make_async_remote_copy(src_ref=src, dst_ref=dsts[i].at[p - 1], send_sem=send_sems.at[k],
                                                  recv_sem=recv_sems.at[k], device_id=(px, py, c), device_id_type=MESH)
                cp.start()
                cps.append(cp)
        for cp in cps:
            cp.wait_recv()
        for cp in cps:
            cp.wait_send()

    nk = n * (N_CHIPS - 1)
    return _hbm_comm_call(body, name=name, n_in=n, out_shape=outs, seq_id=seq_id,
                          sem_shapes=[pltpu.SemaphoreType.DMA((nk,)), pltpu.SemaphoreType.DMA((nk,))])(*hbs)


def sum_chip_pieces(h_own, pieces, name):
    hr, c = h_own.shape
    tr = hr if hr <= 512 else (256 if hr % 256 == 0 else hr // 2)
    assert hr % tr == 0

    def body(h_ref, p_ref, q_ref):
        q_ref[...] = ((h_ref[...] + p_ref[0].astype(F32)) + p_ref[1].astype(F32)) + p_ref[2].astype(F32)

    blk = pl.BlockSpec((tr, c), lambda i: (i, 0))
    return _pcall(body, name=name, grid=(hr // tr,), in_specs=[blk, pl.BlockSpec((N_CHIPS - 1, tr, c), lambda i: (0, i, 0))],
                  out_specs=blk, out_shape=jax.ShapeDtypeStruct((hr, c), F32), compiler_params=_params(("parallel",)))(h_own, pieces)


def exchange_reduced_halves(qs, name, seq_id):
    n = len(qs)

    def body(*refs):
        srcs, dsts = refs[:n], refs[n:2 * n]
        send_sems, recv_sems = refs[2 * n:]
        x, y, c = _me()
        _handshake([(x, y, 1 - c)])
        cps = []
        for i in range(n):
            cp = pltpu.make_async_remote_copy(src_ref=srcs[i], dst_ref=dsts[i], send_sem=send_sems.at[i], recv_sem=recv_sems.at[i],
                                              device_id=(x, y, 1 - c), device_id_type=MESH)
            cp.start()
            cps.append(cp)
        for cp in cps:
            cp.wait_recv()
        for cp in cps:
            cp.wait_send()

    return _seq_call(body, name=name, n_in=n, out_shape=[jax.ShapeDtypeStruct(q.shape, F32) for q in qs],
                     sem_shapes=[pltpu.SemaphoreType.DMA((n,)), pltpu.SemaphoreType.DMA((n,))], collective_id=seq_id)(*qs)


def _rows128(a):
    return a.reshape(-1, LANES)


def _after(xs, *deps):
    flat = []
    for d in deps:
        flat.extend(d if isinstance(d, (list, tuple)) else [d])
    return list(lax.optimization_barrier((tuple(xs), tuple(flat)))[0])


def _block_diag(w):
    H, d, _ = w.shape
    eye = jnp.eye(H, dtype=w.dtype)
    return jnp.einsum("hde,hg->hdge", w, eye).reshape(H * d, H * d)


def _diag_blocks(g4, H, d):
    nb = g4.shape[0]
    per = LANES // d
    g = g4.reshape(nb, per, d, per, d)
    return jnp.stack([g[:, j, :, j, :] for j in range(per)], axis=1).reshape(H, d, d)


def kernel(x, c, w_mod, b_mod, g_ffn1, w_ffn1_in, w_ffn1_out, g_mix, w_in, conv_w, conv_b, ln_g, ln_b, rnn_conv_w, rnn_conv_b, w_a, b_a, w_i, b_i, lru_lambda, w_out, g_ffn2, w_ffn2_in, w_ffn2_out, w_fmod, b_fmod, g_final, loss_target, m_w_mod, m_b_mod, m_g_ffn1, m_w_ffn1_in, m_w_ffn1_out, m_g_mix, m_w_in, m_conv_w, m_conv_b, m_ln_g, m_ln_b, m_rnn_conv_w, m_rnn_conv_b, m_w_a, m_b_a, m_w_i, m_b_i, m_lru_lambda, m_w_out, m_g_ffn2, m_w_ffn2_in, m_w_ffn2_out, m_w_fmod, m_b_fmod, m_g_final, v_w_mod, v_b_mod, v_g_ffn1, v_w_ffn1_in, v_w_ffn1_out, v_g_mix, v_w_in, v_conv_w, v_conv_b, v_ln_g, v_ln_b, v_rnn_conv_w, v_rnn_conv_b, v_w_a, v_b_a, v_w_i, v_b_i, v_lru_lambda, v_w_out, v_g_ffn2, v_w_ffn2_in, v_w_ffn2_out, v_w_fmod, v_b_fmod, v_g_final):
    S, D = x.shape[1], x.shape[2]
    M = conv_b.shape[1]
    H, HD = w_a.shape[1], w_a.shape[2]
    nb = M // LANES
    ix, iy, ic = lax.axis_index("x"), lax.axis_index("y"), lax.axis_index("c")
    chip = 2 * ix + iy
    dev = 2 * chip + ic
    core_chip = jnp.stack([ic, chip]).astype(I32)
    cidx = core_chip
    xs = x[0]
    tgt = loss_target[0]

    kinds = ["col", "row"]
    w_f1, w_mx, w_f2 = [w_ffn1_in[0], w_ffn1_out[0]], [w_in[0], w_out[0]], [w_ffn2_in[0], w_ffn2_out[0]]
    as_bf = lambda ws: [w.astype(BF) for w in ws]
    shapes_of = lambda ws: [w.shape for w in ws]
    b_f1, b_mx, b_f2 = as_bf(w_f1), as_bf(w_mx), as_bf(w_f2)
    got_f1i = allgather_weights(b_f1[:1], kinds[:1], "gather_ffn1_in", seq_id=9)
    got_f1o = allgather_weights(b_f1[1:], kinds[1:], "gather_ffn1_out", seq_id=13)
    got_mx = allgather_weights(b_mx, kinds, "gather_mix", seq_id=1)
    got_f2 = allgather_weights(b_f2, kinds, "gather_ffn2", seq_id=2)

    c_all =allgather_devices(_rows128(c), "gather_c")[0].reshape(N_DEV, D)
    mod_cols = cond_matmul(c_all, w_mod[0], "mod_proj")
    fmod_cols = cond_matmul(c_all, w_fmod, "fmod_proj")
    convw_pad = jnp.pad(conv_w[0], ((0, CONV_TAPS - CONV_WIDTH), (0, 0)))
    rnnw_pad = jnp.pad(rnn_conv_w[0], ((0, SUBLANES - RNN_CONV_WIDTH), (0, 0)))
    n_mod, n_fmod = mod_cols.shape[1], fmod_cols.shape[1]
    small = jnp.concatenate([_rows128(mod_cols), _rows128(fmod_cols), convw_pad, rnnw_pad], axis=0)
    small4 = allgather_chips(small, "gather_cond")
    r0 = N_DEV * n_mod // LANES
    r1 = r0 + N_DEV * n_fmod // LANES
    mod_all = small4[:, :r0].reshape(N_CHIPS, N_DEV, n_mod)
    fmod_all = small4[:, r0:r1].reshape(N_CHIPS, N_DEV, n_fmod)
    convw4 = small4[:, r1:r1 + CONV_TAPS]
    rnnw4 = small4[:, r1 + CONV_TAPS:r1 + CONV_TAPS + SUBLANES]
    mod_row = lax.dynamic_index_in_dim(mod_all, dev, axis=1, keepdims=False).reshape(1, N_CHIPS * n_mod) + b_mod
    fmod_row = lax.dynamic_index_in_dim(fmod_all, dev, axis=1, keepdims=False).reshape(1, N_CHIPS * n_fmod) + b_fmod[None, :]
    vecs = jnp.concatenate([mod_row.reshape(9, D), fmod_row.reshape(2, D), g_ffn1, g_mix, g_ffn2, g_final[None, :],
                            jnp.zeros((1, D), F32)], axis=0)
    lnv = jnp.concatenate([ln_g, ln_b, jnp.zeros((SUBLANES - 2, M), F32)], axis=0)
    bda = _block_diag(w_a[0]).astype(BF)
    bdi = _block_diag(w_i[0]).astype(BF)

    def reduce_add(gs, recv, ws, tag, kinds_=kinds):
        pairs = [add_sibling_half(g, r_, k, w.shape, core_chip, f"add_sibling_{tag}{j}")
                 for j, (g, r_, k, w) in enumerate(zip(gs, recv, kinds_, ws))]
        return [p[0] for p in pairs], [p[1] for p in pairs]

    def reduce_sum(hs_, recv, ws, tag, kinds_=kinds):
        return [sum_chip_pieces(h_, p_, f"sum_chips_{tag}{j}") for j, (h_, p_) in enumerate(zip(hs_, recv))]

    rows1 = (R_SH1, R_SC1, R_GT1, R_G1)
    rows3 = (R_SH3, R_SC3, R_GT3, R_G3)
    (wi1,) = place_local_shards(got_f1i, b_f1[:1], kinds[:1], "place_ffn1_in")
    g1s, u1s, a1s = ffn_fwd_in(xs, vecs, wi1, rows1, "ffn1_fwd_in")
    (wo1,) = place_local_shards(_after(got_f1o, a1s), b_f1[1:], kinds[1:], "place_ffn1_out")
    x1, y1 = ffn_fwd_out(a1s, xs, vecs, wo1, rows1, "ffn1_fwd_out")
    win, wout = place_local_shards(_after(got_mx, x1), b_mx, kinds, "place_mix")
    proj = norm_matmul(x1, vecs, win, (R_SH2, R_SC2, R_G2), "mix_in_proj")
    cq = conv_fwd(proj, convw4, conv_b, "conv_fwd")
    xr, ra, ii, hh = rnn_fwd(proj, rnnw4, rnn_conv_b, bda, bdi, b_a, b_i, lru_lambda, "rnn_fwd")
    x2, ym, ycat = mix_out(cq, proj, hh, x1, vecs, lnv, wout, "mix_out")
    wi2, wo2 = place_local_shards(_after(got_f2, x2), b_f2, kinds, "place_ffn2")
    dx3, g2s, u2s, y2, vgf = ffn_fwd(x2, vecs, wi2, wo2, rows3, "ffn2_fwd", final_tgt=tgt)

    Fd = wo1.shape[0]
    tk = S
    dx2, act2, dg2, du2, h3b, dy2b, vg3 = ffn_bwd(dx3, x2, vecs, g2s, u2s, y2, wi2, wo2, rows3, "ffn2_bwd")
    gwo2 = matmul(act2, dy2b, "tn", tm=DW_TILE, tn=D, tk=tk, out_dtype=BF, name="ffn2_dwo")
    gwi2 = matmul(h3b, dg2, "tn", tm=D, tn=DW_TILE, tk=tk, out_dtype=BF, name="ffn2_dwg", out_cols=2 * Fd)
    gwi2 = matmul(h3b, du2, "tn", tm=D, tn=DW_TILE, tk=tk, out_dtype=BF, name="ffn2_dwu", out_cols=2 * Fd, col_off=Fd, prev=gwi2)
    recv1_f2 = exchange_sibling_halves([gwi2, gwo2], kinds, shapes_of(w_f2), "reduce1_ffn2", seq_id=3)
    dcq, dhout, duy, dymb, vgd, vgm = mix_out_bwd(dx2, ym, vecs, wout, cq, lnv, proj, hh, "mix_out_bwd")
    gwout = matmul(ycat, dymb, "tn", tm=2 * M, tn=DW_TILE, tk=tk, out_dtype=BF, name="mix_dwout")
    recv1_f2 = _after(recv1_f2, gwout)
    h_f2, hb_f2 = reduce_add([gwi2, gwo2], recv1_f2, w_f2, "ffn2_")
    recv2_f2 = exchange_chip_pieces(hb_f2, kinds, shapes_of(w_f2), "reduce2_ffn2", seq_id=4)
    duv, dug, dconvw4, dconvb = conv_bwd(_after([dcq], hb_f2)[0], proj, convw4, "conv_bwd")
    dux, dwa4, dwi4, drnnw4, rvec = rnn_bwd(dhout, hh, xr, ra, ii, proj, rnnw4, bda, bdi, lru_lambda, "rnn_bwd")
    dx1, h2b, dpb, vg2 = mix_in_bwd((duv, dug, dux, duy), x1, dx2, vecs, win, "mix_in_bwd")
    gwin = matmul(h2b, dpb, "tn", tm=D, tn=DW_TILE, tk=tk, out_dtype=BF, name="mix_dwin")
    recv1_mx = exchange_sibling_halves([gwin, gwout], kinds, shapes_of(w_mx), "reduce1_mix", seq_id=5)
    q_f2 = reduce_sum(_after(h_f2, gwin), recv2_f2, w_f2, "ffn2_")
    r_f2 = exchange_reduced_halves(q_f2, "reduce3_ffn2", seq_id=14)
    h_mx, hb_mx = reduce_add([gwin, gwout], _after(recv1_mx, q_f2), w_mx, "mix_")
    recv2_mx = exchange_chip_pieces(hb_mx, kinds, shapes_of(w_mx), "reduce2_mix", seq_id=6)
    dx0, act1, dg1, du1, h1b, dy1b, vg1 = ffn_bwd(_after([dx1], hb_mx)[0], xs, vecs, g1s, u1s, y1, wi1, wo1, rows1, "ffn1_bwd")
    dmod_row = jnp.concatenate([vg1[1:3], vg1[0:1], vg2[0:2], vgd[0:1], vg3[1:3], vg3[0:1]], axis=0)
    gains = jnp.concatenate([vg1[3:4], vg2[2:3], vg3[3:4], vgf[2:4]], axis=0)
    mvecs = jnp.concatenate([dconvb, vgm[0:2], rvec[0:4], jnp.zeros((1, M), F32)], axis=0)
    parts = [_rows128(dmod_row), _rows128(vgf[0:2]), _rows128(gains), _rows128(mvecs),
             _rows128(dconvw4), _rows128(drnnw4), _rows128(_diag_blocks(dwa4, H, HD)), _rows128(_diag_blocks(dwi4, H, HD))]
    sizes = [p.shape[0] for p in parts]
    packed = jnp.concatenate(parts, axis=0)
    gathered = allgather_devices_hbm(packed, "gather_small", seq_id=10)

    gwo1 = matmul(_after([act1], recv2_mx, packed)[0], dy1b, "tn", tm=DW_TILE, tn=D, tk=tk, out_dtype=BF, name="ffn1_dwo")
    w_f1o, w_f1i = w_f1[1:], w_f1[:1]
    recv1_f1o = exchange_sibling_halves([gwo1], ["row"], shapes_of(w_f1o), "reduce1_ffn1_out", seq_id=7)
    q_mx = reduce_sum(_after(h_mx, gwo1), recv2_mx, w_mx, "mix_")
    r_mx = exchange_reduced_halves(q_mx, "reduce3_mix", seq_id=15)
    gwi1 = matmul(_after([h1b], q_mx)[0], dg1, "tn", tm=D, tn=DW_TILE, tk=tk, out_dtype=BF, name="ffn1_dwg", out_cols=2 * Fd)
    h_f1o, hb_f1o = reduce_add([gwo1], _after(recv1_f1o, gwi1), w_f1o, "ffn1_out", ["row"])
    recv2_f1o = exchange_chip_pieces(hb_f1o, ["row"], shapes_of(w_f1o), "reduce2_ffn1_out", seq_id=11)
    gwi1 = matmul(h1b, _after([du1], hb_f1o, gathered)[0], "tn", tm=D, tn=DW_TILE, tk=tk, out_dtype=BF, name="ffn1_dwu", out_cols=2 * Fd,
                  col_off=Fd, prev=gwi1)
    recv1_f1i = exchange_sibling_halves([gwi1], ["col"], shapes_of(w_f1i), "reduce1_ffn1_in", seq_id=12)
    q_f1o = reduce_sum(_after(h_f1o, gwi1), recv2_f1o, w_f1o, "ffn1_out", ["row"])
    r_f1o = exchange_reduced_halves(q_f1o, "reduce3_ffn1_out", seq_id=16)
    summed = sum_slots(gathered, "sum_small")
    offs = [0]
    for s in sizes:
        offs.append(offs[-1] + s)
    seg = lambda k: summed[offs[k]:offs[k + 1]]
    g_b_mod = seg(0).reshape(1, 9 * D)
    g_b_fmod = seg(1).reshape(1, 2 * D)
    gsum = seg(2).reshape(5, D)
    loss = (0.5 / D) * jnp.sum(gsum[4])
    msum = seg(3).reshape(SUBLANES, M)
    g_conv_w = lax.dynamic_index_in_dim(seg(4).reshape(nb, CONV_TAPS, LANES), chip, axis=0, keepdims=False)[:CONV_WIDTH]
    g_rnn_w = lax.dynamic_index_in_dim(seg(5).reshape(nb, SUBLANES, LANES), chip, axis=0, keepdims=False)[:RNN_CONV_WIDTH]
    g_w_a = seg(6).reshape(H, HD, HD)
    g_w_i = seg(7).reshape(H, HD, HD)
    dmod_all = gathered[:, offs[0]:offs[1]].reshape(N_DEV, 9 * D)
    dfmod_all = gathered[:, offs[1]:offs[2]].reshape(N_DEV, 2 * D)
    dmod_cols = lax.dynamic_slice_in_dim(dmod_all, chip * n_mod, n_mod, axis=1)
    dfmod_cols = lax.dynamic_slice_in_dim(dfmod_all, chip * n_fmod, n_fmod, axis=1)

    h_f1i, hb_f1i = reduce_add([gwi1], _after(recv1_f1i, q_f1o), w_f1i, "ffn1_in", ["col"])
    recv2_f1i = exchange_chip_pieces(hb_f1i, ["col"], shapes_of(w_f1i), "reduce2_ffn1_in", seq_id=8)
    dmod_cols, dfmod_cols = _after([dmod_cols, dfmod_cols], hb_f1i)
    g_w_mod, d_w_mod, nm_w_mod, nv_w_mod = adam_cond(c_all, dmod_cols, w_mod[0], m_w_mod[0], v_w_mod[0], "adam_w_mod")
    g_w_fmod, d_w_fmod, nm_w_fmod, nv_w_fmod = adam_cond(c_all, dfmod_cols, w_fmod, m_w_fmod, v_w_fmod, "adam_w_fmod")

    def adam_group(ws, qs_, rs_, ms, vs, tags, after):
        qs_ = _after(list(qs_), *after) if after else list(qs_)
        return [adam_big(w, q_, r_, m, v, cidx, "adam_" + t) for w, q_, r_, m, v, t in zip(ws, qs_, rs_, ms, vs, tags)]

    ad_f2 = adam_group(w_f2, q_f2, r_f2, [m_w_ffn2_in[0], m_w_ffn2_out[0]], [v_w_ffn2_in[0], v_w_ffn2_out[0]],
                       ["ffn2_in", "ffn2_out"], [hb_f1i])
    ad_mx = adam_group(w_mx, q_mx, r_mx, [m_w_in[0], m_w_out[0]], [v_w_in[0], v_w_out[0]], ["w_in", "w_out"], [hb_f1i])
    ad_f1o = adam_group(w_f1o, q_f1o, r_f1o, [m_w_ffn1_out[0]], [v_w_ffn1_out[0]], ["ffn1_out"], [hb_f1i])
    q_f1i = reduce_sum(_after(h_f1i, ad_f2[0][0], ad_f2[1][0], ad_mx[0][0], ad_mx[1][0], ad_f1o[0][0], g_w_mod, g_w_fmod),
                       recv2_f1i, w_f1i, "ffn1_in", ["col"])
    r_f1i = exchange_reduced_halves(q_f1i, "reduce3_ffn1_in", seq_id=17)
    ad_f1i = adam_group(w_f1i, q_f1i, r_f1i, [m_w_ffn1_in[0]], [v_w_ffn1_in[0]], ["ffn1_in"], [])
    big_out = ad_f1i + ad_f1o + ad_mx + ad_f2

    flat2 = lambda a: a.reshape(-1, a.shape[-1])
    small_names = ["b_mod", "g_ffn1", "g_mix", "conv_w", "conv_b", "ln_g", "ln_b", "rnn_conv_w", "rnn_conv_b", "w_a", "b_a",
                   "w_i", "b_i", "lru_lambda", "g_ffn2", "b_fmod", "g_final"]
    small_w = [b_mod, g_ffn1, g_mix, conv_w, conv_b, ln_g, ln_b, rnn_conv_w, rnn_conv_b, w_a, b_a, w_i, b_i, lru_lambda,
               g_ffn2, b_fmod, g_final]
    small_m = [m_b_mod, m_g_ffn1, m_g_mix, m_conv_w, m_conv_b, m_ln_g, m_ln_b, m_rnn_conv_w, m_rnn_conv_b, m_w_a, m_b_a,
               m_w_i, m_b_i, m_lru_lambda, m_g_ffn2, m_b_fmod, m_g_final]
    small_v = [v_b_mod, v_g_ffn1, v_g_mix, v_conv_w, v_conv_b, v_ln_g, v_ln_b, v_rnn_conv_w, v_rnn_conv_b, v_w_a, v_b_a,
               v_w_i, v_b_i, v_lru_lambda, v_g_ffn2, v_b_fmod, v_g_final]
    small_g = [g_b_mod, gsum[0:1], gsum[1:2], g_conv_w, msum[0:1], msum[1:2], msum[2:3], g_rnn_w, msum[3:4], g_w_a, msum[4:5],
               g_w_i, msum[5:6], msum[6:7], gsum[2:3], g_b_fmod, gsum[3:4]]
    small_g = [g.reshape(w.shape) for g, w in zip(small_g, small_w)]
    two_d = lambda a: a.reshape(1, -1) if a.ndim == 1 else flat2(a)
    sd, sm, sv = adam_small([two_d(a) for a in small_w], [two_d(a) for a in small_g], [two_d(a) for a in small_m],
                            [two_d(a) for a in small_v], "adam_small")
    small = {}
    for k, nm in enumerate(small_names):
        shp = small_w[k].shape
        small[nm] = (small_g[k], sd[k].reshape(shp), sm[k].reshape(shp), sv[k].reshape(shp))

    big = {"w_mod": tuple(a[None] for a in (g_w_mod, d_w_mod, nm_w_mod, nv_w_mod)),
           "w_fmod": (g_w_fmod, d_w_fmod, nm_w_fmod, nv_w_fmod)}
    for nm, res in zip(["w_ffn1_in", "w_ffn1_out", "w_in", "w_out", "w_ffn2_in", "w_ffn2_out"], big_out):
        big[nm] = tuple(a[None] for a in res)
    order = ["w_mod", "b_mod", "g_ffn1", "w_ffn1_in", "w_ffn1_out", "g_mix", "w_in", "conv_w", "conv_b", "ln_g", "ln_b",
             "rnn_conv_w", "rnn_conv_b", "w_a", "b_a", "w_i", "b_i", "lru_lambda", "w_out", "g_ffn2", "w_ffn2_in",
             "w_ffn2_out", "w_fmod", "b_fmod", "g_final"]
    table = {**small, **big}
    outs = [loss, dx0[None]]
    for kind_ in range(4):
        outs.extend(table[nm][kind_] for nm in order)
    return tuple(outs)
```

```python
import functools

import jax
import jax.numpy as jnp
from jax import lax
from jax.experimental import pallas as pl
from jax.experimental.pallas import tpu as pltpu
from jax.experimental.pallas import tpu_sc as plsc

F32 = jnp.float32
BF = jnp.bfloat16
I32 = jnp.int32
MESH = pl.DeviceIdType.MESH

EPS = 1e-6
RG_C = 8.0
MACARON_W = 0.5
CONV_WIDTH = 31
RNN_CONV_WIDTH = 4
ADAM_LR = 0.001
ADAM_B1 = 0.9
ADAM_B2 = 0.999
ADAM_EPS = 1e-08
ADAM_WD = 0.01
ADAM_STEP = 10

LANES = 128
SUBLANES = 8
VMEM_LIMIT = 62 * 1024 * 1024
N_CHIPS = 4
N_DEV = 8

R_SH1, R_SC1, R_GT1, R_SH2, R_SC2, R_GT2, R_SH3, R_SC3, R_GT3, R_FSH, R_FSC, R_G1, R_G2, R_G3, R_GF = range(15)

CONTRACT_LAST = (((1,), (1,)), ((), ()))
CONTRACT_FIRST = (((0,), (0,)), ((), ()))


def _pcall(body, **kw):
    return pl.pallas_call(body, **kw)


def _params(sem=None, vmem=VMEM_LIMIT):
    if sem is None:
        return pltpu.CompilerParams(vmem_limit_bytes=vmem)
    return pltpu.CompilerParams(dimension_semantics=sem, vmem_limit_bytes=vmem)


def _row(ref, r):
    return ref[r:r + 1, :]


def _sigmoid(x):
    return 0.5 * jnp.tanh(0.5 * x) + 0.5


def _colsum(x):
    return jnp.sum(x, axis=0, keepdims=True)


def _rowmean(x):
    return jnp.mean(x, axis=-1, keepdims=True)


def matmul(a, b, mode, *, tm, tn, tk, name, out_dtype=F32, out_cols=None, col_off=0, prev=None):
    if mode == "nn":
        (M, K), (K2, N) = a.shape, b.shape
    elif mode == "nt":
        (M, K), (N, K2) = a.shape, b.shape
    else:
        (K, M), (K2, N) = a.shape, b.shape
    assert K == K2 and M % tm == 0 and N % tn == 0 and K % tk == 0 and col_off % tn == 0
    nk = K // tk
    out_cols = N if out_cols is None else out_cols
    off = col_off // tn

    def body(*refs):
        if prev is None:
            a_ref, b_ref, o_ref, acc = refs
        else:
            a_ref, b_ref, _, o_ref, acc = refs
        k = pl.program_id(2)
        av = a_ref[...].astype(BF)
        bv = b_ref[...].astype(BF)
        if mode == "nn":
            part = jnp.dot(av, bv, preferred_element_type=F32)
        elif mode == "nt":
            part = lax.dot_general(av, bv, CONTRACT_LAST, preferred_element_type=F32)
        else:
            part = lax.dot_general(av, bv, CONTRACT_FIRST, preferred_element_type=F32)
        if nk == 1:
            o_ref[...] = part.astype(out_dtype)
            return

        @pl.when(k == 0)
        def _():
            acc[...] = part

        @pl.when((k > 0) & (k < nk - 1))
        def _():
            acc[...] += part

        @pl.when(k == nk - 1)
        def _():
            o_ref[...] = (acc[...] + part).astype(out_dtype)

    if mode == "nn":
        a_spec = pl.BlockSpec((tm, tk), lambda m, n, k: (m, k))
        b_spec = pl.BlockSpec((tk, tn), lambda m, n, k: (k, n))
    elif mode == "nt":
        a_spec = pl.BlockSpec((tm, tk), lambda m, n, k: (m, k))
        b_spec = pl.BlockSpec((tn, tk), lambda m, n, k: (n, k))
    else:
        a_spec = pl.BlockSpec((tk, tm), lambda m, n, k: (k, m))
        b_spec = pl.BlockSpec((tk, tn), lambda m, n, k: (k, n))
    in_specs = [a_spec, b_spec]
    args = [a, b]
    aliases = {}
    if prev is not None:
        in_specs.append(pl.BlockSpec(memory_space=pl.ANY))
        args.append(prev)
        aliases = {2: 0}
    return _pcall(
        body, name=name, grid=(M // tm, N // tn, nk), in_specs=in_specs,
        out_specs=pl.BlockSpec((tm, tn), lambda m, n, k: (m, n + off)),
        out_shape=jax.ShapeDtypeStruct((M, out_cols), out_dtype),
        scratch_shapes=[pltpu.VMEM((tm, tn), F32)], input_output_aliases=aliases,
        compiler_params=_params(("parallel", "parallel", "arbitrary")),
    )(*args)


def cond_matmul(c_all, w, name):
    B, K = c_all.shape
    N = w.shape[1]
    tn = 256
    assert N % tn == 0

    def body(c_ref, w_ref, o_ref):
        cv = c_ref[...]
        ca = cv * _sigmoid(cv)
        o_ref[...] = jnp.dot(ca.astype(BF), w_ref[...].astype(BF), preferred_element_type=F32)

    return _pcall(
        body, name=name, grid=(N // tn,),
        in_specs=[pl.BlockSpec((B, K), lambda n: (0, 0)), pl.BlockSpec((K, tn), lambda n: (0, n))],
        out_specs=pl.BlockSpec((B, tn), lambda n: (0, n)),
        out_shape=jax.ShapeDtypeStruct((B, N), F32), compiler_params=_params(("parallel",)),
    )(c_all, w)


FFN_FWD_TS = 512
FFN_BWD_TS = 256
DW_TILE = 256


def _resident(shape, index_map):
    return pl.BlockSpec(shape, index_map, pipeline_mode=pl.Buffered(1))


def _streamed(shape, index_map):
    return pl.BlockSpec(shape, index_map)


def _final_norm_loss_grad(xv, t, v_ref, vg_ref):
    D = xv.shape[-1]
    r = lax.rsqrt(_rowmean(xv * xv) + EPS)
    n = xv * r
    g = _row(v_ref, R_GF)
    sc1 = 1.0 + _row(v_ref, R_FSC)
    gsc = g * sc1
    e = n * gsc + _row(v_ref, R_FSH) - t
    vg_ref[3:4, :] += _colsum(e * e)
    dout = e * (1.0 / D)
    dn_ = dout * n
    vg_ref[0:1, :] += _colsum(dout)
    vg_ref[1:2, :] += _colsum(dn_) * g
    vg_ref[2:3, :] += _colsum(dn_) * sc1
    dn = dout * gsc
    return r * (dn - n * _rowmean(dn * n))


def ffn_fwd(x, vecs, wi, wo, rows, name, final_tgt=None):
    r_sh, r_sc, r_gt, r_g = rows
    S, D = x.shape
    Fd = wo.shape[0]
    ts = min(FFN_FWD_TS, S)
    with_final = final_tgt is not None

    def body(*refs):
        if with_final:
            x_ref, v_ref, wg_ref, wu_ref, wo_ref, t_ref, xo_ref, g_ref, u_ref, y_ref, vg_ref = refs
        else:
            x_ref, v_ref, wg_ref, wu_ref, wo_ref, xo_ref, g_ref, u_ref, y_ref = refs
        xv = x_ref[...]
        r = lax.rsqrt(_rowmean(xv * xv) + EPS)
        gs = _row(v_ref, r_g) * (1.0 + _row(v_ref, r_sc))
        hb = (xv * r * gs + _row(v_ref, r_sh)).astype(BF)
        G = jnp.dot(hb, wg_ref[...], preferred_element_type=F32)
        U = jnp.dot(hb, wu_ref[...], preferred_element_type=F32)
        g_ref[...] = G.astype(BF)
        u_ref[...] = U.astype(BF)
        act = (G * _sigmoid(G) * U).astype(BF)
        Y = jnp.dot(act, wo_ref[...], preferred_element_type=F32)
        y_ref[...] = Y.astype(BF)
        xo = xv + (MACARON_W * _row(v_ref, r_gt)) * Y
        if with_final:
            @pl.when(pl.program_id(0) == 0)
            def _():
                vg_ref[...] = jnp.zeros_like(vg_ref)

            xo_ref[...] = _final_norm_loss_grad(xo, t_ref[...], v_ref, vg_ref)
        else:
            xo_ref[...] = xo

    tok = pl.BlockSpec((ts, D), lambda i: (i, 0))
    hid = pl.BlockSpec((ts, Fd), lambda i: (i, 0))
    in_specs = [tok, pl.BlockSpec(vecs.shape, lambda i: (0, 0)), _resident((D, Fd), lambda i: (0, 0)),
                _resident((D, Fd), lambda i: (0, 1)), _resident((Fd, D), lambda i: (0, 0))]
    out_specs = [tok, hid, hid, tok]
    out_shape = [jax.ShapeDtypeStruct((S, D), F32), jax.ShapeDtypeStruct((S, Fd), BF),
                 jax.ShapeDtypeStruct((S, Fd), BF), jax.ShapeDtypeStruct((S, D), BF)]
    args = [x, vecs, wi, wi, wo]
    if with_final:
        in_specs.append(tok)
        args.append(final_tgt)
        out_specs.append(pl.BlockSpec((SUBLANES, D), lambda i: (0, 0)))
        out_shape.append(jax.ShapeDtypeStruct((SUBLANES, D), F32))
    return _pcall(body, name=name, grid=(S // ts,), in_specs=in_specs, out_specs=out_specs, out_shape=out_shape,
                  compiler_params=_params(("arbitrary",)))(*args)


def ffn_fwd_in(x, vecs, wi, rows, name):
    r_sh, r_sc, r_gt, r_g = rows
    S, D = x.shape
    Fd = wi.shape[1] // 2
    ts = min(FFN_FWD_TS, S)

    def body(x_ref, v_ref, wg_ref, wu_ref, g_ref, u_ref, a_ref):
        xv = x_ref[...]
        r = lax.rsqrt(_rowmean(xv * xv) + EPS)
        gs = _row(v_ref, r_g) * (1.0 + _row(v_ref, r_sc))
        hb = (xv * r * gs + _row(v_ref, r_sh)).astype(BF)
        G = jnp.dot(hb, wg_ref[...], preferred_element_type=F32)
        U = jnp.dot(hb, wu_ref[...], preferred_element_type=F32)
        g_ref[...] = G.astype(BF)
        u_ref[...] = U.astype(BF)
        a_ref[...] = (G * _sigmoid(G) * U).astype(BF)

    hid = pl.BlockSpec((ts, Fd), lambda i: (i, 0))
    return _pcall(
        body, name=name, grid=(S // ts,),
        in_specs=[pl.BlockSpec((ts, D), lambda i: (i, 0)), pl.BlockSpec(vecs.shape, lambda i: (0, 0)),
                  _resident((D, Fd), lambda i: (0, 0)), _resident((D, Fd), lambda i: (0, 1))],
        out_specs=[hid, hid, hid], out_shape=[jax.ShapeDtypeStruct((S, Fd), BF)] * 3,
        compiler_params=_params(("arbitrary",)),
    )(x, vecs, wi, wi)


def ffn_fwd_out(act, x, vecs, wo, rows, name):
    r_sh, r_sc, r_gt, r_g = rows
    S, D = x.shape
    Fd = wo.shape[0]
    ts = min(FFN_FWD_TS, S)

    def body(a_ref, x_ref, v_ref, wo_ref, xo_ref, y_ref):
        Y = jnp.dot(a_ref[...], wo_ref[...], preferred_element_type=F32)
        y_ref[...] = Y.astype(BF)
        xo_ref[...] = x_ref[...] + (MACARON_W * _row(v_ref, r_gt)) * Y

    tok = pl.BlockSpec((ts, D), lambda i: (i, 0))
    return _pcall(
        body, name=name, grid=(S // ts,),
        in_specs=[pl.BlockSpec((ts, Fd), lambda i: (i, 0)), tok, pl.BlockSpec(vecs.shape, lambda i: (0, 0)),
                  _resident((Fd, D), lambda i: (0, 0))],
        out_specs=[tok, tok], out_shape=[jax.ShapeDtypeStruct((S, D), F32), jax.ShapeDtypeStruct((S, D), BF)],
        compiler_params=_params(("arbitrary",)),
    )(act, x, vecs, wo)


def ffn_bwd(dxo, x, vecs, gs_, us_, y, wi, wo, rows, name):
    r_sh, r_sc, r_gt, r_g = rows
    S, D = x.shape
    Fd = wo.shape[0]
    ts = min(FFN_BWD_TS, S)

    def body(dxo_ref, x_ref, v_ref, g_ref, u_ref, y_ref, wg_ref, wu_ref, wo_ref,
             dx_ref, act_ref, dg_ref, du_ref, hb_ref, dyb_ref, vg_ref):
        @pl.when(pl.program_id(0) == 0)
        def _():
            vg_ref[...] = jnp.zeros_like(vg_ref)

        dxo_v = dxo_ref[...]
        dyb = ((MACARON_W * _row(v_ref, r_gt)) * dxo_v).astype(BF)
        dyb_ref[...] = dyb
        vg_ref[0:1, :] += MACARON_W * _colsum(dxo_v * y_ref[...].astype(F32))
        dA = lax.dot_general(dyb, wo_ref[...], CONTRACT_LAST, preferred_element_type=F32)
        G = g_ref[...].astype(F32)
        U = u_ref[...].astype(F32)
        sg = _sigmoid(G)
        sl = G * sg
        dU = (dA * sl).astype(BF)
        dG = (dA * U * (sg * (1.0 + G * (1.0 - sg)))).astype(BF)
        act_ref[...] = (sl * U).astype(BF)
        dg_ref[...] = dG
        du_ref[...] = dU
        dh = (lax.dot_general(dG, wg_ref[...], CONTRACT_LAST, preferred_element_type=F32)
              + lax.dot_general(dU, wu_ref[...], CONTRACT_LAST, preferred_element_type=F32))
        xv = x_ref[...]
        r = lax.rsqrt(_rowmean(xv * xv) + EPS)
        n = xv * r
        g = _row(v_ref, r_g)
        sc1 = 1.0 + _row(v_ref, r_sc)
        gsc = g * sc1
        hb_ref[...] = (n * gsc + _row(v_ref, r_sh)).astype(BF)
        dhn = dh * n
        vg_ref[1:2, :] += _colsum(dh)
        vg_ref[2:3, :] += _colsum(dhn) * g
        vg_ref[3:4, :] += _colsum(dhn) * sc1
        dn = dh * gsc
        dx_ref[...] = dxo_v + r * (dn - n * _rowmean(dn * n))

    tok = pl.BlockSpec((ts, D), lambda i: (i, 0))
    hid = pl.BlockSpec((ts, Fd), lambda i: (i, 0))
    return _pcall(
        body, name=name, grid=(S // ts,),
        in_specs=[tok, tok, pl.BlockSpec(vecs.shape, lambda i: (0, 0)), hid, hid, tok, _resident((D, Fd), lambda i: (0, 0)),
                  _resident((D, Fd), lambda i: (0, 1)), _resident((Fd, D), lambda i: (0, 0))],
        out_specs=[tok, hid, hid, hid, tok, tok, pl.BlockSpec((SUBLANES, D), lambda i: (0, 0))],
        out_shape=[jax.ShapeDtypeStruct((S, D), F32), jax.ShapeDtypeStruct((S, Fd), BF),
                   jax.ShapeDtypeStruct((S, Fd), BF), jax.ShapeDtypeStruct((S, Fd), BF),
                   jax.ShapeDtypeStruct((S, D), BF), jax.ShapeDtypeStruct((S, D), BF),
                   jax.ShapeDtypeStruct((SUBLANES, D), F32)],
        compiler_params=_params(("arbitrary",)),
    )(dxo, x, vecs, gs_, us_, y, wi, wi, wo)


def norm_matmul(x, vecs, w, rows, name):
    r_sh, r_sc, r_g = rows
    S, D = x.shape
    N = w.shape[1]
    ts = min(512, S)

    def body(x_ref, v_ref, w_ref, o_ref):
        xv = x_ref[...]
        r = lax.rsqrt(_rowmean(xv * xv) + EPS)
        gs = _row(v_ref, r_g) * (1.0 + _row(v_ref, r_sc))
        hb = (xv * r * gs + _row(v_ref, r_sh)).astype(BF)
        o_ref[...] = jnp.dot(hb, w_ref[...], preferred_element_type=F32).astype(BF)

    return _pcall(
        body, name=name, grid=(S // ts,),
        in_specs=[pl.BlockSpec((ts, D), lambda i: (i, 0)), pl.BlockSpec(vecs.shape, lambda i: (0, 0)),
                  _resident((D, N), lambda i: (0, 0))],
        out_specs=pl.BlockSpec((ts, N), lambda i: (i, 0)),
        out_shape=jax.ShapeDtypeStruct((S, N), BF),
        compiler_params=_params(("arbitrary",)),
    )(x, vecs, w)


LOOKAHEAD = 2
SEQ_TT = 256
SCAN_SEGMENTS = 4
CONV_BWD_TT = 128
CONV_PAD = 32
CONV_TAPS = 32


def conv_fwd(proj, convw4, conv_b, name):
    S = proj.shape[0]
    M = conv_b.shape[1]
    nb = M // LANES
    tt = min(SEQ_TT, S)

    def body(uv_ref, ug_ref, w_ref, b_ref, cq_ref, qp):
        qp[0:CONV_PAD, :] = jnp.zeros((CONV_PAD, LANES), F32)

        def step(t, carry):
            base = pl.multiple_of(t * tt, tt)
            qp[pl.ds(base + CONV_PAD, tt), :] = uv_ref[pl.ds(base, tt), :].astype(F32) * _sigmoid(ug_ref[pl.ds(base, tt), :].astype(F32))
            acc = jnp.broadcast_to(b_ref[...], (tt, LANES))
            for k in range(CONV_WIDTH):
                acc = acc + w_ref[k:k + 1, :] * qp[pl.ds(base + (CONV_PAD - CONV_WIDTH + 1) + k, tt), :]
            cq_ref[pl.ds(base, tt), :] = acc
            return carry

        lax.fori_loop(0, S // tt, step, 0)

    return _pcall(
        body, name=name, grid=(nb,),
        in_specs=[pl.BlockSpec((S, LANES), lambda c: (0, c)), pl.BlockSpec((S, LANES), lambda c: (0, c + nb)),
                  pl.BlockSpec((None, CONV_TAPS, LANES), lambda c: (c, 0, 0)), pl.BlockSpec((1, LANES), lambda c: (0, c))],
        out_specs=pl.BlockSpec((S, LANES), lambda c: (0, c)),
        out_shape=jax.ShapeDtypeStruct((S, M), F32),
        scratch_shapes=[pltpu.VMEM((S + CONV_PAD, LANES), F32)],
        compiler_params=_params(("arbitrary",)),
    )(proj, proj, convw4, conv_b)


def conv_bwd(dcq, proj, convw4, name):
    S, M = dcq.shape
    nb = M // LANES
    tt = min(CONV_BWD_TT, S)

    def body(dcq_ref, uv_ref, ug_ref, w_ref, duv_ref, dug_ref, dw_ref, db_ref, dp, dw8, db8):
        dp[S:S + CONV_PAD, :] = jnp.zeros((CONV_PAD, LANES), F32)
        dw8[...] = jnp.zeros_like(dw8)
        db8[...] = jnp.zeros_like(db8)

        def fill(t, carry):
            base = pl.multiple_of(t * tt, tt)
            dp[pl.ds(base, tt), :] = dcq_ref[pl.ds(base, tt), :].astype(F32)
            return carry

        lax.fori_loop(0, S // tt, fill, 0)

        def step(t, carry):
            base = pl.multiple_of(t * tt, tt)
            uv = uv_ref[pl.ds(base, tt), :].astype(F32)
            sg = _sigmoid(ug_ref[pl.ds(base, tt), :].astype(F32))
            q_t = uv * sg
            db8[...] += dp[pl.ds(base, tt), :].reshape(tt // SUBLANES, SUBLANES, LANES).sum(axis=0)
            dq = jnp.zeros((tt, LANES), F32)
            for k in range(CONV_WIDTH):
                shifted = dp[pl.ds(base + (CONV_WIDTH - 1) - k, tt), :]
                dw8[k] += (shifted * q_t).reshape(tt // SUBLANES, SUBLANES, LANES).sum(axis=0)
                dq = dq + w_ref[k:k + 1, :] * shifted
            duv_ref[pl.ds(base, tt), :] = (dq * sg).astype(BF)
            dug_ref[pl.ds(base, tt), :] = (dq * uv * sg * (1.0 - sg)).astype(BF)
            return carry

        lax.fori_loop(0, S // tt, step, 0)
        dw_ref[...] = jnp.zeros_like(dw_ref)
        for k in range(CONV_WIDTH):
            dw_ref[k:k + 1, :] = _colsum(dw8[k])
        db_ref[...] = _colsum(db8[...])

    col = lambda o: pl.BlockSpec((S, LANES), lambda c: (0, c + o))
    return _pcall(
        body, name=name, grid=(nb,),
        in_specs=[col(0), col(0), col(nb), pl.BlockSpec((None, CONV_TAPS, LANES), lambda c: (c, 0, 0))],
        out_specs=[col(0), col(0), pl.BlockSpec((None, CONV_TAPS, LANES), lambda c: (c, 0, 0)),
                   pl.BlockSpec((1, LANES), lambda c: (0, c))],
        out_shape=[jax.ShapeDtypeStruct((S, M), BF), jax.ShapeDtypeStruct((S, M), BF),
                   jax.ShapeDtypeStruct((nb, CONV_TAPS, LANES), F32), jax.ShapeDtypeStruct((1, M), F32)],
        scratch_shapes=[pltpu.VMEM((S + CONV_PAD, LANES), F32),
                        pltpu.VMEM((CONV_TAPS, SUBLANES, LANES), F32), pltpu.VMEM((SUBLANES, LANES), F32)],
        compiler_params=_params(("arbitrary",)),
    )(dcq, proj, proj, convw4)


def _log_sigmoid(x):
    return jnp.minimum(x, 0.0) - jnp.log(1.0 + jnp.exp(-jnp.abs(x)))


def _rg_gate_terms(ra, ls):
    la = RG_C * ra * ls
    a = jnp.exp(la)
    th = jnp.tanh(la)
    mult = jnp.sqrt(-2.0 * th / (1.0 - th))
    return a, mult


def rnn_fwd(proj, rnnw4, rnn_b, bda, bdi, b_a, b_i, lam, name):
    S = proj.shape[0]
    M = rnn_b.shape[1]
    nb = M // LANES
    tt = min(SEQ_TT, S)
    KW = RNN_CONV_WIDTH
    nseg = SCAN_SEGMENTS if S % (SCAN_SEGMENTS * tt) == 0 else 1

    def body(ux_ref, w_ref, rb_ref, bda_ref, bdi_ref, ba_ref, bi_ref, lam_ref,
             xr_ref, ra_ref, ii_ref, h_ref, uxp, a_sc, b_sc):
        uxp[0:SUBLANES, :] = jnp.zeros((SUBLANES, LANES), F32)
        ls = _log_sigmoid(lam_ref[...])

        def step(t, carry):
            base = pl.multiple_of(t * tt, tt)
            uxp[pl.ds(base + SUBLANES, tt), :] = ux_ref[pl.ds(base, tt), :].astype(F32)
            xr = jnp.broadcast_to(rb_ref[...], (tt, LANES))
            for k in range(KW):
                xr = xr + w_ref[k:k + 1, :] * uxp[pl.ds(base + (SUBLANES - KW + 1) + k, tt), :]
            xb = xr.astype(BF)
            ra = _sigmoid(jnp.dot(xb, bda_ref[...], preferred_element_type=F32) + ba_ref[...])
            ii = _sigmoid(jnp.dot(xb, bdi_ref[...], preferred_element_type=F32) + bi_ref[...])
            a, mult = _rg_gate_terms(ra, ls)
            xr_ref[pl.ds(base, tt), :] = xr
            ra_ref[pl.ds(base, tt), :] = ra
            ii_ref[pl.ds(base, tt), :] = ii
            a_sc[pl.ds(base, tt), :] = a
            b_sc[pl.ds(base, tt), :] = mult * (ii * xr)
            return carry

        lax.fori_loop(0, S // tt, step, 0)

        rows = lax.broadcasted_iota(I32, (SUBLANES, LANES), 0)
        seg = S // nseg
        last = lambda v: jnp.broadcast_to(v[SUBLANES - 1:SUBLANES, :], (SUBLANES, LANES))

        def scan(t, carry):
            hs, ps = carry
            new_h, new_p = [], []
            for s in range(nseg):
                base = pl.multiple_of(s * seg + t * SUBLANES, SUBLANES)
                A = a_sc[pl.ds(base, SUBLANES), :]
                B = b_sc[pl.ds(base, SUBLANES), :]
                for d in (1, 2, 4):
                    As = jnp.where(rows >= d, pltpu.roll(A, d, axis=0), 1.0)
                    Bs = jnp.where(rows >= d, pltpu.roll(B, d, axis=0), 0.0)
                    B = A * Bs + B
                    A = A * As
                hh = B + A * hs[s]
                h_ref[pl.ds(base, SUBLANES), :] = hh
                pp = A * ps[s]
                if s > 0:
                    a_sc[pl.ds(base, SUBLANES), :] = pp
                new_h.append(last(hh))
                new_p.append(last(pp))
            return tuple(new_h), tuple(new_p)

        zero8 = jnp.zeros((SUBLANES, LANES), F32)
        one8 = jnp.ones((SUBLANES, LANES), F32)
        hs, ps = lax.fori_loop(0, seg // SUBLANES, scan, ((zero8,) * nseg, (one8,) * nseg))
        carry_in = hs[0]
        for s in range(1, nseg):
            c_row = carry_in[0:1, :]

            def fix(t, c, s=s, c_row=c_row):
                base = pl.multiple_of(s * seg + t * tt, tt)
                h_ref[pl.ds(base, tt), :] = h_ref[pl.ds(base, tt), :] + a_sc[pl.ds(base, tt), :] * c_row
                return c

            lax.fori_loop(0, seg // tt, fix, 0)
            carry_in = hs[s] + ps[s] * carry_in

    col = lambda o: pl.BlockSpec((S, LANES), lambda c: (0, c + o))
    vec = pl.BlockSpec((1, LANES), lambda c: (0, c))
    diag = pl.BlockSpec((LANES, LANES), lambda c: (c, c))
    return _pcall(
        body, name=name, grid=(nb,),
        in_specs=[col(2 * nb), pl.BlockSpec((None, SUBLANES, LANES), lambda c: (c, 0, 0)), vec, diag, diag, vec, vec, vec],
        out_specs=[col(0)] * 4,
        out_shape=[jax.ShapeDtypeStruct((S, M), F32)] * 4,
        scratch_shapes=[pltpu.VMEM((S + SUBLANES, LANES), F32), pltpu.VMEM((S, LANES), F32), pltpu.VMEM((S, LANES), F32)],
        compiler_params=_params(("arbitrary",)),
    )(proj, rnnw4, rnn_b, bda, bdi, b_a, b_i, lam)


def rnn_bwd(dhout, h, xr, ra, ii, proj, rnnw4, bda, bdi, lam, name):
    S, M = h.shape
    nb = M // LANES
    tt = min(SEQ_TT, S)
    KW = RNN_CONV_WIDTH
    SL = SUBLANES
    nseg = SCAN_SEGMENTS if S % (SCAN_SEGMENTS * tt) == 0 else 1

    def body(dh_ref, h_ref, xr_ref, ra_ref, ii_ref, ux_ref, w_ref, bda_ref, bdi_ref, lam_ref,
             dux_ref, dwa_ref, dwi_ref, drw_ref, vec_ref,
             a_sc, hp, g_sc, dpa_sc, dpi_sc, dxp, uxp, acc8, drw8, p_sc):
        zero8 = jnp.zeros((SL, LANES), F32)
        a_sc[S:S + SL, :] = zero8
        hp[0:SL, :] = zero8
        dxp[S:S + SL, :] = zero8
        uxp[0:SL, :] = zero8
        acc8[...] = jnp.zeros_like(acc8)
        drw8[...] = jnp.zeros_like(drw8)
        lamv = lam_ref[...]
        ls = _log_sigmoid(lamv)

        def fill(t, carry):
            base = pl.multiple_of(t * tt, tt)
            a_sc[pl.ds(base, tt), :] = jnp.exp(RG_C * ra_ref[pl.ds(base, tt), :] * ls)
            hp[pl.ds(base + SL, tt), :] = h_ref[pl.ds(base, tt), :]
            uxp[pl.ds(base + SL, tt), :] = ux_ref[pl.ds(base, tt), :].astype(F32)
            return carry

        lax.fori_loop(0, S // tt, fill, 0)

        rows = lax.broadcasted_iota(I32, (SL, LANES), 0)
        seg = S // nseg
        nt8 = seg // SL
        first = lambda v: jnp.broadcast_to(v[0:1, :], (SL, LANES))

        def rscan(t, carry):
            gs, ps = carry
            new_g, new_p = [], []
            for s in range(nseg):
                base = pl.multiple_of(s * seg + (nt8 - 1 - t) * SL, SL)
                A = a_sc[pl.ds(base + 1, SL), :]
                B = dh_ref[pl.ds(base, SL), :]
                for d in (1, 2, 4):
                    As = jnp.where(rows < SL - d, pltpu.roll(A, SL - d, axis=0), 1.0)
                    Bs = jnp.where(rows < SL - d, pltpu.roll(B, SL - d, axis=0), 0.0)
                    B = A * Bs + B
                    A = A * As
                g = B + A * gs[s]
                g_sc[pl.ds(base, SL), :] = g
                pp = A * ps[s]
                if s < nseg - 1:
                    p_sc[pl.ds(base, SL), :] = pp
                new_g.append(first(g))
                new_p.append(first(pp))
            return tuple(new_g), tuple(new_p)

        one8 = jnp.ones((SL, LANES), F32)
        gs, ps = lax.fori_loop(0, nt8, rscan, ((zero8,) * nseg, (one8,) * nseg))
        carry_in = gs[nseg - 1]
        for s in range(nseg - 2, -1, -1):
            c_row = carry_in[0:1, :]

            def fix(t, c, s=s, c_row=c_row):
                base = pl.multiple_of(s * seg + t * tt, tt)
                g_sc[pl.ds(base, tt), :] = g_sc[pl.ds(base, tt), :] + p_sc[pl.ds(base, tt), :] * c_row
                return c

            lax.fori_loop(0, seg // tt, fix, 0)
            carry_in = gs[s] + ps[s] * carry_in

        def red8(v):
            return v.reshape(tt // SL, SL, LANES).sum(axis=0)

        def step(t, carry):
            base = pl.multiple_of(t * tt, tt)
            g = g_sc[pl.ds(base, tt), :]
            hprev = hp[pl.ds(base + SL - 1, tt), :]
            xr_t = xr_ref[pl.ds(base, tt), :]
            ra_t = ra_ref[pl.ds(base, tt), :]
            ii_t = ii_ref[pl.ds(base, tt), :]
            a, mult = _rg_gate_terms(ra_t, ls)
            gx = g * xr_t
            dmult = gx * ii_t
            dii = gx * mult
            dxr = g * (mult * ii_t)
            dla = g * hprev * a - dmult * (a * a) / mult
            acc8[3] += red8(dla * ra_t)
            dpa = dla * (RG_C * ls) * ra_t * (1.0 - ra_t)
            dpi = dii * ii_t * (1.0 - ii_t)
            dpab = dpa.astype(BF)
            dpib = dpi.astype(BF)
            dxr = dxr + (lax.dot_general(dpab, bda_ref[...], CONTRACT_LAST, preferred_element_type=F32)
                         + lax.dot_general(dpib, bdi_ref[...], CONTRACT_LAST, preferred_element_type=F32))
            dpa_sc[pl.ds(base, tt), :] = dpab
            dpi_sc[pl.ds(base, tt), :] = dpib
            dxp[pl.ds(base, tt), :] = dxr
            acc8[0] += red8(dxr)
            acc8[1] += red8(dpa)
            acc8[2] += red8(dpi)
            return carry

        lax.fori_loop(0, S // tt, step, 0)

        def convb(t, carry):
            base = pl.multiple_of(t * tt, tt)
            d_t = dxp[pl.ds(base, tt), :]
            dux = jnp.zeros((tt, LANES), F32)
            for k in range(KW):
                drw8[k] += red8(d_t * uxp[pl.ds(base + (SL - KW + 1) + k, tt), :])
                dux = dux + w_ref[k:k + 1, :] * dxp[pl.ds(base + (KW - 1) - k, tt), :]
            dux_ref[pl.ds(base, tt), :] = dux.astype(BF)
            return carry

        lax.fori_loop(0, S // tt, convb, 0)

        xb = xr_ref[...].astype(BF)
        dwa_ref[...] = lax.dot_general(xb, dpa_sc[...], CONTRACT_FIRST, preferred_element_type=F32)
        dwi_ref[...] = lax.dot_general(xb, dpi_sc[...], CONTRACT_FIRST, preferred_element_type=F32)
        drw_ref[...] = jnp.zeros_like(drw_ref)
        vec_ref[...] = jnp.zeros_like(vec_ref)
        for k in range(KW):
            drw_ref[k:k + 1, :] = _colsum(drw8[k])
        for k in range(3):
            vec_ref[k:k + 1, :] = _colsum(acc8[k])
        vec_ref[3:4, :] = _colsum(acc8[3]) * (RG_C * _sigmoid(-lamv))

    col = lambda o: pl.BlockSpec((S, LANES), lambda c: (0, c + o))
    vec = pl.BlockSpec((1, LANES), lambda c: (0, c))
    diag = pl.BlockSpec((LANES, LANES), lambda c: (c, c))
    blk3 = lambda r: pl.BlockSpec((None, r, LANES), lambda c: (c, 0, 0))
    return _pcall(
        body, name=name, grid=(nb,),
        in_specs=[col(0), col(0), col(0), col(0), col(0), col(2 * nb), blk3(SL), diag, diag, vec],
        out_specs=[col(0), blk3(LANES), blk3(LANES), blk3(SL), pl.BlockSpec((SL, LANES), lambda c: (0, c))],
        out_shape=[jax.ShapeDtypeStruct((S, M), BF), jax.ShapeDtypeStruct((nb, LANES, LANES), F32),
                   jax.ShapeDtypeStruct((nb, LANES, LANES), F32), jax.ShapeDtypeStruct((nb, SL, LANES), F32),
                   jax.ShapeDtypeStruct((SL, M), F32)],
        scratch_shapes=[pltpu.VMEM((S + SL, LANES), F32), pltpu.VMEM((S + SL, LANES), F32), pltpu.VMEM((S, LANES), F32),
                        pltpu.VMEM((S, LANES), BF), pltpu.VMEM((S, LANES), BF), pltpu.VMEM((S + SL, LANES), F32),
                        pltpu.VMEM((S + SL, LANES), F32), pltpu.VMEM((SL, SL, LANES), F32), pltpu.VMEM((SL, SL, LANES), F32),
                        pltpu.VMEM((S, LANES), F32)],
        compiler_params=_params(("arbitrary",)),
    )(dhout, h, xr, ra, ii, proj, rnnw4, bda, bdi, lam)


GELU_K = 0.7978845608028654
GELU_C = 0.044715


def _layernorm_parts(cq):
    mu = _rowmean(cq)
    d = cq - mu
    rstd = lax.rsqrt(_rowmean(d * d) + EPS)
    return d * rstd, rstd


def mix_out(cq, proj, h, x, vecs, lnv, wout, name):
    S, D = x.shape
    M = cq.shape[1]
    ts = min(512, S)

    def body(cq_ref, uy_ref, h_ref, x_ref, v_ref, ln_ref, w_ref, xo_ref, ym_ref, yc_ref):
        z, _ = _layernorm_parts(cq_ref[...])
        l = z * _row(ln_ref, 0) + _row(ln_ref, 1)
        yc_ref[:, 0:M] = (l * _sigmoid(l)).astype(BF)
        uy = uy_ref[...].astype(F32)
        gelu = 0.5 * uy * (1.0 + jnp.tanh(GELU_K * (uy + GELU_C * uy * uy * uy)))
        yc_ref[:, M:2 * M] = (gelu * h_ref[...]).astype(BF)
        ym = jnp.dot(yc_ref[...], w_ref[...], preferred_element_type=F32)
        ym_ref[...] = ym.astype(BF)
        xo_ref[...] = x_ref[...] + _row(v_ref, R_GT2) * ym

    tok = pl.BlockSpec((ts, D), lambda i: (i, 0))
    mtok = lambda o: pl.BlockSpec((ts, M), lambda i: (i, o))
    tok_in = _streamed((ts, D), lambda i: (i, 0))
    mtok_in = lambda o: _streamed((ts, M), lambda i: (i, o))
    return _pcall(
        body, name=name, grid=(S // ts,),
        in_specs=[mtok_in(0), mtok_in(3), mtok_in(0), tok_in, pl.BlockSpec(vecs.shape, lambda i: (0, 0)),
                  pl.BlockSpec(lnv.shape, lambda i: (0, 0)), _resident(wout.shape, lambda i: (0, 0))],
        out_specs=[tok, tok, pl.BlockSpec((ts, 2 * M), lambda i: (i, 0))],
        out_shape=[jax.ShapeDtypeStruct((S, D), F32), jax.ShapeDtypeStruct((S, D), BF),
                   jax.ShapeDtypeStruct((S, 2 * M), BF)],
        compiler_params=_params(("arbitrary",)),
    )(cq, proj, h, x, vecs, lnv, wout)


def mix_out_bwd(dxo, ym, vecs, wout, cq, lnv, proj, h, name):
    S, D = dxo.shape
    M = cq.shape[1]
    ts = min(512, S)
    nt = S // ts
    ahead = min(LOOKAHEAD, nt)

    def body(dxo_hbm, ym_ref, v_ref, w_ref, cq_ref, ln_ref, uy_ref, h_ref,
             dcq_ref, dh_ref, duy_ref, dyb_ref, vgd_ref, vgm_ref, dxo_buf, dxo_sems):
        i = pl.program_id(0)

        def fetch(t):
            slot = t % (ahead + 1)
            row0 = t * ts if isinstance(t, int) else pl.multiple_of(t * ts, ts)
            return pltpu.make_async_copy(dxo_hbm.at[pl.ds(row0, ts), :], dxo_buf.at[slot], dxo_sems.at[slot])

        @pl.when(i == 0)
        def _():
            vgd_ref[...] = jnp.zeros_like(vgd_ref)
            vgm_ref[...] = jnp.zeros_like(vgm_ref)
            for t in range(ahead):
                fetch(t).start()

        @pl.when(i + ahead < nt)
        def _():
            fetch(i + ahead).start()

        fetch(i).wait()
        dxo_v = dxo_buf[i % (ahead + 1)]
        dyb = (_row(v_ref, R_GT2) * dxo_v).astype(BF)
        dyb_ref[...] = dyb
        vgd_ref[0:1, :] += _colsum(dxo_v * ym_ref[...].astype(F32))
        dycat = lax.dot_general(dyb, w_ref[...], CONTRACT_LAST, preferred_element_type=F32)
        dyc = dycat[:, 0:M]
        dyr = dycat[:, M:2 * M]
        z, rstd = _layernorm_parts(cq_ref[...])
        lng = _row(ln_ref, 0)
        l = z * lng + _row(ln_ref, 1)
        sl = _sigmoid(l)
        dl = dyc * (sl * (1.0 + l * (1.0 - sl)))
        vgm_ref[0:1, :] += _colsum(dl * z)
        vgm_ref[1:2, :] += _colsum(dl)
        dz = dl * lng
        dcq_ref[...] = (rstd * (dz - _rowmean(dz) - z * _rowmean(dz * z))).astype(BF)
        uy = uy_ref[...].astype(F32)
        u2 = uy * uy
        th = jnp.tanh(GELU_K * (uy + GELU_C * uy * u2))
        gelu = 0.5 * uy * (1.0 + th)
        dgelu = 0.5 * (1.0 + th) + 0.5 * uy * (1.0 - th * th) * (GELU_K * (1.0 + 3.0 * GELU_C * u2))
        dh_ref[...] = dyr * gelu
        duy_ref[...] = (dyr * h_ref[...] * dgelu).astype(BF)

    tok = pl.BlockSpec((ts, D), lambda i: (i, 0))
    mtok = lambda o: pl.BlockSpec((ts, M), lambda i: (i, o))
    tok_in = _streamed((ts, D), lambda i: (i, 0))
    mtok_in = lambda o: _streamed((ts, M), lambda i: (i, o))
    return _pcall(
        body, name=name, grid=(S // ts,),
        in_specs=[pl.BlockSpec(memory_space=pl.ANY), tok_in, pl.BlockSpec(vecs.shape, lambda i: (0, 0)),
                  _resident(wout.shape, lambda i: (0, 0)),
                  mtok_in(0), pl.BlockSpec(lnv.shape, lambda i: (0, 0)), mtok_in(3), mtok_in(0)],
        out_specs=[mtok(0), mtok(0), mtok(0), tok, pl.BlockSpec((SUBLANES, D), lambda i: (0, 0)),
                   pl.BlockSpec((SUBLANES, M), lambda i: (0, 0))],
        out_shape=[jax.ShapeDtypeStruct((S, M), BF), jax.ShapeDtypeStruct((S, M), F32), jax.ShapeDtypeStruct((S, M), BF),
                   jax.ShapeDtypeStruct((S, D), BF),
                   jax.ShapeDtypeStruct((SUBLANES, D), F32), jax.ShapeDtypeStruct((SUBLANES, M), F32)],
        scratch_shapes=[pltpu.VMEM((ahead + 1, ts, D), F32), pltpu.SemaphoreType.DMA((ahead + 1,))],
        compiler_params=_params(("arbitrary",)),
    )(dxo, ym, vecs, wout, cq, lnv, proj, h)


def mix_in_bwd(dparts, x, dxo, vecs, win, name):
    S, D = x.shape
    M = dparts[0].shape[1]
    ts = min(512, S)

    def body(d0, d1, d2, d3, x_ref, dxo_ref, v_ref, w_ref, dx_ref, hb_ref, dp_ref, vg_ref):
        @pl.when(pl.program_id(0) == 0)
        def _():
            vg_ref[...] = jnp.zeros_like(vg_ref)

        for q, dref in enumerate((d0, d1, d2, d3)):
            dp_ref[:, q * M:(q + 1) * M] = dref[...].astype(BF)
        dh = lax.dot_general(dp_ref[...], w_ref[...], CONTRACT_LAST, preferred_element_type=F32)
        xv = x_ref[...]
        r = lax.rsqrt(_rowmean(xv * xv) + EPS)
        n = xv * r
        g = _row(v_ref, R_G2)
        sc1 = 1.0 + _row(v_ref, R_SC2)
        gsc = g * sc1
        hb_ref[...] = (n * gsc + _row(v_ref, R_SH2)).astype(BF)
        dhn = dh * n
        vg_ref[0:1, :] += _colsum(dh)
        vg_ref[1:2, :] += _colsum(dhn) * g
        vg_ref[2:3, :] += _colsum(dhn) * sc1
        dn = dh * gsc
        dx_ref[...] = dxo_ref[...] + r * (dn - n * _rowmean(dn * n))

    tok = pl.BlockSpec((ts, D), lambda i: (i, 0))
    mtok_in = _streamed((ts, M), lambda i: (i, 0))
    tok_in = _streamed((ts, D), lambda i: (i, 0))
    return _pcall(
        body, name=name, grid=(S // ts,),
        in_specs=[mtok_in] * 4 + [tok_in, tok_in, pl.BlockSpec(vecs.shape, lambda i: (0, 0)), _resident(win.shape, lambda i: (0, 0))],
        out_specs=[tok, tok, pl.BlockSpec((ts, 4 * M), lambda i: (i, 0)), pl.BlockSpec((SUBLANES, D), lambda i: (0, 0))],
        out_shape=[jax.ShapeDtypeStruct((S, D), F32), jax.ShapeDtypeStruct((S, D), BF),
                   jax.ShapeDtypeStruct((S, 4 * M), BF), jax.ShapeDtypeStruct((SUBLANES, D), F32)],
        compiler_params=_params(("arbitrary",)),
    )(*dparts, x, dxo, vecs, win)


def _adamw(w, g, m, v):
    m = ADAM_B1 * m + (1.0 - ADAM_B1) * g
    v = ADAM_B2 * v + (1.0 - ADAM_B2) * (g * g)
    m_hat = m / (1.0 - ADAM_B1 ** ADAM_STEP)
    v_hat = v / (1.0 - ADAM_B2 ** ADAM_STEP)
    delta = -ADAM_LR * (m_hat / (jnp.sqrt(v_hat) + ADAM_EPS) + ADAM_WD * w)
    return delta, m, v


def adam_big(w, g_mine, g_sib, m, v, cidx, name):
    R, C = w.shape
    hr = R // 2
    tr = 256 if hr % 256 == 0 else hr
    tc = C if C <= 1536 else (1152 if C % 1152 == 0 else 1024)
    assert hr % tr == 0 and C % tc == 0 and g_mine.shape == (hr, C)
    nrb = hr // tr

    def body(ci_ref, w_ref, gm_ref, gs_ref, m_ref, v_ref, g_ref, d_ref, nm_ref, nv_ref):
        mine = (pl.program_id(0) // nrb) == ci_ref[0]
        g = jnp.where(mine, gm_ref[...], gs_ref[...])
        d, nm, nv = _adamw(w_ref[...], g, m_ref[...], v_ref[...])
        g_ref[...] = g
        d_ref[...] = d
        nm_ref[...] = nm
        nv_ref[...] = nv

    blk = pl.BlockSpec((tr, tc), lambda i, j, ci: (i, j))
    mine_spec = pl.BlockSpec((tr, tc), lambda i, j, ci: (jnp.where(i // nrb == ci[0], i % nrb, 0), j))
    sib_spec = pl.BlockSpec((tr, tc), lambda i, j, ci: (jnp.where(i // nrb == ci[0], 0, i % nrb), j))
    gs = pltpu.PrefetchScalarGridSpec(num_scalar_prefetch=1, grid=(R // tr, C // tc),
                                      in_specs=[blk, mine_spec, sib_spec, blk, blk], out_specs=[blk] * 4)
    return _pcall(body, name=name, grid_spec=gs, out_shape=[jax.ShapeDtypeStruct((R, C), F32)] * 4,
                  compiler_params=_params(("parallel", "parallel")))(cidx, w, g_mine, g_sib, m, v)


def adam_cond(c_all, dmod, w, m, v, name):
    B, Kin = c_all.shape
    N = w.shape[1]
    tn = 768 if N % 768 == 0 else 256
    assert N % tn == 0

    def body(c_ref, d_ref, w_ref, m_ref, v_ref, g_ref, dl_ref, nm_ref, nv_ref):
        cv = c_ref[...]
        ca = cv * _sigmoid(cv)
        g = lax.dot_general(ca.astype(BF), d_ref[...].astype(BF), CONTRACT_FIRST, preferred_element_type=F32)
        d, nm, nv = _adamw(w_ref[...], g, m_ref[...], v_ref[...])
        g_ref[...] = g
        dl_ref[...] = d
        nm_ref[...] = nm
        nv_ref[...] = nv

    blk = pl.BlockSpec((Kin, tn), lambda n: (0, n))
    return _pcall(
        body, name=name, grid=(N // tn,),
        in_specs=[pl.BlockSpec((B, Kin), lambda n: (0, 0)), pl.BlockSpec((B, tn), lambda n: (0, n)), blk, blk, blk],
        out_specs=[blk] * 4, out_shape=[jax.ShapeDtypeStruct((Kin, N), F32)] * 4,
        compiler_params=_params(("parallel",)),
    )(c_all, dmod, w, m, v)


def adam_small(ws, gs, ms, vs, name):
    n = len(ws)

    def body(*refs):
        ins, outs = refs[:4 * n], refs[4 * n:]
        for k in range(n):
            d, nm, nv = _adamw(ins[k][...], ins[n + k][...], ins[2 * n + k][...], ins[3 * n + k][...])
            outs[k][...] = d
            outs[n + k][...] = nm
            outs[2 * n + k][...] = nv

    specs = [pl.BlockSpec(w.shape, lambda i: (0, 0)) for w in ws]
    shapes = [jax.ShapeDtypeStruct(w.shape, F32) for w in ws]
    out = _pcall(body, name=name, grid=(1,), in_specs=specs * 4, out_specs=specs * 3, out_shape=shapes * 3,
                 compiler_params=_params(("arbitrary",)))(*ws, *gs, *ms, *vs)
    return out[:n], out[n:2 * n], out[2 * n:]


def _me():
    return lax.axis_index("x"), lax.axis_index("y"), lax.axis_index("c")


def _flip(x, y, p):
    return (x ^ (p >> 1) if (p >> 1) else x), (y ^ (p & 1) if (p & 1) else y)


def _handshake(peers):
    barrier = pltpu.get_barrier_semaphore()
    for peer in peers:
        pl.semaphore_signal(barrier, inc=1, device_id=peer, device_id_type=MESH)
    pl.semaphore_wait(barrier, len(peers))


def _seq_call(body, *, name, n_in, out_shape, sem_shapes, collective_id):
    del n_in
    return pl.kernel(body, out_type=out_shape, mesh=plsc.ScalarSubcoreMesh(axis_name="sq", num_cores=1), name=name,
                     scratch_types=sem_shapes, compiler_params=pltpu.CompilerParams(collective_id=collective_id))


def _hbm_comm_call(body, *, name, n_in, out_shape, sem_shapes, seq_id):
    if seq_id is not None:
        return _seq_call(body, name=name, n_in=n_in, out_shape=out_shape, sem_shapes=sem_shapes, collective_id=seq_id)
    anyspec = pl.BlockSpec(memory_space=pl.ANY)
    return _pcall(body, name=name, in_specs=[anyspec] * n_in, out_specs=[anyspec] * len(out_shape), out_shape=out_shape,
                  scratch_shapes=sem_shapes, compiler_params=_params())


def allgather_devices(v, name, with_sum=False):
    R, L = v.shape

    def body(v_ref, out_ref, *rest):
        if with_sum:
            sum_ref, send_sems, recv_sems = rest
        else:
            send_sems, recv_sems = rest
        x, y, c = _me()
        me = 4 * x + 2 * y + c
        out_ref[me] = v_ref[...]
        copies = []
        for p in range(1, N_DEV):
            px, py = _flip(x, y, p >> 1)
            pc = (1 - c) if (p & 1) else c
            peer = 4 * px + 2 * py + pc
            send = pltpu.make_async_remote_copy(src_ref=v_ref, dst_ref=out_ref.at[me], send_sem=send_sems.at[p - 1],
                                                recv_sem=recv_sems.at[p - 1], device_id=(px, py, pc), device_id_type=MESH)
            send.start()
            recv = pltpu.make_async_remote_copy(src_ref=v_ref, dst_ref=out_ref.at[peer], send_sem=send_sems.at[p - 1],
                                                recv_sem=recv_sems.at[p - 1], device_id=(px, py, pc), device_id_type=MESH)
            copies.append((send, recv))
        for send, recv in copies:
            recv.wait_recv()
        for send, recv in copies:
            send.wait_send()
        if with_sum:
            s = out_ref[0]
            for k in range(1, N_DEV):
                s = s + out_ref[k]
            sum_ref[...] = s

    vm = pl.BlockSpec(memory_space=pltpu.VMEM)
    out_shape = [jax.ShapeDtypeStruct((N_DEV, R, L), F32)]
    if with_sum:
        out_shape.append(jax.ShapeDtypeStruct((R, L), F32))
    return _pcall(
        body, name=name, in_specs=[vm], out_specs=[vm] * len(out_shape), out_shape=out_shape,
        scratch_shapes=[pltpu.SemaphoreType.DMA((N_DEV - 1,)), pltpu.SemaphoreType.DMA((N_DEV - 1,))],
        compiler_params=_params(),
    )(v)


def allgather_devices_hbm(v, name, seq_id):
    R, L = v.shape

    def body(v_ref, out_ref, send_sems, recv_sems, local_sem):
        x, y, c = _me()
        me = 4 * x + 2 * y + c
        peers = []
        for p in range(1, N_DEV):
            px, py = _flip(x, y, p >> 1)
            peers.append((px, py, (1 - c) if (p & 1) else c))
        _handshake(peers)
        lc = pltpu.make_async_copy(v_ref, out_ref.at[me], local_sem)
        lc.start()
        copies = []
        for p, (px, py, pc) in enumerate(peers):
            send = pltpu.make_async_remote_copy(src_ref=v_ref, dst_ref=out_ref.at[me], send_sem=send_sems.at[p],
                                                recv_sem=recv_sems.at[p], device_id=(px, py, pc), device_id_type=MESH)
            send.start()
            recv = pltpu.make_async_remote_copy(src_ref=v_ref, dst_ref=out_ref.at[4 * px + 2 * py + pc], send_sem=send_sems.at[p],
                                                recv_sem=recv_sems.at[p], device_id=(px, py, pc), device_id_type=MESH)
            copies.append((send, recv))
        for send, recv in copies:
            recv.wait_recv()
        for send, recv in copies:
            send.wait_send()
        lc.wait()

    return _seq_call(body, name=name, n_in=1, out_shape=[jax.ShapeDtypeStruct((N_DEV, R, L), F32)],
                     sem_shapes=[pltpu.SemaphoreType.DMA((N_DEV - 1,)), pltpu.SemaphoreType.DMA((N_DEV - 1,)),
                                 pltpu.SemaphoreType.DMA], collective_id=seq_id)(v)[0]


def sum_slots(g, name):
    n, R, L = g.shape
    tr = 216 if R % 216 == 0 else R
    assert R % tr == 0 and tr % SUBLANES == 0

    def body(g_ref, o_ref):
        s = g_ref[0]
        for k in range(1, n):
            s = s + g_ref[k]
        o_ref[...] = s

    return _pcall(body, name=name, grid=(R // tr,), in_specs=[pl.BlockSpec((n, tr, L), lambda i: (0, i, 0))],
                  out_specs=pl.BlockSpec((tr, L), lambda i: (i, 0)), out_shape=jax.ShapeDtypeStruct((R, L), F32),
                  compiler_params=_params(("parallel",)))(g)


def allgather_chips(v, name):
    R, L = v.shape

    def body(v_ref, out_ref, send_sems, recv_sems):
        x, y, c = _me()
        chip = 2 * x + y
        out_ref[chip] = v_ref[...]
        copies = []
        for p in range(1, N_CHIPS):
            px, py = _flip(x, y, p)
            send = pltpu.make_async_remote_copy(src_ref=v_ref, dst_ref=out_ref.at[chip], send_sem=send_sems.at[p - 1],
                                                recv_sem=recv_sems.at[p - 1], device_id=(px, py, c), device_id_type=MESH)
            send.start()
            recv = pltpu.make_async_remote_copy(src_ref=v_ref, dst_ref=out_ref.at[2 * px + py], send_sem=send_sems.at[p - 1],
                                                recv_sem=recv_sems.at[p - 1], device_id=(px, py, c), device_id_type=MESH)
            copies.append((send, recv))
        for send, recv in copies:
            recv.wait_recv()
        for send, recv in copies:
            send.wait_send()

    vm = pl.BlockSpec(memory_space=pltpu.VMEM)
    return _pcall(
        body, name=name, in_specs=[vm], out_specs=vm, out_shape=jax.ShapeDtypeStruct((N_CHIPS, R, L), F32),
        scratch_shapes=[pltpu.SemaphoreType.DMA((N_CHIPS - 1,)), pltpu.SemaphoreType.DMA((N_CHIPS - 1,))],
        compiler_params=_params(),
    )(v)


def _shard_window(ref, kind, shard_shape, chip, half):
    r, c = shard_shape
    hr = r // 2
    if kind == "col":
        return ref.at[pl.ds(pl.multiple_of(half * hr, hr), hr), pl.ds(pl.multiple_of(chip * c, c), c)]
    return ref.at[pl.ds(pl.multiple_of(chip * r + half * hr, hr), hr), :]


def allgather_weights(shards, kinds, name, seq_id=None):
    n = len(shards)
    fulls = []
    for s, kind in zip(shards, kinds):
        r, c = s.shape
        fulls.append(jax.ShapeDtypeStruct((r, N_CHIPS * c) if kind == "col" else (N_CHIPS * r, c), s.dtype))

    def body(*refs):
        srcs, outs = refs[:n], refs[n:2 * n]
        send_sems, recv_sems, fsend_sems, frecv_sems = refs[2 * n:]
        x, y, c = _me()
        chip = 2 * x + y
        sib = (x, y, 1 - c)
        if seq_id is not None:
            _handshake([(*_flip(x, y, p), c) for p in range(1, N_CHIPS)] + [sib])
        sends, fwds = [], []
        for i in range(n):
            shp = srcs[i].shape
            hr = shp[0] // 2
            my_half = srcs[i].at[pl.ds(pl.multiple_of(c * hr, hr), hr), :]
            for p in range(1, N_CHIPS):
                px, py = _flip(x, y, p)
                k = i * (N_CHIPS - 1) + p - 1
                cp = pltpu.make_async_remote_copy(src_ref=my_half, dst_ref=_shard_window(outs[i], kinds[i], shp, chip, c),
                                                  send_sem=send_sems.at[k], recv_sem=recv_sems.at[k],
                                                  device_id=(px, py, c), device_id_type=MESH)
                cp.start()
                sends.append(cp)
        for i in range(n):
            shp = srcs[i].shape
            for p in range(1, N_CHIPS):
                px, py = _flip(x, y, p)
                k = i * (N_CHIPS - 1) + p - 1
                landed = _shard_window(outs[i], kinds[i], shp, 2 * px + py, c)
                pltpu.make_async_remote_copy(src_ref=landed, dst_ref=landed, send_sem=send_sems.at[k], recv_sem=recv_sems.at[k],
                                             device_id=(px, py, c), device_id_type=MESH).wait_recv()
                fw = pltpu.make_async_remote_copy(src_ref=landed, dst_ref=landed, send_sem=fsend_sems.at[k],
                                                  recv_sem=frecv_sems.at[k], device_id=sib, device_id_type=MESH)
                fw.start()
                fwds.append(fw)
        for i in range(n):
            shp = srcs[i].shape
            for p in range(1, N_CHIPS):
                px, py = _flip(x, y, p)
                k = i * (N_CHIPS - 1) + p - 1
                other = _shard_window(outs[i], kinds[i], shp, 2 * px + py, 1 - c)
                pltpu.make_async_remote_copy(src_ref=other, dst_ref=other, send_sem=fsend_sems.at[k], recv_sem=frecv_sems.at[k],
                                             device_id=sib, device_id_type=MESH).wait_recv()
        for cp in sends + fwds:
            cp.wait_send()

    nk = n * (N_CHIPS - 1)
    gathered = _hbm_comm_call(
        body, name=name, n_in=n, out_shape=fulls, seq_id=seq_id,
        sem_shapes=[pltpu.SemaphoreType.DMA((nk,)), pltpu.SemaphoreType.DMA((nk,)), pltpu.SemaphoreType.DMA((nk,)),
                    pltpu.SemaphoreType.DMA((nk,))],
    )(*shards)
    return gathered


def place_local_shards(fulls, shards, kinds, name):
    n = len(shards)
    chip = jnp.reshape(2 * lax.axis_index("x") + lax.axis_index("y"), (1,)).astype(I32)

    def body(ci_ref, *refs):
        for i in range(n):
            refs[2 * n + i][...] = refs[i][...]

    in_specs = [pl.BlockSpec(s.shape, lambda i, ci: (0, 0)) for s in shards] + [pl.BlockSpec(memory_space=pl.ANY)] * n
    out_specs = [pl.BlockSpec(s.shape, (lambda i, ci: (0, ci[0])) if k == "col" else (lambda i, ci: (ci[0], 0)))
                 for s, k in zip(shards, kinds)]
    gs = pltpu.PrefetchScalarGridSpec(num_scalar_prefetch=1, grid=(1,), in_specs=in_specs, out_specs=out_specs)
    return _pcall(body, name=name, grid_spec=gs, out_shape=[jax.ShapeDtypeStruct(f.shape, f.dtype) for f in fulls],
                  input_output_aliases={1 + n + i: i for i in range(n)}, compiler_params=_params(("arbitrary",)))(chip, *shards, *fulls)


def _as_halves(g, kind, shard_shape):
    r, c = shard_shape
    if kind == "col":
        return g.reshape(2, r // 2, N_CHIPS * c)
    return g.reshape(N_CHIPS, 2, r // 2, c)


def exchange_sibling_halves(grads, kinds, shard_shapes, name, seq_id=None):
    n = len(grads)
    views = [_as_halves(g, k, s) for g, k, s in zip(grads, kinds, shard_shapes)]
    outs = []
    for k, (r, c) in zip(kinds, shard_shapes):
        outs.append(jax.ShapeDtypeStruct((r // 2, N_CHIPS * c) if k == "col" else (N_CHIPS, r // 2, c), grads[0].dtype))

    def body(*refs):
        srcs, dsts = refs[:n], refs[n:2 * n]
        send_sems, recv_sems = refs[2 * n:]
        x, y, c = _me()
        if seq_id is not None:
            _handshake([(x, y, 1 - c)])
        cps = []
        for i in range(n):
            src = srcs[i].at[1 - c] if kinds[i] == "col" else srcs[i].at[:, 1 - c]
            cp = pltpu.make_async_remote_copy(src_ref=src, dst_ref=dsts[i], send_sem=send_sems.at[i], recv_sem=recv_sems.at[i],
                                              device_id=(x, y, 1 - c), device_id_type=MESH)
            cp.start()
            cps.append(cp)
        for cp in cps:
            cp.wait_recv()
        for cp in cps:
            cp.wait_send()

    return _hbm_comm_call(body, name=name, n_in=n, out_shape=outs, seq_id=seq_id,
                          sem_shapes=[pltpu.SemaphoreType.DMA((n,)), pltpu.SemaphoreType.DMA((n,))])(*views)


def add_sibling_half(g, recv, kind, shard_shape, core_chip, name):
    r, c = shard_shape
    hr = r // 2
    gv = _as_halves(g, kind, shard_shape)
    tr = hr if hr <= 512 else (256 if hr % 256 == 0 else hr // 2)
    assert hr % tr == 0

    def body(ci_ref, g_ref, r_ref, h_ref, hb_ref):
        s = g_ref[...].astype(F32) + r_ref[...].astype(F32)
        hb_ref[...] = s.astype(BF)

        @pl.when(pl.program_id(1) == ci_ref[1])
        def _():
            h_ref[...] = s

    grid = (hr // tr, N_CHIPS)
    if kind == "col":
        g_spec = pl.BlockSpec((None, tr, c), lambda i, k, ci: (ci[0], i, k))
        o_spec = pl.BlockSpec((tr, c), lambda i, k, ci: (i, k))
    else:
        g_spec = pl.BlockSpec((None, None, tr, c), lambda i, k, ci: (k, ci[0], i, 0))
        o_spec = pl.BlockSpec((None, tr, c), lambda i, k, ci: (k, i, 0))
    own_spec = pl.BlockSpec((tr, c), lambda i, k, ci: (i, 0))
    gs = pltpu.PrefetchScalarGridSpec(num_scalar_prefetch=1, grid=grid, in_specs=[g_spec, o_spec], out_specs=[own_spec, o_spec])
    return _pcall(
        body, name=name, grid_spec=gs,
        out_shape=[jax.ShapeDtypeStruct((hr, c), F32), jax.ShapeDtypeStruct(recv.shape, BF)],
        compiler_params=_params(("parallel", "arbitrary")),
    )(core_chip, gv, recv)


def exchange_chip_pieces(hbs, kinds, shard_shapes, name, seq_id=None):
    n = len(hbs)
    outs = [jax.ShapeDtypeStruct((N_CHIPS - 1, r // 2, c), BF) for (r, c) in shard_shapes]

    def body(*refs):
        srcs, dsts = refs[:n], refs[n:2 * n]
        send_sems, recv_sems = refs[2 * n:]
        x, y, c = _me()
        if seq_id is not None:
            _handshake([(*_flip(x, y, p), c) for p in range(1, N_CHIPS)])
        cps = []
        for i in range(n):
            cc = shard_shapes[i][1]
            for p in range(1, N_CHIPS):
                px, py = _flip(x, y, p)
                pchip = 2 * px + py
                src = (srcs[i].at[:, pl.ds(pl.multiple_of(pchip * cc, cc), cc)] if kinds[i] == "col" else srcs[i].at[pchip])
                k = i * (N_CHIPS - 1) + p - 1
                cp = pltpu.make_async_remote_copy(src_ref=src, dst_ref=dsts[i].at[p - 1], send_sem=send_sems.at[k],
                                                  recv_sem=recv_sems.at[k], device_id=(px, py, c), device_id_type=MESH)
                cp.start()
                cps.append(cp)
        for cp in cps:
            cp.wait_recv()
        for cp in cps:
            cp.wait_send()

    nk = n * (N_CHIPS - 1)
    return _hbm_comm_call(body, name=name, n_in=n, out_shape=outs, seq_id=seq_id,
                          sem_shapes=[pltpu.SemaphoreType.DMA((nk,)), pltpu.SemaphoreType.DMA((nk,))])(*hbs)


def sum_chip_pieces(h_own, pieces, name):
    hr, c = h_own.shape
    tr = hr if hr <= 512 else (256 if hr % 256 == 0 else hr // 2)
    assert hr % tr == 0

    def body(h_ref, p_ref, q_ref):
        q_ref[...] = ((h_ref[...] + p_ref[0].astype(F32)) + p_ref[1].astype(F32)) + p_ref[2].astype(F32)

    blk = pl.BlockSpec((tr, c), lambda i: (i, 0))
    return _pcall(body, name=name, grid=(hr // tr,), in_specs=[blk, pl.BlockSpec((N_CHIPS - 1, tr, c), lambda i: (0, i, 0))],
                  out_specs=blk, out_shape=jax.ShapeDtypeStruct((hr, c), F32), compiler_params=_params(("parallel",)))(h_own, pieces)


def exchange_reduced_halves(qs, name, seq_id):
    n = len(qs)

    def body(*refs):
        srcs, dsts = refs[:n], refs[n:2 * n]
        send_sems, recv_sems = refs[2 * n:]
        x, y, c = _me()
        _handshake([(x, y, 1 - c)])
        cps = []
        for i in range(n):
            cp = pltpu.make_async_remote_copy(src_ref=srcs[i], dst_ref=dsts[i], send_sem=send_sems.at[i], recv_sem=recv_sems.at[i],
                                              device_id=(x, y, 1 - c), device_id_type=MESH)
            cp.start()
            cps.append(cp)
        for cp in cps:
            cp.wait_recv()
        for cp in cps:
            cp.wait_send()

    return _seq_call(body, name=name, n_in=n, out_shape=[jax.ShapeDtypeStruct(q.shape, F32) for q in qs],
                     sem_shapes=[pltpu.SemaphoreType.DMA((n,)), pltpu.SemaphoreType.DMA((n,))], collective_id=seq_id)(*qs)


def _rows128(a):
    return a.reshape(-1, LANES)


def _after(xs, *deps):
    flat = []
    for d in deps:
        flat.extend(d if isinstance(d, (list, tuple)) else [d])
    return list(lax.optimization_barrier((tuple(xs), tuple(flat)))[0])


def _block_diag(w):
    H, d, _ = w.shape
    eye = jnp.eye(H, dtype=w.dtype)
    return jnp.einsum("hde,hg->hdge", w, eye).reshape(H * d, H * d)


def _diag_blocks(g4, H, d):
    nb = g4.shape[0]
    per = LANES // d
    g = g4.reshape(nb, per, d, per, d)
    return jnp.stack([g[:, j, :, j, :] for j in range(per)], axis=1).reshape(H, d, d)


def kernel(x, c, w_mod, b_mod, g_ffn1, w_ffn1_in, w_ffn1_out, g_mix, w_in, conv_w, conv_b, ln_g, ln_b, rnn_conv_w, rnn_conv_b, w_a, b_a, w_i, b_i, lru_lambda, w_out, g_ffn2, w_ffn2_in, w_ffn2_out, w_fmod, b_fmod, g_final, loss_target, m_w_mod, m_b_mod, m_g_ffn1, m_w_ffn1_in, m_w_ffn1_out, m_g_mix, m_w_in, m_conv_w, m_conv_b, m_ln_g, m_ln_b, m_rnn_conv_w, m_rnn_conv_b, m_w_a, m_b_a, m_w_i, m_b_i, m_lru_lambda, m_w_out, m_g_ffn2, m_w_ffn2_in, m_w_ffn2_out, m_w_fmod, m_b_fmod, m_g_final, v_w_mod, v_b_mod, v_g_ffn1, v_w_ffn1_in, v_w_ffn1_out, v_g_mix, v_w_in, v_conv_w, v_conv_b, v_ln_g, v_ln_b, v_rnn_conv_w, v_rnn_conv_b, v_w_a, v_b_a, v_w_i, v_b_i, v_lru_lambda, v_w_out, v_g_ffn2, v_w_ffn2_in, v_w_ffn2_out, v_w_fmod, v_b_fmod, v_g_final):
    S, D = x.shape[1], x.shape[2]
    M = conv_b.shape[1]
    H, HD = w_a.shape[1], w_a.shape[2]
    nb = M // LANES
    ix, iy, ic = lax.axis_index("x"), lax.axis_index("y"), lax.axis_index("c")
    chip = 2 * ix + iy
    dev = 2 * chip + ic
    core_chip = jnp.stack([ic, chip]).astype(I32)
    cidx = core_chip
    xs = x[0]
    tgt = loss_target[0]

    kinds = ["col", "row"]
    w_f1, w_mx, w_f2 = [w_ffn1_in[0], w_ffn1_out[0]], [w_in[0], w_out[0]], [w_ffn2_in[0], w_ffn2_out[0]]
    as_bf = lambda ws: [w.astype(BF) for w in ws]
    shapes_of = lambda ws: [w.shape for w in ws]
    b_f1, b_mx, b_f2 = as_bf(w_f1), as_bf(w_mx), as_bf(w_f2)
    got_f1i = allgather_weights(b_f1[:1], kinds[:1], "gather_ffn1_in", seq_id=9)
    got_f1o = allgather_weights(b_f1[1:], kinds[1:], "gather_ffn1_out", seq_id=13)
    got_mx = allgather_weights(b_mx, kinds, "gather_mix", seq_id=1)
    got_f2 = allgather_weights(b_f2, kinds, "gather_ffn2", seq_id=2)

    c_all =allgather_devices(_rows128(c), "gather_c")[0].reshape(N_DEV, D)
    mod_cols = cond_matmul(c_all, w_mod[0], "mod_proj")
    fmod_cols = cond_matmul(c_all, w_fmod, "fmod_proj")
    convw_pad = jnp.pad(conv_w[0], ((0, CONV_TAPS - CONV_WIDTH), (0, 0)))
    rnnw_pad = jnp.pad(rnn_conv_w[0], ((0, SUBLANES - RNN_CONV_WIDTH), (0, 0)))
    n_mod, n_fmod = mod_cols.shape[1], fmod_cols.shape[1]
    small = jnp.concatenate([_rows128(mod_cols), _rows128(fmod_cols), convw_pad, rnnw_pad], axis=0)
    small4 = allgather_chips(small, "gather_cond")
    r0 = N_DEV * n_mod // LANES
    r1 = r0 + N_DEV * n_fmod // LANES
    mod_all = small4[:, :r0].reshape(N_CHIPS, N_DEV, n_mod)
    fmod_all = small4[:, r0:r1].reshape(N_CHIPS, N_DEV, n_fmod)
    convw4 = small4[:, r1:r1 + CONV_TAPS]
    rnnw4 = small4[:, r1 + CONV_TAPS:r1 + CONV_TAPS + SUBLANES]
    mod_row = lax.dynamic_index_in_dim(mod_all, dev, axis=1, keepdims=False).reshape(1, N_CHIPS * n_mod) + b_mod
    fmod_row = lax.dynamic_index_in_dim(fmod_all, dev, axis=1, keepdims=False).reshape(1, N_CHIPS * n_fmod) + b_fmod[None, :]
    vecs = jnp.concatenate([mod_row.reshape(9, D), fmod_row.reshape(2, D), g_ffn1, g_mix, g_ffn2, g_final[None, :],
                            jnp.zeros((1, D), F32)], axis=0)
    lnv = jnp.concatenate([ln_g, ln_b, jnp.zeros((SUBLANES - 2, M), F32)], axis=0)
    bda = _block_diag(w_a[0]).astype(BF)
    bdi = _block_diag(w_i[0]).astype(BF)

    def reduce_add(gs, recv, ws, tag, kinds_=kinds):
        pairs = [add_sibling_half(g, r_, k, w.shape, core_chip, f"add_sibling_{tag}{j}")
                 for j, (g, r_, k, w) in enumerate(zip(gs, recv, kinds_, ws))]
        return [p[0] for p in pairs], [p[1] for p in pairs]

    def reduce_sum(hs_, recv, ws, tag, kinds_=kinds):
        return [sum_chip_pieces(h_, p_, f"sum_chips_{tag}{j}") for j, (h_, p_) in enumerate(zip(hs_, recv))]

    rows1 = (R_SH1, R_SC1, R_GT1, R_G1)
    rows3 = (R_SH3, R_SC3, R_GT3, R_G3)
    (wi1,) = place_local_shards(got_f1i, b_f1[:1], kinds[:1], "place_ffn1_in")
    g1s, u1s, a1s = ffn_fwd_in(xs, vecs, wi1, rows1, "ffn1_fwd_in")
    (wo1,) = place_local_shards(_after(got_f1o, a1s), b_f1[1:], kinds[1:], "place_ffn1_out")
    x1, y1 = ffn_fwd_out(a1s, xs, vecs, wo1, rows1, "ffn1_fwd_out")
    win, wout = place_local_shards(_after(got_mx, x1), b_mx, kinds, "place_mix")
    proj = norm_matmul(x1, vecs, win, (R_SH2, R_SC2, R_G2), "mix_in_proj")
    cq = conv_fwd(proj, convw4, conv_b, "conv_fwd")
    xr, ra, ii, hh = rnn_fwd(proj, rnnw4, rnn_conv_b, bda, bdi, b_a, b_i, lru_lambda, "rnn_fwd")
    x2, ym, ycat = mix_out(cq, proj, hh, x1, vecs, lnv, wout, "mix_out")
    wi2, wo2 = place_local_shards(_after(got_f2, x2), b_f2, kinds, "place_ffn2")
    dx3, g2s, u2s, y2, vgf = ffn_fwd(x2, vecs, wi2, wo2, rows3, "ffn2_fwd", final_tgt=tgt)

    Fd = wo1.shape[0]
    tk = S
    dx2, act2, dg2, du2, h3b, dy2b, vg3 = ffn_bwd(dx3, x2, vecs, g2s, u2s, y2, wi2, wo2, rows3, "ffn2_bwd")
    gwo2 = matmul(act2, dy2b, "tn", tm=DW_TILE, tn=D, tk=tk, out_dtype=BF, name="ffn2_dwo")
    gwi2 = matmul(h3b, dg2, "tn", tm=D, tn=DW_TILE, tk=tk, out_dtype=BF, name="ffn2_dwg", out_cols=2 * Fd)
    gwi2 = matmul(h3b, du2, "tn", tm=D, tn=DW_TILE, tk=tk, out_dtype=BF, name="ffn2_dwu", out_cols=2 * Fd, col_off=Fd, prev=gwi2)
    recv1_f2 = exchange_sibling_halves([gwi2, gwo2], kinds, shapes_of(w_f2), "reduce1_ffn2", seq_id=3)
    dcq, dhout, duy, dymb, vgd, vgm = mix_out_bwd(dx2, ym, vecs, wout, cq, lnv, proj, hh, "mix_out_bwd")
    gwout = matmul(ycat, dymb, "tn", tm=2 * M, tn=DW_TILE, tk=tk, out_dtype=BF, name="mix_dwout")
    recv1_f2 = _after(recv1_f2, gwout)
    h_f2, hb_f2 = reduce_add([gwi2, gwo2], recv1_f2, w_f2, "ffn2_")
    recv2_f2 = exchange_chip_pieces(hb_f2, kinds, shapes_of(w_f2), "reduce2_ffn2", seq_id=4)
    duv, dug, dconvw4, dconvb = conv_bwd(_after([dcq], hb_f2)[0], proj, convw4, "conv_bwd")
    dux, dwa4, dwi4, drnnw4, rvec = rnn_bwd(dhout, hh, xr, ra, ii, proj, rnnw4, bda, bdi, lru_lambda, "rnn_bwd")
    dx1, h2b, dpb, vg2 = mix_in_bwd((duv, dug, dux, duy), x1, dx2, vecs, win, "mix_in_bwd")
    gwin = matmul(h2b, dpb, "tn", tm=D, tn=DW_TILE, tk=tk, out_dtype=BF, name="mix_dwin")
    recv1_mx = exchange_sibling_halves([gwin, gwout], kinds, shapes_of(w_mx), "reduce1_mix", seq_id=5)
    q_f2 = reduce_sum(_after(h_f2, gwin), recv2_f2, w_f2, "ffn2_")
    r_f2 = exchange_reduced_halves(q_f2, "reduce3_ffn2", seq_id=14)
    h_mx, hb_mx = reduce_add([gwin, gwout], _after(recv1_mx, q_f2), w_mx, "mix_")
    recv2_mx = exchange_chip_pieces(hb_mx, kinds, shapes_of(w_mx), "reduce2_mix", seq_id=6)
    dx0, act1, dg1, du1, h1b, dy1b, vg1 = ffn_bwd(_after([dx1], hb_mx)[0], xs, vecs, g1s, u1s, y1, wi1, wo1, rows1, "ffn1_bwd")
    dmod_row = jnp.concatenate([vg1[1:3], vg1[0:1], vg2[0:2], vgd[0:1], vg3[1:3], vg3[0:1]], axis=0)
    gains = jnp.concatenate([vg1[3:4], vg2[2:3], vg3[3:4], vgf[2:4]], axis=0)
    mvecs = jnp.concatenate([dconvb, vgm[0:2], rvec[0:4], jnp.zeros((1, M), F32)], axis=0)
    parts = [_rows128(dmod_row), _rows128(vgf[0:2]), _rows128(gains), _rows128(mvecs),
             _rows128(dconvw4), _rows128(drnnw4), _rows128(_diag_blocks(dwa4, H, HD)), _rows128(_diag_blocks(dwi4, H, HD))]
    sizes = [p.shape[0] for p in parts]
    packed = jnp.concatenate(parts, axis=0)
    gathered = allgather_devices_hbm(packed, "gather_small", seq_id=10)

    gwo1 = matmul(_after([act1], recv2_mx, packed)[0], dy1b, "tn", tm=DW_TILE, tn=D, tk=tk, out_dtype=BF, name="ffn1_dwo")
    w_f1o, w_f1i = w_f1[1:], w_f1[:1]
    recv1_f1o = exchange_sibling_halves([gwo1], ["row"], shapes_of(w_f1o), "reduce1_ffn1_out", seq_id=7)
    q_mx = reduce_sum(_after(h_mx, gwo1), recv2_mx, w_mx, "mix_")
    r_mx = exchange_reduced_halves(q_mx, "reduce3_mix", seq_id=15)
    gwi1 = matmul(_after([h1b], q_mx)[0], dg1, "tn", tm=D, tn=DW_TILE, tk=tk, out_dtype=BF, name="ffn1_dwg", out_cols=2 * Fd)
    h_f1o, hb_f1o = reduce_add([gwo1], _after(recv1_f1o, gwi1), w_f1o, "ffn1_out", ["row"])
    recv2_f1o = exchange_chip_pieces(hb_f1o, ["row"], shapes_of(w_f1o), "reduce2_ffn1_out", seq_id=11)
    gwi1 = matmul(h1b, _after([du1], hb_f1o, gathered)[0], "tn", tm=D, tn=DW_TILE, tk=tk, out_dtype=BF, name="ffn1_dwu", out_cols=2 * Fd,
                  col_off=Fd, prev=gwi1)
    recv1_f1i = exchange_sibling_halves([gwi1], ["col"], shapes_of(w_f1i), "reduce1_ffn1_in", seq_id=12)
    q_f1o = reduce_sum(_after(h_f1o, gwi1), recv2_f1o, w_f1o, "ffn1_out", ["row"])
    r_f1o = exchange_reduced_halves(q_f1o, "reduce3_ffn1_out", seq_id=16)
    summed = sum_slots(gathered, "sum_small")
    offs = [0]
    for s in sizes:
        offs.append(offs[-1] + s)
    seg = lambda k: summed[offs[k]:offs[k + 1]]
    g_b_mod = seg(0).reshape(1, 9 * D)
    g_b_fmod = seg(1).reshape(1, 2 * D)
    gsum = seg(2).reshape(5, D)
    loss = (0.5 / D) * jnp.sum(gsum[4])
    msum = seg(3).reshape(SUBLANES, M)
    g_conv_w = lax.dynamic_index_in_dim(seg(4).reshape(nb, CONV_TAPS, LANES), chip, axis=0, keepdims=False)[:CONV_WIDTH]
    g_rnn_w = lax.dynamic_index_in_dim(seg(5).reshape(nb, SUBLANES, LANES), chip, axis=0, keepdims=False)[:RNN_CONV_WIDTH]
    g_w_a = seg(6).reshape(H, HD, HD)
    g_w_i = seg(7).reshape(H, HD, HD)
    dmod_all = gathered[:, offs[0]:offs[1]].reshape(N_DEV, 9 * D)
    dfmod_all = gathered[:, offs[1]:offs[2]].reshape(N_DEV, 2 * D)
    dmod_cols = lax.dynamic_slice_in_dim(dmod_all, chip * n_mod, n_mod, axis=1)
    dfmod_cols = lax.dynamic_slice_in_dim(dfmod_all, chip * n_fmod, n_fmod, axis=1)

    h_f1i, hb_f1i = reduce_add([gwi1], _after(recv1_f1i, q_f1o), w_f1i, "ffn1_in", ["col"])
    recv2_f1i = exchange_chip_pieces(hb_f1i, ["col"], shapes_of(w_f1i), "reduce2_ffn1_in", seq_id=8)
    dmod_cols, dfmod_cols = _after([dmod_cols, dfmod_cols], hb_f1i)
    g_w_mod, d_w_mod, nm_w_mod, nv_w_mod = adam_cond(c_all, dmod_cols, w_mod[0], m_w_mod[0], v_w_mod[0], "adam_w_mod")
    g_w_fmod, d_w_fmod, nm_w_fmod, nv_w_fmod = adam_cond(c_all, dfmod_cols, w_fmod, m_w_fmod, v_w_fmod, "adam_w_fmod")

    def adam_group(ws, qs_, rs_, ms, vs, tags, after):
        qs_ = _after(list(qs_), *after) if after else list(qs_)
        return [adam_big(w, q_, r_, m, v, cidx, "adam_" + t) for w, q_, r_, m, v, t in zip(ws, qs_, rs_, ms, vs, tags)]

    ad_f2 = adam_group(w_f2, q_f2, r_f2, [m_w_ffn2_in[0], m_w_ffn2_out[0]], [v_w_ffn2_in[0], v_w_ffn2_out[0]],
                       ["ffn2_in", "ffn2_out"], [hb_f1i])
    ad_mx = adam_group(w_mx, q_mx, r_mx, [m_w_in[0], m_w_out[0]], [v_w_in[0], v_w_out[0]], ["w_in", "w_out"], [hb_f1i])
    ad_f1o = adam_group(w_f1o, q_f1o, r_f1o, [m_w_ffn1_out[0]], [v_w_ffn1_out[0]], ["ffn1_out"], [hb_f1i])
    q_f1i = reduce_sum(_after(h_f1i, ad_f2[0][0], ad_f2[1][0], ad_mx[0][0], ad_mx[1][0], ad_f1o[0][0], g_w_mod, g_w_fmod),
                       recv2_f1i, w_f1i, "ffn1_in", ["col"])
    r_f1i = exchange_reduced_halves(q_f1i, "reduce3_ffn1_in", seq_id=17)
    ad_f1i = adam_group(w_f1i, q_f1i, r_f1i, [m_w_ffn1_in[0]], [v_w_ffn1_in[0]], ["ffn1_in"], [])
    big_out = ad_f1i + ad_f1o + ad_mx + ad_f2

    flat2 = lambda a: a.reshape(-1, a.shape[-1])
    small_names = ["b_mod", "g_ffn1", "g_mix", "conv_w", "conv_b", "ln_g", "ln_b", "rnn_conv_w", "rnn_conv_b", "w_a", "b_a",
                   "w_i", "b_i", "lru_lambda", "g_ffn2", "b_fmod", "g_final"]
    small_w = [b_mod, g_ffn1, g_mix, conv_w, conv_b, ln_g, ln_b, rnn_conv_w, rnn_conv_b, w_a, b_a, w_i, b_i, lru_lambda,
               g_ffn2, b_fmod, g_final]
    small_m = [m_b_mod, m_g_ffn1, m_g_mix, m_conv_w, m_conv_b, m_ln_g, m_ln_b, m_rnn_conv_w, m_rnn_conv_b, m_w_a, m_b_a,
               m_w_i, m_b_i, m_lru_lambda, m_g_ffn2, m_b_fmod, m_g_final]
    small_v = [v_b_mod, v_g_ffn1, v_g_mix, v_conv_w, v_conv_b, v_ln_g, v_ln_b, v_rnn_conv_w, v_rnn_conv_b, v_w_a, v_b_a,
               v_w_i, v_b_i, v_lru_lambda, v_g_ffn2, v_b_fmod, v_g_final]
    small_g = [g_b_mod, gsum[0:1], gsum[1:2], g_conv_w, msum[0:1], msum[1:2], msum[2:3], g_rnn_w, msum[3:4], g_w_a, msum[4:5],
               g_w_i, msum[5:6], msum[6:7], gsum[2:3], g_b_fmod, gsum[3:4]]
    small_g = [g.reshape(w.shape) for g, w in zip(small_g, small_w)]
    two_d = lambda a: a.reshape(1, -1) if a.ndim == 1 else flat2(a)
    sd, sm, sv = adam_small([two_d(a) for a in small_w], [two_d(a) for a in small_g], [two_d(a) for a in small_m],
                            [two_d(a) for a in small_v], "adam_small")
    small = {}
    for k, nm in enumerate(small_names):
        shp = small_w[k].shape
        small[nm] = (small_g[k], sd[k].reshape(shp), sm[k].reshape(shp), sv[k].reshape(shp))

    big = {"w_mod": tuple(a[None] for a in (g_w_mod, d_w_mod, nm_w_mod, nv_w_mod)),
           "w_fmod": (g_w_fmod, d_w_fmod, nm_w_fmod, nv_w_fmod)}
    for nm, res in zip(["w_ffn1_in", "w_ffn1_out", "w_in", "w_out", "w_ffn2_in", "w_ffn2_out"], big_out):
        big[nm] = tuple(a[None] for a in res)
    order = ["w_mod", "b_mod", "g_ffn1", "w_ffn1_in", "w_ffn1_out", "g_mix", "w_in", "conv_w", "conv_b", "ln_g", "ln_b",
             "rnn_conv_w", "rnn_conv_b", "w_a", "b_a", "w_i", "b_i", "lru_lambda", "w_out", "g_ffn2", "w_ffn2_in",
             "w_ffn2_out", "w_fmod", "b_fmod", "g_final"]
    table = {**small, **big}
    outs = [loss, dx0[None]]
    for kind_ in range(4):
        outs.extend(table[nm][kind_] for nm in order)
    return tuple(outs)
```

```python
import functools

import jax
import jax.numpy as jnp
from jax import lax
from jax.experimental import pallas as pl
from jax.experimental.pallas import tpu as pltpu
from jax.experimental.pallas import tpu_sc as plsc

F32 = jnp.float32
BF = jnp.bfloat16
I32 = jnp.int32
MESH = pl.DeviceIdType.MESH

EPS = 1e-6
RG_C = 8.0
MACARON_W = 0.5
CONV_WIDTH = 31
RNN_CONV_WIDTH = 4
ADAM_LR = 0.001
ADAM_B1 = 0.9
ADAM_B2 = 0.999
ADAM_EPS = 1e-08
ADAM_WD = 0.01
ADAM_STEP = 10

LANES = 128
SUBLANES = 8
VMEM_LIMIT = 62 * 1024 * 1024
N_CHIPS = 4
N_DEV = 8

R_SH1, R_SC1, R_GT1, R_SH2, R_SC2, R_GT2, R_SH3, R_SC3, R_GT3, R_FSH, R_FSC, R_G1, R_G2, R_G3, R_GF = range(15)

CONTRACT_LAST = (((1,), (1,)), ((), ()))
CONTRACT_FIRST = (((0,), (0,)), ((), ()))


def _pcall(body, **kw):
    return pl.pallas_call(body, **kw)


def _params(sem=None, vmem=VMEM_LIMIT):
    if sem is None:
        return pltpu.CompilerParams(vmem_limit_bytes=vmem)
    return pltpu.CompilerParams(dimension_semantics=sem, vmem_limit_bytes=vmem)


def _row(ref, r):
    return ref[r:r + 1, :]


def _sigmoid(x):
    return 0.5 * jnp.tanh(0.5 * x) + 0.5


def _colsum(x):
    return jnp.sum(x, axis=0, keepdims=True)


def _rowmean(x):
    return jnp.mean(x, axis=-1, keepdims=True)


def matmul(a, b, mode, *, tm, tn, tk, name, out_dtype=F32, out_cols=None, col_off=0, prev=None):
    if mode == "nn":
        (M, K), (K2, N) = a.shape, b.shape
    elif mode == "nt":
        (M, K), (N, K2) = a.shape, b.shape
    else:
        (K, M), (K2, N) = a.shape, b.shape
    assert K == K2 and M % tm == 0 and N % tn == 0 and K % tk == 0 and col_off % tn == 0
    nk = K // tk
    out_cols = N if out_cols is None else out_cols
    off = col_off // tn

    def body(*refs):
        if prev is None:
            a_ref, b_ref, o_ref, acc = refs
        else:
            a_ref, b_ref, _, o_ref, acc = refs
        k = pl.program_id(2)
        av = a_ref[...].astype(BF)
        bv = b_ref[...].astype(BF)
        if mode == "nn":
            part = jnp.dot(av, bv, preferred_element_type=F32)
        elif mode == "nt":
            part = lax.dot_general(av, bv, CONTRACT_LAST, preferred_element_type=F32)
        else:
            part = lax.dot_general(av, bv, CONTRACT_FIRST, preferred_element_type=F32)
        if nk == 1:
            o_ref[...] = part.astype(out_dtype)
            return

        @pl.when(k == 0)
        def _():
            acc[...] = part

        @pl.when((k > 0) & (k < nk - 1))
        def _():
            acc[...] += part

        @pl.when(k == nk - 1)
        def _():
            o_ref[...] = (acc[...] + part).astype(out_dtype)

    if mode == "nn":
        a_spec = pl.BlockSpec((tm, tk), lambda m, n, k: (m, k))
        b_spec = pl.BlockSpec((tk, tn), lambda m, n, k: (k, n))
    elif mode == "nt":
        a_spec = pl.BlockSpec((tm, tk), lambda m, n, k: (m, k))
        b_spec = pl.BlockSpec((tn, tk), lambda m, n, k: (n, k))
    else:
        a_spec = pl.BlockSpec((tk, tm), lambda m, n, k: (k, m))
        b_spec = pl.BlockSpec((tk, tn), lambda m, n, k: (k, n))
    in_specs = [a_spec, b_spec]
    args = [a, b]
    aliases = {}
    if prev is not None:
        in_specs.append(pl.BlockSpec(memory_space=pl.ANY))
        args.append(prev)
        aliases = {2: 0}
    return _pcall(
        body, name=name, grid=(M // tm, N // tn, nk), in_specs=in_specs,
        out_specs=pl.BlockSpec((tm, tn), lambda m, n, k: (m, n + off)),
        out_shape=jax.ShapeDtypeStruct((M, out_cols), out_dtype),
        scratch_shapes=[pltpu.VMEM((tm, tn), F32)], input_output_aliases=aliases,
        compiler_params=_params(("parallel", "parallel", "arbitrary")),
    )(*args)


def cond_matmul(c_all, w, name):
    B, K = c_all.shape
    N = w.shape[1]
    tn = 256
    assert N % tn == 0

    def body(c_ref, w_ref, o_ref):
        cv = c_ref[...]
        ca = cv * _sigmoid(cv)
        o_ref[...] = jnp.dot(ca.astype(BF), w_ref[...].astype(BF), preferred_element_type=F32)

    return _pcall(
        body, name=name, grid=(N // tn,),
        in_specs=[pl.BlockSpec((B, K), lambda n: (0, 0)), pl.BlockSpec((K, tn), lambda n: (0, n))],
        out_specs=pl.BlockSpec((B, tn), lambda n: (0, n)),
        out_shape=jax.ShapeDtypeStruct((B, N), F32), compiler_params=_params(("parallel",)),
    )(c_all, w)


FFN_FWD_TS = 512
FFN_BWD_TS = 256
DW_TILE = 256


def _resident(shape, index_map):
    return pl.BlockSpec(shape, index_map, pipeline_mode=pl.Buffered(1))


def _streamed(shape, index_map):
    return pl.BlockSpec(shape, index_map)


def _final_norm_loss_grad(xv, t, v_ref, vg_ref):
    D = xv.shape[-1]
    r = lax.rsqrt(_rowmean(xv * xv) + EPS)
    n = xv * r
    g = _row(v_ref, R_GF)
    sc1 = 1.0 + _row(v_ref, R_FSC)
    gsc = g * sc1
    e = n * gsc + _row(v_ref, R_FSH) - t
    vg_ref[3:4, :] += _colsum(e * e)
    dout = e * (1.0 / D)
    dn_ = dout * n
    vg_ref[0:1, :] += _colsum(dout)
    vg_ref[1:2, :] += _colsum(dn_) * g
    vg_ref[2:3, :] += _colsum(dn_) * sc1
    dn = dout * gsc
    return r * (dn - n * _rowmean(dn * n))


def ffn_fwd(x, vecs, wi, wo, rows, name, final_tgt=None):
    r_sh, r_sc, r_gt, r_g = rows
    S, D = x.shape
    Fd = wo.shape[0]
    ts = min(FFN_FWD_TS, S)
    with_final = final_tgt is not None

    def body(*refs):
        if with_final:
            x_ref, v_ref, wg_ref, wu_ref, wo_ref, t_ref, xo_ref, g_ref, u_ref, y_ref, vg_ref = refs
        else:
            x_ref, v_ref, wg_ref, wu_ref, wo_ref, xo_ref, g_ref, u_ref, y_ref = refs
        xv = x_ref[...]
        r = lax.rsqrt(_rowmean(xv * xv) + EPS)
        gs = _row(v_ref, r_g) * (1.0 + _row(v_ref, r_sc))
        hb = (xv * r * gs + _row(v_ref, r_sh)).astype(BF)
        G = jnp.dot(hb, wg_ref[...], preferred_element_type=F32)
        U = jnp.dot(hb, wu_ref[...], preferred_element_type=F32)
        g_ref[...] = G.astype(BF)
        u_ref[...] = U.astype(BF)
        act = (G * _sigmoid(G) * U).astype(BF)
        Y = jnp.dot(act, wo_ref[...], preferred_element_type=F32)
        y_ref[...] = Y.astype(BF)
        xo = xv + (MACARON_W * _row(v_ref, r_gt)) * Y
        if with_final:
            @pl.when(pl.program_id(0) == 0)
            def _():
                vg_ref[...] = jnp.zeros_like(vg_ref)

            xo_ref[...] = _final_norm_loss_grad(xo, t_ref[...], v_ref, vg_ref)
        else:
            xo_ref[...] = xo

    tok = pl.BlockSpec((ts, D), lambda i: (i, 0))
    hid = pl.BlockSpec((ts, Fd), lambda i: (i, 0))
    in_specs = [tok, pl.BlockSpec(vecs.shape, lambda i: (0, 0)), _resident((D, Fd), lambda i: (0, 0)),
                _resident((D, Fd), lambda i: (0, 1)), _resident((Fd, D), lambda i: (0, 0))]
    out_specs = [tok, hid, hid, tok]
    out_shape = [jax.ShapeDtypeStruct((S, D), F32), jax.ShapeDtypeStruct((S, Fd), BF),
                 jax.ShapeDtypeStruct((S, Fd), BF), jax.ShapeDtypeStruct((S, D), BF)]
    args = [x, vecs, wi, wi, wo]
    if with_final:
        in_specs.append(tok)
        args.append(final_tgt)
        out_specs.append(pl.BlockSpec((SUBLANES, D), lambda i: (0, 0)))
        out_shape.append(jax.ShapeDtypeStruct((SUBLANES, D), F32))
    return _pcall(body, name=name, grid=(S // ts,), in_specs=in_specs, out_specs=out_specs, out_shape=out_shape,
                  compiler_params=_params(("arbitrary",)))(*args)


def ffn_fwd_in(x, vecs, wi, rows, name):
    r_sh, r_sc, r_gt, r_g = rows
    S, D = x.shape
    Fd = wi.shape[1] // 2
    ts = min(FFN_FWD_TS, S)

    def body(x_ref, v_ref, wg_ref, wu_ref, g_ref, u_ref, a_ref):
        xv = x_ref[...]
        r = lax.rsqrt(_rowmean(xv * xv) + EPS)
        gs = _row(v_ref, r_g) * (1.0 + _row(v_ref, r_sc))
        hb = (xv * r * gs + _row(v_ref, r_sh)).astype(BF)
        G = jnp.dot(hb, wg_ref[...], preferred_element_type=F32)
        U = jnp.dot(hb, wu_ref[...], preferred_element_type=F32)
        g_ref[...] = G.astype(BF)
        u_ref[...] = U.astype(BF)
        a_ref[...] = (G * _sigmoid(G) * U).astype(BF)

    hid = pl.BlockSpec((ts, Fd), lambda i: (i, 0))
    return _pcall(
        body, name=name, grid=(S // ts,),
        in_specs=[pl.BlockSpec((ts, D), lambda i: (i, 0)), pl.BlockSpec(vecs.shape, lambda i: (0, 0)),
                  _resident((D, Fd), lambda i: (0, 0)), _resident((D, Fd), lambda i: (0, 1))],
        out_specs=[hid, hid, hid], out_shape=[jax.ShapeDtypeStruct((S, Fd), BF)] * 3,
        compiler_params=_params(("arbitrary",)),
    )(x, vecs, wi, wi)


def ffn_fwd_out(act, x, vecs, wo, rows, name):
    r_sh, r_sc, r_gt, r_g = rows
    S, D = x.shape
    Fd = wo.shape[0]
    ts = min(FFN_FWD_TS, S)

    def body(a_ref, x_ref, v_ref, wo_ref, xo_ref, y_ref):
        Y = jnp.dot(a_ref[...], wo_ref[...], preferred_element_type=F32)
        y_ref[...] = Y.astype(BF)
        xo_ref[...] = x_ref[...] + (MACARON_W * _row(v_ref, r_gt)) * Y

    tok = pl.BlockSpec((ts, D), lambda i: (i, 0))
    return _pcall(
        body, name=name, grid=(S // ts,),
        in_specs=[pl.BlockSpec((ts, Fd), lambda i: (i, 0)), tok, pl.BlockSpec(vecs.shape, lambda i: (0, 0)),
                  _resident((Fd, D), lambda i: (0, 0))],
        out_specs=[tok, tok], out_shape=[jax.ShapeDtypeStruct((S, D), F32), jax.ShapeDtypeStruct((S, D), BF)],
        compiler_params=_params(("arbitrary",)),
    )(act, x, vecs, wo)


def ffn_bwd(dxo, x, vecs, gs_, us_, y, wi, wo, rows, name):
    r_sh, r_sc, r_gt, r_g = rows
    S, D = x.shape
    Fd = wo.shape[0]
    ts = min(FFN_BWD_TS, S)

    def body(dxo_ref, x_ref, v_ref, g_ref, u_ref, y_ref, wg_ref, wu_ref, wo_ref,
             dx_ref, act_ref, dg_ref, du_ref, hb_ref, dyb_ref, vg_ref):
        @pl.when(pl.program_id(0) == 0)
        def _():
            vg_ref[...] = jnp.zeros_like(vg_ref)

        dxo_v = dxo_ref[...]
        dyb = ((MACARON_W * _row(v_ref, r_gt)) * dxo_v).astype(BF)
        dyb_ref[...] = dyb
        vg_ref[0:1, :] += MACARON_W * _colsum(dxo_v * y_ref[...].astype(F32))
        dA = lax.dot_general(dyb, wo_ref[...], CONTRACT_LAST, preferred_element_type=F32)
        G = g_ref[...].astype(F32)
        U = u_ref[...].astype(F32)
        sg = _sigmoid(G)
        sl = G * sg
        dU = (dA * sl).astype(BF)
        dG = (dA * U * (sg * (1.0 + G * (1.0 - sg)))).astype(BF)
        act_ref[...] = (sl * U).astype(BF)
        dg_ref[...] = dG
        du_ref[...] = dU
        dh = (lax.dot_general(dG, wg_ref[...], CONTRACT_LAST, preferred_element_type=F32)
              + lax.dot_general(dU, wu_ref[...], CONTRACT_LAST, preferred_element_type=F32))
        xv = x_ref[...]
        r = lax.rsqrt(_rowmean(xv * xv) + EPS)
        n = xv * r
        g = _row(v_ref, r_g)
        sc1 = 1.0 + _row(v_ref, r_sc)
        gsc = g * sc1
        hb_ref[...] = (n * gsc + _row(v_ref, r_sh)).astype(BF)
        dhn = dh * n
        vg_ref[1:2, :] += _colsum(dh)
        vg_ref[2:3, :] += _colsum(dhn) * g
        vg_ref[3:4, :] += _colsum(dhn) * sc1
        dn = dh * gsc
        dx_ref[...] = dxo_v + r * (dn - n * _rowmean(dn * n))

    tok = pl.BlockSpec((ts, D), lambda i: (i, 0))
    hid = pl.BlockSpec((ts, Fd), lambda i: (i, 0))
    return _pcall(
        body, name=name, grid=(S // ts,),
        in_specs=[tok, tok, pl.BlockSpec(vecs.shape, lambda i: (0, 0)), hid, hid, tok, _resident((D, Fd), lambda i: (0, 0)),
                  _resident((D, Fd), lambda i: (0, 1)), _resident((Fd, D), lambda i: (0, 0))],
        out_specs=[tok, hid, hid, hid, tok, tok, pl.BlockSpec((SUBLANES, D), lambda i: (0, 0))],
        out_shape=[jax.ShapeDtypeStruct((S, D), F32), jax.ShapeDtypeStruct((S, Fd), BF),
                   jax.ShapeDtypeStruct((S, Fd), BF), jax.ShapeDtypeStruct((S, Fd), BF),
                   jax.ShapeDtypeStruct((S, D), BF), jax.ShapeDtypeStruct((S, D), BF),
                   jax.ShapeDtypeStruct((SUBLANES, D), F32)],
        compiler_params=_params(("arbitrary",)),
    )(dxo, x, vecs, gs_, us_, y, wi, wi, wo)


def norm_matmul(x, vecs, w, rows, name):
    r_sh, r_sc, r_g = rows
    S, D = x.shape
    N = w.shape[1]
    ts = min(512, S)

    def body(x_ref, v_ref, w_ref, o_ref):
        xv = x_ref[...]
        r = lax.rsqrt(_rowmean(xv * xv) + EPS)
        gs = _row(v_ref, r_g) * (1.0 + _row(v_ref, r_sc))
        hb = (xv * r * gs + _row(v_ref, r_sh)).astype(BF)
        o_ref[...] = jnp.dot(hb, w_ref[...], preferred_element_type=F32).astype(BF)

    return _pcall(
        body, name=name, grid=(S // ts,),
        in_specs=[pl.BlockSpec((ts, D), lambda i: (i, 0)), pl.BlockSpec(vecs.shape, lambda i: (0, 0)),
                  _resident((D, N), lambda i: (0, 0))],
        out_specs=pl.BlockSpec((ts, N), lambda i: (i, 0)),
        out_shape=jax.ShapeDtypeStruct((S, N), BF),
        compiler_params=_params(("arbitrary",)),
    )(x, vecs, w)


LOOKAHEAD = 2
SEQ_TT = 256
SCAN_SEGMENTS = 4
CONV_BWD_TT = 128
CONV_PAD = 32
CONV_TAPS = 32


def conv_fwd(proj, convw4, conv_b, name):
    S = proj.shape[0]
    M = conv_b.shape[1]
    nb = M // LANES
    tt = min(SEQ_TT, S)

    def body(uv_ref, ug_ref, w_ref, b_ref, cq_ref, qp):
        qp[0:CONV_PAD, :] = jnp.zeros((CONV_PAD, LANES), F32)

        def step(t, carry):
            base = pl.multiple_of(t * tt, tt)
            qp[pl.ds(base + CONV_PAD, tt), :] = uv_ref[pl.ds(base, tt), :].astype(F32) * _sigmoid(ug_ref[pl.ds(base, tt), :].astype(F32))
            acc = jnp.broadcast_to(b_ref[...], (tt, LANES))
            for k in range(CONV_WIDTH):
                acc = acc + w_ref[k:k + 1, :] * qp[pl.ds(base + (CONV_PAD - CONV_WIDTH + 1) + k, tt), :]
            cq_ref[pl.ds(base, tt), :] = acc
            return carry

        lax.fori_loop(0, S // tt, step, 0)

    return _pcall(
        body, name=name, grid=(nb,),
        in_specs=[pl.BlockSpec((S, LANES), lambda c: (0, c)), pl.BlockSpec((S, LANES), lambda c: (0, c + nb)),
                  pl.BlockSpec((None, CONV_TAPS, LANES), lambda c: (c, 0, 0)), pl.BlockSpec((1, LANES), lambda c: (0, c))],
        out_specs=pl.BlockSpec((S, LANES), lambda c: (0, c)),
        out_shape=jax.ShapeDtypeStruct((S, M), F32),
        scratch_shapes=[pltpu.VMEM((S + CONV_PAD, LANES), F32)],
        compiler_params=_params(("arbitrary",)),
    )(proj, proj, convw4, conv_b)


def conv_bwd(dcq, proj, convw4, name):
    S, M = dcq.shape
    nb = M // LANES
    tt = min(CONV_BWD_TT, S)

    def body(dcq_ref, uv_ref, ug_ref, w_ref, duv_ref, dug_ref, dw_ref, db_ref, dp, dw8, db8):
        dp[S:S + CONV_PAD, :] = jnp.zeros((CONV_PAD, LANES), F32)
        dw8[...] = jnp.zeros_like(dw8)
        db8[...] = jnp.zeros_like(db8)

        def fill(t, carry):
            base = pl.multiple_of(t * tt, tt)
            dp[pl.ds(base, tt), :] = dcq_ref[pl.ds(base, tt), :].astype(F32)
            return carry

        lax.fori_loop(0, S // tt, fill, 0)

        def step(t, carry):
            base = pl.multiple_of(t * tt, tt)
            uv = uv_ref[pl.ds(base, tt), :].astype(F32)
            sg = _sigmoid(ug_ref[pl.ds(base, tt), :].astype(F32))
            q_t = uv * sg
            db8[...] += dp[pl.ds(base, tt), :].reshape(tt // SUBLANES, SUBLANES, LANES).sum(axis=0)
            dq = jnp.zeros((tt, LANES), F32)
            for k in range(CONV_WIDTH):
                shifted = dp[pl.ds(base + (CONV_WIDTH - 1) - k, tt), :]
                dw8[k] += (shifted * q_t).reshape(tt // SUBLANES, SUBLANES, LANES).sum(axis=0)
                dq = dq + w_ref[k:k + 1, :] * shifted
            duv_ref[pl.ds(base, tt), :] = (dq * sg).astype(BF)
            dug_ref[pl.ds(base, tt), :] = (dq * uv * sg * (1.0 - sg)).astype(BF)
            return carry

        lax.fori_loop(0, S // tt, step, 0)
        dw_ref[...] = jnp.zeros_like(dw_ref)
        for k in range(CONV_WIDTH):
            dw_ref[k:k + 1, :] = _colsum(dw8[k])
        db_ref[...] = _colsum(db8[...])

    col = lambda o: pl.BlockSpec((S, LANES), lambda c: (0, c + o))
    return _pcall(
        body, name=name, grid=(nb,),
        in_specs=[col(0), col(0), col(nb), pl.BlockSpec((None, CONV_TAPS, LANES), lambda c: (c, 0, 0))],
        out_specs=[col(0), col(0), pl.BlockSpec((None, CONV_TAPS, LANES), lambda c: (c, 0, 0)),
                   pl.BlockSpec((1, LANES), lambda c: (0, c))],
        out_shape=[jax.ShapeDtypeStruct((S, M), BF), jax.ShapeDtypeStruct((S, M), BF),
                   jax.ShapeDtypeStruct((nb, CONV_TAPS, LANES), F32), jax.ShapeDtypeStruct((1, M), F32)],
        scratch_shapes=[pltpu.VMEM((S + CONV_PAD, LANES), F32),
                        pltpu.VMEM((CONV_TAPS, SUBLANES, LANES), F32), pltpu.VMEM((SUBLANES, LANES), F32)],
        compiler_params=_params(("arbitrary",)),
    )(dcq, proj, proj, convw4)


def _log_sigmoid(x):
    return jnp.minimum(x, 0.0) - jnp.log(1.0 + jnp.exp(-jnp.abs(x)))


def _rg_gate_terms(ra, ls):
    la = RG_C * ra * ls
    a = jnp.exp(la)
    th = jnp.tanh(la)
    mult = jnp.sqrt(-2.0 * th / (1.0 - th))
    return a, mult


def rnn_fwd(proj, rnnw4, rnn_b, bda, bdi, b_a, b_i, lam, name):
    S = proj.shape[0]
    M = rnn_b.shape[1]
    nb = M // LANES
    tt = min(SEQ_TT, S)
    KW = RNN_CONV_WIDTH
    nseg = SCAN_SEGMENTS if S % (SCAN_SEGMENTS * tt) == 0 else 1

    def body(ux_ref, w_ref, rb_ref, bda_ref, bdi_ref, ba_ref, bi_ref, lam_ref,
             xr_ref, ra_ref, ii_ref, h_ref, uxp, a_sc, b_sc):
        uxp[0:SUBLANES, :] = jnp.zeros((SUBLANES, LANES), F32)
        ls = _log_sigmoid(lam_ref[...])

        def step(t, carry):
            base = pl.multiple_of(t * tt, tt)
            uxp[pl.ds(base + SUBLANES, tt), :] = ux_ref[pl.ds(base, tt), :].astype(F32)
            xr = jnp.broadcast_to(rb_ref[...], (tt, LANES))
            for k in range(KW):
                xr = xr + w_ref[k:k + 1, :] * uxp[pl.ds(base + (SUBLANES - KW + 1) + k, tt), :]
            xb = xr.astype(BF)
            ra = _sigmoid(jnp.dot(xb, bda_ref[...], preferred_element_type=F32) + ba_ref[...])
            ii = _sigmoid(jnp.dot(xb, bdi_ref[...], preferred_element_type=F32) + bi_ref[...])
            a, mult = _rg_gate_terms(ra, ls)
            xr_ref[pl.ds(base, tt), :] = xr
            ra_ref[pl.ds(base, tt), :] = ra
            ii_ref[pl.ds(base, tt), :] = ii
            a_sc[pl.ds(base, tt), :] = a
            b_sc[pl.ds(base, tt), :] = mult * (ii * xr)
            return carry

        lax.fori_loop(0, S // tt, step, 0)

        rows = lax.broadcasted_iota(I32, (SUBLANES, LANES), 0)
        seg = S // nseg
        last = lambda v: jnp.broadcast_to(v[SUBLANES - 1:SUBLANES, :], (SUBLANES, LANES))

        def scan(t, carry):
            hs, ps = carry
            new_h, new_p = [], []
            for s in range(nseg):
                base = pl.multiple_of(s * seg + t * SUBLANES, SUBLANES)
                A = a_sc[pl.ds(base, SUBLANES), :]
                B = b_sc[pl.ds(base, SUBLANES), :]
                for d in (1, 2, 4):
                    As = jnp.where(rows >= d, pltpu.roll(A, d, axis=0), 1.0)
                    Bs = jnp.where(rows >= d, pltpu.roll(B, d, axis=0), 0.0)
                    B = A * Bs + B
                    A = A * As
                hh = B + A * hs[s]
                h_ref[pl.ds(base, SUBLANES), :] = hh
                pp = A * ps[s]
                if s > 0:
                    a_sc[pl.ds(base, SUBLANES), :] = pp
                new_h.append(last(hh))
                new_p.append(last(pp))
            return tuple(new_h), tuple(new_p)

        zero8 = jnp.zeros((SUBLANES, LANES), F32)
        one8 = jnp.ones((SUBLANES, LANES), F32)
        hs, ps = lax.fori_loop(0, seg // SUBLANES, scan, ((zero8,) * nseg, (one8,) * nseg))
        carry_in = hs[0]
        for s in range(1, nseg):
            c_row = carry_in[0:1, :]

            def fix(t, c, s=s, c_row=c_row):
                base = pl.multiple_of(s * seg + t * tt, tt)
                h_ref[pl.ds(base, tt), :] = h_ref[pl.ds(base, tt), :] + a_sc[pl.ds(base, tt), :] * c_row
                return c

            lax.fori_loop(0, seg // tt, fix, 0)
            carry_in = hs[s] + ps[s] * carry_in

    col = lambda o: pl.BlockSpec((S, LANES), lambda c: (0, c + o))
    vec = pl.BlockSpec((1, LANES), lambda c: (0, c))
    diag = pl.BlockSpec((LANES, LANES), lambda c: (c, c))
    return _pcall(
        body, name=name, grid=(nb,),
        in_specs=[col(2 * nb), pl.BlockSpec((None, SUBLANES, LANES), lambda c: (c, 0, 0)), vec, diag, diag, vec, vec, vec],
        out_specs=[col(0)] * 4,
        out_shape=[jax.ShapeDtypeStruct((S, M), F32)] * 4,
        scratch_shapes=[pltpu.VMEM((S + SUBLANES, LANES), F32), pltpu.VMEM((S, LANES), F32), pltpu.VMEM((S, LANES), F32)],
        compiler_params=_params(("arbitrary",)),
    )(proj, rnnw4, rnn_b, bda, bdi, b_a, b_i, lam)


def rnn_bwd(dhout, h, xr, ra, ii, proj, rnnw4, bda, bdi, lam, name):
    S, M = h.shape
    nb = M // LANES
    tt = min(SEQ_TT, S)
    KW = RNN_CONV_WIDTH
    SL = SUBLANES
    nseg = SCAN_SEGMENTS if S % (SCAN_SEGMENTS * tt) == 0 else 1

    def body(dh_ref, h_ref, xr_ref, ra_ref, ii_ref, ux_ref, w_ref, bda_ref, bdi_ref, lam_ref,
             dux_ref, dwa_ref, dwi_ref, drw_ref, vec_ref,
             a_sc, hp, g_sc, dpa_sc, dpi_sc, dxp, uxp, acc8, drw8, p_sc):
        zero8 = jnp.zeros((SL, LANES), F32)
        a_sc[S:S + SL, :] = zero8
        hp[0:SL, :] = zero8
        dxp[S:S + SL, :] = zero8
        uxp[0:SL, :] = zero8
        acc8[...] = jnp.zeros_like(acc8)
        drw8[...] = jnp.zeros_like(drw8)
        lamv = lam_ref[...]
        ls = _log_sigmoid(lamv)

        def fill(t, carry):
            base = pl.multiple_of(t * tt, tt)
            a_sc[pl.ds(base, tt), :] = jnp.exp(RG_C * ra_ref[pl.ds(base, tt), :] * ls)
            hp[pl.ds(base + SL, tt), :] = h_ref[pl.ds(base, tt), :]
            uxp[pl.ds(base + SL, tt), :] = ux_ref[pl.ds(base, tt), :].astype(F32)
            return carry

        lax.fori_loop(0, S // tt, fill, 0)

        rows = lax.broadcasted_iota(I32, (SL, LANES), 0)
        seg = S // nseg
        nt8 = seg // SL
        first = lambda v: jnp.broadcast_to(v[0:1, :], (SL, LANES))

        def rscan(t, carry):
            gs, ps = carry
            new_g, new_p = [], []
            for s in range(nseg):
                base = pl.multiple_of(s * seg + (nt8 - 1 - t) * SL, SL)
                A = a_sc[pl.ds(base + 1, SL), :]
                B = dh_ref[pl.ds(base, SL), :]
                for d in (1, 2, 4):
                    As = jnp.where(rows < SL - d, pltpu.roll(A, SL - d, axis=0), 1.0)
                    Bs = jnp.where(rows < SL - d, pltpu.roll(B, SL - d, axis=0), 0.0)
                    B = A * Bs + B
                    A = A * As
                g = B + A * gs[s]
                g_sc[pl.ds(base, SL), :] = g
                pp = A * ps[s]
                if s < nseg - 1:
                    p_sc[pl.ds(base, SL), :] = pp
                new_g.append(first(g))
                new_p.append(first(pp))
            return tuple(new_g), tuple(new_p)

        one8 = jnp.ones((SL, LANES), F32)
        gs, ps = lax.fori_loop(0, nt8, rscan, ((zero8,) * nseg, (one8,) * nseg))
        carry_in = gs[nseg - 1]
        for s in range(nseg - 2, -1, -1):
            c_row = carry_in[0:1, :]

            def fix(t, c, s=s, c_row=c_row):
                base = pl.multiple_of(s * seg + t * tt, tt)
                g_sc[pl.ds(base, tt), :] = g_sc[pl.ds(base, tt), :] + p_sc[pl.ds(base, tt), :] * c_row
                return c

            lax.fori_loop(0, seg // tt, fix, 0)
            carry_in = gs[s] + ps[s] * carry_in

        def red8(v):
            return v.reshape(tt // SL, SL, LANES).sum(axis=0)

        def step(t, carry):
            base = pl.multiple_of(t * tt, tt)
            g = g_sc[pl.ds(base, tt), :]
            hprev = hp[pl.ds(base + SL - 1, tt), :]
            xr_t = xr_ref[pl.ds(base, tt), :]
            ra_t = ra_ref[pl.ds(base, tt), :]
            ii_t = ii_ref[pl.ds(base, tt), :]
            a, mult = _rg_gate_terms(ra_t, ls)
            gx = g * xr_t
            dmult = gx * ii_t
            dii = gx * mult
            dxr = g * (mult * ii_t)
            dla = g * hprev * a - dmult * (a * a) / mult
            acc8[3] += red8(dla * ra_t)
            dpa = dla * (RG_C * ls) * ra_t * (1.0 - ra_t)
            dpi = dii * ii_t * (1.0 - ii_t)
            dpab = dpa.astype(BF)
            dpib = dpi.astype(BF)
            dxr = dxr + (lax.dot_general(dpab, bda_ref[...], CONTRACT_LAST, preferred_element_type=F32)
                         + lax.dot_general(dpib, bdi_ref[...], CONTRACT_LAST, preferred_element_type=F32))
            dpa_sc[pl.ds(base, tt), :] = dpab
            dpi_sc[pl.ds(base, tt), :] = dpib
            dxp[pl.ds(base, tt), :] = dxr
            acc8[0] += red8(dxr)
            acc8[1] += red8(dpa)
            acc8[2] += red8(dpi)
            return carry

        lax.fori_loop(0, S // tt, step, 0)

        def convb(t, carry):
            base = pl.multiple_of(t * tt, tt)
            d_t = dxp[pl.ds(base, tt), :]
            dux = jnp.zeros((tt, LANES), F32)
            for k in range(KW):
                drw8[k] += red8(d_t * uxp[pl.ds(base + (SL - KW + 1) + k, tt), :])
                dux = dux + w_ref[k:k + 1, :] * dxp[pl.ds(base + (KW - 1) - k, tt), :]
            dux_ref[pl.ds(base, tt), :] = dux.astype(BF)
            return carry

        lax.fori_loop(0, S // tt, convb, 0)

        xb = xr_ref[...].astype(BF)
        dwa_ref[...] = lax.dot_general(xb, dpa_sc[...], CONTRACT_FIRST, preferred_element_type=F32)
        dwi_ref[...] = lax.dot_general(xb, dpi_sc[...], CONTRACT_FIRST, preferred_element_type=F32)
        drw_ref[...] = jnp.zeros_like(drw_ref)
        vec_ref[...] = jnp.zeros_like(vec_ref)
        for k in range(KW):
            drw_ref[k:k + 1, :] = _colsum(drw8[k])
        for k in range(3):
            vec_ref[k:k + 1, :] = _colsum(acc8[k])
        vec_ref[3:4, :] = _colsum(acc8[3]) * (RG_C * _sigmoid(-lamv))

    col = lambda o: pl.BlockSpec((S, LANES), lambda c: (0, c + o))
    vec = pl.BlockSpec((1, LANES), lambda c: (0, c))
    diag = pl.BlockSpec((LANES, LANES), lambda c: (c, c))
    blk3 = lambda r: pl.BlockSpec((None, r, LANES), lambda c: (c, 0, 0))
    return _pcall(
        body, name=name, grid=(nb,),
        in_specs=[col(0), col(0), col(0), col(0), col(0), col(2 * nb), blk3(SL), diag, diag, vec],
        out_specs=[col(0), blk3(LANES), blk3(LANES), blk3(SL), pl.BlockSpec((SL, LANES), lambda c: (0, c))],
        out_shape=[jax.ShapeDtypeStruct((S, M), BF), jax.ShapeDtypeStruct((nb, LANES, LANES), F32),
                   jax.ShapeDtypeStruct((nb, LANES, LANES), F32), jax.ShapeDtypeStruct((nb, SL, LANES), F32),
                   jax.ShapeDtypeStruct((SL, M), F32)],
        scratch_shapes=[pltpu.VMEM((S + SL, LANES), F32), pltpu.VMEM((S + SL, LANES), F32), pltpu.VMEM((S, LANES), F32),
                        pltpu.VMEM((S, LANES), BF), pltpu.VMEM((S, LANES), BF), pltpu.VMEM((S + SL, LANES), F32),
                        pltpu.VMEM((S + SL, LANES), F32), pltpu.VMEM((SL, SL, LANES), F32), pltpu.VMEM((SL, SL, LANES), F32),
                        pltpu.VMEM((S, LANES), F32)],
        compiler_params=_params(("arbitrary",)),
    )(dhout, h, xr, ra, ii, proj, rnnw4, bda, bdi, lam)


GELU_K = 0.7978845608028654
GELU_C = 0.044715


def _layernorm_parts(cq):
    mu = _rowmean(cq)
    d = cq - mu
    rstd = lax.rsqrt(_rowmean(d * d) + EPS)
    return d * rstd, rstd


def mix_out(cq, proj, h, x, vecs, lnv, wout, name):
    S, D = x.shape
    M = cq.shape[1]
    ts = min(512, S)

    def body(cq_ref, uy_ref, h_ref, x_ref, v_ref, ln_ref, w_ref, xo_ref, ym_ref, yc_ref):
        z, _ = _layernorm_parts(cq_ref[...])
        l = z * _row(ln_ref, 0) + _row(ln_ref, 1)
        yc_ref[:, 0:M] = (l * _sigmoid(l)).astype(BF)
        uy = uy_ref[...].astype(F32)
        gelu = 0.5 * uy * (1.0 + jnp.tanh(GELU_K * (uy + GELU_C * uy * uy * uy)))
        yc_ref[:, M:2 * M] = (gelu * h_ref[...]).astype(BF)
        ym = jnp.dot(yc_ref[...], w_ref[...], preferred_element_type=F32)
        ym_ref[...] = ym.astype(BF)
        xo_ref[...] = x_ref[...] + _row(v_ref, R_GT2) * ym

    tok = pl.BlockSpec((ts, D), lambda i: (i, 0))
    mtok = lambda o: pl.BlockSpec((ts, M), lambda i: (i, o))
    tok_in = _streamed((ts, D), lambda i: (i, 0))
    mtok_in = lambda o: _streamed((ts, M), lambda i: (i, o))
    return _pcall(
        body, name=name, grid=(S // ts,),
        in_specs=[mtok_in(0), mtok_in(3), mtok_in(0), tok_in, pl.BlockSpec(vecs.shape, lambda i: (0, 0)),
                  pl.BlockSpec(lnv.shape, lambda i: (0, 0)), _resident(wout.shape, lambda i: (0, 0))],
        out_specs=[tok, tok, pl.BlockSpec((ts, 2 * M), lambda i: (i, 0))],
        out_shape=[jax.ShapeDtypeStruct((S, D), F32), jax.ShapeDtypeStruct((S, D), BF),
                   jax.ShapeDtypeStruct((S, 2 * M), BF)],
        compiler_params=_params(("arbitrary",)),
    )(cq, proj, h, x, vecs, lnv, wout)


def mix_out_bwd(dxo, ym, vecs, wout, cq, lnv, proj, h, name):
    S, D = dxo.shape
    M = cq.shape[1]
    ts = min(512, S)
    nt = S // ts
    ahead = min(LOOKAHEAD, nt)

    def body(dxo_hbm, ym_ref, v_ref, w_ref, cq_ref, ln_ref, uy_ref, h_ref,
             dcq_ref, dh_ref, duy_ref, dyb_ref, vgd_ref, vgm_ref, dxo_buf, dxo_sems):
        i = pl.program_id(0)

        def fetch(t):
            slot = t % (ahead + 1)
            row0 = t * ts if isinstance(t, int) else pl.multiple_of(t * ts, ts)
            return pltpu.make_async_copy(dxo_hbm.at[pl.ds(row0, ts), :], dxo_buf.at[slot], dxo_sems.at[slot])

        @pl.when(i == 0)
        def _():
            vgd_ref[...] = jnp.zeros_like(vgd_ref)
            vgm_ref[...] = jnp.zeros_like(vgm_ref)
            for t in range(ahead):
                fetch(t).start()

        @pl.when(i + ahead < nt)
        def _():
            fetch(i + ahead).start()

        fetch(i).wait()
        dxo_v = dxo_buf[i % (ahead + 1)]
        dyb = (_row(v_ref, R_GT2) * dxo_v).astype(BF)
        dyb_ref[...] = dyb
        vgd_ref[0:1, :] += _colsum(dxo_v * ym_ref[...].astype(F32))
        dycat = lax.dot_general(dyb, w_ref[...], CONTRACT_LAST, preferred_element_type=F32)
        dyc = dycat[:, 0:M]
        dyr = dycat[:, M:2 * M]
        z, rstd = _layernorm_parts(cq_ref[...])
        lng = _row(ln_ref, 0)
        l = z * lng + _row(ln_ref, 1)
        sl = _sigmoid(l)
        dl = dyc * (sl * (1.0 + l * (1.0 - sl)))
        vgm_ref[0:1, :] += _colsum(dl * z)
        vgm_ref[1:2, :] += _colsum(dl)
        dz = dl * lng
        dcq_ref[...] = (rstd * (dz - _rowmean(dz) - z * _rowmean(dz * z))).astype(BF)
        uy = uy_ref[...].astype(F32)
        u2 = uy * uy
        th = jnp.tanh(GELU_K * (uy + GELU_C * uy * u2))
        gelu = 0.5 * uy * (1.0 + th)
        dgelu = 0.5 * (1.0 + th) + 0.5 * uy * (1.0 - th * th) * (GELU_K * (1.0 + 3.0 * GELU_C * u2))
        dh_ref[...] = dyr * gelu
        duy_ref[...] = (dyr * h_ref[...] * dgelu).astype(BF)

    tok = pl.BlockSpec((ts, D), lambda i: (i, 0))
    mtok = lambda o: pl.BlockSpec((ts, M), lambda i: (i, o))
    tok_in = _streamed((ts, D), lambda i: (i, 0))
    mtok_in = lambda o: _streamed((ts, M), lambda i: (i, o))
    return _pcall(
        body, name=name, grid=(S // ts,),
        in_specs=[pl.BlockSpec(memory_space=pl.ANY), tok_in, pl.BlockSpec(vecs.shape, lambda i: (0, 0)),
                  _resident(wout.shape, lambda i: (0, 0)),
                  mtok_in(0), pl.BlockSpec(lnv.shape, lambda i: (0, 0)), mtok_in(3), mtok_in(0)],
        out_specs=[mtok(0), mtok(0), mtok(0), tok, pl.BlockSpec((SUBLANES, D), lambda i: (0, 0)),
                   pl.BlockSpec((SUBLANES, M), lambda i: (0, 0))],
        out_shape=[jax.ShapeDtypeStruct((S, M), BF), jax.ShapeDtypeStruct((S, M), F32), jax.ShapeDtypeStruct((S, M), BF),
                   jax.ShapeDtypeStruct((S, D), BF),
                   jax.ShapeDtypeStruct((SUBLANES, D), F32), jax.ShapeDtypeStruct((SUBLANES, M), F32)],
        scratch_shapes=[pltpu.VMEM((ahead + 1, ts, D), F32), pltpu.SemaphoreType.DMA((ahead + 1,))],
        compiler_params=_params(("arbitrary",)),
    )(dxo, ym, vecs, wout, cq, lnv, proj, h)


def mix_in_bwd(dparts, x, dxo, vecs, win, name):
    S, D = x.shape
    M = dparts[0].shape[1]
    ts = min(512, S)
    nt = S // ts
    ahead = min(LOOKAHEAD, nt)

    def body(d0, d1, d2, d3, x_hbm, dxo_hbm, v_ref, w_ref, dx_ref, hb_ref, dp_ref, vg_ref, x_buf, dxo_buf, sems):
        i = pl.program_id(0)

        def fetch(t):
            slot = t % (ahead + 1)
            row0 = t * ts if isinstance(t, int) else pl.multiple_of(t * ts, ts)
            return (pltpu.make_async_copy(x_hbm.at[pl.ds(row0, ts), :], x_buf.at[slot], sems.at[0, slot]),
                    pltpu.make_async_copy(dxo_hbm.at[pl.ds(row0, ts), :], dxo_buf.at[slot], sems.at[1, slot]))

        @pl.when(i == 0)
        def _():
            vg_ref[...] = jnp.zeros_like(vg_ref)
            for t in range(ahead):
                for cp in fetch(t):
                    cp.start()

        @pl.when(i + ahead < nt)
        def _():
            for cp in fetch(i + ahead):
                cp.start()

        for q, dref in enumerate((d0, d1, d2, d3)):
            dp_ref[:, q * M:(q + 1) * M] = dref[...].astype(BF)
        dh = lax.dot_general(dp_ref[...], w_ref[...], CONTRACT_LAST, preferred_element_type=F32)
        for cp in fetch(i):
            cp.wait()
        slot = i % (ahead + 1)
        xv = x_buf[slot]
        r = lax.rsqrt(_rowmean(xv * xv) + EPS)
        n = xv * r
        g = _row(v_ref, R_G2)
        sc1 = 1.0 + _row(v_ref, R_SC2)
        gsc = g * sc1
        hb_ref[...] = (n * gsc + _row(v_ref, R_SH2)).astype(BF)
        dhn = dh * n
        vg_ref[0:1, :] += _colsum(dh)
        vg_ref[1:2, :] += _colsum(dhn) * g
        vg_ref[2:3, :] += _colsum(dhn) * sc1
        dn = dh * gsc
        dx_ref[...] = dxo_buf[slot] + r * (dn - n * _rowmean(dn * n))

    tok = pl.BlockSpec((ts, D), lambda i: (i, 0))
    mtok_in = _streamed((ts, M), lambda i: (i, 0))
    hbm = pl.BlockSpec(memory_space=pl.ANY)
    return _pcall(
        body, name=name, grid=(S // ts,),
        in_specs=[mtok_in] * 4 + [hbm, hbm, pl.BlockSpec(vecs.shape, lambda i: (0, 0)), _resident(win.shape, lambda i: (0, 0))],
        out_specs=[tok, tok, pl.BlockSpec((ts, 4 * M), lambda i: (i, 0)), pl.BlockSpec((SUBLANES, D), lambda i: (0, 0))],
        out_shape=[jax.ShapeDtypeStruct((S, D), F32), jax.ShapeDtypeStruct((S, D), BF),
                   jax.ShapeDtypeStruct((S, 4 * M), BF), jax.ShapeDtypeStruct((SUBLANES, D), F32)],
        scratch_shapes=[pltpu.VMEM((ahead + 1, ts, D), F32), pltpu.VMEM((ahead + 1, ts, D), F32),
                        pltpu.SemaphoreType.DMA((2, ahead + 1))],
        compiler_params=_params(("arbitrary",)),
    )(*dparts, x, dxo, vecs, win)


def _adamw(w, g, m, v):
    m = ADAM_B1 * m + (1.0 - ADAM_B1) * g
    v = ADAM_B2 * v + (1.0 - ADAM_B2) * (g * g)
    m_hat = m / (1.0 - ADAM_B1 ** ADAM_STEP)
    v_hat = v / (1.0 - ADAM_B2 ** ADAM_STEP)
    delta = -ADAM_LR * (m_hat / (jnp.sqrt(v_hat) + ADAM_EPS) + ADAM_WD * w)
    return delta, m, v


def adam_big(w, g_mine, g_sib, m, v, cidx, name):
    R, C = w.shape
    hr = R // 2
    tr = 256 if hr % 256 == 0 else hr
    tc = C if C <= 1536 else (1152 if C % 1152 == 0 else 1024)
    assert hr % tr == 0 and C % tc == 0 and g_mine.shape == (hr, C)
    nrb = hr // tr

    def body(ci_ref, w_ref, gm_ref, gs_ref, m_ref, v_ref, g_ref, d_ref, nm_ref, nv_ref):
        mine = (pl.program_id(0) // nrb) == ci_ref[0]
        g = jnp.where(mine, gm_ref[...], gs_ref[...])
        d, nm, nv = _adamw(w_ref[...], g, m_ref[...], v_ref[...])
        g_ref[...] = g
        d_ref[...] = d
        nm_ref[...] = nm
        nv_ref[...] = nv

    blk = pl.BlockSpec((tr, tc), lambda i, j, ci: (i, j))
    mine_spec = pl.BlockSpec((tr, tc), lambda i, j, ci: (jnp.where(i // nrb == ci[0], i % nrb, 0), j))
    sib_spec = pl.BlockSpec((tr, tc), lambda i, j, ci: (jnp.where(i // nrb == ci[0], 0, i % nrb), j))
    gs = pltpu.PrefetchScalarGridSpec(num_scalar_prefetch=1, grid=(R // tr, C // tc),
                                      in_specs=[blk, mine_spec, sib_spec, blk, blk], out_specs=[blk] * 4)
    return _pcall(body, name=name, grid_spec=gs, out_shape=[jax.ShapeDtypeStruct((R, C), F32)] * 4,
                  compiler_params=_params(("parallel", "parallel")))(cidx, w, g_mine, g_sib, m, v)


def adam_cond(c_all, dmod, w, m, v, name):
    B, Kin = c_all.shape
    N = w.shape[1]
    tn = 768 if N % 768 == 0 else 256
    assert N % tn == 0

    def body(c_ref, d_ref, w_ref, m_ref, v_ref, g_ref, dl_ref, nm_ref, nv_ref):
        cv = c_ref[...]
        ca = cv * _sigmoid(cv)
        g = lax.dot_general(ca.astype(BF), d_ref[...].astype(BF), CONTRACT_FIRST, preferred_element_type=F32)
        d, nm, nv = _adamw(w_ref[...], g, m_ref[...], v_ref[...])
        g_ref[...] = g
        dl_ref[...] = d
        nm_ref[...] = nm
        nv_ref[...] = nv

    blk = pl.BlockSpec((Kin, tn), lambda n: (0, n))
    return _pcall(
        body, name=name, grid=(N // tn,),
        in_specs=[pl.BlockSpec((B, Kin), lambda n: (0, 0)), pl.BlockSpec((B, tn), lambda n: (0, n)), blk, blk, blk],
        out_specs=[blk] * 4, out_shape=[jax.ShapeDtypeStruct((Kin, N), F32)] * 4,
        compiler_params=_params(("parallel",)),
    )(c_all, dmod, w, m, v)


def adam_small(ws, gs, ms, vs, name):
    n = len(ws)

    def body(*refs):
        ins, outs = refs[:4 * n], refs[4 * n:]
        for k in range(n):
            d, nm, nv = _adamw(ins[k][...], ins[n + k][...], ins[2 * n + k][...], ins[3 * n + k][...])
            outs[k][...] = d
            outs[n + k][...] = nm
            outs[2 * n + k][...] = nv

    specs = [pl.BlockSpec(w.shape, lambda i: (0, 0)) for w in ws]
    shapes = [jax.ShapeDtypeStruct(w.shape, F32) for w in ws]
    out = _pcall(body, name=name, grid=(1,), in_specs=specs * 4, out_specs=specs * 3, out_shape=shapes * 3,
                 compiler_params=_params(("arbitrary",)))(*ws, *gs, *ms, *vs)
    return out[:n], out[n:2 * n], out[2 * n:]


def _me():
    return lax.axis_index("x"), lax.axis_index("y"), lax.axis_index("c")


def _flip(x, y, p):
    return (x ^ (p >> 1) if (p >> 1) else x), (y ^ (p & 1) if (p & 1) else y)


def _handshake(peers):
    barrier = pltpu.get_barrier_semaphore()
    for peer in peers:
        pl.semaphore_signal(barrier, inc=1, device_id=peer, device_id_type=MESH)
    pl.semaphore_wait(barrier, len(peers))


def _seq_call(body, *, name, n_in, out_shape, sem_shapes, collective_id):
    del n_in
    return pl.kernel(body, out_type=out_shape, mesh=plsc.ScalarSubcoreMesh(axis_name="sq", num_cores=1), name=name,
                     scratch_types=sem_shapes, compiler_params=pltpu.CompilerParams(collective_id=collective_id))


def _hbm_comm_call(body, *, name, n_in, out_shape, sem_shapes, seq_id):
    if seq_id is not None:
        return _seq_call(body, name=name, n_in=n_in, out_shape=out_shape, sem_shapes=sem_shapes, collective_id=seq_id)
    anyspec = pl.BlockSpec(memory_space=pl.ANY)
    return _pcall(body, name=name, in_specs=[anyspec] * n_in, out_specs=[anyspec] * len(out_shape), out_shape=out_shape,
                  scratch_shapes=sem_shapes, compiler_params=_params())


def allgather_devices(v, name, with_sum=False):
    R, L = v.shape

    def body(v_ref, out_ref, *rest):
        if with_sum:
            sum_ref, send_sems, recv_sems = rest
        else:
            send_sems, recv_sems = rest
        x, y, c = _me()
        me = 4 * x + 2 * y + c
        out_ref[me] = v_ref[...]
        copies = []
        for p in range(1, N_DEV):
            px, py = _flip(x, y, p >> 1)
            pc = (1 - c) if (p & 1) else c
            peer = 4 * px + 2 * py + pc
            send = pltpu.make_async_remote_copy(src_ref=v_ref, dst_ref=out_ref.at[me], send_sem=send_sems.at[p - 1],
                                                recv_sem=recv_sems.at[p - 1], device_id=(px, py, pc), device_id_type=MESH)
            send.start()
            recv = pltpu.make_async_remote_copy(src_ref=v_ref, dst_ref=out_ref.at[peer], send_sem=send_sems.at[p - 1],
                                                recv_sem=recv_sems.at[p - 1], device_id=(px, py, pc), device_id_type=MESH)
            copies.append((send, recv))
        for send, recv in copies:
            recv.wait_recv()
        for send, recv in copies:
            send.wait_send()
        if with_sum:
            s = out_ref[0]
            for k in range(1, N_DEV):
                s = s + out_ref[k]
            sum_ref[...] = s

    vm = pl.BlockSpec(memory_space=pltpu.VMEM)
    out_shape = [jax.ShapeDtypeStruct((N_DEV, R, L), F32)]
    if with_sum:
        out_shape.append(jax.ShapeDtypeStruct((R, L), F32))
    return _pcall(
        body, name=name, in_specs=[vm], out_specs=[vm] * len(out_shape), out_shape=out_shape,
        scratch_shapes=[pltpu.SemaphoreType.DMA((N_DEV - 1,)), pltpu.SemaphoreType.DMA((N_DEV - 1,))],
        compiler_params=_params(),
    )(v)


def allgather_devices_hbm(v, name, seq_id):
    R, L = v.shape

    def body(v_ref, out_ref, send_sems, recv_sems, local_sem):
        x, y, c = _me()
        me = 4 * x + 2 * y + c
        peers = []
        for p in range(1, N_DEV):
            px, py = _flip(x, y, p >> 1)
            peers.append((px, py, (1 - c) if (p & 1) else c))
        _handshake(peers)
        lc = pltpu.make_async_copy(v_ref, out_ref.at[me], local_sem)
        lc.start()
        copies = []
        for p, (px, py, pc) in enumerate(peers):
            send = pltpu.make_async_remote_copy(src_ref=v_ref, dst_ref=out_ref.at[me], send_sem=send_sems.at[p],
                                                recv_sem=recv_sems.at[p], device_id=(px, py, pc), device_id_type=MESH)
            send.start()
            recv = pltpu.make_async_remote_copy(src_ref=v_ref, dst_ref=out_ref.at[4 * px + 2 * py + pc], send_sem=send_sems.at[p],
                                                recv_sem=recv_sems.at[p], device_id=(px, py, pc), device_id_type=MESH)
            copies.append((send, recv))
        for send, recv in copies:
            recv.wait_recv()
        for send, recv in copies:
            send.wait_send()
        lc.wait()

    return _seq_call(body, name=name, n_in=1, out_shape=[jax.ShapeDtypeStruct((N_DEV, R, L), F32)],
                     sem_shapes=[pltpu.SemaphoreType.DMA((N_DEV - 1,)), pltpu.SemaphoreType.DMA((N_DEV - 1,)),
                                 pltpu.SemaphoreType.DMA], collective_id=seq_id)(v)[0]


def sum_slots(g, name):
    n, R, L = g.shape
    tr = 216 if R % 216 == 0 else R
    assert R % tr == 0 and tr % SUBLANES == 0

    def body(g_ref, o_ref):
        s = g_ref[0]
        for k in range(1, n):
            s = s + g_ref[k]
        o_ref[...] = s

    return _pcall(body, name=name, grid=(R // tr,), in_specs=[pl.BlockSpec((n, tr, L), lambda i: (0, i, 0))],
                  out_specs=pl.BlockSpec((tr, L), lambda i: (i, 0)), out_shape=jax.ShapeDtypeStruct((R, L), F32),
                  compiler_params=_params(("parallel",)))(g)


def allgather_chips(v, name):
    R, L = v.shape

    def body(v_ref, out_ref, send_sems, recv_sems):
        x, y, c = _me()
        chip = 2 * x + y
        out_ref[chip] = v_ref[...]
        copies = []
        for p in range(1, N_CHIPS):
            px, py = _flip(x, y, p)
            send = pltpu.make_async_remote_copy(src_ref=v_ref, dst_ref=out_ref.at[chip], send_sem=send_sems.at[p - 1],
                                                recv_sem=recv_sems.at[p - 1], device_id=(px, py, c), device_id_type=MESH)
            send.start()
            recv = pltpu.make_async_remote_copy(src_ref=v_ref, dst_ref=out_ref.at[2 * px + py], send_sem=send_sems.at[p - 1],
                                                recv_sem=recv_sems.at[p - 1], device_id=(px, py, c), device_id_type=MESH)
            copies.append((send, recv))
        for send, recv in copies:
            recv.wait_recv()
        for send, recv in copies:
            send.wait_send()

    vm = pl.BlockSpec(memory_space=pltpu.VMEM)
    return _pcall(
        body, name=name, in_specs=[vm], out_specs=vm, out_shape=jax.ShapeDtypeStruct((N_CHIPS, R, L), F32),
        scratch_shapes=[pltpu.SemaphoreType.DMA((N_CHIPS - 1,)), pltpu.SemaphoreType.DMA((N_CHIPS - 1,))],
        compiler_params=_params(),
    )(v)


def _shard_window(ref, kind, shard_shape, chip, half):
    r, c = shard_shape
    hr = r // 2
    if kind == "col":
        return ref.at[pl.ds(pl.multiple_of(half * hr, hr), hr), pl.ds(pl.multiple_of(chip * c, c), c)]
    return ref.at[pl.ds(pl.multiple_of(chip * r + half * hr, hr), hr), :]


def allgather_weights(shards, kinds, name, seq_id=None):
    n = len(shards)
    fulls = []
    for s, kind in zip(shards, kinds):
        r, c = s.shape
        fulls.append(jax.ShapeDtypeStruct((r, N_CHIPS * c) if kind == "col" else (N_CHIPS * r, c), s.dtype))

    def body(*refs):
        srcs, outs = refs[:n], refs[n:2 * n]
        send_sems, recv_sems, fsend_sems, frecv_sems = refs[2 * n:]
        x, y, c = _me()
        chip = 2 * x + y
        sib = (x, y, 1 - c)
        if seq_id is not None:
            _handshake([(*_flip(x, y, p), c) for p in range(1, N_CHIPS)] + [sib])
        sends, fwds = [], []
        for i in range(n):
            shp = srcs[i].shape
            hr = shp[0] // 2
            my_half = srcs[i].at[pl.ds(pl.multiple_of(c * hr, hr), hr), :]
            for p in range(1, N_CHIPS):
                px, py = _flip(x, y, p)
                k = i * (N_CHIPS - 1) + p - 1
                cp = pltpu.make_async_remote_copy(src_ref=my_half, dst_ref=_shard_window(outs[i], kinds[i], shp, chip, c),
                                                  send_sem=send_sems.at[k], recv_sem=recv_sems.at[k],
                                                  device_id=(px, py, c), device_id_type=MESH)
                cp.start()
                sends.append(cp)
        for i in range(n):
            shp = srcs[i].shape
            for p in range(1, N_CHIPS):
                px, py = _flip(x, y, p)
                k = i * (N_CHIPS - 1) + p - 1
                landed = _shard_window(outs[i], kinds[i], shp, 2 * px + py, c)
                pltpu.make_async_remote_copy(src_ref=landed, dst_ref=landed, send_sem=send_sems.at[k], recv_sem=recv_sems.at[k],
                                             device_id=(px, py, c), device_id_type=MESH).wait_recv()
                fw = pltpu.make_async_remote_copy(src_ref=landed, dst_ref=landed, send_sem=fsend_sems.at[k],
                                                  recv_sem=frecv_sems.at[k], device_id=sib, device_id_type=MESH)
                fw.start()
                fwds.append(fw)
        for i in range(n):
            shp = srcs[i].shape
            for p in range(1, N_CHIPS):
                px, py = _flip(x, y, p)
                k = i * (N_CHIPS - 1) + p - 1
                other = _shard_window(outs[i], kinds[i], shp, 2 * px + py, 1 - c)
                pltpu.make_async_remote_copy(src_ref=other, dst_ref=other, send_sem=fsend_sems.at[k], recv_sem=frecv_sems.at[k],
                                             device_id=sib, device_id_type=MESH).wait_recv()
        for cp in sends + fwds:
            cp.wait_send()

    nk = n * (N_CHIPS - 1)
    gathered = _hbm_comm_call(
        body, name=name, n_in=n, out_shape=fulls, seq_id=seq_id,
        sem_shapes=[pltpu.SemaphoreType.DMA((nk,)), pltpu.SemaphoreType.DMA((nk,)), pltpu.SemaphoreType.DMA((nk,)),
                    pltpu.SemaphoreType.DMA((nk,))],
    )(*shards)
    return gathered


def place_local_shards(fulls, shards, kinds, name):
    n = len(shards)
    chip = jnp.reshape(2 * lax.axis_index("x") + lax.axis_index("y"), (1,)).astype(I32)

    def body(ci_ref, *refs):
        for i in range(n):
            refs[2 * n + i][...] = refs[i][...]

    in_specs = [pl.BlockSpec(s.shape, lambda i, ci: (0, 0)) for s in shards] + [pl.BlockSpec(memory_space=pl.ANY)] * n
    out_specs = [pl.BlockSpec(s.shape, (lambda i, ci: (0, ci[0])) if k == "col" else (lambda i, ci: (ci[0], 0)))
                 for s, k in zip(shards, kinds)]
    gs = pltpu.PrefetchScalarGridSpec(num_scalar_prefetch=1, grid=(1,), in_specs=in_specs, out_specs=out_specs)
    return _pcall(body, name=name, grid_spec=gs, out_shape=[jax.ShapeDtypeStruct(f.shape, f.dtype) for f in fulls],
                  input_output_aliases={1 + n + i: i for i in range(n)}, compiler_params=_params(("arbitrary",)))(chip, *shards, *fulls)


def _as_halves(g, kind, shard_shape):
    r, c = shard_shape
    if kind == "col":
        return g.reshape(2, r // 2, N_CHIPS * c)
    return g.reshape(N_CHIPS, 2, r // 2, c)


def exchange_sibling_halves(grads, kinds, shard_shapes, name, seq_id=None):
    n = len(grads)
    views = [_as_halves(g, k, s) for g, k, s in zip(grads, kinds, shard_shapes)]
    outs = []
    for k, (r, c) in zip(kinds, shard_shapes):
        outs.append(jax.ShapeDtypeStruct((r // 2, N_CHIPS * c) if k == "col" else (N_CHIPS, r // 2, c), grads[0].dtype))

    def body(*refs):
        srcs, dsts = refs[:n], refs[n:2 * n]
        send_sems, recv_sems = refs[2 * n:]
        x, y, c = _me()
        if seq_id is not None:
            _handshake([(x, y, 1 - c)])
        cps = []
        for i in range(n):
            src = srcs[i].at[1 - c] if kinds[i] == "col" else srcs[i].at[:, 1 - c]
            cp = pltpu.make_async_remote_copy(src_ref=src, dst_ref=dsts[i], send_sem=send_sems.at[i], recv_sem=recv_sems.at[i],
                                              device_id=(x, y, 1 - c), device_id_type=MESH)
            cp.start()
            cps.append(cp)
        for cp in cps:
            cp.wait_recv()
        for cp in cps:
            cp.wait_send()

    return _hbm_comm_call(body, name=name, n_in=n, out_shape=outs, seq_id=seq_id,
                          sem_shapes=[pltpu.SemaphoreType.DMA((n,)), pltpu.SemaphoreType.DMA((n,))])(*views)


def add_sibling_half(g, recv, kind, shard_shape, core_chip, name):
    r, c = shard_shape
    hr = r // 2
    gv = _as_halves(g, kind, shard_shape)
    tr = hr if hr <= 512 else (256 if hr % 256 == 0 else hr // 2)
    assert hr % tr == 0

    def body(ci_ref, g_ref, r_ref, h_ref, hb_ref):
        s = g_ref[...].astype(F32) + r_ref[...].astype(F32)
        hb_ref[...] = s.astype(BF)

        @pl.when(pl.program_id(1) == ci_ref[1])
        def _():
            h_ref[...] = s

    grid = (hr // tr, N_CHIPS)
    if kind == "col":
        g_spec = pl.BlockSpec((None, tr, c), lambda i, k, ci: (ci[0], i, k))
        o_spec = pl.BlockSpec((tr, c), lambda i, k, ci: (i, k))
    else:
        g_spec = pl.BlockSpec((None, None, tr, c), lambda i, k, ci: (k, ci[0], i, 0))
        o_spec = pl.BlockSpec((None, tr, c), lambda i, k, ci: (k, i, 0))
    own_spec = pl.BlockSpec((tr, c), lambda i, k, ci: (i, 0))
    gs = pltpu.PrefetchScalarGridSpec(num_scalar_prefetch=1, grid=grid, in_specs=[g_spec, o_spec], out_specs=[own_spec, o_spec])
    return _pcall(
        body, name=name, grid_spec=gs,
        out_shape=[jax.ShapeDtypeStruct((hr, c), F32), jax.ShapeDtypeStruct(recv.shape, BF)],
        compiler_params=_params(("parallel", "arbitrary")),
    )(core_chip, gv, recv)


def exchange_chip_pieces(hbs, kinds, shard_shapes, name, seq_id=None):
    n = len(hbs)
    outs = [jax.ShapeDtypeStruct((N_CHIPS - 1, r // 2, c), BF) for (r, c) in shard_shapes]

    def body(*refs):
        srcs, dsts = refs[:n], refs[n:2 * n]
        send_sems, recv_sems = refs[2 * n:]
        x, y, c = _me()
        if seq_id is not None:
            _handshake([(*_flip(x, y, p), c) for p in range(1, N_CHIPS)])
        cps = []
        for i in range(n):
            cc = shard_shapes[i][1]
            for p in range(1, N_CHIPS):
                px, py = _flip(x, y, p)
                pchip = 2 * px + py
                src = (srcs[i].at[:, pl.ds(pl.multiple_of(pchip * cc, cc), cc)] if kinds[i] == "col" else srcs[i].at[pchip])
                k = i * (N_CHIPS - 1) + p - 1
                cp = pltpu.make_async_remote_copy(src_ref=src, dst_ref=dsts[i].at[p - 1], send_sem=send_sems.at[k],
                                                  recv_sem=recv_sems.at[k], device_id=(px, py, c), device_id_type=MESH)
                cp.start()
                cps.append(cp)
        for cp in cps:
            cp.wait_recv()
        for cp in cps:
            cp.wait_send()

    nk = n * (N_CHIPS - 1)
    return _hbm_comm_call(body, name=name, n_in=n, out_shape=outs, seq_id=seq_id,
                          sem_shapes=[pltpu.SemaphoreType.DMA((nk,)), pltpu.SemaphoreType.DMA((nk,))])(*hbs)


def sum_chip_pieces(h_own, pieces, name):
    hr, c = h_own.shape
    tr = hr if hr <= 512 else (256 if hr % 256 == 0 else hr // 2)
    assert hr % tr == 0

    def body(h_ref, p_ref, q_ref):
        q_ref[...] = ((h_ref[...] + p_ref[0].astype(F32)) + p_ref[1].astype(F32)) + p_ref[2].astype(F32)

    blk = pl.BlockSpec((tr, c), lambda i: (i, 0))
    return _pcall(body, name=name, grid=(hr // tr,), in_specs=[blk, pl.BlockSpec((N_CHIPS - 1, tr, c), lambda i: (0, i, 0))],
                  out_specs=blk, out_shape=jax.ShapeDtypeStruct((hr, c), F32), compiler_params=_params(("parallel",)))(h_own, pieces)


def exchange_reduced_halves(qs, name, seq_id):
    n = len(qs)

    def body(*refs):
        srcs, dsts = refs[:n], refs[n:2 * n]
        send_sems, recv_sems = refs[2 * n:]
        x, y, c = _me()
        _handshake([(x, y, 1 - c)])
        cps = []
        for i in range(n):
            cp = pltpu.make_async_remote_copy(src_ref=srcs[i], dst_ref=dsts[i], send_sem=send_sems.at[i], recv_sem=recv_sems.at[i],
                                              device_id=(x, y, 1 - c), device_id_type=MESH)
            cp.start()
            cps.append(cp)
        for cp in cps:
            cp.wait_recv()
        for cp in cps:
            cp.wait_send()

    return _seq_call(body, name=name, n_in=n, out_shape=[jax.ShapeDtypeStruct(q.shape, F32) for q in qs],
                     sem_shapes=[pltpu.SemaphoreType.DMA((n,)), pltpu.SemaphoreType.DMA((n,))], collective_id=seq_id)(*qs)


def _rows128(a):
    return a.reshape(-1, LANES)


def _after(xs, *deps):
    flat = []
    for d in deps:
        flat.extend(d if isinstance(d, (list, tuple)) else [d])
    return list(lax.optimization_barrier((tuple(xs), tuple(flat)))[0])


def _block_diag(w):
    H, d, _ = w.shape
    eye = jnp.eye(H, dtype=w.dtype)
    return jnp.einsum("hde,hg->hdge", w, eye).reshape(H * d, H * d)


def _diag_blocks(g4, H, d):
    nb = g4.shape[0]
    per = LANES // d
    g = g4.reshape(nb, per, d, per, d)
    return jnp.stack([g[:, j, :, j, :] for j in range(per)], axis=1).reshape(H, d, d)


def kernel(x, c, w_mod, b_mod, g_ffn1, w_ffn1_in, w_ffn1_out, g_mix, w_in, conv_w, conv_b, ln_g, ln_b, rnn_conv_w, rnn_conv_b, w_a, b_a, w_i, b_i, lru_lambda, w_out, g_ffn2, w_ffn2_in, w_ffn2_out, w_fmod, b_fmod, g_final, loss_target, m_w_mod, m_b_mod, m_g_ffn1, m_w_ffn1_in, m_w_ffn1_out, m_g_mix, m_w_in, m_conv_w, m_conv_b, m_ln_g, m_ln_b, m_rnn_conv_w, m_rnn_conv_b, m_w_a, m_b_a, m_w_i, m_b_i, m_lru_lambda, m_w_out, m_g_ffn2, m_w_ffn2_in, m_w_ffn2_out, m_w_fmod, m_b_fmod, m_g_final, v_w_mod, v_b_mod, v_g_ffn1, v_w_ffn1_in, v_w_ffn1_out, v_g_mix, v_w_in, v_conv_w, v_conv_b, v_ln_g, v_ln_b, v_rnn_conv_w, v_rnn_conv_b, v_w_a, v_b_a, v_w_i, v_b_i, v_lru_lambda, v_w_out, v_g_ffn2, v_w_ffn2_in, v_w_ffn2_out, v_w_fmod, v_b_fmod, v_g_final):
    S, D = x.shape[1], x.shape[2]
    M = conv_b.shape[1]
    H, HD = w_a.shape[1], w_a.shape[2]
    nb = M // LANES
    ix, iy, ic = lax.axis_index("x"), lax.axis_index("y"), lax.axis_index("c")
    chip = 2 * ix + iy
    dev = 2 * chip + ic
    core_chip = jnp.stack([ic, chip]).astype(I32)
    cidx = core_chip
    xs = x[0]
    tgt = loss_target[0]

    kinds = ["col", "row"]
    w_f1, w_mx, w_f2 = [w_ffn1_in[0], w_ffn1_out[0]], [w_in[0], w_out[0]], [w_ffn2_in[0], w_ffn2_out[0]]
    as_bf = lambda ws: [w.astype(BF) for w in ws]
    shapes_of = lambda ws: [w.shape for w in ws]
    b_f1, b_mx, b_f2 = as_bf(w_f1), as_bf(w_mx), as_bf(w_f2)
    got_f1i = allgather_weights(b_f1[:1], kinds[:1], "gather_ffn1_in", seq_id=9)
    got_f1o = allgather_weights(b_f1[1:], kinds[1:], "gather_ffn1_out", seq_id=13)
    got_mx = allgather_weights(b_mx, kinds, "gather_mix", seq_id=1)
    got_f2 = allgather_weights(b_f2, kinds, "gather_ffn2", seq_id=2)

    c_all =allgather_devices(_rows128(c), "gather_c")[0].reshape(N_DEV, D)
    mod_cols = cond_matmul(c_all, w_mod[0], "mod_proj")
    fmod_cols = cond_matmul(c_all, w_fmod, "fmod_proj")
    convw_pad = jnp.pad(conv_w[0], ((0, CONV_TAPS - CONV_WIDTH), (0, 0)))
    rnnw_pad = jnp.pad(rnn_conv_w[0], ((0, SUBLANES - RNN_CONV_WIDTH), (0, 0)))
    n_mod, n_fmod = mod_cols.shape[1], fmod_cols.shape[1]
    small = jnp.concatenate([_rows128(mod_cols), _rows128(fmod_cols), convw_pad, rnnw_pad], axis=0)
    small4 = allgather_chips(small, "gather_cond")
    r0 = N_DEV * n_mod // LANES
    r1 = r0 + N_DEV * n_fmod // LANES
    mod_all = small4[:, :r0].reshape(N_CHIPS, N_DEV, n_mod)
    fmod_all = small4[:, r0:r1].reshape(N_CHIPS, N_DEV, n_fmod)
    convw4 = small4[:, r1:r1 + CONV_TAPS]
    rnnw4 = small4[:, r1 + CONV_TAPS:r1 + CONV_TAPS + SUBLANES]
    mod_row = lax.dynamic_index_in_dim(mod_all, dev, axis=1, keepdims=False).reshape(1, N_CHIPS * n_mod) + b_mod
    fmod_row = lax.dynamic_index_in_dim(fmod_all, dev, axis=1, keepdims=False).reshape(1, N_CHIPS * n_fmod) + b_fmod[None, :]
    vecs = jnp.concatenate([mod_row.reshape(9, D), fmod_row.reshape(2, D), g_ffn1, g_mix, g_ffn2, g_final[None, :],
                            jnp.zeros((1, D), F32)], axis=0)
    lnv = jnp.concatenate([ln_g, ln_b, jnp.zeros((SUBLANES - 2, M), F32)], axis=0)
    bda = _block_diag(w_a[0]).astype(BF)
    bdi = _block_diag(w_i[0]).astype(BF)

    def reduce_add(gs, recv, ws, tag, kinds_=kinds):
        pairs = [add_sibling_half(g, r_, k, w.shape, core_chip, f"add_sibling_{tag}{j}")
                 for j, (g, r_, k, w) in enumerate(zip(gs, recv, kinds_, ws))]
        return [p[0] for p in pairs], [p[1] for p in pairs]

    def reduce_sum(hs_, recv, ws, tag, kinds_=kinds):
        return [sum_chip_pieces(h_, p_, f"sum_chips_{tag}{j}") for j, (h_, p_) in enumerate(zip(hs_, recv))]

    rows1 = (R_SH1, R_SC1, R_GT1, R_G1)
    rows3 = (R_SH3, R_SC3, R_GT3, R_G3)
    (wi1,) = place_local_shards(got_f1i, b_f1[:1], kinds[:1], "place_ffn1_in")
    g1s, u1s, a1s = ffn_fwd_in(xs, vecs, wi1, rows1, "ffn1_fwd_in")
    (wo1,) = place_local_shards(_after(got_f1o, a1s), b_f1[1:], kinds[1:], "place_ffn1_out")
    x1, y1 = ffn_fwd_out(a1s, xs, vecs, wo1, rows1, "ffn1_fwd_out")
    win, wout = place_local_shards(_after(got_mx, x1), b_mx, kinds, "place_mix")
    proj = norm_matmul(x1, vecs, win, (R_SH2, R_SC2, R_G2), "mix_in_proj")
    cq = conv_fwd(proj, convw4, conv_b, "conv_fwd")
    xr, ra, ii, hh = rnn_fwd(proj, rnnw4, rnn_conv_b, bda, bdi, b_a, b_i, lru_lambda, "rnn_fwd")
    x2, ym, ycat = mix_out(cq, proj, hh, x1, vecs, lnv, wout, "mix_out")
    wi2, wo2 = place_local_shards(_after(got_f2, x2), b_f2, kinds, "place_ffn2")
    dx3, g2s, u2s, y2, vgf = ffn_fwd(x2, vecs, wi2, wo2, rows3, "ffn2_fwd", final_tgt=tgt)

    Fd = wo1.shape[0]
    tk = S
    dx2, act2, dg2, du2, h3b, dy2b, vg3 = ffn_bwd(dx3, x2, vecs, g2s, u2s, y2, wi2, wo2, rows3, "ffn2_bwd")
    gwo2 = matmul(act2, dy2b, "tn", tm=DW_TILE, tn=D, tk=tk, out_dtype=BF, name="ffn2_dwo")
    gwi2 = matmul(h3b, dg2, "tn", tm=D, tn=DW_TILE, tk=tk, out_dtype=BF, name="ffn2_dwg", out_cols=2 * Fd)
    gwi2 = matmul(h3b, du2, "tn", tm=D, tn=DW_TILE, tk=tk, out_dtype=BF, name="ffn2_dwu", out_cols=2 * Fd, col_off=Fd, prev=gwi2)
    recv1_f2 = exchange_sibling_halves([gwi2, gwo2], kinds, shapes_of(w_f2), "reduce1_ffn2", seq_id=3)
    dcq, dhout, duy, dymb, vgd, vgm = mix_out_bwd(dx2, ym, vecs, wout, cq, lnv, proj, hh, "mix_out_bwd")
    gwout = matmul(ycat, dymb, "tn", tm=2 * M, tn=DW_TILE, tk=tk, out_dtype=BF, name="mix_dwout")
    recv1_f2 = _after(recv1_f2, gwout)
    h_f2, hb_f2 = reduce_add([gwi2, gwo2], recv1_f2, w_f2, "ffn2_")
    recv2_f2 = exchange_chip_pieces(hb_f2, kinds, shapes_of(w_f2), "reduce2_ffn2", seq_id=4)
    duv, dug, dconvw4, dconvb = conv_bwd(_after([dcq], hb_f2)[0], proj, convw4, "conv_bwd")
    dux, dwa4, dwi4, drnnw4, rvec = rnn_bwd(dhout, hh, xr, ra, ii, proj, rnnw4, bda, bdi, lru_lambda, "rnn_bwd")
    dx1, h2b, dpb, vg2 = mix_in_bwd((duv, dug, dux, duy), x1, dx2, vecs, win, "mix_in_bwd")
    gwin = matmul(h2b, dpb, "tn", tm=D, tn=DW_TILE, tk=tk, out_dtype=BF, name="mix_dwin")
    recv1_mx = exchange_sibling_halves([gwin, gwout], kinds, shapes_of(w_mx), "reduce1_mix", seq_id=5)
    q_f2 = reduce_sum(_after(h_f2, gwin), recv2_f2, w_f2, "ffn2_")
    r_f2 = exchange_reduced_halves(q_f2, "reduce3_ffn2", seq_id=14)
    h_mx, hb_mx = reduce_add([gwin, gwout], _after(recv1_mx, q_f2), w_mx, "mix_")
    recv2_mx = exchange_chip_pieces(hb_mx, kinds, shapes_of(w_mx), "reduce2_mix", seq_id=6)
    dx0, act1, dg1, du1, h1b, dy1b, vg1 = ffn_bwd(_after([dx1], hb_mx)[0], xs, vecs, g1s, u1s, y1, wi1, wo1, rows1, "ffn1_bwd")
    dmod_row = jnp.concatenate([vg1[1:3], vg1[0:1], vg2[0:2], vgd[0:1], vg3[1:3], vg3[0:1]], axis=0)
    gains = jnp.concatenate([vg1[3:4], vg2[2:3], vg3[3:4], vgf[2:4]], axis=0)
    mvecs = jnp.concatenate([dconvb, vgm[0:2], rvec[0:4], jnp.zeros((1, M), F32)], axis=0)
    parts = [_rows128(dmod_row), _rows128(vgf[0:2]), _rows128(gains), _rows128(mvecs),
             _rows128(dconvw4), _rows128(drnnw4), _rows128(_diag_blocks(dwa4, H, HD)), _rows128(_diag_blocks(dwi4, H, HD))]
    sizes = [p.shape[0] for p in parts]
    packed = jnp.concatenate(parts, axis=0)
    gathered = allgather_devices_hbm(packed, "gather_small", seq_id=10)

    gwo1 = matmul(_after([act1], recv2_mx, packed)[0], dy1b, "tn", tm=DW_TILE, tn=D, tk=tk, out_dtype=BF, name="ffn1_dwo")
    w_f1o, w_f1i = w_f1[1:], w_f1[:1]
    recv1_f1o = exchange_sibling_halves([gwo1], ["row"], shapes_of(w_f1o), "reduce1_ffn1_out", seq_id=7)
    q_mx = reduce_sum(_after(h_mx, gwo1), recv2_mx, w_mx, "mix_")
    r_mx = exchange_reduced_halves(q_mx, "reduce3_mix", seq_id=15)
    gwi1 = matmul(_after([h1b], q_mx)[0], dg1, "tn", tm=D, tn=DW_TILE, tk=tk, out_dtype=BF, name="ffn1_dwg", out_cols=2 * Fd)
    h_f1o, hb_f1o = reduce_add([gwo1], _after(recv1_f1o, gwi1), w_f1o, "ffn1_out", ["row"])
    recv2_f1o = exchange_chip_pieces(hb_f1o, ["row"], shapes_of(w_f1o), "reduce2_ffn1_out", seq_id=11)
    gwi1 = matmul(h1b, _after([du1], hb_f1o, gathered)[0], "tn", tm=D, tn=DW_TILE, tk=tk, out_dtype=BF, name="ffn1_dwu", out_cols=2 * Fd,
                  col_off=Fd, prev=gwi1)
    recv1_f1i = exchange_sibling_halves([gwi1], ["col"], shapes_of(w_f1i), "reduce1_ffn1_in", seq_id=12)
    q_f1o = reduce_sum(_after(h_f1o, gwi1), recv2_f1o, w_f1o, "ffn1_out", ["row"])
    r_f1o = exchange_reduced_halves(q_f1o, "reduce3_ffn1_out", seq_id=16)
    summed = sum_slots(gathered, "sum_small")
    offs = [0]
    for s in sizes:
        offs.append(offs[-1] + s)
    seg = lambda k: summed[offs[k]:offs[k + 1]]
    g_b_mod = seg(0).reshape(1, 9 * D)
    g_b_fmod = seg(1).reshape(1, 2 * D)
    gsum = seg(2).reshape(5, D)
    loss = (0.5 / D) * jnp.sum(gsum[4])
    msum = seg(3).reshape(SUBLANES, M)
    g_conv_w = lax.dynamic_index_in_dim(seg(4).reshape(nb, CONV_TAPS, LANES), chip, axis=0, keepdims=False)[:CONV_WIDTH]
    g_rnn_w = lax.dynamic_index_in_dim(seg(5).reshape(nb, SUBLANES, LANES), chip, axis=0, keepdims=False)[:RNN_CONV_WIDTH]
    g_w_a = seg(6).reshape(H, HD, HD)
    g_w_i = seg(7).reshape(H, HD, HD)
    dmod_all = gathered[:, offs[0]:offs[1]].reshape(N_DEV, 9 * D)
    dfmod_all = gathered[:, offs[1]:offs[2]].reshape(N_DEV, 2 * D)
    dmod_cols = lax.dynamic_slice_in_dim(dmod_all, chip * n_mod, n_mod, axis=1)
    dfmod_cols = lax.dynamic_slice_in_dim(dfmod_all, chip * n_fmod, n_fmod, axis=1)

    h_f1i, hb_f1i = reduce_add([gwi1], _after(recv1_f1i, q_f1o), w_f1i, "ffn1_in", ["col"])
    recv2_f1i = exchange_chip_pieces(hb_f1i, ["col"], shapes_of(w_f1i), "reduce2_ffn1_in", seq_id=8)
    dmod_cols, dfmod_cols = _after([dmod_cols, dfmod_cols], hb_f1i)
    g_w_mod, d_w_mod, nm_w_mod, nv_w_mod = adam_cond(c_all, dmod_cols, w_mod[0], m_w_mod[0], v_w_mod[0], "adam_w_mod")
    g_w_fmod, d_w_fmod, nm_w_fmod, nv_w_fmod = adam_cond(c_all, dfmod_cols, w_fmod, m_w_fmod, v_w_fmod, "adam_w_fmod")

    def adam_group(ws, qs_, rs_, ms, vs, tags, after):
        qs_ = _after(list(qs_), *after) if after else list(qs_)
        return [adam_big(w, q_, r_, m, v, cidx, "adam_" + t) for w, q_, r_, m, v, t in zip(ws, qs_, rs_, ms, vs, tags)]

    ad_f2 = adam_group(w_f2, q_f2, r_f2, [m_w_ffn2_in[0], m_w_ffn2_out[0]], [v_w_ffn2_in[0], v_w_ffn2_out[0]],
                       ["ffn2_in", "ffn2_out"], [hb_f1i])
    ad_mx = adam_group(w_mx, q_mx, r_mx, [m_w_in[0], m_w_out[0]], [v_w_in[0], v_w_out[0]], ["w_in", "w_out"], [hb_f1i])
    ad_f1o = adam_group(w_f1o, q_f1o, r_f1o, [m_w_ffn1_out[0]], [v_w_ffn1_out[0]], ["ffn1_out"], [hb_f1i])
    q_f1i = reduce_sum(_after(h_f1i, ad_f2[0][0], ad_f2[1][0], ad_mx[0][0], ad_mx[1][0], ad_f1o[0][0], g_w_mod, g_w_fmod),
                       recv2_f1i, w_f1i, "ffn1_in", ["col"])
    r_f1i = exchange_reduced_halves(q_f1i, "reduce3_ffn1_in", seq_id=17)
    ad_f1i = adam_group(w_f1i, q_f1i, r_f1i, [m_w_ffn1_in[0]], [v_w_ffn1_in[0]], ["ffn1_in"], [])
    big_out = ad_f1i + ad_f1o + ad_mx + ad_f2

    flat2 = lambda a: a.reshape(-1, a.shape[-1])
    small_names = ["b_mod", "g_ffn1", "g_mix", "conv_w", "conv_b", "ln_g", "ln_b", "rnn_conv_w", "rnn_conv_b", "w_a", "b_a",
                   "w_i", "b_i", "lru_lambda", "g_ffn2", "b_fmod", "g_final"]
    small_w = [b_mod, g_ffn1, g_mix, conv_w, conv_b, ln_g, ln_b, rnn_conv_w, rnn_conv_b, w_a, b_a, w_i, b_i, lru_lambda,
               g_ffn2, b_fmod, g_final]
    small_m = [m_b_mod, m_g_ffn1, m_g_mix, m_conv_w, m_conv_b, m_ln_g, m_ln_b, m_rnn_conv_w, m_rnn_conv_b, m_w_a, m_b_a,
               m_w_i, m_b_i, m_lru_lambda, m_g_ffn2, m_b_fmod, m_g_final]
    small_v = [v_b_mod, v_g_ffn1, v_g_mix, v_conv_w, v_conv_b, v_ln_g, v_ln_b, v_rnn_conv_w, v_rnn_conv_b, v_w_a, v_b_a,
               v_w_i, v_b_i, v_lru_lambda, v_g_ffn2, v_b_fmod, v_g_final]
    small_g = [g_b_mod, gsum[0:1], gsum[1:2], g_conv_w, msum[0:1], msum[1:2], msum[2:3], g_rnn_w, msum[3:4], g_w_a, msum[4:5],
               g_w_i, msum[5:6], msum[6:7], gsum[2:3], g_b_fmod, gsum[3:4]]
    small_g = [g.reshape(w.shape) for g, w in zip(small_g, small_w)]
    two_d = lambda a: a.reshape(1, -1) if a.ndim == 1 else flat2(a)
    sd, sm, sv = adam_small([two_d(a) for a in small_w], [two_d(a) for a in small_g], [two_d(a) for a in small_m],
                            [two_d(a) for a in small_v], "adam_small")
    small = {}
    for k, nm in enumerate(small_names):
        shp = small_w[k].shape
        small[nm] = (small_g[k], sd[k].reshape(shp), sm[k].reshape(shp), sv[k].reshape(shp))

    big = {"w_mod": tuple(a[None] for a in (g_w_mod, d_w_mod, nm_w_mod, nv_w_mod)),
           "w_fmod": (g_w_fmod, d_w_fmod, nm_w_fmod, nv_w_fmod)}
    for nm, res in zip(["w_ffn1_in", "w_ffn1_out", "w_in", "w_out", "w_ffn2_in", "w_ffn2_out"], big_out):
        big[nm] = tuple(a[None] for a in res)
    order = ["w_mod", "b_mod", "g_ffn1", "w_ffn1_in", "w_ffn1_out", "g_mix", "w_in", "conv_w", "conv_b", "ln_g", "ln_b",
             "rnn_conv_w", "rnn_conv_b", "w_a", "b_a", "w_i", "b_i", "lru_lambda", "w_out", "g_ffn2", "w_ffn2_in",
             "w_ffn2_out", "w_fmod", "b_fmod", "g_final"]
    table = {**small, **big}
    outs = [loss, dx0[None]]
    for kind_ in range(4):
        outs.extend(table[nm][kind_] for nm in order)
    return tuple(outs)
```
